```python
import jax
import jax.numpy as jnp
from jax import lax
import numpy as np

D_MODEL = 1024
BATCH = 16
SEQ = 2048
DEPTH = 2

CONV_WIDTH = 512
CONV_GROUPS = 8
CONV_K = 3
MLA_HEADS = 8
MLA_Q_LORA = 256
MLA_KV_LORA = 128
MLA_NOPE = 64
MLA_ROPE = 32
MLA_V = 64
MLA_QK = MLA_NOPE + MLA_ROPE
ROPE_THETA = 10000.0
DIL_PATTERNS = ((128, 1), (512, 4), (2048, 16))
DIL_GROUPS = len(DIL_PATTERNS)
DIL_HEADS = 8
DIL_HEAD_DIM = 64
DIL_WIDTH = DIL_HEADS * DIL_HEAD_DIM
N_BRANCH = 3
Q_BLOCK = 128
EPS = 1e-6

SPLIT_SIZES = ((CONV_WIDTH,) * 4
               + (MLA_Q_LORA, MLA_KV_LORA, MLA_ROPE, MLA_HEADS * MLA_V)
               + (DIL_GROUPS * DIL_WIDTH,) * 3 + (DIL_WIDTH,)
               + (N_BRANCH * D_MODEL,))
SPLIT_POINTS = tuple(int(v) for v in np.cumsum(SPLIT_SIZES)[:-1])
N_IN = int(sum(SPLIT_SIZES))

kernel_name = 'hybrid_gatedconv_mla_dilated_layer'


def rmsnorm(x, g):
    xf = x.astype(jnp.float32)
    y = xf * lax.rsqrt(jnp.mean(xf * xf, axis=-1, keepdims=True) + EPS)
    return (y * g.astype(jnp.float32)).astype(x.dtype)


def rope_tables(S):
    inv = ROPE_THETA ** (-jnp.arange(0, MLA_ROPE, 2, dtype=jnp.float32) / MLA_ROPE)
    ang = jnp.arange(S, dtype=jnp.float32)[:, None] * inv[None, :]
    return jnp.cos(ang), jnp.sin(ang)


def apply_rope(t, cos, sin):
    tf = t.astype(jnp.float32)
    t1, t2 = jnp.split(tf, 2, axis=-1)
    c = cos[None, :, None, :]
    s = sin[None, :, None, :]
    return jnp.concatenate([t1 * c - t2 * s, t2 * c + t1 * s], axis=-1).astype(t.dtype)


def alibi_slopes():
    n = DIL_GROUPS * DIL_HEADS
    m = 2.0 ** (-8.0 * jnp.arange(1, n + 1, dtype=jnp.float32) / n)
    return m.reshape(DIL_GROUPS, DIL_HEADS)


def causal_block_attention(q, k, v, scale):
    B, H, S, Dk = q.shape
    Dv = v.shape[-1]
    nqb = S // Q_BLOCK
    qb = q.reshape(B, H, nqb, Q_BLOCK, Dk).transpose(2, 0, 1, 3, 4)
    kf = k.astype(jnp.float32)
    vf = v.astype(jnp.float32)
    kpos = jnp.arange(S)

    def one_block(args):
        qblk, i = args
        s = jnp.einsum('bhqc,bhkc->bhqk', qblk.astype(jnp.float32), kf) * scale
        qpos = i * Q_BLOCK + jnp.arange(Q_BLOCK)
        s = jnp.where(kpos[None, :] <= qpos[:, None], s, -jnp.inf)
        p = jax.nn.softmax(s, axis=-1)
        return jnp.einsum('bhqk,bhkc->bhqc', p, vf)

    o = lax.map(one_block, (qb, jnp.arange(nqb)))
    return o.transpose(1, 2, 0, 3, 4).reshape(B, H, S, Dv)


def dilated_group_attention(q, k, v, slopes, dilation, n_back):
    B, S, H, hd = q.shape
    L = S // dilation
    nb = -(-L // Q_BLOCK)
    Lp = nb * Q_BLOCK

    def to_sub(t):
        return t.reshape(B, L, dilation, H, hd).transpose(0, 2, 3, 1, 4)

    qs = jnp.pad(to_sub(q).astype(jnp.float32), ((0, 0), (0, 0), (0, 0), (0, Lp - L), (0, 0)))
    qs = qs.reshape(B, dilation, H, nb, Q_BLOCK, hd)

    def windows(t):
        tp = jnp.pad(to_sub(t).astype(jnp.float32), ((0, 0), (0, 0), (0, 0), (Q_BLOCK, Lp - L), (0, 0)))
        tb = tp.reshape(B, dilation, H, nb + 1, Q_BLOCK, hd)
        return jnp.concatenate([tb[:, :, :, :-1], tb[:, :, :, 1:]], axis=4)

    kw = windows(k)
    vw = windows(v)
    qq = jnp.arange(Q_BLOCK)[:, None]
    kk = jnp.arange(2 * Q_BLOCK)[None, :]
    j = Q_BLOCK + qq - kk
    key_idx = (jnp.arange(nb)[:, None, None] - 1) * Q_BLOCK + kk[None]
    valid = (j >= 0) & (j <= n_back) & (key_idx >= 0)
    dist = (dilation * j).astype(jnp.float32)
    bias = -slopes.reshape(1, 1, H, 1, 1, 1) * dist
    scale = DIL_HEAD_DIM ** -0.5
    s = jnp.einsum('bdhnqc,bdhnkc->bdhnqk', qs, kw) * scale + bias
    s = jnp.where(valid, s, -jnp.inf)
    lse = jax.nn.logsumexp(s, axis=-1)
    p = jnp.exp(s - lse[..., None])
    o = jnp.einsum('bdhnqk,bdhnkc->bdhnqc', p, vw)
    o = o.reshape(B, dilation, H, Lp, hd)[:, :, :, :L]
    lse = lse.reshape(B, dilation, H, Lp)[:, :, :, :L]
    o = o.transpose(0, 3, 1, 2, 4).reshape(B, S, H, hd)
    lse = lse.transpose(0, 3, 1, 2).reshape(B, S, H)
    return o, lse


def hybrid_layer(x, norm_g, w_in, b_gate, conv_w, conv_b, q_a_norm_g, w_uq, kv_a_norm_g, w_ukv,
                 mla_q_norm_g, mla_k_norm_g, dil_q_norm_g, dil_k_norm_g,
                 w_out_a, w_out_b, w_out_c, w_o, cos, sin, slopes):
    B, S, _ = x.shape
    h = rmsnorm(x, norm_g)
    proj = h @ w_in
    (a_b, a_c, a_x, a_z, c_q, c_kv, k_pe, b_z, dq, dk, dv, c_z, gate_pre) = jnp.split(proj, SPLIT_POINTS, axis=-1)

    u = a_c * a_x
    up = jnp.pad(u, ((0, 0), (CONV_K - 1, 0), (0, 0)))
    conv = conv_b
    for tap in range(CONV_K):
        conv = conv + up[:, tap:tap + S] * conv_w[tap]
    y_a = a_b * conv * jax.nn.silu(a_z)

    q = (rmsnorm(c_q, q_a_norm_g) @ w_uq).reshape(B, S, MLA_HEADS, MLA_QK)
    kv = (rmsnorm(c_kv, kv_a_norm_g) @ w_ukv).reshape(B, S, MLA_HEADS, MLA_NOPE + MLA_V)
    k_nope, v = jnp.split(kv, [MLA_NOPE], axis=-1)
    k = jnp.concatenate([k_nope, jnp.broadcast_to(k_pe[:, :, None, :], (B, S, MLA_HEADS, MLA_ROPE))], axis=-1)
    q = rmsnorm(q, mla_q_norm_g)
    k = rmsnorm(k, mla_k_norm_g)
    q = jnp.concatenate([q[..., :MLA_NOPE], apply_rope(q[..., MLA_NOPE:], cos, sin)], axis=-1)
    k = jnp.concatenate([k[..., :MLA_NOPE], apply_rope(k[..., MLA_NOPE:], cos, sin)], axis=-1)
    o_b = causal_block_attention(q.transpose(0, 2, 1, 3), k.transpose(0, 2, 1, 3), v.transpose(0, 2, 1, 3),
                                 MLA_QK ** -0.5)
    o_b = o_b.transpose(0, 2, 1, 3).reshape(B, S, MLA_HEADS * MLA_V).astype(x.dtype)
    y_b = o_b * jax.nn.silu(b_z)

    dq = rmsnorm(dq.reshape(B, S, DIL_GROUPS, DIL_HEADS, DIL_HEAD_DIM), dil_q_norm_g[:, None, :])
    dk = rmsnorm(dk.reshape(B, S, DIL_GROUPS, DIL_HEADS, DIL_HEAD_DIM), dil_k_norm_g[:, None, :])
    dv = dv.reshape(B, S, DIL_GROUPS, DIL_HEADS, DIL_HEAD_DIM)
    outs = []
    lses = []
    for gi, (window, dilation) in enumerate(DIL_PATTERNS):
        o_g, lse_g = dilated_group_attention(dq[:, :, gi], dk[:, :, gi], dv[:, :, gi], slopes[gi],
                                             dilation, window // dilation)
        outs.append(o_g)
        lses.append(lse_g)
    alpha = jax.nn.softmax(jnp.stack(lses, axis=0), axis=0)
    o_c = jnp.sum(alpha[..., None] * jnp.stack(outs, axis=0), axis=0)
    o_c = o_c.reshape(B, S, DIL_WIDTH).astype(x.dtype)
    y_c = o_c * jax.nn.silu(c_z)

    g_a, g_b, g_c = jnp.split(jax.nn.sigmoid(gate_pre + b_gate), N_BRANCH, axis=-1)
    merged = g_a * (y_a @ w_out_a) + g_b * (y_b @ w_out_b) + g_c * (y_c @ w_out_c)
    return x + merged @ w_o


def _fwd_setup_inputs(seed: int = 0) -> dict:
    key = jax.random.key(seed)
    ks = jax.random.split(key, 18)
    f32 = jnp.float32

    def nrm(k, shape, scale):
        return jax.random.normal(k, shape, f32) * scale

    def gain(k, shape):
        return 1.0 + 0.02 * jax.random.normal(k, shape, f32)

    Ld = DEPTH
    return {
        'x': jax.random.normal(ks[0], (BATCH, SEQ, D_MODEL), f32),
        'norm_g': gain(ks[1], (Ld, D_MODEL)),
        'w_in': nrm(ks[2], (Ld, D_MODEL, N_IN), D_MODEL ** -0.5),
        'b_gate': nrm(ks[3], (Ld, N_BRANCH * D_MODEL), 0.1),
        'conv_w': nrm(ks[4], (Ld, CONV_K, CONV_WIDTH), CONV_K ** -0.5),
        'conv_b': nrm(ks[5], (Ld, CONV_WIDTH), 0.02),
        'q_a_norm_g': gain(ks[6], (Ld, MLA_Q_LORA)),
        'w_uq': nrm(ks[7], (Ld, MLA_Q_LORA, MLA_HEADS * MLA_QK), MLA_Q_LORA ** -0.5),
        'kv_a_norm_g': gain(ks[8], (Ld, MLA_KV_LORA)),
        'w_ukv': nrm(ks[9], (Ld, MLA_KV_LORA, MLA_HEADS * (MLA_NOPE + MLA_V)), MLA_KV_LORA ** -0.5),
        'mla_q_norm_g': gain(ks[10], (Ld, MLA_QK)),
        'mla_k_norm_g': gain(ks[11], (Ld, MLA_QK)),
        'dil_q_norm_g': gain(ks[12], (Ld, DIL_GROUPS, DIL_HEAD_DIM)),
        'dil_k_norm_g': gain(ks[13], (Ld, DIL_GROUPS, DIL_HEAD_DIM)),
        'w_out_a': nrm(ks[14], (Ld, CONV_WIDTH, D_MODEL), CONV_WIDTH ** -0.5),
        'w_out_b': nrm(ks[15], (Ld, MLA_HEADS * MLA_V, D_MODEL), (MLA_HEADS * MLA_V) ** -0.5),
        'w_out_c': nrm(ks[16], (Ld, DIL_WIDTH, D_MODEL), DIL_WIDTH ** -0.5),
        'w_o': nrm(ks[17], (Ld, D_MODEL, D_MODEL), D_MODEL ** -0.5),
    }


def _fwd_reference(x, norm_g, w_in, b_gate, conv_w, conv_b, q_a_norm_g, w_uq, kv_a_norm_g, w_ukv,
              mla_q_norm_g, mla_k_norm_g, dil_q_norm_g, dil_k_norm_g, w_out_a, w_out_b, w_out_c, w_o):
    cos, sin = rope_tables(x.shape[1])
    slopes = alibi_slopes()
    for l in range(DEPTH):
        x = hybrid_layer(x, norm_g[l], w_in[l], b_gate[l], conv_w[l], conv_b[l], q_a_norm_g[l], w_uq[l],
                         kv_a_norm_g[l], w_ukv[l], mla_q_norm_g[l], mla_k_norm_g[l], dil_q_norm_g[l],
                         dil_k_norm_g[l], w_out_a[l], w_out_b[l], w_out_c[l], w_o[l], cos, sin, slopes)
    return x


import jax as _jax
import jax.numpy as _jnp

TWIN_FORMAT = 'train_step'
FWD_PARAMS = ['x', 'norm_g', 'w_in', 'b_gate', 'conv_w', 'conv_b', 'q_a_norm_g', 'w_uq', 'kv_a_norm_g', 'w_ukv', 'mla_q_norm_g', 'mla_k_norm_g', 'dil_q_norm_g', 'dil_k_norm_g', 'w_out_a', 'w_out_b', 'w_out_c', 'w_o']
TWIN_WEIGHTS = ['norm_g', 'w_in', 'b_gate', 'conv_w', 'conv_b', 'q_a_norm_g', 'w_uq', 'kv_a_norm_g', 'w_ukv', 'mla_q_norm_g', 'mla_k_norm_g', 'dil_q_norm_g', 'dil_k_norm_g', 'w_out_a', 'w_out_b', 'w_out_c', 'w_o']
TWIN_DIFF_INPUT = 'x'
TWIN_INPUTS = ['x', 'norm_g', 'w_in', 'b_gate', 'conv_w', 'conv_b', 'q_a_norm_g', 'w_uq', 'kv_a_norm_g', 'w_ukv', 'mla_q_norm_g', 'mla_k_norm_g', 'dil_q_norm_g', 'dil_k_norm_g', 'w_out_a', 'w_out_b', 'w_out_c', 'w_o', 'loss_target', 'm_norm_g', 'm_w_in', 'm_b_gate', 'm_conv_w', 'm_conv_b', 'm_q_a_norm_g', 'm_w_uq', 'm_kv_a_norm_g', 'm_w_ukv', 'm_mla_q_norm_g', 'm_mla_k_norm_g', 'm_dil_q_norm_g', 'm_dil_k_norm_g', 'm_w_out_a', 'm_w_out_b', 'm_w_out_c', 'm_w_o', 'v_norm_g', 'v_w_in', 'v_b_gate', 'v_conv_w', 'v_conv_b', 'v_q_a_norm_g', 'v_w_uq', 'v_kv_a_norm_g', 'v_w_ukv', 'v_mla_q_norm_g', 'v_mla_k_norm_g', 'v_dil_q_norm_g', 'v_dil_k_norm_g', 'v_w_out_a', 'v_w_out_b', 'v_w_out_c', 'v_w_o']
TWIN_OUTPUTS = ['loss', 'grad_x', 'grad_norm_g', 'grad_w_in', 'grad_b_gate', 'grad_conv_w', 'grad_conv_b', 'grad_q_a_norm_g', 'grad_w_uq', 'grad_kv_a_norm_g', 'grad_w_ukv', 'grad_mla_q_norm_g', 'grad_mla_k_norm_g', 'grad_dil_q_norm_g', 'grad_dil_k_norm_g', 'grad_w_out_a', 'grad_w_out_b', 'grad_w_out_c', 'grad_w_o', 'delta_norm_g', 'delta_w_in', 'delta_b_gate', 'delta_conv_w', 'delta_conv_b', 'delta_q_a_norm_g', 'delta_w_uq', 'delta_kv_a_norm_g', 'delta_w_ukv', 'delta_mla_q_norm_g', 'delta_mla_k_norm_g', 'delta_dil_q_norm_g', 'delta_dil_k_norm_g', 'delta_w_out_a', 'delta_w_out_b', 'delta_w_out_c', 'delta_w_o', 'new_m_norm_g', 'new_m_w_in', 'new_m_b_gate', 'new_m_conv_w', 'new_m_conv_b', 'new_m_q_a_norm_g', 'new_m_w_uq', 'new_m_kv_a_norm_g', 'new_m_w_ukv', 'new_m_mla_q_norm_g', 'new_m_mla_k_norm_g', 'new_m_dil_q_norm_g', 'new_m_dil_k_norm_g', 'new_m_w_out_a', 'new_m_w_out_b', 'new_m_w_out_c', 'new_m_w_o', 'new_v_norm_g', 'new_v_w_in', 'new_v_b_gate', 'new_v_conv_w', 'new_v_conv_b', 'new_v_q_a_norm_g', 'new_v_w_uq', 'new_v_kv_a_norm_g', 'new_v_w_ukv', 'new_v_mla_q_norm_g', 'new_v_mla_k_norm_g', 'new_v_dil_q_norm_g', 'new_v_dil_k_norm_g', 'new_v_w_out_a', 'new_v_w_out_b', 'new_v_w_out_c', 'new_v_w_o']
TWIN_LEAF_KINDS = {'loss': 'loss', 'grad_x': 'grad_x', 'grad_norm_g': 'grad_w', 'grad_w_in': 'grad_w', 'grad_b_gate': 'grad_w', 'grad_conv_w': 'grad_w', 'grad_conv_b': 'grad_w', 'grad_q_a_norm_g': 'grad_w', 'grad_w_uq': 'grad_w', 'grad_kv_a_norm_g': 'grad_w', 'grad_w_ukv': 'grad_w', 'grad_mla_q_norm_g': 'grad_w', 'grad_mla_k_norm_g': 'grad_w', 'grad_dil_q_norm_g': 'grad_w', 'grad_dil_k_norm_g': 'grad_w', 'grad_w_out_a': 'grad_w', 'grad_w_out_b': 'grad_w', 'grad_w_out_c': 'grad_w', 'grad_w_o': 'grad_w', 'delta_norm_g': 'delta_w', 'delta_w_in': 'delta_w', 'delta_b_gate': 'delta_w', 'delta_conv_w': 'delta_w', 'delta_conv_b': 'delta_w', 'delta_q_a_norm_g': 'delta_w', 'delta_w_uq': 'delta_w', 'delta_kv_a_norm_g': 'delta_w', 'delta_w_ukv': 'delta_w', 'delta_mla_q_norm_g': 'delta_w', 'delta_mla_k_norm_g': 'delta_w', 'delta_dil_q_norm_g': 'delta_w', 'delta_dil_k_norm_g': 'delta_w', 'delta_w_out_a': 'delta_w', 'delta_w_out_b': 'delta_w', 'delta_w_out_c': 'delta_w', 'delta_w_o': 'delta_w', 'new_m_norm_g': 'new_m', 'new_m_w_in': 'new_m', 'new_m_b_gate': 'new_m', 'new_m_conv_w': 'new_m', 'new_m_conv_b': 'new_m', 'new_m_q_a_norm_g': 'new_m', 'new_m_w_uq': 'new_m', 'new_m_kv_a_norm_g': 'new_m', 'new_m_w_ukv': 'new_m', 'new_m_mla_q_norm_g': 'new_m', 'new_m_mla_k_norm_g': 'new_m', 'new_m_dil_q_norm_g': 'new_m', 'new_m_dil_k_norm_g': 'new_m', 'new_m_w_out_a': 'new_m', 'new_m_w_out_b': 'new_m', 'new_m_w_out_c': 'new_m', 'new_m_w_o': 'new_m', 'new_v_norm_g': 'new_v', 'new_v_w_in': 'new_v', 'new_v_b_gate': 'new_v', 'new_v_conv_w': 'new_v', 'new_v_conv_b': 'new_v', 'new_v_q_a_norm_g': 'new_v', 'new_v_w_uq': 'new_v', 'new_v_kv_a_norm_g': 'new_v', 'new_v_w_ukv': 'new_v', 'new_v_mla_q_norm_g': 'new_v', 'new_v_mla_k_norm_g': 'new_v', 'new_v_dil_q_norm_g': 'new_v', 'new_v_dil_k_norm_g': 'new_v', 'new_v_w_out_a': 'new_v', 'new_v_w_out_b': 'new_v', 'new_v_w_out_c': 'new_v', 'new_v_w_o': 'new_v'}


def _forward(args):
    return _fwd_reference(*[args[k] for k in FWD_PARAMS])


def _output_shape():
    out = _jax.eval_shape(lambda: _forward(_fwd_setup_inputs(0)))
    return out.shape, out.dtype

N_MICROBATCH = 1
ADAM_LR = 0.001
ADAM_B1 = 0.9
ADAM_B2 = 0.999
ADAM_EPS = 1e-08
ADAM_WD = 0.01
ADAM_STEP = 10
PER_EXAMPLE_BATCH_AXIS = {'x': 0, 'loss_target': 0}
SHARED_INPUTS = []
_WEIGHT_DTYPES = {'norm_g': _jnp.float32, 'w_in': _jnp.float32, 'b_gate': _jnp.float32, 'conv_w': _jnp.float32, 'conv_b': _jnp.float32, 'q_a_norm_g': _jnp.float32, 'w_uq': _jnp.float32, 'kv_a_norm_g': _jnp.float32, 'w_ukv': _jnp.float32, 'mla_q_norm_g': _jnp.float32, 'mla_k_norm_g': _jnp.float32, 'dil_q_norm_g': _jnp.float32, 'dil_k_norm_g': _jnp.float32, 'w_out_a': _jnp.float32, 'w_out_b': _jnp.float32, 'w_out_c': _jnp.float32, 'w_o': _jnp.float32}
MOMENT_SCALE = {'norm_g': 1.472056e+01, 'w_in': 1.785351e-01, 'b_gate': 6.654887e-01, 'conv_w': 3.566083e+00, 'conv_b': 2.824124e-01, 'q_a_norm_g': 6.160354e-02, 'w_uq': 3.323323e-02, 'kv_a_norm_g': 2.651518e-01, 'w_ukv': 4.257573e-02, 'mla_q_norm_g': 2.054498e-01, 'mla_k_norm_g': 2.052218e-01, 'dil_q_norm_g': 6.928381e-01, 'dil_k_norm_g': 6.940953e-01, 'w_out_a': 2.040332e-01, 'w_out_b': 3.405340e-02, 'w_out_c': 4.905363e-02, 'w_o': 1.798336e-01}


def _to_microbatches(a, axis):
    t = _jnp.moveaxis(a, axis, 0)
    t = t.reshape((N_MICROBATCH, t.shape[0] // N_MICROBATCH) + t.shape[1:])
    return _jnp.moveaxis(t, 1, axis + 1)


def setup_inputs(seed: int = 0) -> dict:
    inp = _fwd_setup_inputs(seed)
    key = _jax.random.fold_in(_jax.random.key(seed), 7919)
    shape, _ = _output_shape()
    out = dict(inp)
    out["loss_target"] = _jax.random.normal(_jax.random.fold_in(key, 0), shape, _jnp.float32)
    for i, name in enumerate(TWIN_WEIGHTS):
        w = inp[name].astype(_jnp.float32)
        if MOMENT_SCALE is None:
            s = _jnp.sqrt(_jnp.mean(_jnp.square(w)) + 1e-30)
        else:
            s = MOMENT_SCALE[name]
        km, kv = _jax.random.split(_jax.random.fold_in(key, i + 1))
        out[name] = w
        out["m_" + name] = s * _jax.random.normal(km, w.shape, _jnp.float32)
        out["v_" + name] = (s * s) * _jax.random.uniform(kv, w.shape, _jnp.float32, 0.5, 1.5)
    if N_MICROBATCH > 1:
        for name, axis in PER_EXAMPLE_BATCH_AXIS.items():
            out[name] = _to_microbatches(out[name], axis)
    return {'x': out['x'], 'norm_g': out['norm_g'], 'w_in': out['w_in'], 'b_gate': out['b_gate'], 'conv_w': out['conv_w'], 'conv_b': out['conv_b'], 'q_a_norm_g': out['q_a_norm_g'], 'w_uq': out['w_uq'], 'kv_a_norm_g': out['kv_a_norm_g'], 'w_ukv': out['w_ukv'], 'mla_q_norm_g': out['mla_q_norm_g'], 'mla_k_norm_g': out['mla_k_norm_g'], 'dil_q_norm_g': out['dil_q_norm_g'], 'dil_k_norm_g': out['dil_k_norm_g'], 'w_out_a': out['w_out_a'], 'w_out_b': out['w_out_b'], 'w_out_c': out['w_out_c'], 'w_o': out['w_o'], 'loss_target': out['loss_target'], 'm_norm_g': out['m_norm_g'], 'm_w_in': out['m_w_in'], 'm_b_gate': out['m_b_gate'], 'm_conv_w': out['m_conv_w'], 'm_conv_b': out['m_conv_b'], 'm_q_a_norm_g': out['m_q_a_norm_g'], 'm_w_uq': out['m_w_uq'], 'm_kv_a_norm_g': out['m_kv_a_norm_g'], 'm_w_ukv': out['m_w_ukv'], 'm_mla_q_norm_g': out['m_mla_q_norm_g'], 'm_mla_k_norm_g': out['m_mla_k_norm_g'], 'm_dil_q_norm_g': out['m_dil_q_norm_g'], 'm_dil_k_norm_g': out['m_dil_k_norm_g'], 'm_w_out_a': out['m_w_out_a'], 'm_w_out_b': out['m_w_out_b'], 'm_w_out_c': out['m_w_out_c'], 'm_w_o': out['m_w_o'], 'v_norm_g': out['v_norm_g'], 'v_w_in': out['v_w_in'], 'v_b_gate': out['v_b_gate'], 'v_conv_w': out['v_conv_w'], 'v_conv_b': out['v_conv_b'], 'v_q_a_norm_g': out['v_q_a_norm_g'], 'v_w_uq': out['v_w_uq'], 'v_kv_a_norm_g': out['v_kv_a_norm_g'], 'v_w_ukv': out['v_w_ukv'], 'v_mla_q_norm_g': out['v_mla_q_norm_g'], 'v_mla_k_norm_g': out['v_mla_k_norm_g'], 'v_dil_q_norm_g': out['v_dil_q_norm_g'], 'v_dil_k_norm_g': out['v_dil_k_norm_g'], 'v_w_out_a': out['v_w_out_a'], 'v_w_out_b': out['v_w_out_b'], 'v_w_out_c': out['v_w_out_c'], 'v_w_o': out['v_w_o']}


def _loss(weights, diff, rest, loss_target):
    with _jax.named_scope("forward"):
        args = {**rest, TWIN_DIFF_INPUT: diff, **{k: w.astype(_WEIGHT_DTYPES[k]) for k, w in weights.items()}}
        y = _forward(args)
    with _jax.named_scope("loss_head"):
        err = _jnp.square(y.astype(_jnp.float32) - loss_target)
        return 0.5 * _jnp.sum(_jnp.mean(err, axis=-1)) if err.ndim else 0.5 * err


def _adamw(w, g, m, v):
    m = ADAM_B1 * m + (1.0 - ADAM_B1) * g
    v = ADAM_B2 * v + (1.0 - ADAM_B2) * _jnp.square(g)
    m_hat = m / (1.0 - ADAM_B1 ** ADAM_STEP)
    v_hat = v / (1.0 - ADAM_B2 ** ADAM_STEP)
    delta = -ADAM_LR * (m_hat / (_jnp.sqrt(v_hat) + ADAM_EPS) + ADAM_WD * w)
    return delta, m, v


def reference(x, norm_g, w_in, b_gate, conv_w, conv_b, q_a_norm_g, w_uq, kv_a_norm_g, w_ukv, mla_q_norm_g, mla_k_norm_g, dil_q_norm_g, dil_k_norm_g, w_out_a, w_out_b, w_out_c, w_o, loss_target, m_norm_g, m_w_in, m_b_gate, m_conv_w, m_conv_b, m_q_a_norm_g, m_w_uq, m_kv_a_norm_g, m_w_ukv, m_mla_q_norm_g, m_mla_k_norm_g, m_dil_q_norm_g, m_dil_k_norm_g, m_w_out_a, m_w_out_b, m_w_out_c, m_w_o, v_norm_g, v_w_in, v_b_gate, v_conv_w, v_conv_b, v_q_a_norm_g, v_w_uq, v_kv_a_norm_g, v_w_ukv, v_mla_q_norm_g, v_mla_k_norm_g, v_dil_q_norm_g, v_dil_k_norm_g, v_w_out_a, v_w_out_b, v_w_out_c, v_w_o):
    given = dict(x=x, norm_g=norm_g, w_in=w_in, b_gate=b_gate, conv_w=conv_w, conv_b=conv_b, q_a_norm_g=q_a_norm_g, w_uq=w_uq, kv_a_norm_g=kv_a_norm_g, w_ukv=w_ukv, mla_q_norm_g=mla_q_norm_g, mla_k_norm_g=mla_k_norm_g, dil_q_norm_g=dil_q_norm_g, dil_k_norm_g=dil_k_norm_g, w_out_a=w_out_a, w_out_b=w_out_b, w_out_c=w_out_c, w_o=w_o, loss_target=loss_target, m_norm_g=m_norm_g, m_w_in=m_w_in, m_b_gate=m_b_gate, m_conv_w=m_conv_w, m_conv_b=m_conv_b, m_q_a_norm_g=m_q_a_norm_g, m_w_uq=m_w_uq, m_kv_a_norm_g=m_kv_a_norm_g, m_w_ukv=m_w_ukv, m_mla_q_norm_g=m_mla_q_norm_g, m_mla_k_norm_g=m_mla_k_norm_g, m_dil_q_norm_g=m_dil_q_norm_g, m_dil_k_norm_g=m_dil_k_norm_g, m_w_out_a=m_w_out_a, m_w_out_b=m_w_out_b, m_w_out_c=m_w_out_c, m_w_o=m_w_o, v_norm_g=v_norm_g, v_w_in=v_w_in, v_b_gate=v_b_gate, v_conv_w=v_conv_w, v_conv_b=v_conv_b, v_q_a_norm_g=v_q_a_norm_g, v_w_uq=v_w_uq, v_kv_a_norm_g=v_kv_a_norm_g, v_w_ukv=v_w_ukv, v_mla_q_norm_g=v_mla_q_norm_g, v_mla_k_norm_g=v_mla_k_norm_g, v_dil_q_norm_g=v_dil_q_norm_g, v_dil_k_norm_g=v_dil_k_norm_g, v_w_out_a=v_w_out_a, v_w_out_b=v_w_out_b, v_w_out_c=v_w_out_c, v_w_o=v_w_o)
    weights = {n: given[n] for n in TWIN_WEIGHTS}
    shared = {n: given[n] for n in SHARED_INPUTS}
    per_example = {n: given[n] for n in ['x']}
    grad_fn = _jax.value_and_grad(_loss, argnums=(0, 1))

    def one_microbatch(ex, loss_target):
        ex = dict(ex)
        diff = ex.pop(TWIN_DIFF_INPUT)
        return grad_fn(weights, diff, {**shared, **ex}, loss_target)

    if N_MICROBATCH == 1:
        loss, (grad_w, grad_x) = one_microbatch(per_example, given["loss_target"])
    else:
        def body(carry, xs):
            loss_sum, grad_sum = carry
            l_k, (gw_k, gx_k) = one_microbatch(xs[0], xs[1])
            with _jax.named_scope("update"):
                return (loss_sum + l_k, _jax.tree.map(_jnp.add, grad_sum, gw_k)), gx_k

        init = (_jnp.zeros((), _jnp.float32), _jax.tree.map(_jnp.zeros_like, weights))
        (loss, grad_w), grad_x = _jax.lax.scan(body, init, (per_example, given["loss_target"]))
    with _jax.named_scope("update"):
        delta_w, new_m, new_v = {}, {}, {}
        for n in TWIN_WEIGHTS:
            delta_w[n], new_m[n], new_v[n] = _adamw(weights[n], grad_w[n], given["m_" + n], given["v_" + n])
    return (loss, grad_x, *[grad_w[n] for n in TWIN_WEIGHTS], *[delta_w[n] for n in TWIN_WEIGHTS],
            *[new_m[n] for n in TWIN_WEIGHTS], *[new_v[n] for n in TWIN_WEIGHTS])
```

```python
import functools

import numpy as np
import jax
import jax.numpy as jnp
from jax import lax
from jax.experimental import pallas as pl
from jax.experimental.pallas import tpu as pltpu

F32 = jnp.float32
BF16 = jnp.bfloat16

D = 1024
S = 2048
NL = 2
CW = 512
NH = 8
QL = 256
KVL = 128
NOPE = 64
ROPE = 32
VD = 64
QK = NOPE + ROPE
QKP = 128
ROPE_THETA = 10000.0
DIL = ((128, 1), (512, 4), (2048, 16))
NG = 3
DH = 8
HD = 64
DWID = DH * HD
QB = 128
EPS = 1e-6
NIN = 11168
NINP = 11264
O_A, O_CQ, O_CKV, O_KPE, O_BZ, O_DQ, O_DK, O_DV, O_CZ, O_G = 0, 2048, 2304, 2432, 2560, 3072, 4608, 6144, 7680, 8192
KPE_END = 2464
NEG = -1e30
MLA_SCALE = QK ** -0.5
DIL_SCALE = HD ** -0.5
LANE = 128
PACK_W = 512
PACK_ROWS = 7040
VMEM_LIMIT = 48 * 1024 * 1024

ADAM_LR = 0.001
ADAM_B1 = 0.9
ADAM_B2 = 0.999
ADAM_EPS = 1e-08
ADAM_WD = 0.01
ADAM_STEP = 10

MESH = pl.DeviceIdType.MESH
BIG = ("w_in", "w_uq", "w_ukv", "w_out_a", "w_out_b", "w_out_c", "w_o")
SMALL = ("norm_g", "b_gate", "conv_b", "q_a_norm_g", "kv_a_norm_g", "mla_q_norm_g", "mla_k_norm_g",
         "dil_q_norm_g", "dil_k_norm_g")
WEIGHTS = ("norm_g", "w_in", "b_gate", "conv_w", "conv_b", "q_a_norm_g", "w_uq", "kv_a_norm_g", "w_ukv",
           "mla_q_norm_g", "mla_k_norm_g", "dil_q_norm_g", "dil_k_norm_g", "w_out_a", "w_out_b", "w_out_c", "w_o")


def _dot(a, b):
    return jnp.dot(a, b, preferred_element_type=F32)


def _dot_nt(a, b):
    return lax.dot_general(a, b, (((1,), (1,)), ((), ())), preferred_element_type=F32)


def _dot_tn(a, b):
    return lax.dot_general(a, b, (((0,), (0,)), ((), ())), preferred_element_type=F32)


def _pcall(name, fn, grid, ins, outs):
    n_in = len(ins)
    n_out = len(outs)
    acc_axis = len(grid) - 1
    is_acc = [len(o) > 4 and o[4] for o in outs]
    outs = [o[:4] for o in outs]

    def body(*refs):
        vals = fn(*[r[...] for r in refs[:n_in]])
        if not isinstance(vals, (tuple, list)):
            vals = (vals,)
        for k in range(n_out):
            r = refs[n_in + k]
            v = vals[k].astype(r.dtype).reshape(r.shape)
            if is_acc[k]:
                first = pl.program_id(acc_axis) == 0

                @pl.when(first)
                def _():
                    r[...] = v

                @pl.when(jnp.logical_not(first))
                def _():
                    r[...] += v
            else:
                r[...] = v

    return pl.pallas_call(
        body,
        grid=grid,
        in_specs=[pl.BlockSpec(bs, im) for _, bs, im in ins],
        out_specs=[pl.BlockSpec(bs, im) for _, _, bs, im in outs],
        out_shape=[jax.ShapeDtypeStruct(sh, dt) for sh, dt, _, _ in outs],
        name=name,
        compiler_params=pltpu.CompilerParams(
            dimension_semantics=("arbitrary",) * len(grid), vmem_limit_bytes=VMEM_LIMIT),
    )(*[a for a, _, _ in ins])


def _mm(name, a, b, *, ta=False, tb=False, out_dtype=F32, add=None, tm=512, tn=1024, tk=1024):
    if ta:
        K, M = a.shape
    else:
        M, K = a.shape
    if tb:
        N, K2 = b.shape
    else:
        K2, N = b.shape
    assert K == K2, (name, a.shape, b.shape)
    tm, tn, tk = min(tm, M), min(tn, N), min(tk, K)
    assert M % tm == 0 and N % tn == 0 and K % tk == 0, (name, M, N, K)
    nk = K // tk
    dims = (((0 if ta else 1,), (1 if tb else 0,)), ((), ()))
    a_spec = pl.BlockSpec((tk, tm), lambda j, i, k: (k, i)) if ta else pl.BlockSpec((tm, tk), lambda j, i, k: (i, k))
    b_spec = pl.BlockSpec((tn, tk), lambda j, i, k: (j, k)) if tb else pl.BlockSpec((tk, tn), lambda j, i, k: (k, j))
    o_spec = pl.BlockSpec((tm, tn), lambda j, i, k: (i, j))
    has_add = add is not None

    def body(*refs):
        a_ref, b_ref = refs[0], refs[1]
        add_ref = refs[2] if has_add else None
        o_ref = refs[3] if has_add else refs[2]
        p = lax.dot_general(a_ref[...].astype(BF16), b_ref[...].astype(BF16), dims, preferred_element_type=F32)
        if nk == 1:
            if has_add:
                p = p + add_ref[...]
            o_ref[...] = p.astype(out_dtype)
        else:
            acc = refs[-1]
            k = pl.program_id(2)

            @pl.when(k == 0)
            def _():
                acc[...] = p

            @pl.when(k > 0)
            def _():
                acc[...] += p

            @pl.when(k == nk - 1)
            def _():
                r = acc[...]
                if has_add:
                    r = r + add_ref[...]
                o_ref[...] = r.astype(out_dtype)

    in_specs = [a_spec, b_spec] + ([o_spec] if has_add else [])
    args = [a, b] + ([add] if has_add else [])
    return pl.pallas_call(
        body,
        grid=(N // tn, M // tm, nk),
        in_specs=in_specs,
        out_specs=o_spec,
        out_shape=jax.ShapeDtypeStruct((M, N), out_dtype),
        scratch_shapes=[pltpu.VMEM((tm, tn), F32)] if nk > 1 else [],
        name=name,
        compiler_params=pltpu.CompilerParams(
            dimension_semantics=("arbitrary", "arbitrary", "arbitrary"), vmem_limit_bytes=VMEM_LIMIT),
    )(*args)


def _vjp_of(f, n_diff):
    def g(*args, n_prim):
        prim = args[:n_diff]
        consts = args[n_diff:n_prim]
        cts = args[n_prim:]
        _, pull = jax.vjp(lambda *p: f(*p, *consts), *prim)
        out = jax.eval_shape(lambda *p: f(*p, *consts), *prim)
        if isinstance(out, (tuple, list)):
            cts = tuple(c.astype(o.dtype) for c, o in zip(cts, out))
        else:
            cts = cts[0].astype(out.dtype)
        return pull(cts)
    return g


def _rms(x, g, n=None):
    n = x.shape[-1] if n is None else n
    ms = jnp.sum(x * x, axis=-1, keepdims=True) / n
    return x * lax.rsqrt(ms + EPS) * g


def _silu(z):
    return z * jax.nn.sigmoid(z)


def _roll_rows(u, k):
    n = u.shape[0]
    r = pltpu.roll(u, k % n, 0)
    t = lax.broadcasted_iota(jnp.int32, u.shape, 0)
    if k > 0:
        return jnp.where(t >= k, r, 0.0)
    return jnp.where(t < n + k, r, 0.0)


@functools.partial(jax.custom_vjp, nondiff_argnums=(1,))
def _shift(u, k):
    return _roll_rows(u, k)


def _shift_fwd(u, k):
    return _roll_rows(u, k), None


def _shift_bwd(k, _, g):
    return (_roll_rows(g, -k),)


_shift.defvjp(_shift_fwd, _shift_bwd)


@functools.partial(jax.custom_vjp, nondiff_argnums=(1,))
def _lane_roll(u, k):
    return pltpu.roll(u, k % LANE, 1)


def _lane_roll_fwd(u, k):
    return pltpu.roll(u, k % LANE, 1), None


def _lane_roll_bwd(k, _, g):
    return (pltpu.roll(g, (-k) % LANE, 1),)


_lane_roll.defvjp(_lane_roll_fwd, _lane_roll_bwd)


def _conv_math(ab, ac, ax, az, cw, cb):
    u = ac * ax
    conv = cb + _shift(u, 2) * cw[0:1] + _shift(u, 1) * cw[1:2] + u * cw[2:3]
    return ab * conv * _silu(az)


def _mla_pre_math(cq, ckv, gq, gkv):
    return _rms(cq, gq), _rms(ckv, gkv)


def _rope_math(q, k, gq, gk, c, s1, s2):
    def one(t, g):
        tn = _rms(t, g, QK)
        return tn * c + _lane_roll(tn, -16) * s1 + _lane_roll(tn, 16) * s2
    return one(q, gq), one(k, gk)


def _gate_math(o, z):
    return o * _silu(z)


def _dil_pre_math(q, k, gq, gk):
    return _rms(q, gq), _rms(k, gk)


def _mergec_math(o0, o1, o2, l0, l1, l2, cz):
    m = lax.stop_gradient(jnp.maximum(jnp.maximum(l0, l1), l2))
    e0, e1, e2 = jnp.exp(l0 - m), jnp.exp(l1 - m), jnp.exp(l2 - m)
    den = e0 + e1 + e2
    oc = (e0 / den) * o0 + (e1 / den) * o1 + (e2 / den) * o2
    return oc * _silu(cz)


def _merge_math(g0, g1, g2, b0, b1, b2, pa, pb, pc):
    return (jax.nn.sigmoid(g0 + b0) * pa + jax.nn.sigmoid(g1 + b1) * pb) + jax.nn.sigmoid(g2 + b2) * pc


MLA_T = 256


def _mla_fwd(q, k, v):
    BH = q.shape[0]
    T = MLA_T

    def body(q_ref, k_ref, v_ref, o_ref, l_ref):
        qi = pl.program_id(1)
        qb = q_ref[...]
        row = qi * T + lax.broadcasted_iota(jnp.int32, (T, T), 0)
        col = lax.broadcasted_iota(jnp.int32, (T, T), 1)

        def step(j, carry):
            m, l, acc = carry
            off = pl.multiple_of(j * T, T)
            kb = k_ref[pl.ds(off, T), :]
            vb = v_ref[pl.ds(off, T), :]
            s = _dot_nt(qb, kb) * MLA_SCALE
            s = jnp.where(col + j * T <= row, s, NEG)
            m_new = jnp.maximum(m, jnp.max(s, axis=-1, keepdims=True))
            a = jnp.exp(m - m_new)
            p = jnp.exp(s - m_new)
            l = a * l + jnp.sum(p, axis=-1, keepdims=True)
            acc = a * acc + _dot(p.astype(BF16), vb)
            return m_new, l, acc

        init = (jnp.full((T, 1), NEG, F32), jnp.zeros((T, 1), F32), jnp.zeros((T, VD), F32))
        m, l, acc = lax.fori_loop(0, qi + 1, step, init)
        o_ref[...] = acc / l
        l_ref[...] = jnp.broadcast_to(m + jnp.log(l), (T, VD))

    return pl.pallas_call(
        body,
        grid=(BH, S // T),
        in_specs=[pl.BlockSpec((None, T, QKP), lambda b, i: (b, i, 0)),
                  pl.BlockSpec((None, S, QKP), lambda b, i: (b, 0, 0)),
                  pl.BlockSpec((None, S, VD), lambda b, i: (b, 0, 0))],
        out_specs=[pl.BlockSpec((None, T, VD), lambda b, i: (b, i, 0)),
                   pl.BlockSpec((None, T, VD), lambda b, i: (b, i, 0))],
        out_shape=[jax.ShapeDtypeStruct((BH, S, VD), F32), jax.ShapeDtypeStruct((BH, S, VD), F32)],
        name="mla_attn_fwd",
        compiler_params=pltpu.CompilerParams(dimension_semantics=("arbitrary", "arbitrary"),
                                             vmem_limit_bytes=VMEM_LIMIT),
    )(q, k, v)


def _mla_bwd(q, k, v, do, o, lse):
    BH = q.shape[0]
    T = MLA_T
    NB = S // T

    def body(q_ref, k_ref, v_ref, do_ref, o_ref, l_ref, dq_ref, dk_ref, dv_ref, delta_ref):
        dq_ref[...] = jnp.zeros((S, QKP), F32)
        delta_ref[...] = jnp.sum(do_ref[...] * o_ref[...], axis=-1, keepdims=True)
        row = lax.broadcasted_iota(jnp.int32, (T, T), 0)
        col = lax.broadcasted_iota(jnp.int32, (T, T), 1)

        def kv_step(j, _):
            koff = pl.multiple_of(j * T, T)
            kb = k_ref[pl.ds(koff, T), :]
            vb = v_ref[pl.ds(koff, T), :]

            def q_step(i, carry):
                dk, dv = carry
                qoff = pl.multiple_of(i * T, T)
                qb = q_ref[pl.ds(qoff, T), :]
                dob = do_ref[pl.ds(qoff, T), :].astype(BF16)
                lb = l_ref[pl.ds(qoff, T), :][:, 0:1]
                s = _dot_nt(qb, kb) * MLA_SCALE
                s = jnp.where(col + koff <= row + qoff, s, NEG)
                p = jnp.exp(s - lb)
                dv = dv + _dot_tn(p.astype(BF16), dob)
                dp = _dot_nt(dob, vb)
                ds = (p * (dp - delta_ref[pl.ds(qoff, T), :]) * MLA_SCALE).astype(BF16)
                dk = dk + _dot_tn(ds, qb)
                dq_ref[pl.ds(qoff, T), :] += _dot(ds, kb)
                return dk, dv

            dk, dv = lax.fori_loop(j, NB, q_step, (jnp.zeros((T, QKP), F32), jnp.zeros((T, VD), F32)))
            dk_ref[pl.ds(koff, T), :] = dk
            dv_ref[pl.ds(koff, T), :] = dv
            return 0

        lax.fori_loop(0, NB, kv_step, 0)

    def spec(w):
        return pl.BlockSpec((None, S, w), lambda b: (b, 0, 0))

    return pl.pallas_call(
        body,
        grid=(BH,),
        in_specs=[spec(QKP), spec(QKP), spec(VD), spec(VD), spec(VD), spec(VD)],
        out_specs=[spec(QKP), spec(QKP), spec(VD)],
        out_shape=[jax.ShapeDtypeStruct((BH, S, QKP), F32), jax.ShapeDtypeStruct((BH, S, QKP), F32),
                   jax.ShapeDtypeStruct((BH, S, VD), F32)],
        scratch_shapes=[pltpu.VMEM((S, 1), F32)],
        name="mla_attn_bwd",
        compiler_params=pltpu.CompilerParams(dimension_semantics=("arbitrary",), vmem_limit_bytes=VMEM_LIMIT),
    )(q, k, v, do, o, lse)


DIL_NB = 8


def _dil_tables(batch):
    slopes = (2.0 ** (-8.0 * np.arange(1, NG * DH + 1, dtype=np.float32) / (NG * DH))).astype(np.float32).reshape(NG, DH)
    nblk = batch * DH * S // QB
    tab = np.zeros((NG, nblk, 8, LANE), np.float32)
    for gi, (_, d) in enumerate(DIL):
        nb = (S // d) // QB
        for i in range(nblk):
            seq, n = divmod(i, nb)
            tab[gi, i, 0, :] = slopes[gi, seq % DH]
            tab[gi, i, 1, :] = float(d)
            tab[gi, i, 2, :] = 1.0 if n == 0 else 0.0
    return jnp.asarray(tab)


def _dil_scores(q, kc, kp, t, jc, jp, vc, vp):
    sl, dd, first = t[0:1, :], t[1:2, :], t[2:3, :]
    sc = _dot_nt(q, kc) * DIL_SCALE - sl * (dd * jc)
    sc = jnp.where(vc, sc, NEG)
    sp = _dot_nt(q, kp) * DIL_SCALE - sl * (dd * jp)
    sp = jnp.where(jnp.logical_and(vp, first < 0.5), sp, NEG)
    return sc, sp


def _dil_masks():
    qq = lax.broadcasted_iota(jnp.int32, (QB, QB), 0)
    kk = lax.broadcasted_iota(jnp.int32, (QB, QB), 1)
    jc = (qq - kk).astype(F32)
    return jc, jc + float(QB), kk <= qq, kk >= qq


def _dil_fwd(q, k, v, tab):
    N = q.shape[1]
    NB = DIL_NB
    R = NB * QB
    NT = N // R

    def body(q_ref, k_ref, kh_ref, v_ref, vh_ref, t_ref, o_ref, l_ref):
        jc, jp, vc, vp = _dil_masks()
        for n in range(NB):
            rows = slice(n * QB, (n + 1) * QB)
            prev = slice((n - 1) * QB, n * QB)
            qb = q_ref[rows, :]
            kc, vcur = k_ref[rows, :], v_ref[rows, :]
            kp = kh_ref[...] if n == 0 else k_ref[prev, :]
            vprev = vh_ref[...] if n == 0 else v_ref[prev, :]
            sc, sp = _dil_scores(qb, kc, kp, t_ref[n], jc, jp, vc, vp)
            m = jnp.maximum(jnp.max(sc, axis=-1, keepdims=True), jnp.max(sp, axis=-1, keepdims=True))
            pc = jnp.exp(sc - m)
            pp = jnp.exp(sp - m)
            l = jnp.sum(pc, axis=-1, keepdims=True) + jnp.sum(pp, axis=-1, keepdims=True)
            o = _dot(pc.astype(BF16), vcur) + _dot(pp.astype(BF16), vprev)
            o_ref[rows, :] = o / l
            l_ref[rows, :] = jnp.broadcast_to(m + jnp.log(l), (QB, HD))

    main = pl.BlockSpec((None, R, HD), lambda g, t: (g, t, 0))
    halo = pl.BlockSpec((None, QB, HD), lambda g, t: (g, jnp.maximum(t * NB - 1, 0), 0))
    tspec = pl.BlockSpec((None, NB, 8, LANE), lambda g, t: (g, t, 0, 0))
    return pl.pallas_call(
        body,
        grid=(NG, NT),
        in_specs=[main, main, halo, main, halo, tspec],
        out_specs=[main, main],
        out_shape=[jax.ShapeDtypeStruct((NG, N, HD), F32), jax.ShapeDtypeStruct((NG, N, HD), F32)],
        name="dil_attn_fwd",
        compiler_params=pltpu.CompilerParams(dimension_semantics=("arbitrary", "arbitrary"),
                                             vmem_limit_bytes=VMEM_LIMIT),
    )(q, k, k, v, v, tab)


def _dil_bwd(q, k, v, do, o, lse, dlse, tab):
    N = q.shape[1]
    NB = DIL_NB
    R = NB * QB
    NT = N // R

    def body(q_ref, k_ref, kh_ref, v_ref, vh_ref, do_ref, o_ref, l_ref, dl_ref, t_ref,
             dq_ref, dk_ref, dv_ref, ck_ref, cv_ref):
        @pl.when(pl.program_id(1) == 0)
        def _():
            ck_ref[...] = jnp.zeros((QB, HD), F32)
            cv_ref[...] = jnp.zeros((QB, HD), F32)

        jc, jp, vc, vp = _dil_masks()
        dkc, dkp, dvc, dvp = [], [], [], []
        for n in range(NB):
            rows = slice(n * QB, (n + 1) * QB)
            prev = slice((n - 1) * QB, n * QB)
            qb = q_ref[rows, :]
            kc, vcur = k_ref[rows, :], v_ref[rows, :]
            kp = kh_ref[...] if n == 0 else k_ref[prev, :]
            vprev = vh_ref[...] if n == 0 else v_ref[prev, :]
            dob = do_ref[rows, :]
            lb = l_ref[rows, :][:, 0:1]
            dl = jnp.sum(dl_ref[rows, :], axis=-1, keepdims=True)
            delta = jnp.sum(dob * o_ref[rows, :], axis=-1, keepdims=True)
            sc, sp = _dil_scores(qb, kc, kp, t_ref[n], jc, jp, vc, vp)
            pc = jnp.exp(sc - lb)
            pp = jnp.exp(sp - lb)
            do16 = dob.astype(BF16)
            corr = dl - delta
            dsc = (pc * (_dot_nt(do16, vcur) + corr) * DIL_SCALE).astype(BF16)
            dsp = (pp * (_dot_nt(do16, vprev) + corr) * DIL_SCALE).astype(BF16)
            dq_ref[rows, :] = _dot(dsc, kc) + _dot(dsp, kp)
            dkc.append(_dot_tn(dsc, qb))
            dkp.append(_dot_tn(dsp, qb))
            dvc.append(_dot_tn(pc.astype(BF16), do16))
            dvp.append(_dot_tn(pp.astype(BF16), do16))
        carry_k = ck_ref[...]
        carry_v = cv_ref[...]
        for n in range(NB):
            rows = slice(n * QB, (n + 1) * QB)
            dk_ref[rows, :] = dkc[n] + (dkp[n + 1] if n < NB - 1 else carry_k)
            dv_ref[rows, :] = dvc[n] + (dvp[n + 1] if n < NB - 1 else carry_v)
        ck_ref[...] = dkp[0]
        cv_ref[...] = dvp[0]

    main = pl.BlockSpec((None, R, HD), lambda g, t: (g, NT - 1 - t, 0))
    halo = pl.BlockSpec((None, QB, HD), lambda g, t: (g, jnp.maximum((NT - 1 - t) * NB - 1, 0), 0))
    tspec = pl.BlockSpec((None, NB, 8, LANE), lambda g, t: (g, NT - 1 - t, 0, 0))
    return pl.pallas_call(
        body,
        grid=(NG, NT),
        in_specs=[main, main, halo, main, halo, main, main, main, main, tspec],
        out_specs=[main, main, main],
        out_shape=[jax.ShapeDtypeStruct((NG, N, HD), F32)] * 3,
        scratch_shapes=[pltpu.VMEM((QB, HD), F32), pltpu.VMEM((QB, HD), F32)],
        name="dil_attn_bwd",
        compiler_params=pltpu.CompilerParams(dimension_semantics=("arbitrary", "arbitrary"),
                                             vmem_limit_bytes=VMEM_LIMIT),
    )(q, k, k, v, v, do, o, lse, dlse, tab)


def _to_sub(t, batch):
    t5 = t.reshape(batch, S, NG, DH, HD)
    outs = []
    for gi, (_, d) in enumerate(DIL):
        outs.append(t5[:, :, gi].reshape(batch, S // d, d, DH, HD).transpose(0, 2, 3, 1, 4).reshape(-1, HD))
    return jnp.stack(outs)


def _from_sub(t, batch):
    outs = []
    for gi, (_, d) in enumerate(DIL):
        outs.append(t[gi].reshape(batch, d, DH, S // d, HD).transpose(0, 3, 1, 2, 4).reshape(batch * S, DWID))
    return outs


def _rope_tables():
    inv = ROPE_THETA ** (-jnp.arange(0, ROPE, 2, dtype=F32) / ROPE)
    ang = jnp.arange(S, dtype=F32)[:, None] * inv[None, :]
    cos, sin = jnp.cos(ang), jnp.sin(ang)
    z16 = jnp.zeros((S, 16), F32)
    c = jnp.concatenate([jnp.ones((S, NOPE), F32), cos, cos, jnp.zeros((S, 32), F32)], axis=1)
    s1 = jnp.concatenate([jnp.zeros((S, NOPE), F32), -sin, z16, jnp.zeros((S, 32), F32)], axis=1)
    s2 = jnp.concatenate([jnp.zeros((S, NOPE), F32), z16, sin, jnp.zeros((S, 32), F32)], axis=1)
    return c, s1, s2


BR = 512
BRM = 256
BRR = 1024
BRD = 2048


def _layer_fwd(x, w, tabs, batch):
    T = batch * S
    rope_c, rope_s1, rope_s2, dil_tab = tabs
    res = {"x": x}
    row = lambda c: (lambda i: (i, c))
    fix = lambda i: (0, 0)

    h = _pcall("norm_fwd", _rms, (T // BR,),
               [(x, (BR, D), row(0)), (w["norm_g"], (1, D), fix)],
               [((T, D), BF16, (BR, D), row(0))])[0]
    proj = _mm("in_proj", h, w["w_in"], tm=512, tn=1024)
    res["h"], res["proj"] = h, proj
    proj3 = proj.reshape(batch, S, NINP)

    cblk = lambda s: (lambda j, b: (b, 0, 4 * s + j))
    y_a = _pcall("conv_fwd", _conv_math, (4, batch),
                 [(proj3, (None, S, LANE), cblk(0)), (proj3, (None, S, LANE), cblk(1)),
                  (proj3, (None, S, LANE), cblk(2)), (proj3, (None, S, LANE), cblk(3)),
                  (w["conv_w"], (3, LANE), lambda j, b: (0, j)), (w["conv_b"], (1, LANE), lambda j, b: (0, j))],
                 [((batch, S, CW), BF16, (None, S, LANE), lambda j, b: (b, 0, j))])[0].reshape(T, CW)
    res["y_a"] = y_a

    cqn, ckvn = _pcall("mla_pre_fwd", _mla_pre_math, (T // BR,),
                       [(proj, (BR, QL), row(O_CQ // QL)), (proj, (BR, KVL), row(O_CKV // KVL)),
                        (w["q_a_norm_g"], (1, QL), fix), (w["kv_a_norm_g"], (1, KVL), fix)],
                       [((T, QL), BF16, (BR, QL), row(0)), ((T, KVL), BF16, (BR, KVL), row(0))])
    q = _mm("uq", cqn, w["w_uq"])
    kv = _mm("ukv", ckvn, w["w_ukv"])
    q4 = jnp.pad(q.reshape(batch, S, NH, QK).transpose(0, 2, 1, 3), ((0, 0), (0, 0), (0, 0), (0, QKP - QK)))
    q4 = q4.reshape(batch * NH * S, QKP)
    kv4 = kv.reshape(batch, S, NH, NOPE + VD).transpose(0, 2, 1, 3)
    kpe = proj3[:, :, O_KPE:O_KPE + ROPE]
    k4 = jnp.concatenate([kv4[..., :NOPE], jnp.broadcast_to(kpe[:, None], (batch, NH, S, ROPE)),
                          jnp.zeros((batch, NH, S, QKP - QK), F32)], axis=-1).reshape(batch * NH * S, QKP)
    v4 = kv4[..., NOPE:].astype(BF16).reshape(batch * NH, S, VD)
    nrr = S // BRR
    tab_row = lambda i: (i % nrr, 0)
    qr, kr = _pcall("rope_fwd", _rope_math, (batch * NH * S // BRR,),
                    [(q4, (BRR, QKP), row(0)), (k4, (BRR, QKP), row(0)),
                     (w["mla_q_norm_g"], (1, QKP), fix), (w["mla_k_norm_g"], (1, QKP), fix),
                     (rope_c, (BRR, QKP), tab_row), (rope_s1, (BRR, QKP), tab_row), (rope_s2, (BRR, QKP), tab_row)],
                    [((batch * NH * S, QKP), BF16, (BRR, QKP), row(0))] * 2)
    qr = qr.reshape(batch * NH, S, QKP)
    kr = kr.reshape(batch * NH, S, QKP)
    o_b, l_b = _mla_fwd(qr, kr, v4)
    ob2 = o_b.reshape(batch, NH, S, VD).transpose(0, 2, 1, 3).reshape(T, NH * VD)
    y_b = _pcall("gateb_fwd", _gate_math, (T // BR,),
                 [(ob2, (BR, 512), row(0)), (proj, (BR, 512), row(O_BZ // 512))],
                 [((T, 512), BF16, (BR, 512), row(0))])[0]
    res.update(cqn=cqn, ckvn=ckvn, q4=q4, k4=k4, qr=qr, kr=kr, v4=v4, o_b=o_b, l_b=l_b, ob2=ob2, y_b=y_b)

    dq_s = _to_sub(proj[:, O_DQ:O_DQ + 1536], batch)
    dk_s = _to_sub(proj[:, O_DK:O_DK + 1536], batch)
    dv_s = _to_sub(proj[:, O_DV:O_DV + 1536], batch).astype(BF16)
    N = dq_s.shape[1]
    blk = lambda g, i: (g, i, 0)
    gblk = lambda g, i: (g, 0, 0)
    qn, kn = _pcall("dil_pre_fwd", _dil_pre_math, (NG, N // BRD),
                    [(dq_s, (None, BRD, HD), blk), (dk_s, (None, BRD, HD), blk),
                     (w["dil_q_norm_g"], (None, 1, HD), gblk), (w["dil_k_norm_g"], (None, 1, HD), gblk)],
                    [((NG, N, HD), BF16, (None, BRD, HD), blk)] * 2)
    o_s, l_s = _dil_fwd(qn, kn, dv_s, dil_tab)
    o0, o1, o2 = _from_sub(o_s, batch)
    l0, l1, l2 = _from_sub(l_s, batch)
    y_c = _pcall("mergec_fwd", _mergec_math, (T // BR,),
                 [(t, (BR, 512), row(0)) for t in (o0, o1, o2, l0, l1, l2)] + [(proj, (BR, 512), row(O_CZ // 512))],
                 [((T, 512), BF16, (BR, 512), row(0))])[0]
    res.update(dq_s=dq_s, dk_s=dk_s, qn=qn, kn=kn, vn=dv_s, o_s=o_s, l_s=l_s, oc=(o0, o1, o2, l0, l1, l2), y_c=y_c)

    pa = _mm("out_a", y_a, w["w_out_a"])
    pb = _mm("out_b", y_b, w["w_out_b"])
    pc = _mm("out_c", y_c, w["w_out_c"])
    merged = _pcall("merge_fwd", _merge_math, (T // BRM,),
                    [(proj, (BRM, D), row(O_G // D + s)) for s in range(3)]
                    + [(w["b_gate"], (1, D), (lambda s: (lambda i: (0, s)))(s)) for s in range(3)]
                    + [(t, (BRM, D), row(0)) for t in (pa, pb, pc)],
                    [((T, D), BF16, (BRM, D), row(0))])[0]
    out = _mm("o_proj", merged, w["w_o"], add=x)
    res.update(pa=pa, pb=pb, pc=pc, merged=merged)
    return out, res


def _norm_bwd_math(x, g, dh, dy):
    _, pull = jax.vjp(_rms, x, g)
    dx, dg = pull(dh)
    return dx + dy, dg


def _layer_bwd(dy, w, res, tabs, batch):
    T = batch * S
    rope_c, rope_s1, rope_s2, dil_tab = tabs
    row = lambda c: (lambda i: (i, c))
    fix = lambda i: (0, 0)
    x, proj, h = res["x"], res["proj"], res["h"]
    proj3 = proj.reshape(batch, S, NINP)
    g = {}

    d_merged = _mm("o_proj_dx", dy, w["w_o"], tb=True)
    g["w_o"] = _mm("o_proj_dw", res["merged"], dy, ta=True, tm=1024)

    merge_bwd = functools.partial(_vjp_of(_merge_math, 9), n_prim=9)
    dg0, dg1, dg2, db0, db1, db2, dpa, dpb, dpc = _pcall(
        "merge_bwd", merge_bwd, (T // BRM,),
        [(proj, (BRM, D), row(O_G // D + s)) for s in range(3)]
        + [(w["b_gate"], (1, D), (lambda s: (lambda i: (0, s)))(s)) for s in range(3)]
        + [(t, (BRM, D), row(0)) for t in (res["pa"], res["pb"], res["pc"])]
        + [(d_merged, (BRM, D), row(0))],
        [((T, D), BF16, (BRM, D), row(0))] * 3 + [((1, D), F32, (1, D), fix, True)] * 3
        + [((T, D), BF16, (BRM, D), row(0))] * 3)
    g["b_gate"] = jnp.concatenate([db0, db1, db2], axis=1)

    d_ya = _mm("out_a_dx", dpa, w["w_out_a"], tb=True)
    d_yb = _mm("out_b_dx", dpb, w["w_out_b"], tb=True)
    d_yc = _mm("out_c_dx", dpc, w["w_out_c"], tb=True)
    g["w_out_a"] = _mm("out_a_dw", res["y_a"], dpa, ta=True)
    g["w_out_b"] = _mm("out_b_dw", res["y_b"], dpb, ta=True)
    g["w_out_c"] = _mm("out_c_dw", res["y_c"], dpc, ta=True)

    cblk = lambda s: (lambda j, b: (b, 0, 4 * s + j))
    oblk = lambda j, b: (b, 0, j)
    conv_bwd = functools.partial(_vjp_of(_conv_math, 6), n_prim=6)
    d_ab, d_ac, d_ax, d_az, g["conv_w"], g["conv_b"] = _pcall(
        "conv_bwd", conv_bwd, (4, batch),
        [(proj3, (None, S, LANE), cblk(s)) for s in range(4)]
        + [(w["conv_w"], (3, LANE), lambda j, b: (0, j)), (w["conv_b"], (1, LANE), lambda j, b: (0, j)),
           (d_ya.reshape(batch, S, CW), (None, S, LANE), oblk)],
        [((batch, S, CW), BF16, (None, S, LANE), oblk)] * 4
        + [((3, CW), F32, (3, LANE), lambda j, b: (0, j), True), ((1, CW), F32, (1, LANE), lambda j, b: (0, j), True)])

    gate_bwd = functools.partial(_vjp_of(_gate_math, 2), n_prim=2)
    d_ob, d_bz = _pcall("gateb_bwd", gate_bwd, (T // BR,),
                        [(res["ob2"], (BR, 512), row(0)), (proj, (BR, 512), row(O_BZ // 512)), (d_yb, (BR, 512), row(0))],
                        [((T, 512), F32, (BR, 512), row(0)), ((T, 512), BF16, (BR, 512), row(0))])
    d_ob4 = d_ob.reshape(batch, S, NH, VD).transpose(0, 2, 1, 3).reshape(batch * NH, S, VD)
    dqr, dkr, dv4 = _mla_bwd(res["qr"], res["kr"], res["v4"], d_ob4, res["o_b"], res["l_b"])
    nrr = S // BRR
    tab_row = lambda i: (i % nrr, 0)
    rows = batch * NH * S
    rope_bwd = functools.partial(_vjp_of(_rope_math, 4), n_prim=7)
    d_q4, d_k4, g["mla_q_norm_g"], g["mla_k_norm_g"] = _pcall(
        "rope_bwd", rope_bwd, (rows // BRR,),
        [(res["q4"], (BRR, QKP), row(0)), (res["k4"], (BRR, QKP), row(0)),
         (w["mla_q_norm_g"], (1, QKP), fix), (w["mla_k_norm_g"], (1, QKP), fix),
         (rope_c, (BRR, QKP), tab_row), (rope_s1, (BRR, QKP), tab_row), (rope_s2, (BRR, QKP), tab_row),
         (dqr.reshape(rows, QKP), (BRR, QKP), row(0)), (dkr.reshape(rows, QKP), (BRR, QKP), row(0))],
        [((rows, QKP), F32, (BRR, QKP), row(0))] * 2 + [((1, QKP), F32, (1, QKP), fix, True)] * 2)
    d_q = d_q4.reshape(batch, NH, S, QKP)[..., :QK].transpose(0, 2, 1, 3).reshape(T, NH * QK)
    d_k4 = d_k4.reshape(batch, NH, S, QKP)
    d_kpe = jnp.sum(d_k4[..., NOPE:QK], axis=1).reshape(T, ROPE)
    d_kv = jnp.concatenate([d_k4[..., :NOPE], dv4.reshape(batch, NH, S, VD)], axis=-1)
    d_kv = d_kv.transpose(0, 2, 1, 3).reshape(T, NH * (NOPE + VD))
    d_cqn = _mm("uq_dx", d_q, w["w_uq"], tb=True)
    d_ckvn = _mm("ukv_dx", d_kv, w["w_ukv"], tb=True)
    g["w_uq"] = _mm("uq_dw", res["cqn"], d_q, ta=True)
    g["w_ukv"] = _mm("ukv_dw", res["ckvn"], d_kv, ta=True)
    pre_bwd = functools.partial(_vjp_of(_mla_pre_math, 4), n_prim=4)
    d_cq, d_ckv, g["q_a_norm_g"], g["kv_a_norm_g"] = _pcall(
        "mla_pre_bwd", pre_bwd, (T // BR,),
        [(proj, (BR, QL), row(O_CQ // QL)), (proj, (BR, KVL), row(O_CKV // KVL)),
         (w["q_a_norm_g"], (1, QL), fix), (w["kv_a_norm_g"], (1, KVL), fix),
         (d_cqn, (BR, QL), row(0)), (d_ckvn, (BR, KVL), row(0))],
        [((T, QL), BF16, (BR, QL), row(0)), ((T, KVL), BF16, (BR, KVL), row(0)),
         ((1, QL), F32, (1, QL), fix, True), ((1, KVL), F32, (1, KVL), fix, True)])

    mergec_bwd = functools.partial(_vjp_of(_mergec_math, 7), n_prim=7)
    outs = _pcall("mergec_bwd", mergec_bwd, (T // BR,),
                  [(t, (BR, 512), row(0)) for t in res["oc"]] + [(proj, (BR, 512), row(O_CZ // 512)),
                                                                 (d_yc, (BR, 512), row(0))],
                  [((T, 512), F32, (BR, 512), row(0))] * 6 + [((T, 512), BF16, (BR, 512), row(0))])
    d_cz = outs[6]
    do_s = _to_sub(jnp.concatenate(outs[0:3], axis=1), batch)
    dl_s = _to_sub(jnp.concatenate(outs[3:6], axis=1), batch)
    d_qn, d_kn, d_vn = _dil_bwd(res["qn"], res["kn"], res["vn"], do_s, res["o_s"], res["l_s"], dl_s, dil_tab)
    N = d_qn.shape[1]
    blk = lambda gi, i: (gi, i, 0)
    gblk = lambda gi, i: (gi, 0, 0)
    dpre_bwd = functools.partial(_vjp_of(_dil_pre_math, 4), n_prim=4)
    d_dq_s, d_dk_s, g["dil_q_norm_g"], g["dil_k_norm_g"] = _pcall(
        "dil_pre_bwd", dpre_bwd, (NG, N // BRD),
        [(res["dq_s"], (None, BRD, HD), blk), (res["dk_s"], (None, BRD, HD), blk),
         (w["dil_q_norm_g"], (None, 1, HD), gblk), (w["dil_k_norm_g"], (None, 1, HD), gblk),
         (d_qn, (None, BRD, HD), blk), (d_kn, (None, BRD, HD), blk)],
        [((NG, N, HD), BF16, (None, BRD, HD), blk)] * 2 + [((NG, 1, HD), F32, (None, 1, HD), gblk, True)] * 2)
    d_dq = jnp.concatenate(_from_sub(d_dq_s, batch), axis=1)
    d_dk = jnp.concatenate(_from_sub(d_dk_s, batch), axis=1)
    d_dv = jnp.concatenate(_from_sub(d_vn.astype(BF16), batch), axis=1)

    d_kpe_p = jnp.pad(d_kpe, ((0, 0), (0, LANE - ROPE))).astype(BF16)
    dproj = jnp.concatenate(
        [t.reshape(T, CW) for t in (d_ab, d_ac, d_ax, d_az)]
        + [d_cq, d_ckv, d_kpe_p, d_bz, d_dq, d_dk, d_dv, d_cz, dg0, dg1, dg2], axis=1)
    d_h = _mm("in_proj_dx", dproj, w["w_in"], tb=True, tm=1024)
    g["w_in"] = _mm("in_proj_dw", h, dproj, ta=True, tm=1024)
    dx, g["norm_g"] = _pcall("norm_bwd", _norm_bwd_math, (T // BR,),
                             [(x, (BR, D), row(0)), (w["norm_g"], (1, D), fix), (d_h, (BR, D), row(0)),
                              (dy, (BR, D), row(0))],
                             [((T, D), F32, (BR, D), row(0)), ((1, D), F32, (1, D), fix, True)])
    return dx, g


def _loss_math(y, t):
    e = y - t
    return e * (1.0 / D), 0.5 * jnp.sum(jnp.sum(e * e, axis=-1, keepdims=True) / D, axis=0, keepdims=True)


def _local_step(x, target, ws, batch):
    T = batch * S
    tabs = _rope_tables() + (_dil_tables(batch),)
    saved = []
    y = x
    for l in range(NL):
        y, res = _layer_fwd(y, ws[l], tabs, batch)
        saved.append(res)
    row = lambda i: (i, 0)
    dy, loss = _pcall("loss", _loss_math, (T // BR,),
                      [(y, (BR, D), row), (target, (BR, D), row)],
                      [((T, D), F32, (BR, D), row), ((1, 1), F32, (1, 1), lambda i: (0, 0), True)])
    grads = [None] * NL
    for l in reversed(range(NL)):
        dy, grads[l] = _layer_bwd(dy, ws[l], saved[l], tabs, batch)
    return loss, dy, grads


ANY = pl.BlockSpec(memory_space=pl.ANY)
HALF = PACK_ROWS // 2


def _me():
    return lax.axis_index("x"), lax.axis_index("y"), lax.axis_index("c")


def _all_gather_chips(wp):
    def body(w_ref, o_ref, send_sems, recv_sems, local_sem):
        x, y, c = _me()
        sib = (x, y, 1 - c)
        chips = [(1 - x, y), (x, 1 - y), (1 - x, 1 - y)]
        mine_rows = pl.ds(pl.multiple_of(c * HALF, 16), HALF)
        sib_rows = pl.ds(pl.multiple_of((1 - c) * HALF, 16), HALF)

        def copy(k, src, dst, to):
            return pltpu.make_async_remote_copy(src_ref=src, dst_ref=dst, send_sem=send_sems.at[k],
                                                recv_sem=recv_sems.at[k], device_id=to, device_id_type=MESH)

        mine = pltpu.make_async_copy(w_ref, o_ref.at[2 * x + y], local_sem)
        mine.start()
        first = [copy(j, w_ref.at[mine_rows], o_ref.at[2 * x + y, mine_rows], (cx, cy, c))
                 for j, (cx, cy) in enumerate(chips)]
        for cp in first:
            cp.start()
        passed = []
        for j, (cx, cy) in enumerate(chips):
            landed = o_ref.at[2 * cx + cy, mine_rows]
            copy(j, landed, landed, (cx, cy, c)).wait_recv()
            cp = copy(3 + j, landed, landed, sib)
            cp.start()
            passed.append(cp)
        for j, (cx, cy) in enumerate(chips):
            landed = o_ref.at[2 * cx + cy, sib_rows]
            copy(3 + j, landed, landed, sib).wait_recv()
        for cp in first + passed:
            cp.wait_send()
        mine.wait()

    return pl.pallas_call(
        body,
        out_shape=jax.ShapeDtypeStruct((4, PACK_ROWS, PACK_W), wp.dtype),
        in_specs=[ANY], out_specs=ANY,
        scratch_shapes=[pltpu.SemaphoreType.DMA((6,)), pltpu.SemaphoreType.DMA((6,)), pltpu.SemaphoreType.DMA],
        name="weights_all_gather",
    )(wp)


def _rs_sibling_swap(gp):
    def body(g_ref, r_ref, send_sems, recv_sems):
        x, y, c = _me()
        cps = [pltpu.make_async_remote_copy(src_ref=g_ref.at[j, 1 - c], dst_ref=r_ref.at[j], send_sem=send_sems.at[j],
                                            recv_sem=recv_sems.at[j], device_id=(x, y, 1 - c), device_id_type=MESH)
               for j in range(4)]
        for cp in cps:
            cp.start()
        for cp in cps:
            cp.wait()

    return pl.pallas_call(
        body,
        out_shape=jax.ShapeDtypeStruct((4, HALF, PACK_W), F32),
        in_specs=[ANY], out_specs=ANY,
        scratch_shapes=[pltpu.SemaphoreType.DMA((4,)), pltpu.SemaphoreType.DMA((4,))],
        name="grads_sibling_swap",
    )(gp)


RS_BR = 704


def _rs_chip_sum(gp, rb, cidx):
    def body(c_ref, g_ref, r_ref, o_ref):
        o_ref[...] = (g_ref[...] + r_ref[...]).astype(BF16)

    return pl.pallas_call(
        body,
        grid_spec=pltpu.PrefetchScalarGridSpec(
            num_scalar_prefetch=1, grid=(4, HALF // RS_BR),
            in_specs=[pl.BlockSpec((None, None, RS_BR, PACK_W), lambda j, i, cr: (j, cr[0], i, 0)),
                      pl.BlockSpec((None, RS_BR, PACK_W), lambda j, i, cr: (j, i, 0))],
            out_specs=pl.BlockSpec((None, RS_BR, PACK_W), lambda j, i, cr: (j, i, 0))),
        out_shape=jax.ShapeDtypeStruct((4, HALF, PACK_W), BF16),
        name="grads_chip_sum",
        compiler_params=pltpu.CompilerParams(dimension_semantics=("arbitrary", "arbitrary"),
                                             vmem_limit_bytes=VMEM_LIMIT),
    )(cidx, gp, rb)


def _rs_chip_exchange(s1):
    def body(s_ref, r_ref, send_sems, recv_sems):
        x, y, c = _me()
        chips = [(1 - x, y), (x, 1 - y), (1 - x, 1 - y)]
        cps = [pltpu.make_async_remote_copy(src_ref=s_ref.at[2 * cx + cy], dst_ref=r_ref.at[j],
                                            send_sem=send_sems.at[j], recv_sem=recv_sems.at[j],
                                            device_id=(cx, cy, c), device_id_type=MESH)
               for j, (cx, cy) in enumerate(chips)]
        for cp in cps:
            cp.start()
        for cp in cps:
            cp.wait()

    return pl.pallas_call(
        body,
        out_shape=jax.ShapeDtypeStruct((3, HALF, PACK_W), BF16),
        in_specs=[ANY], out_specs=ANY,
        scratch_shapes=[pltpu.SemaphoreType.DMA((3,)), pltpu.SemaphoreType.DMA((3,))],
        name="grads_chip_exchange",
    )(s1)


def _rs_final_sum(gp, rb, r2, idx):
    def body(i_ref, g_ref, r_ref, a_ref, b_ref, c_ref, o_ref):
        o_ref[...] = (((g_ref[...] + r_ref[...]) + a_ref[...].astype(F32)) + b_ref[...].astype(F32)) \
            + c_ref[...].astype(F32)

    def peer(j):
        return pl.BlockSpec((None, RS_BR, PACK_W), lambda i, ir: (j, i, 0))

    return pl.pallas_call(
        body,
        grid_spec=pltpu.PrefetchScalarGridSpec(
            num_scalar_prefetch=1, grid=(HALF // RS_BR,),
            in_specs=[pl.BlockSpec((None, None, RS_BR, PACK_W), lambda i, ir: (ir[0], ir[1], i, 0)),
                      pl.BlockSpec((None, RS_BR, PACK_W), lambda i, ir: (ir[0], i, 0)),
                      peer(0), peer(1), peer(2)],
            out_specs=pl.BlockSpec((RS_BR, PACK_W), lambda i, ir: (i, 0))),
        out_shape=jax.ShapeDtypeStruct((HALF, PACK_W), F32),
        name="grads_final_sum",
        compiler_params=pltpu.CompilerParams(dimension_semantics=("arbitrary",), vmem_limit_bytes=VMEM_LIMIT),
    )(idx, gp, rb, r2, r2, r2)


def _rs_share(fin):
    def body(f_ref, o_ref, send_sem, recv_sem, local_sem):
        x, y, c = _me()
        mine = pltpu.make_async_copy(f_ref, o_ref.at[c], local_sem)
        mine.start()
        cp = pltpu.make_async_remote_copy(src_ref=f_ref, dst_ref=o_ref.at[c], send_sem=send_sem, recv_sem=recv_sem,
                                          device_id=(x, y, 1 - c), device_id_type=MESH)
        cp.start()
        other = o_ref.at[1 - c]
        pltpu.make_async_remote_copy(src_ref=f_ref, dst_ref=other, send_sem=send_sem, recv_sem=recv_sem,
                                     device_id=(x, y, 1 - c), device_id_type=MESH).wait_recv()
        cp.wait_send()
        mine.wait()

    return pl.pallas_call(
        body,
        out_shape=jax.ShapeDtypeStruct((2, HALF, PACK_W), F32),
        in_specs=[ANY], out_specs=ANY,
        scratch_shapes=[pltpu.SemaphoreType.DMA, pltpu.SemaphoreType.DMA, pltpu.SemaphoreType.DMA],
        name="grads_share",
    )(fin)


def _reduce_scatter(gp):
    x, y, c = _me()
    gp = gp.reshape(4, 2, HALF, PACK_W)
    rb = _rs_sibling_swap(gp)
    s1 = _rs_chip_sum(gp, rb, jnp.reshape(c, (1,)).astype(jnp.int32))
    r2 = _rs_chip_exchange(s1)
    fin = _rs_final_sum(gp, rb, r2, jnp.stack([2 * x + y, c]).astype(jnp.int32))
    return _rs_share(fin).reshape(PACK_ROWS, PACK_W)


def _all_reduce_small(gs):
    rows = gs.shape[0]

    def body(g_ref, o_ref, buf, send_sems, recv_sems):
        x, y, c = _me()
        me = 4 * x + 2 * y + c
        buf[me] = g_ref[...]
        cps = []
        for r in range(1, 8):
            fx, fy, fc = (r >> 2) & 1, (r >> 1) & 1, r & 1
            px, py, pc = jnp.bitwise_xor(x, fx), jnp.bitwise_xor(y, fy), jnp.bitwise_xor(c, fc)
            cps.append((pltpu.make_async_remote_copy(
                src_ref=g_ref, dst_ref=buf.at[me], send_sem=send_sems.at[r - 1], recv_sem=recv_sems.at[r - 1],
                device_id=(px, py, pc), device_id_type=MESH), 4 * px + 2 * py + pc))
        for cp, _ in cps:
            cp.start()
        for r, (cp, peer) in enumerate(cps):
            pltpu.make_async_remote_copy(
                src_ref=g_ref, dst_ref=buf.at[peer], send_sem=send_sems.at[r], recv_sem=recv_sems.at[r],
                device_id=(x, y, c), device_id_type=MESH).wait_recv()
        for cp, _ in cps:
            cp.wait_send()
        acc = buf[0]
        for k in range(1, 8):
            acc = acc + buf[k]
        o_ref[...] = acc

    return pl.pallas_call(
        body,
        out_shape=jax.ShapeDtypeStruct((rows, LANE), F32),
        in_specs=[pl.BlockSpec(memory_space=pltpu.VMEM)],
        out_specs=pl.BlockSpec(memory_space=pltpu.VMEM),
        scratch_shapes=[pltpu.VMEM((8, rows, LANE), F32), pltpu.SemaphoreType.DMA((7,)), pltpu.SemaphoreType.DMA((7,))],
        name="small_grads_all_reduce",
    )(gs)


PACK_SPLIT = (("w_in", 5584, (D, 2792)), ("w_uq", 96, (QL, 192)), ("w_ukv", 64, (KVL, 256)),
              ("w_out_a", 256, (CW, 256)), ("w_out_b", 256, (CW, 256)), ("w_out_c", 256, (CW, 256)),
              ("w_o", 512, (256, D)))
MAT_ROWS = 7024
CONV_SHARD = 3 * 128


def _pack_weights(wl):
    parts = [wl[n].astype(BF16).reshape(-1, PACK_W) for n, _, _ in PACK_SPLIT]
    cw = wl["conv_w"].reshape(-1)
    hi = cw.astype(BF16)
    r1 = cw - hi.astype(F32)
    mid = r1.astype(BF16)
    lo = (r1 - mid.astype(F32)).astype(BF16)
    cterms = jnp.pad(jnp.concatenate([hi, mid, lo]), (0, 3 * PACK_W - 3 * CONV_SHARD)).reshape(3, PACK_W)
    pad = jnp.zeros((PACK_ROWS - MAT_ROWS - 3, PACK_W), BF16)
    return jnp.concatenate(parts + [cterms, pad], axis=0)


def _unpack_weights(gath):
    out = {}
    r = 0
    for n, nrows, shp in PACK_SPLIT:
        t = gath[:, r:r + nrows].reshape((4,) + shp)
        r += nrows
        if n == "w_o":
            out[n] = t.reshape(4 * shp[0], shp[1])
        else:
            out[n] = t.transpose(1, 0, 2).reshape(shp[0], 4 * shp[1])
    w_in = out["w_in"]
    out["w_in"] = jnp.concatenate([w_in[:, :KPE_END], jnp.zeros((D, NINP - NIN), BF16), w_in[:, KPE_END:]], axis=1)
    ct = gath[:, r:r + 3].reshape(4, 3 * PACK_W)[:, :3 * CONV_SHARD].astype(F32).reshape(4, 3, CONV_SHARD)
    cw = (ct[:, 0] + ct[:, 1]) + ct[:, 2]
    out["conv_w"] = cw.reshape(4, 3, 128).transpose(1, 0, 2).reshape(3, CW)
    return out


def _pack_grads(g):
    parts = []
    w_in = jnp.concatenate([g["w_in"][:, :KPE_END], g["w_in"][:, KPE_END + NINP - NIN:]], axis=1)
    for n, nrows, shp in PACK_SPLIT:
        t = w_in if n == "w_in" else g[n]
        if n == "w_o":
            t = t.reshape((4,) + shp)
        else:
            t = t.reshape(shp[0], 4, shp[1]).transpose(1, 0, 2)
        parts.append(t.reshape(4, nrows, PACK_W))
    cw = g["conv_w"].reshape(3, 4, 128).transpose(1, 0, 2).reshape(4, 1, CONV_SHARD)
    parts.append(jnp.pad(cw, ((0, 0), (0, 0), (0, PACK_W - CONV_SHARD))))
    parts.append(jnp.zeros((4, PACK_ROWS - MAT_ROWS - 1, PACK_W), F32))
    return jnp.concatenate(parts, axis=1)


def _unpack_grads(red):
    out = {}
    r = 0
    for n, nrows, shp in PACK_SPLIT:
        out[n] = red[r:r + nrows].reshape(shp)
        r += nrows
    out["conv_w"] = red[r, :CONV_SHARD].reshape(3, 128)
    return out


SMALL_SIZES = (("norm_g", D), ("b_gate", 3 * D), ("conv_b", CW), ("q_a_norm_g", QL), ("kv_a_norm_g", KVL),
               ("mla_q_norm_g", QK), ("mla_k_norm_g", QK), ("dil_q_norm_g", NG * HD), ("dil_k_norm_g", NG * HD))
SMALL_ROWS = 88


def _pack_small(per_name):
    flat = jnp.concatenate([per_name[n].reshape(-1).astype(F32) for n, _ in SMALL_SIZES])
    return jnp.pad(flat, (0, SMALL_ROWS * LANE - flat.shape[0])).reshape(SMALL_ROWS, LANE)


def _unpack_small(packed, like):
    out = {}
    flat = packed.reshape(-1)
    r = 0
    for n, sz in SMALL_SIZES:
        out[n] = flat[r:r + NL * sz].reshape(like[n].shape)
        r += NL * sz
    return out


def _adamw_math(w, g, m, v):
    m = ADAM_B1 * m + (1.0 - ADAM_B1) * g
    v = ADAM_B2 * v + (1.0 - ADAM_B2) * jnp.square(g)
    m_hat = m / (1.0 - ADAM_B1 ** ADAM_STEP)
    v_hat = v / (1.0 - ADAM_B2 ** ADAM_STEP)
    delta = -ADAM_LR * (m_hat / (jnp.sqrt(v_hat) + ADAM_EPS) + ADAM_WD * w)
    return delta, m, v


def _adamw(name, w, g, m, v, br):
    L, R, C = w.shape
    blk = lambda l, i: (l, i, 0)
    return _pcall(name, _adamw_math, (L, R // br), [(t, (None, br, C), blk) for t in (w, g, m, v)],
                  [((L, R, C), F32, (None, br, C), blk)] * 3)


ADAM_ROWS = {"w_in": 128, "w_uq": 256, "w_ukv": 128, "w_out_a": 512, "w_out_b": 512, "w_out_c": 512, "w_o": 256,
             "conv_w": 3}


def kernel(x, norm_g, w_in, b_gate, conv_w, conv_b, q_a_norm_g, w_uq, kv_a_norm_g, w_ukv, mla_q_norm_g, mla_k_norm_g, dil_q_norm_g, dil_k_norm_g, w_out_a, w_out_b, w_out_c, w_o, loss_target, m_norm_g, m_w_in, m_b_gate, m_conv_w, m_conv_b, m_q_a_norm_g, m_w_uq, m_kv_a_norm_g, m_w_ukv, m_mla_q_norm_g, m_mla_k_norm_g, m_dil_q_norm_g, m_dil_k_norm_g, m_w_out_a, m_w_out_b, m_w_out_c, m_w_o, v_norm_g, v_w_in, v_b_gate, v_conv_w, v_conv_b, v_q_a_norm_g, v_w_uq, v_kv_a_norm_g, v_w_ukv, v_mla_q_norm_g, v_mla_k_norm_g, v_dil_q_norm_g, v_dil_k_norm_g, v_w_out_a, v_w_out_b, v_w_out_c, v_w_o):
    W = dict(norm_g=norm_g, w_in=w_in, b_gate=b_gate, conv_w=conv_w, conv_b=conv_b, q_a_norm_g=q_a_norm_g, w_uq=w_uq,
             kv_a_norm_g=kv_a_norm_g, w_ukv=w_ukv, mla_q_norm_g=mla_q_norm_g, mla_k_norm_g=mla_k_norm_g,
             dil_q_norm_g=dil_q_norm_g, dil_k_norm_g=dil_k_norm_g, w_out_a=w_out_a, w_out_b=w_out_b, w_out_c=w_out_c,
             w_o=w_o)
    M = dict(norm_g=m_norm_g, w_in=m_w_in, b_gate=m_b_gate, conv_w=m_conv_w, conv_b=m_conv_b, q_a_norm_g=m_q_a_norm_g,
             w_uq=m_w_uq, kv_a_norm_g=m_kv_a_norm_g, w_ukv=m_w_ukv, mla_q_norm_g=m_mla_q_norm_g,
             mla_k_norm_g=m_mla_k_norm_g, dil_q_norm_g=m_dil_q_norm_g, dil_k_norm_g=m_dil_k_norm_g, w_out_a=m_w_out_a,
             w_out_b=m_w_out_b, w_out_c=m_w_out_c, w_o=m_w_o)
    V = dict(norm_g=v_norm_g, w_in=v_w_in, b_gate=v_b_gate, conv_w=v_conv_w, conv_b=v_conv_b, q_a_norm_g=v_q_a_norm_g,
             w_uq=v_w_uq, kv_a_norm_g=v_kv_a_norm_g, w_ukv=v_w_ukv, mla_q_norm_g=v_mla_q_norm_g,
             mla_k_norm_g=v_mla_k_norm_g, dil_q_norm_g=v_dil_q_norm_g, dil_k_norm_g=v_dil_k_norm_g, w_out_a=v_w_out_a,
             w_out_b=v_w_out_b, w_out_c=v_w_out_c, w_o=v_w_o)
    batch = x.shape[0]
    T = batch * S

    ws = []
    for l in range(NL):
        full = _unpack_weights(_all_gather_chips(_pack_weights({n: W[n][l] for n in BIG + ("conv_w",)})))
        pad_qk = lambda t: jnp.pad(t, (0, QKP - QK)).reshape(1, QKP)
        full.update(
            norm_g=norm_g[l].reshape(1, D), b_gate=b_gate[l].reshape(1, 3 * D), conv_b=conv_b[l].reshape(1, CW),
            q_a_norm_g=q_a_norm_g[l].reshape(1, QL), kv_a_norm_g=kv_a_norm_g[l].reshape(1, KVL),
            mla_q_norm_g=pad_qk(mla_q_norm_g[l]), mla_k_norm_g=pad_qk(mla_k_norm_g[l]),
            dil_q_norm_g=dil_q_norm_g[l].reshape(NG, 1, HD), dil_k_norm_g=dil_k_norm_g[l].reshape(NG, 1, HD))
        ws.append(full)

    loss, dx, grads = _local_step(x.reshape(T, D), loss_target.reshape(T, D), ws, batch)
    loss = lax.psum(loss[0, 0], ("x", "y", "c"))
    grad_x = dx.reshape(batch, S, D)

    red = [_unpack_grads(_reduce_scatter(_pack_grads(grads[l]))) for l in range(NL)]
    G = {n: jnp.stack([red[l][n] for l in range(NL)]) for n in BIG + ("conv_w",)}
    small_g = {n: jnp.stack([grads[l][n].reshape(-1)[:sz] for l in range(NL)]) for n, sz in SMALL_SIZES}
    small_red = _all_reduce_small(_pack_small(small_g))
    G.update(_unpack_small(small_red, {n: W[n] for n in SMALL}))

    delta, new_m, new_v = {}, {}, {}
    for n in BIG + ("conv_w",):
        delta[n], new_m[n], new_v[n] = _adamw("adamw_" + n, W[n], G[n], M[n], V[n], ADAM_ROWS[n])
    sw, sm, sv = (_pack_small({n: t[n] for n in SMALL})[None] for t in (W, M, V))
    sd, snm, snv = _adamw("adamw_small", sw, small_red[None], sm, sv, SMALL_ROWS)
    like = {n: W[n] for n in SMALL}
    delta.update(_unpack_small(sd[0], like))
    new_m.update(_unpack_small(snm[0], like))
    new_v.update(_unpack_small(snv[0], like))

    return (loss, grad_x, *[G[n] for n in WEIGHTS], *[delta[n] for n in WEIGHTS],
            *[new_m[n] for n in WEIGHTS], *[new_v[n] for n in WEIGHTS])
```

```python
import functools

import numpy as np
import jax
import jax.numpy as jnp
from jax import lax
from jax.experimental import pallas as pl
from jax.experimental.pallas import tpu as pltpu

F32 = jnp.float32
BF16 = jnp.bfloat16

D = 1024
S = 2048
NL = 2
CW = 512
NH = 8
QL = 256
KVL = 128
NOPE = 64
ROPE = 32
VD = 64
QK = NOPE + ROPE
QKP = 128
ROPE_THETA = 10000.0
DIL = ((128, 1), (512, 4), (2048, 16))
NG = 3
DH = 8
HD = 64
DWID = DH * HD
QB = 128
EPS = 1e-6
NIN = 11168
NINP = 11264
O_A, O_CQ, O_CKV, O_KPE, O_BZ, O_DQ, O_DK, O_DV, O_CZ, O_G = 0, 2048, 2304, 2432, 2560, 3072, 4608, 6144, 7680, 8192
KPE_END = 2464
NEG = -1e30
MLA_SCALE = QK ** -0.5
DIL_SCALE = HD ** -0.5
LANE = 128
PACK_W = 512
VMEM_LIMIT = 48 * 1024 * 1024

ADAM_LR = 0.001
ADAM_B1 = 0.9
ADAM_B2 = 0.999
ADAM_EPS = 1e-08
ADAM_WD = 0.01
ADAM_STEP = 10

MESH = pl.DeviceIdType.MESH
BIG = ("w_in", "w_uq", "w_ukv", "w_out_a", "w_out_b", "w_out_c", "w_o")
SMALL = ("norm_g", "b_gate", "conv_b", "q_a_norm_g", "kv_a_norm_g", "mla_q_norm_g", "mla_k_norm_g",
         "dil_q_norm_g", "dil_k_norm_g")
WEIGHTS = ("norm_g", "w_in", "b_gate", "conv_w", "conv_b", "q_a_norm_g", "w_uq", "kv_a_norm_g", "w_ukv",
           "mla_q_norm_g", "mla_k_norm_g", "dil_q_norm_g", "dil_k_norm_g", "w_out_a", "w_out_b", "w_out_c", "w_o")


def _dot(a, b):
    return jnp.dot(a, b, preferred_element_type=F32)


def _dot_nt(a, b):
    return lax.dot_general(a, b, (((1,), (1,)), ((), ())), preferred_element_type=F32)


def _dot_tn(a, b):
    return lax.dot_general(a, b, (((0,), (0,)), ((), ())), preferred_element_type=F32)


def _pcall(name, fn, grid, ins, outs):
    n_in = len(ins)
    n_out = len(outs)
    acc_axis = len(grid) - 1
    is_acc = [len(o) > 4 and o[4] for o in outs]
    outs = [o[:4] for o in outs]

    def body(*refs):
        vals = fn(*[r[...] for r in refs[:n_in]])
        if not isinstance(vals, (tuple, list)):
            vals = (vals,)
        for k in range(n_out):
            r = refs[n_in + k]
            v = vals[k].astype(r.dtype).reshape(r.shape)
            if is_acc[k]:
                first = pl.program_id(acc_axis) == 0

                @pl.when(first)
                def _():
                    r[...] = v

                @pl.when(jnp.logical_not(first))
                def _():
                    r[...] += v
            else:
                r[...] = v

    return pl.pallas_call(
        body,
        grid=grid,
        in_specs=[pl.BlockSpec(bs, im) for _, bs, im in ins],
        out_specs=[pl.BlockSpec(bs, im) for _, _, bs, im in outs],
        out_shape=[jax.ShapeDtypeStruct(sh, dt) for sh, dt, _, _ in outs],
        name=name,
        compiler_params=pltpu.CompilerParams(
            dimension_semantics=("arbitrary",) * len(grid), vmem_limit_bytes=VMEM_LIMIT),
    )(*[a for a, _, _ in ins])


def _mm(name, a, b, *, ta=False, tb=False, out_dtype=F32, add=None, tm=512, tn=1024, tk=1024):
    if ta:
        K, M = a.shape
    else:
        M, K = a.shape
    if tb:
        N, K2 = b.shape
    else:
        K2, N = b.shape
    assert K == K2, (name, a.shape, b.shape)
    tm, tn, tk = min(tm, M), min(tn, N), min(tk, K)
    assert M % tm == 0 and N % tn == 0 and K % tk == 0, (name, M, N, K)
    nk = K // tk
    dims = (((0 if ta else 1,), (1 if tb else 0,)), ((), ()))
    a_spec = pl.BlockSpec((tk, tm), lambda j, i, k: (k, i)) if ta else pl.BlockSpec((tm, tk), lambda j, i, k: (i, k))
    b_spec = pl.BlockSpec((tn, tk), lambda j, i, k: (j, k)) if tb else pl.BlockSpec((tk, tn), lambda j, i, k: (k, j))
    o_spec = pl.BlockSpec((tm, tn), lambda j, i, k: (i, j))
    has_add = add is not None

    def body(*refs):
        a_ref, b_ref = refs[0], refs[1]
        add_ref = refs[2] if has_add else None
        o_ref = refs[3] if has_add else refs[2]
        p = lax.dot_general(a_ref[...].astype(BF16), b_ref[...].astype(BF16), dims, preferred_element_type=F32)
        if nk == 1:
            if has_add:
                p = p + add_ref[...]
            o_ref[...] = p.astype(out_dtype)
        else:
            acc = refs[-1]
            k = pl.program_id(2)

            @pl.when(k == 0)
            def _():
                acc[...] = p

            @pl.when(k > 0)
            def _():
                acc[...] += p

            @pl.when(k == nk - 1)
            def _():
                r = acc[...]
                if has_add:
                    r = r + add_ref[...]
                o_ref[...] = r.astype(out_dtype)

    in_specs = [a_spec, b_spec] + ([o_spec] if has_add else [])
    args = [a, b] + ([add] if has_add else [])
    return pl.pallas_call(
        body,
        grid=(N // tn, M // tm, nk),
        in_specs=in_specs,
        out_specs=o_spec,
        out_shape=jax.ShapeDtypeStruct((M, N), out_dtype),
        scratch_shapes=[pltpu.VMEM((tm, tn), F32)] if nk > 1 else [],
        name=name,
        compiler_params=pltpu.CompilerParams(
            dimension_semantics=("arbitrary", "arbitrary", "arbitrary"), vmem_limit_bytes=VMEM_LIMIT),
    )(*args)


def _vjp_of(f, n_diff):
    def g(*args, n_prim):
        prim = args[:n_diff]
        consts = args[n_diff:n_prim]
        cts = args[n_prim:]
        _, pull = jax.vjp(lambda *p: f(*p, *consts), *prim)
        out = jax.eval_shape(lambda *p: f(*p, *consts), *prim)
        if isinstance(out, (tuple, list)):
            cts = tuple(c.astype(o.dtype) for c, o in zip(cts, out))
        else:
            cts = cts[0].astype(out.dtype)
        return pull(cts)
    return g


def _rms(x, g, n=None):
    n = x.shape[-1] if n is None else n
    ms = jnp.sum(x * x, axis=-1, keepdims=True) / n
    return x * lax.rsqrt(ms + EPS) * g


def _silu(z):
    return z * jax.nn.sigmoid(z)


def _roll_rows(u, k):
    n = u.shape[0]
    r = pltpu.roll(u, k % n, 0)
    t = lax.broadcasted_iota(jnp.int32, u.shape, 0)
    if k > 0:
        return jnp.where(t >= k, r, 0.0)
    return jnp.where(t < n + k, r, 0.0)


@functools.partial(jax.custom_vjp, nondiff_argnums=(1,))
def _shift(u, k):
    return _roll_rows(u, k)


def _shift_fwd(u, k):
    return _roll_rows(u, k), None


def _shift_bwd(k, _, g):
    return (_roll_rows(g, -k),)


_shift.defvjp(_shift_fwd, _shift_bwd)


@functools.partial(jax.custom_vjp, nondiff_argnums=(1,))
def _lane_roll(u, k):
    return pltpu.roll(u, k % LANE, 1)


def _lane_roll_fwd(u, k):
    return pltpu.roll(u, k % LANE, 1), None


def _lane_roll_bwd(k, _, g):
    return (pltpu.roll(g, (-k) % LANE, 1),)


_lane_roll.defvjp(_lane_roll_fwd, _lane_roll_bwd)


def _conv_math(ab, ac, ax, az, cw, cb):
    u = ac * ax
    conv = cb + _shift(u, 2) * cw[0:1] + _shift(u, 1) * cw[1:2] + u * cw[2:3]
    return ab * conv * _silu(az)


def _mla_pre_math(cq, ckv, gq, gkv):
    return _rms(cq, gq), _rms(ckv, gkv)


def _rope_math(q, kn, kpe, gq, gk, c, s1, s2):
    lane = lax.broadcasted_iota(jnp.int32, kpe.shape, 1)
    pe = _lane_roll(jnp.where(lane < ROPE, kpe, 0.0), NOPE)

    def one(t, g):
        tn = _rms(t, g, QK)
        return tn * c + _lane_roll(tn, -16) * s1 + _lane_roll(tn, 16) * s2

    qs, ks = [], []
    for h in range(NH):
        sl = slice(h * QKP, (h + 1) * QKP)
        qs.append(one(q[:, sl], gq))
        ks.append(one(kn[:, sl] + pe, gk))
    return jnp.concatenate(qs, axis=1), jnp.concatenate(ks, axis=1)


def _gate_math(o, z):
    return o * _silu(z)


def _mergec_math(o0, o1, o2, l0, l1, l2, cz):
    m = lax.stop_gradient(jnp.maximum(jnp.maximum(l0, l1), l2))
    e0, e1, e2 = jnp.exp(l0 - m), jnp.exp(l1 - m), jnp.exp(l2 - m)
    den = e0 + e1 + e2
    oc = (e0 / den) * o0 + (e1 / den) * o1 + (e2 / den) * o2
    return oc * _silu(cz)


def _merge_math(g0, g1, g2, b0, b1, b2, pa, pb, pc):
    return (jax.nn.sigmoid(g0 + b0) * pa + jax.nn.sigmoid(g1 + b1) * pb) + jax.nn.sigmoid(g2 + b2) * pc


MLA_T = 256


def _mla_fwd(q, k, v):
    B = q.shape[0]
    T = MLA_T

    def body(q_ref, k_ref, v_ref, o_ref, l_ref):
        qi = pl.program_id(2)
        qb = q_ref[...]
        row = qi * T + lax.broadcasted_iota(jnp.int32, (T, T), 0)
        col = lax.broadcasted_iota(jnp.int32, (T, T), 1)
        lo = _lo_mask((T, LANE))

        def step(j, carry):
            m, l, acc = carry
            off = pl.multiple_of(j * T, T)
            kb = k_ref[pl.ds(off, T), :]
            vb = v_ref[pl.ds(off, T), :]
            keep = col + j * T <= row
            s = jnp.concatenate(
                [jnp.where(keep, _dot_nt(qb[:, e * QKP:(e + 1) * QKP], kb[:, e * QKP:(e + 1) * QKP]) * MLA_SCALE, NEG)
                 for e in (0, 1)], axis=0)
            m_new = jnp.maximum(m, jnp.max(s, axis=-1, keepdims=True))
            a = jnp.exp(m - m_new)
            p = jnp.exp(s - m_new)
            l = a * l + jnp.sum(p, axis=-1, keepdims=True)
            acc = a * acc + _dot(p.astype(BF16), vb)
            return m_new, l, acc

        init = (jnp.full((2 * T, 1), NEG, F32), jnp.zeros((2 * T, 1), F32), jnp.zeros((2 * T, LANE), F32))
        m, l, acc = lax.fori_loop(0, qi + 1, step, init)
        o = acc / l
        lse = m + jnp.log(l)
        o_ref[...] = jnp.where(lo, o[:T], o[T:])
        l_ref[...] = jnp.where(lo, lse[:T], lse[T:])

    return pl.pallas_call(
        body,
        grid=(B, NH // 2, S // T),
        in_specs=[pl.BlockSpec((None, T, 2 * QKP), lambda b, hp, i: (b, i, hp)),
                  pl.BlockSpec((None, S, 2 * QKP), lambda b, hp, i: (b, 0, hp)),
                  pl.BlockSpec((None, S, LANE), lambda b, hp, i: (b, 0, hp))],
        out_specs=[pl.BlockSpec((None, T, LANE), lambda b, hp, i: (b, i, hp))] * 2,
        out_shape=[jax.ShapeDtypeStruct((B, S, NH * VD), F32)] * 2,
        name="mla_attn_fwd",
        compiler_params=pltpu.CompilerParams(dimension_semantics=("arbitrary",) * 3, vmem_limit_bytes=VMEM_LIMIT),
    )(q, k, v)


def _mla_bwd(q, k, v, do, o, lse):
    B = q.shape[0]
    T = MLA_T
    NB = S // T

    def body(q_ref, k_ref, v_ref, do_ref, o_ref, l_ref, dq_ref, dk_ref, dv_ref, delta_ref):
        dq_ref[...] = jnp.zeros((S, 2 * QKP), F32)
        delta_ref[...] = _head_sum(do_ref[...] * o_ref[...])
        row = lax.broadcasted_iota(jnp.int32, (T, T), 0)
        col = lax.broadcasted_iota(jnp.int32, (T, T), 1)
        lo = _lo_mask((T, LANE))

        def kv_step(j, _):
            koff = pl.multiple_of(j * T, T)
            kb = k_ref[pl.ds(koff, T), :]
            vb = v_ref[pl.ds(koff, T), :]

            def q_step(i, carry):
                dk0, dk1, dv = carry
                qoff = pl.multiple_of(i * T, T)
                qb = q_ref[pl.ds(qoff, T), :]
                do2 = _stack_heads(do_ref[pl.ds(qoff, T), :], lo).astype(BF16)
                lb = l_ref[pl.ds(qoff, T), :]
                db = delta_ref[pl.ds(qoff, T), :]
                keep = col + koff <= row + qoff
                dp2 = _dot_nt(do2, vb)
                dks = []
                for e in (0, 1):
                    qe = qb[:, e * QKP:(e + 1) * QKP]
                    ke = kb[:, e * QKP:(e + 1) * QKP]
                    s = jnp.where(keep, _dot_nt(qe, ke) * MLA_SCALE, NEG)
                    p = jnp.exp(s - lb[:, e * HD:e * HD + 1])
                    dv = dv + _dot_tn(p.astype(BF16), do2[e * T:(e + 1) * T])
                    ds = (p * (dp2[e * T:(e + 1) * T] - db[:, e * HD:e * HD + 1]) * MLA_SCALE).astype(BF16)
                    dks.append(_dot_tn(ds, qe))
                    dq_ref[pl.ds(qoff, T), e * QKP:(e + 1) * QKP] += _dot(ds, ke)
                return dk0 + dks[0], dk1 + dks[1], dv

            z = jnp.zeros((T, QKP), F32)
            dk0, dk1, dv = lax.fori_loop(j, NB, q_step, (z, z, jnp.zeros((T, LANE), F32)))
            dk_ref[pl.ds(koff, T), 0:QKP] = dk0
            dk_ref[pl.ds(koff, T), QKP:2 * QKP] = dk1
            dv_ref[pl.ds(koff, T), :] = dv
            return 0

        lax.fori_loop(0, NB, kv_step, 0)

    def spec(w):
        return pl.BlockSpec((None, S, w), lambda b, hp: (b, 0, hp))

    return pl.pallas_call(
        body,
        grid=(B, NH // 2),
        in_specs=[spec(2 * QKP), spec(2 * QKP), spec(LANE), spec(LANE), spec(LANE), spec(LANE)],
        out_specs=[spec(2 * QKP), spec(2 * QKP), spec(LANE)],
        out_shape=[jax.ShapeDtypeStruct((B, S, NH * QKP), F32), jax.ShapeDtypeStruct((B, S, NH * QKP), F32),
                   jax.ShapeDtypeStruct((B, S, NH * VD), F32)],
        scratch_shapes=[pltpu.VMEM((S, LANE), F32)],
        name="mla_attn_bwd",
        compiler_params=pltpu.CompilerParams(dimension_semantics=("arbitrary",) * 2, vmem_limit_bytes=VMEM_LIMIT),
    )(q, k, v, do, o, lse)


def _lo_mask(shape):
    return lax.broadcasted_iota(jnp.int32, shape, len(shape) - 1) < HD


def _head_sum(u):
    lo = _lo_mask(u.shape)
    s_lo = jnp.sum(jnp.where(lo, u, 0.0), axis=-1, keepdims=True)
    s_hi = jnp.sum(jnp.where(lo, 0.0, u), axis=-1, keepdims=True)
    return jnp.where(lo, s_lo, s_hi)


def _rms2(x, g):
    return x * lax.rsqrt(_head_sum(x * x) / HD + EPS) * g


def _dil_bias(t_ref, gi, d):
    qq = lax.broadcasted_iota(jnp.int32, (QB, QB), 0)
    kk = lax.broadcasted_iota(jnp.int32, (QB, QB), 1)
    jc = (qq - kk).astype(F32)
    rows = []
    for e in (0, 1):
        sl = t_ref[2 * gi + e:2 * gi + e + 1, :] * float(d)
        bp = jnp.where(kk >= qq, -sl * (jc + float(QB)), NEG)
        bc = jnp.where(kk <= qq, -sl * jc, NEG)
        rows.append(jnp.concatenate([bp, bc], axis=1))
    return jnp.concatenate(rows, axis=0)


def _dil_rows(cur, d):
    return pl.ds(cur, QB, stride=d) if d > 1 else pl.ds(pl.multiple_of(cur, QB), QB)


def _dil_walk(d, block):
    if d == 1:
        block(0, None)

        def body(i, c):
            block(i * QB, (i - 1) * QB)
            return c
        lax.fori_loop(1, S // QB, body, 0)
    elif d == 16:
        def body(r, c):
            block(r, None)
            return c
        lax.fori_loop(0, d, body, 0)
    else:
        nb = S // d // QB

        def cls(r, c):
            block(r, None)

            def body(i, c2):
                block(r + i * QB * d, r + (i - 1) * QB * d)
                return c2
            lax.fori_loop(1, nb, body, 0)
            return c
        lax.fori_loop(0, d, cls, 0)


def _stack_heads(x, lo):
    return jnp.concatenate([jnp.where(lo, x, 0.0), jnp.where(lo, 0.0, x)], axis=0)


def _dilc_fwd(proj3, gq, gk, tab):
    B = proj3.shape[0]

    def body(q_ref, k_ref, v_ref, cz_ref, gq_ref, gk_ref, t_ref, y_ref, o_ref, l_ref, qs, ks, vs):
        g = pl.program_id(2)
        lo = _lo_mask((QB, LANE))

        def group(gi):
            d = DIL[gi][1]
            qs[...] = _rms2(q_ref[...], gq_ref[gi:gi + 1, :])
            ks[...] = _rms2(k_ref[...], gk_ref[gi:gi + 1, :])
            vs[...] = v_ref[...]
            bias = _dil_bias(t_ref, gi, d)

            def block(cur, prev):
                rows = _dil_rows(cur, d)
                q2 = _stack_heads(qs[rows, :], lo).astype(BF16)
                kc, vc = ks[rows, :], vs[rows, :]
                if prev is None:
                    kcat, vcat, b = kc, vc, bias[:, QB:]
                else:
                    prow = _dil_rows(prev, d)
                    kcat = jnp.concatenate([ks[prow, :], kc], axis=0)
                    vcat = jnp.concatenate([vs[prow, :], vc], axis=0)
                    b = bias
                s = _dot_nt(q2, kcat.astype(BF16)) * DIL_SCALE + b
                m = jnp.max(s, axis=-1, keepdims=True)
                p = jnp.exp(s - m)
                l = jnp.sum(p, axis=-1, keepdims=True)
                o = _dot(p.astype(BF16), vcat.astype(BF16)) / l
                lse = m + jnp.log(l)
                o_ref[gi, rows, :] = jnp.where(lo, o[:QB], o[QB:])
                l_ref[gi, rows, :] = jnp.where(lo, lse[:QB], lse[QB:])

            _dil_walk(d, block)

        for gi in range(NG):
            pl.when(g == gi)(functools.partial(group, gi))

        @pl.when(g == NG - 1)
        def _():
            y_ref[...] = _mergec_math(o_ref[0], o_ref[1], o_ref[2], l_ref[0], l_ref[1], l_ref[2],
                                      cz_ref[...]).astype(BF16)

    def col(base):
        return pl.BlockSpec((None, S, LANE), lambda b, hp, g: (b, 0, base // LANE + 4 * g + hp))

    gspec = pl.BlockSpec((NG, LANE), lambda b, hp, g: (0, 0))
    saved = pl.BlockSpec((NG, None, S, LANE), lambda b, hp, g: (0, b, 0, hp))
    return pl.pallas_call(
        body,
        grid=(B, 4, NG),
        in_specs=[col(O_DQ), col(O_DK), col(O_DV),
                  pl.BlockSpec((None, S, LANE), lambda b, hp, g: (b, 0, O_CZ // LANE + hp)),
                  gspec, gspec, pl.BlockSpec((None, 8, LANE), lambda b, hp, g: (hp, 0, 0))],
        out_specs=[pl.BlockSpec((None, S, LANE), lambda b, hp, g: (b, 0, hp)), saved, saved],
        out_shape=[jax.ShapeDtypeStruct((B, S, DWID), BF16), jax.ShapeDtypeStruct((NG, B, S, DWID), F32),
                   jax.ShapeDtypeStruct((NG, B, S, DWID), F32)],
        scratch_shapes=[pltpu.VMEM((S, LANE), F32)] * 3,
        name="dil_mixer_fwd",
        compiler_params=pltpu.CompilerParams(dimension_semantics=("arbitrary",) * 3, vmem_limit_bytes=VMEM_LIMIT),
    )(proj3, proj3, proj3, proj3, gq, gk, tab)


def _dilc_bwd(proj3, gq, gk, tab, o_all, l_all, d_yc):
    B = proj3.shape[0]

    def body(q_ref, k_ref, v_ref, cz_ref, gq_ref, gk_ref, t_ref, o_ref, l_ref, dy_ref,
             dq_out, dk_out, dv_out, dcz_out, dgq_out, dgk_out, qs, ks, vs, dos, dls, dqs, dks, dvs):
        g = pl.program_id(2)
        lo = _lo_mask((QB, LANE))

        @pl.when(jnp.logical_and(jnp.logical_and(pl.program_id(0) == 0, pl.program_id(1) == 0), g == 0))
        def _():
            dgq_out[...] = jnp.zeros((NG, LANE), F32)
            dgk_out[...] = jnp.zeros((NG, LANE), F32)

        def group(gi):
            d = DIL[gi][1]
            ls = [l_ref[j] for j in range(NG)]
            m = jnp.maximum(jnp.maximum(ls[0], ls[1]), ls[2])
            es = [jnp.exp(t - m) for t in ls]
            den = (es[0] + es[1]) + es[2]
            al = [e / den for e in es]
            os_ = [o_ref[j] for j in range(NG)]
            oc = (al[0] * os_[0] + al[1] * os_[1]) + al[2] * os_[2]
            cz = cz_ref[...]
            sg = jax.nn.sigmoid(cz)
            dy = dy_ref[...]
            d_oc = dy * (cz * sg)
            dcz_out[...] = (dy * oc * (sg * (1.0 + cz * (1.0 - sg)))).astype(BF16)
            ts = [_head_sum(d_oc * os_[j]) for j in range(NG)]
            tbar = (al[0] * ts[0] + al[1] * ts[1]) + al[2] * ts[2]
            dos[...] = al[gi] * d_oc
            dls[...] = al[gi] * (ts[gi] - tbar)

            qs[...] = _rms2(q_ref[...], gq_ref[gi:gi + 1, :])
            ks[...] = _rms2(k_ref[...], gk_ref[gi:gi + 1, :])
            vs[...] = v_ref[...]
            dks[...] = jnp.zeros((S, LANE), F32)
            dvs[...] = jnp.zeros((S, LANE), F32)
            bias = _dil_bias(t_ref, gi, d)

            def block(cur, prev):
                rows = _dil_rows(cur, d)
                q2 = _stack_heads(qs[rows, :], lo).astype(BF16)
                dob = dos[rows, :]
                do2 = _stack_heads(dob, lo).astype(BF16)
                kc, vc = ks[rows, :], vs[rows, :]
                if prev is None:
                    kcat, vcat, b = kc, vc, bias[:, QB:]
                else:
                    prow = _dil_rows(prev, d)
                    kcat = jnp.concatenate([ks[prow, :], kc], axis=0)
                    vcat = jnp.concatenate([vs[prow, :], vc], axis=0)
                    b = bias
                kcat = kcat.astype(BF16)
                vcat = vcat.astype(BF16)
                lse_b = l_ref[gi, rows, :]
                corr_b = dls[rows, :] - _head_sum(dob * o_ref[gi, rows, :])
                lse2 = jnp.concatenate([lse_b[:, 0:1], lse_b[:, HD:HD + 1]], axis=0)
                corr2 = jnp.concatenate([corr_b[:, 0:1], corr_b[:, HD:HD + 1]], axis=0)
                s = _dot_nt(q2, kcat) * DIL_SCALE + b
                p = jnp.exp(s - lse2)
                ds = (p * (_dot_nt(do2, vcat) + corr2) * DIL_SCALE).astype(BF16)
                dq2 = _dot(ds, kcat)
                dqs[rows, :] = jnp.where(lo, dq2[:QB], dq2[QB:])
                dk = _dot_tn(ds, q2)
                dv = _dot_tn(p.astype(BF16), do2)
                if prev is None:
                    dks[rows, :] += dk
                    dvs[rows, :] += dv
                else:
                    dks[prow, :] += dk[:QB]
                    dvs[prow, :] += dv[:QB]
                    dks[rows, :] += dk[QB:]
                    dvs[rows, :] += dv[QB:]

            _dil_walk(d, block)

            _, pull_q = jax.vjp(_rms2, q_ref[...], gq_ref[gi:gi + 1, :])
            dxq, dgq = pull_q(dqs[...])
            dq_out[...] = dxq.astype(BF16)
            dgq_out[gi:gi + 1, :] += dgq
            _, pull_k = jax.vjp(_rms2, k_ref[...], gk_ref[gi:gi + 1, :])
            dxk, dgk = pull_k(dks[...])
            dk_out[...] = dxk.astype(BF16)
            dgk_out[gi:gi + 1, :] += dgk
            dv_out[...] = dvs[...].astype(BF16)

        for gi in range(NG):
            pl.when(g == gi)(functools.partial(group, gi))

    def col(base):
        return pl.BlockSpec((None, S, LANE), lambda b, hp, g: (b, 0, base // LANE + 4 * g + hp))

    gspec = pl.BlockSpec((NG, LANE), lambda b, hp, g: (0, 0))
    saved = pl.BlockSpec((NG, None, S, LANE), lambda b, hp, g: (0, b, 0, hp))
    per_pair = pl.BlockSpec((None, S, LANE), lambda b, hp, g: (b, 0, hp))
    dcol = pl.BlockSpec((None, S, LANE), lambda b, hp, g: (b, 0, 4 * g + hp))
    return pl.pallas_call(
        body,
        grid=(B, 4, NG),
        in_specs=[col(O_DQ), col(O_DK), col(O_DV),
                  pl.BlockSpec((None, S, LANE), lambda b, hp, g: (b, 0, O_CZ // LANE + hp)),
                  gspec, gspec, pl.BlockSpec((None, 8, LANE), lambda b, hp, g: (hp, 0, 0)),
                  saved, saved, per_pair],
        out_specs=[dcol, dcol, dcol, per_pair, gspec, gspec],
        out_shape=[jax.ShapeDtypeStruct((B, S, NG * DWID), BF16)] * 3
        + [jax.ShapeDtypeStruct((B, S, DWID), BF16), jax.ShapeDtypeStruct((NG, LANE), F32),
           jax.ShapeDtypeStruct((NG, LANE), F32)],
        scratch_shapes=[pltpu.VMEM((S, LANE), F32)] * 8,
        name="dil_mixer_bwd",
        compiler_params=pltpu.CompilerParams(dimension_semantics=("arbitrary",) * 3, vmem_limit_bytes=VMEM_LIMIT),
    )(proj3, proj3, proj3, proj3, gq, gk, tab, o_all, l_all, d_yc)


def _dil_slopes():
    slopes = (2.0 ** (-8.0 * np.arange(1, NG * DH + 1, dtype=np.float32) / (NG * DH))).astype(np.float32).reshape(NG, DH)
    tab = np.zeros((4, 8, LANE), np.float32)
    for hp in range(4):
        for gi in range(NG):
            for e in (0, 1):
                tab[hp, 2 * gi + e, :] = slopes[gi, 2 * hp + e]
    return jnp.asarray(tab)


def _rope_tables():
    inv = ROPE_THETA ** (-jnp.arange(0, ROPE, 2, dtype=F32) / ROPE)
    ang = jnp.arange(S, dtype=F32)[:, None] * inv[None, :]
    cos, sin = jnp.cos(ang), jnp.sin(ang)
    z16 = jnp.zeros((S, 16), F32)
    c = jnp.concatenate([jnp.ones((S, NOPE), F32), cos, cos, jnp.zeros((S, 32), F32)], axis=1)
    s1 = jnp.concatenate([jnp.zeros((S, NOPE), F32), -sin, z16, jnp.zeros((S, 32), F32)], axis=1)
    s2 = jnp.concatenate([jnp.zeros((S, NOPE), F32), z16, sin, jnp.zeros((S, 32), F32)], axis=1)
    return c, s1, s2


def _pad_heads_uq(w):
    return jnp.pad(w.reshape(QL, NH, QK), ((0, 0), (0, 0), (0, QKP - QK))).reshape(QL, NH * QKP)


def _unpad_heads_uq(g):
    return g.reshape(QL, NH, QKP)[:, :, :QK].reshape(QL, NH * QK)


def _split_ukv(w):
    w3 = w.reshape(KVL, NH, NOPE + VD)
    uk = jnp.pad(w3[:, :, :NOPE], ((0, 0), (0, 0), (0, QKP - NOPE))).reshape(KVL, NH * QKP)
    return uk, w3[:, :, NOPE:].reshape(KVL, NH * VD)


def _join_ukv(guk, guv):
    return jnp.concatenate([guk.reshape(KVL, NH, QKP)[:, :, :NOPE], guv.reshape(KVL, NH, VD)],
                           axis=-1).reshape(KVL, NH * (NOPE + VD))


BR = 512
BRM = 256


def _layer_fwd(x, w, tabs, batch):
    T = batch * S
    rope_c, rope_s1, rope_s2, dil_tab = tabs
    res = {"x": x}
    row = lambda c: (lambda i: (i, c))
    fix = lambda i: (0, 0)

    h = _pcall("norm_fwd", _rms, (T // BR,),
               [(x, (BR, D), row(0)), (w["norm_g"], (1, D), fix)],
               [((T, D), BF16, (BR, D), row(0))])[0]
    proj = _mm("in_proj", h, w["w_in_t"], tb=True, tm=512, tn=1024)
    res["h"], res["proj"] = h, proj
    proj3 = proj.reshape(batch, S, NINP)

    cblk = lambda s: (lambda j, b: (b, 0, 4 * s + j))
    y_a = _pcall("conv_fwd", _conv_math, (4, batch),
                 [(proj3, (None, S, LANE), cblk(0)), (proj3, (None, S, LANE), cblk(1)),
                  (proj3, (None, S, LANE), cblk(2)), (proj3, (None, S, LANE), cblk(3)),
                  (w["conv_w"], (3, LANE), lambda j, b: (0, j)), (w["conv_b"], (1, LANE), lambda j, b: (0, j))],
                 [((batch, S, CW), BF16, (None, S, LANE), lambda j, b: (b, 0, j))])[0].reshape(T, CW)
    res["y_a"] = y_a

    cqn, ckvn = _pcall("mla_pre_fwd", _mla_pre_math, (T // BR,),
                       [(proj, (BR, QL), row(O_CQ // QL)), (proj, (BR, KVL), row(O_CKV // KVL)),
                        (w["q_a_norm_g"], (1, QL), fix), (w["kv_a_norm_g"], (1, KVL), fix)],
                       [((T, QL), BF16, (BR, QL), row(0)), ((T, KVL), BF16, (BR, KVL), row(0))])
    w_uq_p = _pad_heads_uq(w["w_uq"])
    w_uk, w_uv = _split_ukv(w["w_ukv"])
    q = _mm("uq", cqn, w_uq_p)
    kn = _mm("uk", ckvn, w_uk)
    v = _mm("uv", ckvn, w_uv, out_dtype=BF16)
    nrr = S // BR
    tab_row = lambda i: (i % nrr, 0)
    qr, kr = _pcall("rope_fwd", _rope_math, (T // BR,),
                    [(q, (BR, NH * QKP), row(0)), (kn, (BR, NH * QKP), row(0)), (proj, (BR, LANE), row(O_KPE // LANE)),
                     (w["mla_q_norm_g"], (1, QKP), fix), (w["mla_k_norm_g"], (1, QKP), fix),
                     (rope_c, (BR, QKP), tab_row), (rope_s1, (BR, QKP), tab_row), (rope_s2, (BR, QKP), tab_row)],
                    [((T, NH * QKP), BF16, (BR, NH * QKP), row(0))] * 2)
    qr = qr.reshape(batch, S, NH * QKP)
    kr = kr.reshape(batch, S, NH * QKP)
    v = v.reshape(batch, S, NH * VD)
    o_b, l_b = _mla_fwd(qr, kr, v)
    ob2 = o_b.reshape(T, NH * VD)
    y_b = _pcall("gateb_fwd", _gate_math, (T // BR,),
                 [(ob2, (BR, 512), row(0)), (proj, (BR, 512), row(O_BZ // 512))],
                 [((T, 512), BF16, (BR, 512), row(0))])[0]
    res.update(cqn=cqn, ckvn=ckvn, q=q, kn=kn, qr=qr, kr=kr, v=v, o_b=o_b, l_b=l_b, ob2=ob2, y_b=y_b,
               w_uq_p=w_uq_p, w_uk=w_uk, w_uv=w_uv)

    gq2 = jnp.tile(w["dil_q_norm_g"].reshape(NG, HD), (1, 2))
    gk2 = jnp.tile(w["dil_k_norm_g"].reshape(NG, HD), (1, 2))
    y_c, o_all, l_all = _dilc_fwd(proj3, gq2, gk2, dil_tab)
    y_c = y_c.reshape(T, DWID)
    res.update(o_all=o_all, l_all=l_all, y_c=y_c)

    pa = _mm("out_a", y_a, w["w_out_a"])
    pb = _mm("out_b", y_b, w["w_out_b"])
    pc = _mm("out_c", y_c, w["w_out_c"])
    merged = _pcall("merge_fwd", _merge_math, (T // BRM,),
                    [(proj, (BRM, D), row(O_G // D + s)) for s in range(3)]
                    + [(w["b_gate"], (1, D), (lambda s: (lambda i: (0, s)))(s)) for s in range(3)]
                    + [(t, (BRM, D), row(0)) for t in (pa, pb, pc)],
                    [((T, D), BF16, (BRM, D), row(0))])[0]
    out = _mm("o_proj", merged, w["w_o"], add=x)
    res.update(pa=pa, pb=pb, pc=pc, merged=merged)
    return out, res


def _norm_bwd_math(x, g, dh, dy):
    _, pull = jax.vjp(_rms, x, g)
    dx, dg = pull(dh)
    return dx + dy, dg


def _layer_bwd(dy, w, res, tabs, batch):
    T = batch * S
    rope_c, rope_s1, rope_s2, dil_tab = tabs
    row = lambda c: (lambda i: (i, c))
    fix = lambda i: (0, 0)
    x, proj, h = res["x"], res["proj"], res["h"]
    proj3 = proj.reshape(batch, S, NINP)
    g = {}

    d_merged = _mm("o_proj_dx", dy, w["w_o"], tb=True)
    g["w_o"] = _mm("o_proj_dw", res["merged"], dy, ta=True, tm=1024)

    merge_bwd = functools.partial(_vjp_of(_merge_math, 9), n_prim=9)
    dg0, dg1, dg2, db0, db1, db2, dpa, dpb, dpc = _pcall(
        "merge_bwd", merge_bwd, (T // BRM,),
        [(proj, (BRM, D), row(O_G // D + s)) for s in range(3)]
        + [(w["b_gate"], (1, D), (lambda s: (lambda i: (0, s)))(s)) for s in range(3)]
        + [(t, (BRM, D), row(0)) for t in (res["pa"], res["pb"], res["pc"])]
        + [(d_merged, (BRM, D), row(0))],
        [((T, D), BF16, (BRM, D), row(0))] * 3 + [((1, D), F32, (1, D), fix, True)] * 3
        + [((T, D), BF16, (BRM, D), row(0))] * 3)
    g["b_gate"] = jnp.concatenate([db0, db1, db2], axis=1)

    d_ya = _mm("out_a_dx", dpa, w["w_out_a"], tb=True)
    d_yb = _mm("out_b_dx", dpb, w["w_out_b"], tb=True)
    d_yc = _mm("out_c_dx", dpc, w["w_out_c"], tb=True)
    g["w_out_a"] = _mm("out_a_dw", res["y_a"], dpa, ta=True)
    g["w_out_b"] = _mm("out_b_dw", res["y_b"], dpb, ta=True)
    g["w_out_c"] = _mm("out_c_dw", res["y_c"], dpc, ta=True)

    cblk = lambda s: (lambda j, b: (b, 0, 4 * s + j))
    oblk = lambda j, b: (b, 0, j)
    conv_bwd = functools.partial(_vjp_of(_conv_math, 6), n_prim=6)
    d_ab, d_ac, d_ax, d_az, g["conv_w"], g["conv_b"] = _pcall(
        "conv_bwd", conv_bwd, (4, batch),
        [(proj3, (None, S, LANE), cblk(s)) for s in range(4)]
        + [(w["conv_w"], (3, LANE), lambda j, b: (0, j)), (w["conv_b"], (1, LANE), lambda j, b: (0, j)),
           (d_ya.reshape(batch, S, CW), (None, S, LANE), oblk)],
        [((batch, S, CW), BF16, (None, S, LANE), oblk)] * 4
        + [((3, CW), F32, (3, LANE), lambda j, b: (0, j), True), ((1, CW), F32, (1, LANE), lambda j, b: (0, j), True)])

    gate_bwd = functools.partial(_vjp_of(_gate_math, 2), n_prim=2)
    d_ob, d_bz = _pcall("gateb_bwd", gate_bwd, (T // BR,),
                        [(res["ob2"], (BR, 512), row(0)), (proj, (BR, 512), row(O_BZ // 512)), (d_yb, (BR, 512), row(0))],
                        [((T, 512), F32, (BR, 512), row(0)), ((T, 512), BF16, (BR, 512), row(0))])
    dqr, dkr, dv = _mla_bwd(res["qr"], res["kr"], res["v"], d_ob.reshape(batch, S, NH * VD), res["o_b"], res["l_b"])
    nrr = S // BR
    tab_row = lambda i: (i % nrr, 0)
    rope_bwd = functools.partial(_vjp_of(_rope_math, 5), n_prim=8)
    d_q, d_kn, d_kpe_p, g["mla_q_norm_g"], g["mla_k_norm_g"] = _pcall(
        "rope_bwd", rope_bwd, (T // BR,),
        [(res["q"], (BR, NH * QKP), row(0)), (res["kn"], (BR, NH * QKP), row(0)), (proj, (BR, LANE), row(O_KPE // LANE)),
         (w["mla_q_norm_g"], (1, QKP), fix), (w["mla_k_norm_g"], (1, QKP), fix),
         (rope_c, (BR, QKP), tab_row), (rope_s1, (BR, QKP), tab_row), (rope_s2, (BR, QKP), tab_row),
         (dqr.reshape(T, NH * QKP), (BR, NH * QKP), row(0)), (dkr.reshape(T, NH * QKP), (BR, NH * QKP), row(0))],
        [((T, NH * QKP), BF16, (BR, NH * QKP), row(0))] * 2 + [((T, LANE), BF16, (BR, LANE), row(0))]
        + [((1, QKP), F32, (1, QKP), fix, True)] * 2)
    dv = dv.reshape(T, NH * VD)
    d_cqn = _mm("uq_dx", d_q, res["w_uq_p"], tb=True)
    d_ckvn = _mm("uk_dx", d_kn, res["w_uk"], tb=True)
    d_ckvn = _mm("uv_dx", dv, res["w_uv"], tb=True, add=d_ckvn)
    g["w_uq"] = _unpad_heads_uq(_mm("uq_dw", res["cqn"], d_q, ta=True))
    g["w_ukv"] = _join_ukv(_mm("uk_dw", res["ckvn"], d_kn, ta=True), _mm("uv_dw", res["ckvn"], dv, ta=True))
    pre_bwd = functools.partial(_vjp_of(_mla_pre_math, 4), n_prim=4)
    d_cq, d_ckv, g["q_a_norm_g"], g["kv_a_norm_g"] = _pcall(
        "mla_pre_bwd", pre_bwd, (T // BR,),
        [(proj, (BR, QL), row(O_CQ // QL)), (proj, (BR, KVL), row(O_CKV // KVL)),
         (w["q_a_norm_g"], (1, QL), fix), (w["kv_a_norm_g"], (1, KVL), fix),
         (d_cqn, (BR, QL), row(0)), (d_ckvn, (BR, KVL), row(0))],
        [((T, QL), BF16, (BR, QL), row(0)), ((T, KVL), BF16, (BR, KVL), row(0)),
         ((1, QL), F32, (1, QL), fix, True), ((1, KVL), F32, (1, KVL), fix, True)])

    gq2 = jnp.tile(w["dil_q_norm_g"].reshape(NG, HD), (1, 2))
    gk2 = jnp.tile(w["dil_k_norm_g"].reshape(NG, HD), (1, 2))
    d_dq, d_dk, d_dv, d_cz, dgq, dgk = _dilc_bwd(proj3, gq2, gk2, dil_tab, res["o_all"], res["l_all"],
                                                 d_yc.reshape(batch, S, DWID))
    g["dil_q_norm_g"] = dgq[:, :HD] + dgq[:, HD:]
    g["dil_k_norm_g"] = dgk[:, :HD] + dgk[:, HD:]
    d_dq, d_dk, d_dv = (t.reshape(T, NG * DWID) for t in (d_dq, d_dk, d_dv))
    d_cz = d_cz.reshape(T, DWID)

    dproj = jnp.concatenate(
        [t.reshape(T, CW) for t in (d_ab, d_ac, d_ax, d_az)]
        + [d_cq, d_ckv, d_kpe_p, d_bz, d_dq, d_dk, d_dv, d_cz, dg0, dg1, dg2], axis=1)
    d_h = _mm("in_proj_dx", dproj, w["w_in_t"], tm=1024)
    g["w_in_t"] = _mm("in_proj_dw", dproj, h, ta=True, tm=1024)
    dx, g["norm_g"] = _pcall("norm_bwd", _norm_bwd_math, (T // BR,),
                             [(x, (BR, D), row(0)), (w["norm_g"], (1, D), fix), (d_h, (BR, D), row(0)),
                              (dy, (BR, D), row(0))],
                             [((T, D), F32, (BR, D), row(0)), ((1, D), F32, (1, D), fix, True)])
    return dx, g


def _loss_math(y, t):
    e = y - t
    return e * (1.0 / D), 0.5 * jnp.sum(jnp.sum(e * e, axis=-1, keepdims=True) / D, axis=0, keepdims=True)


def _local_step(x, target, ws, batch):
    T = batch * S
    tabs = _rope_tables() + (_dil_slopes(),)
    saved = []
    y = x
    for l in range(NL):
        y, res = _layer_fwd(y, ws[l], tabs, batch)
        saved.append(res)
    row = lambda i: (i, 0)
    dy, loss = _pcall("loss", _loss_math, (T // BR,),
                      [(y, (BR, D), row), (target, (BR, D), row)],
                      [((T, D), F32, (BR, D), row), ((1, 1), F32, (1, 1), lambda i: (0, 0), True)])
    grads = [None] * NL
    for l in reversed(range(NL)):
        dy, grads[l] = _layer_bwd(dy, ws[l], saved[l], tabs, batch)
    return loss, dy, grads


ANY = pl.BlockSpec(memory_space=pl.ANY)
U32 = jnp.uint32
WSH = NIN // 4
WA = KPE_END
WB = WSH - WA
CWD = 512
PACK_ROWS = 1472
HW = PACK_W // 2


def _me():
    return lax.axis_index("x"), lax.axis_index("y"), lax.axis_index("c")


def _piece_rows(k):
    a = k * WSH + jnp.where(k > 0, NINP - NIN, 0)
    b = k * WSH + WA + (NINP - NIN)
    return ((0, pl.multiple_of(a, 8), WA), (WA, pl.multiple_of(b, 8), WB))


def _pack_words(lo, hi):
    ul = lax.bitcast_convert_type(lo.astype(BF16).astype(F32), U32)
    uh = lax.bitcast_convert_type(hi.astype(BF16).astype(F32), U32)
    w = jnp.bitwise_or(jnp.bitwise_and(uh, jnp.uint32(0xFFFF0000)), jnp.right_shift(ul, jnp.uint32(16)))
    return lax.bitcast_convert_type(w, F32)


def _unpack_words(w):
    w = lax.bitcast_convert_type(w, U32)
    lo = lax.bitcast_convert_type(jnp.left_shift(w, jnp.uint32(16)), F32)
    hi = lax.bitcast_convert_type(jnp.bitwise_and(w, jnp.uint32(0xFFFF0000)), F32)
    return lo, hi


def _all_gather(wc, sp):
    def body(w_ref, s_ref, ow_ref, os_ref, send_sems, recv_sems, local_sems):
        x, y, c = _me()
        k_me = 2 * x + y
        sib = (x, y, 1 - c)
        chips = [(1 - x, y), (x, 1 - y), (1 - x, 1 - y)]
        wcols = lambda cc: pl.ds(pl.multiple_of(cc * (CWD // 2), LANE), CWD // 2)
        scols = lambda cc: pl.ds(pl.multiple_of(cc * HW, LANE), HW)

        def windows(k, cc):
            pcs = _piece_rows(k)
            return ([(w_ref.at[pl.ds(l0, n), wcols(cc)], ow_ref.at[pl.ds(p0, n), wcols(cc)]) for l0, p0, n in pcs]
                    + [(s_ref.at[:, scols(cc)], os_ref.at[k, :, scols(cc)])])

        def copy(i, src, dst, to):
            return pltpu.make_async_remote_copy(src_ref=src, dst_ref=dst, send_sem=send_sems.at[i],
                                                recv_sem=recv_sems.at[i], device_id=to, device_id_type=MESH)

        own = []
        for i, (l0, p0, n) in enumerate(_piece_rows(k_me)):
            own.append(pltpu.make_async_copy(w_ref.at[pl.ds(l0, n)], ow_ref.at[pl.ds(p0, n)], local_sems.at[i]))
        own.append(pltpu.make_async_copy(s_ref, os_ref.at[k_me], local_sems.at[2]))
        for cp in own:
            cp.start()
        first = []
        for j, (cx, cy) in enumerate(chips):
            for i, (src, dst) in enumerate(windows(k_me, c)):
                first.append(copy(3 * j + i, src, dst, (cx, cy, c)))
        for cp in first:
            cp.start()
        passed = []
        for j, (cx, cy) in enumerate(chips):
            for i, (_, dst) in enumerate(windows(2 * cx + cy, c)):
                copy(3 * j + i, dst, dst, (cx, cy, c)).wait_recv()
                cp = copy(9 + 3 * j + i, dst, dst, sib)
                cp.start()
                passed.append(cp)
        for j, (cx, cy) in enumerate(chips):
            for i, (_, dst) in enumerate(windows(2 * cx + cy, 1 - c)):
                copy(9 + 3 * j + i, dst, dst, sib).wait_recv()
        for cp in first + passed:
            cp.wait_send()
        for cp in own:
            cp.wait()

    return pl.pallas_call(
        body,
        out_shape=[jax.ShapeDtypeStruct((NINP, CWD), F32), jax.ShapeDtypeStruct((4, PACK_ROWS, PACK_W), BF16)],
        in_specs=[ANY, ANY], out_specs=[ANY, ANY],
        scratch_shapes=[pltpu.SemaphoreType.DMA((18,)), pltpu.SemaphoreType.DMA((18,)), pltpu.SemaphoreType.DMA((3,))],
        name="weights_all_gather",
    )(wc, sp)


UNPACK_BR = 512


def _unpack_w_in(cont):
    def body(c_ref, o_ref):
        lo, hi = _unpack_words(c_ref[...])
        r = pl.program_id(0) * UNPACK_BR + lax.broadcasted_iota(jnp.int32, (UNPACK_BR, CWD), 0)
        pad = jnp.logical_and(r >= KPE_END, r < KPE_END + NINP - NIN)
        o_ref[:, 0:CWD] = jnp.where(pad, 0.0, lo).astype(BF16)
        o_ref[:, CWD:2 * CWD] = jnp.where(pad, 0.0, hi).astype(BF16)

    return pl.pallas_call(
        body, grid=(NINP // UNPACK_BR,),
        in_specs=[pl.BlockSpec((UNPACK_BR, CWD), lambda i: (i, 0))],
        out_specs=pl.BlockSpec((UNPACK_BR, D), lambda i: (i, 0)),
        out_shape=jax.ShapeDtypeStruct((NINP, D), BF16),
        name="w_in_unpack",
        compiler_params=pltpu.CompilerParams(dimension_semantics=("arbitrary",), vmem_limit_bytes=VMEM_LIMIT),
    )(cont)


def _rs_swap(gw, gs):
    def body(w_ref, s_ref, rw_ref, rs_ref, send_sems, recv_sems):
        x, y, c = _me()
        oc = 1 - c
        cps = [pltpu.make_async_remote_copy(src_ref=w_ref.at[:, pl.ds(pl.multiple_of(oc * (D // 2), LANE), D // 2)],
                                            dst_ref=rw_ref, send_sem=send_sems.at[0], recv_sem=recv_sems.at[0],
                                            device_id=(x, y, oc), device_id_type=MESH),
               pltpu.make_async_remote_copy(src_ref=s_ref.at[:, :, pl.ds(pl.multiple_of(oc * HW, LANE), HW)],
                                            dst_ref=rs_ref, send_sem=send_sems.at[1], recv_sem=recv_sems.at[1],
                                            device_id=(x, y, oc), device_id_type=MESH)]
        for cp in cps:
            cp.start()
        for cp in cps:
            cp.wait()

    return pl.pallas_call(
        body,
        out_shape=[jax.ShapeDtypeStruct((NINP, D // 2), F32), jax.ShapeDtypeStruct((4, PACK_ROWS, HW), F32)],
        in_specs=[ANY, ANY], out_specs=[ANY, ANY],
        scratch_shapes=[pltpu.SemaphoreType.DMA((2,)), pltpu.SemaphoreType.DMA((2,))],
        name="grads_sibling_swap",
    )(gw, gs)


SUM_BR = 512


def _rs_chip_sum_w(gw, rw, cidx):
    def body(c_ref, g_ref, r_ref, o_ref):
        s = g_ref[...] + r_ref[...]
        q = D // 8
        o_ref[...] = jnp.concatenate([_pack_words(s[:, 0:q], s[:, q:2 * q]),
                                      _pack_words(s[:, 2 * q:3 * q], s[:, 3 * q:4 * q])], axis=1)

    return pl.pallas_call(
        body,
        grid_spec=pltpu.PrefetchScalarGridSpec(
            num_scalar_prefetch=1, grid=(NINP // SUM_BR,),
            in_specs=[pl.BlockSpec((SUM_BR, D // 2), lambda i, cr: (i, cr[0])),
                      pl.BlockSpec((SUM_BR, D // 2), lambda i, cr: (i, 0))],
            out_specs=pl.BlockSpec((SUM_BR, D // 4), lambda i, cr: (i, 0))),
        out_shape=jax.ShapeDtypeStruct((NINP, D // 4), F32),
        name="grads_chip_sum_w",
        compiler_params=pltpu.CompilerParams(dimension_semantics=("arbitrary",), vmem_limit_bytes=VMEM_LIMIT),
    )(cidx, gw, rw)


def _rs_chip_sum_s(gs, rs, cidx):
    def body(c_ref, g_ref, r_ref, o_ref):
        o_ref[...] = (g_ref[...] + r_ref[...]).astype(BF16)

    return pl.pallas_call(
        body,
        grid_spec=pltpu.PrefetchScalarGridSpec(
            num_scalar_prefetch=1, grid=(4,),
            in_specs=[pl.BlockSpec((None, PACK_ROWS, HW), lambda j, cr: (j, 0, cr[0])),
                      pl.BlockSpec((None, PACK_ROWS, HW), lambda j, cr: (j, 0, 0))],
            out_specs=pl.BlockSpec((None, PACK_ROWS, HW), lambda j, cr: (j, 0, 0))),
        out_shape=jax.ShapeDtypeStruct((4, PACK_ROWS, HW), BF16),
        name="grads_chip_sum_s",
        compiler_params=pltpu.CompilerParams(dimension_semantics=("arbitrary",), vmem_limit_bytes=VMEM_LIMIT),
    )(cidx, gs, rs)


def _rs_exchange(sw, ss, gw, rw):
    def body(sw_ref, ss_ref, gw_ref, rw_ref, r2w_ref, r2s_ref, og_ref, or_ref, send_sems, recv_sems, local_sems):
        x, y, c = _me()
        k_me = 2 * x + y
        chips = [(1 - x, y), (x, 1 - y), (1 - x, 1 - y)]
        own = []
        for i, (l0, p0, n) in enumerate(_piece_rows(k_me)):
            own.append(pltpu.make_async_copy(gw_ref.at[pl.ds(p0, n), pl.ds(pl.multiple_of(c * (D // 2), LANE), D // 2)],
                                             og_ref.at[pl.ds(l0, n)], local_sems.at[2 * i]))
            own.append(pltpu.make_async_copy(rw_ref.at[pl.ds(p0, n)], or_ref.at[pl.ds(l0, n)], local_sems.at[2 * i + 1]))
        for cp in own:
            cp.start()
        cps = []
        for j, (cx, cy) in enumerate(chips):
            k = 2 * cx + cy
            for i, (l0, p0, n) in enumerate(_piece_rows(k)):
                cps.append(pltpu.make_async_remote_copy(
                    src_ref=sw_ref.at[pl.ds(p0, n)], dst_ref=r2w_ref.at[j, pl.ds(l0, n)], send_sem=send_sems.at[3 * j + i],
                    recv_sem=recv_sems.at[3 * j + i], device_id=(cx, cy, c), device_id_type=MESH))
            cps.append(pltpu.make_async_remote_copy(
                src_ref=ss_ref.at[k], dst_ref=r2s_ref.at[j], send_sem=send_sems.at[3 * j + 2],
                recv_sem=recv_sems.at[3 * j + 2], device_id=(cx, cy, c), device_id_type=MESH))
        for cp in cps:
            cp.start()
        for cp in cps:
            cp.wait()
        for cp in own:
            cp.wait()

    return pl.pallas_call(
        body,
        out_shape=[jax.ShapeDtypeStruct((3, WSH, D // 4), F32), jax.ShapeDtypeStruct((3, PACK_ROWS, HW), BF16),
                   jax.ShapeDtypeStruct((WSH, D // 2), F32), jax.ShapeDtypeStruct((WSH, D // 2), F32)],
        in_specs=[ANY] * 4, out_specs=[ANY] * 4,
        scratch_shapes=[pltpu.SemaphoreType.DMA((9,)), pltpu.SemaphoreType.DMA((9,)), pltpu.SemaphoreType.DMA((4,))],
        name="grads_chip_exchange",
    )(sw, ss, gw, rw)


def _rs_final_w(og, orr, r2w):
    q = D // 8

    def body(g_ref, r_ref, p_ref, o_ref):
        acc = g_ref[...] + r_ref[...]
        for j in range(3):
            lo, hi = _unpack_words(p_ref[j])
            acc = acc + jnp.concatenate([lo, hi], axis=1)
        o_ref[...] = acc

    return pl.pallas_call(
        body, grid=(2,),
        in_specs=[pl.BlockSpec((WSH, 2 * q), lambda i: (0, i)), pl.BlockSpec((WSH, 2 * q), lambda i: (0, i)),
                  pl.BlockSpec((3, WSH, q), lambda i: (0, 0, i))],
        out_specs=pl.BlockSpec((WSH, 2 * q), lambda i: (0, i)),
        out_shape=jax.ShapeDtypeStruct((WSH, D // 2), F32),
        name="grads_final_sum_w",
        compiler_params=pltpu.CompilerParams(dimension_semantics=("arbitrary",), vmem_limit_bytes=VMEM_LIMIT),
    )(og, orr, r2w)


def _rs_final_s(gs, rs, r2s, idx):
    def body(i_ref, g_ref, r_ref, p_ref, o_ref):
        acc = g_ref[...] + r_ref[...]
        for j in range(3):
            acc = acc + p_ref[j].astype(F32)
        o_ref[...] = acc

    return pl.pallas_call(
        body,
        grid_spec=pltpu.PrefetchScalarGridSpec(
            num_scalar_prefetch=1, grid=(1,),
            in_specs=[pl.BlockSpec((None, PACK_ROWS, HW), lambda i, ir: (ir[0], 0, ir[1])),
                      pl.BlockSpec((None, PACK_ROWS, HW), lambda i, ir: (ir[0], 0, 0)),
                      pl.BlockSpec((3, PACK_ROWS, HW), lambda i, ir: (0, 0, 0))],
            out_specs=pl.BlockSpec((PACK_ROWS, HW), lambda i, ir: (0, 0))),
        out_shape=jax.ShapeDtypeStruct((PACK_ROWS, HW), F32),
        name="grads_final_sum_s",
        compiler_params=pltpu.CompilerParams(dimension_semantics=("arbitrary",), vmem_limit_bytes=VMEM_LIMIT),
    )(idx, gs, rs, r2s)


def _rs_share(fw, fs):
    def body(w_ref, s_ref, ow_ref, os_ref, send_sems, recv_sems, local_sems):
        x, y, c = _me()
        sib = (x, y, 1 - c)
        wcol = lambda cc: pl.ds(pl.multiple_of(cc * (D // 2), LANE), D // 2)
        scol = lambda cc: pl.ds(pl.multiple_of(cc * HW, LANE), HW)
        own = [pltpu.make_async_copy(w_ref, ow_ref.at[:, wcol(c)], local_sems.at[0]),
               pltpu.make_async_copy(s_ref, os_ref.at[:, scol(c)], local_sems.at[1])]
        for cp in own:
            cp.start()

        def copies(cc):
            return [pltpu.make_async_remote_copy(src_ref=w_ref, dst_ref=ow_ref.at[:, wcol(cc)], send_sem=send_sems.at[0],
                                                 recv_sem=recv_sems.at[0], device_id=sib, device_id_type=MESH),
                    pltpu.make_async_remote_copy(src_ref=s_ref, dst_ref=os_ref.at[:, scol(cc)], send_sem=send_sems.at[1],
                                                 recv_sem=recv_sems.at[1], device_id=sib, device_id_type=MESH)]
        out = copies(c)
        for cp in out:
            cp.start()
        for cp in copies(1 - c):
            cp.wait_recv()
        for cp in out:
            cp.wait_send()
        for cp in own:
            cp.wait()

    return pl.pallas_call(
        body,
        out_shape=[jax.ShapeDtypeStruct((WSH, D), F32), jax.ShapeDtypeStruct((PACK_ROWS, PACK_W), F32)],
        in_specs=[ANY, ANY], out_specs=[ANY, ANY],
        scratch_shapes=[pltpu.SemaphoreType.DMA((2,)), pltpu.SemaphoreType.DMA((2,)), pltpu.SemaphoreType.DMA((2,))],
        name="grads_share",
    )(fw, fs)


def _reduce_scatter(gw, gs):
    x, y, c = _me()
    cidx = jnp.reshape(c, (1,)).astype(jnp.int32)
    rw, rs = _rs_swap(gw, gs)
    sw = _rs_chip_sum_w(gw, rw, cidx)
    ss = _rs_chip_sum_s(gs, rs, cidx)
    r2w, r2s, og, orr = _rs_exchange(sw, ss, gw, rw)
    fw = _rs_final_w(og, orr, r2w)
    fs = _rs_final_s(gs, rs, r2s, jnp.stack([2 * x + y, c]).astype(jnp.int32))
    return _rs_share(fw, fs)


def _all_reduce_small(gs):
    rows = gs.shape[0]

    def body(g_ref, o_ref, buf, send_sems, recv_sems):
        x, y, c = _me()
        me = 4 * x + 2 * y + c
        buf[me] = g_ref[...]
        cps = []
        for r in range(1, 8):
            fx, fy, fc = (r >> 2) & 1, (r >> 1) & 1, r & 1
            px, py, pc = jnp.bitwise_xor(x, fx), jnp.bitwise_xor(y, fy), jnp.bitwise_xor(c, fc)
            cps.append((pltpu.make_async_remote_copy(
                src_ref=g_ref, dst_ref=buf.at[me], send_sem=send_sems.at[r - 1], recv_sem=recv_sems.at[r - 1],
                device_id=(px, py, pc), device_id_type=MESH), 4 * px + 2 * py + pc))
        for cp, _ in cps:
            cp.start()
        for r, (cp, peer) in enumerate(cps):
            pltpu.make_async_remote_copy(
                src_ref=g_ref, dst_ref=buf.at[peer], send_sem=send_sems.at[r], recv_sem=recv_sems.at[r],
                device_id=(x, y, c), device_id_type=MESH).wait_recv()
        for cp, _ in cps:
            cp.wait_send()
        acc = buf[0]
        for k in range(1, 8):
            acc = acc + buf[k]
        o_ref[...] = acc

    return pl.pallas_call(
        body,
        out_shape=jax.ShapeDtypeStruct((rows, LANE), F32),
        in_specs=[pl.BlockSpec(memory_space=pltpu.VMEM)],
        out_specs=pl.BlockSpec(memory_space=pltpu.VMEM),
        scratch_shapes=[pltpu.VMEM((8, rows, LANE), F32), pltpu.SemaphoreType.DMA((7,)), pltpu.SemaphoreType.DMA((7,))],
        name="small_grads_all_reduce",
    )(gs)


PACK_SPLIT = (("w_uq", 96, (QL, 192)), ("w_ukv", 64, (KVL, 256)),
              ("w_out_a", 256, (CW, 256)), ("w_out_b", 256, (CW, 256)), ("w_out_c", 256, (CW, 256)),
              ("w_o", 512, (256, D)))
MAT_ROWS = 1440
CONV_SHARD = 3 * 128


def _w_in_words(w_in_shard):
    t = w_in_shard.T
    return _pack_words(t[:, :CWD], t[:, CWD:])


def _pack_weights(wl):
    parts = [wl[n].astype(BF16).reshape(-1, PACK_W) for n, _, _ in PACK_SPLIT]
    cw = wl["conv_w"].reshape(-1)
    hi = cw.astype(BF16)
    r1 = cw - hi.astype(F32)
    mid = r1.astype(BF16)
    lo = (r1 - mid.astype(F32)).astype(BF16)
    cterms = jnp.pad(jnp.concatenate([hi, mid, lo]), (0, 3 * PACK_W - 3 * CONV_SHARD)).reshape(3, PACK_W)
    pad = jnp.zeros((PACK_ROWS - MAT_ROWS - 3, PACK_W), BF16)
    return jnp.concatenate(parts + [cterms, pad], axis=0)


def _unpack_weights(gath):
    out = {}
    r = 0
    for n, nrows, shp in PACK_SPLIT:
        t = gath[:, r:r + nrows].reshape((4,) + shp)
        r += nrows
        if n == "w_o":
            out[n] = t.reshape(4 * shp[0], shp[1])
        else:
            out[n] = t.transpose(1, 0, 2).reshape(shp[0], 4 * shp[1])
    ct = gath[:, r:r + 3].reshape(4, 3 * PACK_W)[:, :3 * CONV_SHARD].astype(F32).reshape(4, 3, CONV_SHARD)
    cw = (ct[:, 0] + ct[:, 1]) + ct[:, 2]
    out["conv_w"] = cw.reshape(4, 3, 128).transpose(1, 0, 2).reshape(3, CW)
    return out


def _pack_grads(g):
    parts = []
    for n, nrows, shp in PACK_SPLIT:
        t = g[n]
        if n == "w_o":
            t = t.reshape((4,) + shp)
        else:
            t = t.reshape(shp[0], 4, shp[1]).transpose(1, 0, 2)
        parts.append(t.reshape(4, nrows, PACK_W))
    cw = g["conv_w"].reshape(3, 4, 128).transpose(1, 0, 2).reshape(4, 1, CONV_SHARD)
    parts.append(jnp.pad(cw, ((0, 0), (0, 0), (0, PACK_W - CONV_SHARD))))
    parts.append(jnp.zeros((4, PACK_ROWS - MAT_ROWS - 1, PACK_W), F32))
    return jnp.concatenate(parts, axis=1)


def _unpack_grads(red):
    out = {}
    r = 0
    for n, nrows, shp in PACK_SPLIT:
        out[n] = red[r:r + nrows].reshape(shp)
        r += nrows
    out["conv_w"] = red[r, :CONV_SHARD].reshape(3, 128)
    return out


SMALL_SIZES = (("norm_g", D), ("b_gate", 3 * D), ("conv_b", CW), ("q_a_norm_g", QL), ("kv_a_norm_g", KVL),
               ("mla_q_norm_g", QK), ("mla_k_norm_g", QK), ("dil_q_norm_g", NG * HD), ("dil_k_norm_g", NG * HD))
SMALL_ROWS = 88


def _pack_small(per_name):
    flat = jnp.concatenate([per_name[n].reshape(-1).astype(F32) for n, _ in SMALL_SIZES])
    return jnp.pad(flat, (0, SMALL_ROWS * LANE - flat.shape[0])).reshape(SMALL_ROWS, LANE)


def _unpack_small(packed, like):
    out = {}
    flat = packed.reshape(-1)
    r = 0
    for n, sz in SMALL_SIZES:
        out[n] = flat[r:r + NL * sz].reshape(like[n].shape)
        r += NL * sz
    return out


def _adamw_math(w, g, m, v):
    m = ADAM_B1 * m + (1.0 - ADAM_B1) * g
    v = ADAM_B2 * v + (1.0 - ADAM_B2) * jnp.square(g)
    m_hat = m / (1.0 - ADAM_B1 ** ADAM_STEP)
    v_hat = v / (1.0 - ADAM_B2 ** ADAM_STEP)
    delta = -ADAM_LR * (m_hat / (jnp.sqrt(v_hat) + ADAM_EPS) + ADAM_WD * w)
    return delta, m, v


def _adamw(name, w, g, m, v, br):
    L, R, C = w.shape
    blk = lambda l, i: (l, i, 0)
    return _pcall(name, _adamw_math, (L, R // br), [(t, (None, br, C), blk) for t in (w, g, m, v)],
                  [((L, R, C), F32, (None, br, C), blk)] * 3)


ADAM_ROWS = {"w_in": 128, "w_uq": 256, "w_ukv": 128, "w_out_a": 512, "w_out_b": 512, "w_out_c": 512, "w_o": 256,
             "conv_w": 3}


def kernel(x, norm_g, w_in, b_gate, conv_w, conv_b, q_a_norm_g, w_uq, kv_a_norm_g, w_ukv, mla_q_norm_g, mla_k_norm_g, dil_q_norm_g, dil_k_norm_g, w_out_a, w_out_b, w_out_c, w_o, loss_target, m_norm_g, m_w_in, m_b_gate, m_conv_w, m_conv_b, m_q_a_norm_g, m_w_uq, m_kv_a_norm_g, m_w_ukv, m_mla_q_norm_g, m_mla_k_norm_g, m_dil_q_norm_g, m_dil_k_norm_g, m_w_out_a, m_w_out_b, m_w_out_c, m_w_o, v_norm_g, v_w_in, v_b_gate, v_conv_w, v_conv_b, v_q_a_norm_g, v_w_uq, v_kv_a_norm_g, v_w_ukv, v_mla_q_norm_g, v_mla_k_norm_g, v_dil_q_norm_g, v_dil_k_norm_g, v_w_out_a, v_w_out_b, v_w_out_c, v_w_o):
    W = dict(norm_g=norm_g, w_in=w_in, b_gate=b_gate, conv_w=conv_w, conv_b=conv_b, q_a_norm_g=q_a_norm_g, w_uq=w_uq,
             kv_a_norm_g=kv_a_norm_g, w_ukv=w_ukv, mla_q_norm_g=mla_q_norm_g, mla_k_norm_g=mla_k_norm_g,
             dil_q_norm_g=dil_q_norm_g, dil_k_norm_g=dil_k_norm_g, w_out_a=w_out_a, w_out_b=w_out_b, w_out_c=w_out_c,
             w_o=w_o)
    M = dict(norm_g=m_norm_g, w_in=m_w_in, b_gate=m_b_gate, conv_w=m_conv_w, conv_b=m_conv_b, q_a_norm_g=m_q_a_norm_g,
             w_uq=m_w_uq, kv_a_norm_g=m_kv_a_norm_g, w_ukv=m_w_ukv, mla_q_norm_g=m_mla_q_norm_g,
             mla_k_norm_g=m_mla_k_norm_g, dil_q_norm_g=m_dil_q_norm_g, dil_k_norm_g=m_dil_k_norm_g, w_out_a=m_w_out_a,
             w_out_b=m_w_out_b, w_out_c=m_w_out_c, w_o=m_w_o)
    V = dict(norm_g=v_norm_g, w_in=v_w_in, b_gate=v_b_gate, conv_w=v_conv_w, conv_b=v_conv_b, q_a_norm_g=v_q_a_norm_g,
             w_uq=v_w_uq, kv_a_norm_g=v_kv_a_norm_g, w_ukv=v_w_ukv, mla_q_norm_g=v_mla_q_norm_g,
             mla_k_norm_g=v_mla_k_norm_g, dil_q_norm_g=v_dil_q_norm_g, dil_k_norm_g=v_dil_k_norm_g, w_out_a=v_w_out_a,
             w_out_b=v_w_out_b, w_out_c=v_w_out_c, w_o=v_w_o)
    batch = x.shape[0]
    T = batch * S

    ws = []
    for l in range(NL):
        cont, gath = _all_gather(_w_in_words(w_in[l]), _pack_weights({n: W[n][l] for n in BIG[1:] + ("conv_w",)}))
        full = _unpack_weights(gath)
        pad_qk = lambda t: jnp.pad(t, (0, QKP - QK)).reshape(1, QKP)
        full.update(
            w_in_t=_unpack_w_in(cont),
            norm_g=norm_g[l].reshape(1, D), b_gate=b_gate[l].reshape(1, 3 * D), conv_b=conv_b[l].reshape(1, CW),
            q_a_norm_g=q_a_norm_g[l].reshape(1, QL), kv_a_norm_g=kv_a_norm_g[l].reshape(1, KVL),
            mla_q_norm_g=pad_qk(mla_q_norm_g[l]), mla_k_norm_g=pad_qk(mla_k_norm_g[l]),
            dil_q_norm_g=dil_q_norm_g[l].reshape(NG, 1, HD), dil_k_norm_g=dil_k_norm_g[l].reshape(NG, 1, HD))
        ws.append(full)

    loss, dx, grads = _local_step(x.reshape(T, D), loss_target.reshape(T, D), ws, batch)
    loss = lax.psum(loss[0, 0], ("x", "y", "c"))
    grad_x = dx.reshape(batch, S, D)

    red = []
    for l in range(NL):
        rw, rs = _reduce_scatter(grads[l]["w_in_t"], _pack_grads(grads[l]))
        r = _unpack_grads(rs)
        r["w_in"] = rw.T
        red.append(r)
    G = {n: jnp.stack([red[l][n] for l in range(NL)]) for n in BIG + ("conv_w",)}
    small_g = {n: jnp.stack([grads[l][n].reshape(-1)[:sz] for l in range(NL)]) for n, sz in SMALL_SIZES}
    small_red = _all_reduce_small(_pack_small(small_g))
    G.update(_unpack_small(small_red, {n: W[n] for n in SMALL}))

    delta, new_m, new_v = {}, {}, {}
    for n in BIG + ("conv_w",):
        delta[n], new_m[n], new_v[n] = _adamw("adamw_" + n, W[n], G[n], M[n], V[n], ADAM_ROWS[n])
    sw, sm, sv = (_pack_small({n: t[n] for n in SMALL})[None] for t in (W, M, V))
    sd, snm, snv = _adamw("adamw_small", sw, small_red[None], sm, sv, SMALL_ROWS)
    like = {n: W[n] for n in SMALL}
    delta.update(_unpack_small(sd[0], like))
    new_m.update(_unpack_small(snm[0], like))
    new_v.update(_unpack_small(snv[0], like))

    return (loss, grad_x, *[G[n] for n in WEIGHTS], *[delta[n] for n in WEIGHTS],
            *[new_m[n] for n in WEIGHTS], *[new_v[n] for n in WEIGHTS])
```

```python
import functools

import numpy as np
import jax
import jax.numpy as jnp
from jax import lax
from jax.experimental import pallas as pl
from jax.experimental.pallas import tpu as pltpu

F32 = jnp.float32
BF16 = jnp.bfloat16

D = 1024
S = 2048
NL = 2
CW = 512
NH = 8
QL = 256
KVL = 128
NOPE = 64
ROPE = 32
VD = 64
QK = NOPE + ROPE
QKP = 128
ROPE_THETA = 10000.0
DIL = ((128, 1), (512, 4), (2048, 16))
NG = 3
DH = 8
HD = 64
DWID = DH * HD
QB = 128
EPS = 1e-6
NIN = 11168
NINP = 11264
O_A, O_CQ, O_CKV, O_KPE, O_BZ, O_DQ, O_DK, O_DV, O_CZ, O_G = 0, 2048, 2304, 2432, 2560, 3072, 4608, 6144, 7680, 8192
KPE_END = 2464
NEG = -1e30
MLA_SCALE = QK ** -0.5
DIL_SCALE = HD ** -0.5
LANE = 128
PACK_W = 512
VMEM_LIMIT = 48 * 1024 * 1024

ADAM_LR = 0.001
ADAM_B1 = 0.9
ADAM_B2 = 0.999
ADAM_EPS = 1e-08
ADAM_WD = 0.01
ADAM_STEP = 10

MESH = pl.DeviceIdType.MESH
BIG = ("w_in", "w_uq", "w_ukv", "w_out_a", "w_out_b", "w_out_c", "w_o")
SMALL = ("norm_g", "b_gate", "conv_b", "q_a_norm_g", "kv_a_norm_g", "mla_q_norm_g", "mla_k_norm_g",
         "dil_q_norm_g", "dil_k_norm_g")
WEIGHTS = ("norm_g", "w_in", "b_gate", "conv_w", "conv_b", "q_a_norm_g", "w_uq", "kv_a_norm_g", "w_ukv",
           "mla_q_norm_g", "mla_k_norm_g", "dil_q_norm_g", "dil_k_norm_g", "w_out_a", "w_out_b", "w_out_c", "w_o")


def _dot(a, b):
    return jnp.dot(a, b, preferred_element_type=F32)


def _dot_nt(a, b):
    return lax.dot_general(a, b, (((1,), (1,)), ((), ())), preferred_element_type=F32)


def _dot_tn(a, b):
    return lax.dot_general(a, b, (((0,), (0,)), ((), ())), preferred_element_type=F32)


def _pcall(name, fn, grid, ins, outs):
    n_in = len(ins)
    n_out = len(outs)
    acc_axis = len(grid) - 1
    is_acc = [len(o) > 4 and o[4] for o in outs]
    outs = [o[:4] for o in outs]

    def body(*refs):
        vals = fn(*[r[...] for r in refs[:n_in]])
        if not isinstance(vals, (tuple, list)):
            vals = (vals,)
        for k in range(n_out):
            r = refs[n_in + k]
            v = vals[k].astype(r.dtype).reshape(r.shape)
            if is_acc[k]:
                first = pl.program_id(acc_axis) == 0

                @pl.when(first)
                def _():
                    r[...] = v

                @pl.when(jnp.logical_not(first))
                def _():
                    r[...] += v
            else:
                r[...] = v

    return pl.pallas_call(
        body,
        grid=grid,
        in_specs=[pl.BlockSpec(bs, im) for _, bs, im in ins],
        out_specs=[pl.BlockSpec(bs, im) for _, _, bs, im in outs],
        out_shape=[jax.ShapeDtypeStruct(sh, dt) for sh, dt, _, _ in outs],
        name=name,
        compiler_params=pltpu.CompilerParams(
            dimension_semantics=("arbitrary",) * len(grid), vmem_limit_bytes=VMEM_LIMIT),
    )(*[a for a, _, _ in ins])


def _mm(name, a, b, *, ta=False, tb=False, out_dtype=F32, add=None, tm=512, tn=1024, tk=1024):
    if ta:
        K, M = a.shape
    else:
        M, K = a.shape
    if tb:
        N, K2 = b.shape
    else:
        K2, N = b.shape
    assert K == K2, (name, a.shape, b.shape)
    tm, tn, tk = min(tm, M), min(tn, N), min(tk, K)
    assert M % tm == 0 and N % tn == 0 and K % tk == 0, (name, M, N, K)
    nk = K // tk
    dims = (((0 if ta else 1,), (1 if tb else 0,)), ((), ()))
    a_spec = pl.BlockSpec((tk, tm), lambda j, i, k: (k, i)) if ta else pl.BlockSpec((tm, tk), lambda j, i, k: (i, k))
    b_spec = pl.BlockSpec((tn, tk), lambda j, i, k: (j, k)) if tb else pl.BlockSpec((tk, tn), lambda j, i, k: (k, j))
    o_spec = pl.BlockSpec((tm, tn), lambda j, i, k: (i, j))
    has_add = add is not None

    def body(*refs):
        a_ref, b_ref = refs[0], refs[1]
        add_ref = refs[2] if has_add else None
        o_ref = refs[3] if has_add else refs[2]
        p = lax.dot_general(a_ref[...].astype(BF16), b_ref[...].astype(BF16), dims, preferred_element_type=F32)
        if nk == 1:
            if has_add:
                p = p + add_ref[...]
            o_ref[...] = p.astype(out_dtype)
        else:
            acc = refs[-1]
            k = pl.program_id(2)

            @pl.when(k == 0)
            def _():
                acc[...] = p

            @pl.when(k > 0)
            def _():
                acc[...] += p

            @pl.when(k == nk - 1)
            def _():
                r = acc[...]
                if has_add:
                    r = r + add_ref[...]
                o_ref[...] = r.astype(out_dtype)

    in_specs = [a_spec, b_spec] + ([o_spec] if has_add else [])
    args = [a, b] + ([add] if has_add else [])
    return pl.pallas_call(
        body,
        grid=(N // tn, M // tm, nk),
        in_specs=in_specs,
        out_specs=o_spec,
        out_shape=jax.ShapeDtypeStruct((M, N), out_dtype),
        scratch_shapes=[pltpu.VMEM((tm, tn), F32)] if nk > 1 else [],
        name=name,
        compiler_params=pltpu.CompilerParams(
            dimension_semantics=("arbitrary", "arbitrary", "arbitrary"), vmem_limit_bytes=VMEM_LIMIT),
    )(*args)


def _vjp_of(f, n_diff):
    def g(*args, n_prim):
        prim = args[:n_diff]
        consts = args[n_diff:n_prim]
        cts = args[n_prim:]
        _, pull = jax.vjp(lambda *p: f(*p, *consts), *prim)
        out = jax.eval_shape(lambda *p: f(*p, *consts), *prim)
        if isinstance(out, (tuple, list)):
            cts = tuple(c.astype(o.dtype) for c, o in zip(cts, out))
        else:
            cts = cts[0].astype(out.dtype)
        return pull(cts)
    return g


def _rms(x, g, n=None):
    n = x.shape[-1] if n is None else n
    ms = jnp.sum(x * x, axis=-1, keepdims=True) / n
    return x * lax.rsqrt(ms + EPS) * g


def _silu(z):
    return z * jax.nn.sigmoid(z)


def _roll_rows(u, k):
    n = u.shape[0]
    r = pltpu.roll(u, k % n, 0)
    t = lax.broadcasted_iota(jnp.int32, u.shape, 0)
    if k > 0:
        return jnp.where(t >= k, r, 0.0)
    return jnp.where(t < n + k, r, 0.0)


@functools.partial(jax.custom_vjp, nondiff_argnums=(1,))
def _shift(u, k):
    return _roll_rows(u, k)


def _shift_fwd(u, k):
    return _roll_rows(u, k), None


def _shift_bwd(k, _, g):
    return (_roll_rows(g, -k),)


_shift.defvjp(_shift_fwd, _shift_bwd)


@functools.partial(jax.custom_vjp, nondiff_argnums=(1,))
def _lane_roll(u, k):
    return pltpu.roll(u, k % LANE, 1)


def _lane_roll_fwd(u, k):
    return pltpu.roll(u, k % LANE, 1), None


def _lane_roll_bwd(k, _, g):
    return (pltpu.roll(g, (-k) % LANE, 1),)


_lane_roll.defvjp(_lane_roll_fwd, _lane_roll_bwd)


def _conv_math(ab, ac, ax, az, cw, cb):
    u = ac * ax
    conv = cb + _shift(u, 2) * cw[0:1] + _shift(u, 1) * cw[1:2] + u * cw[2:3]
    return ab * conv * _silu(az)


def _mla_pre_math(cq, ckv, gq, gkv):
    return _rms(cq, gq), _rms(ckv, gkv)


def _rope_math(q, kn, kpe, gq, gk, c, s1, s2):
    lane = lax.broadcasted_iota(jnp.int32, kpe.shape, 1)
    pe = _lane_roll(jnp.where(lane < ROPE, kpe, 0.0), NOPE)

    def one(t, g):
        tn = _rms(t, g, QK)
        return tn * c + _lane_roll(tn, -16) * s1 + _lane_roll(tn, 16) * s2

    qs, ks = [], []
    for h in range(NH):
        sl = slice(h * QKP, (h + 1) * QKP)
        qs.append(one(q[:, sl], gq))
        ks.append(one(kn[:, sl] + pe, gk))
    return jnp.concatenate(qs, axis=1), jnp.concatenate(ks, axis=1)


def _gate_math(o, z):
    return o * _silu(z)


def _mergec_math(o0, o1, o2, l0, l1, l2, cz):
    m = lax.stop_gradient(jnp.maximum(jnp.maximum(l0, l1), l2))
    e0, e1, e2 = jnp.exp(l0 - m), jnp.exp(l1 - m), jnp.exp(l2 - m)
    den = e0 + e1 + e2
    oc = (e0 / den) * o0 + (e1 / den) * o1 + (e2 / den) * o2
    return oc * _silu(cz)


def _merge_math(g0, g1, g2, b0, b1, b2, pa, pb, pc):
    return (jax.nn.sigmoid(g0 + b0) * pa + jax.nn.sigmoid(g1 + b1) * pb) + jax.nn.sigmoid(g2 + b2) * pc


MLA_T = 256


def _mla_fwd(q, k, v):
    B = q.shape[0]
    T = MLA_T

    def body(q_ref, k_ref, v_ref, o_ref, l_ref):
        qi = pl.program_id(2)
        qb = q_ref[...]
        row = lax.broadcasted_iota(jnp.int32, (T, T), 0)
        col = lax.broadcasted_iota(jnp.int32, (T, T), 1)
        lo = _lo_mask((T, LANE))

        def step(j, carry, diagonal):
            m, l, acc = carry
            off = pl.multiple_of(j * T, T)
            kb = k_ref[pl.ds(off, T), :]
            vb = v_ref[pl.ds(off, T), :]
            ss = []
            for e in (0, 1):
                se = _dot_nt(qb[:, e * QKP:(e + 1) * QKP], kb[:, e * QKP:(e + 1) * QKP]) * MLA_SCALE
                ss.append(jnp.where(col <= row, se, NEG) if diagonal else se)
            s = jnp.concatenate(ss, axis=0)
            m_new = jnp.maximum(m, jnp.max(s, axis=-1, keepdims=True))
            a = jnp.exp(m - m_new)
            p = jnp.exp(s - m_new)
            l = a * l + jnp.sum(p, axis=-1, keepdims=True)
            acc = a * acc + _dot(p.astype(BF16), vb)
            return m_new, l, acc

        init = (jnp.full((2 * T, 1), NEG, F32), jnp.zeros((2 * T, 1), F32), jnp.zeros((2 * T, LANE), F32))
        carry = lax.fori_loop(0, qi, functools.partial(step, diagonal=False), init)
        m, l, acc = step(qi, carry, True)
        o = acc / l
        lse = m + jnp.log(l)
        o_ref[...] = jnp.where(lo, o[:T], o[T:])
        l_ref[...] = jnp.where(lo, lse[:T], lse[T:])

    return pl.pallas_call(
        body,
        grid=(B, NH // 2, S // T),
        in_specs=[pl.BlockSpec((None, T, 2 * QKP), lambda b, hp, i: (b, i, hp)),
                  pl.BlockSpec((None, S, 2 * QKP), lambda b, hp, i: (b, 0, hp)),
                  pl.BlockSpec((None, S, LANE), lambda b, hp, i: (b, 0, hp))],
        out_specs=[pl.BlockSpec((None, T, LANE), lambda b, hp, i: (b, i, hp))] * 2,
        out_shape=[jax.ShapeDtypeStruct((B, S, NH * VD), F32)] * 2,
        name="mla_attn_fwd",
        compiler_params=pltpu.CompilerParams(dimension_semantics=("arbitrary",) * 3, vmem_limit_bytes=VMEM_LIMIT),
    )(q, k, v)


def _mla_bwd(q, k, v, do, o, lse):
    B = q.shape[0]
    T = MLA_T
    NB = S // T

    def body(q_ref, k_ref, v_ref, do_ref, o_ref, l_ref, dq_ref, dk_ref, dv_ref, delta_ref):
        dq_ref[...] = jnp.zeros((S, 2 * QKP), F32)
        delta_ref[...] = _head_sum(do_ref[...] * o_ref[...])
        row = lax.broadcasted_iota(jnp.int32, (T, T), 0)
        col = lax.broadcasted_iota(jnp.int32, (T, T), 1)
        lo = _lo_mask((T, LANE))

        def kv_step(j, _):
            koff = pl.multiple_of(j * T, T)
            kb = k_ref[pl.ds(koff, T), :]
            vb = v_ref[pl.ds(koff, T), :]

            def q_step(i, carry, diagonal):
                dk0, dk1, dv = carry
                qoff = pl.multiple_of(i * T, T)
                qb = q_ref[pl.ds(qoff, T), :]
                do2 = _stack_heads(do_ref[pl.ds(qoff, T), :], lo).astype(BF16)
                lb = l_ref[pl.ds(qoff, T), :]
                db = delta_ref[pl.ds(qoff, T), :]
                dp2 = _dot_nt(do2, vb)
                dks = []
                for e in (0, 1):
                    qe = qb[:, e * QKP:(e + 1) * QKP]
                    ke = kb[:, e * QKP:(e + 1) * QKP]
                    s = _dot_nt(qe, ke) * MLA_SCALE
                    if diagonal:
                        s = jnp.where(col <= row, s, NEG)
                    p = jnp.exp(s - lb[:, e * HD:e * HD + 1])
                    dv = dv + _dot_tn(p.astype(BF16), do2[e * T:(e + 1) * T])
                    ds = (p * (dp2[e * T:(e + 1) * T] - db[:, e * HD:e * HD + 1]) * MLA_SCALE).astype(BF16)
                    dks.append(_dot_tn(ds, qe))
                    dq_ref[pl.ds(qoff, T), e * QKP:(e + 1) * QKP] += _dot(ds, ke)
                return dk0 + dks[0], dk1 + dks[1], dv

            z = jnp.zeros((T, QKP), F32)
            carry = q_step(j, (z, z, jnp.zeros((T, LANE), F32)), True)
            dk0, dk1, dv = lax.fori_loop(j + 1, NB, functools.partial(q_step, diagonal=False), carry)
            dk_ref[pl.ds(koff, T), 0:QKP] = dk0
            dk_ref[pl.ds(koff, T), QKP:2 * QKP] = dk1
            dv_ref[pl.ds(koff, T), :] = dv
            return 0

        lax.fori_loop(0, NB, kv_step, 0)

    def spec(w):
        return pl.BlockSpec((None, S, w), lambda b, hp: (b, 0, hp))

    return pl.pallas_call(
        body,
        grid=(B, NH // 2),
        in_specs=[spec(2 * QKP), spec(2 * QKP), spec(LANE), spec(LANE), spec(LANE), spec(LANE)],
        out_specs=[spec(2 * QKP), spec(2 * QKP), spec(LANE)],
        out_shape=[jax.ShapeDtypeStruct((B, S, NH * QKP), F32), jax.ShapeDtypeStruct((B, S, NH * QKP), F32),
                   jax.ShapeDtypeStruct((B, S, NH * VD), F32)],
        scratch_shapes=[pltpu.VMEM((S, LANE), F32)],
        name="mla_attn_bwd",
        compiler_params=pltpu.CompilerParams(dimension_semantics=("arbitrary",) * 2, vmem_limit_bytes=VMEM_LIMIT),
    )(q, k, v, do, o, lse)


def _lo_mask(shape):
    return lax.broadcasted_iota(jnp.int32, shape, len(shape) - 1) < HD


def _head_sum(u):
    r = lax.broadcasted_iota(jnp.int32, (LANE, LANE), 0) < HD
    c = lax.broadcasted_iota(jnp.int32, (LANE, LANE), 1) < HD
    ones = jnp.where(r == c, 1.0, 0.0).astype(BF16)
    hi = u.astype(BF16)
    lo = (u - hi.astype(F32)).astype(BF16)
    return _dot(hi, ones) + _dot(lo, ones)


def _rms2(x, g):
    return x * lax.rsqrt(_head_sum(x * x) / HD + EPS) * g


def _dil_bias(t_ref, gi, d):
    qq = lax.broadcasted_iota(jnp.int32, (QB, QB), 0)
    kk = lax.broadcasted_iota(jnp.int32, (QB, QB), 1)
    jc = (qq - kk).astype(F32)
    rows = []
    for e in (0, 1):
        sl = t_ref[2 * gi + e:2 * gi + e + 1, :] * float(d)
        bp = jnp.where(kk >= qq, -sl * (jc + float(QB)), NEG)
        bc = jnp.where(kk <= qq, -sl * jc, NEG)
        rows.append(jnp.concatenate([bp, bc], axis=1))
    return jnp.concatenate(rows, axis=0)


def _dil_rows(cur, d):
    return pl.ds(cur, QB, stride=d) if d > 1 else pl.ds(pl.multiple_of(cur, QB), QB)


def _dil_walk(d, block):
    if d == 1:
        block(0, None)

        def body(i, c):
            block(i * QB, (i - 1) * QB)
            return c
        lax.fori_loop(1, S // QB, body, 0)
    elif d == 16:
        def body(r, c):
            block(r, None)
            return c
        lax.fori_loop(0, d, body, 0)
    else:
        nb = S // d // QB

        def cls(r, c):
            block(r, None)

            def body(i, c2):
                block(r + i * QB * d, r + (i - 1) * QB * d)
                return c2
            lax.fori_loop(1, nb, body, 0)
            return c
        lax.fori_loop(0, d, cls, 0)


def _stack_heads(x, lo):
    return jnp.concatenate([jnp.where(lo, x, 0.0), jnp.where(lo, 0.0, x)], axis=0)


def _dilc_fwd(proj3, gq, gk, tab):
    B = proj3.shape[0]

    def body(q_ref, k_ref, v_ref, cz_ref, gq_ref, gk_ref, t_ref, y_ref, o_ref, l_ref, qs, ks, vs):
        g = pl.program_id(2)
        lo = _lo_mask((QB, LANE))

        def group(gi):
            d = DIL[gi][1]
            qs[...] = _rms2(q_ref[...], gq_ref[gi:gi + 1, :])
            ks[...] = _rms2(k_ref[...], gk_ref[gi:gi + 1, :])
            vs[...] = v_ref[...]
            bias = _dil_bias(t_ref, gi, d)

            def block(cur, prev):
                rows = _dil_rows(cur, d)
                q2 = _stack_heads(qs[rows, :], lo).astype(BF16)
                kc, vc = ks[rows, :], vs[rows, :]
                if prev is None:
                    kcat, vcat, b = kc, vc, bias[:, QB:]
                else:
                    prow = _dil_rows(prev, d)
                    kcat = jnp.concatenate([ks[prow, :], kc], axis=0)
                    vcat = jnp.concatenate([vs[prow, :], vc], axis=0)
                    b = bias
                s = _dot_nt(q2, kcat.astype(BF16)) * DIL_SCALE + b
                m = jnp.max(s, axis=-1, keepdims=True)
                p = jnp.exp(s - m)
                l = jnp.sum(p, axis=-1, keepdims=True)
                o = _dot(p.astype(BF16), vcat.astype(BF16)) / l
                lse = m + jnp.log(l)
                o_ref[gi, rows, :] = jnp.where(lo, o[:QB], o[QB:])
                l_ref[gi, rows, :] = jnp.where(lo, lse[:QB], lse[QB:])

            _dil_walk(d, block)

        for gi in range(NG):
            pl.when(g == gi)(functools.partial(group, gi))

        @pl.when(g == NG - 1)
        def _():
            y_ref[...] = _mergec_math(o_ref[0], o_ref[1], o_ref[2], l_ref[0], l_ref[1], l_ref[2],
                                      cz_ref[...]).astype(BF16)

    def col(base):
        return pl.BlockSpec((None, S, LANE), lambda b, hp, g: (b, 0, base // LANE + 4 * g + hp))

    gspec = pl.BlockSpec((NG, LANE), lambda b, hp, g: (0, 0))
    saved = pl.BlockSpec((NG, None, S, LANE), lambda b, hp, g: (0, b, 0, hp))
    return pl.pallas_call(
        body,
        grid=(B, 4, NG),
        in_specs=[col(O_DQ), col(O_DK), col(O_DV),
                  pl.BlockSpec((None, S, LANE), lambda b, hp, g: (b, 0, O_CZ // LANE + hp)),
                  gspec, gspec, pl.BlockSpec((None, 8, LANE), lambda b, hp, g: (hp, 0, 0))],
        out_specs=[pl.BlockSpec((None, S, LANE), lambda b, hp, g: (b, 0, hp)), saved, saved],
        out_shape=[jax.ShapeDtypeStruct((B, S, DWID), BF16), jax.ShapeDtypeStruct((NG, B, S, DWID), F32),
                   jax.ShapeDtypeStruct((NG, B, S, DWID), F32)],
        scratch_shapes=[pltpu.VMEM((S, LANE), F32)] * 3,
        name="dil_mixer_fwd",
        compiler_params=pltpu.CompilerParams(dimension_semantics=("arbitrary",) * 3, vmem_limit_bytes=VMEM_LIMIT),
    )(proj3, proj3, proj3, proj3, gq, gk, tab)


def _dilc_bwd(proj3, gq, gk, tab, o_all, l_all, d_yc):
    B = proj3.shape[0]

    def body(q_ref, k_ref, v_ref, cz_ref, gq_ref, gk_ref, t_ref, o_ref, l_ref, dy_ref,
             dq_out, dk_out, dv_out, dcz_out, dgq_out, dgk_out, qs, ks, vs, dos, dls, dqs, dks, dvs):
        g = pl.program_id(2)
        lo = _lo_mask((QB, LANE))

        @pl.when(jnp.logical_and(jnp.logical_and(pl.program_id(0) == 0, pl.program_id(1) == 0), g == 0))
        def _():
            dgq_out[...] = jnp.zeros((NG, LANE), F32)
            dgk_out[...] = jnp.zeros((NG, LANE), F32)

        def group(gi):
            d = DIL[gi][1]
            ls = [l_ref[j] for j in range(NG)]
            m = jnp.maximum(jnp.maximum(ls[0], ls[1]), ls[2])
            es = [jnp.exp(t - m) for t in ls]
            den = (es[0] + es[1]) + es[2]
            al = [e / den for e in es]
            os_ = [o_ref[j] for j in range(NG)]
            oc = (al[0] * os_[0] + al[1] * os_[1]) + al[2] * os_[2]
            cz = cz_ref[...]
            sg = jax.nn.sigmoid(cz)
            dy = dy_ref[...]
            d_oc = dy * (cz * sg)
            dcz_out[...] = (dy * oc * (sg * (1.0 + cz * (1.0 - sg)))).astype(BF16)
            ts = [_head_sum(d_oc * os_[j]) for j in range(NG)]
            tbar = (al[0] * ts[0] + al[1] * ts[1]) + al[2] * ts[2]
            dos[...] = al[gi] * d_oc
            dls[...] = al[gi] * (ts[gi] - tbar)

            qs[...] = _rms2(q_ref[...], gq_ref[gi:gi + 1, :])
            ks[...] = _rms2(k_ref[...], gk_ref[gi:gi + 1, :])
            vs[...] = v_ref[...]
            dks[...] = jnp.zeros((S, LANE), F32)
            dvs[...] = jnp.zeros((S, LANE), F32)
            bias = _dil_bias(t_ref, gi, d)

            def block(cur, prev):
                rows = _dil_rows(cur, d)
                q2 = _stack_heads(qs[rows, :], lo).astype(BF16)
                dob = dos[rows, :]
                do2 = _stack_heads(dob, lo).astype(BF16)
                kc, vc = ks[rows, :], vs[rows, :]
                if prev is None:
                    kcat, vcat, b = kc, vc, bias[:, QB:]
                else:
                    prow = _dil_rows(prev, d)
                    kcat = jnp.concatenate([ks[prow, :], kc], axis=0)
                    vcat = jnp.concatenate([vs[prow, :], vc], axis=0)
                    b = bias
                kcat = kcat.astype(BF16)
                vcat = vcat.astype(BF16)
                lse_b = l_ref[gi, rows, :]
                corr_b = dls[rows, :] - _head_sum(dob * o_ref[gi, rows, :])
                lse2 = jnp.concatenate([lse_b[:, 0:1], lse_b[:, HD:HD + 1]], axis=0)
                corr2 = jnp.concatenate([corr_b[:, 0:1], corr_b[:, HD:HD + 1]], axis=0)
                s = _dot_nt(q2, kcat) * DIL_SCALE + b
                p = jnp.exp(s - lse2)
                ds = (p * (_dot_nt(do2, vcat) + corr2) * DIL_SCALE).astype(BF16)
                dq2 = _dot(ds, kcat)
                dqs[rows, :] = jnp.where(lo, dq2[:QB], dq2[QB:])
                dk = _dot_tn(ds, q2)
                dv = _dot_tn(p.astype(BF16), do2)
                if prev is None:
                    dks[rows, :] += dk
                    dvs[rows, :] += dv
                else:
                    dks[prow, :] += dk[:QB]
                    dvs[prow, :] += dv[:QB]
                    dks[rows, :] += dk[QB:]
                    dvs[rows, :] += dv[QB:]

            _dil_walk(d, block)

            _, pull_q = jax.vjp(_rms2, q_ref[...], gq_ref[gi:gi + 1, :])
            dxq, dgq = pull_q(dqs[...])
            dq_out[...] = dxq.astype(BF16)
            dgq_out[gi:gi + 1, :] += dgq
            _, pull_k = jax.vjp(_rms2, k_ref[...], gk_ref[gi:gi + 1, :])
            dxk, dgk = pull_k(dks[...])
            dk_out[...] = dxk.astype(BF16)
            dgk_out[gi:gi + 1, :] += dgk
            dv_out[...] = dvs[...].astype(BF16)

        for gi in range(NG):
            pl.when(g == gi)(functools.partial(group, gi))

    def col(base):
        return pl.BlockSpec((None, S, LANE), lambda b, hp, g: (b, 0, base // LANE + 4 * g + hp))

    gspec = pl.BlockSpec((NG, LANE), lambda b, hp, g: (0, 0))
    saved = pl.BlockSpec((NG, None, S, LANE), lambda b, hp, g: (0, b, 0, hp))
    per_pair = pl.BlockSpec((None, S, LANE), lambda b, hp, g: (b, 0, hp))
    dcol = pl.BlockSpec((None, S, LANE), lambda b, hp, g: (b, 0, 4 * g + hp))
    return pl.pallas_call(
        body,
        grid=(B, 4, NG),
        in_specs=[col(O_DQ), col(O_DK), col(O_DV),
                  pl.BlockSpec((None, S, LANE), lambda b, hp, g: (b, 0, O_CZ // LANE + hp)),
                  gspec, gspec, pl.BlockSpec((None, 8, LANE), lambda b, hp, g: (hp, 0, 0)),
                  saved, saved, per_pair],
        out_specs=[dcol, dcol, dcol, per_pair, gspec, gspec],
        out_shape=[jax.ShapeDtypeStruct((B, S, NG * DWID), BF16)] * 3
        + [jax.ShapeDtypeStruct((B, S, DWID), BF16), jax.ShapeDtypeStruct((NG, LANE), F32),
           jax.ShapeDtypeStruct((NG, LANE), F32)],
        scratch_shapes=[pltpu.VMEM((S, LANE), F32)] * 8,
        name="dil_mixer_bwd",
        compiler_params=pltpu.CompilerParams(dimension_semantics=("arbitrary",) * 3, vmem_limit_bytes=VMEM_LIMIT),
    )(proj3, proj3, proj3, proj3, gq, gk, tab, o_all, l_all, d_yc)


def _dil_slopes():
    slopes = (2.0 ** (-8.0 * np.arange(1, NG * DH + 1, dtype=np.float32) / (NG * DH))).astype(np.float32).reshape(NG, DH)
    tab = np.zeros((4, 8, LANE), np.float32)
    for hp in range(4):
        for gi in range(NG):
            for e in (0, 1):
                tab[hp, 2 * gi + e, :] = slopes[gi, 2 * hp + e]
    return jnp.asarray(tab)


def _rope_tables():
    inv = ROPE_THETA ** (-jnp.arange(0, ROPE, 2, dtype=F32) / ROPE)
    ang = jnp.arange(S, dtype=F32)[:, None] * inv[None, :]
    cos, sin = jnp.cos(ang), jnp.sin(ang)
    z16 = jnp.zeros((S, 16), F32)
    c = jnp.concatenate([jnp.ones((S, NOPE), F32), cos, cos, jnp.zeros((S, 32), F32)], axis=1)
    s1 = jnp.concatenate([jnp.zeros((S, NOPE), F32), -sin, z16, jnp.zeros((S, 32), F32)], axis=1)
    s2 = jnp.concatenate([jnp.zeros((S, NOPE), F32), z16, sin, jnp.zeros((S, 32), F32)], axis=1)
    return c, s1, s2


def _pad_heads_uq(w):
    return jnp.pad(w.reshape(QL, NH, QK), ((0, 0), (0, 0), (0, QKP - QK))).reshape(QL, NH * QKP)


def _unpad_heads_uq(g):
    return g.reshape(QL, NH, QKP)[:, :, :QK].reshape(QL, NH * QK)


def _split_ukv(w):
    w3 = w.reshape(KVL, NH, NOPE + VD)
    uk = jnp.pad(w3[:, :, :NOPE], ((0, 0), (0, 0), (0, QKP - NOPE))).reshape(KVL, NH * QKP)
    return uk, w3[:, :, NOPE:].reshape(KVL, NH * VD)


def _join_ukv(guk, guv):
    return jnp.concatenate([guk.reshape(KVL, NH, QKP)[:, :, :NOPE], guv.reshape(KVL, NH, VD)],
                           axis=-1).reshape(KVL, NH * (NOPE + VD))


BR = 512
BRM = 256


def _layer_fwd(x, w, tabs, batch):
    T = batch * S
    rope_c, rope_s1, rope_s2, dil_tab = tabs
    res = {"x": x}
    row = lambda c: (lambda i: (i, c))
    fix = lambda i: (0, 0)

    h = _pcall("norm_fwd", _rms, (T // BR,),
               [(x, (BR, D), row(0)), (w["norm_g"], (1, D), fix)],
               [((T, D), BF16, (BR, D), row(0))])[0]
    proj = _mm("in_proj", h, w["w_in_t"], tb=True, tm=512, tn=1024)
    res["h"], res["proj"] = h, proj
    proj3 = proj.reshape(batch, S, NINP)

    cblk = lambda s: (lambda j, b: (b, 0, 4 * s + j))
    y_a = _pcall("conv_fwd", _conv_math, (4, batch),
                 [(proj3, (None, S, LANE), cblk(0)), (proj3, (None, S, LANE), cblk(1)),
                  (proj3, (None, S, LANE), cblk(2)), (proj3, (None, S, LANE), cblk(3)),
                  (w["conv_w"], (3, LANE), lambda j, b: (0, j)), (w["conv_b"], (1, LANE), lambda j, b: (0, j))],
                 [((batch, S, CW), BF16, (None, S, LANE), lambda j, b: (b, 0, j))])[0].reshape(T, CW)
    res["y_a"] = y_a

    cqn, ckvn = _pcall("mla_pre_fwd", _mla_pre_math, (T // BR,),
                       [(proj, (BR, QL), row(O_CQ // QL)), (proj, (BR, KVL), row(O_CKV // KVL)),
                        (w["q_a_norm_g"], (1, QL), fix), (w["kv_a_norm_g"], (1, KVL), fix)],
                       [((T, QL), BF16, (BR, QL), row(0)), ((T, KVL), BF16, (BR, KVL), row(0))])
    w_uq_p = _pad_heads_uq(w["w_uq"])
    w_uk, w_uv = _split_ukv(w["w_ukv"])
    q = _mm("uq", cqn, w_uq_p)
    kn = _mm("uk", ckvn, w_uk)
    v = _mm("uv", ckvn, w_uv, out_dtype=BF16)
    nrr = S // BR
    tab_row = lambda i: (i % nrr, 0)
    qr, kr = _pcall("rope_fwd", _rope_math, (T // BR,),
                    [(q, (BR, NH * QKP), row(0)), (kn, (BR, NH * QKP), row(0)), (proj, (BR, LANE), row(O_KPE // LANE)),
                     (w["mla_q_norm_g"], (1, QKP), fix), (w["mla_k_norm_g"], (1, QKP), fix),
                     (rope_c, (BR, QKP), tab_row), (rope_s1, (BR, QKP), tab_row), (rope_s2, (BR, QKP), tab_row)],
                    [((T, NH * QKP), BF16, (BR, NH * QKP), row(0))] * 2)
    qr = qr.reshape(batch, S, NH * QKP)
    kr = kr.reshape(batch, S, NH * QKP)
    v = v.reshape(batch, S, NH * VD)
    o_b, l_b = _mla_fwd(qr, kr, v)
    ob2 = o_b.reshape(T, NH * VD)
    y_b = _pcall("gateb_fwd", _gate_math, (T // BR,),
                 [(ob2, (BR, 512), row(0)), (proj, (BR, 512), row(O_BZ // 512))],
                 [((T, 512), BF16, (BR, 512), row(0))])[0]
    res.update(cqn=cqn, ckvn=ckvn, q=q, kn=kn, qr=qr, kr=kr, v=v, o_b=o_b, l_b=l_b, ob2=ob2, y_b=y_b,
               w_uq_p=w_uq_p, w_uk=w_uk, w_uv=w_uv)

    gq2 = jnp.tile(w["dil_q_norm_g"].reshape(NG, HD), (1, 2))
    gk2 = jnp.tile(w["dil_k_norm_g"].reshape(NG, HD), (1, 2))
    y_c, o_all, l_all = _dilc_fwd(proj3, gq2, gk2, dil_tab)
    y_c = y_c.reshape(T, DWID)
    res.update(o_all=o_all, l_all=l_all, y_c=y_c)

    pa = _mm("out_a", y_a, w["w_out_a"])
    pb = _mm("out_b", y_b, w["w_out_b"])
    pc = _mm("out_c", y_c, w["w_out_c"])
    merged = _pcall("merge_fwd", _merge_math, (T // BRM,),
                    [(proj, (BRM, D), row(O_G // D + s)) for s in range(3)]
                    + [(w["b_gate"], (1, D), (lambda s: (lambda i: (0, s)))(s)) for s in range(3)]
                    + [(t, (BRM, D), row(0)) for t in (pa, pb, pc)],
                    [((T, D), BF16, (BRM, D), row(0))])[0]
    out = _mm("o_proj", merged, w["w_o"], add=x)
    res.update(pa=pa, pb=pb, pc=pc, merged=merged)
    return out, res


def _norm_bwd_math(x, g, dh, dy):
    _, pull = jax.vjp(_rms, x, g)
    dx, dg = pull(dh)
    return dx + dy, dg


def _layer_bwd(dy, w, res, tabs, batch):
    T = batch * S
    rope_c, rope_s1, rope_s2, dil_tab = tabs
    row = lambda c: (lambda i: (i, c))
    fix = lambda i: (0, 0)
    x, proj, h = res["x"], res["proj"], res["h"]
    proj3 = proj.reshape(batch, S, NINP)
    g = {}

    d_merged = _mm("o_proj_dx", dy, w["w_o"], tb=True)
    g["w_o"] = _mm("o_proj_dw", res["merged"], dy, ta=True, tm=1024)

    merge_bwd = functools.partial(_vjp_of(_merge_math, 9), n_prim=9)
    dg0, dg1, dg2, db0, db1, db2, dpa, dpb, dpc = _pcall(
        "merge_bwd", merge_bwd, (T // BRM,),
        [(proj, (BRM, D), row(O_G // D + s)) for s in range(3)]
        + [(w["b_gate"], (1, D), (lambda s: (lambda i: (0, s)))(s)) for s in range(3)]
        + [(t, (BRM, D), row(0)) for t in (res["pa"], res["pb"], res["pc"])]
        + [(d_merged, (BRM, D), row(0))],
        [((T, D), BF16, (BRM, D), row(0))] * 3 + [((1, D), F32, (1, D), fix, True)] * 3
        + [((T, D), BF16, (BRM, D), row(0))] * 3)
    g["b_gate"] = jnp.concatenate([db0, db1, db2], axis=1)

    d_ya = _mm("out_a_dx", dpa, w["w_out_a"], tb=True)
    d_yb = _mm("out_b_dx", dpb, w["w_out_b"], tb=True)
    d_yc = _mm("out_c_dx", dpc, w["w_out_c"], tb=True)
    g["w_out_a"] = _mm("out_a_dw", res["y_a"], dpa, ta=True)
    g["w_out_b"] = _mm("out_b_dw", res["y_b"], dpb, ta=True)
    g["w_out_c"] = _mm("out_c_dw", res["y_c"], dpc, ta=True)

    cblk = lambda s: (lambda j, b: (b, 0, 4 * s + j))
    oblk = lambda j, b: (b, 0, j)
    conv_bwd = functools.partial(_vjp_of(_conv_math, 6), n_prim=6)
    d_ab, d_ac, d_ax, d_az, g["conv_w"], g["conv_b"] = _pcall(
        "conv_bwd", conv_bwd, (4, batch),
        [(proj3, (None, S, LANE), cblk(s)) for s in range(4)]
        + [(w["conv_w"], (3, LANE), lambda j, b: (0, j)), (w["conv_b"], (1, LANE), lambda j, b: (0, j)),
           (d_ya.reshape(batch, S, CW), (None, S, LANE), oblk)],
        [((batch, S, CW), BF16, (None, S, LANE), oblk)] * 4
        + [((3, CW), F32, (3, LANE), lambda j, b: (0, j), True), ((1, CW), F32, (1, LANE), lambda j, b: (0, j), True)])

    gate_bwd = functools.partial(_vjp_of(_gate_math, 2), n_prim=2)
    d_ob, d_bz = _pcall("gateb_bwd", gate_bwd, (T // BR,),
                        [(res["ob2"], (BR, 512), row(0)), (proj, (BR, 512), row(O_BZ // 512)), (d_yb, (BR, 512), row(0))],
                        [((T, 512), F32, (BR, 512), row(0)), ((T, 512), BF16, (BR, 512), row(0))])
    dqr, dkr, dv = _mla_bwd(res["qr"], res["kr"], res["v"], d_ob.reshape(batch, S, NH * VD), res["o_b"], res["l_b"])
    nrr = S // BR
    tab_row = lambda i: (i % nrr, 0)
    rope_bwd = functools.partial(_vjp_of(_rope_math, 5), n_prim=8)
    d_q, d_kn, d_kpe_p, g["mla_q_norm_g"], g["mla_k_norm_g"] = _pcall(
        "rope_bwd", rope_bwd, (T // BR,),
        [(res["q"], (BR, NH * QKP), row(0)), (res["kn"], (BR, NH * QKP), row(0)), (proj, (BR, LANE), row(O_KPE // LANE)),
         (w["mla_q_norm_g"], (1, QKP), fix), (w["mla_k_norm_g"], (1, QKP), fix),
         (rope_c, (BR, QKP), tab_row), (rope_s1, (BR, QKP), tab_row), (rope_s2, (BR, QKP), tab_row),
         (dqr.reshape(T, NH * QKP), (BR, NH * QKP), row(0)), (dkr.reshape(T, NH * QKP), (BR, NH * QKP), row(0))],
        [((T, NH * QKP), BF16, (BR, NH * QKP), row(0))] * 2 + [((T, LANE), BF16, (BR, LANE), row(0))]
        + [((1, QKP), F32, (1, QKP), fix, True)] * 2)
    dv = dv.reshape(T, NH * VD)
    d_cqn = _mm("uq_dx", d_q, res["w_uq_p"], tb=True)
    d_ckvn = _mm("uk_dx", d_kn, res["w_uk"], tb=True)
    d_ckvn = _mm("uv_dx", dv, res["w_uv"], tb=True, add=d_ckvn)
    g["w_uq"] = _unpad_heads_uq(_mm("uq_dw", res["cqn"], d_q, ta=True))
    g["w_ukv"] = _join_ukv(_mm("uk_dw", res["ckvn"], d_kn, ta=True), _mm("uv_dw", res["ckvn"], dv, ta=True))
    pre_bwd = functools.partial(_vjp_of(_mla_pre_math, 4), n_prim=4)
    d_cq, d_ckv, g["q_a_norm_g"], g["kv_a_norm_g"] = _pcall(
        "mla_pre_bwd", pre_bwd, (T // BR,),
        [(proj, (BR, QL), row(O_CQ // QL)), (proj, (BR, KVL), row(O_CKV // KVL)),
         (w["q_a_norm_g"], (1, QL), fix), (w["kv_a_norm_g"], (1, KVL), fix),
         (d_cqn, (BR, QL), row(0)), (d_ckvn, (BR, KVL), row(0))],
        [((T, QL), BF16, (BR, QL), row(0)), ((T, KVL), BF16, (BR, KVL), row(0)),
         ((1, QL), F32, (1, QL), fix, True), ((1, KVL), F32, (1, KVL), fix, True)])

    gq2 = jnp.tile(w["dil_q_norm_g"].reshape(NG, HD), (1, 2))
    gk2 = jnp.tile(w["dil_k_norm_g"].reshape(NG, HD), (1, 2))
    d_dq, d_dk, d_dv, d_cz, dgq, dgk = _dilc_bwd(proj3, gq2, gk2, dil_tab, res["o_all"], res["l_all"],
                                                 d_yc.reshape(batch, S, DWID))
    g["dil_q_norm_g"] = dgq[:, :HD] + dgq[:, HD:]
    g["dil_k_norm_g"] = dgk[:, :HD] + dgk[:, HD:]
    d_dq, d_dk, d_dv = (t.reshape(T, NG * DWID) for t in (d_dq, d_dk, d_dv))
    d_cz = d_cz.reshape(T, DWID)

    dproj = jnp.concatenate(
        [t.reshape(T, CW) for t in (d_ab, d_ac, d_ax, d_az)]
        + [d_cq, d_ckv, d_kpe_p, d_bz, d_dq, d_dk, d_dv, d_cz, dg0, dg1, dg2], axis=1)
    d_h = _mm("in_proj_dx", dproj, w["w_in_t"], tm=1024)
    g["w_in_t"] = _mm("in_proj_dw", dproj, h, ta=True, tm=1024)
    dx, g["norm_g"] = _pcall("norm_bwd", _norm_bwd_math, (T // BR,),
                             [(x, (BR, D), row(0)), (w["norm_g"], (1, D), fix), (d_h, (BR, D), row(0)),
                              (dy, (BR, D), row(0))],
                             [((T, D), F32, (BR, D), row(0)), ((1, D), F32, (1, D), fix, True)])
    return dx, g


def _loss_math(y, t):
    e = y - t
    return e * (1.0 / D), 0.5 * jnp.sum(jnp.sum(e * e, axis=-1, keepdims=True) / D, axis=0, keepdims=True)


def _local_step(x, target, ws, batch):
    T = batch * S
    tabs = _rope_tables() + (_dil_slopes(),)
    saved = []
    y = x
    for l in range(NL):
        y, res = _layer_fwd(y, ws[l], tabs, batch)
        saved.append(res)
    row = lambda i: (i, 0)
    dy, loss = _pcall("loss", _loss_math, (T // BR,),
                      [(y, (BR, D), row), (target, (BR, D), row)],
                      [((T, D), F32, (BR, D), row), ((1, 1), F32, (1, 1), lambda i: (0, 0), True)])
    grads = [None] * NL
    for l in reversed(range(NL)):
        dy, grads[l] = _layer_bwd(dy, ws[l], saved[l], tabs, batch)
    return loss, dy, grads


ANY = pl.BlockSpec(memory_space=pl.ANY)
U32 = jnp.uint32
WSH = NIN // 4
WA = KPE_END
WB = WSH - WA
CWD = 512
PACK_ROWS = 1472
HW = PACK_W // 2


def _me():
    return lax.axis_index("x"), lax.axis_index("y"), lax.axis_index("c")


def _piece_rows(k):
    a = k * WSH + jnp.where(k > 0, NINP - NIN, 0)
    b = k * WSH + WA + (NINP - NIN)
    return ((0, pl.multiple_of(a, 8), WA), (WA, pl.multiple_of(b, 8), WB))


def _pack_words(lo, hi):
    ul = lax.bitcast_convert_type(lo.astype(BF16).astype(F32), U32)
    uh = lax.bitcast_convert_type(hi.astype(BF16).astype(F32), U32)
    w = jnp.bitwise_or(jnp.bitwise_and(uh, jnp.uint32(0xFFFF0000)), jnp.right_shift(ul, jnp.uint32(16)))
    return lax.bitcast_convert_type(w, F32)


def _unpack_words(w):
    w = lax.bitcast_convert_type(w, U32)
    lo = lax.bitcast_convert_type(jnp.left_shift(w, jnp.uint32(16)), F32)
    hi = lax.bitcast_convert_type(jnp.bitwise_and(w, jnp.uint32(0xFFFF0000)), F32)
    return lo, hi


def _all_gather(wc, sp):
    def body(w_ref, s_ref, ow_ref, os_ref, send_sems, recv_sems):
        x, y, c = _me()
        k_me = 2 * x + y
        sib = (x, y, 1 - c)
        chips = [(1 - x, y), (x, 1 - y), (1 - x, 1 - y)]
        wcols = lambda cc: pl.ds(pl.multiple_of(cc * (CWD // 2), LANE), CWD // 2)
        scols = lambda cc: pl.ds(pl.multiple_of(cc * HW, LANE), HW)

        def windows(k, cc):
            pcs = _piece_rows(k)
            return ([(w_ref.at[pl.ds(l0, n), wcols(cc)], ow_ref.at[pl.ds(p0, n), wcols(cc)]) for l0, p0, n in pcs]
                    + [(s_ref.at[:, scols(cc)], os_ref.at[k, :, scols(cc)])])

        def copy(i, src, dst, to):
            return pltpu.make_async_remote_copy(src_ref=src, dst_ref=dst, send_sem=send_sems.at[i],
                                                recv_sem=recv_sems.at[i], device_id=to, device_id_type=MESH)

        def own_windows():
            return ([(w_ref.at[pl.ds(l0, n)], ow_ref.at[pl.ds(p0, n)]) for l0, p0, n in _piece_rows(k_me)]
                    + [(s_ref, os_ref.at[k_me])])

        first = [copy(18 + i, src, dst, sib) for i, (src, dst) in enumerate(own_windows())]
        for j, (cx, cy) in enumerate(chips):
            for i, (src, dst) in enumerate(windows(k_me, c)):
                first.append(copy(3 * j + i, src, dst, (cx, cy, c)))
        for cp in first:
            cp.start()
        passed = []
        for j, (cx, cy) in enumerate(chips):
            for i, (_, dst) in enumerate(windows(2 * cx + cy, c)):
                copy(3 * j + i, dst, dst, (cx, cy, c)).wait_recv()
                cp = copy(9 + 3 * j + i, dst, dst, sib)
                cp.start()
                passed.append(cp)
        for j, (cx, cy) in enumerate(chips):
            for i, (_, dst) in enumerate(windows(2 * cx + cy, 1 - c)):
                copy(9 + 3 * j + i, dst, dst, sib).wait_recv()
        for i, (_, dst) in enumerate(own_windows()):
            copy(18 + i, dst, dst, sib).wait_recv()
        for cp in first + passed:
            cp.wait_send()

    return pl.pallas_call(
        body,
        out_shape=[jax.ShapeDtypeStruct((NINP, CWD), F32), jax.ShapeDtypeStruct((4, PACK_ROWS, PACK_W), BF16)],
        in_specs=[ANY, ANY], out_specs=[ANY, ANY],
        scratch_shapes=[pltpu.SemaphoreType.DMA((21,)), pltpu.SemaphoreType.DMA((21,))],
        name="weights_all_gather",
    )(wc, sp)


UNPACK_BR = 512


def _unpack_w_in(cont):
    def body(c_ref, o_ref):
        lo, hi = _unpack_words(c_ref[...])
        r = pl.program_id(0) * UNPACK_BR + lax.broadcasted_iota(jnp.int32, (UNPACK_BR, CWD), 0)
        pad = jnp.logical_and(r >= KPE_END, r < KPE_END + NINP - NIN)
        o_ref[:, 0:CWD] = jnp.where(pad, 0.0, lo).astype(BF16)
        o_ref[:, CWD:2 * CWD] = jnp.where(pad, 0.0, hi).astype(BF16)

    return pl.pallas_call(
        body, grid=(NINP // UNPACK_BR,),
        in_specs=[pl.BlockSpec((UNPACK_BR, CWD), lambda i: (i, 0))],
        out_specs=pl.BlockSpec((UNPACK_BR, D), lambda i: (i, 0)),
        out_shape=jax.ShapeDtypeStruct((NINP, D), BF16),
        name="w_in_unpack",
        compiler_params=pltpu.CompilerParams(dimension_semantics=("arbitrary",), vmem_limit_bytes=VMEM_LIMIT),
    )(cont)


def _rs_swap(gw, gs):
    def body(w_ref, s_ref, rw_ref, rs_ref, send_sems, recv_sems):
        x, y, c = _me()
        oc = 1 - c
        cps = [pltpu.make_async_remote_copy(src_ref=w_ref.at[:, pl.ds(pl.multiple_of(oc * (D // 2), LANE), D // 2)],
                                            dst_ref=rw_ref, send_sem=send_sems.at[0], recv_sem=recv_sems.at[0],
                                            device_id=(x, y, oc), device_id_type=MESH),
               pltpu.make_async_remote_copy(src_ref=s_ref.at[:, :, pl.ds(pl.multiple_of(oc * HW, LANE), HW)],
                                            dst_ref=rs_ref, send_sem=send_sems.at[1], recv_sem=recv_sems.at[1],
                                            device_id=(x, y, oc), device_id_type=MESH)]
        for cp in cps:
            cp.start()
        for cp in cps:
            cp.wait()

    return pl.pallas_call(
        body,
        out_shape=[jax.ShapeDtypeStruct((NINP, D // 2), F32), jax.ShapeDtypeStruct((4, PACK_ROWS, HW), F32)],
        in_specs=[ANY, ANY], out_specs=[ANY, ANY],
        scratch_shapes=[pltpu.SemaphoreType.DMA((2,)), pltpu.SemaphoreType.DMA((2,))],
        name="grads_sibling_swap",
    )(gw, gs)


SUM_BR = 512


def _rs_chip_sum_w(gw, rw, cidx):
    def body(c_ref, g_ref, r_ref, o_ref):
        s = g_ref[...] + r_ref[...]
        q = D // 8
        o_ref[...] = jnp.concatenate([_pack_words(s[:, 0:q], s[:, q:2 * q]),
                                      _pack_words(s[:, 2 * q:3 * q], s[:, 3 * q:4 * q])], axis=1)

    return pl.pallas_call(
        body,
        grid_spec=pltpu.PrefetchScalarGridSpec(
            num_scalar_prefetch=1, grid=(NINP // SUM_BR,),
            in_specs=[pl.BlockSpec((SUM_BR, D // 2), lambda i, cr: (i, cr[0])),
                      pl.BlockSpec((SUM_BR, D // 2), lambda i, cr: (i, 0))],
            out_specs=pl.BlockSpec((SUM_BR, D // 4), lambda i, cr: (i, 0))),
        out_shape=jax.ShapeDtypeStruct((NINP, D // 4), F32),
        name="grads_chip_sum_w",
        compiler_params=pltpu.CompilerParams(dimension_semantics=("arbitrary",), vmem_limit_bytes=VMEM_LIMIT),
    )(cidx, gw, rw)


def _rs_chip_sum_s(gs, rs, cidx):
    def body(c_ref, g_ref, r_ref, o_ref):
        o_ref[...] = (g_ref[...] + r_ref[...]).astype(BF16)

    return pl.pallas_call(
        body,
        grid_spec=pltpu.PrefetchScalarGridSpec(
            num_scalar_prefetch=1, grid=(4,),
            in_specs=[pl.BlockSpec((None, PACK_ROWS, HW), lambda j, cr: (j, 0, cr[0])),
                      pl.BlockSpec((None, PACK_ROWS, HW), lambda j, cr: (j, 0, 0))],
            out_specs=pl.BlockSpec((None, PACK_ROWS, HW), lambda j, cr: (j, 0, 0))),
        out_shape=jax.ShapeDtypeStruct((4, PACK_ROWS, HW), BF16),
        name="grads_chip_sum_s",
        compiler_params=pltpu.CompilerParams(dimension_semantics=("arbitrary",), vmem_limit_bytes=VMEM_LIMIT),
    )(cidx, gs, rs)


def _rs_exchange(sw, ss):
    def body(sw_ref, ss_ref, r2w_ref, r2s_ref, send_sems, recv_sems):
        x, y, c = _me()
        chips = [(1 - x, y), (x, 1 - y), (1 - x, 1 - y)]
        cps = []
        for j, (cx, cy) in enumerate(chips):
            k = 2 * cx + cy
            for i, (l0, p0, n) in enumerate(_piece_rows(k)):
                cps.append(pltpu.make_async_remote_copy(
                    src_ref=sw_ref.at[pl.ds(p0, n)], dst_ref=r2w_ref.at[j, pl.ds(l0, n)], send_sem=send_sems.at[3 * j + i],
                    recv_sem=recv_sems.at[3 * j + i], device_id=(cx, cy, c), device_id_type=MESH))
            cps.append(pltpu.make_async_remote_copy(
                src_ref=ss_ref.at[k], dst_ref=r2s_ref.at[j], send_sem=send_sems.at[3 * j + 2],
                recv_sem=recv_sems.at[3 * j + 2], device_id=(cx, cy, c), device_id_type=MESH))
        for cp in cps:
            cp.start()
        for cp in cps:
            cp.wait()

    return pl.pallas_call(
        body,
        out_shape=[jax.ShapeDtypeStruct((3, WSH, D // 4), F32), jax.ShapeDtypeStruct((3, PACK_ROWS, HW), BF16)],
        in_specs=[ANY] * 2, out_specs=[ANY] * 2,
        scratch_shapes=[pltpu.SemaphoreType.DMA((9,)), pltpu.SemaphoreType.DMA((9,))],
        name="grads_chip_exchange",
    )(sw, ss)


def _rs_final_w(gw, rw, r2w, idx):
    q = D // 8

    def body(i_ref, g_ref, r_ref, p_ref, o_ref, gbuf, rbuf, sems):
        i = pl.program_id(0)
        k, c = i_ref[0], i_ref[1]
        cps = []
        for n_, (l0, p0, n) in enumerate(_piece_rows(k)):
            gcol = pl.ds(pl.multiple_of(c * (D // 2) + i * 2 * q, LANE), 2 * q)
            rcol = pl.ds(pl.multiple_of(i * 2 * q, LANE), 2 * q)
            cps.append(pltpu.make_async_copy(g_ref.at[pl.ds(p0, n), gcol], gbuf.at[pl.ds(l0, n)], sems.at[2 * n_]))
            cps.append(pltpu.make_async_copy(r_ref.at[pl.ds(p0, n), rcol], rbuf.at[pl.ds(l0, n)], sems.at[2 * n_ + 1]))
        for cp in cps:
            cp.start()
        for cp in cps:
            cp.wait()
        acc = gbuf[...] + rbuf[...]
        for j in range(3):
            lo, hi = _unpack_words(p_ref[j])
            acc = acc + jnp.concatenate([lo, hi], axis=1)
        o_ref[...] = acc

    return pl.pallas_call(
        body,
        grid_spec=pltpu.PrefetchScalarGridSpec(
            num_scalar_prefetch=1, grid=(2,),
            in_specs=[ANY, ANY, pl.BlockSpec((3, WSH, q), lambda i, ir: (0, 0, i))],
            out_specs=pl.BlockSpec((WSH, 2 * q), lambda i, ir: (0, i)),
            scratch_shapes=[pltpu.VMEM((WSH, 2 * q), F32), pltpu.VMEM((WSH, 2 * q), F32), pltpu.SemaphoreType.DMA((4,))]),
        out_shape=jax.ShapeDtypeStruct((WSH, D // 2), F32),
        name="grads_final_sum_w",
        compiler_params=pltpu.CompilerParams(dimension_semantics=("arbitrary",), vmem_limit_bytes=VMEM_LIMIT),
    )(idx, gw, rw, r2w)


def _rs_final_s(gs, rs, r2s, idx):
    def body(i_ref, g_ref, r_ref, p_ref, o_ref):
        acc = g_ref[...] + r_ref[...]
        for j in range(3):
            acc = acc + p_ref[j].astype(F32)
        o_ref[...] = acc

    return pl.pallas_call(
        body,
        grid_spec=pltpu.PrefetchScalarGridSpec(
            num_scalar_prefetch=1, grid=(1,),
            in_specs=[pl.BlockSpec((None, PACK_ROWS, HW), lambda i, ir: (ir[0], 0, ir[1])),
                      pl.BlockSpec((None, PACK_ROWS, HW), lambda i, ir: (ir[0], 0, 0)),
                      pl.BlockSpec((3, PACK_ROWS, HW), lambda i, ir: (0, 0, 0))],
            out_specs=pl.BlockSpec((PACK_ROWS, HW), lambda i, ir: (0, 0))),
        out_shape=jax.ShapeDtypeStruct((PACK_ROWS, HW), F32),
        name="grads_final_sum_s",
        compiler_params=pltpu.CompilerParams(dimension_semantics=("arbitrary",), vmem_limit_bytes=VMEM_LIMIT),
    )(idx, gs, rs, r2s)


def _rs_share(fw, fs):
    def body(w_ref, s_ref, ow_ref, os_ref, send_sems, recv_sems):
        x, y, c = _me()
        cps = [pltpu.make_async_remote_copy(src_ref=w_ref, dst_ref=ow_ref, send_sem=send_sems.at[0],
                                            recv_sem=recv_sems.at[0], device_id=(x, y, 1 - c), device_id_type=MESH),
               pltpu.make_async_remote_copy(src_ref=s_ref, dst_ref=os_ref, send_sem=send_sems.at[1],
                                            recv_sem=recv_sems.at[1], device_id=(x, y, 1 - c), device_id_type=MESH)]
        for cp in cps:
            cp.start()
        for cp in cps:
            cp.wait()

    return pl.pallas_call(
        body,
        out_shape=[jax.ShapeDtypeStruct((WSH, D // 2), F32), jax.ShapeDtypeStruct((PACK_ROWS, HW), F32)],
        in_specs=[ANY, ANY], out_specs=[ANY, ANY],
        scratch_shapes=[pltpu.SemaphoreType.DMA((2,)), pltpu.SemaphoreType.DMA((2,))],
        name="grads_share",
    )(fw, fs)


def _both_halves(mine, other, c):
    return jnp.where(c == 0, jnp.concatenate([mine, other], axis=1), jnp.concatenate([other, mine], axis=1))


def _reduce_scatter(gw, gs):
    x, y, c = _me()
    cidx = jnp.reshape(c, (1,)).astype(jnp.int32)
    idx = jnp.stack([2 * x + y, c]).astype(jnp.int32)
    rw, rs = _rs_swap(gw, gs)
    sw = _rs_chip_sum_w(gw, rw, cidx)
    ss = _rs_chip_sum_s(gs, rs, cidx)
    r2w, r2s = _rs_exchange(sw, ss)
    fw = _rs_final_w(gw, rw, r2w, idx)
    fs = _rs_final_s(gs, rs, r2s, idx)
    ow, os_ = _rs_share(fw, fs)
    return _both_halves(fw, ow, c), _both_halves(fs, os_, c)


def _all_reduce_small(gs):
    rows = gs.shape[0]

    def body(g_ref, o_ref, buf, send_sems, recv_sems):
        x, y, c = _me()
        me = 4 * x + 2 * y + c
        buf[me] = g_ref[...]
        cps = []
        for r in range(1, 8):
            fx, fy, fc = (r >> 2) & 1, (r >> 1) & 1, r & 1
            px, py, pc = jnp.bitwise_xor(x, fx), jnp.bitwise_xor(y, fy), jnp.bitwise_xor(c, fc)
            cps.append((pltpu.make_async_remote_copy(
                src_ref=g_ref, dst_ref=buf.at[me], send_sem=send_sems.at[r - 1], recv_sem=recv_sems.at[r - 1],
                device_id=(px, py, pc), device_id_type=MESH), 4 * px + 2 * py + pc))
        for cp, _ in cps:
            cp.start()
        for r, (cp, peer) in enumerate(cps):
            pltpu.make_async_remote_copy(
                src_ref=g_ref, dst_ref=buf.at[peer], send_sem=send_sems.at[r], recv_sem=recv_sems.at[r],
                device_id=(x, y, c), device_id_type=MESH).wait_recv()
        for cp, _ in cps:
            cp.wait_send()
        acc = buf[0]
        for k in range(1, 8):
            acc = acc + buf[k]
        o_ref[...] = acc

    return pl.pallas_call(
        body,
        out_shape=jax.ShapeDtypeStruct((rows, LANE), F32),
        in_specs=[pl.BlockSpec(memory_space=pltpu.VMEM)],
        out_specs=pl.BlockSpec(memory_space=pltpu.VMEM),
        scratch_shapes=[pltpu.VMEM((8, rows, LANE), F32), pltpu.SemaphoreType.DMA((7,)), pltpu.SemaphoreType.DMA((7,))],
        name="small_grads_all_reduce",
    )(gs)


PACK_SPLIT = (("w_uq", 96, (QL, 192)), ("w_ukv", 64, (KVL, 256)),
              ("w_out_a", 256, (CW, 256)), ("w_out_b", 256, (CW, 256)), ("w_out_c", 256, (CW, 256)),
              ("w_o", 512, (256, D)))
MAT_ROWS = 1440
CONV_SHARD = 3 * 128


def _w_in_words(w_in_shard):
    t = w_in_shard.T
    return _pack_words(t[:, :CWD], t[:, CWD:])


def _pack_weights(wl):
    parts = [wl[n].astype(BF16).reshape(-1, PACK_W) for n, _, _ in PACK_SPLIT]
    cw = wl["conv_w"].reshape(-1)
    hi = cw.astype(BF16)
    r1 = cw - hi.astype(F32)
    mid = r1.astype(BF16)
    lo = (r1 - mid.astype(F32)).astype(BF16)
    cterms = jnp.pad(jnp.concatenate([hi, mid, lo]), (0, 3 * PACK_W - 3 * CONV_SHARD)).reshape(3, PACK_W)
    pad = jnp.zeros((PACK_ROWS - MAT_ROWS - 3, PACK_W), BF16)
    return jnp.concatenate(parts + [cterms, pad], axis=0)


def _unpack_weights(gath):
    out = {}
    r = 0
    for n, nrows, shp in PACK_SPLIT:
        t = gath[:, r:r + nrows].reshape((4,) + shp)
        r += nrows
        if n == "w_o":
            out[n] = t.reshape(4 * shp[0], shp[1])
        else:
            out[n] = t.transpose(1, 0, 2).reshape(shp[0], 4 * shp[1])
    ct = gath[:, r:r + 3].reshape(4, 3 * PACK_W)[:, :3 * CONV_SHARD].astype(F32).reshape(4, 3, CONV_SHARD)
    cw = (ct[:, 0] + ct[:, 1]) + ct[:, 2]
    out["conv_w"] = cw.reshape(4, 3, 128).transpose(1, 0, 2).reshape(3, CW)
    return out


def _pack_grads(g):
    parts = []
    for n, nrows, shp in PACK_SPLIT:
        t = g[n]
        if n == "w_o":
            t = t.reshape((4,) + shp)
        else:
            t = t.reshape(shp[0], 4, shp[1]).transpose(1, 0, 2)
        parts.append(t.reshape(4, nrows, PACK_W))
    cw = g["conv_w"].reshape(3, 4, 128).transpose(1, 0, 2).reshape(4, 1, CONV_SHARD)
    parts.append(jnp.pad(cw, ((0, 0), (0, 0), (0, PACK_W - CONV_SHARD))))
    parts.append(jnp.zeros((4, PACK_ROWS - MAT_ROWS - 1, PACK_W), F32))
    return jnp.concatenate(parts, axis=1)


def _unpack_grads(red):
    out = {}
    r = 0
    for n, nrows, shp in PACK_SPLIT:
        out[n] = red[r:r + nrows].reshape(shp)
        r += nrows
    out["conv_w"] = red[r, :CONV_SHARD].reshape(3, 128)
    return out


SMALL_SIZES = (("norm_g", D), ("b_gate", 3 * D), ("conv_b", CW), ("q_a_norm_g", QL), ("kv_a_norm_g", KVL),
               ("mla_q_norm_g", QK), ("mla_k_norm_g", QK), ("dil_q_norm_g", NG * HD), ("dil_k_norm_g", NG * HD))
SMALL_ROWS = 88


def _pack_small(per_name):
    flat = jnp.concatenate([per_name[n].reshape(-1).astype(F32) for n, _ in SMALL_SIZES])
    return jnp.pad(flat, (0, SMALL_ROWS * LANE - flat.shape[0])).reshape(SMALL_ROWS, LANE)


def _unpack_small(packed, like):
    out = {}
    flat = packed.reshape(-1)
    r = 0
    for n, sz in SMALL_SIZES:
        out[n] = flat[r:r + NL * sz].reshape(like[n].shape)
        r += NL * sz
    return out


def _adamw_math(w, g, m, v):
    m = ADAM_B1 * m + (1.0 - ADAM_B1) * g
    v = ADAM_B2 * v + (1.0 - ADAM_B2) * jnp.square(g)
    m_hat = m / (1.0 - ADAM_B1 ** ADAM_STEP)
    v_hat = v / (1.0 - ADAM_B2 ** ADAM_STEP)
    delta = -ADAM_LR * (m_hat / (jnp.sqrt(v_hat) + ADAM_EPS) + ADAM_WD * w)
    return delta, m, v


def _adamw(name, w, g, m, v, br, bc=None):
    L, R, C = w.shape
    bc = C if bc is None else bc
    blk = lambda l, i, j: (l, i, j)
    return _pcall(name, _adamw_math, (L, R // br, C // bc), [(t, (None, br, bc), blk) for t in (w, g, m, v)],
                  [((L, R, C), F32, (None, br, bc), blk)] * 3)


ADAM_ROWS = {"w_uq": 256, "w_ukv": 128, "w_out_a": 512, "w_out_b": 512, "w_out_c": 512, "w_o": 256,
             "conv_w": 3}


def kernel(x, norm_g, w_in, b_gate, conv_w, conv_b, q_a_norm_g, w_uq, kv_a_norm_g, w_ukv, mla_q_norm_g, mla_k_norm_g, dil_q_norm_g, dil_k_norm_g, w_out_a, w_out_b, w_out_c, w_o, loss_target, m_norm_g, m_w_in, m_b_gate, m_conv_w, m_conv_b, m_q_a_norm_g, m_w_uq, m_kv_a_norm_g, m_w_ukv, m_mla_q_norm_g, m_mla_k_norm_g, m_dil_q_norm_g, m_dil_k_norm_g, m_w_out_a, m_w_out_b, m_w_out_c, m_w_o, v_norm_g, v_w_in, v_b_gate, v_conv_w, v_conv_b, v_q_a_norm_g, v_w_uq, v_kv_a_norm_g, v_w_ukv, v_mla_q_norm_g, v_mla_k_norm_g, v_dil_q_norm_g, v_dil_k_norm_g, v_w_out_a, v_w_out_b, v_w_out_c, v_w_o):
    W = dict(norm_g=norm_g, w_in=w_in, b_gate=b_gate, conv_w=conv_w, conv_b=conv_b, q_a_norm_g=q_a_norm_g, w_uq=w_uq,
             kv_a_norm_g=kv_a_norm_g, w_ukv=w_ukv, mla_q_norm_g=mla_q_norm_g, mla_k_norm_g=mla_k_norm_g,
             dil_q_norm_g=dil_q_norm_g, dil_k_norm_g=dil_k_norm_g, w_out_a=w_out_a, w_out_b=w_out_b, w_out_c=w_out_c,
             w_o=w_o)
    M = dict(norm_g=m_norm_g, w_in=m_w_in, b_gate=m_b_gate, conv_w=m_conv_w, conv_b=m_conv_b, q_a_norm_g=m_q_a_norm_g,
             w_uq=m_w_uq, kv_a_norm_g=m_kv_a_norm_g, w_ukv=m_w_ukv, mla_q_norm_g=m_mla_q_norm_g,
             mla_k_norm_g=m_mla_k_norm_g, dil_q_norm_g=m_dil_q_norm_g, dil_k_norm_g=m_dil_k_norm_g, w_out_a=m_w_out_a,
             w_out_b=m_w_out_b, w_out_c=m_w_out_c, w_o=m_w_o)
    V = dict(norm_g=v_norm_g, w_in=v_w_in, b_gate=v_b_gate, conv_w=v_conv_w, conv_b=v_conv_b, q_a_norm_g=v_q_a_norm_g,
             w_uq=v_w_uq, kv_a_norm_g=v_kv_a_norm_g, w_ukv=v_w_ukv, mla_q_norm_g=v_mla_q_norm_g,
             mla_k_norm_g=v_mla_k_norm_g, dil_q_norm_g=v_dil_q_norm_g, dil_k_norm_g=v_dil_k_norm_g, w_out_a=v_w_out_a,
             w_out_b=v_w_out_b, w_out_c=v_w_out_c, w_o=v_w_o)
    batch = x.shape[0]
    T = batch * S

    ws = []
    for l in range(NL):
        cont, gath = _all_gather(_w_in_words(w_in[l]), _pack_weights({n: W[n][l] for n in BIG[1:] + ("conv_w",)}))
        full = _unpack_weights(gath)
        pad_qk = lambda t: jnp.pad(t, (0, QKP - QK)).reshape(1, QKP)
        full.update(
            w_in_t=_unpack_w_in(cont),
            norm_g=norm_g[l].reshape(1, D), b_gate=b_gate[l].reshape(1, 3 * D), conv_b=conv_b[l].reshape(1, CW),
            q_a_norm_g=q_a_norm_g[l].reshape(1, QL), kv_a_norm_g=kv_a_norm_g[l].reshape(1, KVL),
            mla_q_norm_g=pad_qk(mla_q_norm_g[l]), mla_k_norm_g=pad_qk(mla_k_norm_g[l]),
            dil_q_norm_g=dil_q_norm_g[l].reshape(NG, 1, HD), dil_k_norm_g=dil_k_norm_g[l].reshape(NG, 1, HD))
        ws.append(full)

    loss, dx, grads = _local_step(x.reshape(T, D), loss_target.reshape(T, D), ws, batch)
    loss = lax.psum(loss[0, 0], ("x", "y", "c"))
    grad_x = dx.reshape(batch, S, D)

    red = []
    for l in range(NL):
        rw, rs = _reduce_scatter(grads[l]["w_in_t"], _pack_grads(grads[l]))
        r = _unpack_grads(rs)
        r["w_in_t"] = rw
        red.append(r)
    G = {n: jnp.stack([red[l][n] for l in range(NL)]) for n in BIG[1:] + ("conv_w",)}
    g_in_t = jnp.stack([red[l]["w_in_t"] for l in range(NL)])
    G["w_in"] = jnp.swapaxes(g_in_t, 1, 2)
    small_g = {n: jnp.stack([grads[l][n].reshape(-1)[:sz] for l in range(NL)]) for n, sz in SMALL_SIZES}
    small_red = _all_reduce_small(_pack_small(small_g))
    G.update(_unpack_small(small_red, {n: W[n] for n in SMALL}))

    delta, new_m, new_v = {}, {}, {}
    for n in BIG[1:] + ("conv_w",):
        delta[n], new_m[n], new_v[n] = _adamw("adamw_" + n, W[n], G[n], M[n], V[n], ADAM_ROWS[n])
    tr = lambda t: jnp.swapaxes(t, 1, 2)
    delta["w_in"], new_m["w_in"], new_v["w_in"] = (
        tr(t) for t in _adamw("adamw_w_in", tr(w_in), g_in_t, tr(m_w_in), tr(v_w_in), WSH, LANE))
    sw, sm, sv = (_pack_small({n: t[n] for n in SMALL})[None] for t in (W, M, V))
    sd, snm, snv = _adamw("adamw_small", sw, small_red[None], sm, sv, SMALL_ROWS)
    like = {n: W[n] for n in SMALL}
    delta.update(_unpack_small(sd[0], like))
    new_m.update(_unpack_small(snm[0], like))
    new_v.update(_unpack_small(snv[0], like))

    return (loss, grad_x, *[G[n] for n in WEIGHTS], *[delta[n] for n in WEIGHTS],
            *[new_m[n] for n in WEIGHTS], *[new_v[n] for n in WEIGHTS])
```

```python
import functools

import numpy as np
import jax
import jax.numpy as jnp
from jax import lax
from jax.experimental import pallas as pl
from jax.experimental.pallas import tpu as pltpu

F32 = jnp.float32
BF16 = jnp.bfloat16

D = 1024
S = 2048
NL = 2
CW = 512
NH = 8
QL = 256
KVL = 128
NOPE = 64
ROPE = 32
VD = 64
QK = NOPE + ROPE
QKP = 128
ROPE_THETA = 10000.0
DIL = ((128, 1), (512, 4), (2048, 16))
NG = 3
DH = 8
HD = 64
DWID = DH * HD
QB = 128
EPS = 1e-6
NIN = 11168
NINP = 11264
O_A, O_CQ, O_CKV, O_KPE, O_BZ, O_DQ, O_DK, O_DV, O_CZ, O_G = 0, 2048, 2304, 2432, 2560, 3072, 4608, 6144, 7680, 8192
KPE_END = 2464
NEG = -1e30
MLA_SCALE = QK ** -0.5
DIL_SCALE = HD ** -0.5
LANE = 128
PACK_W = 512
VMEM_LIMIT = 48 * 1024 * 1024

ADAM_LR = 0.001
ADAM_B1 = 0.9
ADAM_B2 = 0.999
ADAM_EPS = 1e-08
ADAM_WD = 0.01
ADAM_STEP = 10

MESH = pl.DeviceIdType.MESH
BIG = ("w_in", "w_uq", "w_ukv", "w_out_a", "w_out_b", "w_out_c", "w_o")
SMALL = ("norm_g", "b_gate", "conv_b", "q_a_norm_g", "kv_a_norm_g", "mla_q_norm_g", "mla_k_norm_g",
         "dil_q_norm_g", "dil_k_norm_g")
WEIGHTS = ("norm_g", "w_in", "b_gate", "conv_w", "conv_b", "q_a_norm_g", "w_uq", "kv_a_norm_g", "w_ukv",
           "mla_q_norm_g", "mla_k_norm_g", "dil_q_norm_g", "dil_k_norm_g", "w_out_a", "w_out_b", "w_out_c", "w_o")


def _dot(a, b):
    return jnp.dot(a, b, preferred_element_type=F32)


def _dot_nt(a, b):
    return lax.dot_general(a, b, (((1,), (1,)), ((), ())), preferred_element_type=F32)


def _dot_tn(a, b):
    return lax.dot_general(a, b, (((0,), (0,)), ((), ())), preferred_element_type=F32)


def _pcall(name, fn, grid, ins, outs):
    n_in = len(ins)
    n_out = len(outs)
    acc_axis = len(grid) - 1
    is_acc = [len(o) > 4 and o[4] for o in outs]
    outs = [o[:4] for o in outs]

    def body(*refs):
        vals = fn(*[r[...] for r in refs[:n_in]])
        if not isinstance(vals, (tuple, list)):
            vals = (vals,)
        for k in range(n_out):
            r = refs[n_in + k]
            v = vals[k].astype(r.dtype).reshape(r.shape)
            if is_acc[k]:
                first = pl.program_id(acc_axis) == 0

                @pl.when(first)
                def _():
                    r[...] = v

                @pl.when(jnp.logical_not(first))
                def _():
                    r[...] += v
            else:
                r[...] = v

    return pl.pallas_call(
        body,
        grid=grid,
        in_specs=[pl.BlockSpec(bs, im) for _, bs, im in ins],
        out_specs=[pl.BlockSpec(bs, im) for _, _, bs, im in outs],
        out_shape=[jax.ShapeDtypeStruct(sh, dt) for sh, dt, _, _ in outs],
        name=name,
        compiler_params=pltpu.CompilerParams(
            dimension_semantics=("arbitrary",) * len(grid), vmem_limit_bytes=VMEM_LIMIT),
    )(*[a for a, _, _ in ins])


def _mm(name, a, b, *, ta=False, tb=False, out_dtype=F32, add=None, tm=512, tn=1024, tk=1024):
    if ta:
        K, M = a.shape
    else:
        M, K = a.shape
    if tb:
        N, K2 = b.shape
    else:
        K2, N = b.shape
    assert K == K2, (name, a.shape, b.shape)
    tm, tn, tk = min(tm, M), min(tn, N), min(tk, K)
    assert M % tm == 0 and N % tn == 0 and K % tk == 0, (name, M, N, K)
    nk = K // tk
    dims = (((0 if ta else 1,), (1 if tb else 0,)), ((), ()))
    a_spec = pl.BlockSpec((tk, tm), lambda j, i, k: (k, i)) if ta else pl.BlockSpec((tm, tk), lambda j, i, k: (i, k))
    b_spec = pl.BlockSpec((tn, tk), lambda j, i, k: (j, k)) if tb else pl.BlockSpec((tk, tn), lambda j, i, k: (k, j))
    o_spec = pl.BlockSpec((tm, tn), lambda j, i, k: (i, j))
    has_add = add is not None

    def body(*refs):
        a_ref, b_ref = refs[0], refs[1]
        add_ref = refs[2] if has_add else None
        o_ref = refs[3] if has_add else refs[2]
        p = lax.dot_general(a_ref[...].astype(BF16), b_ref[...].astype(BF16), dims, preferred_element_type=F32)
        if nk == 1:
            if has_add:
                p = p + add_ref[...]
            o_ref[...] = p.astype(out_dtype)
        else:
            acc = refs[-1]
            k = pl.program_id(2)

            @pl.when(k == 0)
            def _():
                acc[...] = p

            @pl.when(k > 0)
            def _():
                acc[...] += p

            @pl.when(k == nk - 1)
            def _():
                r = acc[...]
                if has_add:
                    r = r + add_ref[...]
                o_ref[...] = r.astype(out_dtype)

    in_specs = [a_spec, b_spec] + ([o_spec] if has_add else [])
    args = [a, b] + ([add] if has_add else [])
    return pl.pallas_call(
        body,
        grid=(N // tn, M // tm, nk),
        in_specs=in_specs,
        out_specs=o_spec,
        out_shape=jax.ShapeDtypeStruct((M, N), out_dtype),
        scratch_shapes=[pltpu.VMEM((tm, tn), F32)] if nk > 1 else [],
        name=name,
        compiler_params=pltpu.CompilerParams(
            dimension_semantics=("arbitrary", "arbitrary", "arbitrary"), vmem_limit_bytes=VMEM_LIMIT),
    )(*args)


def _vjp_of(f, n_diff):
    def g(*args, n_prim):
        prim = args[:n_diff]
        consts = args[n_diff:n_prim]
        cts = args[n_prim:]
        _, pull = jax.vjp(lambda *p: f(*p, *consts), *prim)
        out = jax.eval_shape(lambda *p: f(*p, *consts), *prim)
        if isinstance(out, (tuple, list)):
            cts = tuple(c.astype(o.dtype) for c, o in zip(cts, out))
        else:
            cts = cts[0].astype(out.dtype)
        return pull(cts)
    return g


def _rms(x, g, n=None):
    n = x.shape[-1] if n is None else n
    ms = jnp.sum(x * x, axis=-1, keepdims=True) / n
    return x * lax.rsqrt(ms + EPS) * g


def _silu(z):
    return z * jax.nn.sigmoid(z)


def _roll_rows(u, k):
    n = u.shape[0]
    r = pltpu.roll(u, k % n, 0)
    t = lax.broadcasted_iota(jnp.int32, u.shape, 0)
    if k > 0:
        return jnp.where(t >= k, r, 0.0)
    return jnp.where(t < n + k, r, 0.0)


@functools.partial(jax.custom_vjp, nondiff_argnums=(1,))
def _shift(u, k):
    return _roll_rows(u, k)


def _shift_fwd(u, k):
    return _roll_rows(u, k), None


def _shift_bwd(k, _, g):
    return (_roll_rows(g, -k),)


_shift.defvjp(_shift_fwd, _shift_bwd)


@functools.partial(jax.custom_vjp, nondiff_argnums=(1,))
def _lane_roll(u, k):
    return pltpu.roll(u, k % LANE, 1)


def _lane_roll_fwd(u, k):
    return pltpu.roll(u, k % LANE, 1), None


def _lane_roll_bwd(k, _, g):
    return (pltpu.roll(g, (-k) % LANE, 1),)


_lane_roll.defvjp(_lane_roll_fwd, _lane_roll_bwd)


def _conv_math(ab, ac, ax, az, cw, cb):
    u = ac * ax
    conv = cb + _shift(u, 2) * cw[0:1] + _shift(u, 1) * cw[1:2] + u * cw[2:3]
    return ab * conv * _silu(az)


def _mla_pre_math(cq, ckv, gq, gkv):
    return _rms(cq, gq), _rms(ckv, gkv)


def _rope_math(q, kn, kpe, gq, gk, c, s1, s2):
    lane = lax.broadcasted_iota(jnp.int32, kpe.shape, 1)
    pe = _lane_roll(jnp.where(lane < ROPE, kpe, 0.0), NOPE)

    def one(t, g):
        tn = _rms(t, g, QK)
        return tn * c + _lane_roll(tn, -16) * s1 + _lane_roll(tn, 16) * s2

    qs, ks = [], []
    for h in range(NH):
        sl = slice(h * QKP, (h + 1) * QKP)
        qs.append(one(q[:, sl], gq))
        ks.append(one(kn[:, sl] + pe, gk))
    return jnp.concatenate(qs, axis=1), jnp.concatenate(ks, axis=1)


def _gate_math(o, z):
    return o * _silu(z)


def _mergec_math(o0, o1, o2, l0, l1, l2, cz):
    m = lax.stop_gradient(jnp.maximum(jnp.maximum(l0, l1), l2))
    e0, e1, e2 = jnp.exp(l0 - m), jnp.exp(l1 - m), jnp.exp(l2 - m)
    den = e0 + e1 + e2
    oc = (e0 / den) * o0 + (e1 / den) * o1 + (e2 / den) * o2
    return oc * _silu(cz)


def _merge_math(g0, g1, g2, b0, b1, b2, pa, pb, pc):
    return (jax.nn.sigmoid(g0 + b0) * pa + jax.nn.sigmoid(g1 + b1) * pb) + jax.nn.sigmoid(g2 + b2) * pc


MLA_T = 256
MLA_UNROLL = True


def _mla_fwd(q, k, v):
    B = q.shape[0]
    T = MLA_T
    NB = S // T

    def body(q_ref, k_ref, v_ref, o_ref, l_ref):
        row = lax.broadcasted_iota(jnp.int32, (T, T), 0)
        col = lax.broadcasted_iota(jnp.int32, (T, T), 1)
        lo = _lo_mask((T, LANE))

        for qi in range(NB):
            qb = q_ref[qi * T:(qi + 1) * T, :]

            def step(j, carry, diagonal):
                m, l, acc = carry
                off = pl.multiple_of(j * T, T)
                kb = k_ref[pl.ds(off, T), :]
                vb = v_ref[pl.ds(off, T), :]
                ss = []
                for e in (0, 1):
                    se = _dot_nt(qb[:, e * QKP:(e + 1) * QKP], kb[:, e * QKP:(e + 1) * QKP]) * MLA_SCALE
                    ss.append(jnp.where(col <= row, se, NEG) if diagonal else se)
                s = jnp.concatenate(ss, axis=0)
                m_new = jnp.maximum(m, jnp.max(s, axis=-1, keepdims=True))
                a = jnp.exp(m - m_new)
                p = jnp.exp(s - m_new)
                l = a * l + jnp.sum(p, axis=-1, keepdims=True)
                acc = a * acc + _dot(p.astype(BF16), vb)
                return m_new, l, acc

            init = (jnp.full((2 * T, 1), NEG, F32), jnp.zeros((2 * T, 1), F32), jnp.zeros((2 * T, LANE), F32))
            carry = lax.fori_loop(0, qi, functools.partial(step, diagonal=False), init, unroll=MLA_UNROLL)
            m, l, acc = step(qi, carry, True)
            o = acc / l
            lse = m + jnp.log(l)
            o_ref[qi * T:(qi + 1) * T, :] = jnp.where(lo, o[:T], o[T:])
            l_ref[qi * T:(qi + 1) * T, :] = jnp.where(lo, lse[:T], lse[T:])

    def spec(w):
        return pl.BlockSpec((None, S, w), lambda b, hp: (b, 0, hp))

    return pl.pallas_call(
        body,
        grid=(B, NH // 2),
        in_specs=[spec(2 * QKP), spec(2 * QKP), spec(LANE)],
        out_specs=[spec(LANE), spec(LANE)],
        out_shape=[jax.ShapeDtypeStruct((B, S, NH * VD), F32)] * 2,
        name="mla_attn_fwd",
        compiler_params=pltpu.CompilerParams(dimension_semantics=("arbitrary",) * 2, vmem_limit_bytes=VMEM_LIMIT),
    )(q, k, v)


def _mla_bwd(q, k, v, do, o, lse):
    B = q.shape[0]
    T = MLA_T
    NB = S // T

    def body(q_ref, k_ref, v_ref, do_ref, o_ref, l_ref, dq_ref, dk_ref, dv_ref, delta_ref):
        delta_ref[...] = _head_sum(do_ref[...] * o_ref[...])
        row = lax.broadcasted_iota(jnp.int32, (T, T), 0)
        col = lax.broadcasted_iota(jnp.int32, (T, T), 1)
        lo = _lo_mask((T, LANE))

        for j in range(NB):
            krows = slice(j * T, (j + 1) * T)
            kb = k_ref[krows, :]
            vb = v_ref[krows, :]
            dk = [jnp.zeros((T, QKP), F32), jnp.zeros((T, QKP), F32)]
            dv = jnp.zeros((T, LANE), F32)
            for i in range(j, NB):
                qrows = slice(i * T, (i + 1) * T)
                qb = q_ref[qrows, :]
                do2 = _stack_heads(do_ref[qrows, :], lo).astype(BF16)
                lb = l_ref[qrows, :]
                db = delta_ref[qrows, :]
                dp2 = _dot_nt(do2, vb)
                for e in (0, 1):
                    cols = slice(e * QKP, (e + 1) * QKP)
                    qe, ke = qb[:, cols], kb[:, cols]
                    s = _dot_nt(qe, ke) * MLA_SCALE
                    if i == j:
                        s = jnp.where(col <= row, s, NEG)
                    p = jnp.exp(s - lb[:, e * HD:e * HD + 1])
                    dv = dv + _dot_tn(p.astype(BF16), do2[e * T:(e + 1) * T])
                    ds = (p * (dp2[e * T:(e + 1) * T] - db[:, e * HD:e * HD + 1]) * MLA_SCALE).astype(BF16)
                    dk[e] = dk[e] + _dot_tn(ds, qe)
                    if j == 0:
                        dq_ref[qrows, cols] = _dot(ds, ke)
                    else:
                        dq_ref[qrows, cols] += _dot(ds, ke)
            dk_ref[krows, 0:QKP] = dk[0]
            dk_ref[krows, QKP:2 * QKP] = dk[1]
            dv_ref[krows, :] = dv

    def spec(w):
        return pl.BlockSpec((None, S, w), lambda b, hp: (b, 0, hp))

    return pl.pallas_call(
        body,
        grid=(B, NH // 2),
        in_specs=[spec(2 * QKP), spec(2 * QKP), spec(LANE), spec(LANE), spec(LANE), spec(LANE)],
        out_specs=[spec(2 * QKP), spec(2 * QKP), spec(LANE)],
        out_shape=[jax.ShapeDtypeStruct((B, S, NH * QKP), F32), jax.ShapeDtypeStruct((B, S, NH * QKP), F32),
                   jax.ShapeDtypeStruct((B, S, NH * VD), F32)],
        scratch_shapes=[pltpu.VMEM((S, LANE), F32)],
        name="mla_attn_bwd",
        compiler_params=pltpu.CompilerParams(dimension_semantics=("arbitrary",) * 2, vmem_limit_bytes=VMEM_LIMIT),
    )(q, k, v, do, o, lse)


def _lo_mask(shape):
    return lax.broadcasted_iota(jnp.int32, shape, len(shape) - 1) < HD


def _head_sum(u):
    r = lax.broadcasted_iota(jnp.int32, (LANE, LANE), 0) < HD
    c = lax.broadcasted_iota(jnp.int32, (LANE, LANE), 1) < HD
    ones = jnp.where(r == c, 1.0, 0.0).astype(BF16)
    hi = u.astype(BF16)
    lo = (u - hi.astype(F32)).astype(BF16)
    return _dot(hi, ones) + _dot(lo, ones)


def _rms2(x, g):
    return x * lax.rsqrt(_head_sum(x * x) / HD + EPS) * g


def _dil_bias(t_ref, gi, d):
    qq = lax.broadcasted_iota(jnp.int32, (QB, QB), 0)
    kk = lax.broadcasted_iota(jnp.int32, (QB, QB), 1)
    jc = (qq - kk).astype(F32)
    rows = []
    for e in (0, 1):
        sl = t_ref[2 * gi + e:2 * gi + e + 1, :] * float(d)
        bp = jnp.where(kk >= qq, -sl * (jc + float(QB)), NEG)
        bc = jnp.where(kk <= qq, -sl * jc, NEG)
        rows.append(jnp.concatenate([bp, bc], axis=1))
    return jnp.concatenate(rows, axis=0)


def _dil_rows(cur, d):
    return pl.ds(cur, QB, stride=d) if d > 1 else pl.ds(pl.multiple_of(cur, QB), QB)


def _dil_walk(d, block, full):
    if d == 1:
        block(0, None)

        def body(i, c):
            block(i * QB, (i - 1) * QB)
            return c
        lax.fori_loop(1, S // QB, body, 0, unroll=True if full else 5)
    elif d == 16:
        def body(r, c):
            block(r, None)
            return c
        lax.fori_loop(0, d, body, 0, unroll=True if full else 4)
    else:
        nb = S // d // QB

        def cls(r, c):
            block(r, None)

            def body(i, c2):
                block(r + i * QB * d, r + (i - 1) * QB * d)
                return c2
            lax.fori_loop(1, nb, body, 0, unroll=True)
            return c
        lax.fori_loop(0, d, cls, 0, unroll=full)


def _stack_heads(x, lo):
    return jnp.concatenate([jnp.where(lo, x, 0.0), jnp.where(lo, 0.0, x)], axis=0)


def _dilc_fwd(proj3, gq, gk, tab):
    B = proj3.shape[0]

    def body(q_ref, k_ref, v_ref, cz_ref, gq_ref, gk_ref, t_ref, y_ref, o_ref, l_ref, qs, ks, vs):
        g = pl.program_id(2)
        lo = _lo_mask((QB, LANE))

        def group(gi):
            d = DIL[gi][1]
            qs[...] = _rms2(q_ref[...], gq_ref[gi:gi + 1, :])
            ks[...] = _rms2(k_ref[...], gk_ref[gi:gi + 1, :])
            vs[...] = v_ref[...]
            bias = _dil_bias(t_ref, gi, d)

            def block(cur, prev):
                rows = _dil_rows(cur, d)
                q2 = _stack_heads(qs[rows, :], lo).astype(BF16)
                kc, vc = ks[rows, :], vs[rows, :]
                if prev is None:
                    kcat, vcat, b = kc, vc, bias[:, QB:]
                else:
                    prow = _dil_rows(prev, d)
                    kcat = jnp.concatenate([ks[prow, :], kc], axis=0)
                    vcat = jnp.concatenate([vs[prow, :], vc], axis=0)
                    b = bias
                s = _dot_nt(q2, kcat.astype(BF16)) * DIL_SCALE + b
                m = jnp.max(s, axis=-1, keepdims=True)
                p = jnp.exp(s - m)
                l = jnp.sum(p, axis=-1, keepdims=True)
                o = _dot(p.astype(BF16), vcat.astype(BF16)) / l
                lse = m + jnp.log(l)
                o_ref[gi, rows, :] = jnp.where(lo, o[:QB], o[QB:])
                l_ref[gi, rows, :] = jnp.where(lo, lse[:QB], lse[QB:])

            _dil_walk(d, block, True)

        for gi in range(NG):
            pl.when(g == gi)(functools.partial(group, gi))

        @pl.when(g == NG - 1)
        def _():
            y_ref[...] = _mergec_math(o_ref[0], o_ref[1], o_ref[2], l_ref[0], l_ref[1], l_ref[2],
                                      cz_ref[...]).astype(BF16)

    def col(base):
        return pl.BlockSpec((None, S, LANE), lambda b, hp, g: (b, 0, base // LANE + 4 * g + hp))

    gspec = pl.BlockSpec((NG, LANE), lambda b, hp, g: (0, 0))
    saved = pl.BlockSpec((NG, None, S, LANE), lambda b, hp, g: (0, b, 0, hp))
    return pl.pallas_call(
        body,
        grid=(B, 4, NG),
        in_specs=[col(O_DQ), col(O_DK), col(O_DV),
                  pl.BlockSpec((None, S, LANE), lambda b, hp, g: (b, 0, O_CZ // LANE + hp)),
                  gspec, gspec, pl.BlockSpec((None, 8, LANE), lambda b, hp, g: (hp, 0, 0))],
        out_specs=[pl.BlockSpec((None, S, LANE), lambda b, hp, g: (b, 0, hp)), saved, saved],
        out_shape=[jax.ShapeDtypeStruct((B, S, DWID), BF16), jax.ShapeDtypeStruct((NG, B, S, DWID), F32),
                   jax.ShapeDtypeStruct((NG, B, S, DWID), F32)],
        scratch_shapes=[pltpu.VMEM((S, LANE), F32)] * 3,
        name="dil_mixer_fwd",
        compiler_params=pltpu.CompilerParams(dimension_semantics=("arbitrary",) * 3, vmem_limit_bytes=VMEM_LIMIT),
    )(proj3, proj3, proj3, proj3, gq, gk, tab)


def _dilc_bwd(proj3, gq, gk, tab, o_all, l_all, d_yc):
    B = proj3.shape[0]

    def body(q_ref, k_ref, v_ref, cz_ref, gq_ref, gk_ref, t_ref, o_ref, l_ref, dy_ref,
             dq_out, dk_out, dv_out, dcz_out, dgq_out, dgk_out, qs, ks, vs, dos, dls, dqs, dks, dvs):
        g = pl.program_id(2)
        lo = _lo_mask((QB, LANE))

        @pl.when(jnp.logical_and(jnp.logical_and(pl.program_id(0) == 0, pl.program_id(1) == 0), g == 0))
        def _():
            dgq_out[...] = jnp.zeros((NG, LANE), F32)
            dgk_out[...] = jnp.zeros((NG, LANE), F32)

        def group(gi):
            d = DIL[gi][1]
            ls = [l_ref[j] for j in range(NG)]
            m = jnp.maximum(jnp.maximum(ls[0], ls[1]), ls[2])
            es = [jnp.exp(t - m) for t in ls]
            den = (es[0] + es[1]) + es[2]
            al = [e / den for e in es]
            os_ = [o_ref[j] for j in range(NG)]
            oc = (al[0] * os_[0] + al[1] * os_[1]) + al[2] * os_[2]
            cz = cz_ref[...]
            sg = jax.nn.sigmoid(cz)
            dy = dy_ref[...]
            d_oc = dy * (cz * sg)
            dcz_out[...] = (dy * oc * (sg * (1.0 + cz * (1.0 - sg)))).astype(BF16)
            ts = [_head_sum(d_oc * os_[j]) for j in range(NG)]
            tbar = (al[0] * ts[0] + al[1] * ts[1]) + al[2] * ts[2]
            dos[...] = al[gi] * d_oc
            dls[...] = al[gi] * (ts[gi] - tbar)

            qs[...] = _rms2(q_ref[...], gq_ref[gi:gi + 1, :])
            ks[...] = _rms2(k_ref[...], gk_ref[gi:gi + 1, :])
            vs[...] = v_ref[...]
            dks[...] = jnp.zeros((S, LANE), F32)
            dvs[...] = jnp.zeros((S, LANE), F32)
            bias = _dil_bias(t_ref, gi, d)

            def block(cur, prev):
                rows = _dil_rows(cur, d)
                q2 = _stack_heads(qs[rows, :], lo).astype(BF16)
                dob = dos[rows, :]
                do2 = _stack_heads(dob, lo).astype(BF16)
                kc, vc = ks[rows, :], vs[rows, :]
                if prev is None:
                    kcat, vcat, b = kc, vc, bias[:, QB:]
                else:
                    prow = _dil_rows(prev, d)
                    kcat = jnp.concatenate([ks[prow, :], kc], axis=0)
                    vcat = jnp.concatenate([vs[prow, :], vc], axis=0)
                    b = bias
                kcat = kcat.astype(BF16)
                vcat = vcat.astype(BF16)
                lse_b = l_ref[gi, rows, :]
                corr_b = dls[rows, :] - _head_sum(dob * o_ref[gi, rows, :])
                lse2 = jnp.concatenate([lse_b[:, 0:1], lse_b[:, HD:HD + 1]], axis=0)
                corr2 = jnp.concatenate([corr_b[:, 0:1], corr_b[:, HD:HD + 1]], axis=0)
                s = _dot_nt(q2, kcat) * DIL_SCALE + b
                p = jnp.exp(s - lse2)
                ds = (p * (_dot_nt(do2, vcat) + corr2) * DIL_SCALE).astype(BF16)
                dq2 = _dot(ds, kcat)
                dqs[rows, :] = jnp.where(lo, dq2[:QB], dq2[QB:])
                dk = _dot_tn(ds, q2)
                dv = _dot_tn(p.astype(BF16), do2)
                if prev is None:
                    dks[rows, :] += dk
                    dvs[rows, :] += dv
                else:
                    dks[prow, :] += dk[:QB]
                    dvs[prow, :] += dv[:QB]
                    dks[rows, :] += dk[QB:]
                    dvs[rows, :] += dv[QB:]

            _dil_walk(d, block, False)

            _, pull_q = jax.vjp(_rms2, q_ref[...], gq_ref[gi:gi + 1, :])
            dxq, dgq = pull_q(dqs[...])
            dq_out[...] = dxq.astype(BF16)
            dgq_out[gi:gi + 1, :] += dgq
            _, pull_k = jax.vjp(_rms2, k_ref[...], gk_ref[gi:gi + 1, :])
            dxk, dgk = pull_k(dks[...])
            dk_out[...] = dxk.astype(BF16)
            dgk_out[gi:gi + 1, :] += dgk
            dv_out[...] = dvs[...].astype(BF16)

        for gi in range(NG):
            pl.when(g == gi)(functools.partial(group, gi))

    def col(base):
        return pl.BlockSpec((None, S, LANE), lambda b, hp, g: (b, 0, base // LANE + 4 * g + hp))

    gspec = pl.BlockSpec((NG, LANE), lambda b, hp, g: (0, 0))
    saved = pl.BlockSpec((NG, None, S, LANE), lambda b, hp, g: (0, b, 0, hp))
    per_pair = pl.BlockSpec((None, S, LANE), lambda b, hp, g: (b, 0, hp))
    dcol = pl.BlockSpec((None, S, LANE), lambda b, hp, g: (b, 0, 4 * g + hp))
    return pl.pallas_call(
        body,
        grid=(B, 4, NG),
        in_specs=[col(O_DQ), col(O_DK), col(O_DV),
                  pl.BlockSpec((None, S, LANE), lambda b, hp, g: (b, 0, O_CZ // LANE + hp)),
                  gspec, gspec, pl.BlockSpec((None, 8, LANE), lambda b, hp, g: (hp, 0, 0)),
                  saved, saved, per_pair],
        out_specs=[dcol, dcol, dcol, per_pair, gspec, gspec],
        out_shape=[jax.ShapeDtypeStruct((B, S, NG * DWID), BF16)] * 3
        + [jax.ShapeDtypeStruct((B, S, DWID), BF16), jax.ShapeDtypeStruct((NG, LANE), F32),
           jax.ShapeDtypeStruct((NG, LANE), F32)],
        scratch_shapes=[pltpu.VMEM((S, LANE), F32)] * 8,
        name="dil_mixer_bwd",
        compiler_params=pltpu.CompilerParams(dimension_semantics=("arbitrary",) * 3, vmem_limit_bytes=VMEM_LIMIT),
    )(proj3, proj3, proj3, proj3, gq, gk, tab, o_all, l_all, d_yc)


def _dil_slopes():
    slopes = (2.0 ** (-8.0 * np.arange(1, NG * DH + 1, dtype=np.float32) / (NG * DH))).astype(np.float32).reshape(NG, DH)
    tab = np.zeros((4, 8, LANE), np.float32)
    for hp in range(4):
        for gi in range(NG):
            for e in (0, 1):
                tab[hp, 2 * gi + e, :] = slopes[gi, 2 * hp + e]
    return jnp.asarray(tab)


def _rope_tables():
    inv = ROPE_THETA ** (-jnp.arange(0, ROPE, 2, dtype=F32) / ROPE)
    ang = jnp.arange(S, dtype=F32)[:, None] * inv[None, :]
    cos, sin = jnp.cos(ang), jnp.sin(ang)
    z16 = jnp.zeros((S, 16), F32)
    c = jnp.concatenate([jnp.ones((S, NOPE), F32), cos, cos, jnp.zeros((S, 32), F32)], axis=1)
    s1 = jnp.concatenate([jnp.zeros((S, NOPE), F32), -sin, z16, jnp.zeros((S, 32), F32)], axis=1)
    s2 = jnp.concatenate([jnp.zeros((S, NOPE), F32), z16, sin, jnp.zeros((S, 32), F32)], axis=1)
    return c, s1, s2


def _pad_heads_uq(w):
    return jnp.pad(w.reshape(QL, NH, QK), ((0, 0), (0, 0), (0, QKP - QK))).reshape(QL, NH * QKP)


def _unpad_heads_uq(g):
    return g.reshape(QL, NH, QKP)[:, :, :QK].reshape(QL, NH * QK)


def _split_ukv(w):
    w3 = w.reshape(KVL, NH, NOPE + VD)
    uk = jnp.pad(w3[:, :, :NOPE], ((0, 0), (0, 0), (0, QKP - NOPE))).reshape(KVL, NH * QKP)
    return uk, w3[:, :, NOPE:].reshape(KVL, NH * VD)


def _join_ukv(guk, guv):
    return jnp.concatenate([guk.reshape(KVL, NH, QKP)[:, :, :NOPE], guv.reshape(KVL, NH, VD)],
                           axis=-1).reshape(KVL, NH * (NOPE + VD))


BR = 512
BRM = 256


def _layer_fwd(x, w, tabs, batch):
    T = batch * S
    rope_c, rope_s1, rope_s2, dil_tab = tabs
    res = {"x": x}
    row = lambda c: (lambda i: (i, c))
    fix = lambda i: (0, 0)

    h = _pcall("norm_fwd", _rms, (T // BR,),
               [(x, (BR, D), row(0)), (w["norm_g"], (1, D), fix)],
               [((T, D), BF16, (BR, D), row(0))])[0]
    proj = _mm("in_proj", h, w["w_in_t"], tb=True, tm=512, tn=1024)
    res["h"], res["proj"] = h, proj
    proj3 = proj.reshape(batch, S, NINP)

    cblk = lambda s: (lambda j, b: (b, 0, 4 * s + j))
    y_a = _pcall("conv_fwd", _conv_math, (4, batch),
                 [(proj3, (None, S, LANE), cblk(0)), (proj3, (None, S, LANE), cblk(1)),
                  (proj3, (None, S, LANE), cblk(2)), (proj3, (None, S, LANE), cblk(3)),
                  (w["conv_w"], (3, LANE), lambda j, b: (0, j)), (w["conv_b"], (1, LANE), lambda j, b: (0, j))],
                 [((batch, S, CW), BF16, (None, S, LANE), lambda j, b: (b, 0, j))])[0].reshape(T, CW)
    res["y_a"] = y_a

    cqn, ckvn = _pcall("mla_pre_fwd", _mla_pre_math, (T // BR,),
                       [(proj, (BR, QL), row(O_CQ // QL)), (proj, (BR, KVL), row(O_CKV // KVL)),
                        (w["q_a_norm_g"], (1, QL), fix), (w["kv_a_norm_g"], (1, KVL), fix)],
                       [((T, QL), BF16, (BR, QL), row(0)), ((T, KVL), BF16, (BR, KVL), row(0))])
    w_uq_p = _pad_heads_uq(w["w_uq"])
    w_uk, w_uv = _split_ukv(w["w_ukv"])
    q = _mm("uq", cqn, w_uq_p)
    kn = _mm("uk", ckvn, w_uk)
    v = _mm("uv", ckvn, w_uv, out_dtype=BF16)
    nrr = S // BR
    tab_row = lambda i: (i % nrr, 0)
    qr, kr = _pcall("rope_fwd", _rope_math, (T // BR,),
                    [(q, (BR, NH * QKP), row(0)), (kn, (BR, NH * QKP), row(0)), (proj, (BR, LANE), row(O_KPE // LANE)),
                     (w["mla_q_norm_g"], (1, QKP), fix), (w["mla_k_norm_g"], (1, QKP), fix),
                     (rope_c, (BR, QKP), tab_row), (rope_s1, (BR, QKP), tab_row), (rope_s2, (BR, QKP), tab_row)],
                    [((T, NH * QKP), BF16, (BR, NH * QKP), row(0))] * 2)
    qr = qr.reshape(batch, S, NH * QKP)
    kr = kr.reshape(batch, S, NH * QKP)
    v = v.reshape(batch, S, NH * VD)
    o_b, l_b = _mla_fwd(qr, kr, v)
    ob2 = o_b.reshape(T, NH * VD)
    y_b = _pcall("gateb_fwd", _gate_math, (T // BR,),
                 [(ob2, (BR, 512), row(0)), (proj, (BR, 512), row(O_BZ // 512))],
                 [((T, 512), BF16, (BR, 512), row(0))])[0]
    res.update(cqn=cqn, ckvn=ckvn, q=q, kn=kn, qr=qr, kr=kr, v=v, o_b=o_b, l_b=l_b, ob2=ob2, y_b=y_b,
               w_uq_p=w_uq_p, w_uk=w_uk, w_uv=w_uv)

    gq2 = jnp.tile(w["dil_q_norm_g"].reshape(NG, HD), (1, 2))
    gk2 = jnp.tile(w["dil_k_norm_g"].reshape(NG, HD), (1, 2))
    y_c, o_all, l_all = _dilc_fwd(proj3, gq2, gk2, dil_tab)
    y_c = y_c.reshape(T, DWID)
    res.update(o_all=o_all, l_all=l_all, y_c=y_c)

    pa = _mm("out_a", y_a, w["w_out_a"])
    pb = _mm("out_b", y_b, w["w_out_b"])
    pc = _mm("out_c", y_c, w["w_out_c"])
    merged = _pcall("merge_fwd", _merge_math, (T // BRM,),
                    [(proj, (BRM, D), row(O_G // D + s)) for s in range(3)]
                    + [(w["b_gate"], (1, D), (lambda s: (lambda i: (0, s)))(s)) for s in range(3)]
                    + [(t, (BRM, D), row(0)) for t in (pa, pb, pc)],
                    [((T, D), BF16, (BRM, D), row(0))])[0]
    out = _mm("o_proj", merged, w["w_o"], add=x)
    res.update(pa=pa, pb=pb, pc=pc, merged=merged)
    return out, res


def _norm_bwd_math(x, g, dh, dy):
    _, pull = jax.vjp(_rms, x, g)
    dx, dg = pull(dh)
    return dx + dy, dg


def _layer_bwd(dy, w, res, tabs, batch):
    T = batch * S
    rope_c, rope_s1, rope_s2, dil_tab = tabs
    row = lambda c: (lambda i: (i, c))
    fix = lambda i: (0, 0)
    x, proj, h = res["x"], res["proj"], res["h"]
    proj3 = proj.reshape(batch, S, NINP)
    g = {}

    d_merged = _mm("o_proj_dx", dy, w["w_o"], tb=True)
    g["w_o"] = _mm("o_proj_dw", res["merged"], dy, ta=True, tm=1024)

    merge_bwd = functools.partial(_vjp_of(_merge_math, 9), n_prim=9)
    dg0, dg1, dg2, db0, db1, db2, dpa, dpb, dpc = _pcall(
        "merge_bwd", merge_bwd, (T // BRM,),
        [(proj, (BRM, D), row(O_G // D + s)) for s in range(3)]
        + [(w["b_gate"], (1, D), (lambda s: (lambda i: (0, s)))(s)) for s in range(3)]
        + [(t, (BRM, D), row(0)) for t in (res["pa"], res["pb"], res["pc"])]
        + [(d_merged, (BRM, D), row(0))],
        [((T, D), BF16, (BRM, D), row(0))] * 3 + [((1, D), F32, (1, D), fix, True)] * 3
        + [((T, D), BF16, (BRM, D), row(0))] * 3)
    g["b_gate"] = jnp.concatenate([db0, db1, db2], axis=1)

    d_ya = _mm("out_a_dx", dpa, w["w_out_a"], tb=True)
    d_yb = _mm("out_b_dx", dpb, w["w_out_b"], tb=True)
    d_yc = _mm("out_c_dx", dpc, w["w_out_c"], tb=True)
    g["w_out_a"] = _mm("out_a_dw", res["y_a"], dpa, ta=True)
    g["w_out_b"] = _mm("out_b_dw", res["y_b"], dpb, ta=True)
    g["w_out_c"] = _mm("out_c_dw", res["y_c"], dpc, ta=True)

    cblk = lambda s: (lambda j, b: (b, 0, 4 * s + j))
    oblk = lambda j, b: (b, 0, j)
    conv_bwd = functools.partial(_vjp_of(_conv_math, 6), n_prim=6)
    d_ab, d_ac, d_ax, d_az, g["conv_w"], g["conv_b"] = _pcall(
        "conv_bwd", conv_bwd, (4, batch),
        [(proj3, (None, S, LANE), cblk(s)) for s in range(4)]
        + [(w["conv_w"], (3, LANE), lambda j, b: (0, j)), (w["conv_b"], (1, LANE), lambda j, b: (0, j)),
           (d_ya.reshape(batch, S, CW), (None, S, LANE), oblk)],
        [((batch, S, CW), BF16, (None, S, LANE), oblk)] * 4
        + [((3, CW), F32, (3, LANE), lambda j, b: (0, j), True), ((1, CW), F32, (1, LANE), lambda j, b: (0, j), True)])

    gate_bwd = functools.partial(_vjp_of(_gate_math, 2), n_prim=2)
    d_ob, d_bz = _pcall("gateb_bwd", gate_bwd, (T // BR,),
                        [(res["ob2"], (BR, 512), row(0)), (proj, (BR, 512), row(O_BZ // 512)), (d_yb, (BR, 512), row(0))],
                        [((T, 512), F32, (BR, 512), row(0)), ((T, 512), BF16, (BR, 512), row(0))])
    dqr, dkr, dv = _mla_bwd(res["qr"], res["kr"], res["v"], d_ob.reshape(batch, S, NH * VD), res["o_b"], res["l_b"])
    nrr = S // BR
    tab_row = lambda i: (i % nrr, 0)
    rope_bwd = functools.partial(_vjp_of(_rope_math, 5), n_prim=8)
    d_q, d_kn, d_kpe_p, g["mla_q_norm_g"], g["mla_k_norm_g"] = _pcall(
        "rope_bwd", rope_bwd, (T // BR,),
        [(res["q"], (BR, NH * QKP), row(0)), (res["kn"], (BR, NH * QKP), row(0)), (proj, (BR, LANE), row(O_KPE // LANE)),
         (w["mla_q_norm_g"], (1, QKP), fix), (w["mla_k_norm_g"], (1, QKP), fix),
         (rope_c, (BR, QKP), tab_row), (rope_s1, (BR, QKP), tab_row), (rope_s2, (BR, QKP), tab_row),
         (dqr.reshape(T, NH * QKP), (BR, NH * QKP), row(0)), (dkr.reshape(T, NH * QKP), (BR, NH * QKP), row(0))],
        [((T, NH * QKP), BF16, (BR, NH * QKP), row(0))] * 2 + [((T, LANE), BF16, (BR, LANE), row(0))]
        + [((1, QKP), F32, (1, QKP), fix, True)] * 2)
    dv = dv.reshape(T, NH * VD)
    d_cqn = _mm("uq_dx", d_q, res["w_uq_p"], tb=True)
    d_ckvn = _mm("uk_dx", d_kn, res["w_uk"], tb=True)
    d_ckvn = _mm("uv_dx", dv, res["w_uv"], tb=True, add=d_ckvn)
    g["w_uq"] = _unpad_heads_uq(_mm("uq_dw", res["cqn"], d_q, ta=True))
    g["w_ukv"] = _join_ukv(_mm("uk_dw", res["ckvn"], d_kn, ta=True), _mm("uv_dw", res["ckvn"], dv, ta=True))
    pre_bwd = functools.partial(_vjp_of(_mla_pre_math, 4), n_prim=4)
    d_cq, d_ckv, g["q_a_norm_g"], g["kv_a_norm_g"] = _pcall(
        "mla_pre_bwd", pre_bwd, (T // BR,),
        [(proj, (BR, QL), row(O_CQ // QL)), (proj, (BR, KVL), row(O_CKV // KVL)),
         (w["q_a_norm_g"], (1, QL), fix), (w["kv_a_norm_g"], (1, KVL), fix),
         (d_cqn, (BR, QL), row(0)), (d_ckvn, (BR, KVL), row(0))],
        [((T, QL), BF16, (BR, QL), row(0)), ((T, KVL), BF16, (BR, KVL), row(0)),
         ((1, QL), F32, (1, QL), fix, True), ((1, KVL), F32, (1, KVL), fix, True)])

    gq2 = jnp.tile(w["dil_q_norm_g"].reshape(NG, HD), (1, 2))
    gk2 = jnp.tile(w["dil_k_norm_g"].reshape(NG, HD), (1, 2))
    d_dq, d_dk, d_dv, d_cz, dgq, dgk = _dilc_bwd(proj3, gq2, gk2, dil_tab, res["o_all"], res["l_all"],
                                                 d_yc.reshape(batch, S, DWID))
    g["dil_q_norm_g"] = dgq[:, :HD] + dgq[:, HD:]
    g["dil_k_norm_g"] = dgk[:, :HD] + dgk[:, HD:]
    d_dq, d_dk, d_dv = (t.reshape(T, NG * DWID) for t in (d_dq, d_dk, d_dv))
    d_cz = d_cz.reshape(T, DWID)

    dproj = jnp.concatenate(
        [t.reshape(T, CW) for t in (d_ab, d_ac, d_ax, d_az)]
        + [d_cq, d_ckv, d_kpe_p, d_bz, d_dq, d_dk, d_dv, d_cz, dg0, dg1, dg2], axis=1)
    d_h = _mm("in_proj_dx", dproj, w["w_in_t"], tm=1024)
    g["w_in_t"] = _mm("in_proj_dw", dproj, h, ta=True, tm=1024)
    dx, g["norm_g"] = _pcall("norm_bwd", _norm_bwd_math, (T // BR,),
                             [(x, (BR, D), row(0)), (w["norm_g"], (1, D), fix), (d_h, (BR, D), row(0)),
                              (dy, (BR, D), row(0))],
                             [((T, D), F32, (BR, D), row(0)), ((1, D), F32, (1, D), fix, True)])
    return dx, g


def _loss_math(y, t):
    e = y - t
    return e * (1.0 / D), 0.5 * jnp.sum(jnp.sum(e * e, axis=-1, keepdims=True) / D, axis=0, keepdims=True)


def _local_step(x, target, ws, batch):
    T = batch * S
    tabs = _rope_tables() + (_dil_slopes(),)
    saved = []
    y = x
    for l in range(NL):
        y, res = _layer_fwd(y, ws[l], tabs, batch)
        saved.append(res)
    row = lambda i: (i, 0)
    dy, loss = _pcall("loss", _loss_math, (T // BR,),
                      [(y, (BR, D), row), (target, (BR, D), row)],
                      [((T, D), F32, (BR, D), row), ((1, 1), F32, (1, 1), lambda i: (0, 0), True)])
    grads = [None] * NL
    for l in reversed(range(NL)):
        dy, grads[l] = _layer_bwd(dy, ws[l], saved[l], tabs, batch)
    return loss, dy, grads


ANY = pl.BlockSpec(memory_space=pl.ANY)
U32 = jnp.uint32
WSH = NIN // 4
WA = KPE_END
WB = WSH - WA
CWD = 512
PACK_ROWS = 1472
HW = PACK_W // 2


def _me():
    return lax.axis_index("x"), lax.axis_index("y"), lax.axis_index("c")


def _piece_rows(k):
    a = k * WSH + jnp.where(k > 0, NINP - NIN, 0)
    b = k * WSH + WA + (NINP - NIN)
    return ((0, pl.multiple_of(a, 8), WA), (WA, pl.multiple_of(b, 8), WB))


def _pack_words(lo, hi):
    ul = lax.bitcast_convert_type(lo.astype(BF16).astype(F32), U32)
    uh = lax.bitcast_convert_type(hi.astype(BF16).astype(F32), U32)
    w = jnp.bitwise_or(jnp.bitwise_and(uh, jnp.uint32(0xFFFF0000)), jnp.right_shift(ul, jnp.uint32(16)))
    return lax.bitcast_convert_type(w, F32)


def _unpack_words(w):
    w = lax.bitcast_convert_type(w, U32)
    lo = lax.bitcast_convert_type(jnp.left_shift(w, jnp.uint32(16)), F32)
    hi = lax.bitcast_convert_type(jnp.bitwise_and(w, jnp.uint32(0xFFFF0000)), F32)
    return lo, hi


def _all_gather(wc, sp):
    def body(w_ref, s_ref, ow_ref, os_ref, send_sems, recv_sems):
        x, y, c = _me()
        k_me = 2 * x + y
        sib = (x, y, 1 - c)
        chips = [(1 - x, y), (x, 1 - y), (1 - x, 1 - y)]
        wcols = lambda cc: pl.ds(pl.multiple_of(cc * (CWD // 2), LANE), CWD // 2)
        scols = lambda cc: pl.ds(pl.multiple_of(cc * HW, LANE), HW)

        def windows(k, cc):
            pcs = _piece_rows(k)
            return ([(w_ref.at[pl.ds(l0, n), wcols(cc)], ow_ref.at[pl.ds(p0, n), wcols(cc)]) for l0, p0, n in pcs]
                    + [(s_ref.at[:, scols(cc)], os_ref.at[k, :, scols(cc)])])

        def copy(i, src, dst, to):
            return pltpu.make_async_remote_copy(src_ref=src, dst_ref=dst, send_sem=send_sems.at[i],
                                                recv_sem=recv_sems.at[i], device_id=to, device_id_type=MESH)

        def own_windows():
            return ([(w_ref.at[pl.ds(l0, n)], ow_ref.at[pl.ds(p0, n)]) for l0, p0, n in _piece_rows(k_me)]
                    + [(s_ref, os_ref.at[k_me])])

        first = [copy(18 + i, src, dst, sib) for i, (src, dst) in enumerate(own_windows())]
        for j, (cx, cy) in enumerate(chips):
            for i, (src, dst) in enumerate(windows(k_me, c)):
                first.append(copy(3 * j + i, src, dst, (cx, cy, c)))
        for cp in first:
            cp.start()
        passed = []
        for j, (cx, cy) in enumerate(chips):
            for i, (_, dst) in enumerate(windows(2 * cx + cy, c)):
                copy(3 * j + i, dst, dst, (cx, cy, c)).wait_recv()
                cp = copy(9 + 3 * j + i, dst, dst, sib)
                cp.start()
                passed.append(cp)
        for j, (cx, cy) in enumerate(chips):
            for i, (_, dst) in enumerate(windows(2 * cx + cy, 1 - c)):
                copy(9 + 3 * j + i, dst, dst, sib).wait_recv()
        for i, (_, dst) in enumerate(own_windows()):
            copy(18 + i, dst, dst, sib).wait_recv()
        for cp in first + passed:
            cp.wait_send()

    return pl.pallas_call(
        body,
        out_shape=[jax.ShapeDtypeStruct((NINP, CWD), F32), jax.ShapeDtypeStruct((4, PACK_ROWS, PACK_W), BF16)],
        in_specs=[ANY, ANY], out_specs=[ANY, ANY],
        scratch_shapes=[pltpu.SemaphoreType.DMA((21,)), pltpu.SemaphoreType.DMA((21,))],
        name="weights_all_gather",
    )(wc, sp)


UNPACK_BR = 512


def _unpack_w_in(cont):
    def body(c_ref, o_ref):
        lo, hi = _unpack_words(c_ref[...])
        r = pl.program_id(0) * UNPACK_BR + lax.broadcasted_iota(jnp.int32, (UNPACK_BR, CWD), 0)
        pad = jnp.logical_and(r >= KPE_END, r < KPE_END + NINP - NIN)
        o_ref[:, 0:CWD] = jnp.where(pad, 0.0, lo).astype(BF16)
        o_ref[:, CWD:2 * CWD] = jnp.where(pad, 0.0, hi).astype(BF16)

    return pl.pallas_call(
        body, grid=(NINP // UNPACK_BR,),
        in_specs=[pl.BlockSpec((UNPACK_BR, CWD), lambda i: (i, 0))],
        out_specs=pl.BlockSpec((UNPACK_BR, D), lambda i: (i, 0)),
        out_shape=jax.ShapeDtypeStruct((NINP, D), BF16),
        name="w_in_unpack",
        compiler_params=pltpu.CompilerParams(dimension_semantics=("arbitrary",), vmem_limit_bytes=VMEM_LIMIT),
    )(cont)


def _rs_swap(gw, gs):
    def body(w_ref, s_ref, rw_ref, rs_ref, send_sems, recv_sems):
        x, y, c = _me()
        oc = 1 - c
        cps = [pltpu.make_async_remote_copy(src_ref=w_ref.at[:, pl.ds(pl.multiple_of(oc * (D // 2), LANE), D // 2)],
                                            dst_ref=rw_ref, send_sem=send_sems.at[0], recv_sem=recv_sems.at[0],
                                            device_id=(x, y, oc), device_id_type=MESH),
               pltpu.make_async_remote_copy(src_ref=s_ref.at[:, :, pl.ds(pl.multiple_of(oc * HW, LANE), HW)],
                                            dst_ref=rs_ref, send_sem=send_sems.at[1], recv_sem=recv_sems.at[1],
                                            device_id=(x, y, oc), device_id_type=MESH)]
        for cp in cps:
            cp.start()
        for cp in cps:
            cp.wait()

    return pl.pallas_call(
        body,
        out_shape=[jax.ShapeDtypeStruct((NINP, D // 2), F32), jax.ShapeDtypeStruct((4, PACK_ROWS, HW), F32)],
        in_specs=[ANY, ANY], out_specs=[ANY, ANY],
        scratch_shapes=[pltpu.SemaphoreType.DMA((2,)), pltpu.SemaphoreType.DMA((2,))],
        name="grads_sibling_swap",
    )(gw, gs)


SUM_BR = 512


def _rs_chip_sum_w(gw, rw, cidx):
    def body(c_ref, g_ref, r_ref, o_ref):
        s = g_ref[...] + r_ref[...]
        q = D // 8
        o_ref[...] = jnp.concatenate([_pack_words(s[:, 0:q], s[:, q:2 * q]),
                                      _pack_words(s[:, 2 * q:3 * q], s[:, 3 * q:4 * q])], axis=1)

    return pl.pallas_call(
        body,
        grid_spec=pltpu.PrefetchScalarGridSpec(
            num_scalar_prefetch=1, grid=(NINP // SUM_BR,),
            in_specs=[pl.BlockSpec((SUM_BR, D // 2), lambda i, cr: (i, cr[0])),
                      pl.BlockSpec((SUM_BR, D // 2), lambda i, cr: (i, 0))],
            out_specs=pl.BlockSpec((SUM_BR, D // 4), lambda i, cr: (i, 0))),
        out_shape=jax.ShapeDtypeStruct((NINP, D // 4), F32),
        name="grads_chip_sum_w",
        compiler_params=pltpu.CompilerParams(dimension_semantics=("arbitrary",), vmem_limit_bytes=VMEM_LIMIT),
    )(cidx, gw, rw)


def _rs_chip_sum_s(gs, rs, cidx):
    def body(c_ref, g_ref, r_ref, o_ref):
        o_ref[...] = (g_ref[...] + r_ref[...]).astype(BF16)

    return pl.pallas_call(
        body,
        grid_spec=pltpu.PrefetchScalarGridSpec(
            num_scalar_prefetch=1, grid=(4,),
            in_specs=[pl.BlockSpec((None, PACK_ROWS, HW), lambda j, cr: (j, 0, cr[0])),
                      pl.BlockSpec((None, PACK_ROWS, HW), lambda j, cr: (j, 0, 0))],
            out_specs=pl.BlockSpec((None, PACK_ROWS, HW), lambda j, cr: (j, 0, 0))),
        out_shape=jax.ShapeDtypeStruct((4, PACK_ROWS, HW), BF16),
        name="grads_chip_sum_s",
        compiler_params=pltpu.CompilerParams(dimension_semantics=("arbitrary",), vmem_limit_bytes=VMEM_LIMIT),
    )(cidx, gs, rs)


def _rs_exchange(sw, ss):
    def body(sw_ref, ss_ref, r2w_ref, r2s_ref, send_sems, recv_sems):
        x, y, c = _me()
        chips = [(1 - x, y), (x, 1 - y), (1 - x, 1 - y)]
        cps = []
        for j, (cx, cy) in enumerate(chips):
            k = 2 * cx + cy
            for i, (l0, p0, n) in enumerate(_piece_rows(k)):
                cps.append(pltpu.make_async_remote_copy(
                    src_ref=sw_ref.at[pl.ds(p0, n)], dst_ref=r2w_ref.at[j, pl.ds(l0, n)], send_sem=send_sems.at[3 * j + i],
                    recv_sem=recv_sems.at[3 * j + i], device_id=(cx, cy, c), device_id_type=MESH))
            cps.append(pltpu.make_async_remote_copy(
                src_ref=ss_ref.at[k], dst_ref=r2s_ref.at[j], send_sem=send_sems.at[3 * j + 2],
                recv_sem=recv_sems.at[3 * j + 2], device_id=(cx, cy, c), device_id_type=MESH))
        for cp in cps:
            cp.start()
        for cp in cps:
            cp.wait()

    return pl.pallas_call(
        body,
        out_shape=[jax.ShapeDtypeStruct((3, WSH, D // 4), F32), jax.ShapeDtypeStruct((3, PACK_ROWS, HW), BF16)],
        in_specs=[ANY] * 2, out_specs=[ANY] * 2,
        scratch_shapes=[pltpu.SemaphoreType.DMA((9,)), pltpu.SemaphoreType.DMA((9,))],
        name="grads_chip_exchange",
    )(sw, ss)


def _rs_final_w(gw, rw, r2w, idx):
    q = D // 8

    def body(i_ref, g_ref, r_ref, p_ref, o_ref, gbuf, rbuf, sems):
        i = pl.program_id(0)
        k, c = i_ref[0], i_ref[1]
        cps = []
        for n_, (l0, p0, n) in enumerate(_piece_rows(k)):
            gcol = pl.ds(pl.multiple_of(c * (D // 2) + i * 2 * q, LANE), 2 * q)
            rcol = pl.ds(pl.multiple_of(i * 2 * q, LANE), 2 * q)
            cps.append(pltpu.make_async_copy(g_ref.at[pl.ds(p0, n), gcol], gbuf.at[pl.ds(l0, n)], sems.at[2 * n_]))
            cps.append(pltpu.make_async_copy(r_ref.at[pl.ds(p0, n), rcol], rbuf.at[pl.ds(l0, n)], sems.at[2 * n_ + 1]))
        for cp in cps:
            cp.start()
        for cp in cps:
            cp.wait()
        acc = gbuf[...] + rbuf[...]
        for j in range(3):
            lo, hi = _unpack_words(p_ref[j])
            acc = acc + jnp.concatenate([lo, hi], axis=1)
        o_ref[...] = acc

    return pl.pallas_call(
        body,
        grid_spec=pltpu.PrefetchScalarGridSpec(
            num_scalar_prefetch=1, grid=(2,),
            in_specs=[ANY, ANY, pl.BlockSpec((3, WSH, q), lambda i, ir: (0, 0, i))],
            out_specs=pl.BlockSpec((WSH, 2 * q), lambda i, ir: (0, i)),
            scratch_shapes=[pltpu.VMEM((WSH, 2 * q), F32), pltpu.VMEM((WSH, 2 * q), F32), pltpu.SemaphoreType.DMA((4,))]),
        out_shape=jax.ShapeDtypeStruct((WSH, D // 2), F32),
        name="grads_final_sum_w",
        compiler_params=pltpu.CompilerParams(dimension_semantics=("arbitrary",), vmem_limit_bytes=VMEM_LIMIT),
    )(idx, gw, rw, r2w)


def _rs_final_s(gs, rs, r2s, idx):
    def body(i_ref, g_ref, r_ref, p_ref, o_ref):
        acc = g_ref[...] + r_ref[...]
        for j in range(3):
            acc = acc + p_ref[j].astype(F32)
        o_ref[...] = acc

    return pl.pallas_call(
        body,
        grid_spec=pltpu.PrefetchScalarGridSpec(
            num_scalar_prefetch=1, grid=(1,),
            in_specs=[pl.BlockSpec((None, PACK_ROWS, HW), lambda i, ir: (ir[0], 0, ir[1])),
                      pl.BlockSpec((None, PACK_ROWS, HW), lambda i, ir: (ir[0], 0, 0)),
                      pl.BlockSpec((3, PACK_ROWS, HW), lambda i, ir: (0, 0, 0))],
            out_specs=pl.BlockSpec((PACK_ROWS, HW), lambda i, ir: (0, 0))),
        out_shape=jax.ShapeDtypeStruct((PACK_ROWS, HW), F32),
        name="grads_final_sum_s",
        compiler_params=pltpu.CompilerParams(dimension_semantics=("arbitrary",), vmem_limit_bytes=VMEM_LIMIT),
    )(idx, gs, rs, r2s)


def _rs_share(fw, fs):
    def body(w_ref, s_ref, ow_ref, os_ref, send_sems, recv_sems):
        x, y, c = _me()
        cps = [pltpu.make_async_remote_copy(src_ref=w_ref, dst_ref=ow_ref, send_sem=send_sems.at[0],
                                            recv_sem=recv_sems.at[0], device_id=(x, y, 1 - c), device_id_type=MESH),
               pltpu.make_async_remote_copy(src_ref=s_ref, dst_ref=os_ref, send_sem=send_sems.at[1],
                                            recv_sem=recv_sems.at[1], device_id=(x, y, 1 - c), device_id_type=MESH)]
        for cp in cps:
            cp.start()
        for cp in cps:
            cp.wait()

    return pl.pallas_call(
        body,
        out_shape=[jax.ShapeDtypeStruct((WSH, D // 2), F32), jax.ShapeDtypeStruct((PACK_ROWS, HW), F32)],
        in_specs=[ANY, ANY], out_specs=[ANY, ANY],
        scratch_shapes=[pltpu.SemaphoreType.DMA((2,)), pltpu.SemaphoreType.DMA((2,))],
        name="grads_share",
    )(fw, fs)


def _both_halves(mine, other, c):
    return jnp.where(c == 0, jnp.concatenate([mine, other], axis=1), jnp.concatenate([other, mine], axis=1))


def _reduce_scatter(gw, gs):
    x, y, c = _me()
    cidx = jnp.reshape(c, (1,)).astype(jnp.int32)
    idx = jnp.stack([2 * x + y, c]).astype(jnp.int32)
    rw, rs = _rs_swap(gw, gs)
    sw = _rs_chip_sum_w(gw, rw, cidx)
    ss = _rs_chip_sum_s(gs, rs, cidx)
    r2w, r2s = _rs_exchange(sw, ss)
    fw = _rs_final_w(gw, rw, r2w, idx)
    fs = _rs_final_s(gs, rs, r2s, idx)
    ow, os_ = _rs_share(fw, fs)
    return _both_halves(fw, ow, c), _both_halves(fs, os_, c)


def _all_reduce_small(gs):
    rows = gs.shape[0]

    def body(g_ref, o_ref, buf, send_sems, recv_sems):
        x, y, c = _me()
        me = 4 * x + 2 * y + c
        buf[me] = g_ref[...]
        cps = []
        for r in range(1, 8):
            fx, fy, fc = (r >> 2) & 1, (r >> 1) & 1, r & 1
            px, py, pc = jnp.bitwise_xor(x, fx), jnp.bitwise_xor(y, fy), jnp.bitwise_xor(c, fc)
            cps.append((pltpu.make_async_remote_copy(
                src_ref=g_ref, dst_ref=buf.at[me], send_sem=send_sems.at[r - 1], recv_sem=recv_sems.at[r - 1],
                device_id=(px, py, pc), device_id_type=MESH), 4 * px + 2 * py + pc))
        for cp, _ in cps:
            cp.start()
        for r, (cp, peer) in enumerate(cps):
            pltpu.make_async_remote_copy(
                src_ref=g_ref, dst_ref=buf.at[peer], send_sem=send_sems.at[r], recv_sem=recv_sems.at[r],
                device_id=(x, y, c), device_id_type=MESH).wait_recv()
        for cp, _ in cps:
            cp.wait_send()
        acc = buf[0]
        for k in range(1, 8):
            acc = acc + buf[k]
        o_ref[...] = acc

    return pl.pallas_call(
        body,
        out_shape=jax.ShapeDtypeStruct((rows, LANE), F32),
        in_specs=[pl.BlockSpec(memory_space=pltpu.VMEM)],
        out_specs=pl.BlockSpec(memory_space=pltpu.VMEM),
        scratch_shapes=[pltpu.VMEM((8, rows, LANE), F32), pltpu.SemaphoreType.DMA((7,)), pltpu.SemaphoreType.DMA((7,))],
        name="small_grads_all_reduce",
    )(gs)


PACK_SPLIT = (("w_uq", 96, (QL, 192)), ("w_ukv", 64, (KVL, 256)),
              ("w_out_a", 256, (CW, 256)), ("w_out_b", 256, (CW, 256)), ("w_out_c", 256, (CW, 256)),
              ("w_o", 512, (256, D)))
MAT_ROWS = 1440
CONV_SHARD = 3 * 128


def _w_in_words(w_in_shard):
    t = w_in_shard.T
    return _pack_words(t[:, :CWD], t[:, CWD:])


def _pack_weights(wl):
    parts = [wl[n].astype(BF16).reshape(-1, PACK_W) for n, _, _ in PACK_SPLIT]
    cw = wl["conv_w"].reshape(-1)
    hi = cw.astype(BF16)
    r1 = cw - hi.astype(F32)
    mid = r1.astype(BF16)
    lo = (r1 - mid.astype(F32)).astype(BF16)
    cterms = jnp.pad(jnp.concatenate([hi, mid, lo]), (0, 3 * PACK_W - 3 * CONV_SHARD)).reshape(3, PACK_W)
    pad = jnp.zeros((PACK_ROWS - MAT_ROWS - 3, PACK_W), BF16)
    return jnp.concatenate(parts + [cterms, pad], axis=0)


def _unpack_weights(gath):
    out = {}
    r = 0
    for n, nrows, shp in PACK_SPLIT:
        t = gath[:, r:r + nrows].reshape((4,) + shp)
        r += nrows
        if n == "w_o":
            out[n] = t.reshape(4 * shp[0], shp[1])
        else:
            out[n] = t.transpose(1, 0, 2).reshape(shp[0], 4 * shp[1])
    ct = gath[:, r:r + 3].reshape(4, 3 * PACK_W)[:, :3 * CONV_SHARD].astype(F32).reshape(4, 3, CONV_SHARD)
    cw = (ct[:, 0] + ct[:, 1]) + ct[:, 2]
    out["conv_w"] = cw.reshape(4, 3, 128).transpose(1, 0, 2).reshape(3, CW)
    return out


def _pack_grads(g):
    parts = []
    for n, nrows, shp in PACK_SPLIT:
        t = g[n]
        if n == "w_o":
            t = t.reshape((4,) + shp)
        else:
            t = t.reshape(shp[0], 4, shp[1]).transpose(1, 0, 2)
        parts.append(t.reshape(4, nrows, PACK_W))
    cw = g["conv_w"].reshape(3, 4, 128).transpose(1, 0, 2).reshape(4, 1, CONV_SHARD)
    parts.append(jnp.pad(cw, ((0, 0), (0, 0), (0, PACK_W - CONV_SHARD))))
    parts.append(jnp.zeros((4, PACK_ROWS - MAT_ROWS - 1, PACK_W), F32))
    return jnp.concatenate(parts, axis=1)


def _unpack_grads(red):
    out = {}
    r = 0
    for n, nrows, shp in PACK_SPLIT:
        out[n] = red[r:r + nrows].reshape(shp)
        r += nrows
    out["conv_w"] = red[r, :CONV_SHARD].reshape(3, 128)
    return out


SMALL_SIZES = (("norm_g", D), ("b_gate", 3 * D), ("conv_b", CW), ("q_a_norm_g", QL), ("kv_a_norm_g", KVL),
               ("mla_q_norm_g", QK), ("mla_k_norm_g", QK), ("dil_q_norm_g", NG * HD), ("dil_k_norm_g", NG * HD))
SMALL_ROWS = 88


def _pack_small(per_name):
    flat = jnp.concatenate([per_name[n].reshape(-1).astype(F32) for n, _ in SMALL_SIZES])
    return jnp.pad(flat, (0, SMALL_ROWS * LANE - flat.shape[0])).reshape(SMALL_ROWS, LANE)


def _unpack_small(packed, like):
    out = {}
    flat = packed.reshape(-1)
    r = 0
    for n, sz in SMALL_SIZES:
        out[n] = flat[r:r + NL * sz].reshape(like[n].shape)
        r += NL * sz
    return out


def _adamw_math(w, g, m, v):
    m = ADAM_B1 * m + (1.0 - ADAM_B1) * g
    v = ADAM_B2 * v + (1.0 - ADAM_B2) * jnp.square(g)
    m_hat = m / (1.0 - ADAM_B1 ** ADAM_STEP)
    v_hat = v / (1.0 - ADAM_B2 ** ADAM_STEP)
    delta = -ADAM_LR * (m_hat / (jnp.sqrt(v_hat) + ADAM_EPS) + ADAM_WD * w)
    return delta, m, v


def _adamw(name, w, g, m, v, br, bc=None):
    L, R, C = w.shape
    bc = C if bc is None else bc
    blk = lambda l, i, j: (l, i, j)
    return _pcall(name, _adamw_math, (L, R // br, C // bc), [(t, (None, br, bc), blk) for t in (w, g, m, v)],
                  [((L, R, C), F32, (None, br, bc), blk)] * 3)


ADAM_ROWS = {"w_uq": 256, "w_ukv": 128, "w_out_a": 512, "w_out_b": 512, "w_out_c": 512, "w_o": 256,
             "conv_w": 3}


def kernel(x, norm_g, w_in, b_gate, conv_w, conv_b, q_a_norm_g, w_uq, kv_a_norm_g, w_ukv, mla_q_norm_g, mla_k_norm_g, dil_q_norm_g, dil_k_norm_g, w_out_a, w_out_b, w_out_c, w_o, loss_target, m_norm_g, m_w_in, m_b_gate, m_conv_w, m_conv_b, m_q_a_norm_g, m_w_uq, m_kv_a_norm_g, m_w_ukv, m_mla_q_norm_g, m_mla_k_norm_g, m_dil_q_norm_g, m_dil_k_norm_g, m_w_out_a, m_w_out_b, m_w_out_c, m_w_o, v_norm_g, v_w_in, v_b_gate, v_conv_w, v_conv_b, v_q_a_norm_g, v_w_uq, v_kv_a_norm_g, v_w_ukv, v_mla_q_norm_g, v_mla_k_norm_g, v_dil_q_norm_g, v_dil_k_norm_g, v_w_out_a, v_w_out_b, v_w_out_c, v_w_o):
    W = dict(norm_g=norm_g, w_in=w_in, b_gate=b_gate, conv_w=conv_w, conv_b=conv_b, q_a_norm_g=q_a_norm_g, w_uq=w_uq,
             kv_a_norm_g=kv_a_norm_g, w_ukv=w_ukv, mla_q_norm_g=mla_q_norm_g, mla_k_norm_g=mla_k_norm_g,
             dil_q_norm_g=dil_q_norm_g, dil_k_norm_g=dil_k_norm_g, w_out_a=w_out_a, w_out_b=w_out_b, w_out_c=w_out_c,
             w_o=w_o)
    M = dict(norm_g=m_norm_g, w_in=m_w_in, b_gate=m_b_gate, conv_w=m_conv_w, conv_b=m_conv_b, q_a_norm_g=m_q_a_norm_g,
             w_uq=m_w_uq, kv_a_norm_g=m_kv_a_norm_g, w_ukv=m_w_ukv, mla_q_norm_g=m_mla_q_norm_g,
             mla_k_norm_g=m_mla_k_norm_g, dil_q_norm_g=m_dil_q_norm_g, dil_k_norm_g=m_dil_k_norm_g, w_out_a=m_w_out_a,
             w_out_b=m_w_out_b, w_out_c=m_w_out_c, w_o=m_w_o)
    V = dict(norm_g=v_norm_g, w_in=v_w_in, b_gate=v_b_gate, conv_w=v_conv_w, conv_b=v_conv_b, q_a_norm_g=v_q_a_norm_g,
             w_uq=v_w_uq, kv_a_norm_g=v_kv_a_norm_g, w_ukv=v_w_ukv, mla_q_norm_g=v_mla_q_norm_g,
             mla_k_norm_g=v_mla_k_norm_g, dil_q_norm_g=v_dil_q_norm_g, dil_k_norm_g=v_dil_k_norm_g, w_out_a=v_w_out_a,
             w_out_b=v_w_out_b, w_out_c=v_w_out_c, w_o=v_w_o)
    batch = x.shape[0]
    T = batch * S

    ws = []
    for l in range(NL):
        cont, gath = _all_gather(_w_in_words(w_in[l]), _pack_weights({n: W[n][l] for n in BIG[1:] + ("conv_w",)}))
        full = _unpack_weights(gath)
        pad_qk = lambda t: jnp.pad(t, (0, QKP - QK)).reshape(1, QKP)
        full.update(
            w_in_t=_unpack_w_in(cont),
            norm_g=norm_g[l].reshape(1, D), b_gate=b_gate[l].reshape(1, 3 * D), conv_b=conv_b[l].reshape(1, CW),
            q_a_norm_g=q_a_norm_g[l].reshape(1, QL), kv_a_norm_g=kv_a_norm_g[l].reshape(1, KVL),
            mla_q_norm_g=pad_qk(mla_q_norm_g[l]), mla_k_norm_g=pad_qk(mla_k_norm_g[l]),
            dil_q_norm_g=dil_q_norm_g[l].reshape(NG, 1, HD), dil_k_norm_g=dil_k_norm_g[l].reshape(NG, 1, HD))
        ws.append(full)

    loss, dx, grads = _local_step(x.reshape(T, D), loss_target.reshape(T, D), ws, batch)
    loss = lax.psum(loss[0, 0], ("x", "y", "c"))
    grad_x = dx.reshape(batch, S, D)

    red = []
    for l in range(NL):
        rw, rs = _reduce_scatter(grads[l]["w_in_t"], _pack_grads(grads[l]))
        r = _unpack_grads(rs)
        r["w_in_t"] = rw
        red.append(r)
    G = {n: jnp.stack([red[l][n] for l in range(NL)]) for n in BIG[1:] + ("conv_w",)}
    g_in_t = jnp.stack([red[l]["w_in_t"] for l in range(NL)])
    G["w_in"] = jnp.swapaxes(g_in_t, 1, 2)
    small_g = {n: jnp.stack([grads[l][n].reshape(-1)[:sz] for l in range(NL)]) for n, sz in SMALL_SIZES}
    small_red = _all_reduce_small(_pack_small(small_g))
    G.update(_unpack_small(small_red, {n: W[n] for n in SMALL}))

    delta, new_m, new_v = {}, {}, {}
    for n in BIG[1:] + ("conv_w",):
        delta[n], new_m[n], new_v[n] = _adamw("adamw_" + n, W[n], G[n], M[n], V[n], ADAM_ROWS[n])
    tr = lambda t: jnp.swapaxes(t, 1, 2)
    delta["w_in"], new_m["w_in"], new_v["w_in"] = (
        tr(t) for t in _adamw("adamw_w_in", tr(w_in), g_in_t, tr(m_w_in), tr(v_w_in), WSH, LANE))
    sw, sm, sv = (_pack_small({n: t[n] for n in SMALL})[None] for t in (W, M, V))
    sd, snm, snv = _adamw("adamw_small", sw, small_red[None], sm, sv, SMALL_ROWS)
    like = {n: W[n] for n in SMALL}
    delta.update(_unpack_small(sd[0], like))
    new_m.update(_unpack_small(snm[0], like))
    new_v.update(_unpack_small(snv[0], like))

    return (loss, grad_x, *[G[n] for n in WEIGHTS], *[delta[n] for n in WEIGHTS],
            *[new_m[n] for n in WEIGHTS], *[new_v[n] for n in WEIGHTS])
```

```python
import functools

import numpy as np
import jax
import jax.numpy as jnp
from jax import lax
from jax.experimental import pallas as pl
from jax.experimental.pallas import tpu as pltpu

F32 = jnp.float32
BF16 = jnp.bfloat16

D = 1024
S = 2048
NL = 2
CW = 512
NH = 8
QL = 256
KVL = 128
NOPE = 64
ROPE = 32
VD = 64
QK = NOPE + ROPE
QKP = 128
ROPE_THETA = 10000.0
DIL = ((128, 1), (512, 4), (2048, 16))
NG = 3
DH = 8
HD = 64
DWID = DH * HD
QB = 128
EPS = 1e-6
NIN = 11168
NINP = 11264
O_A, O_CQ, O_CKV, O_KPE, O_BZ, O_DQ, O_DK, O_DV, O_CZ, O_G = 0, 2048, 2304, 2432, 2560, 3072, 4608, 6144, 7680, 8192
KPE_END = 2464
NEG = -1e30
MLA_SCALE = QK ** -0.5
DIL_SCALE = HD ** -0.5
LANE = 128
PACK_W = 512
VMEM_LIMIT = 48 * 1024 * 1024

ADAM_LR = 0.001
ADAM_B1 = 0.9
ADAM_B2 = 0.999
ADAM_EPS = 1e-08
ADAM_WD = 0.01
ADAM_STEP = 10

MESH = pl.DeviceIdType.MESH
BIG = ("w_in", "w_uq", "w_ukv", "w_out_a", "w_out_b", "w_out_c", "w_o")
SMALL = ("norm_g", "b_gate", "conv_b", "q_a_norm_g", "kv_a_norm_g", "mla_q_norm_g", "mla_k_norm_g",
         "dil_q_norm_g", "dil_k_norm_g")
WEIGHTS = ("norm_g", "w_in", "b_gate", "conv_w", "conv_b", "q_a_norm_g", "w_uq", "kv_a_norm_g", "w_ukv",
           "mla_q_norm_g", "mla_k_norm_g", "dil_q_norm_g", "dil_k_norm_g", "w_out_a", "w_out_b", "w_out_c", "w_o")


def _dot(a, b):
    return jnp.dot(a, b, preferred_element_type=F32)


def _dot_nt(a, b):
    return lax.dot_general(a, b, (((1,), (1,)), ((), ())), preferred_element_type=F32)


def _dot_tn(a, b):
    return lax.dot_general(a, b, (((0,), (0,)), ((), ())), preferred_element_type=F32)


def _pcall(name, fn, grid, ins, outs):
    n_in = len(ins)
    n_out = len(outs)
    acc_axis = len(grid) - 1
    is_acc = [len(o) > 4 and o[4] for o in outs]
    outs = [o[:4] for o in outs]

    def body(*refs):
        vals = fn(*[r[...].astype(F32) for r in refs[:n_in]])
        if not isinstance(vals, (tuple, list)):
            vals = (vals,)
        for k in range(n_out):
            r = refs[n_in + k]
            v = vals[k].astype(r.dtype).reshape(r.shape)
            if is_acc[k]:
                first = pl.program_id(acc_axis) == 0

                @pl.when(first)
                def _():
                    r[...] = v

                @pl.when(jnp.logical_not(first))
                def _():
                    r[...] += v
            else:
                r[...] = v

    return pl.pallas_call(
        body,
        grid=grid,
        in_specs=[pl.BlockSpec(bs, im) for _, bs, im in ins],
        out_specs=[pl.BlockSpec(bs, im) for _, _, bs, im in outs],
        out_shape=[jax.ShapeDtypeStruct(sh, dt) for sh, dt, _, _ in outs],
        name=name,
        compiler_params=pltpu.CompilerParams(
            dimension_semantics=("arbitrary",) * len(grid), vmem_limit_bytes=VMEM_LIMIT),
    )(*[a for a, _, _ in ins])


def _mm(name, a, b, *, ta=False, tb=False, out_dtype=F32, add=None, tm=512, tn=1024, tk=1024):
    if ta:
        K, M = a.shape
    else:
        M, K = a.shape
    if tb:
        N, K2 = b.shape
    else:
        K2, N = b.shape
    assert K == K2, (name, a.shape, b.shape)
    tm, tn, tk = min(tm, M), min(tn, N), min(tk, K)
    assert M % tm == 0 and N % tn == 0 and K % tk == 0, (name, M, N, K)
    nk = K // tk
    dims = (((0 if ta else 1,), (1 if tb else 0,)), ((), ()))
    a_spec = pl.BlockSpec((tk, tm), lambda j, i, k: (k, i)) if ta else pl.BlockSpec((tm, tk), lambda j, i, k: (i, k))
    b_spec = pl.BlockSpec((tn, tk), lambda j, i, k: (j, k)) if tb else pl.BlockSpec((tk, tn), lambda j, i, k: (k, j))
    o_spec = pl.BlockSpec((tm, tn), lambda j, i, k: (i, j))
    has_add = add is not None

    def body(*refs):
        a_ref, b_ref = refs[0], refs[1]
        add_ref = refs[2] if has_add else None
        o_ref = refs[3] if has_add else refs[2]
        p = lax.dot_general(a_ref[...].astype(BF16), b_ref[...].astype(BF16), dims, preferred_element_type=F32)
        if nk == 1:
            if has_add:
                p = p + add_ref[...]
            o_ref[...] = p.astype(out_dtype)
        else:
            acc = refs[-1]
            k = pl.program_id(2)

            @pl.when(k == 0)
            def _():
                acc[...] = p

            @pl.when(k > 0)
            def _():
                acc[...] += p

            @pl.when(k == nk - 1)
            def _():
                r = acc[...]
                if has_add:
                    r = r + add_ref[...]
                o_ref[...] = r.astype(out_dtype)

    in_specs = [a_spec, b_spec] + ([o_spec] if has_add else [])
    args = [a, b] + ([add] if has_add else [])
    return pl.pallas_call(
        body,
        grid=(N // tn, M // tm, nk),
        in_specs=in_specs,
        out_specs=o_spec,
        out_shape=jax.ShapeDtypeStruct((M, N), out_dtype),
        scratch_shapes=[pltpu.VMEM((tm, tn), F32)] if nk > 1 else [],
        name=name,
        compiler_params=pltpu.CompilerParams(
            dimension_semantics=("arbitrary", "arbitrary", "arbitrary"), vmem_limit_bytes=VMEM_LIMIT),
    )(*args)


def _vjp_of(f, n_diff):
    def g(*args, n_prim):
        prim = args[:n_diff]
        consts = args[n_diff:n_prim]
        cts = args[n_prim:]
        _, pull = jax.vjp(lambda *p: f(*p, *consts), *prim)
        out = jax.eval_shape(lambda *p: f(*p, *consts), *prim)
        if isinstance(out, (tuple, list)):
            cts = tuple(c.astype(o.dtype) for c, o in zip(cts, out))
        else:
            cts = cts[0].astype(out.dtype)
        return pull(cts)
    return g


def _rms(x, g, n=None):
    n = x.shape[-1] if n is None else n
    ms = jnp.sum(x * x, axis=-1, keepdims=True) / n
    return x * lax.rsqrt(ms + EPS) * g


def _silu(z):
    return z * jax.nn.sigmoid(z)


def _roll_rows(u, k):
    n = u.shape[0]
    r = pltpu.roll(u, k % n, 0)
    t = lax.broadcasted_iota(jnp.int32, u.shape, 0)
    if k > 0:
        return jnp.where(t >= k, r, 0.0)
    return jnp.where(t < n + k, r, 0.0)


@functools.partial(jax.custom_vjp, nondiff_argnums=(1,))
def _shift(u, k):
    return _roll_rows(u, k)


def _shift_fwd(u, k):
    return _roll_rows(u, k), None


def _shift_bwd(k, _, g):
    return (_roll_rows(g, -k),)


_shift.defvjp(_shift_fwd, _shift_bwd)


@functools.partial(jax.custom_vjp, nondiff_argnums=(1,))
def _lane_roll(u, k):
    return pltpu.roll(u, k % LANE, 1)


def _lane_roll_fwd(u, k):
    return pltpu.roll(u, k % LANE, 1), None


def _lane_roll_bwd(k, _, g):
    return (pltpu.roll(g, (-k) % LANE, 1),)


_lane_roll.defvjp(_lane_roll_fwd, _lane_roll_bwd)


def _conv_math(ab, ac, ax, az, cw, cb):
    u = ac * ax
    conv = cb + _shift(u, 2) * cw[0:1] + _shift(u, 1) * cw[1:2] + u * cw[2:3]
    return ab * conv * _silu(az)


def _mla_pre_math(cq, ckv, gq, gkv):
    return _rms(cq, gq), _rms(ckv, gkv)


def _rope_math(q, kn, kpe, gq, gk, c, s1, s2):
    lane = lax.broadcasted_iota(jnp.int32, kpe.shape, 1)
    pe = _lane_roll(jnp.where(lane < ROPE, kpe, 0.0), NOPE)

    def one(t, g):
        tn = _rms(t, g, QK)
        return tn * c + _lane_roll(tn, -16) * s1 + _lane_roll(tn, 16) * s2

    qs, ks = [], []
    for h in range(NH):
        sl = slice(h * QKP, (h + 1) * QKP)
        qs.append(one(q[:, sl], gq))
        ks.append(one(kn[:, sl] + pe, gk))
    return jnp.concatenate(qs, axis=1), jnp.concatenate(ks, axis=1)


def _gate_math(o, z):
    return o * _silu(z)


def _mergec_math(o0, o1, o2, l0, l1, l2, cz):
    m = lax.stop_gradient(jnp.maximum(jnp.maximum(l0, l1), l2))
    e0, e1, e2 = jnp.exp(l0 - m), jnp.exp(l1 - m), jnp.exp(l2 - m)
    den = e0 + e1 + e2
    oc = (e0 / den) * o0 + (e1 / den) * o1 + (e2 / den) * o2
    return oc * _silu(cz)


def _merge_math(g0, g1, g2, b0, b1, b2, pa, pb, pc):
    return (jax.nn.sigmoid(g0 + b0) * pa + jax.nn.sigmoid(g1 + b1) * pb) + jax.nn.sigmoid(g2 + b2) * pc


MLA_T = 256
MLA_UNROLL = True


def _mla_fwd(q, k, v):
    B = q.shape[0]
    T = MLA_T
    NB = S // T

    def body(q_ref, k_ref, v_ref, o_ref, l_ref):
        row = lax.broadcasted_iota(jnp.int32, (T, T), 0)
        col = lax.broadcasted_iota(jnp.int32, (T, T), 1)
        lo = _lo_mask((T, LANE))

        for qi in range(NB):
            qb = q_ref[qi * T:(qi + 1) * T, :]

            def step(j, carry, diagonal):
                m, l, acc = carry
                off = pl.multiple_of(j * T, T)
                kb = k_ref[pl.ds(off, T), :]
                vb = v_ref[pl.ds(off, T), :]
                ss = []
                for e in (0, 1):
                    se = _dot_nt(qb[:, e * QKP:(e + 1) * QKP], kb[:, e * QKP:(e + 1) * QKP]) * MLA_SCALE
                    ss.append(jnp.where(col <= row, se, NEG) if diagonal else se)
                s = jnp.concatenate(ss, axis=0)
                m_new = jnp.maximum(m, jnp.max(s, axis=-1, keepdims=True))
                a = jnp.exp(m - m_new)
                p = jnp.exp(s - m_new)
                l = a * l + jnp.sum(p, axis=-1, keepdims=True)
                acc = a * acc + _dot(p.astype(BF16), vb)
                return m_new, l, acc

            init = (jnp.full((2 * T, 1), NEG, F32), jnp.zeros((2 * T, 1), F32), jnp.zeros((2 * T, LANE), F32))
            carry = lax.fori_loop(0, qi, functools.partial(step, diagonal=False), init, unroll=MLA_UNROLL)
            m, l, acc = step(qi, carry, True)
            o = acc / l
            lse = m + jnp.log(l)
            o_ref[qi * T:(qi + 1) * T, :] = jnp.where(lo, o[:T], o[T:])
            l_ref[qi * T:(qi + 1) * T, :] = jnp.where(lo, lse[:T], lse[T:])

    def spec(w):
        return pl.BlockSpec((None, S, w), lambda b, hp: (b, 0, hp))

    return pl.pallas_call(
        body,
        grid=(B, NH // 2),
        in_specs=[spec(2 * QKP), spec(2 * QKP), spec(LANE)],
        out_specs=[spec(LANE), spec(LANE)],
        out_shape=[jax.ShapeDtypeStruct((B, S, NH * VD), F32)] * 2,
        name="mla_attn_fwd",
        compiler_params=pltpu.CompilerParams(dimension_semantics=("arbitrary",) * 2, vmem_limit_bytes=VMEM_LIMIT),
    )(q, k, v)


def _mla_bwd(q, k, v, do, o, lse):
    B = q.shape[0]
    T = MLA_T
    NB = S // T

    def body(q_ref, k_ref, v_ref, do_ref, o_ref, l_ref, dq_ref, dk_ref, dv_ref, delta_ref):
        delta_ref[...] = _head_sum(do_ref[...] * o_ref[...])
        row = lax.broadcasted_iota(jnp.int32, (T, T), 0)
        col = lax.broadcasted_iota(jnp.int32, (T, T), 1)
        lo = _lo_mask((T, LANE))

        for j in range(NB):
            krows = slice(j * T, (j + 1) * T)
            kb = k_ref[krows, :]
            vb = v_ref[krows, :]
            dk = [jnp.zeros((T, QKP), F32), jnp.zeros((T, QKP), F32)]
            dv = jnp.zeros((T, LANE), F32)
            for i in range(j, NB):
                qrows = slice(i * T, (i + 1) * T)
                qb = q_ref[qrows, :]
                do2 = _stack_heads(do_ref[qrows, :], lo).astype(BF16)
                lb = l_ref[qrows, :]
                db = delta_ref[qrows, :]
                dp2 = _dot_nt(do2, vb)
                for e in (0, 1):
                    cols = slice(e * QKP, (e + 1) * QKP)
                    qe, ke = qb[:, cols], kb[:, cols]
                    s = _dot_nt(qe, ke) * MLA_SCALE
                    if i == j:
                        s = jnp.where(col <= row, s, NEG)
                    p = jnp.exp(s - lb[:, e * HD:e * HD + 1])
                    dv = dv + _dot_tn(p.astype(BF16), do2[e * T:(e + 1) * T])
                    ds = (p * (dp2[e * T:(e + 1) * T] - db[:, e * HD:e * HD + 1]) * MLA_SCALE).astype(BF16)
                    dk[e] = dk[e] + _dot_tn(ds, qe)
                    if j == 0:
                        dq_ref[qrows, cols] = _dot(ds, ke)
                    else:
                        dq_ref[qrows, cols] += _dot(ds, ke)
            dk_ref[krows, 0:QKP] = dk[0]
            dk_ref[krows, QKP:2 * QKP] = dk[1]
            dv_ref[krows, :] = dv

    def spec(w):
        return pl.BlockSpec((None, S, w), lambda b, hp: (b, 0, hp))

    return pl.pallas_call(
        body,
        grid=(B, NH // 2),
        in_specs=[spec(2 * QKP), spec(2 * QKP), spec(LANE), spec(LANE), spec(LANE), spec(LANE)],
        out_specs=[spec(2 * QKP), spec(2 * QKP), spec(LANE)],
        out_shape=[jax.ShapeDtypeStruct((B, S, NH * QKP), F32), jax.ShapeDtypeStruct((B, S, NH * QKP), F32),
                   jax.ShapeDtypeStruct((B, S, NH * VD), F32)],
        scratch_shapes=[pltpu.VMEM((S, LANE), F32)],
        name="mla_attn_bwd",
        compiler_params=pltpu.CompilerParams(dimension_semantics=("arbitrary",) * 2, vmem_limit_bytes=VMEM_LIMIT),
    )(q, k, v, do, o, lse)


def _lo_mask(shape):
    return lax.broadcasted_iota(jnp.int32, shape, len(shape) - 1) < HD


def _head_sum(u):
    r = lax.broadcasted_iota(jnp.int32, (LANE, LANE), 0) < HD
    c = lax.broadcasted_iota(jnp.int32, (LANE, LANE), 1) < HD
    ones = jnp.where(r == c, 1.0, 0.0).astype(BF16)
    hi = u.astype(BF16)
    lo = (u - hi.astype(F32)).astype(BF16)
    return _dot(hi, ones) + _dot(lo, ones)


def _rms2(x, g):
    return x * lax.rsqrt(_head_sum(x * x) / HD + EPS) * g


def _dil_bias(t_ref, gi, d):
    qq = lax.broadcasted_iota(jnp.int32, (QB, QB), 0)
    kk = lax.broadcasted_iota(jnp.int32, (QB, QB), 1)
    jc = (qq - kk).astype(F32)
    rows = []
    for e in (0, 1):
        sl = t_ref[2 * gi + e:2 * gi + e + 1, :] * float(d)
        bp = jnp.where(kk >= qq, -sl * (jc + float(QB)), NEG)
        bc = jnp.where(kk <= qq, -sl * jc, NEG)
        rows.append(jnp.concatenate([bp, bc], axis=1))
    return jnp.concatenate(rows, axis=0)


def _dil_rows(cur, d):
    return pl.ds(cur, QB, stride=d) if d > 1 else pl.ds(pl.multiple_of(cur, QB), QB)


def _dil_walk(d, block, full):
    if d == 1:
        block(0, None)

        def body(i, c):
            block(i * QB, (i - 1) * QB)
            return c
        lax.fori_loop(1, S // QB, body, 0, unroll=True if full else 5)
    elif d == 16:
        def body(r, c):
            block(r, None)
            return c
        lax.fori_loop(0, d, body, 0, unroll=True if full else 4)
    else:
        nb = S // d // QB

        def cls(r, c):
            block(r, None)

            def body(i, c2):
                block(r + i * QB * d, r + (i - 1) * QB * d)
                return c2
            lax.fori_loop(1, nb, body, 0, unroll=True)
            return c
        lax.fori_loop(0, d, cls, 0, unroll=full)


def _stack_heads(x, lo):
    return jnp.concatenate([jnp.where(lo, x, 0.0), jnp.where(lo, 0.0, x)], axis=0)


def _dilc_fwd(proj3, gq, gk, tab):
    B = proj3.shape[0]

    def body(q_ref, k_ref, v_ref, cz_ref, gq_ref, gk_ref, t_ref, y_ref, o_ref, l_ref, qs, ks, vs):
        g = pl.program_id(2)
        lo = _lo_mask((QB, LANE))

        def group(gi):
            d = DIL[gi][1]
            qs[...] = _rms2(q_ref[...].astype(F32), gq_ref[gi:gi + 1, :])
            ks[...] = _rms2(k_ref[...].astype(F32), gk_ref[gi:gi + 1, :])
            vs[...] = v_ref[...].astype(F32)
            bias = _dil_bias(t_ref, gi, d)

            def block(cur, prev):
                rows = _dil_rows(cur, d)
                q2 = _stack_heads(qs[rows, :], lo).astype(BF16)
                kc, vc = ks[rows, :], vs[rows, :]
                if prev is None:
                    kcat, vcat, b = kc, vc, bias[:, QB:]
                else:
                    prow = _dil_rows(prev, d)
                    kcat = jnp.concatenate([ks[prow, :], kc], axis=0)
                    vcat = jnp.concatenate([vs[prow, :], vc], axis=0)
                    b = bias
                s = _dot_nt(q2, kcat.astype(BF16)) * DIL_SCALE + b
                m = jnp.max(s, axis=-1, keepdims=True)
                p = jnp.exp(s - m)
                l = jnp.sum(p, axis=-1, keepdims=True)
                o = _dot(p.astype(BF16), vcat.astype(BF16)) / l
                lse = m + jnp.log(l)
                o_ref[gi, rows, :] = jnp.where(lo, o[:QB], o[QB:])
                l_ref[gi, rows, :] = jnp.where(lo, lse[:QB], lse[QB:])

            _dil_walk(d, block, True)

        for gi in range(NG):
            pl.when(g == gi)(functools.partial(group, gi))

        @pl.when(g == NG - 1)
        def _():
            y_ref[...] = _mergec_math(o_ref[0], o_ref[1], o_ref[2], l_ref[0], l_ref[1], l_ref[2],
                                      cz_ref[...].astype(F32)).astype(BF16)

    def col(base):
        return pl.BlockSpec((None, S, LANE), lambda b, hp, g: (b, 0, base // LANE + 4 * g + hp))

    gspec = pl.BlockSpec((NG, LANE), lambda b, hp, g: (0, 0))
    saved = pl.BlockSpec((NG, None, S, LANE), lambda b, hp, g: (0, b, 0, hp))
    return pl.pallas_call(
        body,
        grid=(B, 4, NG),
        in_specs=[col(O_DQ), col(O_DK), col(O_DV),
                  pl.BlockSpec((None, S, LANE), lambda b, hp, g: (b, 0, O_CZ // LANE + hp)),
                  gspec, gspec, pl.BlockSpec((None, 8, LANE), lambda b, hp, g: (hp, 0, 0))],
        out_specs=[pl.BlockSpec((None, S, LANE), lambda b, hp, g: (b, 0, hp)), saved, saved],
        out_shape=[jax.ShapeDtypeStruct((B, S, DWID), BF16), jax.ShapeDtypeStruct((NG, B, S, DWID), F32),
                   jax.ShapeDtypeStruct((NG, B, S, DWID), F32)],
        scratch_shapes=[pltpu.VMEM((S, LANE), F32)] * 3,
        name="dil_mixer_fwd",
        compiler_params=pltpu.CompilerParams(dimension_semantics=("arbitrary",) * 3, vmem_limit_bytes=VMEM_LIMIT),
    )(proj3, proj3, proj3, proj3, gq, gk, tab)


def _dilc_bwd(proj3, gq, gk, tab, o_all, l_all, d_yc):
    B = proj3.shape[0]

    def body(q_ref, k_ref, v_ref, cz_ref, gq_ref, gk_ref, t_ref, o_ref, l_ref, dy_ref,
             dq_out, dk_out, dv_out, dcz_out, dgq_out, dgk_out, qs, ks, vs, dos, dls, dqs, dks, dvs):
        g = pl.program_id(2)
        lo = _lo_mask((QB, LANE))

        @pl.when(jnp.logical_and(jnp.logical_and(pl.program_id(0) == 0, pl.program_id(1) == 0), g == 0))
        def _():
            dgq_out[...] = jnp.zeros((NG, LANE), F32)
            dgk_out[...] = jnp.zeros((NG, LANE), F32)

        def group(gi):
            d = DIL[gi][1]
            ls = [l_ref[j] for j in range(NG)]
            m = jnp.maximum(jnp.maximum(ls[0], ls[1]), ls[2])
            es = [jnp.exp(t - m) for t in ls]
            den = (es[0] + es[1]) + es[2]
            al = [e / den for e in es]
            os_ = [o_ref[j] for j in range(NG)]
            oc = (al[0] * os_[0] + al[1] * os_[1]) + al[2] * os_[2]
            cz = cz_ref[...].astype(F32)
            sg = jax.nn.sigmoid(cz)
            dy = dy_ref[...]
            d_oc = dy * (cz * sg)
            dcz_out[...] = (dy * oc * (sg * (1.0 + cz * (1.0 - sg)))).astype(BF16)
            ts = [_head_sum(d_oc * os_[j]) for j in range(NG)]
            tbar = (al[0] * ts[0] + al[1] * ts[1]) + al[2] * ts[2]
            dos[...] = al[gi] * d_oc
            dls[...] = al[gi] * (ts[gi] - tbar)

            qs[...] = _rms2(q_ref[...].astype(F32), gq_ref[gi:gi + 1, :])
            ks[...] = _rms2(k_ref[...].astype(F32), gk_ref[gi:gi + 1, :])
            vs[...] = v_ref[...].astype(F32)
            dks[...] = jnp.zeros((S, LANE), F32)
            dvs[...] = jnp.zeros((S, LANE), F32)
            bias = _dil_bias(t_ref, gi, d)

            def block(cur, prev):
                rows = _dil_rows(cur, d)
                q2 = _stack_heads(qs[rows, :], lo).astype(BF16)
                dob = dos[rows, :]
                do2 = _stack_heads(dob, lo).astype(BF16)
                kc, vc = ks[rows, :], vs[rows, :]
                if prev is None:
                    kcat, vcat, b = kc, vc, bias[:, QB:]
                else:
                    prow = _dil_rows(prev, d)
                    kcat = jnp.concatenate([ks[prow, :], kc], axis=0)
                    vcat = jnp.concatenate([vs[prow, :], vc], axis=0)
                    b = bias
                kcat = kcat.astype(BF16)
                vcat = vcat.astype(BF16)
                lse_b = l_ref[gi, rows, :]
                corr_b = dls[rows, :] - _head_sum(dob * o_ref[gi, rows, :])
                lse2 = jnp.concatenate([lse_b[:, 0:1], lse_b[:, HD:HD + 1]], axis=0)
                corr2 = jnp.concatenate([corr_b[:, 0:1], corr_b[:, HD:HD + 1]], axis=0)
                s = _dot_nt(q2, kcat) * DIL_SCALE + b
                p = jnp.exp(s - lse2)
                ds = (p * (_dot_nt(do2, vcat) + corr2) * DIL_SCALE).astype(BF16)
                dq2 = _dot(ds, kcat)
                dqs[rows, :] = jnp.where(lo, dq2[:QB], dq2[QB:])
                dk = _dot_tn(ds, q2)
                dv = _dot_tn(p.astype(BF16), do2)
                if prev is None:
                    dks[rows, :] += dk
                    dvs[rows, :] += dv
                else:
                    dks[prow, :] += dk[:QB]
                    dvs[prow, :] += dv[:QB]
                    dks[rows, :] += dk[QB:]
                    dvs[rows, :] += dv[QB:]

            _dil_walk(d, block, False)

            _, pull_q = jax.vjp(_rms2, q_ref[...].astype(F32), gq_ref[gi:gi + 1, :])
            dxq, dgq = pull_q(dqs[...])
            dq_out[...] = dxq.astype(BF16)
            dgq_out[gi:gi + 1, :] += dgq
            _, pull_k = jax.vjp(_rms2, k_ref[...].astype(F32), gk_ref[gi:gi + 1, :])
            dxk, dgk = pull_k(dks[...])
            dk_out[...] = dxk.astype(BF16)
            dgk_out[gi:gi + 1, :] += dgk
            dv_out[...] = dvs[...].astype(BF16)

        for gi in range(NG):
            pl.when(g == gi)(functools.partial(group, gi))

    def col(base):
        return pl.BlockSpec((None, S, LANE), lambda b, hp, g: (b, 0, base // LANE + 4 * g + hp))

    gspec = pl.BlockSpec((NG, LANE), lambda b, hp, g: (0, 0))
    saved = pl.BlockSpec((NG, None, S, LANE), lambda b, hp, g: (0, b, 0, hp))
    per_pair = pl.BlockSpec((None, S, LANE), lambda b, hp, g: (b, 0, hp))
    dcol = pl.BlockSpec((None, S, LANE), lambda b, hp, g: (b, 0, 4 * g + hp))
    return pl.pallas_call(
        body,
        grid=(B, 4, NG),
        in_specs=[col(O_DQ), col(O_DK), col(O_DV),
                  pl.BlockSpec((None, S, LANE), lambda b, hp, g: (b, 0, O_CZ // LANE + hp)),
                  gspec, gspec, pl.BlockSpec((None, 8, LANE), lambda b, hp, g: (hp, 0, 0)),
                  saved, saved, per_pair],
        out_specs=[dcol, dcol, dcol, per_pair, gspec, gspec],
        out_shape=[jax.ShapeDtypeStruct((B, S, NG * DWID), BF16)] * 3
        + [jax.ShapeDtypeStruct((B, S, DWID), BF16), jax.ShapeDtypeStruct((NG, LANE), F32),
           jax.ShapeDtypeStruct((NG, LANE), F32)],
        scratch_shapes=[pltpu.VMEM((S, LANE), F32)] * 8,
        name="dil_mixer_bwd",
        compiler_params=pltpu.CompilerParams(dimension_semantics=("arbitrary",) * 3, vmem_limit_bytes=VMEM_LIMIT),
    )(proj3, proj3, proj3, proj3, gq, gk, tab, o_all, l_all, d_yc)


def _dil_slopes():
    slopes = (2.0 ** (-8.0 * np.arange(1, NG * DH + 1, dtype=np.float32) / (NG * DH))).astype(np.float32).reshape(NG, DH)
    tab = np.zeros((4, 8, LANE), np.float32)
    for hp in range(4):
        for gi in range(NG):
            for e in (0, 1):
                tab[hp, 2 * gi + e, :] = slopes[gi, 2 * hp + e]
    return jnp.asarray(tab)


def _rope_tables():
    inv = ROPE_THETA ** (-jnp.arange(0, ROPE, 2, dtype=F32) / ROPE)
    ang = jnp.arange(S, dtype=F32)[:, None] * inv[None, :]
    cos, sin = jnp.cos(ang), jnp.sin(ang)
    z16 = jnp.zeros((S, 16), F32)
    c = jnp.concatenate([jnp.ones((S, NOPE), F32), cos, cos, jnp.zeros((S, 32), F32)], axis=1)
    s1 = jnp.concatenate([jnp.zeros((S, NOPE), F32), -sin, z16, jnp.zeros((S, 32), F32)], axis=1)
    s2 = jnp.concatenate([jnp.zeros((S, NOPE), F32), z16, sin, jnp.zeros((S, 32), F32)], axis=1)
    return c, s1, s2


def _pad_heads_uq(w):
    return jnp.pad(w.reshape(QL, NH, QK), ((0, 0), (0, 0), (0, QKP - QK))).reshape(QL, NH * QKP)


def _unpad_heads_uq(g):
    return g.reshape(QL, NH, QKP)[:, :, :QK].reshape(QL, NH * QK)


def _split_ukv(w):
    w3 = w.reshape(KVL, NH, NOPE + VD)
    uk = jnp.pad(w3[:, :, :NOPE], ((0, 0), (0, 0), (0, QKP - NOPE))).reshape(KVL, NH * QKP)
    return uk, w3[:, :, NOPE:].reshape(KVL, NH * VD)


def _join_ukv(guk, guv):
    return jnp.concatenate([guk.reshape(KVL, NH, QKP)[:, :, :NOPE], guv.reshape(KVL, NH, VD)],
                           axis=-1).reshape(KVL, NH * (NOPE + VD))


BR = 512
BRM = 256


def _layer_fwd(x, w, tabs, batch):
    T = batch * S
    rope_c, rope_s1, rope_s2, dil_tab = tabs
    res = {"x": x}
    row = lambda c: (lambda i: (i, c))
    fix = lambda i: (0, 0)

    h = _pcall("norm_fwd", _rms, (T // BR,),
               [(x, (BR, D), row(0)), (w["norm_g"], (1, D), fix)],
               [((T, D), BF16, (BR, D), row(0))])[0]
    proj = _mm("in_proj", h, w["w_in_t"], tb=True, out_dtype=BF16, tm=512, tn=1024)
    res["h"], res["proj"] = h, proj
    proj3 = proj.reshape(batch, S, NINP)

    cblk = lambda s: (lambda j, b: (b, 0, 4 * s + j))
    y_a = _pcall("conv_fwd", _conv_math, (4, batch),
                 [(proj3, (None, S, LANE), cblk(0)), (proj3, (None, S, LANE), cblk(1)),
                  (proj3, (None, S, LANE), cblk(2)), (proj3, (None, S, LANE), cblk(3)),
                  (w["conv_w"], (3, LANE), lambda j, b: (0, j)), (w["conv_b"], (1, LANE), lambda j, b: (0, j))],
                 [((batch, S, CW), BF16, (None, S, LANE), lambda j, b: (b, 0, j))])[0].reshape(T, CW)
    res["y_a"] = y_a

    cqn, ckvn = _pcall("mla_pre_fwd", _mla_pre_math, (T // BR,),
                       [(proj, (BR, QL), row(O_CQ // QL)), (proj, (BR, KVL), row(O_CKV // KVL)),
                        (w["q_a_norm_g"], (1, QL), fix), (w["kv_a_norm_g"], (1, KVL), fix)],
                       [((T, QL), BF16, (BR, QL), row(0)), ((T, KVL), BF16, (BR, KVL), row(0))])
    w_uq_p = _pad_heads_uq(w["w_uq"])
    w_uk, w_uv = _split_ukv(w["w_ukv"])
    q = _mm("uq", cqn, w_uq_p, out_dtype=BF16)
    kn = _mm("uk", ckvn, w_uk, out_dtype=BF16)
    v = _mm("uv", ckvn, w_uv, out_dtype=BF16)
    nrr = S // BR
    tab_row = lambda i: (i % nrr, 0)
    qr, kr = _pcall("rope_fwd", _rope_math, (T // BR,),
                    [(q, (BR, NH * QKP), row(0)), (kn, (BR, NH * QKP), row(0)), (proj, (BR, LANE), row(O_KPE // LANE)),
                     (w["mla_q_norm_g"], (1, QKP), fix), (w["mla_k_norm_g"], (1, QKP), fix),
                     (rope_c, (BR, QKP), tab_row), (rope_s1, (BR, QKP), tab_row), (rope_s2, (BR, QKP), tab_row)],
                    [((T, NH * QKP), BF16, (BR, NH * QKP), row(0))] * 2)
    qr = qr.reshape(batch, S, NH * QKP)
    kr = kr.reshape(batch, S, NH * QKP)
    v = v.reshape(batch, S, NH * VD)
    o_b, l_b = _mla_fwd(qr, kr, v)
    ob2 = o_b.reshape(T, NH * VD)
    y_b = _pcall("gateb_fwd", _gate_math, (T // BR,),
                 [(ob2, (BR, 512), row(0)), (proj, (BR, 512), row(O_BZ // 512))],
                 [((T, 512), BF16, (BR, 512), row(0))])[0]
    res.update(cqn=cqn, ckvn=ckvn, q=q, kn=kn, qr=qr, kr=kr, v=v, o_b=o_b, l_b=l_b, ob2=ob2, y_b=y_b,
               w_uq_p=w_uq_p, w_uk=w_uk, w_uv=w_uv)

    gq2 = jnp.tile(w["dil_q_norm_g"].reshape(NG, HD), (1, 2))
    gk2 = jnp.tile(w["dil_k_norm_g"].reshape(NG, HD), (1, 2))
    y_c, o_all, l_all = _dilc_fwd(proj3, gq2, gk2, dil_tab)
    y_c = y_c.reshape(T, DWID)
    res.update(o_all=o_all, l_all=l_all, y_c=y_c)

    pa = _mm("out_a", y_a, w["w_out_a"], out_dtype=BF16)
    pb = _mm("out_b", y_b, w["w_out_b"], out_dtype=BF16)
    pc = _mm("out_c", y_c, w["w_out_c"], out_dtype=BF16)
    merged = _pcall("merge_fwd", _merge_math, (T // BRM,),
                    [(proj, (BRM, D), row(O_G // D + s)) for s in range(3)]
                    + [(w["b_gate"], (1, D), (lambda s: (lambda i: (0, s)))(s)) for s in range(3)]
                    + [(t, (BRM, D), row(0)) for t in (pa, pb, pc)],
                    [((T, D), BF16, (BRM, D), row(0))])[0]
    out = _mm("o_proj", merged, w["w_o"], add=x)
    res.update(pa=pa, pb=pb, pc=pc, merged=merged)
    return out, res


def _norm_bwd_math(x, g, dh, dy):
    _, pull = jax.vjp(_rms, x, g)
    dx, dg = pull(dh)
    return dx + dy, dg


def _layer_bwd(dy, w, res, tabs, batch):
    T = batch * S
    rope_c, rope_s1, rope_s2, dil_tab = tabs
    row = lambda c: (lambda i: (i, c))
    fix = lambda i: (0, 0)
    x, proj, h = res["x"], res["proj"], res["h"]
    proj3 = proj.reshape(batch, S, NINP)
    g = {}

    d_merged = _mm("o_proj_dx", dy, w["w_o"], tb=True)
    g["w_o"] = _mm("o_proj_dw", res["merged"], dy, ta=True, tm=1024)

    merge_bwd = functools.partial(_vjp_of(_merge_math, 9), n_prim=9)
    dg0, dg1, dg2, db0, db1, db2, dpa, dpb, dpc = _pcall(
        "merge_bwd", merge_bwd, (T // BRM,),
        [(proj, (BRM, D), row(O_G // D + s)) for s in range(3)]
        + [(w["b_gate"], (1, D), (lambda s: (lambda i: (0, s)))(s)) for s in range(3)]
        + [(t, (BRM, D), row(0)) for t in (res["pa"], res["pb"], res["pc"])]
        + [(d_merged, (BRM, D), row(0))],
        [((T, D), BF16, (BRM, D), row(0))] * 3 + [((1, D), F32, (1, D), fix, True)] * 3
        + [((T, D), BF16, (BRM, D), row(0))] * 3)
    g["b_gate"] = jnp.concatenate([db0, db1, db2], axis=1)

    d_ya = _mm("out_a_dx", dpa, w["w_out_a"], tb=True)
    d_yb = _mm("out_b_dx", dpb, w["w_out_b"], tb=True)
    d_yc = _mm("out_c_dx", dpc, w["w_out_c"], tb=True)
    g["w_out_a"] = _mm("out_a_dw", res["y_a"], dpa, ta=True)
    g["w_out_b"] = _mm("out_b_dw", res["y_b"], dpb, ta=True)
    g["w_out_c"] = _mm("out_c_dw", res["y_c"], dpc, ta=True)

    cblk = lambda s: (lambda j, b: (b, 0, 4 * s + j))
    oblk = lambda j, b: (b, 0, j)
    conv_bwd = functools.partial(_vjp_of(_conv_math, 6), n_prim=6)
    d_ab, d_ac, d_ax, d_az, g["conv_w"], g["conv_b"] = _pcall(
        "conv_bwd", conv_bwd, (4, batch),
        [(proj3, (None, S, LANE), cblk(s)) for s in range(4)]
        + [(w["conv_w"], (3, LANE), lambda j, b: (0, j)), (w["conv_b"], (1, LANE), lambda j, b: (0, j)),
           (d_ya.reshape(batch, S, CW), (None, S, LANE), oblk)],
        [((batch, S, CW), BF16, (None, S, LANE), oblk)] * 4
        + [((3, CW), F32, (3, LANE), lambda j, b: (0, j), True), ((1, CW), F32, (1, LANE), lambda j, b: (0, j), True)])

    gate_bwd = functools.partial(_vjp_of(_gate_math, 2), n_prim=2)
    d_ob, d_bz = _pcall("gateb_bwd", gate_bwd, (T // BR,),
                        [(res["ob2"], (BR, 512), row(0)), (proj, (BR, 512), row(O_BZ // 512)), (d_yb, (BR, 512), row(0))],
                        [((T, 512), F32, (BR, 512), row(0)), ((T, 512), BF16, (BR, 512), row(0))])
    dqr, dkr, dv = _mla_bwd(res["qr"], res["kr"], res["v"], d_ob.reshape(batch, S, NH * VD), res["o_b"], res["l_b"])
    nrr = S // BR
    tab_row = lambda i: (i % nrr, 0)
    rope_bwd = functools.partial(_vjp_of(_rope_math, 5), n_prim=8)
    d_q, d_kn, d_kpe_p, g["mla_q_norm_g"], g["mla_k_norm_g"] = _pcall(
        "rope_bwd", rope_bwd, (T // BR,),
        [(res["q"], (BR, NH * QKP), row(0)), (res["kn"], (BR, NH * QKP), row(0)), (proj, (BR, LANE), row(O_KPE // LANE)),
         (w["mla_q_norm_g"], (1, QKP), fix), (w["mla_k_norm_g"], (1, QKP), fix),
         (rope_c, (BR, QKP), tab_row), (rope_s1, (BR, QKP), tab_row), (rope_s2, (BR, QKP), tab_row),
         (dqr.reshape(T, NH * QKP), (BR, NH * QKP), row(0)), (dkr.reshape(T, NH * QKP), (BR, NH * QKP), row(0))],
        [((T, NH * QKP), BF16, (BR, NH * QKP), row(0))] * 2 + [((T, LANE), BF16, (BR, LANE), row(0))]
        + [((1, QKP), F32, (1, QKP), fix, True)] * 2)
    dv = dv.reshape(T, NH * VD)
    d_cqn = _mm("uq_dx", d_q, res["w_uq_p"], tb=True)
    d_ckvn = _mm("uk_dx", d_kn, res["w_uk"], tb=True)
    d_ckvn = _mm("uv_dx", dv, res["w_uv"], tb=True, add=d_ckvn)
    g["w_uq"] = _unpad_heads_uq(_mm("uq_dw", res["cqn"], d_q, ta=True))
    g["w_ukv"] = _join_ukv(_mm("uk_dw", res["ckvn"], d_kn, ta=True), _mm("uv_dw", res["ckvn"], dv, ta=True))
    pre_bwd = functools.partial(_vjp_of(_mla_pre_math, 4), n_prim=4)
    d_cq, d_ckv, g["q_a_norm_g"], g["kv_a_norm_g"] = _pcall(
        "mla_pre_bwd", pre_bwd, (T // BR,),
        [(proj, (BR, QL), row(O_CQ // QL)), (proj, (BR, KVL), row(O_CKV // KVL)),
         (w["q_a_norm_g"], (1, QL), fix), (w["kv_a_norm_g"], (1, KVL), fix),
         (d_cqn, (BR, QL), row(0)), (d_ckvn, (BR, KVL), row(0))],
        [((T, QL), BF16, (BR, QL), row(0)), ((T, KVL), BF16, (BR, KVL), row(0)),
         ((1, QL), F32, (1, QL), fix, True), ((1, KVL), F32, (1, KVL), fix, True)])

    gq2 = jnp.tile(w["dil_q_norm_g"].reshape(NG, HD), (1, 2))
    gk2 = jnp.tile(w["dil_k_norm_g"].reshape(NG, HD), (1, 2))
    d_dq, d_dk, d_dv, d_cz, dgq, dgk = _dilc_bwd(proj3, gq2, gk2, dil_tab, res["o_all"], res["l_all"],
                                                 d_yc.reshape(batch, S, DWID))
    g["dil_q_norm_g"] = dgq[:, :HD] + dgq[:, HD:]
    g["dil_k_norm_g"] = dgk[:, :HD] + dgk[:, HD:]
    d_dq, d_dk, d_dv = (t.reshape(T, NG * DWID) for t in (d_dq, d_dk, d_dv))
    d_cz = d_cz.reshape(T, DWID)

    dproj = jnp.concatenate(
        [t.reshape(T, CW) for t in (d_ab, d_ac, d_ax, d_az)]
        + [d_cq, d_ckv, d_kpe_p, d_bz, d_dq, d_dk, d_dv, d_cz, dg0, dg1, dg2], axis=1)
    d_h = _mm("in_proj_dx", dproj, w["w_in_t"], tm=1024)
    g["w_in_t"] = _mm("in_proj_dw", dproj, h, ta=True, tm=1024)
    dx, g["norm_g"] = _pcall("norm_bwd", _norm_bwd_math, (T // BR,),
                             [(x, (BR, D), row(0)), (w["norm_g"], (1, D), fix), (d_h, (BR, D), row(0)),
                              (dy, (BR, D), row(0))],
                             [((T, D), F32, (BR, D), row(0)), ((1, D), F32, (1, D), fix, True)])
    return dx, g


def _loss_math(y, t):
    e = y - t
    return e * (1.0 / D), 0.5 * jnp.sum(jnp.sum(e * e, axis=-1, keepdims=True) / D, axis=0, keepdims=True)


def _local_step(x, target, ws, batch):
    T = batch * S
    tabs = _rope_tables() + (_dil_slopes(),)
    saved = []
    y = x
    for l in range(NL):
        y, res = _layer_fwd(y, ws[l], tabs, batch)
        saved.append(res)
    row = lambda i: (i, 0)
    dy, loss = _pcall("loss", _loss_math, (T // BR,),
                      [(y, (BR, D), row), (target, (BR, D), row)],
                      [((T, D), F32, (BR, D), row), ((1, 1), F32, (1, 1), lambda i: (0, 0), True)])
    grads = [None] * NL
    for l in reversed(range(NL)):
        dy, grads[l] = _layer_bwd(dy, ws[l], saved[l], tabs, batch)
    return loss, dy, grads


ANY = pl.BlockSpec(memory_space=pl.ANY)
U32 = jnp.uint32
WSH = NIN // 4
WA = KPE_END
WB = WSH - WA
CWD = 512
PACK_ROWS = 1472
HW = PACK_W // 2


def _me():
    return lax.axis_index("x"), lax.axis_index("y"), lax.axis_index("c")


def _piece_rows(k):
    a = k * WSH + jnp.where(k > 0, NINP - NIN, 0)
    b = k * WSH + WA + (NINP - NIN)
    return ((0, pl.multiple_of(a, 8), WA), (WA, pl.multiple_of(b, 8), WB))


def _pack_words(lo, hi):
    ul = lax.bitcast_convert_type(lo.astype(BF16).astype(F32), U32)
    uh = lax.bitcast_convert_type(hi.astype(BF16).astype(F32), U32)
    w = jnp.bitwise_or(jnp.bitwise_and(uh, jnp.uint32(0xFFFF0000)), jnp.right_shift(ul, jnp.uint32(16)))
    return lax.bitcast_convert_type(w, F32)


def _unpack_words(w):
    w = lax.bitcast_convert_type(w, U32)
    lo = lax.bitcast_convert_type(jnp.left_shift(w, jnp.uint32(16)), F32)
    hi = lax.bitcast_convert_type(jnp.bitwise_and(w, jnp.uint32(0xFFFF0000)), F32)
    return lo, hi


def _all_gather(wc, sp):
    def body(w_ref, s_ref, ow_ref, os_ref, send_sems, recv_sems):
        x, y, c = _me()
        k_me = 2 * x + y
        sib = (x, y, 1 - c)
        chips = [(1 - x, y), (x, 1 - y), (1 - x, 1 - y)]
        wcols = lambda cc: pl.ds(pl.multiple_of(cc * (CWD // 2), LANE), CWD // 2)
        scols = lambda cc: pl.ds(pl.multiple_of(cc * HW, LANE), HW)

        def windows(k, cc):
            pcs = _piece_rows(k)
            return ([(w_ref.at[pl.ds(l0, n), wcols(cc)], ow_ref.at[pl.ds(p0, n), wcols(cc)]) for l0, p0, n in pcs]
                    + [(s_ref.at[:, scols(cc)], os_ref.at[k, :, scols(cc)])])

        def copy(i, src, dst, to):
            return pltpu.make_async_remote_copy(src_ref=src, dst_ref=dst, send_sem=send_sems.at[i],
                                                recv_sem=recv_sems.at[i], device_id=to, device_id_type=MESH)

        def own_windows():
            return ([(w_ref.at[pl.ds(l0, n)], ow_ref.at[pl.ds(p0, n)]) for l0, p0, n in _piece_rows(k_me)]
                    + [(s_ref, os_ref.at[k_me])])

        first = [copy(18 + i, src, dst, sib) for i, (src, dst) in enumerate(own_windows())]
        for j, (cx, cy) in enumerate(chips):
            for i, (src, dst) in enumerate(windows(k_me, c)):
                first.append(copy(3 * j + i, src, dst, (cx, cy, c)))
        for cp in first:
            cp.start()
        passed = []
        for j, (cx, cy) in enumerate(chips):
            for i, (_, dst) in enumerate(windows(2 * cx + cy, c)):
                copy(3 * j + i, dst, dst, (cx, cy, c)).wait_recv()
                cp = copy(9 + 3 * j + i, dst, dst, sib)
                cp.start()
                passed.append(cp)
        for j, (cx, cy) in enumerate(chips):
            for i, (_, dst) in enumerate(windows(2 * cx + cy, 1 - c)):
                copy(9 + 3 * j + i, dst, dst, sib).wait_recv()
        for i, (_, dst) in enumerate(own_windows()):
            copy(18 + i, dst, dst, sib).wait_recv()
        for cp in first + passed:
            cp.wait_send()

    return pl.pallas_call(
        body,
        out_shape=[jax.ShapeDtypeStruct((NINP, CWD), F32), jax.ShapeDtypeStruct((4, PACK_ROWS, PACK_W), BF16)],
        in_specs=[ANY, ANY], out_specs=[ANY, ANY],
        scratch_shapes=[pltpu.SemaphoreType.DMA((21,)), pltpu.SemaphoreType.DMA((21,))],
        name="weights_all_gather",
    )(wc, sp)


UNPACK_BR = 512


def _unpack_w_in(cont):
    def body(c_ref, o_ref):
        lo, hi = _unpack_words(c_ref[...])
        r = pl.program_id(0) * UNPACK_BR + lax.broadcasted_iota(jnp.int32, (UNPACK_BR, CWD), 0)
        pad = jnp.logical_and(r >= KPE_END, r < KPE_END + NINP - NIN)
        o_ref[:, 0:CWD] = jnp.where(pad, 0.0, lo).astype(BF16)
        o_ref[:, CWD:2 * CWD] = jnp.where(pad, 0.0, hi).astype(BF16)

    return pl.pallas_call(
        body, grid=(NINP // UNPACK_BR,),
        in_specs=[pl.BlockSpec((UNPACK_BR, CWD), lambda i: (i, 0))],
        out_specs=pl.BlockSpec((UNPACK_BR, D), lambda i: (i, 0)),
        out_shape=jax.ShapeDtypeStruct((NINP, D), BF16),
        name="w_in_unpack",
        compiler_params=pltpu.CompilerParams(dimension_semantics=("arbitrary",), vmem_limit_bytes=VMEM_LIMIT),
    )(cont)


def _rs_swap(gw, gs):
    def body(w_ref, s_ref, rw_ref, rs_ref, send_sems, recv_sems):
        x, y, c = _me()
        oc = 1 - c
        cps = [pltpu.make_async_remote_copy(src_ref=w_ref.at[:, pl.ds(pl.multiple_of(oc * (D // 2), LANE), D // 2)],
                                            dst_ref=rw_ref, send_sem=send_sems.at[0], recv_sem=recv_sems.at[0],
                                            device_id=(x, y, oc), device_id_type=MESH),
               pltpu.make_async_remote_copy(src_ref=s_ref.at[:, :, pl.ds(pl.multiple_of(oc * HW, LANE), HW)],
                                            dst_ref=rs_ref, send_sem=send_sems.at[1], recv_sem=recv_sems.at[1],
                                            device_id=(x, y, oc), device_id_type=MESH)]
        for cp in cps:
            cp.start()
        for cp in cps:
            cp.wait()

    return pl.pallas_call(
        body,
        out_shape=[jax.ShapeDtypeStruct((NINP, D // 2), F32), jax.ShapeDtypeStruct((4, PACK_ROWS, HW), F32)],
        in_specs=[ANY, ANY], out_specs=[ANY, ANY],
        scratch_shapes=[pltpu.SemaphoreType.DMA((2,)), pltpu.SemaphoreType.DMA((2,))],
        name="grads_sibling_swap",
    )(gw, gs)


SUM_BR = 512


def _rs_chip_sum_w(gw, rw, cidx):
    def body(c_ref, g_ref, r_ref, o_ref):
        s = g_ref[...] + r_ref[...]
        q = D // 8
        o_ref[...] = jnp.concatenate([_pack_words(s[:, 0:q], s[:, q:2 * q]),
                                      _pack_words(s[:, 2 * q:3 * q], s[:, 3 * q:4 * q])], axis=1)

    return pl.pallas_call(
        body,
        grid_spec=pltpu.PrefetchScalarGridSpec(
            num_scalar_prefetch=1, grid=(NINP // SUM_BR,),
            in_specs=[pl.BlockSpec((SUM_BR, D // 2), lambda i, cr: (i, cr[0])),
                      pl.BlockSpec((SUM_BR, D // 2), lambda i, cr: (i, 0))],
            out_specs=pl.BlockSpec((SUM_BR, D // 4), lambda i, cr: (i, 0))),
        out_shape=jax.ShapeDtypeStruct((NINP, D // 4), F32),
        name="grads_chip_sum_w",
        compiler_params=pltpu.CompilerParams(dimension_semantics=("arbitrary",), vmem_limit_bytes=VMEM_LIMIT),
    )(cidx, gw, rw)


def _rs_chip_sum_s(gs, rs, cidx):
    def body(c_ref, g_ref, r_ref, o_ref):
        o_ref[...] = (g_ref[...] + r_ref[...]).astype(BF16)

    return pl.pallas_call(
        body,
        grid_spec=pltpu.PrefetchScalarGridSpec(
            num_scalar_prefetch=1, grid=(4,),
            in_specs=[pl.BlockSpec((None, PACK_ROWS, HW), lambda j, cr: (j, 0, cr[0])),
                      pl.BlockSpec((None, PACK_ROWS, HW), lambda j, cr: (j, 0, 0))],
            out_specs=pl.BlockSpec((None, PACK_ROWS, HW), lambda j, cr: (j, 0, 0))),
        out_shape=jax.ShapeDtypeStruct((4, PACK_ROWS, HW), BF16),
        name="grads_chip_sum_s",
        compiler_params=pltpu.CompilerParams(dimension_semantics=("arbitrary",), vmem_limit_bytes=VMEM_LIMIT),
    )(cidx, gs, rs)


def _rs_exchange(sw, ss):
    def body(sw_ref, ss_ref, r2w_ref, r2s_ref, send_sems, recv_sems):
        x, y, c = _me()
        chips = [(1 - x, y), (x, 1 - y), (1 - x, 1 - y)]
        cps = []
        for j, (cx, cy) in enumerate(chips):
            k = 2 * cx + cy
            for i, (l0, p0, n) in enumerate(_piece_rows(k)):
                cps.append(pltpu.make_async_remote_copy(
                    src_ref=sw_ref.at[pl.ds(p0, n)], dst_ref=r2w_ref.at[j, pl.ds(l0, n)], send_sem=send_sems.at[3 * j + i],
                    recv_sem=recv_sems.at[3 * j + i], device_id=(cx, cy, c), device_id_type=MESH))
            cps.append(pltpu.make_async_remote_copy(
                src_ref=ss_ref.at[k], dst_ref=r2s_ref.at[j], send_sem=send_sems.at[3 * j + 2],
                recv_sem=recv_sems.at[3 * j + 2], device_id=(cx, cy, c), device_id_type=MESH))
        for cp in cps:
            cp.start()
        for cp in cps:
            cp.wait()

    return pl.pallas_call(
        body,
        out_shape=[jax.ShapeDtypeStruct((3, WSH, D // 4), F32), jax.ShapeDtypeStruct((3, PACK_ROWS, HW), BF16)],
        in_specs=[ANY] * 2, out_specs=[ANY] * 2,
        scratch_shapes=[pltpu.SemaphoreType.DMA((9,)), pltpu.SemaphoreType.DMA((9,))],
        name="grads_chip_exchange",
    )(sw, ss)


def _rs_final_w(gw, rw, r2w, idx):
    q = D // 8

    def body(i_ref, g_ref, r_ref, p_ref, o_ref, gbuf, rbuf, sems):
        i = pl.program_id(0)
        k, c = i_ref[0], i_ref[1]
        cps = []
        for n_, (l0, p0, n) in enumerate(_piece_rows(k)):
            gcol = pl.ds(pl.multiple_of(c * (D // 2) + i * 2 * q, LANE), 2 * q)
            rcol = pl.ds(pl.multiple_of(i * 2 * q, LANE), 2 * q)
            cps.append(pltpu.make_async_copy(g_ref.at[pl.ds(p0, n), gcol], gbuf.at[pl.ds(l0, n)], sems.at[2 * n_]))
            cps.append(pltpu.make_async_copy(r_ref.at[pl.ds(p0, n), rcol], rbuf.at[pl.ds(l0, n)], sems.at[2 * n_ + 1]))
        for cp in cps:
            cp.start()
        for cp in cps:
            cp.wait()
        acc = gbuf[...] + rbuf[...]
        for j in range(3):
            lo, hi = _unpack_words(p_ref[j])
            acc = acc + jnp.concatenate([lo, hi], axis=1)
        o_ref[...] = acc

    return pl.pallas_call(
        body,
        grid_spec=pltpu.PrefetchScalarGridSpec(
            num_scalar_prefetch=1, grid=(2,),
            in_specs=[ANY, ANY, pl.BlockSpec((3, WSH, q), lambda i, ir: (0, 0, i))],
            out_specs=pl.BlockSpec((WSH, 2 * q), lambda i, ir: (0, i)),
            scratch_shapes=[pltpu.VMEM((WSH, 2 * q), F32), pltpu.VMEM((WSH, 2 * q), F32), pltpu.SemaphoreType.DMA((4,))]),
        out_shape=jax.ShapeDtypeStruct((WSH, D // 2), F32),
        name="grads_final_sum_w",
        compiler_params=pltpu.CompilerParams(dimension_semantics=("arbitrary",), vmem_limit_bytes=VMEM_LIMIT),
    )(idx, gw, rw, r2w)


def _rs_final_s(gs, rs, r2s, idx):
    def body(i_ref, g_ref, r_ref, p_ref, o_ref):
        acc = g_ref[...] + r_ref[...]
        for j in range(3):
            acc = acc + p_ref[j].astype(F32)
        o_ref[...] = acc

    return pl.pallas_call(
        body,
        grid_spec=pltpu.PrefetchScalarGridSpec(
            num_scalar_prefetch=1, grid=(1,),
            in_specs=[pl.BlockSpec((None, PACK_ROWS, HW), lambda i, ir: (ir[0], 0, ir[1])),
                      pl.BlockSpec((None, PACK_ROWS, HW), lambda i, ir: (ir[0], 0, 0)),
                      pl.BlockSpec((3, PACK_ROWS, HW), lambda i, ir: (0, 0, 0))],
            out_specs=pl.BlockSpec((PACK_ROWS, HW), lambda i, ir: (0, 0))),
        out_shape=jax.ShapeDtypeStruct((PACK_ROWS, HW), F32),
        name="grads_final_sum_s",
        compiler_params=pltpu.CompilerParams(dimension_semantics=("arbitrary",), vmem_limit_bytes=VMEM_LIMIT),
    )(idx, gs, rs, r2s)


def _rs_share(fw, fs):
    def body(w_ref, s_ref, ow_ref, os_ref, send_sems, recv_sems):
        x, y, c = _me()
        cps = [pltpu.make_async_remote_copy(src_ref=w_ref, dst_ref=ow_ref, send_sem=send_sems.at[0],
                                            recv_sem=recv_sems.at[0], device_id=(x, y, 1 - c), device_id_type=MESH),
               pltpu.make_async_remote_copy(src_ref=s_ref, dst_ref=os_ref, send_sem=send_sems.at[1],
                                            recv_sem=recv_sems.at[1], device_id=(x, y, 1 - c), device_id_type=MESH)]
        for cp in cps:
            cp.start()
        for cp in cps:
            cp.wait()

    return pl.pallas_call(
        body,
        out_shape=[jax.ShapeDtypeStruct((WSH, D // 2), F32), jax.ShapeDtypeStruct((PACK_ROWS, HW), F32)],
        in_specs=[ANY, ANY], out_specs=[ANY, ANY],
        scratch_shapes=[pltpu.SemaphoreType.DMA((2,)), pltpu.SemaphoreType.DMA((2,))],
        name="grads_share",
    )(fw, fs)


def _both_halves(mine, other, c):
    return jnp.where(c == 0, jnp.concatenate([mine, other], axis=1), jnp.concatenate([other, mine], axis=1))


def _reduce_scatter(gw, gs):
    x, y, c = _me()
    cidx = jnp.reshape(c, (1,)).astype(jnp.int32)
    idx = jnp.stack([2 * x + y, c]).astype(jnp.int32)
    rw, rs = _rs_swap(gw, gs)
    sw = _rs_chip_sum_w(gw, rw, cidx)
    ss = _rs_chip_sum_s(gs, rs, cidx)
    r2w, r2s = _rs_exchange(sw, ss)
    fw = _rs_final_w(gw, rw, r2w, idx)
    fs = _rs_final_s(gs, rs, r2s, idx)
    ow, os_ = _rs_share(fw, fs)
    return _both_halves(fw, ow, c), _both_halves(fs, os_, c)


def _all_reduce_small(gs):
    rows = gs.shape[0]

    def body(g_ref, o_ref, buf, send_sems, recv_sems):
        x, y, c = _me()
        me = 4 * x + 2 * y + c
        buf[me] = g_ref[...]
        cps = []
        for r in range(1, 8):
            fx, fy, fc = (r >> 2) & 1, (r >> 1) & 1, r & 1
            px, py, pc = jnp.bitwise_xor(x, fx), jnp.bitwise_xor(y, fy), jnp.bitwise_xor(c, fc)
            cps.append((pltpu.make_async_remote_copy(
                src_ref=g_ref, dst_ref=buf.at[me], send_sem=send_sems.at[r - 1], recv_sem=recv_sems.at[r - 1],
                device_id=(px, py, pc), device_id_type=MESH), 4 * px + 2 * py + pc))
        for cp, _ in cps:
            cp.start()
        for r, (cp, peer) in enumerate(cps):
            pltpu.make_async_remote_copy(
                src_ref=g_ref, dst_ref=buf.at[peer], send_sem=send_sems.at[r], recv_sem=recv_sems.at[r],
                device_id=(x, y, c), device_id_type=MESH).wait_recv()
        for cp, _ in cps:
            cp.wait_send()
        acc = buf[0]
        for k in range(1, 8):
            acc = acc + buf[k]
        o_ref[...] = acc

    return pl.pallas_call(
        body,
        out_shape=jax.ShapeDtypeStruct((rows, LANE), F32),
        in_specs=[pl.BlockSpec(memory_space=pltpu.VMEM)],
        out_specs=pl.BlockSpec(memory_space=pltpu.VMEM),
        scratch_shapes=[pltpu.VMEM((8, rows, LANE), F32), pltpu.SemaphoreType.DMA((7,)), pltpu.SemaphoreType.DMA((7,))],
        name="small_grads_all_reduce",
    )(gs)


PACK_SPLIT = (("w_uq", 96, (QL, 192)), ("w_ukv", 64, (KVL, 256)),
              ("w_out_a", 256, (CW, 256)), ("w_out_b", 256, (CW, 256)), ("w_out_c", 256, (CW, 256)),
              ("w_o", 512, (256, D)))
MAT_ROWS = 1440
CONV_SHARD = 3 * 128


def _w_in_words(w_in_shard):
    t = w_in_shard.T
    return _pack_words(t[:, :CWD], t[:, CWD:])


def _pack_weights(wl):
    parts = [wl[n].astype(BF16).reshape(-1, PACK_W) for n, _, _ in PACK_SPLIT]
    cw = wl["conv_w"].reshape(-1)
    hi = cw.astype(BF16)
    r1 = cw - hi.astype(F32)
    mid = r1.astype(BF16)
    lo = (r1 - mid.astype(F32)).astype(BF16)
    cterms = jnp.pad(jnp.concatenate([hi, mid, lo]), (0, 3 * PACK_W - 3 * CONV_SHARD)).reshape(3, PACK_W)
    tail = jnp.pad(cterms, ((0, PACK_ROWS - MAT_ROWS - 3), (0, 0)))
    return jnp.concatenate(parts + [tail], axis=0)


def _unpack_weights(gath):
    out = {}
    r = 0
    for n, nrows, shp in PACK_SPLIT:
        t = gath[:, r:r + nrows].reshape((4,) + shp)
        r += nrows
        if n == "w_o":
            out[n] = t.reshape(4 * shp[0], shp[1])
        else:
            out[n] = t.transpose(1, 0, 2).reshape(shp[0], 4 * shp[1])
    ct = gath[:, r:r + 3].reshape(4, 3 * PACK_W)[:, :3 * CONV_SHARD].astype(F32).reshape(4, 3, CONV_SHARD)
    cw = (ct[:, 0] + ct[:, 1]) + ct[:, 2]
    out["conv_w"] = cw.reshape(4, 3, 128).transpose(1, 0, 2).reshape(3, CW)
    return out


def _pack_grads(g):
    parts = []
    for n, nrows, shp in PACK_SPLIT:
        t = g[n]
        if n == "w_o":
            t = t.reshape((4,) + shp)
        else:
            t = t.reshape(shp[0], 4, shp[1]).transpose(1, 0, 2)
        parts.append(t.reshape(4, nrows, PACK_W))
    cw = g["conv_w"].reshape(3, 4, 128).transpose(1, 0, 2).reshape(4, 1, CONV_SHARD)
    parts.append(jnp.pad(cw, ((0, 0), (0, PACK_ROWS - MAT_ROWS - 1), (0, PACK_W - CONV_SHARD))))
    return jnp.concatenate(parts, axis=1)


def _unpack_grads(red):
    out = {}
    r = 0
    for n, nrows, shp in PACK_SPLIT:
        out[n] = red[r:r + nrows].reshape(shp)
        r += nrows
    out["conv_w"] = red[r, :CONV_SHARD].reshape(3, 128)
    return out


SMALL_SIZES = (("norm_g", D), ("b_gate", 3 * D), ("conv_b", CW), ("q_a_norm_g", QL), ("kv_a_norm_g", KVL),
               ("mla_q_norm_g", QK), ("mla_k_norm_g", QK), ("dil_q_norm_g", NG * HD), ("dil_k_norm_g", NG * HD))
SMALL_ROWS = 88


def _pack_small(per_name):
    flat = jnp.concatenate([per_name[n].reshape(-1).astype(F32) for n, _ in SMALL_SIZES])
    return jnp.pad(flat, (0, SMALL_ROWS * LANE - flat.shape[0])).reshape(SMALL_ROWS, LANE)


def _unpack_small(packed, like):
    out = {}
    flat = packed.reshape(-1)
    r = 0
    for n, sz in SMALL_SIZES:
        out[n] = flat[r:r + NL * sz].reshape(like[n].shape)
        r += NL * sz
    return out


def _adamw_math(w, g, m, v):
    m = ADAM_B1 * m + (1.0 - ADAM_B1) * g
    v = ADAM_B2 * v + (1.0 - ADAM_B2) * jnp.square(g)
    m_hat = m / (1.0 - ADAM_B1 ** ADAM_STEP)
    v_hat = v / (1.0 - ADAM_B2 ** ADAM_STEP)
    delta = -ADAM_LR * (m_hat / (jnp.sqrt(v_hat) + ADAM_EPS) + ADAM_WD * w)
    return delta, m, v


def _adamw(name, w, g, m, v, br, bc=None):
    L, R, C = w.shape
    bc = C if bc is None else bc
    blk = lambda l, i, j: (l, i, j)
    return _pcall(name, _adamw_math, (L, R // br, C // bc), [(t, (None, br, bc), blk) for t in (w, g, m, v)],
                  [((L, R, C), F32, (None, br, bc), blk)] * 3)


ADAM_ROWS = {"w_uq": 256, "w_ukv": 128, "w_out_a": 512, "w_out_b": 512, "w_out_c": 512, "w_o": 256,
             "conv_w": 3}


def kernel(x, norm_g, w_in, b_gate, conv_w, conv_b, q_a_norm_g, w_uq, kv_a_norm_g, w_ukv, mla_q_norm_g, mla_k_norm_g, dil_q_norm_g, dil_k_norm_g, w_out_a, w_out_b, w_out_c, w_o, loss_target, m_norm_g, m_w_in, m_b_gate, m_conv_w, m_conv_b, m_q_a_norm_g, m_w_uq, m_kv_a_norm_g, m_w_ukv, m_mla_q_norm_g, m_mla_k_norm_g, m_dil_q_norm_g, m_dil_k_norm_g, m_w_out_a, m_w_out_b, m_w_out_c, m_w_o, v_norm_g, v_w_in, v_b_gate, v_conv_w, v_conv_b, v_q_a_norm_g, v_w_uq, v_kv_a_norm_g, v_w_ukv, v_mla_q_norm_g, v_mla_k_norm_g, v_dil_q_norm_g, v_dil_k_norm_g, v_w_out_a, v_w_out_b, v_w_out_c, v_w_o):
    W = dict(norm_g=norm_g, w_in=w_in, b_gate=b_gate, conv_w=conv_w, conv_b=conv_b, q_a_norm_g=q_a_norm_g, w_uq=w_uq,
             kv_a_norm_g=kv_a_norm_g, w_ukv=w_ukv, mla_q_norm_g=mla_q_norm_g, mla_k_norm_g=mla_k_norm_g,
             dil_q_norm_g=dil_q_norm_g, dil_k_norm_g=dil_k_norm_g, w_out_a=w_out_a, w_out_b=w_out_b, w_out_c=w_out_c,
             w_o=w_o)
    M = dict(norm_g=m_norm_g, w_in=m_w_in, b_gate=m_b_gate, conv_w=m_conv_w, conv_b=m_conv_b, q_a_norm_g=m_q_a_norm_g,
             w_uq=m_w_uq, kv_a_norm_g=m_kv_a_norm_g, w_ukv=m_w_ukv, mla_q_norm_g=m_mla_q_norm_g,
             mla_k_norm_g=m_mla_k_norm_g, dil_q_norm_g=m_dil_q_norm_g, dil_k_norm_g=m_dil_k_norm_g, w_out_a=m_w_out_a,
             w_out_b=m_w_out_b, w_out_c=m_w_out_c, w_o=m_w_o)
    V = dict(norm_g=v_norm_g, w_in=v_w_in, b_gate=v_b_gate, conv_w=v_conv_w, conv_b=v_conv_b, q_a_norm_g=v_q_a_norm_g,
             w_uq=v_w_uq, kv_a_norm_g=v_kv_a_norm_g, w_ukv=v_w_ukv, mla_q_norm_g=v_mla_q_norm_g,
             mla_k_norm_g=v_mla_k_norm_g, dil_q_norm_g=v_dil_q_norm_g, dil_k_norm_g=v_dil_k_norm_g, w_out_a=v_w_out_a,
             w_out_b=v_w_out_b, w_out_c=v_w_out_c, w_o=v_w_o)
    batch = x.shape[0]
    T = batch * S

    ws = []
    for l in range(NL):
        cont, gath = _all_gather(_w_in_words(w_in[l]), _pack_weights({n: W[n][l] for n in BIG[1:] + ("conv_w",)}))
        full = _unpack_weights(gath)
        pad_qk = lambda t: jnp.pad(t, (0, QKP - QK)).reshape(1, QKP)
        full.update(
            w_in_t=_unpack_w_in(cont),
            norm_g=norm_g[l].reshape(1, D), b_gate=b_gate[l].reshape(1, 3 * D), conv_b=conv_b[l].reshape(1, CW),
            q_a_norm_g=q_a_norm_g[l].reshape(1, QL), kv_a_norm_g=kv_a_norm_g[l].reshape(1, KVL),
            mla_q_norm_g=pad_qk(mla_q_norm_g[l]), mla_k_norm_g=pad_qk(mla_k_norm_g[l]),
            dil_q_norm_g=dil_q_norm_g[l].reshape(NG, 1, HD), dil_k_norm_g=dil_k_norm_g[l].reshape(NG, 1, HD))
        ws.append(full)

    loss, dx, grads = _local_step(x.reshape(T, D), loss_target.reshape(T, D), ws, batch)
    loss = lax.psum(loss[0, 0], ("x", "y", "c"))
    grad_x = dx.reshape(batch, S, D)

    red = []
    for l in range(NL):
        rw, rs = _reduce_scatter(grads[l]["w_in_t"], _pack_grads(grads[l]))
        r = _unpack_grads(rs)
        r["w_in_t"] = rw
        red.append(r)
    G = {n: jnp.stack([red[l][n] for l in range(NL)]) for n in BIG[1:] + ("conv_w",)}
    g_in_t = jnp.stack([red[l]["w_in_t"] for l in range(NL)])
    G["w_in"] = jnp.swapaxes(g_in_t, 1, 2)
    small_g = {n: jnp.stack([grads[l][n].reshape(-1)[:sz] for l in range(NL)]) for n, sz in SMALL_SIZES}
    small_red = _all_reduce_small(_pack_small(small_g))
    G.update(_unpack_small(small_red, {n: W[n] for n in SMALL}))

    delta, new_m, new_v = {}, {}, {}
    for n in BIG[1:] + ("conv_w",):
        delta[n], new_m[n], new_v[n] = _adamw("adamw_" + n, W[n], G[n], M[n], V[n], ADAM_ROWS[n])
    tr = lambda t: jnp.swapaxes(t, 1, 2)
    delta["w_in"], new_m["w_in"], new_v["w_in"] = (
        tr(t) for t in _adamw("adamw_w_in", tr(w_in), g_in_t, tr(m_w_in), tr(v_w_in), WSH, LANE))
    sw, sm, sv = (_pack_small({n: t[n] for n in SMALL})[None] for t in (W, M, V))
    sd, snm, snv = _adamw("adamw_small", sw, small_red[None], sm, sv, SMALL_ROWS)
    like = {n: W[n] for n in SMALL}
    delta.update(_unpack_small(sd[0], like))
    new_m.update(_unpack_small(snm[0], like))
    new_v.update(_unpack_small(snv[0], like))

    return (loss, grad_x, *[G[n] for n in WEIGHTS], *[delta[n] for n in WEIGHTS],
            *[new_m[n] for n in WEIGHTS], *[new_v[n] for n in WEIGHTS])
```

```python
import functools

import numpy as np
import jax
import jax.numpy as jnp
from jax import lax
from jax.experimental import pallas as pl
from jax.experimental.pallas import tpu as pltpu

F32 = jnp.float32
BF16 = jnp.bfloat16

D = 1024
S = 2048
NL = 2
CW = 512
NH = 8
QL = 256
KVL = 128
NOPE = 64
ROPE = 32
VD = 64
QK = NOPE + ROPE
QKP = 128
ROPE_THETA = 10000.0
DIL = ((128, 1), (512, 4), (2048, 16))
NG = 3
DH = 8
HD = 64
DWID = DH * HD
QB = 128
EPS = 1e-6
NIN = 11168
NINP = 11264
O_A, O_CQ, O_CKV, O_KPE, O_BZ, O_DQ, O_DK, O_DV, O_CZ, O_G = 0, 2048, 2304, 2432, 2560, 3072, 4608, 6144, 7680, 8192
KPE_END = 2464
NEG = -1e30
MLA_SCALE = QK ** -0.5
DIL_SCALE = HD ** -0.5
LANE = 128
PACK_W = 512
VMEM_LIMIT = 48 * 1024 * 1024

ADAM_LR = 0.001
ADAM_B1 = 0.9
ADAM_B2 = 0.999
ADAM_EPS = 1e-08
ADAM_WD = 0.01
ADAM_STEP = 10

MESH = pl.DeviceIdType.MESH
BIG = ("w_in", "w_uq", "w_ukv", "w_out_a", "w_out_b", "w_out_c", "w_o")
SMALL = ("norm_g", "b_gate", "conv_b", "q_a_norm_g", "kv_a_norm_g", "mla_q_norm_g", "mla_k_norm_g",
         "dil_q_norm_g", "dil_k_norm_g")
WEIGHTS = ("norm_g", "w_in", "b_gate", "conv_w", "conv_b", "q_a_norm_g", "w_uq", "kv_a_norm_g", "w_ukv",
           "mla_q_norm_g", "mla_k_norm_g", "dil_q_norm_g", "dil_k_norm_g", "w_out_a", "w_out_b", "w_out_c", "w_o")


def _dot(a, b):
    return jnp.dot(a, b, preferred_element_type=F32)


def _dot_nt(a, b):
    return lax.dot_general(a, b, (((1,), (1,)), ((), ())), preferred_element_type=F32)


def _dot_tn(a, b):
    return lax.dot_general(a, b, (((0,), (0,)), ((), ())), preferred_element_type=F32)


def _pcall(name, fn, grid, ins, outs):
    n_in = len(ins)
    n_out = len(outs)
    acc_axis = len(grid) - 1
    is_acc = [len(o) > 4 and o[4] for o in outs]
    outs = [o[:4] for o in outs]

    def body(*refs):
        vals = fn(*[r[...].astype(F32) for r in refs[:n_in]])
        if not isinstance(vals, (tuple, list)):
            vals = (vals,)
        for k in range(n_out):
            r = refs[n_in + k]
            v = vals[k].astype(r.dtype).reshape(r.shape)
            if is_acc[k]:
                first = pl.program_id(acc_axis) == 0

                @pl.when(first)
                def _():
                    r[...] = v

                @pl.when(jnp.logical_not(first))
                def _():
                    r[...] += v
            else:
                r[...] = v

    return pl.pallas_call(
        body,
        grid=grid,
        in_specs=[pl.BlockSpec(bs, im) for _, bs, im in ins],
        out_specs=[pl.BlockSpec(bs, im) for _, _, bs, im in outs],
        out_shape=[jax.ShapeDtypeStruct(sh, dt) for sh, dt, _, _ in outs],
        name=name,
        compiler_params=pltpu.CompilerParams(
            dimension_semantics=("arbitrary",) * len(grid), vmem_limit_bytes=VMEM_LIMIT),
    )(*[a for a, _, _ in ins])


def _mm(name, a, b, *, ta=False, tb=False, out_dtype=F32, add=None, tm=512, tn=1024, tk=1024):
    if ta:
        K, M = a.shape
    else:
        M, K = a.shape
    if tb:
        N, K2 = b.shape
    else:
        K2, N = b.shape
    assert K == K2, (name, a.shape, b.shape)
    tm, tn, tk = min(tm, M), min(tn, N), min(tk, K)
    assert M % tm == 0 and N % tn == 0 and K % tk == 0, (name, M, N, K)
    nk = K // tk
    dims = (((0 if ta else 1,), (1 if tb else 0,)), ((), ()))
    a_spec = pl.BlockSpec((tk, tm), lambda j, i, k: (k, i)) if ta else pl.BlockSpec((tm, tk), lambda j, i, k: (i, k))
    b_spec = pl.BlockSpec((tn, tk), lambda j, i, k: (j, k)) if tb else pl.BlockSpec((tk, tn), lambda j, i, k: (k, j))
    o_spec = pl.BlockSpec((tm, tn), lambda j, i, k: (i, j))
    has_add = add is not None

    def body(*refs):
        a_ref, b_ref = refs[0], refs[1]
        add_ref = refs[2] if has_add else None
        o_ref = refs[3] if has_add else refs[2]
        p = lax.dot_general(a_ref[...].astype(BF16), b_ref[...].astype(BF16), dims, preferred_element_type=F32)
        if nk == 1:
            if has_add:
                p = p + add_ref[...]
            o_ref[...] = p.astype(out_dtype)
        else:
            acc = refs[-1]
            k = pl.program_id(2)

            @pl.when(k == 0)
            def _():
                acc[...] = p

            @pl.when(k > 0)
            def _():
                acc[...] += p

            @pl.when(k == nk - 1)
            def _():
                r = acc[...]
                if has_add:
                    r = r + add_ref[...]
                o_ref[...] = r.astype(out_dtype)

    in_specs = [a_spec, b_spec] + ([o_spec] if has_add else [])
    args = [a, b] + ([add] if has_add else [])
    return pl.pallas_call(
        body,
        grid=(N // tn, M // tm, nk),
        in_specs=in_specs,
        out_specs=o_spec,
        out_shape=jax.ShapeDtypeStruct((M, N), out_dtype),
        scratch_shapes=[pltpu.VMEM((tm, tn), F32)] if nk > 1 else [],
        name=name,
        compiler_params=pltpu.CompilerParams(
            dimension_semantics=("arbitrary", "arbitrary", "arbitrary"), vmem_limit_bytes=VMEM_LIMIT),
    )(*args)


def _vjp_of(f, n_diff):
    def g(*args, n_prim):
        prim = args[:n_diff]
        consts = args[n_diff:n_prim]
        cts = args[n_prim:]
        _, pull = jax.vjp(lambda *p: f(*p, *consts), *prim)
        out = jax.eval_shape(lambda *p: f(*p, *consts), *prim)
        if isinstance(out, (tuple, list)):
            cts = tuple(c.astype(o.dtype) for c, o in zip(cts, out))
        else:
            cts = cts[0].astype(out.dtype)
        return pull(cts)
    return g


def _rms(x, g, n=None):
    n = x.shape[-1] if n is None else n
    ms = jnp.sum(x * x, axis=-1, keepdims=True) / n
    return x * lax.rsqrt(ms + EPS) * g


def _silu(z):
    return z * jax.nn.sigmoid(z)


def _roll_rows(u, k):
    n = u.shape[0]
    r = pltpu.roll(u, k % n, 0)
    t = lax.broadcasted_iota(jnp.int32, u.shape, 0)
    if k > 0:
        return jnp.where(t >= k, r, 0.0)
    return jnp.where(t < n + k, r, 0.0)


@functools.partial(jax.custom_vjp, nondiff_argnums=(1,))
def _shift(u, k):
    return _roll_rows(u, k)


def _shift_fwd(u, k):
    return _roll_rows(u, k), None


def _shift_bwd(k, _, g):
    return (_roll_rows(g, -k),)


_shift.defvjp(_shift_fwd, _shift_bwd)


@functools.partial(jax.custom_vjp, nondiff_argnums=(1,))
def _lane_roll(u, k):
    return pltpu.roll(u, k % LANE, 1)


def _lane_roll_fwd(u, k):
    return pltpu.roll(u, k % LANE, 1), None


def _lane_roll_bwd(k, _, g):
    return (pltpu.roll(g, (-k) % LANE, 1),)


_lane_roll.defvjp(_lane_roll_fwd, _lane_roll_bwd)


def _conv_math(ab, ac, ax, az, cw, cb):
    u = ac * ax
    conv = cb + _shift(u, 2) * cw[0:1] + _shift(u, 1) * cw[1:2] + u * cw[2:3]
    return ab * conv * _silu(az)


def _mla_pre_math(cq, ckv, gq, gkv):
    return _rms(cq, gq), _rms(ckv, gkv)


def _rope_math(q, kn, kpe, gq, gk, c, s1, s2):
    lane = lax.broadcasted_iota(jnp.int32, kpe.shape, 1)
    pe = _lane_roll(jnp.where(lane < ROPE, kpe, 0.0), NOPE)

    def one(t, g):
        tn = _rms(t, g, QK)
        return tn * c + _lane_roll(tn, -16) * s1 + _lane_roll(tn, 16) * s2

    qs, ks = [], []
    for h in range(NH):
        sl = slice(h * QKP, (h + 1) * QKP)
        qs.append(one(q[:, sl], gq))
        ks.append(one(kn[:, sl] + pe, gk))
    return jnp.concatenate(qs, axis=1), jnp.concatenate(ks, axis=1)


def _gate_math(o, z):
    return o * _silu(z)


def _mergec_math(o0, o1, o2, l0, l1, l2, cz):
    m = lax.stop_gradient(jnp.maximum(jnp.maximum(l0, l1), l2))
    e0, e1, e2 = jnp.exp(l0 - m), jnp.exp(l1 - m), jnp.exp(l2 - m)
    den = e0 + e1 + e2
    oc = (e0 / den) * o0 + (e1 / den) * o1 + (e2 / den) * o2
    return oc * _silu(cz)


def _merge_math(g0, g1, g2, b0, b1, b2, pa, pb, pc):
    return (jax.nn.sigmoid(g0 + b0) * pa + jax.nn.sigmoid(g1 + b1) * pb) + jax.nn.sigmoid(g2 + b2) * pc


MLA_T = 256
MLA_UNROLL = True


def _mla_fwd(q, k, v):
    B = q.shape[0]
    T = MLA_T
    NB = S // T

    def body(q_ref, k_ref, v_ref, o_ref, l_ref):
        row = lax.broadcasted_iota(jnp.int32, (T, T), 0)
        col = lax.broadcasted_iota(jnp.int32, (T, T), 1)
        lo = _lo_mask((T, LANE))

        for qi in range(NB):
            qb = q_ref[qi * T:(qi + 1) * T, :]

            def step(j, carry, diagonal):
                m, l, acc = carry
                off = pl.multiple_of(j * T, T)
                kb = k_ref[pl.ds(off, T), :]
                vb = v_ref[pl.ds(off, T), :]
                ss = []
                for e in (0, 1):
                    se = _dot_nt(qb[:, e * QKP:(e + 1) * QKP], kb[:, e * QKP:(e + 1) * QKP]) * MLA_SCALE
                    ss.append(jnp.where(col <= row, se, NEG) if diagonal else se)
                s = jnp.concatenate(ss, axis=0)
                m_new = jnp.maximum(m, jnp.max(s, axis=-1, keepdims=True))
                a = jnp.exp(m - m_new)
                p = jnp.exp(s - m_new)
                l = a * l + jnp.sum(p, axis=-1, keepdims=True)
                acc = a * acc + _dot(p.astype(BF16), vb)
                return m_new, l, acc

            init = (jnp.full((2 * T, 1), NEG, F32), jnp.zeros((2 * T, 1), F32), jnp.zeros((2 * T, LANE), F32))
            carry = lax.fori_loop(0, qi, functools.partial(step, diagonal=False), init, unroll=MLA_UNROLL)
            m, l, acc = step(qi, carry, True)
            o = acc / l
            lse = m + jnp.log(l)
            o_ref[qi * T:(qi + 1) * T, :] = jnp.where(lo, o[:T], o[T:])
            l_ref[qi * T:(qi + 1) * T, :] = jnp.where(lo, lse[:T], lse[T:])

    def spec(w):
        return pl.BlockSpec((None, S, w), lambda b, hp: (b, 0, hp))

    return pl.pallas_call(
        body,
        grid=(B, NH // 2),
        in_specs=[spec(2 * QKP), spec(2 * QKP), spec(LANE)],
        out_specs=[spec(LANE), spec(LANE)],
        out_shape=[jax.ShapeDtypeStruct((B, S, NH * VD), F32)] * 2,
        name="mla_attn_fwd",
        compiler_params=pltpu.CompilerParams(dimension_semantics=("arbitrary",) * 2, vmem_limit_bytes=VMEM_LIMIT),
    )(q, k, v)


def _mla_bwd(q, k, v, do, o, lse):
    B = q.shape[0]
    T = MLA_T
    NB = S // T

    def body(q_ref, k_ref, v_ref, do_ref, o_ref, l_ref, dq_ref, dk_ref, dv_ref, delta_ref):
        delta_ref[...] = _head_sum(do_ref[...] * o_ref[...])
        row = lax.broadcasted_iota(jnp.int32, (T, T), 0)
        col = lax.broadcasted_iota(jnp.int32, (T, T), 1)
        lo = _lo_mask((T, LANE))

        for j in range(NB):
            krows = slice(j * T, (j + 1) * T)
            kb = k_ref[krows, :]
            vb = v_ref[krows, :]
            dk = [jnp.zeros((T, QKP), F32), jnp.zeros((T, QKP), F32)]
            dv = jnp.zeros((T, LANE), F32)
            for i in range(j, NB):
                qrows = slice(i * T, (i + 1) * T)
                qb = q_ref[qrows, :]
                do2 = _stack_heads(do_ref[qrows, :], lo).astype(BF16)
                lb = l_ref[qrows, :]
                db = delta_ref[qrows, :]
                dp2 = _dot_nt(do2, vb)
                for e in (0, 1):
                    cols = slice(e * QKP, (e + 1) * QKP)
                    qe, ke = qb[:, cols], kb[:, cols]
                    s = _dot_nt(qe, ke) * MLA_SCALE
                    if i == j:
                        s = jnp.where(col <= row, s, NEG)
                    p = jnp.exp(s - lb[:, e * HD:e * HD + 1])
                    dv = dv + _dot_tn(p.astype(BF16), do2[e * T:(e + 1) * T])
                    ds = (p * (dp2[e * T:(e + 1) * T] - db[:, e * HD:e * HD + 1]) * MLA_SCALE).astype(BF16)
                    dk[e] = dk[e] + _dot_tn(ds, qe)
                    if j == 0:
                        dq_ref[qrows, cols] = _dot(ds, ke)
                    else:
                        dq_ref[qrows, cols] += _dot(ds, ke)
            dk_ref[krows, 0:QKP] = dk[0]
            dk_ref[krows, QKP:2 * QKP] = dk[1]
            dv_ref[krows, :] = dv

    def spec(w):
        return pl.BlockSpec((None, S, w), lambda b, hp: (b, 0, hp))

    return pl.pallas_call(
        body,
        grid=(B, NH // 2),
        in_specs=[spec(2 * QKP), spec(2 * QKP), spec(LANE), spec(LANE), spec(LANE), spec(LANE)],
        out_specs=[spec(2 * QKP), spec(2 * QKP), spec(LANE)],
        out_shape=[jax.ShapeDtypeStruct((B, S, NH * QKP), F32), jax.ShapeDtypeStruct((B, S, NH * QKP), F32),
                   jax.ShapeDtypeStruct((B, S, NH * VD), F32)],
        scratch_shapes=[pltpu.VMEM((S, LANE), F32)],
        name="mla_attn_bwd",
        compiler_params=pltpu.CompilerParams(dimension_semantics=("arbitrary",) * 2, vmem_limit_bytes=VMEM_LIMIT),
    )(q, k, v, do, o, lse)


def _lo_mask(shape):
    return lax.broadcasted_iota(jnp.int32, shape, len(shape) - 1) < HD


def _head_sum(u):
    r = lax.broadcasted_iota(jnp.int32, (LANE, LANE), 0) < HD
    c = lax.broadcasted_iota(jnp.int32, (LANE, LANE), 1) < HD
    ones = jnp.where(r == c, 1.0, 0.0).astype(BF16)
    hi = u.astype(BF16)
    lo = (u - hi.astype(F32)).astype(BF16)
    return _dot(hi, ones) + _dot(lo, ones)


def _rms2(x, g):
    return x * lax.rsqrt(_head_sum(x * x) / HD + EPS) * g


def _dil_bias(t_ref, gi, d):
    qq = lax.broadcasted_iota(jnp.int32, (QB, QB), 0)
    kk = lax.broadcasted_iota(jnp.int32, (QB, QB), 1)
    jc = (qq - kk).astype(F32)
    rows = []
    for e in (0, 1):
        sl = t_ref[2 * gi + e:2 * gi + e + 1, :] * float(d)
        bp = jnp.where(kk >= qq, -sl * (jc + float(QB)), NEG)
        bc = jnp.where(kk <= qq, -sl * jc, NEG)
        rows.append(jnp.concatenate([bp, bc], axis=1))
    return jnp.concatenate(rows, axis=0)


def _dil_rows(cur, d):
    return pl.ds(cur, QB, stride=d) if d > 1 else pl.ds(pl.multiple_of(cur, QB), QB)


def _dil_walk(d, block, full):
    if d == 1:
        block(0, None)

        def body(i, c):
            block(i * QB, (i - 1) * QB)
            return c
        lax.fori_loop(1, S // QB, body, 0, unroll=True if full else 5)
    elif d == 16:
        def body(r, c):
            block(r, None)
            return c
        lax.fori_loop(0, d, body, 0, unroll=True if full else 4)
    else:
        nb = S // d // QB

        def cls(r, c):
            block(r, None)

            def body(i, c2):
                block(r + i * QB * d, r + (i - 1) * QB * d)
                return c2
            lax.fori_loop(1, nb, body, 0, unroll=True)
            return c
        lax.fori_loop(0, d, cls, 0, unroll=full)


def _stack_heads(x, lo):
    return jnp.concatenate([jnp.where(lo, x, 0.0), jnp.where(lo, 0.0, x)], axis=0)


def _dilc_fwd(proj3, gq, gk, tab):
    B = proj3.shape[0]

    def body(q_ref, k_ref, v_ref, cz_ref, gq_ref, gk_ref, t_ref, y_ref, o_ref, l_ref, qs, ks, vs):
        g = pl.program_id(2)
        lo = _lo_mask((QB, LANE))

        def group(gi):
            d = DIL[gi][1]
            qs[...] = _rms2(q_ref[...].astype(F32), gq_ref[gi:gi + 1, :])
            ks[...] = _rms2(k_ref[...].astype(F32), gk_ref[gi:gi + 1, :])
            vs[...] = v_ref[...].astype(F32)
            bias = _dil_bias(t_ref, gi, d)

            def block(cur, prev):
                rows = _dil_rows(cur, d)
                q2 = _stack_heads(qs[rows, :], lo).astype(BF16)
                kc, vc = ks[rows, :], vs[rows, :]
                if prev is None:
                    kcat, vcat, b = kc, vc, bias[:, QB:]
                else:
                    prow = _dil_rows(prev, d)
                    kcat = jnp.concatenate([ks[prow, :], kc], axis=0)
                    vcat = jnp.concatenate([vs[prow, :], vc], axis=0)
                    b = bias
                s = _dot_nt(q2, kcat.astype(BF16)) * DIL_SCALE + b
                m = jnp.max(s, axis=-1, keepdims=True)
                p = jnp.exp(s - m)
                l = jnp.sum(p, axis=-1, keepdims=True)
                o = _dot(p.astype(BF16), vcat.astype(BF16)) / l
                lse = m + jnp.log(l)
                o_ref[gi, rows, :] = jnp.where(lo, o[:QB], o[QB:])
                l_ref[gi, rows, :] = jnp.where(lo, lse[:QB], lse[QB:])

            _dil_walk(d, block, True)

        for gi in range(NG):
            pl.when(g == gi)(functools.partial(group, gi))

        @pl.when(g == NG - 1)
        def _():
            y_ref[...] = _mergec_math(o_ref[0], o_ref[1], o_ref[2], l_ref[0], l_ref[1], l_ref[2],
                                      cz_ref[...].astype(F32)).astype(BF16)

    def col(base):
        return pl.BlockSpec((None, S, LANE), lambda b, hp, g: (b, 0, base // LANE + 4 * g + hp))

    gspec = pl.BlockSpec((NG, LANE), lambda b, hp, g: (0, 0))
    saved = pl.BlockSpec((NG, None, S, LANE), lambda b, hp, g: (0, b, 0, hp))
    return pl.pallas_call(
        body,
        grid=(B, 4, NG),
        in_specs=[col(O_DQ), col(O_DK), col(O_DV),
                  pl.BlockSpec((None, S, LANE), lambda b, hp, g: (b, 0, O_CZ // LANE + hp)),
                  gspec, gspec, pl.BlockSpec((None, 8, LANE), lambda b, hp, g: (hp, 0, 0))],
        out_specs=[pl.BlockSpec((None, S, LANE), lambda b, hp, g: (b, 0, hp)), saved, saved],
        out_shape=[jax.ShapeDtypeStruct((B, S, DWID), BF16), jax.ShapeDtypeStruct((NG, B, S, DWID), F32),
                   jax.ShapeDtypeStruct((NG, B, S, DWID), F32)],
        scratch_shapes=[pltpu.VMEM((S, LANE), F32)] * 3,
        name="dil_mixer_fwd",
        compiler_params=pltpu.CompilerParams(dimension_semantics=("arbitrary",) * 3, vmem_limit_bytes=VMEM_LIMIT),
    )(proj3, proj3, proj3, proj3, gq, gk, tab)


def _dilc_bwd(proj3, gq, gk, tab, o_all, l_all, d_yc):
    B = proj3.shape[0]

    def body(q_ref, k_ref, v_ref, cz_ref, gq_ref, gk_ref, t_ref, o_ref, l_ref, dy_ref,
             dq_out, dk_out, dv_out, dcz_out, dgq_out, dgk_out, qs, ks, vs, dos, dls, dqs, dks, dvs):
        g = pl.program_id(2)
        lo = _lo_mask((QB, LANE))

        @pl.when(jnp.logical_and(jnp.logical_and(pl.program_id(0) == 0, pl.program_id(1) == 0), g == 0))
        def _():
            dgq_out[...] = jnp.zeros((NG, LANE), F32)
            dgk_out[...] = jnp.zeros((NG, LANE), F32)

        def group(gi):
            d = DIL[gi][1]
            ls = [l_ref[j] for j in range(NG)]
            m = jnp.maximum(jnp.maximum(ls[0], ls[1]), ls[2])
            es = [jnp.exp(t - m) for t in ls]
            den = (es[0] + es[1]) + es[2]
            al = [e / den for e in es]
            os_ = [o_ref[j] for j in range(NG)]
            oc = (al[0] * os_[0] + al[1] * os_[1]) + al[2] * os_[2]
            cz = cz_ref[...].astype(F32)
            sg = jax.nn.sigmoid(cz)
            dy = dy_ref[...]
            d_oc = dy * (cz * sg)
            dcz_out[...] = (dy * oc * (sg * (1.0 + cz * (1.0 - sg)))).astype(BF16)
            ts = [_head_sum(d_oc * os_[j]) for j in range(NG)]
            tbar = (al[0] * ts[0] + al[1] * ts[1]) + al[2] * ts[2]
            dos[...] = al[gi] * d_oc
            dls[...] = al[gi] * (ts[gi] - tbar)

            qs[...] = _rms2(q_ref[...].astype(F32), gq_ref[gi:gi + 1, :])
            ks[...] = _rms2(k_ref[...].astype(F32), gk_ref[gi:gi + 1, :])
            vs[...] = v_ref[...].astype(F32)
            dks[...] = jnp.zeros((S, LANE), F32)
            dvs[...] = jnp.zeros((S, LANE), F32)
            bias = _dil_bias(t_ref, gi, d)

            def block(cur, prev):
                rows = _dil_rows(cur, d)
                q2 = _stack_heads(qs[rows, :], lo).astype(BF16)
                dob = dos[rows, :]
                do2 = _stack_heads(dob, lo).astype(BF16)
                kc, vc = ks[rows, :], vs[rows, :]
                if prev is None:
                    kcat, vcat, b = kc, vc, bias[:, QB:]
                else:
                    prow = _dil_rows(prev, d)
                    kcat = jnp.concatenate([ks[prow, :], kc], axis=0)
                    vcat = jnp.concatenate([vs[prow, :], vc], axis=0)
                    b = bias
                kcat = kcat.astype(BF16)
                vcat = vcat.astype(BF16)
                lse_b = l_ref[gi, rows, :]
                corr_b = dls[rows, :] - _head_sum(dob * o_ref[gi, rows, :])
                lse2 = jnp.concatenate([lse_b[:, 0:1], lse_b[:, HD:HD + 1]], axis=0)
                corr2 = jnp.concatenate([corr_b[:, 0:1], corr_b[:, HD:HD + 1]], axis=0)
                s = _dot_nt(q2, kcat) * DIL_SCALE + b
                p = jnp.exp(s - lse2)
                ds = (p * (_dot_nt(do2, vcat) + corr2) * DIL_SCALE).astype(BF16)
                dq2 = _dot(ds, kcat)
                dqs[rows, :] = jnp.where(lo, dq2[:QB], dq2[QB:])
                dk = _dot_tn(ds, q2)
                dv = _dot_tn(p.astype(BF16), do2)
                if prev is None:
                    dks[rows, :] += dk
                    dvs[rows, :] += dv
                else:
                    dks[prow, :] += dk[:QB]
                    dvs[prow, :] += dv[:QB]
                    dks[rows, :] += dk[QB:]
                    dvs[rows, :] += dv[QB:]

            _dil_walk(d, block, False)

            _, pull_q = jax.vjp(_rms2, q_ref[...].astype(F32), gq_ref[gi:gi + 1, :])
            dxq, dgq = pull_q(dqs[...])
            dq_out[...] = dxq.astype(BF16)
            dgq_out[gi:gi + 1, :] += dgq
            _, pull_k = jax.vjp(_rms2, k_ref[...].astype(F32), gk_ref[gi:gi + 1, :])
            dxk, dgk = pull_k(dks[...])
            dk_out[...] = dxk.astype(BF16)
            dgk_out[gi:gi + 1, :] += dgk
            dv_out[...] = dvs[...].astype(BF16)

        for gi in range(NG):
            pl.when(g == gi)(functools.partial(group, gi))

    def col(base):
        return pl.BlockSpec((None, S, LANE), lambda b, hp, g: (b, 0, base // LANE + 4 * g + hp))

    gspec = pl.BlockSpec((NG, LANE), lambda b, hp, g: (0, 0))
    saved = pl.BlockSpec((NG, None, S, LANE), lambda b, hp, g: (0, b, 0, hp))
    per_pair = pl.BlockSpec((None, S, LANE), lambda b, hp, g: (b, 0, hp))
    dcol = pl.BlockSpec((None, S, LANE), lambda b, hp, g: (b, 0, 4 * g + hp))
    return pl.pallas_call(
        body,
        grid=(B, 4, NG),
        in_specs=[col(O_DQ), col(O_DK), col(O_DV),
                  pl.BlockSpec((None, S, LANE), lambda b, hp, g: (b, 0, O_CZ // LANE + hp)),
                  gspec, gspec, pl.BlockSpec((None, 8, LANE), lambda b, hp, g: (hp, 0, 0)),
                  saved, saved, per_pair],
        out_specs=[dcol, dcol, dcol, per_pair, gspec, gspec],
        out_shape=[jax.ShapeDtypeStruct((B, S, NG * DWID), BF16)] * 3
        + [jax.ShapeDtypeStruct((B, S, DWID), BF16), jax.ShapeDtypeStruct((NG, LANE), F32),
           jax.ShapeDtypeStruct((NG, LANE), F32)],
        scratch_shapes=[pltpu.VMEM((S, LANE), F32)] * 8,
        name="dil_mixer_bwd",
        compiler_params=pltpu.CompilerParams(dimension_semantics=("arbitrary",) * 3, vmem_limit_bytes=VMEM_LIMIT),
    )(proj3, proj3, proj3, proj3, gq, gk, tab, o_all, l_all, d_yc)


def _dil_slopes():
    slopes = (2.0 ** (-8.0 * np.arange(1, NG * DH + 1, dtype=np.float32) / (NG * DH))).astype(np.float32).reshape(NG, DH)
    tab = np.zeros((4, 8, LANE), np.float32)
    for hp in range(4):
        for gi in range(NG):
            for e in (0, 1):
                tab[hp, 2 * gi + e, :] = slopes[gi, 2 * hp + e]
    return jnp.asarray(tab)


def _rope_tables():
    inv = ROPE_THETA ** (-jnp.arange(0, ROPE, 2, dtype=F32) / ROPE)
    ang = jnp.arange(S, dtype=F32)[:, None] * inv[None, :]
    cos, sin = jnp.cos(ang), jnp.sin(ang)
    z16 = jnp.zeros((S, 16), F32)
    c = jnp.concatenate([jnp.ones((S, NOPE), F32), cos, cos, jnp.zeros((S, 32), F32)], axis=1)
    s1 = jnp.concatenate([jnp.zeros((S, NOPE), F32), -sin, z16, jnp.zeros((S, 32), F32)], axis=1)
    s2 = jnp.concatenate([jnp.zeros((S, NOPE), F32), z16, sin, jnp.zeros((S, 32), F32)], axis=1)
    return c, s1, s2


def _pad_heads_uq(w):
    return jnp.pad(w.reshape(QL, NH, QK), ((0, 0), (0, 0), (0, QKP - QK))).reshape(QL, NH * QKP)


def _unpad_heads_uq(g):
    return g.reshape(QL, NH, QKP)[:, :, :QK].reshape(QL, NH * QK)


def _split_ukv(w):
    w3 = w.reshape(KVL, NH, NOPE + VD)
    uk = jnp.pad(w3[:, :, :NOPE], ((0, 0), (0, 0), (0, QKP - NOPE))).reshape(KVL, NH * QKP)
    return uk, w3[:, :, NOPE:].reshape(KVL, NH * VD)


def _join_ukv(guk, guv):
    return jnp.concatenate([guk.reshape(KVL, NH, QKP)[:, :, :NOPE], guv.reshape(KVL, NH, VD)],
                           axis=-1).reshape(KVL, NH * (NOPE + VD))


BR = 512
BRM = 256


def _layer_fwd(x, w, tabs, batch):
    T = batch * S
    rope_c, rope_s1, rope_s2, dil_tab = tabs
    res = {"x": x}
    row = lambda c: (lambda i: (i, c))
    fix = lambda i: (0, 0)

    h = _pcall("norm_fwd", _rms, (T // BR,),
               [(x, (BR, D), row(0)), (w["norm_g"], (1, D), fix)],
               [((T, D), BF16, (BR, D), row(0))])[0]
    proj = _mm("in_proj", h, w["w_in_t"], tb=True, out_dtype=BF16, tm=512, tn=1024)
    res["h"], res["proj"] = h, proj
    proj3 = proj.reshape(batch, S, NINP)

    cblk = lambda s: (lambda j, b: (b, 0, 4 * s + j))
    y_a = _pcall("conv_fwd", _conv_math, (4, batch),
                 [(proj3, (None, S, LANE), cblk(0)), (proj3, (None, S, LANE), cblk(1)),
                  (proj3, (None, S, LANE), cblk(2)), (proj3, (None, S, LANE), cblk(3)),
                  (w["conv_w"], (3, LANE), lambda j, b: (0, j)), (w["conv_b"], (1, LANE), lambda j, b: (0, j))],
                 [((batch, S, CW), BF16, (None, S, LANE), lambda j, b: (b, 0, j))])[0].reshape(T, CW)
    res["y_a"] = y_a

    cqn, ckvn = _pcall("mla_pre_fwd", _mla_pre_math, (T // BR,),
                       [(proj, (BR, QL), row(O_CQ // QL)), (proj, (BR, KVL), row(O_CKV // KVL)),
                        (w["q_a_norm_g"], (1, QL), fix), (w["kv_a_norm_g"], (1, KVL), fix)],
                       [((T, QL), BF16, (BR, QL), row(0)), ((T, KVL), BF16, (BR, KVL), row(0))])
    w_uq_p = _pad_heads_uq(w["w_uq"])
    w_uk, w_uv = _split_ukv(w["w_ukv"])
    q = _mm("uq", cqn, w_uq_p, out_dtype=BF16)
    kn = _mm("uk", ckvn, w_uk, out_dtype=BF16)
    v = _mm("uv", ckvn, w_uv, out_dtype=BF16)
    nrr = S // BR
    tab_row = lambda i: (i % nrr, 0)
    qr, kr = _pcall("rope_fwd", _rope_math, (T // BR,),
                    [(q, (BR, NH * QKP), row(0)), (kn, (BR, NH * QKP), row(0)), (proj, (BR, LANE), row(O_KPE // LANE)),
                     (w["mla_q_norm_g"], (1, QKP), fix), (w["mla_k_norm_g"], (1, QKP), fix),
                     (rope_c, (BR, QKP), tab_row), (rope_s1, (BR, QKP), tab_row), (rope_s2, (BR, QKP), tab_row)],
                    [((T, NH * QKP), BF16, (BR, NH * QKP), row(0))] * 2)
    qr = qr.reshape(batch, S, NH * QKP)
    kr = kr.reshape(batch, S, NH * QKP)
    v = v.reshape(batch, S, NH * VD)
    o_b, l_b = _mla_fwd(qr, kr, v)
    ob2 = o_b.reshape(T, NH * VD)
    y_b = _pcall("gateb_fwd", _gate_math, (T // BR,),
                 [(ob2, (BR, 512), row(0)), (proj, (BR, 512), row(O_BZ // 512))],
                 [((T, 512), BF16, (BR, 512), row(0))])[0]
    res.update(cqn=cqn, ckvn=ckvn, q=q, kn=kn, qr=qr, kr=kr, v=v, o_b=o_b, l_b=l_b, ob2=ob2, y_b=y_b,
               w_uq_p=w_uq_p, w_uk=w_uk, w_uv=w_uv)

    gq2 = jnp.tile(w["dil_q_norm_g"].reshape(NG, HD), (1, 2))
    gk2 = jnp.tile(w["dil_k_norm_g"].reshape(NG, HD), (1, 2))
    y_c, o_all, l_all = _dilc_fwd(proj3, gq2, gk2, dil_tab)
    y_c = y_c.reshape(T, DWID)
    res.update(o_all=o_all, l_all=l_all, y_c=y_c)

    pa = _mm("out_a", y_a, w["w_out_a"], out_dtype=BF16)
    pb = _mm("out_b", y_b, w["w_out_b"], out_dtype=BF16)
    pc = _mm("out_c", y_c, w["w_out_c"], out_dtype=BF16)
    merged = _pcall("merge_fwd", _merge_math, (T // BRM,),
                    [(proj, (BRM, D), row(O_G // D + s)) for s in range(3)]
                    + [(w["b_gate"], (1, D), (lambda s: (lambda i: (0, s)))(s)) for s in range(3)]
                    + [(t, (BRM, D), row(0)) for t in (pa, pb, pc)],
                    [((T, D), BF16, (BRM, D), row(0))])[0]
    out = _mm("o_proj", merged, w["w_o"], add=x)
    res.update(pa=pa, pb=pb, pc=pc, merged=merged)
    return out, res


def _norm_bwd_math(x, g, dh, dy):
    _, pull = jax.vjp(_rms, x, g)
    dx, dg = pull(dh)
    return dx + dy, dg


def _layer_bwd(dy, w, res, tabs, batch):
    T = batch * S
    rope_c, rope_s1, rope_s2, dil_tab = tabs
    row = lambda c: (lambda i: (i, c))
    fix = lambda i: (0, 0)
    x, proj, h = res["x"], res["proj"], res["h"]
    proj3 = proj.reshape(batch, S, NINP)
    g = {}

    d_merged = _mm("o_proj_dx", dy, w["w_o"], tb=True)
    g["w_o"] = _mm("o_proj_dw", res["merged"], dy, ta=True, tm=1024)

    merge_bwd = functools.partial(_vjp_of(_merge_math, 9), n_prim=9)
    dg0, dg1, dg2, db0, db1, db2, dpa, dpb, dpc = _pcall(
        "merge_bwd", merge_bwd, (T // BRM,),
        [(proj, (BRM, D), row(O_G // D + s)) for s in range(3)]
        + [(w["b_gate"], (1, D), (lambda s: (lambda i: (0, s)))(s)) for s in range(3)]
        + [(t, (BRM, D), row(0)) for t in (res["pa"], res["pb"], res["pc"])]
        + [(d_merged, (BRM, D), row(0))],
        [((T, D), BF16, (BRM, D), row(0))] * 3 + [((1, D), F32, (1, D), fix, True)] * 3
        + [((T, D), BF16, (BRM, D), row(0))] * 3)
    g["b_gate"] = jnp.concatenate([db0, db1, db2], axis=1)

    d_ya = _mm("out_a_dx", dpa, w["w_out_a"], tb=True)
    d_yb = _mm("out_b_dx", dpb, w["w_out_b"], tb=True)
    d_yc = _mm("out_c_dx", dpc, w["w_out_c"], tb=True)
    g["w_out_a"] = _mm("out_a_dw", res["y_a"], dpa, ta=True)
    g["w_out_b"] = _mm("out_b_dw", res["y_b"], dpb, ta=True)
    g["w_out_c"] = _mm("out_c_dw", res["y_c"], dpc, ta=True)

    cblk = lambda s: (lambda j, b: (b, 0, 4 * s + j))
    oblk = lambda j, b: (b, 0, j)
    conv_bwd = functools.partial(_vjp_of(_conv_math, 6), n_prim=6)
    d_ab, d_ac, d_ax, d_az, g["conv_w"], g["conv_b"] = _pcall(
        "conv_bwd", conv_bwd, (4, batch),
        [(proj3, (None, S, LANE), cblk(s)) for s in range(4)]
        + [(w["conv_w"], (3, LANE), lambda j, b: (0, j)), (w["conv_b"], (1, LANE), lambda j, b: (0, j)),
           (d_ya.reshape(batch, S, CW), (None, S, LANE), oblk)],
        [((batch, S, CW), BF16, (None, S, LANE), oblk)] * 4
        + [((3, CW), F32, (3, LANE), lambda j, b: (0, j), True), ((1, CW), F32, (1, LANE), lambda j, b: (0, j), True)])

    gate_bwd = functools.partial(_vjp_of(_gate_math, 2), n_prim=2)
    d_ob, d_bz = _pcall("gateb_bwd", gate_bwd, (T // BR,),
                        [(res["ob2"], (BR, 512), row(0)), (proj, (BR, 512), row(O_BZ // 512)), (d_yb, (BR, 512), row(0))],
                        [((T, 512), F32, (BR, 512), row(0)), ((T, 512), BF16, (BR, 512), row(0))])
    dqr, dkr, dv = _mla_bwd(res["qr"], res["kr"], res["v"], d_ob.reshape(batch, S, NH * VD), res["o_b"], res["l_b"])
    nrr = S // BR
    tab_row = lambda i: (i % nrr, 0)
    rope_bwd = functools.partial(_vjp_of(_rope_math, 5), n_prim=8)
    d_q, d_kn, d_kpe_p, g["mla_q_norm_g"], g["mla_k_norm_g"] = _pcall(
        "rope_bwd", rope_bwd, (T // BR,),
        [(res["q"], (BR, NH * QKP), row(0)), (res["kn"], (BR, NH * QKP), row(0)), (proj, (BR, LANE), row(O_KPE // LANE)),
         (w["mla_q_norm_g"], (1, QKP), fix), (w["mla_k_norm_g"], (1, QKP), fix),
         (rope_c, (BR, QKP), tab_row), (rope_s1, (BR, QKP), tab_row), (rope_s2, (BR, QKP), tab_row),
         (dqr.reshape(T, NH * QKP), (BR, NH * QKP), row(0)), (dkr.reshape(T, NH * QKP), (BR, NH * QKP), row(0))],
        [((T, NH * QKP), BF16, (BR, NH * QKP), row(0))] * 2 + [((T, LANE), BF16, (BR, LANE), row(0))]
        + [((1, QKP), F32, (1, QKP), fix, True)] * 2)
    dv = dv.reshape(T, NH * VD)
    d_cqn = _mm("uq_dx", d_q, res["w_uq_p"], tb=True)
    d_ckvn = _mm("uk_dx", d_kn, res["w_uk"], tb=True)
    d_ckvn = _mm("uv_dx", dv, res["w_uv"], tb=True, add=d_ckvn)
    g["w_uq"] = _unpad_heads_uq(_mm("uq_dw", res["cqn"], d_q, ta=True))
    g["w_ukv"] = _join_ukv(_mm("uk_dw", res["ckvn"], d_kn, ta=True), _mm("uv_dw", res["ckvn"], dv, ta=True))
    pre_bwd = functools.partial(_vjp_of(_mla_pre_math, 4), n_prim=4)
    d_cq, d_ckv, g["q_a_norm_g"], g["kv_a_norm_g"] = _pcall(
        "mla_pre_bwd", pre_bwd, (T // BR,),
        [(proj, (BR, QL), row(O_CQ // QL)), (proj, (BR, KVL), row(O_CKV // KVL)),
         (w["q_a_norm_g"], (1, QL), fix), (w["kv_a_norm_g"], (1, KVL), fix),
         (d_cqn, (BR, QL), row(0)), (d_ckvn, (BR, KVL), row(0))],
        [((T, QL), BF16, (BR, QL), row(0)), ((T, KVL), BF16, (BR, KVL), row(0)),
         ((1, QL), F32, (1, QL), fix, True), ((1, KVL), F32, (1, KVL), fix, True)])

    gq2 = jnp.tile(w["dil_q_norm_g"].reshape(NG, HD), (1, 2))
    gk2 = jnp.tile(w["dil_k_norm_g"].reshape(NG, HD), (1, 2))
    d_dq, d_dk, d_dv, d_cz, dgq, dgk = _dilc_bwd(proj3, gq2, gk2, dil_tab, res["o_all"], res["l_all"],
                                                 d_yc.reshape(batch, S, DWID))
    g["dil_q_norm_g"] = dgq[:, :HD] + dgq[:, HD:]
    g["dil_k_norm_g"] = dgk[:, :HD] + dgk[:, HD:]
    d_dq, d_dk, d_dv = (t.reshape(T, NG * DWID) for t in (d_dq, d_dk, d_dv))
    d_cz = d_cz.reshape(T, DWID)

    dproj = jnp.concatenate(
        [t.reshape(T, CW) for t in (d_ab, d_ac, d_ax, d_az)]
        + [d_cq, d_ckv, d_kpe_p, d_bz, d_dq, d_dk, d_dv, d_cz, dg0, dg1, dg2], axis=1)
    d_h = _mm("in_proj_dx", dproj, w["w_in_t"], tm=1024)
    g["w_in_t"] = _mm("in_proj_dw", dproj, h, ta=True, tm=1024)
    dx, g["norm_g"] = _pcall("norm_bwd", _norm_bwd_math, (T // BR,),
                             [(x, (BR, D), row(0)), (w["norm_g"], (1, D), fix), (d_h, (BR, D), row(0)),
                              (dy, (BR, D), row(0))],
                             [((T, D), F32, (BR, D), row(0)), ((1, D), F32, (1, D), fix, True)])
    return dx, g


def _loss_math(y, t):
    e = y - t
    return e * (1.0 / D), 0.5 * jnp.sum(jnp.sum(e * e, axis=-1, keepdims=True) / D, axis=0, keepdims=True)


def _local_step(x, target, ws, batch):
    T = batch * S
    tabs = _rope_tables() + (_dil_slopes(),)
    saved = []
    y = x
    for l in range(NL):
        y, res = _layer_fwd(y, ws[l], tabs, batch)
        saved.append(res)
    row = lambda i: (i, 0)
    dy, loss = _pcall("loss", _loss_math, (T // BR,),
                      [(y, (BR, D), row), (target, (BR, D), row)],
                      [((T, D), F32, (BR, D), row), ((1, 1), F32, (1, 1), lambda i: (0, 0), True)])
    grads = [None] * NL
    for l in reversed(range(NL)):
        dy, grads[l] = _layer_bwd(dy, ws[l], saved[l], tabs, batch)
    return loss, dy, grads


ANY = pl.BlockSpec(memory_space=pl.ANY)
U32 = jnp.uint32
WSH = NIN // 4
WA = KPE_END
WB = WSH - WA
CWD = 512
PACK_ROWS = 1472
HW = PACK_W // 2


def _me():
    return lax.axis_index("x"), lax.axis_index("y"), lax.axis_index("c")


def _piece_rows(k):
    a = k * WSH + jnp.where(k > 0, NINP - NIN, 0)
    b = k * WSH + WA + (NINP - NIN)
    return ((0, pl.multiple_of(a, 8), WA), (WA, pl.multiple_of(b, 8), WB))


def _pack_words(lo, hi):
    ul = lax.bitcast_convert_type(lo.astype(BF16).astype(F32), U32)
    uh = lax.bitcast_convert_type(hi.astype(BF16).astype(F32), U32)
    w = jnp.bitwise_or(jnp.bitwise_and(uh, jnp.uint32(0xFFFF0000)), jnp.right_shift(ul, jnp.uint32(16)))
    return lax.bitcast_convert_type(w, F32)


def _unpack_words(w):
    w = lax.bitcast_convert_type(w, U32)
    lo = lax.bitcast_convert_type(jnp.left_shift(w, jnp.uint32(16)), F32)
    hi = lax.bitcast_convert_type(jnp.bitwise_and(w, jnp.uint32(0xFFFF0000)), F32)
    return lo, hi


def _all_gather(wc, sp):
    def body(w_ref, s_ref, ow_ref, os_ref, send_sems, recv_sems):
        x, y, c = _me()
        k_me = 2 * x + y
        sib = (x, y, 1 - c)
        chips = [(1 - x, y), (x, 1 - y), (1 - x, 1 - y)]
        wcols = lambda cc: pl.ds(pl.multiple_of(cc * (CWD // 2), LANE), CWD // 2)
        scols = lambda cc: pl.ds(pl.multiple_of(cc * HW, LANE), HW)

        def windows(k, cc):
            pcs = _piece_rows(k)
            return ([(w_ref.at[pl.ds(l0, n), wcols(cc)], ow_ref.at[pl.ds(p0, n), wcols(cc)]) for l0, p0, n in pcs]
                    + [(s_ref.at[:, scols(cc)], os_ref.at[k, :, scols(cc)])])

        def copy(i, src, dst, to):
            return pltpu.make_async_remote_copy(src_ref=src, dst_ref=dst, send_sem=send_sems.at[i],
                                                recv_sem=recv_sems.at[i], device_id=to, device_id_type=MESH)

        def own_windows():
            return ([(w_ref.at[pl.ds(l0, n)], ow_ref.at[pl.ds(p0, n)]) for l0, p0, n in _piece_rows(k_me)]
                    + [(s_ref, os_ref.at[k_me])])

        first = [copy(18 + i, src, dst, sib) for i, (src, dst) in enumerate(own_windows())]
        for j, (cx, cy) in enumerate(chips):
            for i, (src, dst) in enumerate(windows(k_me, c)):
                first.append(copy(3 * j + i, src, dst, (cx, cy, c)))
        for cp in first:
            cp.start()
        passed = []
        for j, (cx, cy) in enumerate(chips):
            for i, (_, dst) in enumerate(windows(2 * cx + cy, c)):
                copy(3 * j + i, dst, dst, (cx, cy, c)).wait_recv()
                cp = copy(9 + 3 * j + i, dst, dst, sib)
                cp.start()
                passed.append(cp)
        for j, (cx, cy) in enumerate(chips):
            for i, (_, dst) in enumerate(windows(2 * cx + cy, 1 - c)):
                copy(9 + 3 * j + i, dst, dst, sib).wait_recv()
        for i, (_, dst) in enumerate(own_windows()):
            copy(18 + i, dst, dst, sib).wait_recv()
        for cp in first + passed:
            cp.wait_send()

    return pl.pallas_call(
        body,
        out_shape=[jax.ShapeDtypeStruct((NINP, CWD), F32), jax.ShapeDtypeStruct((4, PACK_ROWS, PACK_W), BF16)],
        in_specs=[ANY, ANY], out_specs=[ANY, ANY],
        scratch_shapes=[pltpu.SemaphoreType.DMA((21,)), pltpu.SemaphoreType.DMA((21,))],
        name="weights_all_gather",
    )(wc, sp)


HBM = pl.BlockSpec(memory_space=pltpu.HBM)
SEM = pl.BlockSpec(memory_space=pltpu.SEMAPHORE)
EFFECT = pltpu.SideEffectType.DATAFLOW_SIDE_EFFECTING


def _in_hbm(a):
    return pltpu.with_memory_space_constraint(a, pltpu.HBM)


def _ag_windows(w_ref, s_ref, lw_ref, ls_ref, k, cc):
    wcols = pl.ds(pl.multiple_of(cc * (CWD // 2), LANE), CWD // 2)
    scols = pl.ds(pl.multiple_of(cc * HW, LANE), HW)
    return ([(w_ref.at[pl.ds(l0, n), wcols], lw_ref.at[pl.ds(p0, n), wcols]) for l0, p0, n in _piece_rows(k)]
            + [(s_ref.at[:, scols], ls_ref.at[k, :, scols])])


def _ag_ici_copies(w_ref, s_ref, lw_ref, ls_ref, send_sems, recv_sems):
    x, y, c = _me()
    mine, theirs = [], []
    for j, (cx, cy) in enumerate([(1 - x, y), (x, 1 - y), (1 - x, 1 - y)]):
        for i, ((src, dst), (_, got)) in enumerate(zip(_ag_windows(w_ref, s_ref, lw_ref, ls_ref, 2 * x + y, c),
                                                       _ag_windows(w_ref, s_ref, lw_ref, ls_ref, 2 * cx + cy, c))):
            mk = lambda s_, d_: pltpu.make_async_remote_copy(
                src_ref=s_, dst_ref=d_, send_sem=send_sems.at[3 * j + i], recv_sem=recv_sems.at[3 * j + i],
                device_id=(cx, cy, c), device_id_type=MESH)
            mine.append(mk(src, dst))
            theirs.append(mk(got, got))
    return mine, theirs


def _ag_ici_start(wc, sp, dep):
    def body(w_ref, s_ref, lw_ref, ls_ref, dep_ref, send_sems, recv_sems, w_thru, s_thru, lw_thru, ls_thru, token):
        mine, _ = _ag_ici_copies(w_ref, s_ref, lw_ref, ls_ref, send_sems, recv_sems)
        for cp in mine:
            cp.start()
        token[...] = jnp.zeros_like(token)

    return pl.pallas_call(
        body, name="weights_gather_start",
        out_shape=(pltpu.SemaphoreType.DMA((9,)), pltpu.SemaphoreType.DMA((9,)), pltpu.HBM(wc.shape, wc.dtype),
                   pltpu.HBM(sp.shape, sp.dtype), pltpu.HBM((NINP, CWD), F32), pltpu.HBM((4, PACK_ROWS, PACK_W), BF16),
                   jax.ShapeDtypeStruct((8, LANE), F32)),
        in_specs=(HBM, HBM, HBM, HBM, ANY),
        out_specs=(SEM, SEM, HBM, HBM, HBM, HBM, pl.BlockSpec(memory_space=pltpu.VMEM)),
        input_output_aliases={0: 2, 1: 3, 2: 4, 3: 5},
        compiler_params=pltpu.CompilerParams(has_side_effects=EFFECT),
    )(_in_hbm(wc), _in_hbm(sp), _in_hbm(lax.empty((NINP, CWD), F32)), _in_hbm(lax.empty((4, PACK_ROWS, PACK_W), BF16)), dep)


def _ag_ici_wait(send_sems, recv_sems, wc, sp, lw, ls, after):
    def body(w_ref, s_ref, lw_ref, ls_ref, send_sems, recv_sems, after_ref, w_dead, s_dead, lw_out, ls_out):
        mine, theirs = _ag_ici_copies(w_ref, s_ref, lw_ref, ls_ref, send_sems, recv_sems)
        for cp in mine:
            cp.wait_send()
        for cp in theirs:
            cp.wait_recv()

    out = pl.pallas_call(
        body, name="weights_gather_wait",
        out_shape=(pltpu.HBM(wc.shape, wc.dtype), pltpu.HBM(sp.shape, sp.dtype), pltpu.HBM(lw.shape, lw.dtype),
                   pltpu.HBM(ls.shape, ls.dtype)),
        in_specs=(HBM, HBM, HBM, HBM, SEM, SEM, ANY), out_specs=(HBM, HBM, HBM, HBM),
        input_output_aliases={0: 0, 1: 1, 2: 2, 3: 3},
        compiler_params=pltpu.CompilerParams(has_side_effects=EFFECT),
    )(wc, sp, lw, ls, send_sems, recv_sems, after)
    return out[2], out[3]


def _ag_finish(wc, sp, lw, ls):
    def body(w_ref, s_ref, lw_ref, ls_ref, ow_ref, os_ref, send_sems, recv_sems):
        x, y, c = _me()
        k_me = 2 * x + y
        sib = (x, y, 1 - c)
        chips = [(1 - x, y), (x, 1 - y), (1 - x, 1 - y)]

        def copy(i, src, dst):
            return pltpu.make_async_remote_copy(src_ref=src, dst_ref=dst, send_sem=send_sems.at[i],
                                                recv_sem=recv_sems.at[i], device_id=sib, device_id_type=MESH)

        def own_windows():
            return ([(w_ref.at[pl.ds(l0, n)], ow_ref.at[pl.ds(p0, n)]) for l0, p0, n in _piece_rows(k_me)]
                    + [(s_ref, os_ref.at[k_me])])

        out = [copy(9 + i, src, dst) for i, (src, dst) in enumerate(own_windows())]
        for j, (cx, cy) in enumerate(chips):
            landed = _ag_windows(w_ref, s_ref, lw_ref, ls_ref, 2 * cx + cy, c)
            for i, (_, dst) in enumerate(_ag_windows(w_ref, s_ref, ow_ref, os_ref, 2 * cx + cy, c)):
                out.append(copy(3 * j + i, landed[i][1], dst))
        for cp in out:
            cp.start()
        for j, (cx, cy) in enumerate(chips):
            for i, (_, dst) in enumerate(_ag_windows(w_ref, s_ref, ow_ref, os_ref, 2 * cx + cy, 1 - c)):
                copy(3 * j + i, dst, dst).wait_recv()
        for i, (_, dst) in enumerate(own_windows()):
            copy(9 + i, dst, dst).wait_recv()
        for cp in out:
            cp.wait_send()

    return pl.pallas_call(
        body,
        out_shape=[jax.ShapeDtypeStruct(lw.shape, lw.dtype), jax.ShapeDtypeStruct(ls.shape, ls.dtype)],
        in_specs=[ANY] * 4, out_specs=[ANY, ANY],
        input_output_aliases={2: 0, 3: 1},
        scratch_shapes=[pltpu.SemaphoreType.DMA((12,)), pltpu.SemaphoreType.DMA((12,))],
        name="weights_gather_finish",
    )(wc, sp, lw, ls)


UNPACK_BR = 512


def _unpack_w_in(cont):
    def body(c_ref, o_ref):
        lo, hi = _unpack_words(c_ref[...])
        r = pl.program_id(0) * UNPACK_BR + lax.broadcasted_iota(jnp.int32, (UNPACK_BR, CWD), 0)
        pad = jnp.logical_and(r >= KPE_END, r < KPE_END + NINP - NIN)
        o_ref[:, 0:CWD] = jnp.where(pad, 0.0, lo).astype(BF16)
        o_ref[:, CWD:2 * CWD] = jnp.where(pad, 0.0, hi).astype(BF16)

    return pl.pallas_call(
        body, grid=(NINP // UNPACK_BR,),
        in_specs=[pl.BlockSpec((UNPACK_BR, CWD), lambda i: (i, 0))],
        out_specs=pl.BlockSpec((UNPACK_BR, D), lambda i: (i, 0)),
        out_shape=jax.ShapeDtypeStruct((NINP, D), BF16),
        name="w_in_unpack",
        compiler_params=pltpu.CompilerParams(dimension_semantics=("arbitrary",), vmem_limit_bytes=VMEM_LIMIT),
    )(cont)


def _rs_swap(gw, gs):
    def body(w_ref, s_ref, rw_ref, rs_ref, send_sems, recv_sems):
        x, y, c = _me()
        oc = 1 - c
        cps = [pltpu.make_async_remote_copy(src_ref=w_ref.at[:, pl.ds(pl.multiple_of(oc * (D // 2), LANE), D // 2)],
                                            dst_ref=rw_ref, send_sem=send_sems.at[0], recv_sem=recv_sems.at[0],
                                            device_id=(x, y, oc), device_id_type=MESH),
               pltpu.make_async_remote_copy(src_ref=s_ref.at[:, :, pl.ds(pl.multiple_of(oc * HW, LANE), HW)],
                                            dst_ref=rs_ref, send_sem=send_sems.at[1], recv_sem=recv_sems.at[1],
                                            device_id=(x, y, oc), device_id_type=MESH)]
        for cp in cps:
            cp.start()
        for cp in cps:
            cp.wait()

    return pl.pallas_call(
        body,
        out_shape=[jax.ShapeDtypeStruct((NINP, D // 2), F32), jax.ShapeDtypeStruct((4, PACK_ROWS, HW), F32)],
        in_specs=[ANY, ANY], out_specs=[ANY, ANY],
        scratch_shapes=[pltpu.SemaphoreType.DMA((2,)), pltpu.SemaphoreType.DMA((2,))],
        name="grads_sibling_swap",
    )(gw, gs)


SUM_BR = 512


def _rs_chip_sum_w(gw, rw, cidx):
    def body(c_ref, g_ref, r_ref, o_ref):
        s = g_ref[...] + r_ref[...]
        q = D // 8
        o_ref[...] = jnp.concatenate([_pack_words(s[:, 0:q], s[:, q:2 * q]),
                                      _pack_words(s[:, 2 * q:3 * q], s[:, 3 * q:4 * q])], axis=1)

    return pl.pallas_call(
        body,
        grid_spec=pltpu.PrefetchScalarGridSpec(
            num_scalar_prefetch=1, grid=(NINP // SUM_BR,),
            in_specs=[pl.BlockSpec((SUM_BR, D // 2), lambda i, cr: (i, cr[0])),
                      pl.BlockSpec((SUM_BR, D // 2), lambda i, cr: (i, 0))],
            out_specs=pl.BlockSpec((SUM_BR, D // 4), lambda i, cr: (i, 0))),
        out_shape=jax.ShapeDtypeStruct((NINP, D // 4), F32),
        name="grads_chip_sum_w",
        compiler_params=pltpu.CompilerParams(dimension_semantics=("arbitrary",), vmem_limit_bytes=VMEM_LIMIT),
    )(cidx, gw, rw)


def _rs_chip_sum_s(gs, rs, cidx):
    def body(c_ref, g_ref, r_ref, o_ref):
        o_ref[...] = (g_ref[...] + r_ref[...]).astype(BF16)

    return pl.pallas_call(
        body,
        grid_spec=pltpu.PrefetchScalarGridSpec(
            num_scalar_prefetch=1, grid=(4,),
            in_specs=[pl.BlockSpec((None, PACK_ROWS, HW), lambda j, cr: (j, 0, cr[0])),
                      pl.BlockSpec((None, PACK_ROWS, HW), lambda j, cr: (j, 0, 0))],
            out_specs=pl.BlockSpec((None, PACK_ROWS, HW), lambda j, cr: (j, 0, 0))),
        out_shape=jax.ShapeDtypeStruct((4, PACK_ROWS, HW), BF16),
        name="grads_chip_sum_s",
        compiler_params=pltpu.CompilerParams(dimension_semantics=("arbitrary",), vmem_limit_bytes=VMEM_LIMIT),
    )(cidx, gs, rs)


def _rs_exchange(sw, ss):
    def body(sw_ref, ss_ref, r2w_ref, r2s_ref, send_sems, recv_sems):
        x, y, c = _me()
        chips = [(1 - x, y), (x, 1 - y), (1 - x, 1 - y)]
        cps = []
        for j, (cx, cy) in enumerate(chips):
            k = 2 * cx + cy
            for i, (l0, p0, n) in enumerate(_piece_rows(k)):
                cps.append(pltpu.make_async_remote_copy(
                    src_ref=sw_ref.at[pl.ds(p0, n)], dst_ref=r2w_ref.at[j, pl.ds(l0, n)], send_sem=send_sems.at[3 * j + i],
                    recv_sem=recv_sems.at[3 * j + i], device_id=(cx, cy, c), device_id_type=MESH))
            cps.append(pltpu.make_async_remote_copy(
                src_ref=ss_ref.at[k], dst_ref=r2s_ref.at[j], send_sem=send_sems.at[3 * j + 2],
                recv_sem=recv_sems.at[3 * j + 2], device_id=(cx, cy, c), device_id_type=MESH))
        for cp in cps:
            cp.start()
        for cp in cps:
            cp.wait()

    return pl.pallas_call(
        body,
        out_shape=[jax.ShapeDtypeStruct((3, WSH, D // 4), F32), jax.ShapeDtypeStruct((3, PACK_ROWS, HW), BF16)],
        in_specs=[ANY] * 2, out_specs=[ANY] * 2,
        scratch_shapes=[pltpu.SemaphoreType.DMA((9,)), pltpu.SemaphoreType.DMA((9,))],
        name="grads_chip_exchange",
    )(sw, ss)


def _rs_exchange_copies(sw_ref, ss_ref, r2w_ref, r2s_ref, send_sems, recv_sems):
    x, y, c = _me()
    mine, theirs = [], []
    for j, (cx, cy) in enumerate([(1 - x, y), (x, 1 - y), (1 - x, 1 - y)]):
        def mk(i, src, dst):
            return pltpu.make_async_remote_copy(src_ref=src, dst_ref=dst, send_sem=send_sems.at[3 * j + i],
                                                recv_sem=recv_sems.at[3 * j + i], device_id=(cx, cy, c), device_id_type=MESH)
        for i, (l0, p0, n) in enumerate(_piece_rows(2 * cx + cy)):
            mine.append(mk(i, sw_ref.at[pl.ds(p0, n)], r2w_ref.at[j, pl.ds(l0, n)]))
            theirs.append(mk(i, r2w_ref.at[j, pl.ds(l0, n)], r2w_ref.at[j, pl.ds(l0, n)]))
        mine.append(mk(2, ss_ref.at[2 * cx + cy], r2s_ref.at[j]))
        theirs.append(mk(2, r2s_ref.at[j], r2s_ref.at[j]))
    return mine, theirs


def _rs_exchange_start(sw, ss):
    def body(sw_ref, ss_ref, r2w_ref, r2s_ref, send_sems, recv_sems, sw_thru, ss_thru, r2w_thru, r2s_thru, token):
        mine, _ = _rs_exchange_copies(sw_ref, ss_ref, r2w_ref, r2s_ref, send_sems, recv_sems)
        for cp in mine:
            cp.start()
        token[...] = jnp.zeros_like(token)

    return pl.pallas_call(
        body, name="grads_exchange_start",
        out_shape=(pltpu.SemaphoreType.DMA((9,)), pltpu.SemaphoreType.DMA((9,)), pltpu.HBM(sw.shape, sw.dtype),
                   pltpu.HBM(ss.shape, ss.dtype), pltpu.HBM((3, WSH, D // 4), F32), pltpu.HBM((3, PACK_ROWS, HW), BF16),
                   jax.ShapeDtypeStruct((8, LANE), F32)),
        in_specs=(HBM, HBM, HBM, HBM),
        out_specs=(SEM, SEM, HBM, HBM, HBM, HBM, pl.BlockSpec(memory_space=pltpu.VMEM)),
        input_output_aliases={0: 2, 1: 3, 2: 4, 3: 5},
        compiler_params=pltpu.CompilerParams(has_side_effects=EFFECT),
    )(_in_hbm(sw), _in_hbm(ss), _in_hbm(lax.empty((3, WSH, D // 4), F32)), _in_hbm(lax.empty((3, PACK_ROWS, HW), BF16)))


def _rs_exchange_wait(send_sems, recv_sems, sw, ss, r2w, r2s, after):
    def body(sw_ref, ss_ref, r2w_ref, r2s_ref, send_sems, recv_sems, after_ref, sw_dead, ss_dead, r2w_out, r2s_out):
        mine, theirs = _rs_exchange_copies(sw_ref, ss_ref, r2w_ref, r2s_ref, send_sems, recv_sems)
        for cp in mine:
            cp.wait_send()
        for cp in theirs:
            cp.wait_recv()

    out = pl.pallas_call(
        body, name="grads_exchange_wait",
        out_shape=(pltpu.HBM(sw.shape, sw.dtype), pltpu.HBM(ss.shape, ss.dtype), pltpu.HBM(r2w.shape, r2w.dtype),
                   pltpu.HBM(r2s.shape, r2s.dtype)),
        in_specs=(HBM, HBM, HBM, HBM, SEM, SEM, ANY), out_specs=(HBM, HBM, HBM, HBM),
        input_output_aliases={0: 0, 1: 1, 2: 2, 3: 3},
        compiler_params=pltpu.CompilerParams(has_side_effects=EFFECT),
    )(sw, ss, r2w, r2s, send_sems, recv_sems, after)
    return out[2], out[3]


def _rs_final_w(gw, rw, r2w, idx):
    q = D // 8

    def body(i_ref, g_ref, r_ref, p_ref, o_ref, gbuf, rbuf, sems):
        i = pl.program_id(0)
        k, c = i_ref[0], i_ref[1]
        cps = []
        for n_, (l0, p0, n) in enumerate(_piece_rows(k)):
            gcol = pl.ds(pl.multiple_of(c * (D // 2) + i * 2 * q, LANE), 2 * q)
            rcol = pl.ds(pl.multiple_of(i * 2 * q, LANE), 2 * q)
            cps.append(pltpu.make_async_copy(g_ref.at[pl.ds(p0, n), gcol], gbuf.at[pl.ds(l0, n)], sems.at[2 * n_]))
            cps.append(pltpu.make_async_copy(r_ref.at[pl.ds(p0, n), rcol], rbuf.at[pl.ds(l0, n)], sems.at[2 * n_ + 1]))
        for cp in cps:
            cp.start()
        for cp in cps:
            cp.wait()
        acc = gbuf[...] + rbuf[...]
        for j in range(3):
            lo, hi = _unpack_words(p_ref[j])
            acc = acc + jnp.concatenate([lo, hi], axis=1)
        o_ref[...] = acc

    return pl.pallas_call(
        body,
        grid_spec=pltpu.PrefetchScalarGridSpec(
            num_scalar_prefetch=1, grid=(2,),
            in_specs=[ANY, ANY, pl.BlockSpec((3, WSH, q), lambda i, ir: (0, 0, i))],
            out_specs=pl.BlockSpec((WSH, 2 * q), lambda i, ir: (0, i)),
            scratch_shapes=[pltpu.VMEM((WSH, 2 * q), F32), pltpu.VMEM((WSH, 2 * q), F32), pltpu.SemaphoreType.DMA((4,))]),
        out_shape=jax.ShapeDtypeStruct((WSH, D // 2), F32),
        name="grads_final_sum_w",
        compiler_params=pltpu.CompilerParams(dimension_semantics=("arbitrary",), vmem_limit_bytes=VMEM_LIMIT),
    )(idx, gw, rw, r2w)


def _rs_final_s(gs, rs, r2s, idx):
    def body(i_ref, g_ref, r_ref, p_ref, o_ref):
        acc = g_ref[...] + r_ref[...]
        for j in range(3):
            acc = acc + p_ref[j].astype(F32)
        o_ref[...] = acc

    return pl.pallas_call(
        body,
        grid_spec=pltpu.PrefetchScalarGridSpec(
            num_scalar_prefetch=1, grid=(1,),
            in_specs=[pl.BlockSpec((None, PACK_ROWS, HW), lambda i, ir: (ir[0], 0, ir[1])),
                      pl.BlockSpec((None, PACK_ROWS, HW), lambda i, ir: (ir[0], 0, 0)),
                      pl.BlockSpec((3, PACK_ROWS, HW), lambda i, ir: (0, 0, 0))],
            out_specs=pl.BlockSpec((PACK_ROWS, HW), lambda i, ir: (0, 0))),
        out_shape=jax.ShapeDtypeStruct((PACK_ROWS, HW), F32),
        name="grads_final_sum_s",
        compiler_params=pltpu.CompilerParams(dimension_semantics=("arbitrary",), vmem_limit_bytes=VMEM_LIMIT),
    )(idx, gs, rs, r2s)


def _rs_share(fw, fs):
    def body(w_ref, s_ref, ow_ref, os_ref, send_sems, recv_sems):
        x, y, c = _me()
        cps = [pltpu.make_async_remote_copy(src_ref=w_ref, dst_ref=ow_ref, send_sem=send_sems.at[0],
                                            recv_sem=recv_sems.at[0], device_id=(x, y, 1 - c), device_id_type=MESH),
               pltpu.make_async_remote_copy(src_ref=s_ref, dst_ref=os_ref, send_sem=send_sems.at[1],
                                            recv_sem=recv_sems.at[1], device_id=(x, y, 1 - c), device_id_type=MESH)]
        for cp in cps:
            cp.start()
        for cp in cps:
            cp.wait()

    return pl.pallas_call(
        body,
        out_shape=[jax.ShapeDtypeStruct((WSH, D // 2), F32), jax.ShapeDtypeStruct((PACK_ROWS, HW), F32)],
        in_specs=[ANY, ANY], out_specs=[ANY, ANY],
        scratch_shapes=[pltpu.SemaphoreType.DMA((2,)), pltpu.SemaphoreType.DMA((2,))],
        name="grads_share",
    )(fw, fs)


def _both_halves(mine, other, c):
    return jnp.where(c == 0, jnp.concatenate([mine, other], axis=1), jnp.concatenate([other, mine], axis=1))


def _rs_begin(gw, gs):
    x, y, c = _me()
    cidx = jnp.reshape(c, (1,)).astype(jnp.int32)
    rw, rs = _rs_swap(gw, gs)
    return dict(gw=gw, gs=gs, rw=rw, rs=rs, sw=_rs_chip_sum_w(gw, rw, cidx), ss=_rs_chip_sum_s(gs, rs, cidx))


def _rs_end(st, r2w, r2s):
    x, y, c = _me()
    idx = jnp.stack([2 * x + y, c]).astype(jnp.int32)
    fw = _rs_final_w(st["gw"], st["rw"], r2w, idx)
    fs = _rs_final_s(st["gs"], st["rs"], r2s, idx)
    ow, os_ = _rs_share(fw, fs)
    return _both_halves(fw, ow, c), _both_halves(fs, os_, c)


def _reduce_scatter(gw, gs):
    st = _rs_begin(gw, gs)
    return _rs_end(st, *_rs_exchange(st["sw"], st["ss"]))


def _all_reduce_small(gs):
    rows = gs.shape[0]

    def body(g_ref, o_ref, buf, send_sems, recv_sems):
        x, y, c = _me()
        me = 4 * x + 2 * y + c
        buf[me] = g_ref[...]
        cps = []
        for r in range(1, 8):
            fx, fy, fc = (r >> 2) & 1, (r >> 1) & 1, r & 1
            px, py, pc = jnp.bitwise_xor(x, fx), jnp.bitwise_xor(y, fy), jnp.bitwise_xor(c, fc)
            cps.append((pltpu.make_async_remote_copy(
                src_ref=g_ref, dst_ref=buf.at[me], send_sem=send_sems.at[r - 1], recv_sem=recv_sems.at[r - 1],
                device_id=(px, py, pc), device_id_type=MESH), 4 * px + 2 * py + pc))
        for cp, _ in cps:
            cp.start()
        for r, (cp, peer) in enumerate(cps):
            pltpu.make_async_remote_copy(
                src_ref=g_ref, dst_ref=buf.at[peer], send_sem=send_sems.at[r], recv_sem=recv_sems.at[r],
                device_id=(x, y, c), device_id_type=MESH).wait_recv()
        for cp, _ in cps:
            cp.wait_send()
        acc = buf[0]
        for k in range(1, 8):
            acc = acc + buf[k]
        o_ref[...] = acc

    return pl.pallas_call(
        body,
        out_shape=jax.ShapeDtypeStruct((rows, LANE), F32),
        in_specs=[pl.BlockSpec(memory_space=pltpu.VMEM)],
        out_specs=pl.BlockSpec(memory_space=pltpu.VMEM),
        scratch_shapes=[pltpu.VMEM((8, rows, LANE), F32), pltpu.SemaphoreType.DMA((7,)), pltpu.SemaphoreType.DMA((7,))],
        name="small_grads_all_reduce",
    )(gs)


PACK_SPLIT = (("w_uq", 96, (QL, 192)), ("w_ukv", 64, (KVL, 256)),
              ("w_out_a", 256, (CW, 256)), ("w_out_b", 256, (CW, 256)), ("w_out_c", 256, (CW, 256)),
              ("w_o", 512, (256, D)))
MAT_ROWS = 1440
CONV_SHARD = 3 * 128


def _w_in_words(w_in_shard):
    t = w_in_shard.T
    return _pack_words(t[:, :CWD], t[:, CWD:])


def _pack_weights(wl):
    parts = [wl[n].astype(BF16).reshape(-1, PACK_W) for n, _, _ in PACK_SPLIT]
    cw = wl["conv_w"].reshape(-1)
    hi = cw.astype(BF16)
    r1 = cw - hi.astype(F32)
    mid = r1.astype(BF16)
    lo = (r1 - mid.astype(F32)).astype(BF16)
    cterms = jnp.pad(jnp.concatenate([hi, mid, lo]), (0, 3 * PACK_W - 3 * CONV_SHARD)).reshape(3, PACK_W)
    tail = jnp.pad(cterms, ((0, PACK_ROWS - MAT_ROWS - 3), (0, 0)))
    return jnp.concatenate(parts + [tail], axis=0)


def _unpack_weights(gath):
    out = {}
    r = 0
    for n, nrows, shp in PACK_SPLIT:
        t = gath[:, r:r + nrows].reshape((4,) + shp)
        r += nrows
        if n == "w_o":
            out[n] = t.reshape(4 * shp[0], shp[1])
        else:
            out[n] = t.transpose(1, 0, 2).reshape(shp[0], 4 * shp[1])
    ct = gath[:, r:r + 3].reshape(4, 3 * PACK_W)[:, :3 * CONV_SHARD].astype(F32).reshape(4, 3, CONV_SHARD)
    cw = (ct[:, 0] + ct[:, 1]) + ct[:, 2]
    out["conv_w"] = cw.reshape(4, 3, 128).transpose(1, 0, 2).reshape(3, CW)
    return out


def _pack_grads(g):
    parts = []
    for n, nrows, shp in PACK_SPLIT:
        t = g[n]
        if n == "w_o":
            t = t.reshape((4,) + shp)
        else:
            t = t.reshape(shp[0], 4, shp[1]).transpose(1, 0, 2)
        parts.append(t.reshape(4, nrows, PACK_W))
    cw = g["conv_w"].reshape(3, 4, 128).transpose(1, 0, 2).reshape(4, 1, CONV_SHARD)
    parts.append(jnp.pad(cw, ((0, 0), (0, PACK_ROWS - MAT_ROWS - 1), (0, PACK_W - CONV_SHARD))))
    return jnp.concatenate(parts, axis=1)


def _unpack_grads(red):
    out = {}
    r = 0
    for n, nrows, shp in PACK_SPLIT:
        out[n] = red[r:r + nrows].reshape(shp)
        r += nrows
    out["conv_w"] = red[r, :CONV_SHARD].reshape(3, 128)
    return out


SMALL_SIZES = (("norm_g", D), ("b_gate", 3 * D), ("conv_b", CW), ("q_a_norm_g", QL), ("kv_a_norm_g", KVL),
               ("mla_q_norm_g", QK), ("mla_k_norm_g", QK), ("dil_q_norm_g", NG * HD), ("dil_k_norm_g", NG * HD))
SMALL_ROWS = 88


def _pack_small(per_name):
    flat = jnp.concatenate([per_name[n].reshape(-1).astype(F32) for n, _ in SMALL_SIZES])
    return jnp.pad(flat, (0, SMALL_ROWS * LANE - flat.shape[0])).reshape(SMALL_ROWS, LANE)


def _unpack_small(packed, like):
    out = {}
    flat = packed.reshape(-1)
    r = 0
    for n, sz in SMALL_SIZES:
        out[n] = flat[r:r + NL * sz].reshape(like[n].shape)
        r += NL * sz
    return out


def _adamw_math(w, g, m, v):
    m = ADAM_B1 * m + (1.0 - ADAM_B1) * g
    v = ADAM_B2 * v + (1.0 - ADAM_B2) * jnp.square(g)
    m_hat = m / (1.0 - ADAM_B1 ** ADAM_STEP)
    v_hat = v / (1.0 - ADAM_B2 ** ADAM_STEP)
    delta = -ADAM_LR * (m_hat / (jnp.sqrt(v_hat) + ADAM_EPS) + ADAM_WD * w)
    return delta, m, v


def _adamw(name, w, g, m, v, br, bc=None):
    L, R, C = w.shape
    bc = C if bc is None else bc
    blk = lambda l, i, j: (l, i, j)
    return _pcall(name, _adamw_math, (L, R // br, C // bc), [(t, (None, br, bc), blk) for t in (w, g, m, v)],
                  [((L, R, C), F32, (None, br, bc), blk)] * 3)


ADAM_ROWS = {"w_uq": 256, "w_ukv": 128, "w_out_a": 512, "w_out_b": 512, "w_out_c": 512, "w_o": 256,
             "conv_w": 3}


def kernel(x, norm_g, w_in, b_gate, conv_w, conv_b, q_a_norm_g, w_uq, kv_a_norm_g, w_ukv, mla_q_norm_g, mla_k_norm_g, dil_q_norm_g, dil_k_norm_g, w_out_a, w_out_b, w_out_c, w_o, loss_target, m_norm_g, m_w_in, m_b_gate, m_conv_w, m_conv_b, m_q_a_norm_g, m_w_uq, m_kv_a_norm_g, m_w_ukv, m_mla_q_norm_g, m_mla_k_norm_g, m_dil_q_norm_g, m_dil_k_norm_g, m_w_out_a, m_w_out_b, m_w_out_c, m_w_o, v_norm_g, v_w_in, v_b_gate, v_conv_w, v_conv_b, v_q_a_norm_g, v_w_uq, v_kv_a_norm_g, v_w_ukv, v_mla_q_norm_g, v_mla_k_norm_g, v_dil_q_norm_g, v_dil_k_norm_g, v_w_out_a, v_w_out_b, v_w_out_c, v_w_o):
    W = dict(norm_g=norm_g, w_in=w_in, b_gate=b_gate, conv_w=conv_w, conv_b=conv_b, q_a_norm_g=q_a_norm_g, w_uq=w_uq,
             kv_a_norm_g=kv_a_norm_g, w_ukv=w_ukv, mla_q_norm_g=mla_q_norm_g, mla_k_norm_g=mla_k_norm_g,
             dil_q_norm_g=dil_q_norm_g, dil_k_norm_g=dil_k_norm_g, w_out_a=w_out_a, w_out_b=w_out_b, w_out_c=w_out_c,
             w_o=w_o)
    M = dict(norm_g=m_norm_g, w_in=m_w_in, b_gate=m_b_gate, conv_w=m_conv_w, conv_b=m_conv_b, q_a_norm_g=m_q_a_norm_g,
             w_uq=m_w_uq, kv_a_norm_g=m_kv_a_norm_g, w_ukv=m_w_ukv, mla_q_norm_g=m_mla_q_norm_g,
             mla_k_norm_g=m_mla_k_norm_g, dil_q_norm_g=m_dil_q_norm_g, dil_k_norm_g=m_dil_k_norm_g, w_out_a=m_w_out_a,
             w_out_b=m_w_out_b, w_out_c=m_w_out_c, w_o=m_w_o)
    V = dict(norm_g=v_norm_g, w_in=v_w_in, b_gate=v_b_gate, conv_w=v_conv_w, conv_b=v_conv_b, q_a_norm_g=v_q_a_norm_g,
             w_uq=v_w_uq, kv_a_norm_g=v_kv_a_norm_g, w_ukv=v_w_ukv, mla_q_norm_g=v_mla_q_norm_g,
             mla_k_norm_g=v_mla_k_norm_g, dil_q_norm_g=v_dil_q_norm_g, dil_k_norm_g=v_dil_k_norm_g, w_out_a=v_w_out_a,
             w_out_b=v_w_out_b, w_out_c=v_w_out_c, w_o=v_w_o)
    batch = x.shape[0]
    T = batch * S

    def layer_weights(l, cont, gath):
        full = _unpack_weights(gath)
        pad_qk = lambda t: jnp.pad(t, (0, QKP - QK)).reshape(1, QKP)
        full.update(
            w_in_t=_unpack_w_in(cont),
            norm_g=norm_g[l].reshape(1, D), b_gate=b_gate[l].reshape(1, 3 * D), conv_b=conv_b[l].reshape(1, CW),
            q_a_norm_g=q_a_norm_g[l].reshape(1, QL), kv_a_norm_g=kv_a_norm_g[l].reshape(1, KVL),
            mla_q_norm_g=pad_qk(mla_q_norm_g[l]), mla_k_norm_g=pad_qk(mla_k_norm_g[l]),
            dil_q_norm_g=dil_q_norm_g[l].reshape(NG, 1, HD), dil_k_norm_g=dil_k_norm_g[l].reshape(NG, 1, HD))
        return full

    words = [_w_in_words(w_in[l]) for l in range(NL)]
    packs = [_pack_weights({n: W[n][l] for n in BIG[1:] + ("conv_w",)}) for l in range(NL)]
    tabs = _rope_tables() + (_dil_slopes(),)
    x2 = x.reshape(T, D)

    cont0, gath0 = _all_gather(words[0], packs[0])
    w0 = layer_weights(0, cont0, gath0)
    ag = _ag_ici_start(words[1], packs[1], gath0)
    w0["norm_g"] = w0["norm_g"] + ag[6][0:1, 0:1]
    y0, res0 = _layer_fwd(x2, w0, tabs, batch)
    lw, ls = _ag_ici_wait(ag[0], ag[1], ag[2], ag[3], ag[4], ag[5], y0)
    w1 = layer_weights(1, *_ag_finish(words[1], packs[1], lw, ls))
    y1, res1 = _layer_fwd(y0, w1, tabs, batch)

    row = lambda i: (i, 0)
    dy, loss = _pcall("loss", _loss_math, (T // BR,),
                      [(y1, (BR, D), row), (loss_target.reshape(T, D), (BR, D), row)],
                      [((T, D), F32, (BR, D), row), ((1, 1), F32, (1, 1), lambda i: (0, 0), True)])
    loss = lax.psum(loss[0, 0], ("x", "y", "c"))

    grads = [None] * NL
    dy, grads[1] = _layer_bwd(dy, w1, res1, tabs, batch)
    st1 = _rs_begin(grads[1]["w_in_t"], _pack_grads(grads[1]))
    ex = _rs_exchange_start(st1["sw"], st1["ss"])
    w0["w_o"] = w0["w_o"] + ex[6][0:1, 0:1].astype(BF16)
    dx, grads[0] = _layer_bwd(dy, w0, res0, tabs, batch)
    grad_x = dx.reshape(batch, S, D)
    r2w, r2s = _rs_exchange_wait(ex[0], ex[1], ex[2], ex[3], ex[4], ex[5], dx)

    red = [None] * NL
    for l in range(NL):
        if l == 1:
            rw, rs = _rs_end(st1, r2w, r2s)
        else:
            rw, rs = _reduce_scatter(grads[l]["w_in_t"], _pack_grads(grads[l]))
        r = _unpack_grads(rs)
        r["w_in_t"] = rw
        red[l] = r
    G = {n: jnp.stack([red[l][n] for l in range(NL)]) for n in BIG[1:] + ("conv_w",)}
    g_in_t = jnp.stack([red[l]["w_in_t"] for l in range(NL)])
    G["w_in"] = jnp.swapaxes(g_in_t, 1, 2)
    small_g = {n: jnp.stack([grads[l][n].reshape(-1)[:sz] for l in range(NL)]) for n, sz in SMALL_SIZES}
    small_red = _all_reduce_small(_pack_small(small_g))
    G.update(_unpack_small(small_red, {n: W[n] for n in SMALL}))

    delta, new_m, new_v = {}, {}, {}
    for n in BIG[1:] + ("conv_w",):
        delta[n], new_m[n], new_v[n] = _adamw("adamw_" + n, W[n], G[n], M[n], V[n], ADAM_ROWS[n])
    tr = lambda t: jnp.swapaxes(t, 1, 2)
    delta["w_in"], new_m["w_in"], new_v["w_in"] = (
        tr(t) for t in _adamw("adamw_w_in", tr(w_in), g_in_t, tr(m_w_in), tr(v_w_in), WSH, LANE))
    sw, sm, sv = (_pack_small({n: t[n] for n in SMALL})[None] for t in (W, M, V))
    sd, snm, snv = _adamw("adamw_small", sw, small_red[None], sm, sv, SMALL_ROWS)
    like = {n: W[n] for n in SMALL}
    delta.update(_unpack_small(sd[0], like))
    new_m.update(_unpack_small(snm[0], like))
    new_v.update(_unpack_small(snv[0], like))

    return (loss, grad_x, *[G[n] for n in WEIGHTS], *[delta[n] for n in WEIGHTS],
            *[new_m[n] for n in WEIGHTS], *[new_v[n] for n in WEIGHTS])
```

```python
import functools

import numpy as np
import jax
import jax.numpy as jnp
from jax import lax
from jax.experimental import pallas as pl
from jax.experimental.pallas import tpu as pltpu

F32 = jnp.float32
BF16 = jnp.bfloat16

D = 1024
S = 2048
NL = 2
CW = 512
NH = 8
QL = 256
KVL = 128
NOPE = 64
ROPE = 32
VD = 64
QK = NOPE + ROPE
QKP = 128
ROPE_THETA = 10000.0
DIL = ((128, 1), (512, 4), (2048, 16))
NG = 3
DH = 8
HD = 64
DWID = DH * HD
QB = 128
EPS = 1e-6
NIN = 11168
NINP = 11264
O_A, O_CQ, O_CKV, O_KPE, O_BZ, O_DQ, O_DK, O_DV, O_CZ, O_G = 0, 2048, 2304, 2432, 2560, 3072, 4608, 6144, 7680, 8192
KPE_END = 2464
NEG = -1e30
MLA_SCALE = QK ** -0.5
DIL_SCALE = HD ** -0.5
LANE = 128
PACK_W = 512
VMEM_LIMIT = 48 * 1024 * 1024

ADAM_LR = 0.001
ADAM_B1 = 0.9
ADAM_B2 = 0.999
ADAM_EPS = 1e-08
ADAM_WD = 0.01
ADAM_STEP = 10

MESH = pl.DeviceIdType.MESH
BIG = ("w_in", "w_uq", "w_ukv", "w_out_a", "w_out_b", "w_out_c", "w_o")
SMALL = ("norm_g", "b_gate", "conv_b", "q_a_norm_g", "kv_a_norm_g", "mla_q_norm_g", "mla_k_norm_g",
         "dil_q_norm_g", "dil_k_norm_g")
WEIGHTS = ("norm_g", "w_in", "b_gate", "conv_w", "conv_b", "q_a_norm_g", "w_uq", "kv_a_norm_g", "w_ukv",
           "mla_q_norm_g", "mla_k_norm_g", "dil_q_norm_g", "dil_k_norm_g", "w_out_a", "w_out_b", "w_out_c", "w_o")


def _dot(a, b):
    return jnp.dot(a, b, preferred_element_type=F32)


def _dot_nt(a, b):
    return lax.dot_general(a, b, (((1,), (1,)), ((), ())), preferred_element_type=F32)


def _dot_tn(a, b):
    return lax.dot_general(a, b, (((0,), (0,)), ((), ())), preferred_element_type=F32)


def _pcall(name, fn, grid, ins, outs):
    n_in = len(ins)
    n_out = len(outs)
    acc_axis = len(grid) - 1
    is_acc = [len(o) > 4 and o[4] for o in outs]
    outs = [o[:4] for o in outs]

    def body(*refs):
        vals = fn(*[r[...].astype(F32) for r in refs[:n_in]])
        if not isinstance(vals, (tuple, list)):
            vals = (vals,)
        for k in range(n_out):
            r = refs[n_in + k]
            v = vals[k].astype(r.dtype).reshape(r.shape)
            if is_acc[k]:
                first = pl.program_id(acc_axis) == 0

                @pl.when(first)
                def _():
                    r[...] = v

                @pl.when(jnp.logical_not(first))
                def _():
                    r[...] += v
            else:
                r[...] = v

    return pl.pallas_call(
        body,
        grid=grid,
        in_specs=[pl.BlockSpec(bs, im) for _, bs, im in ins],
        out_specs=[pl.BlockSpec(bs, im) for _, _, bs, im in outs],
        out_shape=[jax.ShapeDtypeStruct(sh, dt) for sh, dt, _, _ in outs],
        name=name,
        compiler_params=pltpu.CompilerParams(
            dimension_semantics=("arbitrary",) * len(grid), vmem_limit_bytes=VMEM_LIMIT),
    )(*[a for a, _, _ in ins])


def _mm(name, a, b, *, ta=False, tb=False, out_dtype=F32, add=None, dep=None, tm=512, tn=1024, tk=1024):
    if ta:
        K, M = a.shape
    else:
        M, K = a.shape
    if tb:
        N, K2 = b.shape
    else:
        K2, N = b.shape
    assert K == K2, (name, a.shape, b.shape)
    tm, tn, tk = min(tm, M), min(tn, N), min(tk, K)
    assert M % tm == 0 and N % tn == 0 and K % tk == 0, (name, M, N, K)
    nk = K // tk
    dims = (((0 if ta else 1,), (1 if tb else 0,)), ((), ()))
    a_spec = pl.BlockSpec((tk, tm), lambda j, i, k: (k, i)) if ta else pl.BlockSpec((tm, tk), lambda j, i, k: (i, k))
    b_spec = pl.BlockSpec((tn, tk), lambda j, i, k: (j, k)) if tb else pl.BlockSpec((tk, tn), lambda j, i, k: (k, j))
    o_spec = pl.BlockSpec((tm, tn), lambda j, i, k: (i, j))
    has_add = add is not None
    n_in = 2 + has_add + (dep is not None)

    def body(*refs):
        a_ref, b_ref = refs[0], refs[1]
        add_ref = refs[2] if has_add else None
        o_ref = refs[n_in]
        p = lax.dot_general(a_ref[...].astype(BF16), b_ref[...].astype(BF16), dims, preferred_element_type=F32)
        if nk == 1:
            if has_add:
                p = p + add_ref[...]
            o_ref[...] = p.astype(out_dtype)
        else:
            acc = refs[-1]
            k = pl.program_id(2)

            @pl.when(k == 0)
            def _():
                acc[...] = p

            @pl.when(k > 0)
            def _():
                acc[...] += p

            @pl.when(k == nk - 1)
            def _():
                r = acc[...]
                if has_add:
                    r = r + add_ref[...]
                o_ref[...] = r.astype(out_dtype)

    in_specs = [a_spec, b_spec] + ([o_spec] if has_add else []) + ([pl.BlockSpec(memory_space=pl.ANY)] if dep is not None else [])
    args = [a, b] + ([add] if has_add else []) + ([dep] if dep is not None else [])
    return pl.pallas_call(
        body,
        grid=(N // tn, M // tm, nk),
        in_specs=in_specs,
        out_specs=o_spec,
        out_shape=jax.ShapeDtypeStruct((M, N), out_dtype),
        scratch_shapes=[pltpu.VMEM((tm, tn), F32)] if nk > 1 else [],
        name=name,
        compiler_params=pltpu.CompilerParams(
            dimension_semantics=("arbitrary", "arbitrary", "arbitrary"), vmem_limit_bytes=VMEM_LIMIT),
    )(*args)


def _vjp_of(f, n_diff):
    def g(*args, n_prim):
        prim = args[:n_diff]
        consts = args[n_diff:n_prim]
        cts = args[n_prim:]
        _, pull = jax.vjp(lambda *p: f(*p, *consts), *prim)
        out = jax.eval_shape(lambda *p: f(*p, *consts), *prim)
        if isinstance(out, (tuple, list)):
            cts = tuple(c.astype(o.dtype) for c, o in zip(cts, out))
        else:
            cts = cts[0].astype(out.dtype)
        return pull(cts)
    return g


def _rms(x, g, n=None):
    n = x.shape[-1] if n is None else n
    ms = jnp.sum(x * x, axis=-1, keepdims=True) / n
    return x * lax.rsqrt(ms + EPS) * g


def _silu(z):
    return z * jax.nn.sigmoid(z)


def _roll_rows(u, k):
    n = u.shape[0]
    r = pltpu.roll(u, k % n, 0)
    t = lax.broadcasted_iota(jnp.int32, u.shape, 0)
    if k > 0:
        return jnp.where(t >= k, r, 0.0)
    return jnp.where(t < n + k, r, 0.0)


@functools.partial(jax.custom_vjp, nondiff_argnums=(1,))
def _shift(u, k):
    return _roll_rows(u, k)


def _shift_fwd(u, k):
    return _roll_rows(u, k), None


def _shift_bwd(k, _, g):
    return (_roll_rows(g, -k),)


_shift.defvjp(_shift_fwd, _shift_bwd)


@functools.partial(jax.custom_vjp, nondiff_argnums=(1,))
def _lane_roll(u, k):
    return pltpu.roll(u, k % LANE, 1)


def _lane_roll_fwd(u, k):
    return pltpu.roll(u, k % LANE, 1), None


def _lane_roll_bwd(k, _, g):
    return (pltpu.roll(g, (-k) % LANE, 1),)


_lane_roll.defvjp(_lane_roll_fwd, _lane_roll_bwd)


def _conv_math(ab, ac, ax, az, cw, cb):
    u = ac * ax
    conv = cb + _shift(u, 2) * cw[0:1] + _shift(u, 1) * cw[1:2] + u * cw[2:3]
    return ab * conv * _silu(az)


def _mla_pre_math(cq, ckv, gq, gkv):
    return _rms(cq, gq), _rms(ckv, gkv)


def _rope_math(q, kn, kpe, gq, gk, c, s1, s2):
    lane = lax.broadcasted_iota(jnp.int32, kpe.shape, 1)
    pe = _lane_roll(jnp.where(lane < ROPE, kpe, 0.0), NOPE)

    def one(t, g):
        tn = _rms(t, g, QK)
        return tn * c + _lane_roll(tn, -16) * s1 + _lane_roll(tn, 16) * s2

    qs, ks = [], []
    for h in range(NH):
        sl = slice(h * QKP, (h + 1) * QKP)
        qs.append(one(q[:, sl], gq))
        ks.append(one(kn[:, sl] + pe, gk))
    return jnp.concatenate(qs, axis=1), jnp.concatenate(ks, axis=1)


def _gate_math(o, z):
    return o * _silu(z)


def _mergec_math(o0, o1, o2, l0, l1, l2, cz):
    m = lax.stop_gradient(jnp.maximum(jnp.maximum(l0, l1), l2))
    e0, e1, e2 = jnp.exp(l0 - m), jnp.exp(l1 - m), jnp.exp(l2 - m)
    den = e0 + e1 + e2
    oc = (e0 / den) * o0 + (e1 / den) * o1 + (e2 / den) * o2
    return oc * _silu(cz)


def _merge_math(g0, g1, g2, b0, b1, b2, pa, pb, pc):
    return (jax.nn.sigmoid(g0 + b0) * pa + jax.nn.sigmoid(g1 + b1) * pb) + jax.nn.sigmoid(g2 + b2) * pc


MLA_T = 256
MLA_UNROLL = True


def _mla_fwd(q, k, v):
    B = q.shape[0]
    T = MLA_T
    NB = S // T

    def body(q_ref, k_ref, v_ref, o_ref, l_ref):
        row = lax.broadcasted_iota(jnp.int32, (T, T), 0)
        col = lax.broadcasted_iota(jnp.int32, (T, T), 1)
        lo = _lo_mask((T, LANE))

        for qi in range(NB):
            qb = q_ref[qi * T:(qi + 1) * T, :]

            def step(j, carry, diagonal):
                m, l, acc = carry
                off = pl.multiple_of(j * T, T)
                kb = k_ref[pl.ds(off, T), :]
                vb = v_ref[pl.ds(off, T), :]
                ss = []
                for e in (0, 1):
                    se = _dot_nt(qb[:, e * QKP:(e + 1) * QKP], kb[:, e * QKP:(e + 1) * QKP]) * MLA_SCALE
                    ss.append(jnp.where(col <= row, se, NEG) if diagonal else se)
                s = jnp.concatenate(ss, axis=0)
                m_new = jnp.maximum(m, jnp.max(s, axis=-1, keepdims=True))
                a = jnp.exp(m - m_new)
                p = jnp.exp(s - m_new)
                l = a * l + jnp.sum(p, axis=-1, keepdims=True)
                acc = a * acc + _dot(p.astype(BF16), vb)
                return m_new, l, acc

            init = (jnp.full((2 * T, 1), NEG, F32), jnp.zeros((2 * T, 1), F32), jnp.zeros((2 * T, LANE), F32))
            carry = lax.fori_loop(0, qi, functools.partial(step, diagonal=False), init, unroll=MLA_UNROLL)
            m, l, acc = step(qi, carry, True)
            o = acc / l
            lse = m + jnp.log(l)
            o_ref[qi * T:(qi + 1) * T, :] = jnp.where(lo, o[:T], o[T:])
            l_ref[qi * T:(qi + 1) * T, :] = jnp.where(lo, lse[:T], lse[T:])

    def spec(w):
        return pl.BlockSpec((None, S, w), lambda b, hp: (b, 0, hp))

    return pl.pallas_call(
        body,
        grid=(B, NH // 2),
        in_specs=[spec(2 * QKP), spec(2 * QKP), spec(LANE)],
        out_specs=[spec(LANE), spec(LANE)],
        out_shape=[jax.ShapeDtypeStruct((B, S, NH * VD), F32)] * 2,
        name="mla_attn_fwd",
        compiler_params=pltpu.CompilerParams(dimension_semantics=("arbitrary",) * 2, vmem_limit_bytes=VMEM_LIMIT),
    )(q, k, v)


def _mla_bwd(q, k, v, do, o, lse):
    B = q.shape[0]
    T = MLA_T
    NB = S // T

    def body(q_ref, k_ref, v_ref, do_ref, o_ref, l_ref, dq_ref, dk_ref, dv_ref, delta_ref):
        delta_ref[...] = _head_sum(do_ref[...] * o_ref[...])
        row = lax.broadcasted_iota(jnp.int32, (T, T), 0)
        col = lax.broadcasted_iota(jnp.int32, (T, T), 1)
        lo = _lo_mask((T, LANE))

        for j in range(NB):
            krows = slice(j * T, (j + 1) * T)
            kb = k_ref[krows, :]
            vb = v_ref[krows, :]
            dk = [jnp.zeros((T, QKP), F32), jnp.zeros((T, QKP), F32)]
            dv = jnp.zeros((T, LANE), F32)
            for i in range(j, NB):
                qrows = slice(i * T, (i + 1) * T)
                qb = q_ref[qrows, :]
                do2 = _stack_heads(do_ref[qrows, :], lo).astype(BF16)
                lb = l_ref[qrows, :]
                db = delta_ref[qrows, :]
                dp2 = _dot_nt(do2, vb)
                for e in (0, 1):
                    cols = slice(e * QKP, (e + 1) * QKP)
                    qe, ke = qb[:, cols], kb[:, cols]
                    s = _dot_nt(qe, ke) * MLA_SCALE
                    if i == j:
                        s = jnp.where(col <= row, s, NEG)
                    p = jnp.exp(s - lb[:, e * HD:e * HD + 1])
                    dv = dv + _dot_tn(p.astype(BF16), do2[e * T:(e + 1) * T])
                    ds = (p * (dp2[e * T:(e + 1) * T] - db[:, e * HD:e * HD + 1]) * MLA_SCALE).astype(BF16)
                    dk[e] = dk[e] + _dot_tn(ds, qe)
                    if j == 0:
                        dq_ref[qrows, cols] = _dot(ds, ke)
                    else:
                        dq_ref[qrows, cols] += _dot(ds, ke)
            dk_ref[krows, 0:QKP] = dk[0]
            dk_ref[krows, QKP:2 * QKP] = dk[1]
            dv_ref[krows, :] = dv

    def spec(w):
        return pl.BlockSpec((None, S, w), lambda b, hp: (b, 0, hp))

    return pl.pallas_call(
        body,
        grid=(B, NH // 2),
        in_specs=[spec(2 * QKP), spec(2 * QKP), spec(LANE), spec(LANE), spec(LANE), spec(LANE)],
        out_specs=[spec(2 * QKP), spec(2 * QKP), spec(LANE)],
        out_shape=[jax.ShapeDtypeStruct((B, S, NH * QKP), F32), jax.ShapeDtypeStruct((B, S, NH * QKP), F32),
                   jax.ShapeDtypeStruct((B, S, NH * VD), F32)],
        scratch_shapes=[pltpu.VMEM((S, LANE), F32)],
        name="mla_attn_bwd",
        compiler_params=pltpu.CompilerParams(dimension_semantics=("arbitrary",) * 2, vmem_limit_bytes=VMEM_LIMIT),
    )(q, k, v, do, o, lse)


def _lo_mask(shape):
    return lax.broadcasted_iota(jnp.int32, shape, len(shape) - 1) < HD


def _head_sum(u):
    r = lax.broadcasted_iota(jnp.int32, (LANE, LANE), 0) < HD
    c = lax.broadcasted_iota(jnp.int32, (LANE, LANE), 1) < HD
    ones = jnp.where(r == c, 1.0, 0.0).astype(BF16)
    hi = u.astype(BF16)
    lo = (u - hi.astype(F32)).astype(BF16)
    return _dot(hi, ones) + _dot(lo, ones)


def _rms2(x, g):
    return x * lax.rsqrt(_head_sum(x * x) / HD + EPS) * g


def _dil_bias(t_ref, gi, d):
    qq = lax.broadcasted_iota(jnp.int32, (QB, QB), 0)
    kk = lax.broadcasted_iota(jnp.int32, (QB, QB), 1)
    jc = (qq - kk).astype(F32)
    rows = []
    for e in (0, 1):
        sl = t_ref[2 * gi + e:2 * gi + e + 1, :] * float(d)
        bp = jnp.where(kk >= qq, -sl * (jc + float(QB)), NEG)
        bc = jnp.where(kk <= qq, -sl * jc, NEG)
        rows.append(jnp.concatenate([bp, bc], axis=1))
    return jnp.concatenate(rows, axis=0)


def _dil_rows(cur, d):
    return pl.ds(cur, QB, stride=d) if d > 1 else pl.ds(pl.multiple_of(cur, QB), QB)


def _dil_walk(d, block, full):
    if d == 1:
        block(0, None)

        def body(i, c):
            block(i * QB, (i - 1) * QB)
            return c
        lax.fori_loop(1, S // QB, body, 0, unroll=True if full else 5)
    elif d == 16:
        def body(r, c):
            block(r, None)
            return c
        lax.fori_loop(0, d, body, 0, unroll=True if full else 4)
    else:
        nb = S // d // QB

        def cls(r, c):
            block(r, None)

            def body(i, c2):
                block(r + i * QB * d, r + (i - 1) * QB * d)
                return c2
            lax.fori_loop(1, nb, body, 0, unroll=True)
            return c
        lax.fori_loop(0, d, cls, 0, unroll=full)


def _stack_heads(x, lo):
    return jnp.concatenate([jnp.where(lo, x, 0.0), jnp.where(lo, 0.0, x)], axis=0)


def _dilc_fwd(proj3, gq, gk, tab):
    B = proj3.shape[0]

    def body(q_ref, k_ref, v_ref, cz_ref, gq_ref, gk_ref, t_ref, y_ref, o_ref, l_ref, qs, ks, vs):
        g = pl.program_id(2)
        lo = _lo_mask((QB, LANE))

        def group(gi):
            d = DIL[gi][1]
            qs[...] = _rms2(q_ref[...].astype(F32), gq_ref[gi:gi + 1, :])
            ks[...] = _rms2(k_ref[...].astype(F32), gk_ref[gi:gi + 1, :])
            vs[...] = v_ref[...].astype(F32)
            bias = _dil_bias(t_ref, gi, d)

            def block(cur, prev):
                rows = _dil_rows(cur, d)
                q2 = _stack_heads(qs[rows, :], lo).astype(BF16)
                kc, vc = ks[rows, :], vs[rows, :]
                if prev is None:
                    kcat, vcat, b = kc, vc, bias[:, QB:]
                else:
                    prow = _dil_rows(prev, d)
                    kcat = jnp.concatenate([ks[prow, :], kc], axis=0)
                    vcat = jnp.concatenate([vs[prow, :], vc], axis=0)
                    b = bias
                s = _dot_nt(q2, kcat.astype(BF16)) * DIL_SCALE + b
                m = jnp.max(s, axis=-1, keepdims=True)
                p = jnp.exp(s - m)
                l = jnp.sum(p, axis=-1, keepdims=True)
                o = _dot(p.astype(BF16), vcat.astype(BF16)) / l
                lse = m + jnp.log(l)
                o_ref[gi, rows, :] = jnp.where(lo, o[:QB], o[QB:])
                l_ref[gi, rows, :] = jnp.where(lo, lse[:QB], lse[QB:])

            _dil_walk(d, block, True)

        for gi in range(NG):
            pl.when(g == gi)(functools.partial(group, gi))

        @pl.when(g == NG - 1)
        def _():
            y_ref[...] = _mergec_math(o_ref[0], o_ref[1], o_ref[2], l_ref[0], l_ref[1], l_ref[2],
                                      cz_ref[...].astype(F32)).astype(BF16)

    def col(base):
        return pl.BlockSpec((None, S, LANE), lambda b, hp, g: (b, 0, base // LANE + 4 * g + hp))

    gspec = pl.BlockSpec((NG, LANE), lambda b, hp, g: (0, 0))
    saved = pl.BlockSpec((NG, None, S, LANE), lambda b, hp, g: (0, b, 0, hp))
    return pl.pallas_call(
        body,
        grid=(B, 4, NG),
        in_specs=[col(O_DQ), col(O_DK), col(O_DV),
                  pl.BlockSpec((None, S, LANE), lambda b, hp, g: (b, 0, O_CZ // LANE + hp)),
                  gspec, gspec, pl.BlockSpec((None, 8, LANE), lambda b, hp, g: (hp, 0, 0))],
        out_specs=[pl.BlockSpec((None, S, LANE), lambda b, hp, g: (b, 0, hp)), saved, saved],
        out_shape=[jax.ShapeDtypeStruct((B, S, DWID), BF16), jax.ShapeDtypeStruct((NG, B, S, DWID), F32),
                   jax.ShapeDtypeStruct((NG, B, S, DWID), F32)],
        scratch_shapes=[pltpu.VMEM((S, LANE), F32)] * 3,
        name="dil_mixer_fwd",
        compiler_params=pltpu.CompilerParams(dimension_semantics=("arbitrary",) * 3, vmem_limit_bytes=VMEM_LIMIT),
    )(proj3, proj3, proj3, proj3, gq, gk, tab)


def _dilc_bwd(proj3, gq, gk, tab, o_all, l_all, d_yc):
    B = proj3.shape[0]

    def body(q_ref, k_ref, v_ref, cz_ref, gq_ref, gk_ref, t_ref, o_ref, l_ref, dy_ref,
             dq_out, dk_out, dv_out, dcz_out, dgq_out, dgk_out, qs, ks, vs, dos, dls, dqs, dks, dvs):
        g = pl.program_id(2)
        lo = _lo_mask((QB, LANE))

        @pl.when(jnp.logical_and(jnp.logical_and(pl.program_id(0) == 0, pl.program_id(1) == 0), g == 0))
        def _():
            dgq_out[...] = jnp.zeros((NG, LANE), F32)
            dgk_out[...] = jnp.zeros((NG, LANE), F32)

        def group(gi):
            d = DIL[gi][1]
            ls = [l_ref[j] for j in range(NG)]
            m = jnp.maximum(jnp.maximum(ls[0], ls[1]), ls[2])
            es = [jnp.exp(t - m) for t in ls]
            den = (es[0] + es[1]) + es[2]
            al = [e / den for e in es]
            os_ = [o_ref[j] for j in range(NG)]
            oc = (al[0] * os_[0] + al[1] * os_[1]) + al[2] * os_[2]
            cz = cz_ref[...].astype(F32)
            sg = jax.nn.sigmoid(cz)
            dy = dy_ref[...]
            d_oc = dy * (cz * sg)
            dcz_out[...] = (dy * oc * (sg * (1.0 + cz * (1.0 - sg)))).astype(BF16)
            ts = [_head_sum(d_oc * os_[j]) for j in range(NG)]
            tbar = (al[0] * ts[0] + al[1] * ts[1]) + al[2] * ts[2]
            dos[...] = al[gi] * d_oc
            dls[...] = al[gi] * (ts[gi] - tbar)

            qs[...] = _rms2(q_ref[...].astype(F32), gq_ref[gi:gi + 1, :])
            ks[...] = _rms2(k_ref[...].astype(F32), gk_ref[gi:gi + 1, :])
            vs[...] = v_ref[...].astype(F32)
            dks[...] = jnp.zeros((S, LANE), F32)
            dvs[...] = jnp.zeros((S, LANE), F32)
            bias = _dil_bias(t_ref, gi, d)

            def block(cur, prev):
                rows = _dil_rows(cur, d)
                q2 = _stack_heads(qs[rows, :], lo).astype(BF16)
                dob = dos[rows, :]
                do2 = _stack_heads(dob, lo).astype(BF16)
                kc, vc = ks[rows, :], vs[rows, :]
                if prev is None:
                    kcat, vcat, b = kc, vc, bias[:, QB:]
                else:
                    prow = _dil_rows(prev, d)
                    kcat = jnp.concatenate([ks[prow, :], kc], axis=0)
                    vcat = jnp.concatenate([vs[prow, :], vc], axis=0)
                    b = bias
                kcat = kcat.astype(BF16)
                vcat = vcat.astype(BF16)
                lse_b = l_ref[gi, rows, :]
                corr_b = dls[rows, :] - _head_sum(dob * o_ref[gi, rows, :])
                lse2 = jnp.concatenate([lse_b[:, 0:1], lse_b[:, HD:HD + 1]], axis=0)
                corr2 = jnp.concatenate([corr_b[:, 0:1], corr_b[:, HD:HD + 1]], axis=0)
                s = _dot_nt(q2, kcat) * DIL_SCALE + b
                p = jnp.exp(s - lse2)
                ds = (p * (_dot_nt(do2, vcat) + corr2) * DIL_SCALE).astype(BF16)
                dq2 = _dot(ds, kcat)
                dqs[rows, :] = jnp.where(lo, dq2[:QB], dq2[QB:])
                dk = _dot_tn(ds, q2)
                dv = _dot_tn(p.astype(BF16), do2)
                if prev is None:
                    dks[rows, :] += dk
                    dvs[rows, :] += dv
                else:
                    dks[prow, :] += dk[:QB]
                    dvs[prow, :] += dv[:QB]
                    dks[rows, :] += dk[QB:]
                    dvs[rows, :] += dv[QB:]

            _dil_walk(d, block, False)

            _, pull_q = jax.vjp(_rms2, q_ref[...].astype(F32), gq_ref[gi:gi + 1, :])
            dxq, dgq = pull_q(dqs[...])
            dq_out[...] = dxq.astype(BF16)
            dgq_out[gi:gi + 1, :] += dgq
            _, pull_k = jax.vjp(_rms2, k_ref[...].astype(F32), gk_ref[gi:gi + 1, :])
            dxk, dgk = pull_k(dks[...])
            dk_out[...] = dxk.astype(BF16)
            dgk_out[gi:gi + 1, :] += dgk
            dv_out[...] = dvs[...].astype(BF16)

        for gi in range(NG):
            pl.when(g == gi)(functools.partial(group, gi))

    def col(base):
        return pl.BlockSpec((None, S, LANE), lambda b, hp, g: (b, 0, base // LANE + 4 * g + hp))

    gspec = pl.BlockSpec((NG, LANE), lambda b, hp, g: (0, 0))
    saved = pl.BlockSpec((NG, None, S, LANE), lambda b, hp, g: (0, b, 0, hp))
    per_pair = pl.BlockSpec((None, S, LANE), lambda b, hp, g: (b, 0, hp))
    dcol = pl.BlockSpec((None, S, LANE), lambda b, hp, g: (b, 0, 4 * g + hp))
    return pl.pallas_call(
        body,
        grid=(B, 4, NG),
        in_specs=[col(O_DQ), col(O_DK), col(O_DV),
                  pl.BlockSpec((None, S, LANE), lambda b, hp, g: (b, 0, O_CZ // LANE + hp)),
                  gspec, gspec, pl.BlockSpec((None, 8, LANE), lambda b, hp, g: (hp, 0, 0)),
                  saved, saved, per_pair],
        out_specs=[dcol, dcol, dcol, per_pair, gspec, gspec],
        out_shape=[jax.ShapeDtypeStruct((B, S, NG * DWID), BF16)] * 3
        + [jax.ShapeDtypeStruct((B, S, DWID), BF16), jax.ShapeDtypeStruct((NG, LANE), F32),
           jax.ShapeDtypeStruct((NG, LANE), F32)],
        scratch_shapes=[pltpu.VMEM((S, LANE), F32)] * 8,
        name="dil_mixer_bwd",
        compiler_params=pltpu.CompilerParams(dimension_semantics=("arbitrary",) * 3, vmem_limit_bytes=VMEM_LIMIT),
    )(proj3, proj3, proj3, proj3, gq, gk, tab, o_all, l_all, d_yc)


def _dil_slopes():
    slopes = (2.0 ** (-8.0 * np.arange(1, NG * DH + 1, dtype=np.float32) / (NG * DH))).astype(np.float32).reshape(NG, DH)
    tab = np.zeros((4, 8, LANE), np.float32)
    for hp in range(4):
        for gi in range(NG):
            for e in (0, 1):
                tab[hp, 2 * gi + e, :] = slopes[gi, 2 * hp + e]
    return jnp.asarray(tab)


def _rope_tables():
    inv = ROPE_THETA ** (-jnp.arange(0, ROPE, 2, dtype=F32) / ROPE)
    ang = jnp.arange(S, dtype=F32)[:, None] * inv[None, :]
    cos, sin = jnp.cos(ang), jnp.sin(ang)
    z16 = jnp.zeros((S, 16), F32)
    c = jnp.concatenate([jnp.ones((S, NOPE), F32), cos, cos, jnp.zeros((S, 32), F32)], axis=1)
    s1 = jnp.concatenate([jnp.zeros((S, NOPE), F32), -sin, z16, jnp.zeros((S, 32), F32)], axis=1)
    s2 = jnp.concatenate([jnp.zeros((S, NOPE), F32), z16, sin, jnp.zeros((S, 32), F32)], axis=1)
    return c, s1, s2


def _pad_heads_uq(w):
    return jnp.pad(w.reshape(QL, NH, QK), ((0, 0), (0, 0), (0, QKP - QK))).reshape(QL, NH * QKP)


def _unpad_heads_uq(g):
    return g.reshape(QL, NH, QKP)[:, :, :QK].reshape(QL, NH * QK)


def _split_ukv(w):
    w3 = w.reshape(KVL, NH, NOPE + VD)
    uk = jnp.pad(w3[:, :, :NOPE], ((0, 0), (0, 0), (0, QKP - NOPE))).reshape(KVL, NH * QKP)
    return uk, w3[:, :, NOPE:].reshape(KVL, NH * VD)


def _join_ukv(guk, guv):
    return jnp.concatenate([guk.reshape(KVL, NH, QKP)[:, :, :NOPE], guv.reshape(KVL, NH, VD)],
                           axis=-1).reshape(KVL, NH * (NOPE + VD))


BR = 512
BRM = 256


def _layer_fwd(x, w, tabs, batch):
    T = batch * S
    rope_c, rope_s1, rope_s2, dil_tab = tabs
    res = {"x": x}
    row = lambda c: (lambda i: (i, c))
    fix = lambda i: (0, 0)

    h = _pcall("norm_fwd", _rms, (T // BR,),
               [(x, (BR, D), row(0)), (w["norm_g"], (1, D), fix)],
               [((T, D), BF16, (BR, D), row(0))])[0]
    proj = _mm("in_proj", h, w["w_in_t"], tb=True, out_dtype=BF16, tm=512, tn=1024)
    res["h"], res["proj"] = h, proj
    proj3 = proj.reshape(batch, S, NINP)

    cblk = lambda s: (lambda j, b: (b, 0, 4 * s + j))
    y_a = _pcall("conv_fwd", _conv_math, (4, batch),
                 [(proj3, (None, S, LANE), cblk(0)), (proj3, (None, S, LANE), cblk(1)),
                  (proj3, (None, S, LANE), cblk(2)), (proj3, (None, S, LANE), cblk(3)),
                  (w["conv_w"], (3, LANE), lambda j, b: (0, j)), (w["conv_b"], (1, LANE), lambda j, b: (0, j))],
                 [((batch, S, CW), BF16, (None, S, LANE), lambda j, b: (b, 0, j))])[0].reshape(T, CW)
    res["y_a"] = y_a

    cqn, ckvn = _pcall("mla_pre_fwd", _mla_pre_math, (T // BR,),
                       [(proj, (BR, QL), row(O_CQ // QL)), (proj, (BR, KVL), row(O_CKV // KVL)),
                        (w["q_a_norm_g"], (1, QL), fix), (w["kv_a_norm_g"], (1, KVL), fix)],
                       [((T, QL), BF16, (BR, QL), row(0)), ((T, KVL), BF16, (BR, KVL), row(0))])
    w_uq_p = _pad_heads_uq(w["w_uq"])
    w_uk, w_uv = _split_ukv(w["w_ukv"])
    q = _mm("uq", cqn, w_uq_p, out_dtype=BF16)
    kn = _mm("uk", ckvn, w_uk, out_dtype=BF16)
    v = _mm("uv", ckvn, w_uv, out_dtype=BF16)
    nrr = S // BR
    tab_row = lambda i: (i % nrr, 0)
    qr, kr = _pcall("rope_fwd", _rope_math, (T // BR,),
                    [(q, (BR, NH * QKP), row(0)), (kn, (BR, NH * QKP), row(0)), (proj, (BR, LANE), row(O_KPE // LANE)),
                     (w["mla_q_norm_g"], (1, QKP), fix), (w["mla_k_norm_g"], (1, QKP), fix),
                     (rope_c, (BR, QKP), tab_row), (rope_s1, (BR, QKP), tab_row), (rope_s2, (BR, QKP), tab_row)],
                    [((T, NH * QKP), BF16, (BR, NH * QKP), row(0))] * 2)
    qr = qr.reshape(batch, S, NH * QKP)
    kr = kr.reshape(batch, S, NH * QKP)
    v = v.reshape(batch, S, NH * VD)
    o_b, l_b = _mla_fwd(qr, kr, v)
    ob2 = o_b.reshape(T, NH * VD)
    y_b = _pcall("gateb_fwd", _gate_math, (T // BR,),
                 [(ob2, (BR, 512), row(0)), (proj, (BR, 512), row(O_BZ // 512))],
                 [((T, 512), BF16, (BR, 512), row(0))])[0]
    res.update(cqn=cqn, ckvn=ckvn, q=q, kn=kn, qr=qr, kr=kr, v=v, o_b=o_b, l_b=l_b, ob2=ob2, y_b=y_b,
               w_uq_p=w_uq_p, w_uk=w_uk, w_uv=w_uv)

    gq2 = jnp.tile(w["dil_q_norm_g"].reshape(NG, HD), (1, 2))
    gk2 = jnp.tile(w["dil_k_norm_g"].reshape(NG, HD), (1, 2))
    y_c, o_all, l_all = _dilc_fwd(proj3, gq2, gk2, dil_tab)
    y_c = y_c.reshape(T, DWID)
    res.update(o_all=o_all, l_all=l_all, y_c=y_c)

    pa = _mm("out_a", y_a, w["w_out_a"], out_dtype=BF16)
    pb = _mm("out_b", y_b, w["w_out_b"], out_dtype=BF16)
    pc = _mm("out_c", y_c, w["w_out_c"], out_dtype=BF16)
    merged = _pcall("merge_fwd", _merge_math, (T // BRM,),
                    [(proj, (BRM, D), row(O_G // D + s)) for s in range(3)]
                    + [(w["b_gate"], (1, D), (lambda s: (lambda i: (0, s)))(s)) for s in range(3)]
                    + [(t, (BRM, D), row(0)) for t in (pa, pb, pc)],
                    [((T, D), BF16, (BRM, D), row(0))])[0]
    out = _mm("o_proj", merged, w["w_o"], add=x)
    res.update(pa=pa, pb=pb, pc=pc, merged=merged)
    return out, res


def _norm_bwd_math(x, g, dh, dy):
    _, pull = jax.vjp(_rms, x, g)
    dx, dg = pull(dh)
    return dx + dy, dg


def _layer_bwd(dy, w, res, tabs, batch, after_dw=None):
    T = batch * S
    rope_c, rope_s1, rope_s2, dil_tab = tabs
    row = lambda c: (lambda i: (i, c))
    fix = lambda i: (0, 0)
    x, proj, h = res["x"], res["proj"], res["h"]
    proj3 = proj.reshape(batch, S, NINP)
    g = {}

    d_merged = _mm("o_proj_dx", dy, w["w_o"], tb=True)
    g["w_o"] = _mm("o_proj_dw", res["merged"], dy, ta=True, tm=1024)

    merge_bwd = functools.partial(_vjp_of(_merge_math, 9), n_prim=9)
    dg0, dg1, dg2, db0, db1, db2, dpa, dpb, dpc = _pcall(
        "merge_bwd", merge_bwd, (T // BRM,),
        [(proj, (BRM, D), row(O_G // D + s)) for s in range(3)]
        + [(w["b_gate"], (1, D), (lambda s: (lambda i: (0, s)))(s)) for s in range(3)]
        + [(t, (BRM, D), row(0)) for t in (res["pa"], res["pb"], res["pc"])]
        + [(d_merged, (BRM, D), row(0))],
        [((T, D), BF16, (BRM, D), row(0))] * 3 + [((1, D), F32, (1, D), fix, True)] * 3
        + [((T, D), BF16, (BRM, D), row(0))] * 3)
    g["b_gate"] = jnp.concatenate([db0, db1, db2], axis=1)

    d_ya = _mm("out_a_dx", dpa, w["w_out_a"], tb=True)
    d_yb = _mm("out_b_dx", dpb, w["w_out_b"], tb=True)
    d_yc = _mm("out_c_dx", dpc, w["w_out_c"], tb=True)
    g["w_out_a"] = _mm("out_a_dw", res["y_a"], dpa, ta=True)
    g["w_out_b"] = _mm("out_b_dw", res["y_b"], dpb, ta=True)
    g["w_out_c"] = _mm("out_c_dw", res["y_c"], dpc, ta=True)

    cblk = lambda s: (lambda j, b: (b, 0, 4 * s + j))
    oblk = lambda j, b: (b, 0, j)
    conv_bwd = functools.partial(_vjp_of(_conv_math, 6), n_prim=6)
    d_ab, d_ac, d_ax, d_az, g["conv_w"], g["conv_b"] = _pcall(
        "conv_bwd", conv_bwd, (4, batch),
        [(proj3, (None, S, LANE), cblk(s)) for s in range(4)]
        + [(w["conv_w"], (3, LANE), lambda j, b: (0, j)), (w["conv_b"], (1, LANE), lambda j, b: (0, j)),
           (d_ya.reshape(batch, S, CW), (None, S, LANE), oblk)],
        [((batch, S, CW), BF16, (None, S, LANE), oblk)] * 4
        + [((3, CW), F32, (3, LANE), lambda j, b: (0, j), True), ((1, CW), F32, (1, LANE), lambda j, b: (0, j), True)])

    gate_bwd = functools.partial(_vjp_of(_gate_math, 2), n_prim=2)
    d_ob, d_bz = _pcall("gateb_bwd", gate_bwd, (T // BR,),
                        [(res["ob2"], (BR, 512), row(0)), (proj, (BR, 512), row(O_BZ // 512)), (d_yb, (BR, 512), row(0))],
                        [((T, 512), F32, (BR, 512), row(0)), ((T, 512), BF16, (BR, 512), row(0))])
    dqr, dkr, dv = _mla_bwd(res["qr"], res["kr"], res["v"], d_ob.reshape(batch, S, NH * VD), res["o_b"], res["l_b"])
    nrr = S // BR
    tab_row = lambda i: (i % nrr, 0)
    rope_bwd = functools.partial(_vjp_of(_rope_math, 5), n_prim=8)
    d_q, d_kn, d_kpe_p, g["mla_q_norm_g"], g["mla_k_norm_g"] = _pcall(
        "rope_bwd", rope_bwd, (T // BR,),
        [(res["q"], (BR, NH * QKP), row(0)), (res["kn"], (BR, NH * QKP), row(0)), (proj, (BR, LANE), row(O_KPE // LANE)),
         (w["mla_q_norm_g"], (1, QKP), fix), (w["mla_k_norm_g"], (1, QKP), fix),
         (rope_c, (BR, QKP), tab_row), (rope_s1, (BR, QKP), tab_row), (rope_s2, (BR, QKP), tab_row),
         (dqr.reshape(T, NH * QKP), (BR, NH * QKP), row(0)), (dkr.reshape(T, NH * QKP), (BR, NH * QKP), row(0))],
        [((T, NH * QKP), BF16, (BR, NH * QKP), row(0))] * 2 + [((T, LANE), BF16, (BR, LANE), row(0))]
        + [((1, QKP), F32, (1, QKP), fix, True)] * 2)
    dv = dv.reshape(T, NH * VD)
    d_cqn = _mm("uq_dx", d_q, res["w_uq_p"], tb=True)
    d_ckvn = _mm("uk_dx", d_kn, res["w_uk"], tb=True)
    d_ckvn = _mm("uv_dx", dv, res["w_uv"], tb=True, add=d_ckvn)
    g["w_uq"] = _unpad_heads_uq(_mm("uq_dw", res["cqn"], d_q, ta=True))
    g["w_ukv"] = _join_ukv(_mm("uk_dw", res["ckvn"], d_kn, ta=True), _mm("uv_dw", res["ckvn"], dv, ta=True))
    pre_bwd = functools.partial(_vjp_of(_mla_pre_math, 4), n_prim=4)
    d_cq, d_ckv, g["q_a_norm_g"], g["kv_a_norm_g"] = _pcall(
        "mla_pre_bwd", pre_bwd, (T // BR,),
        [(proj, (BR, QL), row(O_CQ // QL)), (proj, (BR, KVL), row(O_CKV // KVL)),
         (w["q_a_norm_g"], (1, QL), fix), (w["kv_a_norm_g"], (1, KVL), fix),
         (d_cqn, (BR, QL), row(0)), (d_ckvn, (BR, KVL), row(0))],
        [((T, QL), BF16, (BR, QL), row(0)), ((T, KVL), BF16, (BR, KVL), row(0)),
         ((1, QL), F32, (1, QL), fix, True), ((1, KVL), F32, (1, KVL), fix, True)])

    gq2 = jnp.tile(w["dil_q_norm_g"].reshape(NG, HD), (1, 2))
    gk2 = jnp.tile(w["dil_k_norm_g"].reshape(NG, HD), (1, 2))
    d_dq, d_dk, d_dv, d_cz, dgq, dgk = _dilc_bwd(proj3, gq2, gk2, dil_tab, res["o_all"], res["l_all"],
                                                 d_yc.reshape(batch, S, DWID))
    g["dil_q_norm_g"] = dgq[:, :HD] + dgq[:, HD:]
    g["dil_k_norm_g"] = dgk[:, :HD] + dgk[:, HD:]
    d_dq, d_dk, d_dv = (t.reshape(T, NG * DWID) for t in (d_dq, d_dk, d_dv))
    d_cz = d_cz.reshape(T, DWID)

    dproj = jnp.concatenate(
        [t.reshape(T, CW) for t in (d_ab, d_ac, d_ax, d_az)]
        + [d_cq, d_ckv, d_kpe_p, d_bz, d_dq, d_dk, d_dv, d_cz, dg0, dg1, dg2], axis=1)
    g["w_in_t"] = _mm("in_proj_dw", dproj, h, ta=True, tm=1024)
    dep = after_dw(g) if after_dw is not None else None
    d_h = _mm("in_proj_dx", dproj, w["w_in_t"], dep=dep, tm=1024)
    dx, g["norm_g"] = _pcall("norm_bwd", _norm_bwd_math, (T // BR,),
                             [(x, (BR, D), row(0)), (w["norm_g"], (1, D), fix), (d_h, (BR, D), row(0)),
                              (dy, (BR, D), row(0))],
                             [((T, D), F32, (BR, D), row(0)), ((1, D), F32, (1, D), fix, True)])
    return dx, g


def _loss_math(y, t):
    e = y - t
    return e * (1.0 / D), 0.5 * jnp.sum(jnp.sum(e * e, axis=-1, keepdims=True) / D, axis=0, keepdims=True)


def _local_step(x, target, ws, batch):
    T = batch * S
    tabs = _rope_tables() + (_dil_slopes(),)
    saved = []
    y = x
    for l in range(NL):
        y, res = _layer_fwd(y, ws[l], tabs, batch)
        saved.append(res)
    row = lambda i: (i, 0)
    dy, loss = _pcall("loss", _loss_math, (T // BR,),
                      [(y, (BR, D), row), (target, (BR, D), row)],
                      [((T, D), F32, (BR, D), row), ((1, 1), F32, (1, 1), lambda i: (0, 0), True)])
    grads = [None] * NL
    for l in reversed(range(NL)):
        dy, grads[l] = _layer_bwd(dy, ws[l], saved[l], tabs, batch)
    return loss, dy, grads


ANY = pl.BlockSpec(memory_space=pl.ANY)
U32 = jnp.uint32
WSH = NIN // 4
WA = KPE_END
WB = WSH - WA
CWD = 512
PACK_ROWS = 1472
HW = PACK_W // 2


def _me():
    return lax.axis_index("x"), lax.axis_index("y"), lax.axis_index("c")


def _piece_rows(k):
    a = k * WSH + jnp.where(k > 0, NINP - NIN, 0)
    b = k * WSH + WA + (NINP - NIN)
    return ((0, pl.multiple_of(a, 8), WA), (WA, pl.multiple_of(b, 8), WB))


def _pack_words(lo, hi):
    ul = lax.bitcast_convert_type(lo.astype(BF16).astype(F32), U32)
    uh = lax.bitcast_convert_type(hi.astype(BF16).astype(F32), U32)
    w = jnp.bitwise_or(jnp.bitwise_and(uh, jnp.uint32(0xFFFF0000)), jnp.right_shift(ul, jnp.uint32(16)))
    return lax.bitcast_convert_type(w, F32)


def _unpack_words(w):
    w = lax.bitcast_convert_type(w, U32)
    lo = lax.bitcast_convert_type(jnp.left_shift(w, jnp.uint32(16)), F32)
    hi = lax.bitcast_convert_type(jnp.bitwise_and(w, jnp.uint32(0xFFFF0000)), F32)
    return lo, hi


def _all_gather(wc, sp):
    def body(w_ref, s_ref, ow_ref, os_ref, send_sems, recv_sems):
        x, y, c = _me()
        k_me = 2 * x + y
        sib = (x, y, 1 - c)
        chips = [(1 - x, y), (x, 1 - y), (1 - x, 1 - y)]
        wcols = lambda cc: pl.ds(pl.multiple_of(cc * (CWD // 2), LANE), CWD // 2)
        scols = lambda cc: pl.ds(pl.multiple_of(cc * HW, LANE), HW)

        def windows(k, cc):
            pcs = _piece_rows(k)
            return ([(w_ref.at[pl.ds(l0, n), wcols(cc)], ow_ref.at[pl.ds(p0, n), wcols(cc)]) for l0, p0, n in pcs]
                    + [(s_ref.at[:, scols(cc)], os_ref.at[k, :, scols(cc)])])

        def copy(i, src, dst, to):
            return pltpu.make_async_remote_copy(src_ref=src, dst_ref=dst, send_sem=send_sems.at[i],
                                                recv_sem=recv_sems.at[i], device_id=to, device_id_type=MESH)

        def own_windows():
            return ([(w_ref.at[pl.ds(l0, n)], ow_ref.at[pl.ds(p0, n)]) for l0, p0, n in _piece_rows(k_me)]
                    + [(s_ref, os_ref.at[k_me])])

        first = [copy(18 + i, src, dst, sib) for i, (src, dst) in enumerate(own_windows())]
        for j, (cx, cy) in enumerate(chips):
            for i, (src, dst) in enumerate(windows(k_me, c)):
                first.append(copy(3 * j + i, src, dst, (cx, cy, c)))
        for cp in first:
            cp.start()
        passed = []
        for j, (cx, cy) in enumerate(chips):
            for i, (_, dst) in enumerate(windows(2 * cx + cy, c)):
                copy(3 * j + i, dst, dst, (cx, cy, c)).wait_recv()
                cp = copy(9 + 3 * j + i, dst, dst, sib)
                cp.start()
                passed.append(cp)
        for j, (cx, cy) in enumerate(chips):
            for i, (_, dst) in enumerate(windows(2 * cx + cy, 1 - c)):
                copy(9 + 3 * j + i, dst, dst, sib).wait_recv()
        for i, (_, dst) in enumerate(own_windows()):
            copy(18 + i, dst, dst, sib).wait_recv()
        for cp in first + passed:
            cp.wait_send()

    return pl.pallas_call(
        body,
        out_shape=[jax.ShapeDtypeStruct((NINP, CWD), F32), jax.ShapeDtypeStruct((4, PACK_ROWS, PACK_W), BF16)],
        in_specs=[ANY, ANY], out_specs=[ANY, ANY],
        scratch_shapes=[pltpu.SemaphoreType.DMA((21,)), pltpu.SemaphoreType.DMA((21,))],
        name="weights_all_gather",
    )(wc, sp)


HBM = pl.BlockSpec(memory_space=pltpu.HBM)
SEM = pl.BlockSpec(memory_space=pltpu.SEMAPHORE)
EFFECT = pltpu.SideEffectType.DATAFLOW_SIDE_EFFECTING


def _in_hbm(a):
    return pltpu.with_memory_space_constraint(a, pltpu.HBM)


def _ag_windows(w_ref, s_ref, lw_ref, ls_ref, k, cc):
    wcols = pl.ds(pl.multiple_of(cc * (CWD // 2), LANE), CWD // 2)
    scols = pl.ds(pl.multiple_of(cc * HW, LANE), HW)
    return ([(w_ref.at[pl.ds(l0, n), wcols], lw_ref.at[pl.ds(p0, n), wcols]) for l0, p0, n in _piece_rows(k)]
            + [(s_ref.at[:, scols], ls_ref.at[k, :, scols])])


def _ag_ici_copies(w_ref, s_ref, lw_ref, ls_ref, send_sems, recv_sems):
    x, y, c = _me()
    mine, theirs = [], []
    for j, (cx, cy) in enumerate([(1 - x, y), (x, 1 - y), (1 - x, 1 - y)]):
        for i, ((src, dst), (_, got)) in enumerate(zip(_ag_windows(w_ref, s_ref, lw_ref, ls_ref, 2 * x + y, c),
                                                       _ag_windows(w_ref, s_ref, lw_ref, ls_ref, 2 * cx + cy, c))):
            mk = lambda s_, d_: pltpu.make_async_remote_copy(
                src_ref=s_, dst_ref=d_, send_sem=send_sems.at[3 * j + i], recv_sem=recv_sems.at[3 * j + i],
                device_id=(cx, cy, c), device_id_type=MESH)
            mine.append(mk(src, dst))
            theirs.append(mk(got, got))
    return mine, theirs


def _ag_ici_start(wc, sp, dep):
    def body(w_ref, s_ref, lw_ref, ls_ref, dep_ref, send_sems, recv_sems, w_thru, s_thru, lw_thru, ls_thru, token):
        mine, _ = _ag_ici_copies(w_ref, s_ref, lw_ref, ls_ref, send_sems, recv_sems)
        for cp in mine:
            cp.start()
        token[...] = jnp.zeros_like(token)

    return pl.pallas_call(
        body, name="weights_gather_start",
        out_shape=(pltpu.SemaphoreType.DMA((9,)), pltpu.SemaphoreType.DMA((9,)), pltpu.HBM(wc.shape, wc.dtype),
                   pltpu.HBM(sp.shape, sp.dtype), pltpu.HBM((NINP, CWD), F32), pltpu.HBM((4, PACK_ROWS, PACK_W), BF16),
                   jax.ShapeDtypeStruct((8, LANE), F32)),
        in_specs=(HBM, HBM, HBM, HBM, ANY),
        out_specs=(SEM, SEM, HBM, HBM, HBM, HBM, pl.BlockSpec(memory_space=pltpu.VMEM)),
        input_output_aliases={0: 2, 1: 3, 2: 4, 3: 5},
        compiler_params=pltpu.CompilerParams(has_side_effects=EFFECT),
    )(_in_hbm(wc), _in_hbm(sp), _in_hbm(lax.empty((NINP, CWD), F32)), _in_hbm(lax.empty((4, PACK_ROWS, PACK_W), BF16)), dep)


def _ag_ici_wait(send_sems, recv_sems, wc, sp, lw, ls, after):
    def body(w_ref, s_ref, lw_ref, ls_ref, send_sems, recv_sems, after_ref, w_dead, s_dead, lw_out, ls_out):
        mine, theirs = _ag_ici_copies(w_ref, s_ref, lw_ref, ls_ref, send_sems, recv_sems)
        for cp in mine:
            cp.wait_send()
        for cp in theirs:
            cp.wait_recv()

    out = pl.pallas_call(
        body, name="weights_gather_wait",
        out_shape=(pltpu.HBM(wc.shape, wc.dtype), pltpu.HBM(sp.shape, sp.dtype), pltpu.HBM(lw.shape, lw.dtype),
                   pltpu.HBM(ls.shape, ls.dtype)),
        in_specs=(HBM, HBM, HBM, HBM, SEM, SEM, ANY), out_specs=(HBM, HBM, HBM, HBM),
        input_output_aliases={0: 0, 1: 1, 2: 2, 3: 3},
        compiler_params=pltpu.CompilerParams(has_side_effects=EFFECT),
    )(wc, sp, lw, ls, send_sems, recv_sems, after)
    return out[2], out[3]


def _ag_finish(wc, sp, lw, ls):
    def body(w_ref, s_ref, lw_ref, ls_ref, ow_ref, os_ref, send_sems, recv_sems):
        x, y, c = _me()
        k_me = 2 * x + y
        sib = (x, y, 1 - c)
        chips = [(1 - x, y), (x, 1 - y), (1 - x, 1 - y)]

        def copy(i, src, dst):
            return pltpu.make_async_remote_copy(src_ref=src, dst_ref=dst, send_sem=send_sems.at[i],
                                                recv_sem=recv_sems.at[i], device_id=sib, device_id_type=MESH)

        def own_windows():
            return ([(w_ref.at[pl.ds(l0, n)], ow_ref.at[pl.ds(p0, n)]) for l0, p0, n in _piece_rows(k_me)]
                    + [(s_ref, os_ref.at[k_me])])

        out = [copy(9 + i, src, dst) for i, (src, dst) in enumerate(own_windows())]
        for j, (cx, cy) in enumerate(chips):
            landed = _ag_windows(w_ref, s_ref, lw_ref, ls_ref, 2 * cx + cy, c)
            for i, (_, dst) in enumerate(_ag_windows(w_ref, s_ref, ow_ref, os_ref, 2 * cx + cy, c)):
                out.append(copy(3 * j + i, landed[i][1], dst))
        for cp in out:
            cp.start()
        for j, (cx, cy) in enumerate(chips):
            for i, (_, dst) in enumerate(_ag_windows(w_ref, s_ref, ow_ref, os_ref, 2 * cx + cy, 1 - c)):
                copy(3 * j + i, dst, dst).wait_recv()
        for i, (_, dst) in enumerate(own_windows()):
            copy(9 + i, dst, dst).wait_recv()
        for cp in out:
            cp.wait_send()

    return pl.pallas_call(
        body,
        out_shape=[jax.ShapeDtypeStruct(lw.shape, lw.dtype), jax.ShapeDtypeStruct(ls.shape, ls.dtype)],
        in_specs=[ANY] * 4, out_specs=[ANY, ANY],
        input_output_aliases={2: 0, 3: 1},
        scratch_shapes=[pltpu.SemaphoreType.DMA((12,)), pltpu.SemaphoreType.DMA((12,))],
        name="weights_gather_finish",
    )(wc, sp, lw, ls)


UNPACK_BR = 512


def _unpack_w_in(cont):
    def body(c_ref, o_ref):
        lo, hi = _unpack_words(c_ref[...])
        r = pl.program_id(0) * UNPACK_BR + lax.broadcasted_iota(jnp.int32, (UNPACK_BR, CWD), 0)
        pad = jnp.logical_and(r >= KPE_END, r < KPE_END + NINP - NIN)
        o_ref[:, 0:CWD] = jnp.where(pad, 0.0, lo).astype(BF16)
        o_ref[:, CWD:2 * CWD] = jnp.where(pad, 0.0, hi).astype(BF16)

    return pl.pallas_call(
        body, grid=(NINP // UNPACK_BR,),
        in_specs=[pl.BlockSpec((UNPACK_BR, CWD), lambda i: (i, 0))],
        out_specs=pl.BlockSpec((UNPACK_BR, D), lambda i: (i, 0)),
        out_shape=jax.ShapeDtypeStruct((NINP, D), BF16),
        name="w_in_unpack",
        compiler_params=pltpu.CompilerParams(dimension_semantics=("arbitrary",), vmem_limit_bytes=VMEM_LIMIT),
    )(cont)


def _rs_swap(gw, gs):
    def body(w_ref, s_ref, rw_ref, rs_ref, send_sems, recv_sems):
        x, y, c = _me()
        oc = 1 - c
        cps = [pltpu.make_async_remote_copy(src_ref=w_ref.at[:, pl.ds(pl.multiple_of(oc * (D // 2), LANE), D // 2)],
                                            dst_ref=rw_ref, send_sem=send_sems.at[0], recv_sem=recv_sems.at[0],
                                            device_id=(x, y, oc), device_id_type=MESH),
               pltpu.make_async_remote_copy(src_ref=s_ref.at[:, :, pl.ds(pl.multiple_of(oc * HW, LANE), HW)],
                                            dst_ref=rs_ref, send_sem=send_sems.at[1], recv_sem=recv_sems.at[1],
                                            device_id=(x, y, oc), device_id_type=MESH)]
        for cp in cps:
            cp.start()
        for cp in cps:
            cp.wait()

    return pl.pallas_call(
        body,
        out_shape=[jax.ShapeDtypeStruct((NINP, D // 2), F32), jax.ShapeDtypeStruct((4, PACK_ROWS, HW), F32)],
        in_specs=[ANY, ANY], out_specs=[ANY, ANY],
        scratch_shapes=[pltpu.SemaphoreType.DMA((2,)), pltpu.SemaphoreType.DMA((2,))],
        name="grads_sibling_swap",
    )(gw, gs)


SUM_BR = 512


def _rs_chip_sum_w(gw, rw, cidx):
    def body(c_ref, g_ref, r_ref, o_ref):
        s = g_ref[...] + r_ref[...]
        q = D // 8
        o_ref[...] = jnp.concatenate([_pack_words(s[:, 0:q], s[:, q:2 * q]),
                                      _pack_words(s[:, 2 * q:3 * q], s[:, 3 * q:4 * q])], axis=1)

    return pl.pallas_call(
        body,
        grid_spec=pltpu.PrefetchScalarGridSpec(
            num_scalar_prefetch=1, grid=(NINP // SUM_BR,),
            in_specs=[pl.BlockSpec((SUM_BR, D // 2), lambda i, cr: (i, cr[0])),
                      pl.BlockSpec((SUM_BR, D // 2), lambda i, cr: (i, 0))],
            out_specs=pl.BlockSpec((SUM_BR, D // 4), lambda i, cr: (i, 0))),
        out_shape=jax.ShapeDtypeStruct((NINP, D // 4), F32),
        name="grads_chip_sum_w",
        compiler_params=pltpu.CompilerParams(dimension_semantics=("arbitrary",), vmem_limit_bytes=VMEM_LIMIT),
    )(cidx, gw, rw)


def _rs_chip_sum_s(gs, rs, cidx):
    def body(c_ref, g_ref, r_ref, o_ref):
        o_ref[...] = (g_ref[...] + r_ref[...]).astype(BF16)

    return pl.pallas_call(
        body,
        grid_spec=pltpu.PrefetchScalarGridSpec(
            num_scalar_prefetch=1, grid=(4,),
            in_specs=[pl.BlockSpec((None, PACK_ROWS, HW), lambda j, cr: (j, 0, cr[0])),
                      pl.BlockSpec((None, PACK_ROWS, HW), lambda j, cr: (j, 0, 0))],
            out_specs=pl.BlockSpec((None, PACK_ROWS, HW), lambda j, cr: (j, 0, 0))),
        out_shape=jax.ShapeDtypeStruct((4, PACK_ROWS, HW), BF16),
        name="grads_chip_sum_s",
        compiler_params=pltpu.CompilerParams(dimension_semantics=("arbitrary",), vmem_limit_bytes=VMEM_LIMIT),
    )(cidx, gs, rs)


def _rs_exchange_copies(sw_ref, ss_ref, r2w_ref, r2s_ref, send_sems, recv_sems):
    x, y, c = _me()
    mine, theirs = [], []
    for j, (cx, cy) in enumerate([(1 - x, y), (x, 1 - y), (1 - x, 1 - y)]):
        def mk(i, src, dst):
            return pltpu.make_async_remote_copy(src_ref=src, dst_ref=dst, send_sem=send_sems.at[3 * j + i],
                                                recv_sem=recv_sems.at[3 * j + i], device_id=(cx, cy, c), device_id_type=MESH)
        for i, (l0, p0, n) in enumerate(_piece_rows(2 * cx + cy)):
            mine.append(mk(i, sw_ref.at[pl.ds(p0, n)], r2w_ref.at[j, pl.ds(l0, n)]))
            theirs.append(mk(i, r2w_ref.at[j, pl.ds(l0, n)], r2w_ref.at[j, pl.ds(l0, n)]))
        mine.append(mk(2, ss_ref.at[2 * cx + cy], r2s_ref.at[j]))
        theirs.append(mk(2, r2s_ref.at[j], r2s_ref.at[j]))
    return mine, theirs


def _rs_exchange_start(sw, ss, tag):
    def body(sw_ref, ss_ref, r2w_ref, r2s_ref, send_sems, recv_sems, sw_thru, ss_thru, r2w_thru, r2s_thru, token):
        mine, _ = _rs_exchange_copies(sw_ref, ss_ref, r2w_ref, r2s_ref, send_sems, recv_sems)
        for cp in mine:
            cp.start()
        token[...] = jnp.zeros_like(token)

    return pl.pallas_call(
        body, name="grads_exchange_start_" + tag,
        out_shape=(pltpu.SemaphoreType.DMA((9,)), pltpu.SemaphoreType.DMA((9,)), pltpu.HBM(sw.shape, sw.dtype),
                   pltpu.HBM(ss.shape, ss.dtype), pltpu.HBM((3, WSH, D // 4), F32), pltpu.HBM((3, PACK_ROWS, HW), BF16),
                   jax.ShapeDtypeStruct((8, LANE), F32)),
        in_specs=(HBM, HBM, HBM, HBM),
        out_specs=(SEM, SEM, HBM, HBM, HBM, HBM, pl.BlockSpec(memory_space=pltpu.VMEM)),
        input_output_aliases={0: 2, 1: 3, 2: 4, 3: 5},
        compiler_params=pltpu.CompilerParams(has_side_effects=EFFECT),
    )(_in_hbm(sw), _in_hbm(ss), _in_hbm(lax.empty((3, WSH, D // 4), F32)), _in_hbm(lax.empty((3, PACK_ROWS, HW), BF16)))


def _rs_exchange_wait(send_sems, recv_sems, sw, ss, r2w, r2s, after, tag):
    def body(sw_ref, ss_ref, r2w_ref, r2s_ref, send_sems, recv_sems, after_ref, sw_dead, ss_dead, r2w_out, r2s_out):
        mine, theirs = _rs_exchange_copies(sw_ref, ss_ref, r2w_ref, r2s_ref, send_sems, recv_sems)
        for cp in mine:
            cp.wait_send()
        for cp in theirs:
            cp.wait_recv()

    out = pl.pallas_call(
        body, name="grads_exchange_wait_" + tag,
        out_shape=(pltpu.HBM(sw.shape, sw.dtype), pltpu.HBM(ss.shape, ss.dtype), pltpu.HBM(r2w.shape, r2w.dtype),
                   pltpu.HBM(r2s.shape, r2s.dtype)),
        in_specs=(HBM, HBM, HBM, HBM, SEM, SEM, ANY), out_specs=(HBM, HBM, HBM, HBM),
        input_output_aliases={0: 0, 1: 1, 2: 2, 3: 3},
        compiler_params=pltpu.CompilerParams(has_side_effects=EFFECT),
    )(sw, ss, r2w, r2s, send_sems, recv_sems, after)
    return out[2], out[3]


def _rs_final_w(gw, rw, r2w, idx):
    q = D // 8

    def body(i_ref, g_ref, r_ref, p_ref, o_ref, gbuf, rbuf, sems):
        i = pl.program_id(0)
        k, c = i_ref[0], i_ref[1]
        cps = []
        for n_, (l0, p0, n) in enumerate(_piece_rows(k)):
            gcol = pl.ds(pl.multiple_of(c * (D // 2) + i * 2 * q, LANE), 2 * q)
            rcol = pl.ds(pl.multiple_of(i * 2 * q, LANE), 2 * q)
            cps.append(pltpu.make_async_copy(g_ref.at[pl.ds(p0, n), gcol], gbuf.at[pl.ds(l0, n)], sems.at[2 * n_]))
            cps.append(pltpu.make_async_copy(r_ref.at[pl.ds(p0, n), rcol], rbuf.at[pl.ds(l0, n)], sems.at[2 * n_ + 1]))
        for cp in cps:
            cp.start()
        for cp in cps:
            cp.wait()
        acc = gbuf[...] + rbuf[...]
        for j in range(3):
            lo, hi = _unpack_words(p_ref[j])
            acc = acc + jnp.concatenate([lo, hi], axis=1)
        o_ref[...] = acc

    return pl.pallas_call(
        body,
        grid_spec=pltpu.PrefetchScalarGridSpec(
            num_scalar_prefetch=1, grid=(2,),
            in_specs=[ANY, ANY, pl.BlockSpec((3, WSH, q), lambda i, ir: (0, 0, i))],
            out_specs=pl.BlockSpec((WSH, 2 * q), lambda i, ir: (0, i)),
            scratch_shapes=[pltpu.VMEM((WSH, 2 * q), F32), pltpu.VMEM((WSH, 2 * q), F32), pltpu.SemaphoreType.DMA((4,))]),
        out_shape=jax.ShapeDtypeStruct((WSH, D // 2), F32),
        name="grads_final_sum_w",
        compiler_params=pltpu.CompilerParams(dimension_semantics=("arbitrary",), vmem_limit_bytes=VMEM_LIMIT),
    )(idx, gw, rw, r2w)


def _rs_final_s(gs, rs, r2s, idx):
    def body(i_ref, g_ref, r_ref, p_ref, o_ref):
        acc = g_ref[...] + r_ref[...]
        for j in range(3):
            acc = acc + p_ref[j].astype(F32)
        o_ref[...] = acc

    return pl.pallas_call(
        body,
        grid_spec=pltpu.PrefetchScalarGridSpec(
            num_scalar_prefetch=1, grid=(1,),
            in_specs=[pl.BlockSpec((None, PACK_ROWS, HW), lambda i, ir: (ir[0], 0, ir[1])),
                      pl.BlockSpec((None, PACK_ROWS, HW), lambda i, ir: (ir[0], 0, 0)),
                      pl.BlockSpec((3, PACK_ROWS, HW), lambda i, ir: (0, 0, 0))],
            out_specs=pl.BlockSpec((PACK_ROWS, HW), lambda i, ir: (0, 0))),
        out_shape=jax.ShapeDtypeStruct((PACK_ROWS, HW), F32),
        name="grads_final_sum_s",
        compiler_params=pltpu.CompilerParams(dimension_semantics=("arbitrary",), vmem_limit_bytes=VMEM_LIMIT),
    )(idx, gs, rs, r2s)


def _rs_share(fw, fs):
    def body(w_ref, s_ref, ow_ref, os_ref, send_sems, recv_sems):
        x, y, c = _me()
        cps = [pltpu.make_async_remote_copy(src_ref=w_ref, dst_ref=ow_ref, send_sem=send_sems.at[0],
                                            recv_sem=recv_sems.at[0], device_id=(x, y, 1 - c), device_id_type=MESH),
               pltpu.make_async_remote_copy(src_ref=s_ref, dst_ref=os_ref, send_sem=send_sems.at[1],
                                            recv_sem=recv_sems.at[1], device_id=(x, y, 1 - c), device_id_type=MESH)]
        for cp in cps:
            cp.start()
        for cp in cps:
            cp.wait()

    return pl.pallas_call(
        body,
        out_shape=[jax.ShapeDtypeStruct((WSH, D // 2), F32), jax.ShapeDtypeStruct((PACK_ROWS, HW), F32)],
        in_specs=[ANY, ANY], out_specs=[ANY, ANY],
        scratch_shapes=[pltpu.SemaphoreType.DMA((2,)), pltpu.SemaphoreType.DMA((2,))],
        name="grads_share",
    )(fw, fs)


def _both_halves(mine, other, c):
    return jnp.where(c == 0, jnp.concatenate([mine, other], axis=1), jnp.concatenate([other, mine], axis=1))


def _rs_begin(gw, gs):
    x, y, c = _me()
    cidx = jnp.reshape(c, (1,)).astype(jnp.int32)
    rw, rs = _rs_swap(gw, gs)
    return dict(gw=gw, gs=gs, rw=rw, rs=rs, sw=_rs_chip_sum_w(gw, rw, cidx), ss=_rs_chip_sum_s(gs, rs, cidx))


def _rs_end(st, r2w, r2s):
    x, y, c = _me()
    idx = jnp.stack([2 * x + y, c]).astype(jnp.int32)
    fw = _rs_final_w(st["gw"], st["rw"], r2w, idx)
    fs = _rs_final_s(st["gs"], st["rs"], r2s, idx)
    ow, os_ = _rs_share(fw, fs)
    return _both_halves(fw, ow, c), _both_halves(fs, os_, c)


def _all_reduce_small(gs):
    rows = gs.shape[0]

    def body(g_ref, o_ref, buf, send_sems, recv_sems):
        x, y, c = _me()
        me = 4 * x + 2 * y + c
        buf[me] = g_ref[...]
        cps = []
        for r in range(1, 8):
            fx, fy, fc = (r >> 2) & 1, (r >> 1) & 1, r & 1
            px, py, pc = jnp.bitwise_xor(x, fx), jnp.bitwise_xor(y, fy), jnp.bitwise_xor(c, fc)
            cps.append((pltpu.make_async_remote_copy(
                src_ref=g_ref, dst_ref=buf.at[me], send_sem=send_sems.at[r - 1], recv_sem=recv_sems.at[r - 1],
                device_id=(px, py, pc), device_id_type=MESH), 4 * px + 2 * py + pc))
        for cp, _ in cps:
            cp.start()
        for r, (cp, peer) in enumerate(cps):
            pltpu.make_async_remote_copy(
                src_ref=g_ref, dst_ref=buf.at[peer], send_sem=send_sems.at[r], recv_sem=recv_sems.at[r],
                device_id=(x, y, c), device_id_type=MESH).wait_recv()
        for cp, _ in cps:
            cp.wait_send()
        acc = buf[0]
        for k in range(1, 8):
            acc = acc + buf[k]
        o_ref[...] = acc

    return pl.pallas_call(
        body,
        out_shape=jax.ShapeDtypeStruct((rows, LANE), F32),
        in_specs=[pl.BlockSpec(memory_space=pltpu.VMEM)],
        out_specs=pl.BlockSpec(memory_space=pltpu.VMEM),
        scratch_shapes=[pltpu.VMEM((8, rows, LANE), F32), pltpu.SemaphoreType.DMA((7,)), pltpu.SemaphoreType.DMA((7,))],
        name="small_grads_all_reduce",
    )(gs)


PACK_SPLIT = (("w_uq", 96, (QL, 192)), ("w_ukv", 64, (KVL, 256)),
              ("w_out_a", 256, (CW, 256)), ("w_out_b", 256, (CW, 256)), ("w_out_c", 256, (CW, 256)),
              ("w_o", 512, (256, D)))
MAT_ROWS = 1440
CONV_SHARD = 3 * 128


def _w_in_words(w_in_shard):
    t = w_in_shard.T
    return _pack_words(t[:, :CWD], t[:, CWD:])


def _pack_weights(wl):
    parts = [wl[n].astype(BF16).reshape(-1, PACK_W) for n, _, _ in PACK_SPLIT]
    cw = wl["conv_w"].reshape(-1)
    hi = cw.astype(BF16)
    r1 = cw - hi.astype(F32)
    mid = r1.astype(BF16)
    lo = (r1 - mid.astype(F32)).astype(BF16)
    cterms = jnp.pad(jnp.concatenate([hi, mid, lo]), (0, 3 * PACK_W - 3 * CONV_SHARD)).reshape(3, PACK_W)
    tail = jnp.pad(cterms, ((0, PACK_ROWS - MAT_ROWS - 3), (0, 0)))
    return jnp.concatenate(parts + [tail], axis=0)


def _unpack_weights(gath):
    out = {}
    r = 0
    for n, nrows, shp in PACK_SPLIT:
        t = gath[:, r:r + nrows].reshape((4,) + shp)
        r += nrows
        if n == "w_o":
            out[n] = t.reshape(4 * shp[0], shp[1])
        else:
            out[n] = t.transpose(1, 0, 2).reshape(shp[0], 4 * shp[1])
    ct = gath[:, r:r + 3].reshape(4, 3 * PACK_W)[:, :3 * CONV_SHARD].astype(F32).reshape(4, 3, CONV_SHARD)
    cw = (ct[:, 0] + ct[:, 1]) + ct[:, 2]
    out["conv_w"] = cw.reshape(4, 3, 128).transpose(1, 0, 2).reshape(3, CW)
    return out


def _pack_grads(g):
    parts = []
    for n, nrows, shp in PACK_SPLIT:
        t = g[n]
        if n == "w_o":
            t = t.reshape((4,) + shp)
        else:
            t = t.reshape(shp[0], 4, shp[1]).transpose(1, 0, 2)
        parts.append(t.reshape(4, nrows, PACK_W))
    cw = g["conv_w"].reshape(3, 4, 128).transpose(1, 0, 2).reshape(4, 1, CONV_SHARD)
    parts.append(jnp.pad(cw, ((0, 0), (0, PACK_ROWS - MAT_ROWS - 1), (0, PACK_W - CONV_SHARD))))
    return jnp.concatenate(parts, axis=1)


def _unpack_grads(red):
    out = {}
    r = 0
    for n, nrows, shp in PACK_SPLIT:
        out[n] = red[r:r + nrows].reshape(shp)
        r += nrows
    out["conv_w"] = red[r, :CONV_SHARD].reshape(3, 128)
    return out


SMALL_SIZES = (("norm_g", D), ("b_gate", 3 * D), ("conv_b", CW), ("q_a_norm_g", QL), ("kv_a_norm_g", KVL),
               ("mla_q_norm_g", QK), ("mla_k_norm_g", QK), ("dil_q_norm_g", NG * HD), ("dil_k_norm_g", NG * HD))
SMALL_ROWS = 88


def _pack_small(per_name):
    flat = jnp.concatenate([per_name[n].reshape(-1).astype(F32) for n, _ in SMALL_SIZES])
    return jnp.pad(flat, (0, SMALL_ROWS * LANE - flat.shape[0])).reshape(SMALL_ROWS, LANE)


def _unpack_small(packed, like):
    out = {}
    flat = packed.reshape(-1)
    r = 0
    for n, sz in SMALL_SIZES:
        out[n] = flat[r:r + NL * sz].reshape(like[n].shape)
        r += NL * sz
    return out


def _adamw_math(w, g, m, v):
    m = ADAM_B1 * m + (1.0 - ADAM_B1) * g
    v = ADAM_B2 * v + (1.0 - ADAM_B2) * jnp.square(g)
    m_hat = m / (1.0 - ADAM_B1 ** ADAM_STEP)
    v_hat = v / (1.0 - ADAM_B2 ** ADAM_STEP)
    delta = -ADAM_LR * (m_hat / (jnp.sqrt(v_hat) + ADAM_EPS) + ADAM_WD * w)
    return delta, m, v


def _adamw(name, w, g, m, v, br, bc=None):
    L, R, C = w.shape
    bc = C if bc is None else bc
    blk = lambda l, i, j: (l, i, j)
    return _pcall(name, _adamw_math, (L, R // br, C // bc), [(t, (None, br, bc), blk) for t in (w, g, m, v)],
                  [((L, R, C), F32, (None, br, bc), blk)] * 3)


ADAM_ROWS = {"w_uq": 256, "w_ukv": 128, "w_out_a": 512, "w_out_b": 512, "w_out_c": 512, "w_o": 256,
             "conv_w": 3}


def kernel(x, norm_g, w_in, b_gate, conv_w, conv_b, q_a_norm_g, w_uq, kv_a_norm_g, w_ukv, mla_q_norm_g, mla_k_norm_g, dil_q_norm_g, dil_k_norm_g, w_out_a, w_out_b, w_out_c, w_o, loss_target, m_norm_g, m_w_in, m_b_gate, m_conv_w, m_conv_b, m_q_a_norm_g, m_w_uq, m_kv_a_norm_g, m_w_ukv, m_mla_q_norm_g, m_mla_k_norm_g, m_dil_q_norm_g, m_dil_k_norm_g, m_w_out_a, m_w_out_b, m_w_out_c, m_w_o, v_norm_g, v_w_in, v_b_gate, v_conv_w, v_conv_b, v_q_a_norm_g, v_w_uq, v_kv_a_norm_g, v_w_ukv, v_mla_q_norm_g, v_mla_k_norm_g, v_dil_q_norm_g, v_dil_k_norm_g, v_w_out_a, v_w_out_b, v_w_out_c, v_w_o):
    W = dict(norm_g=norm_g, w_in=w_in, b_gate=b_gate, conv_w=conv_w, conv_b=conv_b, q_a_norm_g=q_a_norm_g, w_uq=w_uq,
             kv_a_norm_g=kv_a_norm_g, w_ukv=w_ukv, mla_q_norm_g=mla_q_norm_g, mla_k_norm_g=mla_k_norm_g,
             dil_q_norm_g=dil_q_norm_g, dil_k_norm_g=dil_k_norm_g, w_out_a=w_out_a, w_out_b=w_out_b, w_out_c=w_out_c,
             w_o=w_o)
    M = dict(norm_g=m_norm_g, w_in=m_w_in, b_gate=m_b_gate, conv_w=m_conv_w, conv_b=m_conv_b, q_a_norm_g=m_q_a_norm_g,
             w_uq=m_w_uq, kv_a_norm_g=m_kv_a_norm_g, w_ukv=m_w_ukv, mla_q_norm_g=m_mla_q_norm_g,
             mla_k_norm_g=m_mla_k_norm_g, dil_q_norm_g=m_dil_q_norm_g, dil_k_norm_g=m_dil_k_norm_g, w_out_a=m_w_out_a,
             w_out_b=m_w_out_b, w_out_c=m_w_out_c, w_o=m_w_o)
    V = dict(norm_g=v_norm_g, w_in=v_w_in, b_gate=v_b_gate, conv_w=v_conv_w, conv_b=v_conv_b, q_a_norm_g=v_q_a_norm_g,
             w_uq=v_w_uq, kv_a_norm_g=v_kv_a_norm_g, w_ukv=v_w_ukv, mla_q_norm_g=v_mla_q_norm_g,
             mla_k_norm_g=v_mla_k_norm_g, dil_q_norm_g=v_dil_q_norm_g, dil_k_norm_g=v_dil_k_norm_g, w_out_a=v_w_out_a,
             w_out_b=v_w_out_b, w_out_c=v_w_out_c, w_o=v_w_o)
    batch = x.shape[0]
    T = batch * S

    def layer_weights(l, cont, gath):
        full = _unpack_weights(gath)
        pad_qk = lambda t: jnp.pad(t, (0, QKP - QK)).reshape(1, QKP)
        full.update(
            w_in_t=_unpack_w_in(cont),
            norm_g=norm_g[l].reshape(1, D), b_gate=b_gate[l].reshape(1, 3 * D), conv_b=conv_b[l].reshape(1, CW),
            q_a_norm_g=q_a_norm_g[l].reshape(1, QL), kv_a_norm_g=kv_a_norm_g[l].reshape(1, KVL),
            mla_q_norm_g=pad_qk(mla_q_norm_g[l]), mla_k_norm_g=pad_qk(mla_k_norm_g[l]),
            dil_q_norm_g=dil_q_norm_g[l].reshape(NG, 1, HD), dil_k_norm_g=dil_k_norm_g[l].reshape(NG, 1, HD))
        return full

    words = [_w_in_words(w_in[l]) for l in range(NL)]
    packs = [_pack_weights({n: W[n][l] for n in BIG[1:] + ("conv_w",)}) for l in range(NL)]
    tabs = _rope_tables() + (_dil_slopes(),)
    x2 = x.reshape(T, D)

    cont0, gath0 = _all_gather(words[0], packs[0])
    w0 = layer_weights(0, cont0, gath0)
    ag = _ag_ici_start(words[1], packs[1], gath0)
    w0["norm_g"] = w0["norm_g"] + ag[6][0:1, 0:1]
    y0, res0 = _layer_fwd(x2, w0, tabs, batch)
    lw, ls = _ag_ici_wait(ag[0], ag[1], ag[2], ag[3], ag[4], ag[5], y0)
    w1 = layer_weights(1, *_ag_finish(words[1], packs[1], lw, ls))
    y1, res1 = _layer_fwd(y0, w1, tabs, batch)

    row = lambda i: (i, 0)
    dy, loss = _pcall("loss", _loss_math, (T // BR,),
                      [(y1, (BR, D), row), (loss_target.reshape(T, D), (BR, D), row)],
                      [((T, D), F32, (BR, D), row), ((1, 1), F32, (1, 1), lambda i: (0, 0), True)])
    loss = lax.psum(loss[0, 0], ("x", "y", "c"))

    grads = [None] * NL
    dy, grads[1] = _layer_bwd(dy, w1, res1, tabs, batch)
    st = [None] * NL
    ex = [None] * NL
    st[1] = _rs_begin(grads[1]["w_in_t"], _pack_grads(grads[1]))
    ex[1] = _rs_exchange_start(st[1]["sw"], st[1]["ss"], "1")
    w0["w_o"] = w0["w_o"] + ex[1][6][0:1, 0:1].astype(BF16)

    def start_layer0(g):
        st[0] = _rs_begin(g["w_in_t"], _pack_grads(g))
        ex[0] = _rs_exchange_start(st[0]["sw"], st[0]["ss"], "0")
        return ex[0][6]

    dx, grads[0] = _layer_bwd(dy, w0, res0, tabs, batch, after_dw=start_layer0)
    grad_x = dx.reshape(batch, S, D)

    red = [None] * NL
    for l in (1, 0):
        r2w, r2s = _rs_exchange_wait(*ex[l][:6], dx, str(l))
        rw, rs = _rs_end(st[l], r2w, r2s)
        r = _unpack_grads(rs)
        r["w_in_t"] = rw
        red[l] = r
    G = {n: jnp.stack([red[l][n] for l in range(NL)]) for n in BIG[1:] + ("conv_w",)}
    g_in_t = jnp.stack([red[l]["w_in_t"] for l in range(NL)])
    G["w_in"] = jnp.swapaxes(g_in_t, 1, 2)
    small_g = {n: jnp.stack([grads[l][n].reshape(-1)[:sz] for l in range(NL)]) for n, sz in SMALL_SIZES}
    small_red = _all_reduce_small(_pack_small(small_g))
    G.update(_unpack_small(small_red, {n: W[n] for n in SMALL}))

    delta, new_m, new_v = {}, {}, {}
    for n in BIG[1:] + ("conv_w",):
        delta[n], new_m[n], new_v[n] = _adamw("adamw_" + n, W[n], G[n], M[n], V[n], ADAM_ROWS[n])
    tr = lambda t: jnp.swapaxes(t, 1, 2)
    delta["w_in"], new_m["w_in"], new_v["w_in"] = (
        tr(t) for t in _adamw("adamw_w_in", tr(w_in), g_in_t, tr(m_w_in), tr(v_w_in), WSH, LANE))
    sw, sm, sv = (_pack_small({n: t[n] for n in SMALL})[None] for t in (W, M, V))
    sd, snm, snv = _adamw("adamw_small", sw, small_red[None], sm, sv, SMALL_ROWS)
    like = {n: W[n] for n in SMALL}
    delta.update(_unpack_small(sd[0], like))
    new_m.update(_unpack_small(snm[0], like))
    new_v.update(_unpack_small(snv[0], like))

    return (loss, grad_x, *[G[n] for n in WEIGHTS], *[delta[n] for n in WEIGHTS],
            *[new_m[n] for n in WEIGHTS], *[new_v[n] for n in WEIGHTS])
```

```python
import functools

import numpy as np
import jax
import jax.numpy as jnp
from jax import lax
from jax.experimental import pallas as pl
from jax.experimental.pallas import tpu as pltpu

F32 = jnp.float32
BF16 = jnp.bfloat16

D = 1024
S = 2048
NL = 2
CW = 512
NH = 8
QL = 256
KVL = 128
NOPE = 64
ROPE = 32
VD = 64
QK = NOPE + ROPE
QKP = 128
ROPE_THETA = 10000.0
DIL = ((128, 1), (512, 4), (2048, 16))
NG = 3
DH = 8
HD = 64
DWID = DH * HD
QB = 128
EPS = 1e-6
NIN = 11168
NINP = 11264
O_A, O_CQ, O_CKV, O_KPE, O_BZ, O_DQ, O_DK, O_DV, O_CZ, O_G = 0, 2048, 2304, 2432, 2560, 3072, 4608, 6144, 7680, 8192
KPE_END = 2464
NEG = -1e30
MLA_SCALE = QK ** -0.5
DIL_SCALE = HD ** -0.5
LANE = 128
PACK_W = 512
VMEM_LIMIT = 48 * 1024 * 1024

ADAM_LR = 0.001
ADAM_B1 = 0.9
ADAM_B2 = 0.999
ADAM_EPS = 1e-08
ADAM_WD = 0.01
ADAM_STEP = 10

MESH = pl.DeviceIdType.MESH
BIG = ("w_in", "w_uq", "w_ukv", "w_out_a", "w_out_b", "w_out_c", "w_o")
SMALL = ("norm_g", "b_gate", "conv_b", "q_a_norm_g", "kv_a_norm_g", "mla_q_norm_g", "mla_k_norm_g",
         "dil_q_norm_g", "dil_k_norm_g")
WEIGHTS = ("norm_g", "w_in", "b_gate", "conv_w", "conv_b", "q_a_norm_g", "w_uq", "kv_a_norm_g", "w_ukv",
           "mla_q_norm_g", "mla_k_norm_g", "dil_q_norm_g", "dil_k_norm_g", "w_out_a", "w_out_b", "w_out_c", "w_o")


def _dot(a, b):
    return jnp.dot(a, b, preferred_element_type=F32)


def _dot_nt(a, b):
    return lax.dot_general(a, b, (((1,), (1,)), ((), ())), preferred_element_type=F32)


def _dot_tn(a, b):
    return lax.dot_general(a, b, (((0,), (0,)), ((), ())), preferred_element_type=F32)


def _pcall(name, fn, grid, ins, outs):
    n_in = len(ins)
    n_out = len(outs)
    acc_axis = len(grid) - 1
    is_acc = [len(o) > 4 and o[4] for o in outs]
    outs = [o[:4] for o in outs]

    def body(*refs):
        vals = fn(*[r[...].astype(F32) for r in refs[:n_in]])
        if not isinstance(vals, (tuple, list)):
            vals = (vals,)
        for k in range(n_out):
            r = refs[n_in + k]
            v = vals[k].astype(r.dtype).reshape(r.shape)
            if is_acc[k]:
                first = pl.program_id(acc_axis) == 0

                @pl.when(first)
                def _():
                    r[...] = v

                @pl.when(jnp.logical_not(first))
                def _():
                    r[...] += v
            else:
                r[...] = v

    return pl.pallas_call(
        body,
        grid=grid,
        in_specs=[pl.BlockSpec(bs, im) for _, bs, im in ins],
        out_specs=[pl.BlockSpec(bs, im) for _, _, bs, im in outs],
        out_shape=[jax.ShapeDtypeStruct(sh, dt) for sh, dt, _, _ in outs],
        name=name,
        compiler_params=pltpu.CompilerParams(
            dimension_semantics=("arbitrary",) * len(grid), vmem_limit_bytes=VMEM_LIMIT),
    )(*[a for a, _, _ in ins])


def _mm(name, a, b, *, ta=False, tb=False, out_dtype=F32, add=None, tm=512, tn=1024, tk=1024):
    if ta:
        K, M = a.shape
    else:
        M, K = a.shape
    if tb:
        N, K2 = b.shape
    else:
        K2, N = b.shape
    assert K == K2, (name, a.shape, b.shape)
    tm, tn, tk = min(tm, M), min(tn, N), min(tk, K)
    assert M % tm == 0 and N % tn == 0 and K % tk == 0, (name, M, N, K)
    nk = K // tk
    dims = (((0 if ta else 1,), (1 if tb else 0,)), ((), ()))
    a_spec = pl.BlockSpec((tk, tm), lambda j, i, k: (k, i)) if ta else pl.BlockSpec((tm, tk), lambda j, i, k: (i, k))
    b_spec = pl.BlockSpec((tn, tk), lambda j, i, k: (j, k)) if tb else pl.BlockSpec((tk, tn), lambda j, i, k: (k, j))
    o_spec = pl.BlockSpec((tm, tn), lambda j, i, k: (i, j))
    has_add = add is not None

    def body(*refs):
        a_ref, b_ref = refs[0], refs[1]
        add_ref = refs[2] if has_add else None
        o_ref = refs[3] if has_add else refs[2]
        p = lax.dot_general(a_ref[...].astype(BF16), b_ref[...].astype(BF16), dims, preferred_element_type=F32)
        if nk == 1:
            if has_add:
                p = p + add_ref[...]
            o_ref[...] = p.astype(out_dtype)
        else:
            acc = refs[-1]
            k = pl.program_id(2)

            @pl.when(k == 0)
            def _():
                acc[...] = p

            @pl.when(k > 0)
            def _():
                acc[...] += p

            @pl.when(k == nk - 1)
            def _():
                r = acc[...]
                if has_add:
                    r = r + add_ref[...]
                o_ref[...] = r.astype(out_dtype)

    in_specs = [a_spec, b_spec] + ([o_spec] if has_add else [])
    args = [a, b] + ([add] if has_add else [])
    return pl.pallas_call(
        body,
        grid=(N // tn, M // tm, nk),
        in_specs=in_specs,
        out_specs=o_spec,
        out_shape=jax.ShapeDtypeStruct((M, N), out_dtype),
        scratch_shapes=[pltpu.VMEM((tm, tn), F32)] if nk > 1 else [],
        name=name,
        compiler_params=pltpu.CompilerParams(
            dimension_semantics=("arbitrary", "arbitrary", "arbitrary"), vmem_limit_bytes=VMEM_LIMIT),
    )(*args)


SEG_K = 512


def _seg_units(segs):
    flat, units, start = [], [], 0
    for sg in segs:
        group = sg if isinstance(sg, tuple) else (sg,)
        width = sum(t.shape[1] for t in group)
        assert width % SEG_K == 0 and (len(group) == 1 or width == SEG_K)
        units.append((start, width // SEG_K, list(range(len(flat), len(flat) + len(group)))))
        flat.extend(group)
        start += width // SEG_K
    return flat, units, start


def _seg_lhs(refs, idx):
    parts = [refs[i][...] for i in idx]
    return parts[0] if len(parts) == 1 else jnp.concatenate(parts, axis=1)


def _seg_matmul_dx(name, segs, w_t, dep=None, tm=1024):
    flat, units, nk = _seg_units(segs)
    T = flat[0].shape[0]
    N = w_t.shape[1]
    n_seg = len(flat)

    def body(*refs):
        w_ref = refs[n_seg]
        o_ref, acc = refs[-2], refs[-1]
        k = pl.program_id(1)

        @pl.when(k == 0)
        def _():
            acc[...] = jnp.zeros_like(acc)

        for start, nblk, idx in units:
            @pl.when(jnp.logical_and(k >= start, k < start + nblk))
            def _():
                acc[...] += _dot(_seg_lhs(refs, idx), w_ref[...])

        @pl.when(k == nk - 1)
        def _():
            o_ref[...] = acc[...]

    in_specs = []
    for start, nblk, idx in units:
        for i in idx:
            w = flat[i].shape[1]
            if len(idx) > 1:
                in_specs.append(pl.BlockSpec((tm, w), lambda i_, k: (i_, 0)))
            else:
                in_specs.append(pl.BlockSpec(
                    (tm, SEG_K), lambda i_, k, start=start, nblk=nblk: (i_, jnp.clip(k - start, 0, nblk - 1))))
    in_specs.append(pl.BlockSpec((SEG_K, N), lambda i_, k: (k, 0)))
    args = list(flat) + [w_t]
    if dep is not None:
        in_specs.append(pl.BlockSpec(memory_space=pl.ANY))
        args.append(dep)
    return pl.pallas_call(
        body, grid=(T // tm, nk), in_specs=in_specs,
        out_specs=pl.BlockSpec((tm, N), lambda i_, k: (i_, 0)),
        out_shape=jax.ShapeDtypeStruct((T, N), F32),
        scratch_shapes=[pltpu.VMEM((tm, N), F32)],
        name=name,
        compiler_params=pltpu.CompilerParams(dimension_semantics=("arbitrary", "arbitrary"), vmem_limit_bytes=VMEM_LIMIT),
    )(*args)


def _seg_matmul_dw(name, segs, h, tk=1024):
    flat, units, nj = _seg_units(segs)
    T = flat[0].shape[0]
    N = h.shape[1]
    n_seg = len(flat)
    nk = T // tk

    def body(*refs):
        h_ref = refs[n_seg]
        o_ref, acc = refs[-2], refs[-1]
        j = pl.program_id(0)
        k = pl.program_id(1)

        @pl.when(k == 0)
        def _():
            acc[...] = jnp.zeros_like(acc)

        for start, nblk, idx in units:
            @pl.when(jnp.logical_and(j >= start, j < start + nblk))
            def _():
                acc[...] += _dot_tn(_seg_lhs(refs, idx), h_ref[...])

        @pl.when(k == nk - 1)
        def _():
            o_ref[...] = acc[...]

    in_specs = []
    for start, nblk, idx in units:
        for i in idx:
            w = flat[i].shape[1]

            def rows(j, k, start=start, nblk=nblk):
                return jnp.where(jnp.logical_and(j >= start, j < start + nblk), k, 0)
            if len(idx) > 1:
                in_specs.append(pl.BlockSpec((tk, w), lambda j, k, rows=rows: (rows(j, k), 0)))
            else:
                in_specs.append(pl.BlockSpec(
                    (tk, SEG_K), lambda j, k, rows=rows, start=start, nblk=nblk: (rows(j, k), jnp.clip(j - start, 0, nblk - 1))))
    in_specs.append(pl.BlockSpec((tk, N), lambda j, k: (k, 0)))
    return pl.pallas_call(
        body, grid=(nj, nk), in_specs=in_specs,
        out_specs=pl.BlockSpec((SEG_K, N), lambda j, k: (j, 0)),
        out_shape=jax.ShapeDtypeStruct((nj * SEG_K, N), F32),
        scratch_shapes=[pltpu.VMEM((SEG_K, N), F32)],
        name=name,
        compiler_params=pltpu.CompilerParams(dimension_semantics=("arbitrary", "arbitrary"), vmem_limit_bytes=VMEM_LIMIT),
    )(*flat, h)


def _vjp_of(f, n_diff):
    def g(*args, n_prim):
        prim = args[:n_diff]
        consts = args[n_diff:n_prim]
        cts = args[n_prim:]
        _, pull = jax.vjp(lambda *p: f(*p, *consts), *prim)
        out = jax.eval_shape(lambda *p: f(*p, *consts), *prim)
        if isinstance(out, (tuple, list)):
            cts = tuple(c.astype(o.dtype) for c, o in zip(cts, out))
        else:
            cts = cts[0].astype(out.dtype)
        return pull(cts)
    return g


def _rms(x, g, n=None):
    n = x.shape[-1] if n is None else n
    ms = jnp.sum(x * x, axis=-1, keepdims=True) / n
    return x * lax.rsqrt(ms + EPS) * g


def _silu(z):
    return z * jax.nn.sigmoid(z)


def _roll_rows(u, k):
    n = u.shape[0]
    r = pltpu.roll(u, k % n, 0)
    t = lax.broadcasted_iota(jnp.int32, u.shape, 0)
    if k > 0:
        return jnp.where(t >= k, r, 0.0)
    return jnp.where(t < n + k, r, 0.0)


@functools.partial(jax.custom_vjp, nondiff_argnums=(1,))
def _shift(u, k):
    return _roll_rows(u, k)


def _shift_fwd(u, k):
    return _roll_rows(u, k), None


def _shift_bwd(k, _, g):
    return (_roll_rows(g, -k),)


_shift.defvjp(_shift_fwd, _shift_bwd)


@functools.partial(jax.custom_vjp, nondiff_argnums=(1,))
def _lane_roll(u, k):
    return pltpu.roll(u, k % LANE, 1)


def _lane_roll_fwd(u, k):
    return pltpu.roll(u, k % LANE, 1), None


def _lane_roll_bwd(k, _, g):
    return (pltpu.roll(g, (-k) % LANE, 1),)


_lane_roll.defvjp(_lane_roll_fwd, _lane_roll_bwd)


def _conv_math(ab, ac, ax, az, cw, cb):
    u = ac * ax
    conv = cb + _shift(u, 2) * cw[0:1] + _shift(u, 1) * cw[1:2] + u * cw[2:3]
    return ab * conv * _silu(az)


def _mla_pre_math(cq, ckv, gq, gkv):
    return _rms(cq, gq), _rms(ckv, gkv)


def _rope_math(q, kn, kpe, gq, gk, c, s1, s2):
    lane = lax.broadcasted_iota(jnp.int32, kpe.shape, 1)
    pe = _lane_roll(jnp.where(lane < ROPE, kpe, 0.0), NOPE)

    def one(t, g):
        tn = _rms(t, g, QK)
        return tn * c + _lane_roll(tn, -16) * s1 + _lane_roll(tn, 16) * s2

    qs, ks = [], []
    for h in range(NH):
        sl = slice(h * QKP, (h + 1) * QKP)
        qs.append(one(q[:, sl], gq))
        ks.append(one(kn[:, sl] + pe, gk))
    return jnp.concatenate(qs, axis=1), jnp.concatenate(ks, axis=1)


def _gate_math(o, z):
    return o * _silu(z)


def _mergec_math(o0, o1, o2, l0, l1, l2, cz):
    m = lax.stop_gradient(jnp.maximum(jnp.maximum(l0, l1), l2))
    e0, e1, e2 = jnp.exp(l0 - m), jnp.exp(l1 - m), jnp.exp(l2 - m)
    den = e0 + e1 + e2
    oc = (e0 / den) * o0 + (e1 / den) * o1 + (e2 / den) * o2
    return oc * _silu(cz)


def _merge_math(g0, g1, g2, b0, b1, b2, pa, pb, pc):
    return (jax.nn.sigmoid(g0 + b0) * pa + jax.nn.sigmoid(g1 + b1) * pb) + jax.nn.sigmoid(g2 + b2) * pc


MLA_T = 256
MLA_UNROLL = True


def _mla_fwd(q, k, v):
    B = q.shape[0]
    T = MLA_T
    NB = S // T

    def body(q_ref, k_ref, v_ref, o_ref, l_ref):
        row = lax.broadcasted_iota(jnp.int32, (T, T), 0)
        col = lax.broadcasted_iota(jnp.int32, (T, T), 1)
        lo = _lo_mask((T, LANE))

        for qi in range(NB):
            qb = q_ref[qi * T:(qi + 1) * T, :]

            def step(j, carry, diagonal):
                m, l, acc = carry
                off = pl.multiple_of(j * T, T)
                kb = k_ref[pl.ds(off, T), :]
                vb = v_ref[pl.ds(off, T), :]
                ss = []
                for e in (0, 1):
                    se = _dot_nt(qb[:, e * QKP:(e + 1) * QKP], kb[:, e * QKP:(e + 1) * QKP]) * MLA_SCALE
                    ss.append(jnp.where(col <= row, se, NEG) if diagonal else se)
                s = jnp.concatenate(ss, axis=0)
                m_new = jnp.maximum(m, jnp.max(s, axis=-1, keepdims=True))
                a = jnp.exp(m - m_new)
                p = jnp.exp(s - m_new)
                l = a * l + jnp.sum(p, axis=-1, keepdims=True)
                acc = a * acc + _dot(p.astype(BF16), vb)
                return m_new, l, acc

            init = (jnp.full((2 * T, 1), NEG, F32), jnp.zeros((2 * T, 1), F32), jnp.zeros((2 * T, LANE), F32))
            carry = lax.fori_loop(0, qi, functools.partial(step, diagonal=False), init, unroll=MLA_UNROLL)
            m, l, acc = step(qi, carry, True)
            o = acc / l
            lse = m + jnp.log(l)
            o_ref[qi * T:(qi + 1) * T, :] = jnp.where(lo, o[:T], o[T:])
            l_ref[qi * T:(qi + 1) * T, :] = jnp.where(lo, lse[:T], lse[T:])

    def spec(w):
        return pl.BlockSpec((None, S, w), lambda b, hp: (b, 0, hp))

    return pl.pallas_call(
        body,
        grid=(B, NH // 2),
        in_specs=[spec(2 * QKP), spec(2 * QKP), spec(LANE)],
        out_specs=[spec(LANE), spec(LANE)],
        out_shape=[jax.ShapeDtypeStruct((B, S, NH * VD), F32)] * 2,
        name="mla_attn_fwd",
        compiler_params=pltpu.CompilerParams(dimension_semantics=("arbitrary",) * 2, vmem_limit_bytes=VMEM_LIMIT),
    )(q, k, v)


def _mla_bwd(q, k, v, do, o, lse):
    B = q.shape[0]
    T = MLA_T
    NB = S // T

    def body(q_ref, k_ref, v_ref, do_ref, o_ref, l_ref, dq_ref, dk_ref, dv_ref, delta_ref):
        delta_ref[...] = _head_sum(do_ref[...] * o_ref[...])
        row = lax.broadcasted_iota(jnp.int32, (T, T), 0)
        col = lax.broadcasted_iota(jnp.int32, (T, T), 1)
        lo = _lo_mask((T, LANE))

        for j in range(NB):
            krows = slice(j * T, (j + 1) * T)
            kb = k_ref[krows, :]
            vb = v_ref[krows, :]
            dk = [jnp.zeros((T, QKP), F32), jnp.zeros((T, QKP), F32)]
            dv = jnp.zeros((T, LANE), F32)
            for i in range(j, NB):
                qrows = slice(i * T, (i + 1) * T)
                qb = q_ref[qrows, :]
                do2 = _stack_heads(do_ref[qrows, :], lo).astype(BF16)
                lb = l_ref[qrows, :]
                db = delta_ref[qrows, :]
                dp2 = _dot_nt(do2, vb)
                for e in (0, 1):
                    cols = slice(e * QKP, (e + 1) * QKP)
                    qe, ke = qb[:, cols], kb[:, cols]
                    s = _dot_nt(qe, ke) * MLA_SCALE
                    if i == j:
                        s = jnp.where(col <= row, s, NEG)
                    p = jnp.exp(s - lb[:, e * HD:e * HD + 1])
                    dv = dv + _dot_tn(p.astype(BF16), do2[e * T:(e + 1) * T])
                    ds = (p * (dp2[e * T:(e + 1) * T] - db[:, e * HD:e * HD + 1]) * MLA_SCALE).astype(BF16)
                    dk[e] = dk[e] + _dot_tn(ds, qe)
                    if j == 0:
                        dq_ref[qrows, cols] = _dot(ds, ke)
                    else:
                        dq_ref[qrows, cols] += _dot(ds, ke)
            dk_ref[krows, 0:QKP] = dk[0]
            dk_ref[krows, QKP:2 * QKP] = dk[1]
            dv_ref[krows, :] = dv

    def spec(w):
        return pl.BlockSpec((None, S, w), lambda b, hp: (b, 0, hp))

    return pl.pallas_call(
        body,
        grid=(B, NH // 2),
        in_specs=[spec(2 * QKP), spec(2 * QKP), spec(LANE), spec(LANE), spec(LANE), spec(LANE)],
        out_specs=[spec(2 * QKP), spec(2 * QKP), spec(LANE)],
        out_shape=[jax.ShapeDtypeStruct((B, S, NH * QKP), F32), jax.ShapeDtypeStruct((B, S, NH * QKP), F32),
                   jax.ShapeDtypeStruct((B, S, NH * VD), F32)],
        scratch_shapes=[pltpu.VMEM((S, LANE), F32)],
        name="mla_attn_bwd",
        compiler_params=pltpu.CompilerParams(dimension_semantics=("arbitrary",) * 2, vmem_limit_bytes=VMEM_LIMIT),
    )(q, k, v, do, o, lse)


def _lo_mask(shape):
    return lax.broadcasted_iota(jnp.int32, shape, len(shape) - 1) < HD


def _head_sum(u):
    r = lax.broadcasted_iota(jnp.int32, (LANE, LANE), 0) < HD
    c = lax.broadcasted_iota(jnp.int32, (LANE, LANE), 1) < HD
    ones = jnp.where(r == c, 1.0, 0.0).astype(BF16)
    hi = u.astype(BF16)
    lo = (u - hi.astype(F32)).astype(BF16)
    return _dot(hi, ones) + _dot(lo, ones)


def _rms2(x, g):
    return x * lax.rsqrt(_head_sum(x * x) / HD + EPS) * g


def _dil_bias(t_ref, gi, d):
    qq = lax.broadcasted_iota(jnp.int32, (QB, QB), 0)
    kk = lax.broadcasted_iota(jnp.int32, (QB, QB), 1)
    jc = (qq - kk).astype(F32)
    rows = []
    for e in (0, 1):
        sl = t_ref[2 * gi + e:2 * gi + e + 1, :] * float(d)
        bp = jnp.where(kk >= qq, -sl * (jc + float(QB)), NEG)
        bc = jnp.where(kk <= qq, -sl * jc, NEG)
        rows.append(jnp.concatenate([bp, bc], axis=1))
    return jnp.concatenate(rows, axis=0)


def _dil_rows(cur, d):
    return pl.ds(cur, QB, stride=d) if d > 1 else pl.ds(pl.multiple_of(cur, QB), QB)


def _dil_walk(d, block, full):
    if d == 1:
        block(0, None)

        def body(i, c):
            block(i * QB, (i - 1) * QB)
            return c
        lax.fori_loop(1, S // QB, body, 0, unroll=True if full else 5)
    elif d == 16:
        def body(r, c):
            block(r, None)
            return c
        lax.fori_loop(0, d, body, 0, unroll=True if full else 4)
    else:
        nb = S // d // QB

        def cls(r, c):
            block(r, None)

            def body(i, c2):
                block(r + i * QB * d, r + (i - 1) * QB * d)
                return c2
            lax.fori_loop(1, nb, body, 0, unroll=True)
            return c
        lax.fori_loop(0, d, cls, 0, unroll=full)


def _stack_heads(x, lo):
    return jnp.concatenate([jnp.where(lo, x, 0.0), jnp.where(lo, 0.0, x)], axis=0)


def _dilc_fwd(proj3, gq, gk, tab):
    B = proj3.shape[0]

    def body(q_ref, k_ref, v_ref, cz_ref, gq_ref, gk_ref, t_ref, y_ref, o_ref, l_ref, qs, ks, vs):
        g = pl.program_id(2)
        lo = _lo_mask((QB, LANE))

        def group(gi):
            d = DIL[gi][1]
            qs[...] = _rms2(q_ref[...].astype(F32), gq_ref[gi:gi + 1, :])
            ks[...] = _rms2(k_ref[...].astype(F32), gk_ref[gi:gi + 1, :])
            vs[...] = v_ref[...].astype(F32)
            bias = _dil_bias(t_ref, gi, d)

            def block(cur, prev):
                rows = _dil_rows(cur, d)
                q2 = _stack_heads(qs[rows, :], lo).astype(BF16)
                kc, vc = ks[rows, :], vs[rows, :]
                if prev is None:
                    kcat, vcat, b = kc, vc, bias[:, QB:]
                else:
                    prow = _dil_rows(prev, d)
                    kcat = jnp.concatenate([ks[prow, :], kc], axis=0)
                    vcat = jnp.concatenate([vs[prow, :], vc], axis=0)
                    b = bias
                s = _dot_nt(q2, kcat.astype(BF16)) * DIL_SCALE + b
                m = jnp.max(s, axis=-1, keepdims=True)
                p = jnp.exp(s - m)
                l = jnp.sum(p, axis=-1, keepdims=True)
                o = _dot(p.astype(BF16), vcat.astype(BF16)) / l
                lse = m + jnp.log(l)
                o_ref[gi, rows, :] = jnp.where(lo, o[:QB], o[QB:])
                l_ref[gi, rows, :] = jnp.where(lo, lse[:QB], lse[QB:])

            _dil_walk(d, block, True)

        for gi in range(NG):
            pl.when(g == gi)(functools.partial(group, gi))

        @pl.when(g == NG - 1)
        def _():
            y_ref[...] = _mergec_math(o_ref[0], o_ref[1], o_ref[2], l_ref[0], l_ref[1], l_ref[2],
                                      cz_ref[...].astype(F32)).astype(BF16)

    def col(base):
        return pl.BlockSpec((None, S, LANE), lambda b, hp, g: (b, 0, base // LANE + 4 * g + hp))

    gspec = pl.BlockSpec((NG, LANE), lambda b, hp, g: (0, 0))
    saved = pl.BlockSpec((NG, None, S, LANE), lambda b, hp, g: (0, b, 0, hp))
    return pl.pallas_call(
        body,
        grid=(B, 4, NG),
        in_specs=[col(O_DQ), col(O_DK), col(O_DV),
                  pl.BlockSpec((None, S, LANE), lambda b, hp, g: (b, 0, O_CZ // LANE + hp)),
                  gspec, gspec, pl.BlockSpec((None, 8, LANE), lambda b, hp, g: (hp, 0, 0))],
        out_specs=[pl.BlockSpec((None, S, LANE), lambda b, hp, g: (b, 0, hp)), saved, saved],
        out_shape=[jax.ShapeDtypeStruct((B, S, DWID), BF16), jax.ShapeDtypeStruct((NG, B, S, DWID), F32),
                   jax.ShapeDtypeStruct((NG, B, S, DWID), F32)],
        scratch_shapes=[pltpu.VMEM((S, LANE), F32)] * 3,
        name="dil_mixer_fwd",
        compiler_params=pltpu.CompilerParams(dimension_semantics=("arbitrary",) * 3, vmem_limit_bytes=VMEM_LIMIT),
    )(proj3, proj3, proj3, proj3, gq, gk, tab)


def _dilc_bwd(proj3, gq, gk, tab, o_all, l_all, d_yc):
    B = proj3.shape[0]

    def body(q_ref, k_ref, v_ref, cz_ref, gq_ref, gk_ref, t_ref, o_ref, l_ref, dy_ref,
             dq_out, dk_out, dv_out, dcz_out, dgq_out, dgk_out, qs, ks, vs, dos, dls, dqs, dks, dvs):
        g = pl.program_id(2)
        lo = _lo_mask((QB, LANE))

        @pl.when(jnp.logical_and(jnp.logical_and(pl.program_id(0) == 0, pl.program_id(1) == 0), g == 0))
        def _():
            dgq_out[...] = jnp.zeros((NG, LANE), F32)
            dgk_out[...] = jnp.zeros((NG, LANE), F32)

        def group(gi):
            d = DIL[gi][1]
            ls = [l_ref[j] for j in range(NG)]
            m = jnp.maximum(jnp.maximum(ls[0], ls[1]), ls[2])
            es = [jnp.exp(t - m) for t in ls]
            den = (es[0] + es[1]) + es[2]
            al = [e / den for e in es]
            os_ = [o_ref[j] for j in range(NG)]
            oc = (al[0] * os_[0] + al[1] * os_[1]) + al[2] * os_[2]
            cz = cz_ref[...].astype(F32)
            sg = jax.nn.sigmoid(cz)
            dy = dy_ref[...]
            d_oc = dy * (cz * sg)
            dcz_out[...] = (dy * oc * (sg * (1.0 + cz * (1.0 - sg)))).astype(BF16)
            ts = [_head_sum(d_oc * os_[j]) for j in range(NG)]
            tbar = (al[0] * ts[0] + al[1] * ts[1]) + al[2] * ts[2]
            dos[...] = al[gi] * d_oc
            dls[...] = al[gi] * (ts[gi] - tbar)

            qs[...] = _rms2(q_ref[...].astype(F32), gq_ref[gi:gi + 1, :])
            ks[...] = _rms2(k_ref[...].astype(F32), gk_ref[gi:gi + 1, :])
            vs[...] = v_ref[...].astype(F32)
            dks[...] = jnp.zeros((S, LANE), F32)
            dvs[...] = jnp.zeros((S, LANE), F32)
            bias = _dil_bias(t_ref, gi, d)

            def block(cur, prev):
                rows = _dil_rows(cur, d)
                q2 = _stack_heads(qs[rows, :], lo).astype(BF16)
                dob = dos[rows, :]
                do2 = _stack_heads(dob, lo).astype(BF16)
                kc, vc = ks[rows, :], vs[rows, :]
                if prev is None:
                    kcat, vcat, b = kc, vc, bias[:, QB:]
                else:
                    prow = _dil_rows(prev, d)
                    kcat = jnp.concatenate([ks[prow, :], kc], axis=0)
                    vcat = jnp.concatenate([vs[prow, :], vc], axis=0)
                    b = bias
                kcat = kcat.astype(BF16)
                vcat = vcat.astype(BF16)
                lse_b = l_ref[gi, rows, :]
                corr_b = dls[rows, :] - _head_sum(dob * o_ref[gi, rows, :])
                lse2 = jnp.concatenate([lse_b[:, 0:1], lse_b[:, HD:HD + 1]], axis=0)
                corr2 = jnp.concatenate([corr_b[:, 0:1], corr_b[:, HD:HD + 1]], axis=0)
                s = _dot_nt(q2, kcat) * DIL_SCALE + b
                p = jnp.exp(s - lse2)
                ds = (p * (_dot_nt(do2, vcat) + corr2) * DIL_SCALE).astype(BF16)
                dq2 = _dot(ds, kcat)
                dqs[rows, :] = jnp.where(lo, dq2[:QB], dq2[QB:])
                dk = _dot_tn(ds, q2)
                dv = _dot_tn(p.astype(BF16), do2)
                if prev is None:
                    dks[rows, :] += dk
                    dvs[rows, :] += dv
                else:
                    dks[prow, :] += dk[:QB]
                    dvs[prow, :] += dv[:QB]
                    dks[rows, :] += dk[QB:]
                    dvs[rows, :] += dv[QB:]

            _dil_walk(d, block, False)

            _, pull_q = jax.vjp(_rms2, q_ref[...].astype(F32), gq_ref[gi:gi + 1, :])
            dxq, dgq = pull_q(dqs[...])
            dq_out[...] = dxq.astype(BF16)
            dgq_out[gi:gi + 1, :] += dgq
            _, pull_k = jax.vjp(_rms2, k_ref[...].astype(F32), gk_ref[gi:gi + 1, :])
            dxk, dgk = pull_k(dks[...])
            dk_out[...] = dxk.astype(BF16)
            dgk_out[gi:gi + 1, :] += dgk
            dv_out[...] = dvs[...].astype(BF16)

        for gi in range(NG):
            pl.when(g == gi)(functools.partial(group, gi))

    def col(base):
        return pl.BlockSpec((None, S, LANE), lambda b, hp, g: (b, 0, base // LANE + 4 * g + hp))

    gspec = pl.BlockSpec((NG, LANE), lambda b, hp, g: (0, 0))
    saved = pl.BlockSpec((NG, None, S, LANE), lambda b, hp, g: (0, b, 0, hp))
    per_pair = pl.BlockSpec((None, S, LANE), lambda b, hp, g: (b, 0, hp))
    dcol = pl.BlockSpec((None, S, LANE), lambda b, hp, g: (b, 0, 4 * g + hp))
    return pl.pallas_call(
        body,
        grid=(B, 4, NG),
        in_specs=[col(O_DQ), col(O_DK), col(O_DV),
                  pl.BlockSpec((None, S, LANE), lambda b, hp, g: (b, 0, O_CZ // LANE + hp)),
                  gspec, gspec, pl.BlockSpec((None, 8, LANE), lambda b, hp, g: (hp, 0, 0)),
                  saved, saved, per_pair],
        out_specs=[dcol, dcol, dcol, per_pair, gspec, gspec],
        out_shape=[jax.ShapeDtypeStruct((B, S, NG * DWID), BF16)] * 3
        + [jax.ShapeDtypeStruct((B, S, DWID), BF16), jax.ShapeDtypeStruct((NG, LANE), F32),
           jax.ShapeDtypeStruct((NG, LANE), F32)],
        scratch_shapes=[pltpu.VMEM((S, LANE), F32)] * 8,
        name="dil_mixer_bwd",
        compiler_params=pltpu.CompilerParams(dimension_semantics=("arbitrary",) * 3, vmem_limit_bytes=VMEM_LIMIT),
    )(proj3, proj3, proj3, proj3, gq, gk, tab, o_all, l_all, d_yc)


def _dil_slopes():
    slopes = (2.0 ** (-8.0 * np.arange(1, NG * DH + 1, dtype=np.float32) / (NG * DH))).astype(np.float32).reshape(NG, DH)
    tab = np.zeros((4, 8, LANE), np.float32)
    for hp in range(4):
        for gi in range(NG):
            for e in (0, 1):
                tab[hp, 2 * gi + e, :] = slopes[gi, 2 * hp + e]
    return jnp.asarray(tab)


def _rope_tables():
    inv = ROPE_THETA ** (-jnp.arange(0, ROPE, 2, dtype=F32) / ROPE)
    ang = jnp.arange(S, dtype=F32)[:, None] * inv[None, :]
    cos, sin = jnp.cos(ang), jnp.sin(ang)
    z16 = jnp.zeros((S, 16), F32)
    c = jnp.concatenate([jnp.ones((S, NOPE), F32), cos, cos, jnp.zeros((S, 32), F32)], axis=1)
    s1 = jnp.concatenate([jnp.zeros((S, NOPE), F32), -sin, z16, jnp.zeros((S, 32), F32)], axis=1)
    s2 = jnp.concatenate([jnp.zeros((S, NOPE), F32), z16, sin, jnp.zeros((S, 32), F32)], axis=1)
    return c, s1, s2


def _pad_heads_uq(w):
    return jnp.pad(w.reshape(QL, NH, QK), ((0, 0), (0, 0), (0, QKP - QK))).reshape(QL, NH * QKP)


def _unpad_heads_uq(g):
    return g.reshape(QL, NH, QKP)[:, :, :QK].reshape(QL, NH * QK)


def _split_ukv(w):
    w3 = w.reshape(KVL, NH, NOPE + VD)
    uk = jnp.pad(w3[:, :, :NOPE], ((0, 0), (0, 0), (0, QKP - NOPE))).reshape(KVL, NH * QKP)
    return uk, w3[:, :, NOPE:].reshape(KVL, NH * VD)


def _join_ukv(guk, guv):
    return jnp.concatenate([guk.reshape(KVL, NH, QKP)[:, :, :NOPE], guv.reshape(KVL, NH, VD)],
                           axis=-1).reshape(KVL, NH * (NOPE + VD))


BR = 512
BRM = 256


def _layer_fwd(x, w, tabs, batch):
    T = batch * S
    rope_c, rope_s1, rope_s2, dil_tab = tabs
    res = {"x": x}
    row = lambda c: (lambda i: (i, c))
    fix = lambda i: (0, 0)

    h = _pcall("norm_fwd", _rms, (T // BR,),
               [(x, (BR, D), row(0)), (w["norm_g"], (1, D), fix)],
               [((T, D), BF16, (BR, D), row(0))])[0]
    proj = _mm("in_proj", h, w["w_in_t"], tb=True, out_dtype=BF16, tm=512, tn=1024)
    res["h"], res["proj"] = h, proj
    proj3 = proj.reshape(batch, S, NINP)

    cblk = lambda s: (lambda j, b: (b, 0, 4 * s + j))
    y_a = _pcall("conv_fwd", _conv_math, (4, batch),
                 [(proj3, (None, S, LANE), cblk(0)), (proj3, (None, S, LANE), cblk(1)),
                  (proj3, (None, S, LANE), cblk(2)), (proj3, (None, S, LANE), cblk(3)),
                  (w["conv_w"], (3, LANE), lambda j, b: (0, j)), (w["conv_b"], (1, LANE), lambda j, b: (0, j))],
                 [((batch, S, CW), BF16, (None, S, LANE), lambda j, b: (b, 0, j))])[0].reshape(T, CW)
    res["y_a"] = y_a

    cqn, ckvn = _pcall("mla_pre_fwd", _mla_pre_math, (T // BR,),
                       [(proj, (BR, QL), row(O_CQ // QL)), (proj, (BR, KVL), row(O_CKV // KVL)),
                        (w["q_a_norm_g"], (1, QL), fix), (w["kv_a_norm_g"], (1, KVL), fix)],
                       [((T, QL), BF16, (BR, QL), row(0)), ((T, KVL), BF16, (BR, KVL), row(0))])
    w_uq_p = _pad_heads_uq(w["w_uq"])
    w_uk, w_uv = _split_ukv(w["w_ukv"])
    q = _mm("uq", cqn, w_uq_p, out_dtype=BF16)
    kn = _mm("uk", ckvn, w_uk, out_dtype=BF16)
    v = _mm("uv", ckvn, w_uv, out_dtype=BF16)
    nrr = S // BR
    tab_row = lambda i: (i % nrr, 0)
    qr, kr = _pcall("rope_fwd", _rope_math, (T // BR,),
                    [(q, (BR, NH * QKP), row(0)), (kn, (BR, NH * QKP), row(0)), (proj, (BR, LANE), row(O_KPE // LANE)),
                     (w["mla_q_norm_g"], (1, QKP), fix), (w["mla_k_norm_g"], (1, QKP), fix),
                     (rope_c, (BR, QKP), tab_row), (rope_s1, (BR, QKP), tab_row), (rope_s2, (BR, QKP), tab_row)],
                    [((T, NH * QKP), BF16, (BR, NH * QKP), row(0))] * 2)
    qr = qr.reshape(batch, S, NH * QKP)
    kr = kr.reshape(batch, S, NH * QKP)
    v = v.reshape(batch, S, NH * VD)
    o_b, l_b = _mla_fwd(qr, kr, v)
    ob2 = o_b.reshape(T, NH * VD)
    y_b = _pcall("gateb_fwd", _gate_math, (T // BR,),
                 [(ob2, (BR, 512), row(0)), (proj, (BR, 512), row(O_BZ // 512))],
                 [((T, 512), BF16, (BR, 512), row(0))])[0]
    res.update(cqn=cqn, ckvn=ckvn, q=q, kn=kn, qr=qr, kr=kr, v=v, o_b=o_b, l_b=l_b, ob2=ob2, y_b=y_b,
               w_uq_p=w_uq_p, w_uk=w_uk, w_uv=w_uv)

    gq2 = jnp.tile(w["dil_q_norm_g"].reshape(NG, HD), (1, 2))
    gk2 = jnp.tile(w["dil_k_norm_g"].reshape(NG, HD), (1, 2))
    y_c, o_all, l_all = _dilc_fwd(proj3, gq2, gk2, dil_tab)
    y_c = y_c.reshape(T, DWID)
    res.update(o_all=o_all, l_all=l_all, y_c=y_c)

    pa = _mm("out_a", y_a, w["w_out_a"], out_dtype=BF16)
    pb = _mm("out_b", y_b, w["w_out_b"], out_dtype=BF16)
    pc = _mm("out_c", y_c, w["w_out_c"], out_dtype=BF16)
    merged = _pcall("merge_fwd", _merge_math, (T // BRM,),
                    [(proj, (BRM, D), row(O_G // D + s)) for s in range(3)]
                    + [(w["b_gate"], (1, D), (lambda s: (lambda i: (0, s)))(s)) for s in range(3)]
                    + [(t, (BRM, D), row(0)) for t in (pa, pb, pc)],
                    [((T, D), BF16, (BRM, D), row(0))])[0]
    out = _mm("o_proj", merged, w["w_o"], add=x)
    res.update(pa=pa, pb=pb, pc=pc, merged=merged)
    return out, res


def _norm_bwd_math(x, g, dh, dy):
    _, pull = jax.vjp(_rms, x, g)
    dx, dg = pull(dh)
    return dx + dy, dg


def _layer_bwd(dy, w, res, tabs, batch, after_dw=None):
    T = batch * S
    rope_c, rope_s1, rope_s2, dil_tab = tabs
    row = lambda c: (lambda i: (i, c))
    fix = lambda i: (0, 0)
    x, proj, h = res["x"], res["proj"], res["h"]
    proj3 = proj.reshape(batch, S, NINP)
    g = {}

    d_merged = _mm("o_proj_dx", dy, w["w_o"], tb=True)
    g["w_o"] = _mm("o_proj_dw", res["merged"], dy, ta=True, tm=1024)

    merge_bwd = functools.partial(_vjp_of(_merge_math, 9), n_prim=9)
    dg0, dg1, dg2, db0, db1, db2, dpa, dpb, dpc = _pcall(
        "merge_bwd", merge_bwd, (T // BRM,),
        [(proj, (BRM, D), row(O_G // D + s)) for s in range(3)]
        + [(w["b_gate"], (1, D), (lambda s: (lambda i: (0, s)))(s)) for s in range(3)]
        + [(t, (BRM, D), row(0)) for t in (res["pa"], res["pb"], res["pc"])]
        + [(d_merged, (BRM, D), row(0))],
        [((T, D), BF16, (BRM, D), row(0))] * 3 + [((1, D), F32, (1, D), fix, True)] * 3
        + [((T, D), BF16, (BRM, D), row(0))] * 3)
    g["b_gate"] = jnp.concatenate([db0, db1, db2], axis=1)

    d_ya = _mm("out_a_dx", dpa, w["w_out_a"], tb=True)
    d_yb = _mm("out_b_dx", dpb, w["w_out_b"], tb=True)
    d_yc = _mm("out_c_dx", dpc, w["w_out_c"], tb=True)
    g["w_out_a"] = _mm("out_a_dw", res["y_a"], dpa, ta=True)
    g["w_out_b"] = _mm("out_b_dw", res["y_b"], dpb, ta=True)
    g["w_out_c"] = _mm("out_c_dw", res["y_c"], dpc, ta=True)

    cblk = lambda s: (lambda j, b: (b, 0, 4 * s + j))
    oblk = lambda j, b: (b, 0, j)
    conv_bwd = functools.partial(_vjp_of(_conv_math, 6), n_prim=6)
    d_ab, d_ac, d_ax, d_az, g["conv_w"], g["conv_b"] = _pcall(
        "conv_bwd", conv_bwd, (4, batch),
        [(proj3, (None, S, LANE), cblk(s)) for s in range(4)]
        + [(w["conv_w"], (3, LANE), lambda j, b: (0, j)), (w["conv_b"], (1, LANE), lambda j, b: (0, j)),
           (d_ya.reshape(batch, S, CW), (None, S, LANE), oblk)],
        [((batch, S, CW), BF16, (None, S, LANE), oblk)] * 4
        + [((3, CW), F32, (3, LANE), lambda j, b: (0, j), True), ((1, CW), F32, (1, LANE), lambda j, b: (0, j), True)])

    gate_bwd = functools.partial(_vjp_of(_gate_math, 2), n_prim=2)
    d_ob, d_bz = _pcall("gateb_bwd", gate_bwd, (T // BR,),
                        [(res["ob2"], (BR, 512), row(0)), (proj, (BR, 512), row(O_BZ // 512)), (d_yb, (BR, 512), row(0))],
                        [((T, 512), F32, (BR, 512), row(0)), ((T, 512), BF16, (BR, 512), row(0))])
    dqr, dkr, dv = _mla_bwd(res["qr"], res["kr"], res["v"], d_ob.reshape(batch, S, NH * VD), res["o_b"], res["l_b"])
    nrr = S // BR
    tab_row = lambda i: (i % nrr, 0)
    rope_bwd = functools.partial(_vjp_of(_rope_math, 5), n_prim=8)
    d_q, d_kn, d_kpe_p, g["mla_q_norm_g"], g["mla_k_norm_g"] = _pcall(
        "rope_bwd", rope_bwd, (T // BR,),
        [(res["q"], (BR, NH * QKP), row(0)), (res["kn"], (BR, NH * QKP), row(0)), (proj, (BR, LANE), row(O_KPE // LANE)),
         (w["mla_q_norm_g"], (1, QKP), fix), (w["mla_k_norm_g"], (1, QKP), fix),
         (rope_c, (BR, QKP), tab_row), (rope_s1, (BR, QKP), tab_row), (rope_s2, (BR, QKP), tab_row),
         (dqr.reshape(T, NH * QKP), (BR, NH * QKP), row(0)), (dkr.reshape(T, NH * QKP), (BR, NH * QKP), row(0))],
        [((T, NH * QKP), BF16, (BR, NH * QKP), row(0))] * 2 + [((T, LANE), BF16, (BR, LANE), row(0))]
        + [((1, QKP), F32, (1, QKP), fix, True)] * 2)
    dv = dv.reshape(T, NH * VD)
    d_cqn = _mm("uq_dx", d_q, res["w_uq_p"], tb=True)
    d_ckvn = _mm("uk_dx", d_kn, res["w_uk"], tb=True)
    d_ckvn = _mm("uv_dx", dv, res["w_uv"], tb=True, add=d_ckvn)
    g["w_uq"] = _unpad_heads_uq(_mm("uq_dw", res["cqn"], d_q, ta=True))
    g["w_ukv"] = _join_ukv(_mm("uk_dw", res["ckvn"], d_kn, ta=True), _mm("uv_dw", res["ckvn"], dv, ta=True))
    pre_bwd = functools.partial(_vjp_of(_mla_pre_math, 4), n_prim=4)
    d_cq, d_ckv, g["q_a_norm_g"], g["kv_a_norm_g"] = _pcall(
        "mla_pre_bwd", pre_bwd, (T // BR,),
        [(proj, (BR, QL), row(O_CQ // QL)), (proj, (BR, KVL), row(O_CKV // KVL)),
         (w["q_a_norm_g"], (1, QL), fix), (w["kv_a_norm_g"], (1, KVL), fix),
         (d_cqn, (BR, QL), row(0)), (d_ckvn, (BR, KVL), row(0))],
        [((T, QL), BF16, (BR, QL), row(0)), ((T, KVL), BF16, (BR, KVL), row(0)),
         ((1, QL), F32, (1, QL), fix, True), ((1, KVL), F32, (1, KVL), fix, True)])

    gq2 = jnp.tile(w["dil_q_norm_g"].reshape(NG, HD), (1, 2))
    gk2 = jnp.tile(w["dil_k_norm_g"].reshape(NG, HD), (1, 2))
    d_dq, d_dk, d_dv, d_cz, dgq, dgk = _dilc_bwd(proj3, gq2, gk2, dil_tab, res["o_all"], res["l_all"],
                                                 d_yc.reshape(batch, S, DWID))
    g["dil_q_norm_g"] = dgq[:, :HD] + dgq[:, HD:]
    g["dil_k_norm_g"] = dgk[:, :HD] + dgk[:, HD:]
    d_dq, d_dk, d_dv = (t.reshape(T, NG * DWID) for t in (d_dq, d_dk, d_dv))
    d_cz = d_cz.reshape(T, DWID)

    segs = [t.reshape(T, CW) for t in (d_ab, d_ac, d_ax, d_az)] + [
        (d_cq, d_ckv, d_kpe_p), d_bz, d_dq, d_dk, d_dv, d_cz, dg0, dg1, dg2]
    g["w_in_t"] = _seg_matmul_dw("in_proj_dw", segs, h)
    dep = after_dw(g) if after_dw is not None else None
    d_h = _seg_matmul_dx("in_proj_dx", segs, w["w_in_t"], dep=dep)
    dx, g["norm_g"] = _pcall("norm_bwd", _norm_bwd_math, (T // BR,),
                             [(x, (BR, D), row(0)), (w["norm_g"], (1, D), fix), (d_h, (BR, D), row(0)),
                              (dy, (BR, D), row(0))],
                             [((T, D), F32, (BR, D), row(0)), ((1, D), F32, (1, D), fix, True)])
    return dx, g


def _loss_math(y, t):
    e = y - t
    return e * (1.0 / D), 0.5 * jnp.sum(jnp.sum(e * e, axis=-1, keepdims=True) / D, axis=0, keepdims=True)


def _local_step(x, target, ws, batch):
    T = batch * S
    tabs = _rope_tables() + (_dil_slopes(),)
    saved = []
    y = x
    for l in range(NL):
        y, res = _layer_fwd(y, ws[l], tabs, batch)
        saved.append(res)
    row = lambda i: (i, 0)
    dy, loss = _pcall("loss", _loss_math, (T // BR,),
                      [(y, (BR, D), row), (target, (BR, D), row)],
                      [((T, D), F32, (BR, D), row), ((1, 1), F32, (1, 1), lambda i: (0, 0), True)])
    grads = [None] * NL
    for l in reversed(range(NL)):
        dy, grads[l] = _layer_bwd(dy, ws[l], saved[l], tabs, batch)
    return loss, dy, grads


ANY = pl.BlockSpec(memory_space=pl.ANY)
U32 = jnp.uint32
WSH = NIN // 4
WA = KPE_END
WB = WSH - WA
CWD = 512
PACK_ROWS = 1472
HW = PACK_W // 2


def _me():
    return lax.axis_index("x"), lax.axis_index("y"), lax.axis_index("c")


def _piece_rows(k):
    a = k * WSH + jnp.where(k > 0, NINP - NIN, 0)
    b = k * WSH + WA + (NINP - NIN)
    return ((0, pl.multiple_of(a, 8), WA), (WA, pl.multiple_of(b, 8), WB))


def _pack_words(lo, hi):
    ul = lax.bitcast_convert_type(lo.astype(BF16).astype(F32), U32)
    uh = lax.bitcast_convert_type(hi.astype(BF16).astype(F32), U32)
    w = jnp.bitwise_or(jnp.bitwise_and(uh, jnp.uint32(0xFFFF0000)), jnp.right_shift(ul, jnp.uint32(16)))
    return lax.bitcast_convert_type(w, F32)


def _unpack_words(w):
    w = lax.bitcast_convert_type(w, U32)
    lo = lax.bitcast_convert_type(jnp.left_shift(w, jnp.uint32(16)), F32)
    hi = lax.bitcast_convert_type(jnp.bitwise_and(w, jnp.uint32(0xFFFF0000)), F32)
    return lo, hi


def _all_gather(wc, sp):
    def body(w_ref, s_ref, ow_ref, os_ref, send_sems, recv_sems):
        x, y, c = _me()
        k_me = 2 * x + y
        sib = (x, y, 1 - c)
        chips = [(1 - x, y), (x, 1 - y), (1 - x, 1 - y)]
        wcols = lambda cc: pl.ds(pl.multiple_of(cc * (CWD // 2), LANE), CWD // 2)
        scols = lambda cc: pl.ds(pl.multiple_of(cc * HW, LANE), HW)

        def windows(k, cc):
            pcs = _piece_rows(k)
            return ([(w_ref.at[pl.ds(l0, n), wcols(cc)], ow_ref.at[pl.ds(p0, n), wcols(cc)]) for l0, p0, n in pcs]
                    + [(s_ref.at[:, scols(cc)], os_ref.at[k, :, scols(cc)])])

        def copy(i, src, dst, to):
            return pltpu.make_async_remote_copy(src_ref=src, dst_ref=dst, send_sem=send_sems.at[i],
                                                recv_sem=recv_sems.at[i], device_id=to, device_id_type=MESH)

        def own_windows():
            return ([(w_ref.at[pl.ds(l0, n)], ow_ref.at[pl.ds(p0, n)]) for l0, p0, n in _piece_rows(k_me)]
                    + [(s_ref, os_ref.at[k_me])])

        first = [copy(18 + i, src, dst, sib) for i, (src, dst) in enumerate(own_windows())]
        for j, (cx, cy) in enumerate(chips):
            for i, (src, dst) in enumerate(windows(k_me, c)):
                first.append(copy(3 * j + i, src, dst, (cx, cy, c)))
        for cp in first:
            cp.start()
        passed = []
        for j, (cx, cy) in enumerate(chips):
            for i, (_, dst) in enumerate(windows(2 * cx + cy, c)):
                copy(3 * j + i, dst, dst, (cx, cy, c)).wait_recv()
                cp = copy(9 + 3 * j + i, dst, dst, sib)
                cp.start()
                passed.append(cp)
        for j, (cx, cy) in enumerate(chips):
            for i, (_, dst) in enumerate(windows(2 * cx + cy, 1 - c)):
                copy(9 + 3 * j + i, dst, dst, sib).wait_recv()
        for i, (_, dst) in enumerate(own_windows()):
            copy(18 + i, dst, dst, sib).wait_recv()
        for cp in first + passed:
            cp.wait_send()

    return pl.pallas_call(
        body,
        out_shape=[jax.ShapeDtypeStruct((NINP, CWD), F32), jax.ShapeDtypeStruct((4, PACK_ROWS, PACK_W), BF16)],
        in_specs=[ANY, ANY], out_specs=[ANY, ANY],
        scratch_shapes=[pltpu.SemaphoreType.DMA((21,)), pltpu.SemaphoreType.DMA((21,))],
        name="weights_all_gather",
    )(wc, sp)


HBM = pl.BlockSpec(memory_space=pltpu.HBM)
SEM = pl.BlockSpec(memory_space=pltpu.SEMAPHORE)
EFFECT = pltpu.SideEffectType.DATAFLOW_SIDE_EFFECTING


def _in_hbm(a):
    return pltpu.with_memory_space_constraint(a, pltpu.HBM)


def _ag_windows(w_ref, s_ref, lw_ref, ls_ref, k, cc):
    wcols = pl.ds(pl.multiple_of(cc * (CWD // 2), LANE), CWD // 2)
    scols = pl.ds(pl.multiple_of(cc * HW, LANE), HW)
    return ([(w_ref.at[pl.ds(l0, n), wcols], lw_ref.at[pl.ds(p0, n), wcols]) for l0, p0, n in _piece_rows(k)]
            + [(s_ref.at[:, scols], ls_ref.at[k, :, scols])])


def _ag_ici_copies(w_ref, s_ref, lw_ref, ls_ref, send_sems, recv_sems):
    x, y, c = _me()
    mine, theirs = [], []
    for j, (cx, cy) in enumerate([(1 - x, y), (x, 1 - y), (1 - x, 1 - y)]):
        for i, ((src, dst), (_, got)) in enumerate(zip(_ag_windows(w_ref, s_ref, lw_ref, ls_ref, 2 * x + y, c),
                                                       _ag_windows(w_ref, s_ref, lw_ref, ls_ref, 2 * cx + cy, c))):
            mk = lambda s_, d_: pltpu.make_async_remote_copy(
                src_ref=s_, dst_ref=d_, send_sem=send_sems.at[3 * j + i], recv_sem=recv_sems.at[3 * j + i],
                device_id=(cx, cy, c), device_id_type=MESH)
            mine.append(mk(src, dst))
            theirs.append(mk(got, got))
    return mine, theirs


def _ag_ici_start(wc, sp, dep):
    def body(w_ref, s_ref, lw_ref, ls_ref, dep_ref, send_sems, recv_sems, w_thru, s_thru, lw_thru, ls_thru, token):
        mine, _ = _ag_ici_copies(w_ref, s_ref, lw_ref, ls_ref, send_sems, recv_sems)
        for cp in mine:
            cp.start()
        token[...] = jnp.zeros_like(token)

    return pl.pallas_call(
        body, name="weights_gather_start",
        out_shape=(pltpu.SemaphoreType.DMA((9,)), pltpu.SemaphoreType.DMA((9,)), pltpu.HBM(wc.shape, wc.dtype),
                   pltpu.HBM(sp.shape, sp.dtype), pltpu.HBM((NINP, CWD), F32), pltpu.HBM((4, PACK_ROWS, PACK_W), BF16),
                   jax.ShapeDtypeStruct((8, LANE), F32)),
        in_specs=(HBM, HBM, HBM, HBM, ANY),
        out_specs=(SEM, SEM, HBM, HBM, HBM, HBM, pl.BlockSpec(memory_space=pltpu.VMEM)),
        input_output_aliases={0: 2, 1: 3, 2: 4, 3: 5},
        compiler_params=pltpu.CompilerParams(has_side_effects=EFFECT),
    )(_in_hbm(wc), _in_hbm(sp), _in_hbm(lax.empty((NINP, CWD), F32)), _in_hbm(lax.empty((4, PACK_ROWS, PACK_W), BF16)), dep)


def _ag_ici_wait(send_sems, recv_sems, wc, sp, lw, ls, after):
    def body(w_ref, s_ref, lw_ref, ls_ref, send_sems, recv_sems, after_ref, w_dead, s_dead, lw_out, ls_out):
        mine, theirs = _ag_ici_copies(w_ref, s_ref, lw_ref, ls_ref, send_sems, recv_sems)
        for cp in mine:
            cp.wait_send()
        for cp in theirs:
            cp.wait_recv()

    out = pl.pallas_call(
        body, name="weights_gather_wait",
        out_shape=(pltpu.HBM(wc.shape, wc.dtype), pltpu.HBM(sp.shape, sp.dtype), pltpu.HBM(lw.shape, lw.dtype),
                   pltpu.HBM(ls.shape, ls.dtype)),
        in_specs=(HBM, HBM, HBM, HBM, SEM, SEM, ANY), out_specs=(HBM, HBM, HBM, HBM),
        input_output_aliases={0: 0, 1: 1, 2: 2, 3: 3},
        compiler_params=pltpu.CompilerParams(has_side_effects=EFFECT),
    )(wc, sp, lw, ls, send_sems, recv_sems, after)
    return out[2], out[3]


def _ag_finish(wc, sp, lw, ls):
    def body(w_ref, s_ref, lw_ref, ls_ref, ow_ref, os_ref, send_sems, recv_sems):
        x, y, c = _me()
        k_me = 2 * x + y
        sib = (x, y, 1 - c)
        chips = [(1 - x, y), (x, 1 - y), (1 - x, 1 - y)]

        def copy(i, src, dst):
            return pltpu.make_async_remote_copy(src_ref=src, dst_ref=dst, send_sem=send_sems.at[i],
                                                recv_sem=recv_sems.at[i], device_id=sib, device_id_type=MESH)

        def own_windows():
            return ([(w_ref.at[pl.ds(l0, n)], ow_ref.at[pl.ds(p0, n)]) for l0, p0, n in _piece_rows(k_me)]
                    + [(s_ref, os_ref.at[k_me])])

        out = [copy(9 + i, src, dst) for i, (src, dst) in enumerate(own_windows())]
        for j, (cx, cy) in enumerate(chips):
            landed = _ag_windows(w_ref, s_ref, lw_ref, ls_ref, 2 * cx + cy, c)
            for i, (_, dst) in enumerate(_ag_windows(w_ref, s_ref, ow_ref, os_ref, 2 * cx + cy, c)):
                out.append(copy(3 * j + i, landed[i][1], dst))
        for cp in out:
            cp.start()
        for j, (cx, cy) in enumerate(chips):
            for i, (_, dst) in enumerate(_ag_windows(w_ref, s_ref, ow_ref, os_ref, 2 * cx + cy, 1 - c)):
                copy(3 * j + i, dst, dst).wait_recv()
        for i, (_, dst) in enumerate(own_windows()):
            copy(9 + i, dst, dst).wait_recv()
        for cp in out:
            cp.wait_send()

    return pl.pallas_call(
        body,
        out_shape=[jax.ShapeDtypeStruct(lw.shape, lw.dtype), jax.ShapeDtypeStruct(ls.shape, ls.dtype)],
        in_specs=[ANY] * 4, out_specs=[ANY, ANY],
        input_output_aliases={2: 0, 3: 1},
        scratch_shapes=[pltpu.SemaphoreType.DMA((12,)), pltpu.SemaphoreType.DMA((12,))],
        name="weights_gather_finish",
    )(wc, sp, lw, ls)


UNPACK_BR = 512


def _unpack_w_in(cont):
    def body(c_ref, o_ref):
        lo, hi = _unpack_words(c_ref[...])
        r = pl.program_id(0) * UNPACK_BR + lax.broadcasted_iota(jnp.int32, (UNPACK_BR, CWD), 0)
        pad = jnp.logical_and(r >= KPE_END, r < KPE_END + NINP - NIN)
        o_ref[:, 0:CWD] = jnp.where(pad, 0.0, lo).astype(BF16)
        o_ref[:, CWD:2 * CWD] = jnp.where(pad, 0.0, hi).astype(BF16)

    return pl.pallas_call(
        body, grid=(NINP // UNPACK_BR,),
        in_specs=[pl.BlockSpec((UNPACK_BR, CWD), lambda i: (i, 0))],
        out_specs=pl.BlockSpec((UNPACK_BR, D), lambda i: (i, 0)),
        out_shape=jax.ShapeDtypeStruct((NINP, D), BF16),
        name="w_in_unpack",
        compiler_params=pltpu.CompilerParams(dimension_semantics=("arbitrary",), vmem_limit_bytes=VMEM_LIMIT),
    )(cont)


def _rs_swap(gw, gs):
    def body(w_ref, s_ref, rw_ref, rs_ref, send_sems, recv_sems):
        x, y, c = _me()
        oc = 1 - c
        cps = [pltpu.make_async_remote_copy(src_ref=w_ref.at[:, pl.ds(pl.multiple_of(oc * (D // 2), LANE), D // 2)],
                                            dst_ref=rw_ref, send_sem=send_sems.at[0], recv_sem=recv_sems.at[0],
                                            device_id=(x, y, oc), device_id_type=MESH),
               pltpu.make_async_remote_copy(src_ref=s_ref.at[:, :, pl.ds(pl.multiple_of(oc * HW, LANE), HW)],
                                            dst_ref=rs_ref, send_sem=send_sems.at[1], recv_sem=recv_sems.at[1],
                                            device_id=(x, y, oc), device_id_type=MESH)]
        for cp in cps:
            cp.start()
        for cp in cps:
            cp.wait()

    return pl.pallas_call(
        body,
        out_shape=[jax.ShapeDtypeStruct((NINP, D // 2), F32), jax.ShapeDtypeStruct((4, PACK_ROWS, HW), F32)],
        in_specs=[ANY, ANY], out_specs=[ANY, ANY],
        scratch_shapes=[pltpu.SemaphoreType.DMA((2,)), pltpu.SemaphoreType.DMA((2,))],
        name="grads_sibling_swap",
    )(gw, gs)


SUM_BR = 512


def _rs_chip_sum_w(gw, rw, cidx):
    def body(c_ref, g_ref, r_ref, o_ref):
        s = g_ref[...] + r_ref[...]
        q = D // 8
        o_ref[...] = jnp.concatenate([_pack_words(s[:, 0:q], s[:, q:2 * q]),
                                      _pack_words(s[:, 2 * q:3 * q], s[:, 3 * q:4 * q])], axis=1)

    return pl.pallas_call(
        body,
        grid_spec=pltpu.PrefetchScalarGridSpec(
            num_scalar_prefetch=1, grid=(NINP // SUM_BR,),
            in_specs=[pl.BlockSpec((SUM_BR, D // 2), lambda i, cr: (i, cr[0])),
                      pl.BlockSpec((SUM_BR, D // 2), lambda i, cr: (i, 0))],
            out_specs=pl.BlockSpec((SUM_BR, D // 4), lambda i, cr: (i, 0))),
        out_shape=jax.ShapeDtypeStruct((NINP, D // 4), F32),
        name="grads_chip_sum_w",
        compiler_params=pltpu.CompilerParams(dimension_semantics=("arbitrary",), vmem_limit_bytes=VMEM_LIMIT),
    )(cidx, gw, rw)


def _rs_chip_sum_s(gs, rs, cidx):
    def body(c_ref, g_ref, r_ref, o_ref):
        o_ref[...] = (g_ref[...] + r_ref[...]).astype(BF16)

    return pl.pallas_call(
        body,
        grid_spec=pltpu.PrefetchScalarGridSpec(
            num_scalar_prefetch=1, grid=(4,),
            in_specs=[pl.BlockSpec((None, PACK_ROWS, HW), lambda j, cr: (j, 0, cr[0])),
                      pl.BlockSpec((None, PACK_ROWS, HW), lambda j, cr: (j, 0, 0))],
            out_specs=pl.BlockSpec((None, PACK_ROWS, HW), lambda j, cr: (j, 0, 0))),
        out_shape=jax.ShapeDtypeStruct((4, PACK_ROWS, HW), BF16),
        name="grads_chip_sum_s",
        compiler_params=pltpu.CompilerParams(dimension_semantics=("arbitrary",), vmem_limit_bytes=VMEM_LIMIT),
    )(cidx, gs, rs)


def _rs_exchange_copies(sw_ref, ss_ref, r2w_ref, r2s_ref, send_sems, recv_sems):
    x, y, c = _me()
    mine, theirs = [], []
    for j, (cx, cy) in enumerate([(1 - x, y), (x, 1 - y), (1 - x, 1 - y)]):
        def mk(i, src, dst):
            return pltpu.make_async_remote_copy(src_ref=src, dst_ref=dst, send_sem=send_sems.at[3 * j + i],
                                                recv_sem=recv_sems.at[3 * j + i], device_id=(cx, cy, c), device_id_type=MESH)
        for i, (l0, p0, n) in enumerate(_piece_rows(2 * cx + cy)):
            mine.append(mk(i, sw_ref.at[pl.ds(p0, n)], r2w_ref.at[j, pl.ds(l0, n)]))
            theirs.append(mk(i, r2w_ref.at[j, pl.ds(l0, n)], r2w_ref.at[j, pl.ds(l0, n)]))
        mine.append(mk(2, ss_ref.at[2 * cx + cy], r2s_ref.at[j]))
        theirs.append(mk(2, r2s_ref.at[j], r2s_ref.at[j]))
    return mine, theirs


def _rs_exchange_start(sw, ss, tag):
    def body(sw_ref, ss_ref, r2w_ref, r2s_ref, send_sems, recv_sems, sw_thru, ss_thru, r2w_thru, r2s_thru, token):
        mine, _ = _rs_exchange_copies(sw_ref, ss_ref, r2w_ref, r2s_ref, send_sems, recv_sems)
        for cp in mine:
            cp.start()
        token[...] = jnp.zeros_like(token)

    return pl.pallas_call(
        body, name="grads_exchange_start_" + tag,
        out_shape=(pltpu.SemaphoreType.DMA((9,)), pltpu.SemaphoreType.DMA((9,)), pltpu.HBM(sw.shape, sw.dtype),
                   pltpu.HBM(ss.shape, ss.dtype), pltpu.HBM((3, WSH, D // 4), F32), pltpu.HBM((3, PACK_ROWS, HW), BF16),
                   jax.ShapeDtypeStruct((8, LANE), F32)),
        in_specs=(HBM, HBM, HBM, HBM),
        out_specs=(SEM, SEM, HBM, HBM, HBM, HBM, pl.BlockSpec(memory_space=pltpu.VMEM)),
        input_output_aliases={0: 2, 1: 3, 2: 4, 3: 5},
        compiler_params=pltpu.CompilerParams(has_side_effects=EFFECT),
    )(_in_hbm(sw), _in_hbm(ss), _in_hbm(lax.empty((3, WSH, D // 4), F32)), _in_hbm(lax.empty((3, PACK_ROWS, HW), BF16)))


def _rs_exchange_wait(send_sems, recv_sems, sw, ss, r2w, r2s, after, tag):
    def body(sw_ref, ss_ref, r2w_ref, r2s_ref, send_sems, recv_sems, after_ref, sw_dead, ss_dead, r2w_out, r2s_out):
        mine, theirs = _rs_exchange_copies(sw_ref, ss_ref, r2w_ref, r2s_ref, send_sems, recv_sems)
        for cp in mine:
            cp.wait_send()
        for cp in theirs:
            cp.wait_recv()

    out = pl.pallas_call(
        body, name="grads_exchange_wait_" + tag,
        out_shape=(pltpu.HBM(sw.shape, sw.dtype), pltpu.HBM(ss.shape, ss.dtype), pltpu.HBM(r2w.shape, r2w.dtype),
                   pltpu.HBM(r2s.shape, r2s.dtype)),
        in_specs=(HBM, HBM, HBM, HBM, SEM, SEM, ANY), out_specs=(HBM, HBM, HBM, HBM),
        input_output_aliases={0: 0, 1: 1, 2: 2, 3: 3},
        compiler_params=pltpu.CompilerParams(has_side_effects=EFFECT),
    )(sw, ss, r2w, r2s, send_sems, recv_sems, after)
    return out[2], out[3]


def _rs_final_w(gw, rw, r2w, idx):
    q = D // 8

    def body(i_ref, g_ref, r_ref, p_ref, o_ref, gbuf, rbuf, sems):
        i = pl.program_id(0)
        k, c = i_ref[0], i_ref[1]
        cps = []
        for n_, (l0, p0, n) in enumerate(_piece_rows(k)):
            gcol = pl.ds(pl.multiple_of(c * (D // 2) + i * 2 * q, LANE), 2 * q)
            rcol = pl.ds(pl.multiple_of(i * 2 * q, LANE), 2 * q)
            cps.append(pltpu.make_async_copy(g_ref.at[pl.ds(p0, n), gcol], gbuf.at[pl.ds(l0, n)], sems.at[2 * n_]))
            cps.append(pltpu.make_async_copy(r_ref.at[pl.ds(p0, n), rcol], rbuf.at[pl.ds(l0, n)], sems.at[2 * n_ + 1]))
        for cp in cps:
            cp.start()
        for cp in cps:
            cp.wait()
        acc = gbuf[...] + rbuf[...]
        for j in range(3):
            lo, hi = _unpack_words(p_ref[j])
            acc = acc + jnp.concatenate([lo, hi], axis=1)
        o_ref[...] = acc

    return pl.pallas_call(
        body,
        grid_spec=pltpu.PrefetchScalarGridSpec(
            num_scalar_prefetch=1, grid=(2,),
            in_specs=[ANY, ANY, pl.BlockSpec((3, WSH, q), lambda i, ir: (0, 0, i))],
            out_specs=pl.BlockSpec((WSH, 2 * q), lambda i, ir: (0, i)),
            scratch_shapes=[pltpu.VMEM((WSH, 2 * q), F32), pltpu.VMEM((WSH, 2 * q), F32), pltpu.SemaphoreType.DMA((4,))]),
        out_shape=jax.ShapeDtypeStruct((WSH, D // 2), F32),
        name="grads_final_sum_w",
        compiler_params=pltpu.CompilerParams(dimension_semantics=("arbitrary",), vmem_limit_bytes=VMEM_LIMIT),
    )(idx, gw, rw, r2w)


def _rs_final_s(gs, rs, r2s, idx):
    def body(i_ref, g_ref, r_ref, p_ref, o_ref):
        acc = g_ref[...] + r_ref[...]
        for j in range(3):
            acc = acc + p_ref[j].astype(F32)
        o_ref[...] = acc

    return pl.pallas_call(
        body,
        grid_spec=pltpu.PrefetchScalarGridSpec(
            num_scalar_prefetch=1, grid=(1,),
            in_specs=[pl.BlockSpec((None, PACK_ROWS, HW), lambda i, ir: (ir[0], 0, ir[1])),
                      pl.BlockSpec((None, PACK_ROWS, HW), lambda i, ir: (ir[0], 0, 0)),
                      pl.BlockSpec((3, PACK_ROWS, HW), lambda i, ir: (0, 0, 0))],
            out_specs=pl.BlockSpec((PACK_ROWS, HW), lambda i, ir: (0, 0))),
        out_shape=jax.ShapeDtypeStruct((PACK_ROWS, HW), F32),
        name="grads_final_sum_s",
        compiler_params=pltpu.CompilerParams(dimension_semantics=("arbitrary",), vmem_limit_bytes=VMEM_LIMIT),
    )(idx, gs, rs, r2s)


def _rs_share(fw, fs):
    def body(w_ref, s_ref, ow_ref, os_ref, send_sems, recv_sems):
        x, y, c = _me()
        cps = [pltpu.make_async_remote_copy(src_ref=w_ref, dst_ref=ow_ref, send_sem=send_sems.at[0],
                                            recv_sem=recv_sems.at[0], device_id=(x, y, 1 - c), device_id_type=MESH),
               pltpu.make_async_remote_copy(src_ref=s_ref, dst_ref=os_ref, send_sem=send_sems.at[1],
                                            recv_sem=recv_sems.at[1], device_id=(x, y, 1 - c), device_id_type=MESH)]
        for cp in cps:
            cp.start()
        for cp in cps:
            cp.wait()

    return pl.pallas_call(
        body,
        out_shape=[jax.ShapeDtypeStruct((WSH, D // 2), F32), jax.ShapeDtypeStruct((PACK_ROWS, HW), F32)],
        in_specs=[ANY, ANY], out_specs=[ANY, ANY],
        scratch_shapes=[pltpu.SemaphoreType.DMA((2,)), pltpu.SemaphoreType.DMA((2,))],
        name="grads_share",
    )(fw, fs)


def _both_halves(mine, other, c):
    return jnp.where(c == 0, jnp.concatenate([mine, other], axis=1), jnp.concatenate([other, mine], axis=1))


def _rs_begin(gw, gs):
    x, y, c = _me()
    cidx = jnp.reshape(c, (1,)).astype(jnp.int32)
    rw, rs = _rs_swap(gw, gs)
    return dict(gw=gw, gs=gs, rw=rw, rs=rs, sw=_rs_chip_sum_w(gw, rw, cidx), ss=_rs_chip_sum_s(gs, rs, cidx))


def _rs_end(st, r2w, r2s):
    x, y, c = _me()
    idx = jnp.stack([2 * x + y, c]).astype(jnp.int32)
    fw = _rs_final_w(st["gw"], st["rw"], r2w, idx)
    fs = _rs_final_s(st["gs"], st["rs"], r2s, idx)
    ow, os_ = _rs_share(fw, fs)
    return _both_halves(fw, ow, c), _both_halves(fs, os_, c)


def _all_reduce_small(gs):
    rows = gs.shape[0]

    def body(g_ref, o_ref, buf, send_sems, recv_sems):
        x, y, c = _me()
        me = 4 * x + 2 * y + c
        buf[me] = g_ref[...]
        cps = []
        for r in range(1, 8):
            fx, fy, fc = (r >> 2) & 1, (r >> 1) & 1, r & 1
            px, py, pc = jnp.bitwise_xor(x, fx), jnp.bitwise_xor(y, fy), jnp.bitwise_xor(c, fc)
            cps.append((pltpu.make_async_remote_copy(
                src_ref=g_ref, dst_ref=buf.at[me], send_sem=send_sems.at[r - 1], recv_sem=recv_sems.at[r - 1],
                device_id=(px, py, pc), device_id_type=MESH), 4 * px + 2 * py + pc))
        for cp, _ in cps:
            cp.start()
        for r, (cp, peer) in enumerate(cps):
            pltpu.make_async_remote_copy(
                src_ref=g_ref, dst_ref=buf.at[peer], send_sem=send_sems.at[r], recv_sem=recv_sems.at[r],
                device_id=(x, y, c), device_id_type=MESH).wait_recv()
        for cp, _ in cps:
            cp.wait_send()
        acc = buf[0]
        for k in range(1, 8):
            acc = acc + buf[k]
        o_ref[...] = acc

    return pl.pallas_call(
        body,
        out_shape=jax.ShapeDtypeStruct((rows, LANE), F32),
        in_specs=[pl.BlockSpec(memory_space=pltpu.VMEM)],
        out_specs=pl.BlockSpec(memory_space=pltpu.VMEM),
        scratch_shapes=[pltpu.VMEM((8, rows, LANE), F32), pltpu.SemaphoreType.DMA((7,)), pltpu.SemaphoreType.DMA((7,))],
        name="small_grads_all_reduce",
    )(gs)


PACK_SPLIT = (("w_uq", 96, (QL, 192)), ("w_ukv", 64, (KVL, 256)),
              ("w_out_a", 256, (CW, 256)), ("w_out_b", 256, (CW, 256)), ("w_out_c", 256, (CW, 256)),
              ("w_o", 512, (256, D)))
MAT_ROWS = 1440
CONV_SHARD = 3 * 128


def _w_in_words(w_in_shard):
    t = w_in_shard.T
    return _pack_words(t[:, :CWD], t[:, CWD:])


def _pack_weights(wl):
    parts = [wl[n].astype(BF16).reshape(-1, PACK_W) for n, _, _ in PACK_SPLIT]
    cw = wl["conv_w"].reshape(-1)
    hi = cw.astype(BF16)
    r1 = cw - hi.astype(F32)
    mid = r1.astype(BF16)
    lo = (r1 - mid.astype(F32)).astype(BF16)
    cterms = jnp.pad(jnp.concatenate([hi, mid, lo]), (0, 3 * PACK_W - 3 * CONV_SHARD)).reshape(3, PACK_W)
    tail = jnp.pad(cterms, ((0, PACK_ROWS - MAT_ROWS - 3), (0, 0)))
    return jnp.concatenate(parts + [tail], axis=0)


def _unpack_weights(gath):
    out = {}
    r = 0
    for n, nrows, shp in PACK_SPLIT:
        t = gath[:, r:r + nrows].reshape((4,) + shp)
        r += nrows
        if n == "w_o":
            out[n] = t.reshape(4 * shp[0], shp[1])
        else:
            out[n] = t.transpose(1, 0, 2).reshape(shp[0], 4 * shp[1])
    ct = gath[:, r:r + 3].reshape(4, 3 * PACK_W)[:, :3 * CONV_SHARD].astype(F32).reshape(4, 3, CONV_SHARD)
    cw = (ct[:, 0] + ct[:, 1]) + ct[:, 2]
    out["conv_w"] = cw.reshape(4, 3, 128).transpose(1, 0, 2).reshape(3, CW)
    return out


def _pack_grads(g):
    parts = []
    for n, nrows, shp in PACK_SPLIT:
        t = g[n]
        if n == "w_o":
            t = t.reshape((4,) + shp)
        else:
            t = t.reshape(shp[0], 4, shp[1]).transpose(1, 0, 2)
        parts.append(t.reshape(4, nrows, PACK_W))
    cw = g["conv_w"].reshape(3, 4, 128).transpose(1, 0, 2).reshape(4, 1, CONV_SHARD)
    parts.append(jnp.pad(cw, ((0, 0), (0, PACK_ROWS - MAT_ROWS - 1), (0, PACK_W - CONV_SHARD))))
    return jnp.concatenate(parts, axis=1)


def _unpack_grads(red):
    out = {}
    r = 0
    for n, nrows, shp in PACK_SPLIT:
        out[n] = red[r:r + nrows].reshape(shp)
        r += nrows
    out["conv_w"] = red[r, :CONV_SHARD].reshape(3, 128)
    return out


SMALL_SIZES = (("norm_g", D), ("b_gate", 3 * D), ("conv_b", CW), ("q_a_norm_g", QL), ("kv_a_norm_g", KVL),
               ("mla_q_norm_g", QK), ("mla_k_norm_g", QK), ("dil_q_norm_g", NG * HD), ("dil_k_norm_g", NG * HD))
SMALL_ROWS = 88


def _pack_small(per_name):
    flat = jnp.concatenate([per_name[n].reshape(-1).astype(F32) for n, _ in SMALL_SIZES])
    return jnp.pad(flat, (0, SMALL_ROWS * LANE - flat.shape[0])).reshape(SMALL_ROWS, LANE)


def _unpack_small(packed, like):
    out = {}
    flat = packed.reshape(-1)
    r = 0
    for n, sz in SMALL_SIZES:
        out[n] = flat[r:r + NL * sz].reshape(like[n].shape)
        r += NL * sz
    return out


def _adamw_math(w, g, m, v):
    m = ADAM_B1 * m + (1.0 - ADAM_B1) * g
    v = ADAM_B2 * v + (1.0 - ADAM_B2) * jnp.square(g)
    m_hat = m / (1.0 - ADAM_B1 ** ADAM_STEP)
    v_hat = v / (1.0 - ADAM_B2 ** ADAM_STEP)
    delta = -ADAM_LR * (m_hat / (jnp.sqrt(v_hat) + ADAM_EPS) + ADAM_WD * w)
    return delta, m, v


def _adamw(name, w, g, m, v, br, bc=None):
    L, R, C = w.shape
    bc = C if bc is None else bc
    blk = lambda l, i, j: (l, i, j)
    return _pcall(name, _adamw_math, (L, R // br, C // bc), [(t, (None, br, bc), blk) for t in (w, g, m, v)],
                  [((L, R, C), F32, (None, br, bc), blk)] * 3)


ADAM_ROWS = {"w_uq": 256, "w_ukv": 128, "w_out_a": 512, "w_out_b": 512, "w_out_c": 512, "w_o": 256,
             "conv_w": 3}


def kernel(x, norm_g, w_in, b_gate, conv_w, conv_b, q_a_norm_g, w_uq, kv_a_norm_g, w_ukv, mla_q_norm_g, mla_k_norm_g, dil_q_norm_g, dil_k_norm_g, w_out_a, w_out_b, w_out_c, w_o, loss_target, m_norm_g, m_w_in, m_b_gate, m_conv_w, m_conv_b, m_q_a_norm_g, m_w_uq, m_kv_a_norm_g, m_w_ukv, m_mla_q_norm_g, m_mla_k_norm_g, m_dil_q_norm_g, m_dil_k_norm_g, m_w_out_a, m_w_out_b, m_w_out_c, m_w_o, v_norm_g, v_w_in, v_b_gate, v_conv_w, v_conv_b, v_q_a_norm_g, v_w_uq, v_kv_a_norm_g, v_w_ukv, v_mla_q_norm_g, v_mla_k_norm_g, v_dil_q_norm_g, v_dil_k_norm_g, v_w_out_a, v_w_out_b, v_w_out_c, v_w_o):
    W = dict(norm_g=norm_g, w_in=w_in, b_gate=b_gate, conv_w=conv_w, conv_b=conv_b, q_a_norm_g=q_a_norm_g, w_uq=w_uq,
             kv_a_norm_g=kv_a_norm_g, w_ukv=w_ukv, mla_q_norm_g=mla_q_norm_g, mla_k_norm_g=mla_k_norm_g,
             dil_q_norm_g=dil_q_norm_g, dil_k_norm_g=dil_k_norm_g, w_out_a=w_out_a, w_out_b=w_out_b, w_out_c=w_out_c,
             w_o=w_o)
    M = dict(norm_g=m_norm_g, w_in=m_w_in, b_gate=m_b_gate, conv_w=m_conv_w, conv_b=m_conv_b, q_a_norm_g=m_q_a_norm_g,
             w_uq=m_w_uq, kv_a_norm_g=m_kv_a_norm_g, w_ukv=m_w_ukv, mla_q_norm_g=m_mla_q_norm_g,
             mla_k_norm_g=m_mla_k_norm_g, dil_q_norm_g=m_dil_q_norm_g, dil_k_norm_g=m_dil_k_norm_g, w_out_a=m_w_out_a,
             w_out_b=m_w_out_b, w_out_c=m_w_out_c, w_o=m_w_o)
    V = dict(norm_g=v_norm_g, w_in=v_w_in, b_gate=v_b_gate, conv_w=v_conv_w, conv_b=v_conv_b, q_a_norm_g=v_q_a_norm_g,
             w_uq=v_w_uq, kv_a_norm_g=v_kv_a_norm_g, w_ukv=v_w_ukv, mla_q_norm_g=v_mla_q_norm_g,
             mla_k_norm_g=v_mla_k_norm_g, dil_q_norm_g=v_dil_q_norm_g, dil_k_norm_g=v_dil_k_norm_g, w_out_a=v_w_out_a,
             w_out_b=v_w_out_b, w_out_c=v_w_out_c, w_o=v_w_o)
    batch = x.shape[0]
    T = batch * S

    def layer_weights(l, cont, gath):
        full = _unpack_weights(gath)
        pad_qk = lambda t: jnp.pad(t, (0, QKP - QK)).reshape(1, QKP)
        full.update(
            w_in_t=_unpack_w_in(cont),
            norm_g=norm_g[l].reshape(1, D), b_gate=b_gate[l].reshape(1, 3 * D), conv_b=conv_b[l].reshape(1, CW),
            q_a_norm_g=q_a_norm_g[l].reshape(1, QL), kv_a_norm_g=kv_a_norm_g[l].reshape(1, KVL),
            mla_q_norm_g=pad_qk(mla_q_norm_g[l]), mla_k_norm_g=pad_qk(mla_k_norm_g[l]),
            dil_q_norm_g=dil_q_norm_g[l].reshape(NG, 1, HD), dil_k_norm_g=dil_k_norm_g[l].reshape(NG, 1, HD))
        return full

    words = [_w_in_words(w_in[l]) for l in range(NL)]
    packs = [_pack_weights({n: W[n][l] for n in BIG[1:] + ("conv_w",)}) for l in range(NL)]
    tabs = _rope_tables() + (_dil_slopes(),)
    x2 = x.reshape(T, D)

    cont0, gath0 = _all_gather(words[0], packs[0])
    w0 = layer_weights(0, cont0, gath0)
    ag = _ag_ici_start(words[1], packs[1], gath0)
    w0["norm_g"] = w0["norm_g"] + ag[6][0:1, 0:1]
    y0, res0 = _layer_fwd(x2, w0, tabs, batch)
    lw, ls = _ag_ici_wait(ag[0], ag[1], ag[2], ag[3], ag[4], ag[5], y0)
    w1 = layer_weights(1, *_ag_finish(words[1], packs[1], lw, ls))
    y1, res1 = _layer_fwd(y0, w1, tabs, batch)

    row = lambda i: (i, 0)
    dy, loss = _pcall("loss", _loss_math, (T // BR,),
                      [(y1, (BR, D), row), (loss_target.reshape(T, D), (BR, D), row)],
                      [((T, D), F32, (BR, D), row), ((1, 1), F32, (1, 1), lambda i: (0, 0), True)])
    loss = lax.psum(loss[0, 0], ("x", "y", "c"))

    grads = [None] * NL
    dy, grads[1] = _layer_bwd(dy, w1, res1, tabs, batch)
    st = [None] * NL
    ex = [None] * NL
    st[1] = _rs_begin(grads[1]["w_in_t"], _pack_grads(grads[1]))
    ex[1] = _rs_exchange_start(st[1]["sw"], st[1]["ss"], "1")
    w0["w_o"] = w0["w_o"] + ex[1][6][0:1, 0:1].astype(BF16)

    def start_layer0(g):
        st[0] = _rs_begin(g["w_in_t"], _pack_grads(g))
        ex[0] = _rs_exchange_start(st[0]["sw"], st[0]["ss"], "0")
        return ex[0][6]

    dx, grads[0] = _layer_bwd(dy, w0, res0, tabs, batch, after_dw=start_layer0)
    grad_x = dx.reshape(batch, S, D)

    red = [None] * NL
    for l in (1, 0):
        r2w, r2s = _rs_exchange_wait(*ex[l][:6], dx, str(l))
        rw, rs = _rs_end(st[l], r2w, r2s)
        r = _unpack_grads(rs)
        r["w_in_t"] = rw
        red[l] = r
    G = {n: jnp.stack([red[l][n] for l in range(NL)]) for n in BIG[1:] + ("conv_w",)}
    g_in_t = jnp.stack([red[l]["w_in_t"] for l in range(NL)])
    G["w_in"] = jnp.swapaxes(g_in_t, 1, 2)
    small_g = {n: jnp.stack([grads[l][n].reshape(-1)[:sz] for l in range(NL)]) for n, sz in SMALL_SIZES}
    small_red = _all_reduce_small(_pack_small(small_g))
    G.update(_unpack_small(small_red, {n: W[n] for n in SMALL}))

    delta, new_m, new_v = {}, {}, {}
    for n in BIG[1:] + ("conv_w",):
        delta[n], new_m[n], new_v[n] = _adamw("adamw_" + n, W[n], G[n], M[n], V[n], ADAM_ROWS[n])
    tr = lambda t: jnp.swapaxes(t, 1, 2)
    delta["w_in"], new_m["w_in"], new_v["w_in"] = (
        tr(t) for t in _adamw("adamw_w_in", tr(w_in), g_in_t, tr(m_w_in), tr(v_w_in), WSH, LANE))
    sw, sm, sv = (_pack_small({n: t[n] for n in SMALL})[None] for t in (W, M, V))
    sd, snm, snv = _adamw("adamw_small", sw, small_red[None], sm, sv, SMALL_ROWS)
    like = {n: W[n] for n in SMALL}
    delta.update(_unpack_small(sd[0], like))
    new_m.update(_unpack_small(snm[0], like))
    new_v.update(_unpack_small(snv[0], like))

    return (loss, grad_x, *[G[n] for n in WEIGHTS], *[delta[n] for n in WEIGHTS],
            *[new_m[n] for n in WEIGHTS], *[new_v[n] for n in WEIGHTS])
```

```python
import functools

import numpy as np
import jax
import jax.numpy as jnp
from jax import lax
from jax.experimental import pallas as pl
from jax.experimental.pallas import tpu as pltpu

F32 = jnp.float32
BF16 = jnp.bfloat16

D = 1024
S = 2048
NL = 2
CW = 512
NH = 8
QL = 256
KVL = 128
NOPE = 64
ROPE = 32
VD = 64
QK = NOPE + ROPE
QKP = 128
ROPE_THETA = 10000.0
DIL = ((128, 1), (512, 4), (2048, 16))
NG = 3
DH = 8
HD = 64
DWID = DH * HD
QB = 128
EPS = 1e-6
NIN = 11168
NINP = 11264
O_A, O_CQ, O_CKV, O_KPE, O_BZ, O_DQ, O_DK, O_DV, O_CZ, O_G = 0, 2048, 2304, 2432, 2560, 3072, 4608, 6144, 7680, 8192
KPE_END = 2464
NEG = -1e30
MLA_SCALE = QK ** -0.5
DIL_SCALE = HD ** -0.5
LANE = 128
PACK_W = 512
VMEM_LIMIT = 48 * 1024 * 1024

ADAM_LR = 0.001
ADAM_B1 = 0.9
ADAM_B2 = 0.999
ADAM_EPS = 1e-08
ADAM_WD = 0.01
ADAM_STEP = 10

MESH = pl.DeviceIdType.MESH
BIG = ("w_in", "w_uq", "w_ukv", "w_out_a", "w_out_b", "w_out_c", "w_o")
SMALL = ("norm_g", "b_gate", "conv_b", "q_a_norm_g", "kv_a_norm_g", "mla_q_norm_g", "mla_k_norm_g",
         "dil_q_norm_g", "dil_k_norm_g")
WEIGHTS = ("norm_g", "w_in", "b_gate", "conv_w", "conv_b", "q_a_norm_g", "w_uq", "kv_a_norm_g", "w_ukv",
           "mla_q_norm_g", "mla_k_norm_g", "dil_q_norm_g", "dil_k_norm_g", "w_out_a", "w_out_b", "w_out_c", "w_o")


def _dot(a, b):
    return jnp.dot(a, b, preferred_element_type=F32)


def _dot_nt(a, b):
    return lax.dot_general(a, b, (((1,), (1,)), ((), ())), preferred_element_type=F32)


def _dot_tn(a, b):
    return lax.dot_general(a, b, (((0,), (0,)), ((), ())), preferred_element_type=F32)


def _pcall(name, fn, grid, ins, outs):
    n_in = len(ins)
    n_out = len(outs)
    acc_axis = len(grid) - 1
    is_acc = [len(o) > 4 and o[4] for o in outs]
    outs = [o[:4] for o in outs]

    def body(*refs):
        vals = fn(*[r[...].astype(F32) for r in refs[:n_in]])
        if not isinstance(vals, (tuple, list)):
            vals = (vals,)
        for k in range(n_out):
            r = refs[n_in + k]
            v = vals[k].astype(r.dtype).reshape(r.shape)
            if is_acc[k]:
                first = pl.program_id(acc_axis) == 0

                @pl.when(first)
                def _():
                    r[...] = v

                @pl.when(jnp.logical_not(first))
                def _():
                    r[...] += v
            else:
                r[...] = v

    return pl.pallas_call(
        body,
        grid=grid,
        in_specs=[pl.BlockSpec(bs, im) for _, bs, im in ins],
        out_specs=[pl.BlockSpec(bs, im) for _, _, bs, im in outs],
        out_shape=[jax.ShapeDtypeStruct(sh, dt) for sh, dt, _, _ in outs],
        name=name,
        compiler_params=pltpu.CompilerParams(
            dimension_semantics=("arbitrary",) * len(grid), vmem_limit_bytes=VMEM_LIMIT),
    )(*[a for a, _, _ in ins])


def _mm(name, a, b, *, ta=False, tb=False, out_dtype=F32, add=None, dep=None, tm=512, tn=1024, tk=1024):
    if ta:
        K, M = a.shape
    else:
        M, K = a.shape
    if tb:
        N, K2 = b.shape
    else:
        K2, N = b.shape
    assert K == K2, (name, a.shape, b.shape)
    tm, tn, tk = min(tm, M), min(tn, N), min(tk, K)
    assert M % tm == 0 and N % tn == 0 and K % tk == 0, (name, M, N, K)
    nk = K // tk
    dims = (((0 if ta else 1,), (1 if tb else 0,)), ((), ()))
    a_spec = pl.BlockSpec((tk, tm), lambda j, i, k: (k, i)) if ta else pl.BlockSpec((tm, tk), lambda j, i, k: (i, k))
    b_spec = pl.BlockSpec((tn, tk), lambda j, i, k: (j, k)) if tb else pl.BlockSpec((tk, tn), lambda j, i, k: (k, j))
    o_spec = pl.BlockSpec((tm, tn), lambda j, i, k: (i, j))
    has_add = add is not None
    n_in = 2 + has_add + (dep is not None)

    def body(*refs):
        a_ref, b_ref = refs[0], refs[1]
        add_ref = refs[2] if has_add else None
        o_ref = refs[n_in]
        p = lax.dot_general(a_ref[...].astype(BF16), b_ref[...].astype(BF16), dims, preferred_element_type=F32)
        if nk == 1:
            if has_add:
                p = p + add_ref[...]
            o_ref[...] = p.astype(out_dtype)
        else:
            acc = refs[-1]
            k = pl.program_id(2)

            @pl.when(k == 0)
            def _():
                acc[...] = p

            @pl.when(k > 0)
            def _():
                acc[...] += p

            @pl.when(k == nk - 1)
            def _():
                r = acc[...]
                if has_add:
                    r = r + add_ref[...]
                o_ref[...] = r.astype(out_dtype)

    in_specs = [a_spec, b_spec] + ([o_spec] if has_add else []) + ([pl.BlockSpec(memory_space=pl.ANY)] if dep is not None else [])
    args = [a, b] + ([add] if has_add else []) + ([dep] if dep is not None else [])
    return pl.pallas_call(
        body,
        grid=(N // tn, M // tm, nk),
        in_specs=in_specs,
        out_specs=o_spec,
        out_shape=jax.ShapeDtypeStruct((M, N), out_dtype),
        scratch_shapes=[pltpu.VMEM((tm, tn), F32)] if nk > 1 else [],
        name=name,
        compiler_params=pltpu.CompilerParams(
            dimension_semantics=("arbitrary", "arbitrary", "arbitrary"), vmem_limit_bytes=VMEM_LIMIT),
    )(*args)


def _vjp_of(f, n_diff):
    def g(*args, n_prim):
        prim = args[:n_diff]
        consts = args[n_diff:n_prim]
        cts = args[n_prim:]
        _, pull = jax.vjp(lambda *p: f(*p, *consts), *prim)
        out = jax.eval_shape(lambda *p: f(*p, *consts), *prim)
        if isinstance(out, (tuple, list)):
            cts = tuple(c.astype(o.dtype) for c, o in zip(cts, out))
        else:
            cts = cts[0].astype(out.dtype)
        return pull(cts)
    return g


def _rms(x, g, n=None):
    n = x.shape[-1] if n is None else n
    ms = jnp.sum(x * x, axis=-1, keepdims=True) / n
    return x * lax.rsqrt(ms + EPS) * g


def _silu(z):
    return z * jax.nn.sigmoid(z)


def _roll_rows(u, k):
    n = u.shape[0]
    r = pltpu.roll(u, k % n, 0)
    t = lax.broadcasted_iota(jnp.int32, u.shape, 0)
    if k > 0:
        return jnp.where(t >= k, r, 0.0)
    return jnp.where(t < n + k, r, 0.0)


@functools.partial(jax.custom_vjp, nondiff_argnums=(1,))
def _shift(u, k):
    return _roll_rows(u, k)


def _shift_fwd(u, k):
    return _roll_rows(u, k), None


def _shift_bwd(k, _, g):
    return (_roll_rows(g, -k),)


_shift.defvjp(_shift_fwd, _shift_bwd)


@functools.partial(jax.custom_vjp, nondiff_argnums=(1,))
def _lane_roll(u, k):
    return pltpu.roll(u, k % LANE, 1)


def _lane_roll_fwd(u, k):
    return pltpu.roll(u, k % LANE, 1), None


def _lane_roll_bwd(k, _, g):
    return (pltpu.roll(g, (-k) % LANE, 1),)


_lane_roll.defvjp(_lane_roll_fwd, _lane_roll_bwd)


def _conv_math(ab, ac, ax, az, cw, cb):
    u = ac * ax
    conv = cb + _shift(u, 2) * cw[0:1] + _shift(u, 1) * cw[1:2] + u * cw[2:3]
    return ab * conv * _silu(az)


def _mla_pre_math(cq, ckv, gq, gkv):
    return _rms(cq, gq), _rms(ckv, gkv)


def _rope_math(q, kn, kpe, gq, gk, c, s1, s2):
    lane = lax.broadcasted_iota(jnp.int32, kpe.shape, 1)
    pe = _lane_roll(jnp.where(lane < ROPE, kpe, 0.0), NOPE)

    def one(t, g):
        tn = _rms(t, g, QK)
        return tn * c + _lane_roll(tn, -16) * s1 + _lane_roll(tn, 16) * s2

    qs, ks = [], []
    for h in range(NH):
        sl = slice(h * QKP, (h + 1) * QKP)
        qs.append(one(q[:, sl], gq))
        ks.append(one(kn[:, sl] + pe, gk))
    return jnp.concatenate(qs, axis=1), jnp.concatenate(ks, axis=1)


def _gate_math(o, z):
    return o * _silu(z)


def _mergec_math(o0, o1, o2, l0, l1, l2, cz):
    m = lax.stop_gradient(jnp.maximum(jnp.maximum(l0, l1), l2))
    e0, e1, e2 = jnp.exp(l0 - m), jnp.exp(l1 - m), jnp.exp(l2 - m)
    den = e0 + e1 + e2
    oc = (e0 / den) * o0 + (e1 / den) * o1 + (e2 / den) * o2
    return oc * _silu(cz)


def _merge_math(g0, g1, g2, b0, b1, b2, pa, pb, pc):
    return (jax.nn.sigmoid(g0 + b0) * pa + jax.nn.sigmoid(g1 + b1) * pb) + jax.nn.sigmoid(g2 + b2) * pc


MLA_T = 256
MLA_UNROLL = True


def _mla_fwd(q, k, v):
    B = q.shape[0]
    T = MLA_T
    NB = S // T

    def body(q_ref, k_ref, v_ref, o_ref, l_ref):
        row = lax.broadcasted_iota(jnp.int32, (T, T), 0)
        col = lax.broadcasted_iota(jnp.int32, (T, T), 1)
        lo = _lo_mask((T, LANE))

        for qi in range(NB):
            qb = q_ref[qi * T:(qi + 1) * T, :]

            def step(j, carry, diagonal):
                m, l, acc = carry
                off = pl.multiple_of(j * T, T)
                kb = k_ref[pl.ds(off, T), :]
                vb = v_ref[pl.ds(off, T), :]
                ss = []
                for e in (0, 1):
                    se = _dot_nt(qb[:, e * QKP:(e + 1) * QKP], kb[:, e * QKP:(e + 1) * QKP]) * MLA_SCALE
                    ss.append(jnp.where(col <= row, se, NEG) if diagonal else se)
                s = jnp.concatenate(ss, axis=0)
                m_new = jnp.maximum(m, jnp.max(s, axis=-1, keepdims=True))
                a = jnp.exp(m - m_new)
                p = jnp.exp(s - m_new)
                l = a * l + jnp.sum(p, axis=-1, keepdims=True)
                acc = a * acc + _dot(p.astype(BF16), vb)
                return m_new, l, acc

            init = (jnp.full((2 * T, 1), NEG, F32), jnp.zeros((2 * T, 1), F32), jnp.zeros((2 * T, LANE), F32))
            carry = lax.fori_loop(0, qi, functools.partial(step, diagonal=False), init, unroll=MLA_UNROLL)
            m, l, acc = step(qi, carry, True)
            o = acc / l
            lse = m + jnp.log(l)
            o_ref[qi * T:(qi + 1) * T, :] = jnp.where(lo, o[:T], o[T:])
            l_ref[qi * T:(qi + 1) * T, :] = jnp.where(lo, lse[:T], lse[T:])

    def spec(w):
        return pl.BlockSpec((None, S, w), lambda b, hp: (b, 0, hp))

    return pl.pallas_call(
        body,
        grid=(B, NH // 2),
        in_specs=[spec(2 * QKP), spec(2 * QKP), spec(LANE)],
        out_specs=[spec(LANE), spec(LANE)],
        out_shape=[jax.ShapeDtypeStruct((B, S, NH * VD), F32)] * 2,
        name="mla_attn_fwd",
        compiler_params=pltpu.CompilerParams(dimension_semantics=("arbitrary",) * 2, vmem_limit_bytes=VMEM_LIMIT),
    )(q, k, v)


def _mla_bwd(q, k, v, do, o, lse):
    B = q.shape[0]
    T = MLA_T
    NB = S // T

    def body(q_ref, k_ref, v_ref, do_ref, o_ref, l_ref, dq_ref, dk_ref, dv_ref, delta_ref):
        delta_ref[...] = _head_sum(do_ref[...] * o_ref[...])
        row = lax.broadcasted_iota(jnp.int32, (T, T), 0)
        col = lax.broadcasted_iota(jnp.int32, (T, T), 1)
        lo = _lo_mask((T, LANE))

        for j in range(NB):
            krows = slice(j * T, (j + 1) * T)
            kb = k_ref[krows, :]
            vb = v_ref[krows, :]
            dk = [jnp.zeros((T, QKP), F32), jnp.zeros((T, QKP), F32)]
            dv = jnp.zeros((T, LANE), F32)
            for i in range(j, NB):
                qrows = slice(i * T, (i + 1) * T)
                qb = q_ref[qrows, :]
                do2 = _stack_heads(do_ref[qrows, :], lo).astype(BF16)
                lb = l_ref[qrows, :]
                db = delta_ref[qrows, :]
                dp2 = _dot_nt(do2, vb)
                for e in (0, 1):
                    cols = slice(e * QKP, (e + 1) * QKP)
                    qe, ke = qb[:, cols], kb[:, cols]
                    s = _dot_nt(qe, ke) * MLA_SCALE
                    if i == j:
                        s = jnp.where(col <= row, s, NEG)
                    p = jnp.exp(s - lb[:, e * HD:e * HD + 1])
                    dv = dv + _dot_tn(p.astype(BF16), do2[e * T:(e + 1) * T])
                    ds = (p * (dp2[e * T:(e + 1) * T] - db[:, e * HD:e * HD + 1]) * MLA_SCALE).astype(BF16)
                    dk[e] = dk[e] + _dot_tn(ds, qe)
                    if j == 0:
                        dq_ref[qrows, cols] = _dot(ds, ke)
                    else:
                        dq_ref[qrows, cols] += _dot(ds, ke)
            dk_ref[krows, 0:QKP] = dk[0]
            dk_ref[krows, QKP:2 * QKP] = dk[1]
            dv_ref[krows, :] = dv

    def spec(w):
        return pl.BlockSpec((None, S, w), lambda b, hp: (b, 0, hp))

    return pl.pallas_call(
        body,
        grid=(B, NH // 2),
        in_specs=[spec(2 * QKP), spec(2 * QKP), spec(LANE), spec(LANE), spec(LANE), spec(LANE)],
        out_specs=[spec(2 * QKP), spec(2 * QKP), spec(LANE)],
        out_shape=[jax.ShapeDtypeStruct((B, S, NH * QKP), F32), jax.ShapeDtypeStruct((B, S, NH * QKP), F32),
                   jax.ShapeDtypeStruct((B, S, NH * VD), F32)],
        scratch_shapes=[pltpu.VMEM((S, LANE), F32)],
        name="mla_attn_bwd",
        compiler_params=pltpu.CompilerParams(dimension_semantics=("arbitrary",) * 2, vmem_limit_bytes=VMEM_LIMIT),
    )(q, k, v, do, o, lse)


def _lo_mask(shape):
    return lax.broadcasted_iota(jnp.int32, shape, len(shape) - 1) < HD


def _head_sum(u):
    r = lax.broadcasted_iota(jnp.int32, (LANE, LANE), 0) < HD
    c = lax.broadcasted_iota(jnp.int32, (LANE, LANE), 1) < HD
    ones = jnp.where(r == c, 1.0, 0.0).astype(BF16)
    hi = u.astype(BF16)
    lo = (u - hi.astype(F32)).astype(BF16)
    return _dot(hi, ones) + _dot(lo, ones)


def _rms2(x, g):
    return x * lax.rsqrt(_head_sum(x * x) / HD + EPS) * g


def _dil_bias(t_ref, gi, d):
    qq = lax.broadcasted_iota(jnp.int32, (QB, QB), 0)
    kk = lax.broadcasted_iota(jnp.int32, (QB, QB), 1)
    jc = (qq - kk).astype(F32)
    rows = []
    for e in (0, 1):
        sl = t_ref[2 * gi + e:2 * gi + e + 1, :] * float(d)
        bp = jnp.where(kk >= qq, -sl * (jc + float(QB)), NEG)
        bc = jnp.where(kk <= qq, -sl * jc, NEG)
        rows.append(jnp.concatenate([bp, bc], axis=1))
    return jnp.concatenate(rows, axis=0)


def _dil_rows(cur, d):
    return pl.ds(cur, QB, stride=d) if d > 1 else pl.ds(pl.multiple_of(cur, QB), QB)


def _dil_walk(d, block, full):
    if d == 1:
        block(0, None)

        def body(i, c):
            block(i * QB, (i - 1) * QB)
            return c
        lax.fori_loop(1, S // QB, body, 0, unroll=True if full else 5)
    elif d == 16:
        def body(r, c):
            block(r, None)
            return c
        lax.fori_loop(0, d, body, 0, unroll=True if full else 4)
    else:
        nb = S // d // QB

        def cls(r, c):
            block(r, None)

            def body(i, c2):
                block(r + i * QB * d, r + (i - 1) * QB * d)
                return c2
            lax.fori_loop(1, nb, body, 0, unroll=True)
            return c
        lax.fori_loop(0, d, cls, 0, unroll=full)


def _stack_heads(x, lo):
    return jnp.concatenate([jnp.where(lo, x, 0.0), jnp.where(lo, 0.0, x)], axis=0)


def _dilc_fwd(proj3, gq, gk, tab):
    B = proj3.shape[0]

    def body(q_ref, k_ref, v_ref, cz_ref, gq_ref, gk_ref, t_ref, y_ref, o_ref, l_ref, qs, ks, vs):
        g = pl.program_id(2)
        lo = _lo_mask((QB, LANE))

        def group(gi):
            d = DIL[gi][1]
            qs[...] = _rms2(q_ref[...].astype(F32), gq_ref[gi:gi + 1, :])
            ks[...] = _rms2(k_ref[...].astype(F32), gk_ref[gi:gi + 1, :])
            vs[...] = v_ref[...].astype(F32)
            bias = _dil_bias(t_ref, gi, d)

            def block(cur, prev):
                rows = _dil_rows(cur, d)
                q2 = _stack_heads(qs[rows, :], lo).astype(BF16)
                kc, vc = ks[rows, :], vs[rows, :]
                if prev is None:
                    kcat, vcat, b = kc, vc, bias[:, QB:]
                else:
                    prow = _dil_rows(prev, d)
                    kcat = jnp.concatenate([ks[prow, :], kc], axis=0)
                    vcat = jnp.concatenate([vs[prow, :], vc], axis=0)
                    b = bias
                s = _dot_nt(q2, kcat.astype(BF16)) * DIL_SCALE + b
                m = jnp.max(s, axis=-1, keepdims=True)
                p = jnp.exp(s - m)
                l = jnp.sum(p, axis=-1, keepdims=True)
                o = _dot(p.astype(BF16), vcat.astype(BF16)) / l
                lse = m + jnp.log(l)
                o_ref[gi, rows, :] = jnp.where(lo, o[:QB], o[QB:])
                l_ref[gi, rows, :] = jnp.where(lo, lse[:QB], lse[QB:])

            _dil_walk(d, block, True)

        for gi in range(NG):
            pl.when(g == gi)(functools.partial(group, gi))

        @pl.when(g == NG - 1)
        def _():
            y_ref[...] = _mergec_math(o_ref[0], o_ref[1], o_ref[2], l_ref[0], l_ref[1], l_ref[2],
                                      cz_ref[...].astype(F32)).astype(BF16)

    def col(base):
        return pl.BlockSpec((None, S, LANE), lambda b, hp, g: (b, 0, base // LANE + 4 * g + hp))

    gspec = pl.BlockSpec((NG, LANE), lambda b, hp, g: (0, 0))
    saved = pl.BlockSpec((NG, None, S, LANE), lambda b, hp, g: (0, b, 0, hp))
    return pl.pallas_call(
        body,
        grid=(B, 4, NG),
        in_specs=[col(O_DQ), col(O_DK), col(O_DV),
                  pl.BlockSpec((None, S, LANE), lambda b, hp, g: (b, 0, O_CZ // LANE + hp)),
                  gspec, gspec, pl.BlockSpec((None, 8, LANE), lambda b, hp, g: (hp, 0, 0))],
        out_specs=[pl.BlockSpec((None, S, LANE), lambda b, hp, g: (b, 0, hp)), saved, saved],
        out_shape=[jax.ShapeDtypeStruct((B, S, DWID), BF16), jax.ShapeDtypeStruct((NG, B, S, DWID), F32),
                   jax.ShapeDtypeStruct((NG, B, S, DWID), F32)],
        scratch_shapes=[pltpu.VMEM((S, LANE), F32)] * 3,
        name="dil_mixer_fwd",
        compiler_params=pltpu.CompilerParams(dimension_semantics=("arbitrary",) * 3, vmem_limit_bytes=VMEM_LIMIT),
    )(proj3, proj3, proj3, proj3, gq, gk, tab)


def _dilc_bwd(proj3, gq, gk, tab, o_all, l_all, d_yc):
    B = proj3.shape[0]

    def body(q_ref, k_ref, v_ref, cz_ref, gq_ref, gk_ref, t_ref, o_ref, l_ref, dy_ref,
             dq_out, dk_out, dv_out, dcz_out, dgq_out, dgk_out, qs, ks, vs, dos, dls, dqs, dks, dvs):
        g = pl.program_id(2)
        lo = _lo_mask((QB, LANE))

        @pl.when(jnp.logical_and(jnp.logical_and(pl.program_id(0) == 0, pl.program_id(1) == 0), g == 0))
        def _():
            dgq_out[...] = jnp.zeros((NG, LANE), F32)
            dgk_out[...] = jnp.zeros((NG, LANE), F32)

        def group(gi):
            d = DIL[gi][1]
            ls = [l_ref[j] for j in range(NG)]
            m = jnp.maximum(jnp.maximum(ls[0], ls[1]), ls[2])
            es = [jnp.exp(t - m) for t in ls]
            den = (es[0] + es[1]) + es[2]
            al = [e / den for e in es]
            os_ = [o_ref[j] for j in range(NG)]
            oc = (al[0] * os_[0] + al[1] * os_[1]) + al[2] * os_[2]
            cz = cz_ref[...].astype(F32)
            sg = jax.nn.sigmoid(cz)
            dy = dy_ref[...]
            d_oc = dy * (cz * sg)
            dcz_out[...] = (dy * oc * (sg * (1.0 + cz * (1.0 - sg)))).astype(BF16)
            ts = [_head_sum(d_oc * os_[j]) for j in range(NG)]
            tbar = (al[0] * ts[0] + al[1] * ts[1]) + al[2] * ts[2]
            dos[...] = al[gi] * d_oc
            dls[...] = al[gi] * (ts[gi] - tbar)

            qs[...] = _rms2(q_ref[...].astype(F32), gq_ref[gi:gi + 1, :])
            ks[...] = _rms2(k_ref[...].astype(F32), gk_ref[gi:gi + 1, :])
            vs[...] = v_ref[...].astype(F32)
            dks[...] = jnp.zeros((S, LANE), F32)
            dvs[...] = jnp.zeros((S, LANE), F32)
            bias = _dil_bias(t_ref, gi, d)

            def block(cur, prev):
                rows = _dil_rows(cur, d)
                q2 = _stack_heads(qs[rows, :], lo).astype(BF16)
                dob = dos[rows, :]
                do2 = _stack_heads(dob, lo).astype(BF16)
                kc, vc = ks[rows, :], vs[rows, :]
                if prev is None:
                    kcat, vcat, b = kc, vc, bias[:, QB:]
                else:
                    prow = _dil_rows(prev, d)
                    kcat = jnp.concatenate([ks[prow, :], kc], axis=0)
                    vcat = jnp.concatenate([vs[prow, :], vc], axis=0)
                    b = bias
                kcat = kcat.astype(BF16)
                vcat = vcat.astype(BF16)
                lse_b = l_ref[gi, rows, :]
                corr_b = dls[rows, :] - _head_sum(dob * o_ref[gi, rows, :])
                lse2 = jnp.concatenate([lse_b[:, 0:1], lse_b[:, HD:HD + 1]], axis=0)
                corr2 = jnp.concatenate([corr_b[:, 0:1], corr_b[:, HD:HD + 1]], axis=0)
                s = _dot_nt(q2, kcat) * DIL_SCALE + b
                p = jnp.exp(s - lse2)
                ds = (p * (_dot_nt(do2, vcat) + corr2) * DIL_SCALE).astype(BF16)
                dq2 = _dot(ds, kcat)
                dqs[rows, :] = jnp.where(lo, dq2[:QB], dq2[QB:])
                dk = _dot_tn(ds, q2)
                dv = _dot_tn(p.astype(BF16), do2)
                if prev is None:
                    dks[rows, :] += dk
                    dvs[rows, :] += dv
                else:
                    dks[prow, :] += dk[:QB]
                    dvs[prow, :] += dv[:QB]
                    dks[rows, :] += dk[QB:]
                    dvs[rows, :] += dv[QB:]

            _dil_walk(d, block, False)

            _, pull_q = jax.vjp(_rms2, q_ref[...].astype(F32), gq_ref[gi:gi + 1, :])
            dxq, dgq = pull_q(dqs[...])
            dq_out[...] = dxq.astype(BF16)
            dgq_out[gi:gi + 1, :] += dgq
            _, pull_k = jax.vjp(_rms2, k_ref[...].astype(F32), gk_ref[gi:gi + 1, :])
            dxk, dgk = pull_k(dks[...])
            dk_out[...] = dxk.astype(BF16)
            dgk_out[gi:gi + 1, :] += dgk
            dv_out[...] = dvs[...].astype(BF16)

        for gi in range(NG):
            pl.when(g == gi)(functools.partial(group, gi))

    def col(base):
        return pl.BlockSpec((None, S, LANE), lambda b, hp, g: (b, 0, base // LANE + 4 * g + hp))

    gspec = pl.BlockSpec((NG, LANE), lambda b, hp, g: (0, 0))
    saved = pl.BlockSpec((NG, None, S, LANE), lambda b, hp, g: (0, b, 0, hp))
    per_pair = pl.BlockSpec((None, S, LANE), lambda b, hp, g: (b, 0, hp))
    dcol = pl.BlockSpec((None, S, LANE), lambda b, hp, g: (b, 0, 4 * g + hp))
    return pl.pallas_call(
        body,
        grid=(B, 4, NG),
        in_specs=[col(O_DQ), col(O_DK), col(O_DV),
                  pl.BlockSpec((None, S, LANE), lambda b, hp, g: (b, 0, O_CZ // LANE + hp)),
                  gspec, gspec, pl.BlockSpec((None, 8, LANE), lambda b, hp, g: (hp, 0, 0)),
                  saved, saved, per_pair],
        out_specs=[dcol, dcol, dcol, per_pair, gspec, gspec],
        out_shape=[jax.ShapeDtypeStruct((B, S, NG * DWID), BF16)] * 3
        + [jax.ShapeDtypeStruct((B, S, DWID), BF16), jax.ShapeDtypeStruct((NG, LANE), F32),
           jax.ShapeDtypeStruct((NG, LANE), F32)],
        scratch_shapes=[pltpu.VMEM((S, LANE), F32)] * 8,
        name="dil_mixer_bwd",
        compiler_params=pltpu.CompilerParams(dimension_semantics=("arbitrary",) * 3, vmem_limit_bytes=VMEM_LIMIT),
    )(proj3, proj3, proj3, proj3, gq, gk, tab, o_all, l_all, d_yc)


def _dil_slopes():
    slopes = (2.0 ** (-8.0 * np.arange(1, NG * DH + 1, dtype=np.float32) / (NG * DH))).astype(np.float32).reshape(NG, DH)
    tab = np.zeros((4, 8, LANE), np.float32)
    for hp in range(4):
        for gi in range(NG):
            for e in (0, 1):
                tab[hp, 2 * gi + e, :] = slopes[gi, 2 * hp + e]
    return jnp.asarray(tab)


def _rope_tables():
    inv = ROPE_THETA ** (-jnp.arange(0, ROPE, 2, dtype=F32) / ROPE)
    ang = jnp.arange(S, dtype=F32)[:, None] * inv[None, :]
    cos, sin = jnp.cos(ang), jnp.sin(ang)
    z16 = jnp.zeros((S, 16), F32)
    c = jnp.concatenate([jnp.ones((S, NOPE), F32), cos, cos, jnp.zeros((S, 32), F32)], axis=1)
    s1 = jnp.concatenate([jnp.zeros((S, NOPE), F32), -sin, z16, jnp.zeros((S, 32), F32)], axis=1)
    s2 = jnp.concatenate([jnp.zeros((S, NOPE), F32), z16, sin, jnp.zeros((S, 32), F32)], axis=1)
    return c, s1, s2


def _pad_heads_uq(w):
    return jnp.pad(w.reshape(QL, NH, QK), ((0, 0), (0, 0), (0, QKP - QK))).reshape(QL, NH * QKP)


def _unpad_heads_uq(g):
    return g.reshape(QL, NH, QKP)[:, :, :QK].reshape(QL, NH * QK)


def _split_ukv(w):
    w3 = w.reshape(KVL, NH, NOPE + VD)
    uk = jnp.pad(w3[:, :, :NOPE], ((0, 0), (0, 0), (0, QKP - NOPE))).reshape(KVL, NH * QKP)
    return uk, w3[:, :, NOPE:].reshape(KVL, NH * VD)


def _join_ukv(guk, guv):
    return jnp.concatenate([guk.reshape(KVL, NH, QKP)[:, :, :NOPE], guv.reshape(KVL, NH, VD)],
                           axis=-1).reshape(KVL, NH * (NOPE + VD))


BR = 512
BRM = 256


def _layer_fwd(x, w, tabs, batch):
    T = batch * S
    rope_c, rope_s1, rope_s2, dil_tab = tabs
    res = {"x": x}
    row = lambda c: (lambda i: (i, c))
    fix = lambda i: (0, 0)

    h = _pcall("norm_fwd", _rms, (T // BR,),
               [(x, (BR, D), row(0)), (w["norm_g"], (1, D), fix)],
               [((T, D), BF16, (BR, D), row(0))])[0]
    proj = _mm("in_proj", h, w["w_in_t"], tb=True, out_dtype=BF16, tm=2048, tn=1024)
    res["h"], res["proj"] = h, proj
    proj3 = proj.reshape(batch, S, NINP)

    cblk = lambda s: (lambda j, b: (b, 0, 4 * s + j))
    y_a = _pcall("conv_fwd", _conv_math, (4, batch),
                 [(proj3, (None, S, LANE), cblk(0)), (proj3, (None, S, LANE), cblk(1)),
                  (proj3, (None, S, LANE), cblk(2)), (proj3, (None, S, LANE), cblk(3)),
                  (w["conv_w"], (3, LANE), lambda j, b: (0, j)), (w["conv_b"], (1, LANE), lambda j, b: (0, j))],
                 [((batch, S, CW), BF16, (None, S, LANE), lambda j, b: (b, 0, j))])[0].reshape(T, CW)
    res["y_a"] = y_a

    cqn, ckvn = _pcall("mla_pre_fwd", _mla_pre_math, (T // BR,),
                       [(proj, (BR, QL), row(O_CQ // QL)), (proj, (BR, KVL), row(O_CKV // KVL)),
                        (w["q_a_norm_g"], (1, QL), fix), (w["kv_a_norm_g"], (1, KVL), fix)],
                       [((T, QL), BF16, (BR, QL), row(0)), ((T, KVL), BF16, (BR, KVL), row(0))])
    w_uq_p = _pad_heads_uq(w["w_uq"])
    w_uk, w_uv = _split_ukv(w["w_ukv"])
    q = _mm("uq", cqn, w_uq_p, out_dtype=BF16)
    kn = _mm("uk", ckvn, w_uk, out_dtype=BF16)
    v = _mm("uv", ckvn, w_uv, out_dtype=BF16)
    nrr = S // BR
    tab_row = lambda i: (i % nrr, 0)
    qr, kr = _pcall("rope_fwd", _rope_math, (T // BR,),
                    [(q, (BR, NH * QKP), row(0)), (kn, (BR, NH * QKP), row(0)), (proj, (BR, LANE), row(O_KPE // LANE)),
                     (w["mla_q_norm_g"], (1, QKP), fix), (w["mla_k_norm_g"], (1, QKP), fix),
                     (rope_c, (BR, QKP), tab_row), (rope_s1, (BR, QKP), tab_row), (rope_s2, (BR, QKP), tab_row)],
                    [((T, NH * QKP), BF16, (BR, NH * QKP), row(0))] * 2)
    qr = qr.reshape(batch, S, NH * QKP)
    kr = kr.reshape(batch, S, NH * QKP)
    v = v.reshape(batch, S, NH * VD)
    o_b, l_b = _mla_fwd(qr, kr, v)
    ob2 = o_b.reshape(T, NH * VD)
    y_b = _pcall("gateb_fwd", _gate_math, (T // BR,),
                 [(ob2, (BR, 512), row(0)), (proj, (BR, 512), row(O_BZ // 512))],
                 [((T, 512), BF16, (BR, 512), row(0))])[0]
    res.update(cqn=cqn, ckvn=ckvn, q=q, kn=kn, qr=qr, kr=kr, v=v, o_b=o_b, l_b=l_b, ob2=ob2, y_b=y_b,
               w_uq_p=w_uq_p, w_uk=w_uk, w_uv=w_uv)

    gq2 = jnp.tile(w["dil_q_norm_g"].reshape(NG, HD), (1, 2))
    gk2 = jnp.tile(w["dil_k_norm_g"].reshape(NG, HD), (1, 2))
    y_c, o_all, l_all = _dilc_fwd(proj3, gq2, gk2, dil_tab)
    y_c = y_c.reshape(T, DWID)
    res.update(o_all=o_all, l_all=l_all, y_c=y_c)

    pa = _mm("out_a", y_a, w["w_out_a"], out_dtype=BF16)
    pb = _mm("out_b", y_b, w["w_out_b"], out_dtype=BF16)
    pc = _mm("out_c", y_c, w["w_out_c"], out_dtype=BF16)
    merged = _pcall("merge_fwd", _merge_math, (T // BRM,),
                    [(proj, (BRM, D), row(O_G // D + s)) for s in range(3)]
                    + [(w["b_gate"], (1, D), (lambda s: (lambda i: (0, s)))(s)) for s in range(3)]
                    + [(t, (BRM, D), row(0)) for t in (pa, pb, pc)],
                    [((T, D), BF16, (BRM, D), row(0))])[0]
    out = _mm("o_proj", merged, w["w_o"], add=x)
    res.update(pa=pa, pb=pb, pc=pc, merged=merged)
    return out, res


def _norm_bwd_math(x, g, dh, dy):
    _, pull = jax.vjp(_rms, x, g)
    dx, dg = pull(dh)
    return dx + dy, dg


def _layer_bwd(dy, w, res, tabs, batch, after_dw=None):
    T = batch * S
    rope_c, rope_s1, rope_s2, dil_tab = tabs
    row = lambda c: (lambda i: (i, c))
    fix = lambda i: (0, 0)
    x, proj, h = res["x"], res["proj"], res["h"]
    proj3 = proj.reshape(batch, S, NINP)
    g = {}

    d_merged = _mm("o_proj_dx", dy, w["w_o"], tb=True)
    g["w_o"] = _mm("o_proj_dw", res["merged"], dy, ta=True, tm=1024)

    merge_bwd = functools.partial(_vjp_of(_merge_math, 9), n_prim=9)
    dg0, dg1, dg2, db0, db1, db2, dpa, dpb, dpc = _pcall(
        "merge_bwd", merge_bwd, (T // BRM,),
        [(proj, (BRM, D), row(O_G // D + s)) for s in range(3)]
        + [(w["b_gate"], (1, D), (lambda s: (lambda i: (0, s)))(s)) for s in range(3)]
        + [(t, (BRM, D), row(0)) for t in (res["pa"], res["pb"], res["pc"])]
        + [(d_merged, (BRM, D), row(0))],
        [((T, D), BF16, (BRM, D), row(0))] * 3 + [((1, D), F32, (1, D), fix, True)] * 3
        + [((T, D), BF16, (BRM, D), row(0))] * 3)
    g["b_gate"] = jnp.concatenate([db0, db1, db2], axis=1)

    d_ya = _mm("out_a_dx", dpa, w["w_out_a"], tb=True)
    d_yb = _mm("out_b_dx", dpb, w["w_out_b"], tb=True)
    d_yc = _mm("out_c_dx", dpc, w["w_out_c"], tb=True)
    g["w_out_a"] = _mm("out_a_dw", res["y_a"], dpa, ta=True)
    g["w_out_b"] = _mm("out_b_dw", res["y_b"], dpb, ta=True)
    g["w_out_c"] = _mm("out_c_dw", res["y_c"], dpc, ta=True)

    cblk = lambda s: (lambda j, b: (b, 0, 4 * s + j))
    oblk = lambda j, b: (b, 0, j)
    conv_bwd = functools.partial(_vjp_of(_conv_math, 6), n_prim=6)
    d_ab, d_ac, d_ax, d_az, g["conv_w"], g["conv_b"] = _pcall(
        "conv_bwd", conv_bwd, (4, batch),
        [(proj3, (None, S, LANE), cblk(s)) for s in range(4)]
        + [(w["conv_w"], (3, LANE), lambda j, b: (0, j)), (w["conv_b"], (1, LANE), lambda j, b: (0, j)),
           (d_ya.reshape(batch, S, CW), (None, S, LANE), oblk)],
        [((batch, S, CW), BF16, (None, S, LANE), oblk)] * 4
        + [((3, CW), F32, (3, LANE), lambda j, b: (0, j), True), ((1, CW), F32, (1, LANE), lambda j, b: (0, j), True)])

    gate_bwd = functools.partial(_vjp_of(_gate_math, 2), n_prim=2)
    d_ob, d_bz = _pcall("gateb_bwd", gate_bwd, (T // BR,),
                        [(res["ob2"], (BR, 512), row(0)), (proj, (BR, 512), row(O_BZ // 512)), (d_yb, (BR, 512), row(0))],
                        [((T, 512), F32, (BR, 512), row(0)), ((T, 512), BF16, (BR, 512), row(0))])
    dqr, dkr, dv = _mla_bwd(res["qr"], res["kr"], res["v"], d_ob.reshape(batch, S, NH * VD), res["o_b"], res["l_b"])
    nrr = S // BR
    tab_row = lambda i: (i % nrr, 0)
    rope_bwd = functools.partial(_vjp_of(_rope_math, 5), n_prim=8)
    d_q, d_kn, d_kpe_p, g["mla_q_norm_g"], g["mla_k_norm_g"] = _pcall(
        "rope_bwd", rope_bwd, (T // BR,),
        [(res["q"], (BR, NH * QKP), row(0)), (res["kn"], (BR, NH * QKP), row(0)), (proj, (BR, LANE), row(O_KPE // LANE)),
         (w["mla_q_norm_g"], (1, QKP), fix), (w["mla_k_norm_g"], (1, QKP), fix),
         (rope_c, (BR, QKP), tab_row), (rope_s1, (BR, QKP), tab_row), (rope_s2, (BR, QKP), tab_row),
         (dqr.reshape(T, NH * QKP), (BR, NH * QKP), row(0)), (dkr.reshape(T, NH * QKP), (BR, NH * QKP), row(0))],
        [((T, NH * QKP), BF16, (BR, NH * QKP), row(0))] * 2 + [((T, LANE), BF16, (BR, LANE), row(0))]
        + [((1, QKP), F32, (1, QKP), fix, True)] * 2)
    dv = dv.reshape(T, NH * VD)
    d_cqn = _mm("uq_dx", d_q, res["w_uq_p"], tb=True)
    d_ckvn = _mm("uk_dx", d_kn, res["w_uk"], tb=True)
    d_ckvn = _mm("uv_dx", dv, res["w_uv"], tb=True, add=d_ckvn)
    g["w_uq"] = _unpad_heads_uq(_mm("uq_dw", res["cqn"], d_q, ta=True))
    g["w_ukv"] = _join_ukv(_mm("uk_dw", res["ckvn"], d_kn, ta=True), _mm("uv_dw", res["ckvn"], dv, ta=True))
    pre_bwd = functools.partial(_vjp_of(_mla_pre_math, 4), n_prim=4)
    d_cq, d_ckv, g["q_a_norm_g"], g["kv_a_norm_g"] = _pcall(
        "mla_pre_bwd", pre_bwd, (T // BR,),
        [(proj, (BR, QL), row(O_CQ // QL)), (proj, (BR, KVL), row(O_CKV // KVL)),
         (w["q_a_norm_g"], (1, QL), fix), (w["kv_a_norm_g"], (1, KVL), fix),
         (d_cqn, (BR, QL), row(0)), (d_ckvn, (BR, KVL), row(0))],
        [((T, QL), BF16, (BR, QL), row(0)), ((T, KVL), BF16, (BR, KVL), row(0)),
         ((1, QL), F32, (1, QL), fix, True), ((1, KVL), F32, (1, KVL), fix, True)])

    gq2 = jnp.tile(w["dil_q_norm_g"].reshape(NG, HD), (1, 2))
    gk2 = jnp.tile(w["dil_k_norm_g"].reshape(NG, HD), (1, 2))
    d_dq, d_dk, d_dv, d_cz, dgq, dgk = _dilc_bwd(proj3, gq2, gk2, dil_tab, res["o_all"], res["l_all"],
                                                 d_yc.reshape(batch, S, DWID))
    g["dil_q_norm_g"] = dgq[:, :HD] + dgq[:, HD:]
    g["dil_k_norm_g"] = dgk[:, :HD] + dgk[:, HD:]
    d_dq, d_dk, d_dv = (t.reshape(T, NG * DWID) for t in (d_dq, d_dk, d_dv))
    d_cz = d_cz.reshape(T, DWID)

    dproj = jnp.concatenate(
        [t.reshape(T, CW) for t in (d_ab, d_ac, d_ax, d_az)]
        + [d_cq, d_ckv, d_kpe_p, d_bz, d_dq, d_dk, d_dv, d_cz, dg0, dg1, dg2], axis=1)
    g["w_in_t"] = _mm("in_proj_dw", dproj, h, ta=True, tm=1024, tk=T)
    dep = after_dw(g) if after_dw is not None else None
    d_h = _mm("in_proj_dx", dproj, w["w_in_t"], dep=dep, tm=1024, tk=NINP // 4)
    dx, g["norm_g"] = _pcall("norm_bwd", _norm_bwd_math, (T // BR,),
                             [(x, (BR, D), row(0)), (w["norm_g"], (1, D), fix), (d_h, (BR, D), row(0)),
                              (dy, (BR, D), row(0))],
                             [((T, D), F32, (BR, D), row(0)), ((1, D), F32, (1, D), fix, True)])
    return dx, g


def _loss_math(y, t):
    e = y - t
    return e * (1.0 / D), 0.5 * jnp.sum(jnp.sum(e * e, axis=-1, keepdims=True) / D, axis=0, keepdims=True)


def _local_step(x, target, ws, batch):
    T = batch * S
    tabs = _rope_tables() + (_dil_slopes(),)
    saved = []
    y = x
    for l in range(NL):
        y, res = _layer_fwd(y, ws[l], tabs, batch)
        saved.append(res)
    row = lambda i: (i, 0)
    dy, loss = _pcall("loss", _loss_math, (T // BR,),
                      [(y, (BR, D), row), (target, (BR, D), row)],
                      [((T, D), F32, (BR, D), row), ((1, 1), F32, (1, 1), lambda i: (0, 0), True)])
    grads = [None] * NL
    for l in reversed(range(NL)):
        dy, grads[l] = _layer_bwd(dy, ws[l], saved[l], tabs, batch)
    return loss, dy, grads


ANY = pl.BlockSpec(memory_space=pl.ANY)
U32 = jnp.uint32
WSH = NIN // 4
WA = KPE_END
WB = WSH - WA
CWD = 512
PACK_ROWS = 1472
HW = PACK_W // 2


def _me():
    return lax.axis_index("x"), lax.axis_index("y"), lax.axis_index("c")


def _piece_rows(k):
    a = k * WSH + jnp.where(k > 0, NINP - NIN, 0)
    b = k * WSH + WA + (NINP - NIN)
    return ((0, pl.multiple_of(a, 8), WA), (WA, pl.multiple_of(b, 8), WB))


def _pack_words(lo, hi):
    ul = lax.bitcast_convert_type(lo.astype(BF16).astype(F32), U32)
    uh = lax.bitcast_convert_type(hi.astype(BF16).astype(F32), U32)
    w = jnp.bitwise_or(jnp.bitwise_and(uh, jnp.uint32(0xFFFF0000)), jnp.right_shift(ul, jnp.uint32(16)))
    return lax.bitcast_convert_type(w, F32)


def _unpack_words(w):
    w = lax.bitcast_convert_type(w, U32)
    lo = lax.bitcast_convert_type(jnp.left_shift(w, jnp.uint32(16)), F32)
    hi = lax.bitcast_convert_type(jnp.bitwise_and(w, jnp.uint32(0xFFFF0000)), F32)
    return lo, hi


def _all_gather(wc, sp):
    def body(w_ref, s_ref, ow_ref, os_ref, send_sems, recv_sems):
        x, y, c = _me()
        k_me = 2 * x + y
        sib = (x, y, 1 - c)
        chips = [(1 - x, y), (x, 1 - y), (1 - x, 1 - y)]
        wcols = lambda cc: pl.ds(pl.multiple_of(cc * (CWD // 2), LANE), CWD // 2)
        scols = lambda cc: pl.ds(pl.multiple_of(cc * HW, LANE), HW)

        def windows(k, cc):
            pcs = _piece_rows(k)
            return ([(w_ref.at[pl.ds(l0, n), wcols(cc)], ow_ref.at[pl.ds(p0, n), wcols(cc)]) for l0, p0, n in pcs]
                    + [(s_ref.at[:, scols(cc)], os_ref.at[k, :, scols(cc)])])

        def copy(i, src, dst, to):
            return pltpu.make_async_remote_copy(src_ref=src, dst_ref=dst, send_sem=send_sems.at[i],
                                                recv_sem=recv_sems.at[i], device_id=to, device_id_type=MESH)

        def own_windows():
            return ([(w_ref.at[pl.ds(l0, n)], ow_ref.at[pl.ds(p0, n)]) for l0, p0, n in _piece_rows(k_me)]
                    + [(s_ref, os_ref.at[k_me])])

        first = [copy(18 + i, src, dst, sib) for i, (src, dst) in enumerate(own_windows())]
        for j, (cx, cy) in enumerate(chips):
            for i, (src, dst) in enumerate(windows(k_me, c)):
                first.append(copy(3 * j + i, src, dst, (cx, cy, c)))
        for cp in first:
            cp.start()
        passed = []
        for j, (cx, cy) in enumerate(chips):
            for i, (_, dst) in enumerate(windows(2 * cx + cy, c)):
                copy(3 * j + i, dst, dst, (cx, cy, c)).wait_recv()
                cp = copy(9 + 3 * j + i, dst, dst, sib)
                cp.start()
                passed.append(cp)
        for j, (cx, cy) in enumerate(chips):
            for i, (_, dst) in enumerate(windows(2 * cx + cy, 1 - c)):
                copy(9 + 3 * j + i, dst, dst, sib).wait_recv()
        for i, (_, dst) in enumerate(own_windows()):
            copy(18 + i, dst, dst, sib).wait_recv()
        for cp in first + passed:
            cp.wait_send()

    return pl.pallas_call(
        body,
        out_shape=[jax.ShapeDtypeStruct((NINP, CWD), F32), jax.ShapeDtypeStruct((4, PACK_ROWS, PACK_W), BF16)],
        in_specs=[ANY, ANY], out_specs=[ANY, ANY],
        scratch_shapes=[pltpu.SemaphoreType.DMA((21,)), pltpu.SemaphoreType.DMA((21,))],
        name="weights_all_gather",
    )(wc, sp)


HBM = pl.BlockSpec(memory_space=pltpu.HBM)
SEM = pl.BlockSpec(memory_space=pltpu.SEMAPHORE)
EFFECT = pltpu.SideEffectType.DATAFLOW_SIDE_EFFECTING


def _in_hbm(a):
    return pltpu.with_memory_space_constraint(a, pltpu.HBM)


def _ag_windows(w_ref, s_ref, lw_ref, ls_ref, k, cc):
    wcols = pl.ds(pl.multiple_of(cc * (CWD // 2), LANE), CWD // 2)
    scols = pl.ds(pl.multiple_of(cc * HW, LANE), HW)
    return ([(w_ref.at[pl.ds(l0, n), wcols], lw_ref.at[pl.ds(p0, n), wcols]) for l0, p0, n in _piece_rows(k)]
            + [(s_ref.at[:, scols], ls_ref.at[k, :, scols])])


def _ag_ici_copies(w_ref, s_ref, lw_ref, ls_ref, send_sems, recv_sems):
    x, y, c = _me()
    mine, theirs = [], []
    for j, (cx, cy) in enumerate([(1 - x, y), (x, 1 - y), (1 - x, 1 - y)]):
        for i, ((src, dst), (_, got)) in enumerate(zip(_ag_windows(w_ref, s_ref, lw_ref, ls_ref, 2 * x + y, c),
                                                       _ag_windows(w_ref, s_ref, lw_ref, ls_ref, 2 * cx + cy, c))):
            mk = lambda s_, d_: pltpu.make_async_remote_copy(
                src_ref=s_, dst_ref=d_, send_sem=send_sems.at[3 * j + i], recv_sem=recv_sems.at[3 * j + i],
                device_id=(cx, cy, c), device_id_type=MESH)
            mine.append(mk(src, dst))
            theirs.append(mk(got, got))
    return mine, theirs


def _ag_ici_start(wc, sp, dep):
    def body(w_ref, s_ref, lw_ref, ls_ref, dep_ref, send_sems, recv_sems, w_thru, s_thru, lw_thru, ls_thru, token):
        mine, _ = _ag_ici_copies(w_ref, s_ref, lw_ref, ls_ref, send_sems, recv_sems)
        for cp in mine:
            cp.start()
        token[...] = jnp.zeros_like(token)

    return pl.pallas_call(
        body, name="weights_gather_start",
        out_shape=(pltpu.SemaphoreType.DMA((9,)), pltpu.SemaphoreType.DMA((9,)), pltpu.HBM(wc.shape, wc.dtype),
                   pltpu.HBM(sp.shape, sp.dtype), pltpu.HBM((NINP, CWD), F32), pltpu.HBM((4, PACK_ROWS, PACK_W), BF16),
                   jax.ShapeDtypeStruct((8, LANE), F32)),
        in_specs=(HBM, HBM, HBM, HBM, ANY),
        out_specs=(SEM, SEM, HBM, HBM, HBM, HBM, pl.BlockSpec(memory_space=pltpu.VMEM)),
        input_output_aliases={0: 2, 1: 3, 2: 4, 3: 5},
        compiler_params=pltpu.CompilerParams(has_side_effects=EFFECT),
    )(_in_hbm(wc), _in_hbm(sp), _in_hbm(lax.empty((NINP, CWD), F32)), _in_hbm(lax.empty((4, PACK_ROWS, PACK_W), BF16)), dep)


def _ag_ici_wait(send_sems, recv_sems, wc, sp, lw, ls, after):
    def body(w_ref, s_ref, lw_ref, ls_ref, send_sems, recv_sems, after_ref, w_dead, s_dead, lw_out, ls_out):
        mine, theirs = _ag_ici_copies(w_ref, s_ref, lw_ref, ls_ref, send_sems, recv_sems)
        for cp in mine:
            cp.wait_send()
        for cp in theirs:
            cp.wait_recv()

    out = pl.pallas_call(
        body, name="weights_gather_wait",
        out_shape=(pltpu.HBM(wc.shape, wc.dtype), pltpu.HBM(sp.shape, sp.dtype), pltpu.HBM(lw.shape, lw.dtype),
                   pltpu.HBM(ls.shape, ls.dtype)),
        in_specs=(HBM, HBM, HBM, HBM, SEM, SEM, ANY), out_specs=(HBM, HBM, HBM, HBM),
        input_output_aliases={0: 0, 1: 1, 2: 2, 3: 3},
        compiler_params=pltpu.CompilerParams(has_side_effects=EFFECT),
    )(wc, sp, lw, ls, send_sems, recv_sems, after)
    return out[2], out[3]


def _ag_finish(wc, sp, lw, ls):
    def body(w_ref, s_ref, lw_ref, ls_ref, ow_ref, os_ref, send_sems, recv_sems):
        x, y, c = _me()
        k_me = 2 * x + y
        sib = (x, y, 1 - c)
        chips = [(1 - x, y), (x, 1 - y), (1 - x, 1 - y)]

        def copy(i, src, dst):
            return pltpu.make_async_remote_copy(src_ref=src, dst_ref=dst, send_sem=send_sems.at[i],
                                                recv_sem=recv_sems.at[i], device_id=sib, device_id_type=MESH)

        def own_windows():
            return ([(w_ref.at[pl.ds(l0, n)], ow_ref.at[pl.ds(p0, n)]) for l0, p0, n in _piece_rows(k_me)]
                    + [(s_ref, os_ref.at[k_me])])

        out = [copy(9 + i, src, dst) for i, (src, dst) in enumerate(own_windows())]
        for j, (cx, cy) in enumerate(chips):
            landed = _ag_windows(w_ref, s_ref, lw_ref, ls_ref, 2 * cx + cy, c)
            for i, (_, dst) in enumerate(_ag_windows(w_ref, s_ref, ow_ref, os_ref, 2 * cx + cy, c)):
                out.append(copy(3 * j + i, landed[i][1], dst))
        for cp in out:
            cp.start()
        for j, (cx, cy) in enumerate(chips):
            for i, (_, dst) in enumerate(_ag_windows(w_ref, s_ref, ow_ref, os_ref, 2 * cx + cy, 1 - c)):
                copy(3 * j + i, dst, dst).wait_recv()
        for i, (_, dst) in enumerate(own_windows()):
            copy(9 + i, dst, dst).wait_recv()
        for cp in out:
            cp.wait_send()

    return pl.pallas_call(
        body,
        out_shape=[jax.ShapeDtypeStruct(lw.shape, lw.dtype), jax.ShapeDtypeStruct(ls.shape, ls.dtype)],
        in_specs=[ANY] * 4, out_specs=[ANY, ANY],
        input_output_aliases={2: 0, 3: 1},
        scratch_shapes=[pltpu.SemaphoreType.DMA((12,)), pltpu.SemaphoreType.DMA((12,))],
        name="weights_gather_finish",
    )(wc, sp, lw, ls)


UNPACK_BR = 512


def _unpack_w_in(cont):
    def body(c_ref, o_ref):
        lo, hi = _unpack_words(c_ref[...])
        r = pl.program_id(0) * UNPACK_BR + lax.broadcasted_iota(jnp.int32, (UNPACK_BR, CWD), 0)
        pad = jnp.logical_and(r >= KPE_END, r < KPE_END + NINP - NIN)
        o_ref[:, 0:CWD] = jnp.where(pad, 0.0, lo).astype(BF16)
        o_ref[:, CWD:2 * CWD] = jnp.where(pad, 0.0, hi).astype(BF16)

    return pl.pallas_call(
        body, grid=(NINP // UNPACK_BR,),
        in_specs=[pl.BlockSpec((UNPACK_BR, CWD), lambda i: (i, 0))],
        out_specs=pl.BlockSpec((UNPACK_BR, D), lambda i: (i, 0)),
        out_shape=jax.ShapeDtypeStruct((NINP, D), BF16),
        name="w_in_unpack",
        compiler_params=pltpu.CompilerParams(dimension_semantics=("arbitrary",), vmem_limit_bytes=VMEM_LIMIT),
    )(cont)


def _rs_swap(gw, gs):
    def body(w_ref, s_ref, rw_ref, rs_ref, send_sems, recv_sems):
        x, y, c = _me()
        oc = 1 - c
        cps = [pltpu.make_async_remote_copy(src_ref=w_ref.at[:, pl.ds(pl.multiple_of(oc * (D // 2), LANE), D // 2)],
                                            dst_ref=rw_ref, send_sem=send_sems.at[0], recv_sem=recv_sems.at[0],
                                            device_id=(x, y, oc), device_id_type=MESH),
               pltpu.make_async_remote_copy(src_ref=s_ref.at[:, :, pl.ds(pl.multiple_of(oc * HW, LANE), HW)],
                                            dst_ref=rs_ref, send_sem=send_sems.at[1], recv_sem=recv_sems.at[1],
                                            device_id=(x, y, oc), device_id_type=MESH)]
        for cp in cps:
            cp.start()
        for cp in cps:
            cp.wait()

    return pl.pallas_call(
        body,
        out_shape=[jax.ShapeDtypeStruct((NINP, D // 2), F32), jax.ShapeDtypeStruct((4, PACK_ROWS, HW), F32)],
        in_specs=[ANY, ANY], out_specs=[ANY, ANY],
        scratch_shapes=[pltpu.SemaphoreType.DMA((2,)), pltpu.SemaphoreType.DMA((2,))],
        name="grads_sibling_swap",
    )(gw, gs)


SUM_BR = 512


def _rs_chip_sum_w(gw, rw, cidx):
    def body(c_ref, g_ref, r_ref, o_ref):
        s = g_ref[...] + r_ref[...]
        q = D // 8
        o_ref[...] = jnp.concatenate([_pack_words(s[:, 0:q], s[:, q:2 * q]),
                                      _pack_words(s[:, 2 * q:3 * q], s[:, 3 * q:4 * q])], axis=1)

    return pl.pallas_call(
        body,
        grid_spec=pltpu.PrefetchScalarGridSpec(
            num_scalar_prefetch=1, grid=(NINP // SUM_BR,),
            in_specs=[pl.BlockSpec((SUM_BR, D // 2), lambda i, cr: (i, cr[0])),
                      pl.BlockSpec((SUM_BR, D // 2), lambda i, cr: (i, 0))],
            out_specs=pl.BlockSpec((SUM_BR, D // 4), lambda i, cr: (i, 0))),
        out_shape=jax.ShapeDtypeStruct((NINP, D // 4), F32),
        name="grads_chip_sum_w",
        compiler_params=pltpu.CompilerParams(dimension_semantics=("arbitrary",), vmem_limit_bytes=VMEM_LIMIT),
    )(cidx, gw, rw)


def _rs_chip_sum_s(gs, rs, cidx):
    def body(c_ref, g_ref, r_ref, o_ref):
        o_ref[...] = (g_ref[...] + r_ref[...]).astype(BF16)

    return pl.pallas_call(
        body,
        grid_spec=pltpu.PrefetchScalarGridSpec(
            num_scalar_prefetch=1, grid=(4,),
            in_specs=[pl.BlockSpec((None, PACK_ROWS, HW), lambda j, cr: (j, 0, cr[0])),
                      pl.BlockSpec((None, PACK_ROWS, HW), lambda j, cr: (j, 0, 0))],
            out_specs=pl.BlockSpec((None, PACK_ROWS, HW), lambda j, cr: (j, 0, 0))),
        out_shape=jax.ShapeDtypeStruct((4, PACK_ROWS, HW), BF16),
        name="grads_chip_sum_s",
        compiler_params=pltpu.CompilerParams(dimension_semantics=("arbitrary",), vmem_limit_bytes=VMEM_LIMIT),
    )(cidx, gs, rs)


def _rs_exchange_copies(sw_ref, ss_ref, r2w_ref, r2s_ref, send_sems, recv_sems):
    x, y, c = _me()
    mine, theirs = [], []
    for j, (cx, cy) in enumerate([(1 - x, y), (x, 1 - y), (1 - x, 1 - y)]):
        def mk(i, src, dst):
            return pltpu.make_async_remote_copy(src_ref=src, dst_ref=dst, send_sem=send_sems.at[3 * j + i],
                                                recv_sem=recv_sems.at[3 * j + i], device_id=(cx, cy, c), device_id_type=MESH)
        for i, (l0, p0, n) in enumerate(_piece_rows(2 * cx + cy)):
            mine.append(mk(i, sw_ref.at[pl.ds(p0, n)], r2w_ref.at[j, pl.ds(l0, n)]))
            theirs.append(mk(i, r2w_ref.at[j, pl.ds(l0, n)], r2w_ref.at[j, pl.ds(l0, n)]))
        mine.append(mk(2, ss_ref.at[2 * cx + cy], r2s_ref.at[j]))
        theirs.append(mk(2, r2s_ref.at[j], r2s_ref.at[j]))
    return mine, theirs


def _rs_exchange_start(sw, ss, tag):
    def body(sw_ref, ss_ref, r2w_ref, r2s_ref, send_sems, recv_sems, sw_thru, ss_thru, r2w_thru, r2s_thru, token):
        mine, _ = _rs_exchange_copies(sw_ref, ss_ref, r2w_ref, r2s_ref, send_sems, recv_sems)
        for cp in mine:
            cp.start()
        token[...] = jnp.zeros_like(token)

    return pl.pallas_call(
        body, name="grads_exchange_start_" + tag,
        out_shape=(pltpu.SemaphoreType.DMA((9,)), pltpu.SemaphoreType.DMA((9,)), pltpu.HBM(sw.shape, sw.dtype),
                   pltpu.HBM(ss.shape, ss.dtype), pltpu.HBM((3, WSH, D // 4), F32), pltpu.HBM((3, PACK_ROWS, HW), BF16),
                   jax.ShapeDtypeStruct((8, LANE), F32)),
        in_specs=(HBM, HBM, HBM, HBM),
        out_specs=(SEM, SEM, HBM, HBM, HBM, HBM, pl.BlockSpec(memory_space=pltpu.VMEM)),
        input_output_aliases={0: 2, 1: 3, 2: 4, 3: 5},
        compiler_params=pltpu.CompilerParams(has_side_effects=EFFECT),
    )(_in_hbm(sw), _in_hbm(ss), _in_hbm(lax.empty((3, WSH, D // 4), F32)), _in_hbm(lax.empty((3, PACK_ROWS, HW), BF16)))


def _rs_exchange_wait(send_sems, recv_sems, sw, ss, r2w, r2s, after, tag):
    def body(sw_ref, ss_ref, r2w_ref, r2s_ref, send_sems, recv_sems, after_ref, sw_dead, ss_dead, r2w_out, r2s_out):
        mine, theirs = _rs_exchange_copies(sw_ref, ss_ref, r2w_ref, r2s_ref, send_sems, recv_sems)
        for cp in mine:
            cp.wait_send()
        for cp in theirs:
            cp.wait_recv()

    out = pl.pallas_call(
        body, name="grads_exchange_wait_" + tag,
        out_shape=(pltpu.HBM(sw.shape, sw.dtype), pltpu.HBM(ss.shape, ss.dtype), pltpu.HBM(r2w.shape, r2w.dtype),
                   pltpu.HBM(r2s.shape, r2s.dtype)),
        in_specs=(HBM, HBM, HBM, HBM, SEM, SEM, ANY), out_specs=(HBM, HBM, HBM, HBM),
        input_output_aliases={0: 0, 1: 1, 2: 2, 3: 3},
        compiler_params=pltpu.CompilerParams(has_side_effects=EFFECT),
    )(sw, ss, r2w, r2s, send_sems, recv_sems, after)
    return out[2], out[3]


def _rs_final_w(gw, rw, r2w, idx):
    q = D // 8

    def body(i_ref, g_ref, r_ref, p_ref, o_ref, gbuf, rbuf, sems):
        i = pl.program_id(0)
        k, c = i_ref[0], i_ref[1]
        cps = []
        for n_, (l0, p0, n) in enumerate(_piece_rows(k)):
            gcol = pl.ds(pl.multiple_of(c * (D // 2) + i * 2 * q, LANE), 2 * q)
            rcol = pl.ds(pl.multiple_of(i * 2 * q, LANE), 2 * q)
            cps.append(pltpu.make_async_copy(g_ref.at[pl.ds(p0, n), gcol], gbuf.at[pl.ds(l0, n)], sems.at[2 * n_]))
            cps.append(pltpu.make_async_copy(r_ref.at[pl.ds(p0, n), rcol], rbuf.at[pl.ds(l0, n)], sems.at[2 * n_ + 1]))
        for cp in cps:
            cp.start()
        for cp in cps:
            cp.wait()
        acc = gbuf[...] + rbuf[...]
        for j in range(3):
            lo, hi = _unpack_words(p_ref[j])
            acc = acc + jnp.concatenate([lo, hi], axis=1)
        o_ref[...] = acc

    return pl.pallas_call(
        body,
        grid_spec=pltpu.PrefetchScalarGridSpec(
            num_scalar_prefetch=1, grid=(2,),
            in_specs=[ANY, ANY, pl.BlockSpec((3, WSH, q), lambda i, ir: (0, 0, i))],
            out_specs=pl.BlockSpec((WSH, 2 * q), lambda i, ir: (0, i)),
            scratch_shapes=[pltpu.VMEM((WSH, 2 * q), F32), pltpu.VMEM((WSH, 2 * q), F32), pltpu.SemaphoreType.DMA((4,))]),
        out_shape=jax.ShapeDtypeStruct((WSH, D // 2), F32),
        name="grads_final_sum_w",
        compiler_params=pltpu.CompilerParams(dimension_semantics=("arbitrary",), vmem_limit_bytes=VMEM_LIMIT),
    )(idx, gw, rw, r2w)


def _rs_final_s(gs, rs, r2s, idx):
    def body(i_ref, g_ref, r_ref, p_ref, o_ref):
        acc = g_ref[...] + r_ref[...]
        for j in range(3):
            acc = acc + p_ref[j].astype(F32)
        o_ref[...] = acc

    return pl.pallas_call(
        body,
        grid_spec=pltpu.PrefetchScalarGridSpec(
            num_scalar_prefetch=1, grid=(1,),
            in_specs=[pl.BlockSpec((None, PACK_ROWS, HW), lambda i, ir: (ir[0], 0, ir[1])),
                      pl.BlockSpec((None, PACK_ROWS, HW), lambda i, ir: (ir[0], 0, 0)),
                      pl.BlockSpec((3, PACK_ROWS, HW), lambda i, ir: (0, 0, 0))],
            out_specs=pl.BlockSpec((PACK_ROWS, HW), lambda i, ir: (0, 0))),
        out_shape=jax.ShapeDtypeStruct((PACK_ROWS, HW), F32),
        name="grads_final_sum_s",
        compiler_params=pltpu.CompilerParams(dimension_semantics=("arbitrary",), vmem_limit_bytes=VMEM_LIMIT),
    )(idx, gs, rs, r2s)


def _rs_share(fw, fs):
    def body(w_ref, s_ref, ow_ref, os_ref, send_sems, recv_sems):
        x, y, c = _me()
        cps = [pltpu.make_async_remote_copy(src_ref=w_ref, dst_ref=ow_ref, send_sem=send_sems.at[0],
                                            recv_sem=recv_sems.at[0], device_id=(x, y, 1 - c), device_id_type=MESH),
               pltpu.make_async_remote_copy(src_ref=s_ref, dst_ref=os_ref, send_sem=send_sems.at[1],
                                            recv_sem=recv_sems.at[1], device_id=(x, y, 1 - c), device_id_type=MESH)]
        for cp in cps:
            cp.start()
        for cp in cps:
            cp.wait()

    return pl.pallas_call(
        body,
        out_shape=[jax.ShapeDtypeStruct((WSH, D // 2), F32), jax.ShapeDtypeStruct((PACK_ROWS, HW), F32)],
        in_specs=[ANY, ANY], out_specs=[ANY, ANY],
        scratch_shapes=[pltpu.SemaphoreType.DMA((2,)), pltpu.SemaphoreType.DMA((2,))],
        name="grads_share",
    )(fw, fs)


def _both_halves(mine, other, c):
    return jnp.where(c == 0, jnp.concatenate([mine, other], axis=1), jnp.concatenate([other, mine], axis=1))


def _rs_begin(gw, gs):
    x, y, c = _me()
    cidx = jnp.reshape(c, (1,)).astype(jnp.int32)
    rw, rs = _rs_swap(gw, gs)
    return dict(gw=gw, gs=gs, rw=rw, rs=rs, sw=_rs_chip_sum_w(gw, rw, cidx), ss=_rs_chip_sum_s(gs, rs, cidx))


def _rs_end(st, r2w, r2s):
    x, y, c = _me()
    idx = jnp.stack([2 * x + y, c]).astype(jnp.int32)
    fw = _rs_final_w(st["gw"], st["rw"], r2w, idx)
    fs = _rs_final_s(st["gs"], st["rs"], r2s, idx)
    ow, os_ = _rs_share(fw, fs)
    return _both_halves(fw, ow, c), _both_halves(fs, os_, c)


def _all_reduce_small(gs):
    rows = gs.shape[0]

    def body(g_ref, o_ref, buf, send_sems, recv_sems):
        x, y, c = _me()
        me = 4 * x + 2 * y + c
        buf[me] = g_ref[...]
        cps = []
        for r in range(1, 8):
            fx, fy, fc = (r >> 2) & 1, (r >> 1) & 1, r & 1
            px, py, pc = jnp.bitwise_xor(x, fx), jnp.bitwise_xor(y, fy), jnp.bitwise_xor(c, fc)
            cps.append((pltpu.make_async_remote_copy(
                src_ref=g_ref, dst_ref=buf.at[me], send_sem=send_sems.at[r - 1], recv_sem=recv_sems.at[r - 1],
                device_id=(px, py, pc), device_id_type=MESH), 4 * px + 2 * py + pc))
        for cp, _ in cps:
            cp.start()
        for r, (cp, peer) in enumerate(cps):
            pltpu.make_async_remote_copy(
                src_ref=g_ref, dst_ref=buf.at[peer], send_sem=send_sems.at[r], recv_sem=recv_sems.at[r],
                device_id=(x, y, c), device_id_type=MESH).wait_recv()
        for cp, _ in cps:
            cp.wait_send()
        acc = buf[0]
        for k in range(1, 8):
            acc = acc + buf[k]
        o_ref[...] = acc

    return pl.pallas_call(
        body,
        out_shape=jax.ShapeDtypeStruct((rows, LANE), F32),
        in_specs=[pl.BlockSpec(memory_space=pltpu.VMEM)],
        out_specs=pl.BlockSpec(memory_space=pltpu.VMEM),
        scratch_shapes=[pltpu.VMEM((8, rows, LANE), F32), pltpu.SemaphoreType.DMA((7,)), pltpu.SemaphoreType.DMA((7,))],
        name="small_grads_all_reduce",
    )(gs)


PACK_SPLIT = (("w_uq", 96, (QL, 192)), ("w_ukv", 64, (KVL, 256)),
              ("w_out_a", 256, (CW, 256)), ("w_out_b", 256, (CW, 256)), ("w_out_c", 256, (CW, 256)),
              ("w_o", 512, (256, D)))
MAT_ROWS = 1440
CONV_SHARD = 3 * 128


def _w_in_words(w_in_shard):
    t = w_in_shard.T
    return _pack_words(t[:, :CWD], t[:, CWD:])


def _pack_weights(wl):
    parts = [wl[n].astype(BF16).reshape(-1, PACK_W) for n, _, _ in PACK_SPLIT]
    cw = wl["conv_w"].reshape(-1)
    hi = cw.astype(BF16)
    r1 = cw - hi.astype(F32)
    mid = r1.astype(BF16)
    lo = (r1 - mid.astype(F32)).astype(BF16)
    cterms = jnp.pad(jnp.concatenate([hi, mid, lo]), (0, 3 * PACK_W - 3 * CONV_SHARD)).reshape(3, PACK_W)
    tail = jnp.pad(cterms, ((0, PACK_ROWS - MAT_ROWS - 3), (0, 0)))
    return jnp.concatenate(parts + [tail], axis=0)


def _unpack_weights(gath):
    out = {}
    r = 0
    for n, nrows, shp in PACK_SPLIT:
        t = gath[:, r:r + nrows].reshape((4,) + shp)
        r += nrows
        if n == "w_o":
            out[n] = t.reshape(4 * shp[0], shp[1])
        else:
            out[n] = t.transpose(1, 0, 2).reshape(shp[0], 4 * shp[1])
    ct = gath[:, r:r + 3].reshape(4, 3 * PACK_W)[:, :3 * CONV_SHARD].astype(F32).reshape(4, 3, CONV_SHARD)
    cw = (ct[:, 0] + ct[:, 1]) + ct[:, 2]
    out["conv_w"] = cw.reshape(4, 3, 128).transpose(1, 0, 2).reshape(3, CW)
    return out


def _pack_grads(g):
    parts = []
    for n, nrows, shp in PACK_SPLIT:
        t = g[n]
        if n == "w_o":
            t = t.reshape((4,) + shp)
        else:
            t = t.reshape(shp[0], 4, shp[1]).transpose(1, 0, 2)
        parts.append(t.reshape(4, nrows, PACK_W))
    cw = g["conv_w"].reshape(3, 4, 128).transpose(1, 0, 2).reshape(4, 1, CONV_SHARD)
    parts.append(jnp.pad(cw, ((0, 0), (0, PACK_ROWS - MAT_ROWS - 1), (0, PACK_W - CONV_SHARD))))
    return jnp.concatenate(parts, axis=1)


def _unpack_grads(red):
    out = {}
    r = 0
    for n, nrows, shp in PACK_SPLIT:
        out[n] = red[r:r + nrows].reshape(shp)
        r += nrows
    out["conv_w"] = red[r, :CONV_SHARD].reshape(3, 128)
    return out


SMALL_SIZES = (("norm_g", D), ("b_gate", 3 * D), ("conv_b", CW), ("q_a_norm_g", QL), ("kv_a_norm_g", KVL),
               ("mla_q_norm_g", QK), ("mla_k_norm_g", QK), ("dil_q_norm_g", NG * HD), ("dil_k_norm_g", NG * HD))
SMALL_ROWS = 88


def _pack_small(per_name):
    flat = jnp.concatenate([per_name[n].reshape(-1).astype(F32) for n, _ in SMALL_SIZES])
    return jnp.pad(flat, (0, SMALL_ROWS * LANE - flat.shape[0])).reshape(SMALL_ROWS, LANE)


def _unpack_small(packed, like):
    out = {}
    flat = packed.reshape(-1)
    r = 0
    for n, sz in SMALL_SIZES:
        out[n] = flat[r:r + NL * sz].reshape(like[n].shape)
        r += NL * sz
    return out


def _adamw_math(w, g, m, v):
    m = ADAM_B1 * m + (1.0 - ADAM_B1) * g
    v = ADAM_B2 * v + (1.0 - ADAM_B2) * jnp.square(g)
    m_hat = m / (1.0 - ADAM_B1 ** ADAM_STEP)
    v_hat = v / (1.0 - ADAM_B2 ** ADAM_STEP)
    delta = -ADAM_LR * (m_hat / (jnp.sqrt(v_hat) + ADAM_EPS) + ADAM_WD * w)
    return delta, m, v


def _adamw(name, w, g, m, v, br, bc=None):
    L, R, C = w.shape
    bc = C if bc is None else bc
    blk = lambda l, i, j: (l, i, j)
    return _pcall(name, _adamw_math, (L, R // br, C // bc), [(t, (None, br, bc), blk) for t in (w, g, m, v)],
                  [((L, R, C), F32, (None, br, bc), blk)] * 3)


ADAM_ROWS = {"w_uq": 256, "w_ukv": 128, "w_out_a": 512, "w_out_b": 512, "w_out_c": 512, "w_o": 256,
             "conv_w": 3}


def kernel(x, norm_g, w_in, b_gate, conv_w, conv_b, q_a_norm_g, w_uq, kv_a_norm_g, w_ukv, mla_q_norm_g, mla_k_norm_g, dil_q_norm_g, dil_k_norm_g, w_out_a, w_out_b, w_out_c, w_o, loss_target, m_norm_g, m_w_in, m_b_gate, m_conv_w, m_conv_b, m_q_a_norm_g, m_w_uq, m_kv_a_norm_g, m_w_ukv, m_mla_q_norm_g, m_mla_k_norm_g, m_dil_q_norm_g, m_dil_k_norm_g, m_w_out_a, m_w_out_b, m_w_out_c, m_w_o, v_norm_g, v_w_in, v_b_gate, v_conv_w, v_conv_b, v_q_a_norm_g, v_w_uq, v_kv_a_norm_g, v_w_ukv, v_mla_q_norm_g, v_mla_k_norm_g, v_dil_q_norm_g, v_dil_k_norm_g, v_w_out_a, v_w_out_b, v_w_out_c, v_w_o):
    W = dict(norm_g=norm_g, w_in=w_in, b_gate=b_gate, conv_w=conv_w, conv_b=conv_b, q_a_norm_g=q_a_norm_g, w_uq=w_uq,
             kv_a_norm_g=kv_a_norm_g, w_ukv=w_ukv, mla_q_norm_g=mla_q_norm_g, mla_k_norm_g=mla_k_norm_g,
             dil_q_norm_g=dil_q_norm_g, dil_k_norm_g=dil_k_norm_g, w_out_a=w_out_a, w_out_b=w_out_b, w_out_c=w_out_c,
             w_o=w_o)
    M = dict(norm_g=m_norm_g, w_in=m_w_in, b_gate=m_b_gate, conv_w=m_conv_w, conv_b=m_conv_b, q_a_norm_g=m_q_a_norm_g,
             w_uq=m_w_uq, kv_a_norm_g=m_kv_a_norm_g, w_ukv=m_w_ukv, mla_q_norm_g=m_mla_q_norm_g,
             mla_k_norm_g=m_mla_k_norm_g, dil_q_norm_g=m_dil_q_norm_g, dil_k_norm_g=m_dil_k_norm_g, w_out_a=m_w_out_a,
             w_out_b=m_w_out_b, w_out_c=m_w_out_c, w_o=m_w_o)
    V = dict(norm_g=v_norm_g, w_in=v_w_in, b_gate=v_b_gate, conv_w=v_conv_w, conv_b=v_conv_b, q_a_norm_g=v_q_a_norm_g,
             w_uq=v_w_uq, kv_a_norm_g=v_kv_a_norm_g, w_ukv=v_w_ukv, mla_q_norm_g=v_mla_q_norm_g,
             mla_k_norm_g=v_mla_k_norm_g, dil_q_norm_g=v_dil_q_norm_g, dil_k_norm_g=v_dil_k_norm_g, w_out_a=v_w_out_a,
             w_out_b=v_w_out_b, w_out_c=v_w_out_c, w_o=v_w_o)
    batch = x.shape[0]
    T = batch * S

    def layer_weights(l, cont, gath):
        full = _unpack_weights(gath)
        pad_qk = lambda t: jnp.pad(t, (0, QKP - QK)).reshape(1, QKP)
        full.update(
            w_in_t=_unpack_w_in(cont),
            norm_g=norm_g[l].reshape(1, D), b_gate=b_gate[l].reshape(1, 3 * D), conv_b=conv_b[l].reshape(1, CW),
            q_a_norm_g=q_a_norm_g[l].reshape(1, QL), kv_a_norm_g=kv_a_norm_g[l].reshape(1, KVL),
            mla_q_norm_g=pad_qk(mla_q_norm_g[l]), mla_k_norm_g=pad_qk(mla_k_norm_g[l]),
            dil_q_norm_g=dil_q_norm_g[l].reshape(NG, 1, HD), dil_k_norm_g=dil_k_norm_g[l].reshape(NG, 1, HD))
        return full

    words = [_w_in_words(w_in[l]) for l in range(NL)]
    packs = [_pack_weights({n: W[n][l] for n in BIG[1:] + ("conv_w",)}) for l in range(NL)]
    tabs = _rope_tables() + (_dil_slopes(),)
    x2 = x.reshape(T, D)

    cont0, gath0 = _all_gather(words[0], packs[0])
    w0 = layer_weights(0, cont0, gath0)
    ag = _ag_ici_start(words[1], packs[1], gath0)
    w0["norm_g"] = w0["norm_g"] + ag[6][0:1, 0:1]
    y0, res0 = _layer_fwd(x2, w0, tabs, batch)
    lw, ls = _ag_ici_wait(ag[0], ag[1], ag[2], ag[3], ag[4], ag[5], y0)
    w1 = layer_weights(1, *_ag_finish(words[1], packs[1], lw, ls))
    y1, res1 = _layer_fwd(y0, w1, tabs, batch)

    row = lambda i: (i, 0)
    dy, loss = _pcall("loss", _loss_math, (T // BR,),
                      [(y1, (BR, D), row), (loss_target.reshape(T, D), (BR, D), row)],
                      [((T, D), F32, (BR, D), row), ((1, 1), F32, (1, 1), lambda i: (0, 0), True)])
    loss = lax.psum(loss[0, 0], ("x", "y", "c"))

    grads = [None] * NL
    dy, grads[1] = _layer_bwd(dy, w1, res1, tabs, batch)
    st = [None] * NL
    ex = [None] * NL
    st[1] = _rs_begin(grads[1]["w_in_t"], _pack_grads(grads[1]))
    ex[1] = _rs_exchange_start(st[1]["sw"], st[1]["ss"], "1")
    w0["w_o"] = w0["w_o"] + ex[1][6][0:1, 0:1].astype(BF16)

    def start_layer0(g):
        st[0] = _rs_begin(g["w_in_t"], _pack_grads(g))
        ex[0] = _rs_exchange_start(st[0]["sw"], st[0]["ss"], "0")
        return ex[0][6]

    dx, grads[0] = _layer_bwd(dy, w0, res0, tabs, batch, after_dw=start_layer0)
    grad_x = dx.reshape(batch, S, D)

    red = [None] * NL
    for l in (1, 0):
        r2w, r2s = _rs_exchange_wait(*ex[l][:6], dx, str(l))
        rw, rs = _rs_end(st[l], r2w, r2s)
        r = _unpack_grads(rs)
        r["w_in_t"] = rw
        red[l] = r
    G = {n: jnp.stack([red[l][n] for l in range(NL)]) for n in BIG[1:] + ("conv_w",)}
    g_in_t = jnp.stack([red[l]["w_in_t"] for l in range(NL)])
    G["w_in"] = jnp.swapaxes(g_in_t, 1, 2)
    small_g = {n: jnp.stack([grads[l][n].reshape(-1)[:sz] for l in range(NL)]) for n, sz in SMALL_SIZES}
    small_red = _all_reduce_small(_pack_small(small_g))
    G.update(_unpack_small(small_red, {n: W[n] for n in SMALL}))

    delta, new_m, new_v = {}, {}, {}
    for n in BIG[1:] + ("conv_w",):
        delta[n], new_m[n], new_v[n] = _adamw("adamw_" + n, W[n], G[n], M[n], V[n], ADAM_ROWS[n])
    tr = lambda t: jnp.swapaxes(t, 1, 2)
    delta["w_in"], new_m["w_in"], new_v["w_in"] = (
        tr(t) for t in _adamw("adamw_w_in", tr(w_in), g_in_t, tr(m_w_in), tr(v_w_in), WSH, LANE))
    sw, sm, sv = (_pack_small({n: t[n] for n in SMALL})[None] for t in (W, M, V))
    sd, snm, snv = _adamw("adamw_small", sw, small_red[None], sm, sv, SMALL_ROWS)
    like = {n: W[n] for n in SMALL}
    delta.update(_unpack_small(sd[0], like))
    new_m.update(_unpack_small(snm[0], like))
    new_v.update(_unpack_small(snv[0], like))

    return (loss, grad_x, *[G[n] for n in WEIGHTS], *[delta[n] for n in WEIGHTS],
            *[new_m[n] for n in WEIGHTS], *[new_v[n] for n in WEIGHTS])
```

```python
import functools

import numpy as np
import jax
import jax.numpy as jnp
from jax import lax
from jax.experimental import pallas as pl
from jax.experimental.pallas import tpu as pltpu

F32 = jnp.float32
BF16 = jnp.bfloat16

D = 1024
S = 2048
NL = 2
CW = 512
NH = 8
QL = 256
KVL = 128
NOPE = 64
ROPE = 32
VD = 64
QK = NOPE + ROPE
QKP = 128
ROPE_THETA = 10000.0
DIL = ((128, 1), (512, 4), (2048, 16))
NG = 3
DH = 8
HD = 64
DWID = DH * HD
QB = 128
EPS = 1e-6
NIN = 11168
NINP = 11264
O_A, O_CQ, O_CKV, O_KPE, O_BZ, O_DQ, O_DK, O_DV, O_CZ, O_G = 0, 2048, 2304, 2432, 2560, 3072, 4608, 6144, 7680, 8192
KPE_END = 2464
NEG = -1e30
MLA_SCALE = QK ** -0.5
DIL_SCALE = HD ** -0.5
LANE = 128
PACK_W = 512
VMEM_LIMIT = 48 * 1024 * 1024

ADAM_LR = 0.001
ADAM_B1 = 0.9
ADAM_B2 = 0.999
ADAM_EPS = 1e-08
ADAM_WD = 0.01
ADAM_STEP = 10

MESH = pl.DeviceIdType.MESH
BIG = ("w_in", "w_uq", "w_ukv", "w_out_a", "w_out_b", "w_out_c", "w_o")
SMALL = ("norm_g", "b_gate", "conv_b", "q_a_norm_g", "kv_a_norm_g", "mla_q_norm_g", "mla_k_norm_g",
         "dil_q_norm_g", "dil_k_norm_g")
WEIGHTS = ("norm_g", "w_in", "b_gate", "conv_w", "conv_b", "q_a_norm_g", "w_uq", "kv_a_norm_g", "w_ukv",
           "mla_q_norm_g", "mla_k_norm_g", "dil_q_norm_g", "dil_k_norm_g", "w_out_a", "w_out_b", "w_out_c", "w_o")


def _dot(a, b):
    return jnp.dot(a, b, preferred_element_type=F32)


def _dot_nt(a, b):
    return lax.dot_general(a, b, (((1,), (1,)), ((), ())), preferred_element_type=F32)


def _dot_tn(a, b):
    return lax.dot_general(a, b, (((0,), (0,)), ((), ())), preferred_element_type=F32)


def _pcall(name, fn, grid, ins, outs):
    n_in = len(ins)
    n_out = len(outs)
    acc_axis = len(grid) - 1
    is_acc = [len(o) > 4 and o[4] for o in outs]
    outs = [o[:4] for o in outs]

    def body(*refs):
        vals = fn(*[r[...].astype(F32) for r in refs[:n_in]])
        if not isinstance(vals, (tuple, list)):
            vals = (vals,)
        for k in range(n_out):
            r = refs[n_in + k]
            v = vals[k].astype(r.dtype).reshape(r.shape)
            if is_acc[k]:
                first = pl.program_id(acc_axis) == 0

                @pl.when(first)
                def _():
                    r[...] = v

                @pl.when(jnp.logical_not(first))
                def _():
                    r[...] += v
            else:
                r[...] = v

    return pl.pallas_call(
        body,
        grid=grid,
        in_specs=[pl.BlockSpec(bs, im) for _, bs, im in ins],
        out_specs=[pl.BlockSpec(bs, im) for _, _, bs, im in outs],
        out_shape=[jax.ShapeDtypeStruct(sh, dt) for sh, dt, _, _ in outs],
        name=name,
        compiler_params=pltpu.CompilerParams(
            dimension_semantics=("arbitrary",) * len(grid), vmem_limit_bytes=VMEM_LIMIT),
    )(*[a for a, _, _ in ins])


def _mm(name, a, b, *, ta=False, tb=False, out_dtype=F32, add=None, dep=None, tm=2048, tn=1024, tk=1024):
    if ta:
        K, M = a.shape
    else:
        M, K = a.shape
    if tb:
        N, K2 = b.shape
    else:
        K2, N = b.shape
    assert K == K2, (name, a.shape, b.shape)
    tm, tn, tk = min(tm, M), min(tn, N), min(tk, K)
    assert M % tm == 0 and N % tn == 0 and K % tk == 0, (name, M, N, K)
    nk = K // tk
    dims = (((0 if ta else 1,), (1 if tb else 0,)), ((), ()))
    a_spec = pl.BlockSpec((tk, tm), lambda j, i, k: (k, i)) if ta else pl.BlockSpec((tm, tk), lambda j, i, k: (i, k))
    b_spec = pl.BlockSpec((tn, tk), lambda j, i, k: (j, k)) if tb else pl.BlockSpec((tk, tn), lambda j, i, k: (k, j))
    o_spec = pl.BlockSpec((tm, tn), lambda j, i, k: (i, j))
    has_add = add is not None
    n_in = 2 + has_add + (dep is not None)

    def body(*refs):
        a_ref, b_ref = refs[0], refs[1]
        add_ref = refs[2] if has_add else None
        o_ref = refs[n_in]
        p = lax.dot_general(a_ref[...].astype(BF16), b_ref[...].astype(BF16), dims, preferred_element_type=F32)
        if nk == 1:
            if has_add:
                p = p + add_ref[...]
            o_ref[...] = p.astype(out_dtype)
        else:
            acc = refs[-1]
            k = pl.program_id(2)

            @pl.when(k == 0)
            def _():
                acc[...] = p

            @pl.when(k > 0)
            def _():
                acc[...] += p

            @pl.when(k == nk - 1)
            def _():
                r = acc[...]
                if has_add:
                    r = r + add_ref[...]
                o_ref[...] = r.astype(out_dtype)

    in_specs = [a_spec, b_spec] + ([o_spec] if has_add else []) + ([pl.BlockSpec(memory_space=pl.ANY)] if dep is not None else [])
    args = [a, b] + ([add] if has_add else []) + ([dep] if dep is not None else [])
    return pl.pallas_call(
        body,
        grid=(N // tn, M // tm, nk),
        in_specs=in_specs,
        out_specs=o_spec,
        out_shape=jax.ShapeDtypeStruct((M, N), out_dtype),
        scratch_shapes=[pltpu.VMEM((tm, tn), F32)] if nk > 1 else [],
        name=name,
        compiler_params=pltpu.CompilerParams(
            dimension_semantics=("arbitrary", "arbitrary", "arbitrary"), vmem_limit_bytes=VMEM_LIMIT),
    )(*args)


def _vjp_of(f, n_diff):
    def g(*args, n_prim):
        prim = args[:n_diff]
        consts = args[n_diff:n_prim]
        cts = args[n_prim:]
        _, pull = jax.vjp(lambda *p: f(*p, *consts), *prim)
        out = jax.eval_shape(lambda *p: f(*p, *consts), *prim)
        if isinstance(out, (tuple, list)):
            cts = tuple(c.astype(o.dtype) for c, o in zip(cts, out))
        else:
            cts = cts[0].astype(out.dtype)
        return pull(cts)
    return g


def _rms(x, g, n=None):
    n = x.shape[-1] if n is None else n
    ms = jnp.sum(x * x, axis=-1, keepdims=True) / n
    return x * lax.rsqrt(ms + EPS) * g


def _silu(z):
    return z * jax.nn.sigmoid(z)


def _roll_rows(u, k):
    n = u.shape[0]
    r = pltpu.roll(u, k % n, 0)
    t = lax.broadcasted_iota(jnp.int32, u.shape, 0)
    if k > 0:
        return jnp.where(t >= k, r, 0.0)
    return jnp.where(t < n + k, r, 0.0)


@functools.partial(jax.custom_vjp, nondiff_argnums=(1,))
def _shift(u, k):
    return _roll_rows(u, k)


def _shift_fwd(u, k):
    return _roll_rows(u, k), None


def _shift_bwd(k, _, g):
    return (_roll_rows(g, -k),)


_shift.defvjp(_shift_fwd, _shift_bwd)


@functools.partial(jax.custom_vjp, nondiff_argnums=(1,))
def _lane_roll(u, k):
    return pltpu.roll(u, k % LANE, 1)


def _lane_roll_fwd(u, k):
    return pltpu.roll(u, k % LANE, 1), None


def _lane_roll_bwd(k, _, g):
    return (pltpu.roll(g, (-k) % LANE, 1),)


_lane_roll.defvjp(_lane_roll_fwd, _lane_roll_bwd)


def _conv_math(ab, ac, ax, az, cw, cb):
    u = ac * ax
    conv = cb + _shift(u, 2) * cw[0:1] + _shift(u, 1) * cw[1:2] + u * cw[2:3]
    return ab * conv * _silu(az)


def _mla_pre_math(cq, ckv, gq, gkv):
    return _rms(cq, gq), _rms(ckv, gkv)


def _rope_math(q, kn, kpe, gq, gk, c, s1, s2):
    lane = lax.broadcasted_iota(jnp.int32, kpe.shape, 1)
    pe = _lane_roll(jnp.where(lane < ROPE, kpe, 0.0), NOPE)

    def one(t, g):
        tn = _rms(t, g, QK)
        return tn * c + _lane_roll(tn, -16) * s1 + _lane_roll(tn, 16) * s2

    qs, ks = [], []
    for h in range(NH):
        sl = slice(h * QKP, (h + 1) * QKP)
        qs.append(one(q[:, sl], gq))
        ks.append(one(kn[:, sl] + pe, gk))
    return jnp.concatenate(qs, axis=1), jnp.concatenate(ks, axis=1)


def _gate_math(o, z):
    return o * _silu(z)


def _mergec_math(o0, o1, o2, l0, l1, l2, cz):
    m = lax.stop_gradient(jnp.maximum(jnp.maximum(l0, l1), l2))
    e0, e1, e2 = jnp.exp(l0 - m), jnp.exp(l1 - m), jnp.exp(l2 - m)
    den = e0 + e1 + e2
    oc = (e0 / den) * o0 + (e1 / den) * o1 + (e2 / den) * o2
    return oc * _silu(cz)


def _merge_math(g0, g1, g2, b0, b1, b2, pa, pb, pc):
    return (jax.nn.sigmoid(g0 + b0) * pa + jax.nn.sigmoid(g1 + b1) * pb) + jax.nn.sigmoid(g2 + b2) * pc


MLA_T = 256
MLA_UNROLL = True


def _mla_fwd(q, k, v):
    B = q.shape[0]
    T = MLA_T
    NB = S // T

    def body(q_ref, k_ref, v_ref, o_ref, l_ref):
        row = lax.broadcasted_iota(jnp.int32, (T, T), 0)
        col = lax.broadcasted_iota(jnp.int32, (T, T), 1)
        lo = _lo_mask((T, LANE))

        for qi in range(NB):
            qb = q_ref[qi * T:(qi + 1) * T, :]

            def step(j, carry, diagonal):
                m, l, acc = carry
                off = pl.multiple_of(j * T, T)
                kb = k_ref[pl.ds(off, T), :]
                vb = v_ref[pl.ds(off, T), :]
                ss = []
                for e in (0, 1):
                    se = _dot_nt(qb[:, e * QKP:(e + 1) * QKP], kb[:, e * QKP:(e + 1) * QKP]) * MLA_SCALE
                    ss.append(jnp.where(col <= row, se, NEG) if diagonal else se)
                s = jnp.concatenate(ss, axis=0)
                m_new = jnp.maximum(m, jnp.max(s, axis=-1, keepdims=True))
                a = jnp.exp(m - m_new)
                p = jnp.exp(s - m_new)
                l = a * l + jnp.sum(p, axis=-1, keepdims=True)
                acc = a * acc + _dot(p.astype(BF16), vb)
                return m_new, l, acc

            init = (jnp.full((2 * T, 1), NEG, F32), jnp.zeros((2 * T, 1), F32), jnp.zeros((2 * T, LANE), F32))
            carry = lax.fori_loop(0, qi, functools.partial(step, diagonal=False), init, unroll=MLA_UNROLL)
            m, l, acc = step(qi, carry, True)
            o = acc / l
            lse = m + jnp.log(l)
            o_ref[qi * T:(qi + 1) * T, :] = jnp.where(lo, o[:T], o[T:])
            l_ref[qi * T:(qi + 1) * T, :] = jnp.where(lo, lse[:T], lse[T:])

    def spec(w):
        return pl.BlockSpec((None, S, w), lambda b, hp: (b, 0, hp))

    return pl.pallas_call(
        body,
        grid=(B, NH // 2),
        in_specs=[spec(2 * QKP), spec(2 * QKP), spec(LANE)],
        out_specs=[spec(LANE), spec(LANE)],
        out_shape=[jax.ShapeDtypeStruct((B, S, NH * VD), F32)] * 2,
        name="mla_attn_fwd",
        compiler_params=pltpu.CompilerParams(dimension_semantics=("arbitrary",) * 2, vmem_limit_bytes=VMEM_LIMIT),
    )(q, k, v)


def _mla_bwd(q, k, v, do, o, lse):
    B = q.shape[0]
    T = MLA_T
    NB = S // T

    def body(q_ref, k_ref, v_ref, do_ref, o_ref, l_ref, dq_ref, dk_ref, dv_ref, delta_ref):
        delta_ref[...] = _head_sum(do_ref[...] * o_ref[...])
        row = lax.broadcasted_iota(jnp.int32, (T, T), 0)
        col = lax.broadcasted_iota(jnp.int32, (T, T), 1)
        lo = _lo_mask((T, LANE))

        for j in range(NB):
            krows = slice(j * T, (j + 1) * T)
            kb = k_ref[krows, :]
            vb = v_ref[krows, :]
            dk = [jnp.zeros((T, QKP), F32), jnp.zeros((T, QKP), F32)]
            dv = jnp.zeros((T, LANE), F32)
            for i in range(j, NB):
                qrows = slice(i * T, (i + 1) * T)
                qb = q_ref[qrows, :]
                do2 = _stack_heads(do_ref[qrows, :], lo).astype(BF16)
                lb = l_ref[qrows, :]
                db = delta_ref[qrows, :]
                dp2 = _dot_nt(do2, vb)
                for e in (0, 1):
                    cols = slice(e * QKP, (e + 1) * QKP)
                    qe, ke = qb[:, cols], kb[:, cols]
                    s = _dot_nt(qe, ke) * MLA_SCALE
                    if i == j:
                        s = jnp.where(col <= row, s, NEG)
                    p = jnp.exp(s - lb[:, e * HD:e * HD + 1])
                    dv = dv + _dot_tn(p.astype(BF16), do2[e * T:(e + 1) * T])
                    ds = (p * (dp2[e * T:(e + 1) * T] - db[:, e * HD:e * HD + 1]) * MLA_SCALE).astype(BF16)
                    dk[e] = dk[e] + _dot_tn(ds, qe)
                    if j == 0:
                        dq_ref[qrows, cols] = _dot(ds, ke)
                    else:
                        dq_ref[qrows, cols] += _dot(ds, ke)
            dk_ref[krows, 0:QKP] = dk[0]
            dk_ref[krows, QKP:2 * QKP] = dk[1]
            dv_ref[krows, :] = dv

    def spec(w):
        return pl.BlockSpec((None, S, w), lambda b, hp: (b, 0, hp))

    return pl.pallas_call(
        body,
        grid=(B, NH // 2),
        in_specs=[spec(2 * QKP), spec(2 * QKP), spec(LANE), spec(LANE), spec(LANE), spec(LANE)],
        out_specs=[spec(2 * QKP), spec(2 * QKP), spec(LANE)],
        out_shape=[jax.ShapeDtypeStruct((B, S, NH * QKP), F32), jax.ShapeDtypeStruct((B, S, NH * QKP), F32),
                   jax.ShapeDtypeStruct((B, S, NH * VD), F32)],
        scratch_shapes=[pltpu.VMEM((S, LANE), F32)],
        name="mla_attn_bwd",
        compiler_params=pltpu.CompilerParams(dimension_semantics=("arbitrary",) * 2, vmem_limit_bytes=VMEM_LIMIT),
    )(q, k, v, do, o, lse)


def _lo_mask(shape):
    return lax.broadcasted_iota(jnp.int32, shape, len(shape) - 1) < HD


def _head_sum(u):
    r = lax.broadcasted_iota(jnp.int32, (LANE, LANE), 0) < HD
    c = lax.broadcasted_iota(jnp.int32, (LANE, LANE), 1) < HD
    ones = jnp.where(r == c, 1.0, 0.0).astype(BF16)
    hi = u.astype(BF16)
    lo = (u - hi.astype(F32)).astype(BF16)
    return _dot(hi, ones) + _dot(lo, ones)


def _head_sum_1(u):
    r = lax.broadcasted_iota(jnp.int32, (LANE, LANE), 0) < HD
    c = lax.broadcasted_iota(jnp.int32, (LANE, LANE), 1) < HD
    return _dot(u.astype(BF16), jnp.where(r == c, 1.0, 0.0).astype(BF16))


def _rms2_scale(x):
    return lax.rsqrt(_head_sum(x * x) / HD + EPS)


def _rms2(x, g):
    return x * _rms2_scale(x) * g


def _rms2_bwd(x, r, g, dy):
    xn = x * r
    t = dy * g
    dx = r * (t - xn * (_head_sum_1(xn * t) * (1.0 / HD)))
    return dx, jnp.sum(dy * xn, axis=0, keepdims=True)


def _dil_bias(t_ref, gi, d):
    qq = lax.broadcasted_iota(jnp.int32, (QB, QB), 0)
    kk = lax.broadcasted_iota(jnp.int32, (QB, QB), 1)
    jc = (qq - kk).astype(F32)
    rows = []
    for e in (0, 1):
        sl = t_ref[2 * gi + e:2 * gi + e + 1, :] * float(d)
        bp = jnp.where(kk >= qq, -sl * (jc + float(QB)), NEG)
        bc = jnp.where(kk <= qq, -sl * jc, NEG)
        rows.append(jnp.concatenate([bp, bc], axis=1))
    return jnp.concatenate(rows, axis=0)


def _dil_rows(cur, d):
    return pl.ds(cur, QB, stride=d) if d > 1 else pl.ds(pl.multiple_of(cur, QB), QB)


def _dil_walk(d, block, full):
    if d == 1:
        block(0, None)

        def body(i, c):
            block(i * QB, (i - 1) * QB)
            return c
        lax.fori_loop(1, S // QB, body, 0, unroll=True if full else 5)
    elif d == 16:
        def body(r, c):
            block(r, None)
            return c
        lax.fori_loop(0, d, body, 0, unroll=True if full else 4)
    else:
        nb = S // d // QB

        def cls(r, c):
            block(r, None)

            def body(i, c2):
                block(r + i * QB * d, r + (i - 1) * QB * d)
                return c2
            lax.fori_loop(1, nb, body, 0, unroll=True)
            return c
        lax.fori_loop(0, d, cls, 0, unroll=full)


def _stack_heads(x, lo):
    return jnp.concatenate([jnp.where(lo, x, 0.0), jnp.where(lo, 0.0, x)], axis=0)


def _dilc_fwd(proj3, gq, gk, tab):
    B = proj3.shape[0]

    def body(q_ref, k_ref, v_ref, cz_ref, gq_ref, gk_ref, t_ref, y_ref, o_ref, l_ref, qs, ks, vs):
        g = pl.program_id(2)
        lo = _lo_mask((QB, LANE))

        def group(gi):
            d = DIL[gi][1]
            qs[...] = _rms2(q_ref[...].astype(F32), gq_ref[gi:gi + 1, :])
            ks[...] = _rms2(k_ref[...].astype(F32), gk_ref[gi:gi + 1, :])
            vs[...] = v_ref[...].astype(F32)
            bias = _dil_bias(t_ref, gi, d)

            def block(cur, prev):
                rows = _dil_rows(cur, d)
                q2 = _stack_heads(qs[rows, :], lo).astype(BF16)
                kc, vc = ks[rows, :], vs[rows, :]
                if prev is None:
                    kcat, vcat, b = kc, vc, bias[:, QB:]
                else:
                    prow = _dil_rows(prev, d)
                    kcat = jnp.concatenate([ks[prow, :], kc], axis=0)
                    vcat = jnp.concatenate([vs[prow, :], vc], axis=0)
                    b = bias
                s = _dot_nt(q2, kcat.astype(BF16)) * DIL_SCALE + b
                m = jnp.max(s, axis=-1, keepdims=True)
                p = jnp.exp(s - m)
                l = jnp.sum(p, axis=-1, keepdims=True)
                o = _dot(p.astype(BF16), vcat.astype(BF16)) / l
                lse = m + jnp.log(l)
                o_ref[gi, rows, :] = jnp.where(lo, o[:QB], o[QB:])
                l_ref[gi, rows, :] = jnp.where(lo, lse[:QB], lse[QB:])

            _dil_walk(d, block, True)

        for gi in range(NG):
            pl.when(g == gi)(functools.partial(group, gi))

        @pl.when(g == NG - 1)
        def _():
            y_ref[...] = _mergec_math(o_ref[0], o_ref[1], o_ref[2], l_ref[0], l_ref[1], l_ref[2],
                                      cz_ref[...].astype(F32)).astype(BF16)

    def col(base):
        return pl.BlockSpec((None, S, LANE), lambda b, hp, g: (b, 0, base // LANE + 4 * g + hp))

    gspec = pl.BlockSpec((NG, LANE), lambda b, hp, g: (0, 0))
    saved = pl.BlockSpec((NG, None, S, LANE), lambda b, hp, g: (0, b, 0, hp))
    return pl.pallas_call(
        body,
        grid=(B, 4, NG),
        in_specs=[col(O_DQ), col(O_DK), col(O_DV),
                  pl.BlockSpec((None, S, LANE), lambda b, hp, g: (b, 0, O_CZ // LANE + hp)),
                  gspec, gspec, pl.BlockSpec((None, 8, LANE), lambda b, hp, g: (hp, 0, 0))],
        out_specs=[pl.BlockSpec((None, S, LANE), lambda b, hp, g: (b, 0, hp)), saved, saved],
        out_shape=[jax.ShapeDtypeStruct((B, S, DWID), BF16), jax.ShapeDtypeStruct((NG, B, S, DWID), F32),
                   jax.ShapeDtypeStruct((NG, B, S, DWID), F32)],
        scratch_shapes=[pltpu.VMEM((S, LANE), F32)] * 3,
        name="dil_mixer_fwd",
        compiler_params=pltpu.CompilerParams(dimension_semantics=("arbitrary",) * 3, vmem_limit_bytes=VMEM_LIMIT),
    )(proj3, proj3, proj3, proj3, gq, gk, tab)


MERGE_ROWS = 256


def _dilc_bwd(proj3, gq, gk, tab, o_all, l_all, d_yc):
    B = proj3.shape[0]

    def body(q_ref, k_ref, v_ref, cz_ref, gq_ref, gk_ref, t_ref, o_ref, l_ref, dy_ref,
             dq_out, dk_out, dv_out, dcz_out, dgq_out, dgk_out, qs, ks, vs, dos, dls, dqs, dks, dvs, rqs, rks):
        g = pl.program_id(2)
        lo = _lo_mask((QB, LANE))

        @pl.when(jnp.logical_and(jnp.logical_and(pl.program_id(0) == 0, pl.program_id(1) == 0), g == 0))
        def _():
            dgq_out[...] = jnp.zeros((NG, LANE), F32)
            dgk_out[...] = jnp.zeros((NG, LANE), F32)

        @pl.when(g == 0)
        def _():
            def chunk(i, carry):
                rows = pl.ds(pl.multiple_of(i * MERGE_ROWS, MERGE_ROWS), MERGE_ROWS)
                ls = [l_ref[j, rows, :] for j in range(NG)]
                m = jnp.maximum(jnp.maximum(ls[0], ls[1]), ls[2])
                es = [jnp.exp(t - m) for t in ls]
                den = (es[0] + es[1]) + es[2]
                al = [e / den for e in es]
                os_ = [o_ref[j, rows, :] for j in range(NG)]
                oc = (al[0] * os_[0] + al[1] * os_[1]) + al[2] * os_[2]
                cz = cz_ref[rows, :].astype(F32)
                sg = jax.nn.sigmoid(cz)
                dy = dy_ref[rows, :]
                d_oc = dy * (cz * sg)
                dcz_out[rows, :] = (dy * oc * (sg * (1.0 + cz * (1.0 - sg)))).astype(BF16)
                ts = [_head_sum_1(d_oc * os_[j]) for j in range(NG)]
                tbar = (al[0] * ts[0] + al[1] * ts[1]) + al[2] * ts[2]
                for j in range(NG):
                    dos[j, rows, :] = al[j] * d_oc
                    dls[j, rows, :] = al[j] * (ts[j] - tbar)
                return carry
            lax.fori_loop(0, S // MERGE_ROWS, chunk, 0)

        def group(gi):
            d = DIL[gi][1]
            xq, xk = q_ref[...].astype(F32), k_ref[...].astype(F32)
            rqs[...] = _rms2_scale(xq)
            rks[...] = _rms2_scale(xk)
            qs[...] = xq * rqs[...] * gq_ref[gi:gi + 1, :]
            ks[...] = xk * rks[...] * gk_ref[gi:gi + 1, :]
            vs[...] = v_ref[...].astype(F32)
            dks[...] = jnp.zeros((S, LANE), F32)
            dvs[...] = jnp.zeros((S, LANE), F32)
            bias = _dil_bias(t_ref, gi, d)

            def block(cur, prev):
                rows = _dil_rows(cur, d)
                q2 = _stack_heads(qs[rows, :], lo).astype(BF16)
                dob = dos[gi, rows, :]
                do2 = _stack_heads(dob, lo).astype(BF16)
                kc, vc = ks[rows, :], vs[rows, :]
                if prev is None:
                    kcat, vcat, b = kc, vc, bias[:, QB:]
                else:
                    prow = _dil_rows(prev, d)
                    kcat = jnp.concatenate([ks[prow, :], kc], axis=0)
                    vcat = jnp.concatenate([vs[prow, :], vc], axis=0)
                    b = bias
                kcat = kcat.astype(BF16)
                vcat = vcat.astype(BF16)
                lse_b = l_ref[gi, rows, :]
                corr_b = dls[gi, rows, :] - _head_sum_1(dob * o_ref[gi, rows, :])
                lse2 = jnp.concatenate([lse_b[:, 0:1], lse_b[:, HD:HD + 1]], axis=0)
                corr2 = jnp.concatenate([corr_b[:, 0:1], corr_b[:, HD:HD + 1]], axis=0)
                s = _dot_nt(q2, kcat) * DIL_SCALE + b
                p = jnp.exp(s - lse2)
                ds = (p * (_dot_nt(do2, vcat) + corr2) * DIL_SCALE).astype(BF16)
                dq2 = _dot(ds, kcat)
                dqs[rows, :] = jnp.where(lo, dq2[:QB], dq2[QB:])
                dk = _dot_tn(ds, q2)
                dv = _dot_tn(p.astype(BF16), do2)
                if prev is None:
                    dks[rows, :] += dk
                    dvs[rows, :] += dv
                else:
                    dks[prow, :] += dk[:QB]
                    dvs[prow, :] += dv[:QB]
                    dks[rows, :] += dk[QB:]
                    dvs[rows, :] += dv[QB:]

            _dil_walk(d, block, False)

            dxq, dgq = _rms2_bwd(q_ref[...].astype(F32), rqs[...], gq_ref[gi:gi + 1, :], dqs[...])
            dq_out[...] = dxq.astype(BF16)
            dgq_out[gi:gi + 1, :] += dgq
            dxk, dgk = _rms2_bwd(k_ref[...].astype(F32), rks[...], gk_ref[gi:gi + 1, :], dks[...])
            dk_out[...] = dxk.astype(BF16)
            dgk_out[gi:gi + 1, :] += dgk
            dv_out[...] = dvs[...].astype(BF16)

        for gi in range(NG):
            pl.when(g == gi)(functools.partial(group, gi))

    def col(base):
        return pl.BlockSpec((None, S, LANE), lambda b, hp, g: (b, 0, base // LANE + 4 * g + hp))

    gspec = pl.BlockSpec((NG, LANE), lambda b, hp, g: (0, 0))
    saved = pl.BlockSpec((NG, None, S, LANE), lambda b, hp, g: (0, b, 0, hp))
    per_pair = pl.BlockSpec((None, S, LANE), lambda b, hp, g: (b, 0, hp))
    dcol = pl.BlockSpec((None, S, LANE), lambda b, hp, g: (b, 0, 4 * g + hp))
    return pl.pallas_call(
        body,
        grid=(B, 4, NG),
        in_specs=[col(O_DQ), col(O_DK), col(O_DV),
                  pl.BlockSpec((None, S, LANE), lambda b, hp, g: (b, 0, O_CZ // LANE + hp)),
                  gspec, gspec, pl.BlockSpec((None, 8, LANE), lambda b, hp, g: (hp, 0, 0)),
                  saved, saved, per_pair],
        out_specs=[dcol, dcol, dcol, per_pair, gspec, gspec],
        out_shape=[jax.ShapeDtypeStruct((B, S, NG * DWID), BF16)] * 3
        + [jax.ShapeDtypeStruct((B, S, DWID), BF16), jax.ShapeDtypeStruct((NG, LANE), F32),
           jax.ShapeDtypeStruct((NG, LANE), F32)],
        scratch_shapes=[pltpu.VMEM((S, LANE), F32)] * 3 + [pltpu.VMEM((NG, S, LANE), F32)] * 2
        + [pltpu.VMEM((S, LANE), F32)] * 5,
        name="dil_mixer_bwd",
        compiler_params=pltpu.CompilerParams(dimension_semantics=("arbitrary",) * 3, vmem_limit_bytes=VMEM_LIMIT),
    )(proj3, proj3, proj3, proj3, gq, gk, tab, o_all, l_all, d_yc)


def _dil_slopes():
    slopes = (2.0 ** (-8.0 * np.arange(1, NG * DH + 1, dtype=np.float32) / (NG * DH))).astype(np.float32).reshape(NG, DH)
    tab = np.zeros((4, 8, LANE), np.float32)
    for hp in range(4):
        for gi in range(NG):
            for e in (0, 1):
                tab[hp, 2 * gi + e, :] = slopes[gi, 2 * hp + e]
    return jnp.asarray(tab)


def _rope_tables():
    inv = ROPE_THETA ** (-jnp.arange(0, ROPE, 2, dtype=F32) / ROPE)
    ang = jnp.arange(S, dtype=F32)[:, None] * inv[None, :]
    cos, sin = jnp.cos(ang), jnp.sin(ang)
    z16 = jnp.zeros((S, 16), F32)
    c = jnp.concatenate([jnp.ones((S, NOPE), F32), cos, cos, jnp.zeros((S, 32), F32)], axis=1)
    s1 = jnp.concatenate([jnp.zeros((S, NOPE), F32), -sin, z16, jnp.zeros((S, 32), F32)], axis=1)
    s2 = jnp.concatenate([jnp.zeros((S, NOPE), F32), z16, sin, jnp.zeros((S, 32), F32)], axis=1)
    return c, s1, s2


def _pad_heads_uq(w):
    return jnp.pad(w.reshape(QL, NH, QK), ((0, 0), (0, 0), (0, QKP - QK))).reshape(QL, NH * QKP)


def _unpad_heads_uq(g):
    return g.reshape(QL, NH, QKP)[:, :, :QK].reshape(QL, NH * QK)


def _split_ukv(w):
    w3 = w.reshape(KVL, NH, NOPE + VD)
    uk = jnp.pad(w3[:, :, :NOPE], ((0, 0), (0, 0), (0, QKP - NOPE))).reshape(KVL, NH * QKP)
    return uk, w3[:, :, NOPE:].reshape(KVL, NH * VD)


def _join_ukv(guk, guv):
    return jnp.concatenate([guk.reshape(KVL, NH, QKP)[:, :, :NOPE], guv.reshape(KVL, NH, VD)],
                           axis=-1).reshape(KVL, NH * (NOPE + VD))


BR = 512
BRM = 256


def _layer_fwd(x, w, tabs, batch):
    T = batch * S
    rope_c, rope_s1, rope_s2, dil_tab = tabs
    res = {"x": x}
    row = lambda c: (lambda i: (i, c))
    fix = lambda i: (0, 0)

    h = _pcall("norm_fwd", _rms, (T // BR,),
               [(x, (BR, D), row(0)), (w["norm_g"], (1, D), fix)],
               [((T, D), BF16, (BR, D), row(0))])[0]
    proj = _mm("in_proj", h, w["w_in_t"], tb=True, out_dtype=BF16, tm=2048, tn=1024)
    res["h"], res["proj"] = h, proj
    proj3 = proj.reshape(batch, S, NINP)

    cblk = lambda s: (lambda j, b: (b, 0, 4 * s + j))
    y_a = _pcall("conv_fwd", _conv_math, (4, batch),
                 [(proj3, (None, S, LANE), cblk(0)), (proj3, (None, S, LANE), cblk(1)),
                  (proj3, (None, S, LANE), cblk(2)), (proj3, (None, S, LANE), cblk(3)),
                  (w["conv_w"], (3, LANE), lambda j, b: (0, j)), (w["conv_b"], (1, LANE), lambda j, b: (0, j))],
                 [((batch, S, CW), BF16, (None, S, LANE), lambda j, b: (b, 0, j))])[0].reshape(T, CW)
    res["y_a"] = y_a

    cqn, ckvn = _pcall("mla_pre_fwd", _mla_pre_math, (T // BR,),
                       [(proj, (BR, QL), row(O_CQ // QL)), (proj, (BR, KVL), row(O_CKV // KVL)),
                        (w["q_a_norm_g"], (1, QL), fix), (w["kv_a_norm_g"], (1, KVL), fix)],
                       [((T, QL), BF16, (BR, QL), row(0)), ((T, KVL), BF16, (BR, KVL), row(0))])
    w_uq_p = _pad_heads_uq(w["w_uq"])
    w_uk, w_uv = _split_ukv(w["w_ukv"])
    q = _mm("uq", cqn, w_uq_p, out_dtype=BF16)
    kn = _mm("uk", ckvn, w_uk, out_dtype=BF16)
    v = _mm("uv", ckvn, w_uv, out_dtype=BF16)
    nrr = S // BR
    tab_row = lambda i: (i % nrr, 0)
    qr, kr = _pcall("rope_fwd", _rope_math, (T // BR,),
                    [(q, (BR, NH * QKP), row(0)), (kn, (BR, NH * QKP), row(0)), (proj, (BR, LANE), row(O_KPE // LANE)),
                     (w["mla_q_norm_g"], (1, QKP), fix), (w["mla_k_norm_g"], (1, QKP), fix),
                     (rope_c, (BR, QKP), tab_row), (rope_s1, (BR, QKP), tab_row), (rope_s2, (BR, QKP), tab_row)],
                    [((T, NH * QKP), BF16, (BR, NH * QKP), row(0))] * 2)
    qr = qr.reshape(batch, S, NH * QKP)
    kr = kr.reshape(batch, S, NH * QKP)
    v = v.reshape(batch, S, NH * VD)
    o_b, l_b = _mla_fwd(qr, kr, v)
    ob2 = o_b.reshape(T, NH * VD)
    y_b = _pcall("gateb_fwd", _gate_math, (T // BR,),
                 [(ob2, (BR, 512), row(0)), (proj, (BR, 512), row(O_BZ // 512))],
                 [((T, 512), BF16, (BR, 512), row(0))])[0]
    res.update(cqn=cqn, ckvn=ckvn, q=q, kn=kn, qr=qr, kr=kr, v=v, o_b=o_b, l_b=l_b, ob2=ob2, y_b=y_b,
               w_uq_p=w_uq_p, w_uk=w_uk, w_uv=w_uv)

    gq2 = jnp.tile(w["dil_q_norm_g"].reshape(NG, HD), (1, 2))
    gk2 = jnp.tile(w["dil_k_norm_g"].reshape(NG, HD), (1, 2))
    y_c, o_all, l_all = _dilc_fwd(proj3, gq2, gk2, dil_tab)
    y_c = y_c.reshape(T, DWID)
    res.update(o_all=o_all, l_all=l_all, y_c=y_c)

    pa = _mm("out_a", y_a, w["w_out_a"], out_dtype=BF16)
    pb = _mm("out_b", y_b, w["w_out_b"], out_dtype=BF16)
    pc = _mm("out_c", y_c, w["w_out_c"], out_dtype=BF16)
    merged = _pcall("merge_fwd", _merge_math, (T // BRM,),
                    [(proj, (BRM, D), row(O_G // D + s)) for s in range(3)]
                    + [(w["b_gate"], (1, D), (lambda s: (lambda i: (0, s)))(s)) for s in range(3)]
                    + [(t, (BRM, D), row(0)) for t in (pa, pb, pc)],
                    [((T, D), BF16, (BRM, D), row(0))])[0]
    out = _mm("o_proj", merged, w["w_o"], add=x, tm=1024)
    res.update(pa=pa, pb=pb, pc=pc, merged=merged)
    return out, res


def _norm_bwd_math(x, g, dh, dy):
    _, pull = jax.vjp(_rms, x, g)
    dx, dg = pull(dh)
    return dx + dy, dg


def _layer_bwd(dy, w, res, tabs, batch, after_dw=None):
    T = batch * S
    rope_c, rope_s1, rope_s2, dil_tab = tabs
    row = lambda c: (lambda i: (i, c))
    fix = lambda i: (0, 0)
    x, proj, h = res["x"], res["proj"], res["h"]
    proj3 = proj.reshape(batch, S, NINP)
    g = {}

    d_merged = _mm("o_proj_dx", dy, w["w_o"], tb=True)
    g["w_o"] = _mm("o_proj_dw", res["merged"], dy, ta=True, tm=1024, tk=2048)

    merge_bwd = functools.partial(_vjp_of(_merge_math, 9), n_prim=9)
    dg0, dg1, dg2, db0, db1, db2, dpa, dpb, dpc = _pcall(
        "merge_bwd", merge_bwd, (T // BRM,),
        [(proj, (BRM, D), row(O_G // D + s)) for s in range(3)]
        + [(w["b_gate"], (1, D), (lambda s: (lambda i: (0, s)))(s)) for s in range(3)]
        + [(t, (BRM, D), row(0)) for t in (res["pa"], res["pb"], res["pc"])]
        + [(d_merged, (BRM, D), row(0))],
        [((T, D), BF16, (BRM, D), row(0))] * 3 + [((1, D), F32, (1, D), fix, True)] * 3
        + [((T, D), BF16, (BRM, D), row(0))] * 3)
    g["b_gate"] = jnp.concatenate([db0, db1, db2], axis=1)

    d_ya = _mm("out_a_dx", dpa, w["w_out_a"], tb=True)
    d_yb = _mm("out_b_dx", dpb, w["w_out_b"], tb=True)
    d_yc = _mm("out_c_dx", dpc, w["w_out_c"], tb=True)
    g["w_out_a"] = _mm("out_a_dw", res["y_a"], dpa, ta=True, tk=T)
    g["w_out_b"] = _mm("out_b_dw", res["y_b"], dpb, ta=True, tk=T)
    g["w_out_c"] = _mm("out_c_dw", res["y_c"], dpc, ta=True, tk=T)

    cblk = lambda s: (lambda j, b: (b, 0, 4 * s + j))
    oblk = lambda j, b: (b, 0, j)
    conv_bwd = functools.partial(_vjp_of(_conv_math, 6), n_prim=6)
    d_ab, d_ac, d_ax, d_az, g["conv_w"], g["conv_b"] = _pcall(
        "conv_bwd", conv_bwd, (4, batch),
        [(proj3, (None, S, LANE), cblk(s)) for s in range(4)]
        + [(w["conv_w"], (3, LANE), lambda j, b: (0, j)), (w["conv_b"], (1, LANE), lambda j, b: (0, j)),
           (d_ya.reshape(batch, S, CW), (None, S, LANE), oblk)],
        [((batch, S, CW), BF16, (None, S, LANE), oblk)] * 4
        + [((3, CW), F32, (3, LANE), lambda j, b: (0, j), True), ((1, CW), F32, (1, LANE), lambda j, b: (0, j), True)])

    gate_bwd = functools.partial(_vjp_of(_gate_math, 2), n_prim=2)
    d_ob, d_bz = _pcall("gateb_bwd", gate_bwd, (T // BR,),
                        [(res["ob2"], (BR, 512), row(0)), (proj, (BR, 512), row(O_BZ // 512)), (d_yb, (BR, 512), row(0))],
                        [((T, 512), F32, (BR, 512), row(0)), ((T, 512), BF16, (BR, 512), row(0))])
    dqr, dkr, dv = _mla_bwd(res["qr"], res["kr"], res["v"], d_ob.reshape(batch, S, NH * VD), res["o_b"], res["l_b"])
    nrr = S // BR
    tab_row = lambda i: (i % nrr, 0)
    rope_bwd = functools.partial(_vjp_of(_rope_math, 5), n_prim=8)
    d_q, d_kn, d_kpe_p, g["mla_q_norm_g"], g["mla_k_norm_g"] = _pcall(
        "rope_bwd", rope_bwd, (T // BR,),
        [(res["q"], (BR, NH * QKP), row(0)), (res["kn"], (BR, NH * QKP), row(0)), (proj, (BR, LANE), row(O_KPE // LANE)),
         (w["mla_q_norm_g"], (1, QKP), fix), (w["mla_k_norm_g"], (1, QKP), fix),
         (rope_c, (BR, QKP), tab_row), (rope_s1, (BR, QKP), tab_row), (rope_s2, (BR, QKP), tab_row),
         (dqr.reshape(T, NH * QKP), (BR, NH * QKP), row(0)), (dkr.reshape(T, NH * QKP), (BR, NH * QKP), row(0))],
        [((T, NH * QKP), BF16, (BR, NH * QKP), row(0))] * 2 + [((T, LANE), BF16, (BR, LANE), row(0))]
        + [((1, QKP), F32, (1, QKP), fix, True)] * 2)
    dv = dv.reshape(T, NH * VD)
    d_cqn = _mm("uq_dx", d_q, res["w_uq_p"], tb=True)
    d_ckvn = _mm("uk_dx", d_kn, res["w_uk"], tb=True)
    d_ckvn = _mm("uv_dx", dv, res["w_uv"], tb=True, add=d_ckvn)
    g["w_uq"] = _unpad_heads_uq(_mm("uq_dw", res["cqn"], d_q, ta=True, tk=T))
    g["w_ukv"] = _join_ukv(_mm("uk_dw", res["ckvn"], d_kn, ta=True, tk=T),
                           _mm("uv_dw", res["ckvn"], dv, ta=True, tk=T))
    pre_bwd = functools.partial(_vjp_of(_mla_pre_math, 4), n_prim=4)
    d_cq, d_ckv, g["q_a_norm_g"], g["kv_a_norm_g"] = _pcall(
        "mla_pre_bwd", pre_bwd, (T // BR,),
        [(proj, (BR, QL), row(O_CQ // QL)), (proj, (BR, KVL), row(O_CKV // KVL)),
         (w["q_a_norm_g"], (1, QL), fix), (w["kv_a_norm_g"], (1, KVL), fix),
         (d_cqn, (BR, QL), row(0)), (d_ckvn, (BR, KVL), row(0))],
        [((T, QL), BF16, (BR, QL), row(0)), ((T, KVL), BF16, (BR, KVL), row(0)),
         ((1, QL), F32, (1, QL), fix, True), ((1, KVL), F32, (1, KVL), fix, True)])

    gq2 = jnp.tile(w["dil_q_norm_g"].reshape(NG, HD), (1, 2))
    gk2 = jnp.tile(w["dil_k_norm_g"].reshape(NG, HD), (1, 2))
    d_dq, d_dk, d_dv, d_cz, dgq, dgk = _dilc_bwd(proj3, gq2, gk2, dil_tab, res["o_all"], res["l_all"],
                                                 d_yc.reshape(batch, S, DWID))
    g["dil_q_norm_g"] = dgq[:, :HD] + dgq[:, HD:]
    g["dil_k_norm_g"] = dgk[:, :HD] + dgk[:, HD:]
    d_dq, d_dk, d_dv = (t.reshape(T, NG * DWID) for t in (d_dq, d_dk, d_dv))
    d_cz = d_cz.reshape(T, DWID)

    dproj = jnp.concatenate(
        [t.reshape(T, CW) for t in (d_ab, d_ac, d_ax, d_az)]
        + [d_cq, d_ckv, d_kpe_p, d_bz, d_dq, d_dk, d_dv, d_cz, dg0, dg1, dg2], axis=1)
    g["w_in_t"] = _mm("in_proj_dw", dproj, h, ta=True, tm=1024, tk=T)
    dep = after_dw(g) if after_dw is not None else None
    d_h = _mm("in_proj_dx", dproj, w["w_in_t"], dep=dep, tm=1024, tk=NINP // 4)
    dx, g["norm_g"] = _pcall("norm_bwd", _norm_bwd_math, (T // BR,),
                             [(x, (BR, D), row(0)), (w["norm_g"], (1, D), fix), (d_h, (BR, D), row(0)),
                              (dy, (BR, D), row(0))],
                             [((T, D), F32, (BR, D), row(0)), ((1, D), F32, (1, D), fix, True)])
    return dx, g


def _loss_math(y, t):
    e = y - t
    return e * (1.0 / D), 0.5 * jnp.sum(jnp.sum(e * e, axis=-1, keepdims=True) / D, axis=0, keepdims=True)


def _local_step(x, target, ws, batch):
    T = batch * S
    tabs = _rope_tables() + (_dil_slopes(),)
    saved = []
    y = x
    for l in range(NL):
        y, res = _layer_fwd(y, ws[l], tabs, batch)
        saved.append(res)
    row = lambda i: (i, 0)
    dy, loss = _pcall("loss", _loss_math, (T // BR,),
                      [(y, (BR, D), row), (target, (BR, D), row)],
                      [((T, D), F32, (BR, D), row), ((1, 1), F32, (1, 1), lambda i: (0, 0), True)])
    grads = [None] * NL
    for l in reversed(range(NL)):
        dy, grads[l] = _layer_bwd(dy, ws[l], saved[l], tabs, batch)
    return loss, dy, grads


ANY = pl.BlockSpec(memory_space=pl.ANY)
U32 = jnp.uint32
WSH = NIN // 4
WA = KPE_END
WB = WSH - WA
CWD = 512
PACK_ROWS = 1472
HW = PACK_W // 2


def _me():
    return lax.axis_index("x"), lax.axis_index("y"), lax.axis_index("c")


def _piece_rows(k):
    a = k * WSH + jnp.where(k > 0, NINP - NIN, 0)
    b = k * WSH + WA + (NINP - NIN)
    return ((0, pl.multiple_of(a, 8), WA), (WA, pl.multiple_of(b, 8), WB))


def _pack_words(lo, hi):
    ul = lax.bitcast_convert_type(lo.astype(BF16).astype(F32), U32)
    uh = lax.bitcast_convert_type(hi.astype(BF16).astype(F32), U32)
    w = jnp.bitwise_or(jnp.bitwise_and(uh, jnp.uint32(0xFFFF0000)), jnp.right_shift(ul, jnp.uint32(16)))
    return lax.bitcast_convert_type(w, F32)


def _unpack_words(w):
    w = lax.bitcast_convert_type(w, U32)
    lo = lax.bitcast_convert_type(jnp.left_shift(w, jnp.uint32(16)), F32)
    hi = lax.bitcast_convert_type(jnp.bitwise_and(w, jnp.uint32(0xFFFF0000)), F32)
    return lo, hi


def _all_gather(wc, sp):
    def body(w_ref, s_ref, ow_ref, os_ref, send_sems, recv_sems):
        x, y, c = _me()
        k_me = 2 * x + y
        sib = (x, y, 1 - c)
        chips = [(1 - x, y), (x, 1 - y), (1 - x, 1 - y)]
        wcols = lambda cc: pl.ds(pl.multiple_of(cc * (CWD // 2), LANE), CWD // 2)
        scols = lambda cc: pl.ds(pl.multiple_of(cc * HW, LANE), HW)

        def windows(k, cc):
            pcs = _piece_rows(k)
            return ([(w_ref.at[pl.ds(l0, n), wcols(cc)], ow_ref.at[pl.ds(p0, n), wcols(cc)]) for l0, p0, n in pcs]
                    + [(s_ref.at[:, scols(cc)], os_ref.at[k, :, scols(cc)])])

        def copy(i, src, dst, to):
            return pltpu.make_async_remote_copy(src_ref=src, dst_ref=dst, send_sem=send_sems.at[i],
                                                recv_sem=recv_sems.at[i], device_id=to, device_id_type=MESH)

        def own_windows():
            return ([(w_ref.at[pl.ds(l0, n)], ow_ref.at[pl.ds(p0, n)]) for l0, p0, n in _piece_rows(k_me)]
                    + [(s_ref, os_ref.at[k_me])])

        first = [copy(18 + i, src, dst, sib) for i, (src, dst) in enumerate(own_windows())]
        for j, (cx, cy) in enumerate(chips):
            for i, (src, dst) in enumerate(windows(k_me, c)):
                first.append(copy(3 * j + i, src, dst, (cx, cy, c)))
        for cp in first:
            cp.start()
        passed = []
        for j, (cx, cy) in enumerate(chips):
            for i, (_, dst) in enumerate(windows(2 * cx + cy, c)):
                copy(3 * j + i, dst, dst, (cx, cy, c)).wait_recv()
                cp = copy(9 + 3 * j + i, dst, dst, sib)
                cp.start()
                passed.append(cp)
        for j, (cx, cy) in enumerate(chips):
            for i, (_, dst) in enumerate(windows(2 * cx + cy, 1 - c)):
                copy(9 + 3 * j + i, dst, dst, sib).wait_recv()
        for i, (_, dst) in enumerate(own_windows()):
            copy(18 + i, dst, dst, sib).wait_recv()
        for cp in first + passed:
            cp.wait_send()

    return pl.pallas_call(
        body,
        out_shape=[jax.ShapeDtypeStruct((NINP, CWD), F32), jax.ShapeDtypeStruct((4, PACK_ROWS, PACK_W), BF16)],
        in_specs=[ANY, ANY], out_specs=[ANY, ANY],
        scratch_shapes=[pltpu.SemaphoreType.DMA((21,)), pltpu.SemaphoreType.DMA((21,))],
        name="weights_all_gather",
    )(wc, sp)


HBM = pl.BlockSpec(memory_space=pltpu.HBM)
SEM = pl.BlockSpec(memory_space=pltpu.SEMAPHORE)
EFFECT = pltpu.SideEffectType.DATAFLOW_SIDE_EFFECTING


def _in_hbm(a):
    return pltpu.with_memory_space_constraint(a, pltpu.HBM)


def _ag_windows(w_ref, s_ref, lw_ref, ls_ref, k, cc):
    wcols = pl.ds(pl.multiple_of(cc * (CWD // 2), LANE), CWD // 2)
    scols = pl.ds(pl.multiple_of(cc * HW, LANE), HW)
    return ([(w_ref.at[pl.ds(l0, n), wcols], lw_ref.at[pl.ds(p0, n), wcols]) for l0, p0, n in _piece_rows(k)]
            + [(s_ref.at[:, scols], ls_ref.at[k, :, scols])])


def _ag_ici_copies(w_ref, s_ref, lw_ref, ls_ref, send_sems, recv_sems):
    x, y, c = _me()
    mine, theirs = [], []
    for j, (cx, cy) in enumerate([(1 - x, y), (x, 1 - y), (1 - x, 1 - y)]):
        for i, ((src, dst), (_, got)) in enumerate(zip(_ag_windows(w_ref, s_ref, lw_ref, ls_ref, 2 * x + y, c),
                                                       _ag_windows(w_ref, s_ref, lw_ref, ls_ref, 2 * cx + cy, c))):
            mk = lambda s_, d_: pltpu.make_async_remote_copy(
                src_ref=s_, dst_ref=d_, send_sem=send_sems.at[3 * j + i], recv_sem=recv_sems.at[3 * j + i],
                device_id=(cx, cy, c), device_id_type=MESH)
            mine.append(mk(src, dst))
            theirs.append(mk(got, got))
    return mine, theirs


def _ag_ici_start(wc, sp, dep):
    def body(w_ref, s_ref, lw_ref, ls_ref, dep_ref, send_sems, recv_sems, w_thru, s_thru, lw_thru, ls_thru, token):
        mine, _ = _ag_ici_copies(w_ref, s_ref, lw_ref, ls_ref, send_sems, recv_sems)
        for cp in mine:
            cp.start()
        token[...] = jnp.zeros_like(token)

    return pl.pallas_call(
        body, name="weights_gather_start",
        out_shape=(pltpu.SemaphoreType.DMA((9,)), pltpu.SemaphoreType.DMA((9,)), pltpu.HBM(wc.shape, wc.dtype),
                   pltpu.HBM(sp.shape, sp.dtype), pltpu.HBM((NINP, CWD), F32), pltpu.HBM((4, PACK_ROWS, PACK_W), BF16),
                   jax.ShapeDtypeStruct((8, LANE), F32)),
        in_specs=(HBM, HBM, HBM, HBM, ANY),
        out_specs=(SEM, SEM, HBM, HBM, HBM, HBM, pl.BlockSpec(memory_space=pltpu.VMEM)),
        input_output_aliases={0: 2, 1: 3, 2: 4, 3: 5},
        compiler_params=pltpu.CompilerParams(has_side_effects=EFFECT),
    )(_in_hbm(wc), _in_hbm(sp), _in_hbm(lax.empty((NINP, CWD), F32)), _in_hbm(lax.empty((4, PACK_ROWS, PACK_W), BF16)), dep)


def _ag_ici_wait(send_sems, recv_sems, wc, sp, lw, ls, after):
    def body(w_ref, s_ref, lw_ref, ls_ref, send_sems, recv_sems, after_ref, w_dead, s_dead, lw_out, ls_out):
        mine, theirs = _ag_ici_copies(w_ref, s_ref, lw_ref, ls_ref, send_sems, recv_sems)
        for cp in mine:
            cp.wait_send()
        for cp in theirs:
            cp.wait_recv()

    out = pl.pallas_call(
        body, name="weights_gather_wait",
        out_shape=(pltpu.HBM(wc.shape, wc.dtype), pltpu.HBM(sp.shape, sp.dtype), pltpu.HBM(lw.shape, lw.dtype),
                   pltpu.HBM(ls.shape, ls.dtype)),
        in_specs=(HBM, HBM, HBM, HBM, SEM, SEM, ANY), out_specs=(HBM, HBM, HBM, HBM),
        input_output_aliases={0: 0, 1: 1, 2: 2, 3: 3},
        compiler_params=pltpu.CompilerParams(has_side_effects=EFFECT),
    )(wc, sp, lw, ls, send_sems, recv_sems, after)
    return out[2], out[3]


def _ag_finish(wc, sp, lw, ls):
    def body(w_ref, s_ref, lw_ref, ls_ref, ow_ref, os_ref, send_sems, recv_sems):
        x, y, c = _me()
        k_me = 2 * x + y
        sib = (x, y, 1 - c)
        chips = [(1 - x, y), (x, 1 - y), (1 - x, 1 - y)]

        def copy(i, src, dst):
            return pltpu.make_async_remote_copy(src_ref=src, dst_ref=dst, send_sem=send_sems.at[i],
                                                recv_sem=recv_sems.at[i], device_id=sib, device_id_type=MESH)

        def own_windows():
            return ([(w_ref.at[pl.ds(l0, n)], ow_ref.at[pl.ds(p0, n)]) for l0, p0, n in _piece_rows(k_me)]
                    + [(s_ref, os_ref.at[k_me])])

        out = [copy(9 + i, src, dst) for i, (src, dst) in enumerate(own_windows())]
        for j, (cx, cy) in enumerate(chips):
            landed = _ag_windows(w_ref, s_ref, lw_ref, ls_ref, 2 * cx + cy, c)
            for i, (_, dst) in enumerate(_ag_windows(w_ref, s_ref, ow_ref, os_ref, 2 * cx + cy, c)):
                out.append(copy(3 * j + i, landed[i][1], dst))
        for cp in out:
            cp.start()
        for j, (cx, cy) in enumerate(chips):
            for i, (_, dst) in enumerate(_ag_windows(w_ref, s_ref, ow_ref, os_ref, 2 * cx + cy, 1 - c)):
                copy(3 * j + i, dst, dst).wait_recv()
        for i, (_, dst) in enumerate(own_windows()):
            copy(9 + i, dst, dst).wait_recv()
        for cp in out:
            cp.wait_send()

    return pl.pallas_call(
        body,
        out_shape=[jax.ShapeDtypeStruct(lw.shape, lw.dtype), jax.ShapeDtypeStruct(ls.shape, ls.dtype)],
        in_specs=[ANY] * 4, out_specs=[ANY, ANY],
        input_output_aliases={2: 0, 3: 1},
        scratch_shapes=[pltpu.SemaphoreType.DMA((12,)), pltpu.SemaphoreType.DMA((12,))],
        name="weights_gather_finish",
    )(wc, sp, lw, ls)


UNPACK_BR = 512


def _unpack_w_in(cont):
    def body(c_ref, o_ref):
        lo, hi = _unpack_words(c_ref[...])
        r = pl.program_id(0) * UNPACK_BR + lax.broadcasted_iota(jnp.int32, (UNPACK_BR, CWD), 0)
        pad = jnp.logical_and(r >= KPE_END, r < KPE_END + NINP - NIN)
        o_ref[:, 0:CWD] = jnp.where(pad, 0.0, lo).astype(BF16)
        o_ref[:, CWD:2 * CWD] = jnp.where(pad, 0.0, hi).astype(BF16)

    return pl.pallas_call(
        body, grid=(NINP // UNPACK_BR,),
        in_specs=[pl.BlockSpec((UNPACK_BR, CWD), lambda i: (i, 0))],
        out_specs=pl.BlockSpec((UNPACK_BR, D), lambda i: (i, 0)),
        out_shape=jax.ShapeDtypeStruct((NINP, D), BF16),
        name="w_in_unpack",
        compiler_params=pltpu.CompilerParams(dimension_semantics=("arbitrary",), vmem_limit_bytes=VMEM_LIMIT),
    )(cont)


def _rs_swap(gw, gs):
    def body(w_ref, s_ref, rw_ref, rs_ref, send_sems, recv_sems):
        x, y, c = _me()
        oc = 1 - c
        cps = [pltpu.make_async_remote_copy(src_ref=w_ref.at[:, pl.ds(pl.multiple_of(oc * (D // 2), LANE), D // 2)],
                                            dst_ref=rw_ref, send_sem=send_sems.at[0], recv_sem=recv_sems.at[0],
                                            device_id=(x, y, oc), device_id_type=MESH),
               pltpu.make_async_remote_copy(src_ref=s_ref.at[:, :, pl.ds(pl.multiple_of(oc * HW, LANE), HW)],
                                            dst_ref=rs_ref, send_sem=send_sems.at[1], recv_sem=recv_sems.at[1],
                                            device_id=(x, y, oc), device_id_type=MESH)]
        for cp in cps:
            cp.start()
        for cp in cps:
            cp.wait()

    return pl.pallas_call(
        body,
        out_shape=[jax.ShapeDtypeStruct((NINP, D // 2), F32), jax.ShapeDtypeStruct((4, PACK_ROWS, HW), F32)],
        in_specs=[ANY, ANY], out_specs=[ANY, ANY],
        scratch_shapes=[pltpu.SemaphoreType.DMA((2,)), pltpu.SemaphoreType.DMA((2,))],
        name="grads_sibling_swap",
    )(gw, gs)


SUM_BR = 512


def _rs_chip_sum_w(gw, rw, cidx):
    def body(c_ref, g_ref, r_ref, o_ref):
        s = g_ref[...] + r_ref[...]
        q = D // 8
        o_ref[...] = jnp.concatenate([_pack_words(s[:, 0:q], s[:, q:2 * q]),
                                      _pack_words(s[:, 2 * q:3 * q], s[:, 3 * q:4 * q])], axis=1)

    return pl.pallas_call(
        body,
        grid_spec=pltpu.PrefetchScalarGridSpec(
            num_scalar_prefetch=1, grid=(NINP // SUM_BR,),
            in_specs=[pl.BlockSpec((SUM_BR, D // 2), lambda i, cr: (i, cr[0])),
                      pl.BlockSpec((SUM_BR, D // 2), lambda i, cr: (i, 0))],
            out_specs=pl.BlockSpec((SUM_BR, D // 4), lambda i, cr: (i, 0))),
        out_shape=jax.ShapeDtypeStruct((NINP, D // 4), F32),
        name="grads_chip_sum_w",
        compiler_params=pltpu.CompilerParams(dimension_semantics=("arbitrary",), vmem_limit_bytes=VMEM_LIMIT),
    )(cidx, gw, rw)


def _rs_chip_sum_s(gs, rs, cidx):
    def body(c_ref, g_ref, r_ref, o_ref):
        o_ref[...] = (g_ref[...] + r_ref[...]).astype(BF16)

    return pl.pallas_call(
        body,
        grid_spec=pltpu.PrefetchScalarGridSpec(
            num_scalar_prefetch=1, grid=(4,),
            in_specs=[pl.BlockSpec((None, PACK_ROWS, HW), lambda j, cr: (j, 0, cr[0])),
                      pl.BlockSpec((None, PACK_ROWS, HW), lambda j, cr: (j, 0, 0))],
            out_specs=pl.BlockSpec((None, PACK_ROWS, HW), lambda j, cr: (j, 0, 0))),
        out_shape=jax.ShapeDtypeStruct((4, PACK_ROWS, HW), BF16),
        name="grads_chip_sum_s",
        compiler_params=pltpu.CompilerParams(dimension_semantics=("arbitrary",), vmem_limit_bytes=VMEM_LIMIT),
    )(cidx, gs, rs)


def _rs_exchange_copies(sw_ref, ss_ref, r2w_ref, r2s_ref, send_sems, recv_sems):
    x, y, c = _me()
    mine, theirs = [], []
    for j, (cx, cy) in enumerate([(1 - x, y), (x, 1 - y), (1 - x, 1 - y)]):
        def mk(i, src, dst):
            return pltpu.make_async_remote_copy(src_ref=src, dst_ref=dst, send_sem=send_sems.at[3 * j + i],
                                                recv_sem=recv_sems.at[3 * j + i], device_id=(cx, cy, c), device_id_type=MESH)
        for i, (l0, p0, n) in enumerate(_piece_rows(2 * cx + cy)):
            mine.append(mk(i, sw_ref.at[pl.ds(p0, n)], r2w_ref.at[j, pl.ds(l0, n)]))
            theirs.append(mk(i, r2w_ref.at[j, pl.ds(l0, n)], r2w_ref.at[j, pl.ds(l0, n)]))
        mine.append(mk(2, ss_ref.at[2 * cx + cy], r2s_ref.at[j]))
        theirs.append(mk(2, r2s_ref.at[j], r2s_ref.at[j]))
    return mine, theirs


def _rs_exchange_start(sw, ss, tag):
    def body(sw_ref, ss_ref, r2w_ref, r2s_ref, send_sems, recv_sems, sw_thru, ss_thru, r2w_thru, r2s_thru, token):
        mine, _ = _rs_exchange_copies(sw_ref, ss_ref, r2w_ref, r2s_ref, send_sems, recv_sems)
        for cp in mine:
            cp.start()
        token[...] = jnp.zeros_like(token)

    return pl.pallas_call(
        body, name="grads_exchange_start_" + tag,
        out_shape=(pltpu.SemaphoreType.DMA((9,)), pltpu.SemaphoreType.DMA((9,)), pltpu.HBM(sw.shape, sw.dtype),
                   pltpu.HBM(ss.shape, ss.dtype), pltpu.HBM((3, WSH, D // 4), F32), pltpu.HBM((3, PACK_ROWS, HW), BF16),
                   jax.ShapeDtypeStruct((8, LANE), F32)),
        in_specs=(HBM, HBM, HBM, HBM),
        out_specs=(SEM, SEM, HBM, HBM, HBM, HBM, pl.BlockSpec(memory_space=pltpu.VMEM)),
        input_output_aliases={0: 2, 1: 3, 2: 4, 3: 5},
        compiler_params=pltpu.CompilerParams(has_side_effects=EFFECT),
    )(_in_hbm(sw), _in_hbm(ss), _in_hbm(lax.empty((3, WSH, D // 4), F32)), _in_hbm(lax.empty((3, PACK_ROWS, HW), BF16)))


def _rs_exchange_wait(send_sems, recv_sems, sw, ss, r2w, r2s, after, tag):
    def body(sw_ref, ss_ref, r2w_ref, r2s_ref, send_sems, recv_sems, after_ref, sw_dead, ss_dead, r2w_out, r2s_out):
        mine, theirs = _rs_exchange_copies(sw_ref, ss_ref, r2w_ref, r2s_ref, send_sems, recv_sems)
        for cp in mine:
            cp.wait_send()
        for cp in theirs:
            cp.wait_recv()

    out = pl.pallas_call(
        body, name="grads_exchange_wait_" + tag,
        out_shape=(pltpu.HBM(sw.shape, sw.dtype), pltpu.HBM(ss.shape, ss.dtype), pltpu.HBM(r2w.shape, r2w.dtype),
                   pltpu.HBM(r2s.shape, r2s.dtype)),
        in_specs=(HBM, HBM, HBM, HBM, SEM, SEM, ANY), out_specs=(HBM, HBM, HBM, HBM),
        input_output_aliases={0: 0, 1: 1, 2: 2, 3: 3},
        compiler_params=pltpu.CompilerParams(has_side_effects=EFFECT),
    )(sw, ss, r2w, r2s, send_sems, recv_sems, after)
    return out[2], out[3]


def _rs_final_w(gw, rw, r2w, idx):
    q = D // 8

    def body(i_ref, g_ref, r_ref, p_ref, o_ref, gbuf, rbuf, sems):
        i = pl.program_id(0)
        k, c = i_ref[0], i_ref[1]
        cps = []
        for n_, (l0, p0, n) in enumerate(_piece_rows(k)):
            gcol = pl.ds(pl.multiple_of(c * (D // 2) + i * 2 * q, LANE), 2 * q)
            rcol = pl.ds(pl.multiple_of(i * 2 * q, LANE), 2 * q)
            cps.append(pltpu.make_async_copy(g_ref.at[pl.ds(p0, n), gcol], gbuf.at[pl.ds(l0, n)], sems.at[2 * n_]))
            cps.append(pltpu.make_async_copy(r_ref.at[pl.ds(p0, n), rcol], rbuf.at[pl.ds(l0, n)], sems.at[2 * n_ + 1]))
        for cp in cps:
            cp.start()
        for cp in cps:
            cp.wait()
        acc = gbuf[...] + rbuf[...]
        for j in range(3):
            lo, hi = _unpack_words(p_ref[j])
            acc = acc + jnp.concatenate([lo, hi], axis=1)
        o_ref[...] = acc

    return pl.pallas_call(
        body,
        grid_spec=pltpu.PrefetchScalarGridSpec(
            num_scalar_prefetch=1, grid=(2,),
            in_specs=[ANY, ANY, pl.BlockSpec((3, WSH, q), lambda i, ir: (0, 0, i))],
            out_specs=pl.BlockSpec((WSH, 2 * q), lambda i, ir: (0, i)),
            scratch_shapes=[pltpu.VMEM((WSH, 2 * q), F32), pltpu.VMEM((WSH, 2 * q), F32), pltpu.SemaphoreType.DMA((4,))]),
        out_shape=jax.ShapeDtypeStruct((WSH, D // 2), F32),
        name="grads_final_sum_w",
        compiler_params=pltpu.CompilerParams(dimension_semantics=("arbitrary",), vmem_limit_bytes=VMEM_LIMIT),
    )(idx, gw, rw, r2w)


def _rs_final_s(gs, rs, r2s, idx):
    def body(i_ref, g_ref, r_ref, p_ref, o_ref):
        acc = g_ref[...] + r_ref[...]
        for j in range(3):
            acc = acc + p_ref[j].astype(F32)
        o_ref[...] = acc

    return pl.pallas_call(
        body,
        grid_spec=pltpu.PrefetchScalarGridSpec(
            num_scalar_prefetch=1, grid=(1,),
            in_specs=[pl.BlockSpec((None, PACK_ROWS, HW), lambda i, ir: (ir[0], 0, ir[1])),
                      pl.BlockSpec((None, PACK_ROWS, HW), lambda i, ir: (ir[0], 0, 0)),
                      pl.BlockSpec((3, PACK_ROWS, HW), lambda i, ir: (0, 0, 0))],
            out_specs=pl.BlockSpec((PACK_ROWS, HW), lambda i, ir: (0, 0))),
        out_shape=jax.ShapeDtypeStruct((PACK_ROWS, HW), F32),
        name="grads_final_sum_s",
        compiler_params=pltpu.CompilerParams(dimension_semantics=("arbitrary",), vmem_limit_bytes=VMEM_LIMIT),
    )(idx, gs, rs, r2s)


def _rs_share(fw, fs):
    def body(w_ref, s_ref, ow_ref, os_ref, send_sems, recv_sems):
        x, y, c = _me()
        cps = [pltpu.make_async_remote_copy(src_ref=w_ref, dst_ref=ow_ref, send_sem=send_sems.at[0],
                                            recv_sem=recv_sems.at[0], device_id=(x, y, 1 - c), device_id_type=MESH),
               pltpu.make_async_remote_copy(src_ref=s_ref, dst_ref=os_ref, send_sem=send_sems.at[1],
                                            recv_sem=recv_sems.at[1], device_id=(x, y, 1 - c), device_id_type=MESH)]
        for cp in cps:
            cp.start()
        for cp in cps:
            cp.wait()

    return pl.pallas_call(
        body,
        out_shape=[jax.ShapeDtypeStruct((WSH, D // 2), F32), jax.ShapeDtypeStruct((PACK_ROWS, HW), F32)],
        in_specs=[ANY, ANY], out_specs=[ANY, ANY],
        scratch_shapes=[pltpu.SemaphoreType.DMA((2,)), pltpu.SemaphoreType.DMA((2,))],
        name="grads_share",
    )(fw, fs)


def _both_halves(mine, other, c):
    return jnp.where(c == 0, jnp.concatenate([mine, other], axis=1), jnp.concatenate([other, mine], axis=1))


def _rs_begin(gw, gs):
    x, y, c = _me()
    cidx = jnp.reshape(c, (1,)).astype(jnp.int32)
    rw, rs = _rs_swap(gw, gs)
    return dict(gw=gw, gs=gs, rw=rw, rs=rs, sw=_rs_chip_sum_w(gw, rw, cidx), ss=_rs_chip_sum_s(gs, rs, cidx))


def _rs_end(st, r2w, r2s):
    x, y, c = _me()
    idx = jnp.stack([2 * x + y, c]).astype(jnp.int32)
    fw = _rs_final_w(st["gw"], st["rw"], r2w, idx)
    fs = _rs_final_s(st["gs"], st["rs"], r2s, idx)
    ow, os_ = _rs_share(fw, fs)
    return _both_halves(fw, ow, c), _both_halves(fs, os_, c)


def _all_reduce_small(gs):
    rows = gs.shape[0]

    def body(g_ref, o_ref, buf, send_sems, recv_sems):
        x, y, c = _me()
        me = 4 * x + 2 * y + c
        buf[me] = g_ref[...]
        cps = []
        for r in range(1, 8):
            fx, fy, fc = (r >> 2) & 1, (r >> 1) & 1, r & 1
            px, py, pc = jnp.bitwise_xor(x, fx), jnp.bitwise_xor(y, fy), jnp.bitwise_xor(c, fc)
            cps.append((pltpu.make_async_remote_copy(
                src_ref=g_ref, dst_ref=buf.at[me], send_sem=send_sems.at[r - 1], recv_sem=recv_sems.at[r - 1],
                device_id=(px, py, pc), device_id_type=MESH), 4 * px + 2 * py + pc))
        for cp, _ in cps:
            cp.start()
        for r, (cp, peer) in enumerate(cps):
            pltpu.make_async_remote_copy(
                src_ref=g_ref, dst_ref=buf.at[peer], send_sem=send_sems.at[r], recv_sem=recv_sems.at[r],
                device_id=(x, y, c), device_id_type=MESH).wait_recv()
        for cp, _ in cps:
            cp.wait_send()
        acc = buf[0]
        for k in range(1, 8):
            acc = acc + buf[k]
        o_ref[...] = acc

    return pl.pallas_call(
        body,
        out_shape=jax.ShapeDtypeStruct((rows, LANE), F32),
        in_specs=[pl.BlockSpec(memory_space=pltpu.VMEM)],
        out_specs=pl.BlockSpec(memory_space=pltpu.VMEM),
        scratch_shapes=[pltpu.VMEM((8, rows, LANE), F32), pltpu.SemaphoreType.DMA((7,)), pltpu.SemaphoreType.DMA((7,))],
        name="small_grads_all_reduce",
    )(gs)


PACK_SPLIT = (("w_uq", 96, (QL, 192)), ("w_ukv", 64, (KVL, 256)),
              ("w_out_a", 256, (CW, 256)), ("w_out_b", 256, (CW, 256)), ("w_out_c", 256, (CW, 256)),
              ("w_o", 512, (256, D)))
MAT_ROWS = 1440
CONV_SHARD = 3 * 128


def _w_in_words(w_in_shard):
    t = w_in_shard.T
    return _pack_words(t[:, :CWD], t[:, CWD:])


def _pack_weights(wl):
    parts = [wl[n].astype(BF16).reshape(-1, PACK_W) for n, _, _ in PACK_SPLIT]
    cw = wl["conv_w"].reshape(-1)
    hi = cw.astype(BF16)
    r1 = cw - hi.astype(F32)
    mid = r1.astype(BF16)
    lo = (r1 - mid.astype(F32)).astype(BF16)
    cterms = jnp.pad(jnp.concatenate([hi, mid, lo]), (0, 3 * PACK_W - 3 * CONV_SHARD)).reshape(3, PACK_W)
    tail = jnp.pad(cterms, ((0, PACK_ROWS - MAT_ROWS - 3), (0, 0)))
    return jnp.concatenate(parts + [tail], axis=0)


def _unpack_weights(gath):
    out = {}
    r = 0
    for n, nrows, shp in PACK_SPLIT:
        t = gath[:, r:r + nrows].reshape((4,) + shp)
        r += nrows
        if n == "w_o":
            out[n] = t.reshape(4 * shp[0], shp[1])
        else:
            out[n] = t.transpose(1, 0, 2).reshape(shp[0], 4 * shp[1])
    ct = gath[:, r:r + 3].reshape(4, 3 * PACK_W)[:, :3 * CONV_SHARD].astype(F32).reshape(4, 3, CONV_SHARD)
    cw = (ct[:, 0] + ct[:, 1]) + ct[:, 2]
    out["conv_w"] = cw.reshape(4, 3, 128).transpose(1, 0, 2).reshape(3, CW)
    return out


def _pack_grads(g):
    parts = []
    for n, nrows, shp in PACK_SPLIT:
        t = g[n]
        if n == "w_o":
            t = t.reshape((4,) + shp)
        else:
            t = t.reshape(shp[0], 4, shp[1]).transpose(1, 0, 2)
        parts.append(t.reshape(4, nrows, PACK_W))
    cw = g["conv_w"].reshape(3, 4, 128).transpose(1, 0, 2).reshape(4, 1, CONV_SHARD)
    parts.append(jnp.pad(cw, ((0, 0), (0, PACK_ROWS - MAT_ROWS - 1), (0, PACK_W - CONV_SHARD))))
    return jnp.concatenate(parts, axis=1)


def _unpack_grads(red):
    out = {}
    r = 0
    for n, nrows, shp in PACK_SPLIT:
        out[n] = red[r:r + nrows].reshape(shp)
        r += nrows
    out["conv_w"] = red[r, :CONV_SHARD].reshape(3, 128)
    return out


SMALL_SIZES = (("norm_g", D), ("b_gate", 3 * D), ("conv_b", CW), ("q_a_norm_g", QL), ("kv_a_norm_g", KVL),
               ("mla_q_norm_g", QK), ("mla_k_norm_g", QK), ("dil_q_norm_g", NG * HD), ("dil_k_norm_g", NG * HD))
SMALL_ROWS = 88


def _pack_small(per_name):
    flat = jnp.concatenate([per_name[n].reshape(-1).astype(F32) for n, _ in SMALL_SIZES])
    return jnp.pad(flat, (0, SMALL_ROWS * LANE - flat.shape[0])).reshape(SMALL_ROWS, LANE)


def _unpack_small(packed, like):
    out = {}
    flat = packed.reshape(-1)
    r = 0
    for n, sz in SMALL_SIZES:
        out[n] = flat[r:r + NL * sz].reshape(like[n].shape)
        r += NL * sz
    return out


def _adamw_math(w, g, m, v):
    m = ADAM_B1 * m + (1.0 - ADAM_B1) * g
    v = ADAM_B2 * v + (1.0 - ADAM_B2) * jnp.square(g)
    m_hat = m / (1.0 - ADAM_B1 ** ADAM_STEP)
    v_hat = v / (1.0 - ADAM_B2 ** ADAM_STEP)
    delta = -ADAM_LR * (m_hat / (jnp.sqrt(v_hat) + ADAM_EPS) + ADAM_WD * w)
    return delta, m, v


def _adamw(name, w, g, m, v, br, bc=None):
    L, R, C = w.shape
    bc = C if bc is None else bc
    blk = lambda l, i, j: (l, i, j)
    return _pcall(name, _adamw_math, (L, R // br, C // bc), [(t, (None, br, bc), blk) for t in (w, g, m, v)],
                  [((L, R, C), F32, (None, br, bc), blk)] * 3)


ADAM_ROWS = {"w_uq": 256, "w_ukv": 128, "w_out_a": 512, "w_out_b": 512, "w_out_c": 512, "w_o": 256,
             "conv_w": 3}


def kernel(x, norm_g, w_in, b_gate, conv_w, conv_b, q_a_norm_g, w_uq, kv_a_norm_g, w_ukv, mla_q_norm_g, mla_k_norm_g, dil_q_norm_g, dil_k_norm_g, w_out_a, w_out_b, w_out_c, w_o, loss_target, m_norm_g, m_w_in, m_b_gate, m_conv_w, m_conv_b, m_q_a_norm_g, m_w_uq, m_kv_a_norm_g, m_w_ukv, m_mla_q_norm_g, m_mla_k_norm_g, m_dil_q_norm_g, m_dil_k_norm_g, m_w_out_a, m_w_out_b, m_w_out_c, m_w_o, v_norm_g, v_w_in, v_b_gate, v_conv_w, v_conv_b, v_q_a_norm_g, v_w_uq, v_kv_a_norm_g, v_w_ukv, v_mla_q_norm_g, v_mla_k_norm_g, v_dil_q_norm_g, v_dil_k_norm_g, v_w_out_a, v_w_out_b, v_w_out_c, v_w_o):
    W = dict(norm_g=norm_g, w_in=w_in, b_gate=b_gate, conv_w=conv_w, conv_b=conv_b, q_a_norm_g=q_a_norm_g, w_uq=w_uq,
             kv_a_norm_g=kv_a_norm_g, w_ukv=w_ukv, mla_q_norm_g=mla_q_norm_g, mla_k_norm_g=mla_k_norm_g,
             dil_q_norm_g=dil_q_norm_g, dil_k_norm_g=dil_k_norm_g, w_out_a=w_out_a, w_out_b=w_out_b, w_out_c=w_out_c,
             w_o=w_o)
    M = dict(norm_g=m_norm_g, w_in=m_w_in, b_gate=m_b_gate, conv_w=m_conv_w, conv_b=m_conv_b, q_a_norm_g=m_q_a_norm_g,
             w_uq=m_w_uq, kv_a_norm_g=m_kv_a_norm_g, w_ukv=m_w_ukv, mla_q_norm_g=m_mla_q_norm_g,
             mla_k_norm_g=m_mla_k_norm_g, dil_q_norm_g=m_dil_q_norm_g, dil_k_norm_g=m_dil_k_norm_g, w_out_a=m_w_out_a,
             w_out_b=m_w_out_b, w_out_c=m_w_out_c, w_o=m_w_o)
    V = dict(norm_g=v_norm_g, w_in=v_w_in, b_gate=v_b_gate, conv_w=v_conv_w, conv_b=v_conv_b, q_a_norm_g=v_q_a_norm_g,
             w_uq=v_w_uq, kv_a_norm_g=v_kv_a_norm_g, w_ukv=v_w_ukv, mla_q_norm_g=v_mla_q_norm_g,
             mla_k_norm_g=v_mla_k_norm_g, dil_q_norm_g=v_dil_q_norm_g, dil_k_norm_g=v_dil_k_norm_g, w_out_a=v_w_out_a,
             w_out_b=v_w_out_b, w_out_c=v_w_out_c, w_o=v_w_o)
    batch = x.shape[0]
    T = batch * S

    def layer_weights(l, cont, gath):
        full = _unpack_weights(gath)
        pad_qk = lambda t: jnp.pad(t, (0, QKP - QK)).reshape(1, QKP)
        full.update(
            w_in_t=_unpack_w_in(cont),
            norm_g=norm_g[l].reshape(1, D), b_gate=b_gate[l].reshape(1, 3 * D), conv_b=conv_b[l].reshape(1, CW),
            q_a_norm_g=q_a_norm_g[l].reshape(1, QL), kv_a_norm_g=kv_a_norm_g[l].reshape(1, KVL),
            mla_q_norm_g=pad_qk(mla_q_norm_g[l]), mla_k_norm_g=pad_qk(mla_k_norm_g[l]),
            dil_q_norm_g=dil_q_norm_g[l].reshape(NG, 1, HD), dil_k_norm_g=dil_k_norm_g[l].reshape(NG, 1, HD))
        return full

    words = [_w_in_words(w_in[l]) for l in range(NL)]
    packs = [_pack_weights({n: W[n][l] for n in BIG[1:] + ("conv_w",)}) for l in range(NL)]
    tabs = _rope_tables() + (_dil_slopes(),)
    x2 = x.reshape(T, D)

    cont0, gath0 = _all_gather(words[0], packs[0])
    w0 = layer_weights(0, cont0, gath0)
    ag = _ag_ici_start(words[1], packs[1], gath0)
    w0["norm_g"] = w0["norm_g"] + ag[6][0:1, 0:1]
    y0, res0 = _layer_fwd(x2, w0, tabs, batch)
    lw, ls = _ag_ici_wait(ag[0], ag[1], ag[2], ag[3], ag[4], ag[5], y0)
    w1 = layer_weights(1, *_ag_finish(words[1], packs[1], lw, ls))
    y1, res1 = _layer_fwd(y0, w1, tabs, batch)

    row = lambda i: (i, 0)
    dy, loss = _pcall("loss", _loss_math, (T // BR,),
                      [(y1, (BR, D), row), (loss_target.reshape(T, D), (BR, D), row)],
                      [((T, D), F32, (BR, D), row), ((1, 1), F32, (1, 1), lambda i: (0, 0), True)])
    loss = lax.psum(loss[0, 0], ("x", "y", "c"))

    grads = [None] * NL
    dy, grads[1] = _layer_bwd(dy, w1, res1, tabs, batch)
    st = [None] * NL
    ex = [None] * NL
    st[1] = _rs_begin(grads[1]["w_in_t"], _pack_grads(grads[1]))
    ex[1] = _rs_exchange_start(st[1]["sw"], st[1]["ss"], "1")
    w0["w_o"] = w0["w_o"] + ex[1][6][0:1, 0:1].astype(BF16)

    def start_layer0(g):
        st[0] = _rs_begin(g["w_in_t"], _pack_grads(g))
        ex[0] = _rs_exchange_start(st[0]["sw"], st[0]["ss"], "0")
        return ex[0][6]

    dx, grads[0] = _layer_bwd(dy, w0, res0, tabs, batch, after_dw=start_layer0)
    grad_x = dx.reshape(batch, S, D)

    red = [None] * NL
    for l in (1, 0):
        r2w, r2s = _rs_exchange_wait(*ex[l][:6], dx, str(l))
        rw, rs = _rs_end(st[l], r2w, r2s)
        r = _unpack_grads(rs)
        r["w_in_t"] = rw
        red[l] = r
    G = {n: jnp.stack([red[l][n] for l in range(NL)]) for n in BIG[1:] + ("conv_w",)}
    g_in_t = jnp.stack([red[l]["w_in_t"] for l in range(NL)])
    G["w_in"] = jnp.swapaxes(g_in_t, 1, 2)
    small_g = {n: jnp.stack([grads[l][n].reshape(-1)[:sz] for l in range(NL)]) for n, sz in SMALL_SIZES}
    small_red = _all_reduce_small(_pack_small(small_g))
    G.update(_unpack_small(small_red, {n: W[n] for n in SMALL}))

    delta, new_m, new_v = {}, {}, {}
    for n in BIG[1:] + ("conv_w",):
        delta[n], new_m[n], new_v[n] = _adamw("adamw_" + n, W[n], G[n], M[n], V[n], ADAM_ROWS[n])
    tr = lambda t: jnp.swapaxes(t, 1, 2)
    delta["w_in"], new_m["w_in"], new_v["w_in"] = (
        tr(t) for t in _adamw("adamw_w_in", tr(w_in), g_in_t, tr(m_w_in), tr(v_w_in), WSH, LANE))
    sw, sm, sv = (_pack_small({n: t[n] for n in SMALL})[None] for t in (W, M, V))
    sd, snm, snv = _adamw("adamw_small", sw, small_red[None], sm, sv, SMALL_ROWS)
    like = {n: W[n] for n in SMALL}
    delta.update(_unpack_small(sd[0], like))
    new_m.update(_unpack_small(snm[0], like))
    new_v.update(_unpack_small(snv[0], like))

    return (loss, grad_x, *[G[n] for n in WEIGHTS], *[delta[n] for n in WEIGHTS],
            *[new_m[n] for n in WEIGHTS], *[new_v[n] for n in WEIGHTS])
```

```python
import functools

import numpy as np
import jax
import jax.numpy as jnp
from jax import lax
from jax.experimental import pallas as pl
from jax.experimental.pallas import tpu as pltpu

F32 = jnp.float32
BF16 = jnp.bfloat16

D = 1024
S = 2048
NL = 2
CW = 512
NH = 8
QL = 256
KVL = 128
NOPE = 64
ROPE = 32
VD = 64
QK = NOPE + ROPE
QKP = 128
ROPE_THETA = 10000.0
DIL = ((128, 1), (512, 4), (2048, 16))
NG = 3
DH = 8
HD = 64
DWID = DH * HD
QB = 128
EPS = 1e-6
NIN = 11168
NINP = 11264
O_A, O_CQ, O_CKV, O_KPE, O_BZ, O_DQ, O_DK, O_DV, O_CZ, O_G = 0, 2048, 2304, 2432, 2560, 3072, 4608, 6144, 7680, 8192
KPE_END = 2464
NEG = -1e30
MLA_SCALE = QK ** -0.5
DIL_SCALE = HD ** -0.5
LANE = 128
PACK_W = 512
VMEM_LIMIT = 48 * 1024 * 1024

ADAM_LR = 0.001
ADAM_B1 = 0.9
ADAM_B2 = 0.999
ADAM_EPS = 1e-08
ADAM_WD = 0.01
ADAM_STEP = 10

MESH = pl.DeviceIdType.MESH
BIG = ("w_in", "w_uq", "w_ukv", "w_out_a", "w_out_b", "w_out_c", "w_o")
SMALL = ("norm_g", "b_gate", "conv_b", "q_a_norm_g", "kv_a_norm_g", "mla_q_norm_g", "mla_k_norm_g",
         "dil_q_norm_g", "dil_k_norm_g")
WEIGHTS = ("norm_g", "w_in", "b_gate", "conv_w", "conv_b", "q_a_norm_g", "w_uq", "kv_a_norm_g", "w_ukv",
           "mla_q_norm_g", "mla_k_norm_g", "dil_q_norm_g", "dil_k_norm_g", "w_out_a", "w_out_b", "w_out_c", "w_o")


def _dot(a, b):
    return jnp.dot(a, b, preferred_element_type=F32)


def _dot_nt(a, b):
    return lax.dot_general(a, b, (((1,), (1,)), ((), ())), preferred_element_type=F32)


def _dot_tn(a, b):
    return lax.dot_general(a, b, (((0,), (0,)), ((), ())), preferred_element_type=F32)


def _grid_step(grid):
    step = pl.program_id(0)
    for a in range(1, len(grid)):
        step = step * grid[a] + pl.program_id(a)
    n = 1
    for g in grid:
        n *= g
    return step, n


def _write_windows(buf_ref, stages, sems, step, nsteps, puts):
    slot = step % 2
    for t, (v, dst) in enumerate(puts):
        cp = pltpu.make_async_copy(stages[t].at[slot], dst, sems.at[t, slot])

        @pl.when(step >= 2)
        def _():
            cp.wait()

        stages[t][slot] = v.astype(stages[t].dtype).reshape(stages[t].shape[1:])
        cp.start()

    @pl.when(step == nsteps - 1)
    def _():
        for t, (v, dst) in enumerate(puts):
            pltpu.make_async_copy(stages[t].at[slot], dst, sems.at[t, slot]).wait()
            if nsteps > 1:
                pltpu.make_async_copy(stages[t].at[1 - slot], dst, sems.at[t, 1 - slot]).wait()


def _pcall(name, fn, grid, ins, outs, into=None):
    n_in = len(ins)
    n_out = len(outs)
    acc_axis = len(grid) - 1
    is_acc = [len(o) > 4 and o[4] for o in outs]
    outs = [o[:4] for o in outs]
    targets = into[1] if into is not None else []
    n_t = len(targets)

    def body(*refs):
        vals = fn(*[r[...].astype(F32) for r in refs[:n_in]])
        if not isinstance(vals, (tuple, list)):
            vals = (vals,)
        o0 = n_in + (1 if n_t else 0)
        for k in range(n_out):
            r = refs[o0 + k]
            v = vals[k].astype(r.dtype).reshape(r.shape)
            if is_acc[k]:
                first = pl.program_id(acc_axis) == 0

                @pl.when(first)
                def _():
                    r[...] = v

                @pl.when(jnp.logical_not(first))
                def _():
                    r[...] += v
            else:
                r[...] = v
        if n_t:
            buf_ref = refs[o0 + n_out]
            stages = refs[o0 + n_out + 1:o0 + n_out + 1 + n_t]
            ids = [pl.program_id(a) for a in range(len(grid))]
            step, nsteps = _grid_step(grid)
            _write_windows(buf_ref, stages, refs[-1], step, nsteps,
                           [(vals[n_out + t], targets[t][1](buf_ref, *ids)) for t in range(n_t)])

    in_specs = [pl.BlockSpec(bs, im) for _, bs, im in ins]
    out_specs = [pl.BlockSpec(bs, im) for _, _, bs, im in outs]
    out_shape = [jax.ShapeDtypeStruct(sh, dt) for sh, dt, _, _ in outs]
    args = [a for a, _, _ in ins]
    extra = {}
    if n_t:
        buf = into[0]
        in_specs.append(pl.BlockSpec(memory_space=pl.ANY))
        out_specs.append(pl.BlockSpec(memory_space=pl.ANY))
        out_shape.append(jax.ShapeDtypeStruct(buf.shape, buf.dtype))
        args.append(buf)
        extra = dict(input_output_aliases={n_in: n_out},
                     scratch_shapes=[pltpu.VMEM((2,) + tuple(bs), buf.dtype) for bs, _ in targets]
                     + [pltpu.SemaphoreType.DMA((n_t, 2))])
    return pl.pallas_call(
        body,
        grid=grid,
        in_specs=in_specs,
        out_specs=out_specs,
        out_shape=out_shape,
        name=name,
        compiler_params=pltpu.CompilerParams(
            dimension_semantics=("arbitrary",) * len(grid), vmem_limit_bytes=VMEM_LIMIT),
        **extra,
    )(*args)


def _mm(name, a, b, *, ta=False, tb=False, out_dtype=F32, add=None, dep=None, tm=2048, tn=1024, tk=1024):
    if ta:
        K, M = a.shape
    else:
        M, K = a.shape
    if tb:
        N, K2 = b.shape
    else:
        K2, N = b.shape
    assert K == K2, (name, a.shape, b.shape)
    tm, tn, tk = min(tm, M), min(tn, N), min(tk, K)
    assert M % tm == 0 and N % tn == 0 and K % tk == 0, (name, M, N, K)
    nk = K // tk
    dims = (((0 if ta else 1,), (1 if tb else 0,)), ((), ()))
    a_spec = pl.BlockSpec((tk, tm), lambda j, i, k: (k, i)) if ta else pl.BlockSpec((tm, tk), lambda j, i, k: (i, k))
    b_spec = pl.BlockSpec((tn, tk), lambda j, i, k: (j, k)) if tb else pl.BlockSpec((tk, tn), lambda j, i, k: (k, j))
    o_spec = pl.BlockSpec((tm, tn), lambda j, i, k: (i, j))
    has_add = add is not None
    n_in = 2 + has_add + (dep is not None)

    def body(*refs):
        a_ref, b_ref = refs[0], refs[1]
        add_ref = refs[2] if has_add else None
        o_ref = refs[n_in]
        p = lax.dot_general(a_ref[...].astype(BF16), b_ref[...].astype(BF16), dims, preferred_element_type=F32)
        if nk == 1:
            if has_add:
                p = p + add_ref[...]
            o_ref[...] = p.astype(out_dtype)
        else:
            acc = refs[-1]
            k = pl.program_id(2)

            @pl.when(k == 0)
            def _():
                acc[...] = p

            @pl.when(k > 0)
            def _():
                acc[...] += p

            @pl.when(k == nk - 1)
            def _():
                r = acc[...]
                if has_add:
                    r = r + add_ref[...]
                o_ref[...] = r.astype(out_dtype)

    in_specs = [a_spec, b_spec] + ([o_spec] if has_add else []) + ([pl.BlockSpec(memory_space=pl.ANY)] if dep is not None else [])
    args = [a, b] + ([add] if has_add else []) + ([dep] if dep is not None else [])
    return pl.pallas_call(
        body,
        grid=(N // tn, M // tm, nk),
        in_specs=in_specs,
        out_specs=o_spec,
        out_shape=jax.ShapeDtypeStruct((M, N), out_dtype),
        scratch_shapes=[pltpu.VMEM((tm, tn), F32)] if nk > 1 else [],
        name=name,
        compiler_params=pltpu.CompilerParams(
            dimension_semantics=("arbitrary", "arbitrary", "arbitrary"), vmem_limit_bytes=VMEM_LIMIT),
    )(*args)


def _vjp_of(f, n_diff):
    def g(*args, n_prim):
        prim = args[:n_diff]
        consts = args[n_diff:n_prim]
        cts = args[n_prim:]
        _, pull = jax.vjp(lambda *p: f(*p, *consts), *prim)
        out = jax.eval_shape(lambda *p: f(*p, *consts), *prim)
        if isinstance(out, (tuple, list)):
            cts = tuple(c.astype(o.dtype) for c, o in zip(cts, out))
        else:
            cts = cts[0].astype(out.dtype)
        return pull(cts)
    return g


def _rms(x, g, n=None):
    n = x.shape[-1] if n is None else n
    ms = jnp.sum(x * x, axis=-1, keepdims=True) / n
    return x * lax.rsqrt(ms + EPS) * g


def _silu(z):
    return z * jax.nn.sigmoid(z)


def _roll_rows(u, k):
    n = u.shape[0]
    r = pltpu.roll(u, k % n, 0)
    t = lax.broadcasted_iota(jnp.int32, u.shape, 0)
    if k > 0:
        return jnp.where(t >= k, r, 0.0)
    return jnp.where(t < n + k, r, 0.0)


@functools.partial(jax.custom_vjp, nondiff_argnums=(1,))
def _shift(u, k):
    return _roll_rows(u, k)


def _shift_fwd(u, k):
    return _roll_rows(u, k), None


def _shift_bwd(k, _, g):
    return (_roll_rows(g, -k),)


_shift.defvjp(_shift_fwd, _shift_bwd)


@functools.partial(jax.custom_vjp, nondiff_argnums=(1,))
def _lane_roll(u, k):
    return pltpu.roll(u, k % LANE, 1)


def _lane_roll_fwd(u, k):
    return pltpu.roll(u, k % LANE, 1), None


def _lane_roll_bwd(k, _, g):
    return (pltpu.roll(g, (-k) % LANE, 1),)


_lane_roll.defvjp(_lane_roll_fwd, _lane_roll_bwd)


def _conv_math(ab, ac, ax, az, cw, cb):
    u = ac * ax
    conv = cb + _shift(u, 2) * cw[0:1] + _shift(u, 1) * cw[1:2] + u * cw[2:3]
    return ab * conv * _silu(az)


def _mla_pre_math(cq, ckv, gq, gkv):
    return _rms(cq, gq), _rms(ckv, gkv)


def _rope_math(q, kn, kpe, gq, gk, c, s1, s2):
    lane = lax.broadcasted_iota(jnp.int32, kpe.shape, 1)
    pe = _lane_roll(jnp.where(lane < ROPE, kpe, 0.0), NOPE)

    def one(t, g):
        tn = _rms(t, g, QK)
        return tn * c + _lane_roll(tn, -16) * s1 + _lane_roll(tn, 16) * s2

    qs, ks = [], []
    for h in range(NH):
        sl = slice(h * QKP, (h + 1) * QKP)
        qs.append(one(q[:, sl], gq))
        ks.append(one(kn[:, sl] + pe, gk))
    return jnp.concatenate(qs, axis=1), jnp.concatenate(ks, axis=1)


def _gate_math(o, z):
    return o * _silu(z)


def _mergec_math(o0, o1, o2, l0, l1, l2, cz):
    m = lax.stop_gradient(jnp.maximum(jnp.maximum(l0, l1), l2))
    e0, e1, e2 = jnp.exp(l0 - m), jnp.exp(l1 - m), jnp.exp(l2 - m)
    den = e0 + e1 + e2
    oc = (e0 / den) * o0 + (e1 / den) * o1 + (e2 / den) * o2
    return oc * _silu(cz)


def _merge_math(g0, g1, g2, b0, b1, b2, pa, pb, pc):
    return (jax.nn.sigmoid(g0 + b0) * pa + jax.nn.sigmoid(g1 + b1) * pb) + jax.nn.sigmoid(g2 + b2) * pc


MLA_T = 256
MLA_UNROLL = True


def _mla_fwd(q, k, v):
    B = q.shape[0]
    T = MLA_T
    NB = S // T

    def body(q_ref, k_ref, v_ref, o_ref, l_ref):
        row = lax.broadcasted_iota(jnp.int32, (T, T), 0)
        col = lax.broadcasted_iota(jnp.int32, (T, T), 1)
        lo = _lo_mask((T, LANE))

        for qi in range(NB):
            qb = q_ref[qi * T:(qi + 1) * T, :]

            def step(j, carry, diagonal):
                m, l, acc = carry
                off = pl.multiple_of(j * T, T)
                kb = k_ref[pl.ds(off, T), :]
                vb = v_ref[pl.ds(off, T), :]
                ss = []
                for e in (0, 1):
                    se = _dot_nt(qb[:, e * QKP:(e + 1) * QKP], kb[:, e * QKP:(e + 1) * QKP]) * MLA_SCALE
                    ss.append(jnp.where(col <= row, se, NEG) if diagonal else se)
                s = jnp.concatenate(ss, axis=0)
                m_new = jnp.maximum(m, jnp.max(s, axis=-1, keepdims=True))
                a = jnp.exp(m - m_new)
                p = jnp.exp(s - m_new)
                l = a * l + jnp.sum(p, axis=-1, keepdims=True)
                acc = a * acc + _dot(p.astype(BF16), vb)
                return m_new, l, acc

            init = (jnp.full((2 * T, 1), NEG, F32), jnp.zeros((2 * T, 1), F32), jnp.zeros((2 * T, LANE), F32))
            carry = lax.fori_loop(0, qi, functools.partial(step, diagonal=False), init, unroll=MLA_UNROLL)
            m, l, acc = step(qi, carry, True)
            o = acc / l
            lse = m + jnp.log(l)
            o_ref[qi * T:(qi + 1) * T, :] = jnp.where(lo, o[:T], o[T:])
            l_ref[qi * T:(qi + 1) * T, :] = jnp.where(lo, lse[:T], lse[T:])

    def spec(w):
        return pl.BlockSpec((None, S, w), lambda b, hp: (b, 0, hp))

    return pl.pallas_call(
        body,
        grid=(B, NH // 2),
        in_specs=[spec(2 * QKP), spec(2 * QKP), spec(LANE)],
        out_specs=[spec(LANE), spec(LANE)],
        out_shape=[jax.ShapeDtypeStruct((B, S, NH * VD), F32)] * 2,
        name="mla_attn_fwd",
        compiler_params=pltpu.CompilerParams(dimension_semantics=("arbitrary",) * 2, vmem_limit_bytes=VMEM_LIMIT),
    )(q, k, v)


def _mla_bwd(q, k, v, do, o, lse):
    B = q.shape[0]
    T = MLA_T
    NB = S // T

    def body(q_ref, k_ref, v_ref, do_ref, o_ref, l_ref, dq_ref, dk_ref, dv_ref, delta_ref):
        delta_ref[...] = _head_sum(do_ref[...] * o_ref[...])
        row = lax.broadcasted_iota(jnp.int32, (T, T), 0)
        col = lax.broadcasted_iota(jnp.int32, (T, T), 1)
        lo = _lo_mask((T, LANE))

        for j in range(NB):
            krows = slice(j * T, (j + 1) * T)
            kb = k_ref[krows, :]
            vb = v_ref[krows, :]
            dk = [jnp.zeros((T, QKP), F32), jnp.zeros((T, QKP), F32)]
            dv = jnp.zeros((T, LANE), F32)
            for i in range(j, NB):
                qrows = slice(i * T, (i + 1) * T)
                qb = q_ref[qrows, :]
                do2 = _stack_heads(do_ref[qrows, :], lo).astype(BF16)
                lb = l_ref[qrows, :]
                db = delta_ref[qrows, :]
                dp2 = _dot_nt(do2, vb)
                for e in (0, 1):
                    cols = slice(e * QKP, (e + 1) * QKP)
                    qe, ke = qb[:, cols], kb[:, cols]
                    s = _dot_nt(qe, ke) * MLA_SCALE
                    if i == j:
                        s = jnp.where(col <= row, s, NEG)
                    p = jnp.exp(s - lb[:, e * HD:e * HD + 1])
                    dv = dv + _dot_tn(p.astype(BF16), do2[e * T:(e + 1) * T])
                    ds = (p * (dp2[e * T:(e + 1) * T] - db[:, e * HD:e * HD + 1]) * MLA_SCALE).astype(BF16)
                    dk[e] = dk[e] + _dot_tn(ds, qe)
                    if j == 0:
                        dq_ref[qrows, cols] = _dot(ds, ke)
                    else:
                        dq_ref[qrows, cols] += _dot(ds, ke)
            dk_ref[krows, 0:QKP] = dk[0]
            dk_ref[krows, QKP:2 * QKP] = dk[1]
            dv_ref[krows, :] = dv

    def spec(w):
        return pl.BlockSpec((None, S, w), lambda b, hp: (b, 0, hp))

    return pl.pallas_call(
        body,
        grid=(B, NH // 2),
        in_specs=[spec(2 * QKP), spec(2 * QKP), spec(LANE), spec(LANE), spec(LANE), spec(LANE)],
        out_specs=[spec(2 * QKP), spec(2 * QKP), spec(LANE)],
        out_shape=[jax.ShapeDtypeStruct((B, S, NH * QKP), F32), jax.ShapeDtypeStruct((B, S, NH * QKP), F32),
                   jax.ShapeDtypeStruct((B, S, NH * VD), F32)],
        scratch_shapes=[pltpu.VMEM((S, LANE), F32)],
        name="mla_attn_bwd",
        compiler_params=pltpu.CompilerParams(dimension_semantics=("arbitrary",) * 2, vmem_limit_bytes=VMEM_LIMIT),
    )(q, k, v, do, o, lse)


def _lo_mask(shape):
    return lax.broadcasted_iota(jnp.int32, shape, len(shape) - 1) < HD


def _head_sum(u):
    r = lax.broadcasted_iota(jnp.int32, (LANE, LANE), 0) < HD
    c = lax.broadcasted_iota(jnp.int32, (LANE, LANE), 1) < HD
    ones = jnp.where(r == c, 1.0, 0.0).astype(BF16)
    hi = u.astype(BF16)
    lo = (u - hi.astype(F32)).astype(BF16)
    return _dot(hi, ones) + _dot(lo, ones)


def _head_sum_1(u):
    r = lax.broadcasted_iota(jnp.int32, (LANE, LANE), 0) < HD
    c = lax.broadcasted_iota(jnp.int32, (LANE, LANE), 1) < HD
    return _dot(u.astype(BF16), jnp.where(r == c, 1.0, 0.0).astype(BF16))


def _rms2_scale(x):
    return lax.rsqrt(_head_sum(x * x) / HD + EPS)


def _rms2(x, g):
    return x * _rms2_scale(x) * g


def _rms2_bwd(x, r, g, dy):
    xn = x * r
    t = dy * g
    dx = r * (t - xn * (_head_sum_1(xn * t) * (1.0 / HD)))
    return dx, jnp.sum(dy * xn, axis=0, keepdims=True)


def _dil_bias(t_ref, gi, d):
    qq = lax.broadcasted_iota(jnp.int32, (QB, QB), 0)
    kk = lax.broadcasted_iota(jnp.int32, (QB, QB), 1)
    jc = (qq - kk).astype(F32)
    rows = []
    for e in (0, 1):
        sl = t_ref[2 * gi + e:2 * gi + e + 1, :] * float(d)
        bp = jnp.where(kk >= qq, -sl * (jc + float(QB)), NEG)
        bc = jnp.where(kk <= qq, -sl * jc, NEG)
        rows.append(jnp.concatenate([bp, bc], axis=1))
    return jnp.concatenate(rows, axis=0)


def _dil_rows(cur, d):
    return pl.ds(cur, QB, stride=d) if d > 1 else pl.ds(pl.multiple_of(cur, QB), QB)


def _dil_walk(d, block, full):
    if d == 1:
        block(0, None)

        def body(i, c):
            block(i * QB, (i - 1) * QB)
            return c
        lax.fori_loop(1, S // QB, body, 0, unroll=True if full else 5)
    elif d == 16:
        def body(r, c):
            block(r, None)
            return c
        lax.fori_loop(0, d, body, 0, unroll=True if full else 4)
    else:
        nb = S // d // QB

        def cls(r, c):
            block(r, None)

            def body(i, c2):
                block(r + i * QB * d, r + (i - 1) * QB * d)
                return c2
            lax.fori_loop(1, nb, body, 0, unroll=True)
            return c
        lax.fori_loop(0, d, cls, 0, unroll=full)


def _stack_heads(x, lo):
    return jnp.concatenate([jnp.where(lo, x, 0.0), jnp.where(lo, 0.0, x)], axis=0)


def _dilc_fwd(proj3, gq, gk, tab):
    B = proj3.shape[0]

    def body(q_ref, k_ref, v_ref, cz_ref, gq_ref, gk_ref, t_ref, y_ref, o_ref, l_ref, qs, ks, vs):
        g = pl.program_id(2)
        lo = _lo_mask((QB, LANE))

        def group(gi):
            d = DIL[gi][1]
            qs[...] = _rms2(q_ref[...].astype(F32), gq_ref[gi:gi + 1, :])
            ks[...] = _rms2(k_ref[...].astype(F32), gk_ref[gi:gi + 1, :])
            vs[...] = v_ref[...].astype(F32)
            bias = _dil_bias(t_ref, gi, d)

            def block(cur, prev):
                rows = _dil_rows(cur, d)
                q2 = _stack_heads(qs[rows, :], lo).astype(BF16)
                kc, vc = ks[rows, :], vs[rows, :]
                if prev is None:
                    kcat, vcat, b = kc, vc, bias[:, QB:]
                else:
                    prow = _dil_rows(prev, d)
                    kcat = jnp.concatenate([ks[prow, :], kc], axis=0)
                    vcat = jnp.concatenate([vs[prow, :], vc], axis=0)
                    b = bias
                s = _dot_nt(q2, kcat.astype(BF16)) * DIL_SCALE + b
                m = jnp.max(s, axis=-1, keepdims=True)
                p = jnp.exp(s - m)
                l = jnp.sum(p, axis=-1, keepdims=True)
                o = _dot(p.astype(BF16), vcat.astype(BF16)) / l
                lse = m + jnp.log(l)
                o_ref[gi, rows, :] = jnp.where(lo, o[:QB], o[QB:])
                l_ref[gi, rows, :] = jnp.where(lo, lse[:QB], lse[QB:])

            _dil_walk(d, block, True)

        for gi in range(NG):
            pl.when(g == gi)(functools.partial(group, gi))

        @pl.when(g == NG - 1)
        def _():
            y_ref[...] = _mergec_math(o_ref[0], o_ref[1], o_ref[2], l_ref[0], l_ref[1], l_ref[2],
                                      cz_ref[...].astype(F32)).astype(BF16)

    def col(base):
        return pl.BlockSpec((None, S, LANE), lambda b, hp, g: (b, 0, base // LANE + 4 * g + hp))

    gspec = pl.BlockSpec((NG, LANE), lambda b, hp, g: (0, 0))
    saved = pl.BlockSpec((NG, None, S, LANE), lambda b, hp, g: (0, b, 0, hp))
    return pl.pallas_call(
        body,
        grid=(B, 4, NG),
        in_specs=[col(O_DQ), col(O_DK), col(O_DV),
                  pl.BlockSpec((None, S, LANE), lambda b, hp, g: (b, 0, O_CZ // LANE + hp)),
                  gspec, gspec, pl.BlockSpec((None, 8, LANE), lambda b, hp, g: (hp, 0, 0))],
        out_specs=[pl.BlockSpec((None, S, LANE), lambda b, hp, g: (b, 0, hp)), saved, saved],
        out_shape=[jax.ShapeDtypeStruct((B, S, DWID), BF16), jax.ShapeDtypeStruct((NG, B, S, DWID), F32),
                   jax.ShapeDtypeStruct((NG, B, S, DWID), F32)],
        scratch_shapes=[pltpu.VMEM((S, LANE), F32)] * 3,
        name="dil_mixer_fwd",
        compiler_params=pltpu.CompilerParams(dimension_semantics=("arbitrary",) * 3, vmem_limit_bytes=VMEM_LIMIT),
    )(proj3, proj3, proj3, proj3, gq, gk, tab)


MERGE_ROWS = 256


def _dilc_bwd(proj3, gq, gk, tab, o_all, l_all, d_yc, dproj3):
    B = proj3.shape[0]

    def body(q_ref, k_ref, v_ref, cz_ref, gq_ref, gk_ref, t_ref, o_ref, l_ref, dy_ref, dp_in,
             dp_out, dgq_out, dgk_out, qs, ks, vs, dos, dls, dqs, dks, dvs, rqs, rks, dczs,
             st_q, st_k, st_v, st_z, sems, sem_z):
        b_, hp, g = pl.program_id(0), pl.program_id(1), pl.program_id(2)
        col = lambda base: pl.ds(pl.multiple_of(base + hp * LANE, LANE), LANE)
        lo = _lo_mask((QB, LANE))

        @pl.when(jnp.logical_and(jnp.logical_and(pl.program_id(0) == 0, pl.program_id(1) == 0), g == 0))
        def _():
            dgq_out[...] = jnp.zeros((NG, LANE), F32)
            dgk_out[...] = jnp.zeros((NG, LANE), F32)

        @pl.when(g == 0)
        def _():
            def chunk(i, carry):
                rows = pl.ds(pl.multiple_of(i * MERGE_ROWS, MERGE_ROWS), MERGE_ROWS)
                ls = [l_ref[j, rows, :] for j in range(NG)]
                m = jnp.maximum(jnp.maximum(ls[0], ls[1]), ls[2])
                es = [jnp.exp(t - m) for t in ls]
                den = (es[0] + es[1]) + es[2]
                al = [e / den for e in es]
                os_ = [o_ref[j, rows, :] for j in range(NG)]
                oc = (al[0] * os_[0] + al[1] * os_[1]) + al[2] * os_[2]
                cz = cz_ref[rows, :].astype(F32)
                sg = jax.nn.sigmoid(cz)
                dy = dy_ref[rows, :]
                d_oc = dy * (cz * sg)
                dczs[rows, :] = (dy * oc * (sg * (1.0 + cz * (1.0 - sg)))).astype(BF16)
                ts = [_head_sum_1(d_oc * os_[j]) for j in range(NG)]
                tbar = (al[0] * ts[0] + al[1] * ts[1]) + al[2] * ts[2]
                for j in range(NG):
                    dos[j, rows, :] = al[j] * d_oc
                    dls[j, rows, :] = al[j] * (ts[j] - tbar)
                return carry
            lax.fori_loop(0, S // MERGE_ROWS, chunk, 0)
            _write_windows(dp_out, [st_z], sem_z, b_ * 4 + hp, B * 4, [(dczs[...], dp_out.at[b_, :, col(O_CZ)])])

        def group(gi):
            d = DIL[gi][1]
            xq, xk = q_ref[...].astype(F32), k_ref[...].astype(F32)
            rqs[...] = _rms2_scale(xq)
            rks[...] = _rms2_scale(xk)
            qs[...] = xq * rqs[...] * gq_ref[gi:gi + 1, :]
            ks[...] = xk * rks[...] * gk_ref[gi:gi + 1, :]
            vs[...] = v_ref[...].astype(F32)
            dks[...] = jnp.zeros((S, LANE), F32)
            dvs[...] = jnp.zeros((S, LANE), F32)
            bias = _dil_bias(t_ref, gi, d)

            def block(cur, prev):
                rows = _dil_rows(cur, d)
                q2 = _stack_heads(qs[rows, :], lo).astype(BF16)
                dob = dos[gi, rows, :]
                do2 = _stack_heads(dob, lo).astype(BF16)
                kc, vc = ks[rows, :], vs[rows, :]
                if prev is None:
                    kcat, vcat, b = kc, vc, bias[:, QB:]
                else:
                    prow = _dil_rows(prev, d)
                    kcat = jnp.concatenate([ks[prow, :], kc], axis=0)
                    vcat = jnp.concatenate([vs[prow, :], vc], axis=0)
                    b = bias
                kcat = kcat.astype(BF16)
                vcat = vcat.astype(BF16)
                lse_b = l_ref[gi, rows, :]
                corr_b = dls[gi, rows, :] - _head_sum_1(dob * o_ref[gi, rows, :])
                lse2 = jnp.concatenate([lse_b[:, 0:1], lse_b[:, HD:HD + 1]], axis=0)
                corr2 = jnp.concatenate([corr_b[:, 0:1], corr_b[:, HD:HD + 1]], axis=0)
                s = _dot_nt(q2, kcat) * DIL_SCALE + b
                p = jnp.exp(s - lse2)
                ds = (p * (_dot_nt(do2, vcat) + corr2) * DIL_SCALE).astype(BF16)
                dq2 = _dot(ds, kcat)
                dqs[rows, :] = jnp.where(lo, dq2[:QB], dq2[QB:])
                dk = _dot_tn(ds, q2)
                dv = _dot_tn(p.astype(BF16), do2)
                if prev is None:
                    dks[rows, :] += dk
                    dvs[rows, :] += dv
                else:
                    dks[prow, :] += dk[:QB]
                    dvs[prow, :] += dv[:QB]
                    dks[rows, :] += dk[QB:]
                    dvs[rows, :] += dv[QB:]

            _dil_walk(d, block, False)

            dxq, dgq = _rms2_bwd(q_ref[...].astype(F32), rqs[...], gq_ref[gi:gi + 1, :], dqs[...])
            dgq_out[gi:gi + 1, :] += dgq
            dxk, dgk = _rms2_bwd(k_ref[...].astype(F32), rks[...], gk_ref[gi:gi + 1, :], dks[...])
            dgk_out[gi:gi + 1, :] += dgk
            step, nsteps = _grid_step((B, 4, NG))
            _write_windows(dp_out, [st_q, st_k, st_v], sems, step, nsteps,
                           [(dxq, dp_out.at[b_, :, col(O_DQ + gi * DWID)]), (dxk, dp_out.at[b_, :, col(O_DK + gi * DWID)]),
                            (dvs[...], dp_out.at[b_, :, col(O_DV + gi * DWID)])])

        for gi in range(NG):
            pl.when(g == gi)(functools.partial(group, gi))

    def col(base):
        return pl.BlockSpec((None, S, LANE), lambda b, hp, g: (b, 0, base // LANE + 4 * g + hp))

    gspec = pl.BlockSpec((NG, LANE), lambda b, hp, g: (0, 0))
    saved = pl.BlockSpec((NG, None, S, LANE), lambda b, hp, g: (0, b, 0, hp))
    per_pair = pl.BlockSpec((None, S, LANE), lambda b, hp, g: (b, 0, hp))
    return pl.pallas_call(
        body,
        grid=(B, 4, NG),
        in_specs=[col(O_DQ), col(O_DK), col(O_DV),
                  pl.BlockSpec((None, S, LANE), lambda b, hp, g: (b, 0, O_CZ // LANE + hp)),
                  gspec, gspec, pl.BlockSpec((None, 8, LANE), lambda b, hp, g: (hp, 0, 0)),
                  saved, saved, per_pair, pl.BlockSpec(memory_space=pl.ANY)],
        out_specs=[pl.BlockSpec(memory_space=pl.ANY), gspec, gspec],
        out_shape=[jax.ShapeDtypeStruct(dproj3.shape, dproj3.dtype), jax.ShapeDtypeStruct((NG, LANE), F32),
                   jax.ShapeDtypeStruct((NG, LANE), F32)],
        input_output_aliases={10: 0},
        scratch_shapes=[pltpu.VMEM((S, LANE), F32)] * 3 + [pltpu.VMEM((NG, S, LANE), F32)] * 2
        + [pltpu.VMEM((S, LANE), F32)] * 5 + [pltpu.VMEM((S, LANE), BF16)] + [pltpu.VMEM((2, S, LANE), BF16)] * 4
        + [pltpu.SemaphoreType.DMA((3, 2)), pltpu.SemaphoreType.DMA((1, 2))],
        name="dil_mixer_bwd",
        compiler_params=pltpu.CompilerParams(dimension_semantics=("arbitrary",) * 3, vmem_limit_bytes=VMEM_LIMIT),
    )(proj3, proj3, proj3, proj3, gq, gk, tab, o_all, l_all, d_yc, dproj3)


def _dil_slopes():
    slopes = (2.0 ** (-8.0 * np.arange(1, NG * DH + 1, dtype=np.float32) / (NG * DH))).astype(np.float32).reshape(NG, DH)
    tab = np.zeros((4, 8, LANE), np.float32)
    for hp in range(4):
        for gi in range(NG):
            for e in (0, 1):
                tab[hp, 2 * gi + e, :] = slopes[gi, 2 * hp + e]
    return jnp.asarray(tab)


def _rope_tables():
    inv = ROPE_THETA ** (-jnp.arange(0, ROPE, 2, dtype=F32) / ROPE)
    ang = jnp.arange(S, dtype=F32)[:, None] * inv[None, :]
    cos, sin = jnp.cos(ang), jnp.sin(ang)
    z16 = jnp.zeros((S, 16), F32)
    c = jnp.concatenate([jnp.ones((S, NOPE), F32), cos, cos, jnp.zeros((S, 32), F32)], axis=1)
    s1 = jnp.concatenate([jnp.zeros((S, NOPE), F32), -sin, z16, jnp.zeros((S, 32), F32)], axis=1)
    s2 = jnp.concatenate([jnp.zeros((S, NOPE), F32), z16, sin, jnp.zeros((S, 32), F32)], axis=1)
    return c, s1, s2


def _pad_heads_uq(w):
    return jnp.pad(w.reshape(QL, NH, QK), ((0, 0), (0, 0), (0, QKP - QK))).reshape(QL, NH * QKP)


def _unpad_heads_uq(g):
    return g.reshape(QL, NH, QKP)[:, :, :QK].reshape(QL, NH * QK)


def _split_ukv(w):
    w3 = w.reshape(KVL, NH, NOPE + VD)
    uk = jnp.pad(w3[:, :, :NOPE], ((0, 0), (0, 0), (0, QKP - NOPE))).reshape(KVL, NH * QKP)
    return uk, w3[:, :, NOPE:].reshape(KVL, NH * VD)


def _join_ukv(guk, guv):
    return jnp.concatenate([guk.reshape(KVL, NH, QKP)[:, :, :NOPE], guv.reshape(KVL, NH, VD)],
                           axis=-1).reshape(KVL, NH * (NOPE + VD))


BR = 512
BRM = 256


def _layer_fwd(x, w, tabs, batch):
    T = batch * S
    rope_c, rope_s1, rope_s2, dil_tab = tabs
    res = {"x": x}
    row = lambda c: (lambda i: (i, c))
    fix = lambda i: (0, 0)

    h = _pcall("norm_fwd", _rms, (T // BR,),
               [(x, (BR, D), row(0)), (w["norm_g"], (1, D), fix)],
               [((T, D), BF16, (BR, D), row(0))])[0]
    proj = _mm("in_proj", h, w["w_in_t"], tb=True, out_dtype=BF16, tm=2048, tn=1024)
    res["h"], res["proj"] = h, proj
    proj3 = proj.reshape(batch, S, NINP)

    cblk = lambda s: (lambda j, b: (b, 0, 4 * s + j))
    y_a = _pcall("conv_fwd", _conv_math, (4, batch),
                 [(proj3, (None, S, LANE), cblk(0)), (proj3, (None, S, LANE), cblk(1)),
                  (proj3, (None, S, LANE), cblk(2)), (proj3, (None, S, LANE), cblk(3)),
                  (w["conv_w"], (3, LANE), lambda j, b: (0, j)), (w["conv_b"], (1, LANE), lambda j, b: (0, j))],
                 [((batch, S, CW), BF16, (None, S, LANE), lambda j, b: (b, 0, j))])[0].reshape(T, CW)
    res["y_a"] = y_a

    cqn, ckvn = _pcall("mla_pre_fwd", _mla_pre_math, (T // BR,),
                       [(proj, (BR, QL), row(O_CQ // QL)), (proj, (BR, KVL), row(O_CKV // KVL)),
                        (w["q_a_norm_g"], (1, QL), fix), (w["kv_a_norm_g"], (1, KVL), fix)],
                       [((T, QL), BF16, (BR, QL), row(0)), ((T, KVL), BF16, (BR, KVL), row(0))])
    w_uq_p = _pad_heads_uq(w["w_uq"])
    w_uk, w_uv = _split_ukv(w["w_ukv"])
    q = _mm("uq", cqn, w_uq_p, out_dtype=BF16)
    kn = _mm("uk", ckvn, w_uk, out_dtype=BF16)
    v = _mm("uv", ckvn, w_uv, out_dtype=BF16)
    nrr = S // BR
    tab_row = lambda i: (i % nrr, 0)
    qr, kr = _pcall("rope_fwd", _rope_math, (T // BR,),
                    [(q, (BR, NH * QKP), row(0)), (kn, (BR, NH * QKP), row(0)), (proj, (BR, LANE), row(O_KPE // LANE)),
                     (w["mla_q_norm_g"], (1, QKP), fix), (w["mla_k_norm_g"], (1, QKP), fix),
                     (rope_c, (BR, QKP), tab_row), (rope_s1, (BR, QKP), tab_row), (rope_s2, (BR, QKP), tab_row)],
                    [((T, NH * QKP), BF16, (BR, NH * QKP), row(0))] * 2)
    qr = qr.reshape(batch, S, NH * QKP)
    kr = kr.reshape(batch, S, NH * QKP)
    v = v.reshape(batch, S, NH * VD)
    o_b, l_b = _mla_fwd(qr, kr, v)
    ob2 = o_b.reshape(T, NH * VD)
    y_b = _pcall("gateb_fwd", _gate_math, (T // BR,),
                 [(ob2, (BR, 512), row(0)), (proj, (BR, 512), row(O_BZ // 512))],
                 [((T, 512), BF16, (BR, 512), row(0))])[0]
    res.update(cqn=cqn, ckvn=ckvn, q=q, kn=kn, qr=qr, kr=kr, v=v, o_b=o_b, l_b=l_b, ob2=ob2, y_b=y_b,
               w_uq_p=w_uq_p, w_uk=w_uk, w_uv=w_uv)

    gq2 = jnp.tile(w["dil_q_norm_g"].reshape(NG, HD), (1, 2))
    gk2 = jnp.tile(w["dil_k_norm_g"].reshape(NG, HD), (1, 2))
    y_c, o_all, l_all = _dilc_fwd(proj3, gq2, gk2, dil_tab)
    y_c = y_c.reshape(T, DWID)
    res.update(o_all=o_all, l_all=l_all, y_c=y_c)

    pa = _mm("out_a", y_a, w["w_out_a"], out_dtype=BF16)
    pb = _mm("out_b", y_b, w["w_out_b"], out_dtype=BF16)
    pc = _mm("out_c", y_c, w["w_out_c"], out_dtype=BF16)
    merged = _pcall("merge_fwd", _merge_math, (T // BRM,),
                    [(proj, (BRM, D), row(O_G // D + s)) for s in range(3)]
                    + [(w["b_gate"], (1, D), (lambda s: (lambda i: (0, s)))(s)) for s in range(3)]
                    + [(t, (BRM, D), row(0)) for t in (pa, pb, pc)],
                    [((T, D), BF16, (BRM, D), row(0))])[0]
    out = _mm("o_proj", merged, w["w_o"], add=x, tm=1024)
    res.update(pa=pa, pb=pb, pc=pc, merged=merged)
    return out, res


def _norm_bwd_math(x, g, dh, dy):
    _, pull = jax.vjp(_rms, x, g)
    dx, dg = pull(dh)
    return dx + dy, dg


def _layer_bwd(dy, w, res, tabs, batch, after_dw=None):
    T = batch * S
    rope_c, rope_s1, rope_s2, dil_tab = tabs
    row = lambda c: (lambda i: (i, c))
    fix = lambda i: (0, 0)
    x, proj, h = res["x"], res["proj"], res["h"]
    proj3 = proj.reshape(batch, S, NINP)
    g = {}

    d_merged = _mm("o_proj_dx", dy, w["w_o"], tb=True)
    g["w_o"] = _mm("o_proj_dw", res["merged"], dy, ta=True, tm=1024, tk=2048)

    dproj = lax.empty((T, NINP), BF16)
    rows_of = lambda br: (lambda ref, i: ref.at[pl.ds(pl.multiple_of(i * br, br), br)])

    def merge_bwd(*args):
        dg0, dg1, dg2, db0, db1, db2, dpa, dpb, dpc = _vjp_of(_merge_math, 9)(*args, n_prim=9)
        return db0, db1, db2, dpa, dpb, dpc, jnp.concatenate([dg0, dg1, dg2], axis=1)

    db0, db1, db2, dpa, dpb, dpc, dproj = _pcall(
        "merge_bwd", merge_bwd, (T // BRM,),
        [(proj, (BRM, D), row(O_G // D + s)) for s in range(3)]
        + [(w["b_gate"], (1, D), (lambda s: (lambda i: (0, s)))(s)) for s in range(3)]
        + [(t, (BRM, D), row(0)) for t in (res["pa"], res["pb"], res["pc"])]
        + [(d_merged, (BRM, D), row(0))],
        [((1, D), F32, (1, D), fix, True)] * 3 + [((T, D), BF16, (BRM, D), row(0))] * 3,
        into=(dproj, [((BRM, 3 * D), lambda ref, i: rows_of(BRM)(ref, i).at[:, O_G:O_G + 3 * D])]))
    g["b_gate"] = jnp.concatenate([db0, db1, db2], axis=1)

    d_ya = _mm("out_a_dx", dpa, w["w_out_a"], tb=True)
    d_yb = _mm("out_b_dx", dpb, w["w_out_b"], tb=True)
    d_yc = _mm("out_c_dx", dpc, w["w_out_c"], tb=True)
    g["w_out_a"] = _mm("out_a_dw", res["y_a"], dpa, ta=True, tk=T)
    g["w_out_b"] = _mm("out_b_dw", res["y_b"], dpb, ta=True, tk=T)
    g["w_out_c"] = _mm("out_c_dw", res["y_c"], dpc, ta=True, tk=T)

    cblk = lambda s: (lambda j, b: (b, 0, 4 * s + j))
    oblk = lambda j, b: (b, 0, j)
    def conv_bwd(*args):
        d_ab, d_ac, d_ax, d_az, dcw, dcb = _vjp_of(_conv_math, 6)(*args, n_prim=6)
        return dcw, dcb, d_ab, d_ac, d_ax, d_az

    a_col = lambda s_: (lambda ref, j, b: ref.at[b, :, pl.ds(pl.multiple_of(O_A + s_ * CW + j * LANE, LANE), LANE)])
    g["conv_w"], g["conv_b"], dproj3 = _pcall(
        "conv_bwd", conv_bwd, (4, batch),
        [(proj3, (None, S, LANE), cblk(s)) for s in range(4)]
        + [(w["conv_w"], (3, LANE), lambda j, b: (0, j)), (w["conv_b"], (1, LANE), lambda j, b: (0, j)),
           (d_ya.reshape(batch, S, CW), (None, S, LANE), oblk)],
        [((3, CW), F32, (3, LANE), lambda j, b: (0, j), True), ((1, CW), F32, (1, LANE), lambda j, b: (0, j), True)],
        into=(dproj.reshape(batch, S, NINP), [((S, LANE), a_col(s_)) for s_ in range(4)]))
    dproj = dproj3.reshape(T, NINP)

    gate_bwd = functools.partial(_vjp_of(_gate_math, 2), n_prim=2)
    d_ob, dproj = _pcall("gateb_bwd", gate_bwd, (T // BR,),
                         [(res["ob2"], (BR, 512), row(0)), (proj, (BR, 512), row(O_BZ // 512)), (d_yb, (BR, 512), row(0))],
                         [((T, 512), F32, (BR, 512), row(0))],
                         into=(dproj, [((BR, 512), lambda ref, i: rows_of(BR)(ref, i).at[:, O_BZ:O_BZ + 512])]))
    dqr, dkr, dv = _mla_bwd(res["qr"], res["kr"], res["v"], d_ob.reshape(batch, S, NH * VD), res["o_b"], res["l_b"])
    nrr = S // BR
    tab_row = lambda i: (i % nrr, 0)
    def rope_bwd(*args):
        d_q, d_kn, d_kpe, dgq, dgk = _vjp_of(_rope_math, 5)(*args, n_prim=8)
        return d_q, d_kn, dgq, dgk, d_kpe

    d_q, d_kn, g["mla_q_norm_g"], g["mla_k_norm_g"], dproj = _pcall(
        "rope_bwd", rope_bwd, (T // BR,),
        [(res["q"], (BR, NH * QKP), row(0)), (res["kn"], (BR, NH * QKP), row(0)), (proj, (BR, LANE), row(O_KPE // LANE)),
         (w["mla_q_norm_g"], (1, QKP), fix), (w["mla_k_norm_g"], (1, QKP), fix),
         (rope_c, (BR, QKP), tab_row), (rope_s1, (BR, QKP), tab_row), (rope_s2, (BR, QKP), tab_row),
         (dqr.reshape(T, NH * QKP), (BR, NH * QKP), row(0)), (dkr.reshape(T, NH * QKP), (BR, NH * QKP), row(0))],
        [((T, NH * QKP), BF16, (BR, NH * QKP), row(0))] * 2 + [((1, QKP), F32, (1, QKP), fix, True)] * 2,
        into=(dproj, [((BR, LANE), lambda ref, i: rows_of(BR)(ref, i).at[:, O_KPE:O_KPE + LANE])]))
    dv = dv.reshape(T, NH * VD)
    d_cqn = _mm("uq_dx", d_q, res["w_uq_p"], tb=True)
    d_ckvn = _mm("uk_dx", d_kn, res["w_uk"], tb=True)
    d_ckvn = _mm("uv_dx", dv, res["w_uv"], tb=True, add=d_ckvn)
    g["w_uq"] = _unpad_heads_uq(_mm("uq_dw", res["cqn"], d_q, ta=True, tk=T))
    g["w_ukv"] = _join_ukv(_mm("uk_dw", res["ckvn"], d_kn, ta=True, tk=T),
                           _mm("uv_dw", res["ckvn"], dv, ta=True, tk=T))
    def pre_bwd(*args):
        d_cq, d_ckv, dgq, dgkv = _vjp_of(_mla_pre_math, 4)(*args, n_prim=4)
        return dgq, dgkv, jnp.concatenate([d_cq, d_ckv], axis=1)

    g["q_a_norm_g"], g["kv_a_norm_g"], dproj = _pcall(
        "mla_pre_bwd", pre_bwd, (T // BR,),
        [(proj, (BR, QL), row(O_CQ // QL)), (proj, (BR, KVL), row(O_CKV // KVL)),
         (w["q_a_norm_g"], (1, QL), fix), (w["kv_a_norm_g"], (1, KVL), fix),
         (d_cqn, (BR, QL), row(0)), (d_ckvn, (BR, KVL), row(0))],
        [((1, QL), F32, (1, QL), fix, True), ((1, KVL), F32, (1, KVL), fix, True)],
        into=(dproj, [((BR, QL + KVL), lambda ref, i: rows_of(BR)(ref, i).at[:, O_CQ:O_CQ + QL + KVL])]))

    gq2 = jnp.tile(w["dil_q_norm_g"].reshape(NG, HD), (1, 2))
    gk2 = jnp.tile(w["dil_k_norm_g"].reshape(NG, HD), (1, 2))
    dproj3, dgq, dgk = _dilc_bwd(proj3, gq2, gk2, dil_tab, res["o_all"], res["l_all"],
                                 d_yc.reshape(batch, S, DWID), dproj.reshape(batch, S, NINP))
    dproj = dproj3.reshape(T, NINP)
    g["dil_q_norm_g"] = dgq[:, :HD] + dgq[:, HD:]
    g["dil_k_norm_g"] = dgk[:, :HD] + dgk[:, HD:]

    g["w_in_t"] = _mm("in_proj_dw", dproj, h, ta=True, tm=1024, tk=T)
    dep = after_dw(g) if after_dw is not None else None
    d_h = _mm("in_proj_dx", dproj, w["w_in_t"], dep=dep, tm=1024, tk=NINP // 4)
    dx, g["norm_g"] = _pcall("norm_bwd", _norm_bwd_math, (T // BR,),
                             [(x, (BR, D), row(0)), (w["norm_g"], (1, D), fix), (d_h, (BR, D), row(0)),
                              (dy, (BR, D), row(0))],
                             [((T, D), F32, (BR, D), row(0)), ((1, D), F32, (1, D), fix, True)])
    return dx, g


def _loss_math(y, t):
    e = y - t
    return e * (1.0 / D), 0.5 * jnp.sum(jnp.sum(e * e, axis=-1, keepdims=True) / D, axis=0, keepdims=True)


def _local_step(x, target, ws, batch):
    T = batch * S
    tabs = _rope_tables() + (_dil_slopes(),)
    saved = []
    y = x
    for l in range(NL):
        y, res = _layer_fwd(y, ws[l], tabs, batch)
        saved.append(res)
    row = lambda i: (i, 0)
    dy, loss = _pcall("loss", _loss_math, (T // BR,),
                      [(y, (BR, D), row), (target, (BR, D), row)],
                      [((T, D), F32, (BR, D), row), ((1, 1), F32, (1, 1), lambda i: (0, 0), True)])
    grads = [None] * NL
    for l in reversed(range(NL)):
        dy, grads[l] = _layer_bwd(dy, ws[l], saved[l], tabs, batch)
    return loss, dy, grads


ANY = pl.BlockSpec(memory_space=pl.ANY)
U32 = jnp.uint32
WSH = NIN // 4
WA = KPE_END
WB = WSH - WA
CWD = 512
PACK_ROWS = 1472
HW = PACK_W // 2


def _me():
    return lax.axis_index("x"), lax.axis_index("y"), lax.axis_index("c")


def _piece_rows(k):
    a = k * WSH + jnp.where(k > 0, NINP - NIN, 0)
    b = k * WSH + WA + (NINP - NIN)
    return ((0, pl.multiple_of(a, 8), WA), (WA, pl.multiple_of(b, 8), WB))


def _pack_words(lo, hi):
    ul = lax.bitcast_convert_type(lo.astype(BF16).astype(F32), U32)
    uh = lax.bitcast_convert_type(hi.astype(BF16).astype(F32), U32)
    w = jnp.bitwise_or(jnp.bitwise_and(uh, jnp.uint32(0xFFFF0000)), jnp.right_shift(ul, jnp.uint32(16)))
    return lax.bitcast_convert_type(w, F32)


def _unpack_words(w):
    w = lax.bitcast_convert_type(w, U32)
    lo = lax.bitcast_convert_type(jnp.left_shift(w, jnp.uint32(16)), F32)
    hi = lax.bitcast_convert_type(jnp.bitwise_and(w, jnp.uint32(0xFFFF0000)), F32)
    return lo, hi


def _all_gather(wc, sp):
    def body(w_ref, s_ref, ow_ref, os_ref, send_sems, recv_sems):
        x, y, c = _me()
        k_me = 2 * x + y
        sib = (x, y, 1 - c)
        chips = [(1 - x, y), (x, 1 - y), (1 - x, 1 - y)]
        wcols = lambda cc: pl.ds(pl.multiple_of(cc * (CWD // 2), LANE), CWD // 2)
        scols = lambda cc: pl.ds(pl.multiple_of(cc * HW, LANE), HW)

        def windows(k, cc):
            pcs = _piece_rows(k)
            return ([(w_ref.at[pl.ds(l0, n), wcols(cc)], ow_ref.at[pl.ds(p0, n), wcols(cc)]) for l0, p0, n in pcs]
                    + [(s_ref.at[:, scols(cc)], os_ref.at[k, :, scols(cc)])])

        def copy(i, src, dst, to):
            return pltpu.make_async_remote_copy(src_ref=src, dst_ref=dst, send_sem=send_sems.at[i],
                                                recv_sem=recv_sems.at[i], device_id=to, device_id_type=MESH)

        def own_windows():
            return ([(w_ref.at[pl.ds(l0, n)], ow_ref.at[pl.ds(p0, n)]) for l0, p0, n in _piece_rows(k_me)]
                    + [(s_ref, os_ref.at[k_me])])

        first = [copy(18 + i, src, dst, sib) for i, (src, dst) in enumerate(own_windows())]
        for j, (cx, cy) in enumerate(chips):
            for i, (src, dst) in enumerate(windows(k_me, c)):
                first.append(copy(3 * j + i, src, dst, (cx, cy, c)))
        for cp in first:
            cp.start()
        passed = []
        for j, (cx, cy) in enumerate(chips):
            for i, (_, dst) in enumerate(windows(2 * cx + cy, c)):
                copy(3 * j + i, dst, dst, (cx, cy, c)).wait_recv()
                cp = copy(9 + 3 * j + i, dst, dst, sib)
                cp.start()
                passed.append(cp)
        for j, (cx, cy) in enumerate(chips):
            for i, (_, dst) in enumerate(windows(2 * cx + cy, 1 - c)):
                copy(9 + 3 * j + i, dst, dst, sib).wait_recv()
        for i, (_, dst) in enumerate(own_windows()):
            copy(18 + i, dst, dst, sib).wait_recv()
        for cp in first + passed:
            cp.wait_send()

    return pl.pallas_call(
        body,
        out_shape=[jax.ShapeDtypeStruct((NINP, CWD), F32), jax.ShapeDtypeStruct((4, PACK_ROWS, PACK_W), BF16)],
        in_specs=[ANY, ANY], out_specs=[ANY, ANY],
        scratch_shapes=[pltpu.SemaphoreType.DMA((21,)), pltpu.SemaphoreType.DMA((21,))],
        name="weights_all_gather",
    )(wc, sp)


HBM = pl.BlockSpec(memory_space=pltpu.HBM)
SEM = pl.BlockSpec(memory_space=pltpu.SEMAPHORE)
EFFECT = pltpu.SideEffectType.DATAFLOW_SIDE_EFFECTING


def _in_hbm(a):
    return pltpu.with_memory_space_constraint(a, pltpu.HBM)


def _ag_windows(w_ref, s_ref, lw_ref, ls_ref, k, cc):
    wcols = pl.ds(pl.multiple_of(cc * (CWD // 2), LANE), CWD // 2)
    scols = pl.ds(pl.multiple_of(cc * HW, LANE), HW)
    return ([(w_ref.at[pl.ds(l0, n), wcols], lw_ref.at[pl.ds(p0, n), wcols]) for l0, p0, n in _piece_rows(k)]
            + [(s_ref.at[:, scols], ls_ref.at[k, :, scols])])


def _ag_ici_copies(w_ref, s_ref, lw_ref, ls_ref, send_sems, recv_sems):
    x, y, c = _me()
    mine, theirs = [], []
    for j, (cx, cy) in enumerate([(1 - x, y), (x, 1 - y), (1 - x, 1 - y)]):
        for i, ((src, dst), (_, got)) in enumerate(zip(_ag_windows(w_ref, s_ref, lw_ref, ls_ref, 2 * x + y, c),
                                                       _ag_windows(w_ref, s_ref, lw_ref, ls_ref, 2 * cx + cy, c))):
            mk = lambda s_, d_: pltpu.make_async_remote_copy(
                src_ref=s_, dst_ref=d_, send_sem=send_sems.at[3 * j + i], recv_sem=recv_sems.at[3 * j + i],
                device_id=(cx, cy, c), device_id_type=MESH)
            mine.append(mk(src, dst))
            theirs.append(mk(got, got))
    return mine, theirs


def _ag_ici_start(wc, sp, dep):
    def body(w_ref, s_ref, lw_ref, ls_ref, dep_ref, send_sems, recv_sems, w_thru, s_thru, lw_thru, ls_thru, token):
        mine, _ = _ag_ici_copies(w_ref, s_ref, lw_ref, ls_ref, send_sems, recv_sems)
        for cp in mine:
            cp.start()
        token[...] = jnp.zeros_like(token)

    return pl.pallas_call(
        body, name="weights_gather_start",
        out_shape=(pltpu.SemaphoreType.DMA((9,)), pltpu.SemaphoreType.DMA((9,)), pltpu.HBM(wc.shape, wc.dtype),
                   pltpu.HBM(sp.shape, sp.dtype), pltpu.HBM((NINP, CWD), F32), pltpu.HBM((4, PACK_ROWS, PACK_W), BF16),
                   jax.ShapeDtypeStruct((8, LANE), F32)),
        in_specs=(HBM, HBM, HBM, HBM, ANY),
        out_specs=(SEM, SEM, HBM, HBM, HBM, HBM, pl.BlockSpec(memory_space=pltpu.VMEM)),
        input_output_aliases={0: 2, 1: 3, 2: 4, 3: 5},
        compiler_params=pltpu.CompilerParams(has_side_effects=EFFECT),
    )(_in_hbm(wc), _in_hbm(sp), _in_hbm(lax.empty((NINP, CWD), F32)), _in_hbm(lax.empty((4, PACK_ROWS, PACK_W), BF16)), dep)


def _ag_ici_wait(send_sems, recv_sems, wc, sp, lw, ls, after):
    def body(w_ref, s_ref, lw_ref, ls_ref, send_sems, recv_sems, after_ref, w_dead, s_dead, lw_out, ls_out):
        mine, theirs = _ag_ici_copies(w_ref, s_ref, lw_ref, ls_ref, send_sems, recv_sems)
        for cp in mine:
            cp.wait_send()
        for cp in theirs:
            cp.wait_recv()

    out = pl.pallas_call(
        body, name="weights_gather_wait",
        out_shape=(pltpu.HBM(wc.shape, wc.dtype), pltpu.HBM(sp.shape, sp.dtype), pltpu.HBM(lw.shape, lw.dtype),
                   pltpu.HBM(ls.shape, ls.dtype)),
        in_specs=(HBM, HBM, HBM, HBM, SEM, SEM, ANY), out_specs=(HBM, HBM, HBM, HBM),
        input_output_aliases={0: 0, 1: 1, 2: 2, 3: 3},
        compiler_params=pltpu.CompilerParams(has_side_effects=EFFECT),
    )(wc, sp, lw, ls, send_sems, recv_sems, after)
    return out[2], out[3]


def _ag_finish(wc, sp, lw, ls):
    def body(w_ref, s_ref, lw_ref, ls_ref, ow_ref, os_ref, send_sems, recv_sems):
        x, y, c = _me()
        k_me = 2 * x + y
        sib = (x, y, 1 - c)
        chips = [(1 - x, y), (x, 1 - y), (1 - x, 1 - y)]

        def copy(i, src, dst):
            return pltpu.make_async_remote_copy(src_ref=src, dst_ref=dst, send_sem=send_sems.at[i],
                                                recv_sem=recv_sems.at[i], device_id=sib, device_id_type=MESH)

        def own_windows():
            return ([(w_ref.at[pl.ds(l0, n)], ow_ref.at[pl.ds(p0, n)]) for l0, p0, n in _piece_rows(k_me)]
                    + [(s_ref, os_ref.at[k_me])])

        out = [copy(9 + i, src, dst) for i, (src, dst) in enumerate(own_windows())]
        for j, (cx, cy) in enumerate(chips):
            landed = _ag_windows(w_ref, s_ref, lw_ref, ls_ref, 2 * cx + cy, c)
            for i, (_, dst) in enumerate(_ag_windows(w_ref, s_ref, ow_ref, os_ref, 2 * cx + cy, c)):
                out.append(copy(3 * j + i, landed[i][1], dst))
        for cp in out:
            cp.start()
        for j, (cx, cy) in enumerate(chips):
            for i, (_, dst) in enumerate(_ag_windows(w_ref, s_ref, ow_ref, os_ref, 2 * cx + cy, 1 - c)):
                copy(3 * j + i, dst, dst).wait_recv()
        for i, (_, dst) in enumerate(own_windows()):
            copy(9 + i, dst, dst).wait_recv()
        for cp in out:
            cp.wait_send()

    return pl.pallas_call(
        body,
        out_shape=[jax.ShapeDtypeStruct(lw.shape, lw.dtype), jax.ShapeDtypeStruct(ls.shape, ls.dtype)],
        in_specs=[ANY] * 4, out_specs=[ANY, ANY],
        input_output_aliases={2: 0, 3: 1},
        scratch_shapes=[pltpu.SemaphoreType.DMA((12,)), pltpu.SemaphoreType.DMA((12,))],
        name="weights_gather_finish",
    )(wc, sp, lw, ls)


UNPACK_BR = 512


def _unpack_w_in(cont):
    def body(c_ref, o_ref):
        lo, hi = _unpack_words(c_ref[...])
        r = pl.program_id(0) * UNPACK_BR + lax.broadcasted_iota(jnp.int32, (UNPACK_BR, CWD), 0)
        pad = jnp.logical_and(r >= KPE_END, r < KPE_END + NINP - NIN)
        o_ref[:, 0:CWD] = jnp.where(pad, 0.0, lo).astype(BF16)
        o_ref[:, CWD:2 * CWD] = jnp.where(pad, 0.0, hi).astype(BF16)

    return pl.pallas_call(
        body, grid=(NINP // UNPACK_BR,),
        in_specs=[pl.BlockSpec((UNPACK_BR, CWD), lambda i: (i, 0))],
        out_specs=pl.BlockSpec((UNPACK_BR, D), lambda i: (i, 0)),
        out_shape=jax.ShapeDtypeStruct((NINP, D), BF16),
        name="w_in_unpack",
        compiler_params=pltpu.CompilerParams(dimension_semantics=("arbitrary",), vmem_limit_bytes=VMEM_LIMIT),
    )(cont)


def _rs_swap(gw, gs):
    def body(w_ref, s_ref, rw_ref, rs_ref, send_sems, recv_sems):
        x, y, c = _me()
        oc = 1 - c
        cps = [pltpu.make_async_remote_copy(src_ref=w_ref.at[:, pl.ds(pl.multiple_of(oc * (D // 2), LANE), D // 2)],
                                            dst_ref=rw_ref, send_sem=send_sems.at[0], recv_sem=recv_sems.at[0],
                                            device_id=(x, y, oc), device_id_type=MESH),
               pltpu.make_async_remote_copy(src_ref=s_ref.at[:, :, pl.ds(pl.multiple_of(oc * HW, LANE), HW)],
                                            dst_ref=rs_ref, send_sem=send_sems.at[1], recv_sem=recv_sems.at[1],
                                            device_id=(x, y, oc), device_id_type=MESH)]
        for cp in cps:
            cp.start()
        for cp in cps:
            cp.wait()

    return pl.pallas_call(
        body,
        out_shape=[jax.ShapeDtypeStruct((NINP, D // 2), F32), jax.ShapeDtypeStruct((4, PACK_ROWS, HW), F32)],
        in_specs=[ANY, ANY], out_specs=[ANY, ANY],
        scratch_shapes=[pltpu.SemaphoreType.DMA((2,)), pltpu.SemaphoreType.DMA((2,))],
        name="grads_sibling_swap",
    )(gw, gs)


SUM_BR = 512


def _rs_chip_sum_w(gw, rw, cidx):
    def body(c_ref, g_ref, r_ref, o_ref):
        s = g_ref[...] + r_ref[...]
        q = D // 8
        o_ref[...] = jnp.concatenate([_pack_words(s[:, 0:q], s[:, q:2 * q]),
                                      _pack_words(s[:, 2 * q:3 * q], s[:, 3 * q:4 * q])], axis=1)

    return pl.pallas_call(
        body,
        grid_spec=pltpu.PrefetchScalarGridSpec(
            num_scalar_prefetch=1, grid=(NINP // SUM_BR,),
            in_specs=[pl.BlockSpec((SUM_BR, D // 2), lambda i, cr: (i, cr[0])),
                      pl.BlockSpec((SUM_BR, D // 2), lambda i, cr: (i, 0))],
            out_specs=pl.BlockSpec((SUM_BR, D // 4), lambda i, cr: (i, 0))),
        out_shape=jax.ShapeDtypeStruct((NINP, D // 4), F32),
        name="grads_chip_sum_w",
        compiler_params=pltpu.CompilerParams(dimension_semantics=("arbitrary",), vmem_limit_bytes=VMEM_LIMIT),
    )(cidx, gw, rw)


def _rs_chip_sum_s(gs, rs, cidx):
    def body(c_ref, g_ref, r_ref, o_ref):
        o_ref[...] = (g_ref[...] + r_ref[...]).astype(BF16)

    return pl.pallas_call(
        body,
        grid_spec=pltpu.PrefetchScalarGridSpec(
            num_scalar_prefetch=1, grid=(4,),
            in_specs=[pl.BlockSpec((None, PACK_ROWS, HW), lambda j, cr: (j, 0, cr[0])),
                      pl.BlockSpec((None, PACK_ROWS, HW), lambda j, cr: (j, 0, 0))],
            out_specs=pl.BlockSpec((None, PACK_ROWS, HW), lambda j, cr: (j, 0, 0))),
        out_shape=jax.ShapeDtypeStruct((4, PACK_ROWS, HW), BF16),
        name="grads_chip_sum_s",
        compiler_params=pltpu.CompilerParams(dimension_semantics=("arbitrary",), vmem_limit_bytes=VMEM_LIMIT),
    )(cidx, gs, rs)


def _rs_exchange_copies(sw_ref, ss_ref, r2w_ref, r2s_ref, send_sems, recv_sems):
    x, y, c = _me()
    mine, theirs = [], []
    for j, (cx, cy) in enumerate([(1 - x, y), (x, 1 - y), (1 - x, 1 - y)]):
        def mk(i, src, dst):
            return pltpu.make_async_remote_copy(src_ref=src, dst_ref=dst, send_sem=send_sems.at[3 * j + i],
                                                recv_sem=recv_sems.at[3 * j + i], device_id=(cx, cy, c), device_id_type=MESH)
        for i, (l0, p0, n) in enumerate(_piece_rows(2 * cx + cy)):
            mine.append(mk(i, sw_ref.at[pl.ds(p0, n)], r2w_ref.at[j, pl.ds(l0, n)]))
            theirs.append(mk(i, r2w_ref.at[j, pl.ds(l0, n)], r2w_ref.at[j, pl.ds(l0, n)]))
        mine.append(mk(2, ss_ref.at[2 * cx + cy], r2s_ref.at[j]))
        theirs.append(mk(2, r2s_ref.at[j], r2s_ref.at[j]))
    return mine, theirs


def _rs_exchange_start(sw, ss, tag):
    def body(sw_ref, ss_ref, r2w_ref, r2s_ref, send_sems, recv_sems, sw_thru, ss_thru, r2w_thru, r2s_thru, token):
        mine, _ = _rs_exchange_copies(sw_ref, ss_ref, r2w_ref, r2s_ref, send_sems, recv_sems)
        for cp in mine:
            cp.start()
        token[...] = jnp.zeros_like(token)

    return pl.pallas_call(
        body, name="grads_exchange_start_" + tag,
        out_shape=(pltpu.SemaphoreType.DMA((9,)), pltpu.SemaphoreType.DMA((9,)), pltpu.HBM(sw.shape, sw.dtype),
                   pltpu.HBM(ss.shape, ss.dtype), pltpu.HBM((3, WSH, D // 4), F32), pltpu.HBM((3, PACK_ROWS, HW), BF16),
                   jax.ShapeDtypeStruct((8, LANE), F32)),
        in_specs=(HBM, HBM, HBM, HBM),
        out_specs=(SEM, SEM, HBM, HBM, HBM, HBM, pl.BlockSpec(memory_space=pltpu.VMEM)),
        input_output_aliases={0: 2, 1: 3, 2: 4, 3: 5},
        compiler_params=pltpu.CompilerParams(has_side_effects=EFFECT),
    )(_in_hbm(sw), _in_hbm(ss), _in_hbm(lax.empty((3, WSH, D // 4), F32)), _in_hbm(lax.empty((3, PACK_ROWS, HW), BF16)))


def _rs_exchange_wait(send_sems, recv_sems, sw, ss, r2w, r2s, after, tag):
    def body(sw_ref, ss_ref, r2w_ref, r2s_ref, send_sems, recv_sems, after_ref, sw_dead, ss_dead, r2w_out, r2s_out):
        mine, theirs = _rs_exchange_copies(sw_ref, ss_ref, r2w_ref, r2s_ref, send_sems, recv_sems)
        for cp in mine:
            cp.wait_send()
        for cp in theirs:
            cp.wait_recv()

    out = pl.pallas_call(
        body, name="grads_exchange_wait_" + tag,
        out_shape=(pltpu.HBM(sw.shape, sw.dtype), pltpu.HBM(ss.shape, ss.dtype), pltpu.HBM(r2w.shape, r2w.dtype),
                   pltpu.HBM(r2s.shape, r2s.dtype)),
        in_specs=(HBM, HBM, HBM, HBM, SEM, SEM, ANY), out_specs=(HBM, HBM, HBM, HBM),
        input_output_aliases={0: 0, 1: 1, 2: 2, 3: 3},
        compiler_params=pltpu.CompilerParams(has_side_effects=EFFECT),
    )(sw, ss, r2w, r2s, send_sems, recv_sems, after)
    return out[2], out[3]


def _rs_final_w(gw, rw, r2w, idx):
    q = D // 8

    def body(i_ref, g_ref, r_ref, p_ref, o_ref, gbuf, rbuf, sems):
        i = pl.program_id(0)
        k, c = i_ref[0], i_ref[1]
        cps = []
        for n_, (l0, p0, n) in enumerate(_piece_rows(k)):
            gcol = pl.ds(pl.multiple_of(c * (D // 2) + i * 2 * q, LANE), 2 * q)
            rcol = pl.ds(pl.multiple_of(i * 2 * q, LANE), 2 * q)
            cps.append(pltpu.make_async_copy(g_ref.at[pl.ds(p0, n), gcol], gbuf.at[pl.ds(l0, n)], sems.at[2 * n_]))
            cps.append(pltpu.make_async_copy(r_ref.at[pl.ds(p0, n), rcol], rbuf.at[pl.ds(l0, n)], sems.at[2 * n_ + 1]))
        for cp in cps:
            cp.start()
        for cp in cps:
            cp.wait()
        acc = gbuf[...] + rbuf[...]
        for j in range(3):
            lo, hi = _unpack_words(p_ref[j])
            acc = acc + jnp.concatenate([lo, hi], axis=1)
        o_ref[...] = acc

    return pl.pallas_call(
        body,
        grid_spec=pltpu.PrefetchScalarGridSpec(
            num_scalar_prefetch=1, grid=(2,),
            in_specs=[ANY, ANY, pl.BlockSpec((3, WSH, q), lambda i, ir: (0, 0, i))],
            out_specs=pl.BlockSpec((WSH, 2 * q), lambda i, ir: (0, i)),
            scratch_shapes=[pltpu.VMEM((WSH, 2 * q), F32), pltpu.VMEM((WSH, 2 * q), F32), pltpu.SemaphoreType.DMA((4,))]),
        out_shape=jax.ShapeDtypeStruct((WSH, D // 2), F32),
        name="grads_final_sum_w",
        compiler_params=pltpu.CompilerParams(dimension_semantics=("arbitrary",), vmem_limit_bytes=VMEM_LIMIT),
    )(idx, gw, rw, r2w)


def _rs_final_s(gs, rs, r2s, idx):
    def body(i_ref, g_ref, r_ref, p_ref, o_ref):
        acc = g_ref[...] + r_ref[...]
        for j in range(3):
            acc = acc + p_ref[j].astype(F32)
        o_ref[...] = acc

    return pl.pallas_call(
        body,
        grid_spec=pltpu.PrefetchScalarGridSpec(
            num_scalar_prefetch=1, grid=(1,),
            in_specs=[pl.BlockSpec((None, PACK_ROWS, HW), lambda i, ir: (ir[0], 0, ir[1])),
                      pl.BlockSpec((None, PACK_ROWS, HW), lambda i, ir: (ir[0], 0, 0)),
                      pl.BlockSpec((3, PACK_ROWS, HW), lambda i, ir: (0, 0, 0))],
            out_specs=pl.BlockSpec((PACK_ROWS, HW), lambda i, ir: (0, 0))),
        out_shape=jax.ShapeDtypeStruct((PACK_ROWS, HW), F32),
        name="grads_final_sum_s",
        compiler_params=pltpu.CompilerParams(dimension_semantics=("arbitrary",), vmem_limit_bytes=VMEM_LIMIT),
    )(idx, gs, rs, r2s)


def _rs_share(fw, fs):
    def body(w_ref, s_ref, ow_ref, os_ref, send_sems, recv_sems):
        x, y, c = _me()
        cps = [pltpu.make_async_remote_copy(src_ref=w_ref, dst_ref=ow_ref, send_sem=send_sems.at[0],
                                            recv_sem=recv_sems.at[0], device_id=(x, y, 1 - c), device_id_type=MESH),
               pltpu.make_async_remote_copy(src_ref=s_ref, dst_ref=os_ref, send_sem=send_sems.at[1],
                                            recv_sem=recv_sems.at[1], device_id=(x, y, 1 - c), device_id_type=MESH)]
        for cp in cps:
            cp.start()
        for cp in cps:
            cp.wait()

    return pl.pallas_call(
        body,
        out_shape=[jax.ShapeDtypeStruct((WSH, D // 2), F32), jax.ShapeDtypeStruct((PACK_ROWS, HW), F32)],
        in_specs=[ANY, ANY], out_specs=[ANY, ANY],
        scratch_shapes=[pltpu.SemaphoreType.DMA((2,)), pltpu.SemaphoreType.DMA((2,))],
        name="grads_share",
    )(fw, fs)


def _both_halves(mine, other, c):
    return jnp.where(c == 0, jnp.concatenate([mine, other], axis=1), jnp.concatenate([other, mine], axis=1))


def _rs_begin(gw, gs):
    x, y, c = _me()
    cidx = jnp.reshape(c, (1,)).astype(jnp.int32)
    rw, rs = _rs_swap(gw, gs)
    return dict(gw=gw, gs=gs, rw=rw, rs=rs, sw=_rs_chip_sum_w(gw, rw, cidx), ss=_rs_chip_sum_s(gs, rs, cidx))


def _rs_end(st, r2w, r2s):
    x, y, c = _me()
    idx = jnp.stack([2 * x + y, c]).astype(jnp.int32)
    fw = _rs_final_w(st["gw"], st["rw"], r2w, idx)
    fs = _rs_final_s(st["gs"], st["rs"], r2s, idx)
    ow, os_ = _rs_share(fw, fs)
    return _both_halves(fw, ow, c), _both_halves(fs, os_, c)


def _all_reduce_small(gs):
    rows = gs.shape[0]

    def body(g_ref, o_ref, buf, send_sems, recv_sems):
        x, y, c = _me()
        me = 4 * x + 2 * y + c
        buf[me] = g_ref[...]
        cps = []
        for r in range(1, 8):
            fx, fy, fc = (r >> 2) & 1, (r >> 1) & 1, r & 1
            px, py, pc = jnp.bitwise_xor(x, fx), jnp.bitwise_xor(y, fy), jnp.bitwise_xor(c, fc)
            cps.append((pltpu.make_async_remote_copy(
                src_ref=g_ref, dst_ref=buf.at[me], send_sem=send_sems.at[r - 1], recv_sem=recv_sems.at[r - 1],
                device_id=(px, py, pc), device_id_type=MESH), 4 * px + 2 * py + pc))
        for cp, _ in cps:
            cp.start()
        for r, (cp, peer) in enumerate(cps):
            pltpu.make_async_remote_copy(
                src_ref=g_ref, dst_ref=buf.at[peer], send_sem=send_sems.at[r], recv_sem=recv_sems.at[r],
                device_id=(x, y, c), device_id_type=MESH).wait_recv()
        for cp, _ in cps:
            cp.wait_send()
        acc = buf[0]
        for k in range(1, 8):
            acc = acc + buf[k]
        o_ref[...] = acc

    return pl.pallas_call(
        body,
        out_shape=jax.ShapeDtypeStruct((rows, LANE), F32),
        in_specs=[pl.BlockSpec(memory_space=pltpu.VMEM)],
        out_specs=pl.BlockSpec(memory_space=pltpu.VMEM),
        scratch_shapes=[pltpu.VMEM((8, rows, LANE), F32), pltpu.SemaphoreType.DMA((7,)), pltpu.SemaphoreType.DMA((7,))],
        name="small_grads_all_reduce",
    )(gs)


PACK_SPLIT = (("w_uq", 96, (QL, 192)), ("w_ukv", 64, (KVL, 256)),
              ("w_out_a", 256, (CW, 256)), ("w_out_b", 256, (CW, 256)), ("w_out_c", 256, (CW, 256)),
              ("w_o", 512, (256, D)))
MAT_ROWS = 1440
CONV_SHARD = 3 * 128


def _w_in_words(w_in_shard):
    t = w_in_shard.T
    return _pack_words(t[:, :CWD], t[:, CWD:])


def _pack_weights(wl):
    parts = [wl[n].astype(BF16).reshape(-1, PACK_W) for n, _, _ in PACK_SPLIT]
    cw = wl["conv_w"].reshape(-1)
    hi = cw.astype(BF16)
    r1 = cw - hi.astype(F32)
    mid = r1.astype(BF16)
    lo = (r1 - mid.astype(F32)).astype(BF16)
    cterms = jnp.pad(jnp.concatenate([hi, mid, lo]), (0, 3 * PACK_W - 3 * CONV_SHARD)).reshape(3, PACK_W)
    tail = jnp.pad(cterms, ((0, PACK_ROWS - MAT_ROWS - 3), (0, 0)))
    return jnp.concatenate(parts + [tail], axis=0)


def _unpack_weights(gath):
    out = {}
    r = 0
    for n, nrows, shp in PACK_SPLIT:
        t = gath[:, r:r + nrows].reshape((4,) + shp)
        r += nrows
        if n == "w_o":
            out[n] = t.reshape(4 * shp[0], shp[1])
        else:
            out[n] = t.transpose(1, 0, 2).reshape(shp[0], 4 * shp[1])
    ct = gath[:, r:r + 3].reshape(4, 3 * PACK_W)[:, :3 * CONV_SHARD].astype(F32).reshape(4, 3, CONV_SHARD)
    cw = (ct[:, 0] + ct[:, 1]) + ct[:, 2]
    out["conv_w"] = cw.reshape(4, 3, 128).transpose(1, 0, 2).reshape(3, CW)
    return out


def _pack_grads(g):
    parts = []
    for n, nrows, shp in PACK_SPLIT:
        t = g[n]
        if n == "w_o":
            t = t.reshape((4,) + shp)
        else:
            t = t.reshape(shp[0], 4, shp[1]).transpose(1, 0, 2)
        parts.append(t.reshape(4, nrows, PACK_W))
    cw = g["conv_w"].reshape(3, 4, 128).transpose(1, 0, 2).reshape(4, 1, CONV_SHARD)
    parts.append(jnp.pad(cw, ((0, 0), (0, PACK_ROWS - MAT_ROWS - 1), (0, PACK_W - CONV_SHARD))))
    return jnp.concatenate(parts, axis=1)


def _unpack_grads(red):
    out = {}
    r = 0
    for n, nrows, shp in PACK_SPLIT:
        out[n] = red[r:r + nrows].reshape(shp)
        r += nrows
    out["conv_w"] = red[r, :CONV_SHARD].reshape(3, 128)
    return out


SMALL_SIZES = (("norm_g", D), ("b_gate", 3 * D), ("conv_b", CW), ("q_a_norm_g", QL), ("kv_a_norm_g", KVL),
               ("mla_q_norm_g", QK), ("mla_k_norm_g", QK), ("dil_q_norm_g", NG * HD), ("dil_k_norm_g", NG * HD))
SMALL_ROWS = 88


def _pack_small(per_name):
    flat = jnp.concatenate([per_name[n].reshape(-1).astype(F32) for n, _ in SMALL_SIZES])
    return jnp.pad(flat, (0, SMALL_ROWS * LANE - flat.shape[0])).reshape(SMALL_ROWS, LANE)


def _unpack_small(packed, like):
    out = {}
    flat = packed.reshape(-1)
    r = 0
    for n, sz in SMALL_SIZES:
        out[n] = flat[r:r + NL * sz].reshape(like[n].shape)
        r += NL * sz
    return out


def _adamw_math(w, g, m, v):
    m = ADAM_B1 * m + (1.0 - ADAM_B1) * g
    v = ADAM_B2 * v + (1.0 - ADAM_B2) * jnp.square(g)
    m_hat = m / (1.0 - ADAM_B1 ** ADAM_STEP)
    v_hat = v / (1.0 - ADAM_B2 ** ADAM_STEP)
    delta = -ADAM_LR * (m_hat / (jnp.sqrt(v_hat) + ADAM_EPS) + ADAM_WD * w)
    return delta, m, v


def _adamw(name, w, g, m, v, br, bc=None):
    L, R, C = w.shape
    bc = C if bc is None else bc
    blk = lambda l, i, j: (l, i, j)
    return _pcall(name, _adamw_math, (L, R // br, C // bc), [(t, (None, br, bc), blk) for t in (w, g, m, v)],
                  [((L, R, C), F32, (None, br, bc), blk)] * 3)


ADAM_ROWS = {"w_uq": 256, "w_ukv": 128, "w_out_a": 512, "w_out_b": 512, "w_out_c": 512, "w_o": 256,
             "conv_w": 3}


def kernel(x, norm_g, w_in, b_gate, conv_w, conv_b, q_a_norm_g, w_uq, kv_a_norm_g, w_ukv, mla_q_norm_g, mla_k_norm_g, dil_q_norm_g, dil_k_norm_g, w_out_a, w_out_b, w_out_c, w_o, loss_target, m_norm_g, m_w_in, m_b_gate, m_conv_w, m_conv_b, m_q_a_norm_g, m_w_uq, m_kv_a_norm_g, m_w_ukv, m_mla_q_norm_g, m_mla_k_norm_g, m_dil_q_norm_g, m_dil_k_norm_g, m_w_out_a, m_w_out_b, m_w_out_c, m_w_o, v_norm_g, v_w_in, v_b_gate, v_conv_w, v_conv_b, v_q_a_norm_g, v_w_uq, v_kv_a_norm_g, v_w_ukv, v_mla_q_norm_g, v_mla_k_norm_g, v_dil_q_norm_g, v_dil_k_norm_g, v_w_out_a, v_w_out_b, v_w_out_c, v_w_o):
    W = dict(norm_g=norm_g, w_in=w_in, b_gate=b_gate, conv_w=conv_w, conv_b=conv_b, q_a_norm_g=q_a_norm_g, w_uq=w_uq,
             kv_a_norm_g=kv_a_norm_g, w_ukv=w_ukv, mla_q_norm_g=mla_q_norm_g, mla_k_norm_g=mla_k_norm_g,
             dil_q_norm_g=dil_q_norm_g, dil_k_norm_g=dil_k_norm_g, w_out_a=w_out_a, w_out_b=w_out_b, w_out_c=w_out_c,
             w_o=w_o)
    M = dict(norm_g=m_norm_g, w_in=m_w_in, b_gate=m_b_gate, conv_w=m_conv_w, conv_b=m_conv_b, q_a_norm_g=m_q_a_norm_g,
             w_uq=m_w_uq, kv_a_norm_g=m_kv_a_norm_g, w_ukv=m_w_ukv, mla_q_norm_g=m_mla_q_norm_g,
             mla_k_norm_g=m_mla_k_norm_g, dil_q_norm_g=m_dil_q_norm_g, dil_k_norm_g=m_dil_k_norm_g, w_out_a=m_w_out_a,
             w_out_b=m_w_out_b, w_out_c=m_w_out_c, w_o=m_w_o)
    V = dict(norm_g=v_norm_g, w_in=v_w_in, b_gate=v_b_gate, conv_w=v_conv_w, conv_b=v_conv_b, q_a_norm_g=v_q_a_norm_g,
             w_uq=v_w_uq, kv_a_norm_g=v_kv_a_norm_g, w_ukv=v_w_ukv, mla_q_norm_g=v_mla_q_norm_g,
             mla_k_norm_g=v_mla_k_norm_g, dil_q_norm_g=v_dil_q_norm_g, dil_k_norm_g=v_dil_k_norm_g, w_out_a=v_w_out_a,
             w_out_b=v_w_out_b, w_out_c=v_w_out_c, w_o=v_w_o)
    batch = x.shape[0]
    T = batch * S

    def layer_weights(l, cont, gath):
        full = _unpack_weights(gath)
        pad_qk = lambda t: jnp.pad(t, (0, QKP - QK)).reshape(1, QKP)
        full.update(
            w_in_t=_unpack_w_in(cont),
            norm_g=norm_g[l].reshape(1, D), b_gate=b_gate[l].reshape(1, 3 * D), conv_b=conv_b[l].reshape(1, CW),
            q_a_norm_g=q_a_norm_g[l].reshape(1, QL), kv_a_norm_g=kv_a_norm_g[l].reshape(1, KVL),
            mla_q_norm_g=pad_qk(mla_q_norm_g[l]), mla_k_norm_g=pad_qk(mla_k_norm_g[l]),
            dil_q_norm_g=dil_q_norm_g[l].reshape(NG, 1, HD), dil_k_norm_g=dil_k_norm_g[l].reshape(NG, 1, HD))
        return full

    words = [_w_in_words(w_in[l]) for l in range(NL)]
    packs = [_pack_weights({n: W[n][l] for n in BIG[1:] + ("conv_w",)}) for l in range(NL)]
    tabs = _rope_tables() + (_dil_slopes(),)
    x2 = x.reshape(T, D)

    cont0, gath0 = _all_gather(words[0], packs[0])
    w0 = layer_weights(0, cont0, gath0)
    ag = _ag_ici_start(words[1], packs[1], gath0)
    w0["norm_g"] = w0["norm_g"] + ag[6][0:1, 0:1]
    y0, res0 = _layer_fwd(x2, w0, tabs, batch)
    lw, ls = _ag_ici_wait(ag[0], ag[1], ag[2], ag[3], ag[4], ag[5], y0)
    w1 = layer_weights(1, *_ag_finish(words[1], packs[1], lw, ls))
    y1, res1 = _layer_fwd(y0, w1, tabs, batch)

    row = lambda i: (i, 0)
    dy, loss = _pcall("loss", _loss_math, (T // BR,),
                      [(y1, (BR, D), row), (loss_target.reshape(T, D), (BR, D), row)],
                      [((T, D), F32, (BR, D), row), ((1, 1), F32, (1, 1), lambda i: (0, 0), True)])
    loss = lax.psum(loss[0, 0], ("x", "y", "c"))

    grads = [None] * NL
    dy, grads[1] = _layer_bwd(dy, w1, res1, tabs, batch)
    st = [None] * NL
    ex = [None] * NL
    st[1] = _rs_begin(grads[1]["w_in_t"], _pack_grads(grads[1]))
    ex[1] = _rs_exchange_start(st[1]["sw"], st[1]["ss"], "1")
    w0["w_o"] = w0["w_o"] + ex[1][6][0:1, 0:1].astype(BF16)

    def start_layer0(g):
        st[0] = _rs_begin(g["w_in_t"], _pack_grads(g))
        ex[0] = _rs_exchange_start(st[0]["sw"], st[0]["ss"], "0")
        return ex[0][6]

    dx, grads[0] = _layer_bwd(dy, w0, res0, tabs, batch, after_dw=start_layer0)
    grad_x = dx.reshape(batch, S, D)

    red = [None] * NL
    for l in (1, 0):
        r2w, r2s = _rs_exchange_wait(*ex[l][:6], dx, str(l))
        rw, rs = _rs_end(st[l], r2w, r2s)
        r = _unpack_grads(rs)
        r["w_in_t"] = rw
        red[l] = r
    G = {n: jnp.stack([red[l][n] for l in range(NL)]) for n in BIG[1:] + ("conv_w",)}
    g_in_t = jnp.stack([red[l]["w_in_t"] for l in range(NL)])
    G["w_in"] = jnp.swapaxes(g_in_t, 1, 2)
    small_g = {n: jnp.stack([grads[l][n].reshape(-1)[:sz] for l in range(NL)]) for n, sz in SMALL_SIZES}
    small_red = _all_reduce_small(_pack_small(small_g))
    G.update(_unpack_small(small_red, {n: W[n] for n in SMALL}))

    delta, new_m, new_v = {}, {}, {}
    for n in BIG[1:] + ("conv_w",):
        delta[n], new_m[n], new_v[n] = _adamw("adamw_" + n, W[n], G[n], M[n], V[n], ADAM_ROWS[n])
    tr = lambda t: jnp.swapaxes(t, 1, 2)
    delta["w_in"], new_m["w_in"], new_v["w_in"] = (
        tr(t) for t in _adamw("adamw_w_in", tr(w_in), g_in_t, tr(m_w_in), tr(v_w_in), WSH, LANE))
    sw, sm, sv = (_pack_small({n: t[n] for n in SMALL})[None] for t in (W, M, V))
    sd, snm, snv = _adamw("adamw_small", sw, small_red[None], sm, sv, SMALL_ROWS)
    like = {n: W[n] for n in SMALL}
    delta.update(_unpack_small(sd[0], like))
    new_m.update(_unpack_small(snm[0], like))
    new_v.update(_unpack_small(snv[0], like))

    return (loss, grad_x, *[G[n] for n in WEIGHTS], *[delta[n] for n in WEIGHTS],
            *[new_m[n] for n in WEIGHTS], *[new_v[n] for n in WEIGHTS])
```

```python
import functools

import numpy as np
import jax
import jax.numpy as jnp
from jax import lax
from jax.experimental import pallas as pl
from jax.experimental.pallas import tpu as pltpu

F32 = jnp.float32
BF16 = jnp.bfloat16

D = 1024
S = 2048
NL = 2
CW = 512
NH = 8
QL = 256
KVL = 128
NOPE = 64
ROPE = 32
VD = 64
QK = NOPE + ROPE
QKP = 128
ROPE_THETA = 10000.0
DIL = ((128, 1), (512, 4), (2048, 16))
NG = 3
DH = 8
HD = 64
DWID = DH * HD
QB = 128
EPS = 1e-6
NIN = 11168
NINP = 11264
O_A, O_CQ, O_CKV, O_KPE, O_BZ, O_DQ, O_DK, O_DV, O_CZ, O_G = 0, 2048, 2304, 2432, 2560, 3072, 4608, 6144, 7680, 8192
KPE_END = 2464
NEG = -1e30
MLA_SCALE = QK ** -0.5
DIL_SCALE = HD ** -0.5
LANE = 128
PACK_W = 512
VMEM_LIMIT = 48 * 1024 * 1024

ADAM_LR = 0.001
ADAM_B1 = 0.9
ADAM_B2 = 0.999
ADAM_EPS = 1e-08
ADAM_WD = 0.01
ADAM_STEP = 10

MESH = pl.DeviceIdType.MESH
BIG = ("w_in", "w_uq", "w_ukv", "w_out_a", "w_out_b", "w_out_c", "w_o")
SMALL = ("norm_g", "b_gate", "conv_b", "q_a_norm_g", "kv_a_norm_g", "mla_q_norm_g", "mla_k_norm_g",
         "dil_q_norm_g", "dil_k_norm_g")
WEIGHTS = ("norm_g", "w_in", "b_gate", "conv_w", "conv_b", "q_a_norm_g", "w_uq", "kv_a_norm_g", "w_ukv",
           "mla_q_norm_g", "mla_k_norm_g", "dil_q_norm_g", "dil_k_norm_g", "w_out_a", "w_out_b", "w_out_c", "w_o")


def _dot(a, b):
    return jnp.dot(a, b, preferred_element_type=F32)


def _dot_nt(a, b):
    return lax.dot_general(a, b, (((1,), (1,)), ((), ())), preferred_element_type=F32)


def _dot_tn(a, b):
    return lax.dot_general(a, b, (((0,), (0,)), ((), ())), preferred_element_type=F32)


def _grid_step(grid):
    step = pl.program_id(0)
    for a in range(1, len(grid)):
        step = step * grid[a] + pl.program_id(a)
    n = 1
    for g in grid:
        n *= g
    return step, n


def _write_windows(buf_ref, stages, sems, step, nsteps, puts):
    slot = step % 2
    for t, (v, dst) in enumerate(puts):
        cp = pltpu.make_async_copy(stages[t].at[slot], dst, sems.at[t, slot])

        @pl.when(step >= 2)
        def _():
            cp.wait()

        stages[t][slot] = v.astype(stages[t].dtype).reshape(stages[t].shape[1:])
        cp.start()

    @pl.when(step == nsteps - 1)
    def _():
        for t, (v, dst) in enumerate(puts):
            pltpu.make_async_copy(stages[t].at[slot], dst, sems.at[t, slot]).wait()
            if nsteps > 1:
                pltpu.make_async_copy(stages[t].at[1 - slot], dst, sems.at[t, 1 - slot]).wait()


def _pcall(name, fn, grid, ins, outs, into=None):
    n_in = len(ins)
    n_out = len(outs)
    acc_axis = len(grid) - 1
    is_acc = [len(o) > 4 and o[4] for o in outs]
    outs = [o[:4] for o in outs]
    targets = into[1] if into is not None else []
    n_t = len(targets)

    def body(*refs):
        vals = fn(*[r[...].astype(F32) for r in refs[:n_in]])
        if not isinstance(vals, (tuple, list)):
            vals = (vals,)
        o0 = n_in + (1 if n_t else 0)
        for k in range(n_out):
            r = refs[o0 + k]
            v = vals[k].astype(r.dtype).reshape(r.shape)
            if is_acc[k]:
                first = pl.program_id(acc_axis) == 0

                @pl.when(first)
                def _():
                    r[...] = v

                @pl.when(jnp.logical_not(first))
                def _():
                    r[...] += v
            else:
                r[...] = v
        if n_t:
            buf_ref = refs[o0 + n_out]
            stages = refs[o0 + n_out + 1:o0 + n_out + 1 + n_t]
            ids = [pl.program_id(a) for a in range(len(grid))]
            step, nsteps = _grid_step(grid)
            _write_windows(buf_ref, stages, refs[-1], step, nsteps,
                           [(vals[n_out + t], targets[t][1](buf_ref, *ids)) for t in range(n_t)])

    in_specs = [pl.BlockSpec(bs, im) for _, bs, im in ins]
    out_specs = [pl.BlockSpec(bs, im) for _, _, bs, im in outs]
    out_shape = [jax.ShapeDtypeStruct(sh, dt) for sh, dt, _, _ in outs]
    args = [a for a, _, _ in ins]
    extra = {}
    if n_t:
        buf = into[0]
        in_specs.append(pl.BlockSpec(memory_space=pl.ANY))
        out_specs.append(pl.BlockSpec(memory_space=pl.ANY))
        out_shape.append(jax.ShapeDtypeStruct(buf.shape, buf.dtype))
        args.append(buf)
        extra = dict(input_output_aliases={n_in: n_out},
                     scratch_shapes=[pltpu.VMEM((2,) + tuple(bs), buf.dtype) for bs, _ in targets]
                     + [pltpu.SemaphoreType.DMA((n_t, 2))])
    return pl.pallas_call(
        body,
        grid=grid,
        in_specs=in_specs,
        out_specs=out_specs,
        out_shape=out_shape,
        name=name,
        compiler_params=pltpu.CompilerParams(
            dimension_semantics=("arbitrary",) * len(grid), vmem_limit_bytes=VMEM_LIMIT),
        **extra,
    )(*args)


def _mm(name, a, b, *, ta=False, tb=False, out_dtype=F32, add=None, dep=None, tm=2048, tn=1024, tk=1024):
    if ta:
        K, M = a.shape
    else:
        M, K = a.shape
    if tb:
        N, K2 = b.shape
    else:
        K2, N = b.shape
    assert K == K2, (name, a.shape, b.shape)
    tm, tn, tk = min(tm, M), min(tn, N), min(tk, K)
    assert M % tm == 0 and N % tn == 0 and K % tk == 0, (name, M, N, K)
    nk = K // tk
    dims = (((0 if ta else 1,), (1 if tb else 0,)), ((), ()))
    a_spec = pl.BlockSpec((tk, tm), lambda j, i, k: (k, i)) if ta else pl.BlockSpec((tm, tk), lambda j, i, k: (i, k))
    b_spec = pl.BlockSpec((tn, tk), lambda j, i, k: (j, k)) if tb else pl.BlockSpec((tk, tn), lambda j, i, k: (k, j))
    o_spec = pl.BlockSpec((tm, tn), lambda j, i, k: (i, j))
    has_add = add is not None
    n_in = 2 + has_add + (dep is not None)

    def body(*refs):
        a_ref, b_ref = refs[0], refs[1]
        add_ref = refs[2] if has_add else None
        o_ref = refs[n_in]
        p = lax.dot_general(a_ref[...].astype(BF16), b_ref[...].astype(BF16), dims, preferred_element_type=F32)
        if nk == 1:
            if has_add:
                p = p + add_ref[...]
            o_ref[...] = p.astype(out_dtype)
        else:
            acc = refs[-1]
            k = pl.program_id(2)

            @pl.when(k == 0)
            def _():
                acc[...] = p

            @pl.when(k > 0)
            def _():
                acc[...] += p

            @pl.when(k == nk - 1)
            def _():
                r = acc[...]
                if has_add:
                    r = r + add_ref[...]
                o_ref[...] = r.astype(out_dtype)

    in_specs = [a_spec, b_spec] + ([o_spec] if has_add else []) + ([pl.BlockSpec(memory_space=pl.ANY)] if dep is not None else [])
    args = [a, b] + ([add] if has_add else []) + ([dep] if dep is not None else [])
    return pl.pallas_call(
        body,
        grid=(N // tn, M // tm, nk),
        in_specs=in_specs,
        out_specs=o_spec,
        out_shape=jax.ShapeDtypeStruct((M, N), out_dtype),
        scratch_shapes=[pltpu.VMEM((tm, tn), F32)] if nk > 1 else [],
        name=name,
        compiler_params=pltpu.CompilerParams(
            dimension_semantics=("arbitrary", "arbitrary", "arbitrary"), vmem_limit_bytes=VMEM_LIMIT),
    )(*args)


def _vjp_of(f, n_diff):
    def g(*args, n_prim):
        prim = args[:n_diff]
        consts = args[n_diff:n_prim]
        cts = args[n_prim:]
        _, pull = jax.vjp(lambda *p: f(*p, *consts), *prim)
        out = jax.eval_shape(lambda *p: f(*p, *consts), *prim)
        if isinstance(out, (tuple, list)):
            cts = tuple(c.astype(o.dtype) for c, o in zip(cts, out))
        else:
            cts = cts[0].astype(out.dtype)
        return pull(cts)
    return g


def _rms(x, g, n=None):
    n = x.shape[-1] if n is None else n
    ms = jnp.sum(x * x, axis=-1, keepdims=True) / n
    return x * lax.rsqrt(ms + EPS) * g


def _silu(z):
    return z * jax.nn.sigmoid(z)


def _roll_rows(u, k):
    n = u.shape[0]
    r = pltpu.roll(u, k % n, 0)
    t = lax.broadcasted_iota(jnp.int32, u.shape, 0)
    if k > 0:
        return jnp.where(t >= k, r, 0.0)
    return jnp.where(t < n + k, r, 0.0)


@functools.partial(jax.custom_vjp, nondiff_argnums=(1,))
def _shift(u, k):
    return _roll_rows(u, k)


def _shift_fwd(u, k):
    return _roll_rows(u, k), None


def _shift_bwd(k, _, g):
    return (_roll_rows(g, -k),)


_shift.defvjp(_shift_fwd, _shift_bwd)


@functools.partial(jax.custom_vjp, nondiff_argnums=(1,))
def _lane_roll(u, k):
    return pltpu.roll(u, k % LANE, 1)


def _lane_roll_fwd(u, k):
    return pltpu.roll(u, k % LANE, 1), None


def _lane_roll_bwd(k, _, g):
    return (pltpu.roll(g, (-k) % LANE, 1),)


_lane_roll.defvjp(_lane_roll_fwd, _lane_roll_bwd)


def _conv_math(ab, ac, ax, az, cw, cb):
    u = ac * ax
    conv = cb + _shift(u, 2) * cw[0:1] + _shift(u, 1) * cw[1:2] + u * cw[2:3]
    return ab * conv * _silu(az)


def _mla_pre_math(cq, ckv, gq, gkv):
    return _rms(cq, gq), _rms(ckv, gkv)


def _rope_math(q, kn, kpe, gq, gk, c, s1, s2):
    lane = lax.broadcasted_iota(jnp.int32, kpe.shape, 1)
    pe = _lane_roll(jnp.where(lane < ROPE, kpe, 0.0), NOPE)

    def one(t, g):
        tn = _rms(t, g, QK)
        return tn * c + _lane_roll(tn, -16) * s1 + _lane_roll(tn, 16) * s2

    qs, ks = [], []
    for h in range(NH):
        sl = slice(h * QKP, (h + 1) * QKP)
        qs.append(one(q[:, sl], gq))
        ks.append(one(kn[:, sl] + pe, gk))
    return jnp.concatenate(qs, axis=1), jnp.concatenate(ks, axis=1)


def _gate_math(o, z):
    return o * _silu(z)


def _mergec_math(o0, o1, o2, l0, l1, l2, cz):
    m = lax.stop_gradient(jnp.maximum(jnp.maximum(l0, l1), l2))
    e0, e1, e2 = jnp.exp(l0 - m), jnp.exp(l1 - m), jnp.exp(l2 - m)
    den = e0 + e1 + e2
    oc = (e0 / den) * o0 + (e1 / den) * o1 + (e2 / den) * o2
    return oc * _silu(cz)


def _merge_math(g0, g1, g2, b0, b1, b2, pa, pb, pc):
    return (jax.nn.sigmoid(g0 + b0) * pa + jax.nn.sigmoid(g1 + b1) * pb) + jax.nn.sigmoid(g2 + b2) * pc


MLA_T = 256
MLA_UNROLL = True


def _mla_fwd(q, k, v):
    B = q.shape[0]
    T = MLA_T
    NB = S // T

    def body(q_ref, k_ref, v_ref, o_ref, l_ref):
        row = lax.broadcasted_iota(jnp.int32, (T, T), 0)
        col = lax.broadcasted_iota(jnp.int32, (T, T), 1)
        lo = _lo_mask((T, LANE))

        for qi in range(NB):
            qb = q_ref[qi * T:(qi + 1) * T, :]

            def step(j, carry, diagonal):
                m, l, acc = carry
                off = pl.multiple_of(j * T, T)
                kb = k_ref[pl.ds(off, T), :]
                vb = v_ref[pl.ds(off, T), :]
                ss = []
                for e in (0, 1):
                    se = _dot_nt(qb[:, e * QKP:(e + 1) * QKP], kb[:, e * QKP:(e + 1) * QKP]) * MLA_SCALE
                    ss.append(jnp.where(col <= row, se, NEG) if diagonal else se)
                s = jnp.concatenate(ss, axis=0)
                m_new = jnp.maximum(m, jnp.max(s, axis=-1, keepdims=True))
                a = jnp.exp(m - m_new)
                p = jnp.exp(s - m_new)
                l = a * l + jnp.sum(p, axis=-1, keepdims=True)
                acc = a * acc + _dot(p.astype(BF16), vb)
                return m_new, l, acc

            init = (jnp.full((2 * T, 1), NEG, F32), jnp.zeros((2 * T, 1), F32), jnp.zeros((2 * T, LANE), F32))
            carry = lax.fori_loop(0, qi, functools.partial(step, diagonal=False), init, unroll=MLA_UNROLL)
            m, l, acc = step(qi, carry, True)
            o = acc / l
            lse = m + jnp.log(l)
            o_ref[qi * T:(qi + 1) * T, :] = jnp.where(lo, o[:T], o[T:])
            l_ref[qi * T:(qi + 1) * T, :] = jnp.where(lo, lse[:T], lse[T:])

    def spec(w):
        return pl.BlockSpec((None, S, w), lambda b, hp: (b, 0, hp))

    return pl.pallas_call(
        body,
        grid=(B, NH // 2),
        in_specs=[spec(2 * QKP), spec(2 * QKP), spec(LANE)],
        out_specs=[spec(LANE), spec(LANE)],
        out_shape=[jax.ShapeDtypeStruct((B, S, NH * VD), F32)] * 2,
        name="mla_attn_fwd",
        compiler_params=pltpu.CompilerParams(dimension_semantics=("arbitrary",) * 2, vmem_limit_bytes=VMEM_LIMIT),
    )(q, k, v)


def _mla_bwd(q, k, v, do, o, lse):
    B = q.shape[0]
    T = MLA_T
    NB = S // T

    def body(q_ref, k_ref, v_ref, do_ref, o_ref, l_ref, dq_ref, dk_ref, dv_ref, delta_ref, dqt_ref):
        delta_ref[...] = _head_sum(do_ref[...] * o_ref[...])
        row = lax.broadcasted_iota(jnp.int32, (T, T), 0)
        col = lax.broadcasted_iota(jnp.int32, (T, T), 1)
        lo = _lo_mask((T, LANE))
        tn_t = (((0,), (1,)), ((), ()))

        for j in range(NB):
            krows = slice(j * T, (j + 1) * T)
            kb = k_ref[krows, :]
            vb = v_ref[krows, :]
            dkt = [jnp.zeros((QKP, T), F32), jnp.zeros((QKP, T), F32)]
            dvt = jnp.zeros((LANE, T), F32)
            for i in range(j, NB):
                qrows = slice(i * T, (i + 1) * T)
                qb = q_ref[qrows, :]
                do2 = _stack_heads(do_ref[qrows, :], lo).astype(BF16)
                lb = l_ref[qrows, :]
                db = delta_ref[qrows, :]
                dp2 = _dot_nt(do2, vb)
                ps = []
                for e in (0, 1):
                    cols = slice(e * QKP, (e + 1) * QKP)
                    qe, ke = qb[:, cols], kb[:, cols]
                    s = _dot_nt(qe, ke) * MLA_SCALE
                    if i == j:
                        s = jnp.where(col <= row, s, NEG)
                    p = jnp.exp(s - lb[:, e * HD:e * HD + 1])
                    ps.append(p.astype(BF16))
                    ds = (p * (dp2[e * T:(e + 1) * T] - db[:, e * HD:e * HD + 1]) * MLA_SCALE).astype(BF16)
                    dkt[e] = dkt[e] + _dot_tn(qe, ds)
                    dq_t = lax.dot_general(ke, ds, tn_t, preferred_element_type=F32)
                    if j == 0:
                        dqt_ref[e, :, qrows] = dq_t
                    else:
                        dqt_ref[e, :, qrows] += dq_t
                dvt = dvt + _dot_tn(do2, jnp.concatenate(ps, axis=0))
            dk_ref[krows, 0:QKP] = dkt[0].T
            dk_ref[krows, QKP:2 * QKP] = dkt[1].T
            dv_ref[krows, :] = dvt.T
        dq_ref[:, 0:QKP] = dqt_ref[0].T
        dq_ref[:, QKP:2 * QKP] = dqt_ref[1].T

    def spec(w):
        return pl.BlockSpec((None, S, w), lambda b, hp: (b, 0, hp))

    return pl.pallas_call(
        body,
        grid=(B, NH // 2),
        in_specs=[spec(2 * QKP), spec(2 * QKP), spec(LANE), spec(LANE), spec(LANE), spec(LANE)],
        out_specs=[spec(2 * QKP), spec(2 * QKP), spec(LANE)],
        out_shape=[jax.ShapeDtypeStruct((B, S, NH * QKP), F32), jax.ShapeDtypeStruct((B, S, NH * QKP), F32),
                   jax.ShapeDtypeStruct((B, S, NH * VD), F32)],
        scratch_shapes=[pltpu.VMEM((S, LANE), F32), pltpu.VMEM((2, QKP, S), F32)],
        name="mla_attn_bwd",
        compiler_params=pltpu.CompilerParams(dimension_semantics=("arbitrary",) * 2, vmem_limit_bytes=VMEM_LIMIT),
    )(q, k, v, do, o, lse)


def _lo_mask(shape):
    return lax.broadcasted_iota(jnp.int32, shape, len(shape) - 1) < HD


def _head_sum(u):
    r = lax.broadcasted_iota(jnp.int32, (LANE, LANE), 0) < HD
    c = lax.broadcasted_iota(jnp.int32, (LANE, LANE), 1) < HD
    ones = jnp.where(r == c, 1.0, 0.0).astype(BF16)
    hi = u.astype(BF16)
    lo = (u - hi.astype(F32)).astype(BF16)
    return _dot(hi, ones) + _dot(lo, ones)


def _head_sum_1(u):
    r = lax.broadcasted_iota(jnp.int32, (LANE, LANE), 0) < HD
    c = lax.broadcasted_iota(jnp.int32, (LANE, LANE), 1) < HD
    return _dot(u.astype(BF16), jnp.where(r == c, 1.0, 0.0).astype(BF16))


def _rms2_scale(x):
    return lax.rsqrt(_head_sum(x * x) / HD + EPS)


def _rms2(x, g):
    return x * _rms2_scale(x) * g


def _rms2_bwd(x, r, g, dy):
    xn = x * r
    t = dy * g
    dx = r * (t - xn * (_head_sum_1(xn * t) * (1.0 / HD)))
    return dx, jnp.sum(dy * xn, axis=0, keepdims=True)


def _dil_bias(t_ref, gi, d):
    qq = lax.broadcasted_iota(jnp.int32, (QB, QB), 0)
    kk = lax.broadcasted_iota(jnp.int32, (QB, QB), 1)
    jc = (qq - kk).astype(F32)
    rows = []
    for e in (0, 1):
        sl = t_ref[2 * gi + e:2 * gi + e + 1, :] * float(d)
        bp = jnp.where(kk >= qq, -sl * (jc + float(QB)), NEG)
        bc = jnp.where(kk <= qq, -sl * jc, NEG)
        rows.append(jnp.concatenate([bp, bc], axis=1))
    return jnp.concatenate(rows, axis=0)


def _dil_rows(cur, d):
    return pl.ds(cur, QB, stride=d) if d > 1 else pl.ds(pl.multiple_of(cur, QB), QB)


def _dil_walk(d, block, full):
    if d == 1:
        block(0, None)

        def body(i, c):
            block(i * QB, (i - 1) * QB)
            return c
        lax.fori_loop(1, S // QB, body, 0, unroll=True if full else 5)
    elif d == 16:
        def body(r, c):
            block(r, None)
            return c
        lax.fori_loop(0, d, body, 0, unroll=True if full else 4)
    else:
        nb = S // d // QB

        def cls(r, c):
            block(r, None)

            def body(i, c2):
                block(r + i * QB * d, r + (i - 1) * QB * d)
                return c2
            lax.fori_loop(1, nb, body, 0, unroll=True)
            return c
        lax.fori_loop(0, d, cls, 0, unroll=full)


def _stack_heads(x, lo):
    return jnp.concatenate([jnp.where(lo, x, 0.0), jnp.where(lo, 0.0, x)], axis=0)


def _dilc_fwd(proj3, gq, gk, tab):
    B = proj3.shape[0]

    def body(q_ref, k_ref, v_ref, cz_ref, gq_ref, gk_ref, t_ref, y_ref, o_ref, l_ref, qs, ks, vs):
        g = pl.program_id(2)
        lo = _lo_mask((QB, LANE))

        def group(gi):
            d = DIL[gi][1]
            qs[...] = _rms2(q_ref[...].astype(F32), gq_ref[gi:gi + 1, :])
            ks[...] = _rms2(k_ref[...].astype(F32), gk_ref[gi:gi + 1, :])
            vs[...] = v_ref[...].astype(F32)
            bias = _dil_bias(t_ref, gi, d)

            def block(cur, prev):
                rows = _dil_rows(cur, d)
                q2 = _stack_heads(qs[rows, :], lo).astype(BF16)
                kc, vc = ks[rows, :], vs[rows, :]
                if prev is None:
                    kcat, vcat, b = kc, vc, bias[:, QB:]
                else:
                    prow = _dil_rows(prev, d)
                    kcat = jnp.concatenate([ks[prow, :], kc], axis=0)
                    vcat = jnp.concatenate([vs[prow, :], vc], axis=0)
                    b = bias
                s = _dot_nt(q2, kcat.astype(BF16)) * DIL_SCALE + b
                m = jnp.max(s, axis=-1, keepdims=True)
                p = jnp.exp(s - m)
                l = jnp.sum(p, axis=-1, keepdims=True)
                o = _dot(p.astype(BF16), vcat.astype(BF16)) / l
                lse = m + jnp.log(l)
                o_ref[gi, rows, :] = jnp.where(lo, o[:QB], o[QB:])
                l_ref[gi, rows, :] = jnp.where(lo, lse[:QB], lse[QB:])

            _dil_walk(d, block, True)

        for gi in range(NG):
            pl.when(g == gi)(functools.partial(group, gi))

        @pl.when(g == NG - 1)
        def _():
            y_ref[...] = _mergec_math(o_ref[0], o_ref[1], o_ref[2], l_ref[0], l_ref[1], l_ref[2],
                                      cz_ref[...].astype(F32)).astype(BF16)

    def col(base):
        return pl.BlockSpec((None, S, LANE), lambda b, hp, g: (b, 0, base // LANE + 4 * g + hp))

    gspec = pl.BlockSpec((NG, LANE), lambda b, hp, g: (0, 0))
    saved = pl.BlockSpec((NG, None, S, LANE), lambda b, hp, g: (0, b, 0, hp))
    return pl.pallas_call(
        body,
        grid=(B, 4, NG),
        in_specs=[col(O_DQ), col(O_DK), col(O_DV),
                  pl.BlockSpec((None, S, LANE), lambda b, hp, g: (b, 0, O_CZ // LANE + hp)),
                  gspec, gspec, pl.BlockSpec((None, 8, LANE), lambda b, hp, g: (hp, 0, 0))],
        out_specs=[pl.BlockSpec((None, S, LANE), lambda b, hp, g: (b, 0, hp)), saved, saved],
        out_shape=[jax.ShapeDtypeStruct((B, S, DWID), BF16), jax.ShapeDtypeStruct((NG, B, S, DWID), F32),
                   jax.ShapeDtypeStruct((NG, B, S, DWID), F32)],
        scratch_shapes=[pltpu.VMEM((S, LANE), F32)] * 3,
        name="dil_mixer_fwd",
        compiler_params=pltpu.CompilerParams(dimension_semantics=("arbitrary",) * 3, vmem_limit_bytes=VMEM_LIMIT),
    )(proj3, proj3, proj3, proj3, gq, gk, tab)


MERGE_ROWS = 256


def _dilc_bwd(proj3, gq, gk, tab, o_all, l_all, d_yc, dproj3):
    B = proj3.shape[0]

    def body(q_ref, k_ref, v_ref, cz_ref, gq_ref, gk_ref, t_ref, o_ref, l_ref, dy_ref, dp_in,
             dp_out, dgq_out, dgk_out, qs, ks, vs, dos, dls, dqs, dks, dvs, rqs, rks, dczs,
             st_q, st_k, st_v, st_z, sems, sem_z):
        b_, hp, g = pl.program_id(0), pl.program_id(1), pl.program_id(2)
        col = lambda base: pl.ds(pl.multiple_of(base + hp * LANE, LANE), LANE)
        lo = _lo_mask((QB, LANE))

        @pl.when(jnp.logical_and(jnp.logical_and(pl.program_id(0) == 0, pl.program_id(1) == 0), g == 0))
        def _():
            dgq_out[...] = jnp.zeros((NG, LANE), F32)
            dgk_out[...] = jnp.zeros((NG, LANE), F32)

        @pl.when(g == 0)
        def _():
            def chunk(i, carry):
                rows = pl.ds(pl.multiple_of(i * MERGE_ROWS, MERGE_ROWS), MERGE_ROWS)
                ls = [l_ref[j, rows, :] for j in range(NG)]
                m = jnp.maximum(jnp.maximum(ls[0], ls[1]), ls[2])
                es = [jnp.exp(t - m) for t in ls]
                den = (es[0] + es[1]) + es[2]
                al = [e / den for e in es]
                os_ = [o_ref[j, rows, :] for j in range(NG)]
                oc = (al[0] * os_[0] + al[1] * os_[1]) + al[2] * os_[2]
                cz = cz_ref[rows, :].astype(F32)
                sg = jax.nn.sigmoid(cz)
                dy = dy_ref[rows, :]
                d_oc = dy * (cz * sg)
                dczs[rows, :] = (dy * oc * (sg * (1.0 + cz * (1.0 - sg)))).astype(BF16)
                ts = [_head_sum_1(d_oc * os_[j]) for j in range(NG)]
                tbar = (al[0] * ts[0] + al[1] * ts[1]) + al[2] * ts[2]
                for j in range(NG):
                    dos[j, rows, :] = al[j] * d_oc
                    dls[j, rows, :] = al[j] * (ts[j] - tbar)
                return carry
            lax.fori_loop(0, S // MERGE_ROWS, chunk, 0)
            _write_windows(dp_out, [st_z], sem_z, b_ * 4 + hp, B * 4, [(dczs[...], dp_out.at[b_, :, col(O_CZ)])])

        def group(gi):
            d = DIL[gi][1]
            xq, xk = q_ref[...].astype(F32), k_ref[...].astype(F32)
            rqs[...] = _rms2_scale(xq)
            rks[...] = _rms2_scale(xk)
            qs[...] = xq * rqs[...] * gq_ref[gi:gi + 1, :]
            ks[...] = xk * rks[...] * gk_ref[gi:gi + 1, :]
            vs[...] = v_ref[...].astype(F32)
            dks[...] = jnp.zeros((S, LANE), F32)
            dvs[...] = jnp.zeros((S, LANE), F32)
            bias = _dil_bias(t_ref, gi, d)

            def block(cur, prev):
                rows = _dil_rows(cur, d)
                q2 = _stack_heads(qs[rows, :], lo).astype(BF16)
                dob = dos[gi, rows, :]
                do2 = _stack_heads(dob, lo).astype(BF16)
                kc, vc = ks[rows, :], vs[rows, :]
                if prev is None:
                    kcat, vcat, b = kc, vc, bias[:, QB:]
                else:
                    prow = _dil_rows(prev, d)
                    kcat = jnp.concatenate([ks[prow, :], kc], axis=0)
                    vcat = jnp.concatenate([vs[prow, :], vc], axis=0)
                    b = bias
                kcat = kcat.astype(BF16)
                vcat = vcat.astype(BF16)
                lse_b = l_ref[gi, rows, :]
                corr_b = dls[gi, rows, :] - _head_sum_1(dob * o_ref[gi, rows, :])
                lse2 = jnp.concatenate([lse_b[:, 0:1], lse_b[:, HD:HD + 1]], axis=0)
                corr2 = jnp.concatenate([corr_b[:, 0:1], corr_b[:, HD:HD + 1]], axis=0)
                s = _dot_nt(q2, kcat) * DIL_SCALE + b
                p = jnp.exp(s - lse2)
                ds = (p * (_dot_nt(do2, vcat) + corr2) * DIL_SCALE).astype(BF16)
                dq2 = _dot(ds, kcat)
                dqs[rows, :] = jnp.where(lo, dq2[:QB], dq2[QB:])
                dk = _dot_tn(ds, q2)
                dv = _dot_tn(p.astype(BF16), do2)
                if prev is None:
                    dks[rows, :] += dk
                    dvs[rows, :] += dv
                else:
                    dks[prow, :] += dk[:QB]
                    dvs[prow, :] += dv[:QB]
                    dks[rows, :] += dk[QB:]
                    dvs[rows, :] += dv[QB:]

            _dil_walk(d, block, False)

            dxq, dgq = _rms2_bwd(q_ref[...].astype(F32), rqs[...], gq_ref[gi:gi + 1, :], dqs[...])
            dgq_out[gi:gi + 1, :] += dgq
            dxk, dgk = _rms2_bwd(k_ref[...].astype(F32), rks[...], gk_ref[gi:gi + 1, :], dks[...])
            dgk_out[gi:gi + 1, :] += dgk
            step, nsteps = _grid_step((B, 4, NG))
            _write_windows(dp_out, [st_q, st_k, st_v], sems, step, nsteps,
                           [(dxq, dp_out.at[b_, :, col(O_DQ + gi * DWID)]), (dxk, dp_out.at[b_, :, col(O_DK + gi * DWID)]),
                            (dvs[...], dp_out.at[b_, :, col(O_DV + gi * DWID)])])

        for gi in range(NG):
            pl.when(g == gi)(functools.partial(group, gi))

    def col(base):
        return pl.BlockSpec((None, S, LANE), lambda b, hp, g: (b, 0, base // LANE + 4 * g + hp))

    gspec = pl.BlockSpec((NG, LANE), lambda b, hp, g: (0, 0))
    saved = pl.BlockSpec((NG, None, S, LANE), lambda b, hp, g: (0, b, 0, hp))
    per_pair = pl.BlockSpec((None, S, LANE), lambda b, hp, g: (b, 0, hp))
    return pl.pallas_call(
        body,
        grid=(B, 4, NG),
        in_specs=[col(O_DQ), col(O_DK), col(O_DV),
                  pl.BlockSpec((None, S, LANE), lambda b, hp, g: (b, 0, O_CZ // LANE + hp)),
                  gspec, gspec, pl.BlockSpec((None, 8, LANE), lambda b, hp, g: (hp, 0, 0)),
                  saved, saved, per_pair, pl.BlockSpec(memory_space=pl.ANY)],
        out_specs=[pl.BlockSpec(memory_space=pl.ANY), gspec, gspec],
        out_shape=[jax.ShapeDtypeStruct(dproj3.shape, dproj3.dtype), jax.ShapeDtypeStruct((NG, LANE), F32),
                   jax.ShapeDtypeStruct((NG, LANE), F32)],
        input_output_aliases={10: 0},
        scratch_shapes=[pltpu.VMEM((S, LANE), F32)] * 3 + [pltpu.VMEM((NG, S, LANE), F32)] * 2
        + [pltpu.VMEM((S, LANE), F32)] * 5 + [pltpu.VMEM((S, LANE), BF16)] + [pltpu.VMEM((2, S, LANE), BF16)] * 4
        + [pltpu.SemaphoreType.DMA((3, 2)), pltpu.SemaphoreType.DMA((1, 2))],
        name="dil_mixer_bwd",
        compiler_params=pltpu.CompilerParams(dimension_semantics=("arbitrary",) * 3, vmem_limit_bytes=VMEM_LIMIT),
    )(proj3, proj3, proj3, proj3, gq, gk, tab, o_all, l_all, d_yc, dproj3)


def _dil_slopes():
    slopes = (2.0 ** (-8.0 * np.arange(1, NG * DH + 1, dtype=np.float32) / (NG * DH))).astype(np.float32).reshape(NG, DH)
    tab = np.zeros((4, 8, LANE), np.float32)
    for hp in range(4):
        for gi in range(NG):
            for e in (0, 1):
                tab[hp, 2 * gi + e, :] = slopes[gi, 2 * hp + e]
    return jnp.asarray(tab)


def _rope_tables():
    inv = ROPE_THETA ** (-jnp.arange(0, ROPE, 2, dtype=F32) / ROPE)
    ang = jnp.arange(S, dtype=F32)[:, None] * inv[None, :]
    cos, sin = jnp.cos(ang), jnp.sin(ang)
    z16 = jnp.zeros((S, 16), F32)
    c = jnp.concatenate([jnp.ones((S, NOPE), F32), cos, cos, jnp.zeros((S, 32), F32)], axis=1)
    s1 = jnp.concatenate([jnp.zeros((S, NOPE), F32), -sin, z16, jnp.zeros((S, 32), F32)], axis=1)
    s2 = jnp.concatenate([jnp.zeros((S, NOPE), F32), z16, sin, jnp.zeros((S, 32), F32)], axis=1)
    return c, s1, s2


def _pad_heads_uq(w):
    return jnp.pad(w.reshape(QL, NH, QK), ((0, 0), (0, 0), (0, QKP - QK))).reshape(QL, NH * QKP)


def _unpad_heads_uq(g):
    return g.reshape(QL, NH, QKP)[:, :, :QK].reshape(QL, NH * QK)


def _split_ukv(w):
    w3 = w.reshape(KVL, NH, NOPE + VD)
    uk = jnp.pad(w3[:, :, :NOPE], ((0, 0), (0, 0), (0, QKP - NOPE))).reshape(KVL, NH * QKP)
    return uk, w3[:, :, NOPE:].reshape(KVL, NH * VD)


def _join_ukv(guk, guv):
    return jnp.concatenate([guk.reshape(KVL, NH, QKP)[:, :, :NOPE], guv.reshape(KVL, NH, VD)],
                           axis=-1).reshape(KVL, NH * (NOPE + VD))


BR = 512
BRM = 256


def _layer_fwd(x, w, tabs, batch):
    T = batch * S
    rope_c, rope_s1, rope_s2, dil_tab = tabs
    res = {"x": x}
    row = lambda c: (lambda i: (i, c))
    fix = lambda i: (0, 0)

    h = _pcall("norm_fwd", _rms, (T // BR,),
               [(x, (BR, D), row(0)), (w["norm_g"], (1, D), fix)],
               [((T, D), BF16, (BR, D), row(0))])[0]
    proj = _mm("in_proj", h, w["w_in_t"], tb=True, out_dtype=BF16, tm=2048, tn=1024)
    res["h"], res["proj"] = h, proj
    proj3 = proj.reshape(batch, S, NINP)

    cblk = lambda s: (lambda j, b: (b, 0, 4 * s + j))
    y_a = _pcall("conv_fwd", _conv_math, (4, batch),
                 [(proj3, (None, S, LANE), cblk(0)), (proj3, (None, S, LANE), cblk(1)),
                  (proj3, (None, S, LANE), cblk(2)), (proj3, (None, S, LANE), cblk(3)),
                  (w["conv_w"], (3, LANE), lambda j, b: (0, j)), (w["conv_b"], (1, LANE), lambda j, b: (0, j))],
                 [((batch, S, CW), BF16, (None, S, LANE), lambda j, b: (b, 0, j))])[0].reshape(T, CW)
    res["y_a"] = y_a

    cqn, ckvn = _pcall("mla_pre_fwd", _mla_pre_math, (T // BR,),
                       [(proj, (BR, QL), row(O_CQ // QL)), (proj, (BR, KVL), row(O_CKV // KVL)),
                        (w["q_a_norm_g"], (1, QL), fix), (w["kv_a_norm_g"], (1, KVL), fix)],
                       [((T, QL), BF16, (BR, QL), row(0)), ((T, KVL), BF16, (BR, KVL), row(0))])
    w_uq_p = _pad_heads_uq(w["w_uq"])
    w_uk, w_uv = _split_ukv(w["w_ukv"])
    q = _mm("uq", cqn, w_uq_p, out_dtype=BF16)
    kn = _mm("uk", ckvn, w_uk, out_dtype=BF16)
    v = _mm("uv", ckvn, w_uv, out_dtype=BF16)
    nrr = S // BR
    tab_row = lambda i: (i % nrr, 0)
    qr, kr = _pcall("rope_fwd", _rope_math, (T // BR,),
                    [(q, (BR, NH * QKP), row(0)), (kn, (BR, NH * QKP), row(0)), (proj, (BR, LANE), row(O_KPE // LANE)),
                     (w["mla_q_norm_g"], (1, QKP), fix), (w["mla_k_norm_g"], (1, QKP), fix),
                     (rope_c, (BR, QKP), tab_row), (rope_s1, (BR, QKP), tab_row), (rope_s2, (BR, QKP), tab_row)],
                    [((T, NH * QKP), BF16, (BR, NH * QKP), row(0))] * 2)
    qr = qr.reshape(batch, S, NH * QKP)
    kr = kr.reshape(batch, S, NH * QKP)
    v = v.reshape(batch, S, NH * VD)
    o_b, l_b = _mla_fwd(qr, kr, v)
    ob2 = o_b.reshape(T, NH * VD)
    y_b = _pcall("gateb_fwd", _gate_math, (T // BR,),
                 [(ob2, (BR, 512), row(0)), (proj, (BR, 512), row(O_BZ // 512))],
                 [((T, 512), BF16, (BR, 512), row(0))])[0]
    res.update(cqn=cqn, ckvn=ckvn, q=q, kn=kn, qr=qr, kr=kr, v=v, o_b=o_b, l_b=l_b, ob2=ob2, y_b=y_b,
               w_uq_p=w_uq_p, w_uk=w_uk, w_uv=w_uv)

    gq2 = jnp.tile(w["dil_q_norm_g"].reshape(NG, HD), (1, 2))
    gk2 = jnp.tile(w["dil_k_norm_g"].reshape(NG, HD), (1, 2))
    y_c, o_all, l_all = _dilc_fwd(proj3, gq2, gk2, dil_tab)
    y_c = y_c.reshape(T, DWID)
    res.update(o_all=o_all, l_all=l_all, y_c=y_c)

    pa = _mm("out_a", y_a, w["w_out_a"], out_dtype=BF16)
    pb = _mm("out_b", y_b, w["w_out_b"], out_dtype=BF16)
    pc = _mm("out_c", y_c, w["w_out_c"], out_dtype=BF16)
    merged = _pcall("merge_fwd", _merge_math, (T // BRM,),
                    [(proj, (BRM, D), row(O_G // D + s)) for s in range(3)]
                    + [(w["b_gate"], (1, D), (lambda s: (lambda i: (0, s)))(s)) for s in range(3)]
                    + [(t, (BRM, D), row(0)) for t in (pa, pb, pc)],
                    [((T, D), BF16, (BRM, D), row(0))])[0]
    out = _mm("o_proj", merged, w["w_o"], add=x, tm=1024)
    res.update(pa=pa, pb=pb, pc=pc, merged=merged)
    return out, res


def _norm_bwd_math(x, g, dh, dy):
    _, pull = jax.vjp(_rms, x, g)
    dx, dg = pull(dh)
    return dx + dy, dg


def _layer_bwd(dy, w, res, tabs, batch, after_dw=None):
    T = batch * S
    rope_c, rope_s1, rope_s2, dil_tab = tabs
    row = lambda c: (lambda i: (i, c))
    fix = lambda i: (0, 0)
    x, proj, h = res["x"], res["proj"], res["h"]
    proj3 = proj.reshape(batch, S, NINP)
    g = {}

    d_merged = _mm("o_proj_dx", dy, w["w_o"], tb=True)
    g["w_o"] = _mm("o_proj_dw", res["merged"], dy, ta=True, tm=1024, tk=2048)

    dproj = lax.empty((T, NINP), BF16)
    rows_of = lambda br: (lambda ref, i: ref.at[pl.ds(pl.multiple_of(i * br, br), br)])

    def merge_bwd(*args):
        dg0, dg1, dg2, db0, db1, db2, dpa, dpb, dpc = _vjp_of(_merge_math, 9)(*args, n_prim=9)
        return db0, db1, db2, dpa, dpb, dpc, jnp.concatenate([dg0, dg1, dg2], axis=1)

    db0, db1, db2, dpa, dpb, dpc, dproj = _pcall(
        "merge_bwd", merge_bwd, (T // BRM,),
        [(proj, (BRM, D), row(O_G // D + s)) for s in range(3)]
        + [(w["b_gate"], (1, D), (lambda s: (lambda i: (0, s)))(s)) for s in range(3)]
        + [(t, (BRM, D), row(0)) for t in (res["pa"], res["pb"], res["pc"])]
        + [(d_merged, (BRM, D), row(0))],
        [((1, D), F32, (1, D), fix, True)] * 3 + [((T, D), BF16, (BRM, D), row(0))] * 3,
        into=(dproj, [((BRM, 3 * D), lambda ref, i: rows_of(BRM)(ref, i).at[:, O_G:O_G + 3 * D])]))
    g["b_gate"] = jnp.concatenate([db0, db1, db2], axis=1)

    d_ya = _mm("out_a_dx", dpa, w["w_out_a"], tb=True)
    d_yb = _mm("out_b_dx", dpb, w["w_out_b"], tb=True)
    d_yc = _mm("out_c_dx", dpc, w["w_out_c"], tb=True)
    g["w_out_a"] = _mm("out_a_dw", res["y_a"], dpa, ta=True, tk=T)
    g["w_out_b"] = _mm("out_b_dw", res["y_b"], dpb, ta=True, tk=T)
    g["w_out_c"] = _mm("out_c_dw", res["y_c"], dpc, ta=True, tk=T)

    cblk = lambda s: (lambda j, b: (b, 0, 4 * s + j))
    oblk = lambda j, b: (b, 0, j)
    def conv_bwd(*args):
        d_ab, d_ac, d_ax, d_az, dcw, dcb = _vjp_of(_conv_math, 6)(*args, n_prim=6)
        return dcw, dcb, d_ab, d_ac, d_ax, d_az

    a_col = lambda s_: (lambda ref, j, b: ref.at[b, :, pl.ds(pl.multiple_of(O_A + s_ * CW + j * LANE, LANE), LANE)])
    g["conv_w"], g["conv_b"], dproj3 = _pcall(
        "conv_bwd", conv_bwd, (4, batch),
        [(proj3, (None, S, LANE), cblk(s)) for s in range(4)]
        + [(w["conv_w"], (3, LANE), lambda j, b: (0, j)), (w["conv_b"], (1, LANE), lambda j, b: (0, j)),
           (d_ya.reshape(batch, S, CW), (None, S, LANE), oblk)],
        [((3, CW), F32, (3, LANE), lambda j, b: (0, j), True), ((1, CW), F32, (1, LANE), lambda j, b: (0, j), True)],
        into=(dproj.reshape(batch, S, NINP), [((S, LANE), a_col(s_)) for s_ in range(4)]))
    dproj = dproj3.reshape(T, NINP)

    gate_bwd = functools.partial(_vjp_of(_gate_math, 2), n_prim=2)
    d_ob, dproj = _pcall("gateb_bwd", gate_bwd, (T // BR,),
                         [(res["ob2"], (BR, 512), row(0)), (proj, (BR, 512), row(O_BZ // 512)), (d_yb, (BR, 512), row(0))],
                         [((T, 512), F32, (BR, 512), row(0))],
                         into=(dproj, [((BR, 512), lambda ref, i: rows_of(BR)(ref, i).at[:, O_BZ:O_BZ + 512])]))
    dqr, dkr, dv = _mla_bwd(res["qr"], res["kr"], res["v"], d_ob.reshape(batch, S, NH * VD), res["o_b"], res["l_b"])
    nrr = S // BR
    tab_row = lambda i: (i % nrr, 0)
    def rope_bwd(*args):
        d_q, d_kn, d_kpe, dgq, dgk = _vjp_of(_rope_math, 5)(*args, n_prim=8)
        return d_q, d_kn, dgq, dgk, d_kpe

    d_q, d_kn, g["mla_q_norm_g"], g["mla_k_norm_g"], dproj = _pcall(
        "rope_bwd", rope_bwd, (T // BR,),
        [(res["q"], (BR, NH * QKP), row(0)), (res["kn"], (BR, NH * QKP), row(0)), (proj, (BR, LANE), row(O_KPE // LANE)),
         (w["mla_q_norm_g"], (1, QKP), fix), (w["mla_k_norm_g"], (1, QKP), fix),
         (rope_c, (BR, QKP), tab_row), (rope_s1, (BR, QKP), tab_row), (rope_s2, (BR, QKP), tab_row),
         (dqr.reshape(T, NH * QKP), (BR, NH * QKP), row(0)), (dkr.reshape(T, NH * QKP), (BR, NH * QKP), row(0))],
        [((T, NH * QKP), BF16, (BR, NH * QKP), row(0))] * 2 + [((1, QKP), F32, (1, QKP), fix, True)] * 2,
        into=(dproj, [((BR, LANE), lambda ref, i: rows_of(BR)(ref, i).at[:, O_KPE:O_KPE + LANE])]))
    dv = dv.reshape(T, NH * VD)
    d_cqn = _mm("uq_dx", d_q, res["w_uq_p"], tb=True)
    d_ckvn = _mm("uk_dx", d_kn, res["w_uk"], tb=True)
    d_ckvn = _mm("uv_dx", dv, res["w_uv"], tb=True, add=d_ckvn)
    g["w_uq"] = _unpad_heads_uq(_mm("uq_dw", res["cqn"], d_q, ta=True, tk=T))
    g["w_ukv"] = _join_ukv(_mm("uk_dw", res["ckvn"], d_kn, ta=True, tk=T),
                           _mm("uv_dw", res["ckvn"], dv, ta=True, tk=T))
    def pre_bwd(*args):
        d_cq, d_ckv, dgq, dgkv = _vjp_of(_mla_pre_math, 4)(*args, n_prim=4)
        return dgq, dgkv, jnp.concatenate([d_cq, d_ckv], axis=1)

    g["q_a_norm_g"], g["kv_a_norm_g"], dproj = _pcall(
        "mla_pre_bwd", pre_bwd, (T // BR,),
        [(proj, (BR, QL), row(O_CQ // QL)), (proj, (BR, KVL), row(O_CKV // KVL)),
         (w["q_a_norm_g"], (1, QL), fix), (w["kv_a_norm_g"], (1, KVL), fix),
         (d_cqn, (BR, QL), row(0)), (d_ckvn, (BR, KVL), row(0))],
        [((1, QL), F32, (1, QL), fix, True), ((1, KVL), F32, (1, KVL), fix, True)],
        into=(dproj, [((BR, QL + KVL), lambda ref, i: rows_of(BR)(ref, i).at[:, O_CQ:O_CQ + QL + KVL])]))

    gq2 = jnp.tile(w["dil_q_norm_g"].reshape(NG, HD), (1, 2))
    gk2 = jnp.tile(w["dil_k_norm_g"].reshape(NG, HD), (1, 2))
    dproj3, dgq, dgk = _dilc_bwd(proj3, gq2, gk2, dil_tab, res["o_all"], res["l_all"],
                                 d_yc.reshape(batch, S, DWID), dproj.reshape(batch, S, NINP))
    dproj = dproj3.reshape(T, NINP)
    g["dil_q_norm_g"] = dgq[:, :HD] + dgq[:, HD:]
    g["dil_k_norm_g"] = dgk[:, :HD] + dgk[:, HD:]

    g["w_in_t"] = _mm("in_proj_dw", dproj, h, ta=True, tm=1024, tk=T)
    dep = after_dw(g) if after_dw is not None else None
    d_h = _mm("in_proj_dx", dproj, w["w_in_t"], dep=dep, tm=1024, tk=NINP // 4)
    dx, g["norm_g"] = _pcall("norm_bwd", _norm_bwd_math, (T // BR,),
                             [(x, (BR, D), row(0)), (w["norm_g"], (1, D), fix), (d_h, (BR, D), row(0)),
                              (dy, (BR, D), row(0))],
                             [((T, D), F32, (BR, D), row(0)), ((1, D), F32, (1, D), fix, True)])
    return dx, g


def _loss_math(y, t):
    e = y - t
    return e * (1.0 / D), 0.5 * jnp.sum(jnp.sum(e * e, axis=-1, keepdims=True) / D, axis=0, keepdims=True)


def _local_step(x, target, ws, batch):
    T = batch * S
    tabs = _rope_tables() + (_dil_slopes(),)
    saved = []
    y = x
    for l in range(NL):
        y, res = _layer_fwd(y, ws[l], tabs, batch)
        saved.append(res)
    row = lambda i: (i, 0)
    dy, loss = _pcall("loss", _loss_math, (T // BR,),
                      [(y, (BR, D), row), (target, (BR, D), row)],
                      [((T, D), F32, (BR, D), row), ((1, 1), F32, (1, 1), lambda i: (0, 0), True)])
    grads = [None] * NL
    for l in reversed(range(NL)):
        dy, grads[l] = _layer_bwd(dy, ws[l], saved[l], tabs, batch)
    return loss, dy, grads


ANY = pl.BlockSpec(memory_space=pl.ANY)
U32 = jnp.uint32
WSH = NIN // 4
WA = KPE_END
WB = WSH - WA
CWD = 512
PACK_ROWS = 1472
HW = PACK_W // 2


def _me():
    return lax.axis_index("x"), lax.axis_index("y"), lax.axis_index("c")


def _piece_rows(k):
    a = k * WSH + jnp.where(k > 0, NINP - NIN, 0)
    b = k * WSH + WA + (NINP - NIN)
    return ((0, pl.multiple_of(a, 8), WA), (WA, pl.multiple_of(b, 8), WB))


def _pack_words(lo, hi):
    ul = lax.bitcast_convert_type(lo.astype(BF16).astype(F32), U32)
    uh = lax.bitcast_convert_type(hi.astype(BF16).astype(F32), U32)
    w = jnp.bitwise_or(jnp.bitwise_and(uh, jnp.uint32(0xFFFF0000)), jnp.right_shift(ul, jnp.uint32(16)))
    return lax.bitcast_convert_type(w, F32)


def _unpack_words(w):
    w = lax.bitcast_convert_type(w, U32)
    lo = lax.bitcast_convert_type(jnp.left_shift(w, jnp.uint32(16)), F32)
    hi = lax.bitcast_convert_type(jnp.bitwise_and(w, jnp.uint32(0xFFFF0000)), F32)
    return lo, hi


def _all_gather(wc, sp):
    def body(w_ref, s_ref, ow_ref, os_ref, send_sems, recv_sems):
        x, y, c = _me()
        k_me = 2 * x + y
        sib = (x, y, 1 - c)
        chips = [(1 - x, y), (x, 1 - y), (1 - x, 1 - y)]
        wcols = lambda cc: pl.ds(pl.multiple_of(cc * (CWD // 2), LANE), CWD // 2)
        scols = lambda cc: pl.ds(pl.multiple_of(cc * HW, LANE), HW)

        def windows(k, cc):
            pcs = _piece_rows(k)
            return ([(w_ref.at[pl.ds(l0, n), wcols(cc)], ow_ref.at[pl.ds(p0, n), wcols(cc)]) for l0, p0, n in pcs]
                    + [(s_ref.at[:, scols(cc)], os_ref.at[k, :, scols(cc)])])

        def copy(i, src, dst, to):
            return pltpu.make_async_remote_copy(src_ref=src, dst_ref=dst, send_sem=send_sems.at[i],
                                                recv_sem=recv_sems.at[i], device_id=to, device_id_type=MESH)

        def own_windows():
            return ([(w_ref.at[pl.ds(l0, n)], ow_ref.at[pl.ds(p0, n)]) for l0, p0, n in _piece_rows(k_me)]
                    + [(s_ref, os_ref.at[k_me])])

        first = [copy(18 + i, src, dst, sib) for i, (src, dst) in enumerate(own_windows())]
        for j, (cx, cy) in enumerate(chips):
            for i, (src, dst) in enumerate(windows(k_me, c)):
                first.append(copy(3 * j + i, src, dst, (cx, cy, c)))
        for cp in first:
            cp.start()
        passed = []
        for j, (cx, cy) in enumerate(chips):
            for i, (_, dst) in enumerate(windows(2 * cx + cy, c)):
                copy(3 * j + i, dst, dst, (cx, cy, c)).wait_recv()
                cp = copy(9 + 3 * j + i, dst, dst, sib)
                cp.start()
                passed.append(cp)
        for j, (cx, cy) in enumerate(chips):
            for i, (_, dst) in enumerate(windows(2 * cx + cy, 1 - c)):
                copy(9 + 3 * j + i, dst, dst, sib).wait_recv()
        for i, (_, dst) in enumerate(own_windows()):
            copy(18 + i, dst, dst, sib).wait_recv()
        for cp in first + passed:
            cp.wait_send()

    return pl.pallas_call(
        body,
        out_shape=[jax.ShapeDtypeStruct((NINP, CWD), F32), jax.ShapeDtypeStruct((4, PACK_ROWS, PACK_W), BF16)],
        in_specs=[ANY, ANY], out_specs=[ANY, ANY],
        scratch_shapes=[pltpu.SemaphoreType.DMA((21,)), pltpu.SemaphoreType.DMA((21,))],
        name="weights_all_gather",
    )(wc, sp)


HBM = pl.BlockSpec(memory_space=pltpu.HBM)
SEM = pl.BlockSpec(memory_space=pltpu.SEMAPHORE)
EFFECT = pltpu.SideEffectType.DATAFLOW_SIDE_EFFECTING


def _in_hbm(a):
    return pltpu.with_memory_space_constraint(a, pltpu.HBM)


def _ag_windows(w_ref, s_ref, lw_ref, ls_ref, k, cc):
    wcols = pl.ds(pl.multiple_of(cc * (CWD // 2), LANE), CWD // 2)
    scols = pl.ds(pl.multiple_of(cc * HW, LANE), HW)
    return ([(w_ref.at[pl.ds(l0, n), wcols], lw_ref.at[pl.ds(p0, n), wcols]) for l0, p0, n in _piece_rows(k)]
            + [(s_ref.at[:, scols], ls_ref.at[k, :, scols])])


def _ag_ici_copies(w_ref, s_ref, lw_ref, ls_ref, send_sems, recv_sems):
    x, y, c = _me()
    mine, theirs = [], []
    for j, (cx, cy) in enumerate([(1 - x, y), (x, 1 - y), (1 - x, 1 - y)]):
        for i, ((src, dst), (_, got)) in enumerate(zip(_ag_windows(w_ref, s_ref, lw_ref, ls_ref, 2 * x + y, c),
                                                       _ag_windows(w_ref, s_ref, lw_ref, ls_ref, 2 * cx + cy, c))):
            mk = lambda s_, d_: pltpu.make_async_remote_copy(
                src_ref=s_, dst_ref=d_, send_sem=send_sems.at[3 * j + i], recv_sem=recv_sems.at[3 * j + i],
                device_id=(cx, cy, c), device_id_type=MESH)
            mine.append(mk(src, dst))
            theirs.append(mk(got, got))
    return mine, theirs


def _ag_ici_start(wc, sp, dep):
    def body(w_ref, s_ref, lw_ref, ls_ref, dep_ref, send_sems, recv_sems, w_thru, s_thru, lw_thru, ls_thru, token):
        mine, _ = _ag_ici_copies(w_ref, s_ref, lw_ref, ls_ref, send_sems, recv_sems)
        for cp in mine:
            cp.start()
        token[...] = jnp.zeros_like(token)

    return pl.pallas_call(
        body, name="weights_gather_start",
        out_shape=(pltpu.SemaphoreType.DMA((9,)), pltpu.SemaphoreType.DMA((9,)), pltpu.HBM(wc.shape, wc.dtype),
                   pltpu.HBM(sp.shape, sp.dtype), pltpu.HBM((NINP, CWD), F32), pltpu.HBM((4, PACK_ROWS, PACK_W), BF16),
                   jax.ShapeDtypeStruct((8, LANE), F32)),
        in_specs=(HBM, HBM, HBM, HBM, ANY),
        out_specs=(SEM, SEM, HBM, HBM, HBM, HBM, pl.BlockSpec(memory_space=pltpu.VMEM)),
        input_output_aliases={0: 2, 1: 3, 2: 4, 3: 5},
        compiler_params=pltpu.CompilerParams(has_side_effects=EFFECT),
    )(_in_hbm(wc), _in_hbm(sp), _in_hbm(lax.empty((NINP, CWD), F32)), _in_hbm(lax.empty((4, PACK_ROWS, PACK_W), BF16)), dep)


def _ag_ici_wait(send_sems, recv_sems, wc, sp, lw, ls, after):
    def body(w_ref, s_ref, lw_ref, ls_ref, send_sems, recv_sems, after_ref, w_dead, s_dead, lw_out, ls_out):
        mine, theirs = _ag_ici_copies(w_ref, s_ref, lw_ref, ls_ref, send_sems, recv_sems)
        for cp in mine:
            cp.wait_send()
        for cp in theirs:
            cp.wait_recv()

    out = pl.pallas_call(
        body, name="weights_gather_wait",
        out_shape=(pltpu.HBM(wc.shape, wc.dtype), pltpu.HBM(sp.shape, sp.dtype), pltpu.HBM(lw.shape, lw.dtype),
                   pltpu.HBM(ls.shape, ls.dtype)),
        in_specs=(HBM, HBM, HBM, HBM, SEM, SEM, ANY), out_specs=(HBM, HBM, HBM, HBM),
        input_output_aliases={0: 0, 1: 1, 2: 2, 3: 3},
        compiler_params=pltpu.CompilerParams(has_side_effects=EFFECT),
    )(wc, sp, lw, ls, send_sems, recv_sems, after)
    return out[2], out[3]


def _ag_finish(wc, sp, lw, ls):
    def body(w_ref, s_ref, lw_ref, ls_ref, ow_ref, os_ref, send_sems, recv_sems):
        x, y, c = _me()
        k_me = 2 * x + y
        sib = (x, y, 1 - c)
        chips = [(1 - x, y), (x, 1 - y), (1 - x, 1 - y)]

        def copy(i, src, dst):
            return pltpu.make_async_remote_copy(src_ref=src, dst_ref=dst, send_sem=send_sems.at[i],
                                                recv_sem=recv_sems.at[i], device_id=sib, device_id_type=MESH)

        def own_windows():
            return ([(w_ref.at[pl.ds(l0, n)], ow_ref.at[pl.ds(p0, n)]) for l0, p0, n in _piece_rows(k_me)]
                    + [(s_ref, os_ref.at[k_me])])

        out = [copy(9 + i, src, dst) for i, (src, dst) in enumerate(own_windows())]
        for j, (cx, cy) in enumerate(chips):
            landed = _ag_windows(w_ref, s_ref, lw_ref, ls_ref, 2 * cx + cy, c)
            for i, (_, dst) in enumerate(_ag_windows(w_ref, s_ref, ow_ref, os_ref, 2 * cx + cy, c)):
                out.append(copy(3 * j + i, landed[i][1], dst))
        for cp in out:
            cp.start()
        for j, (cx, cy) in enumerate(chips):
            for i, (_, dst) in enumerate(_ag_windows(w_ref, s_ref, ow_ref, os_ref, 2 * cx + cy, 1 - c)):
                copy(3 * j + i, dst, dst).wait_recv()
        for i, (_, dst) in enumerate(own_windows()):
            copy(9 + i, dst, dst).wait_recv()
        for cp in out:
            cp.wait_send()

    return pl.pallas_call(
        body,
        out_shape=[jax.ShapeDtypeStruct(lw.shape, lw.dtype), jax.ShapeDtypeStruct(ls.shape, ls.dtype)],
        in_specs=[ANY] * 4, out_specs=[ANY, ANY],
        input_output_aliases={2: 0, 3: 1},
        scratch_shapes=[pltpu.SemaphoreType.DMA((12,)), pltpu.SemaphoreType.DMA((12,))],
        name="weights_gather_finish",
    )(wc, sp, lw, ls)


UNPACK_BR = 512


def _unpack_w_in(cont):
    def body(c_ref, o_ref):
        lo, hi = _unpack_words(c_ref[...])
        r = pl.program_id(0) * UNPACK_BR + lax.broadcasted_iota(jnp.int32, (UNPACK_BR, CWD), 0)
        pad = jnp.logical_and(r >= KPE_END, r < KPE_END + NINP - NIN)
        o_ref[:, 0:CWD] = jnp.where(pad, 0.0, lo).astype(BF16)
        o_ref[:, CWD:2 * CWD] = jnp.where(pad, 0.0, hi).astype(BF16)

    return pl.pallas_call(
        body, grid=(NINP // UNPACK_BR,),
        in_specs=[pl.BlockSpec((UNPACK_BR, CWD), lambda i: (i, 0))],
        out_specs=pl.BlockSpec((UNPACK_BR, D), lambda i: (i, 0)),
        out_shape=jax.ShapeDtypeStruct((NINP, D), BF16),
        name="w_in_unpack",
        compiler_params=pltpu.CompilerParams(dimension_semantics=("arbitrary",), vmem_limit_bytes=VMEM_LIMIT),
    )(cont)


def _rs_swap(gw, gs):
    def body(w_ref, s_ref, rw_ref, rs_ref, send_sems, recv_sems):
        x, y, c = _me()
        oc = 1 - c
        cps = [pltpu.make_async_remote_copy(src_ref=w_ref.at[:, pl.ds(pl.multiple_of(oc * (D // 2), LANE), D // 2)],
                                            dst_ref=rw_ref, send_sem=send_sems.at[0], recv_sem=recv_sems.at[0],
                                            device_id=(x, y, oc), device_id_type=MESH),
               pltpu.make_async_remote_copy(src_ref=s_ref.at[:, :, pl.ds(pl.multiple_of(oc * HW, LANE), HW)],
                                            dst_ref=rs_ref, send_sem=send_sems.at[1], recv_sem=recv_sems.at[1],
                                            device_id=(x, y, oc), device_id_type=MESH)]
        for cp in cps:
            cp.start()
        for cp in cps:
            cp.wait()

    return pl.pallas_call(
        body,
        out_shape=[jax.ShapeDtypeStruct((NINP, D // 2), F32), jax.ShapeDtypeStruct((4, PACK_ROWS, HW), F32)],
        in_specs=[ANY, ANY], out_specs=[ANY, ANY],
        scratch_shapes=[pltpu.SemaphoreType.DMA((2,)), pltpu.SemaphoreType.DMA((2,))],
        name="grads_sibling_swap",
    )(gw, gs)


SUM_BR = 512


def _rs_chip_sum_w(gw, rw, cidx):
    def body(c_ref, g_ref, r_ref, o_ref):
        s = g_ref[...] + r_ref[...]
        q = D // 8
        o_ref[...] = jnp.concatenate([_pack_words(s[:, 0:q], s[:, q:2 * q]),
                                      _pack_words(s[:, 2 * q:3 * q], s[:, 3 * q:4 * q])], axis=1)

    return pl.pallas_call(
        body,
        grid_spec=pltpu.PrefetchScalarGridSpec(
            num_scalar_prefetch=1, grid=(NINP // SUM_BR,),
            in_specs=[pl.BlockSpec((SUM_BR, D // 2), lambda i, cr: (i, cr[0])),
                      pl.BlockSpec((SUM_BR, D // 2), lambda i, cr: (i, 0))],
            out_specs=pl.BlockSpec((SUM_BR, D // 4), lambda i, cr: (i, 0))),
        out_shape=jax.ShapeDtypeStruct((NINP, D // 4), F32),
        name="grads_chip_sum_w",
        compiler_params=pltpu.CompilerParams(dimension_semantics=("arbitrary",), vmem_limit_bytes=VMEM_LIMIT),
    )(cidx, gw, rw)


def _rs_chip_sum_s(gs, rs, cidx):
    def body(c_ref, g_ref, r_ref, o_ref):
        o_ref[...] = (g_ref[...] + r_ref[...]).astype(BF16)

    return pl.pallas_call(
        body,
        grid_spec=pltpu.PrefetchScalarGridSpec(
            num_scalar_prefetch=1, grid=(4,),
            in_specs=[pl.BlockSpec((None, PACK_ROWS, HW), lambda j, cr: (j, 0, cr[0])),
                      pl.BlockSpec((None, PACK_ROWS, HW), lambda j, cr: (j, 0, 0))],
            out_specs=pl.BlockSpec((None, PACK_ROWS, HW), lambda j, cr: (j, 0, 0))),
        out_shape=jax.ShapeDtypeStruct((4, PACK_ROWS, HW), BF16),
        name="grads_chip_sum_s",
        compiler_params=pltpu.CompilerParams(dimension_semantics=("arbitrary",), vmem_limit_bytes=VMEM_LIMIT),
    )(cidx, gs, rs)


def _rs_exchange_copies(sw_ref, ss_ref, r2w_ref, r2s_ref, send_sems, recv_sems):
    x, y, c = _me()
    mine, theirs = [], []
    for j, (cx, cy) in enumerate([(1 - x, y), (x, 1 - y), (1 - x, 1 - y)]):
        def mk(i, src, dst):
            return pltpu.make_async_remote_copy(src_ref=src, dst_ref=dst, send_sem=send_sems.at[3 * j + i],
                                                recv_sem=recv_sems.at[3 * j + i], device_id=(cx, cy, c), device_id_type=MESH)
        for i, (l0, p0, n) in enumerate(_piece_rows(2 * cx + cy)):
            mine.append(mk(i, sw_ref.at[pl.ds(p0, n)], r2w_ref.at[j, pl.ds(l0, n)]))
            theirs.append(mk(i, r2w_ref.at[j, pl.ds(l0, n)], r2w_ref.at[j, pl.ds(l0, n)]))
        mine.append(mk(2, ss_ref.at[2 * cx + cy], r2s_ref.at[j]))
        theirs.append(mk(2, r2s_ref.at[j], r2s_ref.at[j]))
    return mine, theirs


def _rs_exchange_start(sw, ss, tag):
    def body(sw_ref, ss_ref, r2w_ref, r2s_ref, send_sems, recv_sems, sw_thru, ss_thru, r2w_thru, r2s_thru, token):
        mine, _ = _rs_exchange_copies(sw_ref, ss_ref, r2w_ref, r2s_ref, send_sems, recv_sems)
        for cp in mine:
            cp.start()
        token[...] = jnp.zeros_like(token)

    return pl.pallas_call(
        body, name="grads_exchange_start_" + tag,
        out_shape=(pltpu.SemaphoreType.DMA((9,)), pltpu.SemaphoreType.DMA((9,)), pltpu.HBM(sw.shape, sw.dtype),
                   pltpu.HBM(ss.shape, ss.dtype), pltpu.HBM((3, WSH, D // 4), F32), pltpu.HBM((3, PACK_ROWS, HW), BF16),
                   jax.ShapeDtypeStruct((8, LANE), F32)),
        in_specs=(HBM, HBM, HBM, HBM),
        out_specs=(SEM, SEM, HBM, HBM, HBM, HBM, pl.BlockSpec(memory_space=pltpu.VMEM)),
        input_output_aliases={0: 2, 1: 3, 2: 4, 3: 5},
        compiler_params=pltpu.CompilerParams(has_side_effects=EFFECT),
    )(_in_hbm(sw), _in_hbm(ss), _in_hbm(lax.empty((3, WSH, D // 4), F32)), _in_hbm(lax.empty((3, PACK_ROWS, HW), BF16)))


def _rs_exchange_wait(send_sems, recv_sems, sw, ss, r2w, r2s, after, tag):
    def body(sw_ref, ss_ref, r2w_ref, r2s_ref, send_sems, recv_sems, after_ref, sw_dead, ss_dead, r2w_out, r2s_out):
        mine, theirs = _rs_exchange_copies(sw_ref, ss_ref, r2w_ref, r2s_ref, send_sems, recv_sems)
        for cp in mine:
            cp.wait_send()
        for cp in theirs:
            cp.wait_recv()

    out = pl.pallas_call(
        body, name="grads_exchange_wait_" + tag,
        out_shape=(pltpu.HBM(sw.shape, sw.dtype), pltpu.HBM(ss.shape, ss.dtype), pltpu.HBM(r2w.shape, r2w.dtype),
                   pltpu.HBM(r2s.shape, r2s.dtype)),
        in_specs=(HBM, HBM, HBM, HBM, SEM, SEM, ANY), out_specs=(HBM, HBM, HBM, HBM),
        input_output_aliases={0: 0, 1: 1, 2: 2, 3: 3},
        compiler_params=pltpu.CompilerParams(has_side_effects=EFFECT),
    )(sw, ss, r2w, r2s, send_sems, recv_sems, after)
    return out[2], out[3]


def _rs_final_w(gw, rw, r2w, idx):
    q = D // 8

    def body(i_ref, g_ref, r_ref, p_ref, o_ref, gbuf, rbuf, sems):
        i = pl.program_id(0)
        k, c = i_ref[0], i_ref[1]
        cps = []
        for n_, (l0, p0, n) in enumerate(_piece_rows(k)):
            gcol = pl.ds(pl.multiple_of(c * (D // 2) + i * 2 * q, LANE), 2 * q)
            rcol = pl.ds(pl.multiple_of(i * 2 * q, LANE), 2 * q)
            cps.append(pltpu.make_async_copy(g_ref.at[pl.ds(p0, n), gcol], gbuf.at[pl.ds(l0, n)], sems.at[2 * n_]))
            cps.append(pltpu.make_async_copy(r_ref.at[pl.ds(p0, n), rcol], rbuf.at[pl.ds(l0, n)], sems.at[2 * n_ + 1]))
        for cp in cps:
            cp.start()
        for cp in cps:
            cp.wait()
        acc = gbuf[...] + rbuf[...]
        for j in range(3):
            lo, hi = _unpack_words(p_ref[j])
            acc = acc + jnp.concatenate([lo, hi], axis=1)
        o_ref[...] = acc

    return pl.pallas_call(
        body,
        grid_spec=pltpu.PrefetchScalarGridSpec(
            num_scalar_prefetch=1, grid=(2,),
            in_specs=[ANY, ANY, pl.BlockSpec((3, WSH, q), lambda i, ir: (0, 0, i))],
            out_specs=pl.BlockSpec((WSH, 2 * q), lambda i, ir: (0, i)),
            scratch_shapes=[pltpu.VMEM((WSH, 2 * q), F32), pltpu.VMEM((WSH, 2 * q), F32), pltpu.SemaphoreType.DMA((4,))]),
        out_shape=jax.ShapeDtypeStruct((WSH, D // 2), F32),
        name="grads_final_sum_w",
        compiler_params=pltpu.CompilerParams(dimension_semantics=("arbitrary",), vmem_limit_bytes=VMEM_LIMIT),
    )(idx, gw, rw, r2w)


def _rs_final_s(gs, rs, r2s, idx):
    def body(i_ref, g_ref, r_ref, p_ref, o_ref):
        acc = g_ref[...] + r_ref[...]
        for j in range(3):
            acc = acc + p_ref[j].astype(F32)
        o_ref[...] = acc

    return pl.pallas_call(
        body,
        grid_spec=pltpu.PrefetchScalarGridSpec(
            num_scalar_prefetch=1, grid=(1,),
            in_specs=[pl.BlockSpec((None, PACK_ROWS, HW), lambda i, ir: (ir[0], 0, ir[1])),
                      pl.BlockSpec((None, PACK_ROWS, HW), lambda i, ir: (ir[0], 0, 0)),
                      pl.BlockSpec((3, PACK_ROWS, HW), lambda i, ir: (0, 0, 0))],
            out_specs=pl.BlockSpec((PACK_ROWS, HW), lambda i, ir: (0, 0))),
        out_shape=jax.ShapeDtypeStruct((PACK_ROWS, HW), F32),
        name="grads_final_sum_s",
        compiler_params=pltpu.CompilerParams(dimension_semantics=("arbitrary",), vmem_limit_bytes=VMEM_LIMIT),
    )(idx, gs, rs, r2s)


def _rs_share(fw, fs):
    def body(w_ref, s_ref, ow_ref, os_ref, send_sems, recv_sems):
        x, y, c = _me()
        cps = [pltpu.make_async_remote_copy(src_ref=w_ref, dst_ref=ow_ref, send_sem=send_sems.at[0],
                                            recv_sem=recv_sems.at[0], device_id=(x, y, 1 - c), device_id_type=MESH),
               pltpu.make_async_remote_copy(src_ref=s_ref, dst_ref=os_ref, send_sem=send_sems.at[1],
                                            recv_sem=recv_sems.at[1], device_id=(x, y, 1 - c), device_id_type=MESH)]
        for cp in cps:
            cp.start()
        for cp in cps:
            cp.wait()

    return pl.pallas_call(
        body,
        out_shape=[jax.ShapeDtypeStruct((WSH, D // 2), F32), jax.ShapeDtypeStruct((PACK_ROWS, HW), F32)],
        in_specs=[ANY, ANY], out_specs=[ANY, ANY],
        scratch_shapes=[pltpu.SemaphoreType.DMA((2,)), pltpu.SemaphoreType.DMA((2,))],
        name="grads_share",
    )(fw, fs)


def _both_halves(mine, other, c):
    return jnp.where(c == 0, jnp.concatenate([mine, other], axis=1), jnp.concatenate([other, mine], axis=1))


def _rs_begin(gw, gs):
    x, y, c = _me()
    cidx = jnp.reshape(c, (1,)).astype(jnp.int32)
    rw, rs = _rs_swap(gw, gs)
    return dict(gw=gw, gs=gs, rw=rw, rs=rs, sw=_rs_chip_sum_w(gw, rw, cidx), ss=_rs_chip_sum_s(gs, rs, cidx))


def _rs_end(st, r2w, r2s):
    x, y, c = _me()
    idx = jnp.stack([2 * x + y, c]).astype(jnp.int32)
    fw = _rs_final_w(st["gw"], st["rw"], r2w, idx)
    fs = _rs_final_s(st["gs"], st["rs"], r2s, idx)
    ow, os_ = _rs_share(fw, fs)
    return _both_halves(fw, ow, c), _both_halves(fs, os_, c)


def _all_reduce_small(gs):
    rows = gs.shape[0]

    def body(g_ref, o_ref, buf, send_sems, recv_sems):
        x, y, c = _me()
        me = 4 * x + 2 * y + c
        buf[me] = g_ref[...]
        cps = []
        for r in range(1, 8):
            fx, fy, fc = (r >> 2) & 1, (r >> 1) & 1, r & 1
            px, py, pc = jnp.bitwise_xor(x, fx), jnp.bitwise_xor(y, fy), jnp.bitwise_xor(c, fc)
            cps.append((pltpu.make_async_remote_copy(
                src_ref=g_ref, dst_ref=buf.at[me], send_sem=send_sems.at[r - 1], recv_sem=recv_sems.at[r - 1],
                device_id=(px, py, pc), device_id_type=MESH), 4 * px + 2 * py + pc))
        for cp, _ in cps:
            cp.start()
        for r, (cp, peer) in enumerate(cps):
            pltpu.make_async_remote_copy(
                src_ref=g_ref, dst_ref=buf.at[peer], send_sem=send_sems.at[r], recv_sem=recv_sems.at[r],
                device_id=(x, y, c), device_id_type=MESH).wait_recv()
        for cp, _ in cps:
            cp.wait_send()
        acc = buf[0]
        for k in range(1, 8):
            acc = acc + buf[k]
        o_ref[...] = acc

    return pl.pallas_call(
        body,
        out_shape=jax.ShapeDtypeStruct((rows, LANE), F32),
        in_specs=[pl.BlockSpec(memory_space=pltpu.VMEM)],
        out_specs=pl.BlockSpec(memory_space=pltpu.VMEM),
        scratch_shapes=[pltpu.VMEM((8, rows, LANE), F32), pltpu.SemaphoreType.DMA((7,)), pltpu.SemaphoreType.DMA((7,))],
        name="small_grads_all_reduce",
    )(gs)


PACK_SPLIT = (("w_uq", 96, (QL, 192)), ("w_ukv", 64, (KVL, 256)),
              ("w_out_a", 256, (CW, 256)), ("w_out_b", 256, (CW, 256)), ("w_out_c", 256, (CW, 256)),
              ("w_o", 512, (256, D)))
MAT_ROWS = 1440
CONV_SHARD = 3 * 128


def _w_in_words(w_in_shard):
    t = w_in_shard.T
    return _pack_words(t[:, :CWD], t[:, CWD:])


def _pack_weights(wl):
    parts = [wl[n].astype(BF16).reshape(-1, PACK_W) for n, _, _ in PACK_SPLIT]
    cw = wl["conv_w"].reshape(-1)
    hi = cw.astype(BF16)
    r1 = cw - hi.astype(F32)
    mid = r1.astype(BF16)
    lo = (r1 - mid.astype(F32)).astype(BF16)
    cterms = jnp.pad(jnp.concatenate([hi, mid, lo]), (0, 3 * PACK_W - 3 * CONV_SHARD)).reshape(3, PACK_W)
    tail = jnp.pad(cterms, ((0, PACK_ROWS - MAT_ROWS - 3), (0, 0)))
    return jnp.concatenate(parts + [tail], axis=0)


def _unpack_weights(gath):
    out = {}
    r = 0
    for n, nrows, shp in PACK_SPLIT:
        t = gath[:, r:r + nrows].reshape((4,) + shp)
        r += nrows
        if n == "w_o":
            out[n] = t.reshape(4 * shp[0], shp[1])
        else:
            out[n] = t.transpose(1, 0, 2).reshape(shp[0], 4 * shp[1])
    ct = gath[:, r:r + 3].reshape(4, 3 * PACK_W)[:, :3 * CONV_SHARD].astype(F32).reshape(4, 3, CONV_SHARD)
    cw = (ct[:, 0] + ct[:, 1]) + ct[:, 2]
    out["conv_w"] = cw.reshape(4, 3, 128).transpose(1, 0, 2).reshape(3, CW)
    return out


def _pack_grads(g):
    parts = []
    for n, nrows, shp in PACK_SPLIT:
        t = g[n]
        if n == "w_o":
            t = t.reshape((4,) + shp)
        else:
            t = t.reshape(shp[0], 4, shp[1]).transpose(1, 0, 2)
        parts.append(t.reshape(4, nrows, PACK_W))
    cw = g["conv_w"].reshape(3, 4, 128).transpose(1, 0, 2).reshape(4, 1, CONV_SHARD)
    parts.append(jnp.pad(cw, ((0, 0), (0, PACK_ROWS - MAT_ROWS - 1), (0, PACK_W - CONV_SHARD))))
    return jnp.concatenate(parts, axis=1)


def _unpack_grads(red):
    out = {}
    r = 0
    for n, nrows, shp in PACK_SPLIT:
        out[n] = red[r:r + nrows].reshape(shp)
        r += nrows
    out["conv_w"] = red[r, :CONV_SHARD].reshape(3, 128)
    return out


SMALL_SIZES = (("norm_g", D), ("b_gate", 3 * D), ("conv_b", CW), ("q_a_norm_g", QL), ("kv_a_norm_g", KVL),
               ("mla_q_norm_g", QK), ("mla_k_norm_g", QK), ("dil_q_norm_g", NG * HD), ("dil_k_norm_g", NG * HD))
SMALL_ROWS = 88


def _pack_small(per_name):
    flat = jnp.concatenate([per_name[n].reshape(-1).astype(F32) for n, _ in SMALL_SIZES])
    return jnp.pad(flat, (0, SMALL_ROWS * LANE - flat.shape[0])).reshape(SMALL_ROWS, LANE)


def _unpack_small(packed, like):
    out = {}
    flat = packed.reshape(-1)
    r = 0
    for n, sz in SMALL_SIZES:
        out[n] = flat[r:r + NL * sz].reshape(like[n].shape)
        r += NL * sz
    return out


def _adamw_math(w, g, m, v):
    m = ADAM_B1 * m + (1.0 - ADAM_B1) * g
    v = ADAM_B2 * v + (1.0 - ADAM_B2) * jnp.square(g)
    m_hat = m / (1.0 - ADAM_B1 ** ADAM_STEP)
    v_hat = v / (1.0 - ADAM_B2 ** ADAM_STEP)
    delta = -ADAM_LR * (m_hat / (jnp.sqrt(v_hat) + ADAM_EPS) + ADAM_WD * w)
    return delta, m, v


def _adamw(name, w, g, m, v, br, bc=None):
    L, R, C = w.shape
    bc = C if bc is None else bc
    blk = lambda l, i, j: (l, i, j)
    return _pcall(name, _adamw_math, (L, R // br, C // bc), [(t, (None, br, bc), blk) for t in (w, g, m, v)],
                  [((L, R, C), F32, (None, br, bc), blk)] * 3)


ADAM_ROWS = {"w_uq": 256, "w_ukv": 128, "w_out_a": 512, "w_out_b": 512, "w_out_c": 512, "w_o": 256,
             "conv_w": 3}


def kernel(x, norm_g, w_in, b_gate, conv_w, conv_b, q_a_norm_g, w_uq, kv_a_norm_g, w_ukv, mla_q_norm_g, mla_k_norm_g, dil_q_norm_g, dil_k_norm_g, w_out_a, w_out_b, w_out_c, w_o, loss_target, m_norm_g, m_w_in, m_b_gate, m_conv_w, m_conv_b, m_q_a_norm_g, m_w_uq, m_kv_a_norm_g, m_w_ukv, m_mla_q_norm_g, m_mla_k_norm_g, m_dil_q_norm_g, m_dil_k_norm_g, m_w_out_a, m_w_out_b, m_w_out_c, m_w_o, v_norm_g, v_w_in, v_b_gate, v_conv_w, v_conv_b, v_q_a_norm_g, v_w_uq, v_kv_a_norm_g, v_w_ukv, v_mla_q_norm_g, v_mla_k_norm_g, v_dil_q_norm_g, v_dil_k_norm_g, v_w_out_a, v_w_out_b, v_w_out_c, v_w_o):
    W = dict(norm_g=norm_g, w_in=w_in, b_gate=b_gate, conv_w=conv_w, conv_b=conv_b, q_a_norm_g=q_a_norm_g, w_uq=w_uq,
             kv_a_norm_g=kv_a_norm_g, w_ukv=w_ukv, mla_q_norm_g=mla_q_norm_g, mla_k_norm_g=mla_k_norm_g,
             dil_q_norm_g=dil_q_norm_g, dil_k_norm_g=dil_k_norm_g, w_out_a=w_out_a, w_out_b=w_out_b, w_out_c=w_out_c,
             w_o=w_o)
    M = dict(norm_g=m_norm_g, w_in=m_w_in, b_gate=m_b_gate, conv_w=m_conv_w, conv_b=m_conv_b, q_a_norm_g=m_q_a_norm_g,
             w_uq=m_w_uq, kv_a_norm_g=m_kv_a_norm_g, w_ukv=m_w_ukv, mla_q_norm_g=m_mla_q_norm_g,
             mla_k_norm_g=m_mla_k_norm_g, dil_q_norm_g=m_dil_q_norm_g, dil_k_norm_g=m_dil_k_norm_g, w_out_a=m_w_out_a,
             w_out_b=m_w_out_b, w_out_c=m_w_out_c, w_o=m_w_o)
    V = dict(norm_g=v_norm_g, w_in=v_w_in, b_gate=v_b_gate, conv_w=v_conv_w, conv_b=v_conv_b, q_a_norm_g=v_q_a_norm_g,
             w_uq=v_w_uq, kv_a_norm_g=v_kv_a_norm_g, w_ukv=v_w_ukv, mla_q_norm_g=v_mla_q_norm_g,
             mla_k_norm_g=v_mla_k_norm_g, dil_q_norm_g=v_dil_q_norm_g, dil_k_norm_g=v_dil_k_norm_g, w_out_a=v_w_out_a,
             w_out_b=v_w_out_b, w_out_c=v_w_out_c, w_o=v_w_o)
    batch = x.shape[0]
    T = batch * S

    def layer_weights(l, cont, gath):
        full = _unpack_weights(gath)
        pad_qk = lambda t: jnp.pad(t, (0, QKP - QK)).reshape(1, QKP)
        full.update(
            w_in_t=_unpack_w_in(cont),
            norm_g=norm_g[l].reshape(1, D), b_gate=b_gate[l].reshape(1, 3 * D), conv_b=conv_b[l].reshape(1, CW),
            q_a_norm_g=q_a_norm_g[l].reshape(1, QL), kv_a_norm_g=kv_a_norm_g[l].reshape(1, KVL),
            mla_q_norm_g=pad_qk(mla_q_norm_g[l]), mla_k_norm_g=pad_qk(mla_k_norm_g[l]),
            dil_q_norm_g=dil_q_norm_g[l].reshape(NG, 1, HD), dil_k_norm_g=dil_k_norm_g[l].reshape(NG, 1, HD))
        return full

    words = [_w_in_words(w_in[l]) for l in range(NL)]
    packs = [_pack_weights({n: W[n][l] for n in BIG[1:] + ("conv_w",)}) for l in range(NL)]
    tabs = _rope_tables() + (_dil_slopes(),)
    x2 = x.reshape(T, D)

    cont0, gath0 = _all_gather(words[0], packs[0])
    w0 = layer_weights(0, cont0, gath0)
    ag = _ag_ici_start(words[1], packs[1], gath0)
    w0["norm_g"] = w0["norm_g"] + ag[6][0:1, 0:1]
    y0, res0 = _layer_fwd(x2, w0, tabs, batch)
    lw, ls = _ag_ici_wait(ag[0], ag[1], ag[2], ag[3], ag[4], ag[5], y0)
    w1 = layer_weights(1, *_ag_finish(words[1], packs[1], lw, ls))
    y1, res1 = _layer_fwd(y0, w1, tabs, batch)

    row = lambda i: (i, 0)
    dy, loss = _pcall("loss", _loss_math, (T // BR,),
                      [(y1, (BR, D), row), (loss_target.reshape(T, D), (BR, D), row)],
                      [((T, D), F32, (BR, D), row), ((1, 1), F32, (1, 1), lambda i: (0, 0), True)])
    loss = lax.psum(loss[0, 0], ("x", "y", "c"))

    grads = [None] * NL
    dy, grads[1] = _layer_bwd(dy, w1, res1, tabs, batch)
    st = [None] * NL
    ex = [None] * NL
    st[1] = _rs_begin(grads[1]["w_in_t"], _pack_grads(grads[1]))
    ex[1] = _rs_exchange_start(st[1]["sw"], st[1]["ss"], "1")
    w0["w_o"] = w0["w_o"] + ex[1][6][0:1, 0:1].astype(BF16)

    def start_layer0(g):
        st[0] = _rs_begin(g["w_in_t"], _pack_grads(g))
        ex[0] = _rs_exchange_start(st[0]["sw"], st[0]["ss"], "0")
        return ex[0][6]

    dx, grads[0] = _layer_bwd(dy, w0, res0, tabs, batch, after_dw=start_layer0)
    grad_x = dx.reshape(batch, S, D)

    red = [None] * NL
    for l in (1, 0):
        r2w, r2s = _rs_exchange_wait(*ex[l][:6], dx, str(l))
        rw, rs = _rs_end(st[l], r2w, r2s)
        r = _unpack_grads(rs)
        r["w_in_t"] = rw
        red[l] = r
    G = {n: jnp.stack([red[l][n] for l in range(NL)]) for n in BIG[1:] + ("conv_w",)}
    g_in_t = jnp.stack([red[l]["w_in_t"] for l in range(NL)])
    G["w_in"] = jnp.swapaxes(g_in_t, 1, 2)
    small_g = {n: jnp.stack([grads[l][n].reshape(-1)[:sz] for l in range(NL)]) for n, sz in SMALL_SIZES}
    small_red = _all_reduce_small(_pack_small(small_g))
    G.update(_unpack_small(small_red, {n: W[n] for n in SMALL}))

    delta, new_m, new_v = {}, {}, {}
    for n in BIG[1:] + ("conv_w",):
        delta[n], new_m[n], new_v[n] = _adamw("adamw_" + n, W[n], G[n], M[n], V[n], ADAM_ROWS[n])
    tr = lambda t: jnp.swapaxes(t, 1, 2)
    delta["w_in"], new_m["w_in"], new_v["w_in"] = (
        tr(t) for t in _adamw("adamw_w_in", tr(w_in), g_in_t, tr(m_w_in), tr(v_w_in), WSH, LANE))
    sw, sm, sv = (_pack_small({n: t[n] for n in SMALL})[None] for t in (W, M, V))
    sd, snm, snv = _adamw("adamw_small", sw, small_red[None], sm, sv, SMALL_ROWS)
    like = {n: W[n] for n in SMALL}
    delta.update(_unpack_small(sd[0], like))
    new_m.update(_unpack_small(snm[0], like))
    new_v.update(_unpack_small(snv[0], like))

    return (loss, grad_x, *[G[n] for n in WEIGHTS], *[delta[n] for n in WEIGHTS],
            *[new_m[n] for n in WEIGHTS], *[new_v[n] for n in WEIGHTS])
```

```python
import functools

import numpy as np
import jax
import jax.numpy as jnp
from jax import lax
from jax.experimental import pallas as pl
from jax.experimental.pallas import tpu as pltpu

F32 = jnp.float32
BF16 = jnp.bfloat16

D = 1024
S = 2048
NL = 2
CW = 512
NH = 8
QL = 256
KVL = 128
NOPE = 64
ROPE = 32
VD = 64
QK = NOPE + ROPE
QKP = 128
ROPE_THETA = 10000.0
DIL = ((128, 1), (512, 4), (2048, 16))
NG = 3
DH = 8
HD = 64
DWID = DH * HD
QB = 128
EPS = 1e-6
NIN = 11168
NINP = 11264
O_A, O_CQ, O_CKV, O_KPE, O_BZ, O_DQ, O_DK, O_DV, O_CZ, O_G = 0, 2048, 2304, 2432, 2560, 3072, 4608, 6144, 7680, 8192
KPE_END = 2464
NEG = -1e30
MLA_SCALE = QK ** -0.5
DIL_SCALE = HD ** -0.5
LANE = 128
PACK_W = 512
VMEM_LIMIT = 48 * 1024 * 1024

ADAM_LR = 0.001
ADAM_B1 = 0.9
ADAM_B2 = 0.999
ADAM_EPS = 1e-08
ADAM_WD = 0.01
ADAM_STEP = 10

MESH = pl.DeviceIdType.MESH
BIG = ("w_in", "w_uq", "w_ukv", "w_out_a", "w_out_b", "w_out_c", "w_o")
SMALL = ("norm_g", "b_gate", "conv_b", "q_a_norm_g", "kv_a_norm_g", "mla_q_norm_g", "mla_k_norm_g",
         "dil_q_norm_g", "dil_k_norm_g")
WEIGHTS = ("norm_g", "w_in", "b_gate", "conv_w", "conv_b", "q_a_norm_g", "w_uq", "kv_a_norm_g", "w_ukv",
           "mla_q_norm_g", "mla_k_norm_g", "dil_q_norm_g", "dil_k_norm_g", "w_out_a", "w_out_b", "w_out_c", "w_o")


def _dot(a, b):
    return jnp.dot(a, b, preferred_element_type=F32)


def _dot_nt(a, b):
    return lax.dot_general(a, b, (((1,), (1,)), ((), ())), preferred_element_type=F32)


def _dot_tn(a, b):
    return lax.dot_general(a, b, (((0,), (0,)), ((), ())), preferred_element_type=F32)


def _grid_step(grid):
    step = pl.program_id(0)
    for a in range(1, len(grid)):
        step = step * grid[a] + pl.program_id(a)
    n = 1
    for g in grid:
        n *= g
    return step, n


def _write_windows(buf_ref, stages, sems, step, nsteps, puts):
    slot = step % 2
    for t, (v, dst) in enumerate(puts):
        cp = pltpu.make_async_copy(stages[t].at[slot], dst, sems.at[t, slot])

        @pl.when(step >= 2)
        def _():
            cp.wait()

        stages[t][slot] = v.astype(stages[t].dtype).reshape(stages[t].shape[1:])
        cp.start()

    @pl.when(step == nsteps - 1)
    def _():
        for t, (v, dst) in enumerate(puts):
            pltpu.make_async_copy(stages[t].at[slot], dst, sems.at[t, slot]).wait()
            if nsteps > 1:
                pltpu.make_async_copy(stages[t].at[1 - slot], dst, sems.at[t, 1 - slot]).wait()


def _pcall(name, fn, grid, ins, outs, into=None):
    n_in = len(ins)
    n_out = len(outs)
    acc_axis = len(grid) - 1
    is_acc = [len(o) > 4 and o[4] for o in outs]
    outs = [o[:4] for o in outs]
    targets = into[1] if into is not None else []
    n_t = len(targets)

    def body(*refs):
        vals = fn(*[r[...].astype(F32) for r in refs[:n_in]])
        if not isinstance(vals, (tuple, list)):
            vals = (vals,)
        o0 = n_in + (1 if n_t else 0)
        for k in range(n_out):
            r = refs[o0 + k]
            v = vals[k].astype(r.dtype).reshape(r.shape)
            if is_acc[k]:
                first = pl.program_id(acc_axis) == 0

                @pl.when(first)
                def _():
                    r[...] = v

                @pl.when(jnp.logical_not(first))
                def _():
                    r[...] += v
            else:
                r[...] = v
        if n_t:
            buf_ref = refs[o0 + n_out]
            stages = refs[o0 + n_out + 1:o0 + n_out + 1 + n_t]
            ids = [pl.program_id(a) for a in range(len(grid))]
            step, nsteps = _grid_step(grid)
            _write_windows(buf_ref, stages, refs[-1], step, nsteps,
                           [(vals[n_out + t], targets[t][1](buf_ref, *ids)) for t in range(n_t)])

    in_specs = [pl.BlockSpec(bs, im) for _, bs, im in ins]
    out_specs = [pl.BlockSpec(bs, im) for _, _, bs, im in outs]
    out_shape = [jax.ShapeDtypeStruct(sh, dt) for sh, dt, _, _ in outs]
    args = [a for a, _, _ in ins]
    extra = {}
    if n_t:
        buf = into[0]
        in_specs.append(pl.BlockSpec(memory_space=pl.ANY))
        out_specs.append(pl.BlockSpec(memory_space=pl.ANY))
        out_shape.append(jax.ShapeDtypeStruct(buf.shape, buf.dtype))
        args.append(buf)
        extra = dict(input_output_aliases={n_in: n_out},
                     scratch_shapes=[pltpu.VMEM((2,) + tuple(bs), buf.dtype) for bs, _ in targets]
                     + [pltpu.SemaphoreType.DMA((n_t, 2))])
    return pl.pallas_call(
        body,
        grid=grid,
        in_specs=in_specs,
        out_specs=out_specs,
        out_shape=out_shape,
        name=name,
        compiler_params=pltpu.CompilerParams(
            dimension_semantics=("arbitrary",) * len(grid), vmem_limit_bytes=VMEM_LIMIT),
        **extra,
    )(*args)


def _mm(name, a, b, *, ta=False, tb=False, out_dtype=F32, add=None, dep=None, tm=2048, tn=1024, tk=1024):
    if ta:
        K, M = a.shape
    else:
        M, K = a.shape
    if tb:
        N, K2 = b.shape
    else:
        K2, N = b.shape
    assert K == K2, (name, a.shape, b.shape)
    tm, tn, tk = min(tm, M), min(tn, N), min(tk, K)
    assert M % tm == 0 and N % tn == 0 and K % tk == 0, (name, M, N, K)
    nk = K // tk
    dims = (((0 if ta else 1,), (1 if tb else 0,)), ((), ()))
    a_spec = pl.BlockSpec((tk, tm), lambda j, i, k: (k, i)) if ta else pl.BlockSpec((tm, tk), lambda j, i, k: (i, k))
    b_spec = pl.BlockSpec((tn, tk), lambda j, i, k: (j, k)) if tb else pl.BlockSpec((tk, tn), lambda j, i, k: (k, j))
    o_spec = pl.BlockSpec((tm, tn), lambda j, i, k: (i, j))
    has_add = add is not None
    n_in = 2 + has_add + (dep is not None)

    def body(*refs):
        a_ref, b_ref = refs[0], refs[1]
        add_ref = refs[2] if has_add else None
        o_ref = refs[n_in]
        p = lax.dot_general(a_ref[...].astype(BF16), b_ref[...].astype(BF16), dims, preferred_element_type=F32)
        if nk == 1:
            if has_add:
                p = p + add_ref[...]
            o_ref[...] = p.astype(out_dtype)
        else:
            acc = refs[-1]
            k = pl.program_id(2)

            @pl.when(k == 0)
            def _():
                acc[...] = p

            @pl.when(k > 0)
            def _():
                acc[...] += p

            @pl.when(k == nk - 1)
            def _():
                r = acc[...]
                if has_add:
                    r = r + add_ref[...]
                o_ref[...] = r.astype(out_dtype)

    in_specs = [a_spec, b_spec] + ([o_spec] if has_add else []) + ([pl.BlockSpec(memory_space=pl.ANY)] if dep is not None else [])
    args = [a, b] + ([add] if has_add else []) + ([dep] if dep is not None else [])
    return pl.pallas_call(
        body,
        grid=(N // tn, M // tm, nk),
        in_specs=in_specs,
        out_specs=o_spec,
        out_shape=jax.ShapeDtypeStruct((M, N), out_dtype),
        scratch_shapes=[pltpu.VMEM((tm, tn), F32)] if nk > 1 else [],
        name=name,
        compiler_params=pltpu.CompilerParams(
            dimension_semantics=("arbitrary", "arbitrary", "arbitrary"), vmem_limit_bytes=VMEM_LIMIT),
    )(*args)


def _vjp_of(f, n_diff):
    def g(*args, n_prim):
        prim = args[:n_diff]
        consts = args[n_diff:n_prim]
        cts = args[n_prim:]
        _, pull = jax.vjp(lambda *p: f(*p, *consts), *prim)
        out = jax.eval_shape(lambda *p: f(*p, *consts), *prim)
        if isinstance(out, (tuple, list)):
            cts = tuple(c.astype(o.dtype) for c, o in zip(cts, out))
        else:
            cts = cts[0].astype(out.dtype)
        return pull(cts)
    return g


def _rms(x, g, n=None):
    n = x.shape[-1] if n is None else n
    ms = jnp.sum(x * x, axis=-1, keepdims=True) / n
    return x * lax.rsqrt(ms + EPS) * g


def _silu(z):
    return z * jax.nn.sigmoid(z)


def _roll_rows(u, k):
    n = u.shape[0]
    r = pltpu.roll(u, k % n, 0)
    t = lax.broadcasted_iota(jnp.int32, u.shape, 0)
    if k > 0:
        return jnp.where(t >= k, r, 0.0)
    return jnp.where(t < n + k, r, 0.0)


@functools.partial(jax.custom_vjp, nondiff_argnums=(1,))
def _shift(u, k):
    return _roll_rows(u, k)


def _shift_fwd(u, k):
    return _roll_rows(u, k), None


def _shift_bwd(k, _, g):
    return (_roll_rows(g, -k),)


_shift.defvjp(_shift_fwd, _shift_bwd)


@functools.partial(jax.custom_vjp, nondiff_argnums=(1,))
def _lane_roll(u, k):
    return pltpu.roll(u, k % LANE, 1)


def _lane_roll_fwd(u, k):
    return pltpu.roll(u, k % LANE, 1), None


def _lane_roll_bwd(k, _, g):
    return (pltpu.roll(g, (-k) % LANE, 1),)


_lane_roll.defvjp(_lane_roll_fwd, _lane_roll_bwd)


def _conv_math(ab, ac, ax, az, cw, cb):
    u = ac * ax
    conv = cb + _shift(u, 2) * cw[0:1] + _shift(u, 1) * cw[1:2] + u * cw[2:3]
    return ab * conv * _silu(az)


def _mla_pre_math(cq, ckv, gq, gkv):
    return _rms(cq, gq), _rms(ckv, gkv)


def _rope_math(q, kn, kpe, gq, gk, c, s1, s2):
    lane = lax.broadcasted_iota(jnp.int32, kpe.shape, 1)
    pe = _lane_roll(jnp.where(lane < ROPE, kpe, 0.0), NOPE)

    def one(t, g):
        tn = _rms(t, g, QK)
        return tn * c + _lane_roll(tn, -16) * s1 + _lane_roll(tn, 16) * s2

    qs, ks = [], []
    for h in range(NH):
        sl = slice(h * QKP, (h + 1) * QKP)
        qs.append(one(q[:, sl], gq))
        ks.append(one(kn[:, sl] + pe, gk))
    return jnp.concatenate(qs, axis=1), jnp.concatenate(ks, axis=1)


def _gate_math(o, z):
    return o * _silu(z)


def _mergec_math(o0, o1, o2, l0, l1, l2, cz):
    m = lax.stop_gradient(jnp.maximum(jnp.maximum(l0, l1), l2))
    e0, e1, e2 = jnp.exp(l0 - m), jnp.exp(l1 - m), jnp.exp(l2 - m)
    den = e0 + e1 + e2
    oc = (e0 / den) * o0 + (e1 / den) * o1 + (e2 / den) * o2
    return oc * _silu(cz)


def _merge_math(g0, g1, g2, b0, b1, b2, pa, pb, pc):
    return (jax.nn.sigmoid(g0 + b0) * pa + jax.nn.sigmoid(g1 + b1) * pb) + jax.nn.sigmoid(g2 + b2) * pc


MLA_T = 256
MLA_UNROLL = True


def _mla_fwd(q, k, v):
    B = q.shape[0]
    T = MLA_T
    NB = S // T

    def body(q_ref, k_ref, v_ref, o_ref, l_ref):
        row = lax.broadcasted_iota(jnp.int32, (T, T), 0)
        col = lax.broadcasted_iota(jnp.int32, (T, T), 1)
        lo = _lo_mask((T, LANE))

        for qi in range(NB):
            qb = q_ref[qi * T:(qi + 1) * T, :]

            def step(j, carry, diagonal):
                m, l, acc = carry
                off = pl.multiple_of(j * T, T)
                kb = k_ref[pl.ds(off, T), :]
                vb = v_ref[pl.ds(off, T), :]
                ss = []
                for e in (0, 1):
                    se = _dot_nt(qb[:, e * QKP:(e + 1) * QKP], kb[:, e * QKP:(e + 1) * QKP]) * MLA_SCALE
                    ss.append(jnp.where(col <= row, se, NEG) if diagonal else se)
                s = jnp.concatenate(ss, axis=0)
                m_new = jnp.maximum(m, jnp.max(s, axis=-1, keepdims=True))
                a = jnp.exp(m - m_new)
                p = jnp.exp(s - m_new)
                l = a * l + jnp.sum(p, axis=-1, keepdims=True)
                acc = a * acc + _dot(p.astype(BF16), vb)
                return m_new, l, acc

            init = (jnp.full((2 * T, 1), NEG, F32), jnp.zeros((2 * T, 1), F32), jnp.zeros((2 * T, LANE), F32))
            carry = lax.fori_loop(0, qi, functools.partial(step, diagonal=False), init, unroll=MLA_UNROLL)
            m, l, acc = step(qi, carry, True)
            o = acc / l
            lse = m + jnp.log(l)
            o_ref[qi * T:(qi + 1) * T, :] = jnp.where(lo, o[:T], o[T:])
            l_ref[qi * T:(qi + 1) * T, :] = jnp.where(lo, lse[:T], lse[T:])

    def spec(w):
        return pl.BlockSpec((None, S, w), lambda b, hp: (b, 0, hp))

    return pl.pallas_call(
        body,
        grid=(B, NH // 2),
        in_specs=[spec(2 * QKP), spec(2 * QKP), spec(LANE)],
        out_specs=[spec(LANE), spec(LANE)],
        out_shape=[jax.ShapeDtypeStruct((B, S, NH * VD), F32)] * 2,
        name="mla_attn_fwd",
        compiler_params=pltpu.CompilerParams(dimension_semantics=("arbitrary",) * 2, vmem_limit_bytes=VMEM_LIMIT),
    )(q, k, v)


def _mla_bwd(q, k, v, do, o, lse):
    B = q.shape[0]
    T = MLA_T
    NB = S // T

    def body(q_ref, k_ref, v_ref, do_ref, o_ref, l_ref, dq_ref, dk_ref, dv_ref, delta_ref, dqt_ref):
        delta_ref[...] = _head_sum(do_ref[...] * o_ref[...])
        row = lax.broadcasted_iota(jnp.int32, (T, T), 0)
        col = lax.broadcasted_iota(jnp.int32, (T, T), 1)
        lo = _lo_mask((T, LANE))
        tn_t = (((0,), (1,)), ((), ()))

        for j in range(NB):
            krows = slice(j * T, (j + 1) * T)
            kb = k_ref[krows, :]
            vb = v_ref[krows, :]
            dkt = [jnp.zeros((QKP, T), F32), jnp.zeros((QKP, T), F32)]
            dvt = jnp.zeros((LANE, T), F32)
            for i in range(j, NB):
                qrows = slice(i * T, (i + 1) * T)
                qb = q_ref[qrows, :]
                do2 = _stack_heads(do_ref[qrows, :], lo).astype(BF16)
                lb = l_ref[qrows, :]
                db = delta_ref[qrows, :]
                dp2 = _dot_nt(do2, vb)
                ps = []
                for e in (0, 1):
                    cols = slice(e * QKP, (e + 1) * QKP)
                    qe, ke = qb[:, cols], kb[:, cols]
                    s = _dot_nt(qe, ke) * MLA_SCALE
                    if i == j:
                        s = jnp.where(col <= row, s, NEG)
                    p = jnp.exp(s - lb[:, e * HD:e * HD + 1])
                    ps.append(p.astype(BF16))
                    ds = (p * (dp2[e * T:(e + 1) * T] - db[:, e * HD:e * HD + 1]) * MLA_SCALE).astype(BF16)
                    dkt[e] = dkt[e] + _dot_tn(qe, ds)
                    dq_t = lax.dot_general(ke, ds, tn_t, preferred_element_type=F32)
                    if j == 0:
                        dqt_ref[e, :, qrows] = dq_t
                    else:
                        dqt_ref[e, :, qrows] += dq_t
                dvt = dvt + _dot_tn(do2, jnp.concatenate(ps, axis=0))
            dk_ref[krows, 0:QKP] = dkt[0].T
            dk_ref[krows, QKP:2 * QKP] = dkt[1].T
            dv_ref[krows, :] = dvt.T
        dq_ref[:, 0:QKP] = dqt_ref[0].T
        dq_ref[:, QKP:2 * QKP] = dqt_ref[1].T

    def spec(w):
        return pl.BlockSpec((None, S, w), lambda b, hp: (b, 0, hp))

    return pl.pallas_call(
        body,
        grid=(B, NH // 2),
        in_specs=[spec(2 * QKP), spec(2 * QKP), spec(LANE), spec(LANE), spec(LANE), spec(LANE)],
        out_specs=[spec(2 * QKP), spec(2 * QKP), spec(LANE)],
        out_shape=[jax.ShapeDtypeStruct((B, S, NH * QKP), F32), jax.ShapeDtypeStruct((B, S, NH * QKP), F32),
                   jax.ShapeDtypeStruct((B, S, NH * VD), F32)],
        scratch_shapes=[pltpu.VMEM((S, LANE), F32), pltpu.VMEM((2, QKP, S), F32)],
        name="mla_attn_bwd",
        compiler_params=pltpu.CompilerParams(dimension_semantics=("arbitrary",) * 2, vmem_limit_bytes=VMEM_LIMIT),
    )(q, k, v, do, o, lse)


def _lo_mask(shape):
    return lax.broadcasted_iota(jnp.int32, shape, len(shape) - 1) < HD


def _head_sum(u):
    r = lax.broadcasted_iota(jnp.int32, (LANE, LANE), 0) < HD
    c = lax.broadcasted_iota(jnp.int32, (LANE, LANE), 1) < HD
    ones = jnp.where(r == c, 1.0, 0.0).astype(BF16)
    hi = u.astype(BF16)
    lo = (u - hi.astype(F32)).astype(BF16)
    return _dot(hi, ones) + _dot(lo, ones)


def _head_sum_1(u):
    r = lax.broadcasted_iota(jnp.int32, (LANE, LANE), 0) < HD
    c = lax.broadcasted_iota(jnp.int32, (LANE, LANE), 1) < HD
    return _dot(u.astype(BF16), jnp.where(r == c, 1.0, 0.0).astype(BF16))


def _rms2_scale(x):
    return lax.rsqrt(_head_sum(x * x) / HD + EPS)


def _rms2(x, g):
    return x * _rms2_scale(x) * g


def _rms2_bwd(x, r, g, dy):
    xn = x * r
    t = dy * g
    dx = r * (t - xn * (_head_sum_1(xn * t) * (1.0 / HD)))
    return dx, jnp.sum(dy * xn, axis=0, keepdims=True)


def _dil_bias(t_ref, gi, d):
    qq = lax.broadcasted_iota(jnp.int32, (QB, QB), 0)
    kk = lax.broadcasted_iota(jnp.int32, (QB, QB), 1)
    jc = (qq - kk).astype(F32)
    rows = []
    for e in (0, 1):
        sl = t_ref[2 * gi + e:2 * gi + e + 1, :] * float(d)
        bp = jnp.where(kk >= qq, -sl * (jc + float(QB)), NEG)
        bc = jnp.where(kk <= qq, -sl * jc, NEG)
        rows.append(jnp.concatenate([bp, bc], axis=1))
    return jnp.concatenate(rows, axis=0)


def _dil_rows(cur, d):
    return pl.ds(cur, QB, stride=d) if d > 1 else pl.ds(pl.multiple_of(cur, QB), QB)


def _dil_walk(d, block, full):
    if d == 1:
        block(0, None)

        def body(i, c):
            block(i * QB, (i - 1) * QB)
            return c
        lax.fori_loop(1, S // QB, body, 0, unroll=True if full else 5)
    elif d == 16:
        def body(r, c):
            block(r, None)
            return c
        lax.fori_loop(0, d, body, 0, unroll=True if full else 4)
    else:
        nb = S // d // QB

        def cls(r, c):
            block(r, None)

            def body(i, c2):
                block(r + i * QB * d, r + (i - 1) * QB * d)
                return c2
            lax.fori_loop(1, nb, body, 0, unroll=True)
            return c
        lax.fori_loop(0, d, cls, 0, unroll=full)


def _stack_heads(x, lo):
    return jnp.concatenate([jnp.where(lo, x, 0.0), jnp.where(lo, 0.0, x)], axis=0)


def _dilc_fwd(proj3, gq, gk, tab):
    B = proj3.shape[0]

    def body(q_ref, k_ref, v_ref, cz_ref, gq_ref, gk_ref, t_ref, y_ref, o_ref, l_ref, qs, ks, vs):
        g = pl.program_id(2)
        lo = _lo_mask((QB, LANE))

        def group(gi):
            d = DIL[gi][1]
            qs[...] = _rms2(q_ref[...].astype(F32), gq_ref[gi:gi + 1, :])
            ks[...] = _rms2(k_ref[...].astype(F32), gk_ref[gi:gi + 1, :])
            vs[...] = v_ref[...].astype(F32)
            bias = _dil_bias(t_ref, gi, d)

            def block(cur, prev):
                rows = _dil_rows(cur, d)
                q2 = _stack_heads(qs[rows, :], lo).astype(BF16)
                kc, vc = ks[rows, :], vs[rows, :]
                if prev is None:
                    kcat, vcat, b = kc, vc, bias[:, QB:]
                else:
                    prow = _dil_rows(prev, d)
                    kcat = jnp.concatenate([ks[prow, :], kc], axis=0)
                    vcat = jnp.concatenate([vs[prow, :], vc], axis=0)
                    b = bias
                s = _dot_nt(q2, kcat.astype(BF16)) * DIL_SCALE + b
                m = jnp.max(s, axis=-1, keepdims=True)
                p = jnp.exp(s - m)
                l = jnp.sum(p, axis=-1, keepdims=True)
                o = _dot(p.astype(BF16), vcat.astype(BF16)) / l
                lse = m + jnp.log(l)
                o_ref[gi, rows, :] = jnp.where(lo, o[:QB], o[QB:])
                l_ref[gi, rows, :] = jnp.where(lo, lse[:QB], lse[QB:])

            _dil_walk(d, block, True)

        for gi in range(NG):
            pl.when(g == gi)(functools.partial(group, gi))

        @pl.when(g == NG - 1)
        def _():
            y_ref[...] = _mergec_math(o_ref[0], o_ref[1], o_ref[2], l_ref[0], l_ref[1], l_ref[2],
                                      cz_ref[...].astype(F32)).astype(BF16)

    def col(base):
        return pl.BlockSpec((None, S, LANE), lambda b, hp, g: (b, 0, base // LANE + 4 * g + hp))

    gspec = pl.BlockSpec((NG, LANE), lambda b, hp, g: (0, 0))
    saved = pl.BlockSpec((NG, None, S, LANE), lambda b, hp, g: (0, b, 0, hp))
    return pl.pallas_call(
        body,
        grid=(B, 4, NG),
        in_specs=[col(O_DQ), col(O_DK), col(O_DV),
                  pl.BlockSpec((None, S, LANE), lambda b, hp, g: (b, 0, O_CZ // LANE + hp)),
                  gspec, gspec, pl.BlockSpec((None, 8, LANE), lambda b, hp, g: (hp, 0, 0))],
        out_specs=[pl.BlockSpec((None, S, LANE), lambda b, hp, g: (b, 0, hp)), saved, saved],
        out_shape=[jax.ShapeDtypeStruct((B, S, DWID), BF16), jax.ShapeDtypeStruct((NG, B, S, DWID), F32),
                   jax.ShapeDtypeStruct((NG, B, S, DWID), F32)],
        scratch_shapes=[pltpu.VMEM((S, LANE), F32)] * 3,
        name="dil_mixer_fwd",
        compiler_params=pltpu.CompilerParams(dimension_semantics=("arbitrary",) * 3, vmem_limit_bytes=VMEM_LIMIT),
    )(proj3, proj3, proj3, proj3, gq, gk, tab)


MERGE_ROWS = 256


def _dilc_bwd(proj3, gq, gk, tab, o_all, l_all, d_yc, dproj3):
    B = proj3.shape[0]

    def body(q_ref, k_ref, v_ref, cz_ref, gq_ref, gk_ref, t_ref, o_ref, l_ref, dy_ref, dp_in,
             dp_out, dgq_out, dgk_out, qs, ks, vs, dos, dls, dqs, dks, dvs, rqs, rks, dczs,
             st_q, st_k, st_v, st_z, sems, sem_z):
        b_, hp, g = pl.program_id(0), pl.program_id(1), pl.program_id(2)
        col = lambda base: pl.ds(pl.multiple_of(base + hp * LANE, LANE), LANE)
        lo = _lo_mask((QB, LANE))

        @pl.when(jnp.logical_and(jnp.logical_and(pl.program_id(0) == 0, pl.program_id(1) == 0), g == 0))
        def _():
            dgq_out[...] = jnp.zeros((NG, LANE), F32)
            dgk_out[...] = jnp.zeros((NG, LANE), F32)

        @pl.when(g == 0)
        def _():
            def chunk(i, carry):
                rows = pl.ds(pl.multiple_of(i * MERGE_ROWS, MERGE_ROWS), MERGE_ROWS)
                ls = [l_ref[j, rows, :] for j in range(NG)]
                m = jnp.maximum(jnp.maximum(ls[0], ls[1]), ls[2])
                es = [jnp.exp(t - m) for t in ls]
                den = (es[0] + es[1]) + es[2]
                al = [e / den for e in es]
                os_ = [o_ref[j, rows, :] for j in range(NG)]
                oc = (al[0] * os_[0] + al[1] * os_[1]) + al[2] * os_[2]
                cz = cz_ref[rows, :].astype(F32)
                sg = jax.nn.sigmoid(cz)
                dy = dy_ref[rows, :]
                d_oc = dy * (cz * sg)
                dczs[rows, :] = (dy * oc * (sg * (1.0 + cz * (1.0 - sg)))).astype(BF16)
                ts = [_head_sum_1(d_oc * os_[j]) for j in range(NG)]
                tbar = (al[0] * ts[0] + al[1] * ts[1]) + al[2] * ts[2]
                for j in range(NG):
                    dos[j, rows, :] = al[j] * d_oc
                    dls[j, rows, :] = al[j] * (ts[j] - tbar)
                return carry
            lax.fori_loop(0, S // MERGE_ROWS, chunk, 0)
            _write_windows(dp_out, [st_z], sem_z, b_ * 4 + hp, B * 4, [(dczs[...], dp_out.at[b_, :, col(O_CZ)])])

        def group(gi):
            d = DIL[gi][1]
            xq, xk = q_ref[...].astype(F32), k_ref[...].astype(F32)
            rqs[...] = _rms2_scale(xq)
            rks[...] = _rms2_scale(xk)
            qs[...] = xq * rqs[...] * gq_ref[gi:gi + 1, :]
            ks[...] = xk * rks[...] * gk_ref[gi:gi + 1, :]
            vs[...] = v_ref[...].astype(F32)
            dks[...] = jnp.zeros((S, LANE), F32)
            dvs[...] = jnp.zeros((S, LANE), F32)
            bias = _dil_bias(t_ref, gi, d)

            def block(cur, prev):
                rows = _dil_rows(cur, d)
                q2 = _stack_heads(qs[rows, :], lo).astype(BF16)
                dob = dos[gi, rows, :]
                do2 = _stack_heads(dob, lo).astype(BF16)
                kc, vc = ks[rows, :], vs[rows, :]
                if prev is None:
                    kcat, vcat, b = kc, vc, bias[:, QB:]
                else:
                    prow = _dil_rows(prev, d)
                    kcat = jnp.concatenate([ks[prow, :], kc], axis=0)
                    vcat = jnp.concatenate([vs[prow, :], vc], axis=0)
                    b = bias
                kcat = kcat.astype(BF16)
                vcat = vcat.astype(BF16)
                lse_b = l_ref[gi, rows, :]
                corr_b = dls[gi, rows, :] - _head_sum_1(dob * o_ref[gi, rows, :])
                lse2 = jnp.concatenate([lse_b[:, 0:1], lse_b[:, HD:HD + 1]], axis=0)
                corr2 = jnp.concatenate([corr_b[:, 0:1], corr_b[:, HD:HD + 1]], axis=0)
                s = _dot_nt(q2, kcat) * DIL_SCALE + b
                p = jnp.exp(s - lse2)
                ds = (p * (_dot_nt(do2, vcat) + corr2) * DIL_SCALE).astype(BF16)
                dq2 = _dot(ds, kcat)
                dqs[rows, :] = jnp.where(lo, dq2[:QB], dq2[QB:])
                dk = _dot_tn(ds, q2)
                dv = _dot_tn(p.astype(BF16), do2)
                if prev is None:
                    dks[rows, :] += dk
                    dvs[rows, :] += dv
                else:
                    dks[prow, :] += dk[:QB]
                    dvs[prow, :] += dv[:QB]
                    dks[rows, :] += dk[QB:]
                    dvs[rows, :] += dv[QB:]

            _dil_walk(d, block, False)

            dxq, dgq = _rms2_bwd(q_ref[...].astype(F32), rqs[...], gq_ref[gi:gi + 1, :], dqs[...])
            dgq_out[gi:gi + 1, :] += dgq
            dxk, dgk = _rms2_bwd(k_ref[...].astype(F32), rks[...], gk_ref[gi:gi + 1, :], dks[...])
            dgk_out[gi:gi + 1, :] += dgk
            step, nsteps = _grid_step((B, 4, NG))
            _write_windows(dp_out, [st_q, st_k, st_v], sems, step, nsteps,
                           [(dxq, dp_out.at[b_, :, col(O_DQ + gi * DWID)]), (dxk, dp_out.at[b_, :, col(O_DK + gi * DWID)]),
                            (dvs[...], dp_out.at[b_, :, col(O_DV + gi * DWID)])])

        for gi in range(NG):
            pl.when(g == gi)(functools.partial(group, gi))

    def col(base):
        return pl.BlockSpec((None, S, LANE), lambda b, hp, g: (b, 0, base // LANE + 4 * g + hp))

    gspec = pl.BlockSpec((NG, LANE), lambda b, hp, g: (0, 0))
    saved = pl.BlockSpec((NG, None, S, LANE), lambda b, hp, g: (0, b, 0, hp))
    per_pair = pl.BlockSpec((None, S, LANE), lambda b, hp, g: (b, 0, hp))
    return pl.pallas_call(
        body,
        grid=(B, 4, NG),
        in_specs=[col(O_DQ), col(O_DK), col(O_DV),
                  pl.BlockSpec((None, S, LANE), lambda b, hp, g: (b, 0, O_CZ // LANE + hp)),
                  gspec, gspec, pl.BlockSpec((None, 8, LANE), lambda b, hp, g: (hp, 0, 0)),
                  saved, saved, per_pair, pl.BlockSpec(memory_space=pl.ANY)],
        out_specs=[pl.BlockSpec(memory_space=pl.ANY), gspec, gspec],
        out_shape=[jax.ShapeDtypeStruct(dproj3.shape, dproj3.dtype), jax.ShapeDtypeStruct((NG, LANE), F32),
                   jax.ShapeDtypeStruct((NG, LANE), F32)],
        input_output_aliases={10: 0},
        scratch_shapes=[pltpu.VMEM((S, LANE), F32)] * 3 + [pltpu.VMEM((NG, S, LANE), F32)] * 2
        + [pltpu.VMEM((S, LANE), F32)] * 5 + [pltpu.VMEM((S, LANE), BF16)] + [pltpu.VMEM((2, S, LANE), BF16)] * 4
        + [pltpu.SemaphoreType.DMA((3, 2)), pltpu.SemaphoreType.DMA((1, 2))],
        name="dil_mixer_bwd",
        compiler_params=pltpu.CompilerParams(dimension_semantics=("arbitrary",) * 3, vmem_limit_bytes=VMEM_LIMIT),
    )(proj3, proj3, proj3, proj3, gq, gk, tab, o_all, l_all, d_yc, dproj3)


def _dil_slopes():
    slopes = (2.0 ** (-8.0 * np.arange(1, NG * DH + 1, dtype=np.float32) / (NG * DH))).astype(np.float32).reshape(NG, DH)
    tab = np.zeros((4, 8, LANE), np.float32)
    for hp in range(4):
        for gi in range(NG):
            for e in (0, 1):
                tab[hp, 2 * gi + e, :] = slopes[gi, 2 * hp + e]
    return jnp.asarray(tab)


def _rope_tables():
    inv = ROPE_THETA ** (-jnp.arange(0, ROPE, 2, dtype=F32) / ROPE)
    ang = jnp.arange(S, dtype=F32)[:, None] * inv[None, :]
    cos, sin = jnp.cos(ang), jnp.sin(ang)
    z16 = jnp.zeros((S, 16), F32)
    c = jnp.concatenate([jnp.ones((S, NOPE), F32), cos, cos, jnp.zeros((S, 32), F32)], axis=1)
    s1 = jnp.concatenate([jnp.zeros((S, NOPE), F32), -sin, z16, jnp.zeros((S, 32), F32)], axis=1)
    s2 = jnp.concatenate([jnp.zeros((S, NOPE), F32), z16, sin, jnp.zeros((S, 32), F32)], axis=1)
    return c, s1, s2


def _pad_heads_uq(w):
    return jnp.pad(w.reshape(QL, NH, QK), ((0, 0), (0, 0), (0, QKP - QK))).reshape(QL, NH * QKP)


def _unpad_heads_uq(g):
    return g.reshape(QL, NH, QKP)[:, :, :QK].reshape(QL, NH * QK)


def _split_ukv(w):
    w3 = w.reshape(KVL, NH, NOPE + VD)
    uk = jnp.pad(w3[:, :, :NOPE], ((0, 0), (0, 0), (0, QKP - NOPE))).reshape(KVL, NH * QKP)
    return uk, w3[:, :, NOPE:].reshape(KVL, NH * VD)


def _join_ukv(guk, guv):
    return jnp.concatenate([guk.reshape(KVL, NH, QKP)[:, :, :NOPE], guv.reshape(KVL, NH, VD)],
                           axis=-1).reshape(KVL, NH * (NOPE + VD))


BR = 512
BRM = 256


def _layer_fwd(x, w, tabs, batch):
    T = batch * S
    rope_c, rope_s1, rope_s2, dil_tab = tabs
    res = {"x": x}
    row = lambda c: (lambda i: (i, c))
    fix = lambda i: (0, 0)

    h = _pcall("norm_fwd", _rms, (T // BR,),
               [(x, (BR, D), row(0)), (w["norm_g"], (1, D), fix)],
               [((T, D), BF16, (BR, D), row(0))])[0]
    proj = _mm("in_proj", h, w["w_in_t"], tb=True, out_dtype=BF16, tm=2048, tn=1024)
    res["h"], res["proj"] = h, proj
    proj3 = proj.reshape(batch, S, NINP)

    cblk = lambda s: (lambda j, b: (b, 0, 4 * s + j))
    y_a = _pcall("conv_fwd", _conv_math, (4, batch),
                 [(proj3, (None, S, LANE), cblk(0)), (proj3, (None, S, LANE), cblk(1)),
                  (proj3, (None, S, LANE), cblk(2)), (proj3, (None, S, LANE), cblk(3)),
                  (w["conv_w"], (3, LANE), lambda j, b: (0, j)), (w["conv_b"], (1, LANE), lambda j, b: (0, j))],
                 [((batch, S, CW), BF16, (None, S, LANE), lambda j, b: (b, 0, j))])[0].reshape(T, CW)
    res["y_a"] = y_a

    cqn, ckvn = _pcall("mla_pre_fwd", _mla_pre_math, (T // BR,),
                       [(proj, (BR, QL), row(O_CQ // QL)), (proj, (BR, KVL), row(O_CKV // KVL)),
                        (w["q_a_norm_g"], (1, QL), fix), (w["kv_a_norm_g"], (1, KVL), fix)],
                       [((T, QL), BF16, (BR, QL), row(0)), ((T, KVL), BF16, (BR, KVL), row(0))])
    w_uq_p = _pad_heads_uq(w["w_uq"])
    w_uk, w_uv = _split_ukv(w["w_ukv"])
    q = _mm("uq", cqn, w_uq_p, out_dtype=BF16)
    kn = _mm("uk", ckvn, w_uk, out_dtype=BF16)
    v = _mm("uv", ckvn, w_uv, out_dtype=BF16)
    nrr = S // BR
    tab_row = lambda i: (i % nrr, 0)
    qr, kr = _pcall("rope_fwd", _rope_math, (T // BR,),
                    [(q, (BR, NH * QKP), row(0)), (kn, (BR, NH * QKP), row(0)), (proj, (BR, LANE), row(O_KPE // LANE)),
                     (w["mla_q_norm_g"], (1, QKP), fix), (w["mla_k_norm_g"], (1, QKP), fix),
                     (rope_c, (BR, QKP), tab_row), (rope_s1, (BR, QKP), tab_row), (rope_s2, (BR, QKP), tab_row)],
                    [((T, NH * QKP), BF16, (BR, NH * QKP), row(0))] * 2)
    qr = qr.reshape(batch, S, NH * QKP)
    kr = kr.reshape(batch, S, NH * QKP)
    v = v.reshape(batch, S, NH * VD)
    o_b, l_b = _mla_fwd(qr, kr, v)
    ob2 = o_b.reshape(T, NH * VD)
    y_b = _pcall("gateb_fwd", _gate_math, (T // BR,),
                 [(ob2, (BR, 512), row(0)), (proj, (BR, 512), row(O_BZ // 512))],
                 [((T, 512), BF16, (BR, 512), row(0))])[0]
    res.update(cqn=cqn, ckvn=ckvn, q=q, kn=kn, qr=qr, kr=kr, v=v, o_b=o_b, l_b=l_b, ob2=ob2, y_b=y_b,
               w_uq_p=w_uq_p, w_uk=w_uk, w_uv=w_uv)

    gq2 = jnp.tile(w["dil_q_norm_g"].reshape(NG, HD), (1, 2))
    gk2 = jnp.tile(w["dil_k_norm_g"].reshape(NG, HD), (1, 2))
    y_c, o_all, l_all = _dilc_fwd(proj3, gq2, gk2, dil_tab)
    y_c = y_c.reshape(T, DWID)
    res.update(o_all=o_all, l_all=l_all, y_c=y_c)

    pa = _mm("out_a", y_a, w["w_out_a"], out_dtype=BF16)
    pb = _mm("out_b", y_b, w["w_out_b"], out_dtype=BF16)
    pc = _mm("out_c", y_c, w["w_out_c"], out_dtype=BF16)
    merged = _pcall("merge_fwd", _merge_math, (T // BRM,),
                    [(proj, (BRM, D), row(O_G // D + s)) for s in range(3)]
                    + [(w["b_gate"], (1, D), (lambda s: (lambda i: (0, s)))(s)) for s in range(3)]
                    + [(t, (BRM, D), row(0)) for t in (pa, pb, pc)],
                    [((T, D), BF16, (BRM, D), row(0))])[0]
    out = _mm("o_proj", merged, w["w_o"], add=x, tm=1024)
    res.update(pa=pa, pb=pb, pc=pc, merged=merged)
    return out, res


def _norm_bwd_math(x, g, dh, dy):
    _, pull = jax.vjp(_rms, x, g)
    dx, dg = pull(dh)
    return dx + dy, dg


def _layer_bwd(dy, w, res, tabs, batch, after_dw=None, after_merge=None):
    T = batch * S
    rope_c, rope_s1, rope_s2, dil_tab = tabs
    row = lambda c: (lambda i: (i, c))
    fix = lambda i: (0, 0)
    x, proj, h = res["x"], res["proj"], res["h"]
    proj3 = proj.reshape(batch, S, NINP)
    g = {}

    d_merged = _mm("o_proj_dx", dy, w["w_o"], tb=True)
    g["w_o"] = _mm("o_proj_dw", res["merged"], dy, ta=True, tm=1024, tk=2048)

    dproj = lax.empty((T, NINP), BF16)
    rows_of = lambda br: (lambda ref, i: ref.at[pl.ds(pl.multiple_of(i * br, br), br)])

    def merge_bwd(*args):
        dg0, dg1, dg2, db0, db1, db2, dpa, dpb, dpc = _vjp_of(_merge_math, 9)(*args, n_prim=9)
        return db0, db1, db2, dpa, dpb, dpc, jnp.concatenate([dg0, dg1, dg2], axis=1)

    db0, db1, db2, dpa, dpb, dpc, dproj = _pcall(
        "merge_bwd", merge_bwd, (T // BRM,),
        [(proj, (BRM, D), row(O_G // D + s)) for s in range(3)]
        + [(w["b_gate"], (1, D), (lambda s: (lambda i: (0, s)))(s)) for s in range(3)]
        + [(t, (BRM, D), row(0)) for t in (res["pa"], res["pb"], res["pc"])]
        + [(d_merged, (BRM, D), row(0))],
        [((1, D), F32, (1, D), fix, True)] * 3 + [((T, D), BF16, (BRM, D), row(0))] * 3,
        into=(dproj, [((BRM, 3 * D), lambda ref, i: rows_of(BRM)(ref, i).at[:, O_G:O_G + 3 * D])]))
    g["b_gate"] = jnp.concatenate([db0, db1, db2], axis=1)

    dep = after_merge(dpa) if after_merge is not None else None
    d_ya = _mm("out_a_dx", dpa, w["w_out_a"], tb=True, dep=dep)
    d_yb = _mm("out_b_dx", dpb, w["w_out_b"], tb=True)
    d_yc = _mm("out_c_dx", dpc, w["w_out_c"], tb=True)
    g["w_out_a"] = _mm("out_a_dw", res["y_a"], dpa, ta=True, tk=T)
    g["w_out_b"] = _mm("out_b_dw", res["y_b"], dpb, ta=True, tk=T)
    g["w_out_c"] = _mm("out_c_dw", res["y_c"], dpc, ta=True, tk=T)

    cblk = lambda s: (lambda j, b: (b, 0, 4 * s + j))
    oblk = lambda j, b: (b, 0, j)
    def conv_bwd(*args):
        d_ab, d_ac, d_ax, d_az, dcw, dcb = _vjp_of(_conv_math, 6)(*args, n_prim=6)
        return dcw, dcb, d_ab, d_ac, d_ax, d_az

    a_col = lambda s_: (lambda ref, j, b: ref.at[b, :, pl.ds(pl.multiple_of(O_A + s_ * CW + j * LANE, LANE), LANE)])
    g["conv_w"], g["conv_b"], dproj3 = _pcall(
        "conv_bwd", conv_bwd, (4, batch),
        [(proj3, (None, S, LANE), cblk(s)) for s in range(4)]
        + [(w["conv_w"], (3, LANE), lambda j, b: (0, j)), (w["conv_b"], (1, LANE), lambda j, b: (0, j)),
           (d_ya.reshape(batch, S, CW), (None, S, LANE), oblk)],
        [((3, CW), F32, (3, LANE), lambda j, b: (0, j), True), ((1, CW), F32, (1, LANE), lambda j, b: (0, j), True)],
        into=(dproj.reshape(batch, S, NINP), [((S, LANE), a_col(s_)) for s_ in range(4)]))
    dproj = dproj3.reshape(T, NINP)

    gate_bwd = functools.partial(_vjp_of(_gate_math, 2), n_prim=2)
    d_ob, dproj = _pcall("gateb_bwd", gate_bwd, (T // BR,),
                         [(res["ob2"], (BR, 512), row(0)), (proj, (BR, 512), row(O_BZ // 512)), (d_yb, (BR, 512), row(0))],
                         [((T, 512), F32, (BR, 512), row(0))],
                         into=(dproj, [((BR, 512), lambda ref, i: rows_of(BR)(ref, i).at[:, O_BZ:O_BZ + 512])]))
    dqr, dkr, dv = _mla_bwd(res["qr"], res["kr"], res["v"], d_ob.reshape(batch, S, NH * VD), res["o_b"], res["l_b"])
    nrr = S // BR
    tab_row = lambda i: (i % nrr, 0)
    def rope_bwd(*args):
        d_q, d_kn, d_kpe, dgq, dgk = _vjp_of(_rope_math, 5)(*args, n_prim=8)
        return d_q, d_kn, dgq, dgk, d_kpe

    d_q, d_kn, g["mla_q_norm_g"], g["mla_k_norm_g"], dproj = _pcall(
        "rope_bwd", rope_bwd, (T // BR,),
        [(res["q"], (BR, NH * QKP), row(0)), (res["kn"], (BR, NH * QKP), row(0)), (proj, (BR, LANE), row(O_KPE // LANE)),
         (w["mla_q_norm_g"], (1, QKP), fix), (w["mla_k_norm_g"], (1, QKP), fix),
         (rope_c, (BR, QKP), tab_row), (rope_s1, (BR, QKP), tab_row), (rope_s2, (BR, QKP), tab_row),
         (dqr.reshape(T, NH * QKP), (BR, NH * QKP), row(0)), (dkr.reshape(T, NH * QKP), (BR, NH * QKP), row(0))],
        [((T, NH * QKP), BF16, (BR, NH * QKP), row(0))] * 2 + [((1, QKP), F32, (1, QKP), fix, True)] * 2,
        into=(dproj, [((BR, LANE), lambda ref, i: rows_of(BR)(ref, i).at[:, O_KPE:O_KPE + LANE])]))
    dv = dv.reshape(T, NH * VD)
    d_cqn = _mm("uq_dx", d_q, res["w_uq_p"], tb=True)
    d_ckvn = _mm("uk_dx", d_kn, res["w_uk"], tb=True)
    d_ckvn = _mm("uv_dx", dv, res["w_uv"], tb=True, add=d_ckvn)
    g["w_uq"] = _unpad_heads_uq(_mm("uq_dw", res["cqn"], d_q, ta=True, tk=T))
    g["w_ukv"] = _join_ukv(_mm("uk_dw", res["ckvn"], d_kn, ta=True, tk=T),
                           _mm("uv_dw", res["ckvn"], dv, ta=True, tk=T))
    def pre_bwd(*args):
        d_cq, d_ckv, dgq, dgkv = _vjp_of(_mla_pre_math, 4)(*args, n_prim=4)
        return dgq, dgkv, jnp.concatenate([d_cq, d_ckv], axis=1)

    g["q_a_norm_g"], g["kv_a_norm_g"], dproj = _pcall(
        "mla_pre_bwd", pre_bwd, (T // BR,),
        [(proj, (BR, QL), row(O_CQ // QL)), (proj, (BR, KVL), row(O_CKV // KVL)),
         (w["q_a_norm_g"], (1, QL), fix), (w["kv_a_norm_g"], (1, KVL), fix),
         (d_cqn, (BR, QL), row(0)), (d_ckvn, (BR, KVL), row(0))],
        [((1, QL), F32, (1, QL), fix, True), ((1, KVL), F32, (1, KVL), fix, True)],
        into=(dproj, [((BR, QL + KVL), lambda ref, i: rows_of(BR)(ref, i).at[:, O_CQ:O_CQ + QL + KVL])]))

    gq2 = jnp.tile(w["dil_q_norm_g"].reshape(NG, HD), (1, 2))
    gk2 = jnp.tile(w["dil_k_norm_g"].reshape(NG, HD), (1, 2))
    dproj3, dgq, dgk = _dilc_bwd(proj3, gq2, gk2, dil_tab, res["o_all"], res["l_all"],
                                 d_yc.reshape(batch, S, DWID), dproj.reshape(batch, S, NINP))
    dproj = dproj3.reshape(T, NINP)
    g["dil_q_norm_g"] = dgq[:, :HD] + dgq[:, HD:]
    g["dil_k_norm_g"] = dgk[:, :HD] + dgk[:, HD:]

    g["w_in_t"] = _mm("in_proj_dw", dproj, h, ta=True, tm=1024, tk=T)
    dep = after_dw(g) if after_dw is not None else None
    d_h = _mm("in_proj_dx", dproj, w["w_in_t"], dep=dep, tm=1024, tk=NINP // 4)
    dx, g["norm_g"] = _pcall("norm_bwd", _norm_bwd_math, (T // BR,),
                             [(x, (BR, D), row(0)), (w["norm_g"], (1, D), fix), (d_h, (BR, D), row(0)),
                              (dy, (BR, D), row(0))],
                             [((T, D), F32, (BR, D), row(0)), ((1, D), F32, (1, D), fix, True)])
    return dx, g


def _loss_math(y, t):
    e = y - t
    return e * (1.0 / D), 0.5 * jnp.sum(jnp.sum(e * e, axis=-1, keepdims=True) / D, axis=0, keepdims=True)


def _local_step(x, target, ws, batch):
    T = batch * S
    tabs = _rope_tables() + (_dil_slopes(),)
    saved = []
    y = x
    for l in range(NL):
        y, res = _layer_fwd(y, ws[l], tabs, batch)
        saved.append(res)
    row = lambda i: (i, 0)
    dy, loss = _pcall("loss", _loss_math, (T // BR,),
                      [(y, (BR, D), row), (target, (BR, D), row)],
                      [((T, D), F32, (BR, D), row), ((1, 1), F32, (1, 1), lambda i: (0, 0), True)])
    grads = [None] * NL
    for l in reversed(range(NL)):
        dy, grads[l] = _layer_bwd(dy, ws[l], saved[l], tabs, batch)
    return loss, dy, grads


ANY = pl.BlockSpec(memory_space=pl.ANY)
U32 = jnp.uint32
WSH = NIN // 4
WA = KPE_END
WB = WSH - WA
CWD = 512
PACK_ROWS = 1472
HW = PACK_W // 2


def _me():
    return lax.axis_index("x"), lax.axis_index("y"), lax.axis_index("c")


def _piece_rows(k):
    a = k * WSH + jnp.where(k > 0, NINP - NIN, 0)
    b = k * WSH + WA + (NINP - NIN)
    return ((0, pl.multiple_of(a, 8), WA), (WA, pl.multiple_of(b, 8), WB))


def _pack_words(lo, hi):
    ul = lax.bitcast_convert_type(lo.astype(BF16).astype(F32), U32)
    uh = lax.bitcast_convert_type(hi.astype(BF16).astype(F32), U32)
    w = jnp.bitwise_or(jnp.bitwise_and(uh, jnp.uint32(0xFFFF0000)), jnp.right_shift(ul, jnp.uint32(16)))
    return lax.bitcast_convert_type(w, F32)


def _unpack_words(w):
    w = lax.bitcast_convert_type(w, U32)
    lo = lax.bitcast_convert_type(jnp.left_shift(w, jnp.uint32(16)), F32)
    hi = lax.bitcast_convert_type(jnp.bitwise_and(w, jnp.uint32(0xFFFF0000)), F32)
    return lo, hi


def _all_gather(wc, sp):
    def body(w_ref, s_ref, ow_ref, os_ref, send_sems, recv_sems):
        x, y, c = _me()
        k_me = 2 * x + y
        sib = (x, y, 1 - c)
        chips = [(1 - x, y), (x, 1 - y), (1 - x, 1 - y)]
        wcols = lambda cc: pl.ds(pl.multiple_of(cc * (CWD // 2), LANE), CWD // 2)
        scols = lambda cc: pl.ds(pl.multiple_of(cc * HW, LANE), HW)

        def windows(k, cc):
            pcs = _piece_rows(k)
            return ([(w_ref.at[pl.ds(l0, n), wcols(cc)], ow_ref.at[pl.ds(p0, n), wcols(cc)]) for l0, p0, n in pcs]
                    + [(s_ref.at[:, scols(cc)], os_ref.at[k, :, scols(cc)])])

        def copy(i, src, dst, to):
            return pltpu.make_async_remote_copy(src_ref=src, dst_ref=dst, send_sem=send_sems.at[i],
                                                recv_sem=recv_sems.at[i], device_id=to, device_id_type=MESH)

        def own_windows():
            return ([(w_ref.at[pl.ds(l0, n)], ow_ref.at[pl.ds(p0, n)]) for l0, p0, n in _piece_rows(k_me)]
                    + [(s_ref, os_ref.at[k_me])])

        first = [copy(18 + i, src, dst, sib) for i, (src, dst) in enumerate(own_windows())]
        for j, (cx, cy) in enumerate(chips):
            for i, (src, dst) in enumerate(windows(k_me, c)):
                first.append(copy(3 * j + i, src, dst, (cx, cy, c)))
        for cp in first:
            cp.start()
        passed = []
        for j, (cx, cy) in enumerate(chips):
            for i, (_, dst) in enumerate(windows(2 * cx + cy, c)):
                copy(3 * j + i, dst, dst, (cx, cy, c)).wait_recv()
                cp = copy(9 + 3 * j + i, dst, dst, sib)
                cp.start()
                passed.append(cp)
        for j, (cx, cy) in enumerate(chips):
            for i, (_, dst) in enumerate(windows(2 * cx + cy, 1 - c)):
                copy(9 + 3 * j + i, dst, dst, sib).wait_recv()
        for i, (_, dst) in enumerate(own_windows()):
            copy(18 + i, dst, dst, sib).wait_recv()
        for cp in first + passed:
            cp.wait_send()

    return pl.pallas_call(
        body,
        out_shape=[jax.ShapeDtypeStruct((NINP, CWD), F32), jax.ShapeDtypeStruct((4, PACK_ROWS, PACK_W), BF16)],
        in_specs=[ANY, ANY], out_specs=[ANY, ANY],
        scratch_shapes=[pltpu.SemaphoreType.DMA((21,)), pltpu.SemaphoreType.DMA((21,))],
        name="weights_all_gather",
    )(wc, sp)


HBM = pl.BlockSpec(memory_space=pltpu.HBM)
SEM = pl.BlockSpec(memory_space=pltpu.SEMAPHORE)
EFFECT = pltpu.SideEffectType.DATAFLOW_SIDE_EFFECTING


def _in_hbm(a):
    return pltpu.with_memory_space_constraint(a, pltpu.HBM)


def _ag_windows(w_ref, s_ref, lw_ref, ls_ref, k, cc):
    wcols = pl.ds(pl.multiple_of(cc * (CWD // 2), LANE), CWD // 2)
    scols = pl.ds(pl.multiple_of(cc * HW, LANE), HW)
    return ([(w_ref.at[pl.ds(l0, n), wcols], lw_ref.at[pl.ds(p0, n), wcols]) for l0, p0, n in _piece_rows(k)]
            + [(s_ref.at[:, scols], ls_ref.at[k, :, scols])])


def _ag_ici_copies(w_ref, s_ref, lw_ref, ls_ref, send_sems, recv_sems):
    x, y, c = _me()
    mine, theirs = [], []
    for j, (cx, cy) in enumerate([(1 - x, y), (x, 1 - y), (1 - x, 1 - y)]):
        for i, ((src, dst), (_, got)) in enumerate(zip(_ag_windows(w_ref, s_ref, lw_ref, ls_ref, 2 * x + y, c),
                                                       _ag_windows(w_ref, s_ref, lw_ref, ls_ref, 2 * cx + cy, c))):
            mk = lambda s_, d_: pltpu.make_async_remote_copy(
                src_ref=s_, dst_ref=d_, send_sem=send_sems.at[3 * j + i], recv_sem=recv_sems.at[3 * j + i],
                device_id=(cx, cy, c), device_id_type=MESH)
            mine.append(mk(src, dst))
            theirs.append(mk(got, got))
    return mine, theirs


def _ag_ici_start(wc, sp, dep):
    def body(w_ref, s_ref, lw_ref, ls_ref, dep_ref, send_sems, recv_sems, w_thru, s_thru, lw_thru, ls_thru, token):
        mine, _ = _ag_ici_copies(w_ref, s_ref, lw_ref, ls_ref, send_sems, recv_sems)
        for cp in mine:
            cp.start()
        token[...] = jnp.zeros_like(token)

    return pl.pallas_call(
        body, name="weights_gather_start",
        out_shape=(pltpu.SemaphoreType.DMA((9,)), pltpu.SemaphoreType.DMA((9,)), pltpu.HBM(wc.shape, wc.dtype),
                   pltpu.HBM(sp.shape, sp.dtype), pltpu.HBM((NINP, CWD), F32), pltpu.HBM((4, PACK_ROWS, PACK_W), BF16),
                   jax.ShapeDtypeStruct((8, LANE), F32)),
        in_specs=(HBM, HBM, HBM, HBM, ANY),
        out_specs=(SEM, SEM, HBM, HBM, HBM, HBM, pl.BlockSpec(memory_space=pltpu.VMEM)),
        input_output_aliases={0: 2, 1: 3, 2: 4, 3: 5},
        compiler_params=pltpu.CompilerParams(has_side_effects=EFFECT),
    )(_in_hbm(wc), _in_hbm(sp), _in_hbm(lax.empty((NINP, CWD), F32)), _in_hbm(lax.empty((4, PACK_ROWS, PACK_W), BF16)), dep)


def _ag_ici_wait(send_sems, recv_sems, wc, sp, lw, ls, after):
    def body(w_ref, s_ref, lw_ref, ls_ref, send_sems, recv_sems, after_ref, w_dead, s_dead, lw_out, ls_out):
        mine, theirs = _ag_ici_copies(w_ref, s_ref, lw_ref, ls_ref, send_sems, recv_sems)
        for cp in mine:
            cp.wait_send()
        for cp in theirs:
            cp.wait_recv()

    out = pl.pallas_call(
        body, name="weights_gather_wait",
        out_shape=(pltpu.HBM(wc.shape, wc.dtype), pltpu.HBM(sp.shape, sp.dtype), pltpu.HBM(lw.shape, lw.dtype),
                   pltpu.HBM(ls.shape, ls.dtype)),
        in_specs=(HBM, HBM, HBM, HBM, SEM, SEM, ANY), out_specs=(HBM, HBM, HBM, HBM),
        input_output_aliases={0: 0, 1: 1, 2: 2, 3: 3},
        compiler_params=pltpu.CompilerParams(has_side_effects=EFFECT),
    )(wc, sp, lw, ls, send_sems, recv_sems, after)
    return out[2], out[3]


def _ag_finish(wc, sp, lw, ls):
    def body(w_ref, s_ref, lw_ref, ls_ref, ow_ref, os_ref, send_sems, recv_sems):
        x, y, c = _me()
        k_me = 2 * x + y
        sib = (x, y, 1 - c)
        chips = [(1 - x, y), (x, 1 - y), (1 - x, 1 - y)]

        def copy(i, src, dst):
            return pltpu.make_async_remote_copy(src_ref=src, dst_ref=dst, send_sem=send_sems.at[i],
                                                recv_sem=recv_sems.at[i], device_id=sib, device_id_type=MESH)

        def own_windows():
            return ([(w_ref.at[pl.ds(l0, n)], ow_ref.at[pl.ds(p0, n)]) for l0, p0, n in _piece_rows(k_me)]
                    + [(s_ref, os_ref.at[k_me])])

        out = [copy(9 + i, src, dst) for i, (src, dst) in enumerate(own_windows())]
        for j, (cx, cy) in enumerate(chips):
            landed = _ag_windows(w_ref, s_ref, lw_ref, ls_ref, 2 * cx + cy, c)
            for i, (_, dst) in enumerate(_ag_windows(w_ref, s_ref, ow_ref, os_ref, 2 * cx + cy, c)):
                out.append(copy(3 * j + i, landed[i][1], dst))
        for cp in out:
            cp.start()
        for j, (cx, cy) in enumerate(chips):
            for i, (_, dst) in enumerate(_ag_windows(w_ref, s_ref, ow_ref, os_ref, 2 * cx + cy, 1 - c)):
                copy(3 * j + i, dst, dst).wait_recv()
        for i, (_, dst) in enumerate(own_windows()):
            copy(9 + i, dst, dst).wait_recv()
        for cp in out:
            cp.wait_send()

    return pl.pallas_call(
        body,
        out_shape=[jax.ShapeDtypeStruct(lw.shape, lw.dtype), jax.ShapeDtypeStruct(ls.shape, ls.dtype)],
        in_specs=[ANY] * 4, out_specs=[ANY, ANY],
        input_output_aliases={2: 0, 3: 1},
        scratch_shapes=[pltpu.SemaphoreType.DMA((12,)), pltpu.SemaphoreType.DMA((12,))],
        name="weights_gather_finish",
    )(wc, sp, lw, ls)


UNPACK_BR = 512


def _unpack_w_in(cont):
    def body(c_ref, o_ref):
        lo, hi = _unpack_words(c_ref[...])
        r = pl.program_id(0) * UNPACK_BR + lax.broadcasted_iota(jnp.int32, (UNPACK_BR, CWD), 0)
        pad = jnp.logical_and(r >= KPE_END, r < KPE_END + NINP - NIN)
        o_ref[:, 0:CWD] = jnp.where(pad, 0.0, lo).astype(BF16)
        o_ref[:, CWD:2 * CWD] = jnp.where(pad, 0.0, hi).astype(BF16)

    return pl.pallas_call(
        body, grid=(NINP // UNPACK_BR,),
        in_specs=[pl.BlockSpec((UNPACK_BR, CWD), lambda i: (i, 0))],
        out_specs=pl.BlockSpec((UNPACK_BR, D), lambda i: (i, 0)),
        out_shape=jax.ShapeDtypeStruct((NINP, D), BF16),
        name="w_in_unpack",
        compiler_params=pltpu.CompilerParams(dimension_semantics=("arbitrary",), vmem_limit_bytes=VMEM_LIMIT),
    )(cont)


def _rs_swap(gw, gs):
    def body(w_ref, s_ref, rw_ref, rs_ref, send_sems, recv_sems):
        x, y, c = _me()
        oc = 1 - c
        cps = [pltpu.make_async_remote_copy(src_ref=w_ref.at[:, pl.ds(pl.multiple_of(oc * (D // 2), LANE), D // 2)],
                                            dst_ref=rw_ref, send_sem=send_sems.at[0], recv_sem=recv_sems.at[0],
                                            device_id=(x, y, oc), device_id_type=MESH),
               pltpu.make_async_remote_copy(src_ref=s_ref.at[:, :, pl.ds(pl.multiple_of(oc * HW, LANE), HW)],
                                            dst_ref=rs_ref, send_sem=send_sems.at[1], recv_sem=recv_sems.at[1],
                                            device_id=(x, y, oc), device_id_type=MESH)]
        for cp in cps:
            cp.start()
        for cp in cps:
            cp.wait()

    return pl.pallas_call(
        body,
        out_shape=[jax.ShapeDtypeStruct((NINP, D // 2), F32), jax.ShapeDtypeStruct((4, PACK_ROWS, HW), F32)],
        in_specs=[ANY, ANY], out_specs=[ANY, ANY],
        scratch_shapes=[pltpu.SemaphoreType.DMA((2,)), pltpu.SemaphoreType.DMA((2,))],
        name="grads_sibling_swap",
    )(gw, gs)


def _rs_swap_copies(w_ref, s_ref, rw_ref, rs_ref, send_sems, recv_sems):
    x, y, c = _me()
    oc = 1 - c
    return [pltpu.make_async_remote_copy(src_ref=w_ref.at[:, pl.ds(pl.multiple_of(oc * (D // 2), LANE), D // 2)],
                                         dst_ref=rw_ref, send_sem=send_sems.at[0], recv_sem=recv_sems.at[0],
                                         device_id=(x, y, oc), device_id_type=MESH),
            pltpu.make_async_remote_copy(src_ref=s_ref.at[:, :, pl.ds(pl.multiple_of(oc * HW, LANE), HW)],
                                         dst_ref=rs_ref, send_sem=send_sems.at[1], recv_sem=recv_sems.at[1],
                                         device_id=(x, y, oc), device_id_type=MESH)]


def _rs_swap_start(gw, gs):
    def body(w_ref, s_ref, rw_ref, rs_ref, send_sems, recv_sems, w_thru, s_thru, rw_thru, rs_thru, token):
        for cp in _rs_swap_copies(w_ref, s_ref, rw_ref, rs_ref, send_sems, recv_sems):
            cp.start()
        token[...] = jnp.zeros_like(token)

    return pl.pallas_call(
        body, name="grads_swap_start",
        out_shape=(pltpu.SemaphoreType.DMA((2,)), pltpu.SemaphoreType.DMA((2,)), pltpu.HBM(gw.shape, gw.dtype),
                   pltpu.HBM(gs.shape, gs.dtype), pltpu.HBM((NINP, D // 2), F32), pltpu.HBM((4, PACK_ROWS, HW), F32),
                   jax.ShapeDtypeStruct((8, LANE), F32)),
        in_specs=(HBM, HBM, HBM, HBM),
        out_specs=(SEM, SEM, HBM, HBM, HBM, HBM, pl.BlockSpec(memory_space=pltpu.VMEM)),
        input_output_aliases={0: 2, 1: 3, 2: 4, 3: 5},
        compiler_params=pltpu.CompilerParams(has_side_effects=EFFECT),
    )(_in_hbm(gw), _in_hbm(gs), _in_hbm(lax.empty((NINP, D // 2), F32)), _in_hbm(lax.empty((4, PACK_ROWS, HW), F32)))


def _rs_swap_wait(send_sems, recv_sems, gw, gs, rw, rs, after):
    def body(w_ref, s_ref, rw_ref, rs_ref, send_sems, recv_sems, after_ref, w_out, s_out, rw_out, rs_out):
        for cp in _rs_swap_copies(w_ref, s_ref, rw_ref, rs_ref, send_sems, recv_sems):
            cp.wait()

    return pl.pallas_call(
        body, name="grads_swap_wait",
        out_shape=(pltpu.HBM(gw.shape, gw.dtype), pltpu.HBM(gs.shape, gs.dtype), pltpu.HBM(rw.shape, rw.dtype),
                   pltpu.HBM(rs.shape, rs.dtype)),
        in_specs=(HBM, HBM, HBM, HBM, SEM, SEM, ANY), out_specs=(HBM, HBM, HBM, HBM),
        input_output_aliases={0: 0, 1: 1, 2: 2, 3: 3},
        compiler_params=pltpu.CompilerParams(has_side_effects=EFFECT),
    )(gw, gs, rw, rs, send_sems, recv_sems, after)


SUM_BR = 512


def _rs_chip_sum_w(gw, rw, cidx):
    def body(c_ref, g_ref, r_ref, o_ref):
        s = g_ref[...] + r_ref[...]
        q = D // 8
        o_ref[...] = jnp.concatenate([_pack_words(s[:, 0:q], s[:, q:2 * q]),
                                      _pack_words(s[:, 2 * q:3 * q], s[:, 3 * q:4 * q])], axis=1)

    return pl.pallas_call(
        body,
        grid_spec=pltpu.PrefetchScalarGridSpec(
            num_scalar_prefetch=1, grid=(NINP // SUM_BR,),
            in_specs=[pl.BlockSpec((SUM_BR, D // 2), lambda i, cr: (i, cr[0])),
                      pl.BlockSpec((SUM_BR, D // 2), lambda i, cr: (i, 0))],
            out_specs=pl.BlockSpec((SUM_BR, D // 4), lambda i, cr: (i, 0))),
        out_shape=jax.ShapeDtypeStruct((NINP, D // 4), F32),
        name="grads_chip_sum_w",
        compiler_params=pltpu.CompilerParams(dimension_semantics=("arbitrary",), vmem_limit_bytes=VMEM_LIMIT),
    )(cidx, gw, rw)


def _rs_chip_sum_s(gs, rs, cidx):
    def body(c_ref, g_ref, r_ref, o_ref):
        o_ref[...] = (g_ref[...] + r_ref[...]).astype(BF16)

    return pl.pallas_call(
        body,
        grid_spec=pltpu.PrefetchScalarGridSpec(
            num_scalar_prefetch=1, grid=(4,),
            in_specs=[pl.BlockSpec((None, PACK_ROWS, HW), lambda j, cr: (j, 0, cr[0])),
                      pl.BlockSpec((None, PACK_ROWS, HW), lambda j, cr: (j, 0, 0))],
            out_specs=pl.BlockSpec((None, PACK_ROWS, HW), lambda j, cr: (j, 0, 0))),
        out_shape=jax.ShapeDtypeStruct((4, PACK_ROWS, HW), BF16),
        name="grads_chip_sum_s",
        compiler_params=pltpu.CompilerParams(dimension_semantics=("arbitrary",), vmem_limit_bytes=VMEM_LIMIT),
    )(cidx, gs, rs)


def _rs_exchange_copies(sw_ref, ss_ref, r2w_ref, r2s_ref, send_sems, recv_sems):
    x, y, c = _me()
    mine, theirs = [], []
    for j, (cx, cy) in enumerate([(1 - x, y), (x, 1 - y), (1 - x, 1 - y)]):
        def mk(i, src, dst):
            return pltpu.make_async_remote_copy(src_ref=src, dst_ref=dst, send_sem=send_sems.at[3 * j + i],
                                                recv_sem=recv_sems.at[3 * j + i], device_id=(cx, cy, c), device_id_type=MESH)
        for i, (l0, p0, n) in enumerate(_piece_rows(2 * cx + cy)):
            mine.append(mk(i, sw_ref.at[pl.ds(p0, n)], r2w_ref.at[j, pl.ds(l0, n)]))
            theirs.append(mk(i, r2w_ref.at[j, pl.ds(l0, n)], r2w_ref.at[j, pl.ds(l0, n)]))
        mine.append(mk(2, ss_ref.at[2 * cx + cy], r2s_ref.at[j]))
        theirs.append(mk(2, r2s_ref.at[j], r2s_ref.at[j]))
    return mine, theirs


def _rs_exchange_start(sw, ss, tag):
    def body(sw_ref, ss_ref, r2w_ref, r2s_ref, send_sems, recv_sems, sw_thru, ss_thru, r2w_thru, r2s_thru, token):
        mine, _ = _rs_exchange_copies(sw_ref, ss_ref, r2w_ref, r2s_ref, send_sems, recv_sems)
        for cp in mine:
            cp.start()
        token[...] = jnp.zeros_like(token)

    return pl.pallas_call(
        body, name="grads_exchange_start_" + tag,
        out_shape=(pltpu.SemaphoreType.DMA((9,)), pltpu.SemaphoreType.DMA((9,)), pltpu.HBM(sw.shape, sw.dtype),
                   pltpu.HBM(ss.shape, ss.dtype), pltpu.HBM((3, WSH, D // 4), F32), pltpu.HBM((3, PACK_ROWS, HW), BF16),
                   jax.ShapeDtypeStruct((8, LANE), F32)),
        in_specs=(HBM, HBM, HBM, HBM),
        out_specs=(SEM, SEM, HBM, HBM, HBM, HBM, pl.BlockSpec(memory_space=pltpu.VMEM)),
        input_output_aliases={0: 2, 1: 3, 2: 4, 3: 5},
        compiler_params=pltpu.CompilerParams(has_side_effects=EFFECT),
    )(_in_hbm(sw), _in_hbm(ss), _in_hbm(lax.empty((3, WSH, D // 4), F32)), _in_hbm(lax.empty((3, PACK_ROWS, HW), BF16)))


def _rs_exchange_wait(send_sems, recv_sems, sw, ss, r2w, r2s, after, tag):
    def body(sw_ref, ss_ref, r2w_ref, r2s_ref, send_sems, recv_sems, after_ref, sw_dead, ss_dead, r2w_out, r2s_out):
        mine, theirs = _rs_exchange_copies(sw_ref, ss_ref, r2w_ref, r2s_ref, send_sems, recv_sems)
        for cp in mine:
            cp.wait_send()
        for cp in theirs:
            cp.wait_recv()

    out = pl.pallas_call(
        body, name="grads_exchange_wait_" + tag,
        out_shape=(pltpu.HBM(sw.shape, sw.dtype), pltpu.HBM(ss.shape, ss.dtype), pltpu.HBM(r2w.shape, r2w.dtype),
                   pltpu.HBM(r2s.shape, r2s.dtype)),
        in_specs=(HBM, HBM, HBM, HBM, SEM, SEM, ANY), out_specs=(HBM, HBM, HBM, HBM),
        input_output_aliases={0: 0, 1: 1, 2: 2, 3: 3},
        compiler_params=pltpu.CompilerParams(has_side_effects=EFFECT),
    )(sw, ss, r2w, r2s, send_sems, recv_sems, after)
    return out[2], out[3]


def _rs_final_w(gw, rw, r2w, idx):
    q = D // 8

    def body(i_ref, g_ref, r_ref, p_ref, o_ref, gbuf, rbuf, sems):
        i = pl.program_id(0)
        k, c = i_ref[0], i_ref[1]
        cps = []
        for n_, (l0, p0, n) in enumerate(_piece_rows(k)):
            gcol = pl.ds(pl.multiple_of(c * (D // 2) + i * 2 * q, LANE), 2 * q)
            rcol = pl.ds(pl.multiple_of(i * 2 * q, LANE), 2 * q)
            cps.append(pltpu.make_async_copy(g_ref.at[pl.ds(p0, n), gcol], gbuf.at[pl.ds(l0, n)], sems.at[2 * n_]))
            cps.append(pltpu.make_async_copy(r_ref.at[pl.ds(p0, n), rcol], rbuf.at[pl.ds(l0, n)], sems.at[2 * n_ + 1]))
        for cp in cps:
            cp.start()
        for cp in cps:
            cp.wait()
        acc = gbuf[...] + rbuf[...]
        for j in range(3):
            lo, hi = _unpack_words(p_ref[j])
            acc = acc + jnp.concatenate([lo, hi], axis=1)
        o_ref[...] = acc

    return pl.pallas_call(
        body,
        grid_spec=pltpu.PrefetchScalarGridSpec(
            num_scalar_prefetch=1, grid=(2,),
            in_specs=[ANY, ANY, pl.BlockSpec((3, WSH, q), lambda i, ir: (0, 0, i))],
            out_specs=pl.BlockSpec((WSH, 2 * q), lambda i, ir: (0, i)),
            scratch_shapes=[pltpu.VMEM((WSH, 2 * q), F32), pltpu.VMEM((WSH, 2 * q), F32), pltpu.SemaphoreType.DMA((4,))]),
        out_shape=jax.ShapeDtypeStruct((WSH, D // 2), F32),
        name="grads_final_sum_w",
        compiler_params=pltpu.CompilerParams(dimension_semantics=("arbitrary",), vmem_limit_bytes=VMEM_LIMIT),
    )(idx, gw, rw, r2w)


def _rs_final_s(gs, rs, r2s, idx):
    def body(i_ref, g_ref, r_ref, p_ref, o_ref):
        acc = g_ref[...] + r_ref[...]
        for j in range(3):
            acc = acc + p_ref[j].astype(F32)
        o_ref[...] = acc

    return pl.pallas_call(
        body,
        grid_spec=pltpu.PrefetchScalarGridSpec(
            num_scalar_prefetch=1, grid=(1,),
            in_specs=[pl.BlockSpec((None, PACK_ROWS, HW), lambda i, ir: (ir[0], 0, ir[1])),
                      pl.BlockSpec((None, PACK_ROWS, HW), lambda i, ir: (ir[0], 0, 0)),
                      pl.BlockSpec((3, PACK_ROWS, HW), lambda i, ir: (0, 0, 0))],
            out_specs=pl.BlockSpec((PACK_ROWS, HW), lambda i, ir: (0, 0))),
        out_shape=jax.ShapeDtypeStruct((PACK_ROWS, HW), F32),
        name="grads_final_sum_s",
        compiler_params=pltpu.CompilerParams(dimension_semantics=("arbitrary",), vmem_limit_bytes=VMEM_LIMIT),
    )(idx, gs, rs, r2s)


def _rs_share(fw, fs):
    def body(w_ref, s_ref, ow_ref, os_ref, send_sems, recv_sems):
        x, y, c = _me()
        cps = [pltpu.make_async_remote_copy(src_ref=w_ref, dst_ref=ow_ref, send_sem=send_sems.at[0],
                                            recv_sem=recv_sems.at[0], device_id=(x, y, 1 - c), device_id_type=MESH),
               pltpu.make_async_remote_copy(src_ref=s_ref, dst_ref=os_ref, send_sem=send_sems.at[1],
                                            recv_sem=recv_sems.at[1], device_id=(x, y, 1 - c), device_id_type=MESH)]
        for cp in cps:
            cp.start()
        for cp in cps:
            cp.wait()

    return pl.pallas_call(
        body,
        out_shape=[jax.ShapeDtypeStruct((WSH, D // 2), F32), jax.ShapeDtypeStruct((PACK_ROWS, HW), F32)],
        in_specs=[ANY, ANY], out_specs=[ANY, ANY],
        scratch_shapes=[pltpu.SemaphoreType.DMA((2,)), pltpu.SemaphoreType.DMA((2,))],
        name="grads_share",
    )(fw, fs)


def _both_halves(mine, other, c):
    return jnp.where(c == 0, jnp.concatenate([mine, other], axis=1), jnp.concatenate([other, mine], axis=1))


def _rs_sums(gw, gs, rw, rs):
    x, y, c = _me()
    cidx = jnp.reshape(c, (1,)).astype(jnp.int32)
    return dict(gw=gw, gs=gs, rw=rw, rs=rs, sw=_rs_chip_sum_w(gw, rw, cidx), ss=_rs_chip_sum_s(gs, rs, cidx))


def _rs_begin(gw, gs):
    return _rs_sums(gw, gs, *_rs_swap(gw, gs))


def _rs_end(st, r2w, r2s):
    x, y, c = _me()
    idx = jnp.stack([2 * x + y, c]).astype(jnp.int32)
    fw = _rs_final_w(st["gw"], st["rw"], r2w, idx)
    fs = _rs_final_s(st["gs"], st["rs"], r2s, idx)
    ow, os_ = _rs_share(fw, fs)
    return _both_halves(fw, ow, c), _both_halves(fs, os_, c)


def _all_reduce_small(gs):
    rows = gs.shape[0]

    def body(g_ref, o_ref, buf, send_sems, recv_sems):
        x, y, c = _me()
        me = 4 * x + 2 * y + c
        buf[me] = g_ref[...]
        cps = []
        for r in range(1, 8):
            fx, fy, fc = (r >> 2) & 1, (r >> 1) & 1, r & 1
            px, py, pc = jnp.bitwise_xor(x, fx), jnp.bitwise_xor(y, fy), jnp.bitwise_xor(c, fc)
            cps.append((pltpu.make_async_remote_copy(
                src_ref=g_ref, dst_ref=buf.at[me], send_sem=send_sems.at[r - 1], recv_sem=recv_sems.at[r - 1],
                device_id=(px, py, pc), device_id_type=MESH), 4 * px + 2 * py + pc))
        for cp, _ in cps:
            cp.start()
        for r, (cp, peer) in enumerate(cps):
            pltpu.make_async_remote_copy(
                src_ref=g_ref, dst_ref=buf.at[peer], send_sem=send_sems.at[r], recv_sem=recv_sems.at[r],
                device_id=(x, y, c), device_id_type=MESH).wait_recv()
        for cp, _ in cps:
            cp.wait_send()
        acc = buf[0]
        for k in range(1, 8):
            acc = acc + buf[k]
        o_ref[...] = acc

    return pl.pallas_call(
        body,
        out_shape=jax.ShapeDtypeStruct((rows, LANE), F32),
        in_specs=[pl.BlockSpec(memory_space=pltpu.VMEM)],
        out_specs=pl.BlockSpec(memory_space=pltpu.VMEM),
        scratch_shapes=[pltpu.VMEM((8, rows, LANE), F32), pltpu.SemaphoreType.DMA((7,)), pltpu.SemaphoreType.DMA((7,))],
        name="small_grads_all_reduce",
    )(gs)


PACK_SPLIT = (("w_uq", 96, (QL, 192)), ("w_ukv", 64, (KVL, 256)),
              ("w_out_a", 256, (CW, 256)), ("w_out_b", 256, (CW, 256)), ("w_out_c", 256, (CW, 256)),
              ("w_o", 512, (256, D)))
MAT_ROWS = 1440
CONV_SHARD = 3 * 128


def _w_in_words(w_in_shard):
    t = w_in_shard.T
    return _pack_words(t[:, :CWD], t[:, CWD:])


def _pack_weights(wl):
    parts = [wl[n].astype(BF16).reshape(-1, PACK_W) for n, _, _ in PACK_SPLIT]
    cw = wl["conv_w"].reshape(-1)
    hi = cw.astype(BF16)
    r1 = cw - hi.astype(F32)
    mid = r1.astype(BF16)
    lo = (r1 - mid.astype(F32)).astype(BF16)
    cterms = jnp.pad(jnp.concatenate([hi, mid, lo]), (0, 3 * PACK_W - 3 * CONV_SHARD)).reshape(3, PACK_W)
    tail = jnp.pad(cterms, ((0, PACK_ROWS - MAT_ROWS - 3), (0, 0)))
    return jnp.concatenate(parts + [tail], axis=0)


def _unpack_weights(gath):
    out = {}
    r = 0
    for n, nrows, shp in PACK_SPLIT:
        t = gath[:, r:r + nrows].reshape((4,) + shp)
        r += nrows
        if n == "w_o":
            out[n] = t.reshape(4 * shp[0], shp[1])
        else:
            out[n] = t.transpose(1, 0, 2).reshape(shp[0], 4 * shp[1])
    ct = gath[:, r:r + 3].reshape(4, 3 * PACK_W)[:, :3 * CONV_SHARD].astype(F32).reshape(4, 3, CONV_SHARD)
    cw = (ct[:, 0] + ct[:, 1]) + ct[:, 2]
    out["conv_w"] = cw.reshape(4, 3, 128).transpose(1, 0, 2).reshape(3, CW)
    return out


def _pack_grads(g):
    parts = []
    for n, nrows, shp in PACK_SPLIT:
        t = g[n]
        if n == "w_o":
            t = t.reshape((4,) + shp)
        else:
            t = t.reshape(shp[0], 4, shp[1]).transpose(1, 0, 2)
        parts.append(t.reshape(4, nrows, PACK_W))
    cw = g["conv_w"].reshape(3, 4, 128).transpose(1, 0, 2).reshape(4, 1, CONV_SHARD)
    parts.append(jnp.pad(cw, ((0, 0), (0, PACK_ROWS - MAT_ROWS - 1), (0, PACK_W - CONV_SHARD))))
    return jnp.concatenate(parts, axis=1)


def _unpack_grads(red):
    out = {}
    r = 0
    for n, nrows, shp in PACK_SPLIT:
        out[n] = red[r:r + nrows].reshape(shp)
        r += nrows
    out["conv_w"] = red[r, :CONV_SHARD].reshape(3, 128)
    return out


SMALL_SIZES = (("norm_g", D), ("b_gate", 3 * D), ("conv_b", CW), ("q_a_norm_g", QL), ("kv_a_norm_g", KVL),
               ("mla_q_norm_g", QK), ("mla_k_norm_g", QK), ("dil_q_norm_g", NG * HD), ("dil_k_norm_g", NG * HD))
SMALL_ROWS = 88


def _pack_small(per_name):
    flat = jnp.concatenate([per_name[n].reshape(-1).astype(F32) for n, _ in SMALL_SIZES])
    return jnp.pad(flat, (0, SMALL_ROWS * LANE - flat.shape[0])).reshape(SMALL_ROWS, LANE)


def _unpack_small(packed, like):
    out = {}
    flat = packed.reshape(-1)
    r = 0
    for n, sz in SMALL_SIZES:
        out[n] = flat[r:r + NL * sz].reshape(like[n].shape)
        r += NL * sz
    return out


def _adamw_math(w, g, m, v):
    m = ADAM_B1 * m + (1.0 - ADAM_B1) * g
    v = ADAM_B2 * v + (1.0 - ADAM_B2) * jnp.square(g)
    m_hat = m / (1.0 - ADAM_B1 ** ADAM_STEP)
    v_hat = v / (1.0 - ADAM_B2 ** ADAM_STEP)
    delta = -ADAM_LR * (m_hat / (jnp.sqrt(v_hat) + ADAM_EPS) + ADAM_WD * w)
    return delta, m, v


def _adamw(name, w, g, m, v, br, bc=None):
    L, R, C = w.shape
    bc = C if bc is None else bc
    blk = lambda l, i, j: (l, i, j)
    return _pcall(name, _adamw_math, (L, R // br, C // bc), [(t, (None, br, bc), blk) for t in (w, g, m, v)],
                  [((L, R, C), F32, (None, br, bc), blk)] * 3)


ADAM_ROWS = {"w_uq": 256, "w_ukv": 128, "w_out_a": 512, "w_out_b": 512, "w_out_c": 512, "w_o": 256,
             "conv_w": 3}


def kernel(x, norm_g, w_in, b_gate, conv_w, conv_b, q_a_norm_g, w_uq, kv_a_norm_g, w_ukv, mla_q_norm_g, mla_k_norm_g, dil_q_norm_g, dil_k_norm_g, w_out_a, w_out_b, w_out_c, w_o, loss_target, m_norm_g, m_w_in, m_b_gate, m_conv_w, m_conv_b, m_q_a_norm_g, m_w_uq, m_kv_a_norm_g, m_w_ukv, m_mla_q_norm_g, m_mla_k_norm_g, m_dil_q_norm_g, m_dil_k_norm_g, m_w_out_a, m_w_out_b, m_w_out_c, m_w_o, v_norm_g, v_w_in, v_b_gate, v_conv_w, v_conv_b, v_q_a_norm_g, v_w_uq, v_kv_a_norm_g, v_w_ukv, v_mla_q_norm_g, v_mla_k_norm_g, v_dil_q_norm_g, v_dil_k_norm_g, v_w_out_a, v_w_out_b, v_w_out_c, v_w_o):
    W = dict(norm_g=norm_g, w_in=w_in, b_gate=b_gate, conv_w=conv_w, conv_b=conv_b, q_a_norm_g=q_a_norm_g, w_uq=w_uq,
             kv_a_norm_g=kv_a_norm_g, w_ukv=w_ukv, mla_q_norm_g=mla_q_norm_g, mla_k_norm_g=mla_k_norm_g,
             dil_q_norm_g=dil_q_norm_g, dil_k_norm_g=dil_k_norm_g, w_out_a=w_out_a, w_out_b=w_out_b, w_out_c=w_out_c,
             w_o=w_o)
    M = dict(norm_g=m_norm_g, w_in=m_w_in, b_gate=m_b_gate, conv_w=m_conv_w, conv_b=m_conv_b, q_a_norm_g=m_q_a_norm_g,
             w_uq=m_w_uq, kv_a_norm_g=m_kv_a_norm_g, w_ukv=m_w_ukv, mla_q_norm_g=m_mla_q_norm_g,
             mla_k_norm_g=m_mla_k_norm_g, dil_q_norm_g=m_dil_q_norm_g, dil_k_norm_g=m_dil_k_norm_g, w_out_a=m_w_out_a,
             w_out_b=m_w_out_b, w_out_c=m_w_out_c, w_o=m_w_o)
    V = dict(norm_g=v_norm_g, w_in=v_w_in, b_gate=v_b_gate, conv_w=v_conv_w, conv_b=v_conv_b, q_a_norm_g=v_q_a_norm_g,
             w_uq=v_w_uq, kv_a_norm_g=v_kv_a_norm_g, w_ukv=v_w_ukv, mla_q_norm_g=v_mla_q_norm_g,
             mla_k_norm_g=v_mla_k_norm_g, dil_q_norm_g=v_dil_q_norm_g, dil_k_norm_g=v_dil_k_norm_g, w_out_a=v_w_out_a,
             w_out_b=v_w_out_b, w_out_c=v_w_out_c, w_o=v_w_o)
    batch = x.shape[0]
    T = batch * S

    def layer_weights(l, cont, gath):
        full = _unpack_weights(gath)
        pad_qk = lambda t: jnp.pad(t, (0, QKP - QK)).reshape(1, QKP)
        full.update(
            w_in_t=_unpack_w_in(cont),
            norm_g=norm_g[l].reshape(1, D), b_gate=b_gate[l].reshape(1, 3 * D), conv_b=conv_b[l].reshape(1, CW),
            q_a_norm_g=q_a_norm_g[l].reshape(1, QL), kv_a_norm_g=kv_a_norm_g[l].reshape(1, KVL),
            mla_q_norm_g=pad_qk(mla_q_norm_g[l]), mla_k_norm_g=pad_qk(mla_k_norm_g[l]),
            dil_q_norm_g=dil_q_norm_g[l].reshape(NG, 1, HD), dil_k_norm_g=dil_k_norm_g[l].reshape(NG, 1, HD))
        return full

    words = [_w_in_words(w_in[l]) for l in range(NL)]
    packs = [_pack_weights({n: W[n][l] for n in BIG[1:] + ("conv_w",)}) for l in range(NL)]
    tabs = _rope_tables() + (_dil_slopes(),)
    x2 = x.reshape(T, D)

    cont0, gath0 = _all_gather(words[0], packs[0])
    w0 = layer_weights(0, cont0, gath0)
    ag = _ag_ici_start(words[1], packs[1], gath0)
    w0["norm_g"] = w0["norm_g"] + ag[6][0:1, 0:1]
    y0, res0 = _layer_fwd(x2, w0, tabs, batch)
    lw, ls = _ag_ici_wait(ag[0], ag[1], ag[2], ag[3], ag[4], ag[5], y0)
    w1 = layer_weights(1, *_ag_finish(words[1], packs[1], lw, ls))
    y1, res1 = _layer_fwd(y0, w1, tabs, batch)

    row = lambda i: (i, 0)
    dy, loss = _pcall("loss", _loss_math, (T // BR,),
                      [(y1, (BR, D), row), (loss_target.reshape(T, D), (BR, D), row)],
                      [((T, D), F32, (BR, D), row), ((1, 1), F32, (1, 1), lambda i: (0, 0), True)])
    loss = lax.psum(loss[0, 0], ("x", "y", "c"))

    grads = [None] * NL
    dy, grads[1] = _layer_bwd(dy, w1, res1, tabs, batch)
    st = [None] * NL
    ex = [None] * NL
    sw1 = _rs_swap_start(grads[1]["w_in_t"], _pack_grads(grads[1]))
    w0["w_o"] = w0["w_o"] + sw1[6][0:1, 0:1].astype(BF16)

    def exchange_layer1(t):
        st[1] = _rs_sums(*_rs_swap_wait(*sw1[:6], t))
        ex[1] = _rs_exchange_start(st[1]["sw"], st[1]["ss"], "1")
        return ex[1][6]

    def start_layer0(g):
        st[0] = _rs_begin(g["w_in_t"], _pack_grads(g))
        ex[0] = _rs_exchange_start(st[0]["sw"], st[0]["ss"], "0")
        return ex[0][6]

    dx, grads[0] = _layer_bwd(dy, w0, res0, tabs, batch, after_dw=start_layer0, after_merge=exchange_layer1)
    grad_x = dx.reshape(batch, S, D)

    red = [None] * NL
    for l in (1, 0):
        r2w, r2s = _rs_exchange_wait(*ex[l][:6], dx, str(l))
        rw, rs = _rs_end(st[l], r2w, r2s)
        r = _unpack_grads(rs)
        r["w_in_t"] = rw
        red[l] = r
    G = {n: jnp.stack([red[l][n] for l in range(NL)]) for n in BIG[1:] + ("conv_w",)}
    g_in_t = jnp.stack([red[l]["w_in_t"] for l in range(NL)])
    G["w_in"] = jnp.swapaxes(g_in_t, 1, 2)
    small_g = {n: jnp.stack([grads[l][n].reshape(-1)[:sz] for l in range(NL)]) for n, sz in SMALL_SIZES}
    small_red = _all_reduce_small(_pack_small(small_g))
    G.update(_unpack_small(small_red, {n: W[n] for n in SMALL}))

    delta, new_m, new_v = {}, {}, {}
    for n in BIG[1:] + ("conv_w",):
        delta[n], new_m[n], new_v[n] = _adamw("adamw_" + n, W[n], G[n], M[n], V[n], ADAM_ROWS[n])
    tr = lambda t: jnp.swapaxes(t, 1, 2)
    delta["w_in"], new_m["w_in"], new_v["w_in"] = (
        tr(t) for t in _adamw("adamw_w_in", tr(w_in), g_in_t, tr(m_w_in), tr(v_w_in), WSH, LANE))
    sw, sm, sv = (_pack_small({n: t[n] for n in SMALL})[None] for t in (W, M, V))
    sd, snm, snv = _adamw("adamw_small", sw, small_red[None], sm, sv, SMALL_ROWS)
    like = {n: W[n] for n in SMALL}
    delta.update(_unpack_small(sd[0], like))
    new_m.update(_unpack_small(snm[0], like))
    new_v.update(_unpack_small(snv[0], like))

    return (loss, grad_x, *[G[n] for n in WEIGHTS], *[delta[n] for n in WEIGHTS],
            *[new_m[n] for n in WEIGHTS], *[new_v[n] for n in WEIGHTS])
```

```python
import functools

import numpy as np
import jax
import jax.numpy as jnp
from jax import lax
from jax.experimental import pallas as pl
from jax.experimental.pallas import tpu as pltpu

F32 = jnp.float32
BF16 = jnp.bfloat16

D = 1024
S = 2048
NL = 2
CW = 512
NH = 8
QL = 256
KVL = 128
NOPE = 64
ROPE = 32
VD = 64
QK = NOPE + ROPE
QKP = 128
ROPE_THETA = 10000.0
DIL = ((128, 1), (512, 4), (2048, 16))
NG = 3
DH = 8
HD = 64
DWID = DH * HD
QB = 128
EPS = 1e-6
NIN = 11168
NINP = 11264
O_A, O_CQ, O_CKV, O_KPE, O_BZ, O_DQ, O_DK, O_DV, O_CZ, O_G = 0, 2048, 2304, 2432, 2560, 3072, 4608, 6144, 7680, 8192
KPE_END = 2464
NEG = -1e30
MLA_SCALE = QK ** -0.5
DIL_SCALE = HD ** -0.5
LANE = 128
PACK_W = 512
VMEM_LIMIT = 48 * 1024 * 1024

ADAM_LR = 0.001
ADAM_B1 = 0.9
ADAM_B2 = 0.999
ADAM_EPS = 1e-08
ADAM_WD = 0.01
ADAM_STEP = 10

MESH = pl.DeviceIdType.MESH
BIG = ("w_in", "w_uq", "w_ukv", "w_out_a", "w_out_b", "w_out_c", "w_o")
SMALL = ("norm_g", "b_gate", "conv_b", "q_a_norm_g", "kv_a_norm_g", "mla_q_norm_g", "mla_k_norm_g",
         "dil_q_norm_g", "dil_k_norm_g")
WEIGHTS = ("norm_g", "w_in", "b_gate", "conv_w", "conv_b", "q_a_norm_g", "w_uq", "kv_a_norm_g", "w_ukv",
           "mla_q_norm_g", "mla_k_norm_g", "dil_q_norm_g", "dil_k_norm_g", "w_out_a", "w_out_b", "w_out_c", "w_o")


def _dot(a, b):
    return jnp.dot(a, b, preferred_element_type=F32)


def _dot_nt(a, b):
    return lax.dot_general(a, b, (((1,), (1,)), ((), ())), preferred_element_type=F32)


def _dot_tn(a, b):
    return lax.dot_general(a, b, (((0,), (0,)), ((), ())), preferred_element_type=F32)


def _grid_step(grid):
    step = pl.program_id(0)
    for a in range(1, len(grid)):
        step = step * grid[a] + pl.program_id(a)
    n = 1
    for g in grid:
        n *= g
    return step, n


def _write_windows(buf_ref, stages, sems, step, nsteps, puts):
    slot = step % 2
    for t, (v, dst) in enumerate(puts):
        cp = pltpu.make_async_copy(stages[t].at[slot], dst, sems.at[t, slot])

        @pl.when(step >= 2)
        def _():
            cp.wait()

        stages[t][slot] = v.astype(stages[t].dtype).reshape(stages[t].shape[1:])
        cp.start()

    @pl.when(step == nsteps - 1)
    def _():
        for t, (v, dst) in enumerate(puts):
            pltpu.make_async_copy(stages[t].at[slot], dst, sems.at[t, slot]).wait()
            if nsteps > 1:
                pltpu.make_async_copy(stages[t].at[1 - slot], dst, sems.at[t, 1 - slot]).wait()


def _pcall(name, fn, grid, ins, outs, into=None):
    n_in = len(ins)
    n_out = len(outs)
    acc_axis = len(grid) - 1
    is_acc = [len(o) > 4 and o[4] for o in outs]
    outs = [o[:4] for o in outs]
    targets = into[1] if into is not None else []
    n_t = len(targets)

    def body(*refs):
        vals = fn(*[r[...].astype(F32) for r in refs[:n_in]])
        if not isinstance(vals, (tuple, list)):
            vals = (vals,)
        o0 = n_in + (1 if n_t else 0)
        for k in range(n_out):
            r = refs[o0 + k]
            v = vals[k].astype(r.dtype).reshape(r.shape)
            if is_acc[k]:
                first = pl.program_id(acc_axis) == 0

                @pl.when(first)
                def _():
                    r[...] = v

                @pl.when(jnp.logical_not(first))
                def _():
                    r[...] += v
            else:
                r[...] = v
        if n_t:
            buf_ref = refs[o0 + n_out]
            stages = refs[o0 + n_out + 1:o0 + n_out + 1 + n_t]
            ids = [pl.program_id(a) for a in range(len(grid))]
            step, nsteps = _grid_step(grid)
            _write_windows(buf_ref, stages, refs[-1], step, nsteps,
                           [(vals[n_out + t], targets[t][1](buf_ref, *ids)) for t in range(n_t)])

    in_specs = [pl.BlockSpec(bs, im) for _, bs, im in ins]
    out_specs = [pl.BlockSpec(bs, im) for _, _, bs, im in outs]
    out_shape = [jax.ShapeDtypeStruct(sh, dt) for sh, dt, _, _ in outs]
    args = [a for a, _, _ in ins]
    extra = {}
    if n_t:
        buf = into[0]
        in_specs.append(pl.BlockSpec(memory_space=pl.ANY))
        out_specs.append(pl.BlockSpec(memory_space=pl.ANY))
        out_shape.append(jax.ShapeDtypeStruct(buf.shape, buf.dtype))
        args.append(buf)
        extra = dict(input_output_aliases={n_in: n_out},
                     scratch_shapes=[pltpu.VMEM((2,) + tuple(bs), buf.dtype) for bs, _ in targets]
                     + [pltpu.SemaphoreType.DMA((n_t, 2))])
    return pl.pallas_call(
        body,
        grid=grid,
        in_specs=in_specs,
        out_specs=out_specs,
        out_shape=out_shape,
        name=name,
        compiler_params=pltpu.CompilerParams(
            dimension_semantics=("arbitrary",) * len(grid), vmem_limit_bytes=VMEM_LIMIT),
        **extra,
    )(*args)


def _mm(name, a, b, *, ta=False, tb=False, out_dtype=F32, add=None, dep=None, b_words=False, tm=2048, tn=1024, tk=1024):
    if ta:
        K, M = a.shape
    else:
        M, K = a.shape
    bshape = (b.shape[0], 2 * b.shape[1]) if b_words else b.shape
    if tb:
        N, K2 = bshape
    else:
        K2, N = bshape
    assert K == K2, (name, a.shape, b.shape)
    tm, tn, tk = min(tm, M), min(tn, N), min(tk, K)
    assert M % tm == 0 and N % tn == 0 and K % tk == 0, (name, M, N, K)
    nk = K // tk
    dims = (((0 if ta else 1,), (1 if tb else 0,)), ((), ()))
    a_spec = pl.BlockSpec((tk, tm), lambda j, i, k: (k, i)) if ta else pl.BlockSpec((tm, tk), lambda j, i, k: (i, k))
    bw = 2 if b_words else 1
    assert not b_words or (tk if tb else tn) == bshape[1]
    b_spec = (pl.BlockSpec((tn, tk // bw), lambda j, i, k: (j, k)) if tb
              else pl.BlockSpec((tk, tn // bw), lambda j, i, k: (k, j)))
    o_spec = pl.BlockSpec((tm, tn), lambda j, i, k: (i, j))
    has_add = add is not None
    n_in = 2 + has_add + (dep is not None)

    def body(*refs):
        a_ref, b_ref = refs[0], refs[1]
        add_ref = refs[2] if has_add else None
        o_ref = refs[n_in]
        bb = b_ref[...]
        if b_words:
            lo, hi = _unpack_words(bb)
            first = (pl.program_id(0) * tn) if tb else (pl.program_id(2) * tk)
            r = first + lax.broadcasted_iota(jnp.int32, lo.shape, 0)
            pad = jnp.logical_and(r >= KPE_END, r < KPE_END + NINP - NIN)
            bb = jnp.concatenate([jnp.where(pad, 0.0, lo), jnp.where(pad, 0.0, hi)], axis=1)
        p = lax.dot_general(a_ref[...].astype(BF16), bb.astype(BF16), dims, preferred_element_type=F32)
        if nk == 1:
            if has_add:
                p = p + add_ref[...]
            o_ref[...] = p.astype(out_dtype)
        else:
            acc = refs[-1]
            k = pl.program_id(2)

            @pl.when(k == 0)
            def _():
                acc[...] = p

            @pl.when(k > 0)
            def _():
                acc[...] += p

            @pl.when(k == nk - 1)
            def _():
                r = acc[...]
                if has_add:
                    r = r + add_ref[...]
                o_ref[...] = r.astype(out_dtype)

    in_specs = [a_spec, b_spec] + ([o_spec] if has_add else []) + ([pl.BlockSpec(memory_space=pl.ANY)] if dep is not None else [])
    args = [a, b] + ([add] if has_add else []) + ([dep] if dep is not None else [])
    return pl.pallas_call(
        body,
        grid=(N // tn, M // tm, nk),
        in_specs=in_specs,
        out_specs=o_spec,
        out_shape=jax.ShapeDtypeStruct((M, N), out_dtype),
        scratch_shapes=[pltpu.VMEM((tm, tn), F32)] if nk > 1 else [],
        name=name,
        compiler_params=pltpu.CompilerParams(
            dimension_semantics=("arbitrary", "arbitrary", "arbitrary"), vmem_limit_bytes=VMEM_LIMIT),
    )(*args)


def _vjp_of(f, n_diff):
    def g(*args, n_prim):
        prim = args[:n_diff]
        consts = args[n_diff:n_prim]
        cts = args[n_prim:]
        _, pull = jax.vjp(lambda *p: f(*p, *consts), *prim)
        out = jax.eval_shape(lambda *p: f(*p, *consts), *prim)
        if isinstance(out, (tuple, list)):
            cts = tuple(c.astype(o.dtype) for c, o in zip(cts, out))
        else:
            cts = cts[0].astype(out.dtype)
        return pull(cts)
    return g


def _rms(x, g, n=None):
    n = x.shape[-1] if n is None else n
    ms = jnp.sum(x * x, axis=-1, keepdims=True) / n
    return x * lax.rsqrt(ms + EPS) * g


def _silu(z):
    return z * jax.nn.sigmoid(z)


def _roll_rows(u, k):
    n = u.shape[0]
    r = pltpu.roll(u, k % n, 0)
    t = lax.broadcasted_iota(jnp.int32, u.shape, 0)
    if k > 0:
        return jnp.where(t >= k, r, 0.0)
    return jnp.where(t < n + k, r, 0.0)


@functools.partial(jax.custom_vjp, nondiff_argnums=(1,))
def _shift(u, k):
    return _roll_rows(u, k)


def _shift_fwd(u, k):
    return _roll_rows(u, k), None


def _shift_bwd(k, _, g):
    return (_roll_rows(g, -k),)


_shift.defvjp(_shift_fwd, _shift_bwd)


@functools.partial(jax.custom_vjp, nondiff_argnums=(1,))
def _lane_roll(u, k):
    return pltpu.roll(u, k % LANE, 1)


def _lane_roll_fwd(u, k):
    return pltpu.roll(u, k % LANE, 1), None


def _lane_roll_bwd(k, _, g):
    return (pltpu.roll(g, (-k) % LANE, 1),)


_lane_roll.defvjp(_lane_roll_fwd, _lane_roll_bwd)


def _conv_math(ab, ac, ax, az, cw, cb):
    u = ac * ax
    conv = cb + _shift(u, 2) * cw[0:1] + _shift(u, 1) * cw[1:2] + u * cw[2:3]
    return ab * conv * _silu(az)


def _mla_pre_math(cq, ckv, gq, gkv):
    return _rms(cq, gq), _rms(ckv, gkv)


def _rope_math(q, kn, kpe, gq, gk, c, s1, s2):
    lane = lax.broadcasted_iota(jnp.int32, kpe.shape, 1)
    pe = _lane_roll(jnp.where(lane < ROPE, kpe, 0.0), NOPE)

    def one(t, g):
        tn = _rms(t, g, QK)
        return tn * c + _lane_roll(tn, -16) * s1 + _lane_roll(tn, 16) * s2

    qs, ks = [], []
    for h in range(NH):
        sl = slice(h * QKP, (h + 1) * QKP)
        qs.append(one(q[:, sl], gq))
        ks.append(one(kn[:, sl] + pe, gk))
    return jnp.concatenate(qs, axis=1), jnp.concatenate(ks, axis=1)


def _gate_math(o, z):
    return o * _silu(z)


def _mergec_math(o0, o1, o2, l0, l1, l2, cz):
    m = lax.stop_gradient(jnp.maximum(jnp.maximum(l0, l1), l2))
    e0, e1, e2 = jnp.exp(l0 - m), jnp.exp(l1 - m), jnp.exp(l2 - m)
    den = e0 + e1 + e2
    oc = (e0 / den) * o0 + (e1 / den) * o1 + (e2 / den) * o2
    return oc * _silu(cz)


def _merge_math(g0, g1, g2, b0, b1, b2, pa, pb, pc):
    return (jax.nn.sigmoid(g0 + b0) * pa + jax.nn.sigmoid(g1 + b1) * pb) + jax.nn.sigmoid(g2 + b2) * pc


MLA_T = 256
MLA_UNROLL = True


def _mla_fwd(q, k, v):
    B = q.shape[0]
    T = MLA_T
    NB = S // T

    def body(q_ref, k_ref, v_ref, o_ref, l_ref):
        row = lax.broadcasted_iota(jnp.int32, (T, T), 0)
        col = lax.broadcasted_iota(jnp.int32, (T, T), 1)
        lo = _lo_mask((T, LANE))

        for qi in range(NB):
            qb = q_ref[qi * T:(qi + 1) * T, :]

            def step(j, carry, diagonal):
                m, l, acc = carry
                off = pl.multiple_of(j * T, T)
                kb = k_ref[pl.ds(off, T), :]
                vb = v_ref[pl.ds(off, T), :]
                ss = []
                for e in (0, 1):
                    se = _dot_nt(qb[:, e * QKP:(e + 1) * QKP], kb[:, e * QKP:(e + 1) * QKP]) * MLA_SCALE
                    ss.append(jnp.where(col <= row, se, NEG) if diagonal else se)
                s = jnp.concatenate(ss, axis=0)
                m_new = jnp.maximum(m, jnp.max(s, axis=-1, keepdims=True))
                a = jnp.exp(m - m_new)
                p = jnp.exp(s - m_new)
                l = a * l + jnp.sum(p, axis=-1, keepdims=True)
                acc = a * acc + _dot(p.astype(BF16), vb)
                return m_new, l, acc

            init = (jnp.full((2 * T, 1), NEG, F32), jnp.zeros((2 * T, 1), F32), jnp.zeros((2 * T, LANE), F32))
            carry = lax.fori_loop(0, qi, functools.partial(step, diagonal=False), init, unroll=MLA_UNROLL)
            m, l, acc = step(qi, carry, True)
            o = acc / l
            lse = m + jnp.log(l)
            o_ref[qi * T:(qi + 1) * T, :] = jnp.where(lo, o[:T], o[T:])
            l_ref[qi * T:(qi + 1) * T, :] = jnp.where(lo, lse[:T], lse[T:])

    def spec(w):
        return pl.BlockSpec((None, S, w), lambda b, hp: (b, 0, hp))

    return pl.pallas_call(
        body,
        grid=(B, NH // 2),
        in_specs=[spec(2 * QKP), spec(2 * QKP), spec(LANE)],
        out_specs=[spec(LANE), spec(LANE)],
        out_shape=[jax.ShapeDtypeStruct((B, S, NH * VD), F32)] * 2,
        name="mla_attn_fwd",
        compiler_params=pltpu.CompilerParams(dimension_semantics=("arbitrary",) * 2, vmem_limit_bytes=VMEM_LIMIT),
    )(q, k, v)


def _mla_bwd(q, k, v, do, o, lse):
    B = q.shape[0]
    T = MLA_T
    NB = S // T

    def body(q_ref, k_ref, v_ref, do_ref, o_ref, l_ref, dq_ref, dk_ref, dv_ref, delta_ref, dqt_ref):
        delta_ref[...] = _head_sum(do_ref[...] * o_ref[...])
        row = lax.broadcasted_iota(jnp.int32, (T, T), 0)
        col = lax.broadcasted_iota(jnp.int32, (T, T), 1)
        lo = _lo_mask((T, LANE))
        tn_t = (((0,), (1,)), ((), ()))

        for j in range(NB):
            krows = slice(j * T, (j + 1) * T)
            kb = k_ref[krows, :]
            vb = v_ref[krows, :]
            dkt = [jnp.zeros((QKP, T), F32), jnp.zeros((QKP, T), F32)]
            dvt = jnp.zeros((LANE, T), F32)
            for i in range(j, NB):
                qrows = slice(i * T, (i + 1) * T)
                qb = q_ref[qrows, :]
                do2 = _stack_heads(do_ref[qrows, :], lo).astype(BF16)
                lb = l_ref[qrows, :]
                db = delta_ref[qrows, :]
                dp2 = _dot_nt(do2, vb)
                ps = []
                for e in (0, 1):
                    cols = slice(e * QKP, (e + 1) * QKP)
                    qe, ke = qb[:, cols], kb[:, cols]
                    s = _dot_nt(qe, ke) * MLA_SCALE
                    if i == j:
                        s = jnp.where(col <= row, s, NEG)
                    p = jnp.exp(s - lb[:, e * HD:e * HD + 1])
                    ps.append(p.astype(BF16))
                    ds = (p * (dp2[e * T:(e + 1) * T] - db[:, e * HD:e * HD + 1]) * MLA_SCALE).astype(BF16)
                    dkt[e] = dkt[e] + _dot_tn(qe, ds)
                    dq_t = lax.dot_general(ke, ds, tn_t, preferred_element_type=F32)
                    if j == 0:
                        dqt_ref[e, :, qrows] = dq_t
                    else:
                        dqt_ref[e, :, qrows] += dq_t
                dvt = dvt + _dot_tn(do2, jnp.concatenate(ps, axis=0))
            dk_ref[krows, 0:QKP] = dkt[0].T
            dk_ref[krows, QKP:2 * QKP] = dkt[1].T
            dv_ref[krows, :] = dvt.T
        dq_ref[:, 0:QKP] = dqt_ref[0].T
        dq_ref[:, QKP:2 * QKP] = dqt_ref[1].T

    def spec(w):
        return pl.BlockSpec((None, S, w), lambda b, hp: (b, 0, hp))

    return pl.pallas_call(
        body,
        grid=(B, NH // 2),
        in_specs=[spec(2 * QKP), spec(2 * QKP), spec(LANE), spec(LANE), spec(LANE), spec(LANE)],
        out_specs=[spec(2 * QKP), spec(2 * QKP), spec(LANE)],
        out_shape=[jax.ShapeDtypeStruct((B, S, NH * QKP), F32), jax.ShapeDtypeStruct((B, S, NH * QKP), F32),
                   jax.ShapeDtypeStruct((B, S, NH * VD), F32)],
        scratch_shapes=[pltpu.VMEM((S, LANE), F32), pltpu.VMEM((2, QKP, S), F32)],
        name="mla_attn_bwd",
        compiler_params=pltpu.CompilerParams(dimension_semantics=("arbitrary",) * 2, vmem_limit_bytes=VMEM_LIMIT),
    )(q, k, v, do, o, lse)


def _lo_mask(shape):
    return lax.broadcasted_iota(jnp.int32, shape, len(shape) - 1) < HD


def _head_sum(u):
    r = lax.broadcasted_iota(jnp.int32, (LANE, LANE), 0) < HD
    c = lax.broadcasted_iota(jnp.int32, (LANE, LANE), 1) < HD
    ones = jnp.where(r == c, 1.0, 0.0).astype(BF16)
    hi = u.astype(BF16)
    lo = (u - hi.astype(F32)).astype(BF16)
    return _dot(hi, ones) + _dot(lo, ones)


def _head_sum_1(u):
    r = lax.broadcasted_iota(jnp.int32, (LANE, LANE), 0) < HD
    c = lax.broadcasted_iota(jnp.int32, (LANE, LANE), 1) < HD
    return _dot(u.astype(BF16), jnp.where(r == c, 1.0, 0.0).astype(BF16))


def _rms2_scale(x):
    return lax.rsqrt(_head_sum(x * x) / HD + EPS)


def _rms2(x, g):
    return x * _rms2_scale(x) * g


def _rms2_bwd(x, r, g, dy):
    xn = x * r
    t = dy * g
    dx = r * (t - xn * (_head_sum_1(xn * t) * (1.0 / HD)))
    return dx, jnp.sum(dy * xn, axis=0, keepdims=True)


def _dil_bias(t_ref, gi, d):
    qq = lax.broadcasted_iota(jnp.int32, (QB, QB), 0)
    kk = lax.broadcasted_iota(jnp.int32, (QB, QB), 1)
    jc = (qq - kk).astype(F32)
    rows = []
    for e in (0, 1):
        sl = t_ref[2 * gi + e:2 * gi + e + 1, :] * float(d)
        bp = jnp.where(kk >= qq, -sl * (jc + float(QB)), NEG)
        bc = jnp.where(kk <= qq, -sl * jc, NEG)
        rows.append(jnp.concatenate([bp, bc], axis=1))
    return jnp.concatenate(rows, axis=0)


def _dil_rows(cur, d):
    return pl.ds(cur, QB, stride=d) if d > 1 else pl.ds(pl.multiple_of(cur, QB), QB)


def _dil_walk(d, block, full):
    if d == 1:
        block(0, None)

        def body(i, c):
            block(i * QB, (i - 1) * QB)
            return c
        lax.fori_loop(1, S // QB, body, 0, unroll=True if full else 5)
    elif d == 16:
        def body(r, c):
            block(r, None)
            return c
        lax.fori_loop(0, d, body, 0, unroll=True if full else 4)
    else:
        nb = S // d // QB

        def cls(r, c):
            block(r, None)

            def body(i, c2):
                block(r + i * QB * d, r + (i - 1) * QB * d)
                return c2
            lax.fori_loop(1, nb, body, 0, unroll=True)
            return c
        lax.fori_loop(0, d, cls, 0, unroll=full)


def _stack_heads(x, lo):
    return jnp.concatenate([jnp.where(lo, x, 0.0), jnp.where(lo, 0.0, x)], axis=0)


def _dilc_fwd(proj3, gq, gk, tab):
    B = proj3.shape[0]

    def body(q_ref, k_ref, v_ref, cz_ref, gq_ref, gk_ref, t_ref, y_ref, o_ref, l_ref, qs, ks, vs):
        g = pl.program_id(2)
        lo = _lo_mask((QB, LANE))

        def group(gi):
            d = DIL[gi][1]
            qs[...] = _rms2(q_ref[...].astype(F32), gq_ref[gi:gi + 1, :])
            ks[...] = _rms2(k_ref[...].astype(F32), gk_ref[gi:gi + 1, :])
            vs[...] = v_ref[...].astype(F32)
            bias = _dil_bias(t_ref, gi, d)

            def block(cur, prev):
                rows = _dil_rows(cur, d)
                q2 = _stack_heads(qs[rows, :], lo).astype(BF16)
                kc, vc = ks[rows, :], vs[rows, :]
                if prev is None:
                    kcat, vcat, b = kc, vc, bias[:, QB:]
                else:
                    prow = _dil_rows(prev, d)
                    kcat = jnp.concatenate([ks[prow, :], kc], axis=0)
                    vcat = jnp.concatenate([vs[prow, :], vc], axis=0)
                    b = bias
                s = _dot_nt(q2, kcat.astype(BF16)) * DIL_SCALE + b
                m = jnp.max(s, axis=-1, keepdims=True)
                p = jnp.exp(s - m)
                l = jnp.sum(p, axis=-1, keepdims=True)
                o = _dot(p.astype(BF16), vcat.astype(BF16)) / l
                lse = m + jnp.log(l)
                o_ref[gi, rows, :] = jnp.where(lo, o[:QB], o[QB:])
                l_ref[gi, rows, :] = jnp.where(lo, lse[:QB], lse[QB:])

            _dil_walk(d, block, True)

        for gi in range(NG):
            pl.when(g == gi)(functools.partial(group, gi))

        @pl.when(g == NG - 1)
        def _():
            y_ref[...] = _mergec_math(o_ref[0], o_ref[1], o_ref[2], l_ref[0], l_ref[1], l_ref[2],
                                      cz_ref[...].astype(F32)).astype(BF16)

    def col(base):
        return pl.BlockSpec((None, S, LANE), lambda b, hp, g: (b, 0, base // LANE + 4 * g + hp))

    gspec = pl.BlockSpec((NG, LANE), lambda b, hp, g: (0, 0))
    saved = pl.BlockSpec((NG, None, S, LANE), lambda b, hp, g: (0, b, 0, hp))
    return pl.pallas_call(
        body,
        grid=(B, 4, NG),
        in_specs=[col(O_DQ), col(O_DK), col(O_DV),
                  pl.BlockSpec((None, S, LANE), lambda b, hp, g: (b, 0, O_CZ // LANE + hp)),
                  gspec, gspec, pl.BlockSpec((None, 8, LANE), lambda b, hp, g: (hp, 0, 0))],
        out_specs=[pl.BlockSpec((None, S, LANE), lambda b, hp, g: (b, 0, hp)), saved, saved],
        out_shape=[jax.ShapeDtypeStruct((B, S, DWID), BF16), jax.ShapeDtypeStruct((NG, B, S, DWID), F32),
                   jax.ShapeDtypeStruct((NG, B, S, DWID), F32)],
        scratch_shapes=[pltpu.VMEM((S, LANE), F32)] * 3,
        name="dil_mixer_fwd",
        compiler_params=pltpu.CompilerParams(dimension_semantics=("arbitrary",) * 3, vmem_limit_bytes=VMEM_LIMIT),
    )(proj3, proj3, proj3, proj3, gq, gk, tab)


MERGE_ROWS = 256


def _dilc_bwd(proj3, gq, gk, tab, o_all, l_all, d_yc, dproj3):
    B = proj3.shape[0]

    def body(q_ref, k_ref, v_ref, cz_ref, gq_ref, gk_ref, t_ref, o_ref, l_ref, dy_ref, dp_in,
             dp_out, dgq_out, dgk_out, qs, ks, vs, dos, dls, dqs, dks, dvs, rqs, rks, dczs,
             st_q, st_k, st_v, st_z, sems, sem_z):
        b_, hp, g = pl.program_id(0), pl.program_id(1), pl.program_id(2)
        col = lambda base: pl.ds(pl.multiple_of(base + hp * LANE, LANE), LANE)
        lo = _lo_mask((QB, LANE))

        @pl.when(jnp.logical_and(jnp.logical_and(pl.program_id(0) == 0, pl.program_id(1) == 0), g == 0))
        def _():
            dgq_out[...] = jnp.zeros((NG, LANE), F32)
            dgk_out[...] = jnp.zeros((NG, LANE), F32)

        @pl.when(g == 0)
        def _():
            def chunk(i, carry):
                rows = pl.ds(pl.multiple_of(i * MERGE_ROWS, MERGE_ROWS), MERGE_ROWS)
                ls = [l_ref[j, rows, :] for j in range(NG)]
                m = jnp.maximum(jnp.maximum(ls[0], ls[1]), ls[2])
                es = [jnp.exp(t - m) for t in ls]
                den = (es[0] + es[1]) + es[2]
                al = [e / den for e in es]
                os_ = [o_ref[j, rows, :] for j in range(NG)]
                oc = (al[0] * os_[0] + al[1] * os_[1]) + al[2] * os_[2]
                cz = cz_ref[rows, :].astype(F32)
                sg = jax.nn.sigmoid(cz)
                dy = dy_ref[rows, :]
                d_oc = dy * (cz * sg)
                dczs[rows, :] = (dy * oc * (sg * (1.0 + cz * (1.0 - sg)))).astype(BF16)
                ts = [_head_sum_1(d_oc * os_[j]) for j in range(NG)]
                tbar = (al[0] * ts[0] + al[1] * ts[1]) + al[2] * ts[2]
                for j in range(NG):
                    dos[j, rows, :] = al[j] * d_oc
                    dls[j, rows, :] = al[j] * (ts[j] - tbar)
                return carry
            lax.fori_loop(0, S // MERGE_ROWS, chunk, 0)
            _write_windows(dp_out, [st_z], sem_z, b_ * 4 + hp, B * 4, [(dczs[...], dp_out.at[b_, :, col(O_CZ)])])

        def group(gi):
            d = DIL[gi][1]
            xq, xk = q_ref[...].astype(F32), k_ref[...].astype(F32)
            rqs[...] = _rms2_scale(xq)
            rks[...] = _rms2_scale(xk)
            qs[...] = xq * rqs[...] * gq_ref[gi:gi + 1, :]
            ks[...] = xk * rks[...] * gk_ref[gi:gi + 1, :]
            vs[...] = v_ref[...].astype(F32)
            dks[...] = jnp.zeros((S, LANE), F32)
            dvs[...] = jnp.zeros((S, LANE), F32)
            bias = _dil_bias(t_ref, gi, d)

            def block(cur, prev):
                rows = _dil_rows(cur, d)
                q2 = _stack_heads(qs[rows, :], lo).astype(BF16)
                dob = dos[gi, rows, :]
                do2 = _stack_heads(dob, lo).astype(BF16)
                kc, vc = ks[rows, :], vs[rows, :]
                if prev is None:
                    kcat, vcat, b = kc, vc, bias[:, QB:]
                else:
                    prow = _dil_rows(prev, d)
                    kcat = jnp.concatenate([ks[prow, :], kc], axis=0)
                    vcat = jnp.concatenate([vs[prow, :], vc], axis=0)
                    b = bias
                kcat = kcat.astype(BF16)
                vcat = vcat.astype(BF16)
                lse_b = l_ref[gi, rows, :]
                corr_b = dls[gi, rows, :] - _head_sum_1(dob * o_ref[gi, rows, :])
                lse2 = jnp.concatenate([lse_b[:, 0:1], lse_b[:, HD:HD + 1]], axis=0)
                corr2 = jnp.concatenate([corr_b[:, 0:1], corr_b[:, HD:HD + 1]], axis=0)
                s = _dot_nt(q2, kcat) * DIL_SCALE + b
                p = jnp.exp(s - lse2)
                ds = (p * (_dot_nt(do2, vcat) + corr2) * DIL_SCALE).astype(BF16)
                dq2 = _dot(ds, kcat)
                dqs[rows, :] = jnp.where(lo, dq2[:QB], dq2[QB:])
                dk = _dot_tn(ds, q2)
                dv = _dot_tn(p.astype(BF16), do2)
                if prev is None:
                    dks[rows, :] += dk
                    dvs[rows, :] += dv
                else:
                    dks[prow, :] += dk[:QB]
                    dvs[prow, :] += dv[:QB]
                    dks[rows, :] += dk[QB:]
                    dvs[rows, :] += dv[QB:]

            _dil_walk(d, block, False)

            dxq, dgq = _rms2_bwd(q_ref[...].astype(F32), rqs[...], gq_ref[gi:gi + 1, :], dqs[...])
            dgq_out[gi:gi + 1, :] += dgq
            dxk, dgk = _rms2_bwd(k_ref[...].astype(F32), rks[...], gk_ref[gi:gi + 1, :], dks[...])
            dgk_out[gi:gi + 1, :] += dgk
            step, nsteps = _grid_step((B, 4, NG))
            _write_windows(dp_out, [st_q, st_k, st_v], sems, step, nsteps,
                           [(dxq, dp_out.at[b_, :, col(O_DQ + gi * DWID)]), (dxk, dp_out.at[b_, :, col(O_DK + gi * DWID)]),
                            (dvs[...], dp_out.at[b_, :, col(O_DV + gi * DWID)])])

        for gi in range(NG):
            pl.when(g == gi)(functools.partial(group, gi))

    def col(base):
        return pl.BlockSpec((None, S, LANE), lambda b, hp, g: (b, 0, base // LANE + 4 * g + hp))

    gspec = pl.BlockSpec((NG, LANE), lambda b, hp, g: (0, 0))
    saved = pl.BlockSpec((NG, None, S, LANE), lambda b, hp, g: (0, b, 0, hp))
    per_pair = pl.BlockSpec((None, S, LANE), lambda b, hp, g: (b, 0, hp))
    return pl.pallas_call(
        body,
        grid=(B, 4, NG),
        in_specs=[col(O_DQ), col(O_DK), col(O_DV),
                  pl.BlockSpec((None, S, LANE), lambda b, hp, g: (b, 0, O_CZ // LANE + hp)),
                  gspec, gspec, pl.BlockSpec((None, 8, LANE), lambda b, hp, g: (hp, 0, 0)),
                  saved, saved, per_pair, pl.BlockSpec(memory_space=pl.ANY)],
        out_specs=[pl.BlockSpec(memory_space=pl.ANY), gspec, gspec],
        out_shape=[jax.ShapeDtypeStruct(dproj3.shape, dproj3.dtype), jax.ShapeDtypeStruct((NG, LANE), F32),
                   jax.ShapeDtypeStruct((NG, LANE), F32)],
        input_output_aliases={10: 0},
        scratch_shapes=[pltpu.VMEM((S, LANE), F32)] * 3 + [pltpu.VMEM((NG, S, LANE), F32)] * 2
        + [pltpu.VMEM((S, LANE), F32)] * 5 + [pltpu.VMEM((S, LANE), BF16)] + [pltpu.VMEM((2, S, LANE), BF16)] * 4
        + [pltpu.SemaphoreType.DMA((3, 2)), pltpu.SemaphoreType.DMA((1, 2))],
        name="dil_mixer_bwd",
        compiler_params=pltpu.CompilerParams(dimension_semantics=("arbitrary",) * 3, vmem_limit_bytes=VMEM_LIMIT),
    )(proj3, proj3, proj3, proj3, gq, gk, tab, o_all, l_all, d_yc, dproj3)


def _dil_slopes():
    slopes = (2.0 ** (-8.0 * np.arange(1, NG * DH + 1, dtype=np.float32) / (NG * DH))).astype(np.float32).reshape(NG, DH)
    tab = np.zeros((4, 8, LANE), np.float32)
    for hp in range(4):
        for gi in range(NG):
            for e in (0, 1):
                tab[hp, 2 * gi + e, :] = slopes[gi, 2 * hp + e]
    return jnp.asarray(tab)


def _rope_tables():
    inv = ROPE_THETA ** (-jnp.arange(0, ROPE, 2, dtype=F32) / ROPE)
    ang = jnp.arange(S, dtype=F32)[:, None] * inv[None, :]
    cos, sin = jnp.cos(ang), jnp.sin(ang)
    z16 = jnp.zeros((S, 16), F32)
    c = jnp.concatenate([jnp.ones((S, NOPE), F32), cos, cos, jnp.zeros((S, 32), F32)], axis=1)
    s1 = jnp.concatenate([jnp.zeros((S, NOPE), F32), -sin, z16, jnp.zeros((S, 32), F32)], axis=1)
    s2 = jnp.concatenate([jnp.zeros((S, NOPE), F32), z16, sin, jnp.zeros((S, 32), F32)], axis=1)
    return c, s1, s2


def _pad_heads_uq(w):
    return jnp.pad(w.reshape(QL, NH, QK), ((0, 0), (0, 0), (0, QKP - QK))).reshape(QL, NH * QKP)


def _unpad_heads_uq(g):
    return g.reshape(QL, NH, QKP)[:, :, :QK].reshape(QL, NH * QK)


def _split_ukv(w):
    w3 = w.reshape(KVL, NH, NOPE + VD)
    uk = jnp.pad(w3[:, :, :NOPE], ((0, 0), (0, 0), (0, QKP - NOPE))).reshape(KVL, NH * QKP)
    return uk, w3[:, :, NOPE:].reshape(KVL, NH * VD)


def _join_ukv(guk, guv):
    return jnp.concatenate([guk.reshape(KVL, NH, QKP)[:, :, :NOPE], guv.reshape(KVL, NH, VD)],
                           axis=-1).reshape(KVL, NH * (NOPE + VD))


BR = 512
BRM = 256


def _layer_fwd(x, w, tabs, batch):
    T = batch * S
    rope_c, rope_s1, rope_s2, dil_tab = tabs
    res = {"x": x}
    row = lambda c: (lambda i: (i, c))
    fix = lambda i: (0, 0)

    h = _pcall("norm_fwd", _rms, (T // BR,),
               [(x, (BR, D), row(0)), (w["norm_g"], (1, D), fix)],
               [((T, D), BF16, (BR, D), row(0))])[0]
    proj = _mm("in_proj", h, w["w_in_t"], tb=True, out_dtype=BF16, b_words=True, tm=2048, tn=1024)
    res["h"], res["proj"] = h, proj
    proj3 = proj.reshape(batch, S, NINP)

    cblk = lambda s: (lambda j, b: (b, 0, 4 * s + j))
    y_a = _pcall("conv_fwd", _conv_math, (4, batch),
                 [(proj3, (None, S, LANE), cblk(0)), (proj3, (None, S, LANE), cblk(1)),
                  (proj3, (None, S, LANE), cblk(2)), (proj3, (None, S, LANE), cblk(3)),
                  (w["conv_w"], (3, LANE), lambda j, b: (0, j)), (w["conv_b"], (1, LANE), lambda j, b: (0, j))],
                 [((batch, S, CW), BF16, (None, S, LANE), lambda j, b: (b, 0, j))])[0].reshape(T, CW)
    res["y_a"] = y_a

    cqn, ckvn = _pcall("mla_pre_fwd", _mla_pre_math, (T // BR,),
                       [(proj, (BR, QL), row(O_CQ // QL)), (proj, (BR, KVL), row(O_CKV // KVL)),
                        (w["q_a_norm_g"], (1, QL), fix), (w["kv_a_norm_g"], (1, KVL), fix)],
                       [((T, QL), BF16, (BR, QL), row(0)), ((T, KVL), BF16, (BR, KVL), row(0))])
    w_uq_p = _pad_heads_uq(w["w_uq"])
    w_uk, w_uv = _split_ukv(w["w_ukv"])
    q = _mm("uq", cqn, w_uq_p, out_dtype=BF16)
    kn = _mm("uk", ckvn, w_uk, out_dtype=BF16)
    v = _mm("uv", ckvn, w_uv, out_dtype=BF16)
    nrr = S // BR
    tab_row = lambda i: (i % nrr, 0)
    qr, kr = _pcall("rope_fwd", _rope_math, (T // BR,),
                    [(q, (BR, NH * QKP), row(0)), (kn, (BR, NH * QKP), row(0)), (proj, (BR, LANE), row(O_KPE // LANE)),
                     (w["mla_q_norm_g"], (1, QKP), fix), (w["mla_k_norm_g"], (1, QKP), fix),
                     (rope_c, (BR, QKP), tab_row), (rope_s1, (BR, QKP), tab_row), (rope_s2, (BR, QKP), tab_row)],
                    [((T, NH * QKP), BF16, (BR, NH * QKP), row(0))] * 2)
    qr = qr.reshape(batch, S, NH * QKP)
    kr = kr.reshape(batch, S, NH * QKP)
    v = v.reshape(batch, S, NH * VD)
    o_b, l_b = _mla_fwd(qr, kr, v)
    ob2 = o_b.reshape(T, NH * VD)
    y_b = _pcall("gateb_fwd", _gate_math, (T // BR,),
                 [(ob2, (BR, 512), row(0)), (proj, (BR, 512), row(O_BZ // 512))],
                 [((T, 512), BF16, (BR, 512), row(0))])[0]
    res.update(cqn=cqn, ckvn=ckvn, q=q, kn=kn, qr=qr, kr=kr, v=v, o_b=o_b, l_b=l_b, ob2=ob2, y_b=y_b,
               w_uq_p=w_uq_p, w_uk=w_uk, w_uv=w_uv)

    gq2 = jnp.tile(w["dil_q_norm_g"].reshape(NG, HD), (1, 2))
    gk2 = jnp.tile(w["dil_k_norm_g"].reshape(NG, HD), (1, 2))
    y_c, o_all, l_all = _dilc_fwd(proj3, gq2, gk2, dil_tab)
    y_c = y_c.reshape(T, DWID)
    res.update(o_all=o_all, l_all=l_all, y_c=y_c)

    pa = _mm("out_a", y_a, w["w_out_a"], out_dtype=BF16)
    pb = _mm("out_b", y_b, w["w_out_b"], out_dtype=BF16)
    pc = _mm("out_c", y_c, w["w_out_c"], out_dtype=BF16)
    merged = _pcall("merge_fwd", _merge_math, (T // BRM,),
                    [(proj, (BRM, D), row(O_G // D + s)) for s in range(3)]
                    + [(w["b_gate"], (1, D), (lambda s: (lambda i: (0, s)))(s)) for s in range(3)]
                    + [(t, (BRM, D), row(0)) for t in (pa, pb, pc)],
                    [((T, D), BF16, (BRM, D), row(0))])[0]
    out = _mm("o_proj", merged, w["w_o"], add=x, tm=1024)
    res.update(pa=pa, pb=pb, pc=pc, merged=merged)
    return out, res


def _norm_bwd_math(x, g, dh, dy):
    _, pull = jax.vjp(_rms, x, g)
    dx, dg = pull(dh)
    return dx + dy, dg


def _layer_bwd(dy, w, res, tabs, batch, after_dw=None, after_merge=None):
    T = batch * S
    rope_c, rope_s1, rope_s2, dil_tab = tabs
    row = lambda c: (lambda i: (i, c))
    fix = lambda i: (0, 0)
    x, proj, h = res["x"], res["proj"], res["h"]
    proj3 = proj.reshape(batch, S, NINP)
    g = {}

    d_merged = _mm("o_proj_dx", dy, w["w_o"], tb=True)
    g["w_o"] = _mm("o_proj_dw", res["merged"], dy, ta=True, tm=1024, tk=2048)

    dproj = lax.empty((T, NINP), BF16)
    rows_of = lambda br: (lambda ref, i: ref.at[pl.ds(pl.multiple_of(i * br, br), br)])

    def merge_bwd(*args):
        dg0, dg1, dg2, db0, db1, db2, dpa, dpb, dpc = _vjp_of(_merge_math, 9)(*args, n_prim=9)
        return db0, db1, db2, dpa, dpb, dpc, jnp.concatenate([dg0, dg1, dg2], axis=1)

    db0, db1, db2, dpa, dpb, dpc, dproj = _pcall(
        "merge_bwd", merge_bwd, (T // BRM,),
        [(proj, (BRM, D), row(O_G // D + s)) for s in range(3)]
        + [(w["b_gate"], (1, D), (lambda s: (lambda i: (0, s)))(s)) for s in range(3)]
        + [(t, (BRM, D), row(0)) for t in (res["pa"], res["pb"], res["pc"])]
        + [(d_merged, (BRM, D), row(0))],
        [((1, D), F32, (1, D), fix, True)] * 3 + [((T, D), BF16, (BRM, D), row(0))] * 3,
        into=(dproj, [((BRM, 3 * D), lambda ref, i: rows_of(BRM)(ref, i).at[:, O_G:O_G + 3 * D])]))
    g["b_gate"] = jnp.concatenate([db0, db1, db2], axis=1)

    dep = after_merge(dpa) if after_merge is not None else None
    d_ya = _mm("out_a_dx", dpa, w["w_out_a"], tb=True, dep=dep)
    d_yb = _mm("out_b_dx", dpb, w["w_out_b"], tb=True)
    d_yc = _mm("out_c_dx", dpc, w["w_out_c"], tb=True)
    g["w_out_a"] = _mm("out_a_dw", res["y_a"], dpa, ta=True, tk=T)
    g["w_out_b"] = _mm("out_b_dw", res["y_b"], dpb, ta=True, tk=T)
    g["w_out_c"] = _mm("out_c_dw", res["y_c"], dpc, ta=True, tk=T)

    cblk = lambda s: (lambda j, b: (b, 0, 4 * s + j))
    oblk = lambda j, b: (b, 0, j)
    def conv_bwd(*args):
        d_ab, d_ac, d_ax, d_az, dcw, dcb = _vjp_of(_conv_math, 6)(*args, n_prim=6)
        return dcw, dcb, d_ab, d_ac, d_ax, d_az

    a_col = lambda s_: (lambda ref, j, b: ref.at[b, :, pl.ds(pl.multiple_of(O_A + s_ * CW + j * LANE, LANE), LANE)])
    g["conv_w"], g["conv_b"], dproj3 = _pcall(
        "conv_bwd", conv_bwd, (4, batch),
        [(proj3, (None, S, LANE), cblk(s)) for s in range(4)]
        + [(w["conv_w"], (3, LANE), lambda j, b: (0, j)), (w["conv_b"], (1, LANE), lambda j, b: (0, j)),
           (d_ya.reshape(batch, S, CW), (None, S, LANE), oblk)],
        [((3, CW), F32, (3, LANE), lambda j, b: (0, j), True), ((1, CW), F32, (1, LANE), lambda j, b: (0, j), True)],
        into=(dproj.reshape(batch, S, NINP), [((S, LANE), a_col(s_)) for s_ in range(4)]))
    dproj = dproj3.reshape(T, NINP)

    gate_bwd = functools.partial(_vjp_of(_gate_math, 2), n_prim=2)
    d_ob, dproj = _pcall("gateb_bwd", gate_bwd, (T // BR,),
                         [(res["ob2"], (BR, 512), row(0)), (proj, (BR, 512), row(O_BZ // 512)), (d_yb, (BR, 512), row(0))],
                         [((T, 512), F32, (BR, 512), row(0))],
                         into=(dproj, [((BR, 512), lambda ref, i: rows_of(BR)(ref, i).at[:, O_BZ:O_BZ + 512])]))
    dqr, dkr, dv = _mla_bwd(res["qr"], res["kr"], res["v"], d_ob.reshape(batch, S, NH * VD), res["o_b"], res["l_b"])
    nrr = S // BR
    tab_row = lambda i: (i % nrr, 0)
    def rope_bwd(*args):
        d_q, d_kn, d_kpe, dgq, dgk = _vjp_of(_rope_math, 5)(*args, n_prim=8)
        return d_q, d_kn, dgq, dgk, d_kpe

    d_q, d_kn, g["mla_q_norm_g"], g["mla_k_norm_g"], dproj = _pcall(
        "rope_bwd", rope_bwd, (T // BR,),
        [(res["q"], (BR, NH * QKP), row(0)), (res["kn"], (BR, NH * QKP), row(0)), (proj, (BR, LANE), row(O_KPE // LANE)),
         (w["mla_q_norm_g"], (1, QKP), fix), (w["mla_k_norm_g"], (1, QKP), fix),
         (rope_c, (BR, QKP), tab_row), (rope_s1, (BR, QKP), tab_row), (rope_s2, (BR, QKP), tab_row),
         (dqr.reshape(T, NH * QKP), (BR, NH * QKP), row(0)), (dkr.reshape(T, NH * QKP), (BR, NH * QKP), row(0))],
        [((T, NH * QKP), BF16, (BR, NH * QKP), row(0))] * 2 + [((1, QKP), F32, (1, QKP), fix, True)] * 2,
        into=(dproj, [((BR, LANE), lambda ref, i: rows_of(BR)(ref, i).at[:, O_KPE:O_KPE + LANE])]))
    dv = dv.reshape(T, NH * VD)
    d_cqn = _mm("uq_dx", d_q, res["w_uq_p"], tb=True)
    d_ckvn = _mm("uk_dx", d_kn, res["w_uk"], tb=True)
    d_ckvn = _mm("uv_dx", dv, res["w_uv"], tb=True, add=d_ckvn)
    g["w_uq"] = _unpad_heads_uq(_mm("uq_dw", res["cqn"], d_q, ta=True, tk=T))
    g["w_ukv"] = _join_ukv(_mm("uk_dw", res["ckvn"], d_kn, ta=True, tk=T),
                           _mm("uv_dw", res["ckvn"], dv, ta=True, tk=T))
    def pre_bwd(*args):
        d_cq, d_ckv, dgq, dgkv = _vjp_of(_mla_pre_math, 4)(*args, n_prim=4)
        return dgq, dgkv, jnp.concatenate([d_cq, d_ckv], axis=1)

    g["q_a_norm_g"], g["kv_a_norm_g"], dproj = _pcall(
        "mla_pre_bwd", pre_bwd, (T // BR,),
        [(proj, (BR, QL), row(O_CQ // QL)), (proj, (BR, KVL), row(O_CKV // KVL)),
         (w["q_a_norm_g"], (1, QL), fix), (w["kv_a_norm_g"], (1, KVL), fix),
         (d_cqn, (BR, QL), row(0)), (d_ckvn, (BR, KVL), row(0))],
        [((1, QL), F32, (1, QL), fix, True), ((1, KVL), F32, (1, KVL), fix, True)],
        into=(dproj, [((BR, QL + KVL), lambda ref, i: rows_of(BR)(ref, i).at[:, O_CQ:O_CQ + QL + KVL])]))

    gq2 = jnp.tile(w["dil_q_norm_g"].reshape(NG, HD), (1, 2))
    gk2 = jnp.tile(w["dil_k_norm_g"].reshape(NG, HD), (1, 2))
    dproj3, dgq, dgk = _dilc_bwd(proj3, gq2, gk2, dil_tab, res["o_all"], res["l_all"],
                                 d_yc.reshape(batch, S, DWID), dproj.reshape(batch, S, NINP))
    dproj = dproj3.reshape(T, NINP)
    g["dil_q_norm_g"] = dgq[:, :HD] + dgq[:, HD:]
    g["dil_k_norm_g"] = dgk[:, :HD] + dgk[:, HD:]

    g["w_in_t"] = _mm("in_proj_dw", dproj, h, ta=True, tm=1024, tk=T)
    dep = after_dw(g) if after_dw is not None else None
    d_h = _mm("in_proj_dx", dproj, w["w_in_t"], dep=dep, b_words=True, tm=1024, tk=NINP // 4)
    dx, g["norm_g"] = _pcall("norm_bwd", _norm_bwd_math, (T // BR,),
                             [(x, (BR, D), row(0)), (w["norm_g"], (1, D), fix), (d_h, (BR, D), row(0)),
                              (dy, (BR, D), row(0))],
                             [((T, D), F32, (BR, D), row(0)), ((1, D), F32, (1, D), fix, True)])
    return dx, g


def _loss_math(y, t):
    e = y - t
    return e * (1.0 / D), 0.5 * jnp.sum(jnp.sum(e * e, axis=-1, keepdims=True) / D, axis=0, keepdims=True)


def _local_step(x, target, ws, batch):
    T = batch * S
    tabs = _rope_tables() + (_dil_slopes(),)
    saved = []
    y = x
    for l in range(NL):
        y, res = _layer_fwd(y, ws[l], tabs, batch)
        saved.append(res)
    row = lambda i: (i, 0)
    dy, loss = _pcall("loss", _loss_math, (T // BR,),
                      [(y, (BR, D), row), (target, (BR, D), row)],
                      [((T, D), F32, (BR, D), row), ((1, 1), F32, (1, 1), lambda i: (0, 0), True)])
    grads = [None] * NL
    for l in reversed(range(NL)):
        dy, grads[l] = _layer_bwd(dy, ws[l], saved[l], tabs, batch)
    return loss, dy, grads


ANY = pl.BlockSpec(memory_space=pl.ANY)
U32 = jnp.uint32
WSH = NIN // 4
WA = KPE_END
WB = WSH - WA
CWD = 512
PACK_ROWS = 1472
HW = PACK_W // 2


def _me():
    return lax.axis_index("x"), lax.axis_index("y"), lax.axis_index("c")


def _piece_rows(k):
    a = k * WSH + jnp.where(k > 0, NINP - NIN, 0)
    b = k * WSH + WA + (NINP - NIN)
    return ((0, pl.multiple_of(a, 8), WA), (WA, pl.multiple_of(b, 8), WB))


def _pack_words(lo, hi):
    ul = lax.bitcast_convert_type(lo.astype(BF16).astype(F32), U32)
    uh = lax.bitcast_convert_type(hi.astype(BF16).astype(F32), U32)
    w = jnp.bitwise_or(jnp.bitwise_and(uh, jnp.uint32(0xFFFF0000)), jnp.right_shift(ul, jnp.uint32(16)))
    return lax.bitcast_convert_type(w, F32)


def _unpack_words(w):
    w = lax.bitcast_convert_type(w, U32)
    lo = lax.bitcast_convert_type(jnp.left_shift(w, jnp.uint32(16)), F32)
    hi = lax.bitcast_convert_type(jnp.bitwise_and(w, jnp.uint32(0xFFFF0000)), F32)
    return lo, hi


def _all_gather(wc, sp):
    def body(w_ref, s_ref, ow_ref, os_ref, send_sems, recv_sems):
        x, y, c = _me()
        k_me = 2 * x + y
        sib = (x, y, 1 - c)
        chips = [(1 - x, y), (x, 1 - y), (1 - x, 1 - y)]
        wcols = lambda cc: pl.ds(pl.multiple_of(cc * (CWD // 2), LANE), CWD // 2)
        scols = lambda cc: pl.ds(pl.multiple_of(cc * HW, LANE), HW)

        def windows(k, cc):
            pcs = _piece_rows(k)
            return ([(w_ref.at[pl.ds(l0, n), wcols(cc)], ow_ref.at[pl.ds(p0, n), wcols(cc)]) for l0, p0, n in pcs]
                    + [(s_ref.at[:, scols(cc)], os_ref.at[k, :, scols(cc)])])

        def copy(i, src, dst, to):
            return pltpu.make_async_remote_copy(src_ref=src, dst_ref=dst, send_sem=send_sems.at[i],
                                                recv_sem=recv_sems.at[i], device_id=to, device_id_type=MESH)

        def own_windows():
            return ([(w_ref.at[pl.ds(l0, n)], ow_ref.at[pl.ds(p0, n)]) for l0, p0, n in _piece_rows(k_me)]
                    + [(s_ref, os_ref.at[k_me])])

        first = [copy(18 + i, src, dst, sib) for i, (src, dst) in enumerate(own_windows())]
        for j, (cx, cy) in enumerate(chips):
            for i, (src, dst) in enumerate(windows(k_me, c)):
                first.append(copy(3 * j + i, src, dst, (cx, cy, c)))
        for cp in first:
            cp.start()
        passed = []
        for j, (cx, cy) in enumerate(chips):
            for i, (_, dst) in enumerate(windows(2 * cx + cy, c)):
                copy(3 * j + i, dst, dst, (cx, cy, c)).wait_recv()
                cp = copy(9 + 3 * j + i, dst, dst, sib)
                cp.start()
                passed.append(cp)
        for j, (cx, cy) in enumerate(chips):
            for i, (_, dst) in enumerate(windows(2 * cx + cy, 1 - c)):
                copy(9 + 3 * j + i, dst, dst, sib).wait_recv()
        for i, (_, dst) in enumerate(own_windows()):
            copy(18 + i, dst, dst, sib).wait_recv()
        for cp in first + passed:
            cp.wait_send()

    return pl.pallas_call(
        body,
        out_shape=[jax.ShapeDtypeStruct((NINP, CWD), F32), jax.ShapeDtypeStruct((4, PACK_ROWS, PACK_W), BF16)],
        in_specs=[ANY, ANY], out_specs=[ANY, ANY],
        scratch_shapes=[pltpu.SemaphoreType.DMA((21,)), pltpu.SemaphoreType.DMA((21,))],
        name="weights_all_gather",
    )(wc, sp)


HBM = pl.BlockSpec(memory_space=pltpu.HBM)
SEM = pl.BlockSpec(memory_space=pltpu.SEMAPHORE)
EFFECT = pltpu.SideEffectType.DATAFLOW_SIDE_EFFECTING


def _in_hbm(a):
    return pltpu.with_memory_space_constraint(a, pltpu.HBM)


def _ag_windows(w_ref, s_ref, lw_ref, ls_ref, k, cc):
    wcols = pl.ds(pl.multiple_of(cc * (CWD // 2), LANE), CWD // 2)
    scols = pl.ds(pl.multiple_of(cc * HW, LANE), HW)
    return ([(w_ref.at[pl.ds(l0, n), wcols], lw_ref.at[pl.ds(p0, n), wcols]) for l0, p0, n in _piece_rows(k)]
            + [(s_ref.at[:, scols], ls_ref.at[k, :, scols])])


def _ag_ici_copies(w_ref, s_ref, lw_ref, ls_ref, send_sems, recv_sems):
    x, y, c = _me()
    mine, theirs = [], []
    for j, (cx, cy) in enumerate([(1 - x, y), (x, 1 - y), (1 - x, 1 - y)]):
        for i, ((src, dst), (_, got)) in enumerate(zip(_ag_windows(w_ref, s_ref, lw_ref, ls_ref, 2 * x + y, c),
                                                       _ag_windows(w_ref, s_ref, lw_ref, ls_ref, 2 * cx + cy, c))):
            mk = lambda s_, d_: pltpu.make_async_remote_copy(
                src_ref=s_, dst_ref=d_, send_sem=send_sems.at[3 * j + i], recv_sem=recv_sems.at[3 * j + i],
                device_id=(cx, cy, c), device_id_type=MESH)
            mine.append(mk(src, dst))
            theirs.append(mk(got, got))
    return mine, theirs


def _ag_ici_start(wc, sp, dep):
    def body(w_ref, s_ref, lw_ref, ls_ref, dep_ref, send_sems, recv_sems, w_thru, s_thru, lw_thru, ls_thru, token):
        mine, _ = _ag_ici_copies(w_ref, s_ref, lw_ref, ls_ref, send_sems, recv_sems)
        for cp in mine:
            cp.start()
        token[...] = jnp.zeros_like(token)

    return pl.pallas_call(
        body, name="weights_gather_start",
        out_shape=(pltpu.SemaphoreType.DMA((9,)), pltpu.SemaphoreType.DMA((9,)), pltpu.HBM(wc.shape, wc.dtype),
                   pltpu.HBM(sp.shape, sp.dtype), pltpu.HBM((NINP, CWD), F32), pltpu.HBM((4, PACK_ROWS, PACK_W), BF16),
                   jax.ShapeDtypeStruct((8, LANE), F32)),
        in_specs=(HBM, HBM, HBM, HBM, ANY),
        out_specs=(SEM, SEM, HBM, HBM, HBM, HBM, pl.BlockSpec(memory_space=pltpu.VMEM)),
        input_output_aliases={0: 2, 1: 3, 2: 4, 3: 5},
        compiler_params=pltpu.CompilerParams(has_side_effects=EFFECT),
    )(_in_hbm(wc), _in_hbm(sp), _in_hbm(lax.empty((NINP, CWD), F32)), _in_hbm(lax.empty((4, PACK_ROWS, PACK_W), BF16)), dep)


def _ag_ici_wait(send_sems, recv_sems, wc, sp, lw, ls, after):
    def body(w_ref, s_ref, lw_ref, ls_ref, send_sems, recv_sems, after_ref, w_dead, s_dead, lw_out, ls_out):
        mine, theirs = _ag_ici_copies(w_ref, s_ref, lw_ref, ls_ref, send_sems, recv_sems)
        for cp in mine:
            cp.wait_send()
        for cp in theirs:
            cp.wait_recv()

    out = pl.pallas_call(
        body, name="weights_gather_wait",
        out_shape=(pltpu.HBM(wc.shape, wc.dtype), pltpu.HBM(sp.shape, sp.dtype), pltpu.HBM(lw.shape, lw.dtype),
                   pltpu.HBM(ls.shape, ls.dtype)),
        in_specs=(HBM, HBM, HBM, HBM, SEM, SEM, ANY), out_specs=(HBM, HBM, HBM, HBM),
        input_output_aliases={0: 0, 1: 1, 2: 2, 3: 3},
        compiler_params=pltpu.CompilerParams(has_side_effects=EFFECT),
    )(wc, sp, lw, ls, send_sems, recv_sems, after)
    return out[2], out[3]


def _ag_finish(wc, sp, lw, ls):
    def body(w_ref, s_ref, lw_ref, ls_ref, ow_ref, os_ref, send_sems, recv_sems):
        x, y, c = _me()
        k_me = 2 * x + y
        sib = (x, y, 1 - c)
        chips = [(1 - x, y), (x, 1 - y), (1 - x, 1 - y)]

        def copy(i, src, dst):
            return pltpu.make_async_remote_copy(src_ref=src, dst_ref=dst, send_sem=send_sems.at[i],
                                                recv_sem=recv_sems.at[i], device_id=sib, device_id_type=MESH)

        def own_windows():
            return ([(w_ref.at[pl.ds(l0, n)], ow_ref.at[pl.ds(p0, n)]) for l0, p0, n in _piece_rows(k_me)]
                    + [(s_ref, os_ref.at[k_me])])

        out = [copy(9 + i, src, dst) for i, (src, dst) in enumerate(own_windows())]
        for j, (cx, cy) in enumerate(chips):
            landed = _ag_windows(w_ref, s_ref, lw_ref, ls_ref, 2 * cx + cy, c)
            for i, (_, dst) in enumerate(_ag_windows(w_ref, s_ref, ow_ref, os_ref, 2 * cx + cy, c)):
                out.append(copy(3 * j + i, landed[i][1], dst))
        for cp in out:
            cp.start()
        for j, (cx, cy) in enumerate(chips):
            for i, (_, dst) in enumerate(_ag_windows(w_ref, s_ref, ow_ref, os_ref, 2 * cx + cy, 1 - c)):
                copy(3 * j + i, dst, dst).wait_recv()
        for i, (_, dst) in enumerate(own_windows()):
            copy(9 + i, dst, dst).wait_recv()
        for cp in out:
            cp.wait_send()

    return pl.pallas_call(
        body,
        out_shape=[jax.ShapeDtypeStruct(lw.shape, lw.dtype), jax.ShapeDtypeStruct(ls.shape, ls.dtype)],
        in_specs=[ANY] * 4, out_specs=[ANY, ANY],
        input_output_aliases={2: 0, 3: 1},
        scratch_shapes=[pltpu.SemaphoreType.DMA((12,)), pltpu.SemaphoreType.DMA((12,))],
        name="weights_gather_finish",
    )(wc, sp, lw, ls)


def _rs_swap(gw, gs):
    def body(w_ref, s_ref, rw_ref, rs_ref, send_sems, recv_sems):
        x, y, c = _me()
        oc = 1 - c
        cps = [pltpu.make_async_remote_copy(src_ref=w_ref.at[:, pl.ds(pl.multiple_of(oc * (D // 2), LANE), D // 2)],
                                            dst_ref=rw_ref, send_sem=send_sems.at[0], recv_sem=recv_sems.at[0],
                                            device_id=(x, y, oc), device_id_type=MESH),
               pltpu.make_async_remote_copy(src_ref=s_ref.at[:, :, pl.ds(pl.multiple_of(oc * HW, LANE), HW)],
                                            dst_ref=rs_ref, send_sem=send_sems.at[1], recv_sem=recv_sems.at[1],
                                            device_id=(x, y, oc), device_id_type=MESH)]
        for cp in cps:
            cp.start()
        for cp in cps:
            cp.wait()

    return pl.pallas_call(
        body,
        out_shape=[jax.ShapeDtypeStruct((NINP, D // 2), F32), jax.ShapeDtypeStruct((4, PACK_ROWS, HW), F32)],
        in_specs=[ANY, ANY], out_specs=[ANY, ANY],
        scratch_shapes=[pltpu.SemaphoreType.DMA((2,)), pltpu.SemaphoreType.DMA((2,))],
        name="grads_sibling_swap",
    )(gw, gs)


def _rs_swap_copies(w_ref, s_ref, rw_ref, rs_ref, send_sems, recv_sems):
    x, y, c = _me()
    oc = 1 - c
    return [pltpu.make_async_remote_copy(src_ref=w_ref.at[:, pl.ds(pl.multiple_of(oc * (D // 2), LANE), D // 2)],
                                         dst_ref=rw_ref, send_sem=send_sems.at[0], recv_sem=recv_sems.at[0],
                                         device_id=(x, y, oc), device_id_type=MESH),
            pltpu.make_async_remote_copy(src_ref=s_ref.at[:, :, pl.ds(pl.multiple_of(oc * HW, LANE), HW)],
                                         dst_ref=rs_ref, send_sem=send_sems.at[1], recv_sem=recv_sems.at[1],
                                         device_id=(x, y, oc), device_id_type=MESH)]


def _rs_swap_start(gw, gs):
    def body(w_ref, s_ref, rw_ref, rs_ref, send_sems, recv_sems, w_thru, s_thru, rw_thru, rs_thru, token):
        for cp in _rs_swap_copies(w_ref, s_ref, rw_ref, rs_ref, send_sems, recv_sems):
            cp.start()
        token[...] = jnp.zeros_like(token)

    return pl.pallas_call(
        body, name="grads_swap_start",
        out_shape=(pltpu.SemaphoreType.DMA((2,)), pltpu.SemaphoreType.DMA((2,)), pltpu.HBM(gw.shape, gw.dtype),
                   pltpu.HBM(gs.shape, gs.dtype), pltpu.HBM((NINP, D // 2), F32), pltpu.HBM((4, PACK_ROWS, HW), F32),
                   jax.ShapeDtypeStruct((8, LANE), F32)),
        in_specs=(HBM, HBM, HBM, HBM),
        out_specs=(SEM, SEM, HBM, HBM, HBM, HBM, pl.BlockSpec(memory_space=pltpu.VMEM)),
        input_output_aliases={0: 2, 1: 3, 2: 4, 3: 5},
        compiler_params=pltpu.CompilerParams(has_side_effects=EFFECT),
    )(_in_hbm(gw), _in_hbm(gs), _in_hbm(lax.empty((NINP, D // 2), F32)), _in_hbm(lax.empty((4, PACK_ROWS, HW), F32)))


def _rs_swap_wait(send_sems, recv_sems, gw, gs, rw, rs, after):
    def body(w_ref, s_ref, rw_ref, rs_ref, send_sems, recv_sems, after_ref, w_out, s_out, rw_out, rs_out):
        for cp in _rs_swap_copies(w_ref, s_ref, rw_ref, rs_ref, send_sems, recv_sems):
            cp.wait()

    return pl.pallas_call(
        body, name="grads_swap_wait",
        out_shape=(pltpu.HBM(gw.shape, gw.dtype), pltpu.HBM(gs.shape, gs.dtype), pltpu.HBM(rw.shape, rw.dtype),
                   pltpu.HBM(rs.shape, rs.dtype)),
        in_specs=(HBM, HBM, HBM, HBM, SEM, SEM, ANY), out_specs=(HBM, HBM, HBM, HBM),
        input_output_aliases={0: 0, 1: 1, 2: 2, 3: 3},
        compiler_params=pltpu.CompilerParams(has_side_effects=EFFECT),
    )(gw, gs, rw, rs, send_sems, recv_sems, after)


SUM_BR = 512


def _rs_chip_sum_w(gw, rw, cidx):
    def body(c_ref, g_ref, r_ref, o_ref):
        s = g_ref[...] + r_ref[...]
        q = D // 8
        o_ref[...] = jnp.concatenate([_pack_words(s[:, 0:q], s[:, q:2 * q]),
                                      _pack_words(s[:, 2 * q:3 * q], s[:, 3 * q:4 * q])], axis=1)

    return pl.pallas_call(
        body,
        grid_spec=pltpu.PrefetchScalarGridSpec(
            num_scalar_prefetch=1, grid=(NINP // SUM_BR,),
            in_specs=[pl.BlockSpec((SUM_BR, D // 2), lambda i, cr: (i, cr[0])),
                      pl.BlockSpec((SUM_BR, D // 2), lambda i, cr: (i, 0))],
            out_specs=pl.BlockSpec((SUM_BR, D // 4), lambda i, cr: (i, 0))),
        out_shape=jax.ShapeDtypeStruct((NINP, D // 4), F32),
        name="grads_chip_sum_w",
        compiler_params=pltpu.CompilerParams(dimension_semantics=("arbitrary",), vmem_limit_bytes=VMEM_LIMIT),
    )(cidx, gw, rw)


def _rs_chip_sum_s(gs, rs, cidx):
    def body(c_ref, g_ref, r_ref, o_ref):
        o_ref[...] = (g_ref[...] + r_ref[...]).astype(BF16)

    return pl.pallas_call(
        body,
        grid_spec=pltpu.PrefetchScalarGridSpec(
            num_scalar_prefetch=1, grid=(4,),
            in_specs=[pl.BlockSpec((None, PACK_ROWS, HW), lambda j, cr: (j, 0, cr[0])),
                      pl.BlockSpec((None, PACK_ROWS, HW), lambda j, cr: (j, 0, 0))],
            out_specs=pl.BlockSpec((None, PACK_ROWS, HW), lambda j, cr: (j, 0, 0))),
        out_shape=jax.ShapeDtypeStruct((4, PACK_ROWS, HW), BF16),
        name="grads_chip_sum_s",
        compiler_params=pltpu.CompilerParams(dimension_semantics=("arbitrary",), vmem_limit_bytes=VMEM_LIMIT),
    )(cidx, gs, rs)


def _rs_exchange_copies(sw_ref, ss_ref, r2w_ref, r2s_ref, send_sems, recv_sems):
    x, y, c = _me()
    mine, theirs = [], []
    for j, (cx, cy) in enumerate([(1 - x, y), (x, 1 - y), (1 - x, 1 - y)]):
        def mk(i, src, dst):
            return pltpu.make_async_remote_copy(src_ref=src, dst_ref=dst, send_sem=send_sems.at[3 * j + i],
                                                recv_sem=recv_sems.at[3 * j + i], device_id=(cx, cy, c), device_id_type=MESH)
        for i, (l0, p0, n) in enumerate(_piece_rows(2 * cx + cy)):
            mine.append(mk(i, sw_ref.at[pl.ds(p0, n)], r2w_ref.at[j, pl.ds(l0, n)]))
            theirs.append(mk(i, r2w_ref.at[j, pl.ds(l0, n)], r2w_ref.at[j, pl.ds(l0, n)]))
        mine.append(mk(2, ss_ref.at[2 * cx + cy], r2s_ref.at[j]))
        theirs.append(mk(2, r2s_ref.at[j], r2s_ref.at[j]))
    return mine, theirs


def _rs_exchange_start(sw, ss, tag):
    def body(sw_ref, ss_ref, r2w_ref, r2s_ref, send_sems, recv_sems, sw_thru, ss_thru, r2w_thru, r2s_thru, token):
        mine, _ = _rs_exchange_copies(sw_ref, ss_ref, r2w_ref, r2s_ref, send_sems, recv_sems)
        for cp in mine:
            cp.start()
        token[...] = jnp.zeros_like(token)

    return pl.pallas_call(
        body, name="grads_exchange_start_" + tag,
        out_shape=(pltpu.SemaphoreType.DMA((9,)), pltpu.SemaphoreType.DMA((9,)), pltpu.HBM(sw.shape, sw.dtype),
                   pltpu.HBM(ss.shape, ss.dtype), pltpu.HBM((3, WSH, D // 4), F32), pltpu.HBM((3, PACK_ROWS, HW), BF16),
                   jax.ShapeDtypeStruct((8, LANE), F32)),
        in_specs=(HBM, HBM, HBM, HBM),
        out_specs=(SEM, SEM, HBM, HBM, HBM, HBM, pl.BlockSpec(memory_space=pltpu.VMEM)),
        input_output_aliases={0: 2, 1: 3, 2: 4, 3: 5},
        compiler_params=pltpu.CompilerParams(has_side_effects=EFFECT),
    )(_in_hbm(sw), _in_hbm(ss), _in_hbm(lax.empty((3, WSH, D // 4), F32)), _in_hbm(lax.empty((3, PACK_ROWS, HW), BF16)))


def _rs_exchange_wait(send_sems, recv_sems, sw, ss, r2w, r2s, after, tag):
    def body(sw_ref, ss_ref, r2w_ref, r2s_ref, send_sems, recv_sems, after_ref, sw_dead, ss_dead, r2w_out, r2s_out):
        mine, theirs = _rs_exchange_copies(sw_ref, ss_ref, r2w_ref, r2s_ref, send_sems, recv_sems)
        for cp in mine:
            cp.wait_send()
        for cp in theirs:
            cp.wait_recv()

    out = pl.pallas_call(
        body, name="grads_exchange_wait_" + tag,
        out_shape=(pltpu.HBM(sw.shape, sw.dtype), pltpu.HBM(ss.shape, ss.dtype), pltpu.HBM(r2w.shape, r2w.dtype),
                   pltpu.HBM(r2s.shape, r2s.dtype)),
        in_specs=(HBM, HBM, HBM, HBM, SEM, SEM, ANY), out_specs=(HBM, HBM, HBM, HBM),
        input_output_aliases={0: 0, 1: 1, 2: 2, 3: 3},
        compiler_params=pltpu.CompilerParams(has_side_effects=EFFECT),
    )(sw, ss, r2w, r2s, send_sems, recv_sems, after)
    return out[2], out[3]


def _rs_final_w(gw, rw, r2w, idx):
    q = D // 8

    def body(i_ref, g_ref, r_ref, p_ref, o_ref, gbuf, rbuf, sems):
        i = pl.program_id(0)
        k, c = i_ref[0], i_ref[1]
        cps = []
        for n_, (l0, p0, n) in enumerate(_piece_rows(k)):
            gcol = pl.ds(pl.multiple_of(c * (D // 2) + i * 2 * q, LANE), 2 * q)
            rcol = pl.ds(pl.multiple_of(i * 2 * q, LANE), 2 * q)
            cps.append(pltpu.make_async_copy(g_ref.at[pl.ds(p0, n), gcol], gbuf.at[pl.ds(l0, n)], sems.at[2 * n_]))
            cps.append(pltpu.make_async_copy(r_ref.at[pl.ds(p0, n), rcol], rbuf.at[pl.ds(l0, n)], sems.at[2 * n_ + 1]))
        for cp in cps:
            cp.start()
        for cp in cps:
            cp.wait()
        acc = gbuf[...] + rbuf[...]
        for j in range(3):
            lo, hi = _unpack_words(p_ref[j])
            acc = acc + jnp.concatenate([lo, hi], axis=1)
        o_ref[...] = acc

    return pl.pallas_call(
        body,
        grid_spec=pltpu.PrefetchScalarGridSpec(
            num_scalar_prefetch=1, grid=(2,),
            in_specs=[ANY, ANY, pl.BlockSpec((3, WSH, q), lambda i, ir: (0, 0, i))],
            out_specs=pl.BlockSpec((WSH, 2 * q), lambda i, ir: (0, i)),
            scratch_shapes=[pltpu.VMEM((WSH, 2 * q), F32), pltpu.VMEM((WSH, 2 * q), F32), pltpu.SemaphoreType.DMA((4,))]),
        out_shape=jax.ShapeDtypeStruct((WSH, D // 2), F32),
        name="grads_final_sum_w",
        compiler_params=pltpu.CompilerParams(dimension_semantics=("arbitrary",), vmem_limit_bytes=VMEM_LIMIT),
    )(idx, gw, rw, r2w)


def _rs_final_s(gs, rs, r2s, idx):
    def body(i_ref, g_ref, r_ref, p_ref, o_ref):
        acc = g_ref[...] + r_ref[...]
        for j in range(3):
            acc = acc + p_ref[j].astype(F32)
        o_ref[...] = acc

    return pl.pallas_call(
        body,
        grid_spec=pltpu.PrefetchScalarGridSpec(
            num_scalar_prefetch=1, grid=(1,),
            in_specs=[pl.BlockSpec((None, PACK_ROWS, HW), lambda i, ir: (ir[0], 0, ir[1])),
                      pl.BlockSpec((None, PACK_ROWS, HW), lambda i, ir: (ir[0], 0, 0)),
                      pl.BlockSpec((3, PACK_ROWS, HW), lambda i, ir: (0, 0, 0))],
            out_specs=pl.BlockSpec((PACK_ROWS, HW), lambda i, ir: (0, 0))),
        out_shape=jax.ShapeDtypeStruct((PACK_ROWS, HW), F32),
        name="grads_final_sum_s",
        compiler_params=pltpu.CompilerParams(dimension_semantics=("arbitrary",), vmem_limit_bytes=VMEM_LIMIT),
    )(idx, gs, rs, r2s)


def _rs_share(fw, fs):
    def body(w_ref, s_ref, ow_ref, os_ref, send_sems, recv_sems):
        x, y, c = _me()
        cps = [pltpu.make_async_remote_copy(src_ref=w_ref, dst_ref=ow_ref, send_sem=send_sems.at[0],
                                            recv_sem=recv_sems.at[0], device_id=(x, y, 1 - c), device_id_type=MESH),
               pltpu.make_async_remote_copy(src_ref=s_ref, dst_ref=os_ref, send_sem=send_sems.at[1],
                                            recv_sem=recv_sems.at[1], device_id=(x, y, 1 - c), device_id_type=MESH)]
        for cp in cps:
            cp.start()
        for cp in cps:
            cp.wait()

    return pl.pallas_call(
        body,
        out_shape=[jax.ShapeDtypeStruct((WSH, D // 2), F32), jax.ShapeDtypeStruct((PACK_ROWS, HW), F32)],
        in_specs=[ANY, ANY], out_specs=[ANY, ANY],
        scratch_shapes=[pltpu.SemaphoreType.DMA((2,)), pltpu.SemaphoreType.DMA((2,))],
        name="grads_share",
    )(fw, fs)


def _both_halves(mine, other, c):
    return jnp.where(c == 0, jnp.concatenate([mine, other], axis=1), jnp.concatenate([other, mine], axis=1))


def _rs_sums(gw, gs, rw, rs):
    x, y, c = _me()
    cidx = jnp.reshape(c, (1,)).astype(jnp.int32)
    return dict(gw=gw, gs=gs, rw=rw, rs=rs, sw=_rs_chip_sum_w(gw, rw, cidx), ss=_rs_chip_sum_s(gs, rs, cidx))


def _rs_begin(gw, gs):
    return _rs_sums(gw, gs, *_rs_swap(gw, gs))


def _rs_end(st, r2w, r2s):
    x, y, c = _me()
    idx = jnp.stack([2 * x + y, c]).astype(jnp.int32)
    fw = _rs_final_w(st["gw"], st["rw"], r2w, idx)
    fs = _rs_final_s(st["gs"], st["rs"], r2s, idx)
    ow, os_ = _rs_share(fw, fs)
    return _both_halves(fw, ow, c), _both_halves(fs, os_, c)


def _all_reduce_small(gs):
    rows = gs.shape[0]

    def body(g_ref, o_ref, buf, send_sems, recv_sems):
        x, y, c = _me()
        me = 4 * x + 2 * y + c
        buf[me] = g_ref[...]
        cps = []
        for r in range(1, 8):
            fx, fy, fc = (r >> 2) & 1, (r >> 1) & 1, r & 1
            px, py, pc = jnp.bitwise_xor(x, fx), jnp.bitwise_xor(y, fy), jnp.bitwise_xor(c, fc)
            cps.append((pltpu.make_async_remote_copy(
                src_ref=g_ref, dst_ref=buf.at[me], send_sem=send_sems.at[r - 1], recv_sem=recv_sems.at[r - 1],
                device_id=(px, py, pc), device_id_type=MESH), 4 * px + 2 * py + pc))
        for cp, _ in cps:
            cp.start()
        for r, (cp, peer) in enumerate(cps):
            pltpu.make_async_remote_copy(
                src_ref=g_ref, dst_ref=buf.at[peer], send_sem=send_sems.at[r], recv_sem=recv_sems.at[r],
                device_id=(x, y, c), device_id_type=MESH).wait_recv()
        for cp, _ in cps:
            cp.wait_send()
        acc = buf[0]
        for k in range(1, 8):
            acc = acc + buf[k]
        o_ref[...] = acc

    return pl.pallas_call(
        body,
        out_shape=jax.ShapeDtypeStruct((rows, LANE), F32),
        in_specs=[pl.BlockSpec(memory_space=pltpu.VMEM)],
        out_specs=pl.BlockSpec(memory_space=pltpu.VMEM),
        scratch_shapes=[pltpu.VMEM((8, rows, LANE), F32), pltpu.SemaphoreType.DMA((7,)), pltpu.SemaphoreType.DMA((7,))],
        name="small_grads_all_reduce",
    )(gs)


PACK_SPLIT = (("w_uq", 96, (QL, 192)), ("w_ukv", 64, (KVL, 256)),
              ("w_out_a", 256, (CW, 256)), ("w_out_b", 256, (CW, 256)), ("w_out_c", 256, (CW, 256)),
              ("w_o", 512, (256, D)))
MAT_ROWS = 1440
CONV_SHARD = 3 * 128


def _w_in_words(w_in_shard):
    t = w_in_shard.T
    return _pack_words(t[:, :CWD], t[:, CWD:])


def _pack_weights(wl):
    parts = [wl[n].astype(BF16).reshape(-1, PACK_W) for n, _, _ in PACK_SPLIT]
    cw = wl["conv_w"].reshape(-1)
    hi = cw.astype(BF16)
    r1 = cw - hi.astype(F32)
    mid = r1.astype(BF16)
    lo = (r1 - mid.astype(F32)).astype(BF16)
    cterms = jnp.pad(jnp.concatenate([hi, mid, lo]), (0, 3 * PACK_W - 3 * CONV_SHARD)).reshape(3, PACK_W)
    tail = jnp.pad(cterms, ((0, PACK_ROWS - MAT_ROWS - 3), (0, 0)))
    return jnp.concatenate(parts + [tail], axis=0)


def _unpack_weights(gath):
    out = {}
    r = 0
    for n, nrows, shp in PACK_SPLIT:
        t = gath[:, r:r + nrows].reshape((4,) + shp)
        r += nrows
        if n == "w_o":
            out[n] = t.reshape(4 * shp[0], shp[1])
        else:
            out[n] = t.transpose(1, 0, 2).reshape(shp[0], 4 * shp[1])
    ct = gath[:, r:r + 3].reshape(4, 3 * PACK_W)[:, :3 * CONV_SHARD].astype(F32).reshape(4, 3, CONV_SHARD)
    cw = (ct[:, 0] + ct[:, 1]) + ct[:, 2]
    out["conv_w"] = cw.reshape(4, 3, 128).transpose(1, 0, 2).reshape(3, CW)
    return out


def _pack_grads(g):
    parts = []
    for n, nrows, shp in PACK_SPLIT:
        t = g[n]
        if n == "w_o":
            t = t.reshape((4,) + shp)
        else:
            t = t.reshape(shp[0], 4, shp[1]).transpose(1, 0, 2)
        parts.append(t.reshape(4, nrows, PACK_W))
    cw = g["conv_w"].reshape(3, 4, 128).transpose(1, 0, 2).reshape(4, 1, CONV_SHARD)
    parts.append(jnp.pad(cw, ((0, 0), (0, PACK_ROWS - MAT_ROWS - 1), (0, PACK_W - CONV_SHARD))))
    return jnp.concatenate(parts, axis=1)


def _unpack_grads(red):
    out = {}
    r = 0
    for n, nrows, shp in PACK_SPLIT:
        out[n] = red[r:r + nrows].reshape(shp)
        r += nrows
    out["conv_w"] = red[r, :CONV_SHARD].reshape(3, 128)
    return out


SMALL_SIZES = (("norm_g", D), ("b_gate", 3 * D), ("conv_b", CW), ("q_a_norm_g", QL), ("kv_a_norm_g", KVL),
               ("mla_q_norm_g", QK), ("mla_k_norm_g", QK), ("dil_q_norm_g", NG * HD), ("dil_k_norm_g", NG * HD))
SMALL_ROWS = 88


def _pack_small(per_name):
    flat = jnp.concatenate([per_name[n].reshape(-1).astype(F32) for n, _ in SMALL_SIZES])
    return jnp.pad(flat, (0, SMALL_ROWS * LANE - flat.shape[0])).reshape(SMALL_ROWS, LANE)


def _unpack_small(packed, like):
    out = {}
    flat = packed.reshape(-1)
    r = 0
    for n, sz in SMALL_SIZES:
        out[n] = flat[r:r + NL * sz].reshape(like[n].shape)
        r += NL * sz
    return out


def _adamw_math(w, g, m, v):
    m = ADAM_B1 * m + (1.0 - ADAM_B1) * g
    v = ADAM_B2 * v + (1.0 - ADAM_B2) * jnp.square(g)
    m_hat = m / (1.0 - ADAM_B1 ** ADAM_STEP)
    v_hat = v / (1.0 - ADAM_B2 ** ADAM_STEP)
    delta = -ADAM_LR * (m_hat / (jnp.sqrt(v_hat) + ADAM_EPS) + ADAM_WD * w)
    return delta, m, v


def _adamw(name, w, g, m, v, br, bc=None):
    L, R, C = w.shape
    bc = C if bc is None else bc
    blk = lambda l, i, j: (l, i, j)
    return _pcall(name, _adamw_math, (L, R // br, C // bc), [(t, (None, br, bc), blk) for t in (w, g, m, v)],
                  [((L, R, C), F32, (None, br, bc), blk)] * 3)


ADAM_ROWS = {"w_uq": 256, "w_ukv": 128, "w_out_a": 512, "w_out_b": 512, "w_out_c": 512, "w_o": 256,
             "conv_w": 3}


def kernel(x, norm_g, w_in, b_gate, conv_w, conv_b, q_a_norm_g, w_uq, kv_a_norm_g, w_ukv, mla_q_norm_g, mla_k_norm_g, dil_q_norm_g, dil_k_norm_g, w_out_a, w_out_b, w_out_c, w_o, loss_target, m_norm_g, m_w_in, m_b_gate, m_conv_w, m_conv_b, m_q_a_norm_g, m_w_uq, m_kv_a_norm_g, m_w_ukv, m_mla_q_norm_g, m_mla_k_norm_g, m_dil_q_norm_g, m_dil_k_norm_g, m_w_out_a, m_w_out_b, m_w_out_c, m_w_o, v_norm_g, v_w_in, v_b_gate, v_conv_w, v_conv_b, v_q_a_norm_g, v_w_uq, v_kv_a_norm_g, v_w_ukv, v_mla_q_norm_g, v_mla_k_norm_g, v_dil_q_norm_g, v_dil_k_norm_g, v_w_out_a, v_w_out_b, v_w_out_c, v_w_o):
    W = dict(norm_g=norm_g, w_in=w_in, b_gate=b_gate, conv_w=conv_w, conv_b=conv_b, q_a_norm_g=q_a_norm_g, w_uq=w_uq,
             kv_a_norm_g=kv_a_norm_g, w_ukv=w_ukv, mla_q_norm_g=mla_q_norm_g, mla_k_norm_g=mla_k_norm_g,
             dil_q_norm_g=dil_q_norm_g, dil_k_norm_g=dil_k_norm_g, w_out_a=w_out_a, w_out_b=w_out_b, w_out_c=w_out_c,
             w_o=w_o)
    M = dict(norm_g=m_norm_g, w_in=m_w_in, b_gate=m_b_gate, conv_w=m_conv_w, conv_b=m_conv_b, q_a_norm_g=m_q_a_norm_g,
             w_uq=m_w_uq, kv_a_norm_g=m_kv_a_norm_g, w_ukv=m_w_ukv, mla_q_norm_g=m_mla_q_norm_g,
             mla_k_norm_g=m_mla_k_norm_g, dil_q_norm_g=m_dil_q_norm_g, dil_k_norm_g=m_dil_k_norm_g, w_out_a=m_w_out_a,
             w_out_b=m_w_out_b, w_out_c=m_w_out_c, w_o=m_w_o)
    V = dict(norm_g=v_norm_g, w_in=v_w_in, b_gate=v_b_gate, conv_w=v_conv_w, conv_b=v_conv_b, q_a_norm_g=v_q_a_norm_g,
             w_uq=v_w_uq, kv_a_norm_g=v_kv_a_norm_g, w_ukv=v_w_ukv, mla_q_norm_g=v_mla_q_norm_g,
             mla_k_norm_g=v_mla_k_norm_g, dil_q_norm_g=v_dil_q_norm_g, dil_k_norm_g=v_dil_k_norm_g, w_out_a=v_w_out_a,
             w_out_b=v_w_out_b, w_out_c=v_w_out_c, w_o=v_w_o)
    batch = x.shape[0]
    T = batch * S

    def layer_weights(l, cont, gath):
        full = _unpack_weights(gath)
        pad_qk = lambda t: jnp.pad(t, (0, QKP - QK)).reshape(1, QKP)
        full.update(
            w_in_t=cont,
            norm_g=norm_g[l].reshape(1, D), b_gate=b_gate[l].reshape(1, 3 * D), conv_b=conv_b[l].reshape(1, CW),
            q_a_norm_g=q_a_norm_g[l].reshape(1, QL), kv_a_norm_g=kv_a_norm_g[l].reshape(1, KVL),
            mla_q_norm_g=pad_qk(mla_q_norm_g[l]), mla_k_norm_g=pad_qk(mla_k_norm_g[l]),
            dil_q_norm_g=dil_q_norm_g[l].reshape(NG, 1, HD), dil_k_norm_g=dil_k_norm_g[l].reshape(NG, 1, HD))
        return full

    words = [_w_in_words(w_in[l]) for l in range(NL)]
    packs = [_pack_weights({n: W[n][l] for n in BIG[1:] + ("conv_w",)}) for l in range(NL)]
    tabs = _rope_tables() + (_dil_slopes(),)
    x2 = x.reshape(T, D)

    cont0, gath0 = _all_gather(words[0], packs[0])
    w0 = layer_weights(0, cont0, gath0)
    ag = _ag_ici_start(words[1], packs[1], gath0)
    w0["norm_g"] = w0["norm_g"] + ag[6][0:1, 0:1]
    y0, res0 = _layer_fwd(x2, w0, tabs, batch)
    lw, ls = _ag_ici_wait(ag[0], ag[1], ag[2], ag[3], ag[4], ag[5], y0)
    w1 = layer_weights(1, *_ag_finish(words[1], packs[1], lw, ls))
    y1, res1 = _layer_fwd(y0, w1, tabs, batch)

    row = lambda i: (i, 0)
    dy, loss = _pcall("loss", _loss_math, (T // BR,),
                      [(y1, (BR, D), row), (loss_target.reshape(T, D), (BR, D), row)],
                      [((T, D), F32, (BR, D), row), ((1, 1), F32, (1, 1), lambda i: (0, 0), True)])
    loss = lax.psum(loss[0, 0], ("x", "y", "c"))

    grads = [None] * NL
    dy, grads[1] = _layer_bwd(dy, w1, res1, tabs, batch)
    st = [None] * NL
    ex = [None] * NL
    sw1 = _rs_swap_start(grads[1]["w_in_t"], _pack_grads(grads[1]))
    w0["w_o"] = w0["w_o"] + sw1[6][0:1, 0:1].astype(BF16)

    def exchange_layer1(t):
        st[1] = _rs_sums(*_rs_swap_wait(*sw1[:6], t))
        ex[1] = _rs_exchange_start(st[1]["sw"], st[1]["ss"], "1")
        return ex[1][6]

    def start_layer0(g):
        st[0] = _rs_begin(g["w_in_t"], _pack_grads(g))
        ex[0] = _rs_exchange_start(st[0]["sw"], st[0]["ss"], "0")
        return ex[0][6]

    dx, grads[0] = _layer_bwd(dy, w0, res0, tabs, batch, after_dw=start_layer0, after_merge=exchange_layer1)
    grad_x = dx.reshape(batch, S, D)

    red = [None] * NL
    for l in (1, 0):
        r2w, r2s = _rs_exchange_wait(*ex[l][:6], dx, str(l))
        rw, rs = _rs_end(st[l], r2w, r2s)
        r = _unpack_grads(rs)
        r["w_in_t"] = rw
        red[l] = r
    G = {n: jnp.stack([red[l][n] for l in range(NL)]) for n in BIG[1:] + ("conv_w",)}
    g_in_t = jnp.stack([red[l]["w_in_t"] for l in range(NL)])
    G["w_in"] = jnp.swapaxes(g_in_t, 1, 2)
    small_g = {n: jnp.stack([grads[l][n].reshape(-1)[:sz] for l in range(NL)]) for n, sz in SMALL_SIZES}
    small_red = _all_reduce_small(_pack_small(small_g))
    G.update(_unpack_small(small_red, {n: W[n] for n in SMALL}))

    delta, new_m, new_v = {}, {}, {}
    for n in BIG[1:] + ("conv_w",):
        delta[n], new_m[n], new_v[n] = _adamw("adamw_" + n, W[n], G[n], M[n], V[n], ADAM_ROWS[n])
    tr = lambda t: jnp.swapaxes(t, 1, 2)
    delta["w_in"], new_m["w_in"], new_v["w_in"] = (
        tr(t) for t in _adamw("adamw_w_in", tr(w_in), g_in_t, tr(m_w_in), tr(v_w_in), WSH, LANE))
    sw, sm, sv = (_pack_small({n: t[n] for n in SMALL})[None] for t in (W, M, V))
    sd, snm, snv = _adamw("adamw_small", sw, small_red[None], sm, sv, SMALL_ROWS)
    like = {n: W[n] for n in SMALL}
    delta.update(_unpack_small(sd[0], like))
    new_m.update(_unpack_small(snm[0], like))
    new_v.update(_unpack_small(snv[0], like))

    return (loss, grad_x, *[G[n] for n in WEIGHTS], *[delta[n] for n in WEIGHTS],
            *[new_m[n] for n in WEIGHTS], *[new_v[n] for n in WEIGHTS])
```

```python
import functools

import numpy as np
import jax
import jax.numpy as jnp
from jax import lax
from jax.experimental import pallas as pl
from jax.experimental.pallas import tpu as pltpu

F32 = jnp.float32
BF16 = jnp.bfloat16

D = 1024
S = 2048
NL = 2
CW = 512
NH = 8
QL = 256
KVL = 128
NOPE = 64
ROPE = 32
VD = 64
QK = NOPE + ROPE
QKP = 128
ROPE_THETA = 10000.0
DIL = ((128, 1), (512, 4), (2048, 16))
NG = 3
DH = 8
HD = 64
DWID = DH * HD
QB = 128
EPS = 1e-6
NIN = 11168
NINP = 11264
O_A, O_CQ, O_CKV, O_KPE, O_BZ, O_DQ, O_DK, O_DV, O_CZ, O_G = 0, 2048, 2304, 2432, 2560, 3072, 4608, 6144, 7680, 8192
KPE_END = 2464
NEG = -1e30
MLA_SCALE = QK ** -0.5
DIL_SCALE = HD ** -0.5
LANE = 128
PACK_W = 512
VMEM_LIMIT = 48 * 1024 * 1024

ADAM_LR = 0.001
ADAM_B1 = 0.9
ADAM_B2 = 0.999
ADAM_EPS = 1e-08
ADAM_WD = 0.01
ADAM_STEP = 10

MESH = pl.DeviceIdType.MESH
BIG = ("w_in", "w_uq", "w_ukv", "w_out_a", "w_out_b", "w_out_c", "w_o")
SMALL = ("norm_g", "b_gate", "conv_b", "q_a_norm_g", "kv_a_norm_g", "mla_q_norm_g", "mla_k_norm_g",
         "dil_q_norm_g", "dil_k_norm_g")
WEIGHTS = ("norm_g", "w_in", "b_gate", "conv_w", "conv_b", "q_a_norm_g", "w_uq", "kv_a_norm_g", "w_ukv",
           "mla_q_norm_g", "mla_k_norm_g", "dil_q_norm_g", "dil_k_norm_g", "w_out_a", "w_out_b", "w_out_c", "w_o")


def _dot(a, b):
    return jnp.dot(a, b, preferred_element_type=F32)


def _dot_nt(a, b):
    return lax.dot_general(a, b, (((1,), (1,)), ((), ())), preferred_element_type=F32)


def _dot_tn(a, b):
    return lax.dot_general(a, b, (((0,), (0,)), ((), ())), preferred_element_type=F32)


def _grid_step(grid):
    step = pl.program_id(0)
    for a in range(1, len(grid)):
        step = step * grid[a] + pl.program_id(a)
    n = 1
    for g in grid:
        n *= g
    return step, n


def _write_windows(buf_ref, stages, sems, step, nsteps, puts):
    slot = step % 2
    for t, (v, dst) in enumerate(puts):
        cp = pltpu.make_async_copy(stages[t].at[slot], dst, sems.at[t, slot])

        @pl.when(step >= 2)
        def _():
            cp.wait()

        stages[t][slot] = v.astype(stages[t].dtype).reshape(stages[t].shape[1:])
        cp.start()

    @pl.when(step == nsteps - 1)
    def _():
        for t, (v, dst) in enumerate(puts):
            pltpu.make_async_copy(stages[t].at[slot], dst, sems.at[t, slot]).wait()
            if nsteps > 1:
                pltpu.make_async_copy(stages[t].at[1 - slot], dst, sems.at[t, 1 - slot]).wait()


def _pcall(name, fn, grid, ins, outs, into=None):
    n_in = len(ins)
    n_out = len(outs)
    acc_axis = len(grid) - 1
    is_acc = [len(o) > 4 and o[4] for o in outs]
    outs = [o[:4] for o in outs]
    targets = into[1] if into is not None else []
    n_t = len(targets)

    def body(*refs):
        vals = fn(*[r[...].astype(F32) for r in refs[:n_in]])
        if not isinstance(vals, (tuple, list)):
            vals = (vals,)
        o0 = n_in + (1 if n_t else 0)
        for k in range(n_out):
            r = refs[o0 + k]
            v = vals[k].astype(r.dtype).reshape(r.shape)
            if is_acc[k]:
                first = pl.program_id(acc_axis) == 0

                @pl.when(first)
                def _():
                    r[...] = v

                @pl.when(jnp.logical_not(first))
                def _():
                    r[...] += v
            else:
                r[...] = v
        if n_t:
            buf_ref = refs[o0 + n_out]
            stages = refs[o0 + n_out + 1:o0 + n_out + 1 + n_t]
            ids = [pl.program_id(a) for a in range(len(grid))]
            step, nsteps = _grid_step(grid)
            _write_windows(buf_ref, stages, refs[-1], step, nsteps,
                           [(vals[n_out + t], targets[t][1](buf_ref, *ids)) for t in range(n_t)])

    in_specs = [pl.BlockSpec(bs, im) for _, bs, im in ins]
    out_specs = [pl.BlockSpec(bs, im) for _, _, bs, im in outs]
    out_shape = [jax.ShapeDtypeStruct(sh, dt) for sh, dt, _, _ in outs]
    args = [a for a, _, _ in ins]
    extra = {}
    if n_t:
        buf = into[0]
        in_specs.append(pl.BlockSpec(memory_space=pl.ANY))
        out_specs.append(pl.BlockSpec(memory_space=pl.ANY))
        out_shape.append(jax.ShapeDtypeStruct(buf.shape, buf.dtype))
        args.append(buf)
        extra = dict(input_output_aliases={n_in: n_out},
                     scratch_shapes=[pltpu.VMEM((2,) + tuple(bs), buf.dtype) for bs, _ in targets]
                     + [pltpu.SemaphoreType.DMA((n_t, 2))])
    return pl.pallas_call(
        body,
        grid=grid,
        in_specs=in_specs,
        out_specs=out_specs,
        out_shape=out_shape,
        name=name,
        compiler_params=pltpu.CompilerParams(
            dimension_semantics=("arbitrary",) * len(grid), vmem_limit_bytes=VMEM_LIMIT),
        **extra,
    )(*args)


def _mm(name, a, b, *, ta=False, tb=False, out_dtype=F32, add=None, dep=None, b_words=False, tm=2048, tn=1024, tk=1024):
    if ta:
        K, M = a.shape
    else:
        M, K = a.shape
    bshape = (b.shape[0], 2 * b.shape[1]) if b_words else b.shape
    if tb:
        N, K2 = bshape
    else:
        K2, N = bshape
    assert K == K2, (name, a.shape, b.shape)
    tm, tn, tk = min(tm, M), min(tn, N), min(tk, K)
    assert M % tm == 0 and N % tn == 0 and K % tk == 0, (name, M, N, K)
    nk = K // tk
    dims = (((0 if ta else 1,), (1 if tb else 0,)), ((), ()))
    a_spec = pl.BlockSpec((tk, tm), lambda j, i, k: (k, i)) if ta else pl.BlockSpec((tm, tk), lambda j, i, k: (i, k))
    bw = 2 if b_words else 1
    assert not b_words or (tk if tb else tn) == bshape[1]
    b_spec = (pl.BlockSpec((tn, tk // bw), lambda j, i, k: (j, k)) if tb
              else pl.BlockSpec((tk, tn // bw), lambda j, i, k: (k, j)))
    o_spec = pl.BlockSpec((tm, tn), lambda j, i, k: (i, j))
    has_add = add is not None
    n_in = 2 + has_add + (dep is not None)

    def body(*refs):
        a_ref, b_ref = refs[0], refs[1]
        add_ref = refs[2] if has_add else None
        o_ref = refs[n_in]
        bb = b_ref[...]
        if b_words:
            lo, hi = _unpack_words(bb)
            first = (pl.program_id(0) * tn) if tb else (pl.program_id(2) * tk)
            r = first + lax.broadcasted_iota(jnp.int32, lo.shape, 0)
            pad = jnp.logical_and(r >= KPE_END, r < KPE_END + NINP - NIN)
            bb = jnp.concatenate([jnp.where(pad, 0.0, lo), jnp.where(pad, 0.0, hi)], axis=1)
        p = lax.dot_general(a_ref[...].astype(BF16), bb.astype(BF16), dims, preferred_element_type=F32)
        if nk == 1:
            if has_add:
                p = p + add_ref[...]
            o_ref[...] = p.astype(out_dtype)
        else:
            acc = refs[-1]
            k = pl.program_id(2)

            @pl.when(k == 0)
            def _():
                acc[...] = p

            @pl.when(k > 0)
            def _():
                acc[...] += p

            @pl.when(k == nk - 1)
            def _():
                r = acc[...]
                if has_add:
                    r = r + add_ref[...]
                o_ref[...] = r.astype(out_dtype)

    in_specs = [a_spec, b_spec] + ([o_spec] if has_add else []) + ([pl.BlockSpec(memory_space=pl.ANY)] if dep is not None else [])
    args = [a, b] + ([add] if has_add else []) + ([dep] if dep is not None else [])
    return pl.pallas_call(
        body,
        grid=(N // tn, M // tm, nk),
        in_specs=in_specs,
        out_specs=o_spec,
        out_shape=jax.ShapeDtypeStruct((M, N), out_dtype),
        scratch_shapes=[pltpu.VMEM((tm, tn), F32)] if nk > 1 else [],
        name=name,
        compiler_params=pltpu.CompilerParams(
            dimension_semantics=("arbitrary", "arbitrary", "arbitrary"), vmem_limit_bytes=VMEM_LIMIT),
    )(*args)


def _vjp_of(f, n_diff):
    def g(*args, n_prim):
        prim = args[:n_diff]
        consts = args[n_diff:n_prim]
        cts = args[n_prim:]
        _, pull = jax.vjp(lambda *p: f(*p, *consts), *prim)
        out = jax.eval_shape(lambda *p: f(*p, *consts), *prim)
        if isinstance(out, (tuple, list)):
            cts = tuple(c.astype(o.dtype) for c, o in zip(cts, out))
        else:
            cts = cts[0].astype(out.dtype)
        return pull(cts)
    return g


def _rms(x, g, n=None):
    n = x.shape[-1] if n is None else n
    ms = jnp.sum(x * x, axis=-1, keepdims=True) / n
    return x * lax.rsqrt(ms + EPS) * g


def _silu(z):
    return z * jax.nn.sigmoid(z)


def _roll_rows(u, k):
    n = u.shape[0]
    r = pltpu.roll(u, k % n, 0)
    t = lax.broadcasted_iota(jnp.int32, u.shape, 0)
    if k > 0:
        return jnp.where(t >= k, r, 0.0)
    return jnp.where(t < n + k, r, 0.0)


@functools.partial(jax.custom_vjp, nondiff_argnums=(1,))
def _shift(u, k):
    return _roll_rows(u, k)


def _shift_fwd(u, k):
    return _roll_rows(u, k), None


def _shift_bwd(k, _, g):
    return (_roll_rows(g, -k),)


_shift.defvjp(_shift_fwd, _shift_bwd)


@functools.partial(jax.custom_vjp, nondiff_argnums=(1,))
def _lane_roll(u, k):
    return pltpu.roll(u, k % LANE, 1)


def _lane_roll_fwd(u, k):
    return pltpu.roll(u, k % LANE, 1), None


def _lane_roll_bwd(k, _, g):
    return (pltpu.roll(g, (-k) % LANE, 1),)


_lane_roll.defvjp(_lane_roll_fwd, _lane_roll_bwd)


def _conv_math(ab, ac, ax, az, cw, cb):
    u = ac * ax
    conv = cb + _shift(u, 2) * cw[0:1] + _shift(u, 1) * cw[1:2] + u * cw[2:3]
    return ab * conv * _silu(az)


def _mla_pre_math(cq, ckv, gq, gkv):
    return _rms(cq, gq), _rms(ckv, gkv)


def _rope_math(q, kn, kpe, gq, gk, c, s1, s2):
    lane = lax.broadcasted_iota(jnp.int32, kpe.shape, 1)
    pe = _lane_roll(jnp.where(lane < ROPE, kpe, 0.0), NOPE)

    def one(t, g):
        tn = _rms(t, g, QK)
        return tn * c + _lane_roll(tn, -16) * s1 + _lane_roll(tn, 16) * s2

    qs, ks = [], []
    for h in range(NH):
        sl = slice(h * QKP, (h + 1) * QKP)
        qs.append(one(q[:, sl], gq))
        ks.append(one(kn[:, sl] + pe, gk))
    return jnp.concatenate(qs, axis=1), jnp.concatenate(ks, axis=1)


def _gate_math(o, z):
    return o * _silu(z)


def _mergec_math(o0, o1, o2, l0, l1, l2, cz):
    m = lax.stop_gradient(jnp.maximum(jnp.maximum(l0, l1), l2))
    e0, e1, e2 = jnp.exp(l0 - m), jnp.exp(l1 - m), jnp.exp(l2 - m)
    den = e0 + e1 + e2
    oc = (e0 / den) * o0 + (e1 / den) * o1 + (e2 / den) * o2
    return oc * _silu(cz)


def _merge_math(g0, g1, g2, b0, b1, b2, pa, pb, pc):
    return (jax.nn.sigmoid(g0 + b0) * pa + jax.nn.sigmoid(g1 + b1) * pb) + jax.nn.sigmoid(g2 + b2) * pc


MLA_T = 256
MLA_UNROLL = True


def _mla_fwd(q, k, v):
    B = q.shape[0]
    T = MLA_T
    NB = S // T

    def body(q_ref, k_ref, v_ref, o_ref, l_ref):
        row = lax.broadcasted_iota(jnp.int32, (T, T), 0)
        col = lax.broadcasted_iota(jnp.int32, (T, T), 1)
        lo = _lo_mask((T, LANE))

        for qi in range(NB):
            qb = q_ref[qi * T:(qi + 1) * T, :]

            def step(j, carry, diagonal):
                m, l, acc = carry
                off = pl.multiple_of(j * T, T)
                kb = k_ref[pl.ds(off, T), :]
                vb = v_ref[pl.ds(off, T), :]
                ss = []
                for e in (0, 1):
                    se = _dot_nt(qb[:, e * QKP:(e + 1) * QKP], kb[:, e * QKP:(e + 1) * QKP]) * MLA_SCALE
                    ss.append(jnp.where(col <= row, se, NEG) if diagonal else se)
                s = jnp.concatenate(ss, axis=0)
                m_new = jnp.maximum(m, jnp.max(s, axis=-1, keepdims=True))
                a = jnp.exp(m - m_new)
                p = jnp.exp(s - m_new)
                l = a * l + jnp.sum(p, axis=-1, keepdims=True)
                acc = a * acc + _dot(p.astype(BF16), vb)
                return m_new, l, acc

            init = (jnp.full((2 * T, 1), NEG, F32), jnp.zeros((2 * T, 1), F32), jnp.zeros((2 * T, LANE), F32))
            carry = lax.fori_loop(0, qi, functools.partial(step, diagonal=False), init, unroll=MLA_UNROLL)
            m, l, acc = step(qi, carry, True)
            o = acc / l
            lse = m + jnp.log(l)
            o_ref[qi * T:(qi + 1) * T, :] = jnp.where(lo, o[:T], o[T:])
            l_ref[qi * T:(qi + 1) * T, :] = jnp.where(lo, lse[:T], lse[T:])

    def spec(w):
        return pl.BlockSpec((None, S, w), lambda b, hp: (b, 0, hp))

    return pl.pallas_call(
        body,
        grid=(B, NH // 2),
        in_specs=[spec(2 * QKP), spec(2 * QKP), spec(LANE)],
        out_specs=[spec(LANE), spec(LANE)],
        out_shape=[jax.ShapeDtypeStruct((B, S, NH * VD), F32)] * 2,
        name="mla_attn_fwd",
        compiler_params=pltpu.CompilerParams(dimension_semantics=("arbitrary",) * 2, vmem_limit_bytes=VMEM_LIMIT),
    )(q, k, v)


def _mla_bwd(q, k, v, do, o, lse):
    B = q.shape[0]
    T = MLA_T
    NB = S // T

    def body(q_ref, k_ref, v_ref, do_ref, o_ref, l_ref, dq_ref, dk_ref, dv_ref, delta_ref, dqt_ref):
        delta_ref[...] = _head_sum(do_ref[...] * o_ref[...])
        row = lax.broadcasted_iota(jnp.int32, (T, T), 0)
        col = lax.broadcasted_iota(jnp.int32, (T, T), 1)
        lo = _lo_mask((T, LANE))
        tn_t = (((0,), (1,)), ((), ()))

        for j in range(NB):
            krows = slice(j * T, (j + 1) * T)
            kb = k_ref[krows, :]
            vb = v_ref[krows, :]
            dkt = [jnp.zeros((QKP, T), F32), jnp.zeros((QKP, T), F32)]
            dvt = jnp.zeros((LANE, T), F32)
            for i in range(j, NB):
                qrows = slice(i * T, (i + 1) * T)
                qb = q_ref[qrows, :]
                do2 = _stack_heads(do_ref[qrows, :], lo).astype(BF16)
                lb = l_ref[qrows, :]
                db = delta_ref[qrows, :]
                dp2 = _dot_nt(do2, vb)
                ps = []
                for e in (0, 1):
                    cols = slice(e * QKP, (e + 1) * QKP)
                    qe, ke = qb[:, cols], kb[:, cols]
                    s = _dot_nt(qe, ke) * MLA_SCALE
                    if i == j:
                        s = jnp.where(col <= row, s, NEG)
                    p = jnp.exp(s - lb[:, e * HD:e * HD + 1])
                    ps.append(p.astype(BF16))
                    ds = (p * (dp2[e * T:(e + 1) * T] - db[:, e * HD:e * HD + 1]) * MLA_SCALE).astype(BF16)
                    dkt[e] = dkt[e] + _dot_tn(qe, ds)
                    dq_t = lax.dot_general(ke, ds, tn_t, preferred_element_type=F32)
                    if j == 0:
                        dqt_ref[e, :, qrows] = dq_t
                    else:
                        dqt_ref[e, :, qrows] += dq_t
                dvt = dvt + _dot_tn(do2, jnp.concatenate(ps, axis=0))
            dk_ref[krows, 0:QKP] = dkt[0].T
            dk_ref[krows, QKP:2 * QKP] = dkt[1].T
            dv_ref[krows, :] = dvt.T
        dq_ref[:, 0:QKP] = dqt_ref[0].T
        dq_ref[:, QKP:2 * QKP] = dqt_ref[1].T

    def spec(w):
        return pl.BlockSpec((None, S, w), lambda b, hp: (b, 0, hp))

    return pl.pallas_call(
        body,
        grid=(B, NH // 2),
        in_specs=[spec(2 * QKP), spec(2 * QKP), spec(LANE), spec(LANE), spec(LANE), spec(LANE)],
        out_specs=[spec(2 * QKP), spec(2 * QKP), spec(LANE)],
        out_shape=[jax.ShapeDtypeStruct((B, S, NH * QKP), F32), jax.ShapeDtypeStruct((B, S, NH * QKP), F32),
                   jax.ShapeDtypeStruct((B, S, NH * VD), F32)],
        scratch_shapes=[pltpu.VMEM((S, LANE), F32), pltpu.VMEM((2, QKP, S), F32)],
        name="mla_attn_bwd",
        compiler_params=pltpu.CompilerParams(dimension_semantics=("arbitrary",) * 2, vmem_limit_bytes=VMEM_LIMIT),
    )(q, k, v, do, o, lse)


def _lo_mask(shape):
    return lax.broadcasted_iota(jnp.int32, shape, len(shape) - 1) < HD


def _head_sum(u):
    r = lax.broadcasted_iota(jnp.int32, (LANE, LANE), 0) < HD
    c = lax.broadcasted_iota(jnp.int32, (LANE, LANE), 1) < HD
    ones = jnp.where(r == c, 1.0, 0.0).astype(BF16)
    hi = u.astype(BF16)
    lo = (u - hi.astype(F32)).astype(BF16)
    return _dot(hi, ones) + _dot(lo, ones)


def _head_sum_1(u):
    r = lax.broadcasted_iota(jnp.int32, (LANE, LANE), 0) < HD
    c = lax.broadcasted_iota(jnp.int32, (LANE, LANE), 1) < HD
    return _dot(u.astype(BF16), jnp.where(r == c, 1.0, 0.0).astype(BF16))


def _rms2_scale(x):
    return lax.rsqrt(_head_sum(x * x) / HD + EPS)


def _rms2(x, g):
    return x * _rms2_scale(x) * g


def _rms2_bwd(x, r, g, dy):
    xn = x * r
    t = dy * g
    dx = r * (t - xn * (_head_sum_1(xn * t) * (1.0 / HD)))
    return dx, jnp.sum(dy * xn, axis=0, keepdims=True)


def _dil_bias(t_ref, gi, d):
    qq = lax.broadcasted_iota(jnp.int32, (QB, QB), 0)
    kk = lax.broadcasted_iota(jnp.int32, (QB, QB), 1)
    jc = (qq - kk).astype(F32)
    rows = []
    for e in (0, 1):
        sl = t_ref[2 * gi + e:2 * gi + e + 1, :] * float(d)
        bp = jnp.where(kk >= qq, -sl * (jc + float(QB)), NEG)
        bc = jnp.where(kk <= qq, -sl * jc, NEG)
        rows.append(jnp.concatenate([bp, bc], axis=1))
    return jnp.concatenate(rows, axis=0)


def _dil_rows(cur, d):
    return pl.ds(cur, QB, stride=d) if d > 1 else pl.ds(pl.multiple_of(cur, QB), QB)


def _dil_walk(d, block, full):
    if d == 1:
        block(0, None)

        def body(i, c):
            block(i * QB, (i - 1) * QB)
            return c
        lax.fori_loop(1, S // QB, body, 0, unroll=True if full else 5)
    elif d == 16:
        def body(r, c):
            block(r, None)
            return c
        lax.fori_loop(0, d, body, 0, unroll=True if full else 4)
    else:
        nb = S // d // QB

        def cls(r, c):
            block(r, None)

            def body(i, c2):
                block(r + i * QB * d, r + (i - 1) * QB * d)
                return c2
            lax.fori_loop(1, nb, body, 0, unroll=True)
            return c
        lax.fori_loop(0, d, cls, 0, unroll=full)


def _stack_heads(x, lo):
    return jnp.concatenate([jnp.where(lo, x, 0.0), jnp.where(lo, 0.0, x)], axis=0)


def _dilc_fwd(proj3, gq, gk, tab):
    B = proj3.shape[0]

    def body(q_ref, k_ref, v_ref, cz_ref, gq_ref, gk_ref, t_ref, y_ref, o_ref, l_ref, qs, ks, vs):
        g = pl.program_id(2)
        lo = _lo_mask((QB, LANE))

        def group(gi):
            d = DIL[gi][1]
            qs[...] = _rms2(q_ref[...].astype(F32), gq_ref[gi:gi + 1, :])
            ks[...] = _rms2(k_ref[...].astype(F32), gk_ref[gi:gi + 1, :])
            vs[...] = v_ref[...].astype(F32)
            bias = _dil_bias(t_ref, gi, d)

            def block(cur, prev):
                rows = _dil_rows(cur, d)
                q2 = _stack_heads(qs[rows, :], lo).astype(BF16)
                kc, vc = ks[rows, :], vs[rows, :]
                if prev is None:
                    kcat, vcat, b = kc, vc, bias[:, QB:]
                else:
                    prow = _dil_rows(prev, d)
                    kcat = jnp.concatenate([ks[prow, :], kc], axis=0)
                    vcat = jnp.concatenate([vs[prow, :], vc], axis=0)
                    b = bias
                s = _dot_nt(q2, kcat.astype(BF16)) * DIL_SCALE + b
                m = jnp.max(s, axis=-1, keepdims=True)
                p = jnp.exp(s - m)
                l = jnp.sum(p, axis=-1, keepdims=True)
                o = _dot(p.astype(BF16), vcat.astype(BF16)) / l
                lse = m + jnp.log(l)
                o_ref[gi, rows, :] = jnp.where(lo, o[:QB], o[QB:])
                l_ref[gi, rows, :] = jnp.where(lo, lse[:QB], lse[QB:])

            _dil_walk(d, block, True)

        for gi in range(NG):
            pl.when(g == gi)(functools.partial(group, gi))

        @pl.when(g == NG - 1)
        def _():
            y_ref[...] = _mergec_math(o_ref[0], o_ref[1], o_ref[2], l_ref[0], l_ref[1], l_ref[2],
                                      cz_ref[...].astype(F32)).astype(BF16)

    def col(base):
        return pl.BlockSpec((None, S, LANE), lambda b, hp, g: (b, 0, base // LANE + 4 * g + hp))

    gspec = pl.BlockSpec((NG, LANE), lambda b, hp, g: (0, 0))
    saved = pl.BlockSpec((NG, None, S, LANE), lambda b, hp, g: (0, b, 0, hp))
    return pl.pallas_call(
        body,
        grid=(B, 4, NG),
        in_specs=[col(O_DQ), col(O_DK), col(O_DV),
                  pl.BlockSpec((None, S, LANE), lambda b, hp, g: (b, 0, O_CZ // LANE + hp)),
                  gspec, gspec, pl.BlockSpec((None, 8, LANE), lambda b, hp, g: (hp, 0, 0))],
        out_specs=[pl.BlockSpec((None, S, LANE), lambda b, hp, g: (b, 0, hp)), saved, saved],
        out_shape=[jax.ShapeDtypeStruct((B, S, DWID), BF16), jax.ShapeDtypeStruct((NG, B, S, DWID), F32),
                   jax.ShapeDtypeStruct((NG, B, S, DWID), F32)],
        scratch_shapes=[pltpu.VMEM((S, LANE), F32)] * 3,
        name="dil_mixer_fwd",
        compiler_params=pltpu.CompilerParams(dimension_semantics=("arbitrary",) * 3, vmem_limit_bytes=VMEM_LIMIT),
    )(proj3, proj3, proj3, proj3, gq, gk, tab)


MERGE_ROWS = 256


def _dilc_bwd(proj3, gq, gk, tab, o_all, l_all, d_yc, dproj3):
    B = proj3.shape[0]

    def body(q_ref, k_ref, v_ref, cz_ref, gq_ref, gk_ref, t_ref, o_ref, l_ref, dy_ref, dp_in,
             dp_out, dgq_out, dgk_out, qs, ks, vs, dos, dls, dqs, dks, dvs, rqs, rks, dczs,
             st_q, st_k, st_v, st_z, sems, sem_z):
        b_, hp, g = pl.program_id(0), pl.program_id(1), pl.program_id(2)
        col = lambda base: pl.ds(pl.multiple_of(base + hp * LANE, LANE), LANE)
        lo = _lo_mask((QB, LANE))

        @pl.when(jnp.logical_and(jnp.logical_and(pl.program_id(0) == 0, pl.program_id(1) == 0), g == 0))
        def _():
            dgq_out[...] = jnp.zeros((NG, LANE), F32)
            dgk_out[...] = jnp.zeros((NG, LANE), F32)

        @pl.when(g == 0)
        def _():
            def chunk(i, carry):
                rows = pl.ds(pl.multiple_of(i * MERGE_ROWS, MERGE_ROWS), MERGE_ROWS)
                ls = [l_ref[j, rows, :] for j in range(NG)]
                m = jnp.maximum(jnp.maximum(ls[0], ls[1]), ls[2])
                es = [jnp.exp(t - m) for t in ls]
                den = (es[0] + es[1]) + es[2]
                al = [e / den for e in es]
                os_ = [o_ref[j, rows, :] for j in range(NG)]
                oc = (al[0] * os_[0] + al[1] * os_[1]) + al[2] * os_[2]
                cz = cz_ref[rows, :].astype(F32)
                sg = jax.nn.sigmoid(cz)
                dy = dy_ref[rows, :]
                d_oc = dy * (cz * sg)
                dczs[rows, :] = (dy * oc * (sg * (1.0 + cz * (1.0 - sg)))).astype(BF16)
                ts = [_head_sum_1(d_oc * os_[j]) for j in range(NG)]
                tbar = (al[0] * ts[0] + al[1] * ts[1]) + al[2] * ts[2]
                for j in range(NG):
                    dos[j, rows, :] = al[j] * d_oc
                    dls[j, rows, :] = al[j] * (ts[j] - tbar)
                return carry
            lax.fori_loop(0, S // MERGE_ROWS, chunk, 0)
            _write_windows(dp_out, [st_z], sem_z, b_ * 4 + hp, B * 4, [(dczs[...], dp_out.at[b_, :, col(O_CZ)])])

        def group(gi):
            d = DIL[gi][1]
            xq, xk = q_ref[...].astype(F32), k_ref[...].astype(F32)
            rqs[...] = _rms2_scale(xq)
            rks[...] = _rms2_scale(xk)
            qs[...] = xq * rqs[...] * gq_ref[gi:gi + 1, :]
            ks[...] = xk * rks[...] * gk_ref[gi:gi + 1, :]
            vs[...] = v_ref[...].astype(F32)
            dks[...] = jnp.zeros((S, LANE), F32)
            dvs[...] = jnp.zeros((S, LANE), F32)
            bias = _dil_bias(t_ref, gi, d)

            def block(cur, prev):
                rows = _dil_rows(cur, d)
                q2 = _stack_heads(qs[rows, :], lo).astype(BF16)
                dob = dos[gi, rows, :]
                do2 = _stack_heads(dob, lo).astype(BF16)
                kc, vc = ks[rows, :], vs[rows, :]
                if prev is None:
                    kcat, vcat, b = kc, vc, bias[:, QB:]
                else:
                    prow = _dil_rows(prev, d)
                    kcat = jnp.concatenate([ks[prow, :], kc], axis=0)
                    vcat = jnp.concatenate([vs[prow, :], vc], axis=0)
                    b = bias
                kcat = kcat.astype(BF16)
                vcat = vcat.astype(BF16)
                lse_b = l_ref[gi, rows, :]
                corr_b = dls[gi, rows, :] - _head_sum_1(dob * o_ref[gi, rows, :])
                lse2 = jnp.concatenate([lse_b[:, 0:1], lse_b[:, HD:HD + 1]], axis=0)
                corr2 = jnp.concatenate([corr_b[:, 0:1], corr_b[:, HD:HD + 1]], axis=0)
                s = _dot_nt(q2, kcat) * DIL_SCALE + b
                p = jnp.exp(s - lse2)
                ds = (p * (_dot_nt(do2, vcat) + corr2) * DIL_SCALE).astype(BF16)
                dq2 = _dot(ds, kcat)
                dqs[rows, :] = jnp.where(lo, dq2[:QB], dq2[QB:])
                dk = _dot_tn(ds, q2)
                dv = _dot_tn(p.astype(BF16), do2)
                if prev is None:
                    dks[rows, :] += dk
                    dvs[rows, :] += dv
                else:
                    dks[prow, :] += dk[:QB]
                    dvs[prow, :] += dv[:QB]
                    dks[rows, :] += dk[QB:]
                    dvs[rows, :] += dv[QB:]

            _dil_walk(d, block, False)

            dxq, dgq = _rms2_bwd(q_ref[...].astype(F32), rqs[...], gq_ref[gi:gi + 1, :], dqs[...])
            dgq_out[gi:gi + 1, :] += dgq
            dxk, dgk = _rms2_bwd(k_ref[...].astype(F32), rks[...], gk_ref[gi:gi + 1, :], dks[...])
            dgk_out[gi:gi + 1, :] += dgk
            step, nsteps = _grid_step((B, 4, NG))
            _write_windows(dp_out, [st_q, st_k, st_v], sems, step, nsteps,
                           [(dxq, dp_out.at[b_, :, col(O_DQ + gi * DWID)]), (dxk, dp_out.at[b_, :, col(O_DK + gi * DWID)]),
                            (dvs[...], dp_out.at[b_, :, col(O_DV + gi * DWID)])])

        for gi in range(NG):
            pl.when(g == gi)(functools.partial(group, gi))

    def col(base):
        return pl.BlockSpec((None, S, LANE), lambda b, hp, g: (b, 0, base // LANE + 4 * g + hp))

    gspec = pl.BlockSpec((NG, LANE), lambda b, hp, g: (0, 0))
    saved = pl.BlockSpec((NG, None, S, LANE), lambda b, hp, g: (0, b, 0, hp))
    per_pair = pl.BlockSpec((None, S, LANE), lambda b, hp, g: (b, 0, hp))
    return pl.pallas_call(
        body,
        grid=(B, 4, NG),
        in_specs=[col(O_DQ), col(O_DK), col(O_DV),
                  pl.BlockSpec((None, S, LANE), lambda b, hp, g: (b, 0, O_CZ // LANE + hp)),
                  gspec, gspec, pl.BlockSpec((None, 8, LANE), lambda b, hp, g: (hp, 0, 0)),
                  saved, saved, per_pair, pl.BlockSpec(memory_space=pl.ANY)],
        out_specs=[pl.BlockSpec(memory_space=pl.ANY), gspec, gspec],
        out_shape=[jax.ShapeDtypeStruct(dproj3.shape, dproj3.dtype), jax.ShapeDtypeStruct((NG, LANE), F32),
                   jax.ShapeDtypeStruct((NG, LANE), F32)],
        input_output_aliases={10: 0},
        scratch_shapes=[pltpu.VMEM((S, LANE), F32)] * 3 + [pltpu.VMEM((NG, S, LANE), F32)] * 2
        + [pltpu.VMEM((S, LANE), F32)] * 5 + [pltpu.VMEM((S, LANE), BF16)] + [pltpu.VMEM((2, S, LANE), BF16)] * 4
        + [pltpu.SemaphoreType.DMA((3, 2)), pltpu.SemaphoreType.DMA((1, 2))],
        name="dil_mixer_bwd",
        compiler_params=pltpu.CompilerParams(dimension_semantics=("arbitrary",) * 3, vmem_limit_bytes=VMEM_LIMIT),
    )(proj3, proj3, proj3, proj3, gq, gk, tab, o_all, l_all, d_yc, dproj3)


def _dil_slopes():
    slopes = (2.0 ** (-8.0 * np.arange(1, NG * DH + 1, dtype=np.float32) / (NG * DH))).astype(np.float32).reshape(NG, DH)
    tab = np.zeros((4, 8, LANE), np.float32)
    for hp in range(4):
        for gi in range(NG):
            for e in (0, 1):
                tab[hp, 2 * gi + e, :] = slopes[gi, 2 * hp + e]
    return jnp.asarray(tab)


def _rope_tables():
    inv = ROPE_THETA ** (-jnp.arange(0, ROPE, 2, dtype=F32) / ROPE)
    ang = jnp.arange(S, dtype=F32)[:, None] * inv[None, :]
    cos, sin = jnp.cos(ang), jnp.sin(ang)
    z16 = jnp.zeros((S, 16), F32)
    c = jnp.concatenate([jnp.ones((S, NOPE), F32), cos, cos, jnp.zeros((S, 32), F32)], axis=1)
    s1 = jnp.concatenate([jnp.zeros((S, NOPE), F32), -sin, z16, jnp.zeros((S, 32), F32)], axis=1)
    s2 = jnp.concatenate([jnp.zeros((S, NOPE), F32), z16, sin, jnp.zeros((S, 32), F32)], axis=1)
    return c, s1, s2


def _pad_heads_uq(w):
    return jnp.pad(w.reshape(QL, NH, QK), ((0, 0), (0, 0), (0, QKP - QK))).reshape(QL, NH * QKP)


def _unpad_heads_uq(g):
    return g.reshape(QL, NH, QKP)[:, :, :QK].reshape(QL, NH * QK)


def _split_ukv(w):
    w3 = w.reshape(KVL, NH, NOPE + VD)
    uk = jnp.pad(w3[:, :, :NOPE], ((0, 0), (0, 0), (0, QKP - NOPE))).reshape(KVL, NH * QKP)
    return uk, w3[:, :, NOPE:].reshape(KVL, NH * VD)


def _join_ukv(guk, guv):
    return jnp.concatenate([guk.reshape(KVL, NH, QKP)[:, :, :NOPE], guv.reshape(KVL, NH, VD)],
                           axis=-1).reshape(KVL, NH * (NOPE + VD))


BR = 512
BRM = 256


def _layer_fwd(x, w, tabs, batch):
    T = batch * S
    rope_c, rope_s1, rope_s2, dil_tab = tabs
    res = {"x": x}
    row = lambda c: (lambda i: (i, c))
    fix = lambda i: (0, 0)

    h = _pcall("norm_fwd", _rms, (T // BR,),
               [(x, (BR, D), row(0)), (w["norm_g"], (1, D), fix)],
               [((T, D), BF16, (BR, D), row(0))])[0]
    proj = _mm("in_proj", h, w["w_in_t"], tb=True, out_dtype=BF16, b_words=True, tm=2048, tn=1024)
    res["h"], res["proj"] = h, proj
    proj3 = proj.reshape(batch, S, NINP)

    cblk = lambda s: (lambda j, b: (b, 0, 4 * s + j))
    y_a = _pcall("conv_fwd", _conv_math, (4, batch),
                 [(proj3, (None, S, LANE), cblk(0)), (proj3, (None, S, LANE), cblk(1)),
                  (proj3, (None, S, LANE), cblk(2)), (proj3, (None, S, LANE), cblk(3)),
                  (w["conv_w"], (3, LANE), lambda j, b: (0, j)), (w["conv_b"], (1, LANE), lambda j, b: (0, j))],
                 [((batch, S, CW), BF16, (None, S, LANE), lambda j, b: (b, 0, j))])[0].reshape(T, CW)
    res["y_a"] = y_a

    cqn, ckvn = _pcall("mla_pre_fwd", _mla_pre_math, (T // BR,),
                       [(proj, (BR, QL), row(O_CQ // QL)), (proj, (BR, KVL), row(O_CKV // KVL)),
                        (w["q_a_norm_g"], (1, QL), fix), (w["kv_a_norm_g"], (1, KVL), fix)],
                       [((T, QL), BF16, (BR, QL), row(0)), ((T, KVL), BF16, (BR, KVL), row(0))])
    w_uq_p = _pad_heads_uq(w["w_uq"])
    w_uk, w_uv = _split_ukv(w["w_ukv"])
    q = _mm("uq", cqn, w_uq_p, out_dtype=BF16)
    kn = _mm("uk", ckvn, w_uk, out_dtype=BF16)
    v = _mm("uv", ckvn, w_uv, out_dtype=BF16)
    nrr = S // BR
    tab_row = lambda i: (i % nrr, 0)
    qr, kr = _pcall("rope_fwd", _rope_math, (T // BR,),
                    [(q, (BR, NH * QKP), row(0)), (kn, (BR, NH * QKP), row(0)), (proj, (BR, LANE), row(O_KPE // LANE)),
                     (w["mla_q_norm_g"], (1, QKP), fix), (w["mla_k_norm_g"], (1, QKP), fix),
                     (rope_c, (BR, QKP), tab_row), (rope_s1, (BR, QKP), tab_row), (rope_s2, (BR, QKP), tab_row)],
                    [((T, NH * QKP), BF16, (BR, NH * QKP), row(0))] * 2)
    qr = qr.reshape(batch, S, NH * QKP)
    kr = kr.reshape(batch, S, NH * QKP)
    v = v.reshape(batch, S, NH * VD)
    o_b, l_b = _mla_fwd(qr, kr, v)
    ob2 = o_b.reshape(T, NH * VD)
    y_b = _pcall("gateb_fwd", _gate_math, (T // BR,),
                 [(ob2, (BR, 512), row(0)), (proj, (BR, 512), row(O_BZ // 512))],
                 [((T, 512), BF16, (BR, 512), row(0))])[0]
    res.update(cqn=cqn, ckvn=ckvn, q=q, kn=kn, qr=qr, kr=kr, v=v, o_b=o_b, l_b=l_b, ob2=ob2, y_b=y_b,
               w_uq_p=w_uq_p, w_uk=w_uk, w_uv=w_uv)

    gq2 = jnp.tile(w["dil_q_norm_g"].reshape(NG, HD), (1, 2))
    gk2 = jnp.tile(w["dil_k_norm_g"].reshape(NG, HD), (1, 2))
    y_c, o_all, l_all = _dilc_fwd(proj3, gq2, gk2, dil_tab)
    y_c = y_c.reshape(T, DWID)
    res.update(o_all=o_all, l_all=l_all, y_c=y_c)

    pa = _mm("out_a", y_a, w["w_out_a"], out_dtype=BF16)
    pb = _mm("out_b", y_b, w["w_out_b"], out_dtype=BF16)
    pc = _mm("out_c", y_c, w["w_out_c"], out_dtype=BF16)
    merged = _pcall("merge_fwd", _merge_math, (T // BRM,),
                    [(proj, (BRM, D), row(O_G // D + s)) for s in range(3)]
                    + [(w["b_gate"], (1, D), (lambda s: (lambda i: (0, s)))(s)) for s in range(3)]
                    + [(t, (BRM, D), row(0)) for t in (pa, pb, pc)],
                    [((T, D), BF16, (BRM, D), row(0))])[0]
    out = _mm("o_proj", merged, w["w_o"], add=x, tm=1024)
    res.update(pa=pa, pb=pb, pc=pc, merged=merged)
    return out, res


def _norm_bwd_math(x, g, dh, dy):
    _, pull = jax.vjp(_rms, x, g)
    dx, dg = pull(dh)
    return dx + dy, dg


def _layer_bwd(dy, w, res, tabs, batch, after_dw=None, after_merge=None):
    T = batch * S
    rope_c, rope_s1, rope_s2, dil_tab = tabs
    row = lambda c: (lambda i: (i, c))
    fix = lambda i: (0, 0)
    x, proj, h = res["x"], res["proj"], res["h"]
    proj3 = proj.reshape(batch, S, NINP)
    g = {}

    d_merged = _mm("o_proj_dx", dy, w["w_o"], tb=True)
    g["w_o"] = _mm("o_proj_dw", res["merged"], dy, ta=True, tm=1024, tk=2048)

    dproj = lax.empty((T, NINP), BF16)
    rows_of = lambda br: (lambda ref, i: ref.at[pl.ds(pl.multiple_of(i * br, br), br)])

    def merge_bwd(*args):
        dg0, dg1, dg2, db0, db1, db2, dpa, dpb, dpc = _vjp_of(_merge_math, 9)(*args, n_prim=9)
        return db0, db1, db2, dpa, dpb, dpc, jnp.concatenate([dg0, dg1, dg2], axis=1)

    db0, db1, db2, dpa, dpb, dpc, dproj = _pcall(
        "merge_bwd", merge_bwd, (T // BRM,),
        [(proj, (BRM, D), row(O_G // D + s)) for s in range(3)]
        + [(w["b_gate"], (1, D), (lambda s: (lambda i: (0, s)))(s)) for s in range(3)]
        + [(t, (BRM, D), row(0)) for t in (res["pa"], res["pb"], res["pc"])]
        + [(d_merged, (BRM, D), row(0))],
        [((1, D), F32, (1, D), fix, True)] * 3 + [((T, D), BF16, (BRM, D), row(0))] * 3,
        into=(dproj, [((BRM, 3 * D), lambda ref, i: rows_of(BRM)(ref, i).at[:, O_G:O_G + 3 * D])]))
    g["b_gate"] = jnp.concatenate([db0, db1, db2], axis=1)

    dep = after_merge(dpa) if after_merge is not None else None
    d_ya = _mm("out_a_dx", dpa, w["w_out_a"], tb=True, dep=dep)
    d_yb = _mm("out_b_dx", dpb, w["w_out_b"], tb=True)
    d_yc = _mm("out_c_dx", dpc, w["w_out_c"], tb=True)
    g["w_out_a"] = _mm("out_a_dw", res["y_a"], dpa, ta=True, tk=T)
    g["w_out_b"] = _mm("out_b_dw", res["y_b"], dpb, ta=True, tk=T)
    g["w_out_c"] = _mm("out_c_dw", res["y_c"], dpc, ta=True, tk=T)

    cblk = lambda s: (lambda j, b: (b, 0, 4 * s + j))
    oblk = lambda j, b: (b, 0, j)
    def conv_bwd(*args):
        d_ab, d_ac, d_ax, d_az, dcw, dcb = _vjp_of(_conv_math, 6)(*args, n_prim=6)
        return dcw, dcb, d_ab, d_ac, d_ax, d_az

    a_col = lambda s_: (lambda ref, j, b: ref.at[b, :, pl.ds(pl.multiple_of(O_A + s_ * CW + j * LANE, LANE), LANE)])
    g["conv_w"], g["conv_b"], dproj3 = _pcall(
        "conv_bwd", conv_bwd, (4, batch),
        [(proj3, (None, S, LANE), cblk(s)) for s in range(4)]
        + [(w["conv_w"], (3, LANE), lambda j, b: (0, j)), (w["conv_b"], (1, LANE), lambda j, b: (0, j)),
           (d_ya.reshape(batch, S, CW), (None, S, LANE), oblk)],
        [((3, CW), F32, (3, LANE), lambda j, b: (0, j), True), ((1, CW), F32, (1, LANE), lambda j, b: (0, j), True)],
        into=(dproj.reshape(batch, S, NINP), [((S, LANE), a_col(s_)) for s_ in range(4)]))
    dproj = dproj3.reshape(T, NINP)

    gate_bwd = functools.partial(_vjp_of(_gate_math, 2), n_prim=2)
    d_ob, dproj = _pcall("gateb_bwd", gate_bwd, (T // BR,),
                         [(res["ob2"], (BR, 512), row(0)), (proj, (BR, 512), row(O_BZ // 512)), (d_yb, (BR, 512), row(0))],
                         [((T, 512), F32, (BR, 512), row(0))],
                         into=(dproj, [((BR, 512), lambda ref, i: rows_of(BR)(ref, i).at[:, O_BZ:O_BZ + 512])]))
    dqr, dkr, dv = _mla_bwd(res["qr"], res["kr"], res["v"], d_ob.reshape(batch, S, NH * VD), res["o_b"], res["l_b"])
    nrr = S // BR
    tab_row = lambda i: (i % nrr, 0)
    def rope_bwd(*args):
        d_q, d_kn, d_kpe, dgq, dgk = _vjp_of(_rope_math, 5)(*args, n_prim=8)
        return d_q, d_kn, dgq, dgk, d_kpe

    d_q, d_kn, g["mla_q_norm_g"], g["mla_k_norm_g"], dproj = _pcall(
        "rope_bwd", rope_bwd, (T // BR,),
        [(res["q"], (BR, NH * QKP), row(0)), (res["kn"], (BR, NH * QKP), row(0)), (proj, (BR, LANE), row(O_KPE // LANE)),
         (w["mla_q_norm_g"], (1, QKP), fix), (w["mla_k_norm_g"], (1, QKP), fix),
         (rope_c, (BR, QKP), tab_row), (rope_s1, (BR, QKP), tab_row), (rope_s2, (BR, QKP), tab_row),
         (dqr.reshape(T, NH * QKP), (BR, NH * QKP), row(0)), (dkr.reshape(T, NH * QKP), (BR, NH * QKP), row(0))],
        [((T, NH * QKP), BF16, (BR, NH * QKP), row(0))] * 2 + [((1, QKP), F32, (1, QKP), fix, True)] * 2,
        into=(dproj, [((BR, LANE), lambda ref, i: rows_of(BR)(ref, i).at[:, O_KPE:O_KPE + LANE])]))
    dv = dv.reshape(T, NH * VD)
    d_cqn = _mm("uq_dx", d_q, res["w_uq_p"], tb=True)
    d_ckvn = _mm("uk_dx", d_kn, res["w_uk"], tb=True)
    d_ckvn = _mm("uv_dx", dv, res["w_uv"], tb=True, add=d_ckvn)
    g["w_uq"] = _unpad_heads_uq(_mm("uq_dw", res["cqn"], d_q, ta=True, tk=T))
    g["w_ukv"] = _join_ukv(_mm("uk_dw", res["ckvn"], d_kn, ta=True, tk=T),
                           _mm("uv_dw", res["ckvn"], dv, ta=True, tk=T))
    def pre_bwd(*args):
        d_cq, d_ckv, dgq, dgkv = _vjp_of(_mla_pre_math, 4)(*args, n_prim=4)
        return dgq, dgkv, jnp.concatenate([d_cq, d_ckv], axis=1)

    g["q_a_norm_g"], g["kv_a_norm_g"], dproj = _pcall(
        "mla_pre_bwd", pre_bwd, (T // BR,),
        [(proj, (BR, QL), row(O_CQ // QL)), (proj, (BR, KVL), row(O_CKV // KVL)),
         (w["q_a_norm_g"], (1, QL), fix), (w["kv_a_norm_g"], (1, KVL), fix),
         (d_cqn, (BR, QL), row(0)), (d_ckvn, (BR, KVL), row(0))],
        [((1, QL), F32, (1, QL), fix, True), ((1, KVL), F32, (1, KVL), fix, True)],
        into=(dproj, [((BR, QL + KVL), lambda ref, i: rows_of(BR)(ref, i).at[:, O_CQ:O_CQ + QL + KVL])]))

    gq2 = jnp.tile(w["dil_q_norm_g"].reshape(NG, HD), (1, 2))
    gk2 = jnp.tile(w["dil_k_norm_g"].reshape(NG, HD), (1, 2))
    dproj3, dgq, dgk = _dilc_bwd(proj3, gq2, gk2, dil_tab, res["o_all"], res["l_all"],
                                 d_yc.reshape(batch, S, DWID), dproj.reshape(batch, S, NINP))
    dproj = dproj3.reshape(T, NINP)
    g["dil_q_norm_g"] = dgq[:, :HD] + dgq[:, HD:]
    g["dil_k_norm_g"] = dgk[:, :HD] + dgk[:, HD:]

    g["w_in_t"] = _mm("in_proj_dw", dproj, h, ta=True, tm=1024, tk=T)
    dep = after_dw(g) if after_dw is not None else None
    d_h = _mm("in_proj_dx", dproj, w["w_in_t"], dep=dep, b_words=True, tm=1024, tk=NINP // 4)
    dx, g["norm_g"] = _pcall("norm_bwd", _norm_bwd_math, (T // BR,),
                             [(x, (BR, D), row(0)), (w["norm_g"], (1, D), fix), (d_h, (BR, D), row(0)),
                              (dy, (BR, D), row(0))],
                             [((T, D), F32, (BR, D), row(0)), ((1, D), F32, (1, D), fix, True)])
    return dx, g


def _loss_math(y, t):
    e = y - t
    return e * (1.0 / D), 0.5 * jnp.sum(jnp.sum(e * e, axis=-1, keepdims=True) / D, axis=0, keepdims=True)


def _local_step(x, target, ws, batch):
    T = batch * S
    tabs = _rope_tables() + (_dil_slopes(),)
    saved = []
    y = x
    for l in range(NL):
        y, res = _layer_fwd(y, ws[l], tabs, batch)
        saved.append(res)
    row = lambda i: (i, 0)
    dy, loss = _pcall("loss", _loss_math, (T // BR,),
                      [(y, (BR, D), row), (target, (BR, D), row)],
                      [((T, D), F32, (BR, D), row), ((1, 1), F32, (1, 1), lambda i: (0, 0), True)])
    grads = [None] * NL
    for l in reversed(range(NL)):
        dy, grads[l] = _layer_bwd(dy, ws[l], saved[l], tabs, batch)
    return loss, dy, grads


ANY = pl.BlockSpec(memory_space=pl.ANY)
U32 = jnp.uint32
WSH = NIN // 4
WA = KPE_END
WB = WSH - WA
CWD = 512
PACK_ROWS = 1472
HW = PACK_W // 2


def _me():
    return lax.axis_index("x"), lax.axis_index("y"), lax.axis_index("c")


def _piece_rows(k):
    a = k * WSH + jnp.where(k > 0, NINP - NIN, 0)
    b = k * WSH + WA + (NINP - NIN)
    return ((0, pl.multiple_of(a, 8), WA), (WA, pl.multiple_of(b, 8), WB))


def _pack_words(lo, hi):
    ul = lax.bitcast_convert_type(lo.astype(BF16).astype(F32), U32)
    uh = lax.bitcast_convert_type(hi.astype(BF16).astype(F32), U32)
    w = jnp.bitwise_or(jnp.bitwise_and(uh, jnp.uint32(0xFFFF0000)), jnp.right_shift(ul, jnp.uint32(16)))
    return lax.bitcast_convert_type(w, F32)


def _unpack_words(w):
    w = lax.bitcast_convert_type(w, U32)
    lo = lax.bitcast_convert_type(jnp.left_shift(w, jnp.uint32(16)), F32)
    hi = lax.bitcast_convert_type(jnp.bitwise_and(w, jnp.uint32(0xFFFF0000)), F32)
    return lo, hi


def _all_gather(wc, sp):
    def body(w_ref, s_ref, ow_ref, os_ref, send_sems, recv_sems):
        x, y, c = _me()
        k_me = 2 * x + y
        sib = (x, y, 1 - c)
        chips = [(1 - x, y), (x, 1 - y), (1 - x, 1 - y)]
        wcols = lambda cc: pl.ds(pl.multiple_of(cc * (CWD // 2), LANE), CWD // 2)
        scols = lambda cc: pl.ds(pl.multiple_of(cc * HW, LANE), HW)

        def windows(k, cc):
            pcs = _piece_rows(k)
            return ([(w_ref.at[pl.ds(l0, n), wcols(cc)], ow_ref.at[pl.ds(p0, n), wcols(cc)]) for l0, p0, n in pcs]
                    + [(s_ref.at[:, scols(cc)], os_ref.at[k, :, scols(cc)])])

        def copy(i, src, dst, to):
            return pltpu.make_async_remote_copy(src_ref=src, dst_ref=dst, send_sem=send_sems.at[i],
                                                recv_sem=recv_sems.at[i], device_id=to, device_id_type=MESH)

        def own_windows():
            return ([(w_ref.at[pl.ds(l0, n)], ow_ref.at[pl.ds(p0, n)]) for l0, p0, n in _piece_rows(k_me)]
                    + [(s_ref, os_ref.at[k_me])])

        first = [copy(18 + i, src, dst, sib) for i, (src, dst) in enumerate(own_windows())]
        for j, (cx, cy) in enumerate(chips):
            for i, (src, dst) in enumerate(windows(k_me, c)):
                first.append(copy(3 * j + i, src, dst, (cx, cy, c)))
        for cp in first:
            cp.start()
        passed = []
        for j, (cx, cy) in enumerate(chips):
            for i, (_, dst) in enumerate(windows(2 * cx + cy, c)):
                copy(3 * j + i, dst, dst, (cx, cy, c)).wait_recv()
                cp = copy(9 + 3 * j + i, dst, dst, sib)
                cp.start()
                passed.append(cp)
        for j, (cx, cy) in enumerate(chips):
            for i, (_, dst) in enumerate(windows(2 * cx + cy, 1 - c)):
                copy(9 + 3 * j + i, dst, dst, sib).wait_recv()
        for i, (_, dst) in enumerate(own_windows()):
            copy(18 + i, dst, dst, sib).wait_recv()
        for cp in first + passed:
            cp.wait_send()

    return pl.pallas_call(
        body,
        out_shape=[jax.ShapeDtypeStruct((NINP, CWD), F32), jax.ShapeDtypeStruct((4, PACK_ROWS, PACK_W), BF16)],
        in_specs=[ANY, ANY], out_specs=[ANY, ANY],
        scratch_shapes=[pltpu.SemaphoreType.DMA((21,)), pltpu.SemaphoreType.DMA((21,))],
        name="weights_all_gather",
    )(wc, sp)


HBM = pl.BlockSpec(memory_space=pltpu.HBM)
SEM = pl.BlockSpec(memory_space=pltpu.SEMAPHORE)
EFFECT = pltpu.SideEffectType.DATAFLOW_SIDE_EFFECTING


def _in_hbm(a):
    return pltpu.with_memory_space_constraint(a, pltpu.HBM)


def _ag_shard(w_ref, s_ref, lw_ref, ls_ref, k):
    return ([(w_ref.at[pl.ds(l0, n)], lw_ref.at[pl.ds(p0, n)]) for l0, p0, n in _piece_rows(k)]
            + [(s_ref, ls_ref.at[k])])


def _ag_behind_copies(w_ref, s_ref, lw_ref, ls_ref, send_sems, recv_sems):
    x, y, c = _me()
    peers = [(1 - x, y, c), (x, 1 - y, c), (1 - x, 1 - y, c), (x, y, 1 - c)]
    mine, theirs = [], []
    for j, (px, py, pc) in enumerate(peers):
        for i, ((src, dst), (_, got)) in enumerate(zip(_ag_shard(w_ref, s_ref, lw_ref, ls_ref, 2 * x + y),
                                                       _ag_shard(w_ref, s_ref, lw_ref, ls_ref, 2 * px + py))):
            mk = lambda s_, d_: pltpu.make_async_remote_copy(
                src_ref=s_, dst_ref=d_, send_sem=send_sems.at[3 * j + i], recv_sem=recv_sems.at[3 * j + i],
                device_id=(px, py, pc), device_id_type=MESH)
            mine.append(mk(src, dst))
            theirs.append(mk(got, got))
    return mine, theirs


def _ag_behind_start(wc, sp, dep):
    def body(w_ref, s_ref, lw_ref, ls_ref, dep_ref, send_sems, recv_sems, w_thru, s_thru, lw_thru, ls_thru, token):
        mine, _ = _ag_behind_copies(w_ref, s_ref, lw_ref, ls_ref, send_sems, recv_sems)
        for cp in mine:
            cp.start()
        token[...] = jnp.zeros_like(token)

    return pl.pallas_call(
        body, name="weights_gather_start",
        out_shape=(pltpu.SemaphoreType.DMA((12,)), pltpu.SemaphoreType.DMA((12,)), pltpu.HBM(wc.shape, wc.dtype),
                   pltpu.HBM(sp.shape, sp.dtype), pltpu.HBM((NINP, CWD), F32), pltpu.HBM((4, PACK_ROWS, PACK_W), BF16),
                   jax.ShapeDtypeStruct((8, LANE), F32)),
        in_specs=(HBM, HBM, HBM, HBM, ANY),
        out_specs=(SEM, SEM, HBM, HBM, HBM, HBM, pl.BlockSpec(memory_space=pltpu.VMEM)),
        input_output_aliases={0: 2, 1: 3, 2: 4, 3: 5},
        compiler_params=pltpu.CompilerParams(has_side_effects=EFFECT),
    )(_in_hbm(wc), _in_hbm(sp), _in_hbm(lax.empty((NINP, CWD), F32)), _in_hbm(lax.empty((4, PACK_ROWS, PACK_W), BF16)), dep)


def _ag_behind_wait(send_sems, recv_sems, wc, sp, lw, ls, after):
    def body(w_ref, s_ref, lw_ref, ls_ref, send_sems, recv_sems, after_ref, w_dead, s_dead, lw_out, ls_out):
        mine, theirs = _ag_behind_copies(w_ref, s_ref, lw_ref, ls_ref, send_sems, recv_sems)
        for cp in mine:
            cp.wait_send()
        for cp in theirs:
            cp.wait_recv()

    out = pl.pallas_call(
        body, name="weights_gather_wait",
        out_shape=(pltpu.HBM(wc.shape, wc.dtype), pltpu.HBM(sp.shape, sp.dtype), pltpu.HBM(lw.shape, lw.dtype),
                   pltpu.HBM(ls.shape, ls.dtype)),
        in_specs=(HBM, HBM, HBM, HBM, SEM, SEM, ANY), out_specs=(HBM, HBM, HBM, HBM),
        input_output_aliases={0: 0, 1: 1, 2: 2, 3: 3},
        compiler_params=pltpu.CompilerParams(has_side_effects=EFFECT),
    )(wc, sp, lw, ls, send_sems, recv_sems, after)
    return out[2], out[3]


def _rs_swap(gw, gs):
    def body(w_ref, s_ref, rw_ref, rs_ref, send_sems, recv_sems):
        x, y, c = _me()
        oc = 1 - c
        cps = [pltpu.make_async_remote_copy(src_ref=w_ref.at[:, pl.ds(pl.multiple_of(oc * (D // 2), LANE), D // 2)],
                                            dst_ref=rw_ref, send_sem=send_sems.at[0], recv_sem=recv_sems.at[0],
                                            device_id=(x, y, oc), device_id_type=MESH),
               pltpu.make_async_remote_copy(src_ref=s_ref.at[:, :, pl.ds(pl.multiple_of(oc * HW, LANE), HW)],
                                            dst_ref=rs_ref, send_sem=send_sems.at[1], recv_sem=recv_sems.at[1],
                                            device_id=(x, y, oc), device_id_type=MESH)]
        for cp in cps:
            cp.start()
        for cp in cps:
            cp.wait()

    return pl.pallas_call(
        body,
        out_shape=[jax.ShapeDtypeStruct((NINP, D // 2), F32), jax.ShapeDtypeStruct((4, PACK_ROWS, HW), F32)],
        in_specs=[ANY, ANY], out_specs=[ANY, ANY],
        scratch_shapes=[pltpu.SemaphoreType.DMA((2,)), pltpu.SemaphoreType.DMA((2,))],
        name="grads_sibling_swap",
    )(gw, gs)


def _rs_swap_copies(w_ref, s_ref, rw_ref, rs_ref, send_sems, recv_sems):
    x, y, c = _me()
    oc = 1 - c
    return [pltpu.make_async_remote_copy(src_ref=w_ref.at[:, pl.ds(pl.multiple_of(oc * (D // 2), LANE), D // 2)],
                                         dst_ref=rw_ref, send_sem=send_sems.at[0], recv_sem=recv_sems.at[0],
                                         device_id=(x, y, oc), device_id_type=MESH),
            pltpu.make_async_remote_copy(src_ref=s_ref.at[:, :, pl.ds(pl.multiple_of(oc * HW, LANE), HW)],
                                         dst_ref=rs_ref, send_sem=send_sems.at[1], recv_sem=recv_sems.at[1],
                                         device_id=(x, y, oc), device_id_type=MESH)]


def _rs_swap_start(gw, gs):
    def body(w_ref, s_ref, rw_ref, rs_ref, send_sems, recv_sems, w_thru, s_thru, rw_thru, rs_thru, token):
        for cp in _rs_swap_copies(w_ref, s_ref, rw_ref, rs_ref, send_sems, recv_sems):
            cp.start()
        token[...] = jnp.zeros_like(token)

    return pl.pallas_call(
        body, name="grads_swap_start",
        out_shape=(pltpu.SemaphoreType.DMA((2,)), pltpu.SemaphoreType.DMA((2,)), pltpu.HBM(gw.shape, gw.dtype),
                   pltpu.HBM(gs.shape, gs.dtype), pltpu.HBM((NINP, D // 2), F32), pltpu.HBM((4, PACK_ROWS, HW), F32),
                   jax.ShapeDtypeStruct((8, LANE), F32)),
        in_specs=(HBM, HBM, HBM, HBM),
        out_specs=(SEM, SEM, HBM, HBM, HBM, HBM, pl.BlockSpec(memory_space=pltpu.VMEM)),
        input_output_aliases={0: 2, 1: 3, 2: 4, 3: 5},
        compiler_params=pltpu.CompilerParams(has_side_effects=EFFECT),
    )(_in_hbm(gw), _in_hbm(gs), _in_hbm(lax.empty((NINP, D // 2), F32)), _in_hbm(lax.empty((4, PACK_ROWS, HW), F32)))


def _rs_swap_wait(send_sems, recv_sems, gw, gs, rw, rs, after):
    def body(w_ref, s_ref, rw_ref, rs_ref, send_sems, recv_sems, after_ref, w_out, s_out, rw_out, rs_out):
        for cp in _rs_swap_copies(w_ref, s_ref, rw_ref, rs_ref, send_sems, recv_sems):
            cp.wait()

    return pl.pallas_call(
        body, name="grads_swap_wait",
        out_shape=(pltpu.HBM(gw.shape, gw.dtype), pltpu.HBM(gs.shape, gs.dtype), pltpu.HBM(rw.shape, rw.dtype),
                   pltpu.HBM(rs.shape, rs.dtype)),
        in_specs=(HBM, HBM, HBM, HBM, SEM, SEM, ANY), out_specs=(HBM, HBM, HBM, HBM),
        input_output_aliases={0: 0, 1: 1, 2: 2, 3: 3},
        compiler_params=pltpu.CompilerParams(has_side_effects=EFFECT),
    )(gw, gs, rw, rs, send_sems, recv_sems, after)


SUM_BR = 512


def _rs_chip_sum_w(gw, rw, cidx):
    def body(c_ref, g_ref, r_ref, o_ref):
        s = g_ref[...] + r_ref[...]
        q = D // 8
        o_ref[...] = jnp.concatenate([_pack_words(s[:, 0:q], s[:, q:2 * q]),
                                      _pack_words(s[:, 2 * q:3 * q], s[:, 3 * q:4 * q])], axis=1)

    return pl.pallas_call(
        body,
        grid_spec=pltpu.PrefetchScalarGridSpec(
            num_scalar_prefetch=1, grid=(NINP // SUM_BR,),
            in_specs=[pl.BlockSpec((SUM_BR, D // 2), lambda i, cr: (i, cr[0])),
                      pl.BlockSpec((SUM_BR, D // 2), lambda i, cr: (i, 0))],
            out_specs=pl.BlockSpec((SUM_BR, D // 4), lambda i, cr: (i, 0))),
        out_shape=jax.ShapeDtypeStruct((NINP, D // 4), F32),
        name="grads_chip_sum_w",
        compiler_params=pltpu.CompilerParams(dimension_semantics=("arbitrary",), vmem_limit_bytes=VMEM_LIMIT),
    )(cidx, gw, rw)


def _rs_chip_sum_s(gs, rs, cidx):
    def body(c_ref, g_ref, r_ref, o_ref):
        o_ref[...] = (g_ref[...] + r_ref[...]).astype(BF16)

    return pl.pallas_call(
        body,
        grid_spec=pltpu.PrefetchScalarGridSpec(
            num_scalar_prefetch=1, grid=(4,),
            in_specs=[pl.BlockSpec((None, PACK_ROWS, HW), lambda j, cr: (j, 0, cr[0])),
                      pl.BlockSpec((None, PACK_ROWS, HW), lambda j, cr: (j, 0, 0))],
            out_specs=pl.BlockSpec((None, PACK_ROWS, HW), lambda j, cr: (j, 0, 0))),
        out_shape=jax.ShapeDtypeStruct((4, PACK_ROWS, HW), BF16),
        name="grads_chip_sum_s",
        compiler_params=pltpu.CompilerParams(dimension_semantics=("arbitrary",), vmem_limit_bytes=VMEM_LIMIT),
    )(cidx, gs, rs)


def _rs_exchange_copies(sw_ref, ss_ref, r2w_ref, r2s_ref, send_sems, recv_sems):
    x, y, c = _me()
    mine, theirs = [], []
    for j, (cx, cy) in enumerate([(1 - x, y), (x, 1 - y), (1 - x, 1 - y)]):
        def mk(i, src, dst):
            return pltpu.make_async_remote_copy(src_ref=src, dst_ref=dst, send_sem=send_sems.at[3 * j + i],
                                                recv_sem=recv_sems.at[3 * j + i], device_id=(cx, cy, c), device_id_type=MESH)
        for i, (l0, p0, n) in enumerate(_piece_rows(2 * cx + cy)):
            mine.append(mk(i, sw_ref.at[pl.ds(p0, n)], r2w_ref.at[j, pl.ds(l0, n)]))
            theirs.append(mk(i, r2w_ref.at[j, pl.ds(l0, n)], r2w_ref.at[j, pl.ds(l0, n)]))
        mine.append(mk(2, ss_ref.at[2 * cx + cy], r2s_ref.at[j]))
        theirs.append(mk(2, r2s_ref.at[j], r2s_ref.at[j]))
    return mine, theirs


def _rs_exchange_start(sw, ss, tag):
    def body(sw_ref, ss_ref, r2w_ref, r2s_ref, send_sems, recv_sems, sw_thru, ss_thru, r2w_thru, r2s_thru, token):
        mine, _ = _rs_exchange_copies(sw_ref, ss_ref, r2w_ref, r2s_ref, send_sems, recv_sems)
        for cp in mine:
            cp.start()
        token[...] = jnp.zeros_like(token)

    return pl.pallas_call(
        body, name="grads_exchange_start_" + tag,
        out_shape=(pltpu.SemaphoreType.DMA((9,)), pltpu.SemaphoreType.DMA((9,)), pltpu.HBM(sw.shape, sw.dtype),
                   pltpu.HBM(ss.shape, ss.dtype), pltpu.HBM((3, WSH, D // 4), F32), pltpu.HBM((3, PACK_ROWS, HW), BF16),
                   jax.ShapeDtypeStruct((8, LANE), F32)),
        in_specs=(HBM, HBM, HBM, HBM),
        out_specs=(SEM, SEM, HBM, HBM, HBM, HBM, pl.BlockSpec(memory_space=pltpu.VMEM)),
        input_output_aliases={0: 2, 1: 3, 2: 4, 3: 5},
        compiler_params=pltpu.CompilerParams(has_side_effects=EFFECT),
    )(_in_hbm(sw), _in_hbm(ss), _in_hbm(lax.empty((3, WSH, D // 4), F32)), _in_hbm(lax.empty((3, PACK_ROWS, HW), BF16)))


def _rs_exchange_wait(send_sems, recv_sems, sw, ss, r2w, r2s, after, tag):
    def body(sw_ref, ss_ref, r2w_ref, r2s_ref, send_sems, recv_sems, after_ref, sw_dead, ss_dead, r2w_out, r2s_out):
        mine, theirs = _rs_exchange_copies(sw_ref, ss_ref, r2w_ref, r2s_ref, send_sems, recv_sems)
        for cp in mine:
            cp.wait_send()
        for cp in theirs:
            cp.wait_recv()

    out = pl.pallas_call(
        body, name="grads_exchange_wait_" + tag,
        out_shape=(pltpu.HBM(sw.shape, sw.dtype), pltpu.HBM(ss.shape, ss.dtype), pltpu.HBM(r2w.shape, r2w.dtype),
                   pltpu.HBM(r2s.shape, r2s.dtype)),
        in_specs=(HBM, HBM, HBM, HBM, SEM, SEM, ANY), out_specs=(HBM, HBM, HBM, HBM),
        input_output_aliases={0: 0, 1: 1, 2: 2, 3: 3},
        compiler_params=pltpu.CompilerParams(has_side_effects=EFFECT),
    )(sw, ss, r2w, r2s, send_sems, recv_sems, after)
    return out[2], out[3]


def _rs_final_w(gw, rw, r2w, idx):
    q = D // 8

    def body(i_ref, g_ref, r_ref, p_ref, o_ref, gbuf, rbuf, sems):
        i = pl.program_id(0)
        k, c = i_ref[0], i_ref[1]
        cps = []
        for n_, (l0, p0, n) in enumerate(_piece_rows(k)):
            gcol = pl.ds(pl.multiple_of(c * (D // 2) + i * 2 * q, LANE), 2 * q)
            rcol = pl.ds(pl.multiple_of(i * 2 * q, LANE), 2 * q)
            cps.append(pltpu.make_async_copy(g_ref.at[pl.ds(p0, n), gcol], gbuf.at[pl.ds(l0, n)], sems.at[2 * n_]))
            cps.append(pltpu.make_async_copy(r_ref.at[pl.ds(p0, n), rcol], rbuf.at[pl.ds(l0, n)], sems.at[2 * n_ + 1]))
        for cp in cps:
            cp.start()
        for cp in cps:
            cp.wait()
        acc = gbuf[...] + rbuf[...]
        for j in range(3):
            lo, hi = _unpack_words(p_ref[j])
            acc = acc + jnp.concatenate([lo, hi], axis=1)
        o_ref[...] = acc

    return pl.pallas_call(
        body,
        grid_spec=pltpu.PrefetchScalarGridSpec(
            num_scalar_prefetch=1, grid=(2,),
            in_specs=[ANY, ANY, pl.BlockSpec((3, WSH, q), lambda i, ir: (0, 0, i))],
            out_specs=pl.BlockSpec((WSH, 2 * q), lambda i, ir: (0, i)),
            scratch_shapes=[pltpu.VMEM((WSH, 2 * q), F32), pltpu.VMEM((WSH, 2 * q), F32), pltpu.SemaphoreType.DMA((4,))]),
        out_shape=jax.ShapeDtypeStruct((WSH, D // 2), F32),
        name="grads_final_sum_w",
        compiler_params=pltpu.CompilerParams(dimension_semantics=("arbitrary",), vmem_limit_bytes=VMEM_LIMIT),
    )(idx, gw, rw, r2w)


def _rs_final_s(gs, rs, r2s, idx):
    def body(i_ref, g_ref, r_ref, p_ref, o_ref):
        acc = g_ref[...] + r_ref[...]
        for j in range(3):
            acc = acc + p_ref[j].astype(F32)
        o_ref[...] = acc

    return pl.pallas_call(
        body,
        grid_spec=pltpu.PrefetchScalarGridSpec(
            num_scalar_prefetch=1, grid=(1,),
            in_specs=[pl.BlockSpec((None, PACK_ROWS, HW), lambda i, ir: (ir[0], 0, ir[1])),
                      pl.BlockSpec((None, PACK_ROWS, HW), lambda i, ir: (ir[0], 0, 0)),
                      pl.BlockSpec((3, PACK_ROWS, HW), lambda i, ir: (0, 0, 0))],
            out_specs=pl.BlockSpec((PACK_ROWS, HW), lambda i, ir: (0, 0))),
        out_shape=jax.ShapeDtypeStruct((PACK_ROWS, HW), F32),
        name="grads_final_sum_s",
        compiler_params=pltpu.CompilerParams(dimension_semantics=("arbitrary",), vmem_limit_bytes=VMEM_LIMIT),
    )(idx, gs, rs, r2s)


def _rs_share(fw, fs):
    def body(w_ref, s_ref, ow_ref, os_ref, send_sems, recv_sems):
        x, y, c = _me()
        cps = [pltpu.make_async_remote_copy(src_ref=w_ref, dst_ref=ow_ref, send_sem=send_sems.at[0],
                                            recv_sem=recv_sems.at[0], device_id=(x, y, 1 - c), device_id_type=MESH),
               pltpu.make_async_remote_copy(src_ref=s_ref, dst_ref=os_ref, send_sem=send_sems.at[1],
                                            recv_sem=recv_sems.at[1], device_id=(x, y, 1 - c), device_id_type=MESH)]
        for cp in cps:
            cp.start()
        for cp in cps:
            cp.wait()

    return pl.pallas_call(
        body,
        out_shape=[jax.ShapeDtypeStruct((WSH, D // 2), F32), jax.ShapeDtypeStruct((PACK_ROWS, HW), F32)],
        in_specs=[ANY, ANY], out_specs=[ANY, ANY],
        scratch_shapes=[pltpu.SemaphoreType.DMA((2,)), pltpu.SemaphoreType.DMA((2,))],
        name="grads_share",
    )(fw, fs)


def _both_halves(mine, other, c):
    return jnp.where(c == 0, jnp.concatenate([mine, other], axis=1), jnp.concatenate([other, mine], axis=1))


def _rs_sums(gw, gs, rw, rs):
    x, y, c = _me()
    cidx = jnp.reshape(c, (1,)).astype(jnp.int32)
    return dict(gw=gw, gs=gs, rw=rw, rs=rs, sw=_rs_chip_sum_w(gw, rw, cidx), ss=_rs_chip_sum_s(gs, rs, cidx))


def _rs_begin(gw, gs):
    return _rs_sums(gw, gs, *_rs_swap(gw, gs))


def _rs_end(st, r2w, r2s):
    x, y, c = _me()
    idx = jnp.stack([2 * x + y, c]).astype(jnp.int32)
    fw = _rs_final_w(st["gw"], st["rw"], r2w, idx)
    fs = _rs_final_s(st["gs"], st["rs"], r2s, idx)
    ow, os_ = _rs_share(fw, fs)
    return _both_halves(fw, ow, c), _both_halves(fs, os_, c)


def _all_reduce_small(gs):
    rows = gs.shape[0]

    def body(g_ref, o_ref, buf, send_sems, recv_sems):
        x, y, c = _me()
        me = 4 * x + 2 * y + c
        buf[me] = g_ref[...]
        cps = []
        for r in range(1, 8):
            fx, fy, fc = (r >> 2) & 1, (r >> 1) & 1, r & 1
            px, py, pc = jnp.bitwise_xor(x, fx), jnp.bitwise_xor(y, fy), jnp.bitwise_xor(c, fc)
            cps.append((pltpu.make_async_remote_copy(
                src_ref=g_ref, dst_ref=buf.at[me], send_sem=send_sems.at[r - 1], recv_sem=recv_sems.at[r - 1],
                device_id=(px, py, pc), device_id_type=MESH), 4 * px + 2 * py + pc))
        for cp, _ in cps:
            cp.start()
        for r, (cp, peer) in enumerate(cps):
            pltpu.make_async_remote_copy(
                src_ref=g_ref, dst_ref=buf.at[peer], send_sem=send_sems.at[r], recv_sem=recv_sems.at[r],
                device_id=(x, y, c), device_id_type=MESH).wait_recv()
        for cp, _ in cps:
            cp.wait_send()
        acc = buf[0]
        for k in range(1, 8):
            acc = acc + buf[k]
        o_ref[...] = acc

    return pl.pallas_call(
        body,
        out_shape=jax.ShapeDtypeStruct((rows, LANE), F32),
        in_specs=[pl.BlockSpec(memory_space=pltpu.VMEM)],
        out_specs=pl.BlockSpec(memory_space=pltpu.VMEM),
        scratch_shapes=[pltpu.VMEM((8, rows, LANE), F32), pltpu.SemaphoreType.DMA((7,)), pltpu.SemaphoreType.DMA((7,))],
        name="small_grads_all_reduce",
    )(gs)


PACK_SPLIT = (("w_uq", 96, (QL, 192)), ("w_ukv", 64, (KVL, 256)),
              ("w_out_a", 256, (CW, 256)), ("w_out_b", 256, (CW, 256)), ("w_out_c", 256, (CW, 256)),
              ("w_o", 512, (256, D)))
MAT_ROWS = 1440
CONV_SHARD = 3 * 128


def _w_in_words(w_in_shard):
    t = w_in_shard.T
    return _pack_words(t[:, :CWD], t[:, CWD:])


def _pack_weights(wl):
    parts = [wl[n].astype(BF16).reshape(-1, PACK_W) for n, _, _ in PACK_SPLIT]
    cw = wl["conv_w"].reshape(-1)
    hi = cw.astype(BF16)
    r1 = cw - hi.astype(F32)
    mid = r1.astype(BF16)
    lo = (r1 - mid.astype(F32)).astype(BF16)
    cterms = jnp.pad(jnp.concatenate([hi, mid, lo]), (0, 3 * PACK_W - 3 * CONV_SHARD)).reshape(3, PACK_W)
    tail = jnp.pad(cterms, ((0, PACK_ROWS - MAT_ROWS - 3), (0, 0)))
    return jnp.concatenate(parts + [tail], axis=0)


def _unpack_weights(gath):
    out = {}
    r = 0
    for n, nrows, shp in PACK_SPLIT:
        t = gath[:, r:r + nrows].reshape((4,) + shp)
        r += nrows
        if n == "w_o":
            out[n] = t.reshape(4 * shp[0], shp[1])
        else:
            out[n] = t.transpose(1, 0, 2).reshape(shp[0], 4 * shp[1])
    ct = gath[:, r:r + 3].reshape(4, 3 * PACK_W)[:, :3 * CONV_SHARD].astype(F32).reshape(4, 3, CONV_SHARD)
    cw = (ct[:, 0] + ct[:, 1]) + ct[:, 2]
    out["conv_w"] = cw.reshape(4, 3, 128).transpose(1, 0, 2).reshape(3, CW)
    return out


def _pack_grads(g):
    parts = []
    for n, nrows, shp in PACK_SPLIT:
        t = g[n]
        if n == "w_o":
            t = t.reshape((4,) + shp)
        else:
            t = t.reshape(shp[0], 4, shp[1]).transpose(1, 0, 2)
        parts.append(t.reshape(4, nrows, PACK_W))
    cw = g["conv_w"].reshape(3, 4, 128).transpose(1, 0, 2).reshape(4, 1, CONV_SHARD)
    parts.append(jnp.pad(cw, ((0, 0), (0, PACK_ROWS - MAT_ROWS - 1), (0, PACK_W - CONV_SHARD))))
    return jnp.concatenate(parts, axis=1)


def _unpack_grads(red):
    out = {}
    r = 0
    for n, nrows, shp in PACK_SPLIT:
        out[n] = red[r:r + nrows].reshape(shp)
        r += nrows
    out["conv_w"] = red[r, :CONV_SHARD].reshape(3, 128)
    return out


SMALL_SIZES = (("norm_g", D), ("b_gate", 3 * D), ("conv_b", CW), ("q_a_norm_g", QL), ("kv_a_norm_g", KVL),
               ("mla_q_norm_g", QK), ("mla_k_norm_g", QK), ("dil_q_norm_g", NG * HD), ("dil_k_norm_g", NG * HD))
SMALL_ROWS = 88


def _pack_small(per_name):
    flat = jnp.concatenate([per_name[n].reshape(-1).astype(F32) for n, _ in SMALL_SIZES])
    return jnp.pad(flat, (0, SMALL_ROWS * LANE - flat.shape[0])).reshape(SMALL_ROWS, LANE)


def _unpack_small(packed, like):
    out = {}
    flat = packed.reshape(-1)
    r = 0
    for n, sz in SMALL_SIZES:
        out[n] = flat[r:r + NL * sz].reshape(like[n].shape)
        r += NL * sz
    return out


def _adamw_math(w, g, m, v):
    m = ADAM_B1 * m + (1.0 - ADAM_B1) * g
    v = ADAM_B2 * v + (1.0 - ADAM_B2) * jnp.square(g)
    m_hat = m / (1.0 - ADAM_B1 ** ADAM_STEP)
    v_hat = v / (1.0 - ADAM_B2 ** ADAM_STEP)
    delta = -ADAM_LR * (m_hat / (jnp.sqrt(v_hat) + ADAM_EPS) + ADAM_WD * w)
    return delta, m, v


def _adamw(name, w, g, m, v, br, bc=None):
    L, R, C = w.shape
    bc = C if bc is None else bc
    blk = lambda l, i, j: (l, i, j)
    return _pcall(name, _adamw_math, (L, R // br, C // bc), [(t, (None, br, bc), blk) for t in (w, g, m, v)],
                  [((L, R, C), F32, (None, br, bc), blk)] * 3)


ADAM_ROWS = {"w_uq": 256, "w_ukv": 128, "w_out_a": 512, "w_out_b": 512, "w_out_c": 512, "w_o": 256,
             "conv_w": 3}


def kernel(x, norm_g, w_in, b_gate, conv_w, conv_b, q_a_norm_g, w_uq, kv_a_norm_g, w_ukv, mla_q_norm_g, mla_k_norm_g, dil_q_norm_g, dil_k_norm_g, w_out_a, w_out_b, w_out_c, w_o, loss_target, m_norm_g, m_w_in, m_b_gate, m_conv_w, m_conv_b, m_q_a_norm_g, m_w_uq, m_kv_a_norm_g, m_w_ukv, m_mla_q_norm_g, m_mla_k_norm_g, m_dil_q_norm_g, m_dil_k_norm_g, m_w_out_a, m_w_out_b, m_w_out_c, m_w_o, v_norm_g, v_w_in, v_b_gate, v_conv_w, v_conv_b, v_q_a_norm_g, v_w_uq, v_kv_a_norm_g, v_w_ukv, v_mla_q_norm_g, v_mla_k_norm_g, v_dil_q_norm_g, v_dil_k_norm_g, v_w_out_a, v_w_out_b, v_w_out_c, v_w_o):
    W = dict(norm_g=norm_g, w_in=w_in, b_gate=b_gate, conv_w=conv_w, conv_b=conv_b, q_a_norm_g=q_a_norm_g, w_uq=w_uq,
             kv_a_norm_g=kv_a_norm_g, w_ukv=w_ukv, mla_q_norm_g=mla_q_norm_g, mla_k_norm_g=mla_k_norm_g,
             dil_q_norm_g=dil_q_norm_g, dil_k_norm_g=dil_k_norm_g, w_out_a=w_out_a, w_out_b=w_out_b, w_out_c=w_out_c,
             w_o=w_o)
    M = dict(norm_g=m_norm_g, w_in=m_w_in, b_gate=m_b_gate, conv_w=m_conv_w, conv_b=m_conv_b, q_a_norm_g=m_q_a_norm_g,
             w_uq=m_w_uq, kv_a_norm_g=m_kv_a_norm_g, w_ukv=m_w_ukv, mla_q_norm_g=m_mla_q_norm_g,
             mla_k_norm_g=m_mla_k_norm_g, dil_q_norm_g=m_dil_q_norm_g, dil_k_norm_g=m_dil_k_norm_g, w_out_a=m_w_out_a,
             w_out_b=m_w_out_b, w_out_c=m_w_out_c, w_o=m_w_o)
    V = dict(norm_g=v_norm_g, w_in=v_w_in, b_gate=v_b_gate, conv_w=v_conv_w, conv_b=v_conv_b, q_a_norm_g=v_q_a_norm_g,
             w_uq=v_w_uq, kv_a_norm_g=v_kv_a_norm_g, w_ukv=v_w_ukv, mla_q_norm_g=v_mla_q_norm_g,
             mla_k_norm_g=v_mla_k_norm_g, dil_q_norm_g=v_dil_q_norm_g, dil_k_norm_g=v_dil_k_norm_g, w_out_a=v_w_out_a,
             w_out_b=v_w_out_b, w_out_c=v_w_out_c, w_o=v_w_o)
    batch = x.shape[0]
    T = batch * S

    def layer_weights(l, cont, gath):
        full = _unpack_weights(gath)
        pad_qk = lambda t: jnp.pad(t, (0, QKP - QK)).reshape(1, QKP)
        full.update(
            w_in_t=cont,
            norm_g=norm_g[l].reshape(1, D), b_gate=b_gate[l].reshape(1, 3 * D), conv_b=conv_b[l].reshape(1, CW),
            q_a_norm_g=q_a_norm_g[l].reshape(1, QL), kv_a_norm_g=kv_a_norm_g[l].reshape(1, KVL),
            mla_q_norm_g=pad_qk(mla_q_norm_g[l]), mla_k_norm_g=pad_qk(mla_k_norm_g[l]),
            dil_q_norm_g=dil_q_norm_g[l].reshape(NG, 1, HD), dil_k_norm_g=dil_k_norm_g[l].reshape(NG, 1, HD))
        return full

    words = [_w_in_words(w_in[l]) for l in range(NL)]
    packs = [_pack_weights({n: W[n][l] for n in BIG[1:] + ("conv_w",)}) for l in range(NL)]
    tabs = _rope_tables() + (_dil_slopes(),)
    x2 = x.reshape(T, D)

    cont0, gath0 = _all_gather(words[0], packs[0])
    w0 = layer_weights(0, cont0, gath0)
    ag = _ag_behind_start(words[1], packs[1], gath0)
    w0["norm_g"] = w0["norm_g"] + ag[6][0:1, 0:1]
    y0, res0 = _layer_fwd(x2, w0, tabs, batch)
    w1 = layer_weights(1, *_ag_behind_wait(ag[0], ag[1], ag[2], ag[3], ag[4], ag[5], y0))
    y1, res1 = _layer_fwd(y0, w1, tabs, batch)

    row = lambda i: (i, 0)
    dy, loss = _pcall("loss", _loss_math, (T // BR,),
                      [(y1, (BR, D), row), (loss_target.reshape(T, D), (BR, D), row)],
                      [((T, D), F32, (BR, D), row), ((1, 1), F32, (1, 1), lambda i: (0, 0), True)])
    loss = lax.psum(loss[0, 0], ("x", "y", "c"))

    grads = [None] * NL
    dy, grads[1] = _layer_bwd(dy, w1, res1, tabs, batch)
    st = [None] * NL
    ex = [None] * NL
    sw1 = _rs_swap_start(grads[1]["w_in_t"], _pack_grads(grads[1]))
    w0["w_o"] = w0["w_o"] + sw1[6][0:1, 0:1].astype(BF16)

    def exchange_layer1(t):
        st[1] = _rs_sums(*_rs_swap_wait(*sw1[:6], t))
        ex[1] = _rs_exchange_start(st[1]["sw"], st[1]["ss"], "1")
        return ex[1][6]

    def start_layer0(g):
        st[0] = _rs_begin(g["w_in_t"], _pack_grads(g))
        ex[0] = _rs_exchange_start(st[0]["sw"], st[0]["ss"], "0")
        return ex[0][6]

    dx, grads[0] = _layer_bwd(dy, w0, res0, tabs, batch, after_dw=start_layer0, after_merge=exchange_layer1)
    grad_x = dx.reshape(batch, S, D)

    red = [None] * NL
    for l in (1, 0):
        r2w, r2s = _rs_exchange_wait(*ex[l][:6], dx, str(l))
        rw, rs = _rs_end(st[l], r2w, r2s)
        r = _unpack_grads(rs)
        r["w_in_t"] = rw
        red[l] = r
    G = {n: jnp.stack([red[l][n] for l in range(NL)]) for n in BIG[1:] + ("conv_w",)}
    g_in_t = jnp.stack([red[l]["w_in_t"] for l in range(NL)])
    G["w_in"] = jnp.swapaxes(g_in_t, 1, 2)
    small_g = {n: jnp.stack([grads[l][n].reshape(-1)[:sz] for l in range(NL)]) for n, sz in SMALL_SIZES}
    small_red = _all_reduce_small(_pack_small(small_g))
    G.update(_unpack_small(small_red, {n: W[n] for n in SMALL}))

    delta, new_m, new_v = {}, {}, {}
    for n in BIG[1:] + ("conv_w",):
        delta[n], new_m[n], new_v[n] = _adamw("adamw_" + n, W[n], G[n], M[n], V[n], ADAM_ROWS[n])
    tr = lambda t: jnp.swapaxes(t, 1, 2)
    delta["w_in"], new_m["w_in"], new_v["w_in"] = (
        tr(t) for t in _adamw("adamw_w_in", tr(w_in), g_in_t, tr(m_w_in), tr(v_w_in), WSH, LANE))
    sw, sm, sv = (_pack_small({n: t[n] for n in SMALL})[None] for t in (W, M, V))
    sd, snm, snv = _adamw("adamw_small", sw, small_red[None], sm, sv, SMALL_ROWS)
    like = {n: W[n] for n in SMALL}
    delta.update(_unpack_small(sd[0], like))
    new_m.update(_unpack_small(snm[0], like))
    new_v.update(_unpack_small(snv[0], like))

    return (loss, grad_x, *[G[n] for n in WEIGHTS], *[delta[n] for n in WEIGHTS],
            *[new_m[n] for n in WEIGHTS], *[new_v[n] for n in WEIGHTS])
```

```python
import functools

import numpy as np
import jax
import jax.numpy as jnp
from jax import lax
from jax.experimental import pallas as pl
from jax.experimental.pallas import tpu as pltpu

F32 = jnp.float32
BF16 = jnp.bfloat16

D = 1024
S = 2048
NL = 2
CW = 512
NH = 8
QL = 256
KVL = 128
NOPE = 64
ROPE = 32
VD = 64
QK = NOPE + ROPE
QKP = 128
ROPE_THETA = 10000.0
DIL = ((128, 1), (512, 4), (2048, 16))
NG = 3
DH = 8
HD = 64
DWID = DH * HD
QB = 128
EPS = 1e-6
NIN = 11168
NINP = 11264
O_A, O_CQ, O_CKV, O_KPE, O_BZ, O_DQ, O_DK, O_DV, O_CZ, O_G = 0, 2048, 2304, 2432, 2560, 3072, 4608, 6144, 7680, 8192
KPE_END = 2464
NEG = -1e30
MLA_SCALE = QK ** -0.5
DIL_SCALE = HD ** -0.5
LANE = 128
PACK_W = 512
VMEM_LIMIT = 48 * 1024 * 1024

ADAM_LR = 0.001
ADAM_B1 = 0.9
ADAM_B2 = 0.999
ADAM_EPS = 1e-08
ADAM_WD = 0.01
ADAM_STEP = 10

MESH = pl.DeviceIdType.MESH
BIG = ("w_in", "w_uq", "w_ukv", "w_out_a", "w_out_b", "w_out_c", "w_o")
SMALL = ("norm_g", "b_gate", "conv_b", "q_a_norm_g", "kv_a_norm_g", "mla_q_norm_g", "mla_k_norm_g",
         "dil_q_norm_g", "dil_k_norm_g")
WEIGHTS = ("norm_g", "w_in", "b_gate", "conv_w", "conv_b", "q_a_norm_g", "w_uq", "kv_a_norm_g", "w_ukv",
           "mla_q_norm_g", "mla_k_norm_g", "dil_q_norm_g", "dil_k_norm_g", "w_out_a", "w_out_b", "w_out_c", "w_o")


def _dot(a, b):
    return jnp.dot(a, b, preferred_element_type=F32)


def _dot_nt(a, b):
    return lax.dot_general(a, b, (((1,), (1,)), ((), ())), preferred_element_type=F32)


def _dot_tn(a, b):
    return lax.dot_general(a, b, (((0,), (0,)), ((), ())), preferred_element_type=F32)


def _grid_step(grid):
    step = pl.program_id(0)
    for a in range(1, len(grid)):
        step = step * grid[a] + pl.program_id(a)
    n = 1
    for g in grid:
        n *= g
    return step, n


def _write_windows(buf_ref, stages, sems, step, nsteps, puts):
    slot = step % 2
    for t, (v, dst) in enumerate(puts):
        cp = pltpu.make_async_copy(stages[t].at[slot], dst, sems.at[t, slot])

        @pl.when(step >= 2)
        def _():
            cp.wait()

        stages[t][slot] = v.astype(stages[t].dtype).reshape(stages[t].shape[1:])
        cp.start()

    @pl.when(step == nsteps - 1)
    def _():
        for t, (v, dst) in enumerate(puts):
            pltpu.make_async_copy(stages[t].at[slot], dst, sems.at[t, slot]).wait()
            if nsteps > 1:
                pltpu.make_async_copy(stages[t].at[1 - slot], dst, sems.at[t, 1 - slot]).wait()


def _pcall(name, fn, grid, ins, outs, into=None):
    n_in = len(ins)
    n_out = len(outs)
    acc_axis = len(grid) - 1
    is_acc = [len(o) > 4 and o[4] for o in outs]
    outs = [o[:4] for o in outs]
    targets = into[1] if into is not None else []
    n_t = len(targets)

    def body(*refs):
        vals = fn(*[r[...].astype(F32) for r in refs[:n_in]])
        if not isinstance(vals, (tuple, list)):
            vals = (vals,)
        o0 = n_in + (1 if n_t else 0)
        for k in range(n_out):
            r = refs[o0 + k]
            v = vals[k].astype(r.dtype).reshape(r.shape)
            if is_acc[k]:
                first = pl.program_id(acc_axis) == 0

                @pl.when(first)
                def _():
                    r[...] = v

                @pl.when(jnp.logical_not(first))
                def _():
                    r[...] += v
            else:
                r[...] = v
        if n_t:
            buf_ref = refs[o0 + n_out]
            stages = refs[o0 + n_out + 1:o0 + n_out + 1 + n_t]
            ids = [pl.program_id(a) for a in range(len(grid))]
            step, nsteps = _grid_step(grid)
            _write_windows(buf_ref, stages, refs[-1], step, nsteps,
                           [(vals[n_out + t], targets[t][1](buf_ref, *ids)) for t in range(n_t)])

    in_specs = [pl.BlockSpec(bs, im) for _, bs, im in ins]
    out_specs = [pl.BlockSpec(bs, im) for _, _, bs, im in outs]
    out_shape = [jax.ShapeDtypeStruct(sh, dt) for sh, dt, _, _ in outs]
    args = [a for a, _, _ in ins]
    extra = {}
    if n_t:
        buf = into[0]
        in_specs.append(pl.BlockSpec(memory_space=pl.ANY))
        out_specs.append(pl.BlockSpec(memory_space=pl.ANY))
        out_shape.append(jax.ShapeDtypeStruct(buf.shape, buf.dtype))
        args.append(buf)
        extra = dict(input_output_aliases={n_in: n_out},
                     scratch_shapes=[pltpu.VMEM((2,) + tuple(bs), buf.dtype) for bs, _ in targets]
                     + [pltpu.SemaphoreType.DMA((n_t, 2))])
    return pl.pallas_call(
        body,
        grid=grid,
        in_specs=in_specs,
        out_specs=out_specs,
        out_shape=out_shape,
        name=name,
        compiler_params=pltpu.CompilerParams(
            dimension_semantics=("arbitrary",) * len(grid), vmem_limit_bytes=VMEM_LIMIT),
        **extra,
    )(*args)


def _mm(name, a, b, *, ta=False, tb=False, out_dtype=F32, add=None, dep=None, b_words=False, tm=2048, tn=1024, tk=1024):
    if ta:
        K, M = a.shape
    else:
        M, K = a.shape
    bshape = (b.shape[0], 2 * b.shape[1]) if b_words else b.shape
    if tb:
        N, K2 = bshape
    else:
        K2, N = bshape
    assert K == K2, (name, a.shape, b.shape)
    tm, tn, tk = min(tm, M), min(tn, N), min(tk, K)
    assert M % tm == 0 and N % tn == 0 and K % tk == 0, (name, M, N, K)
    nk = K // tk
    dims = (((0 if ta else 1,), (1 if tb else 0,)), ((), ()))
    a_spec = pl.BlockSpec((tk, tm), lambda j, i, k: (k, i)) if ta else pl.BlockSpec((tm, tk), lambda j, i, k: (i, k))
    bw = 2 if b_words else 1
    assert not b_words or (tk if tb else tn) == bshape[1]
    b_spec = (pl.BlockSpec((tn, tk // bw), lambda j, i, k: (j, k)) if tb
              else pl.BlockSpec((tk, tn // bw), lambda j, i, k: (k, j)))
    o_spec = pl.BlockSpec((tm, tn), lambda j, i, k: (i, j))
    has_add = add is not None
    n_in = 2 + has_add + (dep is not None)

    def body(*refs):
        a_ref, b_ref = refs[0], refs[1]
        add_ref = refs[2] if has_add else None
        o_ref = refs[n_in]
        bb = b_ref[...]
        if b_words:
            lo, hi = _unpack_words(bb)
            first = (pl.program_id(0) * tn) if tb else (pl.program_id(2) * tk)
            r = first + lax.broadcasted_iota(jnp.int32, lo.shape, 0)
            pad = jnp.logical_and(r >= KPE_END, r < KPE_END + NINP - NIN)
            bb = jnp.concatenate([jnp.where(pad, 0.0, lo), jnp.where(pad, 0.0, hi)], axis=1)
        p = lax.dot_general(a_ref[...].astype(BF16), bb.astype(BF16), dims, preferred_element_type=F32)
        if nk == 1:
            if has_add:
                p = p + add_ref[...]
            o_ref[...] = p.astype(out_dtype)
        else:
            acc = refs[-1]
            k = pl.program_id(2)

            @pl.when(k == 0)
            def _():
                acc[...] = p

            @pl.when(k > 0)
            def _():
                acc[...] += p

            @pl.when(k == nk - 1)
            def _():
                r = acc[...]
                if has_add:
                    r = r + add_ref[...]
                o_ref[...] = r.astype(out_dtype)

    in_specs = [a_spec, b_spec] + ([o_spec] if has_add else []) + ([pl.BlockSpec(memory_space=pl.ANY)] if dep is not None else [])
    args = [a, b] + ([add] if has_add else []) + ([dep] if dep is not None else [])
    return pl.pallas_call(
        body,
        grid=(N // tn, M // tm, nk),
        in_specs=in_specs,
        out_specs=o_spec,
        out_shape=jax.ShapeDtypeStruct((M, N), out_dtype),
        scratch_shapes=[pltpu.VMEM((tm, tn), F32)] if nk > 1 else [],
        name=name,
        compiler_params=pltpu.CompilerParams(
            dimension_semantics=("arbitrary", "arbitrary", "arbitrary"), vmem_limit_bytes=VMEM_LIMIT),
    )(*args)


def _vjp_of(f, n_diff):
    def g(*args, n_prim):
        prim = args[:n_diff]
        consts = args[n_diff:n_prim]
        cts = args[n_prim:]
        _, pull = jax.vjp(lambda *p: f(*p, *consts), *prim)
        out = jax.eval_shape(lambda *p: f(*p, *consts), *prim)
        if isinstance(out, (tuple, list)):
            cts = tuple(c.astype(o.dtype) for c, o in zip(cts, out))
        else:
            cts = cts[0].astype(out.dtype)
        return pull(cts)
    return g


def _rms(x, g, n=None):
    n = x.shape[-1] if n is None else n
    ms = jnp.sum(x * x, axis=-1, keepdims=True) / n
    return x * lax.rsqrt(ms + EPS) * g


def _silu(z):
    return z * jax.nn.sigmoid(z)


def _roll_rows(u, k):
    n = u.shape[0]
    r = pltpu.roll(u, k % n, 0)
    t = lax.broadcasted_iota(jnp.int32, u.shape, 0)
    if k > 0:
        return jnp.where(t >= k, r, 0.0)
    return jnp.where(t < n + k, r, 0.0)


@functools.partial(jax.custom_vjp, nondiff_argnums=(1,))
def _shift(u, k):
    return _roll_rows(u, k)


def _shift_fwd(u, k):
    return _roll_rows(u, k), None


def _shift_bwd(k, _, g):
    return (_roll_rows(g, -k),)


_shift.defvjp(_shift_fwd, _shift_bwd)


@functools.partial(jax.custom_vjp, nondiff_argnums=(1,))
def _lane_roll(u, k):
    return pltpu.roll(u, k % LANE, 1)


def _lane_roll_fwd(u, k):
    return pltpu.roll(u, k % LANE, 1), None


def _lane_roll_bwd(k, _, g):
    return (pltpu.roll(g, (-k) % LANE, 1),)


_lane_roll.defvjp(_lane_roll_fwd, _lane_roll_bwd)


def _conv_math(ab, ac, ax, az, cw, cb):
    u = ac * ax
    conv = cb + _shift(u, 2) * cw[0:1] + _shift(u, 1) * cw[1:2] + u * cw[2:3]
    return ab * conv * _silu(az)


def _mla_pre_math(cq, ckv, gq, gkv):
    return _rms(cq, gq), _rms(ckv, gkv)


def _rope_math(q, kn, kpe, gq, gk, c, s1, s2):
    lane = lax.broadcasted_iota(jnp.int32, kpe.shape, 1)
    pe = _lane_roll(jnp.where(lane < ROPE, kpe, 0.0), NOPE)

    def one(t, g):
        tn = _rms(t, g, QK)
        return tn * c + _lane_roll(tn, -16) * s1 + _lane_roll(tn, 16) * s2

    qs, ks = [], []
    for h in range(NH):
        sl = slice(h * QKP, (h + 1) * QKP)
        qs.append(one(q[:, sl], gq))
        ks.append(one(kn[:, sl] + pe, gk))
    return jnp.concatenate(qs, axis=1), jnp.concatenate(ks, axis=1)


def _gate_math(o, z):
    return o * _silu(z)


def _mergec_math(o0, o1, o2, l0, l1, l2, cz):
    m = lax.stop_gradient(jnp.maximum(jnp.maximum(l0, l1), l2))
    e0, e1, e2 = jnp.exp(l0 - m), jnp.exp(l1 - m), jnp.exp(l2 - m)
    den = e0 + e1 + e2
    oc = (e0 / den) * o0 + (e1 / den) * o1 + (e2 / den) * o2
    return oc * _silu(cz)


def _merge_math(g0, g1, g2, b0, b1, b2, pa, pb, pc):
    return (jax.nn.sigmoid(g0 + b0) * pa + jax.nn.sigmoid(g1 + b1) * pb) + jax.nn.sigmoid(g2 + b2) * pc


MLA_T = 256
MLA_UNROLL = True


def _mla_fwd(q, k, v):
    B = q.shape[0]
    T = MLA_T
    NB = S // T

    def body(q_ref, k_ref, v_ref, o_ref, l_ref):
        row = lax.broadcasted_iota(jnp.int32, (T, T), 0)
        col = lax.broadcasted_iota(jnp.int32, (T, T), 1)
        lo = _lo_mask((T, LANE))

        for qi in range(NB):
            qb = q_ref[qi * T:(qi + 1) * T, :]

            def step(j, carry, diagonal):
                m, l, acc = carry
                off = pl.multiple_of(j * T, T)
                kb = k_ref[pl.ds(off, T), :]
                vb = v_ref[pl.ds(off, T), :]
                ss = []
                for e in (0, 1):
                    se = _dot_nt(qb[:, e * QKP:(e + 1) * QKP], kb[:, e * QKP:(e + 1) * QKP]) * MLA_SCALE
                    ss.append(jnp.where(col <= row, se, NEG) if diagonal else se)
                s = jnp.concatenate(ss, axis=0)
                m_new = jnp.maximum(m, jnp.max(s, axis=-1, keepdims=True))
                a = jnp.exp(m - m_new)
                p = jnp.exp(s - m_new)
                l = a * l + jnp.sum(p, axis=-1, keepdims=True)
                acc = a * acc + _dot(p.astype(BF16), vb)
                return m_new, l, acc

            init = (jnp.full((2 * T, 1), NEG, F32), jnp.zeros((2 * T, 1), F32), jnp.zeros((2 * T, LANE), F32))
            carry = lax.fori_loop(0, qi, functools.partial(step, diagonal=False), init, unroll=MLA_UNROLL)
            m, l, acc = step(qi, carry, True)
            o = acc / l
            lse = m + jnp.log(l)
            o_ref[qi * T:(qi + 1) * T, :] = jnp.where(lo, o[:T], o[T:])
            l_ref[qi * T:(qi + 1) * T, :] = jnp.where(lo, lse[:T], lse[T:])

    def spec(w):
        return pl.BlockSpec((None, S, w), lambda b, hp: (b, 0, hp))

    return pl.pallas_call(
        body,
        grid=(B, NH // 2),
        in_specs=[spec(2 * QKP), spec(2 * QKP), spec(LANE)],
        out_specs=[spec(LANE), spec(LANE)],
        out_shape=[jax.ShapeDtypeStruct((B, S, NH * VD), F32)] * 2,
        name="mla_attn_fwd",
        compiler_params=pltpu.CompilerParams(dimension_semantics=("arbitrary",) * 2, vmem_limit_bytes=VMEM_LIMIT),
    )(q, k, v)


def _mla_bwd(q, k, v, do, o, lse):
    B = q.shape[0]
    T = MLA_T
    NB = S // T

    def body(q_ref, k_ref, v_ref, do_ref, o_ref, l_ref, dq_ref, dk_ref, dv_ref, delta_ref, dqt_ref):
        delta_ref[...] = _head_sum(do_ref[...] * o_ref[...])
        row = lax.broadcasted_iota(jnp.int32, (T, T), 0)
        col = lax.broadcasted_iota(jnp.int32, (T, T), 1)
        lo = _lo_mask((T, LANE))
        tn_t = (((0,), (1,)), ((), ()))

        for j in range(NB):
            krows = slice(j * T, (j + 1) * T)
            kb = k_ref[krows, :]
            vb = v_ref[krows, :]
            dkt = [jnp.zeros((QKP, T), F32), jnp.zeros((QKP, T), F32)]
            dvt = jnp.zeros((LANE, T), F32)
            for i in range(j, NB):
                qrows = slice(i * T, (i + 1) * T)
                qb = q_ref[qrows, :]
                do2 = _stack_heads(do_ref[qrows, :], lo).astype(BF16)
                lb = l_ref[qrows, :]
                db = delta_ref[qrows, :]
                dp2 = _dot_nt(do2, vb)
                ps = []
                for e in (0, 1):
                    cols = slice(e * QKP, (e + 1) * QKP)
                    qe, ke = qb[:, cols], kb[:, cols]
                    s = _dot_nt(qe, ke) * MLA_SCALE
                    if i == j:
                        s = jnp.where(col <= row, s, NEG)
                    p = jnp.exp(s - lb[:, e * HD:e * HD + 1])
                    ps.append(p.astype(BF16))
                    ds = (p * (dp2[e * T:(e + 1) * T] - db[:, e * HD:e * HD + 1]) * MLA_SCALE).astype(BF16)
                    dkt[e] = dkt[e] + _dot_tn(qe, ds)
                    dq_t = lax.dot_general(ke, ds, tn_t, preferred_element_type=F32)
                    if j == 0:
                        dqt_ref[e, :, qrows] = dq_t
                    else:
                        dqt_ref[e, :, qrows] += dq_t
                dvt = dvt + _dot_tn(do2, jnp.concatenate(ps, axis=0))
            dk_ref[krows, 0:QKP] = dkt[0].T
            dk_ref[krows, QKP:2 * QKP] = dkt[1].T
            dv_ref[krows, :] = dvt.T
        dq_ref[:, 0:QKP] = dqt_ref[0].T
        dq_ref[:, QKP:2 * QKP] = dqt_ref[1].T

    def spec(w):
        return pl.BlockSpec((None, S, w), lambda b, hp: (b, 0, hp))

    return pl.pallas_call(
        body,
        grid=(B, NH // 2),
        in_specs=[spec(2 * QKP), spec(2 * QKP), spec(LANE), spec(LANE), spec(LANE), spec(LANE)],
        out_specs=[spec(2 * QKP), spec(2 * QKP), spec(LANE)],
        out_shape=[jax.ShapeDtypeStruct((B, S, NH * QKP), F32), jax.ShapeDtypeStruct((B, S, NH * QKP), F32),
                   jax.ShapeDtypeStruct((B, S, NH * VD), F32)],
        scratch_shapes=[pltpu.VMEM((S, LANE), F32), pltpu.VMEM((2, QKP, S), F32)],
        name="mla_attn_bwd",
        compiler_params=pltpu.CompilerParams(dimension_semantics=("arbitrary",) * 2, vmem_limit_bytes=VMEM_LIMIT),
    )(q, k, v, do, o, lse)


def _lo_mask(shape):
    return lax.broadcasted_iota(jnp.int32, shape, len(shape) - 1) < HD


def _head_sum(u):
    r = lax.broadcasted_iota(jnp.int32, (LANE, LANE), 0) < HD
    c = lax.broadcasted_iota(jnp.int32, (LANE, LANE), 1) < HD
    ones = jnp.where(r == c, 1.0, 0.0).astype(BF16)
    hi = u.astype(BF16)
    lo = (u - hi.astype(F32)).astype(BF16)
    return _dot(hi, ones) + _dot(lo, ones)


def _head_sum_1(u):
    r = lax.broadcasted_iota(jnp.int32, (LANE, LANE), 0) < HD
    c = lax.broadcasted_iota(jnp.int32, (LANE, LANE), 1) < HD
    return _dot(u.astype(BF16), jnp.where(r == c, 1.0, 0.0).astype(BF16))


def _rms2_scale(x):
    return lax.rsqrt(_head_sum(x * x) / HD + EPS)


def _rms2(x, g):
    return x * _rms2_scale(x) * g


def _rms2_bwd(x, r, g, dy):
    xn = x * r
    t = dy * g
    dx = r * (t - xn * (_head_sum_1(xn * t) * (1.0 / HD)))
    return dx, jnp.sum(dy * xn, axis=0, keepdims=True)


def _dil_bias(t_ref, gi, d):
    qq = lax.broadcasted_iota(jnp.int32, (QB, QB), 0)
    kk = lax.broadcasted_iota(jnp.int32, (QB, QB), 1)
    jc = (qq - kk).astype(F32)
    rows = []
    for e in (0, 1):
        sl = t_ref[2 * gi + e:2 * gi + e + 1, :] * float(d)
        bp = jnp.where(kk >= qq, -sl * (jc + float(QB)), NEG)
        bc = jnp.where(kk <= qq, -sl * jc, NEG)
        rows.append(jnp.concatenate([bp, bc], axis=1))
    return jnp.concatenate(rows, axis=0)


def _dil_rows(cur, d):
    return pl.ds(cur, QB, stride=d) if d > 1 else pl.ds(pl.multiple_of(cur, QB), QB)


def _dil_walk(d, block, full):
    if d == 1:
        block(0, None)

        def body(i, c):
            block(i * QB, (i - 1) * QB)
            return c
        lax.fori_loop(1, S // QB, body, 0, unroll=True if full else 5)
    elif d == 16:
        def body(r, c):
            block(r, None)
            return c
        lax.fori_loop(0, d, body, 0, unroll=True if full else 4)
    else:
        nb = S // d // QB

        def cls(r, c):
            block(r, None)

            def body(i, c2):
                block(r + i * QB * d, r + (i - 1) * QB * d)
                return c2
            lax.fori_loop(1, nb, body, 0, unroll=True)
            return c
        lax.fori_loop(0, d, cls, 0, unroll=full)


def _stack_heads(x, lo):
    return jnp.concatenate([jnp.where(lo, x, 0.0), jnp.where(lo, 0.0, x)], axis=0)


def _dilc_fwd(proj3, gq, gk, tab):
    B = proj3.shape[0]

    def body(q_ref, k_ref, v_ref, cz_ref, gq_ref, gk_ref, t_ref, y_ref, o_ref, l_ref, qs, ks, vs):
        g = pl.program_id(2)
        lo = _lo_mask((QB, LANE))

        def group(gi):
            d = DIL[gi][1]
            qs[...] = _rms2(q_ref[...].astype(F32), gq_ref[gi:gi + 1, :])
            ks[...] = _rms2(k_ref[...].astype(F32), gk_ref[gi:gi + 1, :])
            vs[...] = v_ref[...].astype(F32)
            bias = _dil_bias(t_ref, gi, d)

            def block(cur, prev):
                rows = _dil_rows(cur, d)
                q2 = _stack_heads(qs[rows, :], lo).astype(BF16)
                kc, vc = ks[rows, :], vs[rows, :]
                if prev is None:
                    kcat, vcat, b = kc, vc, bias[:, QB:]
                else:
                    prow = _dil_rows(prev, d)
                    kcat = jnp.concatenate([ks[prow, :], kc], axis=0)
                    vcat = jnp.concatenate([vs[prow, :], vc], axis=0)
                    b = bias
                s = _dot_nt(q2, kcat.astype(BF16)) * DIL_SCALE + b
                m = jnp.max(s, axis=-1, keepdims=True)
                p = jnp.exp(s - m)
                l = jnp.sum(p, axis=-1, keepdims=True)
                o = _dot(p.astype(BF16), vcat.astype(BF16)) / l
                lse = m + jnp.log(l)
                o_ref[gi, rows, :] = jnp.where(lo, o[:QB], o[QB:])
                l_ref[gi, rows, :] = jnp.where(lo, lse[:QB], lse[QB:])

            _dil_walk(d, block, True)

        for gi in range(NG):
            pl.when(g == gi)(functools.partial(group, gi))

        @pl.when(g == NG - 1)
        def _():
            y_ref[...] = _mergec_math(o_ref[0], o_ref[1], o_ref[2], l_ref[0], l_ref[1], l_ref[2],
                                      cz_ref[...].astype(F32)).astype(BF16)

    def col(base):
        return pl.BlockSpec((None, S, LANE), lambda b, hp, g: (b, 0, base // LANE + 4 * g + hp))

    gspec = pl.BlockSpec((NG, LANE), lambda b, hp, g: (0, 0))
    saved = pl.BlockSpec((NG, None, S, LANE), lambda b, hp, g: (0, b, 0, hp))
    return pl.pallas_call(
        body,
        grid=(B, 4, NG),
        in_specs=[col(O_DQ), col(O_DK), col(O_DV),
                  pl.BlockSpec((None, S, LANE), lambda b, hp, g: (b, 0, O_CZ // LANE + hp)),
                  gspec, gspec, pl.BlockSpec((None, 8, LANE), lambda b, hp, g: (hp, 0, 0))],
        out_specs=[pl.BlockSpec((None, S, LANE), lambda b, hp, g: (b, 0, hp)), saved, saved],
        out_shape=[jax.ShapeDtypeStruct((B, S, DWID), BF16), jax.ShapeDtypeStruct((NG, B, S, DWID), F32),
                   jax.ShapeDtypeStruct((NG, B, S, DWID), F32)],
        scratch_shapes=[pltpu.VMEM((S, LANE), F32)] * 3,
        name="dil_mixer_fwd",
        compiler_params=pltpu.CompilerParams(dimension_semantics=("arbitrary",) * 3, vmem_limit_bytes=VMEM_LIMIT),
    )(proj3, proj3, proj3, proj3, gq, gk, tab)


MERGE_ROWS = 256


def _dilc_bwd(proj3, gq, gk, tab, o_all, l_all, d_yc, dproj3):
    B = proj3.shape[0]

    def body(q_ref, k_ref, v_ref, cz_ref, gq_ref, gk_ref, t_ref, o_ref, l_ref, dy_ref, dp_in,
             dp_out, dgq_out, dgk_out, qs, ks, vs, dos, dls, dqs, dks, dvs, rqs, rks, dczs,
             st_q, st_k, st_v, st_z, sems, sem_z):
        b_, hp, g = pl.program_id(0), pl.program_id(1), pl.program_id(2)
        col = lambda base: pl.ds(pl.multiple_of(base + hp * LANE, LANE), LANE)
        lo = _lo_mask((QB, LANE))

        @pl.when(jnp.logical_and(jnp.logical_and(pl.program_id(0) == 0, pl.program_id(1) == 0), g == 0))
        def _():
            dgq_out[...] = jnp.zeros((NG, LANE), F32)
            dgk_out[...] = jnp.zeros((NG, LANE), F32)

        @pl.when(g == 0)
        def _():
            def chunk(i, carry):
                rows = pl.ds(pl.multiple_of(i * MERGE_ROWS, MERGE_ROWS), MERGE_ROWS)
                ls = [l_ref[j, rows, :] for j in range(NG)]
                m = jnp.maximum(jnp.maximum(ls[0], ls[1]), ls[2])
                es = [jnp.exp(t - m) for t in ls]
                den = (es[0] + es[1]) + es[2]
                al = [e / den for e in es]
                os_ = [o_ref[j, rows, :] for j in range(NG)]
                oc = (al[0] * os_[0] + al[1] * os_[1]) + al[2] * os_[2]
                cz = cz_ref[rows, :].astype(F32)
                sg = jax.nn.sigmoid(cz)
                dy = dy_ref[rows, :]
                d_oc = dy * (cz * sg)
                dczs[rows, :] = (dy * oc * (sg * (1.0 + cz * (1.0 - sg)))).astype(BF16)
                ts = [_head_sum_1(d_oc * os_[j]) for j in range(NG)]
                tbar = (al[0] * ts[0] + al[1] * ts[1]) + al[2] * ts[2]
                for j in range(NG):
                    dos[j, rows, :] = al[j] * d_oc
                    dls[j, rows, :] = al[j] * (ts[j] - tbar)
                return carry
            lax.fori_loop(0, S // MERGE_ROWS, chunk, 0)
            _write_windows(dp_out, [st_z], sem_z, b_ * 4 + hp, B * 4, [(dczs[...], dp_out.at[b_, :, col(O_CZ)])])

        def group(gi):
            d = DIL[gi][1]
            xq, xk = q_ref[...].astype(F32), k_ref[...].astype(F32)
            rqs[...] = _rms2_scale(xq)
            rks[...] = _rms2_scale(xk)
            qs[...] = xq * rqs[...] * gq_ref[gi:gi + 1, :]
            ks[...] = xk * rks[...] * gk_ref[gi:gi + 1, :]
            vs[...] = v_ref[...].astype(F32)
            dks[...] = jnp.zeros((S, LANE), F32)
            dvs[...] = jnp.zeros((S, LANE), F32)
            bias = _dil_bias(t_ref, gi, d)

            def block(cur, prev):
                rows = _dil_rows(cur, d)
                q2 = _stack_heads(qs[rows, :], lo).astype(BF16)
                dob = dos[gi, rows, :]
                do2 = _stack_heads(dob, lo).astype(BF16)
                kc, vc = ks[rows, :], vs[rows, :]
                if prev is None:
                    kcat, vcat, b = kc, vc, bias[:, QB:]
                else:
                    prow = _dil_rows(prev, d)
                    kcat = jnp.concatenate([ks[prow, :], kc], axis=0)
                    vcat = jnp.concatenate([vs[prow, :], vc], axis=0)
                    b = bias
                kcat = kcat.astype(BF16)
                vcat = vcat.astype(BF16)
                lse_b = l_ref[gi, rows, :]
                corr_b = dls[gi, rows, :] - _head_sum_1(dob * o_ref[gi, rows, :])
                lse2 = jnp.concatenate([lse_b[:, 0:1], lse_b[:, HD:HD + 1]], axis=0)
                corr2 = jnp.concatenate([corr_b[:, 0:1], corr_b[:, HD:HD + 1]], axis=0)
                s = _dot_nt(q2, kcat) * DIL_SCALE + b
                p = jnp.exp(s - lse2)
                ds = (p * (_dot_nt(do2, vcat) + corr2) * DIL_SCALE).astype(BF16)
                dq2 = _dot(ds, kcat)
                dqs[rows, :] = jnp.where(lo, dq2[:QB], dq2[QB:])
                dk = _dot_tn(ds, q2)
                dv = _dot_tn(p.astype(BF16), do2)
                if prev is None:
                    dks[rows, :] += dk
                    dvs[rows, :] += dv
                else:
                    dks[prow, :] += dk[:QB]
                    dvs[prow, :] += dv[:QB]
                    dks[rows, :] += dk[QB:]
                    dvs[rows, :] += dv[QB:]

            _dil_walk(d, block, False)

            dxq, dgq = _rms2_bwd(q_ref[...].astype(F32), rqs[...], gq_ref[gi:gi + 1, :], dqs[...])
            dgq_out[gi:gi + 1, :] += dgq
            dxk, dgk = _rms2_bwd(k_ref[...].astype(F32), rks[...], gk_ref[gi:gi + 1, :], dks[...])
            dgk_out[gi:gi + 1, :] += dgk
            step, nsteps = _grid_step((B, 4, NG))
            _write_windows(dp_out, [st_q, st_k, st_v], sems, step, nsteps,
                           [(dxq, dp_out.at[b_, :, col(O_DQ + gi * DWID)]), (dxk, dp_out.at[b_, :, col(O_DK + gi * DWID)]),
                            (dvs[...], dp_out.at[b_, :, col(O_DV + gi * DWID)])])

        for gi in range(NG):
            pl.when(g == gi)(functools.partial(group, gi))

    def col(base):
        return pl.BlockSpec((None, S, LANE), lambda b, hp, g: (b, 0, base // LANE + 4 * g + hp))

    gspec = pl.BlockSpec((NG, LANE), lambda b, hp, g: (0, 0))
    saved = pl.BlockSpec((NG, None, S, LANE), lambda b, hp, g: (0, b, 0, hp))
    per_pair = pl.BlockSpec((None, S, LANE), lambda b, hp, g: (b, 0, hp))
    return pl.pallas_call(
        body,
        grid=(B, 4, NG),
        in_specs=[col(O_DQ), col(O_DK), col(O_DV),
                  pl.BlockSpec((None, S, LANE), lambda b, hp, g: (b, 0, O_CZ // LANE + hp)),
                  gspec, gspec, pl.BlockSpec((None, 8, LANE), lambda b, hp, g: (hp, 0, 0)),
                  saved, saved, per_pair, pl.BlockSpec(memory_space=pl.ANY)],
        out_specs=[pl.BlockSpec(memory_space=pl.ANY), gspec, gspec],
        out_shape=[jax.ShapeDtypeStruct(dproj3.shape, dproj3.dtype), jax.ShapeDtypeStruct((NG, LANE), F32),
                   jax.ShapeDtypeStruct((NG, LANE), F32)],
        input_output_aliases={10: 0},
        scratch_shapes=[pltpu.VMEM((S, LANE), F32)] * 3 + [pltpu.VMEM((NG, S, LANE), F32)] * 2
        + [pltpu.VMEM((S, LANE), F32)] * 5 + [pltpu.VMEM((S, LANE), BF16)] + [pltpu.VMEM((2, S, LANE), BF16)] * 4
        + [pltpu.SemaphoreType.DMA((3, 2)), pltpu.SemaphoreType.DMA((1, 2))],
        name="dil_mixer_bwd",
        compiler_params=pltpu.CompilerParams(dimension_semantics=("arbitrary",) * 3, vmem_limit_bytes=VMEM_LIMIT),
    )(proj3, proj3, proj3, proj3, gq, gk, tab, o_all, l_all, d_yc, dproj3)


def _dil_slopes():
    slopes = (2.0 ** (-8.0 * np.arange(1, NG * DH + 1, dtype=np.float32) / (NG * DH))).astype(np.float32).reshape(NG, DH)
    tab = np.zeros((4, 8, LANE), np.float32)
    for hp in range(4):
        for gi in range(NG):
            for e in (0, 1):
                tab[hp, 2 * gi + e, :] = slopes[gi, 2 * hp + e]
    return jnp.asarray(tab)


def _rope_tables():
    inv = ROPE_THETA ** (-jnp.arange(0, ROPE, 2, dtype=F32) / ROPE)
    ang = jnp.arange(S, dtype=F32)[:, None] * inv[None, :]
    cos, sin = jnp.cos(ang), jnp.sin(ang)
    z16 = jnp.zeros((S, 16), F32)
    c = jnp.concatenate([jnp.ones((S, NOPE), F32), cos, cos, jnp.zeros((S, 32), F32)], axis=1)
    s1 = jnp.concatenate([jnp.zeros((S, NOPE), F32), -sin, z16, jnp.zeros((S, 32), F32)], axis=1)
    s2 = jnp.concatenate([jnp.zeros((S, NOPE), F32), z16, sin, jnp.zeros((S, 32), F32)], axis=1)
    return c, s1, s2


def _pad_heads_uq(w):
    return jnp.pad(w.reshape(QL, NH, QK), ((0, 0), (0, 0), (0, QKP - QK))).reshape(QL, NH * QKP)


def _unpad_heads_uq(g):
    return g.reshape(QL, NH, QKP)[:, :, :QK].reshape(QL, NH * QK)


def _split_ukv(w):
    w3 = w.reshape(KVL, NH, NOPE + VD)
    uk = jnp.pad(w3[:, :, :NOPE], ((0, 0), (0, 0), (0, QKP - NOPE))).reshape(KVL, NH * QKP)
    return uk, w3[:, :, NOPE:].reshape(KVL, NH * VD)


def _join_ukv(guk, guv):
    return jnp.concatenate([guk.reshape(KVL, NH, QKP)[:, :, :NOPE], guv.reshape(KVL, NH, VD)],
                           axis=-1).reshape(KVL, NH * (NOPE + VD))


BR = 512
BRM = 256


def _layer_fwd(x, w, tabs, batch):
    T = batch * S
    rope_c, rope_s1, rope_s2, dil_tab = tabs
    res = {"x": x}
    row = lambda c: (lambda i: (i, c))
    fix = lambda i: (0, 0)

    h = _pcall("norm_fwd", _rms, (T // BR,),
               [(x, (BR, D), row(0)), (w["norm_g"], (1, D), fix)],
               [((T, D), BF16, (BR, D), row(0))])[0]
    proj = _mm("in_proj", h, w["w_in_t"], tb=True, out_dtype=BF16, b_words=True, tm=2048, tn=1024)
    res["h"], res["proj"] = h, proj
    proj3 = proj.reshape(batch, S, NINP)

    cblk = lambda s: (lambda j, b: (b, 0, 4 * s + j))
    y_a = _pcall("conv_fwd", _conv_math, (4, batch),
                 [(proj3, (None, S, LANE), cblk(0)), (proj3, (None, S, LANE), cblk(1)),
                  (proj3, (None, S, LANE), cblk(2)), (proj3, (None, S, LANE), cblk(3)),
                  (w["conv_w"], (3, LANE), lambda j, b: (0, j)), (w["conv_b"], (1, LANE), lambda j, b: (0, j))],
                 [((batch, S, CW), BF16, (None, S, LANE), lambda j, b: (b, 0, j))])[0].reshape(T, CW)
    res["y_a"] = y_a

    cqn, ckvn = _pcall("mla_pre_fwd", _mla_pre_math, (T // BR,),
                       [(proj, (BR, QL), row(O_CQ // QL)), (proj, (BR, KVL), row(O_CKV // KVL)),
                        (w["q_a_norm_g"], (1, QL), fix), (w["kv_a_norm_g"], (1, KVL), fix)],
                       [((T, QL), BF16, (BR, QL), row(0)), ((T, KVL), BF16, (BR, KVL), row(0))])
    w_uq_p = _pad_heads_uq(w["w_uq"])
    w_uk, w_uv = _split_ukv(w["w_ukv"])
    q = _mm("uq", cqn, w_uq_p, out_dtype=BF16)
    kn = _mm("uk", ckvn, w_uk, out_dtype=BF16)
    v = _mm("uv", ckvn, w_uv, out_dtype=BF16)
    nrr = S // BR
    tab_row = lambda i: (i % nrr, 0)
    qr, kr = _pcall("rope_fwd", _rope_math, (T // BR,),
                    [(q, (BR, NH * QKP), row(0)), (kn, (BR, NH * QKP), row(0)), (proj, (BR, LANE), row(O_KPE // LANE)),
                     (w["mla_q_norm_g"], (1, QKP), fix), (w["mla_k_norm_g"], (1, QKP), fix),
                     (rope_c, (BR, QKP), tab_row), (rope_s1, (BR, QKP), tab_row), (rope_s2, (BR, QKP), tab_row)],
                    [((T, NH * QKP), BF16, (BR, NH * QKP), row(0))] * 2)
    qr = qr.reshape(batch, S, NH * QKP)
    kr = kr.reshape(batch, S, NH * QKP)
    v = v.reshape(batch, S, NH * VD)
    o_b, l_b = _mla_fwd(qr, kr, v)
    ob2 = o_b.reshape(T, NH * VD)
    y_b = _pcall("gateb_fwd", _gate_math, (T // BR,),
                 [(ob2, (BR, 512), row(0)), (proj, (BR, 512), row(O_BZ // 512))],
                 [((T, 512), BF16, (BR, 512), row(0))])[0]
    res.update(cqn=cqn, ckvn=ckvn, q=q, kn=kn, qr=qr, kr=kr, v=v, o_b=o_b, l_b=l_b, ob2=ob2, y_b=y_b,
               w_uq_p=w_uq_p, w_uk=w_uk, w_uv=w_uv)

    gq2 = jnp.tile(w["dil_q_norm_g"].reshape(NG, HD), (1, 2))
    gk2 = jnp.tile(w["dil_k_norm_g"].reshape(NG, HD), (1, 2))
    y_c, o_all, l_all = _dilc_fwd(proj3, gq2, gk2, dil_tab)
    y_c = y_c.reshape(T, DWID)
    res.update(o_all=o_all, l_all=l_all, y_c=y_c)

    pa = _mm("out_a", y_a, w["w_out_a"], out_dtype=BF16)
    pb = _mm("out_b", y_b, w["w_out_b"], out_dtype=BF16)
    pc = _mm("out_c", y_c, w["w_out_c"], out_dtype=BF16)
    merged = _pcall("merge_fwd", _merge_math, (T // BRM,),
                    [(proj, (BRM, D), row(O_G // D + s)) for s in range(3)]
                    + [(w["b_gate"], (1, D), (lambda s: (lambda i: (0, s)))(s)) for s in range(3)]
                    + [(t, (BRM, D), row(0)) for t in (pa, pb, pc)],
                    [((T, D), BF16, (BRM, D), row(0))])[0]
    out = _mm("o_proj", merged, w["w_o"], add=x, tm=1024)
    res.update(pa=pa, pb=pb, pc=pc, merged=merged)
    return out, res


def _norm_bwd_math(x, g, dh, dy):
    _, pull = jax.vjp(_rms, x, g)
    dx, dg = pull(dh)
    return dx + dy, dg


def _layer_bwd(dy, w, res, tabs, batch, after_dw=None, after_merge=None):
    T = batch * S
    rope_c, rope_s1, rope_s2, dil_tab = tabs
    row = lambda c: (lambda i: (i, c))
    fix = lambda i: (0, 0)
    x, proj, h = res["x"], res["proj"], res["h"]
    proj3 = proj.reshape(batch, S, NINP)
    g = {}

    d_merged = _mm("o_proj_dx", dy, w["w_o"], tb=True)
    g["w_o"] = _mm("o_proj_dw", res["merged"], dy, ta=True, tm=1024, tk=2048)

    dproj = lax.empty((T, NINP), BF16)
    rows_of = lambda br: (lambda ref, i: ref.at[pl.ds(pl.multiple_of(i * br, br), br)])

    def merge_bwd(*args):
        dg0, dg1, dg2, db0, db1, db2, dpa, dpb, dpc = _vjp_of(_merge_math, 9)(*args, n_prim=9)
        return db0, db1, db2, dpa, dpb, dpc, jnp.concatenate([dg0, dg1, dg2], axis=1)

    db0, db1, db2, dpa, dpb, dpc, dproj = _pcall(
        "merge_bwd", merge_bwd, (T // BRM,),
        [(proj, (BRM, D), row(O_G // D + s)) for s in range(3)]
        + [(w["b_gate"], (1, D), (lambda s: (lambda i: (0, s)))(s)) for s in range(3)]
        + [(t, (BRM, D), row(0)) for t in (res["pa"], res["pb"], res["pc"])]
        + [(d_merged, (BRM, D), row(0))],
        [((1, D), F32, (1, D), fix, True)] * 3 + [((T, D), BF16, (BRM, D), row(0))] * 3,
        into=(dproj, [((BRM, 3 * D), lambda ref, i: rows_of(BRM)(ref, i).at[:, O_G:O_G + 3 * D])]))
    g["b_gate"] = jnp.concatenate([db0, db1, db2], axis=1)

    dep = after_merge(dpa) if after_merge is not None else None
    d_ya = _mm("out_a_dx", dpa, w["w_out_a"], tb=True, dep=dep)
    d_yb = _mm("out_b_dx", dpb, w["w_out_b"], tb=True)
    d_yc = _mm("out_c_dx", dpc, w["w_out_c"], tb=True)
    g["w_out_a"] = _mm("out_a_dw", res["y_a"], dpa, ta=True, tk=T)
    g["w_out_b"] = _mm("out_b_dw", res["y_b"], dpb, ta=True, tk=T)
    g["w_out_c"] = _mm("out_c_dw", res["y_c"], dpc, ta=True, tk=T)

    cblk = lambda s: (lambda j, b: (b, 0, 4 * s + j))
    oblk = lambda j, b: (b, 0, j)
    def conv_bwd(*args):
        d_ab, d_ac, d_ax, d_az, dcw, dcb = _vjp_of(_conv_math, 6)(*args, n_prim=6)
        return dcw, dcb, d_ab, d_ac, d_ax, d_az

    a_col = lambda s_: (lambda ref, j, b: ref.at[b, :, pl.ds(pl.multiple_of(O_A + s_ * CW + j * LANE, LANE), LANE)])
    g["conv_w"], g["conv_b"], dproj3 = _pcall(
        "conv_bwd", conv_bwd, (4, batch),
        [(proj3, (None, S, LANE), cblk(s)) for s in range(4)]
        + [(w["conv_w"], (3, LANE), lambda j, b: (0, j)), (w["conv_b"], (1, LANE), lambda j, b: (0, j)),
           (d_ya.reshape(batch, S, CW), (None, S, LANE), oblk)],
        [((3, CW), F32, (3, LANE), lambda j, b: (0, j), True), ((1, CW), F32, (1, LANE), lambda j, b: (0, j), True)],
        into=(dproj.reshape(batch, S, NINP), [((S, LANE), a_col(s_)) for s_ in range(4)]))
    dproj = dproj3.reshape(T, NINP)

    gate_bwd = functools.partial(_vjp_of(_gate_math, 2), n_prim=2)
    d_ob, dproj = _pcall("gateb_bwd", gate_bwd, (T // BR,),
                         [(res["ob2"], (BR, 512), row(0)), (proj, (BR, 512), row(O_BZ // 512)), (d_yb, (BR, 512), row(0))],
                         [((T, 512), F32, (BR, 512), row(0))],
                         into=(dproj, [((BR, 512), lambda ref, i: rows_of(BR)(ref, i).at[:, O_BZ:O_BZ + 512])]))
    dqr, dkr, dv = _mla_bwd(res["qr"], res["kr"], res["v"], d_ob.reshape(batch, S, NH * VD), res["o_b"], res["l_b"])
    nrr = S // BR
    tab_row = lambda i: (i % nrr, 0)
    def rope_bwd(*args):
        d_q, d_kn, d_kpe, dgq, dgk = _vjp_of(_rope_math, 5)(*args, n_prim=8)
        return d_q, d_kn, dgq, dgk, d_kpe

    d_q, d_kn, g["mla_q_norm_g"], g["mla_k_norm_g"], dproj = _pcall(
        "rope_bwd", rope_bwd, (T // BR,),
        [(res["q"], (BR, NH * QKP), row(0)), (res["kn"], (BR, NH * QKP), row(0)), (proj, (BR, LANE), row(O_KPE // LANE)),
         (w["mla_q_norm_g"], (1, QKP), fix), (w["mla_k_norm_g"], (1, QKP), fix),
         (rope_c, (BR, QKP), tab_row), (rope_s1, (BR, QKP), tab_row), (rope_s2, (BR, QKP), tab_row),
         (dqr.reshape(T, NH * QKP), (BR, NH * QKP), row(0)), (dkr.reshape(T, NH * QKP), (BR, NH * QKP), row(0))],
        [((T, NH * QKP), BF16, (BR, NH * QKP), row(0))] * 2 + [((1, QKP), F32, (1, QKP), fix, True)] * 2,
        into=(dproj, [((BR, LANE), lambda ref, i: rows_of(BR)(ref, i).at[:, O_KPE:O_KPE + LANE])]))
    dv = dv.reshape(T, NH * VD)
    d_cqn = _mm("uq_dx", d_q, res["w_uq_p"], tb=True)
    d_ckvn = _mm("uk_dx", d_kn, res["w_uk"], tb=True)
    d_ckvn = _mm("uv_dx", dv, res["w_uv"], tb=True, add=d_ckvn)
    g["w_uq"] = _unpad_heads_uq(_mm("uq_dw", res["cqn"], d_q, ta=True, tk=T))
    g["w_ukv"] = _join_ukv(_mm("uk_dw", res["ckvn"], d_kn, ta=True, tk=T),
                           _mm("uv_dw", res["ckvn"], dv, ta=True, tk=T))
    def pre_bwd(*args):
        d_cq, d_ckv, dgq, dgkv = _vjp_of(_mla_pre_math, 4)(*args, n_prim=4)
        return dgq, dgkv, jnp.concatenate([d_cq, d_ckv], axis=1)

    g["q_a_norm_g"], g["kv_a_norm_g"], dproj = _pcall(
        "mla_pre_bwd", pre_bwd, (T // BR,),
        [(proj, (BR, QL), row(O_CQ // QL)), (proj, (BR, KVL), row(O_CKV // KVL)),
         (w["q_a_norm_g"], (1, QL), fix), (w["kv_a_norm_g"], (1, KVL), fix),
         (d_cqn, (BR, QL), row(0)), (d_ckvn, (BR, KVL), row(0))],
        [((1, QL), F32, (1, QL), fix, True), ((1, KVL), F32, (1, KVL), fix, True)],
        into=(dproj, [((BR, QL + KVL), lambda ref, i: rows_of(BR)(ref, i).at[:, O_CQ:O_CQ + QL + KVL])]))

    gq2 = jnp.tile(w["dil_q_norm_g"].reshape(NG, HD), (1, 2))
    gk2 = jnp.tile(w["dil_k_norm_g"].reshape(NG, HD), (1, 2))
    dproj3, dgq, dgk = _dilc_bwd(proj3, gq2, gk2, dil_tab, res["o_all"], res["l_all"],
                                 d_yc.reshape(batch, S, DWID), dproj.reshape(batch, S, NINP))
    dproj = dproj3.reshape(T, NINP)
    g["dil_q_norm_g"] = dgq[:, :HD] + dgq[:, HD:]
    g["dil_k_norm_g"] = dgk[:, :HD] + dgk[:, HD:]

    g["w_in_t"] = _mm("in_proj_dw", dproj, h, ta=True, tm=1024, tk=T)
    dep = after_dw(g) if after_dw is not None else None
    d_h = _mm("in_proj_dx", dproj, w["w_in_t"], dep=dep, b_words=True, tm=1024, tk=NINP // 4)
    dx, g["norm_g"] = _pcall("norm_bwd", _norm_bwd_math, (T // BR,),
                             [(x, (BR, D), row(0)), (w["norm_g"], (1, D), fix), (d_h, (BR, D), row(0)),
                              (dy, (BR, D), row(0))],
                             [((T, D), F32, (BR, D), row(0)), ((1, D), F32, (1, D), fix, True)])
    return dx, g


def _loss_math(y, t):
    e = y - t
    return e * (1.0 / D), 0.5 * jnp.sum(jnp.sum(e * e, axis=-1, keepdims=True) / D, axis=0, keepdims=True)


def _local_step(x, target, ws, batch):
    T = batch * S
    tabs = _rope_tables() + (_dil_slopes(),)
    saved = []
    y = x
    for l in range(NL):
        y, res = _layer_fwd(y, ws[l], tabs, batch)
        saved.append(res)
    row = lambda i: (i, 0)
    dy, loss = _pcall("loss", _loss_math, (T // BR,),
                      [(y, (BR, D), row), (target, (BR, D), row)],
                      [((T, D), F32, (BR, D), row), ((1, 1), F32, (1, 1), lambda i: (0, 0), True)])
    grads = [None] * NL
    for l in reversed(range(NL)):
        dy, grads[l] = _layer_bwd(dy, ws[l], saved[l], tabs, batch)
    return loss, dy, grads


ANY = pl.BlockSpec(memory_space=pl.ANY)
U32 = jnp.uint32
WSH = NIN // 4
WA = KPE_END
WB = WSH - WA
CWD = 512
PACK_ROWS = 1472
HW = PACK_W // 2


def _me():
    return lax.axis_index("x"), lax.axis_index("y"), lax.axis_index("c")


def _piece_rows(k):
    a = k * WSH + jnp.where(k > 0, NINP - NIN, 0)
    b = k * WSH + WA + (NINP - NIN)
    return ((0, pl.multiple_of(a, 8), WA), (WA, pl.multiple_of(b, 8), WB))


def _pack_words(lo, hi):
    ul = lax.bitcast_convert_type(lo.astype(BF16).astype(F32), U32)
    uh = lax.bitcast_convert_type(hi.astype(BF16).astype(F32), U32)
    w = jnp.bitwise_or(jnp.bitwise_and(uh, jnp.uint32(0xFFFF0000)), jnp.right_shift(ul, jnp.uint32(16)))
    return lax.bitcast_convert_type(w, F32)


def _unpack_words(w):
    w = lax.bitcast_convert_type(w, U32)
    lo = lax.bitcast_convert_type(jnp.left_shift(w, jnp.uint32(16)), F32)
    hi = lax.bitcast_convert_type(jnp.bitwise_and(w, jnp.uint32(0xFFFF0000)), F32)
    return lo, hi


def _all_gather(wc, sp):
    def body(w_ref, s_ref, ow_ref, os_ref, send_sems, recv_sems):
        x, y, c = _me()
        k_me = 2 * x + y
        sib = (x, y, 1 - c)
        chips = [(1 - x, y), (x, 1 - y), (1 - x, 1 - y)]
        wcols = lambda cc: pl.ds(pl.multiple_of(cc * (CWD // 2), LANE), CWD // 2)
        scols = lambda cc: pl.ds(pl.multiple_of(cc * HW, LANE), HW)

        def windows(k, cc):
            pcs = _piece_rows(k)
            return ([(w_ref.at[pl.ds(l0, n), wcols(cc)], ow_ref.at[pl.ds(p0, n), wcols(cc)]) for l0, p0, n in pcs]
                    + [(s_ref.at[:, scols(cc)], os_ref.at[k, :, scols(cc)])])

        def copy(i, src, dst, to):
            return pltpu.make_async_remote_copy(src_ref=src, dst_ref=dst, send_sem=send_sems.at[i],
                                                recv_sem=recv_sems.at[i], device_id=to, device_id_type=MESH)

        def own_windows():
            return ([(w_ref.at[pl.ds(l0, n)], ow_ref.at[pl.ds(p0, n)]) for l0, p0, n in _piece_rows(k_me)]
                    + [(s_ref, os_ref.at[k_me])])

        first = [copy(18 + i, src, dst, sib) for i, (src, dst) in enumerate(own_windows())]
        for j, (cx, cy) in enumerate(chips):
            for i, (src, dst) in enumerate(windows(k_me, c)):
                first.append(copy(3 * j + i, src, dst, (cx, cy, c)))
        for cp in first:
            cp.start()
        passed = []
        for j, (cx, cy) in enumerate(chips):
            for i, (_, dst) in enumerate(windows(2 * cx + cy, c)):
                copy(3 * j + i, dst, dst, (cx, cy, c)).wait_recv()
                cp = copy(9 + 3 * j + i, dst, dst, sib)
                cp.start()
                passed.append(cp)
        for j, (cx, cy) in enumerate(chips):
            for i, (_, dst) in enumerate(windows(2 * cx + cy, 1 - c)):
                copy(9 + 3 * j + i, dst, dst, sib).wait_recv()
        for i, (_, dst) in enumerate(own_windows()):
            copy(18 + i, dst, dst, sib).wait_recv()
        for cp in first + passed:
            cp.wait_send()

    return pl.pallas_call(
        body,
        out_shape=[jax.ShapeDtypeStruct((NINP, CWD), F32), jax.ShapeDtypeStruct((4, PACK_ROWS, PACK_W), BF16)],
        in_specs=[ANY, ANY], out_specs=[ANY, ANY],
        scratch_shapes=[pltpu.SemaphoreType.DMA((21,)), pltpu.SemaphoreType.DMA((21,))],
        name="weights_all_gather",
    )(wc, sp)


HBM = pl.BlockSpec(memory_space=pltpu.HBM)
SEM = pl.BlockSpec(memory_space=pltpu.SEMAPHORE)
EFFECT = pltpu.SideEffectType.DATAFLOW_SIDE_EFFECTING


def _in_hbm(a):
    return pltpu.with_memory_space_constraint(a, pltpu.HBM)


def _ag_shard(w_ref, s_ref, lw_ref, ls_ref, k):
    return ([(w_ref.at[pl.ds(l0, n)], lw_ref.at[pl.ds(p0, n)]) for l0, p0, n in _piece_rows(k)]
            + [(s_ref, ls_ref.at[k])])


def _ag_behind_copies(w_ref, s_ref, lw_ref, ls_ref, send_sems, recv_sems):
    x, y, c = _me()
    peers = [(1 - x, y, c), (x, 1 - y, c), (1 - x, 1 - y, c), (x, y, 1 - c)]
    mine, theirs = [], []
    for j, (px, py, pc) in enumerate(peers):
        for i, ((src, dst), (_, got)) in enumerate(zip(_ag_shard(w_ref, s_ref, lw_ref, ls_ref, 2 * x + y),
                                                       _ag_shard(w_ref, s_ref, lw_ref, ls_ref, 2 * px + py))):
            mk = lambda s_, d_: pltpu.make_async_remote_copy(
                src_ref=s_, dst_ref=d_, send_sem=send_sems.at[3 * j + i], recv_sem=recv_sems.at[3 * j + i],
                device_id=(px, py, pc), device_id_type=MESH)
            mine.append(mk(src, dst))
            theirs.append(mk(got, got))
    return mine, theirs


def _ag_behind_start(wc, sp, dep):
    def body(w_ref, s_ref, lw_ref, ls_ref, dep_ref, send_sems, recv_sems, w_thru, s_thru, lw_thru, ls_thru, token):
        mine, _ = _ag_behind_copies(w_ref, s_ref, lw_ref, ls_ref, send_sems, recv_sems)
        for cp in mine:
            cp.start()
        token[...] = jnp.zeros_like(token)

    return pl.pallas_call(
        body, name="weights_gather_start",
        out_shape=(pltpu.SemaphoreType.DMA((12,)), pltpu.SemaphoreType.DMA((12,)), pltpu.HBM(wc.shape, wc.dtype),
                   pltpu.HBM(sp.shape, sp.dtype), pltpu.HBM((NINP, CWD), F32), pltpu.HBM((4, PACK_ROWS, PACK_W), BF16),
                   jax.ShapeDtypeStruct((8, LANE), F32)),
        in_specs=(HBM, HBM, HBM, HBM, ANY),
        out_specs=(SEM, SEM, HBM, HBM, HBM, HBM, pl.BlockSpec(memory_space=pltpu.VMEM)),
        input_output_aliases={0: 2, 1: 3, 2: 4, 3: 5},
        compiler_params=pltpu.CompilerParams(has_side_effects=EFFECT),
    )(_in_hbm(wc), _in_hbm(sp), _in_hbm(lax.empty((NINP, CWD), F32)), _in_hbm(lax.empty((4, PACK_ROWS, PACK_W), BF16)), dep)


def _ag_behind_wait(send_sems, recv_sems, wc, sp, lw, ls, after):
    def body(w_ref, s_ref, lw_ref, ls_ref, send_sems, recv_sems, after_ref, w_dead, s_dead, lw_out, ls_out):
        mine, theirs = _ag_behind_copies(w_ref, s_ref, lw_ref, ls_ref, send_sems, recv_sems)
        for cp in mine:
            cp.wait_send()
        for cp in theirs:
            cp.wait_recv()

    out = pl.pallas_call(
        body, name="weights_gather_wait",
        out_shape=(pltpu.HBM(wc.shape, wc.dtype), pltpu.HBM(sp.shape, sp.dtype), pltpu.HBM(lw.shape, lw.dtype),
                   pltpu.HBM(ls.shape, ls.dtype)),
        in_specs=(HBM, HBM, HBM, HBM, SEM, SEM, ANY), out_specs=(HBM, HBM, HBM, HBM),
        input_output_aliases={0: 0, 1: 1, 2: 2, 3: 3},
        compiler_params=pltpu.CompilerParams(has_side_effects=EFFECT),
    )(wc, sp, lw, ls, send_sems, recv_sems, after)
    return out[2], out[3]


def _rs_swap(gw, gs):
    def body(w_ref, s_ref, rw_ref, rs_ref, send_sems, recv_sems):
        x, y, c = _me()
        oc = 1 - c
        cps = [pltpu.make_async_remote_copy(src_ref=w_ref.at[:, pl.ds(pl.multiple_of(oc * (D // 2), LANE), D // 2)],
                                            dst_ref=rw_ref, send_sem=send_sems.at[0], recv_sem=recv_sems.at[0],
                                            device_id=(x, y, oc), device_id_type=MESH),
               pltpu.make_async_remote_copy(src_ref=s_ref.at[:, :, pl.ds(pl.multiple_of(oc * HW, LANE), HW)],
                                            dst_ref=rs_ref, send_sem=send_sems.at[1], recv_sem=recv_sems.at[1],
                                            device_id=(x, y, oc), device_id_type=MESH)]
        for cp in cps:
            cp.start()
        for cp in cps:
            cp.wait()

    return pl.pallas_call(
        body,
        out_shape=[jax.ShapeDtypeStruct((NINP, D // 2), F32), jax.ShapeDtypeStruct((4, PACK_ROWS, HW), F32)],
        in_specs=[ANY, ANY], out_specs=[ANY, ANY],
        scratch_shapes=[pltpu.SemaphoreType.DMA((2,)), pltpu.SemaphoreType.DMA((2,))],
        name="grads_sibling_swap",
    )(gw, gs)


def _rs_swap_copies(w_ref, s_ref, rw_ref, rs_ref, send_sems, recv_sems):
    x, y, c = _me()
    oc = 1 - c
    return [pltpu.make_async_remote_copy(src_ref=w_ref.at[:, pl.ds(pl.multiple_of(oc * (D // 2), LANE), D // 2)],
                                         dst_ref=rw_ref, send_sem=send_sems.at[0], recv_sem=recv_sems.at[0],
                                         device_id=(x, y, oc), device_id_type=MESH),
            pltpu.make_async_remote_copy(src_ref=s_ref.at[:, :, pl.ds(pl.multiple_of(oc * HW, LANE), HW)],
                                         dst_ref=rs_ref, send_sem=send_sems.at[1], recv_sem=recv_sems.at[1],
                                         device_id=(x, y, oc), device_id_type=MESH)]


def _rs_swap_start(gw, gs):
    def body(w_ref, s_ref, rw_ref, rs_ref, send_sems, recv_sems, w_thru, s_thru, rw_thru, rs_thru, token):
        for cp in _rs_swap_copies(w_ref, s_ref, rw_ref, rs_ref, send_sems, recv_sems):
            cp.start()
        token[...] = jnp.zeros_like(token)

    return pl.pallas_call(
        body, name="grads_swap_start",
        out_shape=(pltpu.SemaphoreType.DMA((2,)), pltpu.SemaphoreType.DMA((2,)), pltpu.HBM(gw.shape, gw.dtype),
                   pltpu.HBM(gs.shape, gs.dtype), pltpu.HBM((NINP, D // 2), F32), pltpu.HBM((4, PACK_ROWS, HW), F32),
                   jax.ShapeDtypeStruct((8, LANE), F32)),
        in_specs=(HBM, HBM, HBM, HBM),
        out_specs=(SEM, SEM, HBM, HBM, HBM, HBM, pl.BlockSpec(memory_space=pltpu.VMEM)),
        input_output_aliases={0: 2, 1: 3, 2: 4, 3: 5},
        compiler_params=pltpu.CompilerParams(has_side_effects=EFFECT),
    )(_in_hbm(gw), _in_hbm(gs), _in_hbm(lax.empty((NINP, D // 2), F32)), _in_hbm(lax.empty((4, PACK_ROWS, HW), F32)))


def _rs_swap_wait(send_sems, recv_sems, gw, gs, rw, rs, after):
    def body(w_ref, s_ref, rw_ref, rs_ref, send_sems, recv_sems, after_ref, w_out, s_out, rw_out, rs_out):
        for cp in _rs_swap_copies(w_ref, s_ref, rw_ref, rs_ref, send_sems, recv_sems):
            cp.wait()

    return pl.pallas_call(
        body, name="grads_swap_wait",
        out_shape=(pltpu.HBM(gw.shape, gw.dtype), pltpu.HBM(gs.shape, gs.dtype), pltpu.HBM(rw.shape, rw.dtype),
                   pltpu.HBM(rs.shape, rs.dtype)),
        in_specs=(HBM, HBM, HBM, HBM, SEM, SEM, ANY), out_specs=(HBM, HBM, HBM, HBM),
        input_output_aliases={0: 0, 1: 1, 2: 2, 3: 3},
        compiler_params=pltpu.CompilerParams(has_side_effects=EFFECT),
    )(gw, gs, rw, rs, send_sems, recv_sems, after)


SUM_BR = 512


def _rs_chip_sum_w(gw, rw, cidx):
    def body(c_ref, g_ref, r_ref, o_ref):
        s = g_ref[...] + r_ref[...]
        q = D // 8
        o_ref[...] = jnp.concatenate([_pack_words(s[:, 0:q], s[:, q:2 * q]),
                                      _pack_words(s[:, 2 * q:3 * q], s[:, 3 * q:4 * q])], axis=1)

    return pl.pallas_call(
        body,
        grid_spec=pltpu.PrefetchScalarGridSpec(
            num_scalar_prefetch=1, grid=(NINP // SUM_BR,),
            in_specs=[pl.BlockSpec((SUM_BR, D // 2), lambda i, cr: (i, cr[0])),
                      pl.BlockSpec((SUM_BR, D // 2), lambda i, cr: (i, 0))],
            out_specs=pl.BlockSpec((SUM_BR, D // 4), lambda i, cr: (i, 0))),
        out_shape=jax.ShapeDtypeStruct((NINP, D // 4), F32),
        name="grads_chip_sum_w",
        compiler_params=pltpu.CompilerParams(dimension_semantics=("arbitrary",), vmem_limit_bytes=VMEM_LIMIT),
    )(cidx, gw, rw)


def _rs_chip_sum_s(gs, rs, cidx):
    def body(c_ref, g_ref, r_ref, o_ref):
        o_ref[...] = (g_ref[...] + r_ref[...]).astype(BF16)

    return pl.pallas_call(
        body,
        grid_spec=pltpu.PrefetchScalarGridSpec(
            num_scalar_prefetch=1, grid=(4,),
            in_specs=[pl.BlockSpec((None, PACK_ROWS, HW), lambda j, cr: (j, 0, cr[0])),
                      pl.BlockSpec((None, PACK_ROWS, HW), lambda j, cr: (j, 0, 0))],
            out_specs=pl.BlockSpec((None, PACK_ROWS, HW), lambda j, cr: (j, 0, 0))),
        out_shape=jax.ShapeDtypeStruct((4, PACK_ROWS, HW), BF16),
        name="grads_chip_sum_s",
        compiler_params=pltpu.CompilerParams(dimension_semantics=("arbitrary",), vmem_limit_bytes=VMEM_LIMIT),
    )(cidx, gs, rs)


def _rs_exchange_copies(sw_ref, ss_ref, r2w_ref, r2s_ref, send_sems, recv_sems):
    x, y, c = _me()
    mine, theirs = [], []
    for j, (cx, cy) in enumerate([(1 - x, y), (x, 1 - y), (1 - x, 1 - y)]):
        def mk(i, src, dst):
            return pltpu.make_async_remote_copy(src_ref=src, dst_ref=dst, send_sem=send_sems.at[3 * j + i],
                                                recv_sem=recv_sems.at[3 * j + i], device_id=(cx, cy, c), device_id_type=MESH)
        for i, (l0, p0, n) in enumerate(_piece_rows(2 * cx + cy)):
            mine.append(mk(i, sw_ref.at[pl.ds(p0, n)], r2w_ref.at[j, pl.ds(l0, n)]))
            theirs.append(mk(i, r2w_ref.at[j, pl.ds(l0, n)], r2w_ref.at[j, pl.ds(l0, n)]))
        mine.append(mk(2, ss_ref.at[2 * cx + cy], r2s_ref.at[j]))
        theirs.append(mk(2, r2s_ref.at[j], r2s_ref.at[j]))
    return mine, theirs


def _rs_exchange_start(sw, ss, tag):
    def body(sw_ref, ss_ref, r2w_ref, r2s_ref, send_sems, recv_sems, sw_thru, ss_thru, r2w_thru, r2s_thru, token):
        mine, _ = _rs_exchange_copies(sw_ref, ss_ref, r2w_ref, r2s_ref, send_sems, recv_sems)
        for cp in mine:
            cp.start()
        token[...] = jnp.zeros_like(token)

    return pl.pallas_call(
        body, name="grads_exchange_start_" + tag,
        out_shape=(pltpu.SemaphoreType.DMA((9,)), pltpu.SemaphoreType.DMA((9,)), pltpu.HBM(sw.shape, sw.dtype),
                   pltpu.HBM(ss.shape, ss.dtype), pltpu.HBM((3, WSH, D // 4), F32), pltpu.HBM((3, PACK_ROWS, HW), BF16),
                   jax.ShapeDtypeStruct((8, LANE), F32)),
        in_specs=(HBM, HBM, HBM, HBM),
        out_specs=(SEM, SEM, HBM, HBM, HBM, HBM, pl.BlockSpec(memory_space=pltpu.VMEM)),
        input_output_aliases={0: 2, 1: 3, 2: 4, 3: 5},
        compiler_params=pltpu.CompilerParams(has_side_effects=EFFECT),
    )(_in_hbm(sw), _in_hbm(ss), _in_hbm(lax.empty((3, WSH, D // 4), F32)), _in_hbm(lax.empty((3, PACK_ROWS, HW), BF16)))


def _rs_exchange_wait(send_sems, recv_sems, sw, ss, r2w, r2s, after, tag):
    def body(sw_ref, ss_ref, r2w_ref, r2s_ref, send_sems, recv_sems, after_ref, sw_dead, ss_dead, r2w_out, r2s_out):
        mine, theirs = _rs_exchange_copies(sw_ref, ss_ref, r2w_ref, r2s_ref, send_sems, recv_sems)
        for cp in mine:
            cp.wait_send()
        for cp in theirs:
            cp.wait_recv()

    out = pl.pallas_call(
        body, name="grads_exchange_wait_" + tag,
        out_shape=(pltpu.HBM(sw.shape, sw.dtype), pltpu.HBM(ss.shape, ss.dtype), pltpu.HBM(r2w.shape, r2w.dtype),
                   pltpu.HBM(r2s.shape, r2s.dtype)),
        in_specs=(HBM, HBM, HBM, HBM, SEM, SEM, ANY), out_specs=(HBM, HBM, HBM, HBM),
        input_output_aliases={0: 0, 1: 1, 2: 2, 3: 3},
        compiler_params=pltpu.CompilerParams(has_side_effects=EFFECT),
    )(sw, ss, r2w, r2s, send_sems, recv_sems, after)
    return out[2], out[3]


def _rs_final_w(gw, rw, r2w, idx):
    q = D // 8

    def body(i_ref, g_ref, r_ref, p_ref, o_ref, gbuf, rbuf, sems):
        i = pl.program_id(0)
        k, c = i_ref[0], i_ref[1]
        cps = []
        for n_, (l0, p0, n) in enumerate(_piece_rows(k)):
            gcol = pl.ds(pl.multiple_of(c * (D // 2) + i * 2 * q, LANE), 2 * q)
            rcol = pl.ds(pl.multiple_of(i * 2 * q, LANE), 2 * q)
            cps.append(pltpu.make_async_copy(g_ref.at[pl.ds(p0, n), gcol], gbuf.at[pl.ds(l0, n)], sems.at[2 * n_]))
            cps.append(pltpu.make_async_copy(r_ref.at[pl.ds(p0, n), rcol], rbuf.at[pl.ds(l0, n)], sems.at[2 * n_ + 1]))
        for cp in cps:
            cp.start()
        for cp in cps:
            cp.wait()
        acc = gbuf[...] + rbuf[...]
        for j in range(3):
            lo, hi = _unpack_words(p_ref[j])
            acc = acc + jnp.concatenate([lo, hi], axis=1)
        o_ref[...] = acc

    return pl.pallas_call(
        body,
        grid_spec=pltpu.PrefetchScalarGridSpec(
            num_scalar_prefetch=1, grid=(2,),
            in_specs=[ANY, ANY, pl.BlockSpec((3, WSH, q), lambda i, ir: (0, 0, i))],
            out_specs=pl.BlockSpec((WSH, 2 * q), lambda i, ir: (0, 2 * ir[1] + i)),
            scratch_shapes=[pltpu.VMEM((WSH, 2 * q), F32), pltpu.VMEM((WSH, 2 * q), F32), pltpu.SemaphoreType.DMA((4,))]),
        out_shape=jax.ShapeDtypeStruct((WSH, D), F32),
        name="grads_final_sum_w",
        compiler_params=pltpu.CompilerParams(dimension_semantics=("arbitrary",), vmem_limit_bytes=VMEM_LIMIT),
    )(idx, gw, rw, r2w)


def _rs_final_s(gs, rs, r2s, idx):
    def body(i_ref, g_ref, r_ref, p_ref, o_ref):
        acc = g_ref[...] + r_ref[...]
        for j in range(3):
            acc = acc + p_ref[j].astype(F32)
        o_ref[...] = acc

    return pl.pallas_call(
        body,
        grid_spec=pltpu.PrefetchScalarGridSpec(
            num_scalar_prefetch=1, grid=(1,),
            in_specs=[pl.BlockSpec((None, PACK_ROWS, HW), lambda i, ir: (ir[0], 0, ir[1])),
                      pl.BlockSpec((None, PACK_ROWS, HW), lambda i, ir: (ir[0], 0, 0)),
                      pl.BlockSpec((3, PACK_ROWS, HW), lambda i, ir: (0, 0, 0))],
            out_specs=pl.BlockSpec((PACK_ROWS, HW), lambda i, ir: (0, ir[1]))),
        out_shape=jax.ShapeDtypeStruct((PACK_ROWS, PACK_W), F32),
        name="grads_final_sum_s",
        compiler_params=pltpu.CompilerParams(dimension_semantics=("arbitrary",), vmem_limit_bytes=VMEM_LIMIT),
    )(idx, gs, rs, r2s)


def _rs_share(fw, fs):
    def body(w_ref, s_ref, ow_ref, os_ref, send_sems, recv_sems):
        x, y, c = _me()
        wcol = lambda cc: pl.ds(pl.multiple_of(cc * (D // 2), LANE), D // 2)
        scol = lambda cc: pl.ds(pl.multiple_of(cc * HW, LANE), HW)

        def copies(cc):
            return [pltpu.make_async_remote_copy(src_ref=w_ref.at[:, wcol(cc)], dst_ref=ow_ref.at[:, wcol(cc)],
                                                 send_sem=send_sems.at[0], recv_sem=recv_sems.at[0],
                                                 device_id=(x, y, 1 - c), device_id_type=MESH),
                    pltpu.make_async_remote_copy(src_ref=s_ref.at[:, scol(cc)], dst_ref=os_ref.at[:, scol(cc)],
                                                 send_sem=send_sems.at[1], recv_sem=recv_sems.at[1],
                                                 device_id=(x, y, 1 - c), device_id_type=MESH)]
        out = copies(c)
        for cp in out:
            cp.start()
        for cp in copies(1 - c):
            cp.wait_recv()
        for cp in out:
            cp.wait_send()

    return pl.pallas_call(
        body,
        out_shape=[jax.ShapeDtypeStruct(fw.shape, F32), jax.ShapeDtypeStruct(fs.shape, F32)],
        in_specs=[ANY, ANY], out_specs=[ANY, ANY],
        input_output_aliases={0: 0, 1: 1},
        scratch_shapes=[pltpu.SemaphoreType.DMA((2,)), pltpu.SemaphoreType.DMA((2,))],
        name="grads_share",
    )(fw, fs)


def _rs_sums(gw, gs, rw, rs):
    x, y, c = _me()
    cidx = jnp.reshape(c, (1,)).astype(jnp.int32)
    return dict(gw=gw, gs=gs, rw=rw, rs=rs, sw=_rs_chip_sum_w(gw, rw, cidx), ss=_rs_chip_sum_s(gs, rs, cidx))


def _rs_begin(gw, gs):
    return _rs_sums(gw, gs, *_rs_swap(gw, gs))


def _rs_end(st, r2w, r2s):
    x, y, c = _me()
    idx = jnp.stack([2 * x + y, c]).astype(jnp.int32)
    return _rs_share(_rs_final_w(st["gw"], st["rw"], r2w, idx), _rs_final_s(st["gs"], st["rs"], r2s, idx))


def _all_reduce_small(gs):
    rows = gs.shape[0]

    def body(g_ref, o_ref, buf, send_sems, recv_sems):
        x, y, c = _me()
        me = 4 * x + 2 * y + c
        buf[me] = g_ref[...]
        cps = []
        for r in range(1, 8):
            fx, fy, fc = (r >> 2) & 1, (r >> 1) & 1, r & 1
            px, py, pc = jnp.bitwise_xor(x, fx), jnp.bitwise_xor(y, fy), jnp.bitwise_xor(c, fc)
            cps.append((pltpu.make_async_remote_copy(
                src_ref=g_ref, dst_ref=buf.at[me], send_sem=send_sems.at[r - 1], recv_sem=recv_sems.at[r - 1],
                device_id=(px, py, pc), device_id_type=MESH), 4 * px + 2 * py + pc))
        for cp, _ in cps:
            cp.start()
        for r, (cp, peer) in enumerate(cps):
            pltpu.make_async_remote_copy(
                src_ref=g_ref, dst_ref=buf.at[peer], send_sem=send_sems.at[r], recv_sem=recv_sems.at[r],
                device_id=(x, y, c), device_id_type=MESH).wait_recv()
        for cp, _ in cps:
            cp.wait_send()
        acc = buf[0]
        for k in range(1, 8):
            acc = acc + buf[k]
        o_ref[...] = acc

    return pl.pallas_call(
        body,
        out_shape=jax.ShapeDtypeStruct((rows, LANE), F32),
        in_specs=[pl.BlockSpec(memory_space=pltpu.VMEM)],
        out_specs=pl.BlockSpec(memory_space=pltpu.VMEM),
        scratch_shapes=[pltpu.VMEM((8, rows, LANE), F32), pltpu.SemaphoreType.DMA((7,)), pltpu.SemaphoreType.DMA((7,))],
        name="small_grads_all_reduce",
    )(gs)


PACK_SPLIT = (("w_uq", 96, (QL, 192)), ("w_ukv", 64, (KVL, 256)),
              ("w_out_a", 256, (CW, 256)), ("w_out_b", 256, (CW, 256)), ("w_out_c", 256, (CW, 256)),
              ("w_o", 512, (256, D)))
MAT_ROWS = 1440
CONV_SHARD = 3 * 128


def _w_in_words(w_in_shard):
    t = w_in_shard.T
    return _pack_words(t[:, :CWD], t[:, CWD:])


def _pack_weights(wl):
    parts = [wl[n].astype(BF16).reshape(-1, PACK_W) for n, _, _ in PACK_SPLIT]
    cw = wl["conv_w"].reshape(-1)
    hi = cw.astype(BF16)
    r1 = cw - hi.astype(F32)
    mid = r1.astype(BF16)
    lo = (r1 - mid.astype(F32)).astype(BF16)
    cterms = jnp.pad(jnp.concatenate([hi, mid, lo]), (0, 3 * PACK_W - 3 * CONV_SHARD)).reshape(3, PACK_W)
    tail = jnp.pad(cterms, ((0, PACK_ROWS - MAT_ROWS - 3), (0, 0)))
    return jnp.concatenate(parts + [tail], axis=0)


def _unpack_weights(gath):
    out = {}
    r = 0
    for n, nrows, shp in PACK_SPLIT:
        t = gath[:, r:r + nrows].reshape((4,) + shp)
        r += nrows
        if n == "w_o":
            out[n] = t.reshape(4 * shp[0], shp[1])
        else:
            out[n] = t.transpose(1, 0, 2).reshape(shp[0], 4 * shp[1])
    ct = gath[:, r:r + 3].reshape(4, 3 * PACK_W)[:, :3 * CONV_SHARD].astype(F32).reshape(4, 3, CONV_SHARD)
    cw = (ct[:, 0] + ct[:, 1]) + ct[:, 2]
    out["conv_w"] = cw.reshape(4, 3, 128).transpose(1, 0, 2).reshape(3, CW)
    return out


def _pack_grads(g):
    parts = []
    for n, nrows, shp in PACK_SPLIT:
        t = g[n]
        if n == "w_o":
            t = t.reshape((4,) + shp)
        else:
            t = t.reshape(shp[0], 4, shp[1]).transpose(1, 0, 2)
        parts.append(t.reshape(4, nrows, PACK_W))
    cw = g["conv_w"].reshape(3, 4, 128).transpose(1, 0, 2).reshape(4, 1, CONV_SHARD)
    parts.append(jnp.pad(cw, ((0, 0), (0, PACK_ROWS - MAT_ROWS - 1), (0, PACK_W - CONV_SHARD))))
    return jnp.concatenate(parts, axis=1)


def _unpack_grads(red):
    out = {}
    r = 0
    for n, nrows, shp in PACK_SPLIT:
        out[n] = red[r:r + nrows].reshape(shp)
        r += nrows
    out["conv_w"] = red[r, :CONV_SHARD].reshape(3, 128)
    return out


SMALL_SIZES = (("norm_g", D), ("b_gate", 3 * D), ("conv_b", CW), ("q_a_norm_g", QL), ("kv_a_norm_g", KVL),
               ("mla_q_norm_g", QK), ("mla_k_norm_g", QK), ("dil_q_norm_g", NG * HD), ("dil_k_norm_g", NG * HD))
SMALL_ROWS = 88


def _pack_small(per_name):
    flat = jnp.concatenate([per_name[n].reshape(-1).astype(F32) for n, _ in SMALL_SIZES])
    return jnp.pad(flat, (0, SMALL_ROWS * LANE - flat.shape[0])).reshape(SMALL_ROWS, LANE)


def _unpack_small(packed, like):
    out = {}
    flat = packed.reshape(-1)
    r = 0
    for n, sz in SMALL_SIZES:
        out[n] = flat[r:r + NL * sz].reshape(like[n].shape)
        r += NL * sz
    return out


def _adamw_math(w, g, m, v):
    m = ADAM_B1 * m + (1.0 - ADAM_B1) * g
    v = ADAM_B2 * v + (1.0 - ADAM_B2) * jnp.square(g)
    m_hat = m / (1.0 - ADAM_B1 ** ADAM_STEP)
    v_hat = v / (1.0 - ADAM_B2 ** ADAM_STEP)
    delta = -ADAM_LR * (m_hat / (jnp.sqrt(v_hat) + ADAM_EPS) + ADAM_WD * w)
    return delta, m, v


def _adamw(name, w, g, m, v, br, bc=None):
    L, R, C = w.shape
    bc = C if bc is None else bc
    blk = lambda l, i, j: (l, i, j)
    return _pcall(name, _adamw_math, (L, R // br, C // bc), [(t, (None, br, bc), blk) for t in (w, g, m, v)],
                  [((L, R, C), F32, (None, br, bc), blk)] * 3)


ADAM_ROWS = {"w_uq": 256, "w_ukv": 128, "w_out_a": 512, "w_out_b": 512, "w_out_c": 512, "w_o": 256,
             "conv_w": 3}


def kernel(x, norm_g, w_in, b_gate, conv_w, conv_b, q_a_norm_g, w_uq, kv_a_norm_g, w_ukv, mla_q_norm_g, mla_k_norm_g, dil_q_norm_g, dil_k_norm_g, w_out_a, w_out_b, w_out_c, w_o, loss_target, m_norm_g, m_w_in, m_b_gate, m_conv_w, m_conv_b, m_q_a_norm_g, m_w_uq, m_kv_a_norm_g, m_w_ukv, m_mla_q_norm_g, m_mla_k_norm_g, m_dil_q_norm_g, m_dil_k_norm_g, m_w_out_a, m_w_out_b, m_w_out_c, m_w_o, v_norm_g, v_w_in, v_b_gate, v_conv_w, v_conv_b, v_q_a_norm_g, v_w_uq, v_kv_a_norm_g, v_w_ukv, v_mla_q_norm_g, v_mla_k_norm_g, v_dil_q_norm_g, v_dil_k_norm_g, v_w_out_a, v_w_out_b, v_w_out_c, v_w_o):
    W = dict(norm_g=norm_g, w_in=w_in, b_gate=b_gate, conv_w=conv_w, conv_b=conv_b, q_a_norm_g=q_a_norm_g, w_uq=w_uq,
             kv_a_norm_g=kv_a_norm_g, w_ukv=w_ukv, mla_q_norm_g=mla_q_norm_g, mla_k_norm_g=mla_k_norm_g,
             dil_q_norm_g=dil_q_norm_g, dil_k_norm_g=dil_k_norm_g, w_out_a=w_out_a, w_out_b=w_out_b, w_out_c=w_out_c,
             w_o=w_o)
    M = dict(norm_g=m_norm_g, w_in=m_w_in, b_gate=m_b_gate, conv_w=m_conv_w, conv_b=m_conv_b, q_a_norm_g=m_q_a_norm_g,
             w_uq=m_w_uq, kv_a_norm_g=m_kv_a_norm_g, w_ukv=m_w_ukv, mla_q_norm_g=m_mla_q_norm_g,
             mla_k_norm_g=m_mla_k_norm_g, dil_q_norm_g=m_dil_q_norm_g, dil_k_norm_g=m_dil_k_norm_g, w_out_a=m_w_out_a,
             w_out_b=m_w_out_b, w_out_c=m_w_out_c, w_o=m_w_o)
    V = dict(norm_g=v_norm_g, w_in=v_w_in, b_gate=v_b_gate, conv_w=v_conv_w, conv_b=v_conv_b, q_a_norm_g=v_q_a_norm_g,
             w_uq=v_w_uq, kv_a_norm_g=v_kv_a_norm_g, w_ukv=v_w_ukv, mla_q_norm_g=v_mla_q_norm_g,
             mla_k_norm_g=v_mla_k_norm_g, dil_q_norm_g=v_dil_q_norm_g, dil_k_norm_g=v_dil_k_norm_g, w_out_a=v_w_out_a,
             w_out_b=v_w_out_b, w_out_c=v_w_out_c, w_o=v_w_o)
    batch = x.shape[0]
    T = batch * S

    def layer_weights(l, cont, gath):
        full = _unpack_weights(gath)
        pad_qk = lambda t: jnp.pad(t, (0, QKP - QK)).reshape(1, QKP)
        full.update(
            w_in_t=cont,
            norm_g=norm_g[l].reshape(1, D), b_gate=b_gate[l].reshape(1, 3 * D), conv_b=conv_b[l].reshape(1, CW),
            q_a_norm_g=q_a_norm_g[l].reshape(1, QL), kv_a_norm_g=kv_a_norm_g[l].reshape(1, KVL),
            mla_q_norm_g=pad_qk(mla_q_norm_g[l]), mla_k_norm_g=pad_qk(mla_k_norm_g[l]),
            dil_q_norm_g=dil_q_norm_g[l].reshape(NG, 1, HD), dil_k_norm_g=dil_k_norm_g[l].reshape(NG, 1, HD))
        return full

    words = [_w_in_words(w_in[l]) for l in range(NL)]
    packs = [_pack_weights({n: W[n][l] for n in BIG[1:] + ("conv_w",)}) for l in range(NL)]
    tabs = _rope_tables() + (_dil_slopes(),)
    x2 = x.reshape(T, D)

    cont0, gath0 = _all_gather(words[0], packs[0])
    w0 = layer_weights(0, cont0, gath0)
    ag = _ag_behind_start(words[1], packs[1], gath0)
    w0["norm_g"] = w0["norm_g"] + ag[6][0:1, 0:1]
    y0, res0 = _layer_fwd(x2, w0, tabs, batch)
    w1 = layer_weights(1, *_ag_behind_wait(ag[0], ag[1], ag[2], ag[3], ag[4], ag[5], y0))
    y1, res1 = _layer_fwd(y0, w1, tabs, batch)

    row = lambda i: (i, 0)
    dy, loss = _pcall("loss", _loss_math, (T // BR,),
                      [(y1, (BR, D), row), (loss_target.reshape(T, D), (BR, D), row)],
                      [((T, D), F32, (BR, D), row), ((1, 1), F32, (1, 1), lambda i: (0, 0), True)])
    loss = lax.psum(loss[0, 0], ("x", "y", "c"))

    grads = [None] * NL
    dy, grads[1] = _layer_bwd(dy, w1, res1, tabs, batch)
    st = [None] * NL
    ex = [None] * NL
    sw1 = _rs_swap_start(grads[1]["w_in_t"], _pack_grads(grads[1]))
    w0["w_o"] = w0["w_o"] + sw1[6][0:1, 0:1].astype(BF16)

    def exchange_layer1(t):
        st[1] = _rs_sums(*_rs_swap_wait(*sw1[:6], t))
        ex[1] = _rs_exchange_start(st[1]["sw"], st[1]["ss"], "1")
        return ex[1][6]

    def start_layer0(g):
        st[0] = _rs_begin(g["w_in_t"], _pack_grads(g))
        ex[0] = _rs_exchange_start(st[0]["sw"], st[0]["ss"], "0")
        return ex[0][6]

    dx, grads[0] = _layer_bwd(dy, w0, res0, tabs, batch, after_dw=start_layer0, after_merge=exchange_layer1)
    grad_x = dx.reshape(batch, S, D)

    red = [None] * NL
    for l in (1, 0):
        r2w, r2s = _rs_exchange_wait(*ex[l][:6], dx, str(l))
        rw, rs = _rs_end(st[l], r2w, r2s)
        r = _unpack_grads(rs)
        r["w_in_t"] = rw
        red[l] = r
    G = {n: jnp.stack([red[l][n] for l in range(NL)]) for n in BIG[1:] + ("conv_w",)}
    g_in_t = jnp.stack([red[l]["w_in_t"] for l in range(NL)])
    G["w_in"] = jnp.swapaxes(g_in_t, 1, 2)
    small_g = {n: jnp.stack([grads[l][n].reshape(-1)[:sz] for l in range(NL)]) for n, sz in SMALL_SIZES}
    small_red = _all_reduce_small(_pack_small(small_g))
    G.update(_unpack_small(small_red, {n: W[n] for n in SMALL}))

    delta, new_m, new_v = {}, {}, {}
    for n in BIG[1:] + ("conv_w",):
        delta[n], new_m[n], new_v[n] = _adamw("adamw_" + n, W[n], G[n], M[n], V[n], ADAM_ROWS[n])
    tr = lambda t: jnp.swapaxes(t, 1, 2)
    delta["w_in"], new_m["w_in"], new_v["w_in"] = (
        tr(t) for t in _adamw("adamw_w_in", tr(w_in), g_in_t, tr(m_w_in), tr(v_w_in), WSH, LANE))
    sw, sm, sv = (_pack_small({n: t[n] for n in SMALL})[None] for t in (W, M, V))
    sd, snm, snv = _adamw("adamw_small", sw, small_red[None], sm, sv, SMALL_ROWS)
    like = {n: W[n] for n in SMALL}
    delta.update(_unpack_small(sd[0], like))
    new_m.update(_unpack_small(snm[0], like))
    new_v.update(_unpack_small(snv[0], like))

    return (loss, grad_x, *[G[n] for n in WEIGHTS], *[delta[n] for n in WEIGHTS],
            *[new_m[n] for n in WEIGHTS], *[new_v[n] for n in WEIGHTS])
```

```python
import functools

import numpy as np
import jax
import jax.numpy as jnp
from jax import lax
from jax.experimental import pallas as pl
from jax.experimental.pallas import tpu as pltpu

F32 = jnp.float32
BF16 = jnp.bfloat16

D = 1024
S = 2048
NL = 2
CW = 512
NH = 8
QL = 256
KVL = 128
NOPE = 64
ROPE = 32
VD = 64
QK = NOPE + ROPE
QKP = 128
ROPE_THETA = 10000.0
DIL = ((128, 1), (512, 4), (2048, 16))
NG = 3
DH = 8
HD = 64
DWID = DH * HD
QB = 128
EPS = 1e-6
NIN = 11168
NINP = 11264
O_A, O_CQ, O_CKV, O_KPE, O_BZ, O_DQ, O_DK, O_DV, O_CZ, O_G = 0, 2048, 2304, 2432, 2560, 3072, 4608, 6144, 7680, 8192
KPE_END = 2464
NEG = -1e30
MLA_SCALE = QK ** -0.5
DIL_SCALE = HD ** -0.5
LANE = 128
PACK_W = 512
VMEM_LIMIT = 48 * 1024 * 1024

ADAM_LR = 0.001
ADAM_B1 = 0.9
ADAM_B2 = 0.999
ADAM_EPS = 1e-08
ADAM_WD = 0.01
ADAM_STEP = 10

MESH = pl.DeviceIdType.MESH
BIG = ("w_in", "w_uq", "w_ukv", "w_out_a", "w_out_b", "w_out_c", "w_o")
SMALL = ("norm_g", "b_gate", "conv_b", "q_a_norm_g", "kv_a_norm_g", "mla_q_norm_g", "mla_k_norm_g",
         "dil_q_norm_g", "dil_k_norm_g")
WEIGHTS = ("norm_g", "w_in", "b_gate", "conv_w", "conv_b", "q_a_norm_g", "w_uq", "kv_a_norm_g", "w_ukv",
           "mla_q_norm_g", "mla_k_norm_g", "dil_q_norm_g", "dil_k_norm_g", "w_out_a", "w_out_b", "w_out_c", "w_o")


def _dot(a, b):
    return jnp.dot(a, b, preferred_element_type=F32)


def _dot_nt(a, b):
    return lax.dot_general(a, b, (((1,), (1,)), ((), ())), preferred_element_type=F32)


def _dot_tn(a, b):
    return lax.dot_general(a, b, (((0,), (0,)), ((), ())), preferred_element_type=F32)


def _grid_step(grid):
    step = pl.program_id(0)
    for a in range(1, len(grid)):
        step = step * grid[a] + pl.program_id(a)
    n = 1
    for g in grid:
        n *= g
    return step, n


def _write_windows(buf_ref, stages, sems, step, nsteps, puts):
    slot = step % 2
    for t, (v, dst) in enumerate(puts):
        cp = pltpu.make_async_copy(stages[t].at[slot], dst, sems.at[t, slot])

        @pl.when(step >= 2)
        def _():
            cp.wait()

        stages[t][slot] = v.astype(stages[t].dtype).reshape(stages[t].shape[1:])
        cp.start()

    @pl.when(step == nsteps - 1)
    def _():
        for t, (v, dst) in enumerate(puts):
            pltpu.make_async_copy(stages[t].at[slot], dst, sems.at[t, slot]).wait()
            if nsteps > 1:
                pltpu.make_async_copy(stages[t].at[1 - slot], dst, sems.at[t, 1 - slot]).wait()


def _pcall(name, fn, grid, ins, outs, into=None):
    n_in = len(ins)
    n_out = len(outs)
    acc_axis = len(grid) - 1
    is_acc = [len(o) > 4 and o[4] for o in outs]
    outs = [o[:4] for o in outs]
    targets = into[1] if into is not None else []
    n_t = len(targets)

    def body(*refs):
        vals = fn(*[r[...].astype(F32) for r in refs[:n_in]])
        if not isinstance(vals, (tuple, list)):
            vals = (vals,)
        o0 = n_in + (1 if n_t else 0)
        for k in range(n_out):
            r = refs[o0 + k]
            v = vals[k].astype(r.dtype).reshape(r.shape)
            if is_acc[k]:
                first = pl.program_id(acc_axis) == 0

                @pl.when(first)
                def _():
                    r[...] = v

                @pl.when(jnp.logical_not(first))
                def _():
                    r[...] += v
            else:
                r[...] = v
        if n_t:
            buf_ref = refs[o0 + n_out]
            stages = refs[o0 + n_out + 1:o0 + n_out + 1 + n_t]
            ids = [pl.program_id(a) for a in range(len(grid))]
            step, nsteps = _grid_step(grid)
            _write_windows(buf_ref, stages, refs[-1], step, nsteps,
                           [(vals[n_out + t], targets[t][1](buf_ref, *ids)) for t in range(n_t)])

    in_specs = [pl.BlockSpec(bs, im) for _, bs, im in ins]
    out_specs = [pl.BlockSpec(bs, im) for _, _, bs, im in outs]
    out_shape = [jax.ShapeDtypeStruct(sh, dt) for sh, dt, _, _ in outs]
    args = [a for a, _, _ in ins]
    extra = {}
    if n_t:
        buf = into[0]
        in_specs.append(pl.BlockSpec(memory_space=pl.ANY))
        out_specs.append(pl.BlockSpec(memory_space=pl.ANY))
        out_shape.append(jax.ShapeDtypeStruct(buf.shape, buf.dtype))
        args.append(buf)
        extra = dict(input_output_aliases={n_in: n_out},
                     scratch_shapes=[pltpu.VMEM((2,) + tuple(bs), buf.dtype) for bs, _ in targets]
                     + [pltpu.SemaphoreType.DMA((n_t, 2))])
    return pl.pallas_call(
        body,
        grid=grid,
        in_specs=in_specs,
        out_specs=out_specs,
        out_shape=out_shape,
        name=name,
        compiler_params=pltpu.CompilerParams(
            dimension_semantics=("arbitrary",) * len(grid), vmem_limit_bytes=VMEM_LIMIT),
        **extra,
    )(*args)


def _mm(name, a, b, *, ta=False, tb=False, out_dtype=F32, add=None, dep=None, b_words=False, tm=2048, tn=1024, tk=1024):
    if ta:
        K, M = a.shape
    else:
        M, K = a.shape
    bshape = (b.shape[0], 2 * b.shape[1]) if b_words else b.shape
    if tb:
        N, K2 = bshape
    else:
        K2, N = bshape
    assert K == K2, (name, a.shape, b.shape)
    tm, tn, tk = min(tm, M), min(tn, N), min(tk, K)
    assert M % tm == 0 and N % tn == 0 and K % tk == 0, (name, M, N, K)
    nk = K // tk
    dims = (((0 if ta else 1,), (1 if tb else 0,)), ((), ()))
    a_spec = pl.BlockSpec((tk, tm), lambda j, i, k: (k, i)) if ta else pl.BlockSpec((tm, tk), lambda j, i, k: (i, k))
    bw = 2 if b_words else 1
    assert not b_words or (tk if tb else tn) == bshape[1]
    b_spec = (pl.BlockSpec((tn, tk // bw), lambda j, i, k: (j, k)) if tb
              else pl.BlockSpec((tk, tn // bw), lambda j, i, k: (k, j)))
    o_spec = pl.BlockSpec((tm, tn), lambda j, i, k: (i, j))
    has_add = add is not None
    n_in = 2 + has_add + (dep is not None)

    def body(*refs):
        a_ref, b_ref = refs[0], refs[1]
        add_ref = refs[2] if has_add else None
        o_ref = refs[n_in]
        bb = b_ref[...]
        if b_words:
            lo, hi = _unpack_words(bb)
            first = (pl.program_id(0) * tn) if tb else (pl.program_id(2) * tk)
            r = first + lax.broadcasted_iota(jnp.int32, lo.shape, 0)
            pad = jnp.logical_and(r >= KPE_END, r < KPE_END + NINP - NIN)
            bb = jnp.concatenate([jnp.where(pad, 0.0, lo), jnp.where(pad, 0.0, hi)], axis=1)
        p = lax.dot_general(a_ref[...].astype(BF16), bb.astype(BF16), dims, preferred_element_type=F32)
        if nk == 1:
            if has_add:
                p = p + add_ref[...]
            o_ref[...] = p.astype(out_dtype)
        else:
            acc = refs[-1]
            k = pl.program_id(2)

            @pl.when(k == 0)
            def _():
                acc[...] = p

            @pl.when(k > 0)
            def _():
                acc[...] += p

            @pl.when(k == nk - 1)
            def _():
                r = acc[...]
                if has_add:
                    r = r + add_ref[...]
                o_ref[...] = r.astype(out_dtype)

    in_specs = [a_spec, b_spec] + ([o_spec] if has_add else []) + ([pl.BlockSpec(memory_space=pl.ANY)] if dep is not None else [])
    args = [a, b] + ([add] if has_add else []) + ([dep] if dep is not None else [])
    return pl.pallas_call(
        body,
        grid=(N // tn, M // tm, nk),
        in_specs=in_specs,
        out_specs=o_spec,
        out_shape=jax.ShapeDtypeStruct((M, N), out_dtype),
        scratch_shapes=[pltpu.VMEM((tm, tn), F32)] if nk > 1 else [],
        name=name,
        compiler_params=pltpu.CompilerParams(
            dimension_semantics=("arbitrary", "arbitrary", "arbitrary"), vmem_limit_bytes=VMEM_LIMIT),
    )(*args)


def _vjp_of(f, n_diff):
    def g(*args, n_prim):
        prim = args[:n_diff]
        consts = args[n_diff:n_prim]
        cts = args[n_prim:]
        _, pull = jax.vjp(lambda *p: f(*p, *consts), *prim)
        out = jax.eval_shape(lambda *p: f(*p, *consts), *prim)
        if isinstance(out, (tuple, list)):
            cts = tuple(c.astype(o.dtype) for c, o in zip(cts, out))
        else:
            cts = cts[0].astype(out.dtype)
        return pull(cts)
    return g


def _rms(x, g, n=None):
    n = x.shape[-1] if n is None else n
    ms = jnp.sum(x * x, axis=-1, keepdims=True) / n
    return x * lax.rsqrt(ms + EPS) * g


def _silu(z):
    return z * jax.nn.sigmoid(z)


def _roll_rows(u, k):
    n = u.shape[0]
    r = pltpu.roll(u, k % n, 0)
    t = lax.broadcasted_iota(jnp.int32, u.shape, 0)
    if k > 0:
        return jnp.where(t >= k, r, 0.0)
    return jnp.where(t < n + k, r, 0.0)


@functools.partial(jax.custom_vjp, nondiff_argnums=(1,))
def _shift(u, k):
    return _roll_rows(u, k)


def _shift_fwd(u, k):
    return _roll_rows(u, k), None


def _shift_bwd(k, _, g):
    return (_roll_rows(g, -k),)


_shift.defvjp(_shift_fwd, _shift_bwd)


@functools.partial(jax.custom_vjp, nondiff_argnums=(1,))
def _lane_roll(u, k):
    return pltpu.roll(u, k % LANE, 1)


def _lane_roll_fwd(u, k):
    return pltpu.roll(u, k % LANE, 1), None


def _lane_roll_bwd(k, _, g):
    return (pltpu.roll(g, (-k) % LANE, 1),)


_lane_roll.defvjp(_lane_roll_fwd, _lane_roll_bwd)


def _conv_math(ab, ac, ax, az, cw, cb):
    u = ac * ax
    conv = cb + _shift(u, 2) * cw[0:1] + _shift(u, 1) * cw[1:2] + u * cw[2:3]
    return ab * conv * _silu(az)


def _mla_pre_math(cq, ckv, gq, gkv):
    return _rms(cq, gq), _rms(ckv, gkv)


def _rope_math(q, kn, kpe, gq, gk, c, s1, s2):
    lane = lax.broadcasted_iota(jnp.int32, kpe.shape, 1)
    pe = _lane_roll(jnp.where(lane < ROPE, kpe, 0.0), NOPE)

    def one(t, g):
        tn = _rms(t, g, QK)
        return tn * c + _lane_roll(tn, -16) * s1 + _lane_roll(tn, 16) * s2

    qs, ks = [], []
    for h in range(NH):
        sl = slice(h * QKP, (h + 1) * QKP)
        qs.append(one(q[:, sl], gq))
        ks.append(one(kn[:, sl] + pe, gk))
    return jnp.concatenate(qs, axis=1), jnp.concatenate(ks, axis=1)


def _gate_math(o, z):
    return o * _silu(z)


def _mergec_math(o0, o1, o2, l0, l1, l2, cz):
    m = lax.stop_gradient(jnp.maximum(jnp.maximum(l0, l1), l2))
    e0, e1, e2 = jnp.exp(l0 - m), jnp.exp(l1 - m), jnp.exp(l2 - m)
    den = e0 + e1 + e2
    oc = (e0 / den) * o0 + (e1 / den) * o1 + (e2 / den) * o2
    return oc * _silu(cz)


def _merge_math(g0, g1, g2, b0, b1, b2, pa, pb, pc):
    return (jax.nn.sigmoid(g0 + b0) * pa + jax.nn.sigmoid(g1 + b1) * pb) + jax.nn.sigmoid(g2 + b2) * pc


MLA_T = 256
MLA_UNROLL = True


def _mla_fwd(q, k, v):
    B = q.shape[0]
    T = MLA_T
    NB = S // T

    def body(q_ref, k_ref, v_ref, o_ref, l_ref):
        row = lax.broadcasted_iota(jnp.int32, (T, T), 0)
        col = lax.broadcasted_iota(jnp.int32, (T, T), 1)
        lo = _lo_mask((T, LANE))

        for qi in range(NB):
            qb = q_ref[qi * T:(qi + 1) * T, :]

            def step(j, carry, diagonal):
                m, l, acc = carry
                off = pl.multiple_of(j * T, T)
                kb = k_ref[pl.ds(off, T), :]
                vb = v_ref[pl.ds(off, T), :]
                ss = []
                for e in (0, 1):
                    se = _dot_nt(qb[:, e * QKP:(e + 1) * QKP], kb[:, e * QKP:(e + 1) * QKP]) * MLA_SCALE
                    ss.append(jnp.where(col <= row, se, NEG) if diagonal else se)
                s = jnp.concatenate(ss, axis=0)
                m_new = jnp.maximum(m, jnp.max(s, axis=-1, keepdims=True))
                a = jnp.exp(m - m_new)
                p = jnp.exp(s - m_new)
                l = a * l + jnp.sum(p, axis=-1, keepdims=True)
                acc = a * acc + _dot(p.astype(BF16), vb)
                return m_new, l, acc

            init = (jnp.full((2 * T, 1), NEG, F32), jnp.zeros((2 * T, 1), F32), jnp.zeros((2 * T, LANE), F32))
            carry = lax.fori_loop(0, qi, functools.partial(step, diagonal=False), init, unroll=MLA_UNROLL)
            m, l, acc = step(qi, carry, True)
            o = acc / l
            lse = m + jnp.log(l)
            o_ref[qi * T:(qi + 1) * T, :] = jnp.where(lo, o[:T], o[T:])
            l_ref[qi * T:(qi + 1) * T, :] = jnp.where(lo, lse[:T], lse[T:])

    def spec(w):
        return pl.BlockSpec((None, S, w), lambda b, hp: (b, 0, hp))

    return pl.pallas_call(
        body,
        grid=(B, NH // 2),
        in_specs=[spec(2 * QKP), spec(2 * QKP), spec(LANE)],
        out_specs=[spec(LANE), spec(LANE)],
        out_shape=[jax.ShapeDtypeStruct((B, S, NH * VD), F32)] * 2,
        name="mla_attn_fwd",
        compiler_params=pltpu.CompilerParams(dimension_semantics=("arbitrary",) * 2, vmem_limit_bytes=VMEM_LIMIT),
    )(q, k, v)


def _mla_bwd(q, k, v, do, o, lse):
    B = q.shape[0]
    T = MLA_T
    NB = S // T

    def body(q_ref, k_ref, v_ref, do_ref, o_ref, l_ref, dq_ref, dk_ref, dv_ref, delta_ref, dqt_ref):
        delta_ref[...] = _head_sum(do_ref[...] * o_ref[...])
        row = lax.broadcasted_iota(jnp.int32, (T, T), 0)
        col = lax.broadcasted_iota(jnp.int32, (T, T), 1)
        lo = _lo_mask((T, LANE))
        tn_t = (((0,), (1,)), ((), ()))

        for j in range(NB):
            krows = slice(j * T, (j + 1) * T)
            kb = k_ref[krows, :]
            vb = v_ref[krows, :]
            dkt = [jnp.zeros((QKP, T), F32), jnp.zeros((QKP, T), F32)]
            dvt = jnp.zeros((LANE, T), F32)
            for i in range(j, NB):
                qrows = slice(i * T, (i + 1) * T)
                qb = q_ref[qrows, :]
                do2 = _stack_heads(do_ref[qrows, :], lo).astype(BF16)
                lb = l_ref[qrows, :]
                db = delta_ref[qrows, :]
                dp2 = _dot_nt(do2, vb)
                ps = []
                for e in (0, 1):
                    cols = slice(e * QKP, (e + 1) * QKP)
                    qe, ke = qb[:, cols], kb[:, cols]
                    s = _dot_nt(qe, ke) * MLA_SCALE
                    if i == j:
                        s = jnp.where(col <= row, s, NEG)
                    p = jnp.exp(s - lb[:, e * HD:e * HD + 1])
                    ps.append(p.astype(BF16))
                    ds = (p * (dp2[e * T:(e + 1) * T] - db[:, e * HD:e * HD + 1]) * MLA_SCALE).astype(BF16)
                    dkt[e] = dkt[e] + _dot_tn(qe, ds)
                    dq_t = lax.dot_general(ke, ds, tn_t, preferred_element_type=F32)
                    if j == 0:
                        dqt_ref[e, :, qrows] = dq_t
                    else:
                        dqt_ref[e, :, qrows] += dq_t
                dvt = dvt + _dot_tn(do2, jnp.concatenate(ps, axis=0))
            dk_ref[krows, 0:QKP] = dkt[0].T
            dk_ref[krows, QKP:2 * QKP] = dkt[1].T
            dv_ref[krows, :] = dvt.T
        dq_ref[:, 0:QKP] = dqt_ref[0].T
        dq_ref[:, QKP:2 * QKP] = dqt_ref[1].T

    def spec(w):
        return pl.BlockSpec((None, S, w), lambda b, hp: (b, 0, hp))

    return pl.pallas_call(
        body,
        grid=(B, NH // 2),
        in_specs=[spec(2 * QKP), spec(2 * QKP), spec(LANE), spec(LANE), spec(LANE), spec(LANE)],
        out_specs=[spec(2 * QKP), spec(2 * QKP), spec(LANE)],
        out_shape=[jax.ShapeDtypeStruct((B, S, NH * QKP), F32), jax.ShapeDtypeStruct((B, S, NH * QKP), F32),
                   jax.ShapeDtypeStruct((B, S, NH * VD), F32)],
        scratch_shapes=[pltpu.VMEM((S, LANE), F32), pltpu.VMEM((2, QKP, S), F32)],
        name="mla_attn_bwd",
        compiler_params=pltpu.CompilerParams(dimension_semantics=("arbitrary",) * 2, vmem_limit_bytes=VMEM_LIMIT),
    )(q, k, v, do, o, lse)


def _lo_mask(shape):
    return lax.broadcasted_iota(jnp.int32, shape, len(shape) - 1) < HD


def _head_sum(u):
    r = lax.broadcasted_iota(jnp.int32, (LANE, LANE), 0) < HD
    c = lax.broadcasted_iota(jnp.int32, (LANE, LANE), 1) < HD
    ones = jnp.where(r == c, 1.0, 0.0).astype(BF16)
    hi = u.astype(BF16)
    lo = (u - hi.astype(F32)).astype(BF16)
    return _dot(hi, ones) + _dot(lo, ones)


def _head_sum_1(u):
    r = lax.broadcasted_iota(jnp.int32, (LANE, LANE), 0) < HD
    c = lax.broadcasted_iota(jnp.int32, (LANE, LANE), 1) < HD
    return _dot(u.astype(BF16), jnp.where(r == c, 1.0, 0.0).astype(BF16))


def _rms2_scale(x):
    return lax.rsqrt(_head_sum(x * x) / HD + EPS)


def _rms2(x, g):
    return x * _rms2_scale(x) * g


def _rms2_bwd(x, r, g, dy):
    xn = x * r
    t = dy * g
    dx = r * (t - xn * (_head_sum_1(xn * t) * (1.0 / HD)))
    return dx, jnp.sum(dy * xn, axis=0, keepdims=True)


def _dil_bias(t_ref, gi, d):
    qq = lax.broadcasted_iota(jnp.int32, (QB, QB), 0)
    kk = lax.broadcasted_iota(jnp.int32, (QB, QB), 1)
    jc = (qq - kk).astype(F32)
    rows = []
    for e in (0, 1):
        sl = t_ref[2 * gi + e:2 * gi + e + 1, :] * float(d)
        bp = jnp.where(kk >= qq, -sl * (jc + float(QB)), NEG)
        bc = jnp.where(kk <= qq, -sl * jc, NEG)
        rows.append(jnp.concatenate([bp, bc], axis=1))
    return jnp.concatenate(rows, axis=0)


def _dil_rows(cur, d):
    return pl.ds(cur, QB, stride=d) if d > 1 else pl.ds(pl.multiple_of(cur, QB), QB)


def _dil_walk(d, block, full):
    if d == 1:
        block(0, None)

        def body(i, c):
            block(i * QB, (i - 1) * QB)
            return c
        lax.fori_loop(1, S // QB, body, 0, unroll=True if full else 5)
    elif d == 16:
        def body(r, c):
            block(r, None)
            return c
        lax.fori_loop(0, d, body, 0, unroll=True if full else 4)
    else:
        nb = S // d // QB

        def cls(r, c):
            block(r, None)

            def body(i, c2):
                block(r + i * QB * d, r + (i - 1) * QB * d)
                return c2
            lax.fori_loop(1, nb, body, 0, unroll=True)
            return c
        lax.fori_loop(0, d, cls, 0, unroll=full)


def _stack_heads(x, lo):
    return jnp.concatenate([jnp.where(lo, x, 0.0), jnp.where(lo, 0.0, x)], axis=0)


def _dilc_fwd(proj3, gq, gk, tab):
    B = proj3.shape[0]

    def body(q_ref, k_ref, v_ref, cz_ref, gq_ref, gk_ref, t_ref, y_ref, o_ref, l_ref, qs, ks, vs):
        g = pl.program_id(2)
        lo = _lo_mask((QB, LANE))

        def group(gi):
            d = DIL[gi][1]
            qs[...] = _rms2(q_ref[...].astype(F32), gq_ref[gi:gi + 1, :])
            ks[...] = _rms2(k_ref[...].astype(F32), gk_ref[gi:gi + 1, :])
            vs[...] = v_ref[...].astype(F32)
            bias = _dil_bias(t_ref, gi, d)

            def block(cur, prev):
                rows = _dil_rows(cur, d)
                q2 = _stack_heads(qs[rows, :], lo).astype(BF16)
                kc, vc = ks[rows, :], vs[rows, :]
                if prev is None:
                    kcat, vcat, b = kc, vc, bias[:, QB:]
                else:
                    prow = _dil_rows(prev, d)
                    kcat = jnp.concatenate([ks[prow, :], kc], axis=0)
                    vcat = jnp.concatenate([vs[prow, :], vc], axis=0)
                    b = bias
                s = _dot_nt(q2, kcat.astype(BF16)) * DIL_SCALE + b
                m = jnp.max(s, axis=-1, keepdims=True)
                p = jnp.exp(s - m)
                l = jnp.sum(p, axis=-1, keepdims=True)
                o = _dot(p.astype(BF16), vcat.astype(BF16)) / l
                lse = m + jnp.log(l)
                o_ref[gi, rows, :] = jnp.where(lo, o[:QB], o[QB:])
                l_ref[gi, rows, :] = jnp.where(lo, lse[:QB], lse[QB:])

            _dil_walk(d, block, True)

        for gi in range(NG):
            pl.when(g == gi)(functools.partial(group, gi))

        @pl.when(g == NG - 1)
        def _():
            y_ref[...] = _mergec_math(o_ref[0], o_ref[1], o_ref[2], l_ref[0], l_ref[1], l_ref[2],
                                      cz_ref[...].astype(F32)).astype(BF16)

    def col(base):
        return pl.BlockSpec((None, S, LANE), lambda b, hp, g: (b, 0, base // LANE + 4 * g + hp))

    gspec = pl.BlockSpec((NG, LANE), lambda b, hp, g: (0, 0))
    saved = pl.BlockSpec((NG, None, S, LANE), lambda b, hp, g: (0, b, 0, hp))
    return pl.pallas_call(
        body,
        grid=(B, 4, NG),
        in_specs=[col(O_DQ), col(O_DK), col(O_DV),
                  pl.BlockSpec((None, S, LANE), lambda b, hp, g: (b, 0, O_CZ // LANE + hp)),
                  gspec, gspec, pl.BlockSpec((None, 8, LANE), lambda b, hp, g: (hp, 0, 0))],
        out_specs=[pl.BlockSpec((None, S, LANE), lambda b, hp, g: (b, 0, hp)), saved, saved],
        out_shape=[jax.ShapeDtypeStruct((B, S, DWID), BF16), jax.ShapeDtypeStruct((NG, B, S, DWID), F32),
                   jax.ShapeDtypeStruct((NG, B, S, DWID), F32)],
        scratch_shapes=[pltpu.VMEM((S, LANE), F32)] * 3,
        name="dil_mixer_fwd",
        compiler_params=pltpu.CompilerParams(dimension_semantics=("arbitrary",) * 3, vmem_limit_bytes=VMEM_LIMIT),
    )(proj3, proj3, proj3, proj3, gq, gk, tab)


MERGE_ROWS = 256


def _dilc_bwd(proj3, gq, gk, tab, o_all, l_all, d_yc, dproj3):
    B = proj3.shape[0]

    def body(q_ref, k_ref, v_ref, cz_ref, gq_ref, gk_ref, t_ref, o_ref, l_ref, dy_ref, dp_in,
             dp_out, dgq_out, dgk_out, qs, ks, vs, dos, dls, dqs, dks, dvs, rqs, rks, dczs,
             st_q, st_k, st_v, st_z, sems, sem_z):
        b_, hp, g = pl.program_id(0), pl.program_id(1), pl.program_id(2)
        col = lambda base: pl.ds(pl.multiple_of(base + hp * LANE, LANE), LANE)
        lo = _lo_mask((QB, LANE))

        @pl.when(jnp.logical_and(jnp.logical_and(pl.program_id(0) == 0, pl.program_id(1) == 0), g == 0))
        def _():
            dgq_out[...] = jnp.zeros((NG, LANE), F32)
            dgk_out[...] = jnp.zeros((NG, LANE), F32)

        @pl.when(g == 0)
        def _():
            def chunk(i, carry):
                rows = pl.ds(pl.multiple_of(i * MERGE_ROWS, MERGE_ROWS), MERGE_ROWS)
                ls = [l_ref[j, rows, :] for j in range(NG)]
                m = jnp.maximum(jnp.maximum(ls[0], ls[1]), ls[2])
                es = [jnp.exp(t - m) for t in ls]
                den = (es[0] + es[1]) + es[2]
                al = [e / den for e in es]
                os_ = [o_ref[j, rows, :] for j in range(NG)]
                oc = (al[0] * os_[0] + al[1] * os_[1]) + al[2] * os_[2]
                cz = cz_ref[rows, :].astype(F32)
                sg = jax.nn.sigmoid(cz)
                dy = dy_ref[rows, :]
                d_oc = dy * (cz * sg)
                dczs[rows, :] = (dy * oc * (sg * (1.0 + cz * (1.0 - sg)))).astype(BF16)
                ts = [_head_sum_1(d_oc * os_[j]) for j in range(NG)]
                tbar = (al[0] * ts[0] + al[1] * ts[1]) + al[2] * ts[2]
                for j in range(NG):
                    dos[j, rows, :] = al[j] * d_oc
                    dls[j, rows, :] = al[j] * (ts[j] - tbar)
                return carry
            lax.fori_loop(0, S // MERGE_ROWS, chunk, 0)
            _write_windows(dp_out, [st_z], sem_z, b_ * 4 + hp, B * 4, [(dczs[...], dp_out.at[b_, :, col(O_CZ)])])

        def group(gi):
            d = DIL[gi][1]
            xq, xk = q_ref[...].astype(F32), k_ref[...].astype(F32)
            rqs[...] = _rms2_scale(xq)
            rks[...] = _rms2_scale(xk)
            qs[...] = xq * rqs[...] * gq_ref[gi:gi + 1, :]
            ks[...] = xk * rks[...] * gk_ref[gi:gi + 1, :]
            vs[...] = v_ref[...].astype(F32)
            dks[...] = jnp.zeros((S, LANE), F32)
            dvs[...] = jnp.zeros((S, LANE), F32)
            bias = _dil_bias(t_ref, gi, d)

            def block(cur, prev):
                rows = _dil_rows(cur, d)
                q2 = _stack_heads(qs[rows, :], lo).astype(BF16)
                dob = dos[gi, rows, :]
                do2 = _stack_heads(dob, lo).astype(BF16)
                kc, vc = ks[rows, :], vs[rows, :]
                if prev is None:
                    kcat, vcat, b = kc, vc, bias[:, QB:]
                else:
                    prow = _dil_rows(prev, d)
                    kcat = jnp.concatenate([ks[prow, :], kc], axis=0)
                    vcat = jnp.concatenate([vs[prow, :], vc], axis=0)
                    b = bias
                kcat = kcat.astype(BF16)
                vcat = vcat.astype(BF16)
                lse_b = l_ref[gi, rows, :]
                corr_b = dls[gi, rows, :] - _head_sum_1(dob * o_ref[gi, rows, :])
                lse2 = jnp.concatenate([lse_b[:, 0:1], lse_b[:, HD:HD + 1]], axis=0)
                corr2 = jnp.concatenate([corr_b[:, 0:1], corr_b[:, HD:HD + 1]], axis=0)
                s = _dot_nt(q2, kcat) * DIL_SCALE + b
                p = jnp.exp(s - lse2)
                ds = (p * (_dot_nt(do2, vcat) + corr2) * DIL_SCALE).astype(BF16)
                dq2 = _dot(ds, kcat)
                dqs[rows, :] = jnp.where(lo, dq2[:QB], dq2[QB:])
                dk = _dot_tn(ds, q2)
                dv = _dot_tn(p.astype(BF16), do2)
                if prev is None:
                    dks[rows, :] += dk
                    dvs[rows, :] += dv
                else:
                    dks[prow, :] += dk[:QB]
                    dvs[prow, :] += dv[:QB]
                    dks[rows, :] += dk[QB:]
                    dvs[rows, :] += dv[QB:]

            _dil_walk(d, block, False)

            dxq, dgq = _rms2_bwd(q_ref[...].astype(F32), rqs[...], gq_ref[gi:gi + 1, :], dqs[...])
            dgq_out[gi:gi + 1, :] += dgq
            dxk, dgk = _rms2_bwd(k_ref[...].astype(F32), rks[...], gk_ref[gi:gi + 1, :], dks[...])
            dgk_out[gi:gi + 1, :] += dgk
            step, nsteps = _grid_step((B, 4, NG))
            _write_windows(dp_out, [st_q, st_k, st_v], sems, step, nsteps,
                           [(dxq, dp_out.at[b_, :, col(O_DQ + gi * DWID)]), (dxk, dp_out.at[b_, :, col(O_DK + gi * DWID)]),
                            (dvs[...], dp_out.at[b_, :, col(O_DV + gi * DWID)])])

        for gi in range(NG):
            pl.when(g == gi)(functools.partial(group, gi))

    def col(base):
        return pl.BlockSpec((None, S, LANE), lambda b, hp, g: (b, 0, base // LANE + 4 * g + hp))

    gspec = pl.BlockSpec((NG, LANE), lambda b, hp, g: (0, 0))
    saved = pl.BlockSpec((NG, None, S, LANE), lambda b, hp, g: (0, b, 0, hp))
    per_pair = pl.BlockSpec((None, S, LANE), lambda b, hp, g: (b, 0, hp))
    return pl.pallas_call(
        body,
        grid=(B, 4, NG),
        in_specs=[col(O_DQ), col(O_DK), col(O_DV),
                  pl.BlockSpec((None, S, LANE), lambda b, hp, g: (b, 0, O_CZ // LANE + hp)),
                  gspec, gspec, pl.BlockSpec((None, 8, LANE), lambda b, hp, g: (hp, 0, 0)),
                  saved, saved, per_pair, pl.BlockSpec(memory_space=pl.ANY)],
        out_specs=[pl.BlockSpec(memory_space=pl.ANY), gspec, gspec],
        out_shape=[jax.ShapeDtypeStruct(dproj3.shape, dproj3.dtype), jax.ShapeDtypeStruct((NG, LANE), F32),
                   jax.ShapeDtypeStruct((NG, LANE), F32)],
        input_output_aliases={10: 0},
        scratch_shapes=[pltpu.VMEM((S, LANE), F32)] * 3 + [pltpu.VMEM((NG, S, LANE), F32)] * 2
        + [pltpu.VMEM((S, LANE), F32)] * 5 + [pltpu.VMEM((S, LANE), BF16)] + [pltpu.VMEM((2, S, LANE), BF16)] * 4
        + [pltpu.SemaphoreType.DMA((3, 2)), pltpu.SemaphoreType.DMA((1, 2))],
        name="dil_mixer_bwd",
        compiler_params=pltpu.CompilerParams(dimension_semantics=("arbitrary",) * 3, vmem_limit_bytes=VMEM_LIMIT),
    )(proj3, proj3, proj3, proj3, gq, gk, tab, o_all, l_all, d_yc, dproj3)


def _dil_slopes():
    slopes = (2.0 ** (-8.0 * np.arange(1, NG * DH + 1, dtype=np.float32) / (NG * DH))).astype(np.float32).reshape(NG, DH)
    tab = np.zeros((4, 8, LANE), np.float32)
    for hp in range(4):
        for gi in range(NG):
            for e in (0, 1):
                tab[hp, 2 * gi + e, :] = slopes[gi, 2 * hp + e]
    return jnp.asarray(tab)


def _rope_tables():
    inv = ROPE_THETA ** (-jnp.arange(0, ROPE, 2, dtype=F32) / ROPE)
    ang = jnp.arange(S, dtype=F32)[:, None] * inv[None, :]
    cos, sin = jnp.cos(ang), jnp.sin(ang)
    z16 = jnp.zeros((S, 16), F32)
    c = jnp.concatenate([jnp.ones((S, NOPE), F32), cos, cos, jnp.zeros((S, 32), F32)], axis=1)
    s1 = jnp.concatenate([jnp.zeros((S, NOPE), F32), -sin, z16, jnp.zeros((S, 32), F32)], axis=1)
    s2 = jnp.concatenate([jnp.zeros((S, NOPE), F32), z16, sin, jnp.zeros((S, 32), F32)], axis=1)
    return c, s1, s2


def _pad_heads_uq(w):
    return jnp.pad(w.reshape(QL, NH, QK), ((0, 0), (0, 0), (0, QKP - QK))).reshape(QL, NH * QKP)


def _unpad_heads_uq(g):
    return g.reshape(QL, NH, QKP)[:, :, :QK].reshape(QL, NH * QK)


def _split_ukv(w):
    w3 = w.reshape(KVL, NH, NOPE + VD)
    uk = jnp.pad(w3[:, :, :NOPE], ((0, 0), (0, 0), (0, QKP - NOPE))).reshape(KVL, NH * QKP)
    return uk, w3[:, :, NOPE:].reshape(KVL, NH * VD)


def _join_ukv(guk, guv):
    return jnp.concatenate([guk.reshape(KVL, NH, QKP)[:, :, :NOPE], guv.reshape(KVL, NH, VD)],
                           axis=-1).reshape(KVL, NH * (NOPE + VD))


BR = 512
BRM = 256


def _layer_fwd(x, w, tabs, batch):
    T = batch * S
    rope_c, rope_s1, rope_s2, dil_tab = tabs
    res = {"x": x}
    row = lambda c: (lambda i: (i, c))
    fix = lambda i: (0, 0)

    h = _pcall("norm_fwd", _rms, (T // BR,),
               [(x, (BR, D), row(0)), (w["norm_g"], (1, D), fix)],
               [((T, D), BF16, (BR, D), row(0))])[0]
    proj = _mm("in_proj", h, w["w_in_t"], tb=True, out_dtype=BF16, b_words=True, tm=2048, tn=1024)
    res["h"], res["proj"] = h, proj
    proj3 = proj.reshape(batch, S, NINP)

    cblk = lambda s: (lambda j, b: (b, 0, 4 * s + j))
    y_a = _pcall("conv_fwd", _conv_math, (4, batch),
                 [(proj3, (None, S, LANE), cblk(0)), (proj3, (None, S, LANE), cblk(1)),
                  (proj3, (None, S, LANE), cblk(2)), (proj3, (None, S, LANE), cblk(3)),
                  (w["conv_w"], (3, LANE), lambda j, b: (0, j)), (w["conv_b"], (1, LANE), lambda j, b: (0, j))],
                 [((batch, S, CW), BF16, (None, S, LANE), lambda j, b: (b, 0, j))])[0].reshape(T, CW)
    res["y_a"] = y_a

    cqn, ckvn = _pcall("mla_pre_fwd", _mla_pre_math, (T // BR,),
                       [(proj, (BR, QL), row(O_CQ // QL)), (proj, (BR, KVL), row(O_CKV // KVL)),
                        (w["q_a_norm_g"], (1, QL), fix), (w["kv_a_norm_g"], (1, KVL), fix)],
                       [((T, QL), BF16, (BR, QL), row(0)), ((T, KVL), BF16, (BR, KVL), row(0))])
    w_uq_p = _pad_heads_uq(w["w_uq"])
    w_uk, w_uv = _split_ukv(w["w_ukv"])
    q = _mm("uq", cqn, w_uq_p, out_dtype=BF16)
    kn = _mm("uk", ckvn, w_uk, out_dtype=BF16)
    v = _mm("uv", ckvn, w_uv, out_dtype=BF16)
    nrr = S // BR
    tab_row = lambda i: (i % nrr, 0)
    qr, kr = _pcall("rope_fwd", _rope_math, (T // BR,),
                    [(q, (BR, NH * QKP), row(0)), (kn, (BR, NH * QKP), row(0)), (proj, (BR, LANE), row(O_KPE // LANE)),
                     (w["mla_q_norm_g"], (1, QKP), fix), (w["mla_k_norm_g"], (1, QKP), fix),
                     (rope_c, (BR, QKP), tab_row), (rope_s1, (BR, QKP), tab_row), (rope_s2, (BR, QKP), tab_row)],
                    [((T, NH * QKP), BF16, (BR, NH * QKP), row(0))] * 2)
    qr = qr.reshape(batch, S, NH * QKP)
    kr = kr.reshape(batch, S, NH * QKP)
    v = v.reshape(batch, S, NH * VD)
    o_b, l_b = _mla_fwd(qr, kr, v)
    ob2 = o_b.reshape(T, NH * VD)
    y_b = _pcall("gateb_fwd", _gate_math, (T // BR,),
                 [(ob2, (BR, 512), row(0)), (proj, (BR, 512), row(O_BZ // 512))],
                 [((T, 512), BF16, (BR, 512), row(0))])[0]
    res.update(cqn=cqn, ckvn=ckvn, q=q, kn=kn, qr=qr, kr=kr, v=v, o_b=o_b, l_b=l_b, ob2=ob2, y_b=y_b,
               w_uq_p=w_uq_p, w_uk=w_uk, w_uv=w_uv)

    gq2 = jnp.tile(w["dil_q_norm_g"].reshape(NG, HD), (1, 2))
    gk2 = jnp.tile(w["dil_k_norm_g"].reshape(NG, HD), (1, 2))
    y_c, o_all, l_all = _dilc_fwd(proj3, gq2, gk2, dil_tab)
    y_c = y_c.reshape(T, DWID)
    res.update(o_all=o_all, l_all=l_all, y_c=y_c)

    pa = _mm("out_a", y_a, w["w_out_a"], out_dtype=BF16)
    pb = _mm("out_b", y_b, w["w_out_b"], out_dtype=BF16)
    pc = _mm("out_c", y_c, w["w_out_c"], out_dtype=BF16)
    merged = _pcall("merge_fwd", _merge_math, (T // BRM,),
                    [(proj, (BRM, D), row(O_G // D + s)) for s in range(3)]
                    + [(w["b_gate"], (1, D), (lambda s: (lambda i: (0, s)))(s)) for s in range(3)]
                    + [(t, (BRM, D), row(0)) for t in (pa, pb, pc)],
                    [((T, D), BF16, (BRM, D), row(0))])[0]
    out = _mm("o_proj", merged, w["w_o"], add=x, tm=1024)
    res.update(pa=pa, pb=pb, pc=pc, merged=merged)
    return out, res


def _norm_bwd_math(x, g, dh, dy):
    _, pull = jax.vjp(_rms, x, g)
    dx, dg = pull(dh)
    return dx + dy, dg


def _layer_bwd(dy, w, res, tabs, batch, after_dw=None, after_merge=None):
    T = batch * S
    rope_c, rope_s1, rope_s2, dil_tab = tabs
    row = lambda c: (lambda i: (i, c))
    fix = lambda i: (0, 0)
    x, proj, h = res["x"], res["proj"], res["h"]
    proj3 = proj.reshape(batch, S, NINP)
    g = {}

    d_merged = _mm("o_proj_dx", dy, w["w_o"], tb=True)
    g["w_o"] = _mm("o_proj_dw", res["merged"], dy, ta=True, tm=1024, tk=2048)

    dproj = lax.empty((T, NINP), BF16)
    rows_of = lambda br: (lambda ref, i: ref.at[pl.ds(pl.multiple_of(i * br, br), br)])

    def merge_bwd(*args):
        dg0, dg1, dg2, db0, db1, db2, dpa, dpb, dpc = _vjp_of(_merge_math, 9)(*args, n_prim=9)
        return db0, db1, db2, dpa, dpb, dpc, jnp.concatenate([dg0, dg1, dg2], axis=1)

    db0, db1, db2, dpa, dpb, dpc, dproj = _pcall(
        "merge_bwd", merge_bwd, (T // BRM,),
        [(proj, (BRM, D), row(O_G // D + s)) for s in range(3)]
        + [(w["b_gate"], (1, D), (lambda s: (lambda i: (0, s)))(s)) for s in range(3)]
        + [(t, (BRM, D), row(0)) for t in (res["pa"], res["pb"], res["pc"])]
        + [(d_merged, (BRM, D), row(0))],
        [((1, D), F32, (1, D), fix, True)] * 3 + [((T, D), BF16, (BRM, D), row(0))] * 3,
        into=(dproj, [((BRM, 3 * D), lambda ref, i: rows_of(BRM)(ref, i).at[:, O_G:O_G + 3 * D])]))
    g["b_gate"] = jnp.concatenate([db0, db1, db2], axis=1)

    dep = after_merge(dpa) if after_merge is not None else None
    d_ya = _mm("out_a_dx", dpa, w["w_out_a"], tb=True, dep=dep)
    d_yb = _mm("out_b_dx", dpb, w["w_out_b"], tb=True)
    d_yc = _mm("out_c_dx", dpc, w["w_out_c"], tb=True)
    g["w_out_a"] = _mm("out_a_dw", res["y_a"], dpa, ta=True, tk=T)
    g["w_out_b"] = _mm("out_b_dw", res["y_b"], dpb, ta=True, tk=T)
    g["w_out_c"] = _mm("out_c_dw", res["y_c"], dpc, ta=True, tk=T)

    cblk = lambda s: (lambda j, b: (b, 0, 4 * s + j))
    oblk = lambda j, b: (b, 0, j)
    def conv_bwd(*args):
        d_ab, d_ac, d_ax, d_az, dcw, dcb = _vjp_of(_conv_math, 6)(*args, n_prim=6)
        return dcw, dcb, d_ab, d_ac, d_ax, d_az

    a_col = lambda s_: (lambda ref, j, b: ref.at[b, :, pl.ds(pl.multiple_of(O_A + s_ * CW + j * LANE, LANE), LANE)])
    g["conv_w"], g["conv_b"], dproj3 = _pcall(
        "conv_bwd", conv_bwd, (4, batch),
        [(proj3, (None, S, LANE), cblk(s)) for s in range(4)]
        + [(w["conv_w"], (3, LANE), lambda j, b: (0, j)), (w["conv_b"], (1, LANE), lambda j, b: (0, j)),
           (d_ya.reshape(batch, S, CW), (None, S, LANE), oblk)],
        [((3, CW), F32, (3, LANE), lambda j, b: (0, j), True), ((1, CW), F32, (1, LANE), lambda j, b: (0, j), True)],
        into=(dproj.reshape(batch, S, NINP), [((S, LANE), a_col(s_)) for s_ in range(4)]))
    dproj = dproj3.reshape(T, NINP)

    gate_bwd = functools.partial(_vjp_of(_gate_math, 2), n_prim=2)
    d_ob, dproj = _pcall("gateb_bwd", gate_bwd, (T // BR,),
                         [(res["ob2"], (BR, 512), row(0)), (proj, (BR, 512), row(O_BZ // 512)), (d_yb, (BR, 512), row(0))],
                         [((T, 512), F32, (BR, 512), row(0))],
                         into=(dproj, [((BR, 512), lambda ref, i: rows_of(BR)(ref, i).at[:, O_BZ:O_BZ + 512])]))
    dqr, dkr, dv = _mla_bwd(res["qr"], res["kr"], res["v"], d_ob.reshape(batch, S, NH * VD), res["o_b"], res["l_b"])
    nrr = S // BR
    tab_row = lambda i: (i % nrr, 0)
    def rope_bwd(*args):
        d_q, d_kn, d_kpe, dgq, dgk = _vjp_of(_rope_math, 5)(*args, n_prim=8)
        return d_q, d_kn, dgq, dgk, d_kpe

    d_q, d_kn, g["mla_q_norm_g"], g["mla_k_norm_g"], dproj = _pcall(
        "rope_bwd", rope_bwd, (T // BR,),
        [(res["q"], (BR, NH * QKP), row(0)), (res["kn"], (BR, NH * QKP), row(0)), (proj, (BR, LANE), row(O_KPE // LANE)),
         (w["mla_q_norm_g"], (1, QKP), fix), (w["mla_k_norm_g"], (1, QKP), fix),
         (rope_c, (BR, QKP), tab_row), (rope_s1, (BR, QKP), tab_row), (rope_s2, (BR, QKP), tab_row),
         (dqr.reshape(T, NH * QKP), (BR, NH * QKP), row(0)), (dkr.reshape(T, NH * QKP), (BR, NH * QKP), row(0))],
        [((T, NH * QKP), BF16, (BR, NH * QKP), row(0))] * 2 + [((1, QKP), F32, (1, QKP), fix, True)] * 2,
        into=(dproj, [((BR, LANE), lambda ref, i: rows_of(BR)(ref, i).at[:, O_KPE:O_KPE + LANE])]))
    dv = dv.reshape(T, NH * VD)
    d_cqn = _mm("uq_dx", d_q, res["w_uq_p"], tb=True)
    d_ckvn = _mm("uk_dx", d_kn, res["w_uk"], tb=True)
    d_ckvn = _mm("uv_dx", dv, res["w_uv"], tb=True, add=d_ckvn)
    g["w_uq"] = _unpad_heads_uq(_mm("uq_dw", res["cqn"], d_q, ta=True, tk=T))
    g["w_ukv"] = _join_ukv(_mm("uk_dw", res["ckvn"], d_kn, ta=True, tk=T),
                           _mm("uv_dw", res["ckvn"], dv, ta=True, tk=T))
    def pre_bwd(*args):
        d_cq, d_ckv, dgq, dgkv = _vjp_of(_mla_pre_math, 4)(*args, n_prim=4)
        return dgq, dgkv, jnp.concatenate([d_cq, d_ckv], axis=1)

    g["q_a_norm_g"], g["kv_a_norm_g"], dproj = _pcall(
        "mla_pre_bwd", pre_bwd, (T // BR,),
        [(proj, (BR, QL), row(O_CQ // QL)), (proj, (BR, KVL), row(O_CKV // KVL)),
         (w["q_a_norm_g"], (1, QL), fix), (w["kv_a_norm_g"], (1, KVL), fix),
         (d_cqn, (BR, QL), row(0)), (d_ckvn, (BR, KVL), row(0))],
        [((1, QL), F32, (1, QL), fix, True), ((1, KVL), F32, (1, KVL), fix, True)],
        into=(dproj, [((BR, QL + KVL), lambda ref, i: rows_of(BR)(ref, i).at[:, O_CQ:O_CQ + QL + KVL])]))

    gq2 = jnp.tile(w["dil_q_norm_g"].reshape(NG, HD), (1, 2))
    gk2 = jnp.tile(w["dil_k_norm_g"].reshape(NG, HD), (1, 2))
    dproj3, dgq, dgk = _dilc_bwd(proj3, gq2, gk2, dil_tab, res["o_all"], res["l_all"],
                                 d_yc.reshape(batch, S, DWID), dproj.reshape(batch, S, NINP))
    dproj = dproj3.reshape(T, NINP)
    g["dil_q_norm_g"] = dgq[:, :HD] + dgq[:, HD:]
    g["dil_k_norm_g"] = dgk[:, :HD] + dgk[:, HD:]

    g["w_in_t"] = _mm("in_proj_dw", dproj, h, ta=True, tm=1024, tk=T)
    dep = after_dw(g) if after_dw is not None else None
    d_h = _mm("in_proj_dx", dproj, w["w_in_t"], dep=dep, b_words=True, tm=1024, tk=NINP // 4)
    dx, g["norm_g"] = _pcall("norm_bwd", _norm_bwd_math, (T // BR,),
                             [(x, (BR, D), row(0)), (w["norm_g"], (1, D), fix), (d_h, (BR, D), row(0)),
                              (dy, (BR, D), row(0))],
                             [((T, D), F32, (BR, D), row(0)), ((1, D), F32, (1, D), fix, True)])
    return dx, g


def _loss_math(y, t):
    e = y - t
    return e * (1.0 / D), 0.5 * jnp.sum(jnp.sum(e * e, axis=-1, keepdims=True) / D, axis=0, keepdims=True)


ANY = pl.BlockSpec(memory_space=pl.ANY)
U32 = jnp.uint32
WSH = NIN // 4
WA = KPE_END
WB = WSH - WA
CWD = 512
PACK_ROWS = 1472
HW = PACK_W // 2


def _me():
    return lax.axis_index("x"), lax.axis_index("y"), lax.axis_index("c")


def _piece_rows(k):
    a = k * WSH + jnp.where(k > 0, NINP - NIN, 0)
    b = k * WSH + WA + (NINP - NIN)
    return ((0, pl.multiple_of(a, 8), WA), (WA, pl.multiple_of(b, 8), WB))


def _pack_words(lo, hi):
    ul = lax.bitcast_convert_type(lo.astype(BF16).astype(F32), U32)
    uh = lax.bitcast_convert_type(hi.astype(BF16).astype(F32), U32)
    w = jnp.bitwise_or(jnp.bitwise_and(uh, jnp.uint32(0xFFFF0000)), jnp.right_shift(ul, jnp.uint32(16)))
    return lax.bitcast_convert_type(w, F32)


def _unpack_words(w):
    w = lax.bitcast_convert_type(w, U32)
    lo = lax.bitcast_convert_type(jnp.left_shift(w, jnp.uint32(16)), F32)
    hi = lax.bitcast_convert_type(jnp.bitwise_and(w, jnp.uint32(0xFFFF0000)), F32)
    return lo, hi


def _all_gather(wc, sp):
    def body(w_ref, s_ref, ow_ref, os_ref, send_sems, recv_sems):
        x, y, c = _me()
        k_me = 2 * x + y
        sib = (x, y, 1 - c)
        chips = [(1 - x, y), (x, 1 - y), (1 - x, 1 - y)]
        wcols = lambda cc: pl.ds(pl.multiple_of(cc * (CWD // 2), LANE), CWD // 2)
        scols = lambda cc: pl.ds(pl.multiple_of(cc * HW, LANE), HW)

        def windows(k, cc):
            pcs = _piece_rows(k)
            return ([(w_ref.at[pl.ds(l0, n), wcols(cc)], ow_ref.at[pl.ds(p0, n), wcols(cc)]) for l0, p0, n in pcs]
                    + [(s_ref.at[:, scols(cc)], os_ref.at[k, :, scols(cc)])])

        def copy(i, src, dst, to):
            return pltpu.make_async_remote_copy(src_ref=src, dst_ref=dst, send_sem=send_sems.at[i],
                                                recv_sem=recv_sems.at[i], device_id=to, device_id_type=MESH)

        def own_windows():
            return ([(w_ref.at[pl.ds(l0, n)], ow_ref.at[pl.ds(p0, n)]) for l0, p0, n in _piece_rows(k_me)]
                    + [(s_ref, os_ref.at[k_me])])

        first = [copy(18 + i, src, dst, sib) for i, (src, dst) in enumerate(own_windows())]
        for j, (cx, cy) in enumerate(chips):
            for i, (src, dst) in enumerate(windows(k_me, c)):
                first.append(copy(3 * j + i, src, dst, (cx, cy, c)))
        for cp in first:
            cp.start()
        passed = []
        for j, (cx, cy) in enumerate(chips):
            for i, (_, dst) in enumerate(windows(2 * cx + cy, c)):
                copy(3 * j + i, dst, dst, (cx, cy, c)).wait_recv()
                cp = copy(9 + 3 * j + i, dst, dst, sib)
                cp.start()
                passed.append(cp)
        for j, (cx, cy) in enumerate(chips):
            for i, (_, dst) in enumerate(windows(2 * cx + cy, 1 - c)):
                copy(9 + 3 * j + i, dst, dst, sib).wait_recv()
        for i, (_, dst) in enumerate(own_windows()):
            copy(18 + i, dst, dst, sib).wait_recv()
        for cp in first + passed:
            cp.wait_send()

    return pl.pallas_call(
        body,
        out_shape=[jax.ShapeDtypeStruct((NINP, CWD), F32), jax.ShapeDtypeStruct((4, PACK_ROWS, PACK_W), BF16)],
        in_specs=[ANY, ANY], out_specs=[ANY, ANY],
        scratch_shapes=[pltpu.SemaphoreType.DMA((21,)), pltpu.SemaphoreType.DMA((21,))],
        name="weights_all_gather",
    )(wc, sp)


HBM = pl.BlockSpec(memory_space=pltpu.HBM)
SEM = pl.BlockSpec(memory_space=pltpu.SEMAPHORE)
EFFECT = pltpu.SideEffectType.DATAFLOW_SIDE_EFFECTING


def _in_hbm(a):
    return pltpu.with_memory_space_constraint(a, pltpu.HBM)


def _ag_shard(w_ref, s_ref, lw_ref, ls_ref, k):
    return ([(w_ref.at[pl.ds(l0, n)], lw_ref.at[pl.ds(p0, n)]) for l0, p0, n in _piece_rows(k)]
            + [(s_ref, ls_ref.at[k])])


def _ag_behind_copies(w_ref, s_ref, lw_ref, ls_ref, send_sems, recv_sems):
    x, y, c = _me()
    peers = [(1 - x, y, c), (x, 1 - y, c), (1 - x, 1 - y, c), (x, y, 1 - c)]
    mine, theirs = [], []
    for j, (px, py, pc) in enumerate(peers):
        for i, ((src, dst), (_, got)) in enumerate(zip(_ag_shard(w_ref, s_ref, lw_ref, ls_ref, 2 * x + y),
                                                       _ag_shard(w_ref, s_ref, lw_ref, ls_ref, 2 * px + py))):
            mk = lambda s_, d_: pltpu.make_async_remote_copy(
                src_ref=s_, dst_ref=d_, send_sem=send_sems.at[3 * j + i], recv_sem=recv_sems.at[3 * j + i],
                device_id=(px, py, pc), device_id_type=MESH)
            mine.append(mk(src, dst))
            theirs.append(mk(got, got))
    return mine, theirs


def _ag_behind_start(wc, sp, dep):
    def body(w_ref, s_ref, lw_ref, ls_ref, dep_ref, send_sems, recv_sems, w_thru, s_thru, lw_thru, ls_thru, token):
        mine, _ = _ag_behind_copies(w_ref, s_ref, lw_ref, ls_ref, send_sems, recv_sems)
        for cp in mine:
            cp.start()
        token[...] = jnp.zeros_like(token)

    return pl.pallas_call(
        body, name="weights_gather_start",
        out_shape=(pltpu.SemaphoreType.DMA((12,)), pltpu.SemaphoreType.DMA((12,)), pltpu.HBM(wc.shape, wc.dtype),
                   pltpu.HBM(sp.shape, sp.dtype), pltpu.HBM((NINP, CWD), F32), pltpu.HBM((4, PACK_ROWS, PACK_W), BF16),
                   jax.ShapeDtypeStruct((8, LANE), F32)),
        in_specs=(HBM, HBM, HBM, HBM, ANY),
        out_specs=(SEM, SEM, HBM, HBM, HBM, HBM, pl.BlockSpec(memory_space=pltpu.VMEM)),
        input_output_aliases={0: 2, 1: 3, 2: 4, 3: 5},
        compiler_params=pltpu.CompilerParams(has_side_effects=EFFECT),
    )(_in_hbm(wc), _in_hbm(sp), _in_hbm(lax.empty((NINP, CWD), F32)), _in_hbm(lax.empty((4, PACK_ROWS, PACK_W), BF16)), dep)


def _ag_behind_wait(send_sems, recv_sems, wc, sp, lw, ls, after):
    def body(w_ref, s_ref, lw_ref, ls_ref, send_sems, recv_sems, after_ref, w_dead, s_dead, lw_out, ls_out):
        mine, theirs = _ag_behind_copies(w_ref, s_ref, lw_ref, ls_ref, send_sems, recv_sems)
        for cp in mine:
            cp.wait_send()
        for cp in theirs:
            cp.wait_recv()

    out = pl.pallas_call(
        body, name="weights_gather_wait",
        out_shape=(pltpu.HBM(wc.shape, wc.dtype), pltpu.HBM(sp.shape, sp.dtype), pltpu.HBM(lw.shape, lw.dtype),
                   pltpu.HBM(ls.shape, ls.dtype)),
        in_specs=(HBM, HBM, HBM, HBM, SEM, SEM, ANY), out_specs=(HBM, HBM, HBM, HBM),
        input_output_aliases={0: 0, 1: 1, 2: 2, 3: 3},
        compiler_params=pltpu.CompilerParams(has_side_effects=EFFECT),
    )(wc, sp, lw, ls, send_sems, recv_sems, after)
    return out[2], out[3]


def _rs_swap(gw, gs):
    def body(w_ref, s_ref, rw_ref, rs_ref, send_sems, recv_sems):
        x, y, c = _me()
        oc = 1 - c
        cps = [pltpu.make_async_remote_copy(src_ref=w_ref.at[:, pl.ds(pl.multiple_of(oc * (D // 2), LANE), D // 2)],
                                            dst_ref=rw_ref, send_sem=send_sems.at[0], recv_sem=recv_sems.at[0],
                                            device_id=(x, y, oc), device_id_type=MESH),
               pltpu.make_async_remote_copy(src_ref=s_ref.at[:, :, pl.ds(pl.multiple_of(oc * HW, LANE), HW)],
                                            dst_ref=rs_ref, send_sem=send_sems.at[1], recv_sem=recv_sems.at[1],
                                            device_id=(x, y, oc), device_id_type=MESH)]
        for cp in cps:
            cp.start()
        for cp in cps:
            cp.wait()

    return pl.pallas_call(
        body,
        out_shape=[jax.ShapeDtypeStruct((NINP, D // 2), F32), jax.ShapeDtypeStruct((4, PACK_ROWS, HW), F32)],
        in_specs=[ANY, ANY], out_specs=[ANY, ANY],
        scratch_shapes=[pltpu.SemaphoreType.DMA((2,)), pltpu.SemaphoreType.DMA((2,))],
        name="grads_sibling_swap",
    )(gw, gs)


def _rs_swap_copies(w_ref, s_ref, rw_ref, rs_ref, send_sems, recv_sems):
    x, y, c = _me()
    oc = 1 - c
    return [pltpu.make_async_remote_copy(src_ref=w_ref.at[:, pl.ds(pl.multiple_of(oc * (D // 2), LANE), D // 2)],
                                         dst_ref=rw_ref, send_sem=send_sems.at[0], recv_sem=recv_sems.at[0],
                                         device_id=(x, y, oc), device_id_type=MESH),
            pltpu.make_async_remote_copy(src_ref=s_ref.at[:, :, pl.ds(pl.multiple_of(oc * HW, LANE), HW)],
                                         dst_ref=rs_ref, send_sem=send_sems.at[1], recv_sem=recv_sems.at[1],
                                         device_id=(x, y, oc), device_id_type=MESH)]


def _rs_swap_start(gw, gs):
    def body(w_ref, s_ref, rw_ref, rs_ref, send_sems, recv_sems, w_thru, s_thru, rw_thru, rs_thru, token):
        for cp in _rs_swap_copies(w_ref, s_ref, rw_ref, rs_ref, send_sems, recv_sems):
            cp.start()
        token[...] = jnp.zeros_like(token)

    return pl.pallas_call(
        body, name="grads_swap_start",
        out_shape=(pltpu.SemaphoreType.DMA((2,)), pltpu.SemaphoreType.DMA((2,)), pltpu.HBM(gw.shape, gw.dtype),
                   pltpu.HBM(gs.shape, gs.dtype), pltpu.HBM((NINP, D // 2), F32), pltpu.HBM((4, PACK_ROWS, HW), F32),
                   jax.ShapeDtypeStruct((8, LANE), F32)),
        in_specs=(HBM, HBM, HBM, HBM),
        out_specs=(SEM, SEM, HBM, HBM, HBM, HBM, pl.BlockSpec(memory_space=pltpu.VMEM)),
        input_output_aliases={0: 2, 1: 3, 2: 4, 3: 5},
        compiler_params=pltpu.CompilerParams(has_side_effects=EFFECT),
    )(_in_hbm(gw), _in_hbm(gs), _in_hbm(lax.empty((NINP, D // 2), F32)), _in_hbm(lax.empty((4, PACK_ROWS, HW), F32)))


def _rs_swap_wait(send_sems, recv_sems, gw, gs, rw, rs, after):
    def body(w_ref, s_ref, rw_ref, rs_ref, send_sems, recv_sems, after_ref, w_out, s_out, rw_out, rs_out):
        for cp in _rs_swap_copies(w_ref, s_ref, rw_ref, rs_ref, send_sems, recv_sems):
            cp.wait()

    return pl.pallas_call(
        body, name="grads_swap_wait",
        out_shape=(pltpu.HBM(gw.shape, gw.dtype), pltpu.HBM(gs.shape, gs.dtype), pltpu.HBM(rw.shape, rw.dtype),
                   pltpu.HBM(rs.shape, rs.dtype)),
        in_specs=(HBM, HBM, HBM, HBM, SEM, SEM, ANY), out_specs=(HBM, HBM, HBM, HBM),
        input_output_aliases={0: 0, 1: 1, 2: 2, 3: 3},
        compiler_params=pltpu.CompilerParams(has_side_effects=EFFECT),
    )(gw, gs, rw, rs, send_sems, recv_sems, after)


SUM_BR = 512


def _rs_chip_sum_w(gw, rw, cidx):
    def body(c_ref, g_ref, r_ref, o_ref):
        s = g_ref[...] + r_ref[...]
        q = D // 8
        o_ref[...] = jnp.concatenate([_pack_words(s[:, 0:q], s[:, q:2 * q]),
                                      _pack_words(s[:, 2 * q:3 * q], s[:, 3 * q:4 * q])], axis=1)

    return pl.pallas_call(
        body,
        grid_spec=pltpu.PrefetchScalarGridSpec(
            num_scalar_prefetch=1, grid=(NINP // SUM_BR,),
            in_specs=[pl.BlockSpec((SUM_BR, D // 2), lambda i, cr: (i, cr[0])),
                      pl.BlockSpec((SUM_BR, D // 2), lambda i, cr: (i, 0))],
            out_specs=pl.BlockSpec((SUM_BR, D // 4), lambda i, cr: (i, 0))),
        out_shape=jax.ShapeDtypeStruct((NINP, D // 4), F32),
        name="grads_chip_sum_w",
        compiler_params=pltpu.CompilerParams(dimension_semantics=("arbitrary",), vmem_limit_bytes=VMEM_LIMIT),
    )(cidx, gw, rw)


def _rs_chip_sum_s(gs, rs, cidx):
    def body(c_ref, g_ref, r_ref, o_ref):
        o_ref[...] = (g_ref[...] + r_ref[...]).astype(BF16)

    return pl.pallas_call(
        body,
        grid_spec=pltpu.PrefetchScalarGridSpec(
            num_scalar_prefetch=1, grid=(4,),
            in_specs=[pl.BlockSpec((None, PACK_ROWS, HW), lambda j, cr: (j, 0, cr[0])),
                      pl.BlockSpec((None, PACK_ROWS, HW), lambda j, cr: (j, 0, 0))],
            out_specs=pl.BlockSpec((None, PACK_ROWS, HW), lambda j, cr: (j, 0, 0))),
        out_shape=jax.ShapeDtypeStruct((4, PACK_ROWS, HW), BF16),
        name="grads_chip_sum_s",
        compiler_params=pltpu.CompilerParams(dimension_semantics=("arbitrary",), vmem_limit_bytes=VMEM_LIMIT),
    )(cidx, gs, rs)


def _rs_exchange_copies(sw_ref, ss_ref, r2w_ref, r2s_ref, send_sems, recv_sems):
    x, y, c = _me()
    mine, theirs = [], []
    for j, (cx, cy) in enumerate([(1 - x, y), (x, 1 - y), (1 - x, 1 - y)]):
        def mk(i, src, dst):
            return pltpu.make_async_remote_copy(src_ref=src, dst_ref=dst, send_sem=send_sems.at[3 * j + i],
                                                recv_sem=recv_sems.at[3 * j + i], device_id=(cx, cy, c), device_id_type=MESH)
        for i, (l0, p0, n) in enumerate(_piece_rows(2 * cx + cy)):
            mine.append(mk(i, sw_ref.at[pl.ds(p0, n)], r2w_ref.at[j, pl.ds(l0, n)]))
            theirs.append(mk(i, r2w_ref.at[j, pl.ds(l0, n)], r2w_ref.at[j, pl.ds(l0, n)]))
        mine.append(mk(2, ss_ref.at[2 * cx + cy], r2s_ref.at[j]))
        theirs.append(mk(2, r2s_ref.at[j], r2s_ref.at[j]))
    return mine, theirs


def _rs_exchange_start(sw, ss, tag):
    def body(sw_ref, ss_ref, r2w_ref, r2s_ref, send_sems, recv_sems, sw_thru, ss_thru, r2w_thru, r2s_thru, token):
        mine, _ = _rs_exchange_copies(sw_ref, ss_ref, r2w_ref, r2s_ref, send_sems, recv_sems)
        for cp in mine:
            cp.start()
        token[...] = jnp.zeros_like(token)

    return pl.pallas_call(
        body, name="grads_exchange_start_" + tag,
        out_shape=(pltpu.SemaphoreType.DMA((9,)), pltpu.SemaphoreType.DMA((9,)), pltpu.HBM(sw.shape, sw.dtype),
                   pltpu.HBM(ss.shape, ss.dtype), pltpu.HBM((3, WSH, D // 4), F32), pltpu.HBM((3, PACK_ROWS, HW), BF16),
                   jax.ShapeDtypeStruct((8, LANE), F32)),
        in_specs=(HBM, HBM, HBM, HBM),
        out_specs=(SEM, SEM, HBM, HBM, HBM, HBM, pl.BlockSpec(memory_space=pltpu.VMEM)),
        input_output_aliases={0: 2, 1: 3, 2: 4, 3: 5},
        compiler_params=pltpu.CompilerParams(has_side_effects=EFFECT),
    )(_in_hbm(sw), _in_hbm(ss), _in_hbm(lax.empty((3, WSH, D // 4), F32)), _in_hbm(lax.empty((3, PACK_ROWS, HW), BF16)))


def _rs_exchange_wait(send_sems, recv_sems, sw, ss, r2w, r2s, after, tag):
    def body(sw_ref, ss_ref, r2w_ref, r2s_ref, send_sems, recv_sems, after_ref, sw_dead, ss_dead, r2w_out, r2s_out):
        mine, theirs = _rs_exchange_copies(sw_ref, ss_ref, r2w_ref, r2s_ref, send_sems, recv_sems)
        for cp in mine:
            cp.wait_send()
        for cp in theirs:
            cp.wait_recv()

    out = pl.pallas_call(
        body, name="grads_exchange_wait_" + tag,
        out_shape=(pltpu.HBM(sw.shape, sw.dtype), pltpu.HBM(ss.shape, ss.dtype), pltpu.HBM(r2w.shape, r2w.dtype),
                   pltpu.HBM(r2s.shape, r2s.dtype)),
        in_specs=(HBM, HBM, HBM, HBM, SEM, SEM, ANY), out_specs=(HBM, HBM, HBM, HBM),
        input_output_aliases={0: 0, 1: 1, 2: 2, 3: 3},
        compiler_params=pltpu.CompilerParams(has_side_effects=EFFECT),
    )(sw, ss, r2w, r2s, send_sems, recv_sems, after)
    return out[2], out[3]


def _rs_final_w(gw, rw, r2w, idx):
    q = D // 8

    def body(i_ref, g_ref, r_ref, p_ref, o_ref, gbuf, rbuf, sems):
        i = pl.program_id(0)
        k, c = i_ref[0], i_ref[1]
        cps = []
        for n_, (l0, p0, n) in enumerate(_piece_rows(k)):
            gcol = pl.ds(pl.multiple_of(c * (D // 2) + i * 2 * q, LANE), 2 * q)
            rcol = pl.ds(pl.multiple_of(i * 2 * q, LANE), 2 * q)
            cps.append(pltpu.make_async_copy(g_ref.at[pl.ds(p0, n), gcol], gbuf.at[pl.ds(l0, n)], sems.at[2 * n_]))
            cps.append(pltpu.make_async_copy(r_ref.at[pl.ds(p0, n), rcol], rbuf.at[pl.ds(l0, n)], sems.at[2 * n_ + 1]))
        for cp in cps:
            cp.start()
        for cp in cps:
            cp.wait()
        acc = gbuf[...] + rbuf[...]
        for j in range(3):
            lo, hi = _unpack_words(p_ref[j])
            acc = acc + jnp.concatenate([lo, hi], axis=1)
        o_ref[...] = acc

    return pl.pallas_call(
        body,
        grid_spec=pltpu.PrefetchScalarGridSpec(
            num_scalar_prefetch=1, grid=(2,),
            in_specs=[ANY, ANY, pl.BlockSpec((3, WSH, q), lambda i, ir: (0, 0, i))],
            out_specs=pl.BlockSpec((WSH, 2 * q), lambda i, ir: (0, 2 * ir[1] + i)),
            scratch_shapes=[pltpu.VMEM((WSH, 2 * q), F32), pltpu.VMEM((WSH, 2 * q), F32), pltpu.SemaphoreType.DMA((4,))]),
        out_shape=jax.ShapeDtypeStruct((WSH, D), F32),
        name="grads_final_sum_w",
        compiler_params=pltpu.CompilerParams(dimension_semantics=("arbitrary",), vmem_limit_bytes=VMEM_LIMIT),
    )(idx, gw, rw, r2w)


def _rs_final_s(gs, rs, r2s, idx):
    def body(i_ref, g_ref, r_ref, p_ref, o_ref):
        acc = g_ref[...] + r_ref[...]
        for j in range(3):
            acc = acc + p_ref[j].astype(F32)
        o_ref[...] = acc

    return pl.pallas_call(
        body,
        grid_spec=pltpu.PrefetchScalarGridSpec(
            num_scalar_prefetch=1, grid=(1,),
            in_specs=[pl.BlockSpec((None, PACK_ROWS, HW), lambda i, ir: (ir[0], 0, ir[1])),
                      pl.BlockSpec((None, PACK_ROWS, HW), lambda i, ir: (ir[0], 0, 0)),
                      pl.BlockSpec((3, PACK_ROWS, HW), lambda i, ir: (0, 0, 0))],
            out_specs=pl.BlockSpec((PACK_ROWS, HW), lambda i, ir: (0, ir[1]))),
        out_shape=jax.ShapeDtypeStruct((PACK_ROWS, PACK_W), F32),
        name="grads_final_sum_s",
        compiler_params=pltpu.CompilerParams(dimension_semantics=("arbitrary",), vmem_limit_bytes=VMEM_LIMIT),
    )(idx, gs, rs, r2s)


def _rs_share(fw, fs):
    def body(w_ref, s_ref, ow_ref, os_ref, send_sems, recv_sems):
        x, y, c = _me()
        wcol = lambda cc: pl.ds(pl.multiple_of(cc * (D // 2), LANE), D // 2)
        scol = lambda cc: pl.ds(pl.multiple_of(cc * HW, LANE), HW)

        def copies(cc):
            return [pltpu.make_async_remote_copy(src_ref=w_ref.at[:, wcol(cc)], dst_ref=ow_ref.at[:, wcol(cc)],
                                                 send_sem=send_sems.at[0], recv_sem=recv_sems.at[0],
                                                 device_id=(x, y, 1 - c), device_id_type=MESH),
                    pltpu.make_async_remote_copy(src_ref=s_ref.at[:, scol(cc)], dst_ref=os_ref.at[:, scol(cc)],
                                                 send_sem=send_sems.at[1], recv_sem=recv_sems.at[1],
                                                 device_id=(x, y, 1 - c), device_id_type=MESH)]
        out = copies(c)
        for cp in out:
            cp.start()
        for cp in copies(1 - c):
            cp.wait_recv()
        for cp in out:
            cp.wait_send()

    return pl.pallas_call(
        body,
        out_shape=[jax.ShapeDtypeStruct(fw.shape, F32), jax.ShapeDtypeStruct(fs.shape, F32)],
        in_specs=[ANY, ANY], out_specs=[ANY, ANY],
        input_output_aliases={0: 0, 1: 1},
        scratch_shapes=[pltpu.SemaphoreType.DMA((2,)), pltpu.SemaphoreType.DMA((2,))],
        name="grads_share",
    )(fw, fs)


def _rs_sums(gw, gs, rw, rs):
    x, y, c = _me()
    cidx = jnp.reshape(c, (1,)).astype(jnp.int32)
    return dict(gw=gw, gs=gs, rw=rw, rs=rs, sw=_rs_chip_sum_w(gw, rw, cidx), ss=_rs_chip_sum_s(gs, rs, cidx))


def _rs_begin(gw, gs):
    return _rs_sums(gw, gs, *_rs_swap(gw, gs))


def _rs_end(st, r2w, r2s):
    x, y, c = _me()
    idx = jnp.stack([2 * x + y, c]).astype(jnp.int32)
    return _rs_share(_rs_final_w(st["gw"], st["rw"], r2w, idx), _rs_final_s(st["gs"], st["rs"], r2s, idx))


def _all_reduce_small(gs):
    rows = gs.shape[0]

    def body(g_ref, o_ref, buf, send_sems, recv_sems):
        x, y, c = _me()
        me = 4 * x + 2 * y + c
        buf[me] = g_ref[...]
        cps = []
        for r in range(1, 8):
            fx, fy, fc = (r >> 2) & 1, (r >> 1) & 1, r & 1
            px, py, pc = jnp.bitwise_xor(x, fx), jnp.bitwise_xor(y, fy), jnp.bitwise_xor(c, fc)
            cps.append((pltpu.make_async_remote_copy(
                src_ref=g_ref, dst_ref=buf.at[me], send_sem=send_sems.at[r - 1], recv_sem=recv_sems.at[r - 1],
                device_id=(px, py, pc), device_id_type=MESH), 4 * px + 2 * py + pc))
        for cp, _ in cps:
            cp.start()
        for r, (cp, peer) in enumerate(cps):
            pltpu.make_async_remote_copy(
                src_ref=g_ref, dst_ref=buf.at[peer], send_sem=send_sems.at[r], recv_sem=recv_sems.at[r],
                device_id=(x, y, c), device_id_type=MESH).wait_recv()
        for cp, _ in cps:
            cp.wait_send()
        acc = buf[0]
        for k in range(1, 8):
            acc = acc + buf[k]
        o_ref[...] = acc

    return pl.pallas_call(
        body,
        out_shape=jax.ShapeDtypeStruct((rows, LANE), F32),
        in_specs=[pl.BlockSpec(memory_space=pltpu.VMEM)],
        out_specs=pl.BlockSpec(memory_space=pltpu.VMEM),
        scratch_shapes=[pltpu.VMEM((8, rows, LANE), F32), pltpu.SemaphoreType.DMA((7,)), pltpu.SemaphoreType.DMA((7,))],
        name="small_grads_all_reduce",
    )(gs)


PACK_SPLIT = (("w_uq", 96, (QL, 192)), ("w_ukv", 64, (KVL, 256)),
              ("w_out_a", 256, (CW, 256)), ("w_out_b", 256, (CW, 256)), ("w_out_c", 256, (CW, 256)),
              ("w_o", 512, (256, D)))
MAT_ROWS = 1440
CONV_SHARD = 3 * 128


def _w_in_words(w_in_shard):
    t = w_in_shard.T
    return _pack_words(t[:, :CWD], t[:, CWD:])


def _pack_weights(wl):
    parts = [wl[n].astype(BF16).reshape(-1, PACK_W) for n, _, _ in PACK_SPLIT]
    cw = wl["conv_w"].reshape(-1)
    hi = cw.astype(BF16)
    r1 = cw - hi.astype(F32)
    mid = r1.astype(BF16)
    lo = (r1 - mid.astype(F32)).astype(BF16)
    cterms = jnp.pad(jnp.concatenate([hi, mid, lo]), (0, 3 * PACK_W - 3 * CONV_SHARD)).reshape(3, PACK_W)
    tail = jnp.pad(cterms, ((0, PACK_ROWS - MAT_ROWS - 3), (0, 0)))
    return jnp.concatenate(parts + [tail], axis=0)


def _unpack_weights(gath):
    out = {}
    r = 0
    for n, nrows, shp in PACK_SPLIT:
        t = gath[:, r:r + nrows].reshape((4,) + shp)
        r += nrows
        if n == "w_o":
            out[n] = t.reshape(4 * shp[0], shp[1])
        else:
            out[n] = t.transpose(1, 0, 2).reshape(shp[0], 4 * shp[1])
    ct = gath[:, r:r + 3].reshape(4, 3 * PACK_W)[:, :3 * CONV_SHARD].astype(F32).reshape(4, 3, CONV_SHARD)
    cw = (ct[:, 0] + ct[:, 1]) + ct[:, 2]
    out["conv_w"] = cw.reshape(4, 3, 128).transpose(1, 0, 2).reshape(3, CW)
    return out


def _pack_grads(g):
    parts = []
    for n, nrows, shp in PACK_SPLIT:
        t = g[n]
        if n == "w_o":
            t = t.reshape((4,) + shp)
        else:
            t = t.reshape(shp[0], 4, shp[1]).transpose(1, 0, 2)
        parts.append(t.reshape(4, nrows, PACK_W))
    cw = g["conv_w"].reshape(3, 4, 128).transpose(1, 0, 2).reshape(4, 1, CONV_SHARD)
    parts.append(jnp.pad(cw, ((0, 0), (0, PACK_ROWS - MAT_ROWS - 1), (0, PACK_W - CONV_SHARD))))
    return jnp.concatenate(parts, axis=1)


def _unpack_grads(red):
    out = {}
    r = 0
    for n, nrows, shp in PACK_SPLIT:
        out[n] = red[r:r + nrows].reshape(shp)
        r += nrows
    out["conv_w"] = red[r, :CONV_SHARD].reshape(3, 128)
    return out


SMALL_SIZES = (("norm_g", D), ("b_gate", 3 * D), ("conv_b", CW), ("q_a_norm_g", QL), ("kv_a_norm_g", KVL),
               ("mla_q_norm_g", QK), ("mla_k_norm_g", QK), ("dil_q_norm_g", NG * HD), ("dil_k_norm_g", NG * HD))
SMALL_ROWS = 88


def _pack_small(per_name):
    flat = jnp.concatenate([per_name[n].reshape(-1).astype(F32) for n, _ in SMALL_SIZES])
    return jnp.pad(flat, (0, SMALL_ROWS * LANE - flat.shape[0])).reshape(SMALL_ROWS, LANE)


def _unpack_small(packed, like):
    out = {}
    flat = packed.reshape(-1)
    r = 0
    for n, sz in SMALL_SIZES:
        out[n] = flat[r:r + NL * sz].reshape(like[n].shape)
        r += NL * sz
    return out


def _adamw_math(w, g, m, v):
    m = ADAM_B1 * m + (1.0 - ADAM_B1) * g
    v = ADAM_B2 * v + (1.0 - ADAM_B2) * jnp.square(g)
    m_hat = m / (1.0 - ADAM_B1 ** ADAM_STEP)
    v_hat = v / (1.0 - ADAM_B2 ** ADAM_STEP)
    delta = -ADAM_LR * (m_hat / (jnp.sqrt(v_hat) + ADAM_EPS) + ADAM_WD * w)
    return delta, m, v


def _adamw(name, w, g, m, v, br):
    L, R, C = w.shape
    blk = lambda l, i: (l, i, 0)
    return _pcall(name, _adamw_math, (L, R // br), [(t, (None, br, C), blk) for t in (w, g, m, v)],
                  [((L, R, C), F32, (None, br, C), blk)] * 3)


def _adamw_w_in(w, g0, g1, m, v):
    nj = D // LANE

    def math(wb, g0b, g1b, mb, vb):
        g = jnp.where(pl.program_id(0) == 0, g0b, g1b)
        return (g,) + _adamw_math(wb, g, mb, vb)

    blk = lambda l, i, j: (l, i, j)
    g_blk = lambda layer, idle: (lambda l, i, j: (0, jnp.where(l == layer, j, idle)))
    return _pcall("adamw_w_in", math, (NL, 1, nj),
                  [(w, (None, WSH, LANE), blk), (g0, (WSH, LANE), g_blk(0, nj - 1)), (g1, (WSH, LANE), g_blk(1, 0)),
                   (m, (None, WSH, LANE), blk), (v, (None, WSH, LANE), blk)],
                  [((NL, WSH, D), F32, (None, WSH, LANE), blk)] * 4)


ADAM_ROWS = {"w_uq": 256, "w_ukv": 128, "w_out_a": 512, "w_out_b": 512, "w_out_c": 512, "w_o": 256,
             "conv_w": 3}


def kernel(x, norm_g, w_in, b_gate, conv_w, conv_b, q_a_norm_g, w_uq, kv_a_norm_g, w_ukv, mla_q_norm_g, mla_k_norm_g, dil_q_norm_g, dil_k_norm_g, w_out_a, w_out_b, w_out_c, w_o, loss_target, m_norm_g, m_w_in, m_b_gate, m_conv_w, m_conv_b, m_q_a_norm_g, m_w_uq, m_kv_a_norm_g, m_w_ukv, m_mla_q_norm_g, m_mla_k_norm_g, m_dil_q_norm_g, m_dil_k_norm_g, m_w_out_a, m_w_out_b, m_w_out_c, m_w_o, v_norm_g, v_w_in, v_b_gate, v_conv_w, v_conv_b, v_q_a_norm_g, v_w_uq, v_kv_a_norm_g, v_w_ukv, v_mla_q_norm_g, v_mla_k_norm_g, v_dil_q_norm_g, v_dil_k_norm_g, v_w_out_a, v_w_out_b, v_w_out_c, v_w_o):
    W = dict(norm_g=norm_g, w_in=w_in, b_gate=b_gate, conv_w=conv_w, conv_b=conv_b, q_a_norm_g=q_a_norm_g, w_uq=w_uq,
             kv_a_norm_g=kv_a_norm_g, w_ukv=w_ukv, mla_q_norm_g=mla_q_norm_g, mla_k_norm_g=mla_k_norm_g,
             dil_q_norm_g=dil_q_norm_g, dil_k_norm_g=dil_k_norm_g, w_out_a=w_out_a, w_out_b=w_out_b, w_out_c=w_out_c,
             w_o=w_o)
    M = dict(norm_g=m_norm_g, w_in=m_w_in, b_gate=m_b_gate, conv_w=m_conv_w, conv_b=m_conv_b, q_a_norm_g=m_q_a_norm_g,
             w_uq=m_w_uq, kv_a_norm_g=m_kv_a_norm_g, w_ukv=m_w_ukv, mla_q_norm_g=m_mla_q_norm_g,
             mla_k_norm_g=m_mla_k_norm_g, dil_q_norm_g=m_dil_q_norm_g, dil_k_norm_g=m_dil_k_norm_g, w_out_a=m_w_out_a,
             w_out_b=m_w_out_b, w_out_c=m_w_out_c, w_o=m_w_o)
    V = dict(norm_g=v_norm_g, w_in=v_w_in, b_gate=v_b_gate, conv_w=v_conv_w, conv_b=v_conv_b, q_a_norm_g=v_q_a_norm_g,
             w_uq=v_w_uq, kv_a_norm_g=v_kv_a_norm_g, w_ukv=v_w_ukv, mla_q_norm_g=v_mla_q_norm_g,
             mla_k_norm_g=v_mla_k_norm_g, dil_q_norm_g=v_dil_q_norm_g, dil_k_norm_g=v_dil_k_norm_g, w_out_a=v_w_out_a,
             w_out_b=v_w_out_b, w_out_c=v_w_out_c, w_o=v_w_o)
    batch = x.shape[0]
    T = batch * S

    def layer_weights(l, cont, gath):
        full = _unpack_weights(gath)
        pad_qk = lambda t: jnp.pad(t, (0, QKP - QK)).reshape(1, QKP)
        full.update(
            w_in_t=cont,
            norm_g=norm_g[l].reshape(1, D), b_gate=b_gate[l].reshape(1, 3 * D), conv_b=conv_b[l].reshape(1, CW),
            q_a_norm_g=q_a_norm_g[l].reshape(1, QL), kv_a_norm_g=kv_a_norm_g[l].reshape(1, KVL),
            mla_q_norm_g=pad_qk(mla_q_norm_g[l]), mla_k_norm_g=pad_qk(mla_k_norm_g[l]),
            dil_q_norm_g=dil_q_norm_g[l].reshape(NG, 1, HD), dil_k_norm_g=dil_k_norm_g[l].reshape(NG, 1, HD))
        return full

    words = [_w_in_words(w_in[l]) for l in range(NL)]
    packs = [_pack_weights({n: W[n][l] for n in BIG[1:] + ("conv_w",)}) for l in range(NL)]
    tabs = _rope_tables() + (_dil_slopes(),)
    x2 = x.reshape(T, D)

    cont0, gath0 = _all_gather(words[0], packs[0])
    w0 = layer_weights(0, cont0, gath0)
    ag = _ag_behind_start(words[1], packs[1], gath0)
    w0["norm_g"] = w0["norm_g"] + ag[6][0:1, 0:1]
    y0, res0 = _layer_fwd(x2, w0, tabs, batch)
    w1 = layer_weights(1, *_ag_behind_wait(ag[0], ag[1], ag[2], ag[3], ag[4], ag[5], y0))
    y1, res1 = _layer_fwd(y0, w1, tabs, batch)

    row = lambda i: (i, 0)
    dy, loss = _pcall("loss", _loss_math, (T // BR,),
                      [(y1, (BR, D), row), (loss_target.reshape(T, D), (BR, D), row)],
                      [((T, D), F32, (BR, D), row), ((1, 1), F32, (1, 1), lambda i: (0, 0), True)])
    loss = lax.psum(loss[0, 0], ("x", "y", "c"))

    grads = [None] * NL
    dy, grads[1] = _layer_bwd(dy, w1, res1, tabs, batch)
    st = [None] * NL
    ex = [None] * NL
    sw1 = _rs_swap_start(grads[1]["w_in_t"], _pack_grads(grads[1]))
    w0["w_o"] = w0["w_o"] + sw1[6][0:1, 0:1].astype(BF16)

    def exchange_layer1(t):
        st[1] = _rs_sums(*_rs_swap_wait(*sw1[:6], t))
        ex[1] = _rs_exchange_start(st[1]["sw"], st[1]["ss"], "1")
        return ex[1][6]

    def start_layer0(g):
        st[0] = _rs_begin(g["w_in_t"], _pack_grads(g))
        ex[0] = _rs_exchange_start(st[0]["sw"], st[0]["ss"], "0")
        return ex[0][6]

    dx, grads[0] = _layer_bwd(dy, w0, res0, tabs, batch, after_dw=start_layer0, after_merge=exchange_layer1)
    grad_x = dx.reshape(batch, S, D)

    red = [None] * NL
    for l in (1, 0):
        r2w, r2s = _rs_exchange_wait(*ex[l][:6], dx, str(l))
        rw, rs = _rs_end(st[l], r2w, r2s)
        r = _unpack_grads(rs)
        r["w_in_t"] = rw
        red[l] = r
    G = {n: jnp.stack([red[l][n] for l in range(NL)]) for n in BIG[1:] + ("conv_w",)}
    small_g = {n: jnp.stack([grads[l][n].reshape(-1)[:sz] for l in range(NL)]) for n, sz in SMALL_SIZES}
    small_red = _all_reduce_small(_pack_small(small_g))
    G.update(_unpack_small(small_red, {n: W[n] for n in SMALL}))

    delta, new_m, new_v = {}, {}, {}
    for n in BIG[1:] + ("conv_w",):
        delta[n], new_m[n], new_v[n] = _adamw("adamw_" + n, W[n], G[n], M[n], V[n], ADAM_ROWS[n])
    tr = lambda t: jnp.swapaxes(t, 1, 2)
    G["w_in"], delta["w_in"], new_m["w_in"], new_v["w_in"] = (
        tr(t) for t in _adamw_w_in(tr(w_in), red[0]["w_in_t"], red[1]["w_in_t"], tr(m_w_in), tr(v_w_in)))
    sw, sm, sv = (_pack_small({n: t[n] for n in SMALL})[None] for t in (W, M, V))
    sd, snm, snv = _adamw("adamw_small", sw, small_red[None], sm, sv, SMALL_ROWS)
    like = {n: W[n] for n in SMALL}
    delta.update(_unpack_small(sd[0], like))
    new_m.update(_unpack_small(snm[0], like))
    new_v.update(_unpack_small(snv[0], like))

    return (loss, grad_x, *[G[n] for n in WEIGHTS], *[delta[n] for n in WEIGHTS],
            *[new_m[n] for n in WEIGHTS], *[new_v[n] for n in WEIGHTS])
```

```python
import functools

import numpy as np
import jax
import jax.numpy as jnp
from jax import lax
from jax.experimental import pallas as pl
from jax.experimental.pallas import tpu as pltpu

F32 = jnp.float32
BF16 = jnp.bfloat16

D = 1024
S = 2048
NL = 2
CW = 512
NH = 8
QL = 256
KVL = 128
NOPE = 64
ROPE = 32
VD = 64
QK = NOPE + ROPE
QKP = 128
ROPE_THETA = 10000.0
DIL = ((128, 1), (512, 4), (2048, 16))
NG = 3
DH = 8
HD = 64
DWID = DH * HD
QB = 128
EPS = 1e-6
NIN = 11168
NINP = 11264
O_A, O_CQ, O_CKV, O_KPE, O_BZ, O_DQ, O_DK, O_DV, O_CZ, O_G = 0, 2048, 2304, 2432, 2560, 3072, 4608, 6144, 7680, 8192
KPE_END = 2464
NEG = -1e30
MLA_SCALE = QK ** -0.5
DIL_SCALE = HD ** -0.5
LANE = 128
PACK_W = 512
VMEM_LIMIT = 48 * 1024 * 1024

ADAM_LR = 0.001
ADAM_B1 = 0.9
ADAM_B2 = 0.999
ADAM_EPS = 1e-08
ADAM_WD = 0.01
ADAM_STEP = 10

MESH = pl.DeviceIdType.MESH
BIG = ("w_in", "w_uq", "w_ukv", "w_out_a", "w_out_b", "w_out_c", "w_o")
SMALL = ("norm_g", "b_gate", "conv_b", "q_a_norm_g", "kv_a_norm_g", "mla_q_norm_g", "mla_k_norm_g",
         "dil_q_norm_g", "dil_k_norm_g")
WEIGHTS = ("norm_g", "w_in", "b_gate", "conv_w", "conv_b", "q_a_norm_g", "w_uq", "kv_a_norm_g", "w_ukv",
           "mla_q_norm_g", "mla_k_norm_g", "dil_q_norm_g", "dil_k_norm_g", "w_out_a", "w_out_b", "w_out_c", "w_o")


def _dot(a, b):
    return jnp.dot(a, b, preferred_element_type=F32)


def _dot_nt(a, b):
    return lax.dot_general(a, b, (((1,), (1,)), ((), ())), preferred_element_type=F32)


def _dot_tn(a, b):
    return lax.dot_general(a, b, (((0,), (0,)), ((), ())), preferred_element_type=F32)


def _grid_step(grid):
    step = pl.program_id(0)
    for a in range(1, len(grid)):
        step = step * grid[a] + pl.program_id(a)
    n = 1
    for g in grid:
        n *= g
    return step, n


def _write_windows(buf_ref, stages, sems, step, nsteps, puts):
    slot = step % 2
    for t, (v, dst) in enumerate(puts):
        cp = pltpu.make_async_copy(stages[t].at[slot], dst, sems.at[t, slot])

        @pl.when(step >= 2)
        def _():
            cp.wait()

        stages[t][slot] = v.astype(stages[t].dtype).reshape(stages[t].shape[1:])
        cp.start()

    @pl.when(step == nsteps - 1)
    def _():
        for t, (v, dst) in enumerate(puts):
            pltpu.make_async_copy(stages[t].at[slot], dst, sems.at[t, slot]).wait()
            if nsteps > 1:
                pltpu.make_async_copy(stages[t].at[1 - slot], dst, sems.at[t, 1 - slot]).wait()


def _pcall(name, fn, grid, ins, outs, into=None):
    n_in = len(ins)
    n_out = len(outs)
    acc_axis = len(grid) - 1
    is_acc = [len(o) > 4 and o[4] for o in outs]
    outs = [o[:4] for o in outs]
    targets = into[1] if into is not None else []
    n_t = len(targets)

    def body(*refs):
        vals = fn(*[r[...].astype(F32) for r in refs[:n_in]])
        if not isinstance(vals, (tuple, list)):
            vals = (vals,)
        o0 = n_in + (1 if n_t else 0)
        for k in range(n_out):
            r = refs[o0 + k]
            v = vals[k].astype(r.dtype).reshape(r.shape)
            if is_acc[k]:
                first = pl.program_id(acc_axis) == 0

                @pl.when(first)
                def _():
                    r[...] = v

                @pl.when(jnp.logical_not(first))
                def _():
                    r[...] += v
            else:
                r[...] = v
        if n_t:
            buf_ref = refs[o0 + n_out]
            stages = refs[o0 + n_out + 1:o0 + n_out + 1 + n_t]
            ids = [pl.program_id(a) for a in range(len(grid))]
            step, nsteps = _grid_step(grid)
            _write_windows(buf_ref, stages, refs[-1], step, nsteps,
                           [(vals[n_out + t], targets[t][1](buf_ref, *ids)) for t in range(n_t)])

    in_specs = [pl.BlockSpec(bs, im) for _, bs, im in ins]
    out_specs = [pl.BlockSpec(bs, im) for _, _, bs, im in outs]
    out_shape = [jax.ShapeDtypeStruct(sh, dt) for sh, dt, _, _ in outs]
    args = [a for a, _, _ in ins]
    extra = {}
    if n_t:
        buf = into[0]
        in_specs.append(pl.BlockSpec(memory_space=pl.ANY))
        out_specs.append(pl.BlockSpec(memory_space=pl.ANY))
        out_shape.append(jax.ShapeDtypeStruct(buf.shape, buf.dtype))
        args.append(buf)
        extra = dict(input_output_aliases={n_in: n_out},
                     scratch_shapes=[pltpu.VMEM((2,) + tuple(bs), buf.dtype) for bs, _ in targets]
                     + [pltpu.SemaphoreType.DMA((n_t, 2))])
    return pl.pallas_call(
        body,
        grid=grid,
        in_specs=in_specs,
        out_specs=out_specs,
        out_shape=out_shape,
        name=name,
        compiler_params=pltpu.CompilerParams(
            dimension_semantics=("arbitrary",) * len(grid), vmem_limit_bytes=VMEM_LIMIT),
        **extra,
    )(*args)


def _mm(name, a, b, *, ta=False, tb=False, out_dtype=F32, add=None, dep=None, b_words=False, tm=2048, tn=1024, tk=1024):
    if ta:
        K, M = a.shape
    else:
        M, K = a.shape
    bshape = (b.shape[0], 2 * b.shape[1]) if b_words else b.shape
    if tb:
        N, K2 = bshape
    else:
        K2, N = bshape
    assert K == K2, (name, a.shape, b.shape)
    tm, tn, tk = min(tm, M), min(tn, N), min(tk, K)
    assert M % tm == 0 and N % tn == 0 and K % tk == 0, (name, M, N, K)
    nk = K // tk
    dims = (((0 if ta else 1,), (1 if tb else 0,)), ((), ()))
    a_spec = pl.BlockSpec((tk, tm), lambda j, i, k: (k, i)) if ta else pl.BlockSpec((tm, tk), lambda j, i, k: (i, k))
    bw = 2 if b_words else 1
    assert not b_words or (tk if tb else tn) == bshape[1]
    b_spec = (pl.BlockSpec((tn, tk // bw), lambda j, i, k: (j, k)) if tb
              else pl.BlockSpec((tk, tn // bw), lambda j, i, k: (k, j)))
    o_spec = pl.BlockSpec((tm, tn), lambda j, i, k: (i, j))
    has_add = add is not None
    n_in = 2 + has_add + (dep is not None)

    def body(*refs):
        a_ref, b_ref = refs[0], refs[1]
        add_ref = refs[2] if has_add else None
        o_ref = refs[n_in]
        bb = b_ref[...]
        if b_words:
            lo, hi = _unpack_words(bb)
            first = (pl.program_id(0) * tn) if tb else (pl.program_id(2) * tk)
            r = first + lax.broadcasted_iota(jnp.int32, lo.shape, 0)
            pad = jnp.logical_and(r >= KPE_END, r < KPE_END + NINP - NIN)
            bb = jnp.concatenate([jnp.where(pad, 0.0, lo), jnp.where(pad, 0.0, hi)], axis=1)
        p = lax.dot_general(a_ref[...].astype(BF16), bb.astype(BF16), dims, preferred_element_type=F32)
        if nk == 1:
            if has_add:
                p = p + add_ref[...]
            o_ref[...] = p.astype(out_dtype)
        else:
            acc = refs[-1]
            k = pl.program_id(2)

            @pl.when(k == 0)
            def _():
                acc[...] = p

            @pl.when(k > 0)
            def _():
                acc[...] += p

            @pl.when(k == nk - 1)
            def _():
                r = acc[...]
                if has_add:
                    r = r + add_ref[...]
                o_ref[...] = r.astype(out_dtype)

    in_specs = [a_spec, b_spec] + ([o_spec] if has_add else []) + ([pl.BlockSpec(memory_space=pl.ANY)] if dep is not None else [])
    args = [a, b] + ([add] if has_add else []) + ([dep] if dep is not None else [])
    return pl.pallas_call(
        body,
        grid=(N // tn, M // tm, nk),
        in_specs=in_specs,
        out_specs=o_spec,
        out_shape=jax.ShapeDtypeStruct((M, N), out_dtype),
        scratch_shapes=[pltpu.VMEM((tm, tn), F32)] if nk > 1 else [],
        name=name,
        compiler_params=pltpu.CompilerParams(
            dimension_semantics=("arbitrary", "arbitrary", "arbitrary"), vmem_limit_bytes=VMEM_LIMIT),
    )(*args)


def _vjp_of(f, n_diff):
    def g(*args, n_prim):
        prim = args[:n_diff]
        consts = args[n_diff:n_prim]
        cts = args[n_prim:]
        _, pull = jax.vjp(lambda *p: f(*p, *consts), *prim)
        out = jax.eval_shape(lambda *p: f(*p, *consts), *prim)
        if isinstance(out, (tuple, list)):
            cts = tuple(c.astype(o.dtype) for c, o in zip(cts, out))
        else:
            cts = cts[0].astype(out.dtype)
        return pull(cts)
    return g


def _rms(x, g, n=None):
    n = x.shape[-1] if n is None else n
    ms = jnp.sum(x * x, axis=-1, keepdims=True) / n
    return x * lax.rsqrt(ms + EPS) * g


def _silu(z):
    return z * jax.nn.sigmoid(z)


def _roll_rows(u, k):
    n = u.shape[0]
    r = pltpu.roll(u, k % n, 0)
    t = lax.broadcasted_iota(jnp.int32, u.shape, 0)
    if k > 0:
        return jnp.where(t >= k, r, 0.0)
    return jnp.where(t < n + k, r, 0.0)


@functools.partial(jax.custom_vjp, nondiff_argnums=(1,))
def _shift(u, k):
    return _roll_rows(u, k)


def _shift_fwd(u, k):
    return _roll_rows(u, k), None


def _shift_bwd(k, _, g):
    return (_roll_rows(g, -k),)


_shift.defvjp(_shift_fwd, _shift_bwd)


@functools.partial(jax.custom_vjp, nondiff_argnums=(1,))
def _lane_roll(u, k):
    return pltpu.roll(u, k % LANE, 1)


def _lane_roll_fwd(u, k):
    return pltpu.roll(u, k % LANE, 1), None


def _lane_roll_bwd(k, _, g):
    return (pltpu.roll(g, (-k) % LANE, 1),)


_lane_roll.defvjp(_lane_roll_fwd, _lane_roll_bwd)


def _conv_math(ab, ac, ax, az, cw, cb):
    u = ac * ax
    conv = cb + _shift(u, 2) * cw[0:1] + _shift(u, 1) * cw[1:2] + u * cw[2:3]
    return ab * conv * _silu(az)


def _mla_pre_math(cq, ckv, gq, gkv):
    return _rms(cq, gq), _rms(ckv, gkv)


def _rope_math(q, kn, kpe, gq, gk, c, s1, s2):
    lane = lax.broadcasted_iota(jnp.int32, kpe.shape, 1)
    pe = _lane_roll(jnp.where(lane < ROPE, kpe, 0.0), NOPE)

    def one(t, g):
        tn = _rms(t, g, QK)
        return tn * c + _lane_roll(tn, -16) * s1 + _lane_roll(tn, 16) * s2

    qs, ks = [], []
    for h in range(NH):
        sl = slice(h * QKP, (h + 1) * QKP)
        qs.append(one(q[:, sl], gq))
        ks.append(one(kn[:, sl] + pe, gk))
    return jnp.concatenate(qs, axis=1), jnp.concatenate(ks, axis=1)


def _gate_math(o, z):
    return o * _silu(z)


def _mergec_math(o0, o1, o2, l0, l1, l2, cz):
    m = lax.stop_gradient(jnp.maximum(jnp.maximum(l0, l1), l2))
    e0, e1, e2 = jnp.exp(l0 - m), jnp.exp(l1 - m), jnp.exp(l2 - m)
    den = e0 + e1 + e2
    oc = (e0 / den) * o0 + (e1 / den) * o1 + (e2 / den) * o2
    return oc * _silu(cz)


def _merge_math(g0, g1, g2, b0, b1, b2, pa, pb, pc):
    return (jax.nn.sigmoid(g0 + b0) * pa + jax.nn.sigmoid(g1 + b1) * pb) + jax.nn.sigmoid(g2 + b2) * pc


MLA_T = 256
MLA_UNROLL = True


def _mla_fwd(q, k, v):
    B = q.shape[0]
    T = MLA_T
    NB = S // T

    def body(q_ref, k_ref, v_ref, o_ref, l_ref):
        row = lax.broadcasted_iota(jnp.int32, (T, T), 0)
        col = lax.broadcasted_iota(jnp.int32, (T, T), 1)
        lo = _lo_mask((T, LANE))

        for qi in range(NB):
            qb = q_ref[qi * T:(qi + 1) * T, :]

            def step(j, carry, diagonal):
                m, l, acc = carry
                off = pl.multiple_of(j * T, T)
                kb = k_ref[pl.ds(off, T), :]
                vb = v_ref[pl.ds(off, T), :]
                ss = []
                for e in (0, 1):
                    se = _dot_nt(qb[:, e * QKP:(e + 1) * QKP], kb[:, e * QKP:(e + 1) * QKP]) * MLA_SCALE
                    ss.append(jnp.where(col <= row, se, NEG) if diagonal else se)
                s = jnp.concatenate(ss, axis=0)
                m_new = jnp.maximum(m, jnp.max(s, axis=-1, keepdims=True))
                a = jnp.exp(m - m_new)
                p = jnp.exp(s - m_new)
                l = a * l + jnp.sum(p, axis=-1, keepdims=True)
                acc = a * acc + _dot(p.astype(BF16), vb)
                return m_new, l, acc

            init = (jnp.full((2 * T, 1), NEG, F32), jnp.zeros((2 * T, 1), F32), jnp.zeros((2 * T, LANE), F32))
            carry = lax.fori_loop(0, qi, functools.partial(step, diagonal=False), init, unroll=MLA_UNROLL)
            m, l, acc = step(qi, carry, True)
            o = acc / l
            lse = m + jnp.log(l)
            o_ref[qi * T:(qi + 1) * T, :] = jnp.where(lo, o[:T], o[T:])
            l_ref[qi * T:(qi + 1) * T, :] = jnp.where(lo, lse[:T], lse[T:])

    def spec(w):
        return pl.BlockSpec((None, S, w), lambda b, hp: (b, 0, hp))

    return pl.pallas_call(
        body,
        grid=(B, NH // 2),
        in_specs=[spec(2 * QKP), spec(2 * QKP), spec(LANE)],
        out_specs=[spec(LANE), spec(LANE)],
        out_shape=[jax.ShapeDtypeStruct((B, S, NH * VD), F32)] * 2,
        name="mla_attn_fwd",
        compiler_params=pltpu.CompilerParams(dimension_semantics=("arbitrary",) * 2, vmem_limit_bytes=VMEM_LIMIT),
    )(q, k, v)


def _mla_bwd(q, k, v, do, o, lse):
    B = q.shape[0]
    T = MLA_T
    NB = S // T

    def body(q_ref, k_ref, v_ref, do_ref, o_ref, l_ref, dq_ref, dk_ref, dv_ref, delta_ref, dqt_ref):
        delta_ref[...] = _head_sum(do_ref[...] * o_ref[...])
        row = lax.broadcasted_iota(jnp.int32, (T, T), 0)
        col = lax.broadcasted_iota(jnp.int32, (T, T), 1)
        lo = _lo_mask((T, LANE))
        tn_t = (((0,), (1,)), ((), ()))

        for j in range(NB):
            krows = slice(j * T, (j + 1) * T)
            kb = k_ref[krows, :]
            vb = v_ref[krows, :]
            dkt = [jnp.zeros((QKP, T), F32), jnp.zeros((QKP, T), F32)]
            dvt = jnp.zeros((LANE, T), F32)
            for i in range(j, NB):
                qrows = slice(i * T, (i + 1) * T)
                qb = q_ref[qrows, :]
                do2 = _stack_heads(do_ref[qrows, :], lo).astype(BF16)
                lb = l_ref[qrows, :]
                db = delta_ref[qrows, :]
                dp2 = _dot_nt(do2, vb)
                ps = []
                for e in (0, 1):
                    cols = slice(e * QKP, (e + 1) * QKP)
                    qe, ke = qb[:, cols], kb[:, cols]
                    s = _dot_nt(qe, ke) * MLA_SCALE
                    if i == j:
                        s = jnp.where(col <= row, s, NEG)
                    p = jnp.exp(s - lb[:, e * HD:e * HD + 1])
                    ps.append(p.astype(BF16))
                    ds = (p * (dp2[e * T:(e + 1) * T] - db[:, e * HD:e * HD + 1]) * MLA_SCALE).astype(BF16)
                    dkt[e] = dkt[e] + _dot_tn(qe, ds)
                    dq_t = lax.dot_general(ke, ds, tn_t, preferred_element_type=F32)
                    if j == 0:
                        dqt_ref[e, :, qrows] = dq_t
                    else:
                        dqt_ref[e, :, qrows] += dq_t
                dvt = dvt + _dot_tn(do2, jnp.concatenate(ps, axis=0))
            dk_ref[krows, 0:QKP] = dkt[0].T
            dk_ref[krows, QKP:2 * QKP] = dkt[1].T
            dv_ref[krows, :] = dvt.T
        dq_ref[:, 0:QKP] = dqt_ref[0].T
        dq_ref[:, QKP:2 * QKP] = dqt_ref[1].T

    def spec(w):
        return pl.BlockSpec((None, S, w), lambda b, hp: (b, 0, hp))

    return pl.pallas_call(
        body,
        grid=(B, NH // 2),
        in_specs=[spec(2 * QKP), spec(2 * QKP), spec(LANE), spec(LANE), spec(LANE), spec(LANE)],
        out_specs=[spec(2 * QKP), spec(2 * QKP), spec(LANE)],
        out_shape=[jax.ShapeDtypeStruct((B, S, NH * QKP), F32), jax.ShapeDtypeStruct((B, S, NH * QKP), F32),
                   jax.ShapeDtypeStruct((B, S, NH * VD), F32)],
        scratch_shapes=[pltpu.VMEM((S, LANE), F32), pltpu.VMEM((2, QKP, S), F32)],
        name="mla_attn_bwd",
        compiler_params=pltpu.CompilerParams(dimension_semantics=("arbitrary",) * 2, vmem_limit_bytes=VMEM_LIMIT),
    )(q, k, v, do, o, lse)


def _lo_mask(shape):
    return lax.broadcasted_iota(jnp.int32, shape, len(shape) - 1) < HD


def _head_sum(u):
    r = lax.broadcasted_iota(jnp.int32, (LANE, LANE), 0) < HD
    c = lax.broadcasted_iota(jnp.int32, (LANE, LANE), 1) < HD
    ones = jnp.where(r == c, 1.0, 0.0).astype(BF16)
    hi = u.astype(BF16)
    lo = (u - hi.astype(F32)).astype(BF16)
    return _dot(hi, ones) + _dot(lo, ones)


def _head_sum_1(u):
    r = lax.broadcasted_iota(jnp.int32, (LANE, LANE), 0) < HD
    c = lax.broadcasted_iota(jnp.int32, (LANE, LANE), 1) < HD
    return _dot(u.astype(BF16), jnp.where(r == c, 1.0, 0.0).astype(BF16))


def _rms2_scale(x):
    return lax.rsqrt(_head_sum(x * x) / HD + EPS)


def _rms2(x, g):
    return x * _rms2_scale(x) * g


def _rms2_bwd(x, r, g, dy):
    xn = x * r
    t = dy * g
    dx = r * (t - xn * (_head_sum_1(xn * t) * (1.0 / HD)))
    return dx, jnp.sum(dy * xn, axis=0, keepdims=True)


def _dil_bias(t_ref, gi, d):
    qq = lax.broadcasted_iota(jnp.int32, (QB, QB), 0)
    kk = lax.broadcasted_iota(jnp.int32, (QB, QB), 1)
    jc = (qq - kk).astype(F32)
    rows = []
    for e in (0, 1):
        sl = t_ref[2 * gi + e:2 * gi + e + 1, :] * float(d)
        bp = jnp.where(kk >= qq, -sl * (jc + float(QB)), NEG)
        bc = jnp.where(kk <= qq, -sl * jc, NEG)
        rows.append(jnp.concatenate([bp, bc], axis=1))
    return jnp.concatenate(rows, axis=0)


def _dil_rows(cur, d):
    return pl.ds(cur, QB, stride=d) if d > 1 else pl.ds(pl.multiple_of(cur, QB), QB)


def _dil_walk(d, block, full):
    if d == 1:
        block(0, None)

        def body(i, c):
            block(i * QB, (i - 1) * QB)
            return c
        lax.fori_loop(1, S // QB, body, 0, unroll=True if full else 5)
    elif d == 16:
        def body(r, c):
            block(r, None)
            return c
        lax.fori_loop(0, d, body, 0, unroll=True if full else 4)
    else:
        nb = S // d // QB

        def cls(r, c):
            block(r, None)

            def body(i, c2):
                block(r + i * QB * d, r + (i - 1) * QB * d)
                return c2
            lax.fori_loop(1, nb, body, 0, unroll=True)
            return c
        lax.fori_loop(0, d, cls, 0, unroll=full)


def _stack_heads(x, lo):
    return jnp.concatenate([jnp.where(lo, x, 0.0), jnp.where(lo, 0.0, x)], axis=0)


def _dilc_fwd(proj3, gq, gk, tab):
    B = proj3.shape[0]

    def body(q_ref, k_ref, v_ref, cz_ref, gq_ref, gk_ref, t_ref, y_ref, o_ref, l_ref, qs, ks, vs):
        g = pl.program_id(2)
        lo = _lo_mask((QB, LANE))

        def group(gi):
            d = DIL[gi][1]
            qs[...] = _rms2(q_ref[...].astype(F32), gq_ref[gi:gi + 1, :])
            ks[...] = _rms2(k_ref[...].astype(F32), gk_ref[gi:gi + 1, :])
            vs[...] = v_ref[...].astype(F32)
            bias = _dil_bias(t_ref, gi, d)

            def block(cur, prev):
                rows = _dil_rows(cur, d)
                q2 = _stack_heads(qs[rows, :], lo).astype(BF16)
                kc, vc = ks[rows, :], vs[rows, :]
                if prev is None:
                    kcat, vcat, b = kc, vc, bias[:, QB:]
                else:
                    prow = _dil_rows(prev, d)
                    kcat = jnp.concatenate([ks[prow, :], kc], axis=0)
                    vcat = jnp.concatenate([vs[prow, :], vc], axis=0)
                    b = bias
                s = _dot_nt(q2, kcat.astype(BF16)) * DIL_SCALE + b
                m = jnp.max(s, axis=-1, keepdims=True)
                p = jnp.exp(s - m)
                l = jnp.sum(p, axis=-1, keepdims=True)
                o = _dot(p.astype(BF16), vcat.astype(BF16)) / l
                lse = m + jnp.log(l)
                o_ref[gi, rows, :] = jnp.where(lo, o[:QB], o[QB:])
                l_ref[gi, rows, :] = jnp.where(lo, lse[:QB], lse[QB:])

            _dil_walk(d, block, True)

        for gi in range(NG):
            pl.when(g == gi)(functools.partial(group, gi))

        @pl.when(g == NG - 1)
        def _():
            y_ref[...] = _mergec_math(o_ref[0], o_ref[1], o_ref[2], l_ref[0], l_ref[1], l_ref[2],
                                      cz_ref[...].astype(F32)).astype(BF16)

    def col(base):
        return pl.BlockSpec((None, S, LANE), lambda b, hp, g: (b, 0, base // LANE + 4 * g + hp))

    gspec = pl.BlockSpec((NG, LANE), lambda b, hp, g: (0, 0))
    saved = pl.BlockSpec((NG, None, S, LANE), lambda b, hp, g: (0, b, 0, hp))
    return pl.pallas_call(
        body,
        grid=(B, 4, NG),
        in_specs=[col(O_DQ), col(O_DK), col(O_DV),
                  pl.BlockSpec((None, S, LANE), lambda b, hp, g: (b, 0, O_CZ // LANE + hp)),
                  gspec, gspec, pl.BlockSpec((None, 8, LANE), lambda b, hp, g: (hp, 0, 0))],
        out_specs=[pl.BlockSpec((None, S, LANE), lambda b, hp, g: (b, 0, hp)), saved, saved],
        out_shape=[jax.ShapeDtypeStruct((B, S, DWID), BF16), jax.ShapeDtypeStruct((NG, B, S, DWID), F32),
                   jax.ShapeDtypeStruct((NG, B, S, DWID), F32)],
        scratch_shapes=[pltpu.VMEM((S, LANE), F32)] * 3,
        name="dil_mixer_fwd",
        compiler_params=pltpu.CompilerParams(dimension_semantics=("arbitrary",) * 3, vmem_limit_bytes=VMEM_LIMIT),
    )(proj3, proj3, proj3, proj3, gq, gk, tab)


MERGE_ROWS = 256


def _dilc_bwd(proj3, gq, gk, tab, o_all, l_all, d_yc, dproj3):
    B = proj3.shape[0]

    def body(q_ref, k_ref, v_ref, cz_ref, gq_ref, gk_ref, t_ref, o_ref, l_ref, dy_ref, dp_in,
             dp_out, dgq_out, dgk_out, qs, ks, vs, dos, dls, dqs, dks, dvs, rqs, rks, dczs,
             st_q, st_k, st_v, st_z, sems, sem_z):
        b_, hp, g = pl.program_id(0), pl.program_id(1), pl.program_id(2)
        col = lambda base: pl.ds(pl.multiple_of(base + hp * LANE, LANE), LANE)
        lo = _lo_mask((QB, LANE))

        @pl.when(jnp.logical_and(jnp.logical_and(pl.program_id(0) == 0, pl.program_id(1) == 0), g == 0))
        def _():
            dgq_out[...] = jnp.zeros((NG, LANE), F32)
            dgk_out[...] = jnp.zeros((NG, LANE), F32)

        @pl.when(g == 0)
        def _():
            def chunk(i, carry):
                rows = pl.ds(pl.multiple_of(i * MERGE_ROWS, MERGE_ROWS), MERGE_ROWS)
                ls = [l_ref[j, rows, :] for j in range(NG)]
                m = jnp.maximum(jnp.maximum(ls[0], ls[1]), ls[2])
                es = [jnp.exp(t - m) for t in ls]
                den = (es[0] + es[1]) + es[2]
                al = [e / den for e in es]
                os_ = [o_ref[j, rows, :] for j in range(NG)]
                oc = (al[0] * os_[0] + al[1] * os_[1]) + al[2] * os_[2]
                cz = cz_ref[rows, :].astype(F32)
                sg = jax.nn.sigmoid(cz)
                dy = dy_ref[rows, :]
                d_oc = dy * (cz * sg)
                dczs[rows, :] = (dy * oc * (sg * (1.0 + cz * (1.0 - sg)))).astype(BF16)
                ts = [_head_sum_1(d_oc * os_[j]) for j in range(NG)]
                tbar = (al[0] * ts[0] + al[1] * ts[1]) + al[2] * ts[2]
                for j in range(NG):
                    dos[j, rows, :] = al[j] * d_oc
                    dls[j, rows, :] = al[j] * (ts[j] - tbar)
                return carry
            lax.fori_loop(0, S // MERGE_ROWS, chunk, 0)
            _write_windows(dp_out, [st_z], sem_z, b_ * 4 + hp, B * 4, [(dczs[...], dp_out.at[b_, :, col(O_CZ)])])

        def group(gi):
            d = DIL[gi][1]
            xq, xk = q_ref[...].astype(F32), k_ref[...].astype(F32)
            rqs[...] = _rms2_scale(xq)
            rks[...] = _rms2_scale(xk)
            qs[...] = xq * rqs[...] * gq_ref[gi:gi + 1, :]
            ks[...] = xk * rks[...] * gk_ref[gi:gi + 1, :]
            vs[...] = v_ref[...].astype(F32)
            dks[...] = jnp.zeros((S, LANE), F32)
            dvs[...] = jnp.zeros((S, LANE), F32)
            bias = _dil_bias(t_ref, gi, d)

            def block(cur, prev):
                rows = _dil_rows(cur, d)
                q2 = _stack_heads(qs[rows, :], lo).astype(BF16)
                dob = dos[gi, rows, :]
                do2 = _stack_heads(dob, lo).astype(BF16)
                kc, vc = ks[rows, :], vs[rows, :]
                if prev is None:
                    kcat, vcat, b = kc, vc, bias[:, QB:]
                else:
                    prow = _dil_rows(prev, d)
                    kcat = jnp.concatenate([ks[prow, :], kc], axis=0)
                    vcat = jnp.concatenate([vs[prow, :], vc], axis=0)
                    b = bias
                kcat = kcat.astype(BF16)
                vcat = vcat.astype(BF16)
                lse_b = l_ref[gi, rows, :]
                corr_b = dls[gi, rows, :] - _head_sum_1(dob * o_ref[gi, rows, :])
                lse2 = jnp.concatenate([lse_b[:, 0:1], lse_b[:, HD:HD + 1]], axis=0)
                corr2 = jnp.concatenate([corr_b[:, 0:1], corr_b[:, HD:HD + 1]], axis=0)
                s = _dot_nt(q2, kcat) * DIL_SCALE + b
                p = jnp.exp(s - lse2)
                ds = (p * (_dot_nt(do2, vcat) + corr2) * DIL_SCALE).astype(BF16)
                dq2 = _dot(ds, kcat)
                dqs[rows, :] = jnp.where(lo, dq2[:QB], dq2[QB:])
                dk = _dot_tn(ds, q2)
                dv = _dot_tn(p.astype(BF16), do2)
                if prev is None:
                    dks[rows, :] += dk
                    dvs[rows, :] += dv
                else:
                    dks[prow, :] += dk[:QB]
                    dvs[prow, :] += dv[:QB]
                    dks[rows, :] += dk[QB:]
                    dvs[rows, :] += dv[QB:]

            _dil_walk(d, block, False)

            dxq, dgq = _rms2_bwd(q_ref[...].astype(F32), rqs[...], gq_ref[gi:gi + 1, :], dqs[...])
            dgq_out[gi:gi + 1, :] += dgq
            dxk, dgk = _rms2_bwd(k_ref[...].astype(F32), rks[...], gk_ref[gi:gi + 1, :], dks[...])
            dgk_out[gi:gi + 1, :] += dgk
            step, nsteps = _grid_step((B, 4, NG))
            _write_windows(dp_out, [st_q, st_k, st_v], sems, step, nsteps,
                           [(dxq, dp_out.at[b_, :, col(O_DQ + gi * DWID)]), (dxk, dp_out.at[b_, :, col(O_DK + gi * DWID)]),
                            (dvs[...], dp_out.at[b_, :, col(O_DV + gi * DWID)])])

        for gi in range(NG):
            pl.when(g == gi)(functools.partial(group, gi))

    def col(base):
        return pl.BlockSpec((None, S, LANE), lambda b, hp, g: (b, 0, base // LANE + 4 * g + hp))

    gspec = pl.BlockSpec((NG, LANE), lambda b, hp, g: (0, 0))
    saved = pl.BlockSpec((NG, None, S, LANE), lambda b, hp, g: (0, b, 0, hp))
    per_pair = pl.BlockSpec((None, S, LANE), lambda b, hp, g: (b, 0, hp))
    return pl.pallas_call(
        body,
        grid=(B, 4, NG),
        in_specs=[col(O_DQ), col(O_DK), col(O_DV),
                  pl.BlockSpec((None, S, LANE), lambda b, hp, g: (b, 0, O_CZ // LANE + hp)),
                  gspec, gspec, pl.BlockSpec((None, 8, LANE), lambda b, hp, g: (hp, 0, 0)),
                  saved, saved, per_pair, pl.BlockSpec(memory_space=pl.ANY)],
        out_specs=[pl.BlockSpec(memory_space=pl.ANY), gspec, gspec],
        out_shape=[jax.ShapeDtypeStruct(dproj3.shape, dproj3.dtype), jax.ShapeDtypeStruct((NG, LANE), F32),
                   jax.ShapeDtypeStruct((NG, LANE), F32)],
        input_output_aliases={10: 0},
        scratch_shapes=[pltpu.VMEM((S, LANE), F32)] * 3 + [pltpu.VMEM((NG, S, LANE), F32)] * 2
        + [pltpu.VMEM((S, LANE), F32)] * 5 + [pltpu.VMEM((S, LANE), BF16)] + [pltpu.VMEM((2, S, LANE), BF16)] * 4
        + [pltpu.SemaphoreType.DMA((3, 2)), pltpu.SemaphoreType.DMA((1, 2))],
        name="dil_mixer_bwd",
        compiler_params=pltpu.CompilerParams(dimension_semantics=("arbitrary",) * 3, vmem_limit_bytes=VMEM_LIMIT),
    )(proj3, proj3, proj3, proj3, gq, gk, tab, o_all, l_all, d_yc, dproj3)


def _dil_slopes():
    slopes = (2.0 ** (-8.0 * np.arange(1, NG * DH + 1, dtype=np.float32) / (NG * DH))).astype(np.float32).reshape(NG, DH)
    tab = np.zeros((4, 8, LANE), np.float32)
    for hp in range(4):
        for gi in range(NG):
            for e in (0, 1):
                tab[hp, 2 * gi + e, :] = slopes[gi, 2 * hp + e]
    return jnp.asarray(tab)


def _rope_tables():
    inv = ROPE_THETA ** (-jnp.arange(0, ROPE, 2, dtype=F32) / ROPE)
    ang = jnp.arange(S, dtype=F32)[:, None] * inv[None, :]
    cos, sin = jnp.cos(ang), jnp.sin(ang)
    z16 = jnp.zeros((S, 16), F32)
    c = jnp.concatenate([jnp.ones((S, NOPE), F32), cos, cos, jnp.zeros((S, 32), F32)], axis=1)
    s1 = jnp.concatenate([jnp.zeros((S, NOPE), F32), -sin, z16, jnp.zeros((S, 32), F32)], axis=1)
    s2 = jnp.concatenate([jnp.zeros((S, NOPE), F32), z16, sin, jnp.zeros((S, 32), F32)], axis=1)
    return c, s1, s2


def _pad_heads_uq(w):
    return jnp.pad(w.reshape(QL, NH, QK), ((0, 0), (0, 0), (0, QKP - QK))).reshape(QL, NH * QKP)


def _unpad_heads_uq(g):
    return g.reshape(QL, NH, QKP)[:, :, :QK].reshape(QL, NH * QK)


def _split_ukv(w):
    w3 = w.reshape(KVL, NH, NOPE + VD)
    uk = jnp.pad(w3[:, :, :NOPE], ((0, 0), (0, 0), (0, QKP - NOPE))).reshape(KVL, NH * QKP)
    return uk, w3[:, :, NOPE:].reshape(KVL, NH * VD)


def _join_ukv(guk, guv):
    return jnp.concatenate([guk.reshape(KVL, NH, QKP)[:, :, :NOPE], guv.reshape(KVL, NH, VD)],
                           axis=-1).reshape(KVL, NH * (NOPE + VD))


BR = 512
BRM = 256


def _layer_fwd(x, w, tabs, batch):
    T = batch * S
    rope_c, rope_s1, rope_s2, dil_tab = tabs
    res = {"x": x}
    row = lambda c: (lambda i: (i, c))
    fix = lambda i: (0, 0)

    h = _pcall("norm_fwd", _rms, (T // BR,),
               [(x, (BR, D), row(0)), (w["norm_g"], (1, D), fix)],
               [((T, D), BF16, (BR, D), row(0))])[0]
    proj = _mm("in_proj", h, w["w_in_t"], tb=True, out_dtype=BF16, b_words=True, tm=2048, tn=1024)
    res["h"], res["proj"] = h, proj
    proj3 = proj.reshape(batch, S, NINP)

    cblk = lambda s: (lambda j, b: (b, 0, 4 * s + j))
    y_a = _pcall("conv_fwd", _conv_math, (4, batch),
                 [(proj3, (None, S, LANE), cblk(0)), (proj3, (None, S, LANE), cblk(1)),
                  (proj3, (None, S, LANE), cblk(2)), (proj3, (None, S, LANE), cblk(3)),
                  (w["conv_w"], (3, LANE), lambda j, b: (0, j)), (w["conv_b"], (1, LANE), lambda j, b: (0, j))],
                 [((batch, S, CW), BF16, (None, S, LANE), lambda j, b: (b, 0, j))])[0].reshape(T, CW)
    res["y_a"] = y_a

    cqn, ckvn = _pcall("mla_pre_fwd", _mla_pre_math, (T // BR,),
                       [(proj, (BR, QL), row(O_CQ // QL)), (proj, (BR, KVL), row(O_CKV // KVL)),
                        (w["q_a_norm_g"], (1, QL), fix), (w["kv_a_norm_g"], (1, KVL), fix)],
                       [((T, QL), BF16, (BR, QL), row(0)), ((T, KVL), BF16, (BR, KVL), row(0))])
    w_uq_p = _pad_heads_uq(w["w_uq"])
    w_uk, w_uv = _split_ukv(w["w_ukv"])
    q = _mm("uq", cqn, w_uq_p, out_dtype=BF16)
    kn = _mm("uk", ckvn, w_uk, out_dtype=BF16)
    v = _mm("uv", ckvn, w_uv, out_dtype=BF16)
    nrr = S // BR
    tab_row = lambda i: (i % nrr, 0)
    qr, kr = _pcall("rope_fwd", _rope_math, (T // BR,),
                    [(q, (BR, NH * QKP), row(0)), (kn, (BR, NH * QKP), row(0)), (proj, (BR, LANE), row(O_KPE // LANE)),
                     (w["mla_q_norm_g"], (1, QKP), fix), (w["mla_k_norm_g"], (1, QKP), fix),
                     (rope_c, (BR, QKP), tab_row), (rope_s1, (BR, QKP), tab_row), (rope_s2, (BR, QKP), tab_row)],
                    [((T, NH * QKP), BF16, (BR, NH * QKP), row(0))] * 2)
    qr = qr.reshape(batch, S, NH * QKP)
    kr = kr.reshape(batch, S, NH * QKP)
    v = v.reshape(batch, S, NH * VD)
    o_b, l_b = _mla_fwd(qr, kr, v)
    ob2 = o_b.reshape(T, NH * VD)
    y_b = _pcall("gateb_fwd", _gate_math, (T // BR,),
                 [(ob2, (BR, 512), row(0)), (proj, (BR, 512), row(O_BZ // 512))],
                 [((T, 512), BF16, (BR, 512), row(0))])[0]
    res.update(cqn=cqn, ckvn=ckvn, q=q, kn=kn, qr=qr, kr=kr, v=v, o_b=o_b, l_b=l_b, ob2=ob2, y_b=y_b,
               w_uq_p=w_uq_p, w_uk=w_uk, w_uv=w_uv)

    gq2 = jnp.tile(w["dil_q_norm_g"].reshape(NG, HD), (1, 2))
    gk2 = jnp.tile(w["dil_k_norm_g"].reshape(NG, HD), (1, 2))
    y_c, o_all, l_all = _dilc_fwd(proj3, gq2, gk2, dil_tab)
    y_c = y_c.reshape(T, DWID)
    res.update(o_all=o_all, l_all=l_all, y_c=y_c)

    pa = _mm("out_a", y_a, w["w_out_a"], out_dtype=BF16)
    pb = _mm("out_b", y_b, w["w_out_b"], out_dtype=BF16)
    pc = _mm("out_c", y_c, w["w_out_c"], out_dtype=BF16)
    merged = _pcall("merge_fwd", _merge_math, (T // BRM,),
                    [(proj, (BRM, D), row(O_G // D + s)) for s in range(3)]
                    + [(w["b_gate"], (1, D), (lambda s: (lambda i: (0, s)))(s)) for s in range(3)]
                    + [(t, (BRM, D), row(0)) for t in (pa, pb, pc)],
                    [((T, D), BF16, (BRM, D), row(0))])[0]
    out = _mm("o_proj", merged, w["w_o"], add=x, tm=1024)
    res.update(pa=pa, pb=pb, pc=pc, merged=merged)
    return out, res


def _norm_bwd_math(x, g, dh, dy):
    _, pull = jax.vjp(_rms, x, g)
    dx, dg = pull(dh)
    return dx + dy, dg


def _layer_bwd(dy, w, res, tabs, batch, after_dw=None, after_merge=None):
    T = batch * S
    rope_c, rope_s1, rope_s2, dil_tab = tabs
    row = lambda c: (lambda i: (i, c))
    fix = lambda i: (0, 0)
    x, proj, h = res["x"], res["proj"], res["h"]
    proj3 = proj.reshape(batch, S, NINP)
    g = {}

    d_merged = _mm("o_proj_dx", dy, w["w_o"], tb=True)
    g["w_o"] = _mm("o_proj_dw", res["merged"], dy, ta=True, tm=1024, tk=2048)

    dproj = lax.empty((T, NINP), BF16)
    rows_of = lambda br: (lambda ref, i: ref.at[pl.ds(pl.multiple_of(i * br, br), br)])

    def merge_bwd(*args):
        dg0, dg1, dg2, db0, db1, db2, dpa, dpb, dpc = _vjp_of(_merge_math, 9)(*args, n_prim=9)
        return db0, db1, db2, dpa, dpb, dpc, jnp.concatenate([dg0, dg1, dg2], axis=1)

    db0, db1, db2, dpa, dpb, dpc, dproj = _pcall(
        "merge_bwd", merge_bwd, (T // BRM,),
        [(proj, (BRM, D), row(O_G // D + s)) for s in range(3)]
        + [(w["b_gate"], (1, D), (lambda s: (lambda i: (0, s)))(s)) for s in range(3)]
        + [(t, (BRM, D), row(0)) for t in (res["pa"], res["pb"], res["pc"])]
        + [(d_merged, (BRM, D), row(0))],
        [((1, D), F32, (1, D), fix, True)] * 3 + [((T, D), BF16, (BRM, D), row(0))] * 3,
        into=(dproj, [((BRM, 3 * D), lambda ref, i: rows_of(BRM)(ref, i).at[:, O_G:O_G + 3 * D])]))
    g["b_gate"] = jnp.concatenate([db0, db1, db2], axis=1)

    dep = after_merge(dpa) if after_merge is not None else None
    d_ya = _mm("out_a_dx", dpa, w["w_out_a"], tb=True, dep=dep)
    d_yb = _mm("out_b_dx", dpb, w["w_out_b"], tb=True)
    d_yc = _mm("out_c_dx", dpc, w["w_out_c"], tb=True)
    g["w_out_a"] = _mm("out_a_dw", res["y_a"], dpa, ta=True, tk=T)
    g["w_out_b"] = _mm("out_b_dw", res["y_b"], dpb, ta=True, tk=T)
    g["w_out_c"] = _mm("out_c_dw", res["y_c"], dpc, ta=True, tk=T)

    cblk = lambda s: (lambda j, b: (b, 0, 4 * s + j))
    oblk = lambda j, b: (b, 0, j)
    def conv_bwd(*args):
        d_ab, d_ac, d_ax, d_az, dcw, dcb = _vjp_of(_conv_math, 6)(*args, n_prim=6)
        return dcw, dcb, d_ab, d_ac, d_ax, d_az

    a_col = lambda s_: (lambda ref, j, b: ref.at[b, :, pl.ds(pl.multiple_of(O_A + s_ * CW + j * LANE, LANE), LANE)])
    g["conv_w"], g["conv_b"], dproj3 = _pcall(
        "conv_bwd", conv_bwd, (4, batch),
        [(proj3, (None, S, LANE), cblk(s)) for s in range(4)]
        + [(w["conv_w"], (3, LANE), lambda j, b: (0, j)), (w["conv_b"], (1, LANE), lambda j, b: (0, j)),
           (d_ya.reshape(batch, S, CW), (None, S, LANE), oblk)],
        [((3, CW), F32, (3, LANE), lambda j, b: (0, j), True), ((1, CW), F32, (1, LANE), lambda j, b: (0, j), True)],
        into=(dproj.reshape(batch, S, NINP), [((S, LANE), a_col(s_)) for s_ in range(4)]))
    dproj = dproj3.reshape(T, NINP)

    gate_bwd = functools.partial(_vjp_of(_gate_math, 2), n_prim=2)
    d_ob, dproj = _pcall("gateb_bwd", gate_bwd, (T // BR,),
                         [(res["ob2"], (BR, 512), row(0)), (proj, (BR, 512), row(O_BZ // 512)), (d_yb, (BR, 512), row(0))],
                         [((T, 512), F32, (BR, 512), row(0))],
                         into=(dproj, [((BR, 512), lambda ref, i: rows_of(BR)(ref, i).at[:, O_BZ:O_BZ + 512])]))
    dqr, dkr, dv = _mla_bwd(res["qr"], res["kr"], res["v"], d_ob.reshape(batch, S, NH * VD), res["o_b"], res["l_b"])
    nrr = S // BR
    tab_row = lambda i: (i % nrr, 0)
    def rope_bwd(*args):
        d_q, d_kn, d_kpe, dgq, dgk = _vjp_of(_rope_math, 5)(*args, n_prim=8)
        return d_q, d_kn, dgq, dgk, d_kpe

    d_q, d_kn, g["mla_q_norm_g"], g["mla_k_norm_g"], dproj = _pcall(
        "rope_bwd", rope_bwd, (T // BR,),
        [(res["q"], (BR, NH * QKP), row(0)), (res["kn"], (BR, NH * QKP), row(0)), (proj, (BR, LANE), row(O_KPE // LANE)),
         (w["mla_q_norm_g"], (1, QKP), fix), (w["mla_k_norm_g"], (1, QKP), fix),
         (rope_c, (BR, QKP), tab_row), (rope_s1, (BR, QKP), tab_row), (rope_s2, (BR, QKP), tab_row),
         (dqr.reshape(T, NH * QKP), (BR, NH * QKP), row(0)), (dkr.reshape(T, NH * QKP), (BR, NH * QKP), row(0))],
        [((T, NH * QKP), BF16, (BR, NH * QKP), row(0))] * 2 + [((1, QKP), F32, (1, QKP), fix, True)] * 2,
        into=(dproj, [((BR, LANE), lambda ref, i: rows_of(BR)(ref, i).at[:, O_KPE:O_KPE + LANE])]))
    dv = dv.reshape(T, NH * VD)
    d_cqn = _mm("uq_dx", d_q, res["w_uq_p"], tb=True)
    d_ckvn = _mm("uk_dx", d_kn, res["w_uk"], tb=True)
    d_ckvn = _mm("uv_dx", dv, res["w_uv"], tb=True, add=d_ckvn)
    g["w_uq"] = _unpad_heads_uq(_mm("uq_dw", res["cqn"], d_q, ta=True, tk=T))
    g["w_ukv"] = _join_ukv(_mm("uk_dw", res["ckvn"], d_kn, ta=True, tk=T),
                           _mm("uv_dw", res["ckvn"], dv, ta=True, tk=T))
    def pre_bwd(*args):
        d_cq, d_ckv, dgq, dgkv = _vjp_of(_mla_pre_math, 4)(*args, n_prim=4)
        return dgq, dgkv, jnp.concatenate([d_cq, d_ckv], axis=1)

    g["q_a_norm_g"], g["kv_a_norm_g"], dproj = _pcall(
        "mla_pre_bwd", pre_bwd, (T // BR,),
        [(proj, (BR, QL), row(O_CQ // QL)), (proj, (BR, KVL), row(O_CKV // KVL)),
         (w["q_a_norm_g"], (1, QL), fix), (w["kv_a_norm_g"], (1, KVL), fix),
         (d_cqn, (BR, QL), row(0)), (d_ckvn, (BR, KVL), row(0))],
        [((1, QL), F32, (1, QL), fix, True), ((1, KVL), F32, (1, KVL), fix, True)],
        into=(dproj, [((BR, QL + KVL), lambda ref, i: rows_of(BR)(ref, i).at[:, O_CQ:O_CQ + QL + KVL])]))

    gq2 = jnp.tile(w["dil_q_norm_g"].reshape(NG, HD), (1, 2))
    gk2 = jnp.tile(w["dil_k_norm_g"].reshape(NG, HD), (1, 2))
    dproj3, dgq, dgk = _dilc_bwd(proj3, gq2, gk2, dil_tab, res["o_all"], res["l_all"],
                                 d_yc.reshape(batch, S, DWID), dproj.reshape(batch, S, NINP))
    dproj = dproj3.reshape(T, NINP)
    g["dil_q_norm_g"] = dgq[:, :HD] + dgq[:, HD:]
    g["dil_k_norm_g"] = dgk[:, :HD] + dgk[:, HD:]

    g["w_in_t"] = _mm("in_proj_dw", dproj, h, ta=True, tm=1024, tk=T)
    dep = after_dw(g) if after_dw is not None else None
    d_h = _mm("in_proj_dx", dproj, w["w_in_t"], dep=dep, b_words=True, tm=1024, tk=NINP // 4)
    dx, g["norm_g"] = _pcall("norm_bwd", _norm_bwd_math, (T // BR,),
                             [(x, (BR, D), row(0)), (w["norm_g"], (1, D), fix), (d_h, (BR, D), row(0)),
                              (dy, (BR, D), row(0))],
                             [((T, D), F32, (BR, D), row(0)), ((1, D), F32, (1, D), fix, True)])
    return dx, g


def _loss_math(y, t):
    e = y - t
    return e * (1.0 / D), 0.5 * jnp.sum(jnp.sum(e * e, axis=-1, keepdims=True) / D, axis=0, keepdims=True)


ANY = pl.BlockSpec(memory_space=pl.ANY)
U32 = jnp.uint32
WSH = NIN // 4
WA = KPE_END
WB = WSH - WA
CWD = 512
PACK_ROWS = 1472
HW = PACK_W // 2


def _me():
    return lax.axis_index("x"), lax.axis_index("y"), lax.axis_index("c")


def _piece_rows(k):
    a = k * WSH + jnp.where(k > 0, NINP - NIN, 0)
    b = k * WSH + WA + (NINP - NIN)
    return ((0, pl.multiple_of(a, 8), WA), (WA, pl.multiple_of(b, 8), WB))


def _pack_words(lo, hi):
    ul = lax.bitcast_convert_type(lo.astype(BF16).astype(F32), U32)
    uh = lax.bitcast_convert_type(hi.astype(BF16).astype(F32), U32)
    w = jnp.bitwise_or(jnp.bitwise_and(uh, jnp.uint32(0xFFFF0000)), jnp.right_shift(ul, jnp.uint32(16)))
    return lax.bitcast_convert_type(w, F32)


def _unpack_words(w):
    w = lax.bitcast_convert_type(w, U32)
    lo = lax.bitcast_convert_type(jnp.left_shift(w, jnp.uint32(16)), F32)
    hi = lax.bitcast_convert_type(jnp.bitwise_and(w, jnp.uint32(0xFFFF0000)), F32)
    return lo, hi


def _all_gather(wc, sp):
    def body(w_ref, s_ref, ow_ref, os_ref, send_sems, recv_sems):
        x, y, c = _me()
        k_me = 2 * x + y
        sib = (x, y, 1 - c)
        chips = [(1 - x, y), (x, 1 - y), (1 - x, 1 - y)]
        wcols = lambda cc: pl.ds(pl.multiple_of(cc * (CWD // 2), LANE), CWD // 2)
        scols = lambda cc: pl.ds(pl.multiple_of(cc * HW, LANE), HW)

        def windows(k, cc):
            pcs = _piece_rows(k)
            return ([(w_ref.at[pl.ds(l0, n), wcols(cc)], ow_ref.at[pl.ds(p0, n), wcols(cc)]) for l0, p0, n in pcs]
                    + [(s_ref.at[:, scols(cc)], os_ref.at[k, :, scols(cc)])])

        def copy(i, src, dst, to):
            return pltpu.make_async_remote_copy(src_ref=src, dst_ref=dst, send_sem=send_sems.at[i],
                                                recv_sem=recv_sems.at[i], device_id=to, device_id_type=MESH)

        def own_windows():
            return ([(w_ref.at[pl.ds(l0, n)], ow_ref.at[pl.ds(p0, n)]) for l0, p0, n in _piece_rows(k_me)]
                    + [(s_ref, os_ref.at[k_me])])

        first = [copy(18 + i, src, dst, sib) for i, (src, dst) in enumerate(own_windows())]
        for j, (cx, cy) in enumerate(chips):
            for i, (src, dst) in enumerate(windows(k_me, c)):
                first.append(copy(3 * j + i, src, dst, (cx, cy, c)))
        for cp in first:
            cp.start()
        passed = []
        for j, (cx, cy) in enumerate(chips):
            for i, (_, dst) in enumerate(windows(2 * cx + cy, c)):
                copy(3 * j + i, dst, dst, (cx, cy, c)).wait_recv()
                cp = copy(9 + 3 * j + i, dst, dst, sib)
                cp.start()
                passed.append(cp)
        for j, (cx, cy) in enumerate(chips):
            for i, (_, dst) in enumerate(windows(2 * cx + cy, 1 - c)):
                copy(9 + 3 * j + i, dst, dst, sib).wait_recv()
        for i, (_, dst) in enumerate(own_windows()):
            copy(18 + i, dst, dst, sib).wait_recv()
        for cp in first + passed:
            cp.wait_send()

    return pl.pallas_call(
        body,
        out_shape=[jax.ShapeDtypeStruct((NINP, CWD), F32), jax.ShapeDtypeStruct((4, PACK_ROWS, PACK_W), BF16)],
        in_specs=[ANY, ANY], out_specs=[ANY, ANY],
        scratch_shapes=[pltpu.SemaphoreType.DMA((21,)), pltpu.SemaphoreType.DMA((21,))],
        name="weights_all_gather",
    )(wc, sp)


HBM = pl.BlockSpec(memory_space=pltpu.HBM)
SEM = pl.BlockSpec(memory_space=pltpu.SEMAPHORE)
EFFECT = pltpu.SideEffectType.DATAFLOW_SIDE_EFFECTING


def _in_hbm(a):
    return pltpu.with_memory_space_constraint(a, pltpu.HBM)


def _ag_shard(w_ref, s_ref, lw_ref, ls_ref, k):
    return ([(w_ref.at[pl.ds(l0, n)], lw_ref.at[pl.ds(p0, n)]) for l0, p0, n in _piece_rows(k)]
            + [(s_ref, ls_ref.at[k])])


def _ag_behind_copies(w_ref, s_ref, lw_ref, ls_ref, send_sems, recv_sems):
    x, y, c = _me()
    peers = [(1 - x, y, c), (x, 1 - y, c), (1 - x, 1 - y, c), (x, y, 1 - c)]
    mine, theirs = [], []
    for j, (px, py, pc) in enumerate(peers):
        for i, ((src, dst), (_, got)) in enumerate(zip(_ag_shard(w_ref, s_ref, lw_ref, ls_ref, 2 * x + y),
                                                       _ag_shard(w_ref, s_ref, lw_ref, ls_ref, 2 * px + py))):
            mk = lambda s_, d_: pltpu.make_async_remote_copy(
                src_ref=s_, dst_ref=d_, send_sem=send_sems.at[3 * j + i], recv_sem=recv_sems.at[3 * j + i],
                device_id=(px, py, pc), device_id_type=MESH)
            mine.append(mk(src, dst))
            theirs.append(mk(got, got))
    return mine, theirs


def _ag_behind_start(wc, sp, dep):
    def body(w_ref, s_ref, lw_ref, ls_ref, dep_ref, send_sems, recv_sems, w_thru, s_thru, lw_thru, ls_thru, token):
        mine, _ = _ag_behind_copies(w_ref, s_ref, lw_ref, ls_ref, send_sems, recv_sems)
        for cp in mine:
            cp.start()
        token[...] = jnp.zeros_like(token)

    return pl.pallas_call(
        body, name="weights_gather_start",
        out_shape=(pltpu.SemaphoreType.DMA((12,)), pltpu.SemaphoreType.DMA((12,)), pltpu.HBM(wc.shape, wc.dtype),
                   pltpu.HBM(sp.shape, sp.dtype), pltpu.HBM((NINP, CWD), F32), pltpu.HBM((4, PACK_ROWS, PACK_W), BF16),
                   jax.ShapeDtypeStruct((8, LANE), F32)),
        in_specs=(HBM, HBM, HBM, HBM, ANY),
        out_specs=(SEM, SEM, HBM, HBM, HBM, HBM, pl.BlockSpec(memory_space=pltpu.VMEM)),
        input_output_aliases={0: 2, 1: 3, 2: 4, 3: 5},
        compiler_params=pltpu.CompilerParams(has_side_effects=EFFECT),
    )(_in_hbm(wc), _in_hbm(sp), _in_hbm(lax.empty((NINP, CWD), F32)), _in_hbm(lax.empty((4, PACK_ROWS, PACK_W), BF16)), dep)


def _ag_behind_wait(send_sems, recv_sems, wc, sp, lw, ls, after):
    def body(w_ref, s_ref, lw_ref, ls_ref, send_sems, recv_sems, after_ref, w_dead, s_dead, lw_out, ls_out):
        mine, theirs = _ag_behind_copies(w_ref, s_ref, lw_ref, ls_ref, send_sems, recv_sems)
        for cp in mine:
            cp.wait_send()
        for cp in theirs:
            cp.wait_recv()

    out = pl.pallas_call(
        body, name="weights_gather_wait",
        out_shape=(pltpu.HBM(wc.shape, wc.dtype), pltpu.HBM(sp.shape, sp.dtype), pltpu.HBM(lw.shape, lw.dtype),
                   pltpu.HBM(ls.shape, ls.dtype)),
        in_specs=(HBM, HBM, HBM, HBM, SEM, SEM, ANY), out_specs=(HBM, HBM, HBM, HBM),
        input_output_aliases={0: 0, 1: 1, 2: 2, 3: 3},
        compiler_params=pltpu.CompilerParams(has_side_effects=EFFECT),
    )(wc, sp, lw, ls, send_sems, recv_sems, after)
    return out[2], out[3]


def _rs_swap_copies(w_ref, s_ref, rw_ref, rs_ref, send_sems, recv_sems):
    x, y, c = _me()
    oc = 1 - c
    return [pltpu.make_async_remote_copy(src_ref=w_ref.at[:, pl.ds(pl.multiple_of(oc * (D // 2), LANE), D // 2)],
                                         dst_ref=rw_ref, send_sem=send_sems.at[0], recv_sem=recv_sems.at[0],
                                         device_id=(x, y, oc), device_id_type=MESH),
            pltpu.make_async_remote_copy(src_ref=s_ref.at[:, :, pl.ds(pl.multiple_of(oc * HW, LANE), HW)],
                                         dst_ref=rs_ref, send_sem=send_sems.at[1], recv_sem=recv_sems.at[1],
                                         device_id=(x, y, oc), device_id_type=MESH)]


def _rs_swap_start(gw, gs, tag):
    def body(w_ref, s_ref, rw_ref, rs_ref, send_sems, recv_sems, w_thru, s_thru, rw_thru, rs_thru, token):
        for cp in _rs_swap_copies(w_ref, s_ref, rw_ref, rs_ref, send_sems, recv_sems):
            cp.start()
        token[...] = jnp.zeros_like(token)

    return pl.pallas_call(
        body, name="grads_swap_start_" + tag,
        out_shape=(pltpu.SemaphoreType.DMA((2,)), pltpu.SemaphoreType.DMA((2,)), pltpu.HBM(gw.shape, gw.dtype),
                   pltpu.HBM(gs.shape, gs.dtype), pltpu.HBM((NINP, D // 2), F32), pltpu.HBM((4, PACK_ROWS, HW), F32),
                   jax.ShapeDtypeStruct((8, LANE), F32)),
        in_specs=(HBM, HBM, HBM, HBM),
        out_specs=(SEM, SEM, HBM, HBM, HBM, HBM, pl.BlockSpec(memory_space=pltpu.VMEM)),
        input_output_aliases={0: 2, 1: 3, 2: 4, 3: 5},
        compiler_params=pltpu.CompilerParams(has_side_effects=EFFECT),
    )(_in_hbm(gw), _in_hbm(gs), _in_hbm(lax.empty((NINP, D // 2), F32)), _in_hbm(lax.empty((4, PACK_ROWS, HW), F32)))


def _rs_swap_wait(send_sems, recv_sems, gw, gs, rw, rs, after, tag):
    def body(w_ref, s_ref, rw_ref, rs_ref, send_sems, recv_sems, after_ref, w_out, s_out, rw_out, rs_out):
        for cp in _rs_swap_copies(w_ref, s_ref, rw_ref, rs_ref, send_sems, recv_sems):
            cp.wait()

    return pl.pallas_call(
        body, name="grads_swap_wait_" + tag,
        out_shape=(pltpu.HBM(gw.shape, gw.dtype), pltpu.HBM(gs.shape, gs.dtype), pltpu.HBM(rw.shape, rw.dtype),
                   pltpu.HBM(rs.shape, rs.dtype)),
        in_specs=(HBM, HBM, HBM, HBM, SEM, SEM, ANY), out_specs=(HBM, HBM, HBM, HBM),
        input_output_aliases={0: 0, 1: 1, 2: 2, 3: 3},
        compiler_params=pltpu.CompilerParams(has_side_effects=EFFECT),
    )(gw, gs, rw, rs, send_sems, recv_sems, after)


SUM_BR = 512


def _rs_chip_sum_w(gw, rw, cidx):
    def body(c_ref, g_ref, r_ref, o_ref):
        s = g_ref[...] + r_ref[...]
        q = D // 8
        o_ref[...] = jnp.concatenate([_pack_words(s[:, 0:q], s[:, q:2 * q]),
                                      _pack_words(s[:, 2 * q:3 * q], s[:, 3 * q:4 * q])], axis=1)

    return pl.pallas_call(
        body,
        grid_spec=pltpu.PrefetchScalarGridSpec(
            num_scalar_prefetch=1, grid=(NINP // SUM_BR,),
            in_specs=[pl.BlockSpec((SUM_BR, D // 2), lambda i, cr: (i, cr[0])),
                      pl.BlockSpec((SUM_BR, D // 2), lambda i, cr: (i, 0))],
            out_specs=pl.BlockSpec((SUM_BR, D // 4), lambda i, cr: (i, 0))),
        out_shape=jax.ShapeDtypeStruct((NINP, D // 4), F32),
        name="grads_chip_sum_w",
        compiler_params=pltpu.CompilerParams(dimension_semantics=("arbitrary",), vmem_limit_bytes=VMEM_LIMIT),
    )(cidx, gw, rw)


def _rs_chip_sum_s(gs, rs, cidx):
    def body(c_ref, g_ref, r_ref, o_ref):
        o_ref[...] = (g_ref[...] + r_ref[...]).astype(BF16)

    return pl.pallas_call(
        body,
        grid_spec=pltpu.PrefetchScalarGridSpec(
            num_scalar_prefetch=1, grid=(4,),
            in_specs=[pl.BlockSpec((None, PACK_ROWS, HW), lambda j, cr: (j, 0, cr[0])),
                      pl.BlockSpec((None, PACK_ROWS, HW), lambda j, cr: (j, 0, 0))],
            out_specs=pl.BlockSpec((None, PACK_ROWS, HW), lambda j, cr: (j, 0, 0))),
        out_shape=jax.ShapeDtypeStruct((4, PACK_ROWS, HW), BF16),
        name="grads_chip_sum_s",
        compiler_params=pltpu.CompilerParams(dimension_semantics=("arbitrary",), vmem_limit_bytes=VMEM_LIMIT),
    )(cidx, gs, rs)


def _rs_exchange_copies(sw_ref, ss_ref, r2w_ref, r2s_ref, send_sems, recv_sems):
    x, y, c = _me()
    mine, theirs = [], []
    for j, (cx, cy) in enumerate([(1 - x, y), (x, 1 - y), (1 - x, 1 - y)]):
        def mk(i, src, dst):
            return pltpu.make_async_remote_copy(src_ref=src, dst_ref=dst, send_sem=send_sems.at[3 * j + i],
                                                recv_sem=recv_sems.at[3 * j + i], device_id=(cx, cy, c), device_id_type=MESH)
        for i, (l0, p0, n) in enumerate(_piece_rows(2 * cx + cy)):
            mine.append(mk(i, sw_ref.at[pl.ds(p0, n)], r2w_ref.at[j, pl.ds(l0, n)]))
            theirs.append(mk(i, r2w_ref.at[j, pl.ds(l0, n)], r2w_ref.at[j, pl.ds(l0, n)]))
        mine.append(mk(2, ss_ref.at[2 * cx + cy], r2s_ref.at[j]))
        theirs.append(mk(2, r2s_ref.at[j], r2s_ref.at[j]))
    return mine, theirs


def _rs_exchange_start(sw, ss, tag):
    def body(sw_ref, ss_ref, r2w_ref, r2s_ref, send_sems, recv_sems, sw_thru, ss_thru, r2w_thru, r2s_thru, token):
        mine, _ = _rs_exchange_copies(sw_ref, ss_ref, r2w_ref, r2s_ref, send_sems, recv_sems)
        for cp in mine:
            cp.start()
        token[...] = jnp.zeros_like(token)

    return pl.pallas_call(
        body, name="grads_exchange_start_" + tag,
        out_shape=(pltpu.SemaphoreType.DMA((9,)), pltpu.SemaphoreType.DMA((9,)), pltpu.HBM(sw.shape, sw.dtype),
                   pltpu.HBM(ss.shape, ss.dtype), pltpu.HBM((3, WSH, D // 4), F32), pltpu.HBM((3, PACK_ROWS, HW), BF16),
                   jax.ShapeDtypeStruct((8, LANE), F32)),
        in_specs=(HBM, HBM, HBM, HBM),
        out_specs=(SEM, SEM, HBM, HBM, HBM, HBM, pl.BlockSpec(memory_space=pltpu.VMEM)),
        input_output_aliases={0: 2, 1: 3, 2: 4, 3: 5},
        compiler_params=pltpu.CompilerParams(has_side_effects=EFFECT),
    )(_in_hbm(sw), _in_hbm(ss), _in_hbm(lax.empty((3, WSH, D // 4), F32)), _in_hbm(lax.empty((3, PACK_ROWS, HW), BF16)))


def _rs_exchange_wait(send_sems, recv_sems, sw, ss, r2w, r2s, after, tag):
    def body(sw_ref, ss_ref, r2w_ref, r2s_ref, send_sems, recv_sems, after_ref, sw_dead, ss_dead, r2w_out, r2s_out):
        mine, theirs = _rs_exchange_copies(sw_ref, ss_ref, r2w_ref, r2s_ref, send_sems, recv_sems)
        for cp in mine:
            cp.wait_send()
        for cp in theirs:
            cp.wait_recv()

    out = pl.pallas_call(
        body, name="grads_exchange_wait_" + tag,
        out_shape=(pltpu.HBM(sw.shape, sw.dtype), pltpu.HBM(ss.shape, ss.dtype), pltpu.HBM(r2w.shape, r2w.dtype),
                   pltpu.HBM(r2s.shape, r2s.dtype)),
        in_specs=(HBM, HBM, HBM, HBM, SEM, SEM, ANY), out_specs=(HBM, HBM, HBM, HBM),
        input_output_aliases={0: 0, 1: 1, 2: 2, 3: 3},
        compiler_params=pltpu.CompilerParams(has_side_effects=EFFECT),
    )(sw, ss, r2w, r2s, send_sems, recv_sems, after)
    return out[2], out[3]


def _rs_final_w(gw, rw, r2w, idx):
    q = D // 8

    def body(i_ref, g_ref, r_ref, p_ref, o_ref, gbuf, rbuf, sems):
        i = pl.program_id(0)
        k, c = i_ref[0], i_ref[1]
        cps = []
        for n_, (l0, p0, n) in enumerate(_piece_rows(k)):
            gcol = pl.ds(pl.multiple_of(c * (D // 2) + i * 2 * q, LANE), 2 * q)
            rcol = pl.ds(pl.multiple_of(i * 2 * q, LANE), 2 * q)
            cps.append(pltpu.make_async_copy(g_ref.at[pl.ds(p0, n), gcol], gbuf.at[pl.ds(l0, n)], sems.at[2 * n_]))
            cps.append(pltpu.make_async_copy(r_ref.at[pl.ds(p0, n), rcol], rbuf.at[pl.ds(l0, n)], sems.at[2 * n_ + 1]))
        for cp in cps:
            cp.start()
        for cp in cps:
            cp.wait()
        acc = gbuf[...] + rbuf[...]
        for j in range(3):
            lo, hi = _unpack_words(p_ref[j])
            acc = acc + jnp.concatenate([lo, hi], axis=1)
        o_ref[...] = acc

    return pl.pallas_call(
        body,
        grid_spec=pltpu.PrefetchScalarGridSpec(
            num_scalar_prefetch=1, grid=(2,),
            in_specs=[ANY, ANY, pl.BlockSpec((3, WSH, q), lambda i, ir: (0, 0, i))],
            out_specs=pl.BlockSpec((WSH, 2 * q), lambda i, ir: (0, 2 * ir[1] + i)),
            scratch_shapes=[pltpu.VMEM((WSH, 2 * q), F32), pltpu.VMEM((WSH, 2 * q), F32), pltpu.SemaphoreType.DMA((4,))]),
        out_shape=jax.ShapeDtypeStruct((WSH, D), F32),
        name="grads_final_sum_w",
        compiler_params=pltpu.CompilerParams(dimension_semantics=("arbitrary",), vmem_limit_bytes=VMEM_LIMIT),
    )(idx, gw, rw, r2w)


def _rs_final_s(gs, rs, r2s, idx):
    def body(i_ref, g_ref, r_ref, p_ref, o_ref):
        acc = g_ref[...] + r_ref[...]
        for j in range(3):
            acc = acc + p_ref[j].astype(F32)
        o_ref[...] = acc

    return pl.pallas_call(
        body,
        grid_spec=pltpu.PrefetchScalarGridSpec(
            num_scalar_prefetch=1, grid=(1,),
            in_specs=[pl.BlockSpec((None, PACK_ROWS, HW), lambda i, ir: (ir[0], 0, ir[1])),
                      pl.BlockSpec((None, PACK_ROWS, HW), lambda i, ir: (ir[0], 0, 0)),
                      pl.BlockSpec((3, PACK_ROWS, HW), lambda i, ir: (0, 0, 0))],
            out_specs=pl.BlockSpec((PACK_ROWS, HW), lambda i, ir: (0, ir[1]))),
        out_shape=jax.ShapeDtypeStruct((PACK_ROWS, PACK_W), F32),
        name="grads_final_sum_s",
        compiler_params=pltpu.CompilerParams(dimension_semantics=("arbitrary",), vmem_limit_bytes=VMEM_LIMIT),
    )(idx, gs, rs, r2s)


def _rs_share(fw, fs):
    def body(w_ref, s_ref, ow_ref, os_ref, send_sems, recv_sems):
        x, y, c = _me()
        wcol = lambda cc: pl.ds(pl.multiple_of(cc * (D // 2), LANE), D // 2)
        scol = lambda cc: pl.ds(pl.multiple_of(cc * HW, LANE), HW)

        def copies(cc):
            return [pltpu.make_async_remote_copy(src_ref=w_ref.at[:, wcol(cc)], dst_ref=ow_ref.at[:, wcol(cc)],
                                                 send_sem=send_sems.at[0], recv_sem=recv_sems.at[0],
                                                 device_id=(x, y, 1 - c), device_id_type=MESH),
                    pltpu.make_async_remote_copy(src_ref=s_ref.at[:, scol(cc)], dst_ref=os_ref.at[:, scol(cc)],
                                                 send_sem=send_sems.at[1], recv_sem=recv_sems.at[1],
                                                 device_id=(x, y, 1 - c), device_id_type=MESH)]
        out = copies(c)
        for cp in out:
            cp.start()
        for cp in copies(1 - c):
            cp.wait_recv()
        for cp in out:
            cp.wait_send()

    return pl.pallas_call(
        body,
        out_shape=[jax.ShapeDtypeStruct(fw.shape, F32), jax.ShapeDtypeStruct(fs.shape, F32)],
        in_specs=[ANY, ANY], out_specs=[ANY, ANY],
        input_output_aliases={0: 0, 1: 1},
        scratch_shapes=[pltpu.SemaphoreType.DMA((2,)), pltpu.SemaphoreType.DMA((2,))],
        name="grads_share",
    )(fw, fs)


def _rs_sums(gw, gs, rw, rs):
    x, y, c = _me()
    cidx = jnp.reshape(c, (1,)).astype(jnp.int32)
    return dict(gw=gw, gs=gs, rw=rw, rs=rs, sw=_rs_chip_sum_w(gw, rw, cidx), ss=_rs_chip_sum_s(gs, rs, cidx))


def _rs_end(st, r2w, r2s):
    x, y, c = _me()
    idx = jnp.stack([2 * x + y, c]).astype(jnp.int32)
    return _rs_share(_rs_final_w(st["gw"], st["rw"], r2w, idx), _rs_final_s(st["gs"], st["rs"], r2s, idx))


def _all_reduce_small(gs):
    rows = gs.shape[0]

    def body(g_ref, o_ref, buf, send_sems, recv_sems):
        x, y, c = _me()
        me = 4 * x + 2 * y + c
        buf[me] = g_ref[...]
        cps = []
        for r in range(1, 8):
            fx, fy, fc = (r >> 2) & 1, (r >> 1) & 1, r & 1
            px, py, pc = jnp.bitwise_xor(x, fx), jnp.bitwise_xor(y, fy), jnp.bitwise_xor(c, fc)
            cps.append((pltpu.make_async_remote_copy(
                src_ref=g_ref, dst_ref=buf.at[me], send_sem=send_sems.at[r - 1], recv_sem=recv_sems.at[r - 1],
                device_id=(px, py, pc), device_id_type=MESH), 4 * px + 2 * py + pc))
        for cp, _ in cps:
            cp.start()
        for r, (cp, peer) in enumerate(cps):
            pltpu.make_async_remote_copy(
                src_ref=g_ref, dst_ref=buf.at[peer], send_sem=send_sems.at[r], recv_sem=recv_sems.at[r],
                device_id=(x, y, c), device_id_type=MESH).wait_recv()
        for cp, _ in cps:
            cp.wait_send()
        acc = buf[0]
        for k in range(1, 8):
            acc = acc + buf[k]
        o_ref[...] = acc

    return pl.pallas_call(
        body,
        out_shape=jax.ShapeDtypeStruct((rows, LANE), F32),
        in_specs=[pl.BlockSpec(memory_space=pltpu.VMEM)],
        out_specs=pl.BlockSpec(memory_space=pltpu.VMEM),
        scratch_shapes=[pltpu.VMEM((8, rows, LANE), F32), pltpu.SemaphoreType.DMA((7,)), pltpu.SemaphoreType.DMA((7,))],
        name="small_grads_all_reduce",
    )(gs)


PACK_SPLIT = (("w_uq", 96, (QL, 192)), ("w_ukv", 64, (KVL, 256)),
              ("w_out_a", 256, (CW, 256)), ("w_out_b", 256, (CW, 256)), ("w_out_c", 256, (CW, 256)),
              ("w_o", 512, (256, D)))
MAT_ROWS = 1440
CONV_SHARD = 3 * 128


def _w_in_words(w_in_shard):
    t = w_in_shard.T
    return _pack_words(t[:, :CWD], t[:, CWD:])


def _pack_weights(wl):
    parts = [wl[n].astype(BF16).reshape(-1, PACK_W) for n, _, _ in PACK_SPLIT]
    cw = wl["conv_w"].reshape(-1)
    hi = cw.astype(BF16)
    r1 = cw - hi.astype(F32)
    mid = r1.astype(BF16)
    lo = (r1 - mid.astype(F32)).astype(BF16)
    cterms = jnp.pad(jnp.concatenate([hi, mid, lo]), (0, 3 * PACK_W - 3 * CONV_SHARD)).reshape(3, PACK_W)
    tail = jnp.pad(cterms, ((0, PACK_ROWS - MAT_ROWS - 3), (0, 0)))
    return jnp.concatenate(parts + [tail], axis=0)


def _unpack_weights(gath):
    out = {}
    r = 0
    for n, nrows, shp in PACK_SPLIT:
        t = gath[:, r:r + nrows].reshape((4,) + shp)
        r += nrows
        if n == "w_o":
            out[n] = t.reshape(4 * shp[0], shp[1])
        else:
            out[n] = t.transpose(1, 0, 2).reshape(shp[0], 4 * shp[1])
    ct = gath[:, r:r + 3].reshape(4, 3 * PACK_W)[:, :3 * CONV_SHARD].astype(F32).reshape(4, 3, CONV_SHARD)
    cw = (ct[:, 0] + ct[:, 1]) + ct[:, 2]
    out["conv_w"] = cw.reshape(4, 3, 128).transpose(1, 0, 2).reshape(3, CW)
    return out


def _pack_grads(g):
    parts = []
    for n, nrows, shp in PACK_SPLIT:
        t = g[n]
        if n == "w_o":
            t = t.reshape((4,) + shp)
        else:
            t = t.reshape(shp[0], 4, shp[1]).transpose(1, 0, 2)
        parts.append(t.reshape(4, nrows, PACK_W))
    cw = g["conv_w"].reshape(3, 4, 128).transpose(1, 0, 2).reshape(4, 1, CONV_SHARD)
    parts.append(jnp.pad(cw, ((0, 0), (0, PACK_ROWS - MAT_ROWS - 1), (0, PACK_W - CONV_SHARD))))
    return jnp.concatenate(parts, axis=1)


def _unpack_grads(red):
    out = {}
    r = 0
    for n, nrows, shp in PACK_SPLIT:
        out[n] = red[r:r + nrows].reshape(shp)
        r += nrows
    out["conv_w"] = red[r, :CONV_SHARD].reshape(3, 128)
    return out


SMALL_SIZES = (("norm_g", D), ("b_gate", 3 * D), ("conv_b", CW), ("q_a_norm_g", QL), ("kv_a_norm_g", KVL),
               ("mla_q_norm_g", QK), ("mla_k_norm_g", QK), ("dil_q_norm_g", NG * HD), ("dil_k_norm_g", NG * HD))
SMALL_ROWS = 88


def _pack_small(per_name):
    flat = jnp.concatenate([per_name[n].reshape(-1).astype(F32) for n, _ in SMALL_SIZES])
    return jnp.pad(flat, (0, SMALL_ROWS * LANE - flat.shape[0])).reshape(SMALL_ROWS, LANE)


def _unpack_small(packed, like):
    out = {}
    flat = packed.reshape(-1)
    r = 0
    for n, sz in SMALL_SIZES:
        out[n] = flat[r:r + NL * sz].reshape(like[n].shape)
        r += NL * sz
    return out


def _adamw_math(w, g, m, v):
    m = ADAM_B1 * m + (1.0 - ADAM_B1) * g
    v = ADAM_B2 * v + (1.0 - ADAM_B2) * jnp.square(g)
    m_hat = m / (1.0 - ADAM_B1 ** ADAM_STEP)
    v_hat = v / (1.0 - ADAM_B2 ** ADAM_STEP)
    delta = -ADAM_LR * (m_hat / (jnp.sqrt(v_hat) + ADAM_EPS) + ADAM_WD * w)
    return delta, m, v


def _adamw(name, w, g, m, v, br, bc=None):
    L, R, C = w.shape
    bc = C if bc is None else bc
    blk = lambda l, i, j: (l, i, j)
    return _pcall(name, _adamw_math, (L, R // br, C // bc), [(t, (None, br, bc), blk) for t in (w, g, m, v)],
                  [((L, R, C), F32, (None, br, bc), blk)] * 3)


ADAM_ROWS = {"w_uq": 256, "w_ukv": 128, "w_out_a": 512, "w_out_b": 512, "w_out_c": 512, "w_o": 256,
             "conv_w": 3}


def kernel(x, norm_g, w_in, b_gate, conv_w, conv_b, q_a_norm_g, w_uq, kv_a_norm_g, w_ukv, mla_q_norm_g, mla_k_norm_g, dil_q_norm_g, dil_k_norm_g, w_out_a, w_out_b, w_out_c, w_o, loss_target, m_norm_g, m_w_in, m_b_gate, m_conv_w, m_conv_b, m_q_a_norm_g, m_w_uq, m_kv_a_norm_g, m_w_ukv, m_mla_q_norm_g, m_mla_k_norm_g, m_dil_q_norm_g, m_dil_k_norm_g, m_w_out_a, m_w_out_b, m_w_out_c, m_w_o, v_norm_g, v_w_in, v_b_gate, v_conv_w, v_conv_b, v_q_a_norm_g, v_w_uq, v_kv_a_norm_g, v_w_ukv, v_mla_q_norm_g, v_mla_k_norm_g, v_dil_q_norm_g, v_dil_k_norm_g, v_w_out_a, v_w_out_b, v_w_out_c, v_w_o):
    W = dict(norm_g=norm_g, w_in=w_in, b_gate=b_gate, conv_w=conv_w, conv_b=conv_b, q_a_norm_g=q_a_norm_g, w_uq=w_uq,
             kv_a_norm_g=kv_a_norm_g, w_ukv=w_ukv, mla_q_norm_g=mla_q_norm_g, mla_k_norm_g=mla_k_norm_g,
             dil_q_norm_g=dil_q_norm_g, dil_k_norm_g=dil_k_norm_g, w_out_a=w_out_a, w_out_b=w_out_b, w_out_c=w_out_c,
             w_o=w_o)
    M = dict(norm_g=m_norm_g, w_in=m_w_in, b_gate=m_b_gate, conv_w=m_conv_w, conv_b=m_conv_b, q_a_norm_g=m_q_a_norm_g,
             w_uq=m_w_uq, kv_a_norm_g=m_kv_a_norm_g, w_ukv=m_w_ukv, mla_q_norm_g=m_mla_q_norm_g,
             mla_k_norm_g=m_mla_k_norm_g, dil_q_norm_g=m_dil_q_norm_g, dil_k_norm_g=m_dil_k_norm_g, w_out_a=m_w_out_a,
             w_out_b=m_w_out_b, w_out_c=m_w_out_c, w_o=m_w_o)
    V = dict(norm_g=v_norm_g, w_in=v_w_in, b_gate=v_b_gate, conv_w=v_conv_w, conv_b=v_conv_b, q_a_norm_g=v_q_a_norm_g,
             w_uq=v_w_uq, kv_a_norm_g=v_kv_a_norm_g, w_ukv=v_w_ukv, mla_q_norm_g=v_mla_q_norm_g,
             mla_k_norm_g=v_mla_k_norm_g, dil_q_norm_g=v_dil_q_norm_g, dil_k_norm_g=v_dil_k_norm_g, w_out_a=v_w_out_a,
             w_out_b=v_w_out_b, w_out_c=v_w_out_c, w_o=v_w_o)
    batch = x.shape[0]
    T = batch * S

    def layer_weights(l, cont, gath):
        full = _unpack_weights(gath)
        pad_qk = lambda t: jnp.pad(t, (0, QKP - QK)).reshape(1, QKP)
        full.update(
            w_in_t=cont,
            norm_g=norm_g[l].reshape(1, D), b_gate=b_gate[l].reshape(1, 3 * D), conv_b=conv_b[l].reshape(1, CW),
            q_a_norm_g=q_a_norm_g[l].reshape(1, QL), kv_a_norm_g=kv_a_norm_g[l].reshape(1, KVL),
            mla_q_norm_g=pad_qk(mla_q_norm_g[l]), mla_k_norm_g=pad_qk(mla_k_norm_g[l]),
            dil_q_norm_g=dil_q_norm_g[l].reshape(NG, 1, HD), dil_k_norm_g=dil_k_norm_g[l].reshape(NG, 1, HD))
        return full

    words = [_w_in_words(w_in[l]) for l in range(NL)]
    packs = [_pack_weights({n: W[n][l] for n in BIG[1:] + ("conv_w",)}) for l in range(NL)]
    tabs = _rope_tables() + (_dil_slopes(),)
    x2 = x.reshape(T, D)

    cont0, gath0 = _all_gather(words[0], packs[0])
    w0 = layer_weights(0, cont0, gath0)
    ag = _ag_behind_start(words[1], packs[1], gath0)
    w0["norm_g"] = w0["norm_g"] + ag[6][0:1, 0:1]
    y0, res0 = _layer_fwd(x2, w0, tabs, batch)
    w1 = layer_weights(1, *_ag_behind_wait(ag[0], ag[1], ag[2], ag[3], ag[4], ag[5], y0))
    y1, res1 = _layer_fwd(y0, w1, tabs, batch)

    row = lambda i: (i, 0)
    dy, loss = _pcall("loss", _loss_math, (T // BR,),
                      [(y1, (BR, D), row), (loss_target.reshape(T, D), (BR, D), row)],
                      [((T, D), F32, (BR, D), row), ((1, 1), F32, (1, 1), lambda i: (0, 0), True)])
    loss = lax.psum(loss[0, 0], ("x", "y", "c"))

    grads = [None] * NL
    dy, grads[1] = _layer_bwd(dy, w1, res1, tabs, batch)
    st = [None] * NL
    ex = [None] * NL
    sw1 = _rs_swap_start(grads[1]["w_in_t"], _pack_grads(grads[1]), "1")
    w0["w_o"] = w0["w_o"] + sw1[6][0:1, 0:1].astype(BF16)

    def exchange_layer1(t):
        st[1] = _rs_sums(*_rs_swap_wait(*sw1[:6], t, "1"))
        ex[1] = _rs_exchange_start(st[1]["sw"], st[1]["ss"], "1")
        return ex[1][6]

    red = [None] * NL

    def finish(l, after):
        r2w, r2s = _rs_exchange_wait(*ex[l][:6], after, str(l))
        rw, rs = _rs_end(st[l], r2w, r2s)
        red[l] = dict(_unpack_grads(rs), w_in_t=rw)
        return rw

    def start_layer0(g):
        sw0 = _rs_swap_start(g["w_in_t"], _pack_grads(g), "0")
        done1 = finish(1, sw0[6])
        st[0] = _rs_sums(*_rs_swap_wait(*sw0[:6], done1, "0"))
        ex[0] = _rs_exchange_start(st[0]["sw"], st[0]["ss"], "0")
        return ex[0][6]

    dx, grads[0] = _layer_bwd(dy, w0, res0, tabs, batch, after_dw=start_layer0, after_merge=exchange_layer1)
    grad_x = dx.reshape(batch, S, D)
    finish(0, dx)

    G = {n: jnp.stack([red[l][n] for l in range(NL)]) for n in BIG[1:] + ("conv_w",)}
    g_in_t = jnp.stack([red[l]["w_in_t"] for l in range(NL)])
    G["w_in"] = jnp.swapaxes(g_in_t, 1, 2)
    small_g = {n: jnp.stack([grads[l][n].reshape(-1)[:sz] for l in range(NL)]) for n, sz in SMALL_SIZES}
    small_red = _all_reduce_small(_pack_small(small_g))
    G.update(_unpack_small(small_red, {n: W[n] for n in SMALL}))

    delta, new_m, new_v = {}, {}, {}
    for n in BIG[1:] + ("conv_w",):
        delta[n], new_m[n], new_v[n] = _adamw("adamw_" + n, W[n], G[n], M[n], V[n], ADAM_ROWS[n])
    tr = lambda t: jnp.swapaxes(t, 1, 2)
    delta["w_in"], new_m["w_in"], new_v["w_in"] = (
        tr(t) for t in _adamw("adamw_w_in", tr(w_in), g_in_t, tr(m_w_in), tr(v_w_in), WSH, LANE))
    sw, sm, sv = (_pack_small({n: t[n] for n in SMALL})[None] for t in (W, M, V))
    sd, snm, snv = _adamw("adamw_small", sw, small_red[None], sm, sv, SMALL_ROWS)
    like = {n: W[n] for n in SMALL}
    delta.update(_unpack_small(sd[0], like))
    new_m.update(_unpack_small(snm[0], like))
    new_v.update(_unpack_small(snv[0], like))

    return (loss, grad_x, *[G[n] for n in WEIGHTS], *[delta[n] for n in WEIGHTS],
            *[new_m[n] for n in WEIGHTS], *[new_v[n] for n in WEIGHTS])
```

```python
import functools

import numpy as np
import jax
import jax.numpy as jnp
from jax import lax
from jax.experimental import pallas as pl
from jax.experimental.pallas import tpu as pltpu

F32 = jnp.float32
BF16 = jnp.bfloat16

D = 1024
S = 2048
NL = 2
CW = 512
NH = 8
QL = 256
KVL = 128
NOPE = 64
ROPE = 32
VD = 64
QK = NOPE + ROPE
QKP = 128
ROPE_THETA = 10000.0
DIL = ((128, 1), (512, 4), (2048, 16))
NG = 3
DH = 8
HD = 64
DWID = DH * HD
QB = 128
EPS = 1e-6
NIN = 11168
NINP = 11264
O_A, O_CQ, O_CKV, O_KPE, O_BZ, O_DQ, O_DK, O_DV, O_CZ, O_G = 0, 2048, 2304, 2432, 2560, 3072, 4608, 6144, 7680, 8192
KPE_END = 2464
NEG = -1e30
MLA_SCALE = QK ** -0.5
DIL_SCALE = HD ** -0.5
LANE = 128
PACK_W = 512
VMEM_LIMIT = 48 * 1024 * 1024

ADAM_LR = 0.001
ADAM_B1 = 0.9
ADAM_B2 = 0.999
ADAM_EPS = 1e-08
ADAM_WD = 0.01
ADAM_STEP = 10

MESH = pl.DeviceIdType.MESH
BIG = ("w_in", "w_uq", "w_ukv", "w_out_a", "w_out_b", "w_out_c", "w_o")
SMALL = ("norm_g", "b_gate", "conv_b", "q_a_norm_g", "kv_a_norm_g", "mla_q_norm_g", "mla_k_norm_g",
         "dil_q_norm_g", "dil_k_norm_g")
WEIGHTS = ("norm_g", "w_in", "b_gate", "conv_w", "conv_b", "q_a_norm_g", "w_uq", "kv_a_norm_g", "w_ukv",
           "mla_q_norm_g", "mla_k_norm_g", "dil_q_norm_g", "dil_k_norm_g", "w_out_a", "w_out_b", "w_out_c", "w_o")


def _dot(a, b):
    return jnp.dot(a, b, preferred_element_type=F32)


def _dot_nt(a, b):
    return lax.dot_general(a, b, (((1,), (1,)), ((), ())), preferred_element_type=F32)


def _dot_tn(a, b):
    return lax.dot_general(a, b, (((0,), (0,)), ((), ())), preferred_element_type=F32)


def _grid_step(grid):
    step = pl.program_id(0)
    for a in range(1, len(grid)):
        step = step * grid[a] + pl.program_id(a)
    n = 1
    for g in grid:
        n *= g
    return step, n


def _write_windows(buf_ref, stages, sems, step, nsteps, puts):
    slot = step % 2
    for t, (v, dst) in enumerate(puts):
        cp = pltpu.make_async_copy(stages[t].at[slot], dst, sems.at[t, slot])

        @pl.when(step >= 2)
        def _():
            cp.wait()

        stages[t][slot] = v.astype(stages[t].dtype).reshape(stages[t].shape[1:])
        cp.start()

    @pl.when(step == nsteps - 1)
    def _():
        for t, (v, dst) in enumerate(puts):
            pltpu.make_async_copy(stages[t].at[slot], dst, sems.at[t, slot]).wait()
            if nsteps > 1:
                pltpu.make_async_copy(stages[t].at[1 - slot], dst, sems.at[t, 1 - slot]).wait()


def _pcall(name, fn, grid, ins, outs, into=None):
    n_in = len(ins)
    n_out = len(outs)
    acc_axis = len(grid) - 1
    is_acc = [len(o) > 4 and o[4] for o in outs]
    outs = [o[:4] for o in outs]
    targets = into[1] if into is not None else []
    n_t = len(targets)

    def body(*refs):
        vals = fn(*[r[...].astype(F32) for r in refs[:n_in]])
        if not isinstance(vals, (tuple, list)):
            vals = (vals,)
        o0 = n_in + (1 if n_t else 0)
        for k in range(n_out):
            r = refs[o0 + k]
            v = vals[k].astype(r.dtype).reshape(r.shape)
            if is_acc[k]:
                first = pl.program_id(acc_axis) == 0

                @pl.when(first)
                def _():
                    r[...] = v

                @pl.when(jnp.logical_not(first))
                def _():
                    r[...] += v
            else:
                r[...] = v
        if n_t:
            buf_ref = refs[o0 + n_out]
            stages = refs[o0 + n_out + 1:o0 + n_out + 1 + n_t]
            ids = [pl.program_id(a) for a in range(len(grid))]
            step, nsteps = _grid_step(grid)
            _write_windows(buf_ref, stages, refs[-1], step, nsteps,
                           [(vals[n_out + t], targets[t][1](buf_ref, *ids)) for t in range(n_t)])

    in_specs = [pl.BlockSpec(bs, im) for _, bs, im in ins]
    out_specs = [pl.BlockSpec(bs, im) for _, _, bs, im in outs]
    out_shape = [jax.ShapeDtypeStruct(sh, dt) for sh, dt, _, _ in outs]
    args = [a for a, _, _ in ins]
    extra = {}
    if n_t:
        buf = into[0]
        in_specs.append(pl.BlockSpec(memory_space=pl.ANY))
        out_specs.append(pl.BlockSpec(memory_space=pl.ANY))
        out_shape.append(jax.ShapeDtypeStruct(buf.shape, buf.dtype))
        args.append(buf)
        extra = dict(input_output_aliases={n_in: n_out},
                     scratch_shapes=[pltpu.VMEM((2,) + tuple(bs), buf.dtype) for bs, _ in targets]
                     + [pltpu.SemaphoreType.DMA((n_t, 2))])
    return pl.pallas_call(
        body,
        grid=grid,
        in_specs=in_specs,
        out_specs=out_specs,
        out_shape=out_shape,
        name=name,
        compiler_params=pltpu.CompilerParams(
            dimension_semantics=("arbitrary",) * len(grid), vmem_limit_bytes=VMEM_LIMIT),
        **extra,
    )(*args)


def _mm(name, a, b, *, ta=False, tb=False, out_dtype=F32, add=None, dep=None, b_words=False, tm=2048, tn=1024, tk=1024):
    if ta:
        K, M = a.shape
    else:
        M, K = a.shape
    bshape = (b.shape[0], 2 * b.shape[1]) if b_words else b.shape
    if tb:
        N, K2 = bshape
    else:
        K2, N = bshape
    assert K == K2, (name, a.shape, b.shape)
    tm, tn, tk = min(tm, M), min(tn, N), min(tk, K)
    assert M % tm == 0 and N % tn == 0 and K % tk == 0, (name, M, N, K)
    nk = K // tk
    dims = (((0 if ta else 1,), (1 if tb else 0,)), ((), ()))
    a_spec = pl.BlockSpec((tk, tm), lambda j, i, k: (k, i)) if ta else pl.BlockSpec((tm, tk), lambda j, i, k: (i, k))
    bw = 2 if b_words else 1
    assert not b_words or (tk if tb else tn) == bshape[1]
    b_spec = (pl.BlockSpec((tn, tk // bw), lambda j, i, k: (j, k)) if tb
              else pl.BlockSpec((tk, tn // bw), lambda j, i, k: (k, j)))
    o_spec = pl.BlockSpec((tm, tn), lambda j, i, k: (i, j))
    has_add = add is not None
    n_in = 2 + has_add + (dep is not None)

    def body(*refs):
        a_ref, b_ref = refs[0], refs[1]
        add_ref = refs[2] if has_add else None
        o_ref = refs[n_in]
        bb = b_ref[...]
        if b_words:
            lo, hi = _unpack_words(bb)
            first = (pl.program_id(0) * tn) if tb else (pl.program_id(2) * tk)
            r = first + lax.broadcasted_iota(jnp.int32, lo.shape, 0)
            pad = jnp.logical_and(r >= KPE_END, r < KPE_END + NINP - NIN)
            bb = jnp.concatenate([jnp.where(pad, 0.0, lo), jnp.where(pad, 0.0, hi)], axis=1)
        p = lax.dot_general(a_ref[...].astype(BF16), bb.astype(BF16), dims, preferred_element_type=F32)
        if nk == 1:
            if has_add:
                p = p + add_ref[...]
            o_ref[...] = p.astype(out_dtype)
        else:
            acc = refs[-1]
            k = pl.program_id(2)

            @pl.when(k == 0)
            def _():
                acc[...] = p

            @pl.when(k > 0)
            def _():
                acc[...] += p

            @pl.when(k == nk - 1)
            def _():
                r = acc[...]
                if has_add:
                    r = r + add_ref[...]
                o_ref[...] = r.astype(out_dtype)

    in_specs = [a_spec, b_spec] + ([o_spec] if has_add else []) + ([pl.BlockSpec(memory_space=pl.ANY)] if dep is not None else [])
    args = [a, b] + ([add] if has_add else []) + ([dep] if dep is not None else [])
    return pl.pallas_call(
        body,
        grid=(N // tn, M // tm, nk),
        in_specs=in_specs,
        out_specs=o_spec,
        out_shape=jax.ShapeDtypeStruct((M, N), out_dtype),
        scratch_shapes=[pltpu.VMEM((tm, tn), F32)] if nk > 1 else [],
        name=name,
        compiler_params=pltpu.CompilerParams(
            dimension_semantics=("arbitrary", "arbitrary", "arbitrary"), vmem_limit_bytes=VMEM_LIMIT),
    )(*args)


def _vjp_of(f, n_diff):
    def g(*args, n_prim):
        prim = args[:n_diff]
        consts = args[n_diff:n_prim]
        cts = args[n_prim:]
        _, pull = jax.vjp(lambda *p: f(*p, *consts), *prim)
        out = jax.eval_shape(lambda *p: f(*p, *consts), *prim)
        if isinstance(out, (tuple, list)):
            cts = tuple(c.astype(o.dtype) for c, o in zip(cts, out))
        else:
            cts = cts[0].astype(out.dtype)
        return pull(cts)
    return g


def _rms(x, g, n=None):
    n = x.shape[-1] if n is None else n
    ms = jnp.sum(x * x, axis=-1, keepdims=True) / n
    return x * lax.rsqrt(ms + EPS) * g


def _silu(z):
    return z * jax.nn.sigmoid(z)


def _roll_rows(u, k):
    n = u.shape[0]
    r = pltpu.roll(u, k % n, 0)
    t = lax.broadcasted_iota(jnp.int32, u.shape, 0)
    if k > 0:
        return jnp.where(t >= k, r, 0.0)
    return jnp.where(t < n + k, r, 0.0)


@functools.partial(jax.custom_vjp, nondiff_argnums=(1,))
def _shift(u, k):
    return _roll_rows(u, k)


def _shift_fwd(u, k):
    return _roll_rows(u, k), None


def _shift_bwd(k, _, g):
    return (_roll_rows(g, -k),)


_shift.defvjp(_shift_fwd, _shift_bwd)


@functools.partial(jax.custom_vjp, nondiff_argnums=(1,))
def _lane_roll(u, k):
    return pltpu.roll(u, k % LANE, 1)


def _lane_roll_fwd(u, k):
    return pltpu.roll(u, k % LANE, 1), None


def _lane_roll_bwd(k, _, g):
    return (pltpu.roll(g, (-k) % LANE, 1),)


_lane_roll.defvjp(_lane_roll_fwd, _lane_roll_bwd)


def _conv_math(ab, ac, ax, az, cw, cb):
    u = ac * ax
    conv = cb + _shift(u, 2) * cw[0:1] + _shift(u, 1) * cw[1:2] + u * cw[2:3]
    return ab * conv * _silu(az)


def _mla_pre_math(cq, ckv, gq, gkv):
    return _rms(cq, gq), _rms(ckv, gkv)


def _rope_math(q, kn, kpe, gq, gk, c, s1, s2):
    lane = lax.broadcasted_iota(jnp.int32, kpe.shape, 1)
    pe = _lane_roll(jnp.where(lane < ROPE, kpe, 0.0), NOPE)

    def one(t, g):
        tn = _rms(t, g, QK)
        return tn * c + _lane_roll(tn, -16) * s1 + _lane_roll(tn, 16) * s2

    qs, ks = [], []
    for h in range(NH):
        sl = slice(h * QKP, (h + 1) * QKP)
        qs.append(one(q[:, sl], gq))
        ks.append(one(kn[:, sl] + pe, gk))
    return jnp.concatenate(qs, axis=1), jnp.concatenate(ks, axis=1)


def _gate_math(o, z):
    return o * _silu(z)


def _mergec_math(o0, o1, o2, l0, l1, l2, cz):
    m = lax.stop_gradient(jnp.maximum(jnp.maximum(l0, l1), l2))
    e0, e1, e2 = jnp.exp(l0 - m), jnp.exp(l1 - m), jnp.exp(l2 - m)
    den = e0 + e1 + e2
    oc = (e0 / den) * o0 + (e1 / den) * o1 + (e2 / den) * o2
    return oc * _silu(cz)


def _merge_math(g0, g1, g2, b0, b1, b2, pa, pb, pc):
    return (jax.nn.sigmoid(g0 + b0) * pa + jax.nn.sigmoid(g1 + b1) * pb) + jax.nn.sigmoid(g2 + b2) * pc


MLA_T = 256
MLA_UNROLL = True


def _mla_fwd(q, k, v):
    B = q.shape[0]
    T = MLA_T
    NB = S // T

    def body(q_ref, k_ref, v_ref, o_ref, l_ref):
        row = lax.broadcasted_iota(jnp.int32, (T, T), 0)
        col = lax.broadcasted_iota(jnp.int32, (T, T), 1)
        lo = _lo_mask((T, LANE))

        for qi in range(NB):
            qb = q_ref[qi * T:(qi + 1) * T, :]

            def step(j, carry, diagonal):
                m, l, acc = carry
                off = pl.multiple_of(j * T, T)
                kb = k_ref[pl.ds(off, T), :]
                vb = v_ref[pl.ds(off, T), :]
                ss = []
                for e in (0, 1):
                    se = _dot_nt(qb[:, e * QKP:(e + 1) * QKP], kb[:, e * QKP:(e + 1) * QKP]) * MLA_SCALE
                    ss.append(jnp.where(col <= row, se, NEG) if diagonal else se)
                s = jnp.concatenate(ss, axis=0)
                m_new = jnp.maximum(m, jnp.max(s, axis=-1, keepdims=True))
                a = jnp.exp(m - m_new)
                p = jnp.exp(s - m_new)
                l = a * l + jnp.sum(p, axis=-1, keepdims=True)
                acc = a * acc + _dot(p.astype(BF16), vb)
                return m_new, l, acc

            init = (jnp.full((2 * T, 1), NEG, F32), jnp.zeros((2 * T, 1), F32), jnp.zeros((2 * T, LANE), F32))
            carry = lax.fori_loop(0, qi, functools.partial(step, diagonal=False), init, unroll=MLA_UNROLL)
            m, l, acc = step(qi, carry, True)
            o = acc / l
            lse = m + jnp.log(l)
            o_ref[qi * T:(qi + 1) * T, :] = jnp.where(lo, o[:T], o[T:])
            l_ref[qi * T:(qi + 1) * T, :] = jnp.where(lo, lse[:T], lse[T:])

    def spec(w):
        return pl.BlockSpec((None, S, w), lambda b, hp: (b, 0, hp))

    return pl.pallas_call(
        body,
        grid=(B, NH // 2),
        in_specs=[spec(2 * QKP), spec(2 * QKP), spec(LANE)],
        out_specs=[spec(LANE), spec(LANE)],
        out_shape=[jax.ShapeDtypeStruct((B, S, NH * VD), F32)] * 2,
        name="mla_attn_fwd",
        compiler_params=pltpu.CompilerParams(dimension_semantics=("arbitrary",) * 2, vmem_limit_bytes=VMEM_LIMIT),
    )(q, k, v)


def _mla_bwd(q, k, v, do, o, lse):
    B = q.shape[0]
    T = MLA_T
    NB = S // T

    def body(q_ref, k_ref, v_ref, do_ref, o_ref, l_ref, dq_ref, dk_ref, dv_ref, delta_ref, dqt_ref):
        delta_ref[...] = _head_sum(do_ref[...] * o_ref[...])
        row = lax.broadcasted_iota(jnp.int32, (T, T), 0)
        col = lax.broadcasted_iota(jnp.int32, (T, T), 1)
        lo = _lo_mask((T, LANE))
        tn_t = (((0,), (1,)), ((), ()))

        for j in range(NB):
            krows = slice(j * T, (j + 1) * T)
            kb = k_ref[krows, :]
            vb = v_ref[krows, :]
            dkt = [jnp.zeros((QKP, T), F32), jnp.zeros((QKP, T), F32)]
            dvt = jnp.zeros((LANE, T), F32)
            for i in range(j, NB):
                qrows = slice(i * T, (i + 1) * T)
                qb = q_ref[qrows, :]
                do2 = _stack_heads(do_ref[qrows, :], lo).astype(BF16)
                lb = l_ref[qrows, :]
                db = delta_ref[qrows, :]
                dp2 = _dot_nt(do2, vb)
                ps = []
                for e in (0, 1):
                    cols = slice(e * QKP, (e + 1) * QKP)
                    qe, ke = qb[:, cols], kb[:, cols]
                    s = _dot_nt(qe, ke) * MLA_SCALE
                    if i == j:
                        s = jnp.where(col <= row, s, NEG)
                    p = jnp.exp(s - lb[:, e * HD:e * HD + 1])
                    ps.append(p.astype(BF16))
                    ds = (p * (dp2[e * T:(e + 1) * T] - db[:, e * HD:e * HD + 1]) * MLA_SCALE).astype(BF16)
                    dkt[e] = dkt[e] + _dot_tn(qe, ds)
                    dq_t = lax.dot_general(ke, ds, tn_t, preferred_element_type=F32)
                    if j == 0:
                        dqt_ref[e, :, qrows] = dq_t
                    else:
                        dqt_ref[e, :, qrows] += dq_t
                dvt = dvt + _dot_tn(do2, jnp.concatenate(ps, axis=0))
            dk_ref[krows, 0:QKP] = dkt[0].T
            dk_ref[krows, QKP:2 * QKP] = dkt[1].T
            dv_ref[krows, :] = dvt.T
        dq_ref[:, 0:QKP] = dqt_ref[0].T
        dq_ref[:, QKP:2 * QKP] = dqt_ref[1].T

    def spec(w):
        return pl.BlockSpec((None, S, w), lambda b, hp: (b, 0, hp))

    return pl.pallas_call(
        body,
        grid=(B, NH // 2),
        in_specs=[spec(2 * QKP), spec(2 * QKP), spec(LANE), spec(LANE), spec(LANE), spec(LANE)],
        out_specs=[spec(2 * QKP), spec(2 * QKP), spec(LANE)],
        out_shape=[jax.ShapeDtypeStruct((B, S, NH * QKP), F32), jax.ShapeDtypeStruct((B, S, NH * QKP), F32),
                   jax.ShapeDtypeStruct((B, S, NH * VD), F32)],
        scratch_shapes=[pltpu.VMEM((S, LANE), F32), pltpu.VMEM((2, QKP, S), F32)],
        name="mla_attn_bwd",
        compiler_params=pltpu.CompilerParams(dimension_semantics=("arbitrary",) * 2, vmem_limit_bytes=VMEM_LIMIT),
    )(q, k, v, do, o, lse)


def _lo_mask(shape):
    return lax.broadcasted_iota(jnp.int32, shape, len(shape) - 1) < HD


def _head_sum(u):
    r = lax.broadcasted_iota(jnp.int32, (LANE, LANE), 0) < HD
    c = lax.broadcasted_iota(jnp.int32, (LANE, LANE), 1) < HD
    ones = jnp.where(r == c, 1.0, 0.0).astype(BF16)
    hi = u.astype(BF16)
    lo = (u - hi.astype(F32)).astype(BF16)
    return _dot(hi, ones) + _dot(lo, ones)


def _head_sum_1(u):
    r = lax.broadcasted_iota(jnp.int32, (LANE, LANE), 0) < HD
    c = lax.broadcasted_iota(jnp.int32, (LANE, LANE), 1) < HD
    return _dot(u.astype(BF16), jnp.where(r == c, 1.0, 0.0).astype(BF16))


def _rms2_scale(x):
    return lax.rsqrt(_head_sum(x * x) / HD + EPS)


def _rms2(x, g):
    return x * _rms2_scale(x) * g


def _rms2_bwd(x, r, g, dy):
    xn = x * r
    t = dy * g
    dx = r * (t - xn * (_head_sum_1(xn * t) * (1.0 / HD)))
    return dx, jnp.sum(dy * xn, axis=0, keepdims=True)


def _dil_bias(t_ref, gi, d):
    qq = lax.broadcasted_iota(jnp.int32, (QB, QB), 0)
    kk = lax.broadcasted_iota(jnp.int32, (QB, QB), 1)
    jc = (qq - kk).astype(F32)
    rows = []
    for e in (0, 1):
        sl = t_ref[2 * gi + e:2 * gi + e + 1, :] * float(d)
        bp = jnp.where(kk >= qq, -sl * (jc + float(QB)), NEG)
        bc = jnp.where(kk <= qq, -sl * jc, NEG)
        rows.append(jnp.concatenate([bp, bc], axis=1))
    return jnp.concatenate(rows, axis=0)


def _dil_rows(cur, d):
    return pl.ds(cur, QB, stride=d) if d > 1 else pl.ds(pl.multiple_of(cur, QB), QB)


def _dil_walk(d, block, full):
    if d == 1:
        block(0, None)

        def body(i, c):
            block(i * QB, (i - 1) * QB)
            return c
        lax.fori_loop(1, S // QB, body, 0, unroll=True if full else 5)
    elif d == 16:
        def body(r, c):
            block(r, None)
            return c
        lax.fori_loop(0, d, body, 0, unroll=True if full else 4)
    else:
        nb = S // d // QB

        def cls(r, c):
            block(r, None)

            def body(i, c2):
                block(r + i * QB * d, r + (i - 1) * QB * d)
                return c2
            lax.fori_loop(1, nb, body, 0, unroll=True)
            return c
        lax.fori_loop(0, d, cls, 0, unroll=full)


def _stack_heads(x, lo):
    return jnp.concatenate([jnp.where(lo, x, 0.0), jnp.where(lo, 0.0, x)], axis=0)


def _dilc_fwd(proj3, gq, gk, tab):
    B = proj3.shape[0]

    def body(q_ref, k_ref, v_ref, cz_ref, gq_ref, gk_ref, t_ref, y_ref, o_ref, l_ref, qs, ks, vs):
        g = pl.program_id(2)
        lo = _lo_mask((QB, LANE))

        def group(gi):
            d = DIL[gi][1]
            qs[...] = _rms2(q_ref[...].astype(F32), gq_ref[gi:gi + 1, :])
            ks[...] = _rms2(k_ref[...].astype(F32), gk_ref[gi:gi + 1, :])
            vs[...] = v_ref[...].astype(F32)
            bias = _dil_bias(t_ref, gi, d)

            def block(cur, prev):
                rows = _dil_rows(cur, d)
                q2 = _stack_heads(qs[rows, :], lo).astype(BF16)
                kc, vc = ks[rows, :], vs[rows, :]
                if prev is None:
                    kcat, vcat, b = kc, vc, bias[:, QB:]
                else:
                    prow = _dil_rows(prev, d)
                    kcat = jnp.concatenate([ks[prow, :], kc], axis=0)
                    vcat = jnp.concatenate([vs[prow, :], vc], axis=0)
                    b = bias
                s = _dot_nt(q2, kcat.astype(BF16)) * DIL_SCALE + b
                m = jnp.max(s, axis=-1, keepdims=True)
                p = jnp.exp(s - m)
                l = jnp.sum(p, axis=-1, keepdims=True)
                o = _dot(p.astype(BF16), vcat.astype(BF16)) / l
                lse = m + jnp.log(l)
                o_ref[gi, rows, :] = jnp.where(lo, o[:QB], o[QB:])
                l_ref[gi, rows, :] = jnp.where(lo, lse[:QB], lse[QB:])

            _dil_walk(d, block, True)

        for gi in range(NG):
            pl.when(g == gi)(functools.partial(group, gi))

        @pl.when(g == NG - 1)
        def _():
            y_ref[...] = _mergec_math(o_ref[0], o_ref[1], o_ref[2], l_ref[0], l_ref[1], l_ref[2],
                                      cz_ref[...].astype(F32)).astype(BF16)

    def col(base):
        return pl.BlockSpec((None, S, LANE), lambda b, hp, g: (b, 0, base // LANE + 4 * g + hp))

    gspec = pl.BlockSpec((NG, LANE), lambda b, hp, g: (0, 0))
    saved = pl.BlockSpec((NG, None, S, LANE), lambda b, hp, g: (0, b, 0, hp))
    return pl.pallas_call(
        body,
        grid=(B, 4, NG),
        in_specs=[col(O_DQ), col(O_DK), col(O_DV),
                  pl.BlockSpec((None, S, LANE), lambda b, hp, g: (b, 0, O_CZ // LANE + hp)),
                  gspec, gspec, pl.BlockSpec((None, 8, LANE), lambda b, hp, g: (hp, 0, 0))],
        out_specs=[pl.BlockSpec((None, S, LANE), lambda b, hp, g: (b, 0, hp)), saved, saved],
        out_shape=[jax.ShapeDtypeStruct((B, S, DWID), BF16), jax.ShapeDtypeStruct((NG, B, S, DWID), F32),
                   jax.ShapeDtypeStruct((NG, B, S, DWID), F32)],
        scratch_shapes=[pltpu.VMEM((S, LANE), F32)] * 3,
        name="dil_mixer_fwd",
        compiler_params=pltpu.CompilerParams(dimension_semantics=("arbitrary",) * 3, vmem_limit_bytes=VMEM_LIMIT),
    )(proj3, proj3, proj3, proj3, gq, gk, tab)


MERGE_ROWS = 256


def _dilc_bwd(proj3, gq, gk, tab, o_all, l_all, d_yc, dproj3):
    B = proj3.shape[0]

    def body(q_ref, k_ref, v_ref, cz_ref, gq_ref, gk_ref, t_ref, o_ref, l_ref, dy_ref, dp_in,
             dp_out, dgq_out, dgk_out, qs, ks, vs, dos, dls, dqs, dks, dvs, rqs, rks, dczs,
             st_q, st_k, st_v, st_z, sems, sem_z):
        b_, hp, g = pl.program_id(0), pl.program_id(1), pl.program_id(2)
        col = lambda base: pl.ds(pl.multiple_of(base + hp * LANE, LANE), LANE)
        lo = _lo_mask((QB, LANE))

        @pl.when(jnp.logical_and(jnp.logical_and(pl.program_id(0) == 0, pl.program_id(1) == 0), g == 0))
        def _():
            dgq_out[...] = jnp.zeros((NG, LANE), F32)
            dgk_out[...] = jnp.zeros((NG, LANE), F32)

        @pl.when(g == 0)
        def _():
            def chunk(i, carry):
                rows = pl.ds(pl.multiple_of(i * MERGE_ROWS, MERGE_ROWS), MERGE_ROWS)
                ls = [l_ref[j, rows, :] for j in range(NG)]
                m = jnp.maximum(jnp.maximum(ls[0], ls[1]), ls[2])
                es = [jnp.exp(t - m) for t in ls]
                den = (es[0] + es[1]) + es[2]
                al = [e / den for e in es]
                os_ = [o_ref[j, rows, :] for j in range(NG)]
                oc = (al[0] * os_[0] + al[1] * os_[1]) + al[2] * os_[2]
                cz = cz_ref[rows, :].astype(F32)
                sg = jax.nn.sigmoid(cz)
                dy = dy_ref[rows, :]
                d_oc = dy * (cz * sg)
                dczs[rows, :] = (dy * oc * (sg * (1.0 + cz * (1.0 - sg)))).astype(BF16)
                ts = [_head_sum_1(d_oc * os_[j]) for j in range(NG)]
                tbar = (al[0] * ts[0] + al[1] * ts[1]) + al[2] * ts[2]
                for j in range(NG):
                    dos[j, rows, :] = al[j] * d_oc
                    dls[j, rows, :] = al[j] * (ts[j] - tbar)
                return carry
            lax.fori_loop(0, S // MERGE_ROWS, chunk, 0)
            _write_windows(dp_out, [st_z], sem_z, b_ * 4 + hp, B * 4, [(dczs[...], dp_out.at[b_, :, col(O_CZ)])])

        def group(gi):
            d = DIL[gi][1]
            xq, xk = q_ref[...].astype(F32), k_ref[...].astype(F32)
            rqs[...] = _rms2_scale(xq)
            rks[...] = _rms2_scale(xk)
            qs[...] = xq * rqs[...] * gq_ref[gi:gi + 1, :]
            ks[...] = xk * rks[...] * gk_ref[gi:gi + 1, :]
            vs[...] = v_ref[...].astype(F32)
            dks[...] = jnp.zeros((S, LANE), F32)
            dvs[...] = jnp.zeros((S, LANE), F32)
            bias = _dil_bias(t_ref, gi, d)

            def block(cur, prev):
                rows = _dil_rows(cur, d)
                q2 = _stack_heads(qs[rows, :], lo).astype(BF16)
                dob = dos[gi, rows, :]
                do2 = _stack_heads(dob, lo).astype(BF16)
                kc, vc = ks[rows, :], vs[rows, :]
                if prev is None:
                    kcat, vcat, b = kc, vc, bias[:, QB:]
                else:
                    prow = _dil_rows(prev, d)
                    kcat = jnp.concatenate([ks[prow, :], kc], axis=0)
                    vcat = jnp.concatenate([vs[prow, :], vc], axis=0)
                    b = bias
                kcat = kcat.astype(BF16)
                vcat = vcat.astype(BF16)
                lse_b = l_ref[gi, rows, :]
                corr_b = dls[gi, rows, :] - _head_sum_1(dob * o_ref[gi, rows, :])
                lse2 = jnp.concatenate([lse_b[:, 0:1], lse_b[:, HD:HD + 1]], axis=0)
                corr2 = jnp.concatenate([corr_b[:, 0:1], corr_b[:, HD:HD + 1]], axis=0)
                s = _dot_nt(q2, kcat) * DIL_SCALE + b
                p = jnp.exp(s - lse2)
                ds = (p * (_dot_nt(do2, vcat) + corr2) * DIL_SCALE).astype(BF16)
                dq2 = _dot(ds, kcat)
                dqs[rows, :] = jnp.where(lo, dq2[:QB], dq2[QB:])
                dk = _dot_tn(ds, q2)
                dv = _dot_tn(p.astype(BF16), do2)
                if prev is None:
                    dks[rows, :] += dk
                    dvs[rows, :] += dv
                else:
                    dks[prow, :] += dk[:QB]
                    dvs[prow, :] += dv[:QB]
                    dks[rows, :] += dk[QB:]
                    dvs[rows, :] += dv[QB:]

            _dil_walk(d, block, False)

            dxq, dgq = _rms2_bwd(q_ref[...].astype(F32), rqs[...], gq_ref[gi:gi + 1, :], dqs[...])
            dgq_out[gi:gi + 1, :] += dgq
            dxk, dgk = _rms2_bwd(k_ref[...].astype(F32), rks[...], gk_ref[gi:gi + 1, :], dks[...])
            dgk_out[gi:gi + 1, :] += dgk
            step, nsteps = _grid_step((B, 4, NG))
            _write_windows(dp_out, [st_q, st_k, st_v], sems, step, nsteps,
                           [(dxq, dp_out.at[b_, :, col(O_DQ + gi * DWID)]), (dxk, dp_out.at[b_, :, col(O_DK + gi * DWID)]),
                            (dvs[...], dp_out.at[b_, :, col(O_DV + gi * DWID)])])

        for gi in range(NG):
            pl.when(g == gi)(functools.partial(group, gi))

    def col(base):
        return pl.BlockSpec((None, S, LANE), lambda b, hp, g: (b, 0, base // LANE + 4 * g + hp))

    gspec = pl.BlockSpec((NG, LANE), lambda b, hp, g: (0, 0))
    saved = pl.BlockSpec((NG, None, S, LANE), lambda b, hp, g: (0, b, 0, hp))
    per_pair = pl.BlockSpec((None, S, LANE), lambda b, hp, g: (b, 0, hp))
    return pl.pallas_call(
        body,
        grid=(B, 4, NG),
        in_specs=[col(O_DQ), col(O_DK), col(O_DV),
                  pl.BlockSpec((None, S, LANE), lambda b, hp, g: (b, 0, O_CZ // LANE + hp)),
                  gspec, gspec, pl.BlockSpec((None, 8, LANE), lambda b, hp, g: (hp, 0, 0)),
                  saved, saved, per_pair, pl.BlockSpec(memory_space=pl.ANY)],
        out_specs=[pl.BlockSpec(memory_space=pl.ANY), gspec, gspec],
        out_shape=[jax.ShapeDtypeStruct(dproj3.shape, dproj3.dtype), jax.ShapeDtypeStruct((NG, LANE), F32),
                   jax.ShapeDtypeStruct((NG, LANE), F32)],
        input_output_aliases={10: 0},
        scratch_shapes=[pltpu.VMEM((S, LANE), F32)] * 3 + [pltpu.VMEM((NG, S, LANE), F32)] * 2
        + [pltpu.VMEM((S, LANE), F32)] * 5 + [pltpu.VMEM((S, LANE), BF16)] + [pltpu.VMEM((2, S, LANE), BF16)] * 4
        + [pltpu.SemaphoreType.DMA((3, 2)), pltpu.SemaphoreType.DMA((1, 2))],
        name="dil_mixer_bwd",
        compiler_params=pltpu.CompilerParams(dimension_semantics=("arbitrary",) * 3, vmem_limit_bytes=VMEM_LIMIT),
    )(proj3, proj3, proj3, proj3, gq, gk, tab, o_all, l_all, d_yc, dproj3)


def _dil_slopes():
    slopes = (2.0 ** (-8.0 * np.arange(1, NG * DH + 1, dtype=np.float32) / (NG * DH))).astype(np.float32).reshape(NG, DH)
    tab = np.zeros((4, 8, LANE), np.float32)
    for hp in range(4):
        for gi in range(NG):
            for e in (0, 1):
                tab[hp, 2 * gi + e, :] = slopes[gi, 2 * hp + e]
    return jnp.asarray(tab)


def _rope_tables():
    inv = ROPE_THETA ** (-jnp.arange(0, ROPE, 2, dtype=F32) / ROPE)
    ang = jnp.arange(S, dtype=F32)[:, None] * inv[None, :]
    cos, sin = jnp.cos(ang), jnp.sin(ang)
    z16 = jnp.zeros((S, 16), F32)
    c = jnp.concatenate([jnp.ones((S, NOPE), F32), cos, cos, jnp.zeros((S, 32), F32)], axis=1)
    s1 = jnp.concatenate([jnp.zeros((S, NOPE), F32), -sin, z16, jnp.zeros((S, 32), F32)], axis=1)
    s2 = jnp.concatenate([jnp.zeros((S, NOPE), F32), z16, sin, jnp.zeros((S, 32), F32)], axis=1)
    return c, s1, s2


def _pad_heads_uq(w):
    return jnp.pad(w.reshape(QL, NH, QK), ((0, 0), (0, 0), (0, QKP - QK))).reshape(QL, NH * QKP)


def _unpad_heads_uq(g):
    return g.reshape(QL, NH, QKP)[:, :, :QK].reshape(QL, NH * QK)


def _split_ukv(w):
    w3 = w.reshape(KVL, NH, NOPE + VD)
    uk = jnp.pad(w3[:, :, :NOPE], ((0, 0), (0, 0), (0, QKP - NOPE))).reshape(KVL, NH * QKP)
    return uk, w3[:, :, NOPE:].reshape(KVL, NH * VD)


def _join_ukv(guk, guv):
    return jnp.concatenate([guk.reshape(KVL, NH, QKP)[:, :, :NOPE], guv.reshape(KVL, NH, VD)],
                           axis=-1).reshape(KVL, NH * (NOPE + VD))


BR = 512
BRM = 256


def _layer_fwd(x, w, tabs, batch):
    T = batch * S
    rope_c, rope_s1, rope_s2, dil_tab = tabs
    res = {"x": x}
    row = lambda c: (lambda i: (i, c))
    fix = lambda i: (0, 0)

    h = _pcall("norm_fwd", _rms, (T // BR,),
               [(x, (BR, D), row(0)), (w["norm_g"], (1, D), fix)],
               [((T, D), BF16, (BR, D), row(0))])[0]
    proj = _mm("in_proj", h, w["w_in_t"], tb=True, out_dtype=BF16, b_words=True, tm=2048, tn=1024)
    res["h"], res["proj"] = h, proj
    proj3 = proj.reshape(batch, S, NINP)

    cblk = lambda s: (lambda j, b: (b, 0, 4 * s + j))
    y_a = _pcall("conv_fwd", _conv_math, (4, batch),
                 [(proj3, (None, S, LANE), cblk(0)), (proj3, (None, S, LANE), cblk(1)),
                  (proj3, (None, S, LANE), cblk(2)), (proj3, (None, S, LANE), cblk(3)),
                  (w["conv_w"], (3, LANE), lambda j, b: (0, j)), (w["conv_b"], (1, LANE), lambda j, b: (0, j))],
                 [((batch, S, CW), BF16, (None, S, LANE), lambda j, b: (b, 0, j))])[0].reshape(T, CW)
    res["y_a"] = y_a

    cqn, ckvn = _pcall("mla_pre_fwd", _mla_pre_math, (T // BR,),
                       [(proj, (BR, QL), row(O_CQ // QL)), (proj, (BR, KVL), row(O_CKV // KVL)),
                        (w["q_a_norm_g"], (1, QL), fix), (w["kv_a_norm_g"], (1, KVL), fix)],
                       [((T, QL), BF16, (BR, QL), row(0)), ((T, KVL), BF16, (BR, KVL), row(0))])
    w_uq_p = _pad_heads_uq(w["w_uq"])
    w_uk, w_uv = _split_ukv(w["w_ukv"])
    q = _mm("uq", cqn, w_uq_p, out_dtype=BF16)
    kn = _mm("uk", ckvn, w_uk, out_dtype=BF16)
    v = _mm("uv", ckvn, w_uv, out_dtype=BF16)
    nrr = S // BR
    tab_row = lambda i: (i % nrr, 0)
    qr, kr = _pcall("rope_fwd", _rope_math, (T // BR,),
                    [(q, (BR, NH * QKP), row(0)), (kn, (BR, NH * QKP), row(0)), (proj, (BR, LANE), row(O_KPE // LANE)),
                     (w["mla_q_norm_g"], (1, QKP), fix), (w["mla_k_norm_g"], (1, QKP), fix),
                     (rope_c, (BR, QKP), tab_row), (rope_s1, (BR, QKP), tab_row), (rope_s2, (BR, QKP), tab_row)],
                    [((T, NH * QKP), BF16, (BR, NH * QKP), row(0))] * 2)
    qr = qr.reshape(batch, S, NH * QKP)
    kr = kr.reshape(batch, S, NH * QKP)
    v = v.reshape(batch, S, NH * VD)
    o_b, l_b = _mla_fwd(qr, kr, v)
    ob2 = o_b.reshape(T, NH * VD)
    y_b = _pcall("gateb_fwd", _gate_math, (T // BR,),
                 [(ob2, (BR, 512), row(0)), (proj, (BR, 512), row(O_BZ // 512))],
                 [((T, 512), BF16, (BR, 512), row(0))])[0]
    res.update(cqn=cqn, ckvn=ckvn, q=q, kn=kn, qr=qr, kr=kr, v=v, o_b=o_b, l_b=l_b, ob2=ob2, y_b=y_b,
               w_uq_p=w_uq_p, w_uk=w_uk, w_uv=w_uv)

    gq2 = jnp.tile(w["dil_q_norm_g"].reshape(NG, HD), (1, 2))
    gk2 = jnp.tile(w["dil_k_norm_g"].reshape(NG, HD), (1, 2))
    y_c, o_all, l_all = _dilc_fwd(proj3, gq2, gk2, dil_tab)
    y_c = y_c.reshape(T, DWID)
    res.update(o_all=o_all, l_all=l_all, y_c=y_c)

    pa = _mm("out_a", y_a, w["w_out_a"], out_dtype=BF16)
    pb = _mm("out_b", y_b, w["w_out_b"], out_dtype=BF16)
    pc = _mm("out_c", y_c, w["w_out_c"], out_dtype=BF16)
    merged = _pcall("merge_fwd", _merge_math, (T // BRM,),
                    [(proj, (BRM, D), row(O_G // D + s)) for s in range(3)]
                    + [(w["b_gate"], (1, D), (lambda s: (lambda i: (0, s)))(s)) for s in range(3)]
                    + [(t, (BRM, D), row(0)) for t in (pa, pb, pc)],
                    [((T, D), BF16, (BRM, D), row(0))])[0]
    out = _mm("o_proj", merged, w["w_o"], add=x, tm=1024)
    res.update(pa=pa, pb=pb, pc=pc, merged=merged)
    return out, res


def _norm_bwd_math(x, g, dh, dy):
    _, pull = jax.vjp(_rms, x, g)
    dx, dg = pull(dh)
    return dx + dy, dg


def _layer_bwd(dy, w, res, tabs, batch, after_dw=None, after_merge=None):
    T = batch * S
    rope_c, rope_s1, rope_s2, dil_tab = tabs
    row = lambda c: (lambda i: (i, c))
    fix = lambda i: (0, 0)
    x, proj, h = res["x"], res["proj"], res["h"]
    proj3 = proj.reshape(batch, S, NINP)
    g = {}

    d_merged = _mm("o_proj_dx", dy, w["w_o"], tb=True)
    g["w_o"] = _mm("o_proj_dw", res["merged"], dy, ta=True, tm=1024, tk=2048)

    dproj = lax.empty((T, NINP), BF16)
    rows_of = lambda br: (lambda ref, i: ref.at[pl.ds(pl.multiple_of(i * br, br), br)])

    def merge_bwd(*args):
        dg0, dg1, dg2, db0, db1, db2, dpa, dpb, dpc = _vjp_of(_merge_math, 9)(*args, n_prim=9)
        return db0, db1, db2, dpa, dpb, dpc, jnp.concatenate([dg0, dg1, dg2], axis=1)

    db0, db1, db2, dpa, dpb, dpc, dproj = _pcall(
        "merge_bwd", merge_bwd, (T // BRM,),
        [(proj, (BRM, D), row(O_G // D + s)) for s in range(3)]
        + [(w["b_gate"], (1, D), (lambda s: (lambda i: (0, s)))(s)) for s in range(3)]
        + [(t, (BRM, D), row(0)) for t in (res["pa"], res["pb"], res["pc"])]
        + [(d_merged, (BRM, D), row(0))],
        [((1, D), F32, (1, D), fix, True)] * 3 + [((T, D), BF16, (BRM, D), row(0))] * 3,
        into=(dproj, [((BRM, 3 * D), lambda ref, i: rows_of(BRM)(ref, i).at[:, O_G:O_G + 3 * D])]))
    g["b_gate"] = jnp.concatenate([db0, db1, db2], axis=1)

    dep = after_merge(dpa) if after_merge is not None else None
    d_ya = _mm("out_a_dx", dpa, w["w_out_a"], tb=True, dep=dep)
    d_yb = _mm("out_b_dx", dpb, w["w_out_b"], tb=True)
    d_yc = _mm("out_c_dx", dpc, w["w_out_c"], tb=True)
    g["w_out_a"] = _mm("out_a_dw", res["y_a"], dpa, ta=True, tk=T)
    g["w_out_b"] = _mm("out_b_dw", res["y_b"], dpb, ta=True, tk=T)
    g["w_out_c"] = _mm("out_c_dw", res["y_c"], dpc, ta=True, tk=T)

    cblk = lambda s: (lambda j, b: (b, 0, 4 * s + j))
    oblk = lambda j, b: (b, 0, j)
    def conv_bwd(*args):
        d_ab, d_ac, d_ax, d_az, dcw, dcb = _vjp_of(_conv_math, 6)(*args, n_prim=6)
        return dcw, dcb, d_ab, d_ac, d_ax, d_az

    a_col = lambda s_: (lambda ref, j, b: ref.at[b, :, pl.ds(pl.multiple_of(O_A + s_ * CW + j * LANE, LANE), LANE)])
    g["conv_w"], g["conv_b"], dproj3 = _pcall(
        "conv_bwd", conv_bwd, (4, batch),
        [(proj3, (None, S, LANE), cblk(s)) for s in range(4)]
        + [(w["conv_w"], (3, LANE), lambda j, b: (0, j)), (w["conv_b"], (1, LANE), lambda j, b: (0, j)),
           (d_ya.reshape(batch, S, CW), (None, S, LANE), oblk)],
        [((3, CW), F32, (3, LANE), lambda j, b: (0, j), True), ((1, CW), F32, (1, LANE), lambda j, b: (0, j), True)],
        into=(dproj.reshape(batch, S, NINP), [((S, LANE), a_col(s_)) for s_ in range(4)]))
    dproj = dproj3.reshape(T, NINP)

    gate_bwd = functools.partial(_vjp_of(_gate_math, 2), n_prim=2)
    d_ob, dproj = _pcall("gateb_bwd", gate_bwd, (T // BR,),
                         [(res["ob2"], (BR, 512), row(0)), (proj, (BR, 512), row(O_BZ // 512)), (d_yb, (BR, 512), row(0))],
                         [((T, 512), F32, (BR, 512), row(0))],
                         into=(dproj, [((BR, 512), lambda ref, i: rows_of(BR)(ref, i).at[:, O_BZ:O_BZ + 512])]))
    dqr, dkr, dv = _mla_bwd(res["qr"], res["kr"], res["v"], d_ob.reshape(batch, S, NH * VD), res["o_b"], res["l_b"])
    nrr = S // BR
    tab_row = lambda i: (i % nrr, 0)
    def rope_bwd(*args):
        d_q, d_kn, d_kpe, dgq, dgk = _vjp_of(_rope_math, 5)(*args, n_prim=8)
        return d_q, d_kn, dgq, dgk, d_kpe

    d_q, d_kn, g["mla_q_norm_g"], g["mla_k_norm_g"], dproj = _pcall(
        "rope_bwd", rope_bwd, (T // BR,),
        [(res["q"], (BR, NH * QKP), row(0)), (res["kn"], (BR, NH * QKP), row(0)), (proj, (BR, LANE), row(O_KPE // LANE)),
         (w["mla_q_norm_g"], (1, QKP), fix), (w["mla_k_norm_g"], (1, QKP), fix),
         (rope_c, (BR, QKP), tab_row), (rope_s1, (BR, QKP), tab_row), (rope_s2, (BR, QKP), tab_row),
         (dqr.reshape(T, NH * QKP), (BR, NH * QKP), row(0)), (dkr.reshape(T, NH * QKP), (BR, NH * QKP), row(0))],
        [((T, NH * QKP), BF16, (BR, NH * QKP), row(0))] * 2 + [((1, QKP), F32, (1, QKP), fix, True)] * 2,
        into=(dproj, [((BR, LANE), lambda ref, i: rows_of(BR)(ref, i).at[:, O_KPE:O_KPE + LANE])]))
    dv = dv.reshape(T, NH * VD)
    d_cqn = _mm("uq_dx", d_q, res["w_uq_p"], tb=True)
    d_ckvn = _mm("uk_dx", d_kn, res["w_uk"], tb=True)
    d_ckvn = _mm("uv_dx", dv, res["w_uv"], tb=True, add=d_ckvn)
    g["w_uq"] = _unpad_heads_uq(_mm("uq_dw", res["cqn"], d_q, ta=True, tk=T))
    g["w_ukv"] = _join_ukv(_mm("uk_dw", res["ckvn"], d_kn, ta=True, tk=T),
                           _mm("uv_dw", res["ckvn"], dv, ta=True, tk=T))
    def pre_bwd(*args):
        d_cq, d_ckv, dgq, dgkv = _vjp_of(_mla_pre_math, 4)(*args, n_prim=4)
        return dgq, dgkv, jnp.concatenate([d_cq, d_ckv], axis=1)

    g["q_a_norm_g"], g["kv_a_norm_g"], dproj = _pcall(
        "mla_pre_bwd", pre_bwd, (T // BR,),
        [(proj, (BR, QL), row(O_CQ // QL)), (proj, (BR, KVL), row(O_CKV // KVL)),
         (w["q_a_norm_g"], (1, QL), fix), (w["kv_a_norm_g"], (1, KVL), fix),
         (d_cqn, (BR, QL), row(0)), (d_ckvn, (BR, KVL), row(0))],
        [((1, QL), F32, (1, QL), fix, True), ((1, KVL), F32, (1, KVL), fix, True)],
        into=(dproj, [((BR, QL + KVL), lambda ref, i: rows_of(BR)(ref, i).at[:, O_CQ:O_CQ + QL + KVL])]))

    gq2 = jnp.tile(w["dil_q_norm_g"].reshape(NG, HD), (1, 2))
    gk2 = jnp.tile(w["dil_k_norm_g"].reshape(NG, HD), (1, 2))
    dproj3, dgq, dgk = _dilc_bwd(proj3, gq2, gk2, dil_tab, res["o_all"], res["l_all"],
                                 d_yc.reshape(batch, S, DWID), dproj.reshape(batch, S, NINP))
    dproj = dproj3.reshape(T, NINP)
    g["dil_q_norm_g"] = dgq[:, :HD] + dgq[:, HD:]
    g["dil_k_norm_g"] = dgk[:, :HD] + dgk[:, HD:]

    g["w_in_t"] = _mm("in_proj_dw", dproj, h, ta=True, tm=1024, tk=T)
    dep = after_dw(g) if after_dw is not None else None
    d_h = _mm("in_proj_dx", dproj, w["w_in_t"], dep=dep, b_words=True, tm=1024, tk=NINP // 4)
    dx, g["norm_g"] = _pcall("norm_bwd", _norm_bwd_math, (T // BR,),
                             [(x, (BR, D), row(0)), (w["norm_g"], (1, D), fix), (d_h, (BR, D), row(0)),
                              (dy, (BR, D), row(0))],
                             [((T, D), F32, (BR, D), row(0)), ((1, D), F32, (1, D), fix, True)])
    return dx, g


def _loss_math(y, t):
    e = y - t
    return e * (1.0 / D), 0.5 * jnp.sum(jnp.sum(e * e, axis=-1, keepdims=True) / D, axis=0, keepdims=True)


ANY = pl.BlockSpec(memory_space=pl.ANY)
U32 = jnp.uint32
WSH = NIN // 4
WA = KPE_END
WB = WSH - WA
CWD = 512
PACK_ROWS = 1472
HW = PACK_W // 2


def _me():
    return lax.axis_index("x"), lax.axis_index("y"), lax.axis_index("c")


def _piece_rows(k):
    a = k * WSH + jnp.where(k > 0, NINP - NIN, 0)
    b = k * WSH + WA + (NINP - NIN)
    return ((0, pl.multiple_of(a, 8), WA), (WA, pl.multiple_of(b, 8), WB))


def _pack_words(lo, hi):
    ul = lax.bitcast_convert_type(lo.astype(BF16).astype(F32), U32)
    uh = lax.bitcast_convert_type(hi.astype(BF16).astype(F32), U32)
    w = jnp.bitwise_or(jnp.bitwise_and(uh, jnp.uint32(0xFFFF0000)), jnp.right_shift(ul, jnp.uint32(16)))
    return lax.bitcast_convert_type(w, F32)


def _unpack_words(w):
    w = lax.bitcast_convert_type(w, U32)
    lo = lax.bitcast_convert_type(jnp.left_shift(w, jnp.uint32(16)), F32)
    hi = lax.bitcast_convert_type(jnp.bitwise_and(w, jnp.uint32(0xFFFF0000)), F32)
    return lo, hi


def _all_gather(wc, sp):
    def body(w_ref, s_ref, ow_ref, os_ref, send_sems, recv_sems):
        x, y, c = _me()
        k_me = 2 * x + y
        sib = (x, y, 1 - c)
        chips = [(1 - x, y), (x, 1 - y), (1 - x, 1 - y)]
        wcols = lambda cc: pl.ds(pl.multiple_of(cc * (CWD // 2), LANE), CWD // 2)
        scols = lambda cc: pl.ds(pl.multiple_of(cc * HW, LANE), HW)

        def windows(k, cc):
            pcs = _piece_rows(k)
            return ([(w_ref.at[pl.ds(l0, n), wcols(cc)], ow_ref.at[pl.ds(p0, n), wcols(cc)]) for l0, p0, n in pcs]
                    + [(s_ref.at[:, scols(cc)], os_ref.at[k, :, scols(cc)])])

        def copy(i, src, dst, to):
            return pltpu.make_async_remote_copy(src_ref=src, dst_ref=dst, send_sem=send_sems.at[i],
                                                recv_sem=recv_sems.at[i], device_id=to, device_id_type=MESH)

        def own_windows():
            return ([(w_ref.at[pl.ds(l0, n)], ow_ref.at[pl.ds(p0, n)]) for l0, p0, n in _piece_rows(k_me)]
                    + [(s_ref, os_ref.at[k_me])])

        first = [copy(18 + i, src, dst, sib) for i, (src, dst) in enumerate(own_windows())]
        for j, (cx, cy) in enumerate(chips):
            for i, (src, dst) in enumerate(windows(k_me, c)):
                first.append(copy(3 * j + i, src, dst, (cx, cy, c)))
        for cp in first:
            cp.start()
        passed = []
        for j, (cx, cy) in enumerate(chips):
            for i, (_, dst) in enumerate(windows(2 * cx + cy, c)):
                copy(3 * j + i, dst, dst, (cx, cy, c)).wait_recv()
                cp = copy(9 + 3 * j + i, dst, dst, sib)
                cp.start()
                passed.append(cp)
        for j, (cx, cy) in enumerate(chips):
            for i, (_, dst) in enumerate(windows(2 * cx + cy, 1 - c)):
                copy(9 + 3 * j + i, dst, dst, sib).wait_recv()
        for i, (_, dst) in enumerate(own_windows()):
            copy(18 + i, dst, dst, sib).wait_recv()
        for cp in first + passed:
            cp.wait_send()

    return pl.pallas_call(
        body,
        out_shape=[jax.ShapeDtypeStruct((NINP, CWD), F32), jax.ShapeDtypeStruct((4, PACK_ROWS, PACK_W), BF16)],
        in_specs=[ANY, ANY], out_specs=[ANY, ANY],
        scratch_shapes=[pltpu.SemaphoreType.DMA((21,)), pltpu.SemaphoreType.DMA((21,))],
        name="weights_all_gather",
    )(wc, sp)


HBM = pl.BlockSpec(memory_space=pltpu.HBM)
SEM = pl.BlockSpec(memory_space=pltpu.SEMAPHORE)
EFFECT = pltpu.SideEffectType.DATAFLOW_SIDE_EFFECTING


def _in_hbm(a):
    return pltpu.with_memory_space_constraint(a, pltpu.HBM)


def _ag_shard(w_ref, s_ref, lw_ref, ls_ref, k):
    return ([(w_ref.at[pl.ds(l0, n)], lw_ref.at[pl.ds(p0, n)]) for l0, p0, n in _piece_rows(k)]
            + [(s_ref, ls_ref.at[k])])


def _ag_behind_copies(w_ref, s_ref, lw_ref, ls_ref, send_sems, recv_sems):
    x, y, c = _me()
    peers = [(1 - x, y, c), (x, 1 - y, c), (1 - x, 1 - y, c), (x, y, 1 - c)]
    mine, theirs = [], []
    for j, (px, py, pc) in enumerate(peers):
        for i, ((src, dst), (_, got)) in enumerate(zip(_ag_shard(w_ref, s_ref, lw_ref, ls_ref, 2 * x + y),
                                                       _ag_shard(w_ref, s_ref, lw_ref, ls_ref, 2 * px + py))):
            mk = lambda s_, d_: pltpu.make_async_remote_copy(
                src_ref=s_, dst_ref=d_, send_sem=send_sems.at[3 * j + i], recv_sem=recv_sems.at[3 * j + i],
                device_id=(px, py, pc), device_id_type=MESH)
            mine.append(mk(src, dst))
            theirs.append(mk(got, got))
    return mine, theirs


def _ag_behind_start(wc, sp, dep):
    def body(w_ref, s_ref, lw_ref, ls_ref, dep_ref, send_sems, recv_sems, w_thru, s_thru, lw_thru, ls_thru, token):
        mine, _ = _ag_behind_copies(w_ref, s_ref, lw_ref, ls_ref, send_sems, recv_sems)
        for cp in mine:
            cp.start()
        token[...] = jnp.zeros_like(token)

    return pl.pallas_call(
        body, name="weights_gather_start",
        out_shape=(pltpu.SemaphoreType.DMA((12,)), pltpu.SemaphoreType.DMA((12,)), pltpu.HBM(wc.shape, wc.dtype),
                   pltpu.HBM(sp.shape, sp.dtype), pltpu.HBM((NINP, CWD), F32), pltpu.HBM((4, PACK_ROWS, PACK_W), BF16),
                   jax.ShapeDtypeStruct((8, LANE), F32)),
        in_specs=(HBM, HBM, HBM, HBM, ANY),
        out_specs=(SEM, SEM, HBM, HBM, HBM, HBM, pl.BlockSpec(memory_space=pltpu.VMEM)),
        input_output_aliases={0: 2, 1: 3, 2: 4, 3: 5},
        compiler_params=pltpu.CompilerParams(has_side_effects=EFFECT),
    )(_in_hbm(wc), _in_hbm(sp), _in_hbm(lax.empty((NINP, CWD), F32)), _in_hbm(lax.empty((4, PACK_ROWS, PACK_W), BF16)), dep)


def _ag_behind_wait(send_sems, recv_sems, wc, sp, lw, ls, after):
    def body(w_ref, s_ref, lw_ref, ls_ref, send_sems, recv_sems, after_ref, w_dead, s_dead, lw_out, ls_out):
        mine, theirs = _ag_behind_copies(w_ref, s_ref, lw_ref, ls_ref, send_sems, recv_sems)
        for cp in mine:
            cp.wait_send()
        for cp in theirs:
            cp.wait_recv()

    out = pl.pallas_call(
        body, name="weights_gather_wait",
        out_shape=(pltpu.HBM(wc.shape, wc.dtype), pltpu.HBM(sp.shape, sp.dtype), pltpu.HBM(lw.shape, lw.dtype),
                   pltpu.HBM(ls.shape, ls.dtype)),
        in_specs=(HBM, HBM, HBM, HBM, SEM, SEM, ANY), out_specs=(HBM, HBM, HBM, HBM),
        input_output_aliases={0: 0, 1: 1, 2: 2, 3: 3},
        compiler_params=pltpu.CompilerParams(has_side_effects=EFFECT),
    )(wc, sp, lw, ls, send_sems, recv_sems, after)
    return out[2], out[3]


def _rs_swap_copies(w_ref, s_ref, rw_ref, rs_ref, send_sems, recv_sems):
    x, y, c = _me()
    oc = 1 - c
    return [pltpu.make_async_remote_copy(src_ref=w_ref.at[:, pl.ds(pl.multiple_of(oc * (D // 2), LANE), D // 2)],
                                         dst_ref=rw_ref, send_sem=send_sems.at[0], recv_sem=recv_sems.at[0],
                                         device_id=(x, y, oc), device_id_type=MESH),
            pltpu.make_async_remote_copy(src_ref=s_ref.at[:, :, pl.ds(pl.multiple_of(oc * HW, LANE), HW)],
                                         dst_ref=rs_ref, send_sem=send_sems.at[1], recv_sem=recv_sems.at[1],
                                         device_id=(x, y, oc), device_id_type=MESH)]


def _rs_swap_start(gw, gs, tag):
    def body(w_ref, s_ref, rw_ref, rs_ref, send_sems, recv_sems, w_thru, s_thru, rw_thru, rs_thru, token):
        for cp in _rs_swap_copies(w_ref, s_ref, rw_ref, rs_ref, send_sems, recv_sems):
            cp.start()
        token[...] = jnp.zeros_like(token)

    return pl.pallas_call(
        body, name="grads_swap_start_" + tag,
        out_shape=(pltpu.SemaphoreType.DMA((2,)), pltpu.SemaphoreType.DMA((2,)), pltpu.HBM(gw.shape, gw.dtype),
                   pltpu.HBM(gs.shape, gs.dtype), pltpu.HBM((NINP, D // 2), F32), pltpu.HBM((4, PACK_ROWS, HW), F32),
                   jax.ShapeDtypeStruct((8, LANE), F32)),
        in_specs=(HBM, HBM, HBM, HBM),
        out_specs=(SEM, SEM, HBM, HBM, HBM, HBM, pl.BlockSpec(memory_space=pltpu.VMEM)),
        input_output_aliases={0: 2, 1: 3, 2: 4, 3: 5},
        compiler_params=pltpu.CompilerParams(has_side_effects=EFFECT),
    )(_in_hbm(gw), _in_hbm(gs), _in_hbm(lax.empty((NINP, D // 2), F32)), _in_hbm(lax.empty((4, PACK_ROWS, HW), F32)))


def _rs_swap_wait(send_sems, recv_sems, gw, gs, rw, rs, after, tag):
    def body(w_ref, s_ref, rw_ref, rs_ref, send_sems, recv_sems, after_ref, w_out, s_out, rw_out, rs_out):
        for cp in _rs_swap_copies(w_ref, s_ref, rw_ref, rs_ref, send_sems, recv_sems):
            cp.wait()

    return pl.pallas_call(
        body, name="grads_swap_wait_" + tag,
        out_shape=(pltpu.HBM(gw.shape, gw.dtype), pltpu.HBM(gs.shape, gs.dtype), pltpu.HBM(rw.shape, rw.dtype),
                   pltpu.HBM(rs.shape, rs.dtype)),
        in_specs=(HBM, HBM, HBM, HBM, SEM, SEM, ANY), out_specs=(HBM, HBM, HBM, HBM),
        input_output_aliases={0: 0, 1: 1, 2: 2, 3: 3},
        compiler_params=pltpu.CompilerParams(has_side_effects=EFFECT),
    )(gw, gs, rw, rs, send_sems, recv_sems, after)


SUM_BR = 512


def _rs_chip_sum_w(gw, rw, cidx):
    def body(c_ref, g_ref, r_ref, o_ref):
        s = g_ref[...] + r_ref[...]
        q = D // 8
        o_ref[...] = jnp.concatenate([_pack_words(s[:, 0:q], s[:, q:2 * q]),
                                      _pack_words(s[:, 2 * q:3 * q], s[:, 3 * q:4 * q])], axis=1)

    return pl.pallas_call(
        body,
        grid_spec=pltpu.PrefetchScalarGridSpec(
            num_scalar_prefetch=1, grid=(NINP // SUM_BR,),
            in_specs=[pl.BlockSpec((SUM_BR, D // 2), lambda i, cr: (i, cr[0])),
                      pl.BlockSpec((SUM_BR, D // 2), lambda i, cr: (i, 0))],
            out_specs=pl.BlockSpec((SUM_BR, D // 4), lambda i, cr: (i, 0))),
        out_shape=jax.ShapeDtypeStruct((NINP, D // 4), F32),
        name="grads_chip_sum_w",
        compiler_params=pltpu.CompilerParams(dimension_semantics=("arbitrary",), vmem_limit_bytes=VMEM_LIMIT),
    )(cidx, gw, rw)


def _rs_chip_sum_s(gs, rs, cidx):
    def body(c_ref, g_ref, r_ref, o_ref):
        o_ref[...] = (g_ref[...] + r_ref[...]).astype(BF16)

    return pl.pallas_call(
        body,
        grid_spec=pltpu.PrefetchScalarGridSpec(
            num_scalar_prefetch=1, grid=(4,),
            in_specs=[pl.BlockSpec((None, PACK_ROWS, HW), lambda j, cr: (j, 0, cr[0])),
                      pl.BlockSpec((None, PACK_ROWS, HW), lambda j, cr: (j, 0, 0))],
            out_specs=pl.BlockSpec((None, PACK_ROWS, HW), lambda j, cr: (j, 0, 0))),
        out_shape=jax.ShapeDtypeStruct((4, PACK_ROWS, HW), BF16),
        name="grads_chip_sum_s",
        compiler_params=pltpu.CompilerParams(dimension_semantics=("arbitrary",), vmem_limit_bytes=VMEM_LIMIT),
    )(cidx, gs, rs)


def _rs_exchange_copies(sw_ref, ss_ref, r2w_ref, r2s_ref, send_sems, recv_sems):
    x, y, c = _me()
    mine, theirs = [], []
    for j, (cx, cy) in enumerate([(1 - x, y), (x, 1 - y), (1 - x, 1 - y)]):
        def mk(i, src, dst):
            return pltpu.make_async_remote_copy(src_ref=src, dst_ref=dst, send_sem=send_sems.at[3 * j + i],
                                                recv_sem=recv_sems.at[3 * j + i], device_id=(cx, cy, c), device_id_type=MESH)
        for i, (l0, p0, n) in enumerate(_piece_rows(2 * cx + cy)):
            mine.append(mk(i, sw_ref.at[pl.ds(p0, n)], r2w_ref.at[j, pl.ds(l0, n)]))
            theirs.append(mk(i, r2w_ref.at[j, pl.ds(l0, n)], r2w_ref.at[j, pl.ds(l0, n)]))
        mine.append(mk(2, ss_ref.at[2 * cx + cy], r2s_ref.at[j]))
        theirs.append(mk(2, r2s_ref.at[j], r2s_ref.at[j]))
    return mine, theirs


def _rs_exchange_start(sw, ss, tag):
    def body(sw_ref, ss_ref, r2w_ref, r2s_ref, send_sems, recv_sems, sw_thru, ss_thru, r2w_thru, r2s_thru, token):
        mine, _ = _rs_exchange_copies(sw_ref, ss_ref, r2w_ref, r2s_ref, send_sems, recv_sems)
        for cp in mine:
            cp.start()
        token[...] = jnp.zeros_like(token)

    return pl.pallas_call(
        body, name="grads_exchange_start_" + tag,
        out_shape=(pltpu.SemaphoreType.DMA((9,)), pltpu.SemaphoreType.DMA((9,)), pltpu.HBM(sw.shape, sw.dtype),
                   pltpu.HBM(ss.shape, ss.dtype), pltpu.HBM((3, WSH, D // 4), F32), pltpu.HBM((3, PACK_ROWS, HW), BF16),
                   jax.ShapeDtypeStruct((8, LANE), F32)),
        in_specs=(HBM, HBM, HBM, HBM),
        out_specs=(SEM, SEM, HBM, HBM, HBM, HBM, pl.BlockSpec(memory_space=pltpu.VMEM)),
        input_output_aliases={0: 2, 1: 3, 2: 4, 3: 5},
        compiler_params=pltpu.CompilerParams(has_side_effects=EFFECT),
    )(_in_hbm(sw), _in_hbm(ss), _in_hbm(lax.empty((3, WSH, D // 4), F32)), _in_hbm(lax.empty((3, PACK_ROWS, HW), BF16)))


def _rs_exchange_wait(send_sems, recv_sems, sw, ss, r2w, r2s, after, tag):
    def body(sw_ref, ss_ref, r2w_ref, r2s_ref, send_sems, recv_sems, after_ref, sw_dead, ss_dead, r2w_out, r2s_out):
        mine, theirs = _rs_exchange_copies(sw_ref, ss_ref, r2w_ref, r2s_ref, send_sems, recv_sems)
        for cp in mine:
            cp.wait_send()
        for cp in theirs:
            cp.wait_recv()

    out = pl.pallas_call(
        body, name="grads_exchange_wait_" + tag,
        out_shape=(pltpu.HBM(sw.shape, sw.dtype), pltpu.HBM(ss.shape, ss.dtype), pltpu.HBM(r2w.shape, r2w.dtype),
                   pltpu.HBM(r2s.shape, r2s.dtype)),
        in_specs=(HBM, HBM, HBM, HBM, SEM, SEM, ANY), out_specs=(HBM, HBM, HBM, HBM),
        input_output_aliases={0: 0, 1: 1, 2: 2, 3: 3},
        compiler_params=pltpu.CompilerParams(has_side_effects=EFFECT),
    )(sw, ss, r2w, r2s, send_sems, recv_sems, after)
    return out[2], out[3]


def _rs_final_w(gw, rw, r2w, idx):
    q = D // 8

    def body(i_ref, g_ref, r_ref, p_ref, o_ref, gbuf, rbuf, sems):
        i = pl.program_id(0)
        k, c = i_ref[0], i_ref[1]
        cps = []
        for n_, (l0, p0, n) in enumerate(_piece_rows(k)):
            gcol = pl.ds(pl.multiple_of(c * (D // 2) + i * 2 * q, LANE), 2 * q)
            rcol = pl.ds(pl.multiple_of(i * 2 * q, LANE), 2 * q)
            cps.append(pltpu.make_async_copy(g_ref.at[pl.ds(p0, n), gcol], gbuf.at[pl.ds(l0, n)], sems.at[2 * n_]))
            cps.append(pltpu.make_async_copy(r_ref.at[pl.ds(p0, n), rcol], rbuf.at[pl.ds(l0, n)], sems.at[2 * n_ + 1]))
        for cp in cps:
            cp.start()
        for cp in cps:
            cp.wait()
        acc = gbuf[...] + rbuf[...]
        for j in range(3):
            lo, hi = _unpack_words(p_ref[j])
            acc = acc + jnp.concatenate([lo, hi], axis=1)
        o_ref[...] = acc

    return pl.pallas_call(
        body,
        grid_spec=pltpu.PrefetchScalarGridSpec(
            num_scalar_prefetch=1, grid=(2,),
            in_specs=[ANY, ANY, pl.BlockSpec((3, WSH, q), lambda i, ir: (0, 0, i))],
            out_specs=pl.BlockSpec((WSH, 2 * q), lambda i, ir: (0, 2 * ir[1] + i)),
            scratch_shapes=[pltpu.VMEM((WSH, 2 * q), F32), pltpu.VMEM((WSH, 2 * q), F32), pltpu.SemaphoreType.DMA((4,))]),
        out_shape=jax.ShapeDtypeStruct((WSH, D), F32),
        name="grads_final_sum_w",
        compiler_params=pltpu.CompilerParams(dimension_semantics=("arbitrary",), vmem_limit_bytes=VMEM_LIMIT),
    )(idx, gw, rw, r2w)


def _rs_final_s(gs, rs, r2s, idx):
    def body(i_ref, g_ref, r_ref, p_ref, o_ref):
        acc = g_ref[...] + r_ref[...]
        for j in range(3):
            acc = acc + p_ref[j].astype(F32)
        o_ref[...] = acc

    return pl.pallas_call(
        body,
        grid_spec=pltpu.PrefetchScalarGridSpec(
            num_scalar_prefetch=1, grid=(1,),
            in_specs=[pl.BlockSpec((None, PACK_ROWS, HW), lambda i, ir: (ir[0], 0, ir[1])),
                      pl.BlockSpec((None, PACK_ROWS, HW), lambda i, ir: (ir[0], 0, 0)),
                      pl.BlockSpec((3, PACK_ROWS, HW), lambda i, ir: (0, 0, 0))],
            out_specs=pl.BlockSpec((PACK_ROWS, HW), lambda i, ir: (0, ir[1]))),
        out_shape=jax.ShapeDtypeStruct((PACK_ROWS, PACK_W), F32),
        name="grads_final_sum_s",
        compiler_params=pltpu.CompilerParams(dimension_semantics=("arbitrary",), vmem_limit_bytes=VMEM_LIMIT),
    )(idx, gs, rs, r2s)


def _rs_share(fw, fs):
    def body(w_ref, s_ref, ow_ref, os_ref, send_sems, recv_sems):
        x, y, c = _me()
        wcol = lambda cc: pl.ds(pl.multiple_of(cc * (D // 2), LANE), D // 2)
        scol = lambda cc: pl.ds(pl.multiple_of(cc * HW, LANE), HW)

        def copies(cc):
            return [pltpu.make_async_remote_copy(src_ref=w_ref.at[:, wcol(cc)], dst_ref=ow_ref.at[:, wcol(cc)],
                                                 send_sem=send_sems.at[0], recv_sem=recv_sems.at[0],
                                                 device_id=(x, y, 1 - c), device_id_type=MESH),
                    pltpu.make_async_remote_copy(src_ref=s_ref.at[:, scol(cc)], dst_ref=os_ref.at[:, scol(cc)],
                                                 send_sem=send_sems.at[1], recv_sem=recv_sems.at[1],
                                                 device_id=(x, y, 1 - c), device_id_type=MESH)]
        out = copies(c)
        for cp in out:
            cp.start()
        for cp in copies(1 - c):
            cp.wait_recv()
        for cp in out:
            cp.wait_send()

    return pl.pallas_call(
        body,
        out_shape=[jax.ShapeDtypeStruct(fw.shape, F32), jax.ShapeDtypeStruct(fs.shape, F32)],
        in_specs=[ANY, ANY], out_specs=[ANY, ANY],
        input_output_aliases={0: 0, 1: 1},
        scratch_shapes=[pltpu.SemaphoreType.DMA((2,)), pltpu.SemaphoreType.DMA((2,))],
        name="grads_share",
    )(fw, fs)


def _rs_sums(gw, gs, rw, rs):
    x, y, c = _me()
    cidx = jnp.reshape(c, (1,)).astype(jnp.int32)
    return dict(gw=gw, gs=gs, rw=rw, rs=rs, sw=_rs_chip_sum_w(gw, rw, cidx), ss=_rs_chip_sum_s(gs, rs, cidx))


def _rs_end(st, r2w, r2s):
    x, y, c = _me()
    idx = jnp.stack([2 * x + y, c]).astype(jnp.int32)
    return _rs_share(_rs_final_w(st["gw"], st["rw"], r2w, idx), _rs_final_s(st["gs"], st["rs"], r2s, idx))


def _all_reduce_small(gs):
    rows = gs.shape[0]

    def body(g_ref, o_ref, buf, send_sems, recv_sems):
        x, y, c = _me()
        me = 4 * x + 2 * y + c
        buf[me] = g_ref[...]
        cps = []
        for r in range(1, 8):
            fx, fy, fc = (r >> 2) & 1, (r >> 1) & 1, r & 1
            px, py, pc = jnp.bitwise_xor(x, fx), jnp.bitwise_xor(y, fy), jnp.bitwise_xor(c, fc)
            cps.append((pltpu.make_async_remote_copy(
                src_ref=g_ref, dst_ref=buf.at[me], send_sem=send_sems.at[r - 1], recv_sem=recv_sems.at[r - 1],
                device_id=(px, py, pc), device_id_type=MESH), 4 * px + 2 * py + pc))
        for cp, _ in cps:
            cp.start()
        for r, (cp, peer) in enumerate(cps):
            pltpu.make_async_remote_copy(
                src_ref=g_ref, dst_ref=buf.at[peer], send_sem=send_sems.at[r], recv_sem=recv_sems.at[r],
                device_id=(x, y, c), device_id_type=MESH).wait_recv()
        for cp, _ in cps:
            cp.wait_send()
        acc = buf[0]
        for k in range(1, 8):
            acc = acc + buf[k]
        o_ref[...] = acc

    return pl.pallas_call(
        body,
        out_shape=jax.ShapeDtypeStruct((rows, LANE), F32),
        in_specs=[pl.BlockSpec(memory_space=pltpu.VMEM)],
        out_specs=pl.BlockSpec(memory_space=pltpu.VMEM),
        scratch_shapes=[pltpu.VMEM((8, rows, LANE), F32), pltpu.SemaphoreType.DMA((7,)), pltpu.SemaphoreType.DMA((7,))],
        name="small_grads_all_reduce",
    )(gs)


PACK_SPLIT = (("w_uq", 96, (QL, 192)), ("w_ukv", 64, (KVL, 256)),
              ("w_out_a", 256, (CW, 256)), ("w_out_b", 256, (CW, 256)), ("w_out_c", 256, (CW, 256)),
              ("w_o", 512, (256, D)))
MAT_ROWS = 1440
CONV_SHARD = 3 * 128


def _w_in_words(w_in_shard):
    t = w_in_shard.T
    return _pack_words(t[:, :CWD], t[:, CWD:])


def _pack_weights(wl):
    parts = [wl[n].astype(BF16).reshape(-1, PACK_W) for n, _, _ in PACK_SPLIT]
    cw = wl["conv_w"].reshape(-1)
    hi = cw.astype(BF16)
    r1 = cw - hi.astype(F32)
    mid = r1.astype(BF16)
    lo = (r1 - mid.astype(F32)).astype(BF16)
    cterms = jnp.pad(jnp.concatenate([hi, mid, lo]), (0, 3 * PACK_W - 3 * CONV_SHARD)).reshape(3, PACK_W)
    tail = jnp.pad(cterms, ((0, PACK_ROWS - MAT_ROWS - 3), (0, 0)))
    return jnp.concatenate(parts + [tail], axis=0)


def _unpack_weights(gath):
    out = {}
    r = 0
    for n, nrows, shp in PACK_SPLIT:
        t = gath[:, r:r + nrows].reshape((4,) + shp)
        r += nrows
        if n == "w_o":
            out[n] = t.reshape(4 * shp[0], shp[1])
        else:
            out[n] = t.transpose(1, 0, 2).reshape(shp[0], 4 * shp[1])
    ct = gath[:, r:r + 3].reshape(4, 3 * PACK_W)[:, :3 * CONV_SHARD].astype(F32).reshape(4, 3, CONV_SHARD)
    cw = (ct[:, 0] + ct[:, 1]) + ct[:, 2]
    out["conv_w"] = cw.reshape(4, 3, 128).transpose(1, 0, 2).reshape(3, CW)
    return out


def _pack_grads(g):
    parts = []
    for n, nrows, shp in PACK_SPLIT:
        t = g[n]
        if n == "w_o":
            t = t.reshape((4,) + shp)
        else:
            t = t.reshape(shp[0], 4, shp[1]).transpose(1, 0, 2)
        parts.append(t.reshape(4, nrows, PACK_W))
    cw = g["conv_w"].reshape(3, 4, 128).transpose(1, 0, 2).reshape(4, 1, CONV_SHARD)
    parts.append(jnp.pad(cw, ((0, 0), (0, PACK_ROWS - MAT_ROWS - 1), (0, PACK_W - CONV_SHARD))))
    return jnp.concatenate(parts, axis=1)


def _unpack_grads(red):
    out = {}
    r = 0
    for n, nrows, shp in PACK_SPLIT:
        out[n] = red[r:r + nrows].reshape(shp)
        r += nrows
    out["conv_w"] = red[r, :CONV_SHARD].reshape(3, 128)
    return out


SMALL_SIZES = (("norm_g", D), ("b_gate", 3 * D), ("conv_b", CW), ("q_a_norm_g", QL), ("kv_a_norm_g", KVL),
               ("mla_q_norm_g", QK), ("mla_k_norm_g", QK), ("dil_q_norm_g", NG * HD), ("dil_k_norm_g", NG * HD))
SMALL_ROWS = 88


def _pack_small(per_name):
    flat = jnp.concatenate([per_name[n].reshape(-1).astype(F32) for n, _ in SMALL_SIZES])
    return jnp.pad(flat, (0, SMALL_ROWS * LANE - flat.shape[0])).reshape(SMALL_ROWS, LANE)


def _unpack_small(packed, like):
    out = {}
    flat = packed.reshape(-1)
    r = 0
    for n, sz in SMALL_SIZES:
        out[n] = flat[r:r + NL * sz].reshape(like[n].shape)
        r += NL * sz
    return out


def _adamw_math(w, g, m, v):
    m = ADAM_B1 * m + (1.0 - ADAM_B1) * g
    v = ADAM_B2 * v + (1.0 - ADAM_B2) * jnp.square(g)
    m_hat = m / (1.0 - ADAM_B1 ** ADAM_STEP)
    v_hat = v / (1.0 - ADAM_B2 ** ADAM_STEP)
    delta = -ADAM_LR * (m_hat / (jnp.sqrt(v_hat) + ADAM_EPS) + ADAM_WD * w)
    return delta, m, v


def _adamw(name, w, g, m, v, br, bc=None):
    L, R, C = w.shape
    bc = C if bc is None else bc
    blk = lambda l, i, j: (l, i, j)
    return _pcall(name, _adamw_math, (L, R // br, C // bc), [(t, (None, br, bc), blk) for t in (w, g, m, v)],
                  [((L, R, C), F32, (None, br, bc), blk)] * 3)


ADAM_ROWS = {"w_uq": 256, "w_ukv": 128, "w_out_a": 512, "w_out_b": 512, "w_out_c": 512, "w_o": 256,
             "conv_w": 3}


def kernel(x, norm_g, w_in, b_gate, conv_w, conv_b, q_a_norm_g, w_uq, kv_a_norm_g, w_ukv, mla_q_norm_g, mla_k_norm_g, dil_q_norm_g, dil_k_norm_g, w_out_a, w_out_b, w_out_c, w_o, loss_target, m_norm_g, m_w_in, m_b_gate, m_conv_w, m_conv_b, m_q_a_norm_g, m_w_uq, m_kv_a_norm_g, m_w_ukv, m_mla_q_norm_g, m_mla_k_norm_g, m_dil_q_norm_g, m_dil_k_norm_g, m_w_out_a, m_w_out_b, m_w_out_c, m_w_o, v_norm_g, v_w_in, v_b_gate, v_conv_w, v_conv_b, v_q_a_norm_g, v_w_uq, v_kv_a_norm_g, v_w_ukv, v_mla_q_norm_g, v_mla_k_norm_g, v_dil_q_norm_g, v_dil_k_norm_g, v_w_out_a, v_w_out_b, v_w_out_c, v_w_o):
    W = dict(norm_g=norm_g, w_in=w_in, b_gate=b_gate, conv_w=conv_w, conv_b=conv_b, q_a_norm_g=q_a_norm_g, w_uq=w_uq,
             kv_a_norm_g=kv_a_norm_g, w_ukv=w_ukv, mla_q_norm_g=mla_q_norm_g, mla_k_norm_g=mla_k_norm_g,
             dil_q_norm_g=dil_q_norm_g, dil_k_norm_g=dil_k_norm_g, w_out_a=w_out_a, w_out_b=w_out_b, w_out_c=w_out_c,
             w_o=w_o)
    M = dict(norm_g=m_norm_g, w_in=m_w_in, b_gate=m_b_gate, conv_w=m_conv_w, conv_b=m_conv_b, q_a_norm_g=m_q_a_norm_g,
             w_uq=m_w_uq, kv_a_norm_g=m_kv_a_norm_g, w_ukv=m_w_ukv, mla_q_norm_g=m_mla_q_norm_g,
             mla_k_norm_g=m_mla_k_norm_g, dil_q_norm_g=m_dil_q_norm_g, dil_k_norm_g=m_dil_k_norm_g, w_out_a=m_w_out_a,
             w_out_b=m_w_out_b, w_out_c=m_w_out_c, w_o=m_w_o)
    V = dict(norm_g=v_norm_g, w_in=v_w_in, b_gate=v_b_gate, conv_w=v_conv_w, conv_b=v_conv_b, q_a_norm_g=v_q_a_norm_g,
             w_uq=v_w_uq, kv_a_norm_g=v_kv_a_norm_g, w_ukv=v_w_ukv, mla_q_norm_g=v_mla_q_norm_g,
             mla_k_norm_g=v_mla_k_norm_g, dil_q_norm_g=v_dil_q_norm_g, dil_k_norm_g=v_dil_k_norm_g, w_out_a=v_w_out_a,
             w_out_b=v_w_out_b, w_out_c=v_w_out_c, w_o=v_w_o)
    batch = x.shape[0]
    T = batch * S

    def layer_weights(l, cont, gath):
        full = _unpack_weights(gath)
        pad_qk = lambda t: jnp.pad(t, (0, QKP - QK)).reshape(1, QKP)
        full.update(
            w_in_t=cont,
            norm_g=norm_g[l].reshape(1, D), b_gate=b_gate[l].reshape(1, 3 * D), conv_b=conv_b[l].reshape(1, CW),
            q_a_norm_g=q_a_norm_g[l].reshape(1, QL), kv_a_norm_g=kv_a_norm_g[l].reshape(1, KVL),
            mla_q_norm_g=pad_qk(mla_q_norm_g[l]), mla_k_norm_g=pad_qk(mla_k_norm_g[l]),
            dil_q_norm_g=dil_q_norm_g[l].reshape(NG, 1, HD), dil_k_norm_g=dil_k_norm_g[l].reshape(NG, 1, HD))
        return full

    words = [_w_in_words(w_in[l]) for l in range(NL)]
    packs = [_pack_weights({n: W[n][l] for n in BIG[1:] + ("conv_w",)}) for l in range(NL)]
    tabs = _rope_tables() + (_dil_slopes(),)
    x2 = x.reshape(T, D)

    cont0, gath0 = _all_gather(words[0], packs[0])
    w0 = layer_weights(0, cont0, gath0)
    ag = _ag_behind_start(words[1], packs[1], gath0)
    w0["norm_g"] = w0["norm_g"] + ag[6][0:1, 0:1]
    y0, res0 = _layer_fwd(x2, w0, tabs, batch)
    w1 = layer_weights(1, *_ag_behind_wait(ag[0], ag[1], ag[2], ag[3], ag[4], ag[5], y0))
    y1, res1 = _layer_fwd(y0, w1, tabs, batch)

    row = lambda i: (i, 0)
    dy, loss = _pcall("loss", _loss_math, (T // BR,),
                      [(y1, (BR, D), row), (loss_target.reshape(T, D), (BR, D), row)],
                      [((T, D), F32, (BR, D), row), ((1, 1), F32, (1, 1), lambda i: (0, 0), True)])
    loss = lax.psum(loss[0, 0], ("x", "y", "c"))

    grads = [None] * NL
    dy, grads[1] = _layer_bwd(dy, w1, res1, tabs, batch)
    st = [None] * NL
    ex = [None] * NL
    sw1 = _rs_swap_start(grads[1]["w_in_t"], _pack_grads(grads[1]), "1")
    w0["w_o"] = w0["w_o"] + sw1[6][0:1, 0:1].astype(BF16)

    def exchange_layer1(t):
        st[1] = _rs_sums(*_rs_swap_wait(*sw1[:6], t, "1"))
        ex[1] = _rs_exchange_start(st[1]["sw"], st[1]["ss"], "1")
        return ex[1][6]

    red = [None] * NL

    def finish(l, after):
        r2w, r2s = _rs_exchange_wait(*ex[l][:6], after, str(l))
        rw, rs = _rs_end(st[l], r2w, r2s)
        red[l] = dict(_unpack_grads(rs), w_in_t=rw)
        return rw

    def start_layer0(g):
        sw0 = _rs_swap_start(g["w_in_t"], _pack_grads(g), "0")
        done1 = finish(1, sw0[6])
        st[0] = _rs_sums(*_rs_swap_wait(*sw0[:6], done1, "0"))
        ex[0] = _rs_exchange_start(st[0]["sw"], st[0]["ss"], "0")
        return ex[0][6]

    dx, grads[0] = _layer_bwd(dy, w0, res0, tabs, batch, after_dw=start_layer0, after_merge=exchange_layer1)
    grad_x = dx.reshape(batch, S, D)
    small_g = {n: jnp.stack([grads[l][n].reshape(-1)[:sz] for l in range(NL)]) for n, sz in SMALL_SIZES}
    small_red = _all_reduce_small(_pack_small(small_g))
    finish(0, small_red)

    G = {n: jnp.stack([red[l][n] for l in range(NL)]) for n in BIG[1:] + ("conv_w",)}
    g_in_t = jnp.stack([red[l]["w_in_t"] for l in range(NL)])
    G["w_in"] = jnp.swapaxes(g_in_t, 1, 2)
    G.update(_unpack_small(small_red, {n: W[n] for n in SMALL}))

    delta, new_m, new_v = {}, {}, {}
    for n in BIG[1:] + ("conv_w",):
        delta[n], new_m[n], new_v[n] = _adamw("adamw_" + n, W[n], G[n], M[n], V[n], ADAM_ROWS[n])
    tr = lambda t: jnp.swapaxes(t, 1, 2)
    delta["w_in"], new_m["w_in"], new_v["w_in"] = (
        tr(t) for t in _adamw("adamw_w_in", tr(w_in), g_in_t, tr(m_w_in), tr(v_w_in), WSH, LANE))
    sw, sm, sv = (_pack_small({n: t[n] for n in SMALL})[None] for t in (W, M, V))
    sd, snm, snv = _adamw("adamw_small", sw, small_red[None], sm, sv, SMALL_ROWS)
    like = {n: W[n] for n in SMALL}
    delta.update(_unpack_small(sd[0], like))
    new_m.update(_unpack_small(snm[0], like))
    new_v.update(_unpack_small(snv[0], like))

    return (loss, grad_x, *[G[n] for n in WEIGHTS], *[delta[n] for n in WEIGHTS],
            *[new_m[n] for n in WEIGHTS], *[new_v[n] for n in WEIGHTS])
```

```python
import functools

import numpy as np
import jax
import jax.numpy as jnp
from jax import lax
from jax.experimental import pallas as pl
from jax.experimental.pallas import tpu as pltpu

F32 = jnp.float32
BF16 = jnp.bfloat16

D = 1024
S = 2048
NL = 2
CW = 512
NH = 8
QL = 256
KVL = 128
NOPE = 64
ROPE = 32
VD = 64
QK = NOPE + ROPE
QKP = 128
ROPE_THETA = 10000.0
DIL = ((128, 1), (512, 4), (2048, 16))
NG = 3
DH = 8
HD = 64
DWID = DH * HD
QB = 128
EPS = 1e-6
NIN = 11168
NINP = 11264
O_A, O_CQ, O_CKV, O_KPE, O_BZ, O_DQ, O_DK, O_DV, O_CZ, O_G = 0, 2048, 2304, 2432, 2560, 3072, 4608, 6144, 7680, 8192
KPE_END = 2464
NEG = -1e30
MLA_SCALE = QK ** -0.5
DIL_SCALE = HD ** -0.5
LANE = 128
PACK_W = 512
VMEM_LIMIT = 48 * 1024 * 1024

ADAM_LR = 0.001
ADAM_B1 = 0.9
ADAM_B2 = 0.999
ADAM_EPS = 1e-08
ADAM_WD = 0.01
ADAM_STEP = 10

MESH = pl.DeviceIdType.MESH
BIG = ("w_in", "w_uq", "w_ukv", "w_out_a", "w_out_b", "w_out_c", "w_o")
SMALL = ("norm_g", "b_gate", "conv_b", "q_a_norm_g", "kv_a_norm_g", "mla_q_norm_g", "mla_k_norm_g",
         "dil_q_norm_g", "dil_k_norm_g")
WEIGHTS = ("norm_g", "w_in", "b_gate", "conv_w", "conv_b", "q_a_norm_g", "w_uq", "kv_a_norm_g", "w_ukv",
           "mla_q_norm_g", "mla_k_norm_g", "dil_q_norm_g", "dil_k_norm_g", "w_out_a", "w_out_b", "w_out_c", "w_o")


def _dot(a, b):
    return jnp.dot(a, b, preferred_element_type=F32)


def _dot_nt(a, b):
    return lax.dot_general(a, b, (((1,), (1,)), ((), ())), preferred_element_type=F32)


def _dot_tn(a, b):
    return lax.dot_general(a, b, (((0,), (0,)), ((), ())), preferred_element_type=F32)


def _grid_step(grid):
    step = pl.program_id(0)
    for a in range(1, len(grid)):
        step = step * grid[a] + pl.program_id(a)
    n = 1
    for g in grid:
        n *= g
    return step, n


def _write_windows(buf_ref, stages, sems, step, nsteps, puts):
    slot = step % 2
    for t, (v, dst) in enumerate(puts):
        cp = pltpu.make_async_copy(stages[t].at[slot], dst, sems.at[t, slot])

        @pl.when(step >= 2)
        def _():
            cp.wait()

        stages[t][slot] = v.astype(stages[t].dtype).reshape(stages[t].shape[1:])
        cp.start()

    @pl.when(step == nsteps - 1)
    def _():
        for t, (v, dst) in enumerate(puts):
            pltpu.make_async_copy(stages[t].at[slot], dst, sems.at[t, slot]).wait()
            if nsteps > 1:
                pltpu.make_async_copy(stages[t].at[1 - slot], dst, sems.at[t, 1 - slot]).wait()


def _pcall(name, fn, grid, ins, outs, into=None):
    n_in = len(ins)
    n_out = len(outs)
    acc_axis = len(grid) - 1
    is_acc = [len(o) > 4 and o[4] for o in outs]
    outs = [o[:4] for o in outs]
    targets = into[1] if into is not None else []
    n_t = len(targets)

    def body(*refs):
        vals = fn(*[r[...].astype(F32) for r in refs[:n_in]])
        if not isinstance(vals, (tuple, list)):
            vals = (vals,)
        o0 = n_in + (1 if n_t else 0)
        for k in range(n_out):
            r = refs[o0 + k]
            v = vals[k].astype(r.dtype).reshape(r.shape)
            if is_acc[k]:
                first = pl.program_id(acc_axis) == 0

                @pl.when(first)
                def _():
                    r[...] = v

                @pl.when(jnp.logical_not(first))
                def _():
                    r[...] += v
            else:
                r[...] = v
        if n_t:
            buf_ref = refs[o0 + n_out]
            stages = refs[o0 + n_out + 1:o0 + n_out + 1 + n_t]
            ids = [pl.program_id(a) for a in range(len(grid))]
            step, nsteps = _grid_step(grid)
            _write_windows(buf_ref, stages, refs[-1], step, nsteps,
                           [(vals[n_out + t], targets[t][1](buf_ref, *ids)) for t in range(n_t)])

    in_specs = [pl.BlockSpec(bs, im) for _, bs, im in ins]
    out_specs = [pl.BlockSpec(bs, im) for _, _, bs, im in outs]
    out_shape = [jax.ShapeDtypeStruct(sh, dt) for sh, dt, _, _ in outs]
    args = [a for a, _, _ in ins]
    extra = {}
    if n_t:
        buf = into[0]
        in_specs.append(pl.BlockSpec(memory_space=pl.ANY))
        out_specs.append(pl.BlockSpec(memory_space=pl.ANY))
        out_shape.append(jax.ShapeDtypeStruct(buf.shape, buf.dtype))
        args.append(buf)
        extra = dict(input_output_aliases={n_in: n_out},
                     scratch_shapes=[pltpu.VMEM((2,) + tuple(bs), buf.dtype) for bs, _ in targets]
                     + [pltpu.SemaphoreType.DMA((n_t, 2))])
    return pl.pallas_call(
        body,
        grid=grid,
        in_specs=in_specs,
        out_specs=out_specs,
        out_shape=out_shape,
        name=name,
        compiler_params=pltpu.CompilerParams(
            dimension_semantics=("arbitrary",) * len(grid), vmem_limit_bytes=VMEM_LIMIT),
        **extra,
    )(*args)


def _mm(name, a, b, *, ta=False, tb=False, out_dtype=F32, add=None, dep=None, b_words=False, tm=2048, tn=1024, tk=1024):
    if ta:
        K, M = a.shape
    else:
        M, K = a.shape
    bshape = (b.shape[0], 2 * b.shape[1]) if b_words else b.shape
    if tb:
        N, K2 = bshape
    else:
        K2, N = bshape
    assert K == K2, (name, a.shape, b.shape)
    tm, tn, tk = min(tm, M), min(tn, N), min(tk, K)
    assert M % tm == 0 and N % tn == 0 and K % tk == 0, (name, M, N, K)
    nk = K // tk
    dims = (((0 if ta else 1,), (1 if tb else 0,)), ((), ()))
    a_spec = pl.BlockSpec((tk, tm), lambda j, i, k: (k, i)) if ta else pl.BlockSpec((tm, tk), lambda j, i, k: (i, k))
    bw = 2 if b_words else 1
    assert not b_words or (tk if tb else tn) == bshape[1]
    b_spec = (pl.BlockSpec((tn, tk // bw), lambda j, i, k: (j, k)) if tb
              else pl.BlockSpec((tk, tn // bw), lambda j, i, k: (k, j)))
    o_spec = pl.BlockSpec((tm, tn), lambda j, i, k: (i, j))
    has_add = add is not None
    n_in = 2 + has_add + (dep is not None)

    def body(*refs):
        a_ref, b_ref = refs[0], refs[1]
        add_ref = refs[2] if has_add else None
        o_ref = refs[n_in]
        bb = b_ref[...]
        if b_words:
            lo, hi = _unpack_words(bb)
            first = (pl.program_id(0) * tn) if tb else (pl.program_id(2) * tk)
            r = first + lax.broadcasted_iota(jnp.int32, lo.shape, 0)
            pad = jnp.logical_and(r >= KPE_END, r < KPE_END + NINP - NIN)
            bb = jnp.concatenate([jnp.where(pad, 0.0, lo), jnp.where(pad, 0.0, hi)], axis=1)
        p = lax.dot_general(a_ref[...].astype(BF16), bb.astype(BF16), dims, preferred_element_type=F32)
        if nk == 1:
            if has_add:
                p = p + add_ref[...]
            o_ref[...] = p.astype(out_dtype)
        else:
            acc = refs[-1]
            k = pl.program_id(2)

            @pl.when(k == 0)
            def _():
                acc[...] = p

            @pl.when(k > 0)
            def _():
                acc[...] += p

            @pl.when(k == nk - 1)
            def _():
                r = acc[...]
                if has_add:
                    r = r + add_ref[...]
                o_ref[...] = r.astype(out_dtype)

    in_specs = [a_spec, b_spec] + ([o_spec] if has_add else []) + ([pl.BlockSpec(memory_space=pl.ANY)] if dep is not None else [])
    args = [a, b] + ([add] if has_add else []) + ([dep] if dep is not None else [])
    return pl.pallas_call(
        body,
        grid=(N // tn, M // tm, nk),
        in_specs=in_specs,
        out_specs=o_spec,
        out_shape=jax.ShapeDtypeStruct((M, N), out_dtype),
        scratch_shapes=[pltpu.VMEM((tm, tn), F32)] if nk > 1 else [],
        name=name,
        compiler_params=pltpu.CompilerParams(
            dimension_semantics=("arbitrary", "arbitrary", "arbitrary"), vmem_limit_bytes=VMEM_LIMIT),
    )(*args)


def _vjp_of(f, n_diff):
    def g(*args, n_prim):
        prim = args[:n_diff]
        consts = args[n_diff:n_prim]
        cts = args[n_prim:]
        _, pull = jax.vjp(lambda *p: f(*p, *consts), *prim)
        out = jax.eval_shape(lambda *p: f(*p, *consts), *prim)
        if isinstance(out, (tuple, list)):
            cts = tuple(c.astype(o.dtype) for c, o in zip(cts, out))
        else:
            cts = cts[0].astype(out.dtype)
        return pull(cts)
    return g


def _rms(x, g, n=None):
    n = x.shape[-1] if n is None else n
    ms = jnp.sum(x * x, axis=-1, keepdims=True) / n
    return x * lax.rsqrt(ms + EPS) * g


def _silu(z):
    return z * jax.nn.sigmoid(z)


def _roll_rows(u, k):
    n = u.shape[0]
    r = pltpu.roll(u, k % n, 0)
    t = lax.broadcasted_iota(jnp.int32, u.shape, 0)
    if k > 0:
        return jnp.where(t >= k, r, 0.0)
    return jnp.where(t < n + k, r, 0.0)


@functools.partial(jax.custom_vjp, nondiff_argnums=(1,))
def _shift(u, k):
    return _roll_rows(u, k)


def _shift_fwd(u, k):
    return _roll_rows(u, k), None


def _shift_bwd(k, _, g):
    return (_roll_rows(g, -k),)


_shift.defvjp(_shift_fwd, _shift_bwd)


@functools.partial(jax.custom_vjp, nondiff_argnums=(1,))
def _lane_roll(u, k):
    return pltpu.roll(u, k % LANE, 1)


def _lane_roll_fwd(u, k):
    return pltpu.roll(u, k % LANE, 1), None


def _lane_roll_bwd(k, _, g):
    return (pltpu.roll(g, (-k) % LANE, 1),)


_lane_roll.defvjp(_lane_roll_fwd, _lane_roll_bwd)


def _conv_math(ab, ac, ax, az, cw, cb):
    u = ac * ax
    conv = cb + _shift(u, 2) * cw[0:1] + _shift(u, 1) * cw[1:2] + u * cw[2:3]
    return ab * conv * _silu(az)


def _mla_pre_math(cq, ckv, gq, gkv):
    return _rms(cq, gq), _rms(ckv, gkv)


def _rope_math(q, kn, kpe, gq, gk, c, s1, s2):
    lane = lax.broadcasted_iota(jnp.int32, kpe.shape, 1)
    pe = _lane_roll(jnp.where(lane < ROPE, kpe, 0.0), NOPE)

    def one(t, g):
        tn = _rms(t, g, QK)
        return tn * c + _lane_roll(tn, -16) * s1 + _lane_roll(tn, 16) * s2

    qs, ks = [], []
    for h in range(NH):
        sl = slice(h * QKP, (h + 1) * QKP)
        qs.append(one(q[:, sl], gq))
        ks.append(one(kn[:, sl] + pe, gk))
    return jnp.concatenate(qs, axis=1), jnp.concatenate(ks, axis=1)


def _gate_math(o, z):
    return o * _silu(z)


def _mergec_math(o0, o1, o2, l0, l1, l2, cz):
    m = lax.stop_gradient(jnp.maximum(jnp.maximum(l0, l1), l2))
    e0, e1, e2 = jnp.exp(l0 - m), jnp.exp(l1 - m), jnp.exp(l2 - m)
    den = e0 + e1 + e2
    oc = (e0 / den) * o0 + (e1 / den) * o1 + (e2 / den) * o2
    return oc * _silu(cz)


def _merge_math(g0, g1, g2, b0, b1, b2, pa, pb, pc):
    return (jax.nn.sigmoid(g0 + b0) * pa + jax.nn.sigmoid(g1 + b1) * pb) + jax.nn.sigmoid(g2 + b2) * pc


MLA_T = 256
MLA_UNROLL = True


def _mla_fwd(q, k, v):
    B = q.shape[0]
    T = MLA_T
    NB = S // T

    def body(q_ref, k_ref, v_ref, o_ref, l_ref):
        row = lax.broadcasted_iota(jnp.int32, (T, T), 0)
        col = lax.broadcasted_iota(jnp.int32, (T, T), 1)
        lo = _lo_mask((T, LANE))

        for qi in range(NB):
            qb = q_ref[qi * T:(qi + 1) * T, :]

            def step(j, carry, diagonal):
                m, l, acc = carry
                off = pl.multiple_of(j * T, T)
                kb = k_ref[pl.ds(off, T), :]
                vb = v_ref[pl.ds(off, T), :]
                ss = []
                for e in (0, 1):
                    se = _dot_nt(qb[:, e * QKP:(e + 1) * QKP], kb[:, e * QKP:(e + 1) * QKP]) * MLA_SCALE
                    ss.append(jnp.where(col <= row, se, NEG) if diagonal else se)
                s = jnp.concatenate(ss, axis=0)
                m_new = jnp.maximum(m, jnp.max(s, axis=-1, keepdims=True))
                a = jnp.exp(m - m_new)
                p = jnp.exp(s - m_new)
                l = a * l + jnp.sum(p, axis=-1, keepdims=True)
                acc = a * acc + _dot(p.astype(BF16), vb)
                return m_new, l, acc

            init = (jnp.full((2 * T, 1), NEG, F32), jnp.zeros((2 * T, 1), F32), jnp.zeros((2 * T, LANE), F32))
            carry = lax.fori_loop(0, qi, functools.partial(step, diagonal=False), init, unroll=MLA_UNROLL)
            m, l, acc = step(qi, carry, True)
            o = acc / l
            lse = m + jnp.log(l)
            o_ref[qi * T:(qi + 1) * T, :] = jnp.where(lo, o[:T], o[T:])
            l_ref[qi * T:(qi + 1) * T, :] = jnp.where(lo, lse[:T], lse[T:])

    def spec(w):
        return pl.BlockSpec((None, S, w), lambda b, hp: (b, 0, hp))

    return pl.pallas_call(
        body,
        grid=(B, NH // 2),
        in_specs=[spec(2 * QKP), spec(2 * QKP), spec(LANE)],
        out_specs=[spec(LANE), spec(LANE)],
        out_shape=[jax.ShapeDtypeStruct((B, S, NH * VD), F32)] * 2,
        name="mla_attn_fwd",
        compiler_params=pltpu.CompilerParams(dimension_semantics=("arbitrary",) * 2, vmem_limit_bytes=VMEM_LIMIT),
    )(q, k, v)


def _mla_bwd(q, k, v, do, o, lse):
    B = q.shape[0]
    T = MLA_T
    NB = S // T

    def body(q_ref, k_ref, v_ref, do_ref, o_ref, l_ref, dq_ref, dk_ref, dv_ref, delta_ref, dqt_ref):
        delta_ref[...] = _head_sum(do_ref[...] * o_ref[...])
        row = lax.broadcasted_iota(jnp.int32, (T, T), 0)
        col = lax.broadcasted_iota(jnp.int32, (T, T), 1)
        lo = _lo_mask((T, LANE))
        tn_t = (((0,), (1,)), ((), ()))

        for j in range(NB):
            krows = slice(j * T, (j + 1) * T)
            kb = k_ref[krows, :]
            vb = v_ref[krows, :]
            dkt = [jnp.zeros((QKP, T), F32), jnp.zeros((QKP, T), F32)]
            dvt = jnp.zeros((LANE, T), F32)
            for i in range(j, NB):
                qrows = slice(i * T, (i + 1) * T)
                qb = q_ref[qrows, :]
                do2 = _stack_heads(do_ref[qrows, :], lo).astype(BF16)
                lb = l_ref[qrows, :]
                db = delta_ref[qrows, :]
                dp2 = _dot_nt(do2, vb)
                ps = []
                for e in (0, 1):
                    cols = slice(e * QKP, (e + 1) * QKP)
                    qe, ke = qb[:, cols], kb[:, cols]
                    s = _dot_nt(qe, ke) * MLA_SCALE
                    if i == j:
                        s = jnp.where(col <= row, s, NEG)
                    p = jnp.exp(s - lb[:, e * HD:e * HD + 1])
                    ps.append(p.astype(BF16))
                    ds = (p * (dp2[e * T:(e + 1) * T] - db[:, e * HD:e * HD + 1]) * MLA_SCALE).astype(BF16)
                    dkt[e] = dkt[e] + _dot_tn(qe, ds)
                    dq_t = lax.dot_general(ke, ds, tn_t, preferred_element_type=F32)
                    if j == 0:
                        dqt_ref[e, :, qrows] = dq_t
                    else:
                        dqt_ref[e, :, qrows] += dq_t
                dvt = dvt + _dot_tn(do2, jnp.concatenate(ps, axis=0))
            dk_ref[krows, 0:QKP] = dkt[0].T
            dk_ref[krows, QKP:2 * QKP] = dkt[1].T
            dv_ref[krows, :] = dvt.T
        dq_ref[:, 0:QKP] = dqt_ref[0].T
        dq_ref[:, QKP:2 * QKP] = dqt_ref[1].T

    def spec(w):
        return pl.BlockSpec((None, S, w), lambda b, hp: (b, 0, hp))

    return pl.pallas_call(
        body,
        grid=(B, NH // 2),
        in_specs=[spec(2 * QKP), spec(2 * QKP), spec(LANE), spec(LANE), spec(LANE), spec(LANE)],
        out_specs=[spec(2 * QKP), spec(2 * QKP), spec(LANE)],
        out_shape=[jax.ShapeDtypeStruct((B, S, NH * QKP), F32), jax.ShapeDtypeStruct((B, S, NH * QKP), F32),
                   jax.ShapeDtypeStruct((B, S, NH * VD), F32)],
        scratch_shapes=[pltpu.VMEM((S, LANE), F32), pltpu.VMEM((2, QKP, S), F32)],
        name="mla_attn_bwd",
        compiler_params=pltpu.CompilerParams(dimension_semantics=("arbitrary",) * 2, vmem_limit_bytes=VMEM_LIMIT),
    )(q, k, v, do, o, lse)


def _lo_mask(shape):
    return lax.broadcasted_iota(jnp.int32, shape, len(shape) - 1) < HD


def _head_sum(u):
    r = lax.broadcasted_iota(jnp.int32, (LANE, LANE), 0) < HD
    c = lax.broadcasted_iota(jnp.int32, (LANE, LANE), 1) < HD
    ones = jnp.where(r == c, 1.0, 0.0).astype(BF16)
    hi = u.astype(BF16)
    lo = (u - hi.astype(F32)).astype(BF16)
    return _dot(hi, ones) + _dot(lo, ones)


def _head_sum_1(u):
    r = lax.broadcasted_iota(jnp.int32, (LANE, LANE), 0) < HD
    c = lax.broadcasted_iota(jnp.int32, (LANE, LANE), 1) < HD
    return _dot(u.astype(BF16), jnp.where(r == c, 1.0, 0.0).astype(BF16))


def _rms2_scale(x):
    return lax.rsqrt(_head_sum(x * x) / HD + EPS)


def _rms2(x, g):
    return x * _rms2_scale(x) * g


def _rms2_bwd(x, r, g, dy):
    xn = x * r
    t = dy * g
    dx = r * (t - xn * (_head_sum_1(xn * t) * (1.0 / HD)))
    return dx, jnp.sum(dy * xn, axis=0, keepdims=True)


def _dil_bias(t_ref, gi, d):
    qq = lax.broadcasted_iota(jnp.int32, (QB, QB), 0)
    kk = lax.broadcasted_iota(jnp.int32, (QB, QB), 1)
    jc = (qq - kk).astype(F32)
    rows = []
    for e in (0, 1):
        sl = t_ref[2 * gi + e:2 * gi + e + 1, :] * float(d)
        bp = jnp.where(kk >= qq, -sl * (jc + float(QB)), NEG)
        bc = jnp.where(kk <= qq, -sl * jc, NEG)
        rows.append(jnp.concatenate([bp, bc], axis=1))
    return jnp.concatenate(rows, axis=0)


def _dil_rows(cur, d):
    return pl.ds(cur, QB, stride=d) if d > 1 else pl.ds(pl.multiple_of(cur, QB), QB)


def _dil_walk(d, block, full):
    if d == 1:
        block(0, None)

        def body(i, c):
            block(i * QB, (i - 1) * QB)
            return c
        lax.fori_loop(1, S // QB, body, 0, unroll=True if full else 5)
    elif d == 16:
        def body(r, c):
            block(r, None)
            return c
        lax.fori_loop(0, d, body, 0, unroll=True if full else 4)
    else:
        nb = S // d // QB

        def cls(r, c):
            block(r, None)

            def body(i, c2):
                block(r + i * QB * d, r + (i - 1) * QB * d)
                return c2
            lax.fori_loop(1, nb, body, 0, unroll=True)
            return c
        lax.fori_loop(0, d, cls, 0, unroll=full)


def _stack_heads(x, lo):
    return jnp.concatenate([jnp.where(lo, x, 0.0), jnp.where(lo, 0.0, x)], axis=0)


def _dilc_fwd(proj3, gq, gk, tab):
    B = proj3.shape[0]

    def body(q_ref, k_ref, v_ref, cz_ref, gq_ref, gk_ref, t_ref, y_ref, o_ref, l_ref, qs, ks, vs):
        g = pl.program_id(2)
        lo = _lo_mask((QB, LANE))

        def group(gi):
            d = DIL[gi][1]
            qs[...] = _rms2(q_ref[...].astype(F32), gq_ref[gi:gi + 1, :])
            ks[...] = _rms2(k_ref[...].astype(F32), gk_ref[gi:gi + 1, :])
            vs[...] = v_ref[...].astype(F32)
            bias = _dil_bias(t_ref, gi, d)

            def block(cur, prev):
                rows = _dil_rows(cur, d)
                q2 = _stack_heads(qs[rows, :], lo).astype(BF16)
                kc, vc = ks[rows, :], vs[rows, :]
                if prev is None:
                    kcat, vcat, b = kc, vc, bias[:, QB:]
                else:
                    prow = _dil_rows(prev, d)
                    kcat = jnp.concatenate([ks[prow, :], kc], axis=0)
                    vcat = jnp.concatenate([vs[prow, :], vc], axis=0)
                    b = bias
                s = _dot_nt(q2, kcat.astype(BF16)) * DIL_SCALE + b
                m = jnp.max(s, axis=-1, keepdims=True)
                p = jnp.exp(s - m)
                l = jnp.sum(p, axis=-1, keepdims=True)
                o = _dot(p.astype(BF16), vcat.astype(BF16)) / l
                lse = m + jnp.log(l)
                o_ref[gi, rows, :] = jnp.where(lo, o[:QB], o[QB:])
                l_ref[gi, rows, :] = jnp.where(lo, lse[:QB], lse[QB:])

            _dil_walk(d, block, True)

        for gi in range(NG):
            pl.when(g == gi)(functools.partial(group, gi))

        @pl.when(g == NG - 1)
        def _():
            y_ref[...] = _mergec_math(o_ref[0], o_ref[1], o_ref[2], l_ref[0], l_ref[1], l_ref[2],
                                      cz_ref[...].astype(F32)).astype(BF16)

    def col(base):
        return pl.BlockSpec((None, S, LANE), lambda b, hp, g: (b, 0, base // LANE + 4 * g + hp))

    gspec = pl.BlockSpec((NG, LANE), lambda b, hp, g: (0, 0))
    saved = pl.BlockSpec((NG, None, S, LANE), lambda b, hp, g: (0, b, 0, hp))
    return pl.pallas_call(
        body,
        grid=(B, 4, NG),
        in_specs=[col(O_DQ), col(O_DK), col(O_DV),
                  pl.BlockSpec((None, S, LANE), lambda b, hp, g: (b, 0, O_CZ // LANE + hp)),
                  gspec, gspec, pl.BlockSpec((None, 8, LANE), lambda b, hp, g: (hp, 0, 0))],
        out_specs=[pl.BlockSpec((None, S, LANE), lambda b, hp, g: (b, 0, hp)), saved, saved],
        out_shape=[jax.ShapeDtypeStruct((B, S, DWID), BF16), jax.ShapeDtypeStruct((NG, B, S, DWID), F32),
                   jax.ShapeDtypeStruct((NG, B, S, DWID), F32)],
        scratch_shapes=[pltpu.VMEM((S, LANE), F32)] * 3,
        name="dil_mixer_fwd",
        compiler_params=pltpu.CompilerParams(dimension_semantics=("arbitrary",) * 3, vmem_limit_bytes=VMEM_LIMIT),
    )(proj3, proj3, proj3, proj3, gq, gk, tab)


MERGE_ROWS = 256


def _dilc_bwd(proj3, gq, gk, tab, o_all, l_all, d_yc, dproj3):
    B = proj3.shape[0]

    def body(q_ref, k_ref, v_ref, cz_ref, gq_ref, gk_ref, t_ref, o_ref, l_ref, dy_ref, dp_in,
             dp_out, dgq_out, dgk_out, qs, ks, vs, dos, dls, dqs, dks, dvs, rqs, rks, dczs,
             st_q, st_k, st_v, st_z, sems, sem_z):
        b_, hp, g = pl.program_id(0), pl.program_id(1), pl.program_id(2)
        col = lambda base: pl.ds(pl.multiple_of(base + hp * LANE, LANE), LANE)
        lo = _lo_mask((QB, LANE))

        @pl.when(jnp.logical_and(jnp.logical_and(pl.program_id(0) == 0, pl.program_id(1) == 0), g == 0))
        def _():
            dgq_out[...] = jnp.zeros((NG, LANE), F32)
            dgk_out[...] = jnp.zeros((NG, LANE), F32)

        @pl.when(g == 0)
        def _():
            def chunk(i, carry):
                rows = pl.ds(pl.multiple_of(i * MERGE_ROWS, MERGE_ROWS), MERGE_ROWS)
                ls = [l_ref[j, rows, :] for j in range(NG)]
                m = jnp.maximum(jnp.maximum(ls[0], ls[1]), ls[2])
                es = [jnp.exp(t - m) for t in ls]
                den = (es[0] + es[1]) + es[2]
                al = [e / den for e in es]
                os_ = [o_ref[j, rows, :] for j in range(NG)]
                oc = (al[0] * os_[0] + al[1] * os_[1]) + al[2] * os_[2]
                cz = cz_ref[rows, :].astype(F32)
                sg = jax.nn.sigmoid(cz)
                dy = dy_ref[rows, :]
                d_oc = dy * (cz * sg)
                dczs[rows, :] = (dy * oc * (sg * (1.0 + cz * (1.0 - sg)))).astype(BF16)
                ts = [_head_sum_1(d_oc * os_[j]) for j in range(NG)]
                tbar = (al[0] * ts[0] + al[1] * ts[1]) + al[2] * ts[2]
                for j in range(NG):
                    dos[j, rows, :] = al[j] * d_oc
                    dls[j, rows, :] = al[j] * (ts[j] - tbar)
                return carry
            lax.fori_loop(0, S // MERGE_ROWS, chunk, 0)
            _write_windows(dp_out, [st_z], sem_z, b_ * 4 + hp, B * 4, [(dczs[...], dp_out.at[b_, :, col(O_CZ)])])

        def group(gi):
            d = DIL[gi][1]
            xq, xk = q_ref[...].astype(F32), k_ref[...].astype(F32)
            rqs[...] = _rms2_scale(xq)
            rks[...] = _rms2_scale(xk)
            qs[...] = xq * rqs[...] * gq_ref[gi:gi + 1, :]
            ks[...] = xk * rks[...] * gk_ref[gi:gi + 1, :]
            vs[...] = v_ref[...].astype(F32)
            dks[...] = jnp.zeros((S, LANE), F32)
            dvs[...] = jnp.zeros((S, LANE), F32)
            bias = _dil_bias(t_ref, gi, d)

            def block(cur, prev):
                rows = _dil_rows(cur, d)
                q2 = _stack_heads(qs[rows, :], lo).astype(BF16)
                dob = dos[gi, rows, :]
                do2 = _stack_heads(dob, lo).astype(BF16)
                kc, vc = ks[rows, :], vs[rows, :]
                if prev is None:
                    kcat, vcat, b = kc, vc, bias[:, QB:]
                else:
                    prow = _dil_rows(prev, d)
                    kcat = jnp.concatenate([ks[prow, :], kc], axis=0)
                    vcat = jnp.concatenate([vs[prow, :], vc], axis=0)
                    b = bias
                kcat = kcat.astype(BF16)
                vcat = vcat.astype(BF16)
                lse_b = l_ref[gi, rows, :]
                corr_b = dls[gi, rows, :] - _head_sum_1(dob * o_ref[gi, rows, :])
                lse2 = jnp.concatenate([lse_b[:, 0:1], lse_b[:, HD:HD + 1]], axis=0)
                corr2 = jnp.concatenate([corr_b[:, 0:1], corr_b[:, HD:HD + 1]], axis=0)
                s = _dot_nt(q2, kcat) * DIL_SCALE + b
                p = jnp.exp(s - lse2)
                ds = (p * (_dot_nt(do2, vcat) + corr2) * DIL_SCALE).astype(BF16)
                dq2 = _dot(ds, kcat)
                dqs[rows, :] = jnp.where(lo, dq2[:QB], dq2[QB:])
                dk = _dot_tn(ds, q2)
                dv = _dot_tn(p.astype(BF16), do2)
                if prev is None:
                    dks[rows, :] += dk
                    dvs[rows, :] += dv
                else:
                    dks[prow, :] += dk[:QB]
                    dvs[prow, :] += dv[:QB]
                    dks[rows, :] += dk[QB:]
                    dvs[rows, :] += dv[QB:]

            _dil_walk(d, block, False)

            dxq, dgq = _rms2_bwd(q_ref[...].astype(F32), rqs[...], gq_ref[gi:gi + 1, :], dqs[...])
            dgq_out[gi:gi + 1, :] += dgq
            dxk, dgk = _rms2_bwd(k_ref[...].astype(F32), rks[...], gk_ref[gi:gi + 1, :], dks[...])
            dgk_out[gi:gi + 1, :] += dgk
            step, nsteps = _grid_step((B, 4, NG))
            _write_windows(dp_out, [st_q, st_k, st_v], sems, step, nsteps,
                           [(dxq, dp_out.at[b_, :, col(O_DQ + gi * DWID)]), (dxk, dp_out.at[b_, :, col(O_DK + gi * DWID)]),
                            (dvs[...], dp_out.at[b_, :, col(O_DV + gi * DWID)])])

        for gi in range(NG):
            pl.when(g == gi)(functools.partial(group, gi))

    def col(base):
        return pl.BlockSpec((None, S, LANE), lambda b, hp, g: (b, 0, base // LANE + 4 * g + hp))

    gspec = pl.BlockSpec((NG, LANE), lambda b, hp, g: (0, 0))
    saved = pl.BlockSpec((NG, None, S, LANE), lambda b, hp, g: (0, b, 0, hp))
    per_pair = pl.BlockSpec((None, S, LANE), lambda b, hp, g: (b, 0, hp))
    return pl.pallas_call(
        body,
        grid=(B, 4, NG),
        in_specs=[col(O_DQ), col(O_DK), col(O_DV),
                  pl.BlockSpec((None, S, LANE), lambda b, hp, g: (b, 0, O_CZ // LANE + hp)),
                  gspec, gspec, pl.BlockSpec((None, 8, LANE), lambda b, hp, g: (hp, 0, 0)),
                  saved, saved, per_pair, pl.BlockSpec(memory_space=pl.ANY)],
        out_specs=[pl.BlockSpec(memory_space=pl.ANY), gspec, gspec],
        out_shape=[jax.ShapeDtypeStruct(dproj3.shape, dproj3.dtype), jax.ShapeDtypeStruct((NG, LANE), F32),
                   jax.ShapeDtypeStruct((NG, LANE), F32)],
        input_output_aliases={10: 0},
        scratch_shapes=[pltpu.VMEM((S, LANE), F32)] * 3 + [pltpu.VMEM((NG, S, LANE), F32)] * 2
        + [pltpu.VMEM((S, LANE), F32)] * 5 + [pltpu.VMEM((S, LANE), BF16)] + [pltpu.VMEM((2, S, LANE), BF16)] * 4
        + [pltpu.SemaphoreType.DMA((3, 2)), pltpu.SemaphoreType.DMA((1, 2))],
        name="dil_mixer_bwd",
        compiler_params=pltpu.CompilerParams(dimension_semantics=("arbitrary",) * 3, vmem_limit_bytes=VMEM_LIMIT),
    )(proj3, proj3, proj3, proj3, gq, gk, tab, o_all, l_all, d_yc, dproj3)


def _dil_slopes():
    slopes = (2.0 ** (-8.0 * np.arange(1, NG * DH + 1, dtype=np.float32) / (NG * DH))).astype(np.float32).reshape(NG, DH)
    tab = np.zeros((4, 8, LANE), np.float32)
    for hp in range(4):
        for gi in range(NG):
            for e in (0, 1):
                tab[hp, 2 * gi + e, :] = slopes[gi, 2 * hp + e]
    return jnp.asarray(tab)


def _rope_tables():
    inv = ROPE_THETA ** (-jnp.arange(0, ROPE, 2, dtype=F32) / ROPE)
    ang = jnp.arange(S, dtype=F32)[:, None] * inv[None, :]
    cos, sin = jnp.cos(ang), jnp.sin(ang)
    z16 = jnp.zeros((S, 16), F32)
    c = jnp.concatenate([jnp.ones((S, NOPE), F32), cos, cos, jnp.zeros((S, 32), F32)], axis=1)
    s1 = jnp.concatenate([jnp.zeros((S, NOPE), F32), -sin, z16, jnp.zeros((S, 32), F32)], axis=1)
    s2 = jnp.concatenate([jnp.zeros((S, NOPE), F32), z16, sin, jnp.zeros((S, 32), F32)], axis=1)
    return c, s1, s2


def _pad_heads_uq(w):
    return jnp.pad(w.reshape(QL, NH, QK), ((0, 0), (0, 0), (0, QKP - QK))).reshape(QL, NH * QKP)


def _unpad_heads_uq(g):
    return g.reshape(QL, NH, QKP)[:, :, :QK].reshape(QL, NH * QK)


def _split_ukv(w):
    w3 = w.reshape(KVL, NH, NOPE + VD)
    uk = jnp.pad(w3[:, :, :NOPE], ((0, 0), (0, 0), (0, QKP - NOPE))).reshape(KVL, NH * QKP)
    return uk, w3[:, :, NOPE:].reshape(KVL, NH * VD)


def _join_ukv(guk, guv):
    return jnp.concatenate([guk.reshape(KVL, NH, QKP)[:, :, :NOPE], guv.reshape(KVL, NH, VD)],
                           axis=-1).reshape(KVL, NH * (NOPE + VD))


BR = 512
BRM = 256


def _layer_fwd(x, w, tabs, batch):
    T = batch * S
    rope_c, rope_s1, rope_s2, dil_tab = tabs
    res = {"x": x}
    row = lambda c: (lambda i: (i, c))
    fix = lambda i: (0, 0)

    h = _pcall("norm_fwd", _rms, (T // BR,),
               [(x, (BR, D), row(0)), (w["norm_g"], (1, D), fix)],
               [((T, D), BF16, (BR, D), row(0))])[0]
    proj = _mm("in_proj", h, w["w_in_t"], tb=True, out_dtype=BF16, b_words=True, tm=2048, tn=1024)
    res["h"], res["proj"] = h, proj
    proj3 = proj.reshape(batch, S, NINP)

    cblk = lambda s: (lambda j, b: (b, 0, 4 * s + j))
    y_a = _pcall("conv_fwd", _conv_math, (4, batch),
                 [(proj3, (None, S, LANE), cblk(0)), (proj3, (None, S, LANE), cblk(1)),
                  (proj3, (None, S, LANE), cblk(2)), (proj3, (None, S, LANE), cblk(3)),
                  (w["conv_w"], (3, LANE), lambda j, b: (0, j)), (w["conv_b"], (1, LANE), lambda j, b: (0, j))],
                 [((batch, S, CW), BF16, (None, S, LANE), lambda j, b: (b, 0, j))])[0].reshape(T, CW)
    res["y_a"] = y_a

    cqn, ckvn = _pcall("mla_pre_fwd", _mla_pre_math, (T // BR,),
                       [(proj, (BR, QL), row(O_CQ // QL)), (proj, (BR, KVL), row(O_CKV // KVL)),
                        (w["q_a_norm_g"], (1, QL), fix), (w["kv_a_norm_g"], (1, KVL), fix)],
                       [((T, QL), BF16, (BR, QL), row(0)), ((T, KVL), BF16, (BR, KVL), row(0))])
    w_uq_p = _pad_heads_uq(w["w_uq"])
    w_uk, w_uv = _split_ukv(w["w_ukv"])
    q = _mm("uq", cqn, w_uq_p, out_dtype=BF16)
    kn = _mm("uk", ckvn, w_uk, out_dtype=BF16)
    v = _mm("uv", ckvn, w_uv, out_dtype=BF16)
    nrr = S // BR
    tab_row = lambda i: (i % nrr, 0)
    qr, kr = _pcall("rope_fwd", _rope_math, (T // BR,),
                    [(q, (BR, NH * QKP), row(0)), (kn, (BR, NH * QKP), row(0)), (proj, (BR, LANE), row(O_KPE // LANE)),
                     (w["mla_q_norm_g"], (1, QKP), fix), (w["mla_k_norm_g"], (1, QKP), fix),
                     (rope_c, (BR, QKP), tab_row), (rope_s1, (BR, QKP), tab_row), (rope_s2, (BR, QKP), tab_row)],
                    [((T, NH * QKP), BF16, (BR, NH * QKP), row(0))] * 2)
    qr = qr.reshape(batch, S, NH * QKP)
    kr = kr.reshape(batch, S, NH * QKP)
    v = v.reshape(batch, S, NH * VD)
    o_b, l_b = _mla_fwd(qr, kr, v)
    ob2 = o_b.reshape(T, NH * VD)
    y_b = _pcall("gateb_fwd", _gate_math, (T // BR,),
                 [(ob2, (BR, 512), row(0)), (proj, (BR, 512), row(O_BZ // 512))],
                 [((T, 512), BF16, (BR, 512), row(0))])[0]
    res.update(cqn=cqn, ckvn=ckvn, q=q, kn=kn, qr=qr, kr=kr, v=v, o_b=o_b, l_b=l_b, ob2=ob2, y_b=y_b,
               w_uq_p=w_uq_p, w_uk=w_uk, w_uv=w_uv)

    gq2 = jnp.tile(w["dil_q_norm_g"].reshape(NG, HD), (1, 2))
    gk2 = jnp.tile(w["dil_k_norm_g"].reshape(NG, HD), (1, 2))
    y_c, o_all, l_all = _dilc_fwd(proj3, gq2, gk2, dil_tab)
    y_c = y_c.reshape(T, DWID)
    res.update(o_all=o_all, l_all=l_all, y_c=y_c)

    pa = _mm("out_a", y_a, w["w_out_a"], out_dtype=BF16)
    pb = _mm("out_b", y_b, w["w_out_b"], out_dtype=BF16)
    pc = _mm("out_c", y_c, w["w_out_c"], out_dtype=BF16)
    merged = _pcall("merge_fwd", _merge_math, (T // BRM,),
                    [(proj, (BRM, D), row(O_G // D + s)) for s in range(3)]
                    + [(w["b_gate"], (1, D), (lambda s: (lambda i: (0, s)))(s)) for s in range(3)]
                    + [(t, (BRM, D), row(0)) for t in (pa, pb, pc)],
                    [((T, D), BF16, (BRM, D), row(0))])[0]
    out = _mm("o_proj", merged, w["w_o"], add=x, tm=1024)
    res.update(pa=pa, pb=pb, pc=pc, merged=merged)
    return out, res


def _norm_bwd_math(x, g, dh, dy):
    _, pull = jax.vjp(_rms, x, g)
    dx, dg = pull(dh)
    return dx + dy, dg


def _layer_bwd(dy, w, res, tabs, batch, after_dw=None, after_merge=None):
    T = batch * S
    rope_c, rope_s1, rope_s2, dil_tab = tabs
    row = lambda c: (lambda i: (i, c))
    fix = lambda i: (0, 0)
    x, proj, h = res["x"], res["proj"], res["h"]
    proj3 = proj.reshape(batch, S, NINP)
    g = {}

    d_merged = _mm("o_proj_dx", dy, w["w_o"], tb=True)
    g["w_o"] = _mm("o_proj_dw", res["merged"], dy, ta=True, tm=1024, tk=2048)

    dproj = lax.empty((T, NINP), BF16)
    rows_of = lambda br: (lambda ref, i: ref.at[pl.ds(pl.multiple_of(i * br, br), br)])

    def merge_bwd(*args):
        dg0, dg1, dg2, db0, db1, db2, dpa, dpb, dpc = _vjp_of(_merge_math, 9)(*args, n_prim=9)
        return db0, db1, db2, dpa, dpb, dpc, jnp.concatenate([dg0, dg1, dg2], axis=1)

    db0, db1, db2, dpa, dpb, dpc, dproj = _pcall(
        "merge_bwd", merge_bwd, (T // BRM,),
        [(proj, (BRM, D), row(O_G // D + s)) for s in range(3)]
        + [(w["b_gate"], (1, D), (lambda s: (lambda i: (0, s)))(s)) for s in range(3)]
        + [(t, (BRM, D), row(0)) for t in (res["pa"], res["pb"], res["pc"])]
        + [(d_merged, (BRM, D), row(0))],
        [((1, D), F32, (1, D), fix, True)] * 3 + [((T, D), BF16, (BRM, D), row(0))] * 3,
        into=(dproj, [((BRM, 3 * D), lambda ref, i: rows_of(BRM)(ref, i).at[:, O_G:O_G + 3 * D])]))
    g["b_gate"] = jnp.concatenate([db0, db1, db2], axis=1)

    dep = after_merge(dpa) if after_merge is not None else None
    d_ya = _mm("out_a_dx", dpa, w["w_out_a"], tb=True, dep=dep)
    d_yb = _mm("out_b_dx", dpb, w["w_out_b"], tb=True)
    d_yc = _mm("out_c_dx", dpc, w["w_out_c"], tb=True)
    g["w_out_a"] = _mm("out_a_dw", res["y_a"], dpa, ta=True, tk=T)
    g["w_out_b"] = _mm("out_b_dw", res["y_b"], dpb, ta=True, tk=T)
    g["w_out_c"] = _mm("out_c_dw", res["y_c"], dpc, ta=True, tk=T)

    cblk = lambda s: (lambda j, b: (b, 0, 4 * s + j))
    oblk = lambda j, b: (b, 0, j)
    def conv_bwd(*args):
        d_ab, d_ac, d_ax, d_az, dcw, dcb = _vjp_of(_conv_math, 6)(*args, n_prim=6)
        return dcw, dcb, d_ab, d_ac, d_ax, d_az

    a_col = lambda s_: (lambda ref, j, b: ref.at[b, :, pl.ds(pl.multiple_of(O_A + s_ * CW + j * LANE, LANE), LANE)])
    g["conv_w"], g["conv_b"], dproj3 = _pcall(
        "conv_bwd", conv_bwd, (4, batch),
        [(proj3, (None, S, LANE), cblk(s)) for s in range(4)]
        + [(w["conv_w"], (3, LANE), lambda j, b: (0, j)), (w["conv_b"], (1, LANE), lambda j, b: (0, j)),
           (d_ya.reshape(batch, S, CW), (None, S, LANE), oblk)],
        [((3, CW), F32, (3, LANE), lambda j, b: (0, j), True), ((1, CW), F32, (1, LANE), lambda j, b: (0, j), True)],
        into=(dproj.reshape(batch, S, NINP), [((S, LANE), a_col(s_)) for s_ in range(4)]))
    dproj = dproj3.reshape(T, NINP)

    gate_bwd = functools.partial(_vjp_of(_gate_math, 2), n_prim=2)
    d_ob, dproj = _pcall("gateb_bwd", gate_bwd, (T // BR,),
                         [(res["ob2"], (BR, 512), row(0)), (proj, (BR, 512), row(O_BZ // 512)), (d_yb, (BR, 512), row(0))],
                         [((T, 512), F32, (BR, 512), row(0))],
                         into=(dproj, [((BR, 512), lambda ref, i: rows_of(BR)(ref, i).at[:, O_BZ:O_BZ + 512])]))
    dqr, dkr, dv = _mla_bwd(res["qr"], res["kr"], res["v"], d_ob.reshape(batch, S, NH * VD), res["o_b"], res["l_b"])
    nrr = S // BR
    tab_row = lambda i: (i % nrr, 0)
    def rope_bwd(*args):
        d_q, d_kn, d_kpe, dgq, dgk = _vjp_of(_rope_math, 5)(*args, n_prim=8)
        return d_q, d_kn, dgq, dgk, d_kpe

    d_q, d_kn, g["mla_q_norm_g"], g["mla_k_norm_g"], dproj = _pcall(
        "rope_bwd", rope_bwd, (T // BR,),
        [(res["q"], (BR, NH * QKP), row(0)), (res["kn"], (BR, NH * QKP), row(0)), (proj, (BR, LANE), row(O_KPE // LANE)),
         (w["mla_q_norm_g"], (1, QKP), fix), (w["mla_k_norm_g"], (1, QKP), fix),
         (rope_c, (BR, QKP), tab_row), (rope_s1, (BR, QKP), tab_row), (rope_s2, (BR, QKP), tab_row),
         (dqr.reshape(T, NH * QKP), (BR, NH * QKP), row(0)), (dkr.reshape(T, NH * QKP), (BR, NH * QKP), row(0))],
        [((T, NH * QKP), BF16, (BR, NH * QKP), row(0))] * 2 + [((1, QKP), F32, (1, QKP), fix, True)] * 2,
        into=(dproj, [((BR, LANE), lambda ref, i: rows_of(BR)(ref, i).at[:, O_KPE:O_KPE + LANE])]))
    dv = dv.reshape(T, NH * VD)
    d_cqn = _mm("uq_dx", d_q, res["w_uq_p"], tb=True)
    d_ckvn = _mm("uk_dx", d_kn, res["w_uk"], tb=True)
    d_ckvn = _mm("uv_dx", dv, res["w_uv"], tb=True, add=d_ckvn)
    g["w_uq"] = _unpad_heads_uq(_mm("uq_dw", res["cqn"], d_q, ta=True, tk=T))
    g["w_ukv"] = _join_ukv(_mm("uk_dw", res["ckvn"], d_kn, ta=True, tk=T),
                           _mm("uv_dw", res["ckvn"], dv, ta=True, tk=T))
    def pre_bwd(*args):
        d_cq, d_ckv, dgq, dgkv = _vjp_of(_mla_pre_math, 4)(*args, n_prim=4)
        return dgq, dgkv, jnp.concatenate([d_cq, d_ckv], axis=1)

    g["q_a_norm_g"], g["kv_a_norm_g"], dproj = _pcall(
        "mla_pre_bwd", pre_bwd, (T // BR,),
        [(proj, (BR, QL), row(O_CQ // QL)), (proj, (BR, KVL), row(O_CKV // KVL)),
         (w["q_a_norm_g"], (1, QL), fix), (w["kv_a_norm_g"], (1, KVL), fix),
         (d_cqn, (BR, QL), row(0)), (d_ckvn, (BR, KVL), row(0))],
        [((1, QL), F32, (1, QL), fix, True), ((1, KVL), F32, (1, KVL), fix, True)],
        into=(dproj, [((BR, QL + KVL), lambda ref, i: rows_of(BR)(ref, i).at[:, O_CQ:O_CQ + QL + KVL])]))

    gq2 = jnp.tile(w["dil_q_norm_g"].reshape(NG, HD), (1, 2))
    gk2 = jnp.tile(w["dil_k_norm_g"].reshape(NG, HD), (1, 2))
    dproj3, dgq, dgk = _dilc_bwd(proj3, gq2, gk2, dil_tab, res["o_all"], res["l_all"],
                                 d_yc.reshape(batch, S, DWID), dproj.reshape(batch, S, NINP))
    dproj = dproj3.reshape(T, NINP)
    g["dil_q_norm_g"] = dgq[:, :HD] + dgq[:, HD:]
    g["dil_k_norm_g"] = dgk[:, :HD] + dgk[:, HD:]

    g["w_in_t"] = _mm("in_proj_dw", dproj, h, ta=True, tm=1024, tk=T)
    dep = after_dw(g) if after_dw is not None else None
    d_h = _mm("in_proj_dx", dproj, w["w_in_t"], dep=dep, b_words=True, tm=1024, tk=NINP // 4)
    dx, g["norm_g"] = _pcall("norm_bwd", _norm_bwd_math, (T // BR,),
                             [(x, (BR, D), row(0)), (w["norm_g"], (1, D), fix), (d_h, (BR, D), row(0)),
                              (dy, (BR, D), row(0))],
                             [((T, D), F32, (BR, D), row(0)), ((1, D), F32, (1, D), fix, True)])
    return dx, g


def _loss_math(y, t):
    e = y - t
    return e * (1.0 / D), 0.5 * jnp.sum(jnp.sum(e * e, axis=-1, keepdims=True) / D, axis=0, keepdims=True)


ANY = pl.BlockSpec(memory_space=pl.ANY)
U32 = jnp.uint32
WSH = NIN // 4
WA = KPE_END
WB = WSH - WA
CWD = 512
PACK_ROWS = 1472
HW = PACK_W // 2


def _me():
    return lax.axis_index("x"), lax.axis_index("y"), lax.axis_index("c")


def _piece_rows(k):
    a = k * WSH + jnp.where(k > 0, NINP - NIN, 0)
    b = k * WSH + WA + (NINP - NIN)
    return ((0, pl.multiple_of(a, 8), WA), (WA, pl.multiple_of(b, 8), WB))


def _pack_words(lo, hi):
    ul = lax.bitcast_convert_type(lo.astype(BF16).astype(F32), U32)
    uh = lax.bitcast_convert_type(hi.astype(BF16).astype(F32), U32)
    w = jnp.bitwise_or(jnp.bitwise_and(uh, jnp.uint32(0xFFFF0000)), jnp.right_shift(ul, jnp.uint32(16)))
    return lax.bitcast_convert_type(w, F32)


def _unpack_words(w):
    w = lax.bitcast_convert_type(w, U32)
    lo = lax.bitcast_convert_type(jnp.left_shift(w, jnp.uint32(16)), F32)
    hi = lax.bitcast_convert_type(jnp.bitwise_and(w, jnp.uint32(0xFFFF0000)), F32)
    return lo, hi


def _all_gather(wc, sp):
    def body(w_ref, s_ref, ow_ref, os_ref, send_sems, recv_sems):
        x, y, c = _me()
        k_me = 2 * x + y
        sib = (x, y, 1 - c)
        chips = [(1 - x, y), (x, 1 - y), (1 - x, 1 - y)]
        wcols = lambda cc: pl.ds(pl.multiple_of(cc * (CWD // 2), LANE), CWD // 2)
        scols = lambda cc: pl.ds(pl.multiple_of(cc * HW, LANE), HW)

        def windows(k, cc):
            pcs = _piece_rows(k)
            return ([(w_ref.at[pl.ds(l0, n), wcols(cc)], ow_ref.at[pl.ds(p0, n), wcols(cc)]) for l0, p0, n in pcs]
                    + [(s_ref.at[:, scols(cc)], os_ref.at[k, :, scols(cc)])])

        def copy(i, src, dst, to):
            return pltpu.make_async_remote_copy(src_ref=src, dst_ref=dst, send_sem=send_sems.at[i],
                                                recv_sem=recv_sems.at[i], device_id=to, device_id_type=MESH)

        def own_windows():
            return ([(w_ref.at[pl.ds(l0, n)], ow_ref.at[pl.ds(p0, n)]) for l0, p0, n in _piece_rows(k_me)]
                    + [(s_ref, os_ref.at[k_me])])

        first = [copy(18 + i, src, dst, sib) for i, (src, dst) in enumerate(own_windows())]
        for j, (cx, cy) in enumerate(chips):
            for i, (src, dst) in enumerate(windows(k_me, c)):
                first.append(copy(3 * j + i, src, dst, (cx, cy, c)))
        for cp in first:
            cp.start()
        passed = []
        for j, (cx, cy) in enumerate(chips):
            for i, (_, dst) in enumerate(windows(2 * cx + cy, c)):
                copy(3 * j + i, dst, dst, (cx, cy, c)).wait_recv()
                cp = copy(9 + 3 * j + i, dst, dst, sib)
                cp.start()
                passed.append(cp)
        for j, (cx, cy) in enumerate(chips):
            for i, (_, dst) in enumerate(windows(2 * cx + cy, 1 - c)):
                copy(9 + 3 * j + i, dst, dst, sib).wait_recv()
        for i, (_, dst) in enumerate(own_windows()):
            copy(18 + i, dst, dst, sib).wait_recv()
        for cp in first + passed:
            cp.wait_send()

    return pl.pallas_call(
        body,
        out_shape=[jax.ShapeDtypeStruct((NINP, CWD), F32), jax.ShapeDtypeStruct((4, PACK_ROWS, PACK_W), BF16)],
        in_specs=[ANY, ANY], out_specs=[ANY, ANY],
        scratch_shapes=[pltpu.SemaphoreType.DMA((21,)), pltpu.SemaphoreType.DMA((21,))],
        name="weights_all_gather",
    )(wc, sp)


HBM = pl.BlockSpec(memory_space=pltpu.HBM)
SEM = pl.BlockSpec(memory_space=pltpu.SEMAPHORE)
EFFECT = pltpu.SideEffectType.DATAFLOW_SIDE_EFFECTING


def _in_hbm(a):
    return pltpu.with_memory_space_constraint(a, pltpu.HBM)


def _ag_shard(w_ref, s_ref, lw_ref, ls_ref, k):
    return ([(w_ref.at[pl.ds(l0, n)], lw_ref.at[pl.ds(p0, n)]) for l0, p0, n in _piece_rows(k)]
            + [(s_ref, ls_ref.at[k])])


def _ag_behind_copies(w_ref, s_ref, lw_ref, ls_ref, send_sems, recv_sems):
    x, y, c = _me()
    peers = [(1 - x, y, c), (x, 1 - y, c), (1 - x, 1 - y, c), (x, y, 1 - c)]
    mine, theirs = [], []
    for j, (px, py, pc) in enumerate(peers):
        for i, ((src, dst), (_, got)) in enumerate(zip(_ag_shard(w_ref, s_ref, lw_ref, ls_ref, 2 * x + y),
                                                       _ag_shard(w_ref, s_ref, lw_ref, ls_ref, 2 * px + py))):
            mk = lambda s_, d_: pltpu.make_async_remote_copy(
                src_ref=s_, dst_ref=d_, send_sem=send_sems.at[3 * j + i], recv_sem=recv_sems.at[3 * j + i],
                device_id=(px, py, pc), device_id_type=MESH)
            mine.append(mk(src, dst))
            theirs.append(mk(got, got))
    return mine, theirs


def _ag_behind_start(wc, sp, dep):
    def body(w_ref, s_ref, lw_ref, ls_ref, dep_ref, send_sems, recv_sems, w_thru, s_thru, lw_thru, ls_thru, token):
        mine, _ = _ag_behind_copies(w_ref, s_ref, lw_ref, ls_ref, send_sems, recv_sems)
        for cp in mine:
            cp.start()
        token[...] = jnp.zeros_like(token)

    return pl.pallas_call(
        body, name="weights_gather_start",
        out_shape=(pltpu.SemaphoreType.DMA((12,)), pltpu.SemaphoreType.DMA((12,)), pltpu.HBM(wc.shape, wc.dtype),
                   pltpu.HBM(sp.shape, sp.dtype), pltpu.HBM((NINP, CWD), F32), pltpu.HBM((4, PACK_ROWS, PACK_W), BF16),
                   jax.ShapeDtypeStruct((8, LANE), F32)),
        in_specs=(HBM, HBM, HBM, HBM, ANY),
        out_specs=(SEM, SEM, HBM, HBM, HBM, HBM, pl.BlockSpec(memory_space=pltpu.VMEM)),
        input_output_aliases={0: 2, 1: 3, 2: 4, 3: 5},
        compiler_params=pltpu.CompilerParams(has_side_effects=EFFECT),
    )(_in_hbm(wc), _in_hbm(sp), _in_hbm(lax.empty((NINP, CWD), F32)), _in_hbm(lax.empty((4, PACK_ROWS, PACK_W), BF16)), dep)


def _ag_behind_wait(send_sems, recv_sems, wc, sp, lw, ls, after):
    def body(w_ref, s_ref, lw_ref, ls_ref, send_sems, recv_sems, after_ref, w_dead, s_dead, lw_out, ls_out):
        mine, theirs = _ag_behind_copies(w_ref, s_ref, lw_ref, ls_ref, send_sems, recv_sems)
        for cp in mine:
            cp.wait_send()
        for cp in theirs:
            cp.wait_recv()

    out = pl.pallas_call(
        body, name="weights_gather_wait",
        out_shape=(pltpu.HBM(wc.shape, wc.dtype), pltpu.HBM(sp.shape, sp.dtype), pltpu.HBM(lw.shape, lw.dtype),
                   pltpu.HBM(ls.shape, ls.dtype)),
        in_specs=(HBM, HBM, HBM, HBM, SEM, SEM, ANY), out_specs=(HBM, HBM, HBM, HBM),
        input_output_aliases={0: 0, 1: 1, 2: 2, 3: 3},
        compiler_params=pltpu.CompilerParams(has_side_effects=EFFECT),
    )(wc, sp, lw, ls, send_sems, recv_sems, after)
    return out[2], out[3]


def _rs_swap_copies(w_ref, s_ref, rw_ref, rs_ref, send_sems, recv_sems):
    x, y, c = _me()
    oc = 1 - c
    return [pltpu.make_async_remote_copy(src_ref=w_ref.at[:, pl.ds(pl.multiple_of(oc * (D // 2), LANE), D // 2)],
                                         dst_ref=rw_ref, send_sem=send_sems.at[0], recv_sem=recv_sems.at[0],
                                         device_id=(x, y, oc), device_id_type=MESH),
            pltpu.make_async_remote_copy(src_ref=s_ref.at[:, :, pl.ds(pl.multiple_of(oc * HW, LANE), HW)],
                                         dst_ref=rs_ref, send_sem=send_sems.at[1], recv_sem=recv_sems.at[1],
                                         device_id=(x, y, oc), device_id_type=MESH)]


def _rs_swap_start(gw, gs, tag):
    def body(w_ref, s_ref, rw_ref, rs_ref, send_sems, recv_sems, w_thru, s_thru, rw_thru, rs_thru, token):
        for cp in _rs_swap_copies(w_ref, s_ref, rw_ref, rs_ref, send_sems, recv_sems):
            cp.start()
        token[...] = jnp.zeros_like(token)

    return pl.pallas_call(
        body, name="grads_swap_start_" + tag,
        out_shape=(pltpu.SemaphoreType.DMA((2,)), pltpu.SemaphoreType.DMA((2,)), pltpu.HBM(gw.shape, gw.dtype),
                   pltpu.HBM(gs.shape, gs.dtype), pltpu.HBM((NINP, D // 2), F32), pltpu.HBM((4, PACK_ROWS, HW), F32),
                   jax.ShapeDtypeStruct((8, LANE), F32)),
        in_specs=(HBM, HBM, HBM, HBM),
        out_specs=(SEM, SEM, HBM, HBM, HBM, HBM, pl.BlockSpec(memory_space=pltpu.VMEM)),
        input_output_aliases={0: 2, 1: 3, 2: 4, 3: 5},
        compiler_params=pltpu.CompilerParams(has_side_effects=EFFECT),
    )(_in_hbm(gw), _in_hbm(gs), _in_hbm(lax.empty((NINP, D // 2), F32)), _in_hbm(lax.empty((4, PACK_ROWS, HW), F32)))


def _rs_swap_wait(send_sems, recv_sems, gw, gs, rw, rs, after, tag):
    def body(w_ref, s_ref, rw_ref, rs_ref, send_sems, recv_sems, after_ref, w_out, s_out, rw_out, rs_out):
        for cp in _rs_swap_copies(w_ref, s_ref, rw_ref, rs_ref, send_sems, recv_sems):
            cp.wait()

    return pl.pallas_call(
        body, name="grads_swap_wait_" + tag,
        out_shape=(pltpu.HBM(gw.shape, gw.dtype), pltpu.HBM(gs.shape, gs.dtype), pltpu.HBM(rw.shape, rw.dtype),
                   pltpu.HBM(rs.shape, rs.dtype)),
        in_specs=(HBM, HBM, HBM, HBM, SEM, SEM, ANY), out_specs=(HBM, HBM, HBM, HBM),
        input_output_aliases={0: 0, 1: 1, 2: 2, 3: 3},
        compiler_params=pltpu.CompilerParams(has_side_effects=EFFECT),
    )(gw, gs, rw, rs, send_sems, recv_sems, after)


SUM_BR = 512


def _rs_chip_sum_w(gw, rw, cidx):
    def body(c_ref, g_ref, r_ref, o_ref):
        s = g_ref[...] + r_ref[...]
        q = D // 8
        o_ref[...] = jnp.concatenate([_pack_words(s[:, 0:q], s[:, q:2 * q]),
                                      _pack_words(s[:, 2 * q:3 * q], s[:, 3 * q:4 * q])], axis=1)

    return pl.pallas_call(
        body,
        grid_spec=pltpu.PrefetchScalarGridSpec(
            num_scalar_prefetch=1, grid=(NINP // SUM_BR,),
            in_specs=[pl.BlockSpec((SUM_BR, D // 2), lambda i, cr: (i, cr[0])),
                      pl.BlockSpec((SUM_BR, D // 2), lambda i, cr: (i, 0))],
            out_specs=pl.BlockSpec((SUM_BR, D // 4), lambda i, cr: (i, 0))),
        out_shape=jax.ShapeDtypeStruct((NINP, D // 4), F32),
        name="grads_chip_sum_w",
        compiler_params=pltpu.CompilerParams(dimension_semantics=("arbitrary",), vmem_limit_bytes=VMEM_LIMIT),
    )(cidx, gw, rw)


def _rs_chip_sum_s(gs, rs, cidx):
    def body(c_ref, g_ref, r_ref, o_ref):
        o_ref[...] = (g_ref[...] + r_ref[...]).astype(BF16)

    return pl.pallas_call(
        body,
        grid_spec=pltpu.PrefetchScalarGridSpec(
            num_scalar_prefetch=1, grid=(4,),
            in_specs=[pl.BlockSpec((None, PACK_ROWS, HW), lambda j, cr: (j, 0, cr[0])),
                      pl.BlockSpec((None, PACK_ROWS, HW), lambda j, cr: (j, 0, 0))],
            out_specs=pl.BlockSpec((None, PACK_ROWS, HW), lambda j, cr: (j, 0, 0))),
        out_shape=jax.ShapeDtypeStruct((4, PACK_ROWS, HW), BF16),
        name="grads_chip_sum_s",
        compiler_params=pltpu.CompilerParams(dimension_semantics=("arbitrary",), vmem_limit_bytes=VMEM_LIMIT),
    )(cidx, gs, rs)


def _rs_exchange_copies(sw_ref, ss_ref, r2w_ref, r2s_ref, send_sems, recv_sems):
    x, y, c = _me()
    mine, theirs = [], []
    for j, (cx, cy) in enumerate([(1 - x, y), (x, 1 - y), (1 - x, 1 - y)]):
        def mk(i, src, dst):
            return pltpu.make_async_remote_copy(src_ref=src, dst_ref=dst, send_sem=send_sems.at[3 * j + i],
                                                recv_sem=recv_sems.at[3 * j + i], device_id=(cx, cy, c), device_id_type=MESH)
        for i, (l0, p0, n) in enumerate(_piece_rows(2 * cx + cy)):
            mine.append(mk(i, sw_ref.at[pl.ds(p0, n)], r2w_ref.at[j, pl.ds(l0, n)]))
            theirs.append(mk(i, r2w_ref.at[j, pl.ds(l0, n)], r2w_ref.at[j, pl.ds(l0, n)]))
        mine.append(mk(2, ss_ref.at[2 * cx + cy], r2s_ref.at[j]))
        theirs.append(mk(2, r2s_ref.at[j], r2s_ref.at[j]))
    return mine, theirs


def _rs_exchange_start(sw, ss, tag):
    def body(sw_ref, ss_ref, r2w_ref, r2s_ref, send_sems, recv_sems, sw_thru, ss_thru, r2w_thru, r2s_thru, token):
        mine, _ = _rs_exchange_copies(sw_ref, ss_ref, r2w_ref, r2s_ref, send_sems, recv_sems)
        for cp in mine:
            cp.start()
        token[...] = jnp.zeros_like(token)

    return pl.pallas_call(
        body, name="grads_exchange_start_" + tag,
        out_shape=(pltpu.SemaphoreType.DMA((9,)), pltpu.SemaphoreType.DMA((9,)), pltpu.HBM(sw.shape, sw.dtype),
                   pltpu.HBM(ss.shape, ss.dtype), pltpu.HBM((3, WSH, D // 4), F32), pltpu.HBM((3, PACK_ROWS, HW), BF16),
                   jax.ShapeDtypeStruct((8, LANE), F32)),
        in_specs=(HBM, HBM, HBM, HBM),
        out_specs=(SEM, SEM, HBM, HBM, HBM, HBM, pl.BlockSpec(memory_space=pltpu.VMEM)),
        input_output_aliases={0: 2, 1: 3, 2: 4, 3: 5},
        compiler_params=pltpu.CompilerParams(has_side_effects=EFFECT),
    )(_in_hbm(sw), _in_hbm(ss), _in_hbm(lax.empty((3, WSH, D // 4), F32)), _in_hbm(lax.empty((3, PACK_ROWS, HW), BF16)))


def _rs_exchange_wait(send_sems, recv_sems, sw, ss, r2w, r2s, after, tag):
    def body(sw_ref, ss_ref, r2w_ref, r2s_ref, send_sems, recv_sems, after_ref, sw_dead, ss_dead, r2w_out, r2s_out):
        mine, theirs = _rs_exchange_copies(sw_ref, ss_ref, r2w_ref, r2s_ref, send_sems, recv_sems)
        for cp in mine:
            cp.wait_send()
        for cp in theirs:
            cp.wait_recv()

    out = pl.pallas_call(
        body, name="grads_exchange_wait_" + tag,
        out_shape=(pltpu.HBM(sw.shape, sw.dtype), pltpu.HBM(ss.shape, ss.dtype), pltpu.HBM(r2w.shape, r2w.dtype),
                   pltpu.HBM(r2s.shape, r2s.dtype)),
        in_specs=(HBM, HBM, HBM, HBM, SEM, SEM, ANY), out_specs=(HBM, HBM, HBM, HBM),
        input_output_aliases={0: 0, 1: 1, 2: 2, 3: 3},
        compiler_params=pltpu.CompilerParams(has_side_effects=EFFECT),
    )(sw, ss, r2w, r2s, send_sems, recv_sems, after)
    return out[2], out[3]


def _rs_final_w(gw, rw, r2w, idx, both, layer):
    q = D // 8

    def body(i_ref, g_ref, r_ref, p_ref, both_ref, o_ref, gbuf, rbuf, sems):
        i = pl.program_id(0)
        k, c = i_ref[0], i_ref[1]
        cps = []
        for n_, (l0, p0, n) in enumerate(_piece_rows(k)):
            gcol = pl.ds(pl.multiple_of(c * (D // 2) + i * 2 * q, LANE), 2 * q)
            rcol = pl.ds(pl.multiple_of(i * 2 * q, LANE), 2 * q)
            cps.append(pltpu.make_async_copy(g_ref.at[pl.ds(p0, n), gcol], gbuf.at[pl.ds(l0, n)], sems.at[2 * n_]))
            cps.append(pltpu.make_async_copy(r_ref.at[pl.ds(p0, n), rcol], rbuf.at[pl.ds(l0, n)], sems.at[2 * n_ + 1]))
        for cp in cps:
            cp.start()
        for cp in cps:
            cp.wait()
        acc = gbuf[...] + rbuf[...]
        for j in range(3):
            lo, hi = _unpack_words(p_ref[j])
            acc = acc + jnp.concatenate([lo, hi], axis=1)
        o_ref[...] = acc

    return pl.pallas_call(
        body,
        grid_spec=pltpu.PrefetchScalarGridSpec(
            num_scalar_prefetch=1, grid=(2,),
            in_specs=[ANY, ANY, pl.BlockSpec((3, WSH, q), lambda i, ir: (0, 0, i)), ANY],
            out_specs=pl.BlockSpec((None, WSH, 2 * q), lambda i, ir: (layer, 0, 2 * ir[1] + i)),
            scratch_shapes=[pltpu.VMEM((WSH, 2 * q), F32), pltpu.VMEM((WSH, 2 * q), F32), pltpu.SemaphoreType.DMA((4,))]),
        out_shape=jax.ShapeDtypeStruct((NL, WSH, D), F32),
        input_output_aliases={4: 0},
        name="grads_final_sum_w",
        compiler_params=pltpu.CompilerParams(dimension_semantics=("arbitrary",), vmem_limit_bytes=VMEM_LIMIT),
    )(idx, gw, rw, r2w, both)


def _rs_final_s(gs, rs, r2s, idx):
    def body(i_ref, g_ref, r_ref, p_ref, o_ref):
        acc = g_ref[...] + r_ref[...]
        for j in range(3):
            acc = acc + p_ref[j].astype(F32)
        o_ref[...] = acc

    return pl.pallas_call(
        body,
        grid_spec=pltpu.PrefetchScalarGridSpec(
            num_scalar_prefetch=1, grid=(1,),
            in_specs=[pl.BlockSpec((None, PACK_ROWS, HW), lambda i, ir: (ir[0], 0, ir[1])),
                      pl.BlockSpec((None, PACK_ROWS, HW), lambda i, ir: (ir[0], 0, 0)),
                      pl.BlockSpec((3, PACK_ROWS, HW), lambda i, ir: (0, 0, 0))],
            out_specs=pl.BlockSpec((PACK_ROWS, HW), lambda i, ir: (0, ir[1]))),
        out_shape=jax.ShapeDtypeStruct((PACK_ROWS, PACK_W), F32),
        name="grads_final_sum_s",
        compiler_params=pltpu.CompilerParams(dimension_semantics=("arbitrary",), vmem_limit_bytes=VMEM_LIMIT),
    )(idx, gs, rs, r2s)


def _rs_share(fw, fs, layer):
    def body(w_ref, s_ref, ow_ref, os_ref, send_sems, recv_sems):
        x, y, c = _me()
        wcol = lambda cc: pl.ds(pl.multiple_of(cc * (D // 2), LANE), D // 2)
        scol = lambda cc: pl.ds(pl.multiple_of(cc * HW, LANE), HW)

        def copies(cc):
            return [pltpu.make_async_remote_copy(src_ref=w_ref.at[layer, :, wcol(cc)],
                                                 dst_ref=ow_ref.at[layer, :, wcol(cc)],
                                                 send_sem=send_sems.at[0], recv_sem=recv_sems.at[0],
                                                 device_id=(x, y, 1 - c), device_id_type=MESH),
                    pltpu.make_async_remote_copy(src_ref=s_ref.at[:, scol(cc)], dst_ref=os_ref.at[:, scol(cc)],
                                                 send_sem=send_sems.at[1], recv_sem=recv_sems.at[1],
                                                 device_id=(x, y, 1 - c), device_id_type=MESH)]
        out = copies(c)
        for cp in out:
            cp.start()
        for cp in copies(1 - c):
            cp.wait_recv()
        for cp in out:
            cp.wait_send()

    return pl.pallas_call(
        body,
        out_shape=[jax.ShapeDtypeStruct(fw.shape, F32), jax.ShapeDtypeStruct(fs.shape, F32)],
        in_specs=[ANY, ANY], out_specs=[ANY, ANY],
        input_output_aliases={0: 0, 1: 1},
        scratch_shapes=[pltpu.SemaphoreType.DMA((2,)), pltpu.SemaphoreType.DMA((2,))],
        name="grads_share",
    )(fw, fs)


def _rs_sums(gw, gs, rw, rs):
    x, y, c = _me()
    cidx = jnp.reshape(c, (1,)).astype(jnp.int32)
    return dict(gw=gw, gs=gs, rw=rw, rs=rs, sw=_rs_chip_sum_w(gw, rw, cidx), ss=_rs_chip_sum_s(gs, rs, cidx))


def _rs_end(st, r2w, r2s, both, layer):
    x, y, c = _me()
    idx = jnp.stack([2 * x + y, c]).astype(jnp.int32)
    return _rs_share(_rs_final_w(st["gw"], st["rw"], r2w, idx, both, layer),
                     _rs_final_s(st["gs"], st["rs"], r2s, idx), layer)


def _all_reduce_small(gs):
    rows = gs.shape[0]

    def body(g_ref, o_ref, buf, send_sems, recv_sems):
        x, y, c = _me()
        me = 4 * x + 2 * y + c
        buf[me] = g_ref[...]
        cps = []
        for r in range(1, 8):
            fx, fy, fc = (r >> 2) & 1, (r >> 1) & 1, r & 1
            px, py, pc = jnp.bitwise_xor(x, fx), jnp.bitwise_xor(y, fy), jnp.bitwise_xor(c, fc)
            cps.append((pltpu.make_async_remote_copy(
                src_ref=g_ref, dst_ref=buf.at[me], send_sem=send_sems.at[r - 1], recv_sem=recv_sems.at[r - 1],
                device_id=(px, py, pc), device_id_type=MESH), 4 * px + 2 * py + pc))
        for cp, _ in cps:
            cp.start()
        for r, (cp, peer) in enumerate(cps):
            pltpu.make_async_remote_copy(
                src_ref=g_ref, dst_ref=buf.at[peer], send_sem=send_sems.at[r], recv_sem=recv_sems.at[r],
                device_id=(x, y, c), device_id_type=MESH).wait_recv()
        for cp, _ in cps:
            cp.wait_send()
        acc = buf[0]
        for k in range(1, 8):
            acc = acc + buf[k]
        o_ref[...] = acc

    return pl.pallas_call(
        body,
        out_shape=jax.ShapeDtypeStruct((rows, LANE), F32),
        in_specs=[pl.BlockSpec(memory_space=pltpu.VMEM)],
        out_specs=pl.BlockSpec(memory_space=pltpu.VMEM),
        scratch_shapes=[pltpu.VMEM((8, rows, LANE), F32), pltpu.SemaphoreType.DMA((7,)), pltpu.SemaphoreType.DMA((7,))],
        name="small_grads_all_reduce",
    )(gs)


PACK_SPLIT = (("w_uq", 96, (QL, 192)), ("w_ukv", 64, (KVL, 256)),
              ("w_out_a", 256, (CW, 256)), ("w_out_b", 256, (CW, 256)), ("w_out_c", 256, (CW, 256)),
              ("w_o", 512, (256, D)))
MAT_ROWS = 1440
CONV_SHARD = 3 * 128


def _w_in_words(w_in_shard):
    t = w_in_shard.T
    return _pack_words(t[:, :CWD], t[:, CWD:])


def _pack_weights(wl):
    parts = [wl[n].astype(BF16).reshape(-1, PACK_W) for n, _, _ in PACK_SPLIT]
    cw = wl["conv_w"].reshape(-1)
    hi = cw.astype(BF16)
    r1 = cw - hi.astype(F32)
    mid = r1.astype(BF16)
    lo = (r1 - mid.astype(F32)).astype(BF16)
    cterms = jnp.pad(jnp.concatenate([hi, mid, lo]), (0, 3 * PACK_W - 3 * CONV_SHARD)).reshape(3, PACK_W)
    tail = jnp.pad(cterms, ((0, PACK_ROWS - MAT_ROWS - 3), (0, 0)))
    return jnp.concatenate(parts + [tail], axis=0)


def _unpack_weights(gath):
    out = {}
    r = 0
    for n, nrows, shp in PACK_SPLIT:
        t = gath[:, r:r + nrows].reshape((4,) + shp)
        r += nrows
        if n == "w_o":
            out[n] = t.reshape(4 * shp[0], shp[1])
        else:
            out[n] = t.transpose(1, 0, 2).reshape(shp[0], 4 * shp[1])
    ct = gath[:, r:r + 3].reshape(4, 3 * PACK_W)[:, :3 * CONV_SHARD].astype(F32).reshape(4, 3, CONV_SHARD)
    cw = (ct[:, 0] + ct[:, 1]) + ct[:, 2]
    out["conv_w"] = cw.reshape(4, 3, 128).transpose(1, 0, 2).reshape(3, CW)
    return out


def _pack_grads(g):
    parts = []
    for n, nrows, shp in PACK_SPLIT:
        t = g[n]
        if n == "w_o":
            t = t.reshape((4,) + shp)
        else:
            t = t.reshape(shp[0], 4, shp[1]).transpose(1, 0, 2)
        parts.append(t.reshape(4, nrows, PACK_W))
    cw = g["conv_w"].reshape(3, 4, 128).transpose(1, 0, 2).reshape(4, 1, CONV_SHARD)
    parts.append(jnp.pad(cw, ((0, 0), (0, PACK_ROWS - MAT_ROWS - 1), (0, PACK_W - CONV_SHARD))))
    return jnp.concatenate(parts, axis=1)


def _unpack_grads(red):
    out = {}
    r = 0
    for n, nrows, shp in PACK_SPLIT:
        out[n] = red[r:r + nrows].reshape(shp)
        r += nrows
    out["conv_w"] = red[r, :CONV_SHARD].reshape(3, 128)
    return out


SMALL_SIZES = (("norm_g", D), ("b_gate", 3 * D), ("conv_b", CW), ("q_a_norm_g", QL), ("kv_a_norm_g", KVL),
               ("mla_q_norm_g", QK), ("mla_k_norm_g", QK), ("dil_q_norm_g", NG * HD), ("dil_k_norm_g", NG * HD))
SMALL_ROWS = 88


def _pack_small(per_name):
    flat = jnp.concatenate([per_name[n].reshape(-1).astype(F32) for n, _ in SMALL_SIZES])
    return jnp.pad(flat, (0, SMALL_ROWS * LANE - flat.shape[0])).reshape(SMALL_ROWS, LANE)


def _unpack_small(packed, like):
    out = {}
    flat = packed.reshape(-1)
    r = 0
    for n, sz in SMALL_SIZES:
        out[n] = flat[r:r + NL * sz].reshape(like[n].shape)
        r += NL * sz
    return out


def _adamw_math(w, g, m, v):
    m = ADAM_B1 * m + (1.0 - ADAM_B1) * g
    v = ADAM_B2 * v + (1.0 - ADAM_B2) * jnp.square(g)
    m_hat = m / (1.0 - ADAM_B1 ** ADAM_STEP)
    v_hat = v / (1.0 - ADAM_B2 ** ADAM_STEP)
    delta = -ADAM_LR * (m_hat / (jnp.sqrt(v_hat) + ADAM_EPS) + ADAM_WD * w)
    return delta, m, v


def _adamw(name, w, g, m, v, br, bc=None):
    L, R, C = w.shape
    bc = C if bc is None else bc
    blk = lambda l, i, j: (l, i, j)
    return _pcall(name, _adamw_math, (L, R // br, C // bc), [(t, (None, br, bc), blk) for t in (w, g, m, v)],
                  [((L, R, C), F32, (None, br, bc), blk)] * 3)


ADAM_ROWS = {"w_uq": 256, "w_ukv": 128, "w_out_a": 512, "w_out_b": 512, "w_out_c": 512, "w_o": 256,
             "conv_w": 3}


def kernel(x, norm_g, w_in, b_gate, conv_w, conv_b, q_a_norm_g, w_uq, kv_a_norm_g, w_ukv, mla_q_norm_g, mla_k_norm_g, dil_q_norm_g, dil_k_norm_g, w_out_a, w_out_b, w_out_c, w_o, loss_target, m_norm_g, m_w_in, m_b_gate, m_conv_w, m_conv_b, m_q_a_norm_g, m_w_uq, m_kv_a_norm_g, m_w_ukv, m_mla_q_norm_g, m_mla_k_norm_g, m_dil_q_norm_g, m_dil_k_norm_g, m_w_out_a, m_w_out_b, m_w_out_c, m_w_o, v_norm_g, v_w_in, v_b_gate, v_conv_w, v_conv_b, v_q_a_norm_g, v_w_uq, v_kv_a_norm_g, v_w_ukv, v_mla_q_norm_g, v_mla_k_norm_g, v_dil_q_norm_g, v_dil_k_norm_g, v_w_out_a, v_w_out_b, v_w_out_c, v_w_o):
    W = dict(norm_g=norm_g, w_in=w_in, b_gate=b_gate, conv_w=conv_w, conv_b=conv_b, q_a_norm_g=q_a_norm_g, w_uq=w_uq,
             kv_a_norm_g=kv_a_norm_g, w_ukv=w_ukv, mla_q_norm_g=mla_q_norm_g, mla_k_norm_g=mla_k_norm_g,
             dil_q_norm_g=dil_q_norm_g, dil_k_norm_g=dil_k_norm_g, w_out_a=w_out_a, w_out_b=w_out_b, w_out_c=w_out_c,
             w_o=w_o)
    M = dict(norm_g=m_norm_g, w_in=m_w_in, b_gate=m_b_gate, conv_w=m_conv_w, conv_b=m_conv_b, q_a_norm_g=m_q_a_norm_g,
             w_uq=m_w_uq, kv_a_norm_g=m_kv_a_norm_g, w_ukv=m_w_ukv, mla_q_norm_g=m_mla_q_norm_g,
             mla_k_norm_g=m_mla_k_norm_g, dil_q_norm_g=m_dil_q_norm_g, dil_k_norm_g=m_dil_k_norm_g, w_out_a=m_w_out_a,
             w_out_b=m_w_out_b, w_out_c=m_w_out_c, w_o=m_w_o)
    V = dict(norm_g=v_norm_g, w_in=v_w_in, b_gate=v_b_gate, conv_w=v_conv_w, conv_b=v_conv_b, q_a_norm_g=v_q_a_norm_g,
             w_uq=v_w_uq, kv_a_norm_g=v_kv_a_norm_g, w_ukv=v_w_ukv, mla_q_norm_g=v_mla_q_norm_g,
             mla_k_norm_g=v_mla_k_norm_g, dil_q_norm_g=v_dil_q_norm_g, dil_k_norm_g=v_dil_k_norm_g, w_out_a=v_w_out_a,
             w_out_b=v_w_out_b, w_out_c=v_w_out_c, w_o=v_w_o)
    batch = x.shape[0]
    T = batch * S

    def layer_weights(l, cont, gath):
        full = _unpack_weights(gath)
        pad_qk = lambda t: jnp.pad(t, (0, QKP - QK)).reshape(1, QKP)
        full.update(
            w_in_t=cont,
            norm_g=norm_g[l].reshape(1, D), b_gate=b_gate[l].reshape(1, 3 * D), conv_b=conv_b[l].reshape(1, CW),
            q_a_norm_g=q_a_norm_g[l].reshape(1, QL), kv_a_norm_g=kv_a_norm_g[l].reshape(1, KVL),
            mla_q_norm_g=pad_qk(mla_q_norm_g[l]), mla_k_norm_g=pad_qk(mla_k_norm_g[l]),
            dil_q_norm_g=dil_q_norm_g[l].reshape(NG, 1, HD), dil_k_norm_g=dil_k_norm_g[l].reshape(NG, 1, HD))
        return full

    words = [_w_in_words(w_in[l]) for l in range(NL)]
    packs = [_pack_weights({n: W[n][l] for n in BIG[1:] + ("conv_w",)}) for l in range(NL)]
    tabs = _rope_tables() + (_dil_slopes(),)
    x2 = x.reshape(T, D)

    cont0, gath0 = _all_gather(words[0], packs[0])
    w0 = layer_weights(0, cont0, gath0)
    ag = _ag_behind_start(words[1], packs[1], gath0)
    w0["norm_g"] = w0["norm_g"] + ag[6][0:1, 0:1]
    y0, res0 = _layer_fwd(x2, w0, tabs, batch)
    w1 = layer_weights(1, *_ag_behind_wait(ag[0], ag[1], ag[2], ag[3], ag[4], ag[5], y0))
    y1, res1 = _layer_fwd(y0, w1, tabs, batch)

    row = lambda i: (i, 0)
    dy, loss = _pcall("loss", _loss_math, (T // BR,),
                      [(y1, (BR, D), row), (loss_target.reshape(T, D), (BR, D), row)],
                      [((T, D), F32, (BR, D), row), ((1, 1), F32, (1, 1), lambda i: (0, 0), True)])
    loss = lax.psum(loss[0, 0], ("x", "y", "c"))

    grads = [None] * NL
    dy, grads[1] = _layer_bwd(dy, w1, res1, tabs, batch)
    st = [None] * NL
    ex = [None] * NL
    sw1 = _rs_swap_start(grads[1]["w_in_t"], _pack_grads(grads[1]), "1")
    w0["w_o"] = w0["w_o"] + sw1[6][0:1, 0:1].astype(BF16)

    def exchange_layer1(t):
        st[1] = _rs_sums(*_rs_swap_wait(*sw1[:6], t, "1"))
        ex[1] = _rs_exchange_start(st[1]["sw"], st[1]["ss"], "1")
        return ex[1][6]

    red = [None] * NL
    g_in_t = [lax.empty((NL, WSH, D), F32)]

    def finish(l, after):
        r2w, r2s = _rs_exchange_wait(*ex[l][:6], after, str(l))
        g_in_t[0], rs = _rs_end(st[l], r2w, r2s, g_in_t[0], l)
        red[l] = _unpack_grads(rs)
        return rs

    def start_layer0(g):
        sw0 = _rs_swap_start(g["w_in_t"], _pack_grads(g), "0")
        done1 = finish(1, sw0[6])
        st[0] = _rs_sums(*_rs_swap_wait(*sw0[:6], done1, "0"))
        ex[0] = _rs_exchange_start(st[0]["sw"], st[0]["ss"], "0")
        return ex[0][6]

    dx, grads[0] = _layer_bwd(dy, w0, res0, tabs, batch, after_dw=start_layer0, after_merge=exchange_layer1)
    grad_x = dx.reshape(batch, S, D)
    finish(0, dx)

    G = {n: jnp.stack([red[l][n] for l in range(NL)]) for n in BIG[1:] + ("conv_w",)}
    g_in_t = g_in_t[0]
    G["w_in"] = jnp.swapaxes(g_in_t, 1, 2)
    small_g = {n: jnp.stack([grads[l][n].reshape(-1)[:sz] for l in range(NL)]) for n, sz in SMALL_SIZES}
    small_red = _all_reduce_small(_pack_small(small_g))
    G.update(_unpack_small(small_red, {n: W[n] for n in SMALL}))

    delta, new_m, new_v = {}, {}, {}
    for n in BIG[1:] + ("conv_w",):
        delta[n], new_m[n], new_v[n] = _adamw("adamw_" + n, W[n], G[n], M[n], V[n], ADAM_ROWS[n])
    tr = lambda t: jnp.swapaxes(t, 1, 2)
    delta["w_in"], new_m["w_in"], new_v["w_in"] = (
        tr(t) for t in _adamw("adamw_w_in", tr(w_in), g_in_t, tr(m_w_in), tr(v_w_in), WSH, LANE))
    sw, sm, sv = (_pack_small({n: t[n] for n in SMALL})[None] for t in (W, M, V))
    sd, snm, snv = _adamw("adamw_small", sw, small_red[None], sm, sv, SMALL_ROWS)
    like = {n: W[n] for n in SMALL}
    delta.update(_unpack_small(sd[0], like))
    new_m.update(_unpack_small(snm[0], like))
    new_v.update(_unpack_small(snv[0], like))

    return (loss, grad_x, *[G[n] for n in WEIGHTS], *[delta[n] for n in WEIGHTS],
            *[new_m[n] for n in WEIGHTS], *[new_v[n] for n in WEIGHTS])
```

```python
import functools

import numpy as np
import jax
import jax.numpy as jnp
from jax import lax
from jax.experimental import pallas as pl
from jax.experimental.pallas import tpu as pltpu

F32 = jnp.float32
BF16 = jnp.bfloat16

D = 1024
S = 2048
NL = 2
CW = 512
NH = 8
QL = 256
KVL = 128
NOPE = 64
ROPE = 32
VD = 64
QK = NOPE + ROPE
QKP = 128
ROPE_THETA = 10000.0
DIL = ((128, 1), (512, 4), (2048, 16))
NG = 3
DH = 8
HD = 64
DWID = DH * HD
QB = 128
EPS = 1e-6
NIN = 11168
NINP = 11264
O_A, O_CQ, O_CKV, O_KPE, O_BZ, O_DQ, O_DK, O_DV, O_CZ, O_G = 0, 2048, 2304, 2432, 2560, 3072, 4608, 6144, 7680, 8192
KPE_END = 2464
NEG = -1e30
MLA_SCALE = QK ** -0.5
DIL_SCALE = HD ** -0.5
LANE = 128
PACK_W = 512
VMEM_LIMIT = 48 * 1024 * 1024

ADAM_LR = 0.001
ADAM_B1 = 0.9
ADAM_B2 = 0.999
ADAM_EPS = 1e-08
ADAM_WD = 0.01
ADAM_STEP = 10

MESH = pl.DeviceIdType.MESH
BIG = ("w_in", "w_uq", "w_ukv", "w_out_a", "w_out_b", "w_out_c", "w_o")
SMALL = ("norm_g", "b_gate", "conv_b", "q_a_norm_g", "kv_a_norm_g", "mla_q_norm_g", "mla_k_norm_g",
         "dil_q_norm_g", "dil_k_norm_g")
WEIGHTS = ("norm_g", "w_in", "b_gate", "conv_w", "conv_b", "q_a_norm_g", "w_uq", "kv_a_norm_g", "w_ukv",
           "mla_q_norm_g", "mla_k_norm_g", "dil_q_norm_g", "dil_k_norm_g", "w_out_a", "w_out_b", "w_out_c", "w_o")


def _dot(a, b):
    return jnp.dot(a, b, preferred_element_type=F32)


def _dot_nt(a, b):
    return lax.dot_general(a, b, (((1,), (1,)), ((), ())), preferred_element_type=F32)


def _dot_tn(a, b):
    return lax.dot_general(a, b, (((0,), (0,)), ((), ())), preferred_element_type=F32)


def _grid_step(grid):
    step = pl.program_id(0)
    for a in range(1, len(grid)):
        step = step * grid[a] + pl.program_id(a)
    n = 1
    for g in grid:
        n *= g
    return step, n


def _write_windows(buf_ref, stages, sems, step, nsteps, puts):
    slot = step % 2
    for t, (v, dst) in enumerate(puts):
        cp = pltpu.make_async_copy(stages[t].at[slot], dst, sems.at[t, slot])

        @pl.when(step >= 2)
        def _():
            cp.wait()

        stages[t][slot] = v.astype(stages[t].dtype).reshape(stages[t].shape[1:])
        cp.start()

    @pl.when(step == nsteps - 1)
    def _():
        for t, (v, dst) in enumerate(puts):
            pltpu.make_async_copy(stages[t].at[slot], dst, sems.at[t, slot]).wait()
            if nsteps > 1:
                pltpu.make_async_copy(stages[t].at[1 - slot], dst, sems.at[t, 1 - slot]).wait()


def _pcall(name, fn, grid, ins, outs, into=None):
    n_in = len(ins)
    n_out = len(outs)
    acc_axis = len(grid) - 1
    is_acc = [len(o) > 4 and o[4] for o in outs]
    outs = [o[:4] for o in outs]
    targets = into[1] if into is not None else []
    n_t = len(targets)

    def body(*refs):
        vals = fn(*[r[...].astype(F32) for r in refs[:n_in]])
        if not isinstance(vals, (tuple, list)):
            vals = (vals,)
        o0 = n_in + (1 if n_t else 0)
        for k in range(n_out):
            r = refs[o0 + k]
            v = vals[k].astype(r.dtype).reshape(r.shape)
            if is_acc[k]:
                first = pl.program_id(acc_axis) == 0

                @pl.when(first)
                def _():
                    r[...] = v

                @pl.when(jnp.logical_not(first))
                def _():
                    r[...] += v
            else:
                r[...] = v
        if n_t:
            buf_ref = refs[o0 + n_out]
            stages = refs[o0 + n_out + 1:o0 + n_out + 1 + n_t]
            ids = [pl.program_id(a) for a in range(len(grid))]
            step, nsteps = _grid_step(grid)
            _write_windows(buf_ref, stages, refs[-1], step, nsteps,
                           [(vals[n_out + t], targets[t][1](buf_ref, *ids)) for t in range(n_t)])

    in_specs = [pl.BlockSpec(bs, im) for _, bs, im in ins]
    out_specs = [pl.BlockSpec(bs, im) for _, _, bs, im in outs]
    out_shape = [jax.ShapeDtypeStruct(sh, dt) for sh, dt, _, _ in outs]
    args = [a for a, _, _ in ins]
    extra = {}
    if n_t:
        buf = into[0]
        in_specs.append(pl.BlockSpec(memory_space=pl.ANY))
        out_specs.append(pl.BlockSpec(memory_space=pl.ANY))
        out_shape.append(jax.ShapeDtypeStruct(buf.shape, buf.dtype))
        args.append(buf)
        extra = dict(input_output_aliases={n_in: n_out},
                     scratch_shapes=[pltpu.VMEM((2,) + tuple(bs), buf.dtype) for bs, _ in targets]
                     + [pltpu.SemaphoreType.DMA((n_t, 2))])
    return pl.pallas_call(
        body,
        grid=grid,
        in_specs=in_specs,
        out_specs=out_specs,
        out_shape=out_shape,
        name=name,
        compiler_params=pltpu.CompilerParams(
            dimension_semantics=("arbitrary",) * len(grid), vmem_limit_bytes=VMEM_LIMIT),
        **extra,
    )(*args)


def _mm(name, a, b, *, ta=False, tb=False, out_dtype=F32, add=None, dep=None, b_words=False, tm=2048, tn=1024, tk=1024):
    if ta:
        K, M = a.shape
    else:
        M, K = a.shape
    bshape = (b.shape[0], 2 * b.shape[1]) if b_words else b.shape
    if tb:
        N, K2 = bshape
    else:
        K2, N = bshape
    assert K == K2, (name, a.shape, b.shape)
    tm, tn, tk = min(tm, M), min(tn, N), min(tk, K)
    assert M % tm == 0 and N % tn == 0 and K % tk == 0, (name, M, N, K)
    nk = K // tk
    dims = (((0 if ta else 1,), (1 if tb else 0,)), ((), ()))
    a_spec = pl.BlockSpec((tk, tm), lambda j, i, k: (k, i)) if ta else pl.BlockSpec((tm, tk), lambda j, i, k: (i, k))
    bw = 2 if b_words else 1
    assert not b_words or (tk if tb else tn) == bshape[1]
    b_spec = (pl.BlockSpec((tn, tk // bw), lambda j, i, k: (j, k)) if tb
              else pl.BlockSpec((tk, tn // bw), lambda j, i, k: (k, j)))
    o_spec = pl.BlockSpec((tm, tn), lambda j, i, k: (i, j))
    has_add = add is not None
    n_in = 2 + has_add + (dep is not None)

    def body(*refs):
        a_ref, b_ref = refs[0], refs[1]
        add_ref = refs[2] if has_add else None
        o_ref = refs[n_in]
        bb = b_ref[...]
        if b_words:
            lo, hi = _unpack_words(bb)
            first = (pl.program_id(0) * tn) if tb else (pl.program_id(2) * tk)
            r = first + lax.broadcasted_iota(jnp.int32, lo.shape, 0)
            pad = jnp.logical_and(r >= KPE_END, r < KPE_END + NINP - NIN)
            bb = jnp.concatenate([jnp.where(pad, 0.0, lo), jnp.where(pad, 0.0, hi)], axis=1)
        p = lax.dot_general(a_ref[...].astype(BF16), bb.astype(BF16), dims, preferred_element_type=F32)
        if nk == 1:
            if has_add:
                p = p + add_ref[...]
            o_ref[...] = p.astype(out_dtype)
        else:
            acc = refs[-1]
            k = pl.program_id(2)

            @pl.when(k == 0)
            def _():
                acc[...] = p

            @pl.when(k > 0)
            def _():
                acc[...] += p

            @pl.when(k == nk - 1)
            def _():
                r = acc[...]
                if has_add:
                    r = r + add_ref[...]
                o_ref[...] = r.astype(out_dtype)

    in_specs = [a_spec, b_spec] + ([o_spec] if has_add else []) + ([pl.BlockSpec(memory_space=pl.ANY)] if dep is not None else [])
    args = [a, b] + ([add] if has_add else []) + ([dep] if dep is not None else [])
    return pl.pallas_call(
        body,
        grid=(N // tn, M // tm, nk),
        in_specs=in_specs,
        out_specs=o_spec,
        out_shape=jax.ShapeDtypeStruct((M, N), out_dtype),
        scratch_shapes=[pltpu.VMEM((tm, tn), F32)] if nk > 1 else [],
        name=name,
        compiler_params=pltpu.CompilerParams(
            dimension_semantics=("arbitrary", "arbitrary", "arbitrary"), vmem_limit_bytes=VMEM_LIMIT),
    )(*args)


def _vjp_of(f, n_diff):
    def g(*args, n_prim):
        prim = args[:n_diff]
        consts = args[n_diff:n_prim]
        cts = args[n_prim:]
        _, pull = jax.vjp(lambda *p: f(*p, *consts), *prim)
        out = jax.eval_shape(lambda *p: f(*p, *consts), *prim)
        if isinstance(out, (tuple, list)):
            cts = tuple(c.astype(o.dtype) for c, o in zip(cts, out))
        else:
            cts = cts[0].astype(out.dtype)
        return pull(cts)
    return g


def _rms(x, g, n=None):
    n = x.shape[-1] if n is None else n
    ms = jnp.sum(x * x, axis=-1, keepdims=True) / n
    return x * lax.rsqrt(ms + EPS) * g


def _silu(z):
    return z * jax.nn.sigmoid(z)


def _roll_rows(u, k):
    n = u.shape[0]
    r = pltpu.roll(u, k % n, 0)
    t = lax.broadcasted_iota(jnp.int32, u.shape, 0)
    if k > 0:
        return jnp.where(t >= k, r, 0.0)
    return jnp.where(t < n + k, r, 0.0)


@functools.partial(jax.custom_vjp, nondiff_argnums=(1,))
def _shift(u, k):
    return _roll_rows(u, k)


def _shift_fwd(u, k):
    return _roll_rows(u, k), None


def _shift_bwd(k, _, g):
    return (_roll_rows(g, -k),)


_shift.defvjp(_shift_fwd, _shift_bwd)


@functools.partial(jax.custom_vjp, nondiff_argnums=(1,))
def _lane_roll(u, k):
    return pltpu.roll(u, k % LANE, 1)


def _lane_roll_fwd(u, k):
    return pltpu.roll(u, k % LANE, 1), None


def _lane_roll_bwd(k, _, g):
    return (pltpu.roll(g, (-k) % LANE, 1),)


_lane_roll.defvjp(_lane_roll_fwd, _lane_roll_bwd)


def _conv_math(ab, ac, ax, az, cw, cb):
    u = ac * ax
    conv = cb + _shift(u, 2) * cw[0:1] + _shift(u, 1) * cw[1:2] + u * cw[2:3]
    return ab * conv * _silu(az)


def _mla_pre_math(cq, ckv, gq, gkv):
    return _rms(cq, gq), _rms(ckv, gkv)


def _rope_math(q, kn, kpe, gq, gk, c, s1, s2):
    lane = lax.broadcasted_iota(jnp.int32, kpe.shape, 1)
    pe = _lane_roll(jnp.where(lane < ROPE, kpe, 0.0), NOPE)

    def one(t, g):
        tn = _rms(t, g, QK)
        return tn * c + _lane_roll(tn, -16) * s1 + _lane_roll(tn, 16) * s2

    qs, ks = [], []
    for h in range(NH):
        sl = slice(h * QKP, (h + 1) * QKP)
        qs.append(one(q[:, sl], gq))
        ks.append(one(kn[:, sl] + pe, gk))
    return jnp.concatenate(qs, axis=1), jnp.concatenate(ks, axis=1)


def _gate_math(o, z):
    return o * _silu(z)


def _mergec_math(o0, o1, o2, l0, l1, l2, cz):
    m = lax.stop_gradient(jnp.maximum(jnp.maximum(l0, l1), l2))
    e0, e1, e2 = jnp.exp(l0 - m), jnp.exp(l1 - m), jnp.exp(l2 - m)
    den = e0 + e1 + e2
    oc = (e0 / den) * o0 + (e1 / den) * o1 + (e2 / den) * o2
    return oc * _silu(cz)


def _merge_math(g0, g1, g2, b0, b1, b2, pa, pb, pc):
    return (jax.nn.sigmoid(g0 + b0) * pa + jax.nn.sigmoid(g1 + b1) * pb) + jax.nn.sigmoid(g2 + b2) * pc


MLA_T = 256
MLA_UNROLL = True


def _mla_fwd(q, k, v):
    B = q.shape[0]
    T = MLA_T
    NB = S // T

    def body(q_ref, k_ref, v_ref, o_ref, l_ref):
        row = lax.broadcasted_iota(jnp.int32, (T, T), 0)
        col = lax.broadcasted_iota(jnp.int32, (T, T), 1)
        lo = _lo_mask((T, LANE))

        for qi in range(NB):
            qb = q_ref[qi * T:(qi + 1) * T, :]

            def step(j, carry, diagonal):
                m, l, acc = carry
                off = pl.multiple_of(j * T, T)
                kb = k_ref[pl.ds(off, T), :]
                vb = v_ref[pl.ds(off, T), :]
                ss = []
                for e in (0, 1):
                    se = _dot_nt(qb[:, e * QKP:(e + 1) * QKP], kb[:, e * QKP:(e + 1) * QKP]) * MLA_SCALE
                    ss.append(jnp.where(col <= row, se, NEG) if diagonal else se)
                s = jnp.concatenate(ss, axis=0)
                m_new = jnp.maximum(m, jnp.max(s, axis=-1, keepdims=True))
                a = jnp.exp(m - m_new)
                p = jnp.exp(s - m_new)
                l = a * l + jnp.sum(p, axis=-1, keepdims=True)
                acc = a * acc + _dot(p.astype(BF16), vb)
                return m_new, l, acc

            init = (jnp.full((2 * T, 1), NEG, F32), jnp.zeros((2 * T, 1), F32), jnp.zeros((2 * T, LANE), F32))
            carry = lax.fori_loop(0, qi, functools.partial(step, diagonal=False), init, unroll=MLA_UNROLL)
            m, l, acc = step(qi, carry, True)
            o = acc / l
            lse = m + jnp.log(l)
            o_ref[qi * T:(qi + 1) * T, :] = jnp.where(lo, o[:T], o[T:])
            l_ref[qi * T:(qi + 1) * T, :] = jnp.where(lo, lse[:T], lse[T:])

    def spec(w):
        return pl.BlockSpec((None, S, w), lambda b, hp: (b, 0, hp))

    return pl.pallas_call(
        body,
        grid=(B, NH // 2),
        in_specs=[spec(2 * QKP), spec(2 * QKP), spec(LANE)],
        out_specs=[spec(LANE), spec(LANE)],
        out_shape=[jax.ShapeDtypeStruct((B, S, NH * VD), F32)] * 2,
        name="mla_attn_fwd",
        compiler_params=pltpu.CompilerParams(dimension_semantics=("arbitrary",) * 2, vmem_limit_bytes=VMEM_LIMIT),
    )(q, k, v)


def _mla_bwd(q, k, v, do, o, lse):
    B = q.shape[0]
    T = MLA_T
    NB = S // T

    def body(q_ref, k_ref, v_ref, do_ref, o_ref, l_ref, dq_ref, dk_ref, dv_ref, delta_ref, dqt_ref):
        delta_ref[...] = _head_sum(do_ref[...] * o_ref[...])
        row = lax.broadcasted_iota(jnp.int32, (T, T), 0)
        col = lax.broadcasted_iota(jnp.int32, (T, T), 1)
        lo = _lo_mask((T, LANE))
        tn_t = (((0,), (1,)), ((), ()))

        for j in range(NB):
            krows = slice(j * T, (j + 1) * T)
            kb = k_ref[krows, :]
            vb = v_ref[krows, :]
            dkt = [jnp.zeros((QKP, T), F32), jnp.zeros((QKP, T), F32)]
            dvt = jnp.zeros((LANE, T), F32)
            for i in range(j, NB):
                qrows = slice(i * T, (i + 1) * T)
                qb = q_ref[qrows, :]
                do2 = _stack_heads(do_ref[qrows, :], lo).astype(BF16)
                lb = l_ref[qrows, :]
                db = delta_ref[qrows, :]
                dp2 = _dot_nt(do2, vb)
                ps = []
                for e in (0, 1):
                    cols = slice(e * QKP, (e + 1) * QKP)
                    qe, ke = qb[:, cols], kb[:, cols]
                    s = _dot_nt(qe, ke) * MLA_SCALE
                    if i == j:
                        s = jnp.where(col <= row, s, NEG)
                    p = jnp.exp(s - lb[:, e * HD:e * HD + 1])
                    ps.append(p.astype(BF16))
                    ds = (p * (dp2[e * T:(e + 1) * T] - db[:, e * HD:e * HD + 1]) * MLA_SCALE).astype(BF16)
                    dkt[e] = dkt[e] + _dot_tn(qe, ds)
                    dq_t = lax.dot_general(ke, ds, tn_t, preferred_element_type=F32)
                    if j == 0:
                        dqt_ref[e, :, qrows] = dq_t
                    else:
                        dqt_ref[e, :, qrows] += dq_t
                dvt = dvt + _dot_tn(do2, jnp.concatenate(ps, axis=0))
            dk_ref[krows, 0:QKP] = dkt[0].T
            dk_ref[krows, QKP:2 * QKP] = dkt[1].T
            dv_ref[krows, :] = dvt.T
        dq_ref[:, 0:QKP] = dqt_ref[0].T
        dq_ref[:, QKP:2 * QKP] = dqt_ref[1].T

    def spec(w):
        return pl.BlockSpec((None, S, w), lambda b, hp: (b, 0, hp))

    return pl.pallas_call(
        body,
        grid=(B, NH // 2),
        in_specs=[spec(2 * QKP), spec(2 * QKP), spec(LANE), spec(LANE), spec(LANE), spec(LANE)],
        out_specs=[spec(2 * QKP), spec(2 * QKP), spec(LANE)],
        out_shape=[jax.ShapeDtypeStruct((B, S, NH * QKP), F32), jax.ShapeDtypeStruct((B, S, NH * QKP), F32),
                   jax.ShapeDtypeStruct((B, S, NH * VD), F32)],
        scratch_shapes=[pltpu.VMEM((S, LANE), F32), pltpu.VMEM((2, QKP, S), F32)],
        name="mla_attn_bwd",
        compiler_params=pltpu.CompilerParams(dimension_semantics=("arbitrary",) * 2, vmem_limit_bytes=VMEM_LIMIT),
    )(q, k, v, do, o, lse)


def _lo_mask(shape):
    return lax.broadcasted_iota(jnp.int32, shape, len(shape) - 1) < HD


def _head_sum(u):
    r = lax.broadcasted_iota(jnp.int32, (LANE, LANE), 0) < HD
    c = lax.broadcasted_iota(jnp.int32, (LANE, LANE), 1) < HD
    ones = jnp.where(r == c, 1.0, 0.0).astype(BF16)
    hi = u.astype(BF16)
    lo = (u - hi.astype(F32)).astype(BF16)
    return _dot(hi, ones) + _dot(lo, ones)


def _head_sum_1(u):
    r = lax.broadcasted_iota(jnp.int32, (LANE, LANE), 0) < HD
    c = lax.broadcasted_iota(jnp.int32, (LANE, LANE), 1) < HD
    return _dot(u.astype(BF16), jnp.where(r == c, 1.0, 0.0).astype(BF16))


def _rms2_scale(x):
    return lax.rsqrt(_head_sum(x * x) / HD + EPS)


def _rms2(x, g):
    return x * _rms2_scale(x) * g


def _rms2_bwd(x, r, g, dy):
    xn = x * r
    t = dy * g
    dx = r * (t - xn * (_head_sum_1(xn * t) * (1.0 / HD)))
    return dx, jnp.sum(dy * xn, axis=0, keepdims=True)


def _dil_bias(t_ref, gi, d):
    qq = lax.broadcasted_iota(jnp.int32, (QB, QB), 0)
    kk = lax.broadcasted_iota(jnp.int32, (QB, QB), 1)
    jc = (qq - kk).astype(F32)
    rows = []
    for e in (0, 1):
        sl = t_ref[2 * gi + e:2 * gi + e + 1, :] * float(d)
        bp = jnp.where(kk >= qq, -sl * (jc + float(QB)), NEG)
        bc = jnp.where(kk <= qq, -sl * jc, NEG)
        rows.append(jnp.concatenate([bp, bc], axis=1))
    return jnp.concatenate(rows, axis=0)


def _dil_rows(cur, d):
    return pl.ds(cur, QB, stride=d) if d > 1 else pl.ds(pl.multiple_of(cur, QB), QB)


def _dil_walk(d, block, full):
    if d == 1:
        block(0, None)

        def body(i, c):
            block(i * QB, (i - 1) * QB)
            return c
        lax.fori_loop(1, S // QB, body, 0, unroll=True if full else 5)
    elif d == 16:
        def body(r, c):
            block(r, None)
            return c
        lax.fori_loop(0, d, body, 0, unroll=True if full else 4)
    else:
        nb = S // d // QB

        def cls(r, c):
            block(r, None)

            def body(i, c2):
                block(r + i * QB * d, r + (i - 1) * QB * d)
                return c2
            lax.fori_loop(1, nb, body, 0, unroll=True)
            return c
        lax.fori_loop(0, d, cls, 0, unroll=full)


def _stack_heads(x, lo):
    return jnp.concatenate([jnp.where(lo, x, 0.0), jnp.where(lo, 0.0, x)], axis=0)


def _dilc_fwd(proj3, gq, gk, tab):
    B = proj3.shape[0]

    def body(q_ref, k_ref, v_ref, cz_ref, gq_ref, gk_ref, t_ref, y_ref, o_ref, l_ref, qs, ks, vs):
        g = pl.program_id(2)
        lo = _lo_mask((QB, LANE))

        def group(gi):
            d = DIL[gi][1]
            qs[...] = _rms2(q_ref[...].astype(F32), gq_ref[gi:gi + 1, :])
            ks[...] = _rms2(k_ref[...].astype(F32), gk_ref[gi:gi + 1, :])
            vs[...] = v_ref[...].astype(F32)
            bias = _dil_bias(t_ref, gi, d)

            def block(cur, prev):
                rows = _dil_rows(cur, d)
                q2 = _stack_heads(qs[rows, :], lo).astype(BF16)
                kc, vc = ks[rows, :], vs[rows, :]
                if prev is None:
                    kcat, vcat, b = kc, vc, bias[:, QB:]
                else:
                    prow = _dil_rows(prev, d)
                    kcat = jnp.concatenate([ks[prow, :], kc], axis=0)
                    vcat = jnp.concatenate([vs[prow, :], vc], axis=0)
                    b = bias
                s = _dot_nt(q2, kcat.astype(BF16)) * DIL_SCALE + b
                m = jnp.max(s, axis=-1, keepdims=True)
                p = jnp.exp(s - m)
                l = jnp.sum(p, axis=-1, keepdims=True)
                o = _dot(p.astype(BF16), vcat.astype(BF16)) / l
                lse = m + jnp.log(l)
                o_ref[gi, rows, :] = jnp.where(lo, o[:QB], o[QB:])
                l_ref[gi, rows, :] = jnp.where(lo, lse[:QB], lse[QB:])

            _dil_walk(d, block, True)

        for gi in range(NG):
            pl.when(g == gi)(functools.partial(group, gi))

        @pl.when(g == NG - 1)
        def _():
            y_ref[...] = _mergec_math(o_ref[0], o_ref[1], o_ref[2], l_ref[0], l_ref[1], l_ref[2],
                                      cz_ref[...].astype(F32)).astype(BF16)

    def col(base):
        return pl.BlockSpec((None, S, LANE), lambda b, hp, g: (b, 0, base // LANE + 4 * g + hp))

    gspec = pl.BlockSpec((NG, LANE), lambda b, hp, g: (0, 0))
    saved = pl.BlockSpec((NG, None, S, LANE), lambda b, hp, g: (0, b, 0, hp))
    return pl.pallas_call(
        body,
        grid=(B, 4, NG),
        in_specs=[col(O_DQ), col(O_DK), col(O_DV),
                  pl.BlockSpec((None, S, LANE), lambda b, hp, g: (b, 0, O_CZ // LANE + hp)),
                  gspec, gspec, pl.BlockSpec((None, 8, LANE), lambda b, hp, g: (hp, 0, 0))],
        out_specs=[pl.BlockSpec((None, S, LANE), lambda b, hp, g: (b, 0, hp)), saved, saved],
        out_shape=[jax.ShapeDtypeStruct((B, S, DWID), BF16), jax.ShapeDtypeStruct((NG, B, S, DWID), F32),
                   jax.ShapeDtypeStruct((NG, B, S, DWID), F32)],
        scratch_shapes=[pltpu.VMEM((S, LANE), F32)] * 3,
        name="dil_mixer_fwd",
        compiler_params=pltpu.CompilerParams(dimension_semantics=("arbitrary",) * 3, vmem_limit_bytes=VMEM_LIMIT),
    )(proj3, proj3, proj3, proj3, gq, gk, tab)


MERGE_ROWS = 256


def _dilc_bwd(proj3, gq, gk, tab, o_all, l_all, d_yc, dproj3):
    B = proj3.shape[0]

    def body(q_ref, k_ref, v_ref, cz_ref, gq_ref, gk_ref, t_ref, o_ref, l_ref, dy_ref, dp_in,
             dp_out, dgq_out, dgk_out, qs, ks, vs, dos, dls, dqs, dks, dvs, rqs, rks, dczs,
             st_q, st_k, st_v, st_z, sems, sem_z):
        b_, hp, g = pl.program_id(0), pl.program_id(1), pl.program_id(2)
        col = lambda base: pl.ds(pl.multiple_of(base + hp * LANE, LANE), LANE)
        lo = _lo_mask((QB, LANE))

        @pl.when(jnp.logical_and(jnp.logical_and(pl.program_id(0) == 0, pl.program_id(1) == 0), g == 0))
        def _():
            dgq_out[...] = jnp.zeros((NG, LANE), F32)
            dgk_out[...] = jnp.zeros((NG, LANE), F32)

        @pl.when(g == 0)
        def _():
            def chunk(i, carry):
                rows = pl.ds(pl.multiple_of(i * MERGE_ROWS, MERGE_ROWS), MERGE_ROWS)
                ls = [l_ref[j, rows, :] for j in range(NG)]
                m = jnp.maximum(jnp.maximum(ls[0], ls[1]), ls[2])
                es = [jnp.exp(t - m) for t in ls]
                den = (es[0] + es[1]) + es[2]
                al = [e / den for e in es]
                os_ = [o_ref[j, rows, :] for j in range(NG)]
                oc = (al[0] * os_[0] + al[1] * os_[1]) + al[2] * os_[2]
                cz = cz_ref[rows, :].astype(F32)
                sg = jax.nn.sigmoid(cz)
                dy = dy_ref[rows, :]
                d_oc = dy * (cz * sg)
                dczs[rows, :] = (dy * oc * (sg * (1.0 + cz * (1.0 - sg)))).astype(BF16)
                ts = [_head_sum_1(d_oc * os_[j]) for j in range(NG)]
                tbar = (al[0] * ts[0] + al[1] * ts[1]) + al[2] * ts[2]
                for j in range(NG):
                    dos[j, rows, :] = al[j] * d_oc
                    dls[j, rows, :] = al[j] * (ts[j] - tbar)
                return carry
            lax.fori_loop(0, S // MERGE_ROWS, chunk, 0)
            _write_windows(dp_out, [st_z], sem_z, b_ * 4 + hp, B * 4, [(dczs[...], dp_out.at[b_, :, col(O_CZ)])])

        def group(gi):
            d = DIL[gi][1]
            xq, xk = q_ref[...].astype(F32), k_ref[...].astype(F32)
            rqs[...] = _rms2_scale(xq)
            rks[...] = _rms2_scale(xk)
            qs[...] = xq * rqs[...] * gq_ref[gi:gi + 1, :]
            ks[...] = xk * rks[...] * gk_ref[gi:gi + 1, :]
            vs[...] = v_ref[...].astype(F32)
            dks[...] = jnp.zeros((S, LANE), F32)
            dvs[...] = jnp.zeros((S, LANE), F32)
            bias = _dil_bias(t_ref, gi, d)

            def block(cur, prev):
                rows = _dil_rows(cur, d)
                q2 = _stack_heads(qs[rows, :], lo).astype(BF16)
                dob = dos[gi, rows, :]
                do2 = _stack_heads(dob, lo).astype(BF16)
                kc, vc = ks[rows, :], vs[rows, :]
                if prev is None:
                    kcat, vcat, b = kc, vc, bias[:, QB:]
                else:
                    prow = _dil_rows(prev, d)
                    kcat = jnp.concatenate([ks[prow, :], kc], axis=0)
                    vcat = jnp.concatenate([vs[prow, :], vc], axis=0)
                    b = bias
                kcat = kcat.astype(BF16)
                vcat = vcat.astype(BF16)
                lse_b = l_ref[gi, rows, :]
                corr_b = dls[gi, rows, :] - _head_sum_1(dob * o_ref[gi, rows, :])
                lse2 = jnp.concatenate([lse_b[:, 0:1], lse_b[:, HD:HD + 1]], axis=0)
                corr2 = jnp.concatenate([corr_b[:, 0:1], corr_b[:, HD:HD + 1]], axis=0)
                s = _dot_nt(q2, kcat) * DIL_SCALE + b
                p = jnp.exp(s - lse2)
                ds = (p * (_dot_nt(do2, vcat) + corr2) * DIL_SCALE).astype(BF16)
                dq2 = _dot(ds, kcat)
                dqs[rows, :] = jnp.where(lo, dq2[:QB], dq2[QB:])
                dk = _dot_tn(ds, q2)
                dv = _dot_tn(p.astype(BF16), do2)
                if prev is None:
                    dks[rows, :] += dk
                    dvs[rows, :] += dv
                else:
                    dks[prow, :] += dk[:QB]
                    dvs[prow, :] += dv[:QB]
                    dks[rows, :] += dk[QB:]
                    dvs[rows, :] += dv[QB:]

            _dil_walk(d, block, False)

            dxq, dgq = _rms2_bwd(q_ref[...].astype(F32), rqs[...], gq_ref[gi:gi + 1, :], dqs[...])
            dgq_out[gi:gi + 1, :] += dgq
            dxk, dgk = _rms2_bwd(k_ref[...].astype(F32), rks[...], gk_ref[gi:gi + 1, :], dks[...])
            dgk_out[gi:gi + 1, :] += dgk
            step, nsteps = _grid_step((B, 4, NG))
            _write_windows(dp_out, [st_q, st_k, st_v], sems, step, nsteps,
                           [(dxq, dp_out.at[b_, :, col(O_DQ + gi * DWID)]), (dxk, dp_out.at[b_, :, col(O_DK + gi * DWID)]),
                            (dvs[...], dp_out.at[b_, :, col(O_DV + gi * DWID)])])

        for gi in range(NG):
            pl.when(g == gi)(functools.partial(group, gi))

    def col(base):
        return pl.BlockSpec((None, S, LANE), lambda b, hp, g: (b, 0, base // LANE + 4 * g + hp))

    gspec = pl.BlockSpec((NG, LANE), lambda b, hp, g: (0, 0))
    saved = pl.BlockSpec((NG, None, S, LANE), lambda b, hp, g: (0, b, 0, hp))
    per_pair = pl.BlockSpec((None, S, LANE), lambda b, hp, g: (b, 0, hp))
    return pl.pallas_call(
        body,
        grid=(B, 4, NG),
        in_specs=[col(O_DQ), col(O_DK), col(O_DV),
                  pl.BlockSpec((None, S, LANE), lambda b, hp, g: (b, 0, O_CZ // LANE + hp)),
                  gspec, gspec, pl.BlockSpec((None, 8, LANE), lambda b, hp, g: (hp, 0, 0)),
                  saved, saved, per_pair, pl.BlockSpec(memory_space=pl.ANY)],
        out_specs=[pl.BlockSpec(memory_space=pl.ANY), gspec, gspec],
        out_shape=[jax.ShapeDtypeStruct(dproj3.shape, dproj3.dtype), jax.ShapeDtypeStruct((NG, LANE), F32),
                   jax.ShapeDtypeStruct((NG, LANE), F32)],
        input_output_aliases={10: 0},
        scratch_shapes=[pltpu.VMEM((S, LANE), F32)] * 3 + [pltpu.VMEM((NG, S, LANE), F32)] * 2
        + [pltpu.VMEM((S, LANE), F32)] * 5 + [pltpu.VMEM((S, LANE), BF16)] + [pltpu.VMEM((2, S, LANE), BF16)] * 4
        + [pltpu.SemaphoreType.DMA((3, 2)), pltpu.SemaphoreType.DMA((1, 2))],
        name="dil_mixer_bwd",
        compiler_params=pltpu.CompilerParams(dimension_semantics=("arbitrary",) * 3, vmem_limit_bytes=VMEM_LIMIT),
    )(proj3, proj3, proj3, proj3, gq, gk, tab, o_all, l_all, d_yc, dproj3)


def _dil_slopes():
    slopes = (2.0 ** (-8.0 * np.arange(1, NG * DH + 1, dtype=np.float32) / (NG * DH))).astype(np.float32).reshape(NG, DH)
    tab = np.zeros((4, 8, LANE), np.float32)
    for hp in range(4):
        for gi in range(NG):
            for e in (0, 1):
                tab[hp, 2 * gi + e, :] = slopes[gi, 2 * hp + e]
    return jnp.asarray(tab)


def _rope_tables():
    inv = ROPE_THETA ** (-jnp.arange(0, ROPE, 2, dtype=F32) / ROPE)
    ang = jnp.arange(S, dtype=F32)[:, None] * inv[None, :]
    cos, sin = jnp.cos(ang), jnp.sin(ang)
    z16 = jnp.zeros((S, 16), F32)
    c = jnp.concatenate([jnp.ones((S, NOPE), F32), cos, cos, jnp.zeros((S, 32), F32)], axis=1)
    s1 = jnp.concatenate([jnp.zeros((S, NOPE), F32), -sin, z16, jnp.zeros((S, 32), F32)], axis=1)
    s2 = jnp.concatenate([jnp.zeros((S, NOPE), F32), z16, sin, jnp.zeros((S, 32), F32)], axis=1)
    return c, s1, s2


def _pad_heads_uq(w):
    return jnp.pad(w.reshape(QL, NH, QK), ((0, 0), (0, 0), (0, QKP - QK))).reshape(QL, NH * QKP)


def _unpad_heads_uq(g):
    return g.reshape(QL, NH, QKP)[:, :, :QK].reshape(QL, NH * QK)


def _split_ukv(w):
    w3 = w.reshape(KVL, NH, NOPE + VD)
    uk = jnp.pad(w3[:, :, :NOPE], ((0, 0), (0, 0), (0, QKP - NOPE))).reshape(KVL, NH * QKP)
    return uk, w3[:, :, NOPE:].reshape(KVL, NH * VD)


def _join_ukv(guk, guv):
    return jnp.concatenate([guk.reshape(KVL, NH, QKP)[:, :, :NOPE], guv.reshape(KVL, NH, VD)],
                           axis=-1).reshape(KVL, NH * (NOPE + VD))


BR = 512
BRM = 256


def _layer_fwd(x, w, tabs, batch, rest=None):
    T = batch * S
    rope_c, rope_s1, rope_s2, dil_tab = tabs
    res = {"x": x}
    row = lambda c: (lambda i: (i, c))
    fix = lambda i: (0, 0)

    h = _pcall("norm_fwd", _rms, (T // BR,),
               [(x, (BR, D), row(0)), (w["norm_g"], (1, D), fix)],
               [((T, D), BF16, (BR, D), row(0))])[0]
    proj = _mm("in_proj", h, w["w_in_t"], tb=True, out_dtype=BF16, b_words=True, tm=2048, tn=1024)
    res["h"], res["proj"] = h, proj
    proj3 = proj.reshape(batch, S, NINP)
    if rest is not None:
        w = rest(proj)

    cblk = lambda s: (lambda j, b: (b, 0, 4 * s + j))
    y_a = _pcall("conv_fwd", _conv_math, (4, batch),
                 [(proj3, (None, S, LANE), cblk(0)), (proj3, (None, S, LANE), cblk(1)),
                  (proj3, (None, S, LANE), cblk(2)), (proj3, (None, S, LANE), cblk(3)),
                  (w["conv_w"], (3, LANE), lambda j, b: (0, j)), (w["conv_b"], (1, LANE), lambda j, b: (0, j))],
                 [((batch, S, CW), BF16, (None, S, LANE), lambda j, b: (b, 0, j))])[0].reshape(T, CW)
    res["y_a"] = y_a

    cqn, ckvn = _pcall("mla_pre_fwd", _mla_pre_math, (T // BR,),
                       [(proj, (BR, QL), row(O_CQ // QL)), (proj, (BR, KVL), row(O_CKV // KVL)),
                        (w["q_a_norm_g"], (1, QL), fix), (w["kv_a_norm_g"], (1, KVL), fix)],
                       [((T, QL), BF16, (BR, QL), row(0)), ((T, KVL), BF16, (BR, KVL), row(0))])
    w_uq_p = _pad_heads_uq(w["w_uq"])
    w_uk, w_uv = _split_ukv(w["w_ukv"])
    q = _mm("uq", cqn, w_uq_p, out_dtype=BF16)
    kn = _mm("uk", ckvn, w_uk, out_dtype=BF16)
    v = _mm("uv", ckvn, w_uv, out_dtype=BF16)
    nrr = S // BR
    tab_row = lambda i: (i % nrr, 0)
    qr, kr = _pcall("rope_fwd", _rope_math, (T // BR,),
                    [(q, (BR, NH * QKP), row(0)), (kn, (BR, NH * QKP), row(0)), (proj, (BR, LANE), row(O_KPE // LANE)),
                     (w["mla_q_norm_g"], (1, QKP), fix), (w["mla_k_norm_g"], (1, QKP), fix),
                     (rope_c, (BR, QKP), tab_row), (rope_s1, (BR, QKP), tab_row), (rope_s2, (BR, QKP), tab_row)],
                    [((T, NH * QKP), BF16, (BR, NH * QKP), row(0))] * 2)
    qr = qr.reshape(batch, S, NH * QKP)
    kr = kr.reshape(batch, S, NH * QKP)
    v = v.reshape(batch, S, NH * VD)
    o_b, l_b = _mla_fwd(qr, kr, v)
    ob2 = o_b.reshape(T, NH * VD)
    y_b = _pcall("gateb_fwd", _gate_math, (T // BR,),
                 [(ob2, (BR, 512), row(0)), (proj, (BR, 512), row(O_BZ // 512))],
                 [((T, 512), BF16, (BR, 512), row(0))])[0]
    res.update(cqn=cqn, ckvn=ckvn, q=q, kn=kn, qr=qr, kr=kr, v=v, o_b=o_b, l_b=l_b, ob2=ob2, y_b=y_b,
               w_uq_p=w_uq_p, w_uk=w_uk, w_uv=w_uv)

    gq2 = jnp.tile(w["dil_q_norm_g"].reshape(NG, HD), (1, 2))
    gk2 = jnp.tile(w["dil_k_norm_g"].reshape(NG, HD), (1, 2))
    y_c, o_all, l_all = _dilc_fwd(proj3, gq2, gk2, dil_tab)
    y_c = y_c.reshape(T, DWID)
    res.update(o_all=o_all, l_all=l_all, y_c=y_c)

    pa = _mm("out_a", y_a, w["w_out_a"], out_dtype=BF16)
    pb = _mm("out_b", y_b, w["w_out_b"], out_dtype=BF16)
    pc = _mm("out_c", y_c, w["w_out_c"], out_dtype=BF16)
    merged = _pcall("merge_fwd", _merge_math, (T // BRM,),
                    [(proj, (BRM, D), row(O_G // D + s)) for s in range(3)]
                    + [(w["b_gate"], (1, D), (lambda s: (lambda i: (0, s)))(s)) for s in range(3)]
                    + [(t, (BRM, D), row(0)) for t in (pa, pb, pc)],
                    [((T, D), BF16, (BRM, D), row(0))])[0]
    out = _mm("o_proj", merged, w["w_o"], add=x, tm=1024)
    res.update(pa=pa, pb=pb, pc=pc, merged=merged)
    return out, res


def _norm_bwd_math(x, g, dh, dy):
    _, pull = jax.vjp(_rms, x, g)
    dx, dg = pull(dh)
    return dx + dy, dg


def _layer_bwd(dy, w, res, tabs, batch, after_dw=None, after_merge=None):
    T = batch * S
    rope_c, rope_s1, rope_s2, dil_tab = tabs
    row = lambda c: (lambda i: (i, c))
    fix = lambda i: (0, 0)
    x, proj, h = res["x"], res["proj"], res["h"]
    proj3 = proj.reshape(batch, S, NINP)
    g = {}

    d_merged = _mm("o_proj_dx", dy, w["w_o"], tb=True)
    g["w_o"] = _mm("o_proj_dw", res["merged"], dy, ta=True, tm=1024, tk=2048)

    dproj = lax.empty((T, NINP), BF16)
    rows_of = lambda br: (lambda ref, i: ref.at[pl.ds(pl.multiple_of(i * br, br), br)])

    def merge_bwd(*args):
        dg0, dg1, dg2, db0, db1, db2, dpa, dpb, dpc = _vjp_of(_merge_math, 9)(*args, n_prim=9)
        return db0, db1, db2, dpa, dpb, dpc, jnp.concatenate([dg0, dg1, dg2], axis=1)

    db0, db1, db2, dpa, dpb, dpc, dproj = _pcall(
        "merge_bwd", merge_bwd, (T // BRM,),
        [(proj, (BRM, D), row(O_G // D + s)) for s in range(3)]
        + [(w["b_gate"], (1, D), (lambda s: (lambda i: (0, s)))(s)) for s in range(3)]
        + [(t, (BRM, D), row(0)) for t in (res["pa"], res["pb"], res["pc"])]
        + [(d_merged, (BRM, D), row(0))],
        [((1, D), F32, (1, D), fix, True)] * 3 + [((T, D), BF16, (BRM, D), row(0))] * 3,
        into=(dproj, [((BRM, 3 * D), lambda ref, i: rows_of(BRM)(ref, i).at[:, O_G:O_G + 3 * D])]))
    g["b_gate"] = jnp.concatenate([db0, db1, db2], axis=1)

    dep = after_merge(dpa) if after_merge is not None else None
    d_ya = _mm("out_a_dx", dpa, w["w_out_a"], tb=True, dep=dep)
    d_yb = _mm("out_b_dx", dpb, w["w_out_b"], tb=True)
    d_yc = _mm("out_c_dx", dpc, w["w_out_c"], tb=True)
    g["w_out_a"] = _mm("out_a_dw", res["y_a"], dpa, ta=True, tk=T)
    g["w_out_b"] = _mm("out_b_dw", res["y_b"], dpb, ta=True, tk=T)
    g["w_out_c"] = _mm("out_c_dw", res["y_c"], dpc, ta=True, tk=T)

    cblk = lambda s: (lambda j, b: (b, 0, 4 * s + j))
    oblk = lambda j, b: (b, 0, j)
    def conv_bwd(*args):
        d_ab, d_ac, d_ax, d_az, dcw, dcb = _vjp_of(_conv_math, 6)(*args, n_prim=6)
        return dcw, dcb, d_ab, d_ac, d_ax, d_az

    a_col = lambda s_: (lambda ref, j, b: ref.at[b, :, pl.ds(pl.multiple_of(O_A + s_ * CW + j * LANE, LANE), LANE)])
    g["conv_w"], g["conv_b"], dproj3 = _pcall(
        "conv_bwd", conv_bwd, (4, batch),
        [(proj3, (None, S, LANE), cblk(s)) for s in range(4)]
        + [(w["conv_w"], (3, LANE), lambda j, b: (0, j)), (w["conv_b"], (1, LANE), lambda j, b: (0, j)),
           (d_ya.reshape(batch, S, CW), (None, S, LANE), oblk)],
        [((3, CW), F32, (3, LANE), lambda j, b: (0, j), True), ((1, CW), F32, (1, LANE), lambda j, b: (0, j), True)],
        into=(dproj.reshape(batch, S, NINP), [((S, LANE), a_col(s_)) for s_ in range(4)]))
    dproj = dproj3.reshape(T, NINP)

    gate_bwd = functools.partial(_vjp_of(_gate_math, 2), n_prim=2)
    d_ob, dproj = _pcall("gateb_bwd", gate_bwd, (T // BR,),
                         [(res["ob2"], (BR, 512), row(0)), (proj, (BR, 512), row(O_BZ // 512)), (d_yb, (BR, 512), row(0))],
                         [((T, 512), F32, (BR, 512), row(0))],
                         into=(dproj, [((BR, 512), lambda ref, i: rows_of(BR)(ref, i).at[:, O_BZ:O_BZ + 512])]))
    dqr, dkr, dv = _mla_bwd(res["qr"], res["kr"], res["v"], d_ob.reshape(batch, S, NH * VD), res["o_b"], res["l_b"])
    nrr = S // BR
    tab_row = lambda i: (i % nrr, 0)
    def rope_bwd(*args):
        d_q, d_kn, d_kpe, dgq, dgk = _vjp_of(_rope_math, 5)(*args, n_prim=8)
        return d_q, d_kn, dgq, dgk, d_kpe

    d_q, d_kn, g["mla_q_norm_g"], g["mla_k_norm_g"], dproj = _pcall(
        "rope_bwd", rope_bwd, (T // BR,),
        [(res["q"], (BR, NH * QKP), row(0)), (res["kn"], (BR, NH * QKP), row(0)), (proj, (BR, LANE), row(O_KPE // LANE)),
         (w["mla_q_norm_g"], (1, QKP), fix), (w["mla_k_norm_g"], (1, QKP), fix),
         (rope_c, (BR, QKP), tab_row), (rope_s1, (BR, QKP), tab_row), (rope_s2, (BR, QKP), tab_row),
         (dqr.reshape(T, NH * QKP), (BR, NH * QKP), row(0)), (dkr.reshape(T, NH * QKP), (BR, NH * QKP), row(0))],
        [((T, NH * QKP), BF16, (BR, NH * QKP), row(0))] * 2 + [((1, QKP), F32, (1, QKP), fix, True)] * 2,
        into=(dproj, [((BR, LANE), lambda ref, i: rows_of(BR)(ref, i).at[:, O_KPE:O_KPE + LANE])]))
    dv = dv.reshape(T, NH * VD)
    d_cqn = _mm("uq_dx", d_q, res["w_uq_p"], tb=True)
    d_ckvn = _mm("uk_dx", d_kn, res["w_uk"], tb=True)
    d_ckvn = _mm("uv_dx", dv, res["w_uv"], tb=True, add=d_ckvn)
    g["w_uq"] = _unpad_heads_uq(_mm("uq_dw", res["cqn"], d_q, ta=True, tk=T))
    g["w_ukv"] = _join_ukv(_mm("uk_dw", res["ckvn"], d_kn, ta=True, tk=T),
                           _mm("uv_dw", res["ckvn"], dv, ta=True, tk=T))
    def pre_bwd(*args):
        d_cq, d_ckv, dgq, dgkv = _vjp_of(_mla_pre_math, 4)(*args, n_prim=4)
        return dgq, dgkv, jnp.concatenate([d_cq, d_ckv], axis=1)

    g["q_a_norm_g"], g["kv_a_norm_g"], dproj = _pcall(
        "mla_pre_bwd", pre_bwd, (T // BR,),
        [(proj, (BR, QL), row(O_CQ // QL)), (proj, (BR, KVL), row(O_CKV // KVL)),
         (w["q_a_norm_g"], (1, QL), fix), (w["kv_a_norm_g"], (1, KVL), fix),
         (d_cqn, (BR, QL), row(0)), (d_ckvn, (BR, KVL), row(0))],
        [((1, QL), F32, (1, QL), fix, True), ((1, KVL), F32, (1, KVL), fix, True)],
        into=(dproj, [((BR, QL + KVL), lambda ref, i: rows_of(BR)(ref, i).at[:, O_CQ:O_CQ + QL + KVL])]))

    gq2 = jnp.tile(w["dil_q_norm_g"].reshape(NG, HD), (1, 2))
    gk2 = jnp.tile(w["dil_k_norm_g"].reshape(NG, HD), (1, 2))
    dproj3, dgq, dgk = _dilc_bwd(proj3, gq2, gk2, dil_tab, res["o_all"], res["l_all"],
                                 d_yc.reshape(batch, S, DWID), dproj.reshape(batch, S, NINP))
    dproj = dproj3.reshape(T, NINP)
    g["dil_q_norm_g"] = dgq[:, :HD] + dgq[:, HD:]
    g["dil_k_norm_g"] = dgk[:, :HD] + dgk[:, HD:]

    g["w_in_t"] = _mm("in_proj_dw", dproj, h, ta=True, tm=1024, tk=T)
    dep = after_dw(g) if after_dw is not None else None
    d_h = _mm("in_proj_dx", dproj, w["w_in_t"], dep=dep, b_words=True, tm=1024, tk=NINP // 4)
    dx, g["norm_g"] = _pcall("norm_bwd", _norm_bwd_math, (T // BR,),
                             [(x, (BR, D), row(0)), (w["norm_g"], (1, D), fix), (d_h, (BR, D), row(0)),
                              (dy, (BR, D), row(0))],
                             [((T, D), F32, (BR, D), row(0)), ((1, D), F32, (1, D), fix, True)])
    return dx, g


def _loss_math(y, t):
    e = y - t
    return e * (1.0 / D), 0.5 * jnp.sum(jnp.sum(e * e, axis=-1, keepdims=True) / D, axis=0, keepdims=True)


ANY = pl.BlockSpec(memory_space=pl.ANY)
U32 = jnp.uint32
WSH = NIN // 4
WA = KPE_END
WB = WSH - WA
CWD = 512
PACK_ROWS = 1472
HW = PACK_W // 2


def _me():
    return lax.axis_index("x"), lax.axis_index("y"), lax.axis_index("c")


def _piece_rows(k):
    a = k * WSH + jnp.where(k > 0, NINP - NIN, 0)
    b = k * WSH + WA + (NINP - NIN)
    return ((0, pl.multiple_of(a, 8), WA), (WA, pl.multiple_of(b, 8), WB))


def _pack_words(lo, hi):
    ul = lax.bitcast_convert_type(lo.astype(BF16).astype(F32), U32)
    uh = lax.bitcast_convert_type(hi.astype(BF16).astype(F32), U32)
    w = jnp.bitwise_or(jnp.bitwise_and(uh, jnp.uint32(0xFFFF0000)), jnp.right_shift(ul, jnp.uint32(16)))
    return lax.bitcast_convert_type(w, F32)


def _unpack_words(w):
    w = lax.bitcast_convert_type(w, U32)
    lo = lax.bitcast_convert_type(jnp.left_shift(w, jnp.uint32(16)), F32)
    hi = lax.bitcast_convert_type(jnp.bitwise_and(w, jnp.uint32(0xFFFF0000)), F32)
    return lo, hi


def _all_gather(wc):
    def body(w_ref, ow_ref, send_sems, recv_sems):
        x, y, c = _me()
        k_me = 2 * x + y
        sib = (x, y, 1 - c)
        chips = [(1 - x, y), (x, 1 - y), (1 - x, 1 - y)]
        wcols = lambda cc: pl.ds(pl.multiple_of(cc * (CWD // 2), LANE), CWD // 2)

        def windows(k, cc):
            return [(w_ref.at[pl.ds(l0, n), wcols(cc)], ow_ref.at[pl.ds(p0, n), wcols(cc)])
                    for l0, p0, n in _piece_rows(k)]

        def copy(i, src, dst, to):
            return pltpu.make_async_remote_copy(src_ref=src, dst_ref=dst, send_sem=send_sems.at[i],
                                                recv_sem=recv_sems.at[i], device_id=to, device_id_type=MESH)

        def own_windows():
            return [(w_ref.at[pl.ds(l0, n)], ow_ref.at[pl.ds(p0, n)]) for l0, p0, n in _piece_rows(k_me)]

        first = [copy(12 + i, src, dst, sib) for i, (src, dst) in enumerate(own_windows())]
        for j, (cx, cy) in enumerate(chips):
            for i, (src, dst) in enumerate(windows(k_me, c)):
                first.append(copy(2 * j + i, src, dst, (cx, cy, c)))
        for cp in first:
            cp.start()
        passed = []
        for j, (cx, cy) in enumerate(chips):
            for i, (_, dst) in enumerate(windows(2 * cx + cy, c)):
                copy(2 * j + i, dst, dst, (cx, cy, c)).wait_recv()
                cp = copy(6 + 2 * j + i, dst, dst, sib)
                cp.start()
                passed.append(cp)
        for j, (cx, cy) in enumerate(chips):
            for i, (_, dst) in enumerate(windows(2 * cx + cy, 1 - c)):
                copy(6 + 2 * j + i, dst, dst, sib).wait_recv()
        for i, (_, dst) in enumerate(own_windows()):
            copy(12 + i, dst, dst, sib).wait_recv()
        for cp in first + passed:
            cp.wait_send()

    return pl.pallas_call(
        body,
        out_shape=jax.ShapeDtypeStruct((NINP, CWD), F32),
        in_specs=[ANY], out_specs=ANY,
        scratch_shapes=[pltpu.SemaphoreType.DMA((14,)), pltpu.SemaphoreType.DMA((14,))],
        name="weights_all_gather",
    )(wc)


HBM = pl.BlockSpec(memory_space=pltpu.HBM)
SEM = pl.BlockSpec(memory_space=pltpu.SEMAPHORE)
EFFECT = pltpu.SideEffectType.DATAFLOW_SIDE_EFFECTING


def _in_hbm(a):
    return pltpu.with_memory_space_constraint(a, pltpu.HBM)


def _ag_shard(w_ref, s_ref, lw_ref, ls_ref, k, with_w):
    pack = [(s_ref, ls_ref.at[k])]
    if not with_w:
        return pack
    return [(w_ref.at[pl.ds(l0, n)], lw_ref.at[pl.ds(p0, n)]) for l0, p0, n in _piece_rows(k)] + pack


def _ag_behind_copies(w_ref, s_ref, lw_ref, ls_ref, send_sems, recv_sems, with_w):
    x, y, c = _me()
    peers = [(1 - x, y, c), (x, 1 - y, c), (1 - x, 1 - y, c), (x, y, 1 - c)]
    mine, theirs = [], []
    for j, (px, py, pc) in enumerate(peers):
        for i, ((src, dst), (_, got)) in enumerate(zip(_ag_shard(w_ref, s_ref, lw_ref, ls_ref, 2 * x + y, with_w),
                                                       _ag_shard(w_ref, s_ref, lw_ref, ls_ref, 2 * px + py, with_w))):
            mk = lambda s_, d_: pltpu.make_async_remote_copy(
                src_ref=s_, dst_ref=d_, send_sem=send_sems.at[3 * j + i], recv_sem=recv_sems.at[3 * j + i],
                device_id=(px, py, pc), device_id_type=MESH)
            mine.append(mk(src, dst))
            theirs.append(mk(got, got))
    return mine, theirs


def _ag_behind_start(wc, sp, dep, tag):
    with_w = wc is not None
    if not with_w:
        wc = jnp.zeros((8, LANE), F32)
    lw = lax.empty((NINP, CWD) if with_w else (8, LANE), F32)

    def body(w_ref, s_ref, lw_ref, ls_ref, dep_ref, send_sems, recv_sems, w_thru, s_thru, lw_thru, ls_thru, token):
        mine, _ = _ag_behind_copies(w_ref, s_ref, lw_ref, ls_ref, send_sems, recv_sems, with_w)
        for cp in mine:
            cp.start()
        token[...] = jnp.zeros_like(token)

    return pl.pallas_call(
        body, name="weights_gather_start_" + tag,
        out_shape=(pltpu.SemaphoreType.DMA((12,)), pltpu.SemaphoreType.DMA((12,)), pltpu.HBM(wc.shape, wc.dtype),
                   pltpu.HBM(sp.shape, sp.dtype), pltpu.HBM(lw.shape, F32), pltpu.HBM((4, PACK_ROWS, PACK_W), BF16),
                   jax.ShapeDtypeStruct((8, LANE), F32)),
        in_specs=(HBM, HBM, HBM, HBM, ANY),
        out_specs=(SEM, SEM, HBM, HBM, HBM, HBM, pl.BlockSpec(memory_space=pltpu.VMEM)),
        input_output_aliases={0: 2, 1: 3, 2: 4, 3: 5},
        compiler_params=pltpu.CompilerParams(has_side_effects=EFFECT),
    )(_in_hbm(wc), _in_hbm(sp), _in_hbm(lw), _in_hbm(lax.empty((4, PACK_ROWS, PACK_W), BF16)), dep)


def _ag_behind_wait(send_sems, recv_sems, wc, sp, lw, ls, after, tag):
    with_w = lw.shape == (NINP, CWD)

    def body(w_ref, s_ref, lw_ref, ls_ref, send_sems, recv_sems, after_ref, w_dead, s_dead, lw_out, ls_out):
        mine, theirs = _ag_behind_copies(w_ref, s_ref, lw_ref, ls_ref, send_sems, recv_sems, with_w)
        for cp in mine:
            cp.wait_send()
        for cp in theirs:
            cp.wait_recv()

    out = pl.pallas_call(
        body, name="weights_gather_wait_" + tag,
        out_shape=(pltpu.HBM(wc.shape, wc.dtype), pltpu.HBM(sp.shape, sp.dtype), pltpu.HBM(lw.shape, lw.dtype),
                   pltpu.HBM(ls.shape, ls.dtype)),
        in_specs=(HBM, HBM, HBM, HBM, SEM, SEM, ANY), out_specs=(HBM, HBM, HBM, HBM),
        input_output_aliases={0: 0, 1: 1, 2: 2, 3: 3},
        compiler_params=pltpu.CompilerParams(has_side_effects=EFFECT),
    )(wc, sp, lw, ls, send_sems, recv_sems, after)
    return out[2], out[3]


def _rs_swap_copies(w_ref, s_ref, rw_ref, rs_ref, send_sems, recv_sems):
    x, y, c = _me()
    oc = 1 - c
    return [pltpu.make_async_remote_copy(src_ref=w_ref.at[:, pl.ds(pl.multiple_of(oc * (D // 2), LANE), D // 2)],
                                         dst_ref=rw_ref, send_sem=send_sems.at[0], recv_sem=recv_sems.at[0],
                                         device_id=(x, y, oc), device_id_type=MESH),
            pltpu.make_async_remote_copy(src_ref=s_ref.at[:, :, pl.ds(pl.multiple_of(oc * HW, LANE), HW)],
                                         dst_ref=rs_ref, send_sem=send_sems.at[1], recv_sem=recv_sems.at[1],
                                         device_id=(x, y, oc), device_id_type=MESH)]


def _rs_swap_start(gw, gs, tag):
    def body(w_ref, s_ref, rw_ref, rs_ref, send_sems, recv_sems, w_thru, s_thru, rw_thru, rs_thru, token):
        for cp in _rs_swap_copies(w_ref, s_ref, rw_ref, rs_ref, send_sems, recv_sems):
            cp.start()
        token[...] = jnp.zeros_like(token)

    return pl.pallas_call(
        body, name="grads_swap_start_" + tag,
        out_shape=(pltpu.SemaphoreType.DMA((2,)), pltpu.SemaphoreType.DMA((2,)), pltpu.HBM(gw.shape, gw.dtype),
                   pltpu.HBM(gs.shape, gs.dtype), pltpu.HBM((NINP, D // 2), F32), pltpu.HBM((4, PACK_ROWS, HW), F32),
                   jax.ShapeDtypeStruct((8, LANE), F32)),
        in_specs=(HBM, HBM, HBM, HBM),
        out_specs=(SEM, SEM, HBM, HBM, HBM, HBM, pl.BlockSpec(memory_space=pltpu.VMEM)),
        input_output_aliases={0: 2, 1: 3, 2: 4, 3: 5},
        compiler_params=pltpu.CompilerParams(has_side_effects=EFFECT),
    )(_in_hbm(gw), _in_hbm(gs), _in_hbm(lax.empty((NINP, D // 2), F32)), _in_hbm(lax.empty((4, PACK_ROWS, HW), F32)))


def _rs_swap_wait(send_sems, recv_sems, gw, gs, rw, rs, after, tag):
    def body(w_ref, s_ref, rw_ref, rs_ref, send_sems, recv_sems, after_ref, w_out, s_out, rw_out, rs_out):
        for cp in _rs_swap_copies(w_ref, s_ref, rw_ref, rs_ref, send_sems, recv_sems):
            cp.wait()

    return pl.pallas_call(
        body, name="grads_swap_wait_" + tag,
        out_shape=(pltpu.HBM(gw.shape, gw.dtype), pltpu.HBM(gs.shape, gs.dtype), pltpu.HBM(rw.shape, rw.dtype),
                   pltpu.HBM(rs.shape, rs.dtype)),
        in_specs=(HBM, HBM, HBM, HBM, SEM, SEM, ANY), out_specs=(HBM, HBM, HBM, HBM),
        input_output_aliases={0: 0, 1: 1, 2: 2, 3: 3},
        compiler_params=pltpu.CompilerParams(has_side_effects=EFFECT),
    )(gw, gs, rw, rs, send_sems, recv_sems, after)


SUM_BR = 512


def _rs_chip_sum_w(gw, rw, cidx):
    def body(c_ref, g_ref, r_ref, o_ref):
        s = g_ref[...] + r_ref[...]
        q = D // 8
        o_ref[...] = jnp.concatenate([_pack_words(s[:, 0:q], s[:, q:2 * q]),
                                      _pack_words(s[:, 2 * q:3 * q], s[:, 3 * q:4 * q])], axis=1)

    return pl.pallas_call(
        body,
        grid_spec=pltpu.PrefetchScalarGridSpec(
            num_scalar_prefetch=1, grid=(NINP // SUM_BR,),
            in_specs=[pl.BlockSpec((SUM_BR, D // 2), lambda i, cr: (i, cr[0])),
                      pl.BlockSpec((SUM_BR, D // 2), lambda i, cr: (i, 0))],
            out_specs=pl.BlockSpec((SUM_BR, D // 4), lambda i, cr: (i, 0))),
        out_shape=jax.ShapeDtypeStruct((NINP, D // 4), F32),
        name="grads_chip_sum_w",
        compiler_params=pltpu.CompilerParams(dimension_semantics=("arbitrary",), vmem_limit_bytes=VMEM_LIMIT),
    )(cidx, gw, rw)


def _rs_chip_sum_s(gs, rs, cidx):
    def body(c_ref, g_ref, r_ref, o_ref):
        o_ref[...] = (g_ref[...] + r_ref[...]).astype(BF16)

    return pl.pallas_call(
        body,
        grid_spec=pltpu.PrefetchScalarGridSpec(
            num_scalar_prefetch=1, grid=(4,),
            in_specs=[pl.BlockSpec((None, PACK_ROWS, HW), lambda j, cr: (j, 0, cr[0])),
                      pl.BlockSpec((None, PACK_ROWS, HW), lambda j, cr: (j, 0, 0))],
            out_specs=pl.BlockSpec((None, PACK_ROWS, HW), lambda j, cr: (j, 0, 0))),
        out_shape=jax.ShapeDtypeStruct((4, PACK_ROWS, HW), BF16),
        name="grads_chip_sum_s",
        compiler_params=pltpu.CompilerParams(dimension_semantics=("arbitrary",), vmem_limit_bytes=VMEM_LIMIT),
    )(cidx, gs, rs)


def _rs_exchange_copies(sw_ref, ss_ref, r2w_ref, r2s_ref, send_sems, recv_sems):
    x, y, c = _me()
    mine, theirs = [], []
    for j, (cx, cy) in enumerate([(1 - x, y), (x, 1 - y), (1 - x, 1 - y)]):
        def mk(i, src, dst):
            return pltpu.make_async_remote_copy(src_ref=src, dst_ref=dst, send_sem=send_sems.at[3 * j + i],
                                                recv_sem=recv_sems.at[3 * j + i], device_id=(cx, cy, c), device_id_type=MESH)
        for i, (l0, p0, n) in enumerate(_piece_rows(2 * cx + cy)):
            mine.append(mk(i, sw_ref.at[pl.ds(p0, n)], r2w_ref.at[j, pl.ds(l0, n)]))
            theirs.append(mk(i, r2w_ref.at[j, pl.ds(l0, n)], r2w_ref.at[j, pl.ds(l0, n)]))
        mine.append(mk(2, ss_ref.at[2 * cx + cy], r2s_ref.at[j]))
        theirs.append(mk(2, r2s_ref.at[j], r2s_ref.at[j]))
    return mine, theirs


def _rs_exchange_start(sw, ss, tag):
    def body(sw_ref, ss_ref, r2w_ref, r2s_ref, send_sems, recv_sems, sw_thru, ss_thru, r2w_thru, r2s_thru, token):
        mine, _ = _rs_exchange_copies(sw_ref, ss_ref, r2w_ref, r2s_ref, send_sems, recv_sems)
        for cp in mine:
            cp.start()
        token[...] = jnp.zeros_like(token)

    return pl.pallas_call(
        body, name="grads_exchange_start_" + tag,
        out_shape=(pltpu.SemaphoreType.DMA((9,)), pltpu.SemaphoreType.DMA((9,)), pltpu.HBM(sw.shape, sw.dtype),
                   pltpu.HBM(ss.shape, ss.dtype), pltpu.HBM((3, WSH, D // 4), F32), pltpu.HBM((3, PACK_ROWS, HW), BF16),
                   jax.ShapeDtypeStruct((8, LANE), F32)),
        in_specs=(HBM, HBM, HBM, HBM),
        out_specs=(SEM, SEM, HBM, HBM, HBM, HBM, pl.BlockSpec(memory_space=pltpu.VMEM)),
        input_output_aliases={0: 2, 1: 3, 2: 4, 3: 5},
        compiler_params=pltpu.CompilerParams(has_side_effects=EFFECT),
    )(_in_hbm(sw), _in_hbm(ss), _in_hbm(lax.empty((3, WSH, D // 4), F32)), _in_hbm(lax.empty((3, PACK_ROWS, HW), BF16)))


def _rs_exchange_wait(send_sems, recv_sems, sw, ss, r2w, r2s, after, tag):
    def body(sw_ref, ss_ref, r2w_ref, r2s_ref, send_sems, recv_sems, after_ref, sw_dead, ss_dead, r2w_out, r2s_out):
        mine, theirs = _rs_exchange_copies(sw_ref, ss_ref, r2w_ref, r2s_ref, send_sems, recv_sems)
        for cp in mine:
            cp.wait_send()
        for cp in theirs:
            cp.wait_recv()

    out = pl.pallas_call(
        body, name="grads_exchange_wait_" + tag,
        out_shape=(pltpu.HBM(sw.shape, sw.dtype), pltpu.HBM(ss.shape, ss.dtype), pltpu.HBM(r2w.shape, r2w.dtype),
                   pltpu.HBM(r2s.shape, r2s.dtype)),
        in_specs=(HBM, HBM, HBM, HBM, SEM, SEM, ANY), out_specs=(HBM, HBM, HBM, HBM),
        input_output_aliases={0: 0, 1: 1, 2: 2, 3: 3},
        compiler_params=pltpu.CompilerParams(has_side_effects=EFFECT),
    )(sw, ss, r2w, r2s, send_sems, recv_sems, after)
    return out[2], out[3]


def _rs_final_w(gw, rw, r2w, idx, both, layer):
    q = D // 8

    def body(i_ref, g_ref, r_ref, p_ref, both_ref, o_ref, gbuf, rbuf, sems):
        i = pl.program_id(0)
        k, c = i_ref[0], i_ref[1]
        cps = []
        for n_, (l0, p0, n) in enumerate(_piece_rows(k)):
            gcol = pl.ds(pl.multiple_of(c * (D // 2) + i * 2 * q, LANE), 2 * q)
            rcol = pl.ds(pl.multiple_of(i * 2 * q, LANE), 2 * q)
            cps.append(pltpu.make_async_copy(g_ref.at[pl.ds(p0, n), gcol], gbuf.at[pl.ds(l0, n)], sems.at[2 * n_]))
            cps.append(pltpu.make_async_copy(r_ref.at[pl.ds(p0, n), rcol], rbuf.at[pl.ds(l0, n)], sems.at[2 * n_ + 1]))
        for cp in cps:
            cp.start()
        for cp in cps:
            cp.wait()
        acc = gbuf[...] + rbuf[...]
        for j in range(3):
            lo, hi = _unpack_words(p_ref[j])
            acc = acc + jnp.concatenate([lo, hi], axis=1)
        o_ref[...] = acc

    return pl.pallas_call(
        body,
        grid_spec=pltpu.PrefetchScalarGridSpec(
            num_scalar_prefetch=1, grid=(2,),
            in_specs=[ANY, ANY, pl.BlockSpec((3, WSH, q), lambda i, ir: (0, 0, i)), ANY],
            out_specs=pl.BlockSpec((None, WSH, 2 * q), lambda i, ir: (layer, 0, 2 * ir[1] + i)),
            scratch_shapes=[pltpu.VMEM((WSH, 2 * q), F32), pltpu.VMEM((WSH, 2 * q), F32), pltpu.SemaphoreType.DMA((4,))]),
        out_shape=jax.ShapeDtypeStruct((NL, WSH, D), F32),
        input_output_aliases={4: 0},
        name="grads_final_sum_w",
        compiler_params=pltpu.CompilerParams(dimension_semantics=("arbitrary",), vmem_limit_bytes=VMEM_LIMIT),
    )(idx, gw, rw, r2w, both)


def _rs_final_s(gs, rs, r2s, idx):
    def body(i_ref, g_ref, r_ref, p_ref, o_ref):
        acc = g_ref[...] + r_ref[...]
        for j in range(3):
            acc = acc + p_ref[j].astype(F32)
        o_ref[...] = acc

    return pl.pallas_call(
        body,
        grid_spec=pltpu.PrefetchScalarGridSpec(
            num_scalar_prefetch=1, grid=(1,),
            in_specs=[pl.BlockSpec((None, PACK_ROWS, HW), lambda i, ir: (ir[0], 0, ir[1])),
                      pl.BlockSpec((None, PACK_ROWS, HW), lambda i, ir: (ir[0], 0, 0)),
                      pl.BlockSpec((3, PACK_ROWS, HW), lambda i, ir: (0, 0, 0))],
            out_specs=pl.BlockSpec((PACK_ROWS, HW), lambda i, ir: (0, ir[1]))),
        out_shape=jax.ShapeDtypeStruct((PACK_ROWS, PACK_W), F32),
        name="grads_final_sum_s",
        compiler_params=pltpu.CompilerParams(dimension_semantics=("arbitrary",), vmem_limit_bytes=VMEM_LIMIT),
    )(idx, gs, rs, r2s)


def _rs_share(fw, fs, layer):
    def body(w_ref, s_ref, ow_ref, os_ref, send_sems, recv_sems):
        x, y, c = _me()
        wcol = lambda cc: pl.ds(pl.multiple_of(cc * (D // 2), LANE), D // 2)
        scol = lambda cc: pl.ds(pl.multiple_of(cc * HW, LANE), HW)

        def copies(cc):
            return [pltpu.make_async_remote_copy(src_ref=w_ref.at[layer, :, wcol(cc)],
                                                 dst_ref=ow_ref.at[layer, :, wcol(cc)],
                                                 send_sem=send_sems.at[0], recv_sem=recv_sems.at[0],
                                                 device_id=(x, y, 1 - c), device_id_type=MESH),
                    pltpu.make_async_remote_copy(src_ref=s_ref.at[:, scol(cc)], dst_ref=os_ref.at[:, scol(cc)],
                                                 send_sem=send_sems.at[1], recv_sem=recv_sems.at[1],
                                                 device_id=(x, y, 1 - c), device_id_type=MESH)]
        out = copies(c)
        for cp in out:
            cp.start()
        for cp in copies(1 - c):
            cp.wait_recv()
        for cp in out:
            cp.wait_send()

    return pl.pallas_call(
        body,
        out_shape=[jax.ShapeDtypeStruct(fw.shape, F32), jax.ShapeDtypeStruct(fs.shape, F32)],
        in_specs=[ANY, ANY], out_specs=[ANY, ANY],
        input_output_aliases={0: 0, 1: 1},
        scratch_shapes=[pltpu.SemaphoreType.DMA((2,)), pltpu.SemaphoreType.DMA((2,))],
        name="grads_share",
    )(fw, fs)


def _rs_sums(gw, gs, rw, rs):
    x, y, c = _me()
    cidx = jnp.reshape(c, (1,)).astype(jnp.int32)
    return dict(gw=gw, gs=gs, rw=rw, rs=rs, sw=_rs_chip_sum_w(gw, rw, cidx), ss=_rs_chip_sum_s(gs, rs, cidx))


def _rs_end(st, r2w, r2s, both, layer):
    x, y, c = _me()
    idx = jnp.stack([2 * x + y, c]).astype(jnp.int32)
    return _rs_share(_rs_final_w(st["gw"], st["rw"], r2w, idx, both, layer),
                     _rs_final_s(st["gs"], st["rs"], r2s, idx), layer)


def _all_reduce_small(gs):
    rows = gs.shape[0]

    def body(g_ref, o_ref, buf, send_sems, recv_sems):
        x, y, c = _me()
        me = 4 * x + 2 * y + c
        buf[me] = g_ref[...]
        cps = []
        for r in range(1, 8):
            fx, fy, fc = (r >> 2) & 1, (r >> 1) & 1, r & 1
            px, py, pc = jnp.bitwise_xor(x, fx), jnp.bitwise_xor(y, fy), jnp.bitwise_xor(c, fc)
            cps.append((pltpu.make_async_remote_copy(
                src_ref=g_ref, dst_ref=buf.at[me], send_sem=send_sems.at[r - 1], recv_sem=recv_sems.at[r - 1],
                device_id=(px, py, pc), device_id_type=MESH), 4 * px + 2 * py + pc))
        for cp, _ in cps:
            cp.start()
        for r, (cp, peer) in enumerate(cps):
            pltpu.make_async_remote_copy(
                src_ref=g_ref, dst_ref=buf.at[peer], send_sem=send_sems.at[r], recv_sem=recv_sems.at[r],
                device_id=(x, y, c), device_id_type=MESH).wait_recv()
        for cp, _ in cps:
            cp.wait_send()
        acc = buf[0]
        for k in range(1, 8):
            acc = acc + buf[k]
        o_ref[...] = acc

    return pl.pallas_call(
        body,
        out_shape=jax.ShapeDtypeStruct((rows, LANE), F32),
        in_specs=[pl.BlockSpec(memory_space=pltpu.VMEM)],
        out_specs=pl.BlockSpec(memory_space=pltpu.VMEM),
        scratch_shapes=[pltpu.VMEM((8, rows, LANE), F32), pltpu.SemaphoreType.DMA((7,)), pltpu.SemaphoreType.DMA((7,))],
        name="small_grads_all_reduce",
    )(gs)


PACK_SPLIT = (("w_uq", 96, (QL, 192)), ("w_ukv", 64, (KVL, 256)),
              ("w_out_a", 256, (CW, 256)), ("w_out_b", 256, (CW, 256)), ("w_out_c", 256, (CW, 256)),
              ("w_o", 512, (256, D)))
MAT_ROWS = 1440
CONV_SHARD = 3 * 128


def _w_in_words(w_in_shard):
    t = w_in_shard.T
    return _pack_words(t[:, :CWD], t[:, CWD:])


def _pack_weights(wl):
    parts = [wl[n].astype(BF16).reshape(-1, PACK_W) for n, _, _ in PACK_SPLIT]
    cw = wl["conv_w"].reshape(-1)
    hi = cw.astype(BF16)
    r1 = cw - hi.astype(F32)
    mid = r1.astype(BF16)
    lo = (r1 - mid.astype(F32)).astype(BF16)
    cterms = jnp.pad(jnp.concatenate([hi, mid, lo]), (0, 3 * PACK_W - 3 * CONV_SHARD)).reshape(3, PACK_W)
    tail = jnp.pad(cterms, ((0, PACK_ROWS - MAT_ROWS - 3), (0, 0)))
    return jnp.concatenate(parts + [tail], axis=0)


def _unpack_weights(gath):
    out = {}
    r = 0
    for n, nrows, shp in PACK_SPLIT:
        t = gath[:, r:r + nrows].reshape((4,) + shp)
        r += nrows
        if n == "w_o":
            out[n] = t.reshape(4 * shp[0], shp[1])
        else:
            out[n] = t.transpose(1, 0, 2).reshape(shp[0], 4 * shp[1])
    ct = gath[:, r:r + 3].reshape(4, 3 * PACK_W)[:, :3 * CONV_SHARD].astype(F32).reshape(4, 3, CONV_SHARD)
    cw = (ct[:, 0] + ct[:, 1]) + ct[:, 2]
    out["conv_w"] = cw.reshape(4, 3, 128).transpose(1, 0, 2).reshape(3, CW)
    return out


def _pack_grads(g):
    parts = []
    for n, nrows, shp in PACK_SPLIT:
        t = g[n]
        if n == "w_o":
            t = t.reshape((4,) + shp)
        else:
            t = t.reshape(shp[0], 4, shp[1]).transpose(1, 0, 2)
        parts.append(t.reshape(4, nrows, PACK_W))
    cw = g["conv_w"].reshape(3, 4, 128).transpose(1, 0, 2).reshape(4, 1, CONV_SHARD)
    parts.append(jnp.pad(cw, ((0, 0), (0, PACK_ROWS - MAT_ROWS - 1), (0, PACK_W - CONV_SHARD))))
    return jnp.concatenate(parts, axis=1)


def _unpack_grads(red):
    out = {}
    r = 0
    for n, nrows, shp in PACK_SPLIT:
        out[n] = red[r:r + nrows].reshape(shp)
        r += nrows
    out["conv_w"] = red[r, :CONV_SHARD].reshape(3, 128)
    return out


SMALL_SIZES = (("norm_g", D), ("b_gate", 3 * D), ("conv_b", CW), ("q_a_norm_g", QL), ("kv_a_norm_g", KVL),
               ("mla_q_norm_g", QK), ("mla_k_norm_g", QK), ("dil_q_norm_g", NG * HD), ("dil_k_norm_g", NG * HD))
SMALL_ROWS = 88


def _pack_small(per_name):
    flat = jnp.concatenate([per_name[n].reshape(-1).astype(F32) for n, _ in SMALL_SIZES])
    return jnp.pad(flat, (0, SMALL_ROWS * LANE - flat.shape[0])).reshape(SMALL_ROWS, LANE)


def _unpack_small(packed, like):
    out = {}
    flat = packed.reshape(-1)
    r = 0
    for n, sz in SMALL_SIZES:
        out[n] = flat[r:r + NL * sz].reshape(like[n].shape)
        r += NL * sz
    return out


def _adamw_math(w, g, m, v):
    m = ADAM_B1 * m + (1.0 - ADAM_B1) * g
    v = ADAM_B2 * v + (1.0 - ADAM_B2) * jnp.square(g)
    m_hat = m / (1.0 - ADAM_B1 ** ADAM_STEP)
    v_hat = v / (1.0 - ADAM_B2 ** ADAM_STEP)
    delta = -ADAM_LR * (m_hat / (jnp.sqrt(v_hat) + ADAM_EPS) + ADAM_WD * w)
    return delta, m, v


def _adamw(name, w, g, m, v, br, bc=None):
    L, R, C = w.shape
    bc = C if bc is None else bc
    blk = lambda l, i, j: (l, i, j)
    return _pcall(name, _adamw_math, (L, R // br, C // bc), [(t, (None, br, bc), blk) for t in (w, g, m, v)],
                  [((L, R, C), F32, (None, br, bc), blk)] * 3)


ADAM_ROWS = {"w_uq": 256, "w_ukv": 128, "w_out_a": 512, "w_out_b": 512, "w_out_c": 512, "w_o": 256,
             "conv_w": 3}


def kernel(x, norm_g, w_in, b_gate, conv_w, conv_b, q_a_norm_g, w_uq, kv_a_norm_g, w_ukv, mla_q_norm_g, mla_k_norm_g, dil_q_norm_g, dil_k_norm_g, w_out_a, w_out_b, w_out_c, w_o, loss_target, m_norm_g, m_w_in, m_b_gate, m_conv_w, m_conv_b, m_q_a_norm_g, m_w_uq, m_kv_a_norm_g, m_w_ukv, m_mla_q_norm_g, m_mla_k_norm_g, m_dil_q_norm_g, m_dil_k_norm_g, m_w_out_a, m_w_out_b, m_w_out_c, m_w_o, v_norm_g, v_w_in, v_b_gate, v_conv_w, v_conv_b, v_q_a_norm_g, v_w_uq, v_kv_a_norm_g, v_w_ukv, v_mla_q_norm_g, v_mla_k_norm_g, v_dil_q_norm_g, v_dil_k_norm_g, v_w_out_a, v_w_out_b, v_w_out_c, v_w_o):
    W = dict(norm_g=norm_g, w_in=w_in, b_gate=b_gate, conv_w=conv_w, conv_b=conv_b, q_a_norm_g=q_a_norm_g, w_uq=w_uq,
             kv_a_norm_g=kv_a_norm_g, w_ukv=w_ukv, mla_q_norm_g=mla_q_norm_g, mla_k_norm_g=mla_k_norm_g,
             dil_q_norm_g=dil_q_norm_g, dil_k_norm_g=dil_k_norm_g, w_out_a=w_out_a, w_out_b=w_out_b, w_out_c=w_out_c,
             w_o=w_o)
    M = dict(norm_g=m_norm_g, w_in=m_w_in, b_gate=m_b_gate, conv_w=m_conv_w, conv_b=m_conv_b, q_a_norm_g=m_q_a_norm_g,
             w_uq=m_w_uq, kv_a_norm_g=m_kv_a_norm_g, w_ukv=m_w_ukv, mla_q_norm_g=m_mla_q_norm_g,
             mla_k_norm_g=m_mla_k_norm_g, dil_q_norm_g=m_dil_q_norm_g, dil_k_norm_g=m_dil_k_norm_g, w_out_a=m_w_out_a,
             w_out_b=m_w_out_b, w_out_c=m_w_out_c, w_o=m_w_o)
    V = dict(norm_g=v_norm_g, w_in=v_w_in, b_gate=v_b_gate, conv_w=v_conv_w, conv_b=v_conv_b, q_a_norm_g=v_q_a_norm_g,
             w_uq=v_w_uq, kv_a_norm_g=v_kv_a_norm_g, w_ukv=v_w_ukv, mla_q_norm_g=v_mla_q_norm_g,
             mla_k_norm_g=v_mla_k_norm_g, dil_q_norm_g=v_dil_q_norm_g, dil_k_norm_g=v_dil_k_norm_g, w_out_a=v_w_out_a,
             w_out_b=v_w_out_b, w_out_c=v_w_out_c, w_o=v_w_o)
    batch = x.shape[0]
    T = batch * S

    def layer_weights(l, cont, gath):
        full = _unpack_weights(gath)
        pad_qk = lambda t: jnp.pad(t, (0, QKP - QK)).reshape(1, QKP)
        full.update(
            w_in_t=cont,
            norm_g=norm_g[l].reshape(1, D), b_gate=b_gate[l].reshape(1, 3 * D), conv_b=conv_b[l].reshape(1, CW),
            q_a_norm_g=q_a_norm_g[l].reshape(1, QL), kv_a_norm_g=kv_a_norm_g[l].reshape(1, KVL),
            mla_q_norm_g=pad_qk(mla_q_norm_g[l]), mla_k_norm_g=pad_qk(mla_k_norm_g[l]),
            dil_q_norm_g=dil_q_norm_g[l].reshape(NG, 1, HD), dil_k_norm_g=dil_k_norm_g[l].reshape(NG, 1, HD))
        return full

    words = [_w_in_words(w_in[l]) for l in range(NL)]
    packs = [_pack_weights({n: W[n][l] for n in BIG[1:] + ("conv_w",)}) for l in range(NL)]
    tabs = _rope_tables() + (_dil_slopes(),)
    x2 = x.reshape(T, D)

    cont0 = _all_gather(words[0])
    ag0 = _ag_behind_start(None, packs[0], cont0, "0")
    ag1 = []

    def rest_of_layer0(proj):
        _, gath0 = _ag_behind_wait(*ag0[:6], proj, "0")
        ag1.extend(_ag_behind_start(words[1], packs[1], gath0, "1"))
        w = layer_weights(0, cont0, gath0)
        w["conv_b"] = w["conv_b"] + ag1[6][0:1, 0:1]
        all0.append(w)
        return w

    first0 = dict(w_in_t=cont0, norm_g=norm_g[0].reshape(1, D) + ag0[6][0:1, 0:1])
    all0 = []
    y0, res0 = _layer_fwd(x2, first0, tabs, batch, rest=rest_of_layer0)
    w0 = all0[0]
    w1 = layer_weights(1, *_ag_behind_wait(*ag1[:6], y0, "1"))
    y1, res1 = _layer_fwd(y0, w1, tabs, batch)

    row = lambda i: (i, 0)
    dy, loss = _pcall("loss", _loss_math, (T // BR,),
                      [(y1, (BR, D), row), (loss_target.reshape(T, D), (BR, D), row)],
                      [((T, D), F32, (BR, D), row), ((1, 1), F32, (1, 1), lambda i: (0, 0), True)])
    loss = lax.psum(loss[0, 0], ("x", "y", "c"))

    grads = [None] * NL
    dy, grads[1] = _layer_bwd(dy, w1, res1, tabs, batch)
    st = [None] * NL
    ex = [None] * NL
    sw1 = _rs_swap_start(grads[1]["w_in_t"], _pack_grads(grads[1]), "1")
    w0["w_o"] = w0["w_o"] + sw1[6][0:1, 0:1].astype(BF16)

    def exchange_layer1(t):
        st[1] = _rs_sums(*_rs_swap_wait(*sw1[:6], t, "1"))
        ex[1] = _rs_exchange_start(st[1]["sw"], st[1]["ss"], "1")
        return ex[1][6]

    red = [None] * NL
    g_in_t = [lax.empty((NL, WSH, D), F32)]

    def finish(l, after):
        r2w, r2s = _rs_exchange_wait(*ex[l][:6], after, str(l))
        g_in_t[0], rs = _rs_end(st[l], r2w, r2s, g_in_t[0], l)
        red[l] = _unpack_grads(rs)
        return rs

    def start_layer0(g):
        sw0 = _rs_swap_start(g["w_in_t"], _pack_grads(g), "0")
        done1 = finish(1, sw0[6])
        st[0] = _rs_sums(*_rs_swap_wait(*sw0[:6], done1, "0"))
        ex[0] = _rs_exchange_start(st[0]["sw"], st[0]["ss"], "0")
        return ex[0][6]

    dx, grads[0] = _layer_bwd(dy, w0, res0, tabs, batch, after_dw=start_layer0, after_merge=exchange_layer1)
    grad_x = dx.reshape(batch, S, D)
    finish(0, dx)

    G = {n: jnp.stack([red[l][n] for l in range(NL)]) for n in BIG[1:] + ("conv_w",)}
    g_in_t = g_in_t[0]
    G["w_in"] = jnp.swapaxes(g_in_t, 1, 2)
    small_g = {n: jnp.stack([grads[l][n].reshape(-1)[:sz] for l in range(NL)]) for n, sz in SMALL_SIZES}
    small_red = _all_reduce_small(_pack_small(small_g))
    G.update(_unpack_small(small_red, {n: W[n] for n in SMALL}))

    delta, new_m, new_v = {}, {}, {}
    for n in BIG[1:] + ("conv_w",):
        delta[n], new_m[n], new_v[n] = _adamw("adamw_" + n, W[n], G[n], M[n], V[n], ADAM_ROWS[n])
    tr = lambda t: jnp.swapaxes(t, 1, 2)
    delta["w_in"], new_m["w_in"], new_v["w_in"] = (
        tr(t) for t in _adamw("adamw_w_in", tr(w_in), g_in_t, tr(m_w_in), tr(v_w_in), WSH, LANE))
    sw, sm, sv = (_pack_small({n: t[n] for n in SMALL})[None] for t in (W, M, V))
    sd, snm, snv = _adamw("adamw_small", sw, small_red[None], sm, sv, SMALL_ROWS)
    like = {n: W[n] for n in SMALL}
    delta.update(_unpack_small(sd[0], like))
    new_m.update(_unpack_small(snm[0], like))
    new_v.update(_unpack_small(snv[0], like))

    return (loss, grad_x, *[G[n] for n in WEIGHTS], *[delta[n] for n in WEIGHTS],
            *[new_m[n] for n in WEIGHTS], *[new_v[n] for n in WEIGHTS])
```

```python
import functools

import numpy as np
import jax
import jax.numpy as jnp
from jax import lax
from jax.experimental import pallas as pl
from jax.experimental.pallas import tpu as pltpu

F32 = jnp.float32
BF16 = jnp.bfloat16

D = 1024
S = 2048
NL = 2
CW = 512
NH = 8
QL = 256
KVL = 128
NOPE = 64
ROPE = 32
VD = 64
QK = NOPE + ROPE
QKP = 128
ROPE_THETA = 10000.0
DIL = ((128, 1), (512, 4), (2048, 16))
NG = 3
DH = 8
HD = 64
DWID = DH * HD
QB = 128
EPS = 1e-6
NIN = 11168
NINP = 11264
O_A, O_CQ, O_CKV, O_KPE, O_BZ, O_DQ, O_DK, O_DV, O_CZ, O_G = 0, 2048, 2304, 2432, 2560, 3072, 4608, 6144, 7680, 8192
KPE_END = 2464
NEG = -1e30
MLA_SCALE = QK ** -0.5
DIL_SCALE = HD ** -0.5
LANE = 128
PACK_W = 512
VMEM_LIMIT = 48 * 1024 * 1024

ADAM_LR = 0.001
ADAM_B1 = 0.9
ADAM_B2 = 0.999
ADAM_EPS = 1e-08
ADAM_WD = 0.01
ADAM_STEP = 10

MESH = pl.DeviceIdType.MESH
BIG = ("w_in", "w_uq", "w_ukv", "w_out_a", "w_out_b", "w_out_c", "w_o")
SMALL = ("norm_g", "b_gate", "conv_b", "q_a_norm_g", "kv_a_norm_g", "mla_q_norm_g", "mla_k_norm_g",
         "dil_q_norm_g", "dil_k_norm_g")
WEIGHTS = ("norm_g", "w_in", "b_gate", "conv_w", "conv_b", "q_a_norm_g", "w_uq", "kv_a_norm_g", "w_ukv",
           "mla_q_norm_g", "mla_k_norm_g", "dil_q_norm_g", "dil_k_norm_g", "w_out_a", "w_out_b", "w_out_c", "w_o")


def _dot(a, b):
    return jnp.dot(a, b, preferred_element_type=F32)


def _dot_nt(a, b):
    return lax.dot_general(a, b, (((1,), (1,)), ((), ())), preferred_element_type=F32)


def _dot_tn(a, b):
    return lax.dot_general(a, b, (((0,), (0,)), ((), ())), preferred_element_type=F32)


def _grid_step(grid):
    step = pl.program_id(0)
    for a in range(1, len(grid)):
        step = step * grid[a] + pl.program_id(a)
    n = 1
    for g in grid:
        n *= g
    return step, n


def _write_windows(buf_ref, stages, sems, step, nsteps, puts):
    slot = step % 2
    for t, (v, dst) in enumerate(puts):
        cp = pltpu.make_async_copy(stages[t].at[slot], dst, sems.at[t, slot])

        @pl.when(step >= 2)
        def _():
            cp.wait()

        stages[t][slot] = v.astype(stages[t].dtype).reshape(stages[t].shape[1:])
        cp.start()

    @pl.when(step == nsteps - 1)
    def _():
        for t, (v, dst) in enumerate(puts):
            pltpu.make_async_copy(stages[t].at[slot], dst, sems.at[t, slot]).wait()
            if nsteps > 1:
                pltpu.make_async_copy(stages[t].at[1 - slot], dst, sems.at[t, 1 - slot]).wait()


def _pcall(name, fn, grid, ins, outs, into=None):
    n_in = len(ins)
    n_out = len(outs)
    acc_axis = len(grid) - 1
    is_acc = [len(o) > 4 and o[4] for o in outs]
    outs = [o[:4] for o in outs]
    targets = into[1] if into is not None else []
    n_t = len(targets)

    def body(*refs):
        vals = fn(*[r[...].astype(F32) for r in refs[:n_in]])
        if not isinstance(vals, (tuple, list)):
            vals = (vals,)
        o0 = n_in + (1 if n_t else 0)
        for k in range(n_out):
            r = refs[o0 + k]
            v = vals[k].astype(r.dtype).reshape(r.shape)
            if is_acc[k]:
                first = pl.program_id(acc_axis) == 0

                @pl.when(first)
                def _():
                    r[...] = v

                @pl.when(jnp.logical_not(first))
                def _():
                    r[...] += v
            else:
                r[...] = v
        if n_t:
            buf_ref = refs[o0 + n_out]
            stages = refs[o0 + n_out + 1:o0 + n_out + 1 + n_t]
            ids = [pl.program_id(a) for a in range(len(grid))]
            step, nsteps = _grid_step(grid)
            _write_windows(buf_ref, stages, refs[-1], step, nsteps,
                           [(vals[n_out + t], targets[t][1](buf_ref, *ids)) for t in range(n_t)])

    in_specs = [pl.BlockSpec(bs, im) for _, bs, im in ins]
    out_specs = [pl.BlockSpec(bs, im) for _, _, bs, im in outs]
    out_shape = [jax.ShapeDtypeStruct(sh, dt) for sh, dt, _, _ in outs]
    args = [a for a, _, _ in ins]
    extra = {}
    if n_t:
        buf = into[0]
        in_specs.append(pl.BlockSpec(memory_space=pl.ANY))
        out_specs.append(pl.BlockSpec(memory_space=pl.ANY))
        out_shape.append(jax.ShapeDtypeStruct(buf.shape, buf.dtype))
        args.append(buf)
        extra = dict(input_output_aliases={n_in: n_out},
                     scratch_shapes=[pltpu.VMEM((2,) + tuple(bs), buf.dtype) for bs, _ in targets]
                     + [pltpu.SemaphoreType.DMA((n_t, 2))])
    return pl.pallas_call(
        body,
        grid=grid,
        in_specs=in_specs,
        out_specs=out_specs,
        out_shape=out_shape,
        name=name,
        compiler_params=pltpu.CompilerParams(
            dimension_semantics=("arbitrary",) * len(grid), vmem_limit_bytes=VMEM_LIMIT),
        **extra,
    )(*args)


def _mm(name, a, b, *, ta=False, tb=False, out_dtype=F32, add=None, dep=None, b_words=False, tm=2048, tn=1024, tk=1024):
    if ta:
        K, M = a.shape
    else:
        M, K = a.shape
    bshape = (b.shape[0], 2 * b.shape[1]) if b_words else b.shape
    if tb:
        N, K2 = bshape
    else:
        K2, N = bshape
    assert K == K2, (name, a.shape, b.shape)
    tm, tn, tk = min(tm, M), min(tn, N), min(tk, K)
    assert M % tm == 0 and N % tn == 0 and K % tk == 0, (name, M, N, K)
    nk = K // tk
    dims = (((0 if ta else 1,), (1 if tb else 0,)), ((), ()))
    a_spec = pl.BlockSpec((tk, tm), lambda j, i, k: (k, i)) if ta else pl.BlockSpec((tm, tk), lambda j, i, k: (i, k))
    bw = 2 if b_words else 1
    assert not b_words or (tk if tb else tn) == bshape[1]
    b_spec = (pl.BlockSpec((tn, tk // bw), lambda j, i, k: (j, k)) if tb
              else pl.BlockSpec((tk, tn // bw), lambda j, i, k: (k, j)))
    o_spec = pl.BlockSpec((tm, tn), lambda j, i, k: (i, j))
    has_add = add is not None
    n_in = 2 + has_add + (dep is not None)

    def body(*refs):
        a_ref, b_ref = refs[0], refs[1]
        add_ref = refs[2] if has_add else None
        o_ref = refs[n_in]
        bb = b_ref[...]
        if b_words:
            lo, hi = _unpack_words(bb)
            first = (pl.program_id(0) * tn) if tb else (pl.program_id(2) * tk)
            r = first + lax.broadcasted_iota(jnp.int32, lo.shape, 0)
            pad = jnp.logical_and(r >= KPE_END, r < KPE_END + NINP - NIN)
            bb = jnp.concatenate([jnp.where(pad, 0.0, lo), jnp.where(pad, 0.0, hi)], axis=1)
        p = lax.dot_general(a_ref[...].astype(BF16), bb.astype(BF16), dims, preferred_element_type=F32)
        if nk == 1:
            if has_add:
                p = p + add_ref[...]
            o_ref[...] = p.astype(out_dtype)
        else:
            acc = refs[-1]
            k = pl.program_id(2)

            @pl.when(k == 0)
            def _():
                acc[...] = p

            @pl.when(k > 0)
            def _():
                acc[...] += p

            @pl.when(k == nk - 1)
            def _():
                r = acc[...]
                if has_add:
                    r = r + add_ref[...]
                o_ref[...] = r.astype(out_dtype)

    in_specs = [a_spec, b_spec] + ([o_spec] if has_add else []) + ([pl.BlockSpec(memory_space=pl.ANY)] if dep is not None else [])
    args = [a, b] + ([add] if has_add else []) + ([dep] if dep is not None else [])
    return pl.pallas_call(
        body,
        grid=(N // tn, M // tm, nk),
        in_specs=in_specs,
        out_specs=o_spec,
        out_shape=jax.ShapeDtypeStruct((M, N), out_dtype),
        scratch_shapes=[pltpu.VMEM((tm, tn), F32)] if nk > 1 else [],
        name=name,
        compiler_params=pltpu.CompilerParams(
            dimension_semantics=("arbitrary", "arbitrary", "arbitrary"), vmem_limit_bytes=VMEM_LIMIT),
    )(*args)


def _vjp_of(f, n_diff):
    def g(*args, n_prim):
        prim = args[:n_diff]
        consts = args[n_diff:n_prim]
        cts = args[n_prim:]
        _, pull = jax.vjp(lambda *p: f(*p, *consts), *prim)
        out = jax.eval_shape(lambda *p: f(*p, *consts), *prim)
        if isinstance(out, (tuple, list)):
            cts = tuple(c.astype(o.dtype) for c, o in zip(cts, out))
        else:
            cts = cts[0].astype(out.dtype)
        return pull(cts)
    return g


def _rms(x, g, n=None):
    n = x.shape[-1] if n is None else n
    ms = jnp.sum(x * x, axis=-1, keepdims=True) / n
    return x * lax.rsqrt(ms + EPS) * g


def _silu(z):
    return z * jax.nn.sigmoid(z)


def _roll_rows(u, k):
    n = u.shape[0]
    r = pltpu.roll(u, k % n, 0)
    t = lax.broadcasted_iota(jnp.int32, u.shape, 0)
    if k > 0:
        return jnp.where(t >= k, r, 0.0)
    return jnp.where(t < n + k, r, 0.0)


@functools.partial(jax.custom_vjp, nondiff_argnums=(1,))
def _shift(u, k):
    return _roll_rows(u, k)


def _shift_fwd(u, k):
    return _roll_rows(u, k), None


def _shift_bwd(k, _, g):
    return (_roll_rows(g, -k),)


_shift.defvjp(_shift_fwd, _shift_bwd)


@functools.partial(jax.custom_vjp, nondiff_argnums=(1,))
def _lane_roll(u, k):
    return pltpu.roll(u, k % LANE, 1)


def _lane_roll_fwd(u, k):
    return pltpu.roll(u, k % LANE, 1), None


def _lane_roll_bwd(k, _, g):
    return (pltpu.roll(g, (-k) % LANE, 1),)


_lane_roll.defvjp(_lane_roll_fwd, _lane_roll_bwd)


def _conv_math(ab, ac, ax, az, cw, cb):
    u = ac * ax
    conv = cb + _shift(u, 2) * cw[0:1] + _shift(u, 1) * cw[1:2] + u * cw[2:3]
    return ab * conv * _silu(az)


def _mla_pre_math(cq, ckv, gq, gkv):
    return _rms(cq, gq), _rms(ckv, gkv)


def _rope_math(q, kn, kpe, gq, gk, c, s1, s2):
    lane = lax.broadcasted_iota(jnp.int32, kpe.shape, 1)
    pe = _lane_roll(jnp.where(lane < ROPE, kpe, 0.0), NOPE)

    def one(t, g):
        tn = _rms(t, g, QK)
        return tn * c + _lane_roll(tn, -16) * s1 + _lane_roll(tn, 16) * s2

    qs, ks = [], []
    for h in range(NH):
        sl = slice(h * QKP, (h + 1) * QKP)
        qs.append(one(q[:, sl], gq))
        ks.append(one(kn[:, sl] + pe, gk))
    return jnp.concatenate(qs, axis=1), jnp.concatenate(ks, axis=1)


def _gate_math(o, z):
    return o * _silu(z)


def _mergec_math(o0, o1, o2, l0, l1, l2, cz):
    m = lax.stop_gradient(jnp.maximum(jnp.maximum(l0, l1), l2))
    e0, e1, e2 = jnp.exp(l0 - m), jnp.exp(l1 - m), jnp.exp(l2 - m)
    den = e0 + e1 + e2
    oc = (e0 / den) * o0 + (e1 / den) * o1 + (e2 / den) * o2
    return oc * _silu(cz)


def _merge_math(g0, g1, g2, b0, b1, b2, pa, pb, pc):
    return (jax.nn.sigmoid(g0 + b0) * pa + jax.nn.sigmoid(g1 + b1) * pb) + jax.nn.sigmoid(g2 + b2) * pc


MLA_T = 256
MLA_UNROLL = True


def _mla_fwd(q, k, v):
    B = q.shape[0]
    T = MLA_T
    NB = S // T

    def body(q_ref, k_ref, v_ref, o_ref, l_ref):
        row = lax.broadcasted_iota(jnp.int32, (T, T), 0)
        col = lax.broadcasted_iota(jnp.int32, (T, T), 1)
        lo = _lo_mask((T, LANE))

        for qi in range(NB):
            qb = q_ref[qi * T:(qi + 1) * T, :]

            def step(j, carry, diagonal):
                m, l, acc = carry
                off = pl.multiple_of(j * T, T)
                kb = k_ref[pl.ds(off, T), :]
                vb = v_ref[pl.ds(off, T), :]
                ss = []
                for e in (0, 1):
                    se = _dot_nt(qb[:, e * QKP:(e + 1) * QKP], kb[:, e * QKP:(e + 1) * QKP]) * MLA_SCALE
                    ss.append(jnp.where(col <= row, se, NEG) if diagonal else se)
                s = jnp.concatenate(ss, axis=0)
                m_new = jnp.maximum(m, jnp.max(s, axis=-1, keepdims=True))
                a = jnp.exp(m - m_new)
                p = jnp.exp(s - m_new)
                l = a * l + jnp.sum(p, axis=-1, keepdims=True)
                acc = a * acc + _dot(p.astype(BF16), vb)
                return m_new, l, acc

            init = (jnp.full((2 * T, 1), NEG, F32), jnp.zeros((2 * T, 1), F32), jnp.zeros((2 * T, LANE), F32))
            carry = lax.fori_loop(0, qi, functools.partial(step, diagonal=False), init, unroll=MLA_UNROLL)
            m, l, acc = step(qi, carry, True)
            o = acc / l
            lse = m + jnp.log(l)
            o_ref[qi * T:(qi + 1) * T, :] = jnp.where(lo, o[:T], o[T:])
            l_ref[qi * T:(qi + 1) * T, :] = jnp.where(lo, lse[:T], lse[T:])

    def spec(w):
        return pl.BlockSpec((None, S, w), lambda b, hp: (b, 0, hp))

    return pl.pallas_call(
        body,
        grid=(B, NH // 2),
        in_specs=[spec(2 * QKP), spec(2 * QKP), spec(LANE)],
        out_specs=[spec(LANE), spec(LANE)],
        out_shape=[jax.ShapeDtypeStruct((B, S, NH * VD), F32)] * 2,
        name="mla_attn_fwd",
        compiler_params=pltpu.CompilerParams(dimension_semantics=("arbitrary",) * 2, vmem_limit_bytes=VMEM_LIMIT),
    )(q, k, v)


def _mla_bwd(q, k, v, do, o, lse):
    B = q.shape[0]
    T = MLA_T
    NB = S // T

    def body(q_ref, k_ref, v_ref, do_ref, o_ref, l_ref, dq_ref, dk_ref, dv_ref, delta_ref, dqt_ref):
        delta_ref[...] = _head_sum(do_ref[...] * o_ref[...])
        row = lax.broadcasted_iota(jnp.int32, (T, T), 0)
        col = lax.broadcasted_iota(jnp.int32, (T, T), 1)
        lo = _lo_mask((T, LANE))
        tn_t = (((0,), (1,)), ((), ()))

        for j in range(NB):
            krows = slice(j * T, (j + 1) * T)
            kb = k_ref[krows, :]
            vb = v_ref[krows, :]
            dkt = [jnp.zeros((QKP, T), F32), jnp.zeros((QKP, T), F32)]
            dvt = jnp.zeros((LANE, T), F32)
            for i in range(j, NB):
                qrows = slice(i * T, (i + 1) * T)
                qb = q_ref[qrows, :]
                do2 = _stack_heads(do_ref[qrows, :], lo).astype(BF16)
                lb = l_ref[qrows, :]
                db = delta_ref[qrows, :]
                dp2 = _dot_nt(do2, vb)
                ps = []
                for e in (0, 1):
                    cols = slice(e * QKP, (e + 1) * QKP)
                    qe, ke = qb[:, cols], kb[:, cols]
                    s = _dot_nt(qe, ke) * MLA_SCALE
                    if i == j:
                        s = jnp.where(col <= row, s, NEG)
                    p = jnp.exp(s - lb[:, e * HD:e * HD + 1])
                    ps.append(p.astype(BF16))
                    ds = (p * (dp2[e * T:(e + 1) * T] - db[:, e * HD:e * HD + 1]) * MLA_SCALE).astype(BF16)
                    dkt[e] = dkt[e] + _dot_tn(qe, ds)
                    dq_t = lax.dot_general(ke, ds, tn_t, preferred_element_type=F32)
                    if j == 0:
                        dqt_ref[e, :, qrows] = dq_t
                    else:
                        dqt_ref[e, :, qrows] += dq_t
                dvt = dvt + _dot_tn(do2, jnp.concatenate(ps, axis=0))
            dk_ref[krows, 0:QKP] = dkt[0].T
            dk_ref[krows, QKP:2 * QKP] = dkt[1].T
            dv_ref[krows, :] = dvt.T
        dq_ref[:, 0:QKP] = dqt_ref[0].T
        dq_ref[:, QKP:2 * QKP] = dqt_ref[1].T

    def spec(w):
        return pl.BlockSpec((None, S, w), lambda b, hp: (b, 0, hp))

    return pl.pallas_call(
        body,
        grid=(B, NH // 2),
        in_specs=[spec(2 * QKP), spec(2 * QKP), spec(LANE), spec(LANE), spec(LANE), spec(LANE)],
        out_specs=[spec(2 * QKP), spec(2 * QKP), spec(LANE)],
        out_shape=[jax.ShapeDtypeStruct((B, S, NH * QKP), F32), jax.ShapeDtypeStruct((B, S, NH * QKP), F32),
                   jax.ShapeDtypeStruct((B, S, NH * VD), F32)],
        scratch_shapes=[pltpu.VMEM((S, LANE), F32), pltpu.VMEM((2, QKP, S), F32)],
        name="mla_attn_bwd",
        compiler_params=pltpu.CompilerParams(dimension_semantics=("arbitrary",) * 2, vmem_limit_bytes=VMEM_LIMIT),
    )(q, k, v, do, o, lse)


def _lo_mask(shape):
    return lax.broadcasted_iota(jnp.int32, shape, len(shape) - 1) < HD


def _head_sum(u):
    r = lax.broadcasted_iota(jnp.int32, (LANE, LANE), 0) < HD
    c = lax.broadcasted_iota(jnp.int32, (LANE, LANE), 1) < HD
    ones = jnp.where(r == c, 1.0, 0.0).astype(BF16)
    hi = u.astype(BF16)
    lo = (u - hi.astype(F32)).astype(BF16)
    return _dot(hi, ones) + _dot(lo, ones)


def _head_sum_1(u):
    r = lax.broadcasted_iota(jnp.int32, (LANE, LANE), 0) < HD
    c = lax.broadcasted_iota(jnp.int32, (LANE, LANE), 1) < HD
    return _dot(u.astype(BF16), jnp.where(r == c, 1.0, 0.0).astype(BF16))


def _rms2_scale(x):
    return lax.rsqrt(_head_sum(x * x) / HD + EPS)


def _rms2(x, g):
    return x * _rms2_scale(x) * g


def _rms2_bwd(x, r, g, dy):
    xn = x * r
    t = dy * g
    dx = r * (t - xn * (_head_sum_1(xn * t) * (1.0 / HD)))
    return dx, jnp.sum(dy * xn, axis=0, keepdims=True)


def _dil_bias(t_ref, gi, d):
    qq = lax.broadcasted_iota(jnp.int32, (QB, QB), 0)
    kk = lax.broadcasted_iota(jnp.int32, (QB, QB), 1)
    jc = (qq - kk).astype(F32)
    rows = []
    for e in (0, 1):
        sl = t_ref[2 * gi + e:2 * gi + e + 1, :] * float(d)
        bp = jnp.where(kk >= qq, -sl * (jc + float(QB)), NEG)
        bc = jnp.where(kk <= qq, -sl * jc, NEG)
        rows.append(jnp.concatenate([bp, bc], axis=1))
    return jnp.concatenate(rows, axis=0)


def _dil_rows(cur, d):
    return pl.ds(cur, QB, stride=d) if d > 1 else pl.ds(pl.multiple_of(cur, QB), QB)


def _dil_walk(d, block, full):
    if d == 1:
        block(0, None)

        def body(i, c):
            block(i * QB, (i - 1) * QB)
            return c
        lax.fori_loop(1, S // QB, body, 0, unroll=True if full else 5)
    elif d == 16:
        def body(r, c):
            block(r, None)
            return c
        lax.fori_loop(0, d, body, 0, unroll=True if full else 4)
    else:
        nb = S // d // QB

        def cls(r, c):
            block(r, None)

            def body(i, c2):
                block(r + i * QB * d, r + (i - 1) * QB * d)
                return c2
            lax.fori_loop(1, nb, body, 0, unroll=True)
            return c
        lax.fori_loop(0, d, cls, 0, unroll=full)


def _stack_heads(x, lo):
    return jnp.concatenate([jnp.where(lo, x, 0.0), jnp.where(lo, 0.0, x)], axis=0)


def _dilc_fwd(proj3, gq, gk, tab):
    B = proj3.shape[0]

    def body(q_ref, k_ref, v_ref, cz_ref, gq_ref, gk_ref, t_ref, y_ref, o_ref, l_ref, qs, ks, vs):
        g = pl.program_id(2)
        lo = _lo_mask((QB, LANE))

        def group(gi):
            d = DIL[gi][1]
            qs[...] = _rms2(q_ref[...].astype(F32), gq_ref[gi:gi + 1, :])
            ks[...] = _rms2(k_ref[...].astype(F32), gk_ref[gi:gi + 1, :])
            vs[...] = v_ref[...].astype(F32)
            bias = _dil_bias(t_ref, gi, d)

            def block(cur, prev):
                rows = _dil_rows(cur, d)
                q2 = _stack_heads(qs[rows, :], lo).astype(BF16)
                kc, vc = ks[rows, :], vs[rows, :]
                if prev is None:
                    kcat, vcat, b = kc, vc, bias[:, QB:]
                else:
                    prow = _dil_rows(prev, d)
                    kcat = jnp.concatenate([ks[prow, :], kc], axis=0)
                    vcat = jnp.concatenate([vs[prow, :], vc], axis=0)
                    b = bias
                s = _dot_nt(q2, kcat.astype(BF16)) * DIL_SCALE + b
                m = jnp.max(s, axis=-1, keepdims=True)
                p = jnp.exp(s - m)
                l = jnp.sum(p, axis=-1, keepdims=True)
                o = _dot(p.astype(BF16), vcat.astype(BF16)) / l
                lse = m + jnp.log(l)
                o_ref[gi, rows, :] = jnp.where(lo, o[:QB], o[QB:])
                l_ref[gi, rows, :] = jnp.where(lo, lse[:QB], lse[QB:])

            _dil_walk(d, block, True)

        for gi in range(NG):
            pl.when(g == gi)(functools.partial(group, gi))

        @pl.when(g == NG - 1)
        def _():
            y_ref[...] = _mergec_math(o_ref[0], o_ref[1], o_ref[2], l_ref[0], l_ref[1], l_ref[2],
                                      cz_ref[...].astype(F32)).astype(BF16)

    def col(base):
        return pl.BlockSpec((None, S, LANE), lambda b, hp, g: (b, 0, base // LANE + 4 * g + hp))

    gspec = pl.BlockSpec((NG, LANE), lambda b, hp, g: (0, 0))
    saved = pl.BlockSpec((NG, None, S, LANE), lambda b, hp, g: (0, b, 0, hp))
    return pl.pallas_call(
        body,
        grid=(B, 4, NG),
        in_specs=[col(O_DQ), col(O_DK), col(O_DV),
                  pl.BlockSpec((None, S, LANE), lambda b, hp, g: (b, 0, O_CZ // LANE + hp)),
                  gspec, gspec, pl.BlockSpec((None, 8, LANE), lambda b, hp, g: (hp, 0, 0))],
        out_specs=[pl.BlockSpec((None, S, LANE), lambda b, hp, g: (b, 0, hp)), saved, saved],
        out_shape=[jax.ShapeDtypeStruct((B, S, DWID), BF16), jax.ShapeDtypeStruct((NG, B, S, DWID), F32),
                   jax.ShapeDtypeStruct((NG, B, S, DWID), F32)],
        scratch_shapes=[pltpu.VMEM((S, LANE), F32)] * 3,
        name="dil_mixer_fwd",
        compiler_params=pltpu.CompilerParams(dimension_semantics=("arbitrary",) * 3, vmem_limit_bytes=VMEM_LIMIT),
    )(proj3, proj3, proj3, proj3, gq, gk, tab)


MERGE_ROWS = 256


def _dilc_bwd(proj3, gq, gk, tab, o_all, l_all, d_yc, dproj3):
    B = proj3.shape[0]

    def body(q_ref, k_ref, v_ref, cz_ref, gq_ref, gk_ref, t_ref, o_ref, l_ref, dy_ref, dp_in,
             dp_out, dgq_out, dgk_out, qs, ks, vs, dos, dls, dqs, dks, dvs, rqs, rks, dczs,
             st_q, st_k, st_v, st_z, sems, sem_z):
        b_, hp, g = pl.program_id(0), pl.program_id(1), pl.program_id(2)
        col = lambda base: pl.ds(pl.multiple_of(base + hp * LANE, LANE), LANE)
        lo = _lo_mask((QB, LANE))

        @pl.when(jnp.logical_and(jnp.logical_and(pl.program_id(0) == 0, pl.program_id(1) == 0), g == 0))
        def _():
            dgq_out[...] = jnp.zeros((NG, LANE), F32)
            dgk_out[...] = jnp.zeros((NG, LANE), F32)

        @pl.when(g == 0)
        def _():
            def chunk(i, carry):
                rows = pl.ds(pl.multiple_of(i * MERGE_ROWS, MERGE_ROWS), MERGE_ROWS)
                ls = [l_ref[j, rows, :] for j in range(NG)]
                m = jnp.maximum(jnp.maximum(ls[0], ls[1]), ls[2])
                es = [jnp.exp(t - m) for t in ls]
                den = (es[0] + es[1]) + es[2]
                al = [e / den for e in es]
                os_ = [o_ref[j, rows, :] for j in range(NG)]
                oc = (al[0] * os_[0] + al[1] * os_[1]) + al[2] * os_[2]
                cz = cz_ref[rows, :].astype(F32)
                sg = jax.nn.sigmoid(cz)
                dy = dy_ref[rows, :]
                d_oc = dy * (cz * sg)
                dczs[rows, :] = (dy * oc * (sg * (1.0 + cz * (1.0 - sg)))).astype(BF16)
                ts = [_head_sum_1(d_oc * os_[j]) for j in range(NG)]
                tbar = (al[0] * ts[0] + al[1] * ts[1]) + al[2] * ts[2]
                for j in range(NG):
                    dos[j, rows, :] = al[j] * d_oc
                    dls[j, rows, :] = al[j] * (ts[j] - tbar)
                return carry
            lax.fori_loop(0, S // MERGE_ROWS, chunk, 0)
            _write_windows(dp_out, [st_z], sem_z, b_ * 4 + hp, B * 4, [(dczs[...], dp_out.at[b_, :, col(O_CZ)])])

        def group(gi):
            d = DIL[gi][1]
            xq, xk = q_ref[...].astype(F32), k_ref[...].astype(F32)
            rqs[...] = _rms2_scale(xq)
            rks[...] = _rms2_scale(xk)
            qs[...] = xq * rqs[...] * gq_ref[gi:gi + 1, :]
            ks[...] = xk * rks[...] * gk_ref[gi:gi + 1, :]
            vs[...] = v_ref[...].astype(F32)
            dks[...] = jnp.zeros((S, LANE), F32)
            dvs[...] = jnp.zeros((S, LANE), F32)
            bias = _dil_bias(t_ref, gi, d)

            def block(cur, prev):
                rows = _dil_rows(cur, d)
                q2 = _stack_heads(qs[rows, :], lo).astype(BF16)
                dob = dos[gi, rows, :]
                do2 = _stack_heads(dob, lo).astype(BF16)
                kc, vc = ks[rows, :], vs[rows, :]
                if prev is None:
                    kcat, vcat, b = kc, vc, bias[:, QB:]
                else:
                    prow = _dil_rows(prev, d)
                    kcat = jnp.concatenate([ks[prow, :], kc], axis=0)
                    vcat = jnp.concatenate([vs[prow, :], vc], axis=0)
                    b = bias
                kcat = kcat.astype(BF16)
                vcat = vcat.astype(BF16)
                lse_b = l_ref[gi, rows, :]
                corr_b = dls[gi, rows, :] - _head_sum_1(dob * o_ref[gi, rows, :])
                lse2 = jnp.concatenate([lse_b[:, 0:1], lse_b[:, HD:HD + 1]], axis=0)
                corr2 = jnp.concatenate([corr_b[:, 0:1], corr_b[:, HD:HD + 1]], axis=0)
                s = _dot_nt(q2, kcat) * DIL_SCALE + b
                p = jnp.exp(s - lse2)
                ds = (p * (_dot_nt(do2, vcat) + corr2) * DIL_SCALE).astype(BF16)
                dq2 = _dot(ds, kcat)
                dqs[rows, :] = jnp.where(lo, dq2[:QB], dq2[QB:])
                dk = _dot_tn(ds, q2)
                dv = _dot_tn(p.astype(BF16), do2)
                if prev is None:
                    dks[rows, :] += dk
                    dvs[rows, :] += dv
                else:
                    dks[prow, :] += dk[:QB]
                    dvs[prow, :] += dv[:QB]
                    dks[rows, :] += dk[QB:]
                    dvs[rows, :] += dv[QB:]

            _dil_walk(d, block, False)

            dxq, dgq = _rms2_bwd(q_ref[...].astype(F32), rqs[...], gq_ref[gi:gi + 1, :], dqs[...])
            dgq_out[gi:gi + 1, :] += dgq
            dxk, dgk = _rms2_bwd(k_ref[...].astype(F32), rks[...], gk_ref[gi:gi + 1, :], dks[...])
            dgk_out[gi:gi + 1, :] += dgk
            step, nsteps = _grid_step((B, 4, NG))
            _write_windows(dp_out, [st_q, st_k, st_v], sems, step, nsteps,
                           [(dxq, dp_out.at[b_, :, col(O_DQ + gi * DWID)]), (dxk, dp_out.at[b_, :, col(O_DK + gi * DWID)]),
                            (dvs[...], dp_out.at[b_, :, col(O_DV + gi * DWID)])])

        for gi in range(NG):
            pl.when(g == gi)(functools.partial(group, gi))

    def col(base):
        return pl.BlockSpec((None, S, LANE), lambda b, hp, g: (b, 0, base // LANE + 4 * g + hp))

    gspec = pl.BlockSpec((NG, LANE), lambda b, hp, g: (0, 0))
    saved = pl.BlockSpec((NG, None, S, LANE), lambda b, hp, g: (0, b, 0, hp))
    per_pair = pl.BlockSpec((None, S, LANE), lambda b, hp, g: (b, 0, hp))
    return pl.pallas_call(
        body,
        grid=(B, 4, NG),
        in_specs=[col(O_DQ), col(O_DK), col(O_DV),
                  pl.BlockSpec((None, S, LANE), lambda b, hp, g: (b, 0, O_CZ // LANE + hp)),
                  gspec, gspec, pl.BlockSpec((None, 8, LANE), lambda b, hp, g: (hp, 0, 0)),
                  saved, saved, per_pair, pl.BlockSpec(memory_space=pl.ANY)],
        out_specs=[pl.BlockSpec(memory_space=pl.ANY), gspec, gspec],
        out_shape=[jax.ShapeDtypeStruct(dproj3.shape, dproj3.dtype), jax.ShapeDtypeStruct((NG, LANE), F32),
                   jax.ShapeDtypeStruct((NG, LANE), F32)],
        input_output_aliases={10: 0},
        scratch_shapes=[pltpu.VMEM((S, LANE), F32)] * 3 + [pltpu.VMEM((NG, S, LANE), F32)] * 2
        + [pltpu.VMEM((S, LANE), F32)] * 5 + [pltpu.VMEM((S, LANE), BF16)] + [pltpu.VMEM((2, S, LANE), BF16)] * 4
        + [pltpu.SemaphoreType.DMA((3, 2)), pltpu.SemaphoreType.DMA((1, 2))],
        name="dil_mixer_bwd",
        compiler_params=pltpu.CompilerParams(dimension_semantics=("arbitrary",) * 3, vmem_limit_bytes=VMEM_LIMIT),
    )(proj3, proj3, proj3, proj3, gq, gk, tab, o_all, l_all, d_yc, dproj3)


def _dil_slopes():
    slopes = (2.0 ** (-8.0 * np.arange(1, NG * DH + 1, dtype=np.float32) / (NG * DH))).astype(np.float32).reshape(NG, DH)
    tab = np.zeros((4, 8, LANE), np.float32)
    for hp in range(4):
        for gi in range(NG):
            for e in (0, 1):
                tab[hp, 2 * gi + e, :] = slopes[gi, 2 * hp + e]
    return jnp.asarray(tab)


def _rope_tables():
    inv = ROPE_THETA ** (-jnp.arange(0, ROPE, 2, dtype=F32) / ROPE)
    ang = jnp.arange(S, dtype=F32)[:, None] * inv[None, :]
    cos, sin = jnp.cos(ang), jnp.sin(ang)
    z16 = jnp.zeros((S, 16), F32)
    c = jnp.concatenate([jnp.ones((S, NOPE), F32), cos, cos, jnp.zeros((S, 32), F32)], axis=1)
    s1 = jnp.concatenate([jnp.zeros((S, NOPE), F32), -sin, z16, jnp.zeros((S, 32), F32)], axis=1)
    s2 = jnp.concatenate([jnp.zeros((S, NOPE), F32), z16, sin, jnp.zeros((S, 32), F32)], axis=1)
    return c, s1, s2


def _pad_heads_uq(w):
    return jnp.pad(w.reshape(QL, NH, QK), ((0, 0), (0, 0), (0, QKP - QK))).reshape(QL, NH * QKP)


def _unpad_heads_uq(g):
    return g.reshape(QL, NH, QKP)[:, :, :QK].reshape(QL, NH * QK)


def _split_ukv(w):
    w3 = w.reshape(KVL, NH, NOPE + VD)
    uk = jnp.pad(w3[:, :, :NOPE], ((0, 0), (0, 0), (0, QKP - NOPE))).reshape(KVL, NH * QKP)
    return uk, w3[:, :, NOPE:].reshape(KVL, NH * VD)


def _join_ukv(guk, guv):
    return jnp.concatenate([guk.reshape(KVL, NH, QKP)[:, :, :NOPE], guv.reshape(KVL, NH, VD)],
                           axis=-1).reshape(KVL, NH * (NOPE + VD))


BR = 512
BRM = 256


def _layer_fwd(x, w, tabs, batch, rest=None):
    T = batch * S
    rope_c, rope_s1, rope_s2, dil_tab = tabs
    res = {"x": x}
    row = lambda c: (lambda i: (i, c))
    fix = lambda i: (0, 0)

    h = _pcall("norm_fwd", _rms, (T // BR,),
               [(x, (BR, D), row(0)), (w["norm_g"], (1, D), fix)],
               [((T, D), BF16, (BR, D), row(0))])[0]
    proj = _mm("in_proj", h, w["w_in_t"], tb=True, out_dtype=BF16, b_words=True, tm=2048, tn=1024)
    res["h"], res["proj"] = h, proj
    proj3 = proj.reshape(batch, S, NINP)
    if rest is not None:
        w = rest(proj)

    cblk = lambda s: (lambda j, b: (b, 0, 4 * s + j))
    y_a = _pcall("conv_fwd", _conv_math, (4, batch),
                 [(proj3, (None, S, LANE), cblk(0)), (proj3, (None, S, LANE), cblk(1)),
                  (proj3, (None, S, LANE), cblk(2)), (proj3, (None, S, LANE), cblk(3)),
                  (w["conv_w"], (3, LANE), lambda j, b: (0, j)), (w["conv_b"], (1, LANE), lambda j, b: (0, j))],
                 [((batch, S, CW), BF16, (None, S, LANE), lambda j, b: (b, 0, j))])[0].reshape(T, CW)
    res["y_a"] = y_a

    cqn, ckvn = _pcall("mla_pre_fwd", _mla_pre_math, (T // BR,),
                       [(proj, (BR, QL), row(O_CQ // QL)), (proj, (BR, KVL), row(O_CKV // KVL)),
                        (w["q_a_norm_g"], (1, QL), fix), (w["kv_a_norm_g"], (1, KVL), fix)],
                       [((T, QL), BF16, (BR, QL), row(0)), ((T, KVL), BF16, (BR, KVL), row(0))])
    w_uq_p = _pad_heads_uq(w["w_uq"])
    w_uk, w_uv = _split_ukv(w["w_ukv"])
    q = _mm("uq", cqn, w_uq_p, out_dtype=BF16)
    kn = _mm("uk", ckvn, w_uk, out_dtype=BF16)
    v = _mm("uv", ckvn, w_uv, out_dtype=BF16)
    nrr = S // BR
    tab_row = lambda i: (i % nrr, 0)
    qr, kr = _pcall("rope_fwd", _rope_math, (T // BR,),
                    [(q, (BR, NH * QKP), row(0)), (kn, (BR, NH * QKP), row(0)), (proj, (BR, LANE), row(O_KPE // LANE)),
                     (w["mla_q_norm_g"], (1, QKP), fix), (w["mla_k_norm_g"], (1, QKP), fix),
                     (rope_c, (BR, QKP), tab_row), (rope_s1, (BR, QKP), tab_row), (rope_s2, (BR, QKP), tab_row)],
                    [((T, NH * QKP), BF16, (BR, NH * QKP), row(0))] * 2)
    qr = qr.reshape(batch, S, NH * QKP)
    kr = kr.reshape(batch, S, NH * QKP)
    v = v.reshape(batch, S, NH * VD)
    o_b, l_b = _mla_fwd(qr, kr, v)
    ob2 = o_b.reshape(T, NH * VD)
    y_b = _pcall("gateb_fwd", _gate_math, (T // BR,),
                 [(ob2, (BR, 512), row(0)), (proj, (BR, 512), row(O_BZ // 512))],
                 [((T, 512), BF16, (BR, 512), row(0))])[0]
    res.update(cqn=cqn, ckvn=ckvn, q=q, kn=kn, qr=qr, kr=kr, v=v, o_b=o_b, l_b=l_b, ob2=ob2, y_b=y_b,
               w_uq_p=w_uq_p, w_uk=w_uk, w_uv=w_uv)

    gq2 = jnp.tile(w["dil_q_norm_g"].reshape(NG, HD), (1, 2))
    gk2 = jnp.tile(w["dil_k_norm_g"].reshape(NG, HD), (1, 2))
    y_c, o_all, l_all = _dilc_fwd(proj3, gq2, gk2, dil_tab)
    y_c = y_c.reshape(T, DWID)
    res.update(o_all=o_all, l_all=l_all, y_c=y_c)

    pa = _mm("out_a", y_a, w["w_out_a"], out_dtype=BF16)
    pb = _mm("out_b", y_b, w["w_out_b"], out_dtype=BF16)
    pc = _mm("out_c", y_c, w["w_out_c"], out_dtype=BF16)
    merged = _pcall("merge_fwd", _merge_math, (T // BRM,),
                    [(proj, (BRM, D), row(O_G // D + s)) for s in range(3)]
                    + [(w["b_gate"], (1, D), (lambda s: (lambda i: (0, s)))(s)) for s in range(3)]
                    + [(t, (BRM, D), row(0)) for t in (pa, pb, pc)],
                    [((T, D), BF16, (BRM, D), row(0))])[0]
    out = _mm("o_proj", merged, w["w_o"], add=x, tm=1024)
    res.update(pa=pa, pb=pb, pc=pc, merged=merged)
    return out, res


def _norm_bwd_math(x, g, dh, dy):
    _, pull = jax.vjp(_rms, x, g)
    dx, dg = pull(dh)
    return dx + dy, dg


def _layer_bwd(dy, w, res, tabs, batch, after_dw=None, after_merge=None):
    T = batch * S
    rope_c, rope_s1, rope_s2, dil_tab = tabs
    row = lambda c: (lambda i: (i, c))
    fix = lambda i: (0, 0)
    x, proj, h = res["x"], res["proj"], res["h"]
    proj3 = proj.reshape(batch, S, NINP)
    g = {}

    d_merged = _mm("o_proj_dx", dy, w["w_o"], tb=True)
    g["w_o"] = _mm("o_proj_dw", res["merged"], dy, ta=True, tm=1024, tk=2048)

    dproj = lax.empty((T, NINP), BF16)
    rows_of = lambda br: (lambda ref, i: ref.at[pl.ds(pl.multiple_of(i * br, br), br)])

    def merge_bwd(*args):
        dg0, dg1, dg2, db0, db1, db2, dpa, dpb, dpc = _vjp_of(_merge_math, 9)(*args, n_prim=9)
        return db0, db1, db2, dpa, dpb, dpc, jnp.concatenate([dg0, dg1, dg2], axis=1)

    db0, db1, db2, dpa, dpb, dpc, dproj = _pcall(
        "merge_bwd", merge_bwd, (T // BRM,),
        [(proj, (BRM, D), row(O_G // D + s)) for s in range(3)]
        + [(w["b_gate"], (1, D), (lambda s: (lambda i: (0, s)))(s)) for s in range(3)]
        + [(t, (BRM, D), row(0)) for t in (res["pa"], res["pb"], res["pc"])]
        + [(d_merged, (BRM, D), row(0))],
        [((1, D), F32, (1, D), fix, True)] * 3 + [((T, D), BF16, (BRM, D), row(0))] * 3,
        into=(dproj, [((BRM, 3 * D), lambda ref, i: rows_of(BRM)(ref, i).at[:, O_G:O_G + 3 * D])]))
    g["b_gate"] = jnp.concatenate([db0, db1, db2], axis=1)

    dep = after_merge(dpa) if after_merge is not None else None
    d_ya = _mm("out_a_dx", dpa, w["w_out_a"], tb=True, dep=dep)
    d_yb = _mm("out_b_dx", dpb, w["w_out_b"], tb=True)
    d_yc = _mm("out_c_dx", dpc, w["w_out_c"], tb=True)
    g["w_out_a"] = _mm("out_a_dw", res["y_a"], dpa, ta=True, tk=T)
    g["w_out_b"] = _mm("out_b_dw", res["y_b"], dpb, ta=True, tk=T)
    g["w_out_c"] = _mm("out_c_dw", res["y_c"], dpc, ta=True, tk=T)

    cblk = lambda s: (lambda j, b: (b, 0, 4 * s + j))
    oblk = lambda j, b: (b, 0, j)
    def conv_bwd(*args):
        d_ab, d_ac, d_ax, d_az, dcw, dcb = _vjp_of(_conv_math, 6)(*args, n_prim=6)
        return dcw, dcb, d_ab, d_ac, d_ax, d_az

    a_col = lambda s_: (lambda ref, j, b: ref.at[b, :, pl.ds(pl.multiple_of(O_A + s_ * CW + j * LANE, LANE), LANE)])
    g["conv_w"], g["conv_b"], dproj3 = _pcall(
        "conv_bwd", conv_bwd, (4, batch),
        [(proj3, (None, S, LANE), cblk(s)) for s in range(4)]
        + [(w["conv_w"], (3, LANE), lambda j, b: (0, j)), (w["conv_b"], (1, LANE), lambda j, b: (0, j)),
           (d_ya.reshape(batch, S, CW), (None, S, LANE), oblk)],
        [((3, CW), F32, (3, LANE), lambda j, b: (0, j), True), ((1, CW), F32, (1, LANE), lambda j, b: (0, j), True)],
        into=(dproj.reshape(batch, S, NINP), [((S, LANE), a_col(s_)) for s_ in range(4)]))
    dproj = dproj3.reshape(T, NINP)

    gate_bwd = functools.partial(_vjp_of(_gate_math, 2), n_prim=2)
    d_ob, dproj = _pcall("gateb_bwd", gate_bwd, (T // BR,),
                         [(res["ob2"], (BR, 512), row(0)), (proj, (BR, 512), row(O_BZ // 512)), (d_yb, (BR, 512), row(0))],
                         [((T, 512), F32, (BR, 512), row(0))],
                         into=(dproj, [((BR, 512), lambda ref, i: rows_of(BR)(ref, i).at[:, O_BZ:O_BZ + 512])]))
    dqr, dkr, dv = _mla_bwd(res["qr"], res["kr"], res["v"], d_ob.reshape(batch, S, NH * VD), res["o_b"], res["l_b"])
    nrr = S // BR
    tab_row = lambda i: (i % nrr, 0)
    def rope_bwd(*args):
        d_q, d_kn, d_kpe, dgq, dgk = _vjp_of(_rope_math, 5)(*args, n_prim=8)
        return d_q, d_kn, dgq, dgk, d_kpe

    d_q, d_kn, g["mla_q_norm_g"], g["mla_k_norm_g"], dproj = _pcall(
        "rope_bwd", rope_bwd, (T // BR,),
        [(res["q"], (BR, NH * QKP), row(0)), (res["kn"], (BR, NH * QKP), row(0)), (proj, (BR, LANE), row(O_KPE // LANE)),
         (w["mla_q_norm_g"], (1, QKP), fix), (w["mla_k_norm_g"], (1, QKP), fix),
         (rope_c, (BR, QKP), tab_row), (rope_s1, (BR, QKP), tab_row), (rope_s2, (BR, QKP), tab_row),
         (dqr.reshape(T, NH * QKP), (BR, NH * QKP), row(0)), (dkr.reshape(T, NH * QKP), (BR, NH * QKP), row(0))],
        [((T, NH * QKP), BF16, (BR, NH * QKP), row(0))] * 2 + [((1, QKP), F32, (1, QKP), fix, True)] * 2,
        into=(dproj, [((BR, LANE), lambda ref, i: rows_of(BR)(ref, i).at[:, O_KPE:O_KPE + LANE])]))
    dv = dv.reshape(T, NH * VD)
    d_cqn = _mm("uq_dx", d_q, res["w_uq_p"], tb=True)
    d_ckvn = _mm("uk_dx", d_kn, res["w_uk"], tb=True)
    d_ckvn = _mm("uv_dx", dv, res["w_uv"], tb=True, add=d_ckvn)
    g["w_uq"] = _unpad_heads_uq(_mm("uq_dw", res["cqn"], d_q, ta=True, tk=T))
    g["w_ukv"] = _join_ukv(_mm("uk_dw", res["ckvn"], d_kn, ta=True, tk=T),
                           _mm("uv_dw", res["ckvn"], dv, ta=True, tk=T))
    def pre_bwd(*args):
        d_cq, d_ckv, dgq, dgkv = _vjp_of(_mla_pre_math, 4)(*args, n_prim=4)
        return dgq, dgkv, jnp.concatenate([d_cq, d_ckv], axis=1)

    g["q_a_norm_g"], g["kv_a_norm_g"], dproj = _pcall(
        "mla_pre_bwd", pre_bwd, (T // BR,),
        [(proj, (BR, QL), row(O_CQ // QL)), (proj, (BR, KVL), row(O_CKV // KVL)),
         (w["q_a_norm_g"], (1, QL), fix), (w["kv_a_norm_g"], (1, KVL), fix),
         (d_cqn, (BR, QL), row(0)), (d_ckvn, (BR, KVL), row(0))],
        [((1, QL), F32, (1, QL), fix, True), ((1, KVL), F32, (1, KVL), fix, True)],
        into=(dproj, [((BR, QL + KVL), lambda ref, i: rows_of(BR)(ref, i).at[:, O_CQ:O_CQ + QL + KVL])]))

    gq2 = jnp.tile(w["dil_q_norm_g"].reshape(NG, HD), (1, 2))
    gk2 = jnp.tile(w["dil_k_norm_g"].reshape(NG, HD), (1, 2))
    dproj3, dgq, dgk = _dilc_bwd(proj3, gq2, gk2, dil_tab, res["o_all"], res["l_all"],
                                 d_yc.reshape(batch, S, DWID), dproj.reshape(batch, S, NINP))
    dproj = dproj3.reshape(T, NINP)
    g["dil_q_norm_g"] = dgq[:, :HD] + dgq[:, HD:]
    g["dil_k_norm_g"] = dgk[:, :HD] + dgk[:, HD:]

    g["w_in_t"] = _mm("in_proj_dw", dproj, h, ta=True, tm=1024, tk=T)
    dep = after_dw(g) if after_dw is not None else None
    d_h = _mm("in_proj_dx", dproj, w["w_in_t"], dep=dep, b_words=True, tm=1024, tk=NINP // 4)
    dx, g["norm_g"] = _pcall("norm_bwd", _norm_bwd_math, (T // BR,),
                             [(x, (BR, D), row(0)), (w["norm_g"], (1, D), fix), (d_h, (BR, D), row(0)),
                              (dy, (BR, D), row(0))],
                             [((T, D), F32, (BR, D), row(0)), ((1, D), F32, (1, D), fix, True)])
    return dx, g


def _loss_math(y, t):
    e = y - t
    return e * (1.0 / D), 0.5 * jnp.sum(jnp.sum(e * e, axis=-1, keepdims=True) / D, axis=0, keepdims=True)


ANY = pl.BlockSpec(memory_space=pl.ANY)
U32 = jnp.uint32
WSH = NIN // 4
WA = KPE_END
WB = WSH - WA
CWD = 512
PACK_ROWS = 1472
HW = PACK_W // 2


def _me():
    return lax.axis_index("x"), lax.axis_index("y"), lax.axis_index("c")


def _piece_rows(k):
    a = k * WSH + jnp.where(k > 0, NINP - NIN, 0)
    b = k * WSH + WA + (NINP - NIN)
    return ((0, pl.multiple_of(a, 8), WA), (WA, pl.multiple_of(b, 8), WB))


def _pack_words(lo, hi):
    ul = lax.bitcast_convert_type(lo.astype(BF16).astype(F32), U32)
    uh = lax.bitcast_convert_type(hi.astype(BF16).astype(F32), U32)
    w = jnp.bitwise_or(jnp.bitwise_and(uh, jnp.uint32(0xFFFF0000)), jnp.right_shift(ul, jnp.uint32(16)))
    return lax.bitcast_convert_type(w, F32)


def _unpack_words(w):
    w = lax.bitcast_convert_type(w, U32)
    lo = lax.bitcast_convert_type(jnp.left_shift(w, jnp.uint32(16)), F32)
    hi = lax.bitcast_convert_type(jnp.bitwise_and(w, jnp.uint32(0xFFFF0000)), F32)
    return lo, hi


def _all_gather(wc):
    def body(w_ref, ow_ref, send_sems, recv_sems):
        x, y, c = _me()
        k_me = 2 * x + y
        sib = (x, y, 1 - c)
        chips = [(1 - x, y), (x, 1 - y), (1 - x, 1 - y)]
        wcols = lambda cc: pl.ds(pl.multiple_of(cc * (CWD // 2), LANE), CWD // 2)

        def windows(k, cc):
            return [(w_ref.at[pl.ds(l0, n), wcols(cc)], ow_ref.at[pl.ds(p0, n), wcols(cc)])
                    for l0, p0, n in _piece_rows(k)]

        def copy(i, src, dst, to):
            return pltpu.make_async_remote_copy(src_ref=src, dst_ref=dst, send_sem=send_sems.at[i],
                                                recv_sem=recv_sems.at[i], device_id=to, device_id_type=MESH)

        def own_windows():
            return [(w_ref.at[pl.ds(l0, n)], ow_ref.at[pl.ds(p0, n)]) for l0, p0, n in _piece_rows(k_me)]

        first = [copy(12 + i, src, dst, sib) for i, (src, dst) in enumerate(own_windows())]
        for j, (cx, cy) in enumerate(chips):
            for i, (src, dst) in enumerate(windows(k_me, c)):
                first.append(copy(2 * j + i, src, dst, (cx, cy, c)))
        for cp in first:
            cp.start()
        passed = []
        for j, (cx, cy) in enumerate(chips):
            for i, (_, dst) in enumerate(windows(2 * cx + cy, c)):
                copy(2 * j + i, dst, dst, (cx, cy, c)).wait_recv()
                cp = copy(6 + 2 * j + i, dst, dst, sib)
                cp.start()
                passed.append(cp)
        for j, (cx, cy) in enumerate(chips):
            for i, (_, dst) in enumerate(windows(2 * cx + cy, 1 - c)):
                copy(6 + 2 * j + i, dst, dst, sib).wait_recv()
        for i, (_, dst) in enumerate(own_windows()):
            copy(12 + i, dst, dst, sib).wait_recv()
        for cp in first + passed:
            cp.wait_send()

    return pl.pallas_call(
        body,
        out_shape=jax.ShapeDtypeStruct((NINP, CWD), F32),
        in_specs=[ANY], out_specs=ANY,
        scratch_shapes=[pltpu.SemaphoreType.DMA((14,)), pltpu.SemaphoreType.DMA((14,))],
        name="weights_all_gather",
    )(wc)


HBM = pl.BlockSpec(memory_space=pltpu.HBM)
SEM = pl.BlockSpec(memory_space=pltpu.SEMAPHORE)
EFFECT = pltpu.SideEffectType.DATAFLOW_SIDE_EFFECTING


def _in_hbm(a):
    return pltpu.with_memory_space_constraint(a, pltpu.HBM)


def _ag_shard(w_ref, s_ref, lw_ref, ls_ref, k, with_w):
    pack = [(s_ref, ls_ref.at[k])]
    if not with_w:
        return pack
    return [(w_ref.at[pl.ds(l0, n)], lw_ref.at[pl.ds(p0, n)]) for l0, p0, n in _piece_rows(k)] + pack


def _ag_behind_copies(w_ref, s_ref, lw_ref, ls_ref, send_sems, recv_sems, with_w):
    x, y, c = _me()
    peers = [(1 - x, y, c), (x, 1 - y, c), (1 - x, 1 - y, c), (x, y, 1 - c)]
    mine, theirs = [], []
    for j, (px, py, pc) in enumerate(peers):
        for i, ((src, dst), (_, got)) in enumerate(zip(_ag_shard(w_ref, s_ref, lw_ref, ls_ref, 2 * x + y, with_w),
                                                       _ag_shard(w_ref, s_ref, lw_ref, ls_ref, 2 * px + py, with_w))):
            mk = lambda s_, d_: pltpu.make_async_remote_copy(
                src_ref=s_, dst_ref=d_, send_sem=send_sems.at[3 * j + i], recv_sem=recv_sems.at[3 * j + i],
                device_id=(px, py, pc), device_id_type=MESH)
            mine.append(mk(src, dst))
            theirs.append(mk(got, got))
    return mine, theirs


def _ag_behind_start(wc, sp, dep, tag):
    with_w = wc is not None
    if not with_w:
        wc = jnp.zeros((8, LANE), F32)
    lw = lax.empty((NINP, CWD) if with_w else (8, LANE), F32)

    def body(w_ref, s_ref, lw_ref, ls_ref, dep_ref, send_sems, recv_sems, w_thru, s_thru, lw_thru, ls_thru, token):
        mine, _ = _ag_behind_copies(w_ref, s_ref, lw_ref, ls_ref, send_sems, recv_sems, with_w)
        for cp in mine:
            cp.start()
        token[...] = jnp.zeros_like(token)

    return pl.pallas_call(
        body, name="weights_gather_start_" + tag,
        out_shape=(pltpu.SemaphoreType.DMA((12,)), pltpu.SemaphoreType.DMA((12,)), pltpu.HBM(wc.shape, wc.dtype),
                   pltpu.HBM(sp.shape, sp.dtype), pltpu.HBM(lw.shape, F32), pltpu.HBM((4, PACK_ROWS, PACK_W), BF16),
                   jax.ShapeDtypeStruct((8, LANE), F32)),
        in_specs=(HBM, HBM, HBM, HBM, ANY),
        out_specs=(SEM, SEM, HBM, HBM, HBM, HBM, pl.BlockSpec(memory_space=pltpu.VMEM)),
        input_output_aliases={0: 2, 1: 3, 2: 4, 3: 5},
        compiler_params=pltpu.CompilerParams(has_side_effects=EFFECT),
    )(_in_hbm(wc), _in_hbm(sp), _in_hbm(lw), _in_hbm(lax.empty((4, PACK_ROWS, PACK_W), BF16)), dep)


def _ag_behind_wait(send_sems, recv_sems, wc, sp, lw, ls, after, tag):
    with_w = lw.shape == (NINP, CWD)

    def body(w_ref, s_ref, lw_ref, ls_ref, send_sems, recv_sems, after_ref, w_dead, s_dead, lw_out, ls_out):
        mine, theirs = _ag_behind_copies(w_ref, s_ref, lw_ref, ls_ref, send_sems, recv_sems, with_w)
        for cp in mine:
            cp.wait_send()
        for cp in theirs:
            cp.wait_recv()

    out = pl.pallas_call(
        body, name="weights_gather_wait_" + tag,
        out_shape=(pltpu.HBM(wc.shape, wc.dtype), pltpu.HBM(sp.shape, sp.dtype), pltpu.HBM(lw.shape, lw.dtype),
                   pltpu.HBM(ls.shape, ls.dtype)),
        in_specs=(HBM, HBM, HBM, HBM, SEM, SEM, ANY), out_specs=(HBM, HBM, HBM, HBM),
        input_output_aliases={0: 0, 1: 1, 2: 2, 3: 3},
        compiler_params=pltpu.CompilerParams(has_side_effects=EFFECT),
    )(wc, sp, lw, ls, send_sems, recv_sems, after)
    return out[2], out[3]


def _rs_swap_copies(w_ref, s_ref, rw_ref, rs_ref, send_sems, recv_sems):
    x, y, c = _me()
    oc = 1 - c
    return [pltpu.make_async_remote_copy(src_ref=w_ref.at[:, pl.ds(pl.multiple_of(oc * (D // 2), LANE), D // 2)],
                                         dst_ref=rw_ref, send_sem=send_sems.at[0], recv_sem=recv_sems.at[0],
                                         device_id=(x, y, oc), device_id_type=MESH),
            pltpu.make_async_remote_copy(src_ref=s_ref.at[:, :, pl.ds(pl.multiple_of(oc * HW, LANE), HW)],
                                         dst_ref=rs_ref, send_sem=send_sems.at[1], recv_sem=recv_sems.at[1],
                                         device_id=(x, y, oc), device_id_type=MESH)]


def _rs_swap_start(gw, gs, tag):
    def body(w_ref, s_ref, rw_ref, rs_ref, send_sems, recv_sems, w_thru, s_thru, rw_thru, rs_thru, token):
        for cp in _rs_swap_copies(w_ref, s_ref, rw_ref, rs_ref, send_sems, recv_sems):
            cp.start()
        token[...] = jnp.zeros_like(token)

    return pl.pallas_call(
        body, name="grads_swap_start_" + tag,
        out_shape=(pltpu.SemaphoreType.DMA((2,)), pltpu.SemaphoreType.DMA((2,)), pltpu.HBM(gw.shape, gw.dtype),
                   pltpu.HBM(gs.shape, gs.dtype), pltpu.HBM((NINP, D // 2), F32), pltpu.HBM((4, PACK_ROWS, HW), F32),
                   jax.ShapeDtypeStruct((8, LANE), F32)),
        in_specs=(HBM, HBM, HBM, HBM),
        out_specs=(SEM, SEM, HBM, HBM, HBM, HBM, pl.BlockSpec(memory_space=pltpu.VMEM)),
        input_output_aliases={0: 2, 1: 3, 2: 4, 3: 5},
        compiler_params=pltpu.CompilerParams(has_side_effects=EFFECT),
    )(_in_hbm(gw), _in_hbm(gs), _in_hbm(lax.empty((NINP, D // 2), F32)), _in_hbm(lax.empty((4, PACK_ROWS, HW), F32)))


def _rs_swap_wait(send_sems, recv_sems, gw, gs, rw, rs, after, tag):
    def body(w_ref, s_ref, rw_ref, rs_ref, send_sems, recv_sems, after_ref, w_out, s_out, rw_out, rs_out):
        for cp in _rs_swap_copies(w_ref, s_ref, rw_ref, rs_ref, send_sems, recv_sems):
            cp.wait()

    return pl.pallas_call(
        body, name="grads_swap_wait_" + tag,
        out_shape=(pltpu.HBM(gw.shape, gw.dtype), pltpu.HBM(gs.shape, gs.dtype), pltpu.HBM(rw.shape, rw.dtype),
                   pltpu.HBM(rs.shape, rs.dtype)),
        in_specs=(HBM, HBM, HBM, HBM, SEM, SEM, ANY), out_specs=(HBM, HBM, HBM, HBM),
        input_output_aliases={0: 0, 1: 1, 2: 2, 3: 3},
        compiler_params=pltpu.CompilerParams(has_side_effects=EFFECT),
    )(gw, gs, rw, rs, send_sems, recv_sems, after)


SUM_BR = 1024
SUM_BUFS = 3


def _rs_chip_sum_w(gw, rw, cidx):
    steps = NINP // SUM_BR

    def body(c_ref, g_ref, r_ref, o_ref, gbuf, rbuf, obuf, in_sems, out_sems):
        gcol = pl.ds(pl.multiple_of(c_ref[0] * (D // 2), LANE), D // 2)
        rows = lambda i: pl.ds(i * SUM_BR, SUM_BR)

        def fetch(i):
            slot = i % SUM_BUFS
            return (pltpu.make_async_copy(g_ref.at[rows(i), gcol], gbuf.at[slot], in_sems.at[2 * slot]),
                    pltpu.make_async_copy(r_ref.at[rows(i)], rbuf.at[slot], in_sems.at[2 * slot + 1]))

        def put(i):
            return pltpu.make_async_copy(obuf.at[i % 2], o_ref.at[rows(i)], out_sems.at[i % 2])

        for i in range(SUM_BUFS):
            for cp in fetch(i):
                cp.start()
        q = D // 8
        for i in range(steps):
            for cp in fetch(i):
                cp.wait()
            if i >= 2:
                put(i - 2).wait()
            s = gbuf[i % SUM_BUFS] + rbuf[i % SUM_BUFS]
            obuf[i % 2] = jnp.concatenate([_pack_words(s[:, 0:q], s[:, q:2 * q]),
                                           _pack_words(s[:, 2 * q:3 * q], s[:, 3 * q:4 * q])], axis=1)
            put(i).start()
            if i + SUM_BUFS < steps:
                for cp in fetch(i + SUM_BUFS):
                    cp.start()
        put(steps - 2).wait()
        put(steps - 1).wait()

    return pl.pallas_call(
        body,
        grid_spec=pltpu.PrefetchScalarGridSpec(
            num_scalar_prefetch=1, grid=(1,), in_specs=[ANY, ANY], out_specs=ANY,
            scratch_shapes=[pltpu.VMEM((SUM_BUFS, SUM_BR, D // 2), F32), pltpu.VMEM((SUM_BUFS, SUM_BR, D // 2), F32),
                            pltpu.VMEM((2, SUM_BR, D // 4), F32), pltpu.SemaphoreType.DMA((2 * SUM_BUFS,)),
                            pltpu.SemaphoreType.DMA((2,))]),
        out_shape=jax.ShapeDtypeStruct((NINP, D // 4), F32),
        name="grads_chip_sum_w",
        compiler_params=pltpu.CompilerParams(dimension_semantics=("arbitrary",), vmem_limit_bytes=VMEM_LIMIT),
    )(cidx, gw, rw)


def _rs_chip_sum_s(gs, rs, cidx):
    def body(c_ref, g_ref, r_ref, o_ref):
        o_ref[...] = (g_ref[...] + r_ref[...]).astype(BF16)

    return pl.pallas_call(
        body,
        grid_spec=pltpu.PrefetchScalarGridSpec(
            num_scalar_prefetch=1, grid=(4,),
            in_specs=[pl.BlockSpec((None, PACK_ROWS, HW), lambda j, cr: (j, 0, cr[0])),
                      pl.BlockSpec((None, PACK_ROWS, HW), lambda j, cr: (j, 0, 0))],
            out_specs=pl.BlockSpec((None, PACK_ROWS, HW), lambda j, cr: (j, 0, 0))),
        out_shape=jax.ShapeDtypeStruct((4, PACK_ROWS, HW), BF16),
        name="grads_chip_sum_s",
        compiler_params=pltpu.CompilerParams(dimension_semantics=("arbitrary",), vmem_limit_bytes=VMEM_LIMIT),
    )(cidx, gs, rs)


def _rs_exchange_copies(sw_ref, ss_ref, r2w_ref, r2s_ref, send_sems, recv_sems):
    x, y, c = _me()
    mine, theirs = [], []
    for j, (cx, cy) in enumerate([(1 - x, y), (x, 1 - y), (1 - x, 1 - y)]):
        def mk(i, src, dst):
            return pltpu.make_async_remote_copy(src_ref=src, dst_ref=dst, send_sem=send_sems.at[3 * j + i],
                                                recv_sem=recv_sems.at[3 * j + i], device_id=(cx, cy, c), device_id_type=MESH)
        for i, (l0, p0, n) in enumerate(_piece_rows(2 * cx + cy)):
            mine.append(mk(i, sw_ref.at[pl.ds(p0, n)], r2w_ref.at[j, pl.ds(l0, n)]))
            theirs.append(mk(i, r2w_ref.at[j, pl.ds(l0, n)], r2w_ref.at[j, pl.ds(l0, n)]))
        mine.append(mk(2, ss_ref.at[2 * cx + cy], r2s_ref.at[j]))
        theirs.append(mk(2, r2s_ref.at[j], r2s_ref.at[j]))
    return mine, theirs


def _rs_exchange_start(sw, ss, tag):
    def body(sw_ref, ss_ref, r2w_ref, r2s_ref, send_sems, recv_sems, sw_thru, ss_thru, r2w_thru, r2s_thru, token):
        mine, _ = _rs_exchange_copies(sw_ref, ss_ref, r2w_ref, r2s_ref, send_sems, recv_sems)
        for cp in mine:
            cp.start()
        token[...] = jnp.zeros_like(token)

    return pl.pallas_call(
        body, name="grads_exchange_start_" + tag,
        out_shape=(pltpu.SemaphoreType.DMA((9,)), pltpu.SemaphoreType.DMA((9,)), pltpu.HBM(sw.shape, sw.dtype),
                   pltpu.HBM(ss.shape, ss.dtype), pltpu.HBM((3, WSH, D // 4), F32), pltpu.HBM((3, PACK_ROWS, HW), BF16),
                   jax.ShapeDtypeStruct((8, LANE), F32)),
        in_specs=(HBM, HBM, HBM, HBM),
        out_specs=(SEM, SEM, HBM, HBM, HBM, HBM, pl.BlockSpec(memory_space=pltpu.VMEM)),
        input_output_aliases={0: 2, 1: 3, 2: 4, 3: 5},
        compiler_params=pltpu.CompilerParams(has_side_effects=EFFECT),
    )(_in_hbm(sw), _in_hbm(ss), _in_hbm(lax.empty((3, WSH, D // 4), F32)), _in_hbm(lax.empty((3, PACK_ROWS, HW), BF16)))


def _rs_exchange_wait(send_sems, recv_sems, sw, ss, r2w, r2s, after, tag):
    def body(sw_ref, ss_ref, r2w_ref, r2s_ref, send_sems, recv_sems, after_ref, sw_dead, ss_dead, r2w_out, r2s_out):
        mine, theirs = _rs_exchange_copies(sw_ref, ss_ref, r2w_ref, r2s_ref, send_sems, recv_sems)
        for cp in mine:
            cp.wait_send()
        for cp in theirs:
            cp.wait_recv()

    out = pl.pallas_call(
        body, name="grads_exchange_wait_" + tag,
        out_shape=(pltpu.HBM(sw.shape, sw.dtype), pltpu.HBM(ss.shape, ss.dtype), pltpu.HBM(r2w.shape, r2w.dtype),
                   pltpu.HBM(r2s.shape, r2s.dtype)),
        in_specs=(HBM, HBM, HBM, HBM, SEM, SEM, ANY), out_specs=(HBM, HBM, HBM, HBM),
        input_output_aliases={0: 0, 1: 1, 2: 2, 3: 3},
        compiler_params=pltpu.CompilerParams(has_side_effects=EFFECT),
    )(sw, ss, r2w, r2s, send_sems, recv_sems, after)
    return out[2], out[3]


def _rs_final_w(gw, rw, r2w, idx, both, layer):
    q = D // 8

    def body(i_ref, g_ref, r_ref, p_ref, both_ref, o_ref, gbuf, rbuf, sems):
        i = pl.program_id(0)
        k, c = i_ref[0], i_ref[1]
        cps = []
        for n_, (l0, p0, n) in enumerate(_piece_rows(k)):
            gcol = pl.ds(pl.multiple_of(c * (D // 2) + i * 2 * q, LANE), 2 * q)
            rcol = pl.ds(pl.multiple_of(i * 2 * q, LANE), 2 * q)
            cps.append(pltpu.make_async_copy(g_ref.at[pl.ds(p0, n), gcol], gbuf.at[pl.ds(l0, n)], sems.at[2 * n_]))
            cps.append(pltpu.make_async_copy(r_ref.at[pl.ds(p0, n), rcol], rbuf.at[pl.ds(l0, n)], sems.at[2 * n_ + 1]))
        for cp in cps:
            cp.start()
        for cp in cps:
            cp.wait()
        acc = gbuf[...] + rbuf[...]
        for j in range(3):
            lo, hi = _unpack_words(p_ref[j])
            acc = acc + jnp.concatenate([lo, hi], axis=1)
        o_ref[...] = acc

    return pl.pallas_call(
        body,
        grid_spec=pltpu.PrefetchScalarGridSpec(
            num_scalar_prefetch=1, grid=(2,),
            in_specs=[ANY, ANY, pl.BlockSpec((3, WSH, q), lambda i, ir: (0, 0, i)), ANY],
            out_specs=pl.BlockSpec((None, WSH, 2 * q), lambda i, ir: (layer, 0, 2 * ir[1] + i)),
            scratch_shapes=[pltpu.VMEM((WSH, 2 * q), F32), pltpu.VMEM((WSH, 2 * q), F32), pltpu.SemaphoreType.DMA((4,))]),
        out_shape=jax.ShapeDtypeStruct((NL, WSH, D), F32),
        input_output_aliases={4: 0},
        name="grads_final_sum_w",
        compiler_params=pltpu.CompilerParams(dimension_semantics=("arbitrary",), vmem_limit_bytes=VMEM_LIMIT),
    )(idx, gw, rw, r2w, both)


def _rs_final_s(gs, rs, r2s, idx):
    def body(i_ref, g_ref, r_ref, p_ref, o_ref):
        acc = g_ref[...] + r_ref[...]
        for j in range(3):
            acc = acc + p_ref[j].astype(F32)
        o_ref[...] = acc

    return pl.pallas_call(
        body,
        grid_spec=pltpu.PrefetchScalarGridSpec(
            num_scalar_prefetch=1, grid=(1,),
            in_specs=[pl.BlockSpec((None, PACK_ROWS, HW), lambda i, ir: (ir[0], 0, ir[1])),
                      pl.BlockSpec((None, PACK_ROWS, HW), lambda i, ir: (ir[0], 0, 0)),
                      pl.BlockSpec((3, PACK_ROWS, HW), lambda i, ir: (0, 0, 0))],
            out_specs=pl.BlockSpec((PACK_ROWS, HW), lambda i, ir: (0, ir[1]))),
        out_shape=jax.ShapeDtypeStruct((PACK_ROWS, PACK_W), F32),
        name="grads_final_sum_s",
        compiler_params=pltpu.CompilerParams(dimension_semantics=("arbitrary",), vmem_limit_bytes=VMEM_LIMIT),
    )(idx, gs, rs, r2s)


def _rs_share(fw, fs, layer):
    def body(w_ref, s_ref, ow_ref, os_ref, send_sems, recv_sems):
        x, y, c = _me()
        wcol = lambda cc: pl.ds(pl.multiple_of(cc * (D // 2), LANE), D // 2)
        scol = lambda cc: pl.ds(pl.multiple_of(cc * HW, LANE), HW)

        def copies(cc):
            return [pltpu.make_async_remote_copy(src_ref=w_ref.at[layer, :, wcol(cc)],
                                                 dst_ref=ow_ref.at[layer, :, wcol(cc)],
                                                 send_sem=send_sems.at[0], recv_sem=recv_sems.at[0],
                                                 device_id=(x, y, 1 - c), device_id_type=MESH),
                    pltpu.make_async_remote_copy(src_ref=s_ref.at[:, scol(cc)], dst_ref=os_ref.at[:, scol(cc)],
                                                 send_sem=send_sems.at[1], recv_sem=recv_sems.at[1],
                                                 device_id=(x, y, 1 - c), device_id_type=MESH)]
        out = copies(c)
        for cp in out:
            cp.start()
        for cp in copies(1 - c):
            cp.wait_recv()
        for cp in out:
            cp.wait_send()

    return pl.pallas_call(
        body,
        out_shape=[jax.ShapeDtypeStruct(fw.shape, F32), jax.ShapeDtypeStruct(fs.shape, F32)],
        in_specs=[ANY, ANY], out_specs=[ANY, ANY],
        input_output_aliases={0: 0, 1: 1},
        scratch_shapes=[pltpu.SemaphoreType.DMA((2,)), pltpu.SemaphoreType.DMA((2,))],
        name="grads_share",
    )(fw, fs)


def _rs_sums(gw, gs, rw, rs):
    x, y, c = _me()
    cidx = jnp.reshape(c, (1,)).astype(jnp.int32)
    return dict(gw=gw, gs=gs, rw=rw, rs=rs, sw=_rs_chip_sum_w(gw, rw, cidx), ss=_rs_chip_sum_s(gs, rs, cidx))


def _rs_end(st, r2w, r2s, both, layer):
    x, y, c = _me()
    idx = jnp.stack([2 * x + y, c]).astype(jnp.int32)
    return _rs_share(_rs_final_w(st["gw"], st["rw"], r2w, idx, both, layer),
                     _rs_final_s(st["gs"], st["rs"], r2s, idx), layer)


def _all_reduce_small(gs):
    rows = gs.shape[0]

    def body(g_ref, o_ref, buf, send_sems, recv_sems):
        x, y, c = _me()
        me = 4 * x + 2 * y + c
        buf[me] = g_ref[...]
        cps = []
        for r in range(1, 8):
            fx, fy, fc = (r >> 2) & 1, (r >> 1) & 1, r & 1
            px, py, pc = jnp.bitwise_xor(x, fx), jnp.bitwise_xor(y, fy), jnp.bitwise_xor(c, fc)
            cps.append((pltpu.make_async_remote_copy(
                src_ref=g_ref, dst_ref=buf.at[me], send_sem=send_sems.at[r - 1], recv_sem=recv_sems.at[r - 1],
                device_id=(px, py, pc), device_id_type=MESH), 4 * px + 2 * py + pc))
        for cp, _ in cps:
            cp.start()
        for r, (cp, peer) in enumerate(cps):
            pltpu.make_async_remote_copy(
                src_ref=g_ref, dst_ref=buf.at[peer], send_sem=send_sems.at[r], recv_sem=recv_sems.at[r],
                device_id=(x, y, c), device_id_type=MESH).wait_recv()
        for cp, _ in cps:
            cp.wait_send()
        acc = buf[0]
        for k in range(1, 8):
            acc = acc + buf[k]
        o_ref[...] = acc

    return pl.pallas_call(
        body,
        out_shape=jax.ShapeDtypeStruct((rows, LANE), F32),
        in_specs=[pl.BlockSpec(memory_space=pltpu.VMEM)],
        out_specs=pl.BlockSpec(memory_space=pltpu.VMEM),
        scratch_shapes=[pltpu.VMEM((8, rows, LANE), F32), pltpu.SemaphoreType.DMA((7,)), pltpu.SemaphoreType.DMA((7,))],
        name="small_grads_all_reduce",
    )(gs)


PACK_SPLIT = (("w_uq", 96, (QL, 192)), ("w_ukv", 64, (KVL, 256)),
              ("w_out_a", 256, (CW, 256)), ("w_out_b", 256, (CW, 256)), ("w_out_c", 256, (CW, 256)),
              ("w_o", 512, (256, D)))
MAT_ROWS = 1440
CONV_SHARD = 3 * 128


def _w_in_words(w_in_shard):
    t = w_in_shard.T
    return _pack_words(t[:, :CWD], t[:, CWD:])


def _pack_weights(wl):
    parts = [wl[n].astype(BF16).reshape(-1, PACK_W) for n, _, _ in PACK_SPLIT]
    cw = wl["conv_w"].reshape(-1)
    hi = cw.astype(BF16)
    r1 = cw - hi.astype(F32)
    mid = r1.astype(BF16)
    lo = (r1 - mid.astype(F32)).astype(BF16)
    cterms = jnp.pad(jnp.concatenate([hi, mid, lo]), (0, 3 * PACK_W - 3 * CONV_SHARD)).reshape(3, PACK_W)
    tail = jnp.pad(cterms, ((0, PACK_ROWS - MAT_ROWS - 3), (0, 0)))
    return jnp.concatenate(parts + [tail], axis=0)


def _unpack_weights(gath):
    out = {}
    r = 0
    for n, nrows, shp in PACK_SPLIT:
        t = gath[:, r:r + nrows].reshape((4,) + shp)
        r += nrows
        if n == "w_o":
            out[n] = t.reshape(4 * shp[0], shp[1])
        else:
            out[n] = t.transpose(1, 0, 2).reshape(shp[0], 4 * shp[1])
    ct = gath[:, r:r + 3].reshape(4, 3 * PACK_W)[:, :3 * CONV_SHARD].astype(F32).reshape(4, 3, CONV_SHARD)
    cw = (ct[:, 0] + ct[:, 1]) + ct[:, 2]
    out["conv_w"] = cw.reshape(4, 3, 128).transpose(1, 0, 2).reshape(3, CW)
    return out


def _pack_grads(g):
    parts = []
    for n, nrows, shp in PACK_SPLIT:
        t = g[n]
        if n == "w_o":
            t = t.reshape((4,) + shp)
        else:
            t = t.reshape(shp[0], 4, shp[1]).transpose(1, 0, 2)
        parts.append(t.reshape(4, nrows, PACK_W))
    cw = g["conv_w"].reshape(3, 4, 128).transpose(1, 0, 2).reshape(4, 1, CONV_SHARD)
    parts.append(jnp.pad(cw, ((0, 0), (0, PACK_ROWS - MAT_ROWS - 1), (0, PACK_W - CONV_SHARD))))
    return jnp.concatenate(parts, axis=1)


def _unpack_grads(red):
    out = {}
    r = 0
    for n, nrows, shp in PACK_SPLIT:
        out[n] = red[r:r + nrows].reshape(shp)
        r += nrows
    out["conv_w"] = red[r, :CONV_SHARD].reshape(3, 128)
    return out


SMALL_SIZES = (("norm_g", D), ("b_gate", 3 * D), ("conv_b", CW), ("q_a_norm_g", QL), ("kv_a_norm_g", KVL),
               ("mla_q_norm_g", QK), ("mla_k_norm_g", QK), ("dil_q_norm_g", NG * HD), ("dil_k_norm_g", NG * HD))
SMALL_ROWS = 88


def _pack_small(per_name):
    flat = jnp.concatenate([per_name[n].reshape(-1).astype(F32) for n, _ in SMALL_SIZES])
    return jnp.pad(flat, (0, SMALL_ROWS * LANE - flat.shape[0])).reshape(SMALL_ROWS, LANE)


def _unpack_small(packed, like):
    out = {}
    flat = packed.reshape(-1)
    r = 0
    for n, sz in SMALL_SIZES:
        out[n] = flat[r:r + NL * sz].reshape(like[n].shape)
        r += NL * sz
    return out


def _adamw_math(w, g, m, v):
    m = ADAM_B1 * m + (1.0 - ADAM_B1) * g
    v = ADAM_B2 * v + (1.0 - ADAM_B2) * jnp.square(g)
    m_hat = m / (1.0 - ADAM_B1 ** ADAM_STEP)
    v_hat = v / (1.0 - ADAM_B2 ** ADAM_STEP)
    delta = -ADAM_LR * (m_hat / (jnp.sqrt(v_hat) + ADAM_EPS) + ADAM_WD * w)
    return delta, m, v


def _adamw(name, w, g, m, v, br, bc=None):
    L, R, C = w.shape
    bc = C if bc is None else bc
    blk = lambda l, i, j: (l, i, j)
    return _pcall(name, _adamw_math, (L, R // br, C // bc), [(t, (None, br, bc), blk) for t in (w, g, m, v)],
                  [((L, R, C), F32, (None, br, bc), blk)] * 3)


ADAM_ROWS = {"w_uq": 256, "w_ukv": 128, "w_out_a": 512, "w_out_b": 512, "w_out_c": 512, "w_o": 256,
             "conv_w": 3}


def kernel(x, norm_g, w_in, b_gate, conv_w, conv_b, q_a_norm_g, w_uq, kv_a_norm_g, w_ukv, mla_q_norm_g, mla_k_norm_g, dil_q_norm_g, dil_k_norm_g, w_out_a, w_out_b, w_out_c, w_o, loss_target, m_norm_g, m_w_in, m_b_gate, m_conv_w, m_conv_b, m_q_a_norm_g, m_w_uq, m_kv_a_norm_g, m_w_ukv, m_mla_q_norm_g, m_mla_k_norm_g, m_dil_q_norm_g, m_dil_k_norm_g, m_w_out_a, m_w_out_b, m_w_out_c, m_w_o, v_norm_g, v_w_in, v_b_gate, v_conv_w, v_conv_b, v_q_a_norm_g, v_w_uq, v_kv_a_norm_g, v_w_ukv, v_mla_q_norm_g, v_mla_k_norm_g, v_dil_q_norm_g, v_dil_k_norm_g, v_w_out_a, v_w_out_b, v_w_out_c, v_w_o):
    W = dict(norm_g=norm_g, w_in=w_in, b_gate=b_gate, conv_w=conv_w, conv_b=conv_b, q_a_norm_g=q_a_norm_g, w_uq=w_uq,
             kv_a_norm_g=kv_a_norm_g, w_ukv=w_ukv, mla_q_norm_g=mla_q_norm_g, mla_k_norm_g=mla_k_norm_g,
             dil_q_norm_g=dil_q_norm_g, dil_k_norm_g=dil_k_norm_g, w_out_a=w_out_a, w_out_b=w_out_b, w_out_c=w_out_c,
             w_o=w_o)
    M = dict(norm_g=m_norm_g, w_in=m_w_in, b_gate=m_b_gate, conv_w=m_conv_w, conv_b=m_conv_b, q_a_norm_g=m_q_a_norm_g,
             w_uq=m_w_uq, kv_a_norm_g=m_kv_a_norm_g, w_ukv=m_w_ukv, mla_q_norm_g=m_mla_q_norm_g,
             mla_k_norm_g=m_mla_k_norm_g, dil_q_norm_g=m_dil_q_norm_g, dil_k_norm_g=m_dil_k_norm_g, w_out_a=m_w_out_a,
             w_out_b=m_w_out_b, w_out_c=m_w_out_c, w_o=m_w_o)
    V = dict(norm_g=v_norm_g, w_in=v_w_in, b_gate=v_b_gate, conv_w=v_conv_w, conv_b=v_conv_b, q_a_norm_g=v_q_a_norm_g,
             w_uq=v_w_uq, kv_a_norm_g=v_kv_a_norm_g, w_ukv=v_w_ukv, mla_q_norm_g=v_mla_q_norm_g,
             mla_k_norm_g=v_mla_k_norm_g, dil_q_norm_g=v_dil_q_norm_g, dil_k_norm_g=v_dil_k_norm_g, w_out_a=v_w_out_a,
             w_out_b=v_w_out_b, w_out_c=v_w_out_c, w_o=v_w_o)
    batch = x.shape[0]
    T = batch * S

    def layer_weights(l, cont, gath):
        full = _unpack_weights(gath)
        pad_qk = lambda t: jnp.pad(t, (0, QKP - QK)).reshape(1, QKP)
        full.update(
            w_in_t=cont,
            norm_g=norm_g[l].reshape(1, D), b_gate=b_gate[l].reshape(1, 3 * D), conv_b=conv_b[l].reshape(1, CW),
            q_a_norm_g=q_a_norm_g[l].reshape(1, QL), kv_a_norm_g=kv_a_norm_g[l].reshape(1, KVL),
            mla_q_norm_g=pad_qk(mla_q_norm_g[l]), mla_k_norm_g=pad_qk(mla_k_norm_g[l]),
            dil_q_norm_g=dil_q_norm_g[l].reshape(NG, 1, HD), dil_k_norm_g=dil_k_norm_g[l].reshape(NG, 1, HD))
        return full

    words = [_w_in_words(w_in[l]) for l in range(NL)]
    packs = [_pack_weights({n: W[n][l] for n in BIG[1:] + ("conv_w",)}) for l in range(NL)]
    tabs = _rope_tables() + (_dil_slopes(),)
    x2 = x.reshape(T, D)

    cont0 = _all_gather(words[0])
    ag0 = _ag_behind_start(None, packs[0], cont0, "0")
    ag1 = []

    def rest_of_layer0(proj):
        _, gath0 = _ag_behind_wait(*ag0[:6], proj, "0")
        ag1.extend(_ag_behind_start(words[1], packs[1], gath0, "1"))
        w = layer_weights(0, cont0, gath0)
        w["conv_b"] = w["conv_b"] + ag1[6][0:1, 0:1]
        all0.append(w)
        return w

    first0 = dict(w_in_t=cont0, norm_g=norm_g[0].reshape(1, D) + ag0[6][0:1, 0:1])
    all0 = []
    y0, res0 = _layer_fwd(x2, first0, tabs, batch, rest=rest_of_layer0)
    w0 = all0[0]
    w1 = layer_weights(1, *_ag_behind_wait(*ag1[:6], y0, "1"))
    y1, res1 = _layer_fwd(y0, w1, tabs, batch)

    row = lambda i: (i, 0)
    dy, loss = _pcall("loss", _loss_math, (T // BR,),
                      [(y1, (BR, D), row), (loss_target.reshape(T, D), (BR, D), row)],
                      [((T, D), F32, (BR, D), row), ((1, 1), F32, (1, 1), lambda i: (0, 0), True)])
    loss = lax.psum(loss[0, 0], ("x", "y", "c"))

    grads = [None] * NL
    dy, grads[1] = _layer_bwd(dy, w1, res1, tabs, batch)
    st = [None] * NL
    ex = [None] * NL
    sw1 = _rs_swap_start(grads[1]["w_in_t"], _pack_grads(grads[1]), "1")
    w0["w_o"] = w0["w_o"] + sw1[6][0:1, 0:1].astype(BF16)

    def exchange_layer1(t):
        st[1] = _rs_sums(*_rs_swap_wait(*sw1[:6], t, "1"))
        ex[1] = _rs_exchange_start(st[1]["sw"], st[1]["ss"], "1")
        return ex[1][6]

    red = [None] * NL
    g_in_t = [lax.empty((NL, WSH, D), F32)]

    def finish(l, after):
        r2w, r2s = _rs_exchange_wait(*ex[l][:6], after, str(l))
        g_in_t[0], rs = _rs_end(st[l], r2w, r2s, g_in_t[0], l)
        red[l] = _unpack_grads(rs)
        return rs

    def start_layer0(g):
        sw0 = _rs_swap_start(g["w_in_t"], _pack_grads(g), "0")
        done1 = finish(1, sw0[6])
        st[0] = _rs_sums(*_rs_swap_wait(*sw0[:6], done1, "0"))
        ex[0] = _rs_exchange_start(st[0]["sw"], st[0]["ss"], "0")
        return ex[0][6]

    dx, grads[0] = _layer_bwd(dy, w0, res0, tabs, batch, after_dw=start_layer0, after_merge=exchange_layer1)
    grad_x = dx.reshape(batch, S, D)
    finish(0, dx)

    G = {n: jnp.stack([red[l][n] for l in range(NL)]) for n in BIG[1:] + ("conv_w",)}
    g_in_t = g_in_t[0]
    G["w_in"] = jnp.swapaxes(g_in_t, 1, 2)
    small_g = {n: jnp.stack([grads[l][n].reshape(-1)[:sz] for l in range(NL)]) for n, sz in SMALL_SIZES}
    small_red = _all_reduce_small(_pack_small(small_g))
    G.update(_unpack_small(small_red, {n: W[n] for n in SMALL}))

    delta, new_m, new_v = {}, {}, {}
    for n in BIG[1:] + ("conv_w",):
        delta[n], new_m[n], new_v[n] = _adamw("adamw_" + n, W[n], G[n], M[n], V[n], ADAM_ROWS[n])
    tr = lambda t: jnp.swapaxes(t, 1, 2)
    delta["w_in"], new_m["w_in"], new_v["w_in"] = (
        tr(t) for t in _adamw("adamw_w_in", tr(w_in), g_in_t, tr(m_w_in), tr(v_w_in), WSH, LANE))
    sw, sm, sv = (_pack_small({n: t[n] for n in SMALL})[None] for t in (W, M, V))
    sd, snm, snv = _adamw("adamw_small", sw, small_red[None], sm, sv, SMALL_ROWS)
    like = {n: W[n] for n in SMALL}
    delta.update(_unpack_small(sd[0], like))
    new_m.update(_unpack_small(snm[0], like))
    new_v.update(_unpack_small(snv[0], like))

    return (loss, grad_x, *[G[n] for n in WEIGHTS], *[delta[n] for n in WEIGHTS],
            *[new_m[n] for n in WEIGHTS], *[new_v[n] for n in WEIGHTS])
```

```python
import functools

import numpy as np
import jax
import jax.numpy as jnp
from jax import lax
from jax.experimental import pallas as pl
from jax.experimental.pallas import tpu as pltpu

F32 = jnp.float32
BF16 = jnp.bfloat16

D = 1024
S = 2048
NL = 2
CW = 512
NH = 8
QL = 256
KVL = 128
NOPE = 64
ROPE = 32
VD = 64
QK = NOPE + ROPE
QKP = 128
ROPE_THETA = 10000.0
DIL = ((128, 1), (512, 4), (2048, 16))
NG = 3
DH = 8
HD = 64
DWID = DH * HD
QB = 128
EPS = 1e-6
NIN = 11168
NINP = 11264
O_A, O_CQ, O_CKV, O_KPE, O_BZ, O_DQ, O_DK, O_DV, O_CZ, O_G = 0, 2048, 2304, 2432, 2560, 3072, 4608, 6144, 7680, 8192
KPE_END = 2464
NEG = -1e30
MLA_SCALE = QK ** -0.5
DIL_SCALE = HD ** -0.5
LANE = 128
PACK_W = 512
VMEM_LIMIT = 48 * 1024 * 1024

ADAM_LR = 0.001
ADAM_B1 = 0.9
ADAM_B2 = 0.999
ADAM_EPS = 1e-08
ADAM_WD = 0.01
ADAM_STEP = 10

MESH = pl.DeviceIdType.MESH
BIG = ("w_in", "w_uq", "w_ukv", "w_out_a", "w_out_b", "w_out_c", "w_o")
SMALL = ("norm_g", "b_gate", "conv_b", "q_a_norm_g", "kv_a_norm_g", "mla_q_norm_g", "mla_k_norm_g",
         "dil_q_norm_g", "dil_k_norm_g")
WEIGHTS = ("norm_g", "w_in", "b_gate", "conv_w", "conv_b", "q_a_norm_g", "w_uq", "kv_a_norm_g", "w_ukv",
           "mla_q_norm_g", "mla_k_norm_g", "dil_q_norm_g", "dil_k_norm_g", "w_out_a", "w_out_b", "w_out_c", "w_o")


def _dot(a, b):
    return jnp.dot(a, b, preferred_element_type=F32)


def _dot_nt(a, b):
    return lax.dot_general(a, b, (((1,), (1,)), ((), ())), preferred_element_type=F32)


def _dot_tn(a, b):
    return lax.dot_general(a, b, (((0,), (0,)), ((), ())), preferred_element_type=F32)


def _grid_step(grid):
    step = pl.program_id(0)
    for a in range(1, len(grid)):
        step = step * grid[a] + pl.program_id(a)
    n = 1
    for g in grid:
        n *= g
    return step, n


def _write_windows(buf_ref, stages, sems, step, nsteps, puts):
    slot = step % 2
    for t, (v, dst) in enumerate(puts):
        cp = pltpu.make_async_copy(stages[t].at[slot], dst, sems.at[t, slot])

        @pl.when(step >= 2)
        def _():
            cp.wait()

        stages[t][slot] = v.astype(stages[t].dtype).reshape(stages[t].shape[1:])
        cp.start()

    @pl.when(step == nsteps - 1)
    def _():
        for t, (v, dst) in enumerate(puts):
            pltpu.make_async_copy(stages[t].at[slot], dst, sems.at[t, slot]).wait()
            if nsteps > 1:
                pltpu.make_async_copy(stages[t].at[1 - slot], dst, sems.at[t, 1 - slot]).wait()


def _pcall(name, fn, grid, ins, outs, into=None):
    n_in = len(ins)
    n_out = len(outs)
    acc_axis = len(grid) - 1
    is_acc = [len(o) > 4 and o[4] for o in outs]
    outs = [o[:4] for o in outs]
    targets = into[1] if into is not None else []
    n_t = len(targets)

    def body(*refs):
        vals = fn(*[r[...].astype(F32) for r in refs[:n_in]])
        if not isinstance(vals, (tuple, list)):
            vals = (vals,)
        o0 = n_in + (1 if n_t else 0)
        for k in range(n_out):
            r = refs[o0 + k]
            v = vals[k].astype(r.dtype).reshape(r.shape)
            if is_acc[k]:
                first = pl.program_id(acc_axis) == 0

                @pl.when(first)
                def _():
                    r[...] = v

                @pl.when(jnp.logical_not(first))
                def _():
                    r[...] += v
            else:
                r[...] = v
        if n_t:
            buf_ref = refs[o0 + n_out]
            stages = refs[o0 + n_out + 1:o0 + n_out + 1 + n_t]
            ids = [pl.program_id(a) for a in range(len(grid))]
            step, nsteps = _grid_step(grid)
            _write_windows(buf_ref, stages, refs[-1], step, nsteps,
                           [(vals[n_out + t], targets[t][1](buf_ref, *ids)) for t in range(n_t)])

    in_specs = [pl.BlockSpec(bs, im) for _, bs, im in ins]
    out_specs = [pl.BlockSpec(bs, im) for _, _, bs, im in outs]
    out_shape = [jax.ShapeDtypeStruct(sh, dt) for sh, dt, _, _ in outs]
    args = [a for a, _, _ in ins]
    extra = {}
    if n_t:
        buf = into[0]
        in_specs.append(pl.BlockSpec(memory_space=pl.ANY))
        out_specs.append(pl.BlockSpec(memory_space=pl.ANY))
        out_shape.append(jax.ShapeDtypeStruct(buf.shape, buf.dtype))
        args.append(buf)
        extra = dict(input_output_aliases={n_in: n_out},
                     scratch_shapes=[pltpu.VMEM((2,) + tuple(bs), buf.dtype) for bs, _ in targets]
                     + [pltpu.SemaphoreType.DMA((n_t, 2))])
    return pl.pallas_call(
        body,
        grid=grid,
        in_specs=in_specs,
        out_specs=out_specs,
        out_shape=out_shape,
        name=name,
        compiler_params=pltpu.CompilerParams(
            dimension_semantics=("arbitrary",) * len(grid), vmem_limit_bytes=VMEM_LIMIT),
        **extra,
    )(*args)


def _mm(name, a, b, *, ta=False, tb=False, out_dtype=F32, add=None, dep=None, b_words=False, tm=2048, tn=1024, tk=1024):
    if ta:
        K, M = a.shape
    else:
        M, K = a.shape
    bshape = (b.shape[0], 2 * b.shape[1]) if b_words else b.shape
    if tb:
        N, K2 = bshape
    else:
        K2, N = bshape
    assert K == K2, (name, a.shape, b.shape)
    tm, tn, tk = min(tm, M), min(tn, N), min(tk, K)
    assert M % tm == 0 and N % tn == 0 and K % tk == 0, (name, M, N, K)
    nk = K // tk
    dims = (((0 if ta else 1,), (1 if tb else 0,)), ((), ()))
    a_spec = pl.BlockSpec((tk, tm), lambda j, i, k: (k, i)) if ta else pl.BlockSpec((tm, tk), lambda j, i, k: (i, k))
    bw = 2 if b_words else 1
    assert not b_words or (tk if tb else tn) == bshape[1]
    b_spec = (pl.BlockSpec((tn, tk // bw), lambda j, i, k: (j, k)) if tb
              else pl.BlockSpec((tk, tn // bw), lambda j, i, k: (k, j)))
    o_spec = pl.BlockSpec((tm, tn), lambda j, i, k: (i, j))
    has_add = add is not None
    n_in = 2 + has_add + (dep is not None)

    def body(*refs):
        a_ref, b_ref = refs[0], refs[1]
        add_ref = refs[2] if has_add else None
        o_ref = refs[n_in]
        bb = b_ref[...]
        if b_words:
            lo, hi = _unpack_words(bb)
            first = (pl.program_id(0) * tn) if tb else (pl.program_id(2) * tk)
            r = first + lax.broadcasted_iota(jnp.int32, lo.shape, 0)
            pad = jnp.logical_and(r >= KPE_END, r < KPE_END + NINP - NIN)
            bb = jnp.concatenate([jnp.where(pad, 0.0, lo), jnp.where(pad, 0.0, hi)], axis=1)
        p = lax.dot_general(a_ref[...].astype(BF16), bb.astype(BF16), dims, preferred_element_type=F32)
        if nk == 1:
            if has_add:
                p = p + add_ref[...]
            o_ref[...] = p.astype(out_dtype)
        else:
            acc = refs[-1]
            k = pl.program_id(2)

            @pl.when(k == 0)
            def _():
                acc[...] = p

            @pl.when(k > 0)
            def _():
                acc[...] += p

            @pl.when(k == nk - 1)
            def _():
                r = acc[...]
                if has_add:
                    r = r + add_ref[...]
                o_ref[...] = r.astype(out_dtype)

    in_specs = [a_spec, b_spec] + ([o_spec] if has_add else []) + ([pl.BlockSpec(memory_space=pl.ANY)] if dep is not None else [])
    args = [a, b] + ([add] if has_add else []) + ([dep] if dep is not None else [])
    return pl.pallas_call(
        body,
        grid=(N // tn, M // tm, nk),
        in_specs=in_specs,
        out_specs=o_spec,
        out_shape=jax.ShapeDtypeStruct((M, N), out_dtype),
        scratch_shapes=[pltpu.VMEM((tm, tn), F32)] if nk > 1 else [],
        name=name,
        compiler_params=pltpu.CompilerParams(
            dimension_semantics=("arbitrary", "arbitrary", "arbitrary"), vmem_limit_bytes=VMEM_LIMIT),
    )(*args)


def _vjp_of(f, n_diff):
    def g(*args, n_prim):
        prim = args[:n_diff]
        consts = args[n_diff:n_prim]
        cts = args[n_prim:]
        _, pull = jax.vjp(lambda *p: f(*p, *consts), *prim)
        out = jax.eval_shape(lambda *p: f(*p, *consts), *prim)
        if isinstance(out, (tuple, list)):
            cts = tuple(c.astype(o.dtype) for c, o in zip(cts, out))
        else:
            cts = cts[0].astype(out.dtype)
        return pull(cts)
    return g


def _rms(x, g, n=None):
    n = x.shape[-1] if n is None else n
    ms = jnp.sum(x * x, axis=-1, keepdims=True) / n
    return x * lax.rsqrt(ms + EPS) * g


def _silu(z):
    return z * jax.nn.sigmoid(z)


def _roll_rows(u, k):
    n = u.shape[0]
    r = pltpu.roll(u, k % n, 0)
    t = lax.broadcasted_iota(jnp.int32, u.shape, 0)
    if k > 0:
        return jnp.where(t >= k, r, 0.0)
    return jnp.where(t < n + k, r, 0.0)


@functools.partial(jax.custom_vjp, nondiff_argnums=(1,))
def _shift(u, k):
    return _roll_rows(u, k)


def _shift_fwd(u, k):
    return _roll_rows(u, k), None


def _shift_bwd(k, _, g):
    return (_roll_rows(g, -k),)


_shift.defvjp(_shift_fwd, _shift_bwd)


@functools.partial(jax.custom_vjp, nondiff_argnums=(1,))
def _lane_roll(u, k):
    return pltpu.roll(u, k % LANE, 1)


def _lane_roll_fwd(u, k):
    return pltpu.roll(u, k % LANE, 1), None


def _lane_roll_bwd(k, _, g):
    return (pltpu.roll(g, (-k) % LANE, 1),)


_lane_roll.defvjp(_lane_roll_fwd, _lane_roll_bwd)


def _conv_math(ab, ac, ax, az, cw, cb):
    u = ac * ax
    conv = cb + _shift(u, 2) * cw[0:1] + _shift(u, 1) * cw[1:2] + u * cw[2:3]
    return ab * conv * _silu(az)


def _mla_pre_math(cq, ckv, gq, gkv):
    return _rms(cq, gq), _rms(ckv, gkv)


def _rope_math(q, kn, kpe, gq, gk, c, s1, s2):
    lane = lax.broadcasted_iota(jnp.int32, kpe.shape, 1)
    pe = _lane_roll(jnp.where(lane < ROPE, kpe, 0.0), NOPE)

    def one(t, g):
        tn = _rms(t, g, QK)
        return tn * c + _lane_roll(tn, -16) * s1 + _lane_roll(tn, 16) * s2

    qs, ks = [], []
    for h in range(NH):
        sl = slice(h * QKP, (h + 1) * QKP)
        qs.append(one(q[:, sl], gq))
        ks.append(one(kn[:, sl] + pe, gk))
    return jnp.concatenate(qs, axis=1), jnp.concatenate(ks, axis=1)


def _gate_math(o, z):
    return o * _silu(z)


def _mergec_math(o0, o1, o2, l0, l1, l2, cz):
    m = lax.stop_gradient(jnp.maximum(jnp.maximum(l0, l1), l2))
    e0, e1, e2 = jnp.exp(l0 - m), jnp.exp(l1 - m), jnp.exp(l2 - m)
    den = e0 + e1 + e2
    oc = (e0 / den) * o0 + (e1 / den) * o1 + (e2 / den) * o2
    return oc * _silu(cz)


def _merge_math(g0, g1, g2, b0, b1, b2, pa, pb, pc):
    return (jax.nn.sigmoid(g0 + b0) * pa + jax.nn.sigmoid(g1 + b1) * pb) + jax.nn.sigmoid(g2 + b2) * pc


MLA_T = 256
MLA_UNROLL = True


def _mla_fwd(q, k, v):
    B = q.shape[0]
    T = MLA_T
    NB = S // T

    def body(q_ref, k_ref, v_ref, o_ref, l_ref):
        row = lax.broadcasted_iota(jnp.int32, (T, T), 0)
        col = lax.broadcasted_iota(jnp.int32, (T, T), 1)
        lo = _lo_mask((T, LANE))

        for qi in range(NB):
            qb = q_ref[qi * T:(qi + 1) * T, :]

            def step(j, carry, diagonal):
                m, l, acc = carry
                off = pl.multiple_of(j * T, T)
                kb = k_ref[pl.ds(off, T), :]
                vb = v_ref[pl.ds(off, T), :]
                ss = []
                for e in (0, 1):
                    se = _dot_nt(qb[:, e * QKP:(e + 1) * QKP], kb[:, e * QKP:(e + 1) * QKP]) * MLA_SCALE
                    ss.append(jnp.where(col <= row, se, NEG) if diagonal else se)
                s = jnp.concatenate(ss, axis=0)
                m_new = jnp.maximum(m, jnp.max(s, axis=-1, keepdims=True))
                a = jnp.exp(m - m_new)
                p = jnp.exp(s - m_new)
                l = a * l + jnp.sum(p, axis=-1, keepdims=True)
                acc = a * acc + _dot(p.astype(BF16), vb)
                return m_new, l, acc

            init = (jnp.full((2 * T, 1), NEG, F32), jnp.zeros((2 * T, 1), F32), jnp.zeros((2 * T, LANE), F32))
            carry = lax.fori_loop(0, qi, functools.partial(step, diagonal=False), init, unroll=MLA_UNROLL)
            m, l, acc = step(qi, carry, True)
            o = acc / l
            lse = m + jnp.log(l)
            o_ref[qi * T:(qi + 1) * T, :] = jnp.where(lo, o[:T], o[T:])
            l_ref[qi * T:(qi + 1) * T, :] = jnp.where(lo, lse[:T], lse[T:])

    def spec(w):
        return pl.BlockSpec((None, S, w), lambda b, hp: (b, 0, hp))

    return pl.pallas_call(
        body,
        grid=(B, NH // 2),
        in_specs=[spec(2 * QKP), spec(2 * QKP), spec(LANE)],
        out_specs=[spec(LANE), spec(LANE)],
        out_shape=[jax.ShapeDtypeStruct((B, S, NH * VD), F32)] * 2,
        name="mla_attn_fwd",
        compiler_params=pltpu.CompilerParams(dimension_semantics=("arbitrary",) * 2, vmem_limit_bytes=VMEM_LIMIT),
    )(q, k, v)


def _mla_bwd(q, k, v, do, o, lse):
    B = q.shape[0]
    T = MLA_T
    NB = S // T

    def body(q_ref, k_ref, v_ref, do_ref, o_ref, l_ref, dq_ref, dk_ref, dv_ref, delta_ref, dqt_ref):
        delta_ref[...] = _head_sum(do_ref[...] * o_ref[...])
        row = lax.broadcasted_iota(jnp.int32, (T, T), 0)
        col = lax.broadcasted_iota(jnp.int32, (T, T), 1)
        lo = _lo_mask((T, LANE))
        tn_t = (((0,), (1,)), ((), ()))

        for j in range(NB):
            krows = slice(j * T, (j + 1) * T)
            kb = k_ref[krows, :]
            vb = v_ref[krows, :]
            dkt = [jnp.zeros((QKP, T), F32), jnp.zeros((QKP, T), F32)]
            dvt = jnp.zeros((LANE, T), F32)
            for i in range(j, NB):
                qrows = slice(i * T, (i + 1) * T)
                qb = q_ref[qrows, :]
                do2 = _stack_heads(do_ref[qrows, :], lo).astype(BF16)
                lb = l_ref[qrows, :]
                db = delta_ref[qrows, :]
                dp2 = _dot_nt(do2, vb)
                ps = []
                for e in (0, 1):
                    cols = slice(e * QKP, (e + 1) * QKP)
                    qe, ke = qb[:, cols], kb[:, cols]
                    s = _dot_nt(qe, ke) * MLA_SCALE
                    if i == j:
                        s = jnp.where(col <= row, s, NEG)
                    p = jnp.exp(s - lb[:, e * HD:e * HD + 1])
                    ps.append(p.astype(BF16))
                    ds = (p * (dp2[e * T:(e + 1) * T] - db[:, e * HD:e * HD + 1]) * MLA_SCALE).astype(BF16)
                    dkt[e] = dkt[e] + _dot_tn(qe, ds)
                    dq_t = lax.dot_general(ke, ds, tn_t, preferred_element_type=F32)
                    if j == 0:
                        dqt_ref[e, :, qrows] = dq_t
                    else:
                        dqt_ref[e, :, qrows] += dq_t
                dvt = dvt + _dot_tn(do2, jnp.concatenate(ps, axis=0))
            dk_ref[krows, 0:QKP] = dkt[0].T
            dk_ref[krows, QKP:2 * QKP] = dkt[1].T
            dv_ref[krows, :] = dvt.T
        dq_ref[:, 0:QKP] = dqt_ref[0].T
        dq_ref[:, QKP:2 * QKP] = dqt_ref[1].T

    def spec(w):
        return pl.BlockSpec((None, S, w), lambda b, hp: (b, 0, hp))

    return pl.pallas_call(
        body,
        grid=(B, NH // 2),
        in_specs=[spec(2 * QKP), spec(2 * QKP), spec(LANE), spec(LANE), spec(LANE), spec(LANE)],
        out_specs=[spec(2 * QKP), spec(2 * QKP), spec(LANE)],
        out_shape=[jax.ShapeDtypeStruct((B, S, NH * QKP), F32), jax.ShapeDtypeStruct((B, S, NH * QKP), F32),
                   jax.ShapeDtypeStruct((B, S, NH * VD), F32)],
        scratch_shapes=[pltpu.VMEM((S, LANE), F32), pltpu.VMEM((2, QKP, S), F32)],
        name="mla_attn_bwd",
        compiler_params=pltpu.CompilerParams(dimension_semantics=("arbitrary",) * 2, vmem_limit_bytes=VMEM_LIMIT),
    )(q, k, v, do, o, lse)


def _lo_mask(shape):
    return lax.broadcasted_iota(jnp.int32, shape, len(shape) - 1) < HD


def _head_sum(u):
    r = lax.broadcasted_iota(jnp.int32, (LANE, LANE), 0) < HD
    c = lax.broadcasted_iota(jnp.int32, (LANE, LANE), 1) < HD
    ones = jnp.where(r == c, 1.0, 0.0).astype(BF16)
    hi = u.astype(BF16)
    lo = (u - hi.astype(F32)).astype(BF16)
    return _dot(hi, ones) + _dot(lo, ones)


def _head_sum_1(u):
    r = lax.broadcasted_iota(jnp.int32, (LANE, LANE), 0) < HD
    c = lax.broadcasted_iota(jnp.int32, (LANE, LANE), 1) < HD
    return _dot(u.astype(BF16), jnp.where(r == c, 1.0, 0.0).astype(BF16))


def _rms2_scale(x):
    return lax.rsqrt(_head_sum(x * x) / HD + EPS)


def _rms2(x, g):
    return x * _rms2_scale(x) * g


def _rms2_bwd(x, r, g, dy):
    xn = x * r
    t = dy * g
    dx = r * (t - xn * (_head_sum_1(xn * t) * (1.0 / HD)))
    return dx, jnp.sum(dy * xn, axis=0, keepdims=True)


def _dil_bias(t_ref, gi, d):
    qq = lax.broadcasted_iota(jnp.int32, (QB, QB), 0)
    kk = lax.broadcasted_iota(jnp.int32, (QB, QB), 1)
    jc = (qq - kk).astype(F32)
    rows = []
    for e in (0, 1):
        sl = t_ref[2 * gi + e:2 * gi + e + 1, :] * float(d)
        bp = jnp.where(kk >= qq, -sl * (jc + float(QB)), NEG)
        bc = jnp.where(kk <= qq, -sl * jc, NEG)
        rows.append(jnp.concatenate([bp, bc], axis=1))
    return jnp.concatenate(rows, axis=0)


def _dil_rows(cur, d):
    return pl.ds(cur, QB, stride=d) if d > 1 else pl.ds(pl.multiple_of(cur, QB), QB)


def _dil_walk(d, block, full):
    if d == 1:
        block(0, None)

        def body(i, c):
            block(i * QB, (i - 1) * QB)
            return c
        lax.fori_loop(1, S // QB, body, 0, unroll=True if full else 5)
    elif d == 16:
        def body(r, c):
            block(r, None)
            return c
        lax.fori_loop(0, d, body, 0, unroll=True if full else 4)
    else:
        nb = S // d // QB

        def cls(r, c):
            block(r, None)

            def body(i, c2):
                block(r + i * QB * d, r + (i - 1) * QB * d)
                return c2
            lax.fori_loop(1, nb, body, 0, unroll=True)
            return c
        lax.fori_loop(0, d, cls, 0, unroll=full)


def _stack_heads(x, lo):
    return jnp.concatenate([jnp.where(lo, x, 0.0), jnp.where(lo, 0.0, x)], axis=0)


def _dilc_fwd(proj3, gq, gk, tab):
    B = proj3.shape[0]

    def body(q_ref, k_ref, v_ref, cz_ref, gq_ref, gk_ref, t_ref, y_ref, o_ref, l_ref, qs, ks, vs):
        g = pl.program_id(2)
        lo = _lo_mask((QB, LANE))

        def group(gi):
            d = DIL[gi][1]
            qs[...] = _rms2(q_ref[...].astype(F32), gq_ref[gi:gi + 1, :])
            ks[...] = _rms2(k_ref[...].astype(F32), gk_ref[gi:gi + 1, :])
            vs[...] = v_ref[...].astype(F32)
            bias = _dil_bias(t_ref, gi, d)

            def block(cur, prev):
                rows = _dil_rows(cur, d)
                q2 = _stack_heads(qs[rows, :], lo).astype(BF16)
                kc, vc = ks[rows, :], vs[rows, :]
                if prev is None:
                    kcat, vcat, b = kc, vc, bias[:, QB:]
                else:
                    prow = _dil_rows(prev, d)
                    kcat = jnp.concatenate([ks[prow, :], kc], axis=0)
                    vcat = jnp.concatenate([vs[prow, :], vc], axis=0)
                    b = bias
                s = _dot_nt(q2, kcat.astype(BF16)) * DIL_SCALE + b
                m = jnp.max(s, axis=-1, keepdims=True)
                p = jnp.exp(s - m)
                l = jnp.sum(p, axis=-1, keepdims=True)
                o = _dot(p.astype(BF16), vcat.astype(BF16)) / l
                lse = m + jnp.log(l)
                o_ref[gi, rows, :] = jnp.where(lo, o[:QB], o[QB:])
                l_ref[gi, rows, :] = jnp.where(lo, lse[:QB], lse[QB:])

            _dil_walk(d, block, True)

        for gi in range(NG):
            pl.when(g == gi)(functools.partial(group, gi))

        @pl.when(g == NG - 1)
        def _():
            y_ref[...] = _mergec_math(o_ref[0], o_ref[1], o_ref[2], l_ref[0], l_ref[1], l_ref[2],
                                      cz_ref[...].astype(F32)).astype(BF16)

    def col(base):
        return pl.BlockSpec((None, S, LANE), lambda b, hp, g: (b, 0, base // LANE + 4 * g + hp))

    gspec = pl.BlockSpec((NG, LANE), lambda b, hp, g: (0, 0))
    saved = pl.BlockSpec((NG, None, S, LANE), lambda b, hp, g: (0, b, 0, hp))
    return pl.pallas_call(
        body,
        grid=(B, 4, NG),
        in_specs=[col(O_DQ), col(O_DK), col(O_DV),
                  pl.BlockSpec((None, S, LANE), lambda b, hp, g: (b, 0, O_CZ // LANE + hp)),
                  gspec, gspec, pl.BlockSpec((None, 8, LANE), lambda b, hp, g: (hp, 0, 0))],
        out_specs=[pl.BlockSpec((None, S, LANE), lambda b, hp, g: (b, 0, hp)), saved, saved],
        out_shape=[jax.ShapeDtypeStruct((B, S, DWID), BF16), jax.ShapeDtypeStruct((NG, B, S, DWID), F32),
                   jax.ShapeDtypeStruct((NG, B, S, DWID), F32)],
        scratch_shapes=[pltpu.VMEM((S, LANE), F32)] * 3,
        name="dil_mixer_fwd",
        compiler_params=pltpu.CompilerParams(dimension_semantics=("arbitrary",) * 3, vmem_limit_bytes=VMEM_LIMIT),
    )(proj3, proj3, proj3, proj3, gq, gk, tab)


MERGE_ROWS = 256


def _dilc_bwd(proj3, gq, gk, tab, o_all, l_all, d_yc, dproj3):
    B = proj3.shape[0]

    def body(q_ref, k_ref, v_ref, cz_ref, gq_ref, gk_ref, t_ref, o_ref, l_ref, dy_ref, dp_in,
             dp_out, dgq_out, dgk_out, qs, ks, vs, dos, dls, dqs, dks, dvs, rqs, rks, dczs,
             st_q, st_k, st_v, st_z, sems, sem_z):
        b_, hp, g = pl.program_id(0), pl.program_id(1), pl.program_id(2)
        col = lambda base: pl.ds(pl.multiple_of(base + hp * LANE, LANE), LANE)
        lo = _lo_mask((QB, LANE))

        @pl.when(jnp.logical_and(jnp.logical_and(pl.program_id(0) == 0, pl.program_id(1) == 0), g == 0))
        def _():
            dgq_out[...] = jnp.zeros((NG, LANE), F32)
            dgk_out[...] = jnp.zeros((NG, LANE), F32)

        @pl.when(g == 0)
        def _():
            def chunk(i, carry):
                rows = pl.ds(pl.multiple_of(i * MERGE_ROWS, MERGE_ROWS), MERGE_ROWS)
                ls = [l_ref[j, rows, :] for j in range(NG)]
                m = jnp.maximum(jnp.maximum(ls[0], ls[1]), ls[2])
                es = [jnp.exp(t - m) for t in ls]
                den = (es[0] + es[1]) + es[2]
                al = [e / den for e in es]
                os_ = [o_ref[j, rows, :] for j in range(NG)]
                oc = (al[0] * os_[0] + al[1] * os_[1]) + al[2] * os_[2]
                cz = cz_ref[rows, :].astype(F32)
                sg = jax.nn.sigmoid(cz)
                dy = dy_ref[rows, :]
                d_oc = dy * (cz * sg)
                dczs[rows, :] = (dy * oc * (sg * (1.0 + cz * (1.0 - sg)))).astype(BF16)
                ts = [_head_sum_1(d_oc * os_[j]) for j in range(NG)]
                tbar = (al[0] * ts[0] + al[1] * ts[1]) + al[2] * ts[2]
                for j in range(NG):
                    dos[j, rows, :] = al[j] * d_oc
                    dls[j, rows, :] = al[j] * (ts[j] - tbar)
                return carry
            lax.fori_loop(0, S // MERGE_ROWS, chunk, 0)
            _write_windows(dp_out, [st_z], sem_z, b_ * 4 + hp, B * 4, [(dczs[...], dp_out.at[b_, :, col(O_CZ)])])

        def group(gi):
            d = DIL[gi][1]
            xq, xk = q_ref[...].astype(F32), k_ref[...].astype(F32)
            rqs[...] = _rms2_scale(xq)
            rks[...] = _rms2_scale(xk)
            qs[...] = xq * rqs[...] * gq_ref[gi:gi + 1, :]
            ks[...] = xk * rks[...] * gk_ref[gi:gi + 1, :]
            vs[...] = v_ref[...].astype(F32)
            dks[...] = jnp.zeros((S, LANE), F32)
            dvs[...] = jnp.zeros((S, LANE), F32)
            bias = _dil_bias(t_ref, gi, d)

            def block(cur, prev):
                rows = _dil_rows(cur, d)
                q2 = _stack_heads(qs[rows, :], lo).astype(BF16)
                dob = dos[gi, rows, :]
                do2 = _stack_heads(dob, lo).astype(BF16)
                kc, vc = ks[rows, :], vs[rows, :]
                if prev is None:
                    kcat, vcat, b = kc, vc, bias[:, QB:]
                else:
                    prow = _dil_rows(prev, d)
                    kcat = jnp.concatenate([ks[prow, :], kc], axis=0)
                    vcat = jnp.concatenate([vs[prow, :], vc], axis=0)
                    b = bias
                kcat = kcat.astype(BF16)
                vcat = vcat.astype(BF16)
                lse_b = l_ref[gi, rows, :]
                corr_b = dls[gi, rows, :] - _head_sum_1(dob * o_ref[gi, rows, :])
                lse2 = jnp.concatenate([lse_b[:, 0:1], lse_b[:, HD:HD + 1]], axis=0)
                corr2 = jnp.concatenate([corr_b[:, 0:1], corr_b[:, HD:HD + 1]], axis=0)
                s = _dot_nt(q2, kcat) * DIL_SCALE + b
                p = jnp.exp(s - lse2)
                ds = (p * (_dot_nt(do2, vcat) + corr2) * DIL_SCALE).astype(BF16)
                dq2 = _dot(ds, kcat)
                dqs[rows, :] = jnp.where(lo, dq2[:QB], dq2[QB:])
                dk = _dot_tn(ds, q2)
                dv = _dot_tn(p.astype(BF16), do2)
                if prev is None:
                    dks[rows, :] += dk
                    dvs[rows, :] += dv
                else:
                    dks[prow, :] += dk[:QB]
                    dvs[prow, :] += dv[:QB]
                    dks[rows, :] += dk[QB:]
                    dvs[rows, :] += dv[QB:]

            _dil_walk(d, block, False)

            dxq, dgq = _rms2_bwd(q_ref[...].astype(F32), rqs[...], gq_ref[gi:gi + 1, :], dqs[...])
            dgq_out[gi:gi + 1, :] += dgq
            dxk, dgk = _rms2_bwd(k_ref[...].astype(F32), rks[...], gk_ref[gi:gi + 1, :], dks[...])
            dgk_out[gi:gi + 1, :] += dgk
            step, nsteps = _grid_step((B, 4, NG))
            _write_windows(dp_out, [st_q, st_k, st_v], sems, step, nsteps,
                           [(dxq, dp_out.at[b_, :, col(O_DQ + gi * DWID)]), (dxk, dp_out.at[b_, :, col(O_DK + gi * DWID)]),
                            (dvs[...], dp_out.at[b_, :, col(O_DV + gi * DWID)])])

        for gi in range(NG):
            pl.when(g == gi)(functools.partial(group, gi))

    def col(base):
        return pl.BlockSpec((None, S, LANE), lambda b, hp, g: (b, 0, base // LANE + 4 * g + hp))

    gspec = pl.BlockSpec((NG, LANE), lambda b, hp, g: (0, 0))
    saved = pl.BlockSpec((NG, None, S, LANE), lambda b, hp, g: (0, b, 0, hp))
    per_pair = pl.BlockSpec((None, S, LANE), lambda b, hp, g: (b, 0, hp))
    return pl.pallas_call(
        body,
        grid=(B, 4, NG),
        in_specs=[col(O_DQ), col(O_DK), col(O_DV),
                  pl.BlockSpec((None, S, LANE), lambda b, hp, g: (b, 0, O_CZ // LANE + hp)),
                  gspec, gspec, pl.BlockSpec((None, 8, LANE), lambda b, hp, g: (hp, 0, 0)),
                  saved, saved, per_pair, pl.BlockSpec(memory_space=pl.ANY)],
        out_specs=[pl.BlockSpec(memory_space=pl.ANY), gspec, gspec],
        out_shape=[jax.ShapeDtypeStruct(dproj3.shape, dproj3.dtype), jax.ShapeDtypeStruct((NG, LANE), F32),
                   jax.ShapeDtypeStruct((NG, LANE), F32)],
        input_output_aliases={10: 0},
        scratch_shapes=[pltpu.VMEM((S, LANE), F32)] * 3 + [pltpu.VMEM((NG, S, LANE), F32)] * 2
        + [pltpu.VMEM((S, LANE), F32)] * 5 + [pltpu.VMEM((S, LANE), BF16)] + [pltpu.VMEM((2, S, LANE), BF16)] * 4
        + [pltpu.SemaphoreType.DMA((3, 2)), pltpu.SemaphoreType.DMA((1, 2))],
        name="dil_mixer_bwd",
        compiler_params=pltpu.CompilerParams(dimension_semantics=("arbitrary",) * 3, vmem_limit_bytes=VMEM_LIMIT),
    )(proj3, proj3, proj3, proj3, gq, gk, tab, o_all, l_all, d_yc, dproj3)


def _dil_slopes():
    slopes = (2.0 ** (-8.0 * np.arange(1, NG * DH + 1, dtype=np.float32) / (NG * DH))).astype(np.float32).reshape(NG, DH)
    tab = np.zeros((4, 8, LANE), np.float32)
    for hp in range(4):
        for gi in range(NG):
            for e in (0, 1):
                tab[hp, 2 * gi + e, :] = slopes[gi, 2 * hp + e]
    return jnp.asarray(tab)


def _rope_tables():
    inv = ROPE_THETA ** (-jnp.arange(0, ROPE, 2, dtype=F32) / ROPE)
    ang = jnp.arange(S, dtype=F32)[:, None] * inv[None, :]
    cos, sin = jnp.cos(ang), jnp.sin(ang)
    z16 = jnp.zeros((S, 16), F32)
    c = jnp.concatenate([jnp.ones((S, NOPE), F32), cos, cos, jnp.zeros((S, 32), F32)], axis=1)
    s1 = jnp.concatenate([jnp.zeros((S, NOPE), F32), -sin, z16, jnp.zeros((S, 32), F32)], axis=1)
    s2 = jnp.concatenate([jnp.zeros((S, NOPE), F32), z16, sin, jnp.zeros((S, 32), F32)], axis=1)
    return c, s1, s2


def _pad_heads_uq(w):
    return jnp.pad(w.reshape(QL, NH, QK), ((0, 0), (0, 0), (0, QKP - QK))).reshape(QL, NH * QKP)


def _unpad_heads_uq(g):
    return g.reshape(QL, NH, QKP)[:, :, :QK].reshape(QL, NH * QK)


def _split_ukv(w):
    w3 = w.reshape(KVL, NH, NOPE + VD)
    uk = jnp.pad(w3[:, :, :NOPE], ((0, 0), (0, 0), (0, QKP - NOPE))).reshape(KVL, NH * QKP)
    return uk, w3[:, :, NOPE:].reshape(KVL, NH * VD)


def _join_ukv(guk, guv):
    return jnp.concatenate([guk.reshape(KVL, NH, QKP)[:, :, :NOPE], guv.reshape(KVL, NH, VD)],
                           axis=-1).reshape(KVL, NH * (NOPE + VD))


BR = 512
BRM = 256


def _layer_fwd(x, w, tabs, batch, rest=None):
    T = batch * S
    rope_c, rope_s1, rope_s2, dil_tab = tabs
    res = {"x": x}
    row = lambda c: (lambda i: (i, c))
    fix = lambda i: (0, 0)

    h = _pcall("norm_fwd", _rms, (T // BR,),
               [(x, (BR, D), row(0)), (w["norm_g"], (1, D), fix)],
               [((T, D), BF16, (BR, D), row(0))])[0]
    proj = _mm("in_proj", h, w["w_in_t"], tb=True, out_dtype=BF16, b_words=True, tm=2048, tn=1024)
    res["h"], res["proj"] = h, proj
    proj3 = proj.reshape(batch, S, NINP)
    if rest is not None:
        w = rest(proj)

    cblk = lambda s: (lambda j, b: (b, 0, 4 * s + j))
    y_a = _pcall("conv_fwd", _conv_math, (4, batch),
                 [(proj3, (None, S, LANE), cblk(0)), (proj3, (None, S, LANE), cblk(1)),
                  (proj3, (None, S, LANE), cblk(2)), (proj3, (None, S, LANE), cblk(3)),
                  (w["conv_w"], (3, LANE), lambda j, b: (0, j)), (w["conv_b"], (1, LANE), lambda j, b: (0, j))],
                 [((batch, S, CW), BF16, (None, S, LANE), lambda j, b: (b, 0, j))])[0].reshape(T, CW)
    res["y_a"] = y_a

    cqn, ckvn = _pcall("mla_pre_fwd", _mla_pre_math, (T // BR,),
                       [(proj, (BR, QL), row(O_CQ // QL)), (proj, (BR, KVL), row(O_CKV // KVL)),
                        (w["q_a_norm_g"], (1, QL), fix), (w["kv_a_norm_g"], (1, KVL), fix)],
                       [((T, QL), BF16, (BR, QL), row(0)), ((T, KVL), BF16, (BR, KVL), row(0))])
    w_uq_p = _pad_heads_uq(w["w_uq"])
    w_uk, w_uv = _split_ukv(w["w_ukv"])
    q = _mm("uq", cqn, w_uq_p, out_dtype=BF16)
    kn = _mm("uk", ckvn, w_uk, out_dtype=BF16)
    v = _mm("uv", ckvn, w_uv, out_dtype=BF16)
    nrr = S // BR
    tab_row = lambda i: (i % nrr, 0)
    qr, kr = _pcall("rope_fwd", _rope_math, (T // BR,),
                    [(q, (BR, NH * QKP), row(0)), (kn, (BR, NH * QKP), row(0)), (proj, (BR, LANE), row(O_KPE // LANE)),
                     (w["mla_q_norm_g"], (1, QKP), fix), (w["mla_k_norm_g"], (1, QKP), fix),
                     (rope_c, (BR, QKP), tab_row), (rope_s1, (BR, QKP), tab_row), (rope_s2, (BR, QKP), tab_row)],
                    [((T, NH * QKP), BF16, (BR, NH * QKP), row(0))] * 2)
    qr = qr.reshape(batch, S, NH * QKP)
    kr = kr.reshape(batch, S, NH * QKP)
    v = v.reshape(batch, S, NH * VD)
    o_b, l_b = _mla_fwd(qr, kr, v)
    ob2 = o_b.reshape(T, NH * VD)
    y_b = _pcall("gateb_fwd", _gate_math, (T // BR,),
                 [(ob2, (BR, 512), row(0)), (proj, (BR, 512), row(O_BZ // 512))],
                 [((T, 512), BF16, (BR, 512), row(0))])[0]
    res.update(cqn=cqn, ckvn=ckvn, q=q, kn=kn, qr=qr, kr=kr, v=v, o_b=o_b, l_b=l_b, ob2=ob2, y_b=y_b,
               w_uq_p=w_uq_p, w_uk=w_uk, w_uv=w_uv)

    gq2 = jnp.tile(w["dil_q_norm_g"].reshape(NG, HD), (1, 2))
    gk2 = jnp.tile(w["dil_k_norm_g"].reshape(NG, HD), (1, 2))
    y_c, o_all, l_all = _dilc_fwd(proj3, gq2, gk2, dil_tab)
    y_c = y_c.reshape(T, DWID)
    res.update(o_all=o_all, l_all=l_all, y_c=y_c)

    pa = _mm("out_a", y_a, w["w_out_a"], out_dtype=BF16)
    pb = _mm("out_b", y_b, w["w_out_b"], out_dtype=BF16)
    pc = _mm("out_c", y_c, w["w_out_c"], out_dtype=BF16)
    merged = _pcall("merge_fwd", _merge_math, (T // BRM,),
                    [(proj, (BRM, D), row(O_G // D + s)) for s in range(3)]
                    + [(w["b_gate"], (1, D), (lambda s: (lambda i: (0, s)))(s)) for s in range(3)]
                    + [(t, (BRM, D), row(0)) for t in (pa, pb, pc)],
                    [((T, D), BF16, (BRM, D), row(0))])[0]
    out = _mm("o_proj", merged, w["w_o"], add=x, tm=1024)
    res.update(pa=pa, pb=pb, pc=pc, merged=merged)
    return out, res


def _norm_bwd_math(x, g, dh, dy):
    _, pull = jax.vjp(_rms, x, g)
    dx, dg = pull(dh)
    return dx + dy, dg


def _layer_bwd(dy, w, res, tabs, batch, after_dw=None, after_merge=None):
    T = batch * S
    rope_c, rope_s1, rope_s2, dil_tab = tabs
    row = lambda c: (lambda i: (i, c))
    fix = lambda i: (0, 0)
    x, proj, h = res["x"], res["proj"], res["h"]
    proj3 = proj.reshape(batch, S, NINP)
    g = {}

    d_merged = _mm("o_proj_dx", dy, w["w_o"], tb=True)
    g["w_o"] = _mm("o_proj_dw", res["merged"], dy, ta=True, tm=1024, tk=2048)

    dproj = lax.empty((T, NINP), BF16)
    rows_of = lambda br: (lambda ref, i: ref.at[pl.ds(pl.multiple_of(i * br, br), br)])

    def merge_bwd(*args):
        dg0, dg1, dg2, db0, db1, db2, dpa, dpb, dpc = _vjp_of(_merge_math, 9)(*args, n_prim=9)
        return db0, db1, db2, dpa, dpb, dpc, jnp.concatenate([dg0, dg1, dg2], axis=1)

    db0, db1, db2, dpa, dpb, dpc, dproj = _pcall(
        "merge_bwd", merge_bwd, (T // BRM,),
        [(proj, (BRM, D), row(O_G // D + s)) for s in range(3)]
        + [(w["b_gate"], (1, D), (lambda s: (lambda i: (0, s)))(s)) for s in range(3)]
        + [(t, (BRM, D), row(0)) for t in (res["pa"], res["pb"], res["pc"])]
        + [(d_merged, (BRM, D), row(0))],
        [((1, D), F32, (1, D), fix, True)] * 3 + [((T, D), BF16, (BRM, D), row(0))] * 3,
        into=(dproj, [((BRM, 3 * D), lambda ref, i: rows_of(BRM)(ref, i).at[:, O_G:O_G + 3 * D])]))
    g["b_gate"] = jnp.concatenate([db0, db1, db2], axis=1)

    dep = after_merge(dpa) if after_merge is not None else None
    d_ya = _mm("out_a_dx", dpa, w["w_out_a"], tb=True, dep=dep)
    d_yb = _mm("out_b_dx", dpb, w["w_out_b"], tb=True)
    d_yc = _mm("out_c_dx", dpc, w["w_out_c"], tb=True)
    g["w_out_a"] = _mm("out_a_dw", res["y_a"], dpa, ta=True, tk=T)
    g["w_out_b"] = _mm("out_b_dw", res["y_b"], dpb, ta=True, tk=T)
    g["w_out_c"] = _mm("out_c_dw", res["y_c"], dpc, ta=True, tk=T)

    cblk = lambda s: (lambda j, b: (b, 0, 4 * s + j))
    oblk = lambda j, b: (b, 0, j)
    def conv_bwd(*args):
        d_ab, d_ac, d_ax, d_az, dcw, dcb = _vjp_of(_conv_math, 6)(*args, n_prim=6)
        return dcw, dcb, d_ab, d_ac, d_ax, d_az

    a_col = lambda s_: (lambda ref, j, b: ref.at[b, :, pl.ds(pl.multiple_of(O_A + s_ * CW + j * LANE, LANE), LANE)])
    g["conv_w"], g["conv_b"], dproj3 = _pcall(
        "conv_bwd", conv_bwd, (4, batch),
        [(proj3, (None, S, LANE), cblk(s)) for s in range(4)]
        + [(w["conv_w"], (3, LANE), lambda j, b: (0, j)), (w["conv_b"], (1, LANE), lambda j, b: (0, j)),
           (d_ya.reshape(batch, S, CW), (None, S, LANE), oblk)],
        [((3, CW), F32, (3, LANE), lambda j, b: (0, j), True), ((1, CW), F32, (1, LANE), lambda j, b: (0, j), True)],
        into=(dproj.reshape(batch, S, NINP), [((S, LANE), a_col(s_)) for s_ in range(4)]))
    dproj = dproj3.reshape(T, NINP)

    gate_bwd = functools.partial(_vjp_of(_gate_math, 2), n_prim=2)
    d_ob, dproj = _pcall("gateb_bwd", gate_bwd, (T // BR,),
                         [(res["ob2"], (BR, 512), row(0)), (proj, (BR, 512), row(O_BZ // 512)), (d_yb, (BR, 512), row(0))],
                         [((T, 512), F32, (BR, 512), row(0))],
                         into=(dproj, [((BR, 512), lambda ref, i: rows_of(BR)(ref, i).at[:, O_BZ:O_BZ + 512])]))
    dqr, dkr, dv = _mla_bwd(res["qr"], res["kr"], res["v"], d_ob.reshape(batch, S, NH * VD), res["o_b"], res["l_b"])
    nrr = S // BR
    tab_row = lambda i: (i % nrr, 0)
    def rope_bwd(*args):
        d_q, d_kn, d_kpe, dgq, dgk = _vjp_of(_rope_math, 5)(*args, n_prim=8)
        return d_q, d_kn, dgq, dgk, d_kpe

    d_q, d_kn, g["mla_q_norm_g"], g["mla_k_norm_g"], dproj = _pcall(
        "rope_bwd", rope_bwd, (T // BR,),
        [(res["q"], (BR, NH * QKP), row(0)), (res["kn"], (BR, NH * QKP), row(0)), (proj, (BR, LANE), row(O_KPE // LANE)),
         (w["mla_q_norm_g"], (1, QKP), fix), (w["mla_k_norm_g"], (1, QKP), fix),
         (rope_c, (BR, QKP), tab_row), (rope_s1, (BR, QKP), tab_row), (rope_s2, (BR, QKP), tab_row),
         (dqr.reshape(T, NH * QKP), (BR, NH * QKP), row(0)), (dkr.reshape(T, NH * QKP), (BR, NH * QKP), row(0))],
        [((T, NH * QKP), BF16, (BR, NH * QKP), row(0))] * 2 + [((1, QKP), F32, (1, QKP), fix, True)] * 2,
        into=(dproj, [((BR, LANE), lambda ref, i: rows_of(BR)(ref, i).at[:, O_KPE:O_KPE + LANE])]))
    dv = dv.reshape(T, NH * VD)
    d_cqn = _mm("uq_dx", d_q, res["w_uq_p"], tb=True)
    d_ckvn = _mm("uk_dx", d_kn, res["w_uk"], tb=True)
    d_ckvn = _mm("uv_dx", dv, res["w_uv"], tb=True, add=d_ckvn)
    g["w_uq"] = _unpad_heads_uq(_mm("uq_dw", res["cqn"], d_q, ta=True, tk=T))
    g["w_ukv"] = _join_ukv(_mm("uk_dw", res["ckvn"], d_kn, ta=True, tk=T),
                           _mm("uv_dw", res["ckvn"], dv, ta=True, tk=T))
    def pre_bwd(*args):
        d_cq, d_ckv, dgq, dgkv = _vjp_of(_mla_pre_math, 4)(*args, n_prim=4)
        return dgq, dgkv, jnp.concatenate([d_cq, d_ckv], axis=1)

    g["q_a_norm_g"], g["kv_a_norm_g"], dproj = _pcall(
        "mla_pre_bwd", pre_bwd, (T // BR,),
        [(proj, (BR, QL), row(O_CQ // QL)), (proj, (BR, KVL), row(O_CKV // KVL)),
         (w["q_a_norm_g"], (1, QL), fix), (w["kv_a_norm_g"], (1, KVL), fix),
         (d_cqn, (BR, QL), row(0)), (d_ckvn, (BR, KVL), row(0))],
        [((1, QL), F32, (1, QL), fix, True), ((1, KVL), F32, (1, KVL), fix, True)],
        into=(dproj, [((BR, QL + KVL), lambda ref, i: rows_of(BR)(ref, i).at[:, O_CQ:O_CQ + QL + KVL])]))

    gq2 = jnp.tile(w["dil_q_norm_g"].reshape(NG, HD), (1, 2))
    gk2 = jnp.tile(w["dil_k_norm_g"].reshape(NG, HD), (1, 2))
    dproj3, dgq, dgk = _dilc_bwd(proj3, gq2, gk2, dil_tab, res["o_all"], res["l_all"],
                                 d_yc.reshape(batch, S, DWID), dproj.reshape(batch, S, NINP))
    dproj = dproj3.reshape(T, NINP)
    g["dil_q_norm_g"] = dgq[:, :HD] + dgq[:, HD:]
    g["dil_k_norm_g"] = dgk[:, :HD] + dgk[:, HD:]

    g["w_in_t"] = _mm("in_proj_dw", dproj, h, ta=True, tm=1024, tk=T)
    dep = after_dw(g) if after_dw is not None else None
    d_h = _mm("in_proj_dx", dproj, w["w_in_t"], dep=dep, b_words=True, tm=1024, tk=NINP // 4)
    dx, g["norm_g"] = _pcall("norm_bwd", _norm_bwd_math, (T // BR,),
                             [(x, (BR, D), row(0)), (w["norm_g"], (1, D), fix), (d_h, (BR, D), row(0)),
                              (dy, (BR, D), row(0))],
                             [((T, D), F32, (BR, D), row(0)), ((1, D), F32, (1, D), fix, True)])
    return dx, g


def _loss_math(y, t):
    e = y - t
    return e * (1.0 / D), 0.5 * jnp.sum(jnp.sum(e * e, axis=-1, keepdims=True) / D, axis=0, keepdims=True)


ANY = pl.BlockSpec(memory_space=pl.ANY)
U32 = jnp.uint32
WSH = NIN // 4
WA = KPE_END
WB = WSH - WA
CWD = 512
PACK_ROWS = 1472
HW = PACK_W // 2


def _me():
    return lax.axis_index("x"), lax.axis_index("y"), lax.axis_index("c")


def _piece_rows(k):
    a = k * WSH + jnp.where(k > 0, NINP - NIN, 0)
    b = k * WSH + WA + (NINP - NIN)
    return ((0, pl.multiple_of(a, 8), WA), (WA, pl.multiple_of(b, 8), WB))


def _pack_words(lo, hi):
    ul = lax.bitcast_convert_type(lo.astype(BF16).astype(F32), U32)
    uh = lax.bitcast_convert_type(hi.astype(BF16).astype(F32), U32)
    w = jnp.bitwise_or(jnp.bitwise_and(uh, jnp.uint32(0xFFFF0000)), jnp.right_shift(ul, jnp.uint32(16)))
    return lax.bitcast_convert_type(w, F32)


def _unpack_words(w):
    w = lax.bitcast_convert_type(w, U32)
    lo = lax.bitcast_convert_type(jnp.left_shift(w, jnp.uint32(16)), F32)
    hi = lax.bitcast_convert_type(jnp.bitwise_and(w, jnp.uint32(0xFFFF0000)), F32)
    return lo, hi


def _all_gather(wc):
    def body(w_ref, ow_ref, send_sems, recv_sems):
        x, y, c = _me()
        k_me = 2 * x + y
        sib = (x, y, 1 - c)
        chips = [(1 - x, y), (x, 1 - y), (1 - x, 1 - y)]
        wcols = lambda cc: pl.ds(pl.multiple_of(cc * (CWD // 2), LANE), CWD // 2)

        def windows(k, cc):
            return [(w_ref.at[pl.ds(l0, n), wcols(cc)], ow_ref.at[pl.ds(p0, n), wcols(cc)])
                    for l0, p0, n in _piece_rows(k)]

        def copy(i, src, dst, to):
            return pltpu.make_async_remote_copy(src_ref=src, dst_ref=dst, send_sem=send_sems.at[i],
                                                recv_sem=recv_sems.at[i], device_id=to, device_id_type=MESH)

        def own_windows():
            return [(w_ref.at[pl.ds(l0, n)], ow_ref.at[pl.ds(p0, n)]) for l0, p0, n in _piece_rows(k_me)]

        first = [copy(12 + i, src, dst, sib) for i, (src, dst) in enumerate(own_windows())]
        for j, (cx, cy) in enumerate(chips):
            for i, (src, dst) in enumerate(windows(k_me, c)):
                first.append(copy(2 * j + i, src, dst, (cx, cy, c)))
        for cp in first:
            cp.start()
        passed = []
        for j, (cx, cy) in enumerate(chips):
            for i, (_, dst) in enumerate(windows(2 * cx + cy, c)):
                copy(2 * j + i, dst, dst, (cx, cy, c)).wait_recv()
                cp = copy(6 + 2 * j + i, dst, dst, sib)
                cp.start()
                passed.append(cp)
        for j, (cx, cy) in enumerate(chips):
            for i, (_, dst) in enumerate(windows(2 * cx + cy, 1 - c)):
                copy(6 + 2 * j + i, dst, dst, sib).wait_recv()
        for i, (_, dst) in enumerate(own_windows()):
            copy(12 + i, dst, dst, sib).wait_recv()
        for cp in first + passed:
            cp.wait_send()

    return pl.pallas_call(
        body,
        out_shape=jax.ShapeDtypeStruct((NINP, CWD), F32),
        in_specs=[ANY], out_specs=ANY,
        scratch_shapes=[pltpu.SemaphoreType.DMA((14,)), pltpu.SemaphoreType.DMA((14,))],
        name="weights_all_gather",
    )(wc)


HBM = pl.BlockSpec(memory_space=pltpu.HBM)
SEM = pl.BlockSpec(memory_space=pltpu.SEMAPHORE)
EFFECT = pltpu.SideEffectType.DATAFLOW_SIDE_EFFECTING


def _in_hbm(a):
    return pltpu.with_memory_space_constraint(a, pltpu.HBM)


def _ag_shard(w_ref, s_ref, lw_ref, ls_ref, k, with_w):
    pack = [(s_ref, ls_ref.at[k])]
    if not with_w:
        return pack
    return [(w_ref.at[pl.ds(l0, n)], lw_ref.at[pl.ds(p0, n)]) for l0, p0, n in _piece_rows(k)] + pack


def _ag_behind_copies(w_ref, s_ref, lw_ref, ls_ref, send_sems, recv_sems, with_w):
    x, y, c = _me()
    peers = [(1 - x, y, c), (x, 1 - y, c), (1 - x, 1 - y, c), (x, y, 1 - c)]
    mine, theirs = [], []
    for j, (px, py, pc) in enumerate(peers):
        for i, ((src, dst), (_, got)) in enumerate(zip(_ag_shard(w_ref, s_ref, lw_ref, ls_ref, 2 * x + y, with_w),
                                                       _ag_shard(w_ref, s_ref, lw_ref, ls_ref, 2 * px + py, with_w))):
            mk = lambda s_, d_: pltpu.make_async_remote_copy(
                src_ref=s_, dst_ref=d_, send_sem=send_sems.at[3 * j + i], recv_sem=recv_sems.at[3 * j + i],
                device_id=(px, py, pc), device_id_type=MESH)
            mine.append(mk(src, dst))
            theirs.append(mk(got, got))
    return mine, theirs


def _ag_behind_start(wc, sp, dep, tag):
    with_w = wc is not None
    if not with_w:
        wc = jnp.zeros((8, LANE), F32)
    lw = lax.empty((NINP, CWD) if with_w else (8, LANE), F32)

    def body(w_ref, s_ref, lw_ref, ls_ref, dep_ref, send_sems, recv_sems, w_thru, s_thru, lw_thru, ls_thru, token):
        mine, _ = _ag_behind_copies(w_ref, s_ref, lw_ref, ls_ref, send_sems, recv_sems, with_w)
        for cp in mine:
            cp.start()
        token[...] = jnp.zeros_like(token)

    return pl.pallas_call(
        body, name="weights_gather_start_" + tag,
        out_shape=(pltpu.SemaphoreType.DMA((12,)), pltpu.SemaphoreType.DMA((12,)), pltpu.HBM(wc.shape, wc.dtype),
                   pltpu.HBM(sp.shape, sp.dtype), pltpu.HBM(lw.shape, F32), pltpu.HBM((4, PACK_ROWS, PACK_W), BF16),
                   jax.ShapeDtypeStruct((8, LANE), F32)),
        in_specs=(HBM, HBM, HBM, HBM, ANY),
        out_specs=(SEM, SEM, HBM, HBM, HBM, HBM, pl.BlockSpec(memory_space=pltpu.VMEM)),
        input_output_aliases={0: 2, 1: 3, 2: 4, 3: 5},
        compiler_params=pltpu.CompilerParams(has_side_effects=EFFECT),
    )(_in_hbm(wc), _in_hbm(sp), _in_hbm(lw), _in_hbm(lax.empty((4, PACK_ROWS, PACK_W), BF16)), dep)


def _ag_behind_wait(send_sems, recv_sems, wc, sp, lw, ls, after, tag):
    with_w = lw.shape == (NINP, CWD)

    def body(w_ref, s_ref, lw_ref, ls_ref, send_sems, recv_sems, after_ref, w_dead, s_dead, lw_out, ls_out):
        mine, theirs = _ag_behind_copies(w_ref, s_ref, lw_ref, ls_ref, send_sems, recv_sems, with_w)
        for cp in mine:
            cp.wait_send()
        for cp in theirs:
            cp.wait_recv()

    out = pl.pallas_call(
        body, name="weights_gather_wait_" + tag,
        out_shape=(pltpu.HBM(wc.shape, wc.dtype), pltpu.HBM(sp.shape, sp.dtype), pltpu.HBM(lw.shape, lw.dtype),
                   pltpu.HBM(ls.shape, ls.dtype)),
        in_specs=(HBM, HBM, HBM, HBM, SEM, SEM, ANY), out_specs=(HBM, HBM, HBM, HBM),
        input_output_aliases={0: 0, 1: 1, 2: 2, 3: 3},
        compiler_params=pltpu.CompilerParams(has_side_effects=EFFECT),
    )(wc, sp, lw, ls, send_sems, recv_sems, after)
    return out[2], out[3]


def _rs_swap_copies(w_ref, s_ref, rw_ref, rs_ref, send_sems, recv_sems):
    x, y, c = _me()
    oc = 1 - c
    return [pltpu.make_async_remote_copy(src_ref=w_ref.at[:, pl.ds(pl.multiple_of(oc * (D // 2), LANE), D // 2)],
                                         dst_ref=rw_ref, send_sem=send_sems.at[0], recv_sem=recv_sems.at[0],
                                         device_id=(x, y, oc), device_id_type=MESH),
            pltpu.make_async_remote_copy(src_ref=s_ref.at[:, :, pl.ds(pl.multiple_of(oc * HW, LANE), HW)],
                                         dst_ref=rs_ref, send_sem=send_sems.at[1], recv_sem=recv_sems.at[1],
                                         device_id=(x, y, oc), device_id_type=MESH)]


def _rs_swap_start(gw, gs, tag):
    def body(w_ref, s_ref, rw_ref, rs_ref, send_sems, recv_sems, w_thru, s_thru, rw_thru, rs_thru, token):
        for cp in _rs_swap_copies(w_ref, s_ref, rw_ref, rs_ref, send_sems, recv_sems):
            cp.start()
        token[...] = jnp.zeros_like(token)

    return pl.pallas_call(
        body, name="grads_swap_start_" + tag,
        out_shape=(pltpu.SemaphoreType.DMA((2,)), pltpu.SemaphoreType.DMA((2,)), pltpu.HBM(gw.shape, gw.dtype),
                   pltpu.HBM(gs.shape, gs.dtype), pltpu.HBM((NINP, D // 2), F32), pltpu.HBM((4, PACK_ROWS, HW), F32),
                   jax.ShapeDtypeStruct((8, LANE), F32)),
        in_specs=(HBM, HBM, HBM, HBM),
        out_specs=(SEM, SEM, HBM, HBM, HBM, HBM, pl.BlockSpec(memory_space=pltpu.VMEM)),
        input_output_aliases={0: 2, 1: 3, 2: 4, 3: 5},
        compiler_params=pltpu.CompilerParams(has_side_effects=EFFECT),
    )(_in_hbm(gw), _in_hbm(gs), _in_hbm(lax.empty((NINP, D // 2), F32)), _in_hbm(lax.empty((4, PACK_ROWS, HW), F32)))


def _rs_swap_wait(send_sems, recv_sems, gw, gs, rw, rs, after, tag):
    def body(w_ref, s_ref, rw_ref, rs_ref, send_sems, recv_sems, after_ref, w_out, s_out, rw_out, rs_out):
        for cp in _rs_swap_copies(w_ref, s_ref, rw_ref, rs_ref, send_sems, recv_sems):
            cp.wait()

    return pl.pallas_call(
        body, name="grads_swap_wait_" + tag,
        out_shape=(pltpu.HBM(gw.shape, gw.dtype), pltpu.HBM(gs.shape, gs.dtype), pltpu.HBM(rw.shape, rw.dtype),
                   pltpu.HBM(rs.shape, rs.dtype)),
        in_specs=(HBM, HBM, HBM, HBM, SEM, SEM, ANY), out_specs=(HBM, HBM, HBM, HBM),
        input_output_aliases={0: 0, 1: 1, 2: 2, 3: 3},
        compiler_params=pltpu.CompilerParams(has_side_effects=EFFECT),
    )(gw, gs, rw, rs, send_sems, recv_sems, after)


SUM_BR = 1024
SUM_BUFS = 3


def _rs_chip_sum_w(gw, rw, cidx):
    steps = NINP // SUM_BR

    def body(c_ref, g_ref, r_ref, o_ref, gbuf, rbuf, obuf, in_sems, out_sems):
        gcol = pl.ds(pl.multiple_of(c_ref[0] * (D // 2), LANE), D // 2)
        rows = lambda i: pl.ds(i * SUM_BR, SUM_BR)

        def fetch(i):
            slot = i % SUM_BUFS
            return (pltpu.make_async_copy(g_ref.at[rows(i), gcol], gbuf.at[slot], in_sems.at[2 * slot]),
                    pltpu.make_async_copy(r_ref.at[rows(i)], rbuf.at[slot], in_sems.at[2 * slot + 1]))

        def put(i):
            return pltpu.make_async_copy(obuf.at[i % 2], o_ref.at[rows(i)], out_sems.at[i % 2])

        for i in range(SUM_BUFS):
            for cp in fetch(i):
                cp.start()
        q = D // 8
        for i in range(steps):
            for cp in fetch(i):
                cp.wait()
            if i >= 2:
                put(i - 2).wait()
            s = gbuf[i % SUM_BUFS] + rbuf[i % SUM_BUFS]
            obuf[i % 2] = jnp.concatenate([_pack_words(s[:, 0:q], s[:, q:2 * q]),
                                           _pack_words(s[:, 2 * q:3 * q], s[:, 3 * q:4 * q])], axis=1)
            put(i).start()
            if i + SUM_BUFS < steps:
                for cp in fetch(i + SUM_BUFS):
                    cp.start()
        put(steps - 2).wait()
        put(steps - 1).wait()

    return pl.pallas_call(
        body,
        grid_spec=pltpu.PrefetchScalarGridSpec(
            num_scalar_prefetch=1, grid=(1,), in_specs=[ANY, ANY], out_specs=ANY,
            scratch_shapes=[pltpu.VMEM((SUM_BUFS, SUM_BR, D // 2), F32), pltpu.VMEM((SUM_BUFS, SUM_BR, D // 2), F32),
                            pltpu.VMEM((2, SUM_BR, D // 4), F32), pltpu.SemaphoreType.DMA((2 * SUM_BUFS,)),
                            pltpu.SemaphoreType.DMA((2,))]),
        out_shape=jax.ShapeDtypeStruct((NINP, D // 4), F32),
        name="grads_chip_sum_w",
        compiler_params=pltpu.CompilerParams(dimension_semantics=("arbitrary",), vmem_limit_bytes=VMEM_LIMIT),
    )(cidx, gw, rw)


def _rs_chip_sum_s(gs, rs, cidx):
    def body(c_ref, g_ref, r_ref, o_ref):
        o_ref[...] = (g_ref[...] + r_ref[...]).astype(BF16)

    return pl.pallas_call(
        body,
        grid_spec=pltpu.PrefetchScalarGridSpec(
            num_scalar_prefetch=1, grid=(4,),
            in_specs=[pl.BlockSpec((None, PACK_ROWS, HW), lambda j, cr: (j, 0, cr[0])),
                      pl.BlockSpec((None, PACK_ROWS, HW), lambda j, cr: (j, 0, 0))],
            out_specs=pl.BlockSpec((None, PACK_ROWS, HW), lambda j, cr: (j, 0, 0))),
        out_shape=jax.ShapeDtypeStruct((4, PACK_ROWS, HW), BF16),
        name="grads_chip_sum_s",
        compiler_params=pltpu.CompilerParams(dimension_semantics=("arbitrary",), vmem_limit_bytes=VMEM_LIMIT),
    )(cidx, gs, rs)


def _rs_exchange_copies(sw_ref, ss_ref, r2w_ref, r2s_ref, send_sems, recv_sems):
    x, y, c = _me()
    mine, theirs = [], []
    for j, (cx, cy) in enumerate([(1 - x, y), (x, 1 - y), (1 - x, 1 - y)]):
        def mk(i, src, dst):
            return pltpu.make_async_remote_copy(src_ref=src, dst_ref=dst, send_sem=send_sems.at[3 * j + i],
                                                recv_sem=recv_sems.at[3 * j + i], device_id=(cx, cy, c), device_id_type=MESH)
        for i, (l0, p0, n) in enumerate(_piece_rows(2 * cx + cy)):
            mine.append(mk(i, sw_ref.at[pl.ds(p0, n)], r2w_ref.at[j, pl.ds(l0, n)]))
            theirs.append(mk(i, r2w_ref.at[j, pl.ds(l0, n)], r2w_ref.at[j, pl.ds(l0, n)]))
        mine.append(mk(2, ss_ref.at[2 * cx + cy], r2s_ref.at[j]))
        theirs.append(mk(2, r2s_ref.at[j], r2s_ref.at[j]))
    return mine, theirs


def _rs_exchange_start(sw, ss, tag):
    def body(sw_ref, ss_ref, r2w_ref, r2s_ref, send_sems, recv_sems, sw_thru, ss_thru, r2w_thru, r2s_thru, token):
        mine, _ = _rs_exchange_copies(sw_ref, ss_ref, r2w_ref, r2s_ref, send_sems, recv_sems)
        for cp in mine:
            cp.start()
        token[...] = jnp.zeros_like(token)

    return pl.pallas_call(
        body, name="grads_exchange_start_" + tag,
        out_shape=(pltpu.SemaphoreType.DMA((9,)), pltpu.SemaphoreType.DMA((9,)), pltpu.HBM(sw.shape, sw.dtype),
                   pltpu.HBM(ss.shape, ss.dtype), pltpu.HBM((3, WSH, D // 4), F32), pltpu.HBM((3, PACK_ROWS, HW), BF16),
                   jax.ShapeDtypeStruct((8, LANE), F32)),
        in_specs=(HBM, HBM, HBM, HBM),
        out_specs=(SEM, SEM, HBM, HBM, HBM, HBM, pl.BlockSpec(memory_space=pltpu.VMEM)),
        input_output_aliases={0: 2, 1: 3, 2: 4, 3: 5},
        compiler_params=pltpu.CompilerParams(has_side_effects=EFFECT),
    )(_in_hbm(sw), _in_hbm(ss), _in_hbm(lax.empty((3, WSH, D // 4), F32)), _in_hbm(lax.empty((3, PACK_ROWS, HW), BF16)))


def _rs_exchange_wait(send_sems, recv_sems, sw, ss, r2w, r2s, after, tag):
    def body(sw_ref, ss_ref, r2w_ref, r2s_ref, send_sems, recv_sems, after_ref, sw_dead, ss_dead, r2w_out, r2s_out):
        mine, theirs = _rs_exchange_copies(sw_ref, ss_ref, r2w_ref, r2s_ref, send_sems, recv_sems)
        for cp in mine:
            cp.wait_send()
        for cp in theirs:
            cp.wait_recv()

    out = pl.pallas_call(
        body, name="grads_exchange_wait_" + tag,
        out_shape=(pltpu.HBM(sw.shape, sw.dtype), pltpu.HBM(ss.shape, ss.dtype), pltpu.HBM(r2w.shape, r2w.dtype),
                   pltpu.HBM(r2s.shape, r2s.dtype)),
        in_specs=(HBM, HBM, HBM, HBM, SEM, SEM, ANY), out_specs=(HBM, HBM, HBM, HBM),
        input_output_aliases={0: 0, 1: 1, 2: 2, 3: 3},
        compiler_params=pltpu.CompilerParams(has_side_effects=EFFECT),
    )(sw, ss, r2w, r2s, send_sems, recv_sems, after)
    return out[2], out[3]


def _rs_final_w(gw, rw, r2w, idx, both, layer):
    q = D // 8

    def body(i_ref, g_ref, r_ref, p_ref, both_ref, o_ref, gbuf, rbuf, sems):
        i = pl.program_id(0)
        k, c = i_ref[0], i_ref[1]
        def fetch(s):
            cps = []
            for n_, (l0, p0, n) in enumerate(_piece_rows(k)):
                gcol = pl.ds(pl.multiple_of(c * (D // 2) + s * 2 * q, LANE), 2 * q)
                rcol = pl.ds(pl.multiple_of(s * 2 * q, LANE), 2 * q)
                cps.append(pltpu.make_async_copy(g_ref.at[pl.ds(p0, n), gcol], gbuf.at[s, pl.ds(l0, n)],
                                                 sems.at[4 * s + 2 * n_]))
                cps.append(pltpu.make_async_copy(r_ref.at[pl.ds(p0, n), rcol], rbuf.at[s, pl.ds(l0, n)],
                                                 sems.at[4 * s + 2 * n_ + 1]))
            return cps

        @pl.when(i == 0)
        def _():
            for cp in fetch(0) + fetch(1):
                cp.start()

        for cp in fetch(i):
            cp.wait()
        acc = gbuf[i] + rbuf[i]
        for j in range(3):
            lo, hi = _unpack_words(p_ref[j])
            acc = acc + jnp.concatenate([lo, hi], axis=1)
        o_ref[...] = acc

    return pl.pallas_call(
        body,
        grid_spec=pltpu.PrefetchScalarGridSpec(
            num_scalar_prefetch=1, grid=(2,),
            in_specs=[ANY, ANY, pl.BlockSpec((3, WSH, q), lambda i, ir: (0, 0, i)), ANY],
            out_specs=pl.BlockSpec((None, WSH, 2 * q), lambda i, ir: (layer, 0, 2 * ir[1] + i)),
            scratch_shapes=[pltpu.VMEM((2, WSH, 2 * q), F32), pltpu.VMEM((2, WSH, 2 * q), F32),
                            pltpu.SemaphoreType.DMA((8,))]),
        out_shape=jax.ShapeDtypeStruct((NL, WSH, D), F32),
        input_output_aliases={4: 0},
        name="grads_final_sum_w",
        compiler_params=pltpu.CompilerParams(dimension_semantics=("arbitrary",), vmem_limit_bytes=VMEM_LIMIT),
    )(idx, gw, rw, r2w, both)


def _rs_final_s(gs, rs, r2s, idx):
    def body(i_ref, g_ref, r_ref, p_ref, o_ref):
        acc = g_ref[...] + r_ref[...]
        for j in range(3):
            acc = acc + p_ref[j].astype(F32)
        o_ref[...] = acc

    return pl.pallas_call(
        body,
        grid_spec=pltpu.PrefetchScalarGridSpec(
            num_scalar_prefetch=1, grid=(1,),
            in_specs=[pl.BlockSpec((None, PACK_ROWS, HW), lambda i, ir: (ir[0], 0, ir[1])),
                      pl.BlockSpec((None, PACK_ROWS, HW), lambda i, ir: (ir[0], 0, 0)),
                      pl.BlockSpec((3, PACK_ROWS, HW), lambda i, ir: (0, 0, 0))],
            out_specs=pl.BlockSpec((PACK_ROWS, HW), lambda i, ir: (0, ir[1]))),
        out_shape=jax.ShapeDtypeStruct((PACK_ROWS, PACK_W), F32),
        name="grads_final_sum_s",
        compiler_params=pltpu.CompilerParams(dimension_semantics=("arbitrary",), vmem_limit_bytes=VMEM_LIMIT),
    )(idx, gs, rs, r2s)


def _rs_share(fw, fs, layer):
    def body(w_ref, s_ref, ow_ref, os_ref, send_sems, recv_sems):
        x, y, c = _me()
        wcol = lambda cc: pl.ds(pl.multiple_of(cc * (D // 2), LANE), D // 2)
        scol = lambda cc: pl.ds(pl.multiple_of(cc * HW, LANE), HW)

        def copies(cc):
            return [pltpu.make_async_remote_copy(src_ref=w_ref.at[layer, :, wcol(cc)],
                                                 dst_ref=ow_ref.at[layer, :, wcol(cc)],
                                                 send_sem=send_sems.at[0], recv_sem=recv_sems.at[0],
                                                 device_id=(x, y, 1 - c), device_id_type=MESH),
                    pltpu.make_async_remote_copy(src_ref=s_ref.at[:, scol(cc)], dst_ref=os_ref.at[:, scol(cc)],
                                                 send_sem=send_sems.at[1], recv_sem=recv_sems.at[1],
                                                 device_id=(x, y, 1 - c), device_id_type=MESH)]
        out = copies(c)
        for cp in out:
            cp.start()
        for cp in copies(1 - c):
            cp.wait_recv()
        for cp in out:
            cp.wait_send()

    return pl.pallas_call(
        body,
        out_shape=[jax.ShapeDtypeStruct(fw.shape, F32), jax.ShapeDtypeStruct(fs.shape, F32)],
        in_specs=[ANY, ANY], out_specs=[ANY, ANY],
        input_output_aliases={0: 0, 1: 1},
        scratch_shapes=[pltpu.SemaphoreType.DMA((2,)), pltpu.SemaphoreType.DMA((2,))],
        name="grads_share",
    )(fw, fs)


def _rs_sums(gw, gs, rw, rs):
    x, y, c = _me()
    cidx = jnp.reshape(c, (1,)).astype(jnp.int32)
    return dict(gw=gw, gs=gs, rw=rw, rs=rs, sw=_rs_chip_sum_w(gw, rw, cidx), ss=_rs_chip_sum_s(gs, rs, cidx))


def _rs_end(st, r2w, r2s, both, layer):
    x, y, c = _me()
    idx = jnp.stack([2 * x + y, c]).astype(jnp.int32)
    return _rs_share(_rs_final_w(st["gw"], st["rw"], r2w, idx, both, layer),
                     _rs_final_s(st["gs"], st["rs"], r2s, idx), layer)


def _all_reduce_small(gs):
    rows = gs.shape[0]

    def body(g_ref, o_ref, buf, send_sems, recv_sems):
        x, y, c = _me()
        me = 4 * x + 2 * y + c
        buf[me] = g_ref[...]
        cps = []
        for r in range(1, 8):
            fx, fy, fc = (r >> 2) & 1, (r >> 1) & 1, r & 1
            px, py, pc = jnp.bitwise_xor(x, fx), jnp.bitwise_xor(y, fy), jnp.bitwise_xor(c, fc)
            cps.append((pltpu.make_async_remote_copy(
                src_ref=g_ref, dst_ref=buf.at[me], send_sem=send_sems.at[r - 1], recv_sem=recv_sems.at[r - 1],
                device_id=(px, py, pc), device_id_type=MESH), 4 * px + 2 * py + pc))
        for cp, _ in cps:
            cp.start()
        for r, (cp, peer) in enumerate(cps):
            pltpu.make_async_remote_copy(
                src_ref=g_ref, dst_ref=buf.at[peer], send_sem=send_sems.at[r], recv_sem=recv_sems.at[r],
                device_id=(x, y, c), device_id_type=MESH).wait_recv()
        for cp, _ in cps:
            cp.wait_send()
        acc = buf[0]
        for k in range(1, 8):
            acc = acc + buf[k]
        o_ref[...] = acc

    return pl.pallas_call(
        body,
        out_shape=jax.ShapeDtypeStruct((rows, LANE), F32),
        in_specs=[pl.BlockSpec(memory_space=pltpu.VMEM)],
        out_specs=pl.BlockSpec(memory_space=pltpu.VMEM),
        scratch_shapes=[pltpu.VMEM((8, rows, LANE), F32), pltpu.SemaphoreType.DMA((7,)), pltpu.SemaphoreType.DMA((7,))],
        name="small_grads_all_reduce",
    )(gs)


PACK_SPLIT = (("w_uq", 96, (QL, 192)), ("w_ukv", 64, (KVL, 256)),
              ("w_out_a", 256, (CW, 256)), ("w_out_b", 256, (CW, 256)), ("w_out_c", 256, (CW, 256)),
              ("w_o", 512, (256, D)))
MAT_ROWS = 1440
CONV_SHARD = 3 * 128


def _w_in_words(w_in_shard):
    t = w_in_shard.T
    return _pack_words(t[:, :CWD], t[:, CWD:])


def _pack_weights(wl):
    parts = [wl[n].astype(BF16).reshape(-1, PACK_W) for n, _, _ in PACK_SPLIT]
    cw = wl["conv_w"].reshape(-1)
    hi = cw.astype(BF16)
    r1 = cw - hi.astype(F32)
    mid = r1.astype(BF16)
    lo = (r1 - mid.astype(F32)).astype(BF16)
    cterms = jnp.pad(jnp.concatenate([hi, mid, lo]), (0, 3 * PACK_W - 3 * CONV_SHARD)).reshape(3, PACK_W)
    tail = jnp.pad(cterms, ((0, PACK_ROWS - MAT_ROWS - 3), (0, 0)))
    return jnp.concatenate(parts + [tail], axis=0)


def _unpack_weights(gath):
    out = {}
    r = 0
    for n, nrows, shp in PACK_SPLIT:
        t = gath[:, r:r + nrows].reshape((4,) + shp)
        r += nrows
        if n == "w_o":
            out[n] = t.reshape(4 * shp[0], shp[1])
        else:
            out[n] = t.transpose(1, 0, 2).reshape(shp[0], 4 * shp[1])
    ct = gath[:, r:r + 3].reshape(4, 3 * PACK_W)[:, :3 * CONV_SHARD].astype(F32).reshape(4, 3, CONV_SHARD)
    cw = (ct[:, 0] + ct[:, 1]) + ct[:, 2]
    out["conv_w"] = cw.reshape(4, 3, 128).transpose(1, 0, 2).reshape(3, CW)
    return out


def _pack_grads(g):
    parts = []
    for n, nrows, shp in PACK_SPLIT:
        t = g[n]
        if n == "w_o":
            t = t.reshape((4,) + shp)
        else:
            t = t.reshape(shp[0], 4, shp[1]).transpose(1, 0, 2)
        parts.append(t.reshape(4, nrows, PACK_W))
    cw = g["conv_w"].reshape(3, 4, 128).transpose(1, 0, 2).reshape(4, 1, CONV_SHARD)
    parts.append(jnp.pad(cw, ((0, 0), (0, PACK_ROWS - MAT_ROWS - 1), (0, PACK_W - CONV_SHARD))))
    return jnp.concatenate(parts, axis=1)


def _unpack_grads(red):
    out = {}
    r = 0
    for n, nrows, shp in PACK_SPLIT:
        out[n] = red[r:r + nrows].reshape(shp)
        r += nrows
    out["conv_w"] = red[r, :CONV_SHARD].reshape(3, 128)
    return out


SMALL_SIZES = (("norm_g", D), ("b_gate", 3 * D), ("conv_b", CW), ("q_a_norm_g", QL), ("kv_a_norm_g", KVL),
               ("mla_q_norm_g", QK), ("mla_k_norm_g", QK), ("dil_q_norm_g", NG * HD), ("dil_k_norm_g", NG * HD))
SMALL_ROWS = 88


def _pack_small(per_name):
    flat = jnp.concatenate([per_name[n].reshape(-1).astype(F32) for n, _ in SMALL_SIZES])
    return jnp.pad(flat, (0, SMALL_ROWS * LANE - flat.shape[0])).reshape(SMALL_ROWS, LANE)


def _unpack_small(packed, like):
    out = {}
    flat = packed.reshape(-1)
    r = 0
    for n, sz in SMALL_SIZES:
        out[n] = flat[r:r + NL * sz].reshape(like[n].shape)
        r += NL * sz
    return out


def _adamw_math(w, g, m, v):
    m = ADAM_B1 * m + (1.0 - ADAM_B1) * g
    v = ADAM_B2 * v + (1.0 - ADAM_B2) * jnp.square(g)
    m_hat = m / (1.0 - ADAM_B1 ** ADAM_STEP)
    v_hat = v / (1.0 - ADAM_B2 ** ADAM_STEP)
    delta = -ADAM_LR * (m_hat / (jnp.sqrt(v_hat) + ADAM_EPS) + ADAM_WD * w)
    return delta, m, v


def _adamw(name, w, g, m, v, br, bc=None):
    L, R, C = w.shape
    bc = C if bc is None else bc
    blk = lambda l, i, j: (l, i, j)
    return _pcall(name, _adamw_math, (L, R // br, C // bc), [(t, (None, br, bc), blk) for t in (w, g, m, v)],
                  [((L, R, C), F32, (None, br, bc), blk)] * 3)


ADAM_ROWS = {"w_uq": 256, "w_ukv": 128, "w_out_a": 512, "w_out_b": 512, "w_out_c": 512, "w_o": 256,
             "conv_w": 3}


def kernel(x, norm_g, w_in, b_gate, conv_w, conv_b, q_a_norm_g, w_uq, kv_a_norm_g, w_ukv, mla_q_norm_g, mla_k_norm_g, dil_q_norm_g, dil_k_norm_g, w_out_a, w_out_b, w_out_c, w_o, loss_target, m_norm_g, m_w_in, m_b_gate, m_conv_w, m_conv_b, m_q_a_norm_g, m_w_uq, m_kv_a_norm_g, m_w_ukv, m_mla_q_norm_g, m_mla_k_norm_g, m_dil_q_norm_g, m_dil_k_norm_g, m_w_out_a, m_w_out_b, m_w_out_c, m_w_o, v_norm_g, v_w_in, v_b_gate, v_conv_w, v_conv_b, v_q_a_norm_g, v_w_uq, v_kv_a_norm_g, v_w_ukv, v_mla_q_norm_g, v_mla_k_norm_g, v_dil_q_norm_g, v_dil_k_norm_g, v_w_out_a, v_w_out_b, v_w_out_c, v_w_o):
    W = dict(norm_g=norm_g, w_in=w_in, b_gate=b_gate, conv_w=conv_w, conv_b=conv_b, q_a_norm_g=q_a_norm_g, w_uq=w_uq,
             kv_a_norm_g=kv_a_norm_g, w_ukv=w_ukv, mla_q_norm_g=mla_q_norm_g, mla_k_norm_g=mla_k_norm_g,
             dil_q_norm_g=dil_q_norm_g, dil_k_norm_g=dil_k_norm_g, w_out_a=w_out_a, w_out_b=w_out_b, w_out_c=w_out_c,
             w_o=w_o)
    M = dict(norm_g=m_norm_g, w_in=m_w_in, b_gate=m_b_gate, conv_w=m_conv_w, conv_b=m_conv_b, q_a_norm_g=m_q_a_norm_g,
             w_uq=m_w_uq, kv_a_norm_g=m_kv_a_norm_g, w_ukv=m_w_ukv, mla_q_norm_g=m_mla_q_norm_g,
             mla_k_norm_g=m_mla_k_norm_g, dil_q_norm_g=m_dil_q_norm_g, dil_k_norm_g=m_dil_k_norm_g, w_out_a=m_w_out_a,
             w_out_b=m_w_out_b, w_out_c=m_w_out_c, w_o=m_w_o)
    V = dict(norm_g=v_norm_g, w_in=v_w_in, b_gate=v_b_gate, conv_w=v_conv_w, conv_b=v_conv_b, q_a_norm_g=v_q_a_norm_g,
             w_uq=v_w_uq, kv_a_norm_g=v_kv_a_norm_g, w_ukv=v_w_ukv, mla_q_norm_g=v_mla_q_norm_g,
             mla_k_norm_g=v_mla_k_norm_g, dil_q_norm_g=v_dil_q_norm_g, dil_k_norm_g=v_dil_k_norm_g, w_out_a=v_w_out_a,
             w_out_b=v_w_out_b, w_out_c=v_w_out_c, w_o=v_w_o)
    batch = x.shape[0]
    T = batch * S

    def layer_weights(l, cont, gath):
        full = _unpack_weights(gath)
        pad_qk = lambda t: jnp.pad(t, (0, QKP - QK)).reshape(1, QKP)
        full.update(
            w_in_t=cont,
            norm_g=norm_g[l].reshape(1, D), b_gate=b_gate[l].reshape(1, 3 * D), conv_b=conv_b[l].reshape(1, CW),
            q_a_norm_g=q_a_norm_g[l].reshape(1, QL), kv_a_norm_g=kv_a_norm_g[l].reshape(1, KVL),
            mla_q_norm_g=pad_qk(mla_q_norm_g[l]), mla_k_norm_g=pad_qk(mla_k_norm_g[l]),
            dil_q_norm_g=dil_q_norm_g[l].reshape(NG, 1, HD), dil_k_norm_g=dil_k_norm_g[l].reshape(NG, 1, HD))
        return full

    words = [_w_in_words(w_in[l]) for l in range(NL)]
    packs = [_pack_weights({n: W[n][l] for n in BIG[1:] + ("conv_w",)}) for l in range(NL)]
    tabs = _rope_tables() + (_dil_slopes(),)
    x2 = x.reshape(T, D)

    cont0 = _all_gather(words[0])
    ag0 = _ag_behind_start(None, packs[0], cont0, "0")
    ag1 = []

    def rest_of_layer0(proj):
        _, gath0 = _ag_behind_wait(*ag0[:6], proj, "0")
        ag1.extend(_ag_behind_start(words[1], packs[1], gath0, "1"))
        w = layer_weights(0, cont0, gath0)
        w["conv_b"] = w["conv_b"] + ag1[6][0:1, 0:1]
        all0.append(w)
        return w

    first0 = dict(w_in_t=cont0, norm_g=norm_g[0].reshape(1, D) + ag0[6][0:1, 0:1])
    all0 = []
    y0, res0 = _layer_fwd(x2, first0, tabs, batch, rest=rest_of_layer0)
    w0 = all0[0]
    w1 = layer_weights(1, *_ag_behind_wait(*ag1[:6], y0, "1"))
    y1, res1 = _layer_fwd(y0, w1, tabs, batch)

    row = lambda i: (i, 0)
    dy, loss = _pcall("loss", _loss_math, (T // BR,),
                      [(y1, (BR, D), row), (loss_target.reshape(T, D), (BR, D), row)],
                      [((T, D), F32, (BR, D), row), ((1, 1), F32, (1, 1), lambda i: (0, 0), True)])
    loss = lax.psum(loss[0, 0], ("x", "y", "c"))

    grads = [None] * NL
    dy, grads[1] = _layer_bwd(dy, w1, res1, tabs, batch)
    st = [None] * NL
    ex = [None] * NL
    sw1 = _rs_swap_start(grads[1]["w_in_t"], _pack_grads(grads[1]), "1")
    w0["w_o"] = w0["w_o"] + sw1[6][0:1, 0:1].astype(BF16)

    def exchange_layer1(t):
        st[1] = _rs_sums(*_rs_swap_wait(*sw1[:6], t, "1"))
        ex[1] = _rs_exchange_start(st[1]["sw"], st[1]["ss"], "1")
        return ex[1][6]

    red = [None] * NL
    g_in_t = [lax.empty((NL, WSH, D), F32)]

    def finish(l, after):
        r2w, r2s = _rs_exchange_wait(*ex[l][:6], after, str(l))
        g_in_t[0], rs = _rs_end(st[l], r2w, r2s, g_in_t[0], l)
        red[l] = _unpack_grads(rs)
        return rs

    def start_layer0(g):
        sw0 = _rs_swap_start(g["w_in_t"], _pack_grads(g), "0")
        done1 = finish(1, sw0[6])
        st[0] = _rs_sums(*_rs_swap_wait(*sw0[:6], done1, "0"))
        ex[0] = _rs_exchange_start(st[0]["sw"], st[0]["ss"], "0")
        return ex[0][6]

    dx, grads[0] = _layer_bwd(dy, w0, res0, tabs, batch, after_dw=start_layer0, after_merge=exchange_layer1)
    grad_x = dx.reshape(batch, S, D)
    finish(0, dx)

    G = {n: jnp.stack([red[l][n] for l in range(NL)]) for n in BIG[1:] + ("conv_w",)}
    g_in_t = g_in_t[0]
    G["w_in"] = jnp.swapaxes(g_in_t, 1, 2)
    small_g = {n: jnp.stack([grads[l][n].reshape(-1)[:sz] for l in range(NL)]) for n, sz in SMALL_SIZES}
    small_red = _all_reduce_small(_pack_small(small_g))
    G.update(_unpack_small(small_red, {n: W[n] for n in SMALL}))

    delta, new_m, new_v = {}, {}, {}
    for n in BIG[1:] + ("conv_w",):
        delta[n], new_m[n], new_v[n] = _adamw("adamw_" + n, W[n], G[n], M[n], V[n], ADAM_ROWS[n])
    tr = lambda t: jnp.swapaxes(t, 1, 2)
    delta["w_in"], new_m["w_in"], new_v["w_in"] = (
        tr(t) for t in _adamw("adamw_w_in", tr(w_in), g_in_t, tr(m_w_in), tr(v_w_in), WSH, LANE))
    sw, sm, sv = (_pack_small({n: t[n] for n in SMALL})[None] for t in (W, M, V))
    sd, snm, snv = _adamw("adamw_small", sw, small_red[None], sm, sv, SMALL_ROWS)
    like = {n: W[n] for n in SMALL}
    delta.update(_unpack_small(sd[0], like))
    new_m.update(_unpack_small(snm[0], like))
    new_v.update(_unpack_small(snv[0], like))

    return (loss, grad_x, *[G[n] for n in WEIGHTS], *[delta[n] for n in WEIGHTS],
            *[new_m[n] for n in WEIGHTS], *[new_v[n] for n in WEIGHTS])
```
